```python
import math
import jax
import jax.numpy as jnp
from jax import lax
import numpy as np

D_MODEL = 1024
BATCH = 8
SEQ = 4096
DEPTH = 2

CHUNK = 64
Q_BLOCK = 128

N_MIXERS = 4
HEAD_DIM = 64
GROUP_WIDTH = D_MODEL // N_MIXERS
N_HEADS_GROUP = GROUP_WIDTH // HEAD_DIM
D_MIX = N_MIXERS * GROUP_WIDTH

MLA_Q_RANK = D_MODEL // 4
MLA_KV_RANK = D_MODEL // 8
MLA_NOPE_DIM = HEAD_DIM
MLA_ROPE_DIM = HEAD_DIM // 2
MLA_V_DIM = HEAD_DIM

RET_DECAY_OFFSET = 5.0

D_FF = 4 * D_MODEL

ROPE_BASE = 10000.0
EPS = 1e-6
FORGET_BIAS = 4.0

D_IN_PROJ = 10 * GROUP_WIDTH + N_HEADS_GROUP + MLA_Q_RANK + MLA_KV_RANK + MLA_ROPE_DIM

kernel_name = 'hybrid_fox_mla_retention_stickbreaking_trunk'

F32 = jnp.float32


def rms_norm(x, gain):
    xf = x.astype(F32)
    y = xf * lax.rsqrt(jnp.mean(xf * xf, axis=-1, keepdims=True) + EPS)
    return (y * gain.astype(F32)).astype(x.dtype)


def head_group_norm(o, gain):
    mu = jnp.mean(o, axis=-1, keepdims=True)
    var = jnp.mean(jnp.square(o - mu), axis=-1, keepdims=True)
    y = (o - mu) * lax.rsqrt(var + EPS)
    return y * gain.astype(F32).reshape(o.shape[2], o.shape[3])


def apply_rope(x, positions):
    half = x.shape[-1] // 2
    inv_freq = ROPE_BASE ** (-jnp.arange(half, dtype=F32) / half)
    ang = positions.astype(F32)[:, :, None, None] * inv_freq
    cos, sin = jnp.cos(ang), jnp.sin(ang)
    xf = x.astype(F32)
    x1, x2 = xf[..., :half], xf[..., half:]
    return jnp.concatenate([x1 * cos - x2 * sin, x1 * sin + x2 * cos], axis=-1).astype(x.dtype)


def to_blocks(t):
    b, s = t.shape[:2]
    return jnp.moveaxis(t.reshape((b, s // Q_BLOCK, Q_BLOCK) + t.shape[2:]), 1, 0)


def from_blocks(t):
    nb, b, qb = t.shape[:3]
    return jnp.moveaxis(t, 0, 1).reshape((b, nb * qb) + t.shape[3:])


def forgetting_attention(q, k, v, f_logit):
    seq = q.shape[1]
    scale = q.shape[-1] ** -0.5
    cum = jnp.cumsum(jax.nn.log_sigmoid(f_logit.astype(F32)), axis=1)
    cum_k = jnp.swapaxes(cum, 1, 2)
    k_pos = jnp.arange(seq)
    q_pos = k_pos.reshape(-1, Q_BLOCK)

    def block(xs):
        qb, cqb, qp = xs
        s = jnp.einsum('bqhd,bkhd->bhqk', qb, k, preferred_element_type=F32) * scale
        s = s + jnp.swapaxes(cqb, 1, 2)[..., None] - cum_k[:, :, None, :]
        s = jnp.where(k_pos[None, :] <= qp[:, None], s, -jnp.inf)
        p = jax.nn.softmax(s, axis=-1)
        return jnp.einsum('bhqk,bkhd->bqhd', p.astype(v.dtype), v)

    return from_blocks(lax.map(block, (to_blocks(q), to_blocks(cum), q_pos)))


def chunk_causal_softmax_attention(q, k, v):
    seq = q.shape[1]
    scale = q.shape[-1] ** -0.5
    k_chunk = jnp.arange(seq) // CHUNK
    q_pos = jnp.arange(seq).reshape(-1, Q_BLOCK)

    def block(xs):
        qb, qp = xs
        s = jnp.einsum('bqhd,bkhd->bhqk', qb, k, preferred_element_type=F32) * scale
        s = jnp.where(k_chunk[None, :] <= (qp // CHUNK)[:, None], s, -jnp.inf)
        p = jax.nn.softmax(s, axis=-1)
        return jnp.einsum('bhqk,bkhd->bqhd', p.astype(v.dtype), v)

    return from_blocks(lax.map(block, (to_blocks(q), q_pos)))


def stick_breaking_attention(q, k, v):
    seq = q.shape[1]
    scale = q.shape[-1] ** -0.5
    k_pos = jnp.arange(seq)
    q_pos = k_pos.reshape(-1, Q_BLOCK)

    def block(xs):
        qb, qp = xs
        z = jnp.einsum('bqhd,bkhd->bhqk', qb, k, preferred_element_type=F32) * scale
        visible = k_pos[None, :] < qp[:, None]
        log_stay = jnp.where(visible, jax.nn.log_sigmoid(-z), 0.0)
        later = lax.cumsum(log_stay, axis=3, reverse=True) - log_stay
        w = jnp.where(visible, jnp.exp(jax.nn.log_sigmoid(z) + later), 0.0)
        return jnp.einsum('bhqk,bkhd->bqhd', w.astype(v.dtype), v)

    return from_blocks(lax.map(block, (to_blocks(q), q_pos)))


def chunkwise_retention(q, k, v, positions):
    b, seq, h, d = q.shape
    n = seq // CHUNK
    qf = apply_rope(q, positions).astype(F32)
    kf = apply_rope(k, positions).astype(F32) * (d ** -0.5)
    vf = v.astype(F32)
    log_gamma = jnp.log1p(-jnp.power(2.0, -RET_DECAY_OFFSET - jnp.arange(h, dtype=F32)))
    idx = jnp.arange(CHUNK, dtype=F32)
    qc = qf.reshape(b, n, CHUNK, h, d)
    kc = kf.reshape(b, n, CHUNK, h, d)
    vc = vf.reshape(b, n, CHUNK, h, d)
    intra_decay = jnp.exp(log_gamma[:, None, None] * jnp.abs(idx[:, None] - idx[None, :]))
    scores = jnp.einsum('bncht,bnmht->bnhcm', qc, kc) * intra_decay
    intra = jnp.einsum('bnhcm,bnmhe->bnche', scores, vc)
    k_tail = kc * jnp.exp(log_gamma[None, :] * (CHUNK - 1 - idx)[:, None])[None, None, :, :, None]
    chunk_kv = jnp.einsum('bnmht,bnmhe->nbhte', k_tail, vc)
    chunk_decay = jnp.exp(log_gamma * CHUNK)[None, :, None, None]

    def step(state, kv):
        return state * chunk_decay + kv, state

    _, prev_state = lax.scan(step, jnp.zeros((b, h, d, d), F32), chunk_kv)
    q_head = qc * jnp.exp(log_gamma[None, :] * (idx + 1.0)[:, None])[None, None, :, :, None]
    inter = jnp.einsum('bncht,nbhte->bnche', q_head, prev_state)
    return (intra + inter).reshape(b, seq, h, d)


def split_columns(proj):
    sizes = [GROUP_WIDTH, GROUP_WIDTH, GROUP_WIDTH, N_HEADS_GROUP,
             MLA_Q_RANK, MLA_KV_RANK, MLA_ROPE_DIM,
             GROUP_WIDTH, GROUP_WIDTH, GROUP_WIDTH, GROUP_WIDTH,
             GROUP_WIDTH, GROUP_WIDTH, GROUP_WIDTH]
    offsets = [int(o) for o in np.cumsum(sizes)[:-1]]
    return jnp.split(proj, offsets, axis=-1)


def hybrid_mixer(h, positions, w_in, b_forget, g_q_lora, w_q_up, g_kv_lora, w_kv_up, g_mix_out, w_out):
    b, seq, _ = h.shape
    heads = lambda t: t.reshape(b, seq, N_HEADS_GROUP, -1)
    proj = jnp.einsum('bsd,dn->bsn', h, w_in)
    (fq, fk, fv, ff, cq, ckv, kr, rq, rk, rv, rg, sq, sk, sv) = split_columns(proj)

    out_a = forgetting_attention(heads(fq), heads(fk), heads(fv), ff + b_forget)
    out_a = rms_norm(out_a.reshape(b, seq, GROUP_WIDTH), g_mix_out[0:GROUP_WIDTH])

    q = jnp.einsum('bsr,rn->bsn', rms_norm(cq, g_q_lora), w_q_up).reshape(b, seq, N_HEADS_GROUP, MLA_NOPE_DIM + MLA_ROPE_DIM)
    q = jnp.concatenate([q[..., :MLA_NOPE_DIM], apply_rope(q[..., MLA_NOPE_DIM:], positions)], axis=-1)
    kv = jnp.einsum('bsr,rn->bsn', rms_norm(ckv, g_kv_lora), w_kv_up).reshape(b, seq, N_HEADS_GROUP, MLA_NOPE_DIM + MLA_V_DIM)
    k_rope = apply_rope(kr[:, :, None, :], positions)
    k = jnp.concatenate([kv[..., :MLA_NOPE_DIM], jnp.broadcast_to(k_rope, (b, seq, N_HEADS_GROUP, MLA_ROPE_DIM))], axis=-1)
    out_b = chunk_causal_softmax_attention(q, k, kv[..., MLA_NOPE_DIM:])
    out_b = rms_norm(out_b.reshape(b, seq, GROUP_WIDTH), g_mix_out[GROUP_WIDTH:2 * GROUP_WIDTH])

    ret = chunkwise_retention(heads(rq), heads(rk), heads(rv), positions)
    ret = head_group_norm(ret, g_mix_out[2 * GROUP_WIDTH:3 * GROUP_WIDTH]).reshape(b, seq, GROUP_WIDTH)
    out_c = (ret * jax.nn.silu(rg.astype(F32))).astype(h.dtype)

    out_d = stick_breaking_attention(heads(sq), heads(sk), heads(sv))
    out_d = rms_norm(out_d.reshape(b, seq, GROUP_WIDTH), g_mix_out[3 * GROUP_WIDTH:])

    mixed = jnp.concatenate([out_a, out_b, out_c, out_d], axis=-1)
    return jnp.einsum('bsn,nd->bsd', mixed, w_out)


def squared_relu_mlp(h, w_up, w_down):
    u = jnp.square(jax.nn.relu(jnp.einsum('bsd,df->bsf', h, w_up)))
    return jnp.einsum('bsf,fd->bsd', u, w_down)


def _fwd_setup_inputs(seed: int = 0) -> dict:
    key = jax.random.key(seed)
    ks = jax.random.split(key, 20)
    nrm = lambda k, shape, fan_in: jax.random.normal(k, shape, F32) * (fan_in ** -0.5)
    gain = lambda k, shape: 1.0 + 0.05 * jax.random.normal(k, shape, F32)
    x = jax.random.normal(ks[0], (BATCH, SEQ, D_MODEL), F32)
    start = jax.random.randint(ks[1], (BATCH, 1), 0, 1024, dtype=jnp.int32)
    positions = start + jnp.arange(SEQ, dtype=jnp.int32)[None, :]
    return {
        'x': x,
        'positions': positions,
        'g_mix_pre': gain(ks[2], (DEPTH, D_MODEL)),
        'w_in': nrm(ks[3], (DEPTH, D_MODEL, D_IN_PROJ), D_MODEL),
        'b_forget': FORGET_BIAS + 0.5 * jax.random.normal(ks[4], (DEPTH, N_HEADS_GROUP), F32),
        'g_q_lora': gain(ks[5], (DEPTH, MLA_Q_RANK)),
        'w_q_up': nrm(ks[6], (DEPTH, MLA_Q_RANK, N_HEADS_GROUP * (MLA_NOPE_DIM + MLA_ROPE_DIM)), MLA_Q_RANK),
        'g_kv_lora': gain(ks[7], (DEPTH, MLA_KV_RANK)),
        'w_kv_up': nrm(ks[8], (DEPTH, MLA_KV_RANK, N_HEADS_GROUP * (MLA_NOPE_DIM + MLA_V_DIM)), MLA_KV_RANK),
        'g_mix_out': gain(ks[9], (DEPTH, D_MIX)),
        'w_out': nrm(ks[10], (DEPTH, D_MIX, D_MODEL), D_MIX),
        'g_mix_post': gain(ks[11], (DEPTH, D_MODEL)),
        'g_ffn_pre': gain(ks[12], (DEPTH, D_MODEL)),
        'w_ffn_up': nrm(ks[13], (DEPTH, D_MODEL, D_FF), D_MODEL),
        'w_ffn_down': nrm(ks[14], (DEPTH, D_FF, D_MODEL), D_FF),
        'g_ffn_post': gain(ks[15], (DEPTH, D_MODEL)),
    }


def _fwd_reference(x, positions, g_mix_pre, w_in, b_forget, g_q_lora, w_q_up, g_kv_lora, w_kv_up,
              g_mix_out, w_out, g_mix_post, g_ffn_pre, w_ffn_up, w_ffn_down, g_ffn_post):
    for layer in range(DEPTH):
        h = rms_norm(x, g_mix_pre[layer])
        mix = hybrid_mixer(h, positions, w_in[layer], b_forget[layer], g_q_lora[layer], w_q_up[layer],
                           g_kv_lora[layer], w_kv_up[layer], g_mix_out[layer], w_out[layer])
        x = x + rms_norm(mix, g_mix_post[layer])
        h = rms_norm(x, g_ffn_pre[layer])
        x = x + rms_norm(squared_relu_mlp(h, w_ffn_up[layer], w_ffn_down[layer]), g_ffn_post[layer])
    return x


import jax as _jax
import jax.numpy as _jnp

TWIN_FORMAT = 'train_step'
FWD_PARAMS = ['x', 'positions', 'g_mix_pre', 'w_in', 'b_forget', 'g_q_lora', 'w_q_up', 'g_kv_lora', 'w_kv_up', 'g_mix_out', 'w_out', 'g_mix_post', 'g_ffn_pre', 'w_ffn_up', 'w_ffn_down', 'g_ffn_post']
TWIN_WEIGHTS = ['g_mix_pre', 'w_in', 'b_forget', 'g_q_lora', 'w_q_up', 'g_kv_lora', 'w_kv_up', 'g_mix_out', 'w_out', 'g_mix_post', 'g_ffn_pre', 'w_ffn_up', 'w_ffn_down', 'g_ffn_post']
TWIN_DIFF_INPUT = 'x'
TWIN_INPUTS = ['x', 'positions', 'g_mix_pre', 'w_in', 'b_forget', 'g_q_lora', 'w_q_up', 'g_kv_lora', 'w_kv_up', 'g_mix_out', 'w_out', 'g_mix_post', 'g_ffn_pre', 'w_ffn_up', 'w_ffn_down', 'g_ffn_post', 'loss_target', 'm_g_mix_pre', 'm_w_in', 'm_b_forget', 'm_g_q_lora', 'm_w_q_up', 'm_g_kv_lora', 'm_w_kv_up', 'm_g_mix_out', 'm_w_out', 'm_g_mix_post', 'm_g_ffn_pre', 'm_w_ffn_up', 'm_w_ffn_down', 'm_g_ffn_post', 'v_g_mix_pre', 'v_w_in', 'v_b_forget', 'v_g_q_lora', 'v_w_q_up', 'v_g_kv_lora', 'v_w_kv_up', 'v_g_mix_out', 'v_w_out', 'v_g_mix_post', 'v_g_ffn_pre', 'v_w_ffn_up', 'v_w_ffn_down', 'v_g_ffn_post']
TWIN_OUTPUTS = ['loss', 'grad_x', 'grad_g_mix_pre', 'grad_w_in', 'grad_b_forget', 'grad_g_q_lora', 'grad_w_q_up', 'grad_g_kv_lora', 'grad_w_kv_up', 'grad_g_mix_out', 'grad_w_out', 'grad_g_mix_post', 'grad_g_ffn_pre', 'grad_w_ffn_up', 'grad_w_ffn_down', 'grad_g_ffn_post', 'delta_g_mix_pre', 'delta_w_in', 'delta_b_forget', 'delta_g_q_lora', 'delta_w_q_up', 'delta_g_kv_lora', 'delta_w_kv_up', 'delta_g_mix_out', 'delta_w_out', 'delta_g_mix_post', 'delta_g_ffn_pre', 'delta_w_ffn_up', 'delta_w_ffn_down', 'delta_g_ffn_post', 'new_m_g_mix_pre', 'new_m_w_in', 'new_m_b_forget', 'new_m_g_q_lora', 'new_m_w_q_up', 'new_m_g_kv_lora', 'new_m_w_kv_up', 'new_m_g_mix_out', 'new_m_w_out', 'new_m_g_mix_post', 'new_m_g_ffn_pre', 'new_m_w_ffn_up', 'new_m_w_ffn_down', 'new_m_g_ffn_post', 'new_v_g_mix_pre', 'new_v_w_in', 'new_v_b_forget', 'new_v_g_q_lora', 'new_v_w_q_up', 'new_v_g_kv_lora', 'new_v_w_kv_up', 'new_v_g_mix_out', 'new_v_w_out', 'new_v_g_mix_post', 'new_v_g_ffn_pre', 'new_v_w_ffn_up', 'new_v_w_ffn_down', 'new_v_g_ffn_post']
TWIN_LEAF_KINDS = {'loss': 'loss', 'grad_x': 'grad_x', 'grad_g_mix_pre': 'grad_w', 'grad_w_in': 'grad_w', 'grad_b_forget': 'grad_w', 'grad_g_q_lora': 'grad_w', 'grad_w_q_up': 'grad_w', 'grad_g_kv_lora': 'grad_w', 'grad_w_kv_up': 'grad_w', 'grad_g_mix_out': 'grad_w', 'grad_w_out': 'grad_w', 'grad_g_mix_post': 'grad_w', 'grad_g_ffn_pre': 'grad_w', 'grad_w_ffn_up': 'grad_w', 'grad_w_ffn_down': 'grad_w', 'grad_g_ffn_post': 'grad_w', 'delta_g_mix_pre': 'delta_w', 'delta_w_in': 'delta_w', 'delta_b_forget': 'delta_w', 'delta_g_q_lora': 'delta_w', 'delta_w_q_up': 'delta_w', 'delta_g_kv_lora': 'delta_w', 'delta_w_kv_up': 'delta_w', 'delta_g_mix_out': 'delta_w', 'delta_w_out': 'delta_w', 'delta_g_mix_post': 'delta_w', 'delta_g_ffn_pre': 'delta_w', 'delta_w_ffn_up': 'delta_w', 'delta_w_ffn_down': 'delta_w', 'delta_g_ffn_post': 'delta_w', 'new_m_g_mix_pre': 'new_m', 'new_m_w_in': 'new_m', 'new_m_b_forget': 'new_m', 'new_m_g_q_lora': 'new_m', 'new_m_w_q_up': 'new_m', 'new_m_g_kv_lora': 'new_m', 'new_m_w_kv_up': 'new_m', 'new_m_g_mix_out': 'new_m', 'new_m_w_out': 'new_m', 'new_m_g_mix_post': 'new_m', 'new_m_g_ffn_pre': 'new_m', 'new_m_w_ffn_up': 'new_m', 'new_m_w_ffn_down': 'new_m', 'new_m_g_ffn_post': 'new_m', 'new_v_g_mix_pre': 'new_v', 'new_v_w_in': 'new_v', 'new_v_b_forget': 'new_v', 'new_v_g_q_lora': 'new_v', 'new_v_w_q_up': 'new_v', 'new_v_g_kv_lora': 'new_v', 'new_v_w_kv_up': 'new_v', 'new_v_g_mix_out': 'new_v', 'new_v_w_out': 'new_v', 'new_v_g_mix_post': 'new_v', 'new_v_g_ffn_pre': 'new_v', 'new_v_w_ffn_up': 'new_v', 'new_v_w_ffn_down': 'new_v', 'new_v_g_ffn_post': 'new_v'}


def _forward(args):
    return _fwd_reference(*[args[k] for k in FWD_PARAMS])


def _output_shape():
    out = _jax.eval_shape(lambda: _forward(_fwd_setup_inputs(0)))
    return out.shape, out.dtype

N_MICROBATCH = 1
ADAM_LR = 0.001
ADAM_B1 = 0.9
ADAM_B2 = 0.999
ADAM_EPS = 1e-08
ADAM_WD = 0.01
ADAM_STEP = 10
PER_EXAMPLE_BATCH_AXIS = {'x': 0, 'positions': 0, 'loss_target': 0}
SHARED_INPUTS = []
_WEIGHT_DTYPES = {'g_mix_pre': _jnp.float32, 'w_in': _jnp.float32, 'b_forget': _jnp.float32, 'g_q_lora': _jnp.float32, 'w_q_up': _jnp.float32, 'g_kv_lora': _jnp.float32, 'w_kv_up': _jnp.float32, 'g_mix_out': _jnp.float32, 'w_out': _jnp.float32, 'g_mix_post': _jnp.float32, 'g_ffn_pre': _jnp.float32, 'w_ffn_up': _jnp.float32, 'w_ffn_down': _jnp.float32, 'g_ffn_post': _jnp.float32}
MOMENT_SCALE = {'g_mix_pre': 7.112039e+00, 'w_in': 4.563958e+00, 'b_forget': 5.529498e+00, 'g_q_lora': 1.137465e+00, 'w_q_up': 8.639294e-01, 'g_kv_lora': 1.429796e+01, 'w_kv_up': 7.935051e+00, 'g_mix_out': 6.871205e+00, 'w_out': 8.165999e+00, 'g_mix_post': 3.442473e+01, 'g_ffn_pre': 4.523304e+00, 'w_ffn_up': 2.038759e+00, 'w_ffn_down': 1.337688e+01, 'g_ffn_post': 3.439906e+01}


def _to_microbatches(a, axis):
    t = _jnp.moveaxis(a, axis, 0)
    t = t.reshape((N_MICROBATCH, t.shape[0] // N_MICROBATCH) + t.shape[1:])
    return _jnp.moveaxis(t, 1, axis + 1)


def setup_inputs(seed: int = 0) -> dict:
    inp = _fwd_setup_inputs(seed)
    key = _jax.random.fold_in(_jax.random.key(seed), 7919)
    shape, _ = _output_shape()
    out = dict(inp)
    out["loss_target"] = _jax.random.normal(_jax.random.fold_in(key, 0), shape, _jnp.float32)
    for i, name in enumerate(TWIN_WEIGHTS):
        w = inp[name].astype(_jnp.float32)
        if MOMENT_SCALE is None:
            s = _jnp.sqrt(_jnp.mean(_jnp.square(w)) + 1e-30)
        else:
            s = MOMENT_SCALE[name]
        km, kv = _jax.random.split(_jax.random.fold_in(key, i + 1))
        out[name] = w
        out["m_" + name] = s * _jax.random.normal(km, w.shape, _jnp.float32)
        out["v_" + name] = (s * s) * _jax.random.uniform(kv, w.shape, _jnp.float32, 0.5, 1.5)
    if N_MICROBATCH > 1:
        for name, axis in PER_EXAMPLE_BATCH_AXIS.items():
            out[name] = _to_microbatches(out[name], axis)
    return {'x': out['x'], 'positions': out['positions'], 'g_mix_pre': out['g_mix_pre'], 'w_in': out['w_in'], 'b_forget': out['b_forget'], 'g_q_lora': out['g_q_lora'], 'w_q_up': out['w_q_up'], 'g_kv_lora': out['g_kv_lora'], 'w_kv_up': out['w_kv_up'], 'g_mix_out': out['g_mix_out'], 'w_out': out['w_out'], 'g_mix_post': out['g_mix_post'], 'g_ffn_pre': out['g_ffn_pre'], 'w_ffn_up': out['w_ffn_up'], 'w_ffn_down': out['w_ffn_down'], 'g_ffn_post': out['g_ffn_post'], 'loss_target': out['loss_target'], 'm_g_mix_pre': out['m_g_mix_pre'], 'm_w_in': out['m_w_in'], 'm_b_forget': out['m_b_forget'], 'm_g_q_lora': out['m_g_q_lora'], 'm_w_q_up': out['m_w_q_up'], 'm_g_kv_lora': out['m_g_kv_lora'], 'm_w_kv_up': out['m_w_kv_up'], 'm_g_mix_out': out['m_g_mix_out'], 'm_w_out': out['m_w_out'], 'm_g_mix_post': out['m_g_mix_post'], 'm_g_ffn_pre': out['m_g_ffn_pre'], 'm_w_ffn_up': out['m_w_ffn_up'], 'm_w_ffn_down': out['m_w_ffn_down'], 'm_g_ffn_post': out['m_g_ffn_post'], 'v_g_mix_pre': out['v_g_mix_pre'], 'v_w_in': out['v_w_in'], 'v_b_forget': out['v_b_forget'], 'v_g_q_lora': out['v_g_q_lora'], 'v_w_q_up': out['v_w_q_up'], 'v_g_kv_lora': out['v_g_kv_lora'], 'v_w_kv_up': out['v_w_kv_up'], 'v_g_mix_out': out['v_g_mix_out'], 'v_w_out': out['v_w_out'], 'v_g_mix_post': out['v_g_mix_post'], 'v_g_ffn_pre': out['v_g_ffn_pre'], 'v_w_ffn_up': out['v_w_ffn_up'], 'v_w_ffn_down': out['v_w_ffn_down'], 'v_g_ffn_post': out['v_g_ffn_post']}


def _loss(weights, diff, rest, loss_target):
    with _jax.named_scope("forward"):
        args = {**rest, TWIN_DIFF_INPUT: diff, **{k: w.astype(_WEIGHT_DTYPES[k]) for k, w in weights.items()}}
        y = _forward(args)
    with _jax.named_scope("loss_head"):
        err = _jnp.square(y.astype(_jnp.float32) - loss_target)
        return 0.5 * _jnp.sum(_jnp.mean(err, axis=-1)) if err.ndim else 0.5 * err


def _adamw(w, g, m, v):
    m = ADAM_B1 * m + (1.0 - ADAM_B1) * g
    v = ADAM_B2 * v + (1.0 - ADAM_B2) * _jnp.square(g)
    m_hat = m / (1.0 - ADAM_B1 ** ADAM_STEP)
    v_hat = v / (1.0 - ADAM_B2 ** ADAM_STEP)
    delta = -ADAM_LR * (m_hat / (_jnp.sqrt(v_hat) + ADAM_EPS) + ADAM_WD * w)
    return delta, m, v


def reference(x, positions, g_mix_pre, w_in, b_forget, g_q_lora, w_q_up, g_kv_lora, w_kv_up, g_mix_out, w_out, g_mix_post, g_ffn_pre, w_ffn_up, w_ffn_down, g_ffn_post, loss_target, m_g_mix_pre, m_w_in, m_b_forget, m_g_q_lora, m_w_q_up, m_g_kv_lora, m_w_kv_up, m_g_mix_out, m_w_out, m_g_mix_post, m_g_ffn_pre, m_w_ffn_up, m_w_ffn_down, m_g_ffn_post, v_g_mix_pre, v_w_in, v_b_forget, v_g_q_lora, v_w_q_up, v_g_kv_lora, v_w_kv_up, v_g_mix_out, v_w_out, v_g_mix_post, v_g_ffn_pre, v_w_ffn_up, v_w_ffn_down, v_g_ffn_post):
    given = dict(x=x, positions=positions, g_mix_pre=g_mix_pre, w_in=w_in, b_forget=b_forget, g_q_lora=g_q_lora, w_q_up=w_q_up, g_kv_lora=g_kv_lora, w_kv_up=w_kv_up, g_mix_out=g_mix_out, w_out=w_out, g_mix_post=g_mix_post, g_ffn_pre=g_ffn_pre, w_ffn_up=w_ffn_up, w_ffn_down=w_ffn_down, g_ffn_post=g_ffn_post, loss_target=loss_target, m_g_mix_pre=m_g_mix_pre, m_w_in=m_w_in, m_b_forget=m_b_forget, m_g_q_lora=m_g_q_lora, m_w_q_up=m_w_q_up, m_g_kv_lora=m_g_kv_lora, m_w_kv_up=m_w_kv_up, m_g_mix_out=m_g_mix_out, m_w_out=m_w_out, m_g_mix_post=m_g_mix_post, m_g_ffn_pre=m_g_ffn_pre, m_w_ffn_up=m_w_ffn_up, m_w_ffn_down=m_w_ffn_down, m_g_ffn_post=m_g_ffn_post, v_g_mix_pre=v_g_mix_pre, v_w_in=v_w_in, v_b_forget=v_b_forget, v_g_q_lora=v_g_q_lora, v_w_q_up=v_w_q_up, v_g_kv_lora=v_g_kv_lora, v_w_kv_up=v_w_kv_up, v_g_mix_out=v_g_mix_out, v_w_out=v_w_out, v_g_mix_post=v_g_mix_post, v_g_ffn_pre=v_g_ffn_pre, v_w_ffn_up=v_w_ffn_up, v_w_ffn_down=v_w_ffn_down, v_g_ffn_post=v_g_ffn_post)
    weights = {n: given[n] for n in TWIN_WEIGHTS}
    shared = {n: given[n] for n in SHARED_INPUTS}
    per_example = {n: given[n] for n in ['x', 'positions']}
    grad_fn = _jax.value_and_grad(_loss, argnums=(0, 1))

    def one_microbatch(ex, loss_target):
        ex = dict(ex)
        diff = ex.pop(TWIN_DIFF_INPUT)
        return grad_fn(weights, diff, {**shared, **ex}, loss_target)

    if N_MICROBATCH == 1:
        loss, (grad_w, grad_x) = one_microbatch(per_example, given["loss_target"])
    else:
        def body(carry, xs):
            loss_sum, grad_sum = carry
            l_k, (gw_k, gx_k) = one_microbatch(xs[0], xs[1])
            with _jax.named_scope("update"):
                return (loss_sum + l_k, _jax.tree.map(_jnp.add, grad_sum, gw_k)), gx_k

        init = (_jnp.zeros((), _jnp.float32), _jax.tree.map(_jnp.zeros_like, weights))
        (loss, grad_w), grad_x = _jax.lax.scan(body, init, (per_example, given["loss_target"]))
    with _jax.named_scope("update"):
        delta_w, new_m, new_v = {}, {}, {}
        for n in TWIN_WEIGHTS:
            delta_w[n], new_m[n], new_v[n] = _adamw(weights[n], grad_w[n], given["m_" + n], given["v_" + n])
    return (loss, grad_x, *[grad_w[n] for n in TWIN_WEIGHTS], *[delta_w[n] for n in TWIN_WEIGHTS],
            *[new_m[n] for n in TWIN_WEIGHTS], *[new_v[n] for n in TWIN_WEIGHTS])
```

```python
import functools
import math

import numpy as np
import jax
import jax.numpy as jnp
from jax import lax
from jax.experimental import pallas as pl
from jax.experimental.pallas import tpu as pltpu

F32 = jnp.float32
BF16 = jnp.bfloat16
MESH = pl.DeviceIdType.MESH

D_MODEL = 1024
DEPTH = 2
CHUNK = 64
GROUP = 256
HEAD = 64
N_HEADS = 4
Q_RANK = 256
KV_RANK = 128
ROPE_DIM = 32
D_FF = 4096
D_IN = 2980
D_INP = 3072
ROPE_BASE = 10000.0
EPS = 1e-6
LANES = 128
TQ = 128
NEG = -1e30

ADAM_LR, ADAM_B1, ADAM_B2, ADAM_EPS, ADAM_WD, ADAM_STEP = 0.001, 0.9, 0.999, 1e-08, 0.01, 10

OFF_FQ, OFF_FK, OFF_FV, OFF_CQ = 0, 2, 4, 6
OFF_RQ, OFF_RK, OFF_RV, OFF_RG = 8, 10, 12, 14
OFF_SQ, OFF_SK, OFF_SV = 16, 18, 20
OFF_CKV, OFF_MISC = 22, 23
FF_LANE, KR_LANE = 0, 64

VMEM_LIMIT = 56 * 1024 * 1024


def _tile(dim, pref):
    return pref if dim % pref == 0 else dim


def _cparams(sem, vmem=None):
    return pltpu.CompilerParams(dimension_semantics=sem, vmem_limit_bytes=vmem or VMEM_LIMIT)


def _dot(a, b):
    return jnp.dot(a, b, preferred_element_type=F32)


def _dot_nt(a, b):
    return lax.dot_general(a, b, (((1,), (1,)), ((), ())), preferred_element_type=F32)


def _dot_tn(a, b):
    return lax.dot_general(a, b, (((0,), (0,)), ((), ())), preferred_element_type=F32)


def _dot_exact(a, b):
    return jnp.dot(a, b, precision=lax.Precision.HIGHEST, preferred_element_type=F32)


def _matmul(a, b, *, name, ta=False, tb=False, out_dtype=F32, tm=512, tn=512, tk=1024,
            relu2=False, relu_of=None, also_bf16=False):
    if ta:
        kdim, m = a.shape
    else:
        m, kdim = a.shape
    n = b.shape[0] if tb else b.shape[1]
    tm, tn, tk = _tile(m, tm), _tile(n, tn), _tile(kdim, tk)
    nk = kdim // tk
    a_spec = pl.BlockSpec((tk, tm), lambda i, j, k: (k, i)) if ta else pl.BlockSpec((tm, tk), lambda i, j, k: (i, k))
    b_spec = pl.BlockSpec((tn, tk), lambda i, j, k: (j, k)) if tb else pl.BlockSpec((tk, tn), lambda i, j, k: (k, j))
    o_spec = pl.BlockSpec((tm, tn), lambda i, j, k: (i, j))
    two = relu2 or also_bf16

    def body(*refs):
        refs = list(refs)
        a_ref, b_ref = refs[0], refs[1]
        e_ref = refs[2] if relu_of is not None else None
        pos = 3 if relu_of is not None else 2
        o_ref = refs[pos]
        o2_ref = refs[pos + 1] if two else None
        acc_ref = refs[-1]
        k = pl.program_id(2)
        av = a_ref[...].astype(BF16)
        bv = b_ref[...].astype(BF16)
        if ta:
            part = _dot_tn(av, bv)
        elif tb:
            part = _dot_nt(av, bv)
        else:
            part = _dot(av, bv)

        @pl.when(k == 0)
        def _():
            acc_ref[...] = part

        @pl.when(k > 0)
        def _():
            acc_ref[...] += part

        @pl.when(k == nk - 1)
        def _():
            r = acc_ref[...]
            if relu_of is not None:
                r = r * (2.0 * jnp.maximum(e_ref[...], 0.0))
            o_ref[...] = r.astype(o_ref.dtype)
            if relu2:
                o2_ref[...] = jnp.square(jnp.maximum(r, 0.0)).astype(BF16)
            elif also_bf16:
                o2_ref[...] = r.astype(BF16)

    in_specs = [a_spec, b_spec]
    args = [a, b]
    if relu_of is not None:
        in_specs.append(o_spec)
        args.append(relu_of)
    out_shape = [jax.ShapeDtypeStruct((m, n), out_dtype)]
    out_specs = [o_spec]
    if two:
        out_shape.append(jax.ShapeDtypeStruct((m, n), BF16))
        out_specs.append(o_spec)
    res = pl.pallas_call(
        body, name=name, grid=(m // tm, n // tn, nk), in_specs=in_specs, out_specs=out_specs, out_shape=out_shape,
        scratch_shapes=[pltpu.VMEM((tm, tn), F32)],
        compiler_params=_cparams(("parallel", "parallel", "arbitrary")),
    )(*args)
    return res if two else res[0]


def _rms(x, g):
    r = lax.rsqrt(jnp.mean(x * x, axis=-1, keepdims=True) + EPS)
    return x * r * g


def _rms_bwd(x, g, dy):
    r = lax.rsqrt(jnp.mean(x * x, axis=-1, keepdims=True) + EPS)
    xh = x * r
    gdy = dy * g
    dx = r * (gdy - xh * jnp.mean(xh * gdy, axis=-1, keepdims=True))
    return dx, xh * dy


def _norm_fwd(x, g, *, name, resid=None, out_dtype=BF16):
    s, d = x.shape
    tr = _tile(s, 256)
    row = pl.BlockSpec((tr, d), lambda i: (i, 0))
    gsp = pl.BlockSpec((1, d), lambda i: (0, 0))

    def body(*refs):
        if resid is None:
            x_ref, g_ref, o_ref = refs
            o_ref[...] = _rms(x_ref[...], g_ref[...]).astype(o_ref.dtype)
        else:
            x_ref, g_ref, r_ref, o_ref = refs
            o_ref[...] = (r_ref[...] + _rms(x_ref[...], g_ref[...])).astype(o_ref.dtype)

    args = [x, g.reshape(1, d)] + ([] if resid is None else [resid])
    return pl.pallas_call(
        body, name=name, grid=(s // tr,), in_specs=[row, gsp] + ([] if resid is None else [row]),
        out_specs=row, out_shape=jax.ShapeDtypeStruct((s, d), out_dtype), compiler_params=_cparams(("parallel",)),
    )(*args)


def _norm_bwd(x, g, dy, *, name, add=None, out_dtype=F32):
    s, d = x.shape
    tr = _tile(s, 256)
    row = pl.BlockSpec((tr, d), lambda i: (i, 0))
    gsp = pl.BlockSpec((1, d), lambda i: (0, 0))

    def body(*refs):
        if add is None:
            x_ref, g_ref, dy_ref, dx_ref, dg_ref = refs
        else:
            x_ref, g_ref, dy_ref, add_ref, dx_ref, dg_ref = refs
        dx, gterm = _rms_bwd(x_ref[...], g_ref[...], dy_ref[...].astype(F32))
        if add is not None:
            dx = dx + add_ref[...]
        dx_ref[...] = dx.astype(dx_ref.dtype)

        @pl.when(pl.program_id(0) == 0)
        def _():
            dg_ref[...] = jnp.zeros_like(dg_ref)

        dg_ref[...] += jnp.sum(gterm, axis=0, keepdims=True)

    args = [x, g.reshape(1, d), dy] + ([] if add is None else [add])
    return pl.pallas_call(
        body, name=name, grid=(s // tr,), in_specs=[row, gsp, row] + ([] if add is None else [row]),
        out_specs=[row, gsp], out_shape=[jax.ShapeDtypeStruct((s, d), out_dtype), jax.ShapeDtypeStruct((1, d), F32)],
        compiler_params=_cparams(("arbitrary",)),
    )(*args)


def _loss_head(y, target):
    s, d = y.shape
    tr = _tile(s, 256)
    row = pl.BlockSpec((tr, d), lambda i: (i, 0))
    lsp = pl.BlockSpec((1, LANES), lambda i: (0, 0))

    def body(y_ref, t_ref, l_ref, dy_ref):
        e = y_ref[...] - t_ref[...]
        dy_ref[...] = e * (1.0 / d)

        @pl.when(pl.program_id(0) == 0)
        def _():
            l_ref[...] = jnp.zeros_like(l_ref)

        part = 0.5 * jnp.sum(jnp.mean(e * e, axis=-1, keepdims=True), axis=0, keepdims=True)
        l_ref[...] += jnp.broadcast_to(part, (1, LANES))

    return pl.pallas_call(
        body, name="loss_head", grid=(s // tr,), in_specs=[row, row], out_specs=[lsp, row],
        out_shape=[jax.ShapeDtypeStruct((1, LANES), F32), jax.ShapeDtypeStruct((s, d), F32)],
        compiler_params=_cparams(("arbitrary",)),
    )(y, target)


def _rope_tables(pos_col):
    s = pos_col.shape[0]
    tr = _tile(s, 512)
    f_mla = ROPE_BASE ** (-jnp.arange(ROPE_DIM // 2, dtype=F32) / (ROPE_DIM // 2))
    f_ret = ROPE_BASE ** (-jnp.arange(HEAD // 2, dtype=F32) / (HEAD // 2))
    fm = jnp.concatenate([jnp.zeros((64,), F32), f_mla, f_mla, jnp.zeros((32,), F32)]).reshape(1, LANES)
    fr = jnp.tile(jnp.concatenate([f_ret, f_ret]), 4).reshape(1, 2 * LANES)

    def body(p_ref, fm_ref, fr_ref, cm_ref, sm_ref, cr_ref, sr_ref):
        p = p_ref[...].astype(F32)
        am = p * fm_ref[...]
        ar = p * fr_ref[...]
        cm_ref[...] = jnp.cos(am)
        sm_ref[...] = jnp.sin(am)
        cr_ref[...] = jnp.cos(ar)
        sr_ref[...] = jnp.sin(ar)

    return pl.pallas_call(
        body, name="rope_tables", grid=(s // tr,),
        in_specs=[pl.BlockSpec((tr, 1), lambda i: (i, 0)), pl.BlockSpec((1, LANES), lambda i: (0, 0)),
                  pl.BlockSpec((1, 2 * LANES), lambda i: (0, 0))],
        out_specs=[pl.BlockSpec((tr, LANES), lambda i: (i, 0))] * 2 + [pl.BlockSpec((tr, 2 * LANES), lambda i: (i, 0))] * 2,
        out_shape=[jax.ShapeDtypeStruct((s, LANES), F32)] * 2 + [jax.ShapeDtypeStruct((s, 2 * LANES), F32)] * 2,
        compiler_params=_cparams(("parallel",)),
    )(pos_col, fm, fr)


def _lane(shape):
    return lax.broadcasted_iota(jnp.int32, shape, len(shape) - 1)


def _rot_mla(z):
    l = _lane(z.shape) % LANES
    n = z.shape[-1]
    return jnp.where(l < 80, -pltpu.roll(z, n - 16, 1), pltpu.roll(z, 16, 1))


def _rot_mla_t(y):
    l = _lane(y.shape) % LANES
    n = y.shape[-1]
    return jnp.where((l >= 64) & (l < 80), pltpu.roll(y, n - 16, 1),
                     jnp.where((l >= 80) & (l < 96), -pltpu.roll(y, 16, 1), 0.0))


def _rot_ret(z):
    l = _lane(z.shape) % HEAD
    n = z.shape[-1]
    return jnp.where(l < 32, -pltpu.roll(z, n - 32, 1), pltpu.roll(z, 32, 1))


def _rot_ret_t(y):
    l = _lane(y.shape) % HEAD
    n = y.shape[-1]
    return jnp.where(l < 32, pltpu.roll(y, n - 32, 1), -pltpu.roll(y, 32, 1))


def _log_sigmoid(x):
    return jnp.minimum(x, 0.0) - jnp.log1p(jnp.exp(-jnp.abs(x)))


def _fox_cum(proj, bias_row):
    s = proj.shape[0]
    nb = s // TQ

    def body(x_ref, b_ref, cc_ref, cr_ref, carry_ref):
        @pl.when(pl.program_id(0) == 0)
        def _():
            carry_ref[...] = jnp.zeros_like(carry_ref)

        ls = _log_sigmoid(x_ref[...] + b_ref[...])
        r = lax.broadcasted_iota(jnp.int32, (TQ, TQ), 0)
        c = lax.broadcasted_iota(jnp.int32, (TQ, TQ), 1)
        tri = (c <= r).astype(F32)
        cum = _dot_exact(tri, ls) + carry_ref[...]
        carry_ref[...] = cum[TQ - 1:TQ, :]
        cc_ref[...] = cum
        cr_ref[...] = cum.T[0:8, :]

    return pl.pallas_call(
        body, name="fox_cum", grid=(nb,),
        in_specs=[pl.BlockSpec((TQ, LANES), lambda i: (i, OFF_MISC)), pl.BlockSpec((1, LANES), lambda i: (0, 0))],
        out_specs=[pl.BlockSpec((TQ, LANES), lambda i: (i, 0)), pl.BlockSpec((8, TQ), lambda i: (0, i))],
        out_shape=[jax.ShapeDtypeStruct((s, LANES), F32), jax.ShapeDtypeStruct((8, s), F32)],
        scratch_shapes=[pltpu.VMEM((1, LANES), F32)],
        compiler_params=_cparams(("arbitrary",)),
    )(proj, bias_row)


def _fox_gate_bwd(dck, drs, proj, bias_row, dkr):
    s = proj.shape[0]
    nb = s // TQ

    def body(d_ref, r_ref, x_ref, b_ref, k_ref, o_ref, db_ref, carry_ref):
        @pl.when(pl.program_id(0) == 0)
        def _():
            carry_ref[...] = jnp.zeros_like(carry_ref)
            db_ref[...] = jnp.zeros_like(db_ref)

        rows = jnp.concatenate([d_ref[0], d_ref[1], jnp.zeros((TQ - 16, TQ), F32)], axis=0)
        t = rows.T
        l = _lane((TQ, LANES))
        r0, r1 = r_ref[0], r_ref[1]
        rsum = jnp.where(l == 0, r0[:, 0:1], jnp.where(l == 1, r0[:, HEAD:HEAD + 1],
                         jnp.where(l == 2, r1[:, 0:1], jnp.where(l == 3, r1[:, HEAD:HEAD + 1], 0.0))))
        dcum = rsum - jnp.where(l < 2, t, pltpu.roll(t, LANES - 6, 1))
        r = lax.broadcasted_iota(jnp.int32, (TQ, TQ), 0)
        c = lax.broadcasted_iota(jnp.int32, (TQ, TQ), 1)
        triu = (c >= r).astype(F32)
        rc = _dot_exact(triu, dcum) + carry_ref[...]
        carry_ref[...] = rc[0:1, :]
        f = x_ref[...] + b_ref[...]
        sig_neg = 1.0 / (1.0 + jnp.exp(f))
        df = jnp.where(l < N_HEADS, rc * sig_neg, 0.0)
        db_ref[...] += jnp.sum(df, axis=0, keepdims=True)
        o_ref[...] = (df + k_ref[...]).astype(o_ref.dtype)

    rev = lambda i: nb - 1 - i
    return pl.pallas_call(
        body, name="fox_gate_bwd", grid=(nb,),
        in_specs=[pl.BlockSpec((2, 8, TQ), lambda i: (0, 0, rev(i))), pl.BlockSpec((2, TQ, LANES), lambda i: (0, rev(i), 0)),
                  pl.BlockSpec((TQ, LANES), lambda i: (rev(i), OFF_MISC)),
                  pl.BlockSpec((1, LANES), lambda i: (0, 0)), pl.BlockSpec((TQ, LANES), lambda i: (rev(i), 0))],
        out_specs=[pl.BlockSpec((TQ, LANES), lambda i: (rev(i), 0)), pl.BlockSpec((1, LANES), lambda i: (0, 0))],
        out_shape=[jax.ShapeDtypeStruct((s, LANES), BF16), jax.ShapeDtypeStruct((1, LANES), F32)],
        scratch_shapes=[pltpu.VMEM((1, LANES), F32)],
        compiler_params=_cparams(("arbitrary",)),
    )(dck, drs, proj, bias_row, dkr)


def _mla_prep(proj, cos_m, sin_m, g_q, g_kv, wq, wk, wv):
    s = proj.shape[0]
    tr = _tile(s, 256)

    def body(cq_ref, ckv_ref, misc_ref, cos_ref, sin_ref, gq_ref, gkv_ref, wq_ref, wk_ref, wv_ref,
             q_ref, k_ref, v_ref, cqn_ref, ckvn_ref):
        cos4 = jnp.tile(cos_ref[...], (1, 4))
        sin4 = jnp.tile(sin_ref[...], (1, 4))
        cqn = _rms(cq_ref[...], gq_ref[...]).astype(BF16)
        ckvn = _rms(ckv_ref[...], gkv_ref[...]).astype(BF16)
        cqn_ref[...] = cqn
        ckvn_ref[...] = ckvn
        zq = _dot(cqn, wq_ref[...])
        q_ref[...] = (zq * cos4 + _rot_mla(zq) * sin4).astype(BF16)
        l = _lane((tr, LANES))
        kr = jnp.where((l >= KR_LANE) & (l < KR_LANE + ROPE_DIM), misc_ref[...], 0.0)
        zk = _dot(ckvn, wk_ref[...]) + jnp.tile(kr, (1, 4))
        k_ref[...] = (zk * cos4 + _rot_mla(zk) * sin4).astype(BF16)
        v_ref[...] = _dot(ckvn, wv_ref[...]).astype(BF16)

    full = lambda a: pl.BlockSpec(a.shape, lambda i: (0, 0))
    rowb = lambda w: pl.BlockSpec((tr, w), lambda i: (i, 0))
    gq2, gkv2 = g_q.reshape(1, Q_RANK), g_kv.reshape(1, KV_RANK)
    return pl.pallas_call(
        body, name="mla_prep", grid=(s // tr,),
        in_specs=[pl.BlockSpec((tr, 256), lambda i: (i, OFF_CQ // 2)), pl.BlockSpec((tr, LANES), lambda i: (i, OFF_CKV)),
                  pl.BlockSpec((tr, LANES), lambda i: (i, OFF_MISC)), rowb(LANES), rowb(LANES),
                  full(gq2), full(gkv2), full(wq), full(wk), full(wv)],
        out_specs=[rowb(512), rowb(512), rowb(512), rowb(256), rowb(128)],
        out_shape=[jax.ShapeDtypeStruct((s, 512), BF16), jax.ShapeDtypeStruct((s, 512), BF16), jax.ShapeDtypeStruct((s, 512), BF16),
                   jax.ShapeDtypeStruct((s, 256), BF16), jax.ShapeDtypeStruct((s, 128), BF16)],
        compiler_params=_cparams(("parallel",)),
    )(proj, proj, proj, cos_m, sin_m, gq2, gkv2, wq, wk, wv)


def _mla_prep_bwd(dq, dk, dv, proj, cqn, ckvn, cos_m, sin_m, g_q, g_kv, wq, wk, wv):
    s = proj.shape[0]
    tr = _tile(s, 256)

    def body(dq_ref, dk_ref, dv_ref, cq_ref, ckv_ref, cqn_ref, ckvn_ref, cos_ref, sin_ref, gq_ref, gkv_ref,
             wq_ref, wk_ref, wv_ref, dcq_ref, dckv_ref, dkr_ref, dwq_ref, dwk_ref, dwv_ref, dgq_ref, dgkv_ref):
        @pl.when(pl.program_id(0) == 0)
        def _():
            for r in (dwq_ref, dwk_ref, dwv_ref, dgq_ref, dgkv_ref):
                r[...] = jnp.zeros_like(r)

        cos4 = jnp.tile(cos_ref[...], (1, 4))
        sin4 = jnp.tile(sin_ref[...], (1, 4))
        dqv = dq_ref[...]
        dzq = dqv * cos4 + _rot_mla_t(dqv * sin4)
        dkv_ = dk_ref[...]
        dzk = dkv_ * cos4 + _rot_mla_t(dkv_ * sin4)
        l = _lane((tr, LANES))
        in_rope = (l >= KR_LANE) & (l < KR_LANE + ROPE_DIM)
        dkr = dzk[:, 0:128] + dzk[:, 128:256] + dzk[:, 256:384] + dzk[:, 384:512]
        dkr_ref[...] = jnp.where(in_rope, dkr, 0.0)
        dzq_b = dzq.astype(BF16)
        dzk_b = dzk.astype(BF16)
        dv_b = dv_ref[...].astype(BF16)
        dcqn = _dot_nt(dzq_b, wq_ref[...])
        dckvn = _dot_nt(dzk_b, wk_ref[...]) + _dot_nt(dv_b, wv_ref[...])
        dwq_ref[...] += _dot_tn(cqn_ref[...], dzq_b)
        dwk_ref[...] += _dot_tn(ckvn_ref[...], dzk_b)
        dwv_ref[...] += _dot_tn(ckvn_ref[...], dv_b)
        dcq, gq_term = _rms_bwd(cq_ref[...], gq_ref[...], dcqn)
        dckv, gkv_term = _rms_bwd(ckv_ref[...], gkv_ref[...], dckvn)
        dcq_ref[...] = dcq.astype(BF16)
        dckv_ref[...] = dckv.astype(BF16)
        dgq_ref[...] += jnp.sum(gq_term, axis=0, keepdims=True)
        dgkv_ref[...] += jnp.sum(gkv_term, axis=0, keepdims=True)

    full = lambda shp: pl.BlockSpec(shp, lambda i: (0, 0))
    rowb = lambda w: pl.BlockSpec((tr, w), lambda i: (i, 0))
    gq2, gkv2 = g_q.reshape(1, Q_RANK), g_kv.reshape(1, KV_RANK)
    return pl.pallas_call(
        body, name="mla_prep_bwd", grid=(s // tr,),
        in_specs=[rowb(512), rowb(512), rowb(512),
                  pl.BlockSpec((tr, 256), lambda i: (i, OFF_CQ // 2)), pl.BlockSpec((tr, LANES), lambda i: (i, OFF_CKV)),
                  rowb(256), rowb(128), rowb(LANES), rowb(LANES), full((1, Q_RANK)), full((1, KV_RANK)),
                  full(wq.shape), full(wk.shape), full(wv.shape)],
        out_specs=[rowb(256), rowb(128), rowb(128), full(wq.shape), full(wk.shape), full(wv.shape),
                   full((1, Q_RANK)), full((1, KV_RANK))],
        out_shape=[jax.ShapeDtypeStruct((s, 256), BF16), jax.ShapeDtypeStruct((s, 128), BF16), jax.ShapeDtypeStruct((s, 128), F32),
                   jax.ShapeDtypeStruct(wq.shape, F32), jax.ShapeDtypeStruct(wk.shape, F32), jax.ShapeDtypeStruct(wv.shape, F32),
                   jax.ShapeDtypeStruct((1, Q_RANK), F32), jax.ShapeDtypeStruct((1, KV_RANK), F32)],
        compiler_params=_cparams(("arbitrary",)),
    )(dq, dk, dv, proj, proj, cqn, ckvn, cos_m, sin_m, gq2, gkv2, wq, wk, wv)


def _ret_prep(proj, cos_r, sin_r):
    s = proj.shape[0]
    tr = _tile(s, 256)

    def body(q_ref, k_ref, cos_ref, sin_ref, qo_ref, ko_ref):
        cos, sin = cos_ref[...], sin_ref[...]
        q, k = q_ref[...], k_ref[...]
        qo_ref[...] = (q * cos + _rot_ret(q) * sin).astype(BF16)
        ko_ref[...] = ((k * cos + _rot_ret(k) * sin) * (HEAD ** -0.5)).astype(BF16)

    rowb = pl.BlockSpec((tr, 256), lambda i: (i, 0))
    return pl.pallas_call(
        body, name="ret_prep", grid=(s // tr,),
        in_specs=[pl.BlockSpec((tr, 256), lambda i: (i, OFF_RQ // 2)), pl.BlockSpec((tr, 256), lambda i: (i, OFF_RK // 2)), rowb, rowb],
        out_specs=[rowb, rowb], out_shape=[jax.ShapeDtypeStruct((s, 256), BF16)] * 2,
        compiler_params=_cparams(("parallel",)),
    )(proj, proj, cos_r, sin_r)


def _ret_prep_bwd(dq, dk, cos_r, sin_r):
    s = dq.shape[0]
    tr = _tile(s, 256)

    def body(dq_ref, dk_ref, cos_ref, sin_ref, qo_ref, ko_ref):
        cos, sin = cos_ref[...], sin_ref[...]
        q, k = dq_ref[...], dk_ref[...] * (HEAD ** -0.5)
        qo_ref[...] = (q * cos + _rot_ret_t(q * sin)).astype(BF16)
        ko_ref[...] = (k * cos + _rot_ret_t(k * sin)).astype(BF16)

    rowb = pl.BlockSpec((tr, 256), lambda i: (i, 0))
    return pl.pallas_call(
        body, name="ret_prep_bwd", grid=(s // tr,), in_specs=[rowb] * 4, out_specs=[rowb, rowb],
        out_shape=[jax.ShapeDtypeStruct((s, 256), BF16)] * 2, compiler_params=_cparams(("parallel",)),
    )(dq, dk, cos_r, sin_r)


_LOG_GAMMA = [float(np.log1p(-np.float32(2.0) ** np.float32(-5.0 - h))) for h in range(N_HEADS)]
_MLA_SCALE = float((HEAD + ROPE_DIM) ** -0.5)
_QK_SCALE = float(HEAD ** -0.5)


def _split3(x):
    h = x.astype(BF16)
    r = x - h.astype(F32)
    m = r.astype(BF16)
    lo = (r - m.astype(F32)).astype(BF16)
    return h, m, lo


def _dot3(x, u):
    h, m, lo = _split3(x)
    return _dot(h, u) + _dot(m, u) + _dot(lo, u)


def _head_pick(block, head, axis):
    idx = lax.broadcasted_iota(jnp.int32, block.shape, axis)
    return jnp.sum(jnp.where(idx == head, block, 0.0), axis=axis, keepdims=True)


def _mixer_common(mode, p, i):
    row = lax.broadcasted_iota(jnp.int32, (TQ, TQ), 0)
    col = lax.broadcasted_iota(jnp.int32, (TQ, TQ), 1)
    if mode == "fox":
        dmask = col <= row
    elif mode == "sb":
        dmask = col < row
    else:
        dmask = (col // CHUNK) <= (row // CHUNK)
    return row, col, dmask


def _log_gamma_of(head):
    lg = jnp.float32(_LOG_GAMMA[3])
    for h in (2, 1, 0):
        lg = jnp.where(head == h, jnp.float32(_LOG_GAMMA[h]), lg)
    return lg


def _mixer_specs(mode, s, q_off, k_off, v_off):
    nhb = 1 if mode == "mla" else 2
    q_spec = pl.BlockSpec((TQ, LANES), lambda p, i: (i, q_off + p))
    k_spec = pl.BlockSpec((s, LANES), lambda p, i: (0, k_off + p))
    v_spec = pl.BlockSpec((s, LANES), lambda p, i: (0, v_off + p))
    return nhb, N_HEADS // nhb, q_spec, k_spec, v_spec


def _mixer_fwd(mode, qa, q_off, ka, k_off, va, v_off, *, cum_col=None, cum_row=None):
    s = qa.shape[0]
    nq = s // TQ
    nhb, nblk, q_spec, k_spec, v_spec = _mixer_specs(mode, s, q_off, k_off, v_off)
    softmax = mode in ("fox", "mla")
    has_stat = mode != "ret"

    def body(*refs):
        refs = list(refs)
        q_ref, k_ref, v_ref = refs[:3]
        refs = refs[3:]
        if mode == "fox":
            cc_ref, cr_ref = refs[:2]
            refs = refs[2:]
        o_ref = refs[0]
        st_ref = refs[1] if has_stat else None
        p = pl.program_id(0)
        i = pl.program_id(1)
        row, col, dmask = _mixer_common(mode, p, i)
        lane = _lane((1, LANES))
        outs, stats = [], []
        for hh in range(nhb):
            head = nhb * p + hh
            if nhb == 1:
                qh = q_ref[...]
            else:
                qh = jnp.where((lane // HEAD) == hh, q_ref[...].astype(F32), 0.0).astype(BF16)
            if mode == "fox":
                cq = _head_pick(cc_ref[...], head, 1)
            if mode == "ret":
                lg = _log_gamma_of(head)
                dist = (row - col).astype(F32)

            def kv(j):
                js = pl.ds(pl.multiple_of(j * TQ, TQ), TQ)
                return k_ref[js, :], v_ref[js, :], js

            if softmax:
                def step(j, carry, diag):
                    m, l, acc = carry
                    kj, vj, js = kv(j)
                    sc = _dot_nt(qh, kj)
                    if mode == "fox":
                        ck = _head_pick(cr_ref[:, js], head, 0)
                        sc = sc * _QK_SCALE + (cq - ck)
                    else:
                        sc = sc * _MLA_SCALE
                    if diag:
                        sc = jnp.where(dmask, sc, NEG)
                    m_new = jnp.maximum(m, jnp.max(sc, axis=-1, keepdims=True))
                    alpha = jnp.exp(m - m_new)
                    pr = jnp.exp(sc - m_new)
                    l = alpha * l + jnp.sum(pr, axis=-1, keepdims=True)
                    acc = alpha * acc + _dot(pr.astype(BF16), vj)
                    return m_new, l, acc

                carry = (jnp.full((TQ, 1), NEG, F32), jnp.zeros((TQ, 1), F32), jnp.zeros((TQ, LANES), F32))
                carry = lax.fori_loop(0, i, lambda j, c: step(j, c, False), carry)
                m, l, acc = step(i, carry, True)
                outs.append(acc / l)
                stats.append(m + jnp.log(l))
            elif mode == "ret":
                def step(j, acc, diag):
                    kj, vj, js = kv(j)
                    sc = _dot_nt(qh, kj)
                    if diag:
                        dec = jnp.where(dmask, jnp.exp(lg * jnp.abs(dist)), 0.0)
                    else:
                        dec = jnp.exp(lg * (dist + ((i - j) * TQ).astype(F32)))
                    return acc + _dot((sc * dec).astype(BF16), vj)

                acc = lax.fori_loop(0, i, lambda j, c: step(j, c, False), jnp.zeros((TQ, LANES), F32))
                outs.append(step(i, acc, True))
            else:
                u_after = (row > col).astype(BF16)

                def step(j, carry, diag):
                    run, acc = carry
                    kj, vj, js = kv(j)
                    z = _dot_nt(qh, kj) * _QK_SCALE
                    lp = jnp.log1p(jnp.exp(-jnp.abs(z)))
                    log_stay = jnp.minimum(-z, 0.0) - lp
                    if diag:
                        log_stay = jnp.where(dmask, log_stay, 0.0)
                    later = _dot3(log_stay, u_after) + run
                    w = jnp.exp(jnp.minimum(z, 0.0) - lp + later)
                    if diag:
                        w = jnp.where(dmask, w, 0.0)
                    run = run + jnp.sum(log_stay, axis=-1, keepdims=True)
                    return run, acc + _dot(w.astype(BF16), vj)

                carry = step(i, (jnp.zeros((TQ, 1), F32), jnp.zeros((TQ, LANES), F32)), True)
                carry = lax.fori_loop(0, i, lambda t, c: step(i - 1 - t, c, False), carry)
                outs.append(carry[1])
                stats.append(carry[0])
        hm0 = (lane // HEAD) == 0
        pick = (lambda a: jnp.broadcast_to(a[0], (TQ, LANES))) if nhb == 1 else (lambda a: jnp.where(hm0, a[0], a[1]))
        o_ref[...] = pick(outs)
        if has_stat:
            st_ref[0] = pick(stats)

    in_specs = [q_spec, k_spec, v_spec]
    args = [qa, ka, va]
    if mode == "fox":
        in_specs += [pl.BlockSpec((TQ, LANES), lambda p, i: (i, 0)), pl.BlockSpec((8, s), lambda p, i: (0, 0))]
        args += [cum_col, cum_row]
    out_specs = [pl.BlockSpec((TQ, LANES), lambda p, i: (i, p))]
    out_shape = [jax.ShapeDtypeStruct((s, nblk * LANES), F32)]
    if has_stat:
        out_specs.append(pl.BlockSpec((1, TQ, LANES), lambda p, i: (p, i, 0)))
        out_shape.append(jax.ShapeDtypeStruct((nblk, s, LANES), F32))
    res = pl.pallas_call(
        body, name=mode + "_fwd", grid=(nblk, nq), in_specs=in_specs, out_specs=out_specs, out_shape=out_shape,
        compiler_params=_cparams(("parallel", "parallel")),
    )(*args)
    return res if has_stat else (res[0], None)


def _mixer_bwd(mode, qa, q_off, ka, k_off, va, v_off, o, do, *, stat=None, cum_col=None, cum_row=None):
    s = qa.shape[0]
    nq = s // TQ
    nhb, nblk, q_spec, k_spec, v_spec = _mixer_specs(mode, s, q_off, k_off, v_off)
    softmax = mode in ("fox", "mla")
    has_stat = mode != "ret"

    def body(*refs):
        refs = list(refs)
        q_ref, k_ref, v_ref, o_ref, do_ref = refs[:5]
        refs = refs[5:]
        if has_stat:
            st_ref = refs[0]
            refs = refs[1:]
        if mode == "fox":
            cc_ref, cr_ref = refs[:2]
            refs = refs[2:]
        dq_ref, dk_ref, dv_ref = refs[:3]
        dck_ref, drs_ref = refs[3:5] if mode == "fox" else (None, None)
        rsums = []
        p = pl.program_id(0)
        i = pl.program_id(1)

        @pl.when(i == 0)
        def _():
            dk_ref[...] = jnp.zeros_like(dk_ref)
            dv_ref[...] = jnp.zeros_like(dv_ref)
            if mode == "fox":
                dck_ref[...] = jnp.zeros_like(dck_ref)

        row, col, dmask = _mixer_common(mode, p, i)
        lane = _lane((1, LANES))
        dqs = []
        for hh in range(nhb):
            head = nhb * p + hh
            dov = do_ref[...]
            if nhb == 1:
                qh = q_ref[...]
                doh = dov.astype(BF16)
                delta = jnp.sum(dov * o_ref[...], axis=-1, keepdims=True)
            else:
                hm = (lane // HEAD) == hh
                qh = jnp.where(hm, q_ref[...].astype(F32), 0.0).astype(BF16)
                doh = jnp.where(hm, dov, 0.0).astype(BF16)
                delta = jnp.sum(jnp.where(hm, dov * o_ref[...], 0.0), axis=-1, keepdims=True)
            if softmax:
                lse_h = st_ref[0][:, hh * HEAD:hh * HEAD + 1]
            if mode == "fox":
                cq = _head_pick(cc_ref[...], head, 1)
            if mode == "ret":
                lg = _log_gamma_of(head)
                dist = (row - col).astype(F32)

            def kv(j):
                js = pl.ds(pl.multiple_of(j * TQ, TQ), TQ)
                return k_ref[js, :], v_ref[js, :], js

            def emit(j, js, ds_b, pr_b, dq):
                dq = dq + _dot(ds_b, k_ref[js, :])
                dk_ref[js, :] += _dot_tn(ds_b, qh)
                dv_ref[js, :] += _dot_tn(pr_b, doh)
                return dq

            if softmax:
                scale = _QK_SCALE if mode == "fox" else _MLA_SCALE

                def step(j, carry, diag):
                    dq, rsum = carry
                    kj, vj, js = kv(j)
                    sc = _dot_nt(qh, kj)
                    if mode == "fox":
                        ck = _head_pick(cr_ref[:, js], head, 0)
                        sc = sc * scale + (cq - ck)
                    else:
                        sc = sc * scale
                    if diag:
                        sc = jnp.where(dmask, sc, NEG)
                    pr = jnp.exp(sc - lse_h)
                    dp = _dot_nt(doh, vj)
                    ds = pr * (dp - delta)
                    if mode == "fox":
                        dck_ref[0, hh:hh + 1, js] += jnp.sum(ds, axis=0, keepdims=True)
                        rsum = rsum + jnp.sum(ds, axis=-1, keepdims=True)
                    return emit(j, js, (ds * scale).astype(BF16), pr.astype(BF16), dq), rsum

                carry = lax.fori_loop(0, i, lambda j, c: step(j, c, False), (jnp.zeros((TQ, LANES), F32), jnp.zeros((TQ, 1), F32)))
                dq, rsum = step(i, carry, True)
                dqs.append(dq)
                rsums.append(rsum)
            elif mode == "ret":
                def step(j, dq, diag):
                    kj, vj, js = kv(j)
                    sc = _dot_nt(qh, kj)
                    if diag:
                        dec = jnp.where(dmask, jnp.exp(lg * jnp.abs(dist)), 0.0)
                    else:
                        dec = jnp.exp(lg * (dist + ((i - j) * TQ).astype(F32)))
                    dp = _dot_nt(doh, vj)
                    return emit(j, js, (dp * dec).astype(BF16), (sc * dec).astype(BF16), dq)

                dq = lax.fori_loop(0, i, lambda j, c: step(j, c, False), jnp.zeros((TQ, LANES), F32))
                dqs.append(step(i, dq, True))
            else:
                u_upto = (row <= col).astype(BF16)
                u_before = (row < col).astype(BF16)
                total = st_ref[0][:, hh * HEAD:hh * HEAD + 1]

                def step(j, carry, diag):
                    seen, gsum, dq = carry
                    kj, vj, js = kv(j)
                    z = _dot_nt(qh, kj) * _QK_SCALE
                    lp = jnp.log1p(jnp.exp(-jnp.abs(z)))
                    log_stay = jnp.minimum(-z, 0.0) - lp
                    if diag:
                        log_stay = jnp.where(dmask, log_stay, 0.0)
                    later = (total - seen) - _dot3(log_stay, u_upto)
                    log_beta = jnp.minimum(z, 0.0) - lp
                    w = jnp.exp(log_beta + later)
                    if diag:
                        w = jnp.where(dmask, w, 0.0)
                    g = _dot_nt(doh, vj) * w
                    before = gsum + _dot3(g, u_before)
                    beta = jnp.exp(log_beta)
                    dz = g * (1.0 - beta) - beta * before
                    if diag:
                        dz = jnp.where(dmask, dz, 0.0)
                    seen = seen + jnp.sum(log_stay, axis=-1, keepdims=True)
                    gsum = gsum + jnp.sum(g, axis=-1, keepdims=True)
                    dq = emit(j, js, (dz * _QK_SCALE).astype(BF16), w.astype(BF16), dq)
                    return seen, gsum, dq

                zero1 = jnp.zeros((TQ, 1), F32)
                carry = lax.fori_loop(0, i, lambda j, c: step(j, c, False), (zero1, zero1, jnp.zeros((TQ, LANES), F32)))
                dqs.append(step(i, carry, True)[2])
        if nhb == 1:
            dq_ref[...] = dqs[0]
        else:
            dq_ref[...] = jnp.where((lane // HEAD) == 0, dqs[0], dqs[1])
        if mode == "fox":
            drs_ref[0] = jnp.where((lane // HEAD) == 0, rsums[0], rsums[1])

    pair_blk = pl.BlockSpec((TQ, LANES), lambda p, i: (i, p))
    full_blk = pl.BlockSpec((s, LANES), lambda p, i: (0, p))
    stat_blk = pl.BlockSpec((1, TQ, LANES), lambda p, i: (p, i, 0))
    in_specs = [q_spec, k_spec, v_spec, pair_blk, pair_blk]
    args = [qa, ka, va, o, do]
    if has_stat:
        in_specs.append(stat_blk)
        args.append(stat)
    if mode == "fox":
        in_specs += [pl.BlockSpec((TQ, LANES), lambda p, i: (i, 0)), pl.BlockSpec((8, s), lambda p, i: (0, 0))]
        args += [cum_col, cum_row]
    out_specs = [pair_blk, full_blk, full_blk]
    out_shape = [jax.ShapeDtypeStruct((s, nblk * LANES), F32)] * 3
    if mode == "fox":
        out_specs += [pl.BlockSpec((1, 8, s), lambda p, i: (p, 0, 0)), stat_blk]
        out_shape += [jax.ShapeDtypeStruct((2, 8, s), F32), jax.ShapeDtypeStruct((2, s, LANES), F32)]
    return pl.pallas_call(
        body, name=mode + "_bwd", grid=(nblk, nq), in_specs=in_specs, out_specs=out_specs, out_shape=out_shape,
        compiler_params=_cparams(("parallel", "arbitrary")),
    )(*args)


def _seg_mean_matrix():
    r = lax.broadcasted_iota(jnp.int32, (GROUP, GROUP), 0)
    c = lax.broadcasted_iota(jnp.int32, (GROUP, GROUP), 1)
    return jnp.where((r // HEAD) == (c // HEAD), 1.0 / HEAD, 0.0).astype(F32)


def _sigmoid(x):
    return 1.0 / (1.0 + jnp.exp(-x))


def _mix_post(oa, ob, oc, od, proj, g):
    s = oa.shape[0]
    tr = _tile(s, 256)

    def body(a_ref, b_ref, c_ref, d_ref, rg_ref, g_ref, o_ref):
        gv = g_ref[...]
        o_ref[:, 0:GROUP] = _rms(a_ref[...], gv[:, 0:GROUP]).astype(BF16)
        o_ref[:, GROUP:2 * GROUP] = _rms(b_ref[...], gv[:, GROUP:2 * GROUP]).astype(BF16)
        seg = _seg_mean_matrix()
        c = c_ref[...]
        cen = c - _dot_exact(c, seg)
        n = cen * lax.rsqrt(_dot_exact(cen * cen, seg) + EPS)
        rg = rg_ref[...]
        o_ref[:, 2 * GROUP:3 * GROUP] = (n * gv[:, 2 * GROUP:3 * GROUP] * (rg * _sigmoid(rg))).astype(BF16)
        o_ref[:, 3 * GROUP:] = _rms(d_ref[...], gv[:, 3 * GROUP:]).astype(BF16)

    blk = pl.BlockSpec((tr, GROUP), lambda i: (i, 0))
    return pl.pallas_call(
        body, name="mix_post", grid=(s // tr,),
        in_specs=[blk] * 4 + [pl.BlockSpec((tr, GROUP), lambda i: (i, OFF_RG // 2)), pl.BlockSpec((1, D_MODEL), lambda i: (0, 0))],
        out_specs=pl.BlockSpec((tr, D_MODEL), lambda i: (i, 0)), out_shape=jax.ShapeDtypeStruct((s, D_MODEL), BF16),
        compiler_params=_cparams(("parallel",)),
    )(oa, ob, oc, od, proj, g.reshape(1, D_MODEL))


def _mix_post_bwd(dmixed, oa, ob, oc, od, proj, g):
    s = oa.shape[0]
    tr = _tile(s, 256)

    def body(dm_ref, a_ref, b_ref, c_ref, d_ref, rg_ref, g_ref, da_ref, db_ref, dc_ref, dd_ref, drg_ref, dg_ref):
        @pl.when(pl.program_id(0) == 0)
        def _():
            dg_ref[...] = jnp.zeros_like(dg_ref)

        gv = g_ref[...]
        dm = dm_ref[...]
        for k, (x_ref, dx_ref) in enumerate(((a_ref, da_ref), (b_ref, db_ref), (None, None), (d_ref, dd_ref))):
            if x_ref is None:
                continue
            cols = slice(k * GROUP, (k + 1) * GROUP)
            dx, gterm = _rms_bwd(x_ref[...], gv[:, cols], dm[:, cols])
            dx_ref[...] = dx
            dg_ref[:, cols] += jnp.sum(gterm, axis=0, keepdims=True)
        cols = slice(2 * GROUP, 3 * GROUP)
        seg = _seg_mean_matrix()
        c = c_ref[...]
        cen = c - _dot_exact(c, seg)
        rstd = lax.rsqrt(_dot_exact(cen * cen, seg) + EPS)
        n = cen * rstd
        rg = rg_ref[...]
        sg = _sigmoid(rg)
        gate = rg * sg
        dy = dm[:, cols]
        gc = gv[:, cols]
        dn = dy * gc * gate
        dg_ref[:, cols] += jnp.sum(dy * n * gate, axis=0, keepdims=True)
        drg_ref[...] = (dy * n * gc * (sg * (1.0 + rg * (1.0 - sg)))).astype(BF16)
        dc_ref[...] = rstd * (dn - _dot_exact(dn, seg) - n * _dot_exact(dn * n, seg))

    blk = pl.BlockSpec((tr, GROUP), lambda i: (i, 0))
    gsp = pl.BlockSpec((1, D_MODEL), lambda i: (0, 0))
    return pl.pallas_call(
        body, name="mix_post_bwd", grid=(s // tr,),
        in_specs=[pl.BlockSpec((tr, D_MODEL), lambda i: (i, 0))] + [blk] * 4 + [pl.BlockSpec((tr, GROUP), lambda i: (i, OFF_RG // 2)), gsp],
        out_specs=[blk] * 5 + [gsp],
        out_shape=[jax.ShapeDtypeStruct((s, GROUP), F32)] * 4 + [jax.ShapeDtypeStruct((s, GROUP), BF16), jax.ShapeDtypeStruct((1, D_MODEL), F32)],
        compiler_params=_cparams(("arbitrary",)),
    )(dmixed, oa, ob, oc, od, proj, g.reshape(1, D_MODEL))


def _pack_w_in(w):
    z = lambda n: jnp.zeros((w.shape[0], n), w.dtype)
    misc = jnp.concatenate([w[:, 768:772], z(KR_LANE - N_HEADS), w[:, 1156:1188], z(LANES - KR_LANE - ROPE_DIM)], axis=1)
    return jnp.concatenate([w[:, 0:768], w[:, 772:1028], w[:, 1188:2980], w[:, 1028:1156], misc], axis=1)


def _unpack_dw_in(d):
    m = OFF_MISC * LANES
    return jnp.concatenate([d[:, 0:768], d[:, m:m + N_HEADS], d[:, 768:1024], d[:, OFF_CKV * LANES:m],
                            d[:, m + KR_LANE:m + KR_LANE + ROPE_DIM], d[:, 1024:OFF_CKV * LANES]], axis=1)


def _pack_w_q(w):
    return jnp.pad(w.reshape(Q_RANK, N_HEADS, HEAD + ROPE_DIM), ((0, 0), (0, 0), (0, LANES - HEAD - ROPE_DIM))).reshape(Q_RANK, 4 * LANES)


def _unpack_dw_q(d):
    return d.reshape(Q_RANK, N_HEADS, LANES)[:, :, :HEAD + ROPE_DIM].reshape(Q_RANK, N_HEADS * (HEAD + ROPE_DIM))


def _pack_w_kv(w):
    w4 = w.reshape(KV_RANK, N_HEADS, 2 * HEAD)
    widen = lambda a: jnp.pad(a, ((0, 0), (0, 0), (0, LANES - HEAD))).reshape(KV_RANK, N_HEADS * LANES)
    return widen(w4[:, :, :HEAD]), widen(w4[:, :, HEAD:])


def _unpack_dw_kv(dk, dv):
    narrow = lambda a: a.reshape(KV_RANK, N_HEADS, LANES)[:, :, :HEAD]
    return jnp.concatenate([narrow(dk), narrow(dv)], axis=2).reshape(KV_RANK, 2 * N_HEADS * HEAD)


def _narrow_heads(a):
    return a.reshape(a.shape[0], N_HEADS, LANES)[:, :, :HEAD].reshape(a.shape[0], N_HEADS * HEAD)


def _widen_heads(a):
    return jnp.pad(a.reshape(a.shape[0], N_HEADS, HEAD), ((0, 0), (0, 0), (0, LANES - HEAD))).reshape(a.shape[0], N_HEADS * LANES)


def _layer_fwd(x, lw, tabs, tag):
    cos_m, sin_m, cos_r, sin_r = tabs
    h1 = _norm_fwd(x, lw["g_mix_pre"], name=tag + "pre_norm")
    proj, projb = _matmul(h1, lw["w_in"], name=tag + "in_proj", also_bf16=True)
    bias_row = jnp.pad(lw["b_forget"], (FF_LANE, LANES - N_HEADS - FF_LANE)).reshape(1, LANES)
    cum_col, cum_row = _fox_cum(proj, bias_row)
    oa, lse_a = _mixer_fwd("fox", projb, OFF_FQ, projb, OFF_FK, projb, OFF_FV, cum_col=cum_col, cum_row=cum_row)
    qm, km, vm, cqn, ckvn = _mla_prep(proj, cos_m, sin_m, lw["g_q_lora"], lw["g_kv_lora"], lw["wq"], lw["wk"], lw["wv"])
    ob_wide, lse_b = _mixer_fwd("mla", qm, 0, km, 0, vm, 0)
    ob = _narrow_heads(ob_wide)
    qr, kr = _ret_prep(proj, cos_r, sin_r)
    oc, _ = _mixer_fwd("ret", qr, 0, kr, 0, projb, OFF_RV)
    od, tot_d = _mixer_fwd("sb", projb, OFF_SQ, projb, OFF_SK, projb, OFF_SV)
    mixed = _mix_post(oa, ob, oc, od, proj, lw["g_mix_out"])
    mix = _matmul(mixed, lw["w_out"], name=tag + "out_proj")
    x1 = _norm_fwd(mix, lw["g_mix_post"], name=tag + "mix_post_norm", resid=x, out_dtype=F32)
    h2 = _norm_fwd(x1, lw["g_ffn_pre"], name=tag + "ffn_pre_norm")
    u_pre, u = _matmul(h2, lw["w_ffn_up"], name=tag + "ffn_up", relu2=True)
    f = _matmul(u, lw["w_ffn_down"], name=tag + "ffn_down")
    x2 = _norm_fwd(f, lw["g_ffn_post"], name=tag + "ffn_post_norm", resid=x1, out_dtype=F32)
    saved = dict(x=x, h1=h1, proj=proj, projb=projb, bias_row=bias_row, cum_col=cum_col, cum_row=cum_row, oa=oa, lse_a=lse_a,
                 qm=qm, km=km, vm=vm, cqn=cqn, ckvn=ckvn, ob=ob, ob_wide=ob_wide, lse_b=lse_b, qr=qr, kr=kr, oc=oc, od=od, tot_d=tot_d, mixed=mixed,
                 mix=mix, x1=x1, h2=h2, u_pre=u_pre, u=u, f=f)
    return x2, saved


def _layer_bwd(dx2, lw, sv, tabs, tag):
    cos_m, sin_m, cos_r, sin_r = tabs
    g = {}
    df, g["g_ffn_post"] = _norm_bwd(sv["f"], lw["g_ffn_post"], dx2, name=tag + "ffn_post_norm_bwd", out_dtype=BF16)
    du_pre = _matmul(df, lw["w_ffn_down"], name=tag + "ffn_down_dx", tb=True, out_dtype=BF16, relu_of=sv["u_pre"])
    g["w_ffn_down"] = _matmul(sv["u"], df, name=tag + "ffn_down_dw", ta=True)
    dh2 = _matmul(du_pre, lw["w_ffn_up"], name=tag + "ffn_up_dx", tb=True)
    g["w_ffn_up"] = _matmul(sv["h2"], du_pre, name=tag + "ffn_up_dw", ta=True)
    dx1, g["g_ffn_pre"] = _norm_bwd(sv["x1"], lw["g_ffn_pre"], dh2, name=tag + "ffn_pre_norm_bwd", add=dx2)
    dmix, g["g_mix_post"] = _norm_bwd(sv["mix"], lw["g_mix_post"], dx1, name=tag + "mix_post_norm_bwd", out_dtype=BF16)
    dmixed = _matmul(dmix, lw["w_out"], name=tag + "out_proj_dx", tb=True)
    g["w_out"] = _matmul(sv["mixed"], dmix, name=tag + "out_proj_dw", ta=True)
    proj, projb = sv["proj"], sv["projb"]
    doa, dob, doc, dod, drg, g["g_mix_out"] = _mix_post_bwd(dmixed, sv["oa"], sv["ob"], sv["oc"], sv["od"], proj, lw["g_mix_out"])
    dfq, dfk, dfv, dck, drs = _mixer_bwd("fox", projb, OFF_FQ, projb, OFF_FK, projb, OFF_FV, sv["oa"], doa, stat=sv["lse_a"],
                                         cum_col=sv["cum_col"], cum_row=sv["cum_row"])
    dqm, dkm, dvm = _mixer_bwd("mla", sv["qm"], 0, sv["km"], 0, sv["vm"], 0, sv["ob_wide"], _widen_heads(dob), stat=sv["lse_b"])
    dcq, dckv, dkr, dwq, dwk, dwv, g["g_q_lora"], g["g_kv_lora"] = _mla_prep_bwd(
        dqm, dkm, dvm, proj, sv["cqn"], sv["ckvn"], cos_m, sin_m, lw["g_q_lora"], lw["g_kv_lora"], lw["wq"], lw["wk"], lw["wv"])
    dqr, dkr_ret, drv = _mixer_bwd("ret", sv["qr"], 0, sv["kr"], 0, projb, OFF_RV, sv["oc"], doc)
    drq, drk = _ret_prep_bwd(dqr, dkr_ret, cos_r, sin_r)
    dsq, dsk, dsv = _mixer_bwd("sb", projb, OFF_SQ, projb, OFF_SK, projb, OFF_SV, sv["od"], dod, stat=sv["tot_d"])
    dmisc, db_row = _fox_gate_bwd(dck, drs, proj, sv["bias_row"], dkr)
    b = lambda a: a.astype(BF16)
    dproj = jnp.concatenate([b(dfq), b(dfk), b(dfv), dcq, drq, drk, b(drv), drg, b(dsq), b(dsk), b(dsv), dckv, dmisc], axis=1)
    dh1 = _matmul(dproj, lw["w_in"], name=tag + "in_proj_dx", tb=True)
    g["w_in"] = _matmul(sv["h1"], dproj, name=tag + "in_proj_dw", ta=True)
    dx, g["g_mix_pre"] = _norm_bwd(sv["x"], lw["g_mix_pre"], dh1, name=tag + "pre_norm_bwd", add=dx1)
    g["b_forget"] = db_row[0, FF_LANE:FF_LANE + N_HEADS]
    g["wq"], g["wk"], g["wv"] = dwq, dwk, dwv
    return dx, g


def _local_step(x, positions, layers, target):
    s = x.shape[0]
    tabs = _rope_tables(positions.reshape(s, 1))
    saved = []
    for li, lw in enumerate(layers):
        x, sv = _layer_fwd(x, lw, tabs, "l%d_" % li)
        saved.append(sv)
    loss_row, dx = _loss_head(x, target)
    grads = [None] * len(layers)
    for li in reversed(range(len(layers))):
        dx, grads[li] = _layer_bwd(dx, layers[li], saved[li], tabs, "l%d_" % li)
    return loss_row[0, 0], dx, grads


def _adamw(w, g, m, v, *, name):
    r, c = w.shape
    tr = 256 if r % 256 == 0 else r
    blk = pl.BlockSpec((tr, c), lambda i: (i, 0))
    c1 = 1.0 - ADAM_B1 ** ADAM_STEP
    c2 = 1.0 - ADAM_B2 ** ADAM_STEP

    def body(w_ref, g_ref, m_ref, v_ref, d_ref, mo_ref, vo_ref):
        gv = g_ref[...]
        mn = ADAM_B1 * m_ref[...] + (1.0 - ADAM_B1) * gv
        vn = ADAM_B2 * v_ref[...] + (1.0 - ADAM_B2) * jnp.square(gv)
        mo_ref[...] = mn
        vo_ref[...] = vn
        d_ref[...] = -ADAM_LR * ((mn / c1) / (jnp.sqrt(vn / c2) + ADAM_EPS) + ADAM_WD * w_ref[...])

    return pl.pallas_call(
        body, name=name, grid=(r // tr,), in_specs=[blk] * 4, out_specs=[blk] * 3,
        out_shape=[jax.ShapeDtypeStruct((r, c), F32)] * 3, compiler_params=_cparams(("parallel",)),
    )(w, g, m, v)


BIG = ("w_in", "w_q_up", "w_kv_up", "w_out", "w_ffn_up", "w_ffn_down")
SMALL = ("g_mix_pre", "b_forget", "g_q_lora", "g_kv_lora", "g_mix_out", "g_mix_post", "g_ffn_pre", "g_ffn_post")
FLAT_COLS = 1024
HALF_TILE = 640
N_CHIPS = 4
ANY = pl.BlockSpec(memory_space=pl.ANY)


def _mesh_pos():
    return lax.axis_index("x"), lax.axis_index("y"), lax.axis_index("c")


def _other_chips(x, y):
    return [(1 - x, y), (x, 1 - y), (1 - x, 1 - y)]


def _gather_shards(flat):
    rows, cols = flat.shape

    def body(x_ref, o_ref, send_sems, recv_sems, own_sem):
        x, y, c = _mesh_pos()
        mine = 2 * x + y
        own = pltpu.make_async_copy(x_ref, o_ref.at[mine], own_sem)
        own.start()
        peers = _other_chips(x, y)

        def copy(j, block):
            px, py = peers[j]
            return pltpu.make_async_remote_copy(src_ref=x_ref, dst_ref=o_ref.at[block], send_sem=send_sems.at[j],
                                                recv_sem=recv_sems.at[j], device_id=(px, py, c), device_id_type=MESH)

        sends = [copy(j, mine) for j in range(3)]
        for cp in sends:
            cp.start()
        for j, (px, py) in enumerate(peers):
            copy(j, 2 * px + py).wait_recv()
        for cp in sends:
            cp.wait_send()
        own.wait()

    return pl.pallas_call(
        body, name="gather_weights", in_specs=[ANY], out_specs=ANY,
        out_shape=jax.ShapeDtypeStruct((N_CHIPS, rows, cols), flat.dtype),
        scratch_shapes=[pltpu.SemaphoreType.DMA((3,)), pltpu.SemaphoreType.DMA((3,)), pltpu.SemaphoreType.DMA],
        compiler_params=pltpu.CompilerParams(has_side_effects=True),
    )(flat)


def _send_other_half(g):
    _, _, h, cols = g.shape

    def body(g_ref, o_ref, send_sems, recv_sems):
        x, y, c = _mesh_pos()

        def copy(k):
            return pltpu.make_async_remote_copy(src_ref=g_ref.at[k, 1 - c], dst_ref=o_ref.at[k], send_sem=send_sems.at[k],
                                                recv_sem=recv_sems.at[k], device_id=(x, y, 1 - c), device_id_type=MESH)

        cps = [copy(k) for k in range(N_CHIPS)]
        for cp in cps:
            cp.start()
        for cp in cps:
            cp.wait_recv()
        for cp in cps:
            cp.wait_send()

    return pl.pallas_call(
        body, name="grad_pair_exchange", in_specs=[ANY], out_specs=ANY, out_shape=jax.ShapeDtypeStruct((N_CHIPS, h, cols), g.dtype),
        scratch_shapes=[pltpu.SemaphoreType.DMA((N_CHIPS,)), pltpu.SemaphoreType.DMA((N_CHIPS,))],
        compiler_params=pltpu.CompilerParams(has_side_effects=True),
    )(g)


def _add_own_half(g, r, c_idx):
    _, _, h, cols = g.shape
    tr = HALF_TILE

    def body(c_ref, g_ref, r_ref, o_ref):
        o_ref[0] = g_ref[0, 0] + r_ref[0]

    return pl.pallas_call(
        body, name="grad_pair_add",
        grid_spec=pltpu.PrefetchScalarGridSpec(
            num_scalar_prefetch=1, grid=(N_CHIPS, h // tr),
            in_specs=[pl.BlockSpec((1, 1, tr, cols), lambda k, i, c_ref: (k, c_ref[0], i, 0)),
                      pl.BlockSpec((1, tr, cols), lambda k, i, c_ref: (k, i, 0))],
            out_specs=pl.BlockSpec((1, tr, cols), lambda k, i, c_ref: (k, i, 0))),
        out_shape=jax.ShapeDtypeStruct((N_CHIPS, h, cols), F32), compiler_params=_cparams(("parallel", "parallel")),
    )(c_idx, g, r)


def _send_to_owners(p):
    _, h, cols = p.shape

    def body(p_ref, o_ref, send_sems, recv_sems):
        x, y, c = _mesh_pos()
        peers = _other_chips(x, y)

        def copy(j):
            px, py = peers[j]
            return pltpu.make_async_remote_copy(src_ref=p_ref.at[2 * px + py], dst_ref=o_ref.at[j], send_sem=send_sems.at[j],
                                                recv_sem=recv_sems.at[j], device_id=(px, py, c), device_id_type=MESH)

        cps = [copy(j) for j in range(3)]
        for cp in cps:
            cp.start()
        for cp in cps:
            cp.wait_recv()
        for cp in cps:
            cp.wait_send()

    return pl.pallas_call(
        body, name="grad_chip_exchange", in_specs=[ANY], out_specs=ANY, out_shape=jax.ShapeDtypeStruct((3, h, cols), p.dtype),
        scratch_shapes=[pltpu.SemaphoreType.DMA((3,)), pltpu.SemaphoreType.DMA((3,))],
        compiler_params=pltpu.CompilerParams(has_side_effects=True),
    )(p)


def _add_chip_partials(p, r, k_idx):
    _, h, cols = p.shape
    tr = HALF_TILE

    def body(k_ref, p_ref, r_ref, o_ref):
        o_ref[...] = ((p_ref[0] + r_ref[0]) + r_ref[1]) + r_ref[2]

    return pl.pallas_call(
        body, name="grad_chip_add",
        grid_spec=pltpu.PrefetchScalarGridSpec(
            num_scalar_prefetch=1, grid=(h // tr,),
            in_specs=[pl.BlockSpec((1, tr, cols), lambda i, k_ref: (k_ref[0], i, 0)),
                      pl.BlockSpec((3, tr, cols), lambda i, k_ref: (0, i, 0))],
            out_specs=pl.BlockSpec((tr, cols), lambda i, k_ref: (i, 0))),
        out_shape=jax.ShapeDtypeStruct((h, cols), F32), compiler_params=_cparams(("parallel",)),
    )(k_idx, p, r)


def _share_half(q):
    h, cols = q.shape

    def body(q_ref, o_ref, send_sem, recv_sem, own_sem):
        x, y, c = _mesh_pos()
        own = pltpu.make_async_copy(q_ref, o_ref.at[c], own_sem)
        own.start()
        send = pltpu.make_async_remote_copy(src_ref=q_ref, dst_ref=o_ref.at[c], send_sem=send_sem, recv_sem=recv_sem,
                                            device_id=(x, y, 1 - c), device_id_type=MESH)
        send.start()
        pltpu.make_async_remote_copy(src_ref=q_ref, dst_ref=o_ref.at[1 - c], send_sem=send_sem, recv_sem=recv_sem,
                                     device_id=(x, y, 1 - c), device_id_type=MESH).wait_recv()
        send.wait_send()
        own.wait()

    return pl.pallas_call(
        body, name="grad_pair_share", in_specs=[ANY], out_specs=ANY, out_shape=jax.ShapeDtypeStruct((2, h, cols), q.dtype),
        scratch_shapes=[pltpu.SemaphoreType.DMA, pltpu.SemaphoreType.DMA, pltpu.SemaphoreType.DMA],
        compiler_params=pltpu.CompilerParams(has_side_effects=True),
    )(q)


def _all_reduce_small(v):
    r, cols = v.shape
    n_dev = 8

    def body(v_ref, o_ref, buf, send_sems, recv_sems):
        x, y, c = _mesh_pos()
        me = 4 * x + 2 * y + c
        buf[me] = v_ref[...]

        def peer(j):
            return (1 - x if j & 4 else x, 1 - y if j & 2 else y, 1 - c if j & 1 else c)

        def copy(j, slot):
            return pltpu.make_async_remote_copy(src_ref=v_ref, dst_ref=buf.at[slot], send_sem=send_sems.at[j - 1],
                                                recv_sem=recv_sems.at[j - 1], device_id=peer(j), device_id_type=MESH)

        sends = [copy(j, me) for j in range(1, n_dev)]
        for cp in sends:
            cp.start()
        for j in range(1, n_dev):
            px, py, pc = peer(j)
            copy(j, 4 * px + 2 * py + pc).wait_recv()
        for cp in sends:
            cp.wait_send()
        acc = buf[0]
        for d in range(1, n_dev):
            acc = acc + buf[d]
        o_ref[...] = acc

    vm = pl.BlockSpec(memory_space=pltpu.VMEM)
    return pl.pallas_call(
        body, name="small_all_reduce", in_specs=[vm], out_specs=vm, out_shape=jax.ShapeDtypeStruct((r, cols), F32),
        scratch_shapes=[pltpu.VMEM((n_dev, r, cols), F32), pltpu.SemaphoreType.DMA((n_dev - 1,)), pltpu.SemaphoreType.DMA((n_dev - 1,))],
        compiler_params=pltpu.CompilerParams(has_side_effects=True),
    )(v)


_COL_SHARDED = ("w_in", "w_q_up", "w_kv_up", "w_ffn_up")


def _flat_rows(shapes):
    n = sum(int(np.prod(shapes[k])) for k in BIG)
    rows = -(-n // FLAT_COLS)
    return n, -(-rows // (2 * HALF_TILE)) * (2 * HALF_TILE)


def _to_flat(shards, dtype):
    n, rows = _flat_rows({k: shards[k].shape for k in BIG})
    v = jnp.concatenate([shards[k].astype(dtype).reshape(-1) for k in BIG])
    return jnp.pad(v, (0, rows * FLAT_COLS - n)).reshape(rows, FLAT_COLS)


def _from_flat(flat, shapes):
    v = flat.reshape(-1)
    out, o = {}, 0
    for k in BIG:
        sz = int(np.prod(shapes[k]))
        out[k] = v[o:o + sz].reshape(shapes[k])
        o += sz
    return out


def _whole_from_gathered(gathered, shapes):
    n = sum(int(np.prod(shapes[k])) for k in BIG)
    v = gathered.reshape(N_CHIPS, -1)
    out, o = {}, 0
    for k in BIG:
        d, r, c = shapes[k]
        sz = d * r * c
        blk = v[:, o:o + sz].reshape(N_CHIPS, d, r, c)
        o += sz
        if k in _COL_SHARDED:
            out[k] = jnp.transpose(blk, (1, 2, 0, 3)).reshape(d, r, N_CHIPS * c)
        else:
            out[k] = jnp.transpose(blk, (1, 0, 2, 3)).reshape(d, N_CHIPS * r, c)
    return out


def _whole_to_blocks(whole, shapes):
    n, rows = _flat_rows(shapes)
    parts = []
    for k in BIG:
        d, r, c = shapes[k]
        w = whole[k]
        if k in _COL_SHARDED:
            blk = jnp.transpose(w.reshape(d, r, N_CHIPS, c), (2, 0, 1, 3))
        else:
            blk = jnp.transpose(w.reshape(d, N_CHIPS, r, c), (1, 0, 2, 3))
        parts.append(blk.reshape(N_CHIPS, -1))
    v = jnp.concatenate(parts, axis=1)
    return jnp.pad(v, ((0, 0), (0, rows * FLAT_COLS - n))).reshape(N_CHIPS, rows, FLAT_COLS)


def _small_to_rows(d):
    v = jnp.concatenate([d[k].astype(F32).reshape(-1) for k in SMALL])
    rows = -(-v.shape[0] // (8 * LANES)) * 8
    return jnp.pad(v, (0, rows * LANES - v.shape[0])).reshape(rows, LANES)


def _small_from_rows(rows, shapes):
    v = rows.reshape(-1)
    out, o = {}, 0
    for k in SMALL:
        sz = int(np.prod(shapes[k]))
        out[k] = v[o:o + sz].reshape(shapes[k])
        o += sz
    return out


_ARG_NAMES = ("x", "positions", "g_mix_pre", "w_in", "b_forget", "g_q_lora", "w_q_up", "g_kv_lora", "w_kv_up", "g_mix_out", "w_out",
              "g_mix_post", "g_ffn_pre", "w_ffn_up", "w_ffn_down", "g_ffn_post")
_WEIGHTS = _ARG_NAMES[2:]


def kernel(x, positions, g_mix_pre, w_in, b_forget, g_q_lora, w_q_up, g_kv_lora, w_kv_up, g_mix_out, w_out, g_mix_post, g_ffn_pre, w_ffn_up, w_ffn_down, g_ffn_post, loss_target, m_g_mix_pre, m_w_in, m_b_forget, m_g_q_lora, m_w_q_up, m_g_kv_lora, m_w_kv_up, m_g_mix_out, m_w_out, m_g_mix_post, m_g_ffn_pre, m_w_ffn_up, m_w_ffn_down, m_g_ffn_post, v_g_mix_pre, v_w_in, v_b_forget, v_g_q_lora, v_w_q_up, v_g_kv_lora, v_w_kv_up, v_g_mix_out, v_w_out, v_g_mix_post, v_g_ffn_pre, v_w_ffn_up, v_w_ffn_down, v_g_ffn_post):
    w = dict(g_mix_pre=g_mix_pre, w_in=w_in, b_forget=b_forget, g_q_lora=g_q_lora, w_q_up=w_q_up, g_kv_lora=g_kv_lora, w_kv_up=w_kv_up,
             g_mix_out=g_mix_out, w_out=w_out, g_mix_post=g_mix_post, g_ffn_pre=g_ffn_pre, w_ffn_up=w_ffn_up, w_ffn_down=w_ffn_down,
             g_ffn_post=g_ffn_post)
    m = dict(g_mix_pre=m_g_mix_pre, w_in=m_w_in, b_forget=m_b_forget, g_q_lora=m_g_q_lora, w_q_up=m_w_q_up, g_kv_lora=m_g_kv_lora,
             w_kv_up=m_w_kv_up, g_mix_out=m_g_mix_out, w_out=m_w_out, g_mix_post=m_g_mix_post, g_ffn_pre=m_g_ffn_pre,
             w_ffn_up=m_w_ffn_up, w_ffn_down=m_w_ffn_down, g_ffn_post=m_g_ffn_post)
    v = dict(g_mix_pre=v_g_mix_pre, w_in=v_w_in, b_forget=v_b_forget, g_q_lora=v_g_q_lora, w_q_up=v_w_q_up, g_kv_lora=v_g_kv_lora,
             w_kv_up=v_w_kv_up, g_mix_out=v_g_mix_out, w_out=v_w_out, g_mix_post=v_g_mix_post, g_ffn_pre=v_g_ffn_pre,
             w_ffn_up=v_w_ffn_up, w_ffn_down=v_w_ffn_down, g_ffn_post=v_g_ffn_post)
    shard_shapes = {k: w[k].shape for k in BIG}
    small_shapes = {k: w[k].shape for k in SMALL}
    c_idx = lax.axis_index("c").astype(jnp.int32).reshape(1)
    k_idx = (2 * lax.axis_index("x") + lax.axis_index("y")).astype(jnp.int32).reshape(1)

    gathered = _gather_shards(_to_flat(w, BF16))
    whole = _whole_from_gathered(gathered, shard_shapes)
    layers = []
    for l in range(DEPTH):
        wk, wv = _pack_w_kv(whole["w_kv_up"][l])
        layers.append(dict(
            g_mix_pre=g_mix_pre[l], w_in=_pack_w_in(whole["w_in"][l]), b_forget=b_forget[l], g_q_lora=g_q_lora[l], g_kv_lora=g_kv_lora[l],
            wq=_pack_w_q(whole["w_q_up"][l]), wk=wk, wv=wv, g_mix_out=g_mix_out[l], w_out=whole["w_out"][l], g_mix_post=g_mix_post[l],
            g_ffn_pre=g_ffn_pre[l], w_ffn_up=whole["w_ffn_up"][l], w_ffn_down=whole["w_ffn_down"][l], g_ffn_post=g_ffn_post[l]))

    loss_local, dx, grads = _local_step(x[0], positions[0], layers, loss_target[0])
    loss = lax.psum(loss_local, ("x", "y", "c"))

    gw = dict(
        w_in=jnp.stack([_unpack_dw_in(grads[l]["w_in"]) for l in range(DEPTH)]),
        w_q_up=jnp.stack([_unpack_dw_q(grads[l]["wq"]) for l in range(DEPTH)]),
        w_kv_up=jnp.stack([_unpack_dw_kv(grads[l]["wk"], grads[l]["wv"]) for l in range(DEPTH)]),
        w_out=jnp.stack([grads[l]["w_out"] for l in range(DEPTH)]),
        w_ffn_up=jnp.stack([grads[l]["w_ffn_up"] for l in range(DEPTH)]),
        w_ffn_down=jnp.stack([grads[l]["w_ffn_down"] for l in range(DEPTH)]))
    blocks = _whole_to_blocks(gw, shard_shapes)
    rows = blocks.shape[1]
    blocks = blocks.reshape(N_CHIPS, 2, rows // 2, FLAT_COLS)
    pair = _add_own_half(blocks, _send_other_half(blocks), c_idx)
    half = _add_chip_partials(pair, _send_to_owners(pair), k_idx)
    g_big = _from_flat(_share_half(half).reshape(rows, FLAT_COLS), shard_shapes)

    g_small_local = {k: jnp.stack([grads[l][k].reshape(small_shapes[k][1:]) for l in range(DEPTH)]) for k in SMALL}
    g_small = _small_from_rows(_all_reduce_small(_small_to_rows(g_small_local)), small_shapes)

    g_all = {**g_big, **g_small}
    delta, new_m, new_v = {}, {}, {}
    for k in BIG:
        d, r, c = shard_shapes[k]
        two_d = lambda a: a.reshape(d * r, c)
        dk, mk, vk = _adamw(two_d(w[k]), two_d(g_all[k]), two_d(m[k]), two_d(v[k]), name="adamw_" + k)
        delta[k], new_m[k], new_v[k] = dk.reshape(d, r, c), mk.reshape(d, r, c), vk.reshape(d, r, c)
    ds, ms, vs = _adamw(_small_to_rows(w), _small_to_rows(g_small), _small_to_rows(m), _small_to_rows(v), name="adamw_small")
    delta.update(_small_from_rows(ds, small_shapes))
    new_m.update(_small_from_rows(ms, small_shapes))
    new_v.update(_small_from_rows(vs, small_shapes))

    grad_x = dx.reshape(x.shape)
    return (loss, grad_x, *[g_all[k] for k in _WEIGHTS], *[delta[k] for k in _WEIGHTS], *[new_m[k] for k in _WEIGHTS],
            *[new_v[k] for k in _WEIGHTS])
```

```python
import functools
import math

import numpy as np
import jax
import jax.numpy as jnp
from jax import lax
from jax.experimental import pallas as pl
from jax.experimental.pallas import tpu as pltpu

F32 = jnp.float32
BF16 = jnp.bfloat16
MESH = pl.DeviceIdType.MESH

D_MODEL = 1024
DEPTH = 2
CHUNK = 64
GROUP = 256
HEAD = 64
N_HEADS = 4
Q_RANK = 256
KV_RANK = 128
ROPE_DIM = 32
D_FF = 4096
D_IN = 2980
D_INP = 3072
ROPE_BASE = 10000.0
EPS = 1e-6
LANES = 128
TQ = 128
NEG = -1e30

ADAM_LR, ADAM_B1, ADAM_B2, ADAM_EPS, ADAM_WD, ADAM_STEP = 0.001, 0.9, 0.999, 1e-08, 0.01, 10

OFF_FQ, OFF_FK, OFF_FV, OFF_CQ = 0, 2, 4, 6
OFF_RQ, OFF_RK, OFF_RV, OFF_RG = 8, 10, 12, 14
OFF_SQ, OFF_SK, OFF_SV = 16, 18, 20
OFF_CKV, OFF_MISC = 22, 23
FF_LANE, KR_LANE = 0, 64

VMEM_LIMIT = 56 * 1024 * 1024


def _tile(dim, pref):
    return pref if dim % pref == 0 else dim


def _cparams(sem, vmem=None):
    return pltpu.CompilerParams(dimension_semantics=sem, vmem_limit_bytes=vmem or VMEM_LIMIT)


def _dot(a, b):
    return jnp.dot(a, b, preferred_element_type=F32)


def _dot_nt(a, b):
    return lax.dot_general(a, b, (((1,), (1,)), ((), ())), preferred_element_type=F32)


def _dot_tn(a, b):
    return lax.dot_general(a, b, (((0,), (0,)), ((), ())), preferred_element_type=F32)


def _dot_exact(a, b):
    return jnp.dot(a, b, precision=lax.Precision.HIGHEST, preferred_element_type=F32)


def _matmul(a, b, *, name, ta=False, tb=False, out_dtype=F32, tm=512, tn=512, tk=1024,
            relu2=False, relu_of=None, also_bf16=False):
    if ta:
        kdim, m = a.shape
    else:
        m, kdim = a.shape
    n = b.shape[0] if tb else b.shape[1]
    tm, tn, tk = _tile(m, tm), _tile(n, tn), _tile(kdim, tk)
    nk = kdim // tk
    a_spec = pl.BlockSpec((tk, tm), lambda i, j, k: (k, i)) if ta else pl.BlockSpec((tm, tk), lambda i, j, k: (i, k))
    b_spec = pl.BlockSpec((tn, tk), lambda i, j, k: (j, k)) if tb else pl.BlockSpec((tk, tn), lambda i, j, k: (k, j))
    o_spec = pl.BlockSpec((tm, tn), lambda i, j, k: (i, j))
    two = relu2 or also_bf16

    def body(*refs):
        refs = list(refs)
        a_ref, b_ref = refs[0], refs[1]
        e_ref = refs[2] if relu_of is not None else None
        pos = 3 if relu_of is not None else 2
        o_ref = refs[pos]
        o2_ref = refs[pos + 1] if two else None
        acc_ref = refs[-1]
        k = pl.program_id(2)
        av = a_ref[...].astype(BF16)
        bv = b_ref[...].astype(BF16)
        if ta:
            part = _dot_tn(av, bv)
        elif tb:
            part = _dot_nt(av, bv)
        else:
            part = _dot(av, bv)

        @pl.when(k == 0)
        def _():
            acc_ref[...] = part

        @pl.when(k > 0)
        def _():
            acc_ref[...] += part

        @pl.when(k == nk - 1)
        def _():
            r = acc_ref[...]
            if relu_of is not None:
                r = r * (2.0 * jnp.maximum(e_ref[...], 0.0))
            o_ref[...] = r.astype(o_ref.dtype)
            if relu2:
                o2_ref[...] = jnp.square(jnp.maximum(r, 0.0)).astype(BF16)
            elif also_bf16:
                o2_ref[...] = r.astype(BF16)

    in_specs = [a_spec, b_spec]
    args = [a, b]
    if relu_of is not None:
        in_specs.append(o_spec)
        args.append(relu_of)
    out_shape = [jax.ShapeDtypeStruct((m, n), out_dtype)]
    out_specs = [o_spec]
    if two:
        out_shape.append(jax.ShapeDtypeStruct((m, n), BF16))
        out_specs.append(o_spec)
    res = pl.pallas_call(
        body, name=name, grid=(m // tm, n // tn, nk), in_specs=in_specs, out_specs=out_specs, out_shape=out_shape,
        scratch_shapes=[pltpu.VMEM((tm, tn), F32)],
        compiler_params=_cparams(("parallel", "parallel", "arbitrary")),
    )(*args)
    return res if two else res[0]


def _rms(x, g):
    r = lax.rsqrt(jnp.mean(x * x, axis=-1, keepdims=True) + EPS)
    return x * r * g


def _rms_bwd(x, g, dy):
    r = lax.rsqrt(jnp.mean(x * x, axis=-1, keepdims=True) + EPS)
    xh = x * r
    gdy = dy * g
    dx = r * (gdy - xh * jnp.mean(xh * gdy, axis=-1, keepdims=True))
    return dx, xh * dy


def _norm_fwd(x, g, *, name, resid=None, out_dtype=BF16):
    s, d = x.shape
    tr = _tile(s, 256)
    row = pl.BlockSpec((tr, d), lambda i: (i, 0))
    gsp = pl.BlockSpec((1, d), lambda i: (0, 0))

    def body(*refs):
        if resid is None:
            x_ref, g_ref, o_ref = refs
            o_ref[...] = _rms(x_ref[...], g_ref[...]).astype(o_ref.dtype)
        else:
            x_ref, g_ref, r_ref, o_ref = refs
            o_ref[...] = (r_ref[...] + _rms(x_ref[...], g_ref[...])).astype(o_ref.dtype)

    args = [x, g.reshape(1, d)] + ([] if resid is None else [resid])
    return pl.pallas_call(
        body, name=name, grid=(s // tr,), in_specs=[row, gsp] + ([] if resid is None else [row]),
        out_specs=row, out_shape=jax.ShapeDtypeStruct((s, d), out_dtype), compiler_params=_cparams(("parallel",)),
    )(*args)


def _norm_bwd(x, g, dy, *, name, add=None, out_dtype=F32):
    s, d = x.shape
    tr = _tile(s, 256)
    row = pl.BlockSpec((tr, d), lambda i: (i, 0))
    gsp = pl.BlockSpec((1, d), lambda i: (0, 0))

    def body(*refs):
        if add is None:
            x_ref, g_ref, dy_ref, dx_ref, dg_ref = refs
        else:
            x_ref, g_ref, dy_ref, add_ref, dx_ref, dg_ref = refs
        dx, gterm = _rms_bwd(x_ref[...], g_ref[...], dy_ref[...].astype(F32))
        if add is not None:
            dx = dx + add_ref[...]
        dx_ref[...] = dx.astype(dx_ref.dtype)

        @pl.when(pl.program_id(0) == 0)
        def _():
            dg_ref[...] = jnp.zeros_like(dg_ref)

        dg_ref[...] += jnp.sum(gterm, axis=0, keepdims=True)

    args = [x, g.reshape(1, d), dy] + ([] if add is None else [add])
    return pl.pallas_call(
        body, name=name, grid=(s // tr,), in_specs=[row, gsp, row] + ([] if add is None else [row]),
        out_specs=[row, gsp], out_shape=[jax.ShapeDtypeStruct((s, d), out_dtype), jax.ShapeDtypeStruct((1, d), F32)],
        compiler_params=_cparams(("arbitrary",)),
    )(*args)


def _loss_head(y, target):
    s, d = y.shape
    tr = _tile(s, 256)
    row = pl.BlockSpec((tr, d), lambda i: (i, 0))
    lsp = pl.BlockSpec((1, LANES), lambda i: (0, 0))

    def body(y_ref, t_ref, l_ref, dy_ref):
        e = y_ref[...] - t_ref[...]
        dy_ref[...] = e * (1.0 / d)

        @pl.when(pl.program_id(0) == 0)
        def _():
            l_ref[...] = jnp.zeros_like(l_ref)

        part = 0.5 * jnp.sum(jnp.mean(e * e, axis=-1, keepdims=True), axis=0, keepdims=True)
        l_ref[...] += jnp.broadcast_to(part, (1, LANES))

    return pl.pallas_call(
        body, name="loss_head", grid=(s // tr,), in_specs=[row, row], out_specs=[lsp, row],
        out_shape=[jax.ShapeDtypeStruct((1, LANES), F32), jax.ShapeDtypeStruct((s, d), F32)],
        compiler_params=_cparams(("arbitrary",)),
    )(y, target)


def _rope_tables(pos_col):
    s = pos_col.shape[0]
    tr = _tile(s, 512)
    f_mla = ROPE_BASE ** (-jnp.arange(ROPE_DIM // 2, dtype=F32) / (ROPE_DIM // 2))
    f_ret = ROPE_BASE ** (-jnp.arange(HEAD // 2, dtype=F32) / (HEAD // 2))
    fm = jnp.concatenate([jnp.zeros((64,), F32), f_mla, f_mla, jnp.zeros((32,), F32)]).reshape(1, LANES)
    fr = jnp.tile(jnp.concatenate([f_ret, f_ret]), 4).reshape(1, 2 * LANES)

    def body(p_ref, fm_ref, fr_ref, cm_ref, sm_ref, cr_ref, sr_ref):
        p = p_ref[...].astype(F32)
        am = p * fm_ref[...]
        ar = p * fr_ref[...]
        cm_ref[...] = jnp.cos(am)
        sm_ref[...] = jnp.sin(am)
        cr_ref[...] = jnp.cos(ar)
        sr_ref[...] = jnp.sin(ar)

    return pl.pallas_call(
        body, name="rope_tables", grid=(s // tr,),
        in_specs=[pl.BlockSpec((tr, 1), lambda i: (i, 0)), pl.BlockSpec((1, LANES), lambda i: (0, 0)),
                  pl.BlockSpec((1, 2 * LANES), lambda i: (0, 0))],
        out_specs=[pl.BlockSpec((tr, LANES), lambda i: (i, 0))] * 2 + [pl.BlockSpec((tr, 2 * LANES), lambda i: (i, 0))] * 2,
        out_shape=[jax.ShapeDtypeStruct((s, LANES), F32)] * 2 + [jax.ShapeDtypeStruct((s, 2 * LANES), F32)] * 2,
        compiler_params=_cparams(("parallel",)),
    )(pos_col, fm, fr)


def _lane(shape):
    return lax.broadcasted_iota(jnp.int32, shape, len(shape) - 1)


def _rot_mla(z):
    l = _lane(z.shape) % LANES
    n = z.shape[-1]
    return jnp.where(l < 80, -pltpu.roll(z, n - 16, 1), pltpu.roll(z, 16, 1))


def _rot_mla_t(y):
    l = _lane(y.shape) % LANES
    n = y.shape[-1]
    return jnp.where((l >= 64) & (l < 80), pltpu.roll(y, n - 16, 1),
                     jnp.where((l >= 80) & (l < 96), -pltpu.roll(y, 16, 1), 0.0))


def _rot_ret(z):
    l = _lane(z.shape) % HEAD
    n = z.shape[-1]
    return jnp.where(l < 32, -pltpu.roll(z, n - 32, 1), pltpu.roll(z, 32, 1))


def _rot_ret_t(y):
    l = _lane(y.shape) % HEAD
    n = y.shape[-1]
    return jnp.where(l < 32, pltpu.roll(y, n - 32, 1), -pltpu.roll(y, 32, 1))


def _log_sigmoid(x):
    return jnp.minimum(x, 0.0) - jnp.log1p(jnp.exp(-jnp.abs(x)))


def _fox_cum(proj, bias_row):
    s = proj.shape[0]
    nb = s // TQ

    def body(x_ref, b_ref, cc_ref, cr_ref, carry_ref):
        @pl.when(pl.program_id(0) == 0)
        def _():
            carry_ref[...] = jnp.zeros_like(carry_ref)

        ls = _log_sigmoid(x_ref[...] + b_ref[...])
        r = lax.broadcasted_iota(jnp.int32, (TQ, TQ), 0)
        c = lax.broadcasted_iota(jnp.int32, (TQ, TQ), 1)
        tri = (c <= r).astype(F32)
        cum = _dot_exact(tri, ls) + carry_ref[...]
        carry_ref[...] = cum[TQ - 1:TQ, :]
        cc_ref[...] = cum
        cr_ref[...] = cum.T[0:8, :]

    return pl.pallas_call(
        body, name="fox_cum", grid=(nb,),
        in_specs=[pl.BlockSpec((TQ, LANES), lambda i: (i, OFF_MISC)), pl.BlockSpec((1, LANES), lambda i: (0, 0))],
        out_specs=[pl.BlockSpec((TQ, LANES), lambda i: (i, 0)), pl.BlockSpec((8, TQ), lambda i: (0, i))],
        out_shape=[jax.ShapeDtypeStruct((s, LANES), F32), jax.ShapeDtypeStruct((8, s), F32)],
        scratch_shapes=[pltpu.VMEM((1, LANES), F32)],
        compiler_params=_cparams(("arbitrary",)),
    )(proj, bias_row)


def _fox_gate_bwd(dck, drs, proj, bias_row, dkr):
    s = proj.shape[0]
    nb = s // TQ

    def body(d_ref, r_ref, x_ref, b_ref, k_ref, o_ref, db_ref, carry_ref):
        @pl.when(pl.program_id(0) == 0)
        def _():
            carry_ref[...] = jnp.zeros_like(carry_ref)
            db_ref[...] = jnp.zeros_like(db_ref)

        rows = jnp.concatenate([d_ref[0], d_ref[1], jnp.zeros((TQ - 16, TQ), F32)], axis=0)
        t = rows.T
        l = _lane((TQ, LANES))
        r0, r1 = r_ref[0], r_ref[1]
        rsum = jnp.where(l == 0, r0[:, 0:1], jnp.where(l == 1, r0[:, HEAD:HEAD + 1],
                         jnp.where(l == 2, r1[:, 0:1], jnp.where(l == 3, r1[:, HEAD:HEAD + 1], 0.0))))
        dcum = rsum - jnp.where(l < 2, t, pltpu.roll(t, LANES - 6, 1))
        r = lax.broadcasted_iota(jnp.int32, (TQ, TQ), 0)
        c = lax.broadcasted_iota(jnp.int32, (TQ, TQ), 1)
        triu = (c >= r).astype(F32)
        rc = _dot_exact(triu, dcum) + carry_ref[...]
        carry_ref[...] = rc[0:1, :]
        f = x_ref[...] + b_ref[...]
        sig_neg = 1.0 / (1.0 + jnp.exp(f))
        df = jnp.where(l < N_HEADS, rc * sig_neg, 0.0)
        db_ref[...] += jnp.sum(df, axis=0, keepdims=True)
        o_ref[...] = (df + k_ref[...]).astype(o_ref.dtype)

    rev = lambda i: nb - 1 - i
    return pl.pallas_call(
        body, name="fox_gate_bwd", grid=(nb,),
        in_specs=[pl.BlockSpec((2, 8, TQ), lambda i: (0, 0, rev(i))), pl.BlockSpec((2, TQ, LANES), lambda i: (0, rev(i), 0)),
                  pl.BlockSpec((TQ, LANES), lambda i: (rev(i), OFF_MISC)),
                  pl.BlockSpec((1, LANES), lambda i: (0, 0)), pl.BlockSpec((TQ, LANES), lambda i: (rev(i), 0))],
        out_specs=[pl.BlockSpec((TQ, LANES), lambda i: (rev(i), 0)), pl.BlockSpec((1, LANES), lambda i: (0, 0))],
        out_shape=[jax.ShapeDtypeStruct((s, LANES), BF16), jax.ShapeDtypeStruct((1, LANES), F32)],
        scratch_shapes=[pltpu.VMEM((1, LANES), F32)],
        compiler_params=_cparams(("arbitrary",)),
    )(dck, drs, proj, bias_row, dkr)


def _mla_prep(proj, cos_m, sin_m, g_q, g_kv, wq, wk, wv):
    s = proj.shape[0]
    tr = _tile(s, 256)

    def body(cq_ref, ckv_ref, misc_ref, cos_ref, sin_ref, gq_ref, gkv_ref, wq_ref, wk_ref, wv_ref,
             q_ref, k_ref, v_ref, cqn_ref, ckvn_ref):
        cos4 = jnp.tile(cos_ref[...], (1, 4))
        sin4 = jnp.tile(sin_ref[...], (1, 4))
        cqn = _rms(cq_ref[...], gq_ref[...]).astype(BF16)
        ckvn = _rms(ckv_ref[...], gkv_ref[...]).astype(BF16)
        cqn_ref[...] = cqn
        ckvn_ref[...] = ckvn
        zq = _dot(cqn, wq_ref[...])
        q_ref[...] = (zq * cos4 + _rot_mla(zq) * sin4).astype(BF16)
        l = _lane((tr, LANES))
        kr = jnp.where((l >= KR_LANE) & (l < KR_LANE + ROPE_DIM), misc_ref[...], 0.0)
        zk = _dot(ckvn, wk_ref[...]) + jnp.tile(kr, (1, 4))
        k_ref[...] = (zk * cos4 + _rot_mla(zk) * sin4).astype(BF16)
        v_ref[...] = _dot(ckvn, wv_ref[...]).astype(BF16)

    full = lambda a: pl.BlockSpec(a.shape, lambda i: (0, 0))
    rowb = lambda w: pl.BlockSpec((tr, w), lambda i: (i, 0))
    gq2, gkv2 = g_q.reshape(1, Q_RANK), g_kv.reshape(1, KV_RANK)
    return pl.pallas_call(
        body, name="mla_prep", grid=(s // tr,),
        in_specs=[pl.BlockSpec((tr, 256), lambda i: (i, OFF_CQ // 2)), pl.BlockSpec((tr, LANES), lambda i: (i, OFF_CKV)),
                  pl.BlockSpec((tr, LANES), lambda i: (i, OFF_MISC)), rowb(LANES), rowb(LANES),
                  full(gq2), full(gkv2), full(wq), full(wk), full(wv)],
        out_specs=[rowb(512), rowb(512), rowb(512), rowb(256), rowb(128)],
        out_shape=[jax.ShapeDtypeStruct((s, 512), BF16), jax.ShapeDtypeStruct((s, 512), BF16), jax.ShapeDtypeStruct((s, 512), BF16),
                   jax.ShapeDtypeStruct((s, 256), BF16), jax.ShapeDtypeStruct((s, 128), BF16)],
        compiler_params=_cparams(("parallel",)),
    )(proj, proj, proj, cos_m, sin_m, gq2, gkv2, wq, wk, wv)


def _mla_prep_bwd(dq, dk, dv, proj, cqn, ckvn, cos_m, sin_m, g_q, g_kv, wq, wk, wv):
    s = proj.shape[0]
    tr = _tile(s, 256)

    def body(dq_ref, dk_ref, dv_ref, cq_ref, ckv_ref, cqn_ref, ckvn_ref, cos_ref, sin_ref, gq_ref, gkv_ref,
             wq_ref, wk_ref, wv_ref, dcq_ref, dckv_ref, dkr_ref, dwq_ref, dwk_ref, dwv_ref, dgq_ref, dgkv_ref):
        @pl.when(pl.program_id(0) == 0)
        def _():
            for r in (dwq_ref, dwk_ref, dwv_ref, dgq_ref, dgkv_ref):
                r[...] = jnp.zeros_like(r)

        cos4 = jnp.tile(cos_ref[...], (1, 4))
        sin4 = jnp.tile(sin_ref[...], (1, 4))
        dqv = dq_ref[...]
        dzq = dqv * cos4 + _rot_mla_t(dqv * sin4)
        dkv_ = dk_ref[...]
        dzk = dkv_ * cos4 + _rot_mla_t(dkv_ * sin4)
        l = _lane((tr, LANES))
        in_rope = (l >= KR_LANE) & (l < KR_LANE + ROPE_DIM)
        dkr = dzk[:, 0:128] + dzk[:, 128:256] + dzk[:, 256:384] + dzk[:, 384:512]
        dkr_ref[...] = jnp.where(in_rope, dkr, 0.0)
        dzq_b = dzq.astype(BF16)
        dzk_b = dzk.astype(BF16)
        dv_b = dv_ref[...].astype(BF16)
        dcqn = _dot_nt(dzq_b, wq_ref[...])
        dckvn = _dot_nt(dzk_b, wk_ref[...]) + _dot_nt(dv_b, wv_ref[...])
        dwq_ref[...] += _dot_tn(cqn_ref[...], dzq_b)
        dwk_ref[...] += _dot_tn(ckvn_ref[...], dzk_b)
        dwv_ref[...] += _dot_tn(ckvn_ref[...], dv_b)
        dcq, gq_term = _rms_bwd(cq_ref[...], gq_ref[...], dcqn)
        dckv, gkv_term = _rms_bwd(ckv_ref[...], gkv_ref[...], dckvn)
        dcq_ref[...] = dcq.astype(BF16)
        dckv_ref[...] = dckv.astype(BF16)
        dgq_ref[...] += jnp.sum(gq_term, axis=0, keepdims=True)
        dgkv_ref[...] += jnp.sum(gkv_term, axis=0, keepdims=True)

    full = lambda shp: pl.BlockSpec(shp, lambda i: (0, 0))
    rowb = lambda w: pl.BlockSpec((tr, w), lambda i: (i, 0))
    gq2, gkv2 = g_q.reshape(1, Q_RANK), g_kv.reshape(1, KV_RANK)
    return pl.pallas_call(
        body, name="mla_prep_bwd", grid=(s // tr,),
        in_specs=[rowb(512), rowb(512), rowb(512),
                  pl.BlockSpec((tr, 256), lambda i: (i, OFF_CQ // 2)), pl.BlockSpec((tr, LANES), lambda i: (i, OFF_CKV)),
                  rowb(256), rowb(128), rowb(LANES), rowb(LANES), full((1, Q_RANK)), full((1, KV_RANK)),
                  full(wq.shape), full(wk.shape), full(wv.shape)],
        out_specs=[rowb(256), rowb(128), rowb(128), full(wq.shape), full(wk.shape), full(wv.shape),
                   full((1, Q_RANK)), full((1, KV_RANK))],
        out_shape=[jax.ShapeDtypeStruct((s, 256), BF16), jax.ShapeDtypeStruct((s, 128), BF16), jax.ShapeDtypeStruct((s, 128), F32),
                   jax.ShapeDtypeStruct(wq.shape, F32), jax.ShapeDtypeStruct(wk.shape, F32), jax.ShapeDtypeStruct(wv.shape, F32),
                   jax.ShapeDtypeStruct((1, Q_RANK), F32), jax.ShapeDtypeStruct((1, KV_RANK), F32)],
        compiler_params=_cparams(("arbitrary",)),
    )(dq, dk, dv, proj, proj, cqn, ckvn, cos_m, sin_m, gq2, gkv2, wq, wk, wv)


def _ret_prep(proj, cos_r, sin_r):
    s = proj.shape[0]
    tr = _tile(s, 256)

    def body(q_ref, k_ref, cos_ref, sin_ref, qo_ref, ko_ref):
        cos, sin = cos_ref[...], sin_ref[...]
        q, k = q_ref[...], k_ref[...]
        qo_ref[...] = (q * cos + _rot_ret(q) * sin).astype(BF16)
        ko_ref[...] = ((k * cos + _rot_ret(k) * sin) * (HEAD ** -0.5)).astype(BF16)

    rowb = pl.BlockSpec((tr, 256), lambda i: (i, 0))
    return pl.pallas_call(
        body, name="ret_prep", grid=(s // tr,),
        in_specs=[pl.BlockSpec((tr, 256), lambda i: (i, OFF_RQ // 2)), pl.BlockSpec((tr, 256), lambda i: (i, OFF_RK // 2)), rowb, rowb],
        out_specs=[rowb, rowb], out_shape=[jax.ShapeDtypeStruct((s, 256), BF16)] * 2,
        compiler_params=_cparams(("parallel",)),
    )(proj, proj, cos_r, sin_r)


def _ret_prep_bwd(dq, dk, cos_r, sin_r):
    s = dq.shape[0]
    tr = _tile(s, 256)

    def body(dq_ref, dk_ref, cos_ref, sin_ref, qo_ref, ko_ref):
        cos, sin = cos_ref[...], sin_ref[...]
        q, k = dq_ref[...], dk_ref[...] * (HEAD ** -0.5)
        qo_ref[...] = (q * cos + _rot_ret_t(q * sin)).astype(BF16)
        ko_ref[...] = (k * cos + _rot_ret_t(k * sin)).astype(BF16)

    rowb = pl.BlockSpec((tr, 256), lambda i: (i, 0))
    return pl.pallas_call(
        body, name="ret_prep_bwd", grid=(s // tr,), in_specs=[rowb] * 4, out_specs=[rowb, rowb],
        out_shape=[jax.ShapeDtypeStruct((s, 256), BF16)] * 2, compiler_params=_cparams(("parallel",)),
    )(dq, dk, cos_r, sin_r)


_LOG_GAMMA = [float(np.log1p(-np.float32(2.0) ** np.float32(-5.0 - h))) for h in range(N_HEADS)]
_MLA_SCALE = float((HEAD + ROPE_DIM) ** -0.5)
_QK_SCALE = float(HEAD ** -0.5)
KEY_BLOCKS = 4


def _split2(x):
    h = x.astype(BF16)
    return h, (x - h.astype(F32)).astype(BF16)


def _dot2(x, u):
    h, lo = _split2(x)
    return _dot(h, u) + _dot(lo, u)


def _head_pick(block, head, axis):
    idx = lax.broadcasted_iota(jnp.int32, block.shape, axis)
    return jnp.sum(jnp.where(idx == head, block, 0.0), axis=axis, keepdims=True)


def _log_gamma_of(head):
    lg = jnp.float32(_LOG_GAMMA[3])
    for h in (2, 1, 0):
        lg = jnp.where(head == h, jnp.float32(_LOG_GAMMA[h]), lg)
    return lg


def _mixer_specs(mode, s, q_off, k_off, v_off):
    nhb = 1 if mode == "mla" else 2
    nsub = KEY_BLOCKS if (s // TQ) % KEY_BLOCKS == 0 else 1
    q_spec = pl.BlockSpec((TQ, LANES), lambda p, i: (i, q_off + p))
    k_spec = pl.BlockSpec((s, LANES), lambda p, i: (0, k_off + p))
    v_spec = pl.BlockSpec((s, LANES), lambda p, i: (0, v_off + p))
    return nhb, N_HEADS // nhb, nsub, q_spec, k_spec, v_spec


def _mixer_geometry(mode, i, nsub):
    w = TQ * nsub
    row = lax.broadcasted_iota(jnp.int32, (TQ, w), 0)
    col = lax.broadcasted_iota(jnp.int32, (TQ, w), 1)
    nfull = i // nsub
    dist = col - row
    if mode in ("fox", "sb"):
        rel = dist
    else:
        rel = col - (row | (CHUNK - 1))

    def visible(c):
        off = c * w - i * TQ
        return (rel + off) < 0 if mode == "sb" else (rel + off) <= 0

    return nfull, dist, visible


def _mixer_fwd(mode, qa, q_off, ka, k_off, va, v_off, *, cum_col=None, cum_row=None):
    s = qa.shape[0]
    nq = s // TQ
    nhb, nblk, nsub, q_spec, k_spec, v_spec = _mixer_specs(mode, s, q_off, k_off, v_off)
    w = TQ * nsub
    softmax = mode in ("fox", "mla")
    has_stat = mode != "ret"

    def body(*refs):
        refs = list(refs)
        q_ref, k_ref, v_ref = refs[:3]
        refs = refs[3:]
        if mode == "fox":
            cc_ref, cr_ref = refs[:2]
            refs = refs[2:]
        o_ref = refs[0]
        st_ref = refs[1] if has_stat else None
        p = pl.program_id(0)
        i = pl.program_id(1)
        nfull, dist, visible = _mixer_geometry(mode, i, nsub)
        lane = _lane((1, LANES))
        heads = [nhb * p + hh for hh in range(nhb)]
        if nhb == 1:
            qs = [q_ref[...]]
        else:
            qf = q_ref[...].astype(F32)
            qs = [jnp.where((lane // HEAD) == hh, qf, 0.0).astype(BF16) for hh in range(nhb)]
        if mode == "fox":
            cqs = [_head_pick(cc_ref[...], h, 1) for h in heads]
        if mode == "ret":
            lgs = [_log_gamma_of(h) for h in heads]
            distf = dist.astype(F32)
        if mode == "sb":
            r1 = lax.broadcasted_iota(jnp.int32, (TQ, TQ), 0)
            c1 = lax.broadcasted_iota(jnp.int32, (TQ, TQ), 1)
            u_after = (r1 > c1).astype(BF16)

        def head_step(hh, c, js, kj, vj, carry, last):
            sc = _dot_nt(qs[hh], kj)
            if softmax:
                m, l, acc = carry
                if mode == "fox":
                    ck = _head_pick(cr_ref[:, js], heads[hh], 0)
                    sc = sc * _QK_SCALE + (cqs[hh] - ck)
                else:
                    sc = sc * _MLA_SCALE
                if last:
                    sc = jnp.where(visible(c), sc, NEG)
                m_new = jnp.maximum(m, jnp.max(sc, axis=-1, keepdims=True))
                alpha = jnp.exp(m - m_new)
                pr = jnp.exp(sc - m_new)
                l = alpha * l + jnp.sum(pr, axis=-1, keepdims=True)
                acc = alpha * acc + _dot(pr.astype(BF16), vj)
                return m_new, l, acc
            if mode == "ret":
                off = (i * TQ - c * w).astype(F32)
                if last:
                    dec = jnp.where(visible(c), jnp.exp(lgs[hh] * jnp.abs(off - distf)), 0.0)
                else:
                    dec = jnp.exp(lgs[hh] * (off - distf))
                return carry + _dot((sc * dec).astype(BF16), vj)
            run, acc = carry
            z = sc * _QK_SCALE
            lp = jnp.log1p(jnp.exp(-jnp.abs(z)))
            log_stay = jnp.minimum(-z, 0.0) - lp
            if last:
                vis = visible(c)
                log_stay = jnp.where(vis, log_stay, 0.0)
            parts = [None] * nsub
            for b in reversed(range(nsub)):
                ls_b = log_stay[:, b * TQ:(b + 1) * TQ]
                parts[b] = _dot2(ls_b, u_after) + run
                run = run + jnp.sum(ls_b, axis=-1, keepdims=True)
            later = parts[0] if nsub == 1 else jnp.concatenate(parts, axis=1)
            wgt = jnp.exp(jnp.minimum(z, 0.0) - lp + later)
            if last:
                wgt = jnp.where(vis, wgt, 0.0)
            return run, acc + _dot(wgt.astype(BF16), vj)

        def step(c, carries, last):
            js = pl.ds(pl.multiple_of(c * w, w), w)
            kj, vj = k_ref[js, :], v_ref[js, :]
            return tuple(head_step(hh, c, js, kj, vj, carries[hh], last) for hh in range(nhb))

        zero_acc = jnp.zeros((TQ, LANES), F32)
        zero1 = jnp.zeros((TQ, 1), F32)
        if softmax:
            init = tuple((jnp.full((TQ, 1), NEG, F32), zero1, zero_acc) for _ in range(nhb))
        elif mode == "ret":
            init = tuple(zero_acc for _ in range(nhb))
        else:
            init = tuple((zero1, zero_acc) for _ in range(nhb))
        if mode == "sb":
            carries = step(nfull, init, True)
            carries = lax.fori_loop(0, nfull, lambda t, cs: step(nfull - 1 - t, cs, False), carries)
        else:
            carries = lax.fori_loop(0, nfull, lambda c, cs: step(c, cs, False), init)
            carries = step(nfull, carries, True)
        if softmax:
            outs = [acc / l for (m, l, acc) in carries]
            stats = [m + jnp.log(l) for (m, l, acc) in carries]
        elif mode == "ret":
            outs, stats = list(carries), None
        else:
            outs, stats = [acc for (run, acc) in carries], [run for (run, acc) in carries]
        hm0 = (lane // HEAD) == 0
        pick = (lambda a: jnp.broadcast_to(a[0], (TQ, LANES))) if nhb == 1 else (lambda a: jnp.where(hm0, a[0], a[1]))
        o_ref[...] = pick(outs)
        if has_stat:
            st_ref[0] = pick(stats)

    in_specs = [q_spec, k_spec, v_spec]
    args = [qa, ka, va]
    if mode == "fox":
        in_specs += [pl.BlockSpec((TQ, LANES), lambda p, i: (i, 0)), pl.BlockSpec((8, s), lambda p, i: (0, 0))]
        args += [cum_col, cum_row]
    out_specs = [pl.BlockSpec((TQ, LANES), lambda p, i: (i, p))]
    out_shape = [jax.ShapeDtypeStruct((s, nblk * LANES), F32)]
    if has_stat:
        out_specs.append(pl.BlockSpec((1, TQ, LANES), lambda p, i: (p, i, 0)))
        out_shape.append(jax.ShapeDtypeStruct((nblk, s, LANES), F32))
    res = pl.pallas_call(
        body, name=mode + "_fwd", grid=(nblk, nq), in_specs=in_specs, out_specs=out_specs, out_shape=out_shape,
        compiler_params=_cparams(("parallel", "parallel")),
    )(*args)
    return res if has_stat else (res[0], None)


def _mixer_bwd(mode, qa, q_off, ka, k_off, va, v_off, o, do, *, stat=None, cum_col=None, cum_row=None):
    s = qa.shape[0]
    nq = s // TQ
    nhb, nblk, nsub, q_spec, k_spec, v_spec = _mixer_specs(mode, s, q_off, k_off, v_off)
    w = TQ * nsub
    softmax = mode in ("fox", "mla")
    has_stat = mode != "ret"

    def body(*refs):
        refs = list(refs)
        q_ref, k_ref, v_ref, o_ref, do_ref = refs[:5]
        refs = refs[5:]
        if has_stat:
            st_ref = refs[0]
            refs = refs[1:]
        if mode == "fox":
            cc_ref, cr_ref = refs[:2]
            refs = refs[2:]
        dq_ref, dk_ref, dv_ref = refs[:3]
        dck_ref, drs_ref = refs[3:5] if mode == "fox" else (None, None)
        p = pl.program_id(0)
        i = pl.program_id(1)

        @pl.when(i == 0)
        def _():
            dk_ref[...] = jnp.zeros_like(dk_ref)
            dv_ref[...] = jnp.zeros_like(dv_ref)
            if mode == "fox":
                dck_ref[...] = jnp.zeros_like(dck_ref)

        nfull, dist, visible = _mixer_geometry(mode, i, nsub)
        lane = _lane((1, LANES))
        heads = [nhb * p + hh for hh in range(nhb)]
        dov = do_ref[...]
        if nhb == 1:
            qs = [q_ref[...]]
            dos = [dov.astype(BF16)]
            deltas = [jnp.sum(dov * o_ref[...], axis=-1, keepdims=True)]
        else:
            qf = q_ref[...].astype(F32)
            prod = dov * o_ref[...]
            hms = [(lane // HEAD) == hh for hh in range(nhb)]
            qs = [jnp.where(hm, qf, 0.0).astype(BF16) for hm in hms]
            dos = [jnp.where(hm, dov, 0.0).astype(BF16) for hm in hms]
            deltas = [jnp.sum(jnp.where(hm, prod, 0.0), axis=-1, keepdims=True) for hm in hms]
        if has_stat:
            st = st_ref[0]
            stats = [st[:, hh * HEAD:hh * HEAD + 1] for hh in range(nhb)]
        if mode == "fox":
            cqs = [_head_pick(cc_ref[...], h, 1) for h in heads]
        if mode == "ret":
            lgs = [_log_gamma_of(h) for h in heads]
            distf = dist.astype(F32)
        if mode == "sb":
            r1 = lax.broadcasted_iota(jnp.int32, (TQ, TQ), 0)
            c1 = lax.broadcasted_iota(jnp.int32, (TQ, TQ), 1)
            u_upto = (r1 <= c1).astype(BF16)
            u_before = (r1 < c1).astype(BF16)

        def emit(hh, js, kj, ds_b, pr_b, dq):
            dk_ref[js, :] += _dot_tn(ds_b, qs[hh])
            dv_ref[js, :] += _dot_tn(pr_b, dos[hh])
            return dq + _dot(ds_b, kj)

        def head_step(hh, c, js, kj, vj, carry, last):
            sc = _dot_nt(qs[hh], kj)
            dp = _dot_nt(dos[hh], vj)
            if softmax:
                dq, rsum = carry
                scale = _QK_SCALE if mode == "fox" else _MLA_SCALE
                if mode == "fox":
                    ck = _head_pick(cr_ref[:, js], heads[hh], 0)
                    sc = sc * scale + (cqs[hh] - ck)
                else:
                    sc = sc * scale
                if last:
                    sc = jnp.where(visible(c), sc, NEG)
                pr = jnp.exp(sc - stats[hh])
                ds = pr * (dp - deltas[hh])
                if mode == "fox":
                    dck_ref[0, hh:hh + 1, js] += jnp.sum(ds, axis=0, keepdims=True)
                    rsum = rsum + jnp.sum(ds, axis=-1, keepdims=True)
                return emit(hh, js, kj, (ds * scale).astype(BF16), pr.astype(BF16), dq), rsum
            if mode == "ret":
                off = (i * TQ - c * w).astype(F32)
                if last:
                    dec = jnp.where(visible(c), jnp.exp(lgs[hh] * jnp.abs(off - distf)), 0.0)
                else:
                    dec = jnp.exp(lgs[hh] * (off - distf))
                return emit(hh, js, kj, (dp * dec).astype(BF16), (sc * dec).astype(BF16), carry)
            seen, gsum, dq = carry
            z = sc * _QK_SCALE
            lp = jnp.log1p(jnp.exp(-jnp.abs(z)))
            log_stay = jnp.minimum(-z, 0.0) - lp
            if last:
                vis = visible(c)
                log_stay = jnp.where(vis, log_stay, 0.0)
            parts = []
            for b in range(nsub):
                ls_b = log_stay[:, b * TQ:(b + 1) * TQ]
                parts.append((stats[hh] - seen) - _dot2(ls_b, u_upto))
                seen = seen + jnp.sum(ls_b, axis=-1, keepdims=True)
            later = parts[0] if nsub == 1 else jnp.concatenate(parts, axis=1)
            log_beta = jnp.minimum(z, 0.0) - lp
            wgt = jnp.exp(log_beta + later)
            if last:
                wgt = jnp.where(vis, wgt, 0.0)
            g = dp * wgt
            parts = []
            for b in range(nsub):
                g_b = g[:, b * TQ:(b + 1) * TQ]
                parts.append(gsum + _dot2(g_b, u_before))
                gsum = gsum + jnp.sum(g_b, axis=-1, keepdims=True)
            before = parts[0] if nsub == 1 else jnp.concatenate(parts, axis=1)
            beta = jnp.exp(log_beta)
            dz = g * (1.0 - beta) - beta * before
            if last:
                dz = jnp.where(vis, dz, 0.0)
            return seen, gsum, emit(hh, js, kj, (dz * _QK_SCALE).astype(BF16), wgt.astype(BF16), dq)

        def step(c, carries, last):
            js = pl.ds(pl.multiple_of(c * w, w), w)
            kj, vj = k_ref[js, :], v_ref[js, :]
            return tuple(head_step(hh, c, js, kj, vj, carries[hh], last) for hh in range(nhb))

        zero_acc = jnp.zeros((TQ, LANES), F32)
        zero1 = jnp.zeros((TQ, 1), F32)
        if softmax:
            init = tuple((zero_acc, zero1) for _ in range(nhb))
        elif mode == "ret":
            init = tuple(zero_acc for _ in range(nhb))
        else:
            init = tuple((zero1, zero1, zero_acc) for _ in range(nhb))
        carries = lax.fori_loop(0, nfull, lambda c, cs: step(c, cs, False), init)
        carries = step(nfull, carries, True)
        if softmax:
            dqs = [dq for (dq, rsum) in carries]
        elif mode == "ret":
            dqs = list(carries)
        else:
            dqs = [dq for (seen, gsum, dq) in carries]
        hm0 = (lane // HEAD) == 0
        dq_ref[...] = dqs[0] if nhb == 1 else jnp.where(hm0, dqs[0], dqs[1])
        if mode == "fox":
            drs_ref[0] = jnp.where(hm0, carries[0][1], carries[1][1])

    pair_blk = pl.BlockSpec((TQ, LANES), lambda p, i: (i, p))
    full_blk = pl.BlockSpec((s, LANES), lambda p, i: (0, p))
    stat_blk = pl.BlockSpec((1, TQ, LANES), lambda p, i: (p, i, 0))
    in_specs = [q_spec, k_spec, v_spec, pair_blk, pair_blk]
    args = [qa, ka, va, o, do]
    if has_stat:
        in_specs.append(stat_blk)
        args.append(stat)
    if mode == "fox":
        in_specs += [pl.BlockSpec((TQ, LANES), lambda p, i: (i, 0)), pl.BlockSpec((8, s), lambda p, i: (0, 0))]
        args += [cum_col, cum_row]
    out_specs = [pair_blk, full_blk, full_blk]
    out_shape = [jax.ShapeDtypeStruct((s, nblk * LANES), F32)] * 3
    if mode == "fox":
        out_specs += [pl.BlockSpec((1, 8, s), lambda p, i: (p, 0, 0)), stat_blk]
        out_shape += [jax.ShapeDtypeStruct((2, 8, s), F32), jax.ShapeDtypeStruct((2, s, LANES), F32)]
    return pl.pallas_call(
        body, name=mode + "_bwd", grid=(nblk, nq), in_specs=in_specs, out_specs=out_specs, out_shape=out_shape,
        compiler_params=_cparams(("parallel", "arbitrary")),
    )(*args)


def _seg_mean_matrix():
    r = lax.broadcasted_iota(jnp.int32, (GROUP, GROUP), 0)
    c = lax.broadcasted_iota(jnp.int32, (GROUP, GROUP), 1)
    return jnp.where((r // HEAD) == (c // HEAD), 1.0 / HEAD, 0.0).astype(F32)


def _sigmoid(x):
    return 1.0 / (1.0 + jnp.exp(-x))


def _mix_post(oa, ob, oc, od, proj, g):
    s = oa.shape[0]
    tr = _tile(s, 256)

    def body(a_ref, b_ref, c_ref, d_ref, rg_ref, g_ref, o_ref):
        gv = g_ref[...]
        o_ref[:, 0:GROUP] = _rms(a_ref[...], gv[:, 0:GROUP]).astype(BF16)
        o_ref[:, GROUP:2 * GROUP] = _rms(b_ref[...], gv[:, GROUP:2 * GROUP]).astype(BF16)
        seg = _seg_mean_matrix()
        c = c_ref[...]
        cen = c - _dot_exact(c, seg)
        n = cen * lax.rsqrt(_dot_exact(cen * cen, seg) + EPS)
        rg = rg_ref[...]
        o_ref[:, 2 * GROUP:3 * GROUP] = (n * gv[:, 2 * GROUP:3 * GROUP] * (rg * _sigmoid(rg))).astype(BF16)
        o_ref[:, 3 * GROUP:] = _rms(d_ref[...], gv[:, 3 * GROUP:]).astype(BF16)

    blk = pl.BlockSpec((tr, GROUP), lambda i: (i, 0))
    return pl.pallas_call(
        body, name="mix_post", grid=(s // tr,),
        in_specs=[blk] * 4 + [pl.BlockSpec((tr, GROUP), lambda i: (i, OFF_RG // 2)), pl.BlockSpec((1, D_MODEL), lambda i: (0, 0))],
        out_specs=pl.BlockSpec((tr, D_MODEL), lambda i: (i, 0)), out_shape=jax.ShapeDtypeStruct((s, D_MODEL), BF16),
        compiler_params=_cparams(("parallel",)),
    )(oa, ob, oc, od, proj, g.reshape(1, D_MODEL))


def _mix_post_bwd(dmixed, oa, ob, oc, od, proj, g):
    s = oa.shape[0]
    tr = _tile(s, 256)

    def body(dm_ref, a_ref, b_ref, c_ref, d_ref, rg_ref, g_ref, da_ref, db_ref, dc_ref, dd_ref, drg_ref, dg_ref):
        @pl.when(pl.program_id(0) == 0)
        def _():
            dg_ref[...] = jnp.zeros_like(dg_ref)

        gv = g_ref[...]
        dm = dm_ref[...]
        for k, (x_ref, dx_ref) in enumerate(((a_ref, da_ref), (b_ref, db_ref), (None, None), (d_ref, dd_ref))):
            if x_ref is None:
                continue
            cols = slice(k * GROUP, (k + 1) * GROUP)
            dx, gterm = _rms_bwd(x_ref[...], gv[:, cols], dm[:, cols])
            dx_ref[...] = dx
            dg_ref[:, cols] += jnp.sum(gterm, axis=0, keepdims=True)
        cols = slice(2 * GROUP, 3 * GROUP)
        seg = _seg_mean_matrix()
        c = c_ref[...]
        cen = c - _dot_exact(c, seg)
        rstd = lax.rsqrt(_dot_exact(cen * cen, seg) + EPS)
        n = cen * rstd
        rg = rg_ref[...]
        sg = _sigmoid(rg)
        gate = rg * sg
        dy = dm[:, cols]
        gc = gv[:, cols]
        dn = dy * gc * gate
        dg_ref[:, cols] += jnp.sum(dy * n * gate, axis=0, keepdims=True)
        drg_ref[...] = (dy * n * gc * (sg * (1.0 + rg * (1.0 - sg)))).astype(BF16)
        dc_ref[...] = rstd * (dn - _dot_exact(dn, seg) - n * _dot_exact(dn * n, seg))

    blk = pl.BlockSpec((tr, GROUP), lambda i: (i, 0))
    gsp = pl.BlockSpec((1, D_MODEL), lambda i: (0, 0))
    return pl.pallas_call(
        body, name="mix_post_bwd", grid=(s // tr,),
        in_specs=[pl.BlockSpec((tr, D_MODEL), lambda i: (i, 0))] + [blk] * 4 + [pl.BlockSpec((tr, GROUP), lambda i: (i, OFF_RG // 2)), gsp],
        out_specs=[blk] * 5 + [gsp],
        out_shape=[jax.ShapeDtypeStruct((s, GROUP), F32)] * 4 + [jax.ShapeDtypeStruct((s, GROUP), BF16), jax.ShapeDtypeStruct((1, D_MODEL), F32)],
        compiler_params=_cparams(("arbitrary",)),
    )(dmixed, oa, ob, oc, od, proj, g.reshape(1, D_MODEL))


def _pack_w_in(w):
    z = lambda n: jnp.zeros((w.shape[0], n), w.dtype)
    misc = jnp.concatenate([w[:, 768:772], z(KR_LANE - N_HEADS), w[:, 1156:1188], z(LANES - KR_LANE - ROPE_DIM)], axis=1)
    return jnp.concatenate([w[:, 0:768], w[:, 772:1028], w[:, 1188:2980], w[:, 1028:1156], misc], axis=1)


def _unpack_dw_in(d):
    m = OFF_MISC * LANES
    return jnp.concatenate([d[:, 0:768], d[:, m:m + N_HEADS], d[:, 768:1024], d[:, OFF_CKV * LANES:m],
                            d[:, m + KR_LANE:m + KR_LANE + ROPE_DIM], d[:, 1024:OFF_CKV * LANES]], axis=1)


def _pack_w_q(w):
    return jnp.pad(w.reshape(Q_RANK, N_HEADS, HEAD + ROPE_DIM), ((0, 0), (0, 0), (0, LANES - HEAD - ROPE_DIM))).reshape(Q_RANK, 4 * LANES)


def _unpack_dw_q(d):
    return d.reshape(Q_RANK, N_HEADS, LANES)[:, :, :HEAD + ROPE_DIM].reshape(Q_RANK, N_HEADS * (HEAD + ROPE_DIM))


def _pack_w_kv(w):
    w4 = w.reshape(KV_RANK, N_HEADS, 2 * HEAD)
    widen = lambda a: jnp.pad(a, ((0, 0), (0, 0), (0, LANES - HEAD))).reshape(KV_RANK, N_HEADS * LANES)
    return widen(w4[:, :, :HEAD]), widen(w4[:, :, HEAD:])


def _unpack_dw_kv(dk, dv):
    narrow = lambda a: a.reshape(KV_RANK, N_HEADS, LANES)[:, :, :HEAD]
    return jnp.concatenate([narrow(dk), narrow(dv)], axis=2).reshape(KV_RANK, 2 * N_HEADS * HEAD)


def _narrow_heads(a):
    return a.reshape(a.shape[0], N_HEADS, LANES)[:, :, :HEAD].reshape(a.shape[0], N_HEADS * HEAD)


def _widen_heads(a):
    return jnp.pad(a.reshape(a.shape[0], N_HEADS, HEAD), ((0, 0), (0, 0), (0, LANES - HEAD))).reshape(a.shape[0], N_HEADS * LANES)


def _layer_fwd(x, lw, tabs, tag):
    cos_m, sin_m, cos_r, sin_r = tabs
    h1 = _norm_fwd(x, lw["g_mix_pre"], name=tag + "pre_norm")
    proj, projb = _matmul(h1, lw["w_in"], name=tag + "in_proj", also_bf16=True)
    bias_row = jnp.pad(lw["b_forget"], (FF_LANE, LANES - N_HEADS - FF_LANE)).reshape(1, LANES)
    cum_col, cum_row = _fox_cum(proj, bias_row)
    oa, lse_a = _mixer_fwd("fox", projb, OFF_FQ, projb, OFF_FK, projb, OFF_FV, cum_col=cum_col, cum_row=cum_row)
    qm, km, vm, cqn, ckvn = _mla_prep(proj, cos_m, sin_m, lw["g_q_lora"], lw["g_kv_lora"], lw["wq"], lw["wk"], lw["wv"])
    ob_wide, lse_b = _mixer_fwd("mla", qm, 0, km, 0, vm, 0)
    ob = _narrow_heads(ob_wide)
    qr, kr = _ret_prep(proj, cos_r, sin_r)
    oc, _ = _mixer_fwd("ret", qr, 0, kr, 0, projb, OFF_RV)
    od, tot_d = _mixer_fwd("sb", projb, OFF_SQ, projb, OFF_SK, projb, OFF_SV)
    mixed = _mix_post(oa, ob, oc, od, proj, lw["g_mix_out"])
    mix = _matmul(mixed, lw["w_out"], name=tag + "out_proj")
    x1 = _norm_fwd(mix, lw["g_mix_post"], name=tag + "mix_post_norm", resid=x, out_dtype=F32)
    h2 = _norm_fwd(x1, lw["g_ffn_pre"], name=tag + "ffn_pre_norm")
    u_pre, u = _matmul(h2, lw["w_ffn_up"], name=tag + "ffn_up", relu2=True)
    f = _matmul(u, lw["w_ffn_down"], name=tag + "ffn_down")
    x2 = _norm_fwd(f, lw["g_ffn_post"], name=tag + "ffn_post_norm", resid=x1, out_dtype=F32)
    saved = dict(x=x, h1=h1, proj=proj, projb=projb, bias_row=bias_row, cum_col=cum_col, cum_row=cum_row, oa=oa, lse_a=lse_a,
                 qm=qm, km=km, vm=vm, cqn=cqn, ckvn=ckvn, ob=ob, ob_wide=ob_wide, lse_b=lse_b, qr=qr, kr=kr, oc=oc, od=od, tot_d=tot_d, mixed=mixed,
                 mix=mix, x1=x1, h2=h2, u_pre=u_pre, u=u, f=f)
    return x2, saved


def _layer_bwd(dx2, lw, sv, tabs, tag):
    cos_m, sin_m, cos_r, sin_r = tabs
    g = {}
    df, g["g_ffn_post"] = _norm_bwd(sv["f"], lw["g_ffn_post"], dx2, name=tag + "ffn_post_norm_bwd", out_dtype=BF16)
    du_pre = _matmul(df, lw["w_ffn_down"], name=tag + "ffn_down_dx", tb=True, out_dtype=BF16, relu_of=sv["u_pre"])
    g["w_ffn_down"] = _matmul(sv["u"], df, name=tag + "ffn_down_dw", ta=True)
    dh2 = _matmul(du_pre, lw["w_ffn_up"], name=tag + "ffn_up_dx", tb=True)
    g["w_ffn_up"] = _matmul(sv["h2"], du_pre, name=tag + "ffn_up_dw", ta=True)
    dx1, g["g_ffn_pre"] = _norm_bwd(sv["x1"], lw["g_ffn_pre"], dh2, name=tag + "ffn_pre_norm_bwd", add=dx2)
    dmix, g["g_mix_post"] = _norm_bwd(sv["mix"], lw["g_mix_post"], dx1, name=tag + "mix_post_norm_bwd", out_dtype=BF16)
    dmixed = _matmul(dmix, lw["w_out"], name=tag + "out_proj_dx", tb=True)
    g["w_out"] = _matmul(sv["mixed"], dmix, name=tag + "out_proj_dw", ta=True)
    proj, projb = sv["proj"], sv["projb"]
    doa, dob, doc, dod, drg, g["g_mix_out"] = _mix_post_bwd(dmixed, sv["oa"], sv["ob"], sv["oc"], sv["od"], proj, lw["g_mix_out"])
    dfq, dfk, dfv, dck, drs = _mixer_bwd("fox", projb, OFF_FQ, projb, OFF_FK, projb, OFF_FV, sv["oa"], doa, stat=sv["lse_a"],
                                         cum_col=sv["cum_col"], cum_row=sv["cum_row"])
    dqm, dkm, dvm = _mixer_bwd("mla", sv["qm"], 0, sv["km"], 0, sv["vm"], 0, sv["ob_wide"], _widen_heads(dob), stat=sv["lse_b"])
    dcq, dckv, dkr, dwq, dwk, dwv, g["g_q_lora"], g["g_kv_lora"] = _mla_prep_bwd(
        dqm, dkm, dvm, proj, sv["cqn"], sv["ckvn"], cos_m, sin_m, lw["g_q_lora"], lw["g_kv_lora"], lw["wq"], lw["wk"], lw["wv"])
    dqr, dkr_ret, drv = _mixer_bwd("ret", sv["qr"], 0, sv["kr"], 0, projb, OFF_RV, sv["oc"], doc)
    drq, drk = _ret_prep_bwd(dqr, dkr_ret, cos_r, sin_r)
    dsq, dsk, dsv = _mixer_bwd("sb", projb, OFF_SQ, projb, OFF_SK, projb, OFF_SV, sv["od"], dod, stat=sv["tot_d"])
    dmisc, db_row = _fox_gate_bwd(dck, drs, proj, sv["bias_row"], dkr)
    b = lambda a: a.astype(BF16)
    dproj = jnp.concatenate([b(dfq), b(dfk), b(dfv), dcq, drq, drk, b(drv), drg, b(dsq), b(dsk), b(dsv), dckv, dmisc], axis=1)
    dh1 = _matmul(dproj, lw["w_in"], name=tag + "in_proj_dx", tb=True)
    g["w_in"] = _matmul(sv["h1"], dproj, name=tag + "in_proj_dw", ta=True)
    dx, g["g_mix_pre"] = _norm_bwd(sv["x"], lw["g_mix_pre"], dh1, name=tag + "pre_norm_bwd", add=dx1)
    g["b_forget"] = db_row[0, FF_LANE:FF_LANE + N_HEADS]
    g["wq"], g["wk"], g["wv"] = dwq, dwk, dwv
    return dx, g


def _local_step(x, positions, layers, target):
    s = x.shape[0]
    tabs = _rope_tables(positions.reshape(s, 1))
    saved = []
    for li, lw in enumerate(layers):
        x, sv = _layer_fwd(x, lw, tabs, "l%d_" % li)
        saved.append(sv)
    loss_row, dx = _loss_head(x, target)
    grads = [None] * len(layers)
    for li in reversed(range(len(layers))):
        dx, grads[li] = _layer_bwd(dx, layers[li], saved[li], tabs, "l%d_" % li)
    return loss_row[0, 0], dx, grads


def _adamw(w, g, m, v, *, name):
    r, c = w.shape
    tr = 256 if r % 256 == 0 else r
    blk = pl.BlockSpec((tr, c), lambda i: (i, 0))
    c1 = 1.0 - ADAM_B1 ** ADAM_STEP
    c2 = 1.0 - ADAM_B2 ** ADAM_STEP

    def body(w_ref, g_ref, m_ref, v_ref, d_ref, mo_ref, vo_ref):
        gv = g_ref[...]
        mn = ADAM_B1 * m_ref[...] + (1.0 - ADAM_B1) * gv
        vn = ADAM_B2 * v_ref[...] + (1.0 - ADAM_B2) * jnp.square(gv)
        mo_ref[...] = mn
        vo_ref[...] = vn
        d_ref[...] = -ADAM_LR * ((mn / c1) / (jnp.sqrt(vn / c2) + ADAM_EPS) + ADAM_WD * w_ref[...])

    return pl.pallas_call(
        body, name=name, grid=(r // tr,), in_specs=[blk] * 4, out_specs=[blk] * 3,
        out_shape=[jax.ShapeDtypeStruct((r, c), F32)] * 3, compiler_params=_cparams(("parallel",)),
    )(w, g, m, v)


BIG = ("w_in", "w_q_up", "w_kv_up", "w_out", "w_ffn_up", "w_ffn_down")
SMALL = ("g_mix_pre", "b_forget", "g_q_lora", "g_kv_lora", "g_mix_out", "g_mix_post", "g_ffn_pre", "g_ffn_post")
FLAT_COLS = 1024
HALF_TILE = 640
N_CHIPS = 4
ANY = pl.BlockSpec(memory_space=pl.ANY)


def _mesh_pos():
    return lax.axis_index("x"), lax.axis_index("y"), lax.axis_index("c")


def _other_chips(x, y):
    return [(1 - x, y), (x, 1 - y), (1 - x, 1 - y)]


def _gather_shards(flat):
    rows, cols = flat.shape

    def body(x_ref, o_ref, send_sems, recv_sems, own_sem):
        x, y, c = _mesh_pos()
        mine = 2 * x + y
        own = pltpu.make_async_copy(x_ref, o_ref.at[mine], own_sem)
        own.start()
        peers = _other_chips(x, y)

        def copy(j, block):
            px, py = peers[j]
            return pltpu.make_async_remote_copy(src_ref=x_ref, dst_ref=o_ref.at[block], send_sem=send_sems.at[j],
                                                recv_sem=recv_sems.at[j], device_id=(px, py, c), device_id_type=MESH)

        sends = [copy(j, mine) for j in range(3)]
        for cp in sends:
            cp.start()
        for j, (px, py) in enumerate(peers):
            copy(j, 2 * px + py).wait_recv()
        for cp in sends:
            cp.wait_send()
        own.wait()

    return pl.pallas_call(
        body, name="gather_weights", in_specs=[ANY], out_specs=ANY,
        out_shape=jax.ShapeDtypeStruct((N_CHIPS, rows, cols), flat.dtype),
        scratch_shapes=[pltpu.SemaphoreType.DMA((3,)), pltpu.SemaphoreType.DMA((3,)), pltpu.SemaphoreType.DMA],
        compiler_params=pltpu.CompilerParams(has_side_effects=True),
    )(flat)


def _send_other_half(g):
    _, _, h, cols = g.shape

    def body(g_ref, o_ref, send_sems, recv_sems):
        x, y, c = _mesh_pos()

        def copy(k):
            return pltpu.make_async_remote_copy(src_ref=g_ref.at[k, 1 - c], dst_ref=o_ref.at[k], send_sem=send_sems.at[k],
                                                recv_sem=recv_sems.at[k], device_id=(x, y, 1 - c), device_id_type=MESH)

        cps = [copy(k) for k in range(N_CHIPS)]
        for cp in cps:
            cp.start()
        for cp in cps:
            cp.wait_recv()
        for cp in cps:
            cp.wait_send()

    return pl.pallas_call(
        body, name="grad_pair_exchange", in_specs=[ANY], out_specs=ANY, out_shape=jax.ShapeDtypeStruct((N_CHIPS, h, cols), g.dtype),
        scratch_shapes=[pltpu.SemaphoreType.DMA((N_CHIPS,)), pltpu.SemaphoreType.DMA((N_CHIPS,))],
        compiler_params=pltpu.CompilerParams(has_side_effects=True),
    )(g)


def _add_own_half(g, r, c_idx):
    _, _, h, cols = g.shape
    tr = HALF_TILE

    def body(c_ref, g_ref, r_ref, o_ref):
        o_ref[0] = g_ref[0, 0] + r_ref[0]

    return pl.pallas_call(
        body, name="grad_pair_add",
        grid_spec=pltpu.PrefetchScalarGridSpec(
            num_scalar_prefetch=1, grid=(N_CHIPS, h // tr),
            in_specs=[pl.BlockSpec((1, 1, tr, cols), lambda k, i, c_ref: (k, c_ref[0], i, 0)),
                      pl.BlockSpec((1, tr, cols), lambda k, i, c_ref: (k, i, 0))],
            out_specs=pl.BlockSpec((1, tr, cols), lambda k, i, c_ref: (k, i, 0))),
        out_shape=jax.ShapeDtypeStruct((N_CHIPS, h, cols), F32), compiler_params=_cparams(("parallel", "parallel")),
    )(c_idx, g, r)


def _send_to_owners(p):
    _, h, cols = p.shape

    def body(p_ref, o_ref, send_sems, recv_sems):
        x, y, c = _mesh_pos()
        peers = _other_chips(x, y)

        def copy(j):
            px, py = peers[j]
            return pltpu.make_async_remote_copy(src_ref=p_ref.at[2 * px + py], dst_ref=o_ref.at[j], send_sem=send_sems.at[j],
                                                recv_sem=recv_sems.at[j], device_id=(px, py, c), device_id_type=MESH)

        cps = [copy(j) for j in range(3)]
        for cp in cps:
            cp.start()
        for cp in cps:
            cp.wait_recv()
        for cp in cps:
            cp.wait_send()

    return pl.pallas_call(
        body, name="grad_chip_exchange", in_specs=[ANY], out_specs=ANY, out_shape=jax.ShapeDtypeStruct((3, h, cols), p.dtype),
        scratch_shapes=[pltpu.SemaphoreType.DMA((3,)), pltpu.SemaphoreType.DMA((3,))],
        compiler_params=pltpu.CompilerParams(has_side_effects=True),
    )(p)


def _add_chip_partials(p, r, k_idx):
    _, h, cols = p.shape
    tr = HALF_TILE

    def body(k_ref, p_ref, r_ref, o_ref):
        o_ref[...] = ((p_ref[0] + r_ref[0]) + r_ref[1]) + r_ref[2]

    return pl.pallas_call(
        body, name="grad_chip_add",
        grid_spec=pltpu.PrefetchScalarGridSpec(
            num_scalar_prefetch=1, grid=(h // tr,),
            in_specs=[pl.BlockSpec((1, tr, cols), lambda i, k_ref: (k_ref[0], i, 0)),
                      pl.BlockSpec((3, tr, cols), lambda i, k_ref: (0, i, 0))],
            out_specs=pl.BlockSpec((tr, cols), lambda i, k_ref: (i, 0))),
        out_shape=jax.ShapeDtypeStruct((h, cols), F32), compiler_params=_cparams(("parallel",)),
    )(k_idx, p, r)


def _share_half(q):
    h, cols = q.shape

    def body(q_ref, o_ref, send_sem, recv_sem, own_sem):
        x, y, c = _mesh_pos()
        own = pltpu.make_async_copy(q_ref, o_ref.at[c], own_sem)
        own.start()
        send = pltpu.make_async_remote_copy(src_ref=q_ref, dst_ref=o_ref.at[c], send_sem=send_sem, recv_sem=recv_sem,
                                            device_id=(x, y, 1 - c), device_id_type=MESH)
        send.start()
        pltpu.make_async_remote_copy(src_ref=q_ref, dst_ref=o_ref.at[1 - c], send_sem=send_sem, recv_sem=recv_sem,
                                     device_id=(x, y, 1 - c), device_id_type=MESH).wait_recv()
        send.wait_send()
        own.wait()

    return pl.pallas_call(
        body, name="grad_pair_share", in_specs=[ANY], out_specs=ANY, out_shape=jax.ShapeDtypeStruct((2, h, cols), q.dtype),
        scratch_shapes=[pltpu.SemaphoreType.DMA, pltpu.SemaphoreType.DMA, pltpu.SemaphoreType.DMA],
        compiler_params=pltpu.CompilerParams(has_side_effects=True),
    )(q)


def _all_reduce_small(v):
    r, cols = v.shape
    n_dev = 8

    def body(v_ref, o_ref, buf, send_sems, recv_sems):
        x, y, c = _mesh_pos()
        me = 4 * x + 2 * y + c
        buf[me] = v_ref[...]

        def peer(j):
            return (1 - x if j & 4 else x, 1 - y if j & 2 else y, 1 - c if j & 1 else c)

        def copy(j, slot):
            return pltpu.make_async_remote_copy(src_ref=v_ref, dst_ref=buf.at[slot], send_sem=send_sems.at[j - 1],
                                                recv_sem=recv_sems.at[j - 1], device_id=peer(j), device_id_type=MESH)

        sends = [copy(j, me) for j in range(1, n_dev)]
        for cp in sends:
            cp.start()
        for j in range(1, n_dev):
            px, py, pc = peer(j)
            copy(j, 4 * px + 2 * py + pc).wait_recv()
        for cp in sends:
            cp.wait_send()
        acc = buf[0]
        for d in range(1, n_dev):
            acc = acc + buf[d]
        o_ref[...] = acc

    vm = pl.BlockSpec(memory_space=pltpu.VMEM)
    return pl.pallas_call(
        body, name="small_all_reduce", in_specs=[vm], out_specs=vm, out_shape=jax.ShapeDtypeStruct((r, cols), F32),
        scratch_shapes=[pltpu.VMEM((n_dev, r, cols), F32), pltpu.SemaphoreType.DMA((n_dev - 1,)), pltpu.SemaphoreType.DMA((n_dev - 1,))],
        compiler_params=pltpu.CompilerParams(has_side_effects=True),
    )(v)


_COL_SHARDED = ("w_in", "w_q_up", "w_kv_up", "w_ffn_up")


def _flat_rows(shapes):
    n = sum(int(np.prod(shapes[k])) for k in BIG)
    rows = -(-n // FLAT_COLS)
    return n, -(-rows // (2 * HALF_TILE)) * (2 * HALF_TILE)


def _to_flat(shards, dtype):
    n, rows = _flat_rows({k: shards[k].shape for k in BIG})
    v = jnp.concatenate([shards[k].astype(dtype).reshape(-1) for k in BIG])
    return jnp.pad(v, (0, rows * FLAT_COLS - n)).reshape(rows, FLAT_COLS)


def _from_flat(flat, shapes):
    v = flat.reshape(-1)
    out, o = {}, 0
    for k in BIG:
        sz = int(np.prod(shapes[k]))
        out[k] = v[o:o + sz].reshape(shapes[k])
        o += sz
    return out


def _whole_from_gathered(gathered, shapes):
    n = sum(int(np.prod(shapes[k])) for k in BIG)
    v = gathered.reshape(N_CHIPS, -1)
    out, o = {}, 0
    for k in BIG:
        d, r, c = shapes[k]
        sz = d * r * c
        blk = v[:, o:o + sz].reshape(N_CHIPS, d, r, c)
        o += sz
        if k in _COL_SHARDED:
            out[k] = jnp.transpose(blk, (1, 2, 0, 3)).reshape(d, r, N_CHIPS * c)
        else:
            out[k] = jnp.transpose(blk, (1, 0, 2, 3)).reshape(d, N_CHIPS * r, c)
    return out


def _whole_to_blocks(whole, shapes):
    n, rows = _flat_rows(shapes)
    parts = []
    for k in BIG:
        d, r, c = shapes[k]
        w = whole[k]
        if k in _COL_SHARDED:
            blk = jnp.transpose(w.reshape(d, r, N_CHIPS, c), (2, 0, 1, 3))
        else:
            blk = jnp.transpose(w.reshape(d, N_CHIPS, r, c), (1, 0, 2, 3))
        parts.append(blk.reshape(N_CHIPS, -1))
    v = jnp.concatenate(parts, axis=1)
    return jnp.pad(v, ((0, 0), (0, rows * FLAT_COLS - n))).reshape(N_CHIPS, rows, FLAT_COLS)


def _small_to_rows(d):
    v = jnp.concatenate([d[k].astype(F32).reshape(-1) for k in SMALL])
    rows = -(-v.shape[0] // (8 * LANES)) * 8
    return jnp.pad(v, (0, rows * LANES - v.shape[0])).reshape(rows, LANES)


def _small_from_rows(rows, shapes):
    v = rows.reshape(-1)
    out, o = {}, 0
    for k in SMALL:
        sz = int(np.prod(shapes[k]))
        out[k] = v[o:o + sz].reshape(shapes[k])
        o += sz
    return out


_ARG_NAMES = ("x", "positions", "g_mix_pre", "w_in", "b_forget", "g_q_lora", "w_q_up", "g_kv_lora", "w_kv_up", "g_mix_out", "w_out",
              "g_mix_post", "g_ffn_pre", "w_ffn_up", "w_ffn_down", "g_ffn_post")
_WEIGHTS = _ARG_NAMES[2:]


def kernel(x, positions, g_mix_pre, w_in, b_forget, g_q_lora, w_q_up, g_kv_lora, w_kv_up, g_mix_out, w_out, g_mix_post, g_ffn_pre, w_ffn_up, w_ffn_down, g_ffn_post, loss_target, m_g_mix_pre, m_w_in, m_b_forget, m_g_q_lora, m_w_q_up, m_g_kv_lora, m_w_kv_up, m_g_mix_out, m_w_out, m_g_mix_post, m_g_ffn_pre, m_w_ffn_up, m_w_ffn_down, m_g_ffn_post, v_g_mix_pre, v_w_in, v_b_forget, v_g_q_lora, v_w_q_up, v_g_kv_lora, v_w_kv_up, v_g_mix_out, v_w_out, v_g_mix_post, v_g_ffn_pre, v_w_ffn_up, v_w_ffn_down, v_g_ffn_post):
    w = dict(g_mix_pre=g_mix_pre, w_in=w_in, b_forget=b_forget, g_q_lora=g_q_lora, w_q_up=w_q_up, g_kv_lora=g_kv_lora, w_kv_up=w_kv_up,
             g_mix_out=g_mix_out, w_out=w_out, g_mix_post=g_mix_post, g_ffn_pre=g_ffn_pre, w_ffn_up=w_ffn_up, w_ffn_down=w_ffn_down,
             g_ffn_post=g_ffn_post)
    m = dict(g_mix_pre=m_g_mix_pre, w_in=m_w_in, b_forget=m_b_forget, g_q_lora=m_g_q_lora, w_q_up=m_w_q_up, g_kv_lora=m_g_kv_lora,
             w_kv_up=m_w_kv_up, g_mix_out=m_g_mix_out, w_out=m_w_out, g_mix_post=m_g_mix_post, g_ffn_pre=m_g_ffn_pre,
             w_ffn_up=m_w_ffn_up, w_ffn_down=m_w_ffn_down, g_ffn_post=m_g_ffn_post)
    v = dict(g_mix_pre=v_g_mix_pre, w_in=v_w_in, b_forget=v_b_forget, g_q_lora=v_g_q_lora, w_q_up=v_w_q_up, g_kv_lora=v_g_kv_lora,
             w_kv_up=v_w_kv_up, g_mix_out=v_g_mix_out, w_out=v_w_out, g_mix_post=v_g_mix_post, g_ffn_pre=v_g_ffn_pre,
             w_ffn_up=v_w_ffn_up, w_ffn_down=v_w_ffn_down, g_ffn_post=v_g_ffn_post)
    shard_shapes = {k: w[k].shape for k in BIG}
    small_shapes = {k: w[k].shape for k in SMALL}
    c_idx = lax.axis_index("c").astype(jnp.int32).reshape(1)
    k_idx = (2 * lax.axis_index("x") + lax.axis_index("y")).astype(jnp.int32).reshape(1)

    gathered = _gather_shards(_to_flat(w, BF16))
    whole = _whole_from_gathered(gathered, shard_shapes)
    layers = []
    for l in range(DEPTH):
        wk, wv = _pack_w_kv(whole["w_kv_up"][l])
        layers.append(dict(
            g_mix_pre=g_mix_pre[l], w_in=_pack_w_in(whole["w_in"][l]), b_forget=b_forget[l], g_q_lora=g_q_lora[l], g_kv_lora=g_kv_lora[l],
            wq=_pack_w_q(whole["w_q_up"][l]), wk=wk, wv=wv, g_mix_out=g_mix_out[l], w_out=whole["w_out"][l], g_mix_post=g_mix_post[l],
            g_ffn_pre=g_ffn_pre[l], w_ffn_up=whole["w_ffn_up"][l], w_ffn_down=whole["w_ffn_down"][l], g_ffn_post=g_ffn_post[l]))

    loss_local, dx, grads = _local_step(x[0], positions[0], layers, loss_target[0])
    loss = lax.psum(loss_local, ("x", "y", "c"))

    gw = dict(
        w_in=jnp.stack([_unpack_dw_in(grads[l]["w_in"]) for l in range(DEPTH)]),
        w_q_up=jnp.stack([_unpack_dw_q(grads[l]["wq"]) for l in range(DEPTH)]),
        w_kv_up=jnp.stack([_unpack_dw_kv(grads[l]["wk"], grads[l]["wv"]) for l in range(DEPTH)]),
        w_out=jnp.stack([grads[l]["w_out"] for l in range(DEPTH)]),
        w_ffn_up=jnp.stack([grads[l]["w_ffn_up"] for l in range(DEPTH)]),
        w_ffn_down=jnp.stack([grads[l]["w_ffn_down"] for l in range(DEPTH)]))
    blocks = _whole_to_blocks(gw, shard_shapes)
    rows = blocks.shape[1]
    blocks = blocks.reshape(N_CHIPS, 2, rows // 2, FLAT_COLS)
    pair = _add_own_half(blocks, _send_other_half(blocks), c_idx)
    half = _add_chip_partials(pair, _send_to_owners(pair), k_idx)
    g_big = _from_flat(_share_half(half).reshape(rows, FLAT_COLS), shard_shapes)

    g_small_local = {k: jnp.stack([grads[l][k].reshape(small_shapes[k][1:]) for l in range(DEPTH)]) for k in SMALL}
    g_small = _small_from_rows(_all_reduce_small(_small_to_rows(g_small_local)), small_shapes)

    g_all = {**g_big, **g_small}
    delta, new_m, new_v = {}, {}, {}
    for k in BIG:
        d, r, c = shard_shapes[k]
        two_d = lambda a: a.reshape(d * r, c)
        dk, mk, vk = _adamw(two_d(w[k]), two_d(g_all[k]), two_d(m[k]), two_d(v[k]), name="adamw_" + k)
        delta[k], new_m[k], new_v[k] = dk.reshape(d, r, c), mk.reshape(d, r, c), vk.reshape(d, r, c)
    ds, ms, vs = _adamw(_small_to_rows(w), _small_to_rows(g_small), _small_to_rows(m), _small_to_rows(v), name="adamw_small")
    delta.update(_small_from_rows(ds, small_shapes))
    new_m.update(_small_from_rows(ms, small_shapes))
    new_v.update(_small_from_rows(vs, small_shapes))

    grad_x = dx.reshape(x.shape)
    return (loss, grad_x, *[g_all[k] for k in _WEIGHTS], *[delta[k] for k in _WEIGHTS], *[new_m[k] for k in _WEIGHTS],
            *[new_v[k] for k in _WEIGHTS])
```

```python
import functools
import math

import numpy as np
import jax
import jax.numpy as jnp
from jax import lax
from jax.experimental import pallas as pl
from jax.experimental.pallas import tpu as pltpu

F32 = jnp.float32
BF16 = jnp.bfloat16
MESH = pl.DeviceIdType.MESH

D_MODEL = 1024
DEPTH = 2
CHUNK = 64
GROUP = 256
HEAD = 64
N_HEADS = 4
Q_RANK = 256
KV_RANK = 128
ROPE_DIM = 32
D_FF = 4096
D_IN = 2980
D_INP = 3072
ROPE_BASE = 10000.0
EPS = 1e-6
LANES = 128
TQ = 128
NEG = -1e30

ADAM_LR, ADAM_B1, ADAM_B2, ADAM_EPS, ADAM_WD, ADAM_STEP = 0.001, 0.9, 0.999, 1e-08, 0.01, 10

OFF_FQ, OFF_FK, OFF_FV, OFF_CQ = 0, 2, 4, 6
OFF_RQ, OFF_RK, OFF_RV, OFF_RG = 8, 10, 12, 14
OFF_SQ, OFF_SK, OFF_SV = 16, 18, 20
OFF_CKV, OFF_MISC = 22, 23
FF_LANE, KR_LANE = 0, 64

VMEM_LIMIT = 56 * 1024 * 1024


def _tile(dim, pref):
    return pref if dim % pref == 0 else dim


def _cparams(sem, vmem=None):
    return pltpu.CompilerParams(dimension_semantics=sem, vmem_limit_bytes=vmem or VMEM_LIMIT)


def _dot(a, b):
    return jnp.dot(a, b, preferred_element_type=F32)


def _dot_nt(a, b):
    return lax.dot_general(a, b, (((1,), (1,)), ((), ())), preferred_element_type=F32)


def _dot_tn(a, b):
    return lax.dot_general(a, b, (((0,), (0,)), ((), ())), preferred_element_type=F32)


def _dot_exact(a, b):
    return jnp.dot(a, b, precision=lax.Precision.HIGHEST, preferred_element_type=F32)


def _matmul(a, b, *, name, ta=False, tb=False, out_dtype=F32, tm=512, tn=512, tk=1024,
            relu2=False, relu_of=None, also_bf16=False):
    if ta:
        kdim, m = a.shape
    else:
        m, kdim = a.shape
    n = b.shape[0] if tb else b.shape[1]
    tm, tn, tk = _tile(m, tm), _tile(n, tn), _tile(kdim, tk)
    nk = kdim // tk
    a_spec = pl.BlockSpec((tk, tm), lambda i, j, k: (k, i)) if ta else pl.BlockSpec((tm, tk), lambda i, j, k: (i, k))
    b_spec = pl.BlockSpec((tn, tk), lambda i, j, k: (j, k)) if tb else pl.BlockSpec((tk, tn), lambda i, j, k: (k, j))
    o_spec = pl.BlockSpec((tm, tn), lambda i, j, k: (i, j))
    two = relu2 or also_bf16

    def body(*refs):
        refs = list(refs)
        a_ref, b_ref = refs[0], refs[1]
        e_ref = refs[2] if relu_of is not None else None
        pos = 3 if relu_of is not None else 2
        o_ref = refs[pos]
        o2_ref = refs[pos + 1] if two else None
        acc_ref = refs[-1]
        k = pl.program_id(2)
        av = a_ref[...].astype(BF16)
        bv = b_ref[...].astype(BF16)
        if ta:
            part = _dot_tn(av, bv)
        elif tb:
            part = _dot_nt(av, bv)
        else:
            part = _dot(av, bv)

        @pl.when(k == 0)
        def _():
            acc_ref[...] = part

        @pl.when(k > 0)
        def _():
            acc_ref[...] += part

        @pl.when(k == nk - 1)
        def _():
            r = acc_ref[...]
            if relu_of is not None:
                r = r * (2.0 * jnp.maximum(e_ref[...], 0.0))
            o_ref[...] = r.astype(o_ref.dtype)
            if relu2:
                o2_ref[...] = jnp.square(jnp.maximum(r, 0.0)).astype(BF16)
            elif also_bf16:
                o2_ref[...] = r.astype(BF16)

    in_specs = [a_spec, b_spec]
    args = [a, b]
    if relu_of is not None:
        in_specs.append(o_spec)
        args.append(relu_of)
    out_shape = [jax.ShapeDtypeStruct((m, n), out_dtype)]
    out_specs = [o_spec]
    if two:
        out_shape.append(jax.ShapeDtypeStruct((m, n), BF16))
        out_specs.append(o_spec)
    res = pl.pallas_call(
        body, name=name, grid=(m // tm, n // tn, nk), in_specs=in_specs, out_specs=out_specs, out_shape=out_shape,
        scratch_shapes=[pltpu.VMEM((tm, tn), F32)],
        compiler_params=_cparams(("parallel", "parallel", "arbitrary")),
    )(*args)
    return res if two else res[0]


def _rms(x, g):
    r = lax.rsqrt(jnp.mean(x * x, axis=-1, keepdims=True) + EPS)
    return x * r * g


def _rms_bwd(x, g, dy):
    r = lax.rsqrt(jnp.mean(x * x, axis=-1, keepdims=True) + EPS)
    xh = x * r
    gdy = dy * g
    dx = r * (gdy - xh * jnp.mean(xh * gdy, axis=-1, keepdims=True))
    return dx, xh * dy


def _norm_fwd(x, g, *, name, resid=None, out_dtype=BF16):
    s, d = x.shape
    tr = _tile(s, 256)
    row = pl.BlockSpec((tr, d), lambda i: (i, 0))
    gsp = pl.BlockSpec((1, d), lambda i: (0, 0))

    def body(*refs):
        if resid is None:
            x_ref, g_ref, o_ref = refs
            o_ref[...] = _rms(x_ref[...], g_ref[...]).astype(o_ref.dtype)
        else:
            x_ref, g_ref, r_ref, o_ref = refs
            o_ref[...] = (r_ref[...] + _rms(x_ref[...], g_ref[...])).astype(o_ref.dtype)

    args = [x, g.reshape(1, d)] + ([] if resid is None else [resid])
    return pl.pallas_call(
        body, name=name, grid=(s // tr,), in_specs=[row, gsp] + ([] if resid is None else [row]),
        out_specs=row, out_shape=jax.ShapeDtypeStruct((s, d), out_dtype), compiler_params=_cparams(("parallel",)),
    )(*args)


def _norm_bwd(x, g, dy, *, name, add=None, out_dtype=F32):
    s, d = x.shape
    tr = _tile(s, 256)
    row = pl.BlockSpec((tr, d), lambda i: (i, 0))
    gsp = pl.BlockSpec((1, d), lambda i: (0, 0))

    def body(*refs):
        if add is None:
            x_ref, g_ref, dy_ref, dx_ref, dg_ref = refs
        else:
            x_ref, g_ref, dy_ref, add_ref, dx_ref, dg_ref = refs
        dx, gterm = _rms_bwd(x_ref[...], g_ref[...], dy_ref[...].astype(F32))
        if add is not None:
            dx = dx + add_ref[...]
        dx_ref[...] = dx.astype(dx_ref.dtype)

        @pl.when(pl.program_id(0) == 0)
        def _():
            dg_ref[...] = jnp.zeros_like(dg_ref)

        dg_ref[...] += jnp.sum(gterm, axis=0, keepdims=True)

    args = [x, g.reshape(1, d), dy] + ([] if add is None else [add])
    return pl.pallas_call(
        body, name=name, grid=(s // tr,), in_specs=[row, gsp, row] + ([] if add is None else [row]),
        out_specs=[row, gsp], out_shape=[jax.ShapeDtypeStruct((s, d), out_dtype), jax.ShapeDtypeStruct((1, d), F32)],
        compiler_params=_cparams(("arbitrary",)),
    )(*args)


def _loss_head(y, target):
    s, d = y.shape
    tr = _tile(s, 256)
    row = pl.BlockSpec((tr, d), lambda i: (i, 0))
    lsp = pl.BlockSpec((1, LANES), lambda i: (0, 0))

    def body(y_ref, t_ref, l_ref, dy_ref):
        e = y_ref[...] - t_ref[...]
        dy_ref[...] = e * (1.0 / d)

        @pl.when(pl.program_id(0) == 0)
        def _():
            l_ref[...] = jnp.zeros_like(l_ref)

        part = 0.5 * jnp.sum(jnp.mean(e * e, axis=-1, keepdims=True), axis=0, keepdims=True)
        l_ref[...] += jnp.broadcast_to(part, (1, LANES))

    return pl.pallas_call(
        body, name="loss_head", grid=(s // tr,), in_specs=[row, row], out_specs=[lsp, row],
        out_shape=[jax.ShapeDtypeStruct((1, LANES), F32), jax.ShapeDtypeStruct((s, d), F32)],
        compiler_params=_cparams(("arbitrary",)),
    )(y, target)


def _rope_tables(pos_col):
    s = pos_col.shape[0]
    tr = _tile(s, 512)
    f_mla = ROPE_BASE ** (-jnp.arange(ROPE_DIM // 2, dtype=F32) / (ROPE_DIM // 2))
    f_ret = ROPE_BASE ** (-jnp.arange(HEAD // 2, dtype=F32) / (HEAD // 2))
    fm = jnp.concatenate([jnp.zeros((64,), F32), f_mla, f_mla, jnp.zeros((32,), F32)]).reshape(1, LANES)
    fr = jnp.tile(jnp.concatenate([f_ret, f_ret]), 4).reshape(1, 2 * LANES)

    def body(p_ref, fm_ref, fr_ref, cm_ref, sm_ref, cr_ref, sr_ref):
        p = p_ref[...].astype(F32)
        am = p * fm_ref[...]
        ar = p * fr_ref[...]
        cm_ref[...] = jnp.cos(am)
        sm_ref[...] = jnp.sin(am)
        cr_ref[...] = jnp.cos(ar)
        sr_ref[...] = jnp.sin(ar)

    return pl.pallas_call(
        body, name="rope_tables", grid=(s // tr,),
        in_specs=[pl.BlockSpec((tr, 1), lambda i: (i, 0)), pl.BlockSpec((1, LANES), lambda i: (0, 0)),
                  pl.BlockSpec((1, 2 * LANES), lambda i: (0, 0))],
        out_specs=[pl.BlockSpec((tr, LANES), lambda i: (i, 0))] * 2 + [pl.BlockSpec((tr, 2 * LANES), lambda i: (i, 0))] * 2,
        out_shape=[jax.ShapeDtypeStruct((s, LANES), F32)] * 2 + [jax.ShapeDtypeStruct((s, 2 * LANES), F32)] * 2,
        compiler_params=_cparams(("parallel",)),
    )(pos_col, fm, fr)


def _lane(shape):
    return lax.broadcasted_iota(jnp.int32, shape, len(shape) - 1)


def _rot_mla(z):
    l = _lane(z.shape) % LANES
    n = z.shape[-1]
    return jnp.where(l < 80, -pltpu.roll(z, n - 16, 1), pltpu.roll(z, 16, 1))


def _rot_mla_t(y):
    l = _lane(y.shape) % LANES
    n = y.shape[-1]
    return jnp.where((l >= 64) & (l < 80), pltpu.roll(y, n - 16, 1),
                     jnp.where((l >= 80) & (l < 96), -pltpu.roll(y, 16, 1), 0.0))


def _rot_ret(z):
    l = _lane(z.shape) % HEAD
    n = z.shape[-1]
    return jnp.where(l < 32, -pltpu.roll(z, n - 32, 1), pltpu.roll(z, 32, 1))


def _rot_ret_t(y):
    l = _lane(y.shape) % HEAD
    n = y.shape[-1]
    return jnp.where(l < 32, pltpu.roll(y, n - 32, 1), -pltpu.roll(y, 32, 1))


def _log_sigmoid(x):
    return jnp.minimum(x, 0.0) - jnp.log1p(jnp.exp(-jnp.abs(x)))


def _fox_cum(proj, bias_row):
    s = proj.shape[0]
    nb = s // TQ

    def body(x_ref, b_ref, cc_ref, cr_ref, carry_ref):
        @pl.when(pl.program_id(0) == 0)
        def _():
            carry_ref[...] = jnp.zeros_like(carry_ref)

        ls = _log_sigmoid(x_ref[...] + b_ref[...])
        r = lax.broadcasted_iota(jnp.int32, (TQ, TQ), 0)
        c = lax.broadcasted_iota(jnp.int32, (TQ, TQ), 1)
        tri = (c <= r).astype(F32)
        cum = _dot_exact(tri, ls) + carry_ref[...]
        carry_ref[...] = cum[TQ - 1:TQ, :]
        cc_ref[...] = cum
        cr_ref[...] = cum.T[0:8, :]

    return pl.pallas_call(
        body, name="fox_cum", grid=(nb,),
        in_specs=[pl.BlockSpec((TQ, LANES), lambda i: (i, OFF_MISC)), pl.BlockSpec((1, LANES), lambda i: (0, 0))],
        out_specs=[pl.BlockSpec((TQ, LANES), lambda i: (i, 0)), pl.BlockSpec((8, TQ), lambda i: (0, i))],
        out_shape=[jax.ShapeDtypeStruct((s, LANES), F32), jax.ShapeDtypeStruct((8, s), F32)],
        scratch_shapes=[pltpu.VMEM((1, LANES), F32)],
        compiler_params=_cparams(("arbitrary",)),
    )(proj, bias_row)


def _fox_gate_bwd(dck, drs, proj, bias_row, dkr):
    s = proj.shape[0]
    nb = s // TQ

    def body(d_ref, r_ref, x_ref, b_ref, k_ref, o_ref, db_ref, carry_ref):
        @pl.when(pl.program_id(0) == 0)
        def _():
            carry_ref[...] = jnp.zeros_like(carry_ref)
            db_ref[...] = jnp.zeros_like(db_ref)

        rows = jnp.concatenate([d_ref[0], d_ref[1], jnp.zeros((TQ - 16, TQ), F32)], axis=0)
        t = rows.T
        l = _lane((TQ, LANES))
        r0, r1 = r_ref[0], r_ref[1]
        rsum = jnp.where(l == 0, r0[:, 0:1], jnp.where(l == 1, r0[:, HEAD:HEAD + 1],
                         jnp.where(l == 2, r1[:, 0:1], jnp.where(l == 3, r1[:, HEAD:HEAD + 1], 0.0))))
        dcum = rsum - jnp.where(l < 2, t, pltpu.roll(t, LANES - 6, 1))
        r = lax.broadcasted_iota(jnp.int32, (TQ, TQ), 0)
        c = lax.broadcasted_iota(jnp.int32, (TQ, TQ), 1)
        triu = (c >= r).astype(F32)
        rc = _dot_exact(triu, dcum) + carry_ref[...]
        carry_ref[...] = rc[0:1, :]
        f = x_ref[...] + b_ref[...]
        sig_neg = 1.0 / (1.0 + jnp.exp(f))
        df = jnp.where(l < N_HEADS, rc * sig_neg, 0.0)
        db_ref[...] += jnp.sum(df, axis=0, keepdims=True)
        o_ref[...] = (df + k_ref[...]).astype(o_ref.dtype)

    rev = lambda i: nb - 1 - i
    return pl.pallas_call(
        body, name="fox_gate_bwd", grid=(nb,),
        in_specs=[pl.BlockSpec((2, 8, TQ), lambda i: (0, 0, rev(i))), pl.BlockSpec((2, TQ, LANES), lambda i: (0, rev(i), 0)),
                  pl.BlockSpec((TQ, LANES), lambda i: (rev(i), OFF_MISC)),
                  pl.BlockSpec((1, LANES), lambda i: (0, 0)), pl.BlockSpec((TQ, LANES), lambda i: (rev(i), 0))],
        out_specs=[pl.BlockSpec((TQ, LANES), lambda i: (rev(i), 0)), pl.BlockSpec((1, LANES), lambda i: (0, 0))],
        out_shape=[jax.ShapeDtypeStruct((s, LANES), BF16), jax.ShapeDtypeStruct((1, LANES), F32)],
        scratch_shapes=[pltpu.VMEM((1, LANES), F32)],
        compiler_params=_cparams(("arbitrary",)),
    )(dck, drs, proj, bias_row, dkr)


def _mla_prep(proj, cos_m, sin_m, g_q, g_kv, wq, wk, wv):
    s = proj.shape[0]
    tr = _tile(s, 256)

    def body(cq_ref, ckv_ref, misc_ref, cos_ref, sin_ref, gq_ref, gkv_ref, wq_ref, wk_ref, wv_ref,
             q_ref, k_ref, v_ref, cqn_ref, ckvn_ref):
        cos4 = jnp.tile(cos_ref[...], (1, 4))
        sin4 = jnp.tile(sin_ref[...], (1, 4))
        cqn = _rms(cq_ref[...], gq_ref[...]).astype(BF16)
        ckvn = _rms(ckv_ref[...], gkv_ref[...]).astype(BF16)
        cqn_ref[...] = cqn
        ckvn_ref[...] = ckvn
        zq = _dot(cqn, wq_ref[...])
        q_ref[...] = (zq * cos4 + _rot_mla(zq) * sin4).astype(BF16)
        l = _lane((tr, LANES))
        kr = jnp.where((l >= KR_LANE) & (l < KR_LANE + ROPE_DIM), misc_ref[...], 0.0)
        zk = _dot(ckvn, wk_ref[...]) + jnp.tile(kr, (1, 4))
        k_ref[...] = (zk * cos4 + _rot_mla(zk) * sin4).astype(BF16)
        v_ref[...] = _dot(ckvn, wv_ref[...]).astype(BF16)

    full = lambda a: pl.BlockSpec(a.shape, lambda i: (0, 0))
    rowb = lambda w: pl.BlockSpec((tr, w), lambda i: (i, 0))
    gq2, gkv2 = g_q.reshape(1, Q_RANK), g_kv.reshape(1, KV_RANK)
    return pl.pallas_call(
        body, name="mla_prep", grid=(s // tr,),
        in_specs=[pl.BlockSpec((tr, 256), lambda i: (i, OFF_CQ // 2)), pl.BlockSpec((tr, LANES), lambda i: (i, OFF_CKV)),
                  pl.BlockSpec((tr, LANES), lambda i: (i, OFF_MISC)), rowb(LANES), rowb(LANES),
                  full(gq2), full(gkv2), full(wq), full(wk), full(wv)],
        out_specs=[rowb(512), rowb(512), rowb(512), rowb(256), rowb(128)],
        out_shape=[jax.ShapeDtypeStruct((s, 512), BF16), jax.ShapeDtypeStruct((s, 512), BF16), jax.ShapeDtypeStruct((s, 512), BF16),
                   jax.ShapeDtypeStruct((s, 256), BF16), jax.ShapeDtypeStruct((s, 128), BF16)],
        compiler_params=_cparams(("parallel",)),
    )(proj, proj, proj, cos_m, sin_m, gq2, gkv2, wq, wk, wv)


def _mla_prep_bwd(dq, dk, dv, proj, cqn, ckvn, cos_m, sin_m, g_q, g_kv, wq, wk, wv):
    s = proj.shape[0]
    tr = _tile(s, 256)

    def body(dq_ref, dk_ref, dv_ref, cq_ref, ckv_ref, cqn_ref, ckvn_ref, cos_ref, sin_ref, gq_ref, gkv_ref,
             wq_ref, wk_ref, wv_ref, dcq_ref, dckv_ref, dkr_ref, dwq_ref, dwk_ref, dwv_ref, dgq_ref, dgkv_ref):
        @pl.when(pl.program_id(0) == 0)
        def _():
            for r in (dwq_ref, dwk_ref, dwv_ref, dgq_ref, dgkv_ref):
                r[...] = jnp.zeros_like(r)

        cos4 = jnp.tile(cos_ref[...], (1, 4))
        sin4 = jnp.tile(sin_ref[...], (1, 4))
        dqv = dq_ref[...]
        dzq = dqv * cos4 + _rot_mla_t(dqv * sin4)
        dkv_ = dk_ref[...]
        dzk = dkv_ * cos4 + _rot_mla_t(dkv_ * sin4)
        l = _lane((tr, LANES))
        in_rope = (l >= KR_LANE) & (l < KR_LANE + ROPE_DIM)
        dkr = dzk[:, 0:128] + dzk[:, 128:256] + dzk[:, 256:384] + dzk[:, 384:512]
        dkr_ref[...] = jnp.where(in_rope, dkr, 0.0)
        dzq_b = dzq.astype(BF16)
        dzk_b = dzk.astype(BF16)
        dv_b = dv_ref[...].astype(BF16)
        dcqn = _dot_nt(dzq_b, wq_ref[...])
        dckvn = _dot_nt(dzk_b, wk_ref[...]) + _dot_nt(dv_b, wv_ref[...])
        dwq_ref[...] += _dot_tn(cqn_ref[...], dzq_b)
        dwk_ref[...] += _dot_tn(ckvn_ref[...], dzk_b)
        dwv_ref[...] += _dot_tn(ckvn_ref[...], dv_b)
        dcq, gq_term = _rms_bwd(cq_ref[...], gq_ref[...], dcqn)
        dckv, gkv_term = _rms_bwd(ckv_ref[...], gkv_ref[...], dckvn)
        dcq_ref[...] = dcq.astype(BF16)
        dckv_ref[...] = dckv.astype(BF16)
        dgq_ref[...] += jnp.sum(gq_term, axis=0, keepdims=True)
        dgkv_ref[...] += jnp.sum(gkv_term, axis=0, keepdims=True)

    full = lambda shp: pl.BlockSpec(shp, lambda i: (0, 0))
    rowb = lambda w: pl.BlockSpec((tr, w), lambda i: (i, 0))
    gq2, gkv2 = g_q.reshape(1, Q_RANK), g_kv.reshape(1, KV_RANK)
    return pl.pallas_call(
        body, name="mla_prep_bwd", grid=(s // tr,),
        in_specs=[rowb(512), rowb(512), rowb(512),
                  pl.BlockSpec((tr, 256), lambda i: (i, OFF_CQ // 2)), pl.BlockSpec((tr, LANES), lambda i: (i, OFF_CKV)),
                  rowb(256), rowb(128), rowb(LANES), rowb(LANES), full((1, Q_RANK)), full((1, KV_RANK)),
                  full(wq.shape), full(wk.shape), full(wv.shape)],
        out_specs=[rowb(256), rowb(128), rowb(128), full(wq.shape), full(wk.shape), full(wv.shape),
                   full((1, Q_RANK)), full((1, KV_RANK))],
        out_shape=[jax.ShapeDtypeStruct((s, 256), BF16), jax.ShapeDtypeStruct((s, 128), BF16), jax.ShapeDtypeStruct((s, 128), F32),
                   jax.ShapeDtypeStruct(wq.shape, F32), jax.ShapeDtypeStruct(wk.shape, F32), jax.ShapeDtypeStruct(wv.shape, F32),
                   jax.ShapeDtypeStruct((1, Q_RANK), F32), jax.ShapeDtypeStruct((1, KV_RANK), F32)],
        compiler_params=_cparams(("arbitrary",)),
    )(dq, dk, dv, proj, proj, cqn, ckvn, cos_m, sin_m, gq2, gkv2, wq, wk, wv)


def _ret_prep(proj, cos_r, sin_r):
    s = proj.shape[0]
    tr = _tile(s, 256)

    def body(q_ref, k_ref, cos_ref, sin_ref, qo_ref, ko_ref):
        cos, sin = cos_ref[...], sin_ref[...]
        q, k = q_ref[...], k_ref[...]
        qo_ref[...] = (q * cos + _rot_ret(q) * sin).astype(BF16)
        ko_ref[...] = ((k * cos + _rot_ret(k) * sin) * (HEAD ** -0.5)).astype(BF16)

    rowb = pl.BlockSpec((tr, 256), lambda i: (i, 0))
    return pl.pallas_call(
        body, name="ret_prep", grid=(s // tr,),
        in_specs=[pl.BlockSpec((tr, 256), lambda i: (i, OFF_RQ // 2)), pl.BlockSpec((tr, 256), lambda i: (i, OFF_RK // 2)), rowb, rowb],
        out_specs=[rowb, rowb], out_shape=[jax.ShapeDtypeStruct((s, 256), BF16)] * 2,
        compiler_params=_cparams(("parallel",)),
    )(proj, proj, cos_r, sin_r)


def _ret_prep_bwd(dq, dk, cos_r, sin_r):
    s = dq.shape[0]
    tr = _tile(s, 256)

    def body(dq_ref, dk_ref, cos_ref, sin_ref, qo_ref, ko_ref):
        cos, sin = cos_ref[...], sin_ref[...]
        q, k = dq_ref[...], dk_ref[...] * (HEAD ** -0.5)
        qo_ref[...] = (q * cos + _rot_ret_t(q * sin)).astype(BF16)
        ko_ref[...] = (k * cos + _rot_ret_t(k * sin)).astype(BF16)

    rowb = pl.BlockSpec((tr, 256), lambda i: (i, 0))
    return pl.pallas_call(
        body, name="ret_prep_bwd", grid=(s // tr,), in_specs=[rowb] * 4, out_specs=[rowb, rowb],
        out_shape=[jax.ShapeDtypeStruct((s, 256), BF16)] * 2, compiler_params=_cparams(("parallel",)),
    )(dq, dk, cos_r, sin_r)


_LOG_GAMMA = [float(np.log1p(-np.float32(2.0) ** np.float32(-5.0 - h))) for h in range(N_HEADS)]
_MLA_SCALE = float((HEAD + ROPE_DIM) ** -0.5)
_QK_SCALE = float(HEAD ** -0.5)
KEY_BLOCKS = 4


def _split2(x):
    h = x.astype(BF16)
    return h, (x - h.astype(F32)).astype(BF16)


def _dot2(x, u):
    h, lo = _split2(x)
    return _dot(h, u) + _dot(lo, u)


def _head_pick(block, head, axis):
    idx = lax.broadcasted_iota(jnp.int32, block.shape, axis)
    return jnp.sum(jnp.where(idx == head, block, 0.0), axis=axis, keepdims=True)


def _log_gamma_of(head):
    lg = jnp.float32(_LOG_GAMMA[3])
    for h in (2, 1, 0):
        lg = jnp.where(head == h, jnp.float32(_LOG_GAMMA[h]), lg)
    return lg


def _mixer_specs(mode, s, q_off, k_off, v_off):
    nhb = 1 if mode == "mla" else 2
    nsub = KEY_BLOCKS if (s // TQ) % KEY_BLOCKS == 0 else 1
    q_spec = pl.BlockSpec((TQ, LANES), lambda p, i: (i, q_off + p))
    k_spec = pl.BlockSpec((s, LANES), lambda p, i: (0, k_off + p))
    v_spec = pl.BlockSpec((s, LANES), lambda p, i: (0, v_off + p))
    return nhb, N_HEADS // nhb, nsub, q_spec, k_spec, v_spec


def _mixer_geometry(mode, i, nsub):
    w = TQ * nsub
    row = lax.broadcasted_iota(jnp.int32, (TQ, w), 0)
    col = lax.broadcasted_iota(jnp.int32, (TQ, w), 1)
    nfull = i // nsub
    dist = col - row
    if mode in ("fox", "sb"):
        rel = dist
    else:
        rel = col - (row | (CHUNK - 1))

    def visible(c):
        off = c * w - i * TQ
        return (rel + off) < 0 if mode == "sb" else (rel + off) <= 0

    return nfull, dist, visible


def _mixer_fwd(mode, qa, q_off, ka, k_off, va, v_off, *, cum_col=None, cum_row=None):
    s = qa.shape[0]
    nq = s // TQ
    nhb, nblk, nsub, q_spec, k_spec, v_spec = _mixer_specs(mode, s, q_off, k_off, v_off)
    w = TQ * nsub
    softmax = mode in ("fox", "mla")
    has_stat = mode != "ret"

    def body(*refs):
        refs = list(refs)
        q_ref, k_ref, v_ref = refs[:3]
        refs = refs[3:]
        if mode == "fox":
            cc_ref, cr_ref = refs[:2]
            refs = refs[2:]
        o_ref = refs[0]
        st_ref = refs[1] if has_stat else None
        p = pl.program_id(0)
        i = pl.program_id(1)
        nfull, dist, visible = _mixer_geometry(mode, i, nsub)
        lane = _lane((1, LANES))
        heads = [nhb * p + hh for hh in range(nhb)]
        if nhb == 1:
            qs = [q_ref[...]]
        else:
            qf = q_ref[...].astype(F32)
            qs = [jnp.where((lane // HEAD) == hh, qf, 0.0).astype(BF16) for hh in range(nhb)]
        if mode == "fox":
            cqs = [_head_pick(cc_ref[...], h, 1) for h in heads]
        if mode == "ret":
            lgs = [_log_gamma_of(h) for h in heads]
            distf = dist.astype(F32)
        if mode == "sb":
            r1 = lax.broadcasted_iota(jnp.int32, (TQ, TQ), 0)
            c1 = lax.broadcasted_iota(jnp.int32, (TQ, TQ), 1)
            u_after = (r1 > c1).astype(BF16)

        def head_step(hh, c, js, kj, vj, carry, last):
            sc = _dot_nt(qs[hh], kj)
            if softmax:
                m, l, acc = carry
                if mode == "fox":
                    ck = _head_pick(cr_ref[:, js], heads[hh], 0)
                    sc = sc * _QK_SCALE + (cqs[hh] - ck)
                else:
                    sc = sc * _MLA_SCALE
                if last:
                    sc = jnp.where(visible(c), sc, NEG)
                m_new = jnp.maximum(m, jnp.max(sc, axis=-1, keepdims=True))
                alpha = jnp.exp(m - m_new)
                pr = jnp.exp(sc - m_new)
                l = alpha * l + jnp.sum(pr, axis=-1, keepdims=True)
                acc = alpha * acc + _dot(pr.astype(BF16), vj)
                return m_new, l, acc
            if mode == "ret":
                off = (i * TQ - c * w).astype(F32)
                if last:
                    dec = jnp.where(visible(c), jnp.exp(lgs[hh] * jnp.abs(off - distf)), 0.0)
                else:
                    dec = jnp.exp(lgs[hh] * (off - distf))
                return carry + _dot((sc * dec).astype(BF16), vj)
            run, acc = carry
            z = sc * _QK_SCALE
            lp = jnp.log1p(jnp.exp(-jnp.abs(z)))
            log_stay = jnp.minimum(-z, 0.0) - lp
            if last:
                vis = visible(c)
                log_stay = jnp.where(vis, log_stay, 0.0)
            parts = [None] * nsub
            for b in reversed(range(nsub)):
                ls_b = log_stay[:, b * TQ:(b + 1) * TQ]
                parts[b] = _dot2(ls_b, u_after) + run
                run = run + jnp.sum(ls_b, axis=-1, keepdims=True)
            later = parts[0] if nsub == 1 else jnp.concatenate(parts, axis=1)
            wgt = jnp.exp(jnp.minimum(z, 0.0) - lp + later)
            if last:
                wgt = jnp.where(vis, wgt, 0.0)
            return run, acc + _dot(wgt.astype(BF16), vj)

        def step(c, carries, last):
            js = pl.ds(pl.multiple_of(c * w, w), w)
            kj, vj = k_ref[js, :], v_ref[js, :]
            return tuple(head_step(hh, c, js, kj, vj, carries[hh], last) for hh in range(nhb))

        zero_acc = jnp.zeros((TQ, LANES), F32)
        zero1 = jnp.zeros((TQ, 1), F32)
        if softmax:
            init = tuple((jnp.full((TQ, 1), NEG, F32), zero1, zero_acc) for _ in range(nhb))
        elif mode == "ret":
            init = tuple(zero_acc for _ in range(nhb))
        else:
            init = tuple((zero1, zero_acc) for _ in range(nhb))
        if mode == "sb":
            carries = step(nfull, init, True)
            carries = lax.fori_loop(0, nfull, lambda t, cs: step(nfull - 1 - t, cs, False), carries)
        else:
            carries = lax.fori_loop(0, nfull, lambda c, cs: step(c, cs, False), init)
            carries = step(nfull, carries, True)
        if softmax:
            outs = [acc / l for (m, l, acc) in carries]
            stats = [m + jnp.log(l) for (m, l, acc) in carries]
        elif mode == "ret":
            outs, stats = list(carries), None
        else:
            outs, stats = [acc for (run, acc) in carries], [run for (run, acc) in carries]
        hm0 = (lane // HEAD) == 0
        pick = (lambda a: jnp.broadcast_to(a[0], (TQ, LANES))) if nhb == 1 else (lambda a: jnp.where(hm0, a[0], a[1]))
        o_ref[...] = pick(outs)
        if has_stat:
            st_ref[0] = pick(stats)

    in_specs = [q_spec, k_spec, v_spec]
    args = [qa, ka, va]
    if mode == "fox":
        in_specs += [pl.BlockSpec((TQ, LANES), lambda p, i: (i, 0)), pl.BlockSpec((8, s), lambda p, i: (0, 0))]
        args += [cum_col, cum_row]
    out_specs = [pl.BlockSpec((TQ, LANES), lambda p, i: (i, p))]
    out_shape = [jax.ShapeDtypeStruct((s, nblk * LANES), F32)]
    if has_stat:
        out_specs.append(pl.BlockSpec((1, TQ, LANES), lambda p, i: (p, i, 0)))
        out_shape.append(jax.ShapeDtypeStruct((nblk, s, LANES), F32))
    res = pl.pallas_call(
        body, name=mode + "_fwd", grid=(nblk, nq), in_specs=in_specs, out_specs=out_specs, out_shape=out_shape,
        compiler_params=_cparams(("parallel", "parallel")),
    )(*args)
    return res if has_stat else (res[0], None)


def _mixer_bwd(mode, qa, q_off, ka, k_off, va, v_off, o, do, *, stat=None, cum_col=None, cum_row=None):
    s = qa.shape[0]
    nq = s // TQ
    nhb, nblk, nsub, q_spec, k_spec, v_spec = _mixer_specs(mode, s, q_off, k_off, v_off)
    w = TQ * nsub
    softmax = mode in ("fox", "mla")
    has_stat = mode != "ret"

    def body(*refs):
        refs = list(refs)
        q_ref, k_ref, v_ref, o_ref, do_ref = refs[:5]
        refs = refs[5:]
        if has_stat:
            st_ref = refs[0]
            refs = refs[1:]
        if mode == "fox":
            cc_ref, cr_ref = refs[:2]
            refs = refs[2:]
        dq_ref, dk_ref, dv_ref = refs[:3]
        dck_ref, drs_ref = refs[3:5] if mode == "fox" else (None, None)
        p = pl.program_id(0)
        i = pl.program_id(1)

        @pl.when(i == 0)
        def _():
            dk_ref[...] = jnp.zeros_like(dk_ref)
            dv_ref[...] = jnp.zeros_like(dv_ref)
            if mode == "fox":
                dck_ref[...] = jnp.zeros_like(dck_ref)

        nfull, dist, visible = _mixer_geometry(mode, i, nsub)
        lane = _lane((1, LANES))
        heads = [nhb * p + hh for hh in range(nhb)]
        dov = do_ref[...]
        if nhb == 1:
            qs = [q_ref[...]]
            dos = [dov.astype(BF16)]
            deltas = [jnp.sum(dov * o_ref[...], axis=-1, keepdims=True)]
        else:
            qf = q_ref[...].astype(F32)
            prod = dov * o_ref[...]
            hms = [(lane // HEAD) == hh for hh in range(nhb)]
            qs = [jnp.where(hm, qf, 0.0).astype(BF16) for hm in hms]
            dos = [jnp.where(hm, dov, 0.0).astype(BF16) for hm in hms]
            deltas = [jnp.sum(jnp.where(hm, prod, 0.0), axis=-1, keepdims=True) for hm in hms]
        if has_stat:
            st = st_ref[0]
            stats = [st[:, hh * HEAD:hh * HEAD + 1] for hh in range(nhb)]
        if mode == "fox":
            cqs = [_head_pick(cc_ref[...], h, 1) for h in heads]
        if mode == "ret":
            lgs = [_log_gamma_of(h) for h in heads]
            distf = dist.astype(F32)
        if mode == "sb":
            r1 = lax.broadcasted_iota(jnp.int32, (TQ, TQ), 0)
            c1 = lax.broadcasted_iota(jnp.int32, (TQ, TQ), 1)
            u_upto = (r1 <= c1).astype(BF16)
            u_before = (r1 < c1).astype(BF16)

        def emit(hh, js, kj, ds_b, pr_b, dq):
            dk_ref[js, :] += _dot_tn(ds_b, qs[hh])
            dv_ref[js, :] += _dot_tn(pr_b, dos[hh])
            return dq + _dot(ds_b, kj)

        def head_step(hh, c, js, kj, vj, carry, last):
            sc = _dot_nt(qs[hh], kj)
            dp = _dot_nt(dos[hh], vj)
            if softmax:
                dq, rsum = carry
                scale = _QK_SCALE if mode == "fox" else _MLA_SCALE
                if mode == "fox":
                    ck = _head_pick(cr_ref[:, js], heads[hh], 0)
                    sc = sc * scale + (cqs[hh] - ck)
                else:
                    sc = sc * scale
                if last:
                    sc = jnp.where(visible(c), sc, NEG)
                pr = jnp.exp(sc - stats[hh])
                ds = pr * (dp - deltas[hh])
                if mode == "fox":
                    dck_ref[0, hh:hh + 1, js] += jnp.sum(ds, axis=0, keepdims=True)
                    rsum = rsum + jnp.sum(ds, axis=-1, keepdims=True)
                return emit(hh, js, kj, (ds * scale).astype(BF16), pr.astype(BF16), dq), rsum
            if mode == "ret":
                off = (i * TQ - c * w).astype(F32)
                if last:
                    dec = jnp.where(visible(c), jnp.exp(lgs[hh] * jnp.abs(off - distf)), 0.0)
                else:
                    dec = jnp.exp(lgs[hh] * (off - distf))
                return emit(hh, js, kj, (dp * dec).astype(BF16), (sc * dec).astype(BF16), carry)
            seen, gsum, dq = carry
            z = sc * _QK_SCALE
            lp = jnp.log1p(jnp.exp(-jnp.abs(z)))
            log_stay = jnp.minimum(-z, 0.0) - lp
            if last:
                vis = visible(c)
                log_stay = jnp.where(vis, log_stay, 0.0)
            parts = []
            for b in range(nsub):
                ls_b = log_stay[:, b * TQ:(b + 1) * TQ]
                parts.append((stats[hh] - seen) - _dot2(ls_b, u_upto))
                seen = seen + jnp.sum(ls_b, axis=-1, keepdims=True)
            later = parts[0] if nsub == 1 else jnp.concatenate(parts, axis=1)
            log_beta = jnp.minimum(z, 0.0) - lp
            wgt = jnp.exp(log_beta + later)
            if last:
                wgt = jnp.where(vis, wgt, 0.0)
            g = dp * wgt
            parts = []
            for b in range(nsub):
                g_b = g[:, b * TQ:(b + 1) * TQ]
                parts.append(gsum + _dot2(g_b, u_before))
                gsum = gsum + jnp.sum(g_b, axis=-1, keepdims=True)
            before = parts[0] if nsub == 1 else jnp.concatenate(parts, axis=1)
            beta = jnp.exp(log_beta)
            dz = g * (1.0 - beta) - beta * before
            if last:
                dz = jnp.where(vis, dz, 0.0)
            return seen, gsum, emit(hh, js, kj, (dz * _QK_SCALE).astype(BF16), wgt.astype(BF16), dq)

        def step(c, carries, last):
            js = pl.ds(pl.multiple_of(c * w, w), w)
            kj, vj = k_ref[js, :], v_ref[js, :]
            return tuple(head_step(hh, c, js, kj, vj, carries[hh], last) for hh in range(nhb))

        zero_acc = jnp.zeros((TQ, LANES), F32)
        zero1 = jnp.zeros((TQ, 1), F32)
        if softmax:
            init = tuple((zero_acc, zero1) for _ in range(nhb))
        elif mode == "ret":
            init = tuple(zero_acc for _ in range(nhb))
        else:
            init = tuple((zero1, zero1, zero_acc) for _ in range(nhb))
        carries = lax.fori_loop(0, nfull, lambda c, cs: step(c, cs, False), init)
        carries = step(nfull, carries, True)
        if softmax:
            dqs = [dq for (dq, rsum) in carries]
        elif mode == "ret":
            dqs = list(carries)
        else:
            dqs = [dq for (seen, gsum, dq) in carries]
        hm0 = (lane // HEAD) == 0
        dq_ref[...] = dqs[0] if nhb == 1 else jnp.where(hm0, dqs[0], dqs[1])
        if mode == "fox":
            drs_ref[0] = jnp.where(hm0, carries[0][1], carries[1][1])

    pair_blk = pl.BlockSpec((TQ, LANES), lambda p, i: (i, p))
    full_blk = pl.BlockSpec((s, LANES), lambda p, i: (0, p))
    stat_blk = pl.BlockSpec((1, TQ, LANES), lambda p, i: (p, i, 0))
    in_specs = [q_spec, k_spec, v_spec, pair_blk, pair_blk]
    args = [qa, ka, va, o, do]
    if has_stat:
        in_specs.append(stat_blk)
        args.append(stat)
    if mode == "fox":
        in_specs += [pl.BlockSpec((TQ, LANES), lambda p, i: (i, 0)), pl.BlockSpec((8, s), lambda p, i: (0, 0))]
        args += [cum_col, cum_row]
    out_specs = [pair_blk, full_blk, full_blk]
    out_shape = [jax.ShapeDtypeStruct((s, nblk * LANES), F32)] * 3
    if mode == "fox":
        out_specs += [pl.BlockSpec((1, 8, s), lambda p, i: (p, 0, 0)), stat_blk]
        out_shape += [jax.ShapeDtypeStruct((2, 8, s), F32), jax.ShapeDtypeStruct((2, s, LANES), F32)]
    return pl.pallas_call(
        body, name=mode + "_bwd", grid=(nblk, nq), in_specs=in_specs, out_specs=out_specs, out_shape=out_shape,
        compiler_params=_cparams(("parallel", "arbitrary")),
    )(*args)


def _seg_mean_matrix():
    r = lax.broadcasted_iota(jnp.int32, (GROUP, GROUP), 0)
    c = lax.broadcasted_iota(jnp.int32, (GROUP, GROUP), 1)
    return jnp.where((r // HEAD) == (c // HEAD), 1.0 / HEAD, 0.0).astype(F32)


def _sigmoid(x):
    return 1.0 / (1.0 + jnp.exp(-x))


def _mix_post(oa, ob, oc, od, proj, g):
    s = oa.shape[0]
    tr = _tile(s, 256)

    def body(a_ref, b_ref, c_ref, d_ref, rg_ref, g_ref, o_ref):
        gv = g_ref[...]
        o_ref[:, 0:GROUP] = _rms(a_ref[...], gv[:, 0:GROUP]).astype(BF16)
        o_ref[:, GROUP:2 * GROUP] = _rms(b_ref[...], gv[:, GROUP:2 * GROUP]).astype(BF16)
        seg = _seg_mean_matrix()
        c = c_ref[...]
        cen = c - _dot_exact(c, seg)
        n = cen * lax.rsqrt(_dot_exact(cen * cen, seg) + EPS)
        rg = rg_ref[...]
        o_ref[:, 2 * GROUP:3 * GROUP] = (n * gv[:, 2 * GROUP:3 * GROUP] * (rg * _sigmoid(rg))).astype(BF16)
        o_ref[:, 3 * GROUP:] = _rms(d_ref[...], gv[:, 3 * GROUP:]).astype(BF16)

    blk = pl.BlockSpec((tr, GROUP), lambda i: (i, 0))
    return pl.pallas_call(
        body, name="mix_post", grid=(s // tr,),
        in_specs=[blk] * 4 + [pl.BlockSpec((tr, GROUP), lambda i: (i, OFF_RG // 2)), pl.BlockSpec((1, D_MODEL), lambda i: (0, 0))],
        out_specs=pl.BlockSpec((tr, D_MODEL), lambda i: (i, 0)), out_shape=jax.ShapeDtypeStruct((s, D_MODEL), BF16),
        compiler_params=_cparams(("parallel",)),
    )(oa, ob, oc, od, proj, g.reshape(1, D_MODEL))


def _mix_post_bwd(dmixed, oa, ob, oc, od, proj, g):
    s = oa.shape[0]
    tr = _tile(s, 256)

    def body(dm_ref, a_ref, b_ref, c_ref, d_ref, rg_ref, g_ref, da_ref, db_ref, dc_ref, dd_ref, drg_ref, dg_ref):
        @pl.when(pl.program_id(0) == 0)
        def _():
            dg_ref[...] = jnp.zeros_like(dg_ref)

        gv = g_ref[...]
        dm = dm_ref[...]
        for k, (x_ref, dx_ref) in enumerate(((a_ref, da_ref), (b_ref, db_ref), (None, None), (d_ref, dd_ref))):
            if x_ref is None:
                continue
            cols = slice(k * GROUP, (k + 1) * GROUP)
            dx, gterm = _rms_bwd(x_ref[...], gv[:, cols], dm[:, cols])
            dx_ref[...] = dx
            dg_ref[:, cols] += jnp.sum(gterm, axis=0, keepdims=True)
        cols = slice(2 * GROUP, 3 * GROUP)
        seg = _seg_mean_matrix()
        c = c_ref[...]
        cen = c - _dot_exact(c, seg)
        rstd = lax.rsqrt(_dot_exact(cen * cen, seg) + EPS)
        n = cen * rstd
        rg = rg_ref[...]
        sg = _sigmoid(rg)
        gate = rg * sg
        dy = dm[:, cols]
        gc = gv[:, cols]
        dn = dy * gc * gate
        dg_ref[:, cols] += jnp.sum(dy * n * gate, axis=0, keepdims=True)
        drg_ref[...] = (dy * n * gc * (sg * (1.0 + rg * (1.0 - sg)))).astype(BF16)
        dc_ref[...] = rstd * (dn - _dot_exact(dn, seg) - n * _dot_exact(dn * n, seg))

    blk = pl.BlockSpec((tr, GROUP), lambda i: (i, 0))
    gsp = pl.BlockSpec((1, D_MODEL), lambda i: (0, 0))
    return pl.pallas_call(
        body, name="mix_post_bwd", grid=(s // tr,),
        in_specs=[pl.BlockSpec((tr, D_MODEL), lambda i: (i, 0))] + [blk] * 4 + [pl.BlockSpec((tr, GROUP), lambda i: (i, OFF_RG // 2)), gsp],
        out_specs=[blk] * 5 + [gsp],
        out_shape=[jax.ShapeDtypeStruct((s, GROUP), F32)] * 4 + [jax.ShapeDtypeStruct((s, GROUP), BF16), jax.ShapeDtypeStruct((1, D_MODEL), F32)],
        compiler_params=_cparams(("arbitrary",)),
    )(dmixed, oa, ob, oc, od, proj, g.reshape(1, D_MODEL))


def _pack_w_in(w):
    z = lambda n: jnp.zeros((w.shape[0], n), w.dtype)
    misc = jnp.concatenate([w[:, 768:772], z(KR_LANE - N_HEADS), w[:, 1156:1188], z(LANES - KR_LANE - ROPE_DIM)], axis=1)
    return jnp.concatenate([w[:, 0:768], w[:, 772:1028], w[:, 1188:2980], w[:, 1028:1156], misc], axis=1)


def _unpack_dw_in(d):
    m = OFF_MISC * LANES
    return jnp.concatenate([d[:, 0:768], d[:, m:m + N_HEADS], d[:, 768:1024], d[:, OFF_CKV * LANES:m],
                            d[:, m + KR_LANE:m + KR_LANE + ROPE_DIM], d[:, 1024:OFF_CKV * LANES]], axis=1)


def _pack_w_q(w):
    return jnp.pad(w.reshape(Q_RANK, N_HEADS, HEAD + ROPE_DIM), ((0, 0), (0, 0), (0, LANES - HEAD - ROPE_DIM))).reshape(Q_RANK, 4 * LANES)


def _unpack_dw_q(d):
    return d.reshape(Q_RANK, N_HEADS, LANES)[:, :, :HEAD + ROPE_DIM].reshape(Q_RANK, N_HEADS * (HEAD + ROPE_DIM))


def _pack_w_kv(w):
    w4 = w.reshape(KV_RANK, N_HEADS, 2 * HEAD)
    widen = lambda a: jnp.pad(a, ((0, 0), (0, 0), (0, LANES - HEAD))).reshape(KV_RANK, N_HEADS * LANES)
    return widen(w4[:, :, :HEAD]), widen(w4[:, :, HEAD:])


def _unpack_dw_kv(dk, dv):
    narrow = lambda a: a.reshape(KV_RANK, N_HEADS, LANES)[:, :, :HEAD]
    return jnp.concatenate([narrow(dk), narrow(dv)], axis=2).reshape(KV_RANK, 2 * N_HEADS * HEAD)


def _narrow_heads(a):
    return a.reshape(a.shape[0], N_HEADS, LANES)[:, :, :HEAD].reshape(a.shape[0], N_HEADS * HEAD)


def _widen_heads(a):
    return jnp.pad(a.reshape(a.shape[0], N_HEADS, HEAD), ((0, 0), (0, 0), (0, LANES - HEAD))).reshape(a.shape[0], N_HEADS * LANES)


def _layer_fwd(x, lw, tabs, tag):
    cos_m, sin_m, cos_r, sin_r = tabs
    h1 = _norm_fwd(x, lw["g_mix_pre"], name=tag + "pre_norm")
    proj, projb = _matmul(h1, lw["w_in"], name=tag + "in_proj", also_bf16=True)
    bias_row = jnp.pad(lw["b_forget"], (FF_LANE, LANES - N_HEADS - FF_LANE)).reshape(1, LANES)
    cum_col, cum_row = _fox_cum(proj, bias_row)
    oa, lse_a = _mixer_fwd("fox", projb, OFF_FQ, projb, OFF_FK, projb, OFF_FV, cum_col=cum_col, cum_row=cum_row)
    qm, km, vm, cqn, ckvn = _mla_prep(proj, cos_m, sin_m, lw["g_q_lora"], lw["g_kv_lora"], lw["wq"], lw["wk"], lw["wv"])
    ob_wide, lse_b = _mixer_fwd("mla", qm, 0, km, 0, vm, 0)
    ob = _narrow_heads(ob_wide)
    qr, kr = _ret_prep(proj, cos_r, sin_r)
    oc, _ = _mixer_fwd("ret", qr, 0, kr, 0, projb, OFF_RV)
    od, tot_d = _mixer_fwd("sb", projb, OFF_SQ, projb, OFF_SK, projb, OFF_SV)
    mixed = _mix_post(oa, ob, oc, od, proj, lw["g_mix_out"])
    mix = _matmul(mixed, lw["w_out"], name=tag + "out_proj")
    x1 = _norm_fwd(mix, lw["g_mix_post"], name=tag + "mix_post_norm", resid=x, out_dtype=F32)
    h2 = _norm_fwd(x1, lw["g_ffn_pre"], name=tag + "ffn_pre_norm")
    u_pre, u = _matmul(h2, lw["w_ffn_up"], name=tag + "ffn_up", relu2=True)
    f = _matmul(u, lw["w_ffn_down"], name=tag + "ffn_down")
    x2 = _norm_fwd(f, lw["g_ffn_post"], name=tag + "ffn_post_norm", resid=x1, out_dtype=F32)
    saved = dict(x=x, h1=h1, proj=proj, projb=projb, bias_row=bias_row, cum_col=cum_col, cum_row=cum_row, oa=oa, lse_a=lse_a,
                 qm=qm, km=km, vm=vm, cqn=cqn, ckvn=ckvn, ob=ob, ob_wide=ob_wide, lse_b=lse_b, qr=qr, kr=kr, oc=oc, od=od, tot_d=tot_d, mixed=mixed,
                 mix=mix, x1=x1, h2=h2, u_pre=u_pre, u=u, f=f)
    return x2, saved


def _layer_bwd(dx2, lw, sv, tabs, tag):
    cos_m, sin_m, cos_r, sin_r = tabs
    g = {}
    df, g["g_ffn_post"] = _norm_bwd(sv["f"], lw["g_ffn_post"], dx2, name=tag + "ffn_post_norm_bwd", out_dtype=BF16)
    du_pre = _matmul(df, lw["w_ffn_down"], name=tag + "ffn_down_dx", tb=True, out_dtype=BF16, relu_of=sv["u_pre"])
    g["w_ffn_down"] = _matmul(sv["u"], df, name=tag + "ffn_down_dw", ta=True)
    dh2 = _matmul(du_pre, lw["w_ffn_up"], name=tag + "ffn_up_dx", tb=True)
    g["w_ffn_up"] = _matmul(sv["h2"], du_pre, name=tag + "ffn_up_dw", ta=True)
    dx1, g["g_ffn_pre"] = _norm_bwd(sv["x1"], lw["g_ffn_pre"], dh2, name=tag + "ffn_pre_norm_bwd", add=dx2)
    dmix, g["g_mix_post"] = _norm_bwd(sv["mix"], lw["g_mix_post"], dx1, name=tag + "mix_post_norm_bwd", out_dtype=BF16)
    dmixed = _matmul(dmix, lw["w_out"], name=tag + "out_proj_dx", tb=True)
    g["w_out"] = _matmul(sv["mixed"], dmix, name=tag + "out_proj_dw", ta=True)
    proj, projb = sv["proj"], sv["projb"]
    doa, dob, doc, dod, drg, g["g_mix_out"] = _mix_post_bwd(dmixed, sv["oa"], sv["ob"], sv["oc"], sv["od"], proj, lw["g_mix_out"])
    dfq, dfk, dfv, dck, drs = _mixer_bwd("fox", projb, OFF_FQ, projb, OFF_FK, projb, OFF_FV, sv["oa"], doa, stat=sv["lse_a"],
                                         cum_col=sv["cum_col"], cum_row=sv["cum_row"])
    dqm, dkm, dvm = _mixer_bwd("mla", sv["qm"], 0, sv["km"], 0, sv["vm"], 0, sv["ob_wide"], _widen_heads(dob), stat=sv["lse_b"])
    dcq, dckv, dkr, dwq, dwk, dwv, g["g_q_lora"], g["g_kv_lora"] = _mla_prep_bwd(
        dqm, dkm, dvm, proj, sv["cqn"], sv["ckvn"], cos_m, sin_m, lw["g_q_lora"], lw["g_kv_lora"], lw["wq"], lw["wk"], lw["wv"])
    dqr, dkr_ret, drv = _mixer_bwd("ret", sv["qr"], 0, sv["kr"], 0, projb, OFF_RV, sv["oc"], doc)
    drq, drk = _ret_prep_bwd(dqr, dkr_ret, cos_r, sin_r)
    dsq, dsk, dsv = _mixer_bwd("sb", projb, OFF_SQ, projb, OFF_SK, projb, OFF_SV, sv["od"], dod, stat=sv["tot_d"])
    dmisc, db_row = _fox_gate_bwd(dck, drs, proj, sv["bias_row"], dkr)
    b = lambda a: a.astype(BF16)
    dproj = jnp.concatenate([b(dfq), b(dfk), b(dfv), dcq, drq, drk, b(drv), drg, b(dsq), b(dsk), b(dsv), dckv, dmisc], axis=1)
    dh1 = _matmul(dproj, lw["w_in"], name=tag + "in_proj_dx", tb=True)
    g["w_in"] = _matmul(sv["h1"], dproj, name=tag + "in_proj_dw", ta=True)
    dx, g["g_mix_pre"] = _norm_bwd(sv["x"], lw["g_mix_pre"], dh1, name=tag + "pre_norm_bwd", add=dx1)
    g["b_forget"] = db_row[0, FF_LANE:FF_LANE + N_HEADS]
    g["wq"], g["wk"], g["wv"] = dwq, dwk, dwv
    return dx, g


def _local_step(x, positions, layers, target):
    s = x.shape[0]
    tabs = _rope_tables(positions.reshape(s, 1))
    saved = []
    for li, lw in enumerate(layers):
        x, sv = _layer_fwd(x, lw, tabs, "l%d_" % li)
        saved.append(sv)
    loss_row, dx = _loss_head(x, target)
    grads = [None] * len(layers)
    for li in reversed(range(len(layers))):
        dx, grads[li] = _layer_bwd(dx, layers[li], saved[li], tabs, "l%d_" % li)
    return loss_row[0, 0], dx, grads


def _adamw(w, g, m, v, *, name):
    r, c = w.shape
    tr = 256 if r % 256 == 0 else r
    blk = pl.BlockSpec((tr, c), lambda i: (i, 0))
    c1 = 1.0 - ADAM_B1 ** ADAM_STEP
    c2 = 1.0 - ADAM_B2 ** ADAM_STEP

    def body(w_ref, g_ref, m_ref, v_ref, d_ref, mo_ref, vo_ref):
        gv = g_ref[...]
        mn = ADAM_B1 * m_ref[...] + (1.0 - ADAM_B1) * gv
        vn = ADAM_B2 * v_ref[...] + (1.0 - ADAM_B2) * jnp.square(gv)
        mo_ref[...] = mn
        vo_ref[...] = vn
        d_ref[...] = -ADAM_LR * ((mn / c1) / (jnp.sqrt(vn / c2) + ADAM_EPS) + ADAM_WD * w_ref[...])

    return pl.pallas_call(
        body, name=name, grid=(r // tr,), in_specs=[blk] * 4, out_specs=[blk] * 3,
        out_shape=[jax.ShapeDtypeStruct((r, c), F32)] * 3, compiler_params=_cparams(("parallel",)),
    )(w, g, m, v)


BIG = ("w_in", "w_q_up", "w_kv_up", "w_out", "w_ffn_up", "w_ffn_down")
SMALL = ("g_mix_pre", "b_forget", "g_q_lora", "g_kv_lora", "g_mix_out", "g_mix_post", "g_ffn_pre", "g_ffn_post")
N_CHIPS = 4
ANY = pl.BlockSpec(memory_space=pl.ANY)


def _mesh_pos():
    return lax.axis_index("x"), lax.axis_index("y"), lax.axis_index("c")


def _other_chips(x, y):
    return [(1 - x, y), (x, 1 - y), (1 - x, 1 - y)]


def _rows_half(ref, half):
    h = ref.shape[-2] // 2
    return ref.at[(slice(None),) * (len(ref.shape) - 2) + (pl.ds(half * h, h), slice(None))]


def _remote(src, dst, send_sem, recv_sem, device):
    return pltpu.make_async_remote_copy(src_ref=src, dst_ref=dst, send_sem=send_sem, recv_sem=recv_sem, device_id=device,
                                        device_id_type=MESH)


def _comm_call(body, name, args, out_shape, n_sems):
    return pl.pallas_call(
        body, name=name, in_specs=[ANY] * len(args), out_specs=[ANY] * len(out_shape), out_shape=out_shape,
        scratch_shapes=[pltpu.SemaphoreType.DMA((n_sems,)), pltpu.SemaphoreType.DMA((n_sems,))],
        compiler_params=pltpu.CompilerParams(has_side_effects=True),
    )(*args)


def _gather_weights(shards):
    n = len(shards)

    def body(*refs):
        ins, outs = refs[:n], refs[n:2 * n]
        ici_send, ici_recv, d2d_send, d2d_recv = refs[2 * n:]
        x, y, c = _mesh_pos()
        mine = 2 * x + y
        peers = _other_chips(x, y)

        def ici(t, j, block):
            px, py = peers[j]
            return _remote(_rows_half(ins[t], c), _rows_half(outs[t].at[block], c), ici_send.at[3 * t + j], ici_recv.at[3 * t + j],
                           (px, py, c))

        def d2d(t, j, block, half):
            region = _rows_half(outs[t].at[block], half)
            return _remote(region, region, d2d_send.at[3 * t + j], d2d_recv.at[3 * t + j], (x, y, 1 - c))

        sends = [ici(t, j, mine) for t in range(n) for j in range(3)]
        for cp in sends:
            cp.start()
        passed = []
        for t in range(n):
            for j, (px, py) in enumerate(peers):
                ici(t, j, 2 * px + py).wait_recv()
                fwd = d2d(t, j, 2 * px + py, c)
                fwd.start()
                passed.append(fwd)
        for t in range(n):
            for j, (px, py) in enumerate(peers):
                d2d(t, j, 2 * px + py, 1 - c).wait_recv()
        for cp in sends + passed:
            cp.wait_send()

    out_shape = [jax.ShapeDtypeStruct((N_CHIPS,) + a.shape, a.dtype) for a in shards]
    return pl.pallas_call(
        body, name="gather_weights", in_specs=[ANY] * n, out_specs=[ANY] * n, out_shape=out_shape,
        scratch_shapes=[pltpu.SemaphoreType.DMA((3 * n,))] * 4,
        compiler_params=pltpu.CompilerParams(has_side_effects=True),
    )(*shards)


def _exchange_halves(gs):
    n = len(gs)

    def body(*refs):
        ins, outs, send_sems, recv_sems = refs[:n], refs[n:2 * n], refs[2 * n], refs[2 * n + 1]
        x, y, c = _mesh_pos()
        cps = [_remote(_rows_half(ins[t], 1 - c), outs[t], send_sems.at[t], recv_sems.at[t], (x, y, 1 - c)) for t in range(n)]
        for cp in cps:
            cp.start()
        for cp in cps:
            cp.wait_recv()
        for cp in cps:
            cp.wait_send()

    out_shape = [jax.ShapeDtypeStruct(g.shape[:2] + (g.shape[2] // 2, g.shape[3]), g.dtype) for g in gs]
    return _comm_call(body, "grad_pair_exchange", gs, out_shape, n)


def _pair_add(g, r, c_idx, *, name):
    nb, d, rows, cols = g.shape
    h = rows // 2
    tr = min(h, 512)
    nt = h // tr

    def body(c_ref, g_ref, r_ref, p_ref, pb_ref):
        s = g_ref[...] + r_ref[...]
        p_ref[...] = s
        pb_ref[...] = s.astype(BF16)

    blk = pl.BlockSpec((1, 1, tr, cols), lambda k, l, i, c_ref: (k, l, i, 0))
    return pl.pallas_call(
        body, name=name,
        grid_spec=pltpu.PrefetchScalarGridSpec(
            num_scalar_prefetch=1, grid=(nb, d, nt),
            in_specs=[pl.BlockSpec((1, 1, tr, cols), lambda k, l, i, c_ref: (k, l, c_ref[0] * nt + i, 0)), blk],
            out_specs=[blk, blk]),
        out_shape=[jax.ShapeDtypeStruct((nb, d, h, cols), F32), jax.ShapeDtypeStruct((nb, d, h, cols), BF16)],
        compiler_params=_cparams(("parallel", "parallel", "parallel")),
    )(c_idx, g, r)


def _exchange_chips(pbs):
    n = len(pbs)

    def body(*refs):
        ins, outs, send_sems, recv_sems = refs[:n], refs[n:2 * n], refs[2 * n], refs[2 * n + 1]
        x, y, c = _mesh_pos()
        cps = [_remote(ins[t].at[2 * px + py], outs[t].at[j], send_sems.at[3 * t + j], recv_sems.at[3 * t + j], (px, py, c))
               for t in range(n) for j, (px, py) in enumerate(_other_chips(x, y))]
        for cp in cps:
            cp.start()
        for cp in cps:
            cp.wait_recv()
        for cp in cps:
            cp.wait_send()

    out_shape = [jax.ShapeDtypeStruct((3,) + p.shape[1:], p.dtype) for p in pbs]
    return _comm_call(body, "grad_chip_exchange", pbs, out_shape, 3 * n)


def _chip_add(p, r, k_idx, *, name):
    _, d, h, cols = p.shape
    tr = min(h, 512)
    nt = h // tr

    def body(k_ref, p_ref, r_ref, o_ref):
        o_ref[0] = ((p_ref[0, 0] + r_ref[0, 0].astype(F32)) + r_ref[1, 0].astype(F32)) + r_ref[2, 0].astype(F32)

    return pl.pallas_call(
        body, name=name,
        grid_spec=pltpu.PrefetchScalarGridSpec(
            num_scalar_prefetch=1, grid=(d, nt),
            in_specs=[pl.BlockSpec((1, 1, tr, cols), lambda l, i, k_ref: (k_ref[0], l, i, 0)),
                      pl.BlockSpec((3, 1, tr, cols), lambda l, i, k_ref: (0, l, i, 0))],
            out_specs=pl.BlockSpec((1, tr, cols), lambda l, i, k_ref: (l, i, 0))),
        out_shape=jax.ShapeDtypeStruct((d, h, cols), F32), compiler_params=_cparams(("parallel", "parallel")),
    )(k_idx, p, r)


def _share_halves(qs):
    n = len(qs)

    def body(*refs):
        ins, outs, send_sems, recv_sems = refs[:n], refs[n:2 * n], refs[2 * n], refs[2 * n + 1]
        x, y, c = _mesh_pos()
        cps = [_remote(ins[t], outs[t], send_sems.at[t], recv_sems.at[t], (x, y, 1 - c)) for t in range(n)]
        for cp in cps:
            cp.start()
        for cp in cps:
            cp.wait_recv()
        for cp in cps:
            cp.wait_send()

    return _comm_call(body, "grad_pair_share", qs, [jax.ShapeDtypeStruct(q.shape, q.dtype) for q in qs], n)


def _all_reduce_small(v):
    r, cols = v.shape
    n_dev = 8

    def body(v_ref, o_ref, buf, send_sems, recv_sems):
        x, y, c = _mesh_pos()
        me = 4 * x + 2 * y + c
        buf[me] = v_ref[...]

        def peer(j):
            return (1 - x if j & 4 else x, 1 - y if j & 2 else y, 1 - c if j & 1 else c)

        def copy(j, slot):
            return pltpu.make_async_remote_copy(src_ref=v_ref, dst_ref=buf.at[slot], send_sem=send_sems.at[j - 1],
                                                recv_sem=recv_sems.at[j - 1], device_id=peer(j), device_id_type=MESH)

        sends = [copy(j, me) for j in range(1, n_dev)]
        for cp in sends:
            cp.start()
        for j in range(1, n_dev):
            px, py, pc = peer(j)
            copy(j, 4 * px + 2 * py + pc).wait_recv()
        for cp in sends:
            cp.wait_send()
        acc = buf[0]
        for d in range(1, n_dev):
            acc = acc + buf[d]
        o_ref[...] = acc

    vm = pl.BlockSpec(memory_space=pltpu.VMEM)
    return pl.pallas_call(
        body, name="small_all_reduce", in_specs=[vm], out_specs=vm, out_shape=jax.ShapeDtypeStruct((r, cols), F32),
        scratch_shapes=[pltpu.VMEM((n_dev, r, cols), F32), pltpu.SemaphoreType.DMA((n_dev - 1,)), pltpu.SemaphoreType.DMA((n_dev - 1,))],
        compiler_params=pltpu.CompilerParams(has_side_effects=True),
    )(v)


_COL_SHARDED = ("w_in", "w_q_up", "w_kv_up", "w_ffn_up")


def _shard_cols(blocks, a, b):
    c = blocks.shape[-1]
    out = []
    while a < b:
        k = a // c
        hi = min(b, (k + 1) * c)
        out.append(blocks[k][:, a - k * c:hi - k * c])
        a = hi
    return out


def _pack_w_in_shards(blocks):
    z = lambda n: [jnp.zeros((blocks.shape[1], n), blocks.dtype)]
    cols = lambda a, b: _shard_cols(blocks, a, b)
    return jnp.concatenate(cols(0, 768) + cols(772, 1028) + cols(1188, 2980) + cols(1028, 1156) + cols(768, 772)
                           + z(KR_LANE - N_HEADS) + cols(1156, 1188) + z(LANES - KR_LANE - ROPE_DIM), axis=1)


def _whole_layer(name, blocks):
    if name in _COL_SHARDED:
        return jnp.concatenate([blocks[k] for k in range(N_CHIPS)], axis=1)
    return blocks.reshape(N_CHIPS * blocks.shape[1], blocks.shape[2])


def _split_layer(name, whole):
    if name in _COL_SHARDED:
        c = whole.shape[1] // N_CHIPS
        return jnp.stack([whole[:, k * c:(k + 1) * c] for k in range(N_CHIPS)])
    return whole.reshape(N_CHIPS, whole.shape[0] // N_CHIPS, whole.shape[1])


def _small_to_rows(d):
    v = jnp.concatenate([d[k].astype(F32).reshape(-1) for k in SMALL])
    rows = -(-v.shape[0] // (8 * LANES)) * 8
    return jnp.pad(v, (0, rows * LANES - v.shape[0])).reshape(rows, LANES)


def _small_from_rows(rows, shapes):
    v = rows.reshape(-1)
    out, o = {}, 0
    for k in SMALL:
        sz = int(np.prod(shapes[k]))
        out[k] = v[o:o + sz].reshape(shapes[k])
        o += sz
    return out


_ARG_NAMES = ("x", "positions", "g_mix_pre", "w_in", "b_forget", "g_q_lora", "w_q_up", "g_kv_lora", "w_kv_up", "g_mix_out", "w_out",
              "g_mix_post", "g_ffn_pre", "w_ffn_up", "w_ffn_down", "g_ffn_post")
_WEIGHTS = _ARG_NAMES[2:]


def kernel(x, positions, g_mix_pre, w_in, b_forget, g_q_lora, w_q_up, g_kv_lora, w_kv_up, g_mix_out, w_out, g_mix_post, g_ffn_pre, w_ffn_up, w_ffn_down, g_ffn_post, loss_target, m_g_mix_pre, m_w_in, m_b_forget, m_g_q_lora, m_w_q_up, m_g_kv_lora, m_w_kv_up, m_g_mix_out, m_w_out, m_g_mix_post, m_g_ffn_pre, m_w_ffn_up, m_w_ffn_down, m_g_ffn_post, v_g_mix_pre, v_w_in, v_b_forget, v_g_q_lora, v_w_q_up, v_g_kv_lora, v_w_kv_up, v_g_mix_out, v_w_out, v_g_mix_post, v_g_ffn_pre, v_w_ffn_up, v_w_ffn_down, v_g_ffn_post):
    w = dict(g_mix_pre=g_mix_pre, w_in=w_in, b_forget=b_forget, g_q_lora=g_q_lora, w_q_up=w_q_up, g_kv_lora=g_kv_lora, w_kv_up=w_kv_up,
             g_mix_out=g_mix_out, w_out=w_out, g_mix_post=g_mix_post, g_ffn_pre=g_ffn_pre, w_ffn_up=w_ffn_up, w_ffn_down=w_ffn_down,
             g_ffn_post=g_ffn_post)
    m = dict(g_mix_pre=m_g_mix_pre, w_in=m_w_in, b_forget=m_b_forget, g_q_lora=m_g_q_lora, w_q_up=m_w_q_up, g_kv_lora=m_g_kv_lora,
             w_kv_up=m_w_kv_up, g_mix_out=m_g_mix_out, w_out=m_w_out, g_mix_post=m_g_mix_post, g_ffn_pre=m_g_ffn_pre,
             w_ffn_up=m_w_ffn_up, w_ffn_down=m_w_ffn_down, g_ffn_post=m_g_ffn_post)
    v = dict(g_mix_pre=v_g_mix_pre, w_in=v_w_in, b_forget=v_b_forget, g_q_lora=v_g_q_lora, w_q_up=v_w_q_up, g_kv_lora=v_g_kv_lora,
             w_kv_up=v_w_kv_up, g_mix_out=v_g_mix_out, w_out=v_w_out, g_mix_post=v_g_mix_post, g_ffn_pre=v_g_ffn_pre,
             w_ffn_up=v_w_ffn_up, w_ffn_down=v_w_ffn_down, g_ffn_post=v_g_ffn_post)
    shard_shapes = {k: w[k].shape for k in BIG}
    small_shapes = {k: w[k].shape for k in SMALL}
    c_idx = lax.axis_index("c").astype(jnp.int32).reshape(1)
    k_idx = (2 * lax.axis_index("x") + lax.axis_index("y")).astype(jnp.int32).reshape(1)

    mine = 2 * lax.axis_index("x") + lax.axis_index("y")
    shards_b = [w[k].astype(BF16) for k in BIG]
    gathered = _gather_weights(shards_b)
    four = {k: lax.dynamic_update_slice(g, s[None], (mine, 0, 0, 0)) for k, g, s in zip(BIG, gathered, shards_b)}
    layers = []
    for l in range(DEPTH):
        wk, wv = _pack_w_kv(_whole_layer("w_kv_up", four["w_kv_up"][:, l]))
        layers.append(dict(
            g_mix_pre=g_mix_pre[l], w_in=_pack_w_in_shards(four["w_in"][:, l]), b_forget=b_forget[l], g_q_lora=g_q_lora[l],
            g_kv_lora=g_kv_lora[l], wq=_pack_w_q(_whole_layer("w_q_up", four["w_q_up"][:, l])), wk=wk, wv=wv, g_mix_out=g_mix_out[l],
            w_out=_whole_layer("w_out", four["w_out"][:, l]), g_mix_post=g_mix_post[l], g_ffn_pre=g_ffn_pre[l],
            w_ffn_up=_whole_layer("w_ffn_up", four["w_ffn_up"][:, l]), w_ffn_down=_whole_layer("w_ffn_down", four["w_ffn_down"][:, l]),
            g_ffn_post=g_ffn_post[l]))

    loss_local, dx, grads = _local_step(x[0], positions[0], layers, loss_target[0])
    loss = lax.psum(loss_local, ("x", "y", "c"))

    whole_grad = dict(
        w_in=lambda l: _unpack_dw_in(grads[l]["w_in"]), w_q_up=lambda l: _unpack_dw_q(grads[l]["wq"]),
        w_kv_up=lambda l: _unpack_dw_kv(grads[l]["wk"], grads[l]["wv"]), w_out=lambda l: grads[l]["w_out"],
        w_ffn_up=lambda l: grads[l]["w_ffn_up"], w_ffn_down=lambda l: grads[l]["w_ffn_down"])
    blocks = [jnp.stack([_split_layer(k, whole_grad[k](l)) for l in range(DEPTH)], axis=1) for k in BIG]
    theirs = _exchange_halves(blocks)
    pair = [_pair_add(g, r, c_idx, name="grad_pair_add_" + k) for k, g, r in zip(BIG, blocks, theirs)]
    partial = _exchange_chips([pb for (_, pb) in pair])
    mine_half = [_chip_add(p, r, k_idx, name="grad_chip_add_" + k) for k, (p, _), r in zip(BIG, pair, partial)]
    sibling_half = _share_halves(mine_half)
    first = lax.axis_index("c") == 0
    g_big = {k: jnp.where(first, jnp.concatenate([q, s], axis=1), jnp.concatenate([s, q], axis=1))
             for k, q, s in zip(BIG, mine_half, sibling_half)}

    g_small_local = {k: jnp.stack([grads[l][k].reshape(small_shapes[k][1:]) for l in range(DEPTH)]) for k in SMALL}
    g_small = _small_from_rows(_all_reduce_small(_small_to_rows(g_small_local)), small_shapes)

    g_all = {**g_big, **g_small}
    delta, new_m, new_v = {}, {}, {}
    for k in BIG:
        d, r, c = shard_shapes[k]
        two_d = lambda a: a.reshape(d * r, c)
        dk, mk, vk = _adamw(two_d(w[k]), two_d(g_all[k]), two_d(m[k]), two_d(v[k]), name="adamw_" + k)
        delta[k], new_m[k], new_v[k] = dk.reshape(d, r, c), mk.reshape(d, r, c), vk.reshape(d, r, c)
    ds, ms, vs = _adamw(_small_to_rows(w), _small_to_rows(g_small), _small_to_rows(m), _small_to_rows(v), name="adamw_small")
    delta.update(_small_from_rows(ds, small_shapes))
    new_m.update(_small_from_rows(ms, small_shapes))
    new_v.update(_small_from_rows(vs, small_shapes))

    grad_x = dx.reshape(x.shape)
    return (loss, grad_x, *[g_all[k] for k in _WEIGHTS], *[delta[k] for k in _WEIGHTS], *[new_m[k] for k in _WEIGHTS],
            *[new_v[k] for k in _WEIGHTS])
```

```python
import functools
import math

import numpy as np
import jax
import jax.numpy as jnp
from jax import lax
from jax.experimental import pallas as pl
from jax.experimental.pallas import tpu as pltpu

F32 = jnp.float32
BF16 = jnp.bfloat16
MESH = pl.DeviceIdType.MESH

D_MODEL = 1024
DEPTH = 2
CHUNK = 64
GROUP = 256
HEAD = 64
N_HEADS = 4
Q_RANK = 256
KV_RANK = 128
ROPE_DIM = 32
D_FF = 4096
D_IN = 2980
D_INP = 3072
ROPE_BASE = 10000.0
EPS = 1e-6
LANES = 128
TQ = 128
NEG = -1e30

ADAM_LR, ADAM_B1, ADAM_B2, ADAM_EPS, ADAM_WD, ADAM_STEP = 0.001, 0.9, 0.999, 1e-08, 0.01, 10

OFF_FQ, OFF_FK, OFF_FV, OFF_CQ = 0, 2, 4, 6
OFF_RQ, OFF_RK, OFF_RV, OFF_RG = 8, 10, 12, 14
OFF_SQ, OFF_SK, OFF_SV = 16, 18, 20
OFF_CKV, OFF_MISC = 22, 23
FF_LANE, KR_LANE = 0, 64

VMEM_LIMIT = 56 * 1024 * 1024


def _tile(dim, pref):
    return pref if dim % pref == 0 else dim


def _cparams(sem, vmem=None):
    return pltpu.CompilerParams(dimension_semantics=sem, vmem_limit_bytes=vmem or VMEM_LIMIT)


def _dot(a, b):
    return jnp.dot(a, b, preferred_element_type=F32)


def _dot_nt(a, b):
    return lax.dot_general(a, b, (((1,), (1,)), ((), ())), preferred_element_type=F32)


def _dot_tn(a, b):
    return lax.dot_general(a, b, (((0,), (0,)), ((), ())), preferred_element_type=F32)


def _dot_exact(a, b):
    return jnp.dot(a, b, precision=lax.Precision.HIGHEST, preferred_element_type=F32)


def _matmul(a, b, *, name, ta=False, tb=False, out_dtype=F32, tm=1024, tn=1024, tk=1024,
            relu2=False, relu2_of=None, also_bf16=False):
    if ta:
        kdim, m = a.shape
    else:
        m, kdim = a.shape
    n = b.shape[0] if tb else b.shape[1]
    tm, tn, tk = _tile(m, tm), _tile(n, tn), _tile(kdim, tk)
    nk = kdim // tk
    a_spec = pl.BlockSpec((tk, tm), lambda i, j, k: (k, i)) if ta else pl.BlockSpec((tm, tk), lambda i, j, k: (i, k))
    b_spec = pl.BlockSpec((tn, tk), lambda i, j, k: (j, k)) if tb else pl.BlockSpec((tk, tn), lambda i, j, k: (k, j))
    o_spec = pl.BlockSpec((tm, tn), lambda i, j, k: (i, j))
    two = also_bf16

    def body(*refs):
        refs = list(refs)
        a_ref, b_ref = refs[0], refs[1]
        e_ref = refs[2] if relu2_of is not None else None
        pos = 3 if relu2_of is not None else 2
        o_ref = refs[pos]
        o2_ref = refs[pos + 1] if two else None
        acc_ref = refs[-1]
        k = pl.program_id(2)
        av = a_ref[...].astype(BF16)
        bv = b_ref[...].astype(BF16)
        if ta:
            part = _dot_tn(av, bv)
        elif tb:
            part = _dot_nt(av, bv)
        else:
            part = _dot(av, bv)

        @pl.when(k == 0)
        def _():
            acc_ref[...] = part

        @pl.when(k > 0)
        def _():
            acc_ref[...] += part

        @pl.when(k == nk - 1)
        def _():
            r = acc_ref[...]
            if relu2_of is not None:
                r = r * (2.0 * jnp.sqrt(e_ref[...].astype(F32)))
            if relu2:
                r = jnp.square(jnp.maximum(r, 0.0))
            o_ref[...] = r.astype(o_ref.dtype)
            if also_bf16:
                o2_ref[...] = r.astype(BF16)

    in_specs = [a_spec, b_spec]
    args = [a, b]
    if relu2_of is not None:
        in_specs.append(o_spec)
        args.append(relu2_of)
    out_shape = [jax.ShapeDtypeStruct((m, n), out_dtype)]
    out_specs = [o_spec]
    if two:
        out_shape.append(jax.ShapeDtypeStruct((m, n), BF16))
        out_specs.append(o_spec)
    res = pl.pallas_call(
        body, name=name, grid=(m // tm, n // tn, nk), in_specs=in_specs, out_specs=out_specs, out_shape=out_shape,
        scratch_shapes=[pltpu.VMEM((tm, tn), F32)],
        compiler_params=_cparams(("parallel", "parallel", "arbitrary")),
    )(*args)
    return res if two else res[0]


def _rms(x, g):
    r = lax.rsqrt(jnp.mean(x * x, axis=-1, keepdims=True) + EPS)
    return x * r * g


def _rms_bwd(x, g, dy):
    r = lax.rsqrt(jnp.mean(x * x, axis=-1, keepdims=True) + EPS)
    xh = x * r
    gdy = dy * g
    dx = r * (gdy - xh * jnp.mean(xh * gdy, axis=-1, keepdims=True))
    return dx, xh * dy


def _norm_fwd(x, g, *, name, resid=None, out_dtype=BF16):
    s, d = x.shape
    tr = _tile(s, 256)
    row = pl.BlockSpec((tr, d), lambda i: (i, 0))
    gsp = pl.BlockSpec((1, d), lambda i: (0, 0))

    def body(*refs):
        if resid is None:
            x_ref, g_ref, o_ref = refs
            o_ref[...] = _rms(x_ref[...], g_ref[...]).astype(o_ref.dtype)
        else:
            x_ref, g_ref, r_ref, o_ref = refs
            o_ref[...] = (r_ref[...] + _rms(x_ref[...], g_ref[...])).astype(o_ref.dtype)

    args = [x, g.reshape(1, d)] + ([] if resid is None else [resid])
    return pl.pallas_call(
        body, name=name, grid=(s // tr,), in_specs=[row, gsp] + ([] if resid is None else [row]),
        out_specs=row, out_shape=jax.ShapeDtypeStruct((s, d), out_dtype), compiler_params=_cparams(("parallel",)),
    )(*args)


def _norm_bwd(x, g, dy, *, name, add=None, out_dtype=F32):
    s, d = x.shape
    tr = _tile(s, 256)
    row = pl.BlockSpec((tr, d), lambda i: (i, 0))
    gsp = pl.BlockSpec((1, d), lambda i: (0, 0))

    def body(*refs):
        if add is None:
            x_ref, g_ref, dy_ref, dx_ref, dg_ref = refs
        else:
            x_ref, g_ref, dy_ref, add_ref, dx_ref, dg_ref = refs
        dx, gterm = _rms_bwd(x_ref[...], g_ref[...], dy_ref[...].astype(F32))
        if add is not None:
            dx = dx + add_ref[...]
        dx_ref[...] = dx.astype(dx_ref.dtype)

        @pl.when(pl.program_id(0) == 0)
        def _():
            dg_ref[...] = jnp.zeros_like(dg_ref)

        dg_ref[...] += jnp.sum(gterm, axis=0, keepdims=True)

    args = [x, g.reshape(1, d), dy] + ([] if add is None else [add])
    return pl.pallas_call(
        body, name=name, grid=(s // tr,), in_specs=[row, gsp, row] + ([] if add is None else [row]),
        out_specs=[row, gsp], out_shape=[jax.ShapeDtypeStruct((s, d), out_dtype), jax.ShapeDtypeStruct((1, d), F32)],
        compiler_params=_cparams(("arbitrary",)),
    )(*args)


def _loss_head(y, target):
    s, d = y.shape
    tr = _tile(s, 256)
    row = pl.BlockSpec((tr, d), lambda i: (i, 0))
    lsp = pl.BlockSpec((1, LANES), lambda i: (0, 0))

    def body(y_ref, t_ref, l_ref, dy_ref):
        e = y_ref[...] - t_ref[...]
        dy_ref[...] = e * (1.0 / d)

        @pl.when(pl.program_id(0) == 0)
        def _():
            l_ref[...] = jnp.zeros_like(l_ref)

        part = 0.5 * jnp.sum(jnp.mean(e * e, axis=-1, keepdims=True), axis=0, keepdims=True)
        l_ref[...] += jnp.broadcast_to(part, (1, LANES))

    return pl.pallas_call(
        body, name="loss_head", grid=(s // tr,), in_specs=[row, row], out_specs=[lsp, row],
        out_shape=[jax.ShapeDtypeStruct((1, LANES), F32), jax.ShapeDtypeStruct((s, d), F32)],
        compiler_params=_cparams(("arbitrary",)),
    )(y, target)


def _rope_tables(pos_col):
    s = pos_col.shape[0]
    tr = _tile(s, 512)
    f_mla = ROPE_BASE ** (-jnp.arange(ROPE_DIM // 2, dtype=F32) / (ROPE_DIM // 2))
    f_ret = ROPE_BASE ** (-jnp.arange(HEAD // 2, dtype=F32) / (HEAD // 2))
    fm = jnp.concatenate([jnp.zeros((64,), F32), f_mla, f_mla, jnp.zeros((32,), F32)]).reshape(1, LANES)
    fr = jnp.tile(jnp.concatenate([f_ret, f_ret]), 4).reshape(1, 2 * LANES)

    def body(p_ref, fm_ref, fr_ref, cm_ref, sm_ref, cr_ref, sr_ref):
        p = p_ref[...].astype(F32)
        am = p * fm_ref[...]
        ar = p * fr_ref[...]
        cm_ref[...] = jnp.cos(am)
        sm_ref[...] = jnp.sin(am)
        cr_ref[...] = jnp.cos(ar)
        sr_ref[...] = jnp.sin(ar)

    return pl.pallas_call(
        body, name="rope_tables", grid=(s // tr,),
        in_specs=[pl.BlockSpec((tr, 1), lambda i: (i, 0)), pl.BlockSpec((1, LANES), lambda i: (0, 0)),
                  pl.BlockSpec((1, 2 * LANES), lambda i: (0, 0))],
        out_specs=[pl.BlockSpec((tr, LANES), lambda i: (i, 0))] * 2 + [pl.BlockSpec((tr, 2 * LANES), lambda i: (i, 0))] * 2,
        out_shape=[jax.ShapeDtypeStruct((s, LANES), F32)] * 2 + [jax.ShapeDtypeStruct((s, 2 * LANES), F32)] * 2,
        compiler_params=_cparams(("parallel",)),
    )(pos_col, fm, fr)


def _lane(shape):
    return lax.broadcasted_iota(jnp.int32, shape, len(shape) - 1)


def _rot_mla(z):
    l = _lane(z.shape) % LANES
    n = z.shape[-1]
    return jnp.where(l < 80, -pltpu.roll(z, n - 16, 1), pltpu.roll(z, 16, 1))


def _rot_mla_t(y):
    l = _lane(y.shape) % LANES
    n = y.shape[-1]
    return jnp.where((l >= 64) & (l < 80), pltpu.roll(y, n - 16, 1),
                     jnp.where((l >= 80) & (l < 96), -pltpu.roll(y, 16, 1), 0.0))


def _rot_ret(z):
    l = _lane(z.shape) % HEAD
    n = z.shape[-1]
    return jnp.where(l < 32, -pltpu.roll(z, n - 32, 1), pltpu.roll(z, 32, 1))


def _rot_ret_t(y):
    l = _lane(y.shape) % HEAD
    n = y.shape[-1]
    return jnp.where(l < 32, pltpu.roll(y, n - 32, 1), -pltpu.roll(y, 32, 1))


def _log_sigmoid(x):
    return jnp.minimum(x, 0.0) - jnp.log1p(jnp.exp(-jnp.abs(x)))


def _fox_cum(proj, bias_row):
    s = proj.shape[0]
    nb = s // TQ

    def body(x_ref, b_ref, cc_ref, cr_ref, carry_ref):
        @pl.when(pl.program_id(0) == 0)
        def _():
            carry_ref[...] = jnp.zeros_like(carry_ref)

        ls = _log_sigmoid(x_ref[...] + b_ref[...])
        r = lax.broadcasted_iota(jnp.int32, (TQ, TQ), 0)
        c = lax.broadcasted_iota(jnp.int32, (TQ, TQ), 1)
        tri = (c <= r).astype(F32)
        cum = _dot_exact(tri, ls) + carry_ref[...]
        carry_ref[...] = cum[TQ - 1:TQ, :]
        cc_ref[...] = cum
        cr_ref[...] = cum.T[0:8, :]

    return pl.pallas_call(
        body, name="fox_cum", grid=(nb,),
        in_specs=[pl.BlockSpec((TQ, LANES), lambda i: (i, OFF_MISC)), pl.BlockSpec((1, LANES), lambda i: (0, 0))],
        out_specs=[pl.BlockSpec((TQ, LANES), lambda i: (i, 0)), pl.BlockSpec((8, TQ), lambda i: (0, i))],
        out_shape=[jax.ShapeDtypeStruct((s, LANES), F32), jax.ShapeDtypeStruct((8, s), F32)],
        scratch_shapes=[pltpu.VMEM((1, LANES), F32)],
        compiler_params=_cparams(("arbitrary",)),
    )(proj, bias_row)


def _fox_gate_bwd(dck, drs, proj, bias_row, dkr):
    s = proj.shape[0]
    nb = s // TQ

    def body(d_ref, r_ref, x_ref, b_ref, k_ref, o_ref, db_ref, carry_ref):
        @pl.when(pl.program_id(0) == 0)
        def _():
            carry_ref[...] = jnp.zeros_like(carry_ref)
            db_ref[...] = jnp.zeros_like(db_ref)

        rows = jnp.concatenate([d_ref[0], d_ref[1], jnp.zeros((TQ - 16, TQ), F32)], axis=0)
        t = rows.T
        l = _lane((TQ, LANES))
        r0, r1 = r_ref[0], r_ref[1]
        rsum = jnp.where(l == 0, r0[:, 0:1], jnp.where(l == 1, r0[:, HEAD:HEAD + 1],
                         jnp.where(l == 2, r1[:, 0:1], jnp.where(l == 3, r1[:, HEAD:HEAD + 1], 0.0))))
        dcum = rsum - jnp.where(l < 2, t, pltpu.roll(t, LANES - 6, 1))
        r = lax.broadcasted_iota(jnp.int32, (TQ, TQ), 0)
        c = lax.broadcasted_iota(jnp.int32, (TQ, TQ), 1)
        triu = (c >= r).astype(F32)
        rc = _dot_exact(triu, dcum) + carry_ref[...]
        carry_ref[...] = rc[0:1, :]
        f = x_ref[...] + b_ref[...]
        sig_neg = 1.0 / (1.0 + jnp.exp(f))
        df = jnp.where(l < N_HEADS, rc * sig_neg, 0.0)
        db_ref[...] += jnp.sum(df, axis=0, keepdims=True)
        o_ref[...] = (df + k_ref[...]).astype(o_ref.dtype)

    rev = lambda i: nb - 1 - i
    return pl.pallas_call(
        body, name="fox_gate_bwd", grid=(nb,),
        in_specs=[pl.BlockSpec((2, 8, TQ), lambda i: (0, 0, rev(i))), pl.BlockSpec((2, TQ, LANES), lambda i: (0, rev(i), 0)),
                  pl.BlockSpec((TQ, LANES), lambda i: (rev(i), OFF_MISC)),
                  pl.BlockSpec((1, LANES), lambda i: (0, 0)), pl.BlockSpec((TQ, LANES), lambda i: (rev(i), 0))],
        out_specs=[pl.BlockSpec((TQ, LANES), lambda i: (rev(i), 0)), pl.BlockSpec((1, LANES), lambda i: (0, 0))],
        out_shape=[jax.ShapeDtypeStruct((s, LANES), BF16), jax.ShapeDtypeStruct((1, LANES), F32)],
        scratch_shapes=[pltpu.VMEM((1, LANES), F32)],
        compiler_params=_cparams(("arbitrary",)),
    )(dck, drs, proj, bias_row, dkr)


def _mla_prep(proj, cos_m, sin_m, g_q, g_kv, wq, wk, wv):
    s = proj.shape[0]
    tr = _tile(s, 256)

    def body(cq_ref, ckv_ref, misc_ref, cos_ref, sin_ref, gq_ref, gkv_ref, wq_ref, wk_ref, wv_ref,
             q_ref, k_ref, v_ref, cqn_ref, ckvn_ref):
        cos4 = jnp.tile(cos_ref[...], (1, 4))
        sin4 = jnp.tile(sin_ref[...], (1, 4))
        cqn = _rms(cq_ref[...], gq_ref[...]).astype(BF16)
        ckvn = _rms(ckv_ref[...], gkv_ref[...]).astype(BF16)
        cqn_ref[...] = cqn
        ckvn_ref[...] = ckvn
        zq = _dot(cqn, wq_ref[...])
        q_ref[...] = (zq * cos4 + _rot_mla(zq) * sin4).astype(BF16)
        l = _lane((tr, LANES))
        kr = jnp.where((l >= KR_LANE) & (l < KR_LANE + ROPE_DIM), misc_ref[...], 0.0)
        zk = _dot(ckvn, wk_ref[...]) + jnp.tile(kr, (1, 4))
        k_ref[...] = (zk * cos4 + _rot_mla(zk) * sin4).astype(BF16)
        v_ref[...] = _dot(ckvn, wv_ref[...]).astype(BF16)

    full = lambda a: pl.BlockSpec(a.shape, lambda i: (0, 0))
    rowb = lambda w: pl.BlockSpec((tr, w), lambda i: (i, 0))
    gq2, gkv2 = g_q.reshape(1, Q_RANK), g_kv.reshape(1, KV_RANK)
    return pl.pallas_call(
        body, name="mla_prep", grid=(s // tr,),
        in_specs=[pl.BlockSpec((tr, 256), lambda i: (i, OFF_CQ // 2)), pl.BlockSpec((tr, LANES), lambda i: (i, OFF_CKV)),
                  pl.BlockSpec((tr, LANES), lambda i: (i, OFF_MISC)), rowb(LANES), rowb(LANES),
                  full(gq2), full(gkv2), full(wq), full(wk), full(wv)],
        out_specs=[rowb(512), rowb(512), rowb(512), rowb(256), rowb(128)],
        out_shape=[jax.ShapeDtypeStruct((s, 512), BF16), jax.ShapeDtypeStruct((s, 512), BF16), jax.ShapeDtypeStruct((s, 512), BF16),
                   jax.ShapeDtypeStruct((s, 256), BF16), jax.ShapeDtypeStruct((s, 128), BF16)],
        compiler_params=_cparams(("parallel",)),
    )(proj, proj, proj, cos_m, sin_m, gq2, gkv2, wq, wk, wv)


def _mla_prep_bwd(dq, dk, dv, proj, cqn, ckvn, cos_m, sin_m, g_q, g_kv, wq, wk, wv):
    s = proj.shape[0]
    tr = _tile(s, 256)

    def body(dq_ref, dk_ref, dv_ref, cq_ref, ckv_ref, cqn_ref, ckvn_ref, cos_ref, sin_ref, gq_ref, gkv_ref,
             wq_ref, wk_ref, wv_ref, dcq_ref, dckv_ref, dkr_ref, dwq_ref, dwk_ref, dwv_ref, dgq_ref, dgkv_ref):
        @pl.when(pl.program_id(0) == 0)
        def _():
            for r in (dwq_ref, dwk_ref, dwv_ref, dgq_ref, dgkv_ref):
                r[...] = jnp.zeros_like(r)

        cos4 = jnp.tile(cos_ref[...], (1, 4))
        sin4 = jnp.tile(sin_ref[...], (1, 4))
        dqv = dq_ref[...]
        dzq = dqv * cos4 + _rot_mla_t(dqv * sin4)
        dkv_ = dk_ref[...]
        dzk = dkv_ * cos4 + _rot_mla_t(dkv_ * sin4)
        l = _lane((tr, LANES))
        in_rope = (l >= KR_LANE) & (l < KR_LANE + ROPE_DIM)
        dkr = dzk[:, 0:128] + dzk[:, 128:256] + dzk[:, 256:384] + dzk[:, 384:512]
        dkr_ref[...] = jnp.where(in_rope, dkr, 0.0)
        dzq_b = dzq.astype(BF16)
        dzk_b = dzk.astype(BF16)
        dv_b = dv_ref[...].astype(BF16)
        dcqn = _dot_nt(dzq_b, wq_ref[...])
        dckvn = _dot_nt(dzk_b, wk_ref[...]) + _dot_nt(dv_b, wv_ref[...])
        dwq_ref[...] += _dot_tn(cqn_ref[...], dzq_b)
        dwk_ref[...] += _dot_tn(ckvn_ref[...], dzk_b)
        dwv_ref[...] += _dot_tn(ckvn_ref[...], dv_b)
        dcq, gq_term = _rms_bwd(cq_ref[...], gq_ref[...], dcqn)
        dckv, gkv_term = _rms_bwd(ckv_ref[...], gkv_ref[...], dckvn)
        dcq_ref[...] = dcq.astype(BF16)
        dckv_ref[...] = dckv.astype(BF16)
        dgq_ref[...] += jnp.sum(gq_term, axis=0, keepdims=True)
        dgkv_ref[...] += jnp.sum(gkv_term, axis=0, keepdims=True)

    full = lambda shp: pl.BlockSpec(shp, lambda i: (0, 0))
    rowb = lambda w: pl.BlockSpec((tr, w), lambda i: (i, 0))
    gq2, gkv2 = g_q.reshape(1, Q_RANK), g_kv.reshape(1, KV_RANK)
    return pl.pallas_call(
        body, name="mla_prep_bwd", grid=(s // tr,),
        in_specs=[rowb(512), rowb(512), rowb(512),
                  pl.BlockSpec((tr, 256), lambda i: (i, OFF_CQ // 2)), pl.BlockSpec((tr, LANES), lambda i: (i, OFF_CKV)),
                  rowb(256), rowb(128), rowb(LANES), rowb(LANES), full((1, Q_RANK)), full((1, KV_RANK)),
                  full(wq.shape), full(wk.shape), full(wv.shape)],
        out_specs=[rowb(256), rowb(128), rowb(128), full(wq.shape), full(wk.shape), full(wv.shape),
                   full((1, Q_RANK)), full((1, KV_RANK))],
        out_shape=[jax.ShapeDtypeStruct((s, 256), BF16), jax.ShapeDtypeStruct((s, 128), BF16), jax.ShapeDtypeStruct((s, 128), F32),
                   jax.ShapeDtypeStruct(wq.shape, F32), jax.ShapeDtypeStruct(wk.shape, F32), jax.ShapeDtypeStruct(wv.shape, F32),
                   jax.ShapeDtypeStruct((1, Q_RANK), F32), jax.ShapeDtypeStruct((1, KV_RANK), F32)],
        compiler_params=_cparams(("arbitrary",)),
    )(dq, dk, dv, proj, proj, cqn, ckvn, cos_m, sin_m, gq2, gkv2, wq, wk, wv)


def _ret_prep(proj, cos_r, sin_r):
    s = proj.shape[0]
    tr = _tile(s, 256)

    def body(q_ref, k_ref, cos_ref, sin_ref, qo_ref, ko_ref):
        cos, sin = cos_ref[...], sin_ref[...]
        q, k = q_ref[...], k_ref[...]
        qo_ref[...] = (q * cos + _rot_ret(q) * sin).astype(BF16)
        ko_ref[...] = ((k * cos + _rot_ret(k) * sin) * (HEAD ** -0.5)).astype(BF16)

    rowb = pl.BlockSpec((tr, 256), lambda i: (i, 0))
    return pl.pallas_call(
        body, name="ret_prep", grid=(s // tr,),
        in_specs=[pl.BlockSpec((tr, 256), lambda i: (i, OFF_RQ // 2)), pl.BlockSpec((tr, 256), lambda i: (i, OFF_RK // 2)), rowb, rowb],
        out_specs=[rowb, rowb], out_shape=[jax.ShapeDtypeStruct((s, 256), BF16)] * 2,
        compiler_params=_cparams(("parallel",)),
    )(proj, proj, cos_r, sin_r)


def _ret_prep_bwd(dq, dk, cos_r, sin_r):
    s = dq.shape[0]
    tr = _tile(s, 256)

    def body(dq_ref, dk_ref, cos_ref, sin_ref, qo_ref, ko_ref):
        cos, sin = cos_ref[...], sin_ref[...]
        q, k = dq_ref[...], dk_ref[...] * (HEAD ** -0.5)
        qo_ref[...] = (q * cos + _rot_ret_t(q * sin)).astype(BF16)
        ko_ref[...] = (k * cos + _rot_ret_t(k * sin)).astype(BF16)

    rowb = pl.BlockSpec((tr, 256), lambda i: (i, 0))
    return pl.pallas_call(
        body, name="ret_prep_bwd", grid=(s // tr,), in_specs=[rowb] * 4, out_specs=[rowb, rowb],
        out_shape=[jax.ShapeDtypeStruct((s, 256), BF16)] * 2, compiler_params=_cparams(("parallel",)),
    )(dq, dk, cos_r, sin_r)


_LOG_GAMMA = [float(np.log1p(-np.float32(2.0) ** np.float32(-5.0 - h))) for h in range(N_HEADS)]
_MLA_SCALE = float((HEAD + ROPE_DIM) ** -0.5)
_QK_SCALE = float(HEAD ** -0.5)
KEY_BLOCKS = 4


def _split2(x):
    h = x.astype(BF16)
    return h, (x - h.astype(F32)).astype(BF16)


def _dot2(x, u):
    h, lo = _split2(x)
    return _dot(h, u) + _dot(lo, u)


def _head_pick(block, head, axis):
    idx = lax.broadcasted_iota(jnp.int32, block.shape, axis)
    return jnp.sum(jnp.where(idx == head, block, 0.0), axis=axis, keepdims=True)


def _log_gamma_of(head):
    lg = jnp.float32(_LOG_GAMMA[3])
    for h in (2, 1, 0):
        lg = jnp.where(head == h, jnp.float32(_LOG_GAMMA[h]), lg)
    return lg


def _mixer_specs(mode, s, q_off, k_off, v_off):
    nhb = 1 if mode == "mla" else 2
    nsub = KEY_BLOCKS if (s // TQ) % KEY_BLOCKS == 0 else 1
    q_spec = pl.BlockSpec((TQ, LANES), lambda p, i: (i, q_off + p))
    k_spec = pl.BlockSpec((s, LANES), lambda p, i: (0, k_off + p))
    v_spec = pl.BlockSpec((s, LANES), lambda p, i: (0, v_off + p))
    return nhb, N_HEADS // nhb, nsub, q_spec, k_spec, v_spec


def _mixer_geometry(mode, i, nsub):
    w = TQ * nsub
    row = lax.broadcasted_iota(jnp.int32, (TQ, w), 0)
    col = lax.broadcasted_iota(jnp.int32, (TQ, w), 1)
    nfull = i // nsub
    dist = col - row
    if mode in ("fox", "sb"):
        rel = dist
    else:
        rel = col - (row | (CHUNK - 1))

    def visible(c):
        off = c * w - i * TQ
        return (rel + off) < 0 if mode == "sb" else (rel + off) <= 0

    return nfull, dist, visible


def _mixer_fwd(mode, qa, q_off, ka, k_off, va, v_off, *, cum_col=None, cum_row=None):
    s = qa.shape[0]
    nq = s // TQ
    nhb, nblk, nsub, q_spec, k_spec, v_spec = _mixer_specs(mode, s, q_off, k_off, v_off)
    w = TQ * nsub
    softmax = mode in ("fox", "mla")
    has_stat = mode != "ret"

    def body(*refs):
        refs = list(refs)
        q_ref, k_ref, v_ref = refs[:3]
        refs = refs[3:]
        if mode == "fox":
            cc_ref, cr_ref = refs[:2]
            refs = refs[2:]
        o_ref = refs[0]
        st_ref = refs[1] if has_stat else None
        p = pl.program_id(0)
        i = pl.program_id(1)
        nfull, dist, visible = _mixer_geometry(mode, i, nsub)
        lane = _lane((1, LANES))
        heads = [nhb * p + hh for hh in range(nhb)]
        if nhb == 1:
            qs = [q_ref[...]]
        else:
            qf = q_ref[...].astype(F32)
            qs = [jnp.where((lane // HEAD) == hh, qf, 0.0).astype(BF16) for hh in range(nhb)]
        if mode == "fox":
            cqs = [_head_pick(cc_ref[...], h, 1) for h in heads]
        if mode == "ret":
            lgs = [_log_gamma_of(h) for h in heads]
            distf = dist.astype(F32)
        if mode == "sb":
            r1 = lax.broadcasted_iota(jnp.int32, (TQ, TQ), 0)
            c1 = lax.broadcasted_iota(jnp.int32, (TQ, TQ), 1)
            u_after = (r1 > c1).astype(BF16)

        def chunk(c):
            return pl.ds(pl.multiple_of(c * w, w), w)

        def scores(c):
            kj = k_ref[chunk(c), :]
            return tuple(_dot_nt(qs[hh], kj) for hh in range(nhb))

        def head_step(hh, c, js, sc, vj, carry, last):
            if softmax:
                m, l, acc = carry
                if mode == "fox":
                    ck = _head_pick(cr_ref[:, js], heads[hh], 0)
                    sc = sc * _QK_SCALE + (cqs[hh] - ck)
                else:
                    sc = sc * _MLA_SCALE
                if last:
                    sc = jnp.where(visible(c), sc, NEG)
                m_new = jnp.maximum(m, jnp.max(sc, axis=-1, keepdims=True))
                alpha = jnp.exp(m - m_new)
                pr = jnp.exp(sc - m_new)
                l = alpha * l + jnp.sum(pr, axis=-1, keepdims=True)
                acc = alpha * acc + _dot(pr.astype(BF16), vj)
                return m_new, l, acc
            if mode == "ret":
                off = (i * TQ - c * w).astype(F32)
                if last:
                    dec = jnp.where(visible(c), jnp.exp(lgs[hh] * jnp.abs(off - distf)), 0.0)
                else:
                    dec = jnp.exp(lgs[hh] * (off - distf))
                return carry + _dot((sc * dec).astype(BF16), vj)
            run, acc = carry
            z = sc * _QK_SCALE
            lp = jnp.log1p(jnp.exp(-jnp.abs(z)))
            log_stay = jnp.minimum(-z, 0.0) - lp
            if last:
                vis = visible(c)
                log_stay = jnp.where(vis, log_stay, 0.0)
            parts = [None] * nsub
            for b in reversed(range(nsub)):
                ls_b = log_stay[:, b * TQ:(b + 1) * TQ]
                parts[b] = _dot2(ls_b, u_after) + run
                run = run + jnp.sum(ls_b, axis=-1, keepdims=True)
            later = parts[0] if nsub == 1 else jnp.concatenate(parts, axis=1)
            wgt = jnp.exp(jnp.minimum(z, 0.0) - lp + later)
            if last:
                wgt = jnp.where(vis, wgt, 0.0)
            return run, acc + _dot(wgt.astype(BF16), vj)

        def step(c, c_next, state, last):
            scs, carries = state
            nxt = scores(c_next) if c_next is not None else None
            js = chunk(c)
            vj = v_ref[js, :]
            return nxt, tuple(head_step(hh, c, js, scs[hh], vj, carries[hh], last) for hh in range(nhb))

        zero_acc = jnp.zeros((TQ, LANES), F32)
        zero1 = jnp.zeros((TQ, 1), F32)
        if softmax:
            init = tuple((jnp.full((TQ, 1), NEG, F32), zero1, zero_acc) for _ in range(nhb))
        elif mode == "ret":
            init = tuple(zero_acc for _ in range(nhb))
        else:
            init = tuple((zero1, zero_acc) for _ in range(nhb))
        if mode == "sb":
            state = step(nfull, jnp.maximum(nfull - 1, 0), (scores(nfull), init), True)
            _, carries = lax.fori_loop(0, nfull, lambda t, st: step(nfull - 1 - t, jnp.maximum(nfull - 2 - t, 0), st, False), state)
        else:
            state = lax.fori_loop(0, nfull, lambda c, st: step(c, c + 1, st, False), (scores(0), init))
            _, carries = step(nfull, None, state, True)
        if softmax:
            outs = [acc / l for (m, l, acc) in carries]
            stats = [m + jnp.log(l) for (m, l, acc) in carries]
        elif mode == "ret":
            outs, stats = list(carries), None
        else:
            outs, stats = [acc for (run, acc) in carries], [run for (run, acc) in carries]
        hm0 = (lane // HEAD) == 0
        pick = (lambda a: jnp.broadcast_to(a[0], (TQ, LANES))) if nhb == 1 else (lambda a: jnp.where(hm0, a[0], a[1]))
        o_ref[...] = pick(outs)
        if has_stat:
            st_ref[0] = pick(stats)

    in_specs = [q_spec, k_spec, v_spec]
    args = [qa, ka, va]
    if mode == "fox":
        in_specs += [pl.BlockSpec((TQ, LANES), lambda p, i: (i, 0)), pl.BlockSpec((8, s), lambda p, i: (0, 0))]
        args += [cum_col, cum_row]
    out_specs = [pl.BlockSpec((TQ, LANES), lambda p, i: (i, p))]
    out_shape = [jax.ShapeDtypeStruct((s, nblk * LANES), F32)]
    if has_stat:
        out_specs.append(pl.BlockSpec((1, TQ, LANES), lambda p, i: (p, i, 0)))
        out_shape.append(jax.ShapeDtypeStruct((nblk, s, LANES), F32))
    res = pl.pallas_call(
        body, name=mode + "_fwd", grid=(nblk, nq), in_specs=in_specs, out_specs=out_specs, out_shape=out_shape,
        compiler_params=_cparams(("parallel", "parallel")),
    )(*args)
    return res if has_stat else (res[0], None)


def _mixer_bwd(mode, qa, q_off, ka, k_off, va, v_off, o, do, *, stat=None, cum_col=None, cum_row=None):
    s = qa.shape[0]
    nq = s // TQ
    nhb, nblk, nsub, q_spec, k_spec, v_spec = _mixer_specs(mode, s, q_off, k_off, v_off)
    w = TQ * nsub
    softmax = mode in ("fox", "mla")
    has_stat = mode != "ret"

    def body(*refs):
        refs = list(refs)
        q_ref, k_ref, v_ref, o_ref, do_ref = refs[:5]
        refs = refs[5:]
        if has_stat:
            st_ref = refs[0]
            refs = refs[1:]
        if mode == "fox":
            cc_ref, cr_ref = refs[:2]
            refs = refs[2:]
        dq_ref, dk_ref, dv_ref = refs[:3]
        dck_ref, drs_ref = refs[3:5] if mode == "fox" else (None, None)
        p = pl.program_id(0)
        i = pl.program_id(1)

        @pl.when(i == 0)
        def _():
            dk_ref[...] = jnp.zeros_like(dk_ref)
            dv_ref[...] = jnp.zeros_like(dv_ref)
            if mode == "fox":
                dck_ref[...] = jnp.zeros_like(dck_ref)

        nfull, dist, visible = _mixer_geometry(mode, i, nsub)
        lane = _lane((1, LANES))
        heads = [nhb * p + hh for hh in range(nhb)]
        dov = do_ref[...]
        if nhb == 1:
            qs = [q_ref[...]]
            dos = [dov.astype(BF16)]
            deltas = [jnp.sum(dov * o_ref[...], axis=-1, keepdims=True)]
        else:
            qf = q_ref[...].astype(F32)
            prod = dov * o_ref[...]
            hms = [(lane // HEAD) == hh for hh in range(nhb)]
            qs = [jnp.where(hm, qf, 0.0).astype(BF16) for hm in hms]
            dos = [jnp.where(hm, dov, 0.0).astype(BF16) for hm in hms]
            deltas = [jnp.sum(jnp.where(hm, prod, 0.0), axis=-1, keepdims=True) for hm in hms]
        if has_stat:
            st = st_ref[0]
            stats = [st[:, hh * HEAD:hh * HEAD + 1] for hh in range(nhb)]
        if mode == "fox":
            cqs = [_head_pick(cc_ref[...], h, 1) for h in heads]
        if mode == "ret":
            lgs = [_log_gamma_of(h) for h in heads]
            distf = dist.astype(F32)
        if mode == "sb":
            r1 = lax.broadcasted_iota(jnp.int32, (TQ, TQ), 0)
            c1 = lax.broadcasted_iota(jnp.int32, (TQ, TQ), 1)
            u_upto = (r1 <= c1).astype(BF16)
            u_before = (r1 < c1).astype(BF16)

        def chunk(c):
            return pl.ds(pl.multiple_of(c * w, w), w)

        def scores(c):
            js = chunk(c)
            kj, vj = k_ref[js, :], v_ref[js, :]
            return tuple((_dot_nt(qs[hh], kj), _dot_nt(dos[hh], vj)) for hh in range(nhb))

        def emit(hh, js, ds_b, pr_b, dq):
            dk_ref[js, :] += _dot_tn(ds_b, qs[hh])
            dv_ref[js, :] += _dot_tn(pr_b, dos[hh])
            return dq + _dot(ds_b, k_ref[js, :])

        def head_step(hh, c, js, sc_dp, carry, last):
            sc, dp = sc_dp
            if softmax:
                dq, rsum = carry
                scale = _QK_SCALE if mode == "fox" else _MLA_SCALE
                if mode == "fox":
                    ck = _head_pick(cr_ref[:, js], heads[hh], 0)
                    sc = sc * scale + (cqs[hh] - ck)
                else:
                    sc = sc * scale
                if last:
                    sc = jnp.where(visible(c), sc, NEG)
                pr = jnp.exp(sc - stats[hh])
                ds = pr * (dp - deltas[hh])
                if mode == "fox":
                    dck_ref[0, hh:hh + 1, js] += jnp.sum(ds, axis=0, keepdims=True)
                    rsum = rsum + jnp.sum(ds, axis=-1, keepdims=True)
                return emit(hh, js, (ds * scale).astype(BF16), pr.astype(BF16), dq), rsum
            if mode == "ret":
                off = (i * TQ - c * w).astype(F32)
                if last:
                    dec = jnp.where(visible(c), jnp.exp(lgs[hh] * jnp.abs(off - distf)), 0.0)
                else:
                    dec = jnp.exp(lgs[hh] * (off - distf))
                return emit(hh, js, (dp * dec).astype(BF16), (sc * dec).astype(BF16), carry)
            seen, gsum, dq = carry
            z = sc * _QK_SCALE
            lp = jnp.log1p(jnp.exp(-jnp.abs(z)))
            log_stay = jnp.minimum(-z, 0.0) - lp
            if last:
                vis = visible(c)
                log_stay = jnp.where(vis, log_stay, 0.0)
            parts = []
            for b in range(nsub):
                ls_b = log_stay[:, b * TQ:(b + 1) * TQ]
                parts.append((stats[hh] - seen) - _dot2(ls_b, u_upto))
                seen = seen + jnp.sum(ls_b, axis=-1, keepdims=True)
            later = parts[0] if nsub == 1 else jnp.concatenate(parts, axis=1)
            log_beta = jnp.minimum(z, 0.0) - lp
            wgt = jnp.exp(log_beta + later)
            if last:
                wgt = jnp.where(vis, wgt, 0.0)
            g = dp * wgt
            parts = []
            for b in range(nsub):
                g_b = g[:, b * TQ:(b + 1) * TQ]
                parts.append(gsum + _dot2(g_b, u_before))
                gsum = gsum + jnp.sum(g_b, axis=-1, keepdims=True)
            before = parts[0] if nsub == 1 else jnp.concatenate(parts, axis=1)
            beta = jnp.exp(log_beta)
            dz = g * (1.0 - beta) - beta * before
            if last:
                dz = jnp.where(vis, dz, 0.0)
            return seen, gsum, emit(hh, js, (dz * _QK_SCALE).astype(BF16), wgt.astype(BF16), dq)

        def step(c, c_next, state, last):
            scs, carries = state
            nxt = scores(c_next) if c_next is not None else None
            js = chunk(c)
            return nxt, tuple(head_step(hh, c, js, scs[hh], carries[hh], last) for hh in range(nhb))

        zero_acc = jnp.zeros((TQ, LANES), F32)
        zero1 = jnp.zeros((TQ, 1), F32)
        if softmax:
            init = tuple((zero_acc, zero1) for _ in range(nhb))
        elif mode == "ret":
            init = tuple(zero_acc for _ in range(nhb))
        else:
            init = tuple((zero1, zero1, zero_acc) for _ in range(nhb))
        state = lax.fori_loop(0, nfull, lambda c, st: step(c, c + 1, st, False), (scores(0), init))
        _, carries = step(nfull, None, state, True)
        if softmax:
            dqs = [dq for (dq, rsum) in carries]
        elif mode == "ret":
            dqs = list(carries)
        else:
            dqs = [dq for (seen, gsum, dq) in carries]
        hm0 = (lane // HEAD) == 0
        dq_ref[...] = dqs[0] if nhb == 1 else jnp.where(hm0, dqs[0], dqs[1])
        if mode == "fox":
            drs_ref[0] = jnp.where(hm0, carries[0][1], carries[1][1])

    pair_blk = pl.BlockSpec((TQ, LANES), lambda p, i: (i, p))
    full_blk = pl.BlockSpec((s, LANES), lambda p, i: (0, p))
    stat_blk = pl.BlockSpec((1, TQ, LANES), lambda p, i: (p, i, 0))
    in_specs = [q_spec, k_spec, v_spec, pair_blk, pair_blk]
    args = [qa, ka, va, o, do]
    if has_stat:
        in_specs.append(stat_blk)
        args.append(stat)
    if mode == "fox":
        in_specs += [pl.BlockSpec((TQ, LANES), lambda p, i: (i, 0)), pl.BlockSpec((8, s), lambda p, i: (0, 0))]
        args += [cum_col, cum_row]
    out_specs = [pair_blk, full_blk, full_blk]
    out_shape = [jax.ShapeDtypeStruct((s, nblk * LANES), F32)] * 3
    if mode == "fox":
        out_specs += [pl.BlockSpec((1, 8, s), lambda p, i: (p, 0, 0)), stat_blk]
        out_shape += [jax.ShapeDtypeStruct((2, 8, s), F32), jax.ShapeDtypeStruct((2, s, LANES), F32)]
    return pl.pallas_call(
        body, name=mode + "_bwd", grid=(nblk, nq), in_specs=in_specs, out_specs=out_specs, out_shape=out_shape,
        compiler_params=_cparams(("parallel", "arbitrary")),
    )(*args)


def _seg_mean_matrix():
    r = lax.broadcasted_iota(jnp.int32, (GROUP, GROUP), 0)
    c = lax.broadcasted_iota(jnp.int32, (GROUP, GROUP), 1)
    return jnp.where((r // HEAD) == (c // HEAD), 1.0 / HEAD, 0.0).astype(F32)


def _sigmoid(x):
    return 1.0 / (1.0 + jnp.exp(-x))


def _mix_post(oa, ob, oc, od, proj, g):
    s = oa.shape[0]
    tr = _tile(s, 256)

    def body(a_ref, b_ref, c_ref, d_ref, rg_ref, g_ref, o_ref):
        gv = g_ref[...]
        o_ref[:, 0:GROUP] = _rms(a_ref[...], gv[:, 0:GROUP]).astype(BF16)
        o_ref[:, GROUP:2 * GROUP] = _rms(b_ref[...], gv[:, GROUP:2 * GROUP]).astype(BF16)
        seg = _seg_mean_matrix()
        c = c_ref[...]
        cen = c - _dot_exact(c, seg)
        n = cen * lax.rsqrt(_dot_exact(cen * cen, seg) + EPS)
        rg = rg_ref[...]
        o_ref[:, 2 * GROUP:3 * GROUP] = (n * gv[:, 2 * GROUP:3 * GROUP] * (rg * _sigmoid(rg))).astype(BF16)
        o_ref[:, 3 * GROUP:] = _rms(d_ref[...], gv[:, 3 * GROUP:]).astype(BF16)

    blk = pl.BlockSpec((tr, GROUP), lambda i: (i, 0))
    return pl.pallas_call(
        body, name="mix_post", grid=(s // tr,),
        in_specs=[blk] * 4 + [pl.BlockSpec((tr, GROUP), lambda i: (i, OFF_RG // 2)), pl.BlockSpec((1, D_MODEL), lambda i: (0, 0))],
        out_specs=pl.BlockSpec((tr, D_MODEL), lambda i: (i, 0)), out_shape=jax.ShapeDtypeStruct((s, D_MODEL), BF16),
        compiler_params=_cparams(("parallel",)),
    )(oa, ob, oc, od, proj, g.reshape(1, D_MODEL))


def _mix_post_bwd(dmixed, oa, ob, oc, od, proj, g):
    s = oa.shape[0]
    tr = _tile(s, 256)

    def body(dm_ref, a_ref, b_ref, c_ref, d_ref, rg_ref, g_ref, da_ref, db_ref, dc_ref, dd_ref, drg_ref, dg_ref):
        @pl.when(pl.program_id(0) == 0)
        def _():
            dg_ref[...] = jnp.zeros_like(dg_ref)

        gv = g_ref[...]
        dm = dm_ref[...]
        for k, (x_ref, dx_ref) in enumerate(((a_ref, da_ref), (b_ref, db_ref), (None, None), (d_ref, dd_ref))):
            if x_ref is None:
                continue
            cols = slice(k * GROUP, (k + 1) * GROUP)
            dx, gterm = _rms_bwd(x_ref[...], gv[:, cols], dm[:, cols])
            dx_ref[...] = dx
            dg_ref[:, cols] += jnp.sum(gterm, axis=0, keepdims=True)
        cols = slice(2 * GROUP, 3 * GROUP)
        seg = _seg_mean_matrix()
        c = c_ref[...]
        cen = c - _dot_exact(c, seg)
        rstd = lax.rsqrt(_dot_exact(cen * cen, seg) + EPS)
        n = cen * rstd
        rg = rg_ref[...]
        sg = _sigmoid(rg)
        gate = rg * sg
        dy = dm[:, cols]
        gc = gv[:, cols]
        dn = dy * gc * gate
        dg_ref[:, cols] += jnp.sum(dy * n * gate, axis=0, keepdims=True)
        drg_ref[...] = (dy * n * gc * (sg * (1.0 + rg * (1.0 - sg)))).astype(BF16)
        dc_ref[...] = rstd * (dn - _dot_exact(dn, seg) - n * _dot_exact(dn * n, seg))

    blk = pl.BlockSpec((tr, GROUP), lambda i: (i, 0))
    gsp = pl.BlockSpec((1, D_MODEL), lambda i: (0, 0))
    return pl.pallas_call(
        body, name="mix_post_bwd", grid=(s // tr,),
        in_specs=[pl.BlockSpec((tr, D_MODEL), lambda i: (i, 0))] + [blk] * 4 + [pl.BlockSpec((tr, GROUP), lambda i: (i, OFF_RG // 2)), gsp],
        out_specs=[blk] * 5 + [gsp],
        out_shape=[jax.ShapeDtypeStruct((s, GROUP), F32)] * 4 + [jax.ShapeDtypeStruct((s, GROUP), BF16), jax.ShapeDtypeStruct((1, D_MODEL), F32)],
        compiler_params=_cparams(("arbitrary",)),
    )(dmixed, oa, ob, oc, od, proj, g.reshape(1, D_MODEL))


def _pack_w_in(w):
    z = lambda n: jnp.zeros((w.shape[0], n), w.dtype)
    misc = jnp.concatenate([w[:, 768:772], z(KR_LANE - N_HEADS), w[:, 1156:1188], z(LANES - KR_LANE - ROPE_DIM)], axis=1)
    return jnp.concatenate([w[:, 0:768], w[:, 772:1028], w[:, 1188:2980], w[:, 1028:1156], misc], axis=1)


def _unpack_dw_in(d):
    m = OFF_MISC * LANES
    return jnp.concatenate([d[:, 0:768], d[:, m:m + N_HEADS], d[:, 768:1024], d[:, OFF_CKV * LANES:m],
                            d[:, m + KR_LANE:m + KR_LANE + ROPE_DIM], d[:, 1024:OFF_CKV * LANES]], axis=1)


def _pack_w_q(w):
    return jnp.pad(w.reshape(Q_RANK, N_HEADS, HEAD + ROPE_DIM), ((0, 0), (0, 0), (0, LANES - HEAD - ROPE_DIM))).reshape(Q_RANK, 4 * LANES)


def _unpack_dw_q(d):
    return d.reshape(Q_RANK, N_HEADS, LANES)[:, :, :HEAD + ROPE_DIM].reshape(Q_RANK, N_HEADS * (HEAD + ROPE_DIM))


def _pack_w_kv(w):
    w4 = w.reshape(KV_RANK, N_HEADS, 2 * HEAD)
    widen = lambda a: jnp.pad(a, ((0, 0), (0, 0), (0, LANES - HEAD))).reshape(KV_RANK, N_HEADS * LANES)
    return widen(w4[:, :, :HEAD]), widen(w4[:, :, HEAD:])


def _unpack_dw_kv(dk, dv):
    narrow = lambda a: a.reshape(KV_RANK, N_HEADS, LANES)[:, :, :HEAD]
    return jnp.concatenate([narrow(dk), narrow(dv)], axis=2).reshape(KV_RANK, 2 * N_HEADS * HEAD)


def _narrow_heads(a):
    return a.reshape(a.shape[0], N_HEADS, LANES)[:, :, :HEAD].reshape(a.shape[0], N_HEADS * HEAD)


def _widen_heads(a):
    return jnp.pad(a.reshape(a.shape[0], N_HEADS, HEAD), ((0, 0), (0, 0), (0, LANES - HEAD))).reshape(a.shape[0], N_HEADS * LANES)


def _layer_fwd(x, lw, tabs, tag):
    cos_m, sin_m, cos_r, sin_r = tabs
    h1 = _norm_fwd(x, lw["g_mix_pre"], name=tag + "pre_norm")
    proj, projb = _matmul(h1, lw["w_in"], name=tag + "in_proj", also_bf16=True)
    bias_row = jnp.pad(lw["b_forget"], (FF_LANE, LANES - N_HEADS - FF_LANE)).reshape(1, LANES)
    cum_col, cum_row = _fox_cum(proj, bias_row)
    oa, lse_a = _mixer_fwd("fox", projb, OFF_FQ, projb, OFF_FK, projb, OFF_FV, cum_col=cum_col, cum_row=cum_row)
    qm, km, vm, cqn, ckvn = _mla_prep(proj, cos_m, sin_m, lw["g_q_lora"], lw["g_kv_lora"], lw["wq"], lw["wk"], lw["wv"])
    ob_wide, lse_b = _mixer_fwd("mla", qm, 0, km, 0, vm, 0)
    ob = _narrow_heads(ob_wide)
    qr, kr = _ret_prep(proj, cos_r, sin_r)
    oc, _ = _mixer_fwd("ret", qr, 0, kr, 0, projb, OFF_RV)
    od, tot_d = _mixer_fwd("sb", projb, OFF_SQ, projb, OFF_SK, projb, OFF_SV)
    mixed = _mix_post(oa, ob, oc, od, proj, lw["g_mix_out"])
    mix = _matmul(mixed, lw["w_out"], name=tag + "out_proj")
    x1 = _norm_fwd(mix, lw["g_mix_post"], name=tag + "mix_post_norm", resid=x, out_dtype=F32)
    h2 = _norm_fwd(x1, lw["g_ffn_pre"], name=tag + "ffn_pre_norm")
    u = _matmul(h2, lw["w_ffn_up"], name=tag + "ffn_up", relu2=True, out_dtype=BF16)
    f = _matmul(u, lw["w_ffn_down"], name=tag + "ffn_down")
    x2 = _norm_fwd(f, lw["g_ffn_post"], name=tag + "ffn_post_norm", resid=x1, out_dtype=F32)
    saved = dict(x=x, h1=h1, proj=proj, projb=projb, bias_row=bias_row, cum_col=cum_col, cum_row=cum_row, oa=oa, lse_a=lse_a,
                 qm=qm, km=km, vm=vm, cqn=cqn, ckvn=ckvn, ob=ob, ob_wide=ob_wide, lse_b=lse_b, qr=qr, kr=kr, oc=oc, od=od, tot_d=tot_d, mixed=mixed,
                 mix=mix, x1=x1, h2=h2, u=u, f=f)
    return x2, saved


def _layer_bwd(dx2, lw, sv, tabs, tag):
    cos_m, sin_m, cos_r, sin_r = tabs
    g = {}
    df, g["g_ffn_post"] = _norm_bwd(sv["f"], lw["g_ffn_post"], dx2, name=tag + "ffn_post_norm_bwd", out_dtype=BF16)
    du_pre = _matmul(df, lw["w_ffn_down"], name=tag + "ffn_down_dx", tb=True, out_dtype=BF16, relu2_of=sv["u"])
    g["w_ffn_down"] = _matmul(sv["u"], df, name=tag + "ffn_down_dw", ta=True)
    dh2 = _matmul(du_pre, lw["w_ffn_up"], name=tag + "ffn_up_dx", tb=True)
    g["w_ffn_up"] = _matmul(sv["h2"], du_pre, name=tag + "ffn_up_dw", ta=True)
    dx1, g["g_ffn_pre"] = _norm_bwd(sv["x1"], lw["g_ffn_pre"], dh2, name=tag + "ffn_pre_norm_bwd", add=dx2)
    dmix, g["g_mix_post"] = _norm_bwd(sv["mix"], lw["g_mix_post"], dx1, name=tag + "mix_post_norm_bwd", out_dtype=BF16)
    dmixed = _matmul(dmix, lw["w_out"], name=tag + "out_proj_dx", tb=True)
    g["w_out"] = _matmul(sv["mixed"], dmix, name=tag + "out_proj_dw", ta=True)
    proj, projb = sv["proj"], sv["projb"]
    doa, dob, doc, dod, drg, g["g_mix_out"] = _mix_post_bwd(dmixed, sv["oa"], sv["ob"], sv["oc"], sv["od"], proj, lw["g_mix_out"])
    dfq, dfk, dfv, dck, drs = _mixer_bwd("fox", projb, OFF_FQ, projb, OFF_FK, projb, OFF_FV, sv["oa"], doa, stat=sv["lse_a"],
                                         cum_col=sv["cum_col"], cum_row=sv["cum_row"])
    dqm, dkm, dvm = _mixer_bwd("mla", sv["qm"], 0, sv["km"], 0, sv["vm"], 0, sv["ob_wide"], _widen_heads(dob), stat=sv["lse_b"])
    dcq, dckv, dkr, dwq, dwk, dwv, g["g_q_lora"], g["g_kv_lora"] = _mla_prep_bwd(
        dqm, dkm, dvm, proj, sv["cqn"], sv["ckvn"], cos_m, sin_m, lw["g_q_lora"], lw["g_kv_lora"], lw["wq"], lw["wk"], lw["wv"])
    dqr, dkr_ret, drv = _mixer_bwd("ret", sv["qr"], 0, sv["kr"], 0, projb, OFF_RV, sv["oc"], doc)
    drq, drk = _ret_prep_bwd(dqr, dkr_ret, cos_r, sin_r)
    dsq, dsk, dsv = _mixer_bwd("sb", projb, OFF_SQ, projb, OFF_SK, projb, OFF_SV, sv["od"], dod, stat=sv["tot_d"])
    dmisc, db_row = _fox_gate_bwd(dck, drs, proj, sv["bias_row"], dkr)
    b = lambda a: a.astype(BF16)
    dproj = jnp.concatenate([b(dfq), b(dfk), b(dfv), dcq, drq, drk, b(drv), drg, b(dsq), b(dsk), b(dsv), dckv, dmisc], axis=1)
    dh1 = _matmul(dproj, lw["w_in"], name=tag + "in_proj_dx", tb=True)
    g["w_in"] = _matmul(sv["h1"], dproj, name=tag + "in_proj_dw", ta=True)
    dx, g["g_mix_pre"] = _norm_bwd(sv["x"], lw["g_mix_pre"], dh1, name=tag + "pre_norm_bwd", add=dx1)
    g["b_forget"] = db_row[0, FF_LANE:FF_LANE + N_HEADS]
    g["wq"], g["wk"], g["wv"] = dwq, dwk, dwv
    return dx, g


def _local_step(x, positions, layers, target):
    s = x.shape[0]
    tabs = _rope_tables(positions.reshape(s, 1))
    saved = []
    for li, lw in enumerate(layers):
        x, sv = _layer_fwd(x, lw, tabs, "l%d_" % li)
        saved.append(sv)
    loss_row, dx = _loss_head(x, target)
    grads = [None] * len(layers)
    for li in reversed(range(len(layers))):
        dx, grads[li] = _layer_bwd(dx, layers[li], saved[li], tabs, "l%d_" % li)
    return loss_row[0, 0], dx, grads


def _adamw(w, g, m, v, *, name):
    r, c = w.shape
    tr = 256 if r % 256 == 0 else r
    blk = pl.BlockSpec((tr, c), lambda i: (i, 0))
    c1 = 1.0 - ADAM_B1 ** ADAM_STEP
    c2 = 1.0 - ADAM_B2 ** ADAM_STEP

    def body(w_ref, g_ref, m_ref, v_ref, d_ref, mo_ref, vo_ref):
        gv = g_ref[...]
        mn = ADAM_B1 * m_ref[...] + (1.0 - ADAM_B1) * gv
        vn = ADAM_B2 * v_ref[...] + (1.0 - ADAM_B2) * jnp.square(gv)
        mo_ref[...] = mn
        vo_ref[...] = vn
        d_ref[...] = -ADAM_LR * ((mn / c1) / (jnp.sqrt(vn / c2) + ADAM_EPS) + ADAM_WD * w_ref[...])

    return pl.pallas_call(
        body, name=name, grid=(r // tr,), in_specs=[blk] * 4, out_specs=[blk] * 3,
        out_shape=[jax.ShapeDtypeStruct((r, c), F32)] * 3, compiler_params=_cparams(("parallel",)),
    )(w, g, m, v)


BIG = ("w_in", "w_q_up", "w_kv_up", "w_out", "w_ffn_up", "w_ffn_down")
SMALL = ("g_mix_pre", "b_forget", "g_q_lora", "g_kv_lora", "g_mix_out", "g_mix_post", "g_ffn_pre", "g_ffn_post")
N_CHIPS = 4
ANY = pl.BlockSpec(memory_space=pl.ANY)


def _mesh_pos():
    return lax.axis_index("x"), lax.axis_index("y"), lax.axis_index("c")


def _other_chips(x, y):
    return [(1 - x, y), (x, 1 - y), (1 - x, 1 - y)]


def _rows_half(ref, half):
    h = ref.shape[-2] // 2
    return ref.at[(slice(None),) * (len(ref.shape) - 2) + (pl.ds(half * h, h), slice(None))]


def _remote(src, dst, send_sem, recv_sem, device):
    return pltpu.make_async_remote_copy(src_ref=src, dst_ref=dst, send_sem=send_sem, recv_sem=recv_sem, device_id=device,
                                        device_id_type=MESH)


def _comm_call(body, name, args, out_shape, n_sems):
    return pl.pallas_call(
        body, name=name, in_specs=[ANY] * len(args), out_specs=[ANY] * len(out_shape), out_shape=out_shape,
        scratch_shapes=[pltpu.SemaphoreType.DMA((n_sems,)), pltpu.SemaphoreType.DMA((n_sems,))],
        compiler_params=pltpu.CompilerParams(has_side_effects=True),
    )(*args)


def _gather_weights(shards):
    n = len(shards)

    def body(*refs):
        ins, outs = refs[:n], refs[n:2 * n]
        ici_send, ici_recv, d2d_send, d2d_recv = refs[2 * n:]
        x, y, c = _mesh_pos()
        mine = 2 * x + y
        peers = _other_chips(x, y)

        def ici(t, j, block):
            px, py = peers[j]
            return _remote(_rows_half(ins[t], c), _rows_half(outs[t].at[block], c), ici_send.at[3 * t + j], ici_recv.at[3 * t + j],
                           (px, py, c))

        def d2d(t, j, block, half):
            region = _rows_half(outs[t].at[block], half)
            return _remote(region, region, d2d_send.at[3 * t + j], d2d_recv.at[3 * t + j], (x, y, 1 - c))

        sends = [ici(t, j, mine) for t in range(n) for j in range(3)]
        for cp in sends:
            cp.start()
        passed = []
        for t in range(n):
            for j, (px, py) in enumerate(peers):
                ici(t, j, 2 * px + py).wait_recv()
                fwd = d2d(t, j, 2 * px + py, c)
                fwd.start()
                passed.append(fwd)
        for t in range(n):
            for j, (px, py) in enumerate(peers):
                d2d(t, j, 2 * px + py, 1 - c).wait_recv()
        for cp in sends + passed:
            cp.wait_send()

    out_shape = [jax.ShapeDtypeStruct((N_CHIPS,) + a.shape, a.dtype) for a in shards]
    return pl.pallas_call(
        body, name="gather_weights", in_specs=[ANY] * n, out_specs=[ANY] * n, out_shape=out_shape,
        scratch_shapes=[pltpu.SemaphoreType.DMA((3 * n,))] * 4,
        compiler_params=pltpu.CompilerParams(has_side_effects=True),
    )(*shards)


def _exchange_halves(gs):
    n = len(gs)

    def body(*refs):
        ins, outs, send_sems, recv_sems = refs[:n], refs[n:2 * n], refs[2 * n], refs[2 * n + 1]
        x, y, c = _mesh_pos()
        cps = [_remote(_rows_half(ins[t], 1 - c), outs[t], send_sems.at[t], recv_sems.at[t], (x, y, 1 - c)) for t in range(n)]
        for cp in cps:
            cp.start()
        for cp in cps:
            cp.wait_recv()
        for cp in cps:
            cp.wait_send()

    out_shape = [jax.ShapeDtypeStruct(g.shape[:2] + (g.shape[2] // 2, g.shape[3]), g.dtype) for g in gs]
    return _comm_call(body, "grad_pair_exchange", gs, out_shape, n)


def _pair_add(g, r, c_idx, *, name):
    nb, d, rows, cols = g.shape
    h = rows // 2
    tr = min(h, 512)
    nt = h // tr

    def body(c_ref, g_ref, r_ref, p_ref, pb_ref):
        s = g_ref[...] + r_ref[...]
        p_ref[...] = s
        pb_ref[...] = s.astype(BF16)

    blk = pl.BlockSpec((1, 1, tr, cols), lambda k, l, i, c_ref: (k, l, i, 0))
    return pl.pallas_call(
        body, name=name,
        grid_spec=pltpu.PrefetchScalarGridSpec(
            num_scalar_prefetch=1, grid=(nb, d, nt),
            in_specs=[pl.BlockSpec((1, 1, tr, cols), lambda k, l, i, c_ref: (k, l, c_ref[0] * nt + i, 0)), blk],
            out_specs=[blk, blk]),
        out_shape=[jax.ShapeDtypeStruct((nb, d, h, cols), F32), jax.ShapeDtypeStruct((nb, d, h, cols), BF16)],
        compiler_params=_cparams(("parallel", "parallel", "parallel")),
    )(c_idx, g, r)


def _exchange_chips(pbs):
    n = len(pbs)

    def body(*refs):
        ins, outs, send_sems, recv_sems = refs[:n], refs[n:2 * n], refs[2 * n], refs[2 * n + 1]
        x, y, c = _mesh_pos()
        cps = [_remote(ins[t].at[2 * px + py], outs[t].at[j], send_sems.at[3 * t + j], recv_sems.at[3 * t + j], (px, py, c))
               for t in range(n) for j, (px, py) in enumerate(_other_chips(x, y))]
        for cp in cps:
            cp.start()
        for cp in cps:
            cp.wait_recv()
        for cp in cps:
            cp.wait_send()

    out_shape = [jax.ShapeDtypeStruct((3,) + p.shape[1:], p.dtype) for p in pbs]
    return _comm_call(body, "grad_chip_exchange", pbs, out_shape, 3 * n)


def _chip_add(p, r, k_idx, *, name):
    _, d, h, cols = p.shape
    tr = min(h, 512)
    nt = h // tr

    def body(k_ref, p_ref, r_ref, o_ref):
        o_ref[0] = ((p_ref[0, 0] + r_ref[0, 0].astype(F32)) + r_ref[1, 0].astype(F32)) + r_ref[2, 0].astype(F32)

    return pl.pallas_call(
        body, name=name,
        grid_spec=pltpu.PrefetchScalarGridSpec(
            num_scalar_prefetch=1, grid=(d, nt),
            in_specs=[pl.BlockSpec((1, 1, tr, cols), lambda l, i, k_ref: (k_ref[0], l, i, 0)),
                      pl.BlockSpec((3, 1, tr, cols), lambda l, i, k_ref: (0, l, i, 0))],
            out_specs=pl.BlockSpec((1, tr, cols), lambda l, i, k_ref: (l, i, 0))),
        out_shape=jax.ShapeDtypeStruct((d, h, cols), F32), compiler_params=_cparams(("parallel", "parallel")),
    )(k_idx, p, r)


def _share_halves(qs):
    n = len(qs)

    def body(*refs):
        ins, outs, send_sems, recv_sems = refs[:n], refs[n:2 * n], refs[2 * n], refs[2 * n + 1]
        x, y, c = _mesh_pos()
        cps = [_remote(ins[t], outs[t], send_sems.at[t], recv_sems.at[t], (x, y, 1 - c)) for t in range(n)]
        for cp in cps:
            cp.start()
        for cp in cps:
            cp.wait_recv()
        for cp in cps:
            cp.wait_send()

    return _comm_call(body, "grad_pair_share", qs, [jax.ShapeDtypeStruct(q.shape, q.dtype) for q in qs], n)


def _all_reduce_small(v):
    r, cols = v.shape
    n_dev = 8

    def body(v_ref, o_ref, buf, send_sems, recv_sems):
        x, y, c = _mesh_pos()
        me = 4 * x + 2 * y + c
        buf[me] = v_ref[...]

        def peer(j):
            return (1 - x if j & 4 else x, 1 - y if j & 2 else y, 1 - c if j & 1 else c)

        def copy(j, slot):
            return pltpu.make_async_remote_copy(src_ref=v_ref, dst_ref=buf.at[slot], send_sem=send_sems.at[j - 1],
                                                recv_sem=recv_sems.at[j - 1], device_id=peer(j), device_id_type=MESH)

        sends = [copy(j, me) for j in range(1, n_dev)]
        for cp in sends:
            cp.start()
        for j in range(1, n_dev):
            px, py, pc = peer(j)
            copy(j, 4 * px + 2 * py + pc).wait_recv()
        for cp in sends:
            cp.wait_send()
        acc = buf[0]
        for d in range(1, n_dev):
            acc = acc + buf[d]
        o_ref[...] = acc

    vm = pl.BlockSpec(memory_space=pltpu.VMEM)
    return pl.pallas_call(
        body, name="small_all_reduce", in_specs=[vm], out_specs=vm, out_shape=jax.ShapeDtypeStruct((r, cols), F32),
        scratch_shapes=[pltpu.VMEM((n_dev, r, cols), F32), pltpu.SemaphoreType.DMA((n_dev - 1,)), pltpu.SemaphoreType.DMA((n_dev - 1,))],
        compiler_params=pltpu.CompilerParams(has_side_effects=True),
    )(v)


_COL_SHARDED = ("w_in", "w_q_up", "w_kv_up", "w_ffn_up")


def _shard_cols(blocks, a, b):
    c = blocks.shape[-1]
    out = []
    while a < b:
        k = a // c
        hi = min(b, (k + 1) * c)
        out.append(blocks[k][:, a - k * c:hi - k * c])
        a = hi
    return out


def _pack_w_in_shards(blocks):
    z = lambda n: [jnp.zeros((blocks.shape[1], n), blocks.dtype)]
    cols = lambda a, b: _shard_cols(blocks, a, b)
    return jnp.concatenate(cols(0, 768) + cols(772, 1028) + cols(1188, 2980) + cols(1028, 1156) + cols(768, 772)
                           + z(KR_LANE - N_HEADS) + cols(1156, 1188) + z(LANES - KR_LANE - ROPE_DIM), axis=1)


def _whole_layer(name, blocks):
    if name in _COL_SHARDED:
        return jnp.concatenate([blocks[k] for k in range(N_CHIPS)], axis=1)
    return blocks.reshape(N_CHIPS * blocks.shape[1], blocks.shape[2])


def _split_layer(name, whole):
    if name in _COL_SHARDED:
        c = whole.shape[1] // N_CHIPS
        return jnp.stack([whole[:, k * c:(k + 1) * c] for k in range(N_CHIPS)])
    return whole.reshape(N_CHIPS, whole.shape[0] // N_CHIPS, whole.shape[1])


def _small_to_rows(d):
    v = jnp.concatenate([d[k].astype(F32).reshape(-1) for k in SMALL])
    rows = -(-v.shape[0] // (8 * LANES)) * 8
    return jnp.pad(v, (0, rows * LANES - v.shape[0])).reshape(rows, LANES)


def _small_from_rows(rows, shapes):
    v = rows.reshape(-1)
    out, o = {}, 0
    for k in SMALL:
        sz = int(np.prod(shapes[k]))
        out[k] = v[o:o + sz].reshape(shapes[k])
        o += sz
    return out


_ARG_NAMES = ("x", "positions", "g_mix_pre", "w_in", "b_forget", "g_q_lora", "w_q_up", "g_kv_lora", "w_kv_up", "g_mix_out", "w_out",
              "g_mix_post", "g_ffn_pre", "w_ffn_up", "w_ffn_down", "g_ffn_post")
_WEIGHTS = _ARG_NAMES[2:]


def kernel(x, positions, g_mix_pre, w_in, b_forget, g_q_lora, w_q_up, g_kv_lora, w_kv_up, g_mix_out, w_out, g_mix_post, g_ffn_pre, w_ffn_up, w_ffn_down, g_ffn_post, loss_target, m_g_mix_pre, m_w_in, m_b_forget, m_g_q_lora, m_w_q_up, m_g_kv_lora, m_w_kv_up, m_g_mix_out, m_w_out, m_g_mix_post, m_g_ffn_pre, m_w_ffn_up, m_w_ffn_down, m_g_ffn_post, v_g_mix_pre, v_w_in, v_b_forget, v_g_q_lora, v_w_q_up, v_g_kv_lora, v_w_kv_up, v_g_mix_out, v_w_out, v_g_mix_post, v_g_ffn_pre, v_w_ffn_up, v_w_ffn_down, v_g_ffn_post):
    w = dict(g_mix_pre=g_mix_pre, w_in=w_in, b_forget=b_forget, g_q_lora=g_q_lora, w_q_up=w_q_up, g_kv_lora=g_kv_lora, w_kv_up=w_kv_up,
             g_mix_out=g_mix_out, w_out=w_out, g_mix_post=g_mix_post, g_ffn_pre=g_ffn_pre, w_ffn_up=w_ffn_up, w_ffn_down=w_ffn_down,
             g_ffn_post=g_ffn_post)
    m = dict(g_mix_pre=m_g_mix_pre, w_in=m_w_in, b_forget=m_b_forget, g_q_lora=m_g_q_lora, w_q_up=m_w_q_up, g_kv_lora=m_g_kv_lora,
             w_kv_up=m_w_kv_up, g_mix_out=m_g_mix_out, w_out=m_w_out, g_mix_post=m_g_mix_post, g_ffn_pre=m_g_ffn_pre,
             w_ffn_up=m_w_ffn_up, w_ffn_down=m_w_ffn_down, g_ffn_post=m_g_ffn_post)
    v = dict(g_mix_pre=v_g_mix_pre, w_in=v_w_in, b_forget=v_b_forget, g_q_lora=v_g_q_lora, w_q_up=v_w_q_up, g_kv_lora=v_g_kv_lora,
             w_kv_up=v_w_kv_up, g_mix_out=v_g_mix_out, w_out=v_w_out, g_mix_post=v_g_mix_post, g_ffn_pre=v_g_ffn_pre,
             w_ffn_up=v_w_ffn_up, w_ffn_down=v_w_ffn_down, g_ffn_post=v_g_ffn_post)
    shard_shapes = {k: w[k].shape for k in BIG}
    small_shapes = {k: w[k].shape for k in SMALL}
    c_idx = lax.axis_index("c").astype(jnp.int32).reshape(1)
    k_idx = (2 * lax.axis_index("x") + lax.axis_index("y")).astype(jnp.int32).reshape(1)

    mine = 2 * lax.axis_index("x") + lax.axis_index("y")
    shards_b = [w[k].astype(BF16) for k in BIG]
    gathered = _gather_weights(shards_b)
    four = {k: lax.dynamic_update_slice(g, s[None], (mine, 0, 0, 0)) for k, g, s in zip(BIG, gathered, shards_b)}
    layers = []
    for l in range(DEPTH):
        wk, wv = _pack_w_kv(_whole_layer("w_kv_up", four["w_kv_up"][:, l]))
        layers.append(dict(
            g_mix_pre=g_mix_pre[l], w_in=_pack_w_in_shards(four["w_in"][:, l]), b_forget=b_forget[l], g_q_lora=g_q_lora[l],
            g_kv_lora=g_kv_lora[l], wq=_pack_w_q(_whole_layer("w_q_up", four["w_q_up"][:, l])), wk=wk, wv=wv, g_mix_out=g_mix_out[l],
            w_out=_whole_layer("w_out", four["w_out"][:, l]), g_mix_post=g_mix_post[l], g_ffn_pre=g_ffn_pre[l],
            w_ffn_up=_whole_layer("w_ffn_up", four["w_ffn_up"][:, l]), w_ffn_down=_whole_layer("w_ffn_down", four["w_ffn_down"][:, l]),
            g_ffn_post=g_ffn_post[l]))

    loss_local, dx, grads = _local_step(x[0], positions[0], layers, loss_target[0])
    loss = lax.psum(loss_local, ("x", "y", "c"))

    whole_grad = dict(
        w_in=lambda l: _unpack_dw_in(grads[l]["w_in"]), w_q_up=lambda l: _unpack_dw_q(grads[l]["wq"]),
        w_kv_up=lambda l: _unpack_dw_kv(grads[l]["wk"], grads[l]["wv"]), w_out=lambda l: grads[l]["w_out"],
        w_ffn_up=lambda l: grads[l]["w_ffn_up"], w_ffn_down=lambda l: grads[l]["w_ffn_down"])
    blocks = [jnp.stack([_split_layer(k, whole_grad[k](l)) for l in range(DEPTH)], axis=1) for k in BIG]
    theirs = _exchange_halves(blocks)
    pair = [_pair_add(g, r, c_idx, name="grad_pair_add_" + k) for k, g, r in zip(BIG, blocks, theirs)]
    partial = _exchange_chips([pb for (_, pb) in pair])
    mine_half = [_chip_add(p, r, k_idx, name="grad_chip_add_" + k) for k, (p, _), r in zip(BIG, pair, partial)]
    sibling_half = _share_halves(mine_half)
    first = lax.axis_index("c") == 0
    g_big = {k: jnp.where(first, jnp.concatenate([q, s], axis=1), jnp.concatenate([s, q], axis=1))
             for k, q, s in zip(BIG, mine_half, sibling_half)}

    g_small_local = {k: jnp.stack([grads[l][k].reshape(small_shapes[k][1:]) for l in range(DEPTH)]) for k in SMALL}
    g_small = _small_from_rows(_all_reduce_small(_small_to_rows(g_small_local)), small_shapes)

    g_all = {**g_big, **g_small}
    delta, new_m, new_v = {}, {}, {}
    for k in BIG:
        d, r, c = shard_shapes[k]
        two_d = lambda a: a.reshape(d * r, c)
        dk, mk, vk = _adamw(two_d(w[k]), two_d(g_all[k]), two_d(m[k]), two_d(v[k]), name="adamw_" + k)
        delta[k], new_m[k], new_v[k] = dk.reshape(d, r, c), mk.reshape(d, r, c), vk.reshape(d, r, c)
    ds, ms, vs = _adamw(_small_to_rows(w), _small_to_rows(g_small), _small_to_rows(m), _small_to_rows(v), name="adamw_small")
    delta.update(_small_from_rows(ds, small_shapes))
    new_m.update(_small_from_rows(ms, small_shapes))
    new_v.update(_small_from_rows(vs, small_shapes))

    grad_x = dx.reshape(x.shape)
    return (loss, grad_x, *[g_all[k] for k in _WEIGHTS], *[delta[k] for k in _WEIGHTS], *[new_m[k] for k in _WEIGHTS],
            *[new_v[k] for k in _WEIGHTS])
```

```python
import functools
import math

import numpy as np
import jax
import jax.numpy as jnp
from jax import lax
from jax.experimental import pallas as pl
from jax.experimental.pallas import tpu as pltpu

F32 = jnp.float32
BF16 = jnp.bfloat16
MESH = pl.DeviceIdType.MESH

D_MODEL = 1024
DEPTH = 2
CHUNK = 64
GROUP = 256
HEAD = 64
N_HEADS = 4
Q_RANK = 256
KV_RANK = 128
ROPE_DIM = 32
D_FF = 4096
D_IN = 2980
D_INP = 3072
ROPE_BASE = 10000.0
EPS = 1e-6
LANES = 128
TQ = 128
NEG = -1e30

ADAM_LR, ADAM_B1, ADAM_B2, ADAM_EPS, ADAM_WD, ADAM_STEP = 0.001, 0.9, 0.999, 1e-08, 0.01, 10

OFF_FQ, OFF_FK, OFF_FV, OFF_CQ = 0, 2, 4, 6
OFF_RQ, OFF_RK, OFF_RV, OFF_RG = 8, 10, 12, 14
OFF_SQ, OFF_SK, OFF_SV = 16, 18, 20
OFF_CKV, OFF_MISC = 22, 23
FF_LANE, KR_LANE = 0, 64

VMEM_LIMIT = 56 * 1024 * 1024


def _tile(dim, pref):
    return pref if dim % pref == 0 else dim


def _cparams(sem, vmem=None):
    return pltpu.CompilerParams(dimension_semantics=sem, vmem_limit_bytes=vmem or VMEM_LIMIT)


def _dot(a, b):
    return jnp.dot(a, b, preferred_element_type=F32)


def _dot_nt(a, b):
    return lax.dot_general(a, b, (((1,), (1,)), ((), ())), preferred_element_type=F32)


def _dot_tn(a, b):
    return lax.dot_general(a, b, (((0,), (0,)), ((), ())), preferred_element_type=F32)


def _dot_exact(a, b):
    return jnp.dot(a, b, precision=lax.Precision.HIGHEST, preferred_element_type=F32)


def _matmul(a, b, *, name, ta=False, tb=False, out_dtype=F32, tm=1024, tn=1024, tk=1024,
            relu2=False, relu2_of=None, also_bf16=False):
    if ta:
        kdim, m = a.shape
    else:
        m, kdim = a.shape
    n = b.shape[0] if tb else b.shape[1]
    tm, tn, tk = _tile(m, tm), _tile(n, tn), _tile(kdim, tk)
    nk = kdim // tk
    a_spec = pl.BlockSpec((tk, tm), lambda i, j, k: (k, i)) if ta else pl.BlockSpec((tm, tk), lambda i, j, k: (i, k))
    b_spec = pl.BlockSpec((tn, tk), lambda i, j, k: (j, k)) if tb else pl.BlockSpec((tk, tn), lambda i, j, k: (k, j))
    o_spec = pl.BlockSpec((tm, tn), lambda i, j, k: (i, j))
    two = also_bf16

    def body(*refs):
        refs = list(refs)
        a_ref, b_ref = refs[0], refs[1]
        e_ref = refs[2] if relu2_of is not None else None
        pos = 3 if relu2_of is not None else 2
        o_ref = refs[pos]
        o2_ref = refs[pos + 1] if two else None
        acc_ref = refs[-1]
        k = pl.program_id(2)
        av = a_ref[...].astype(BF16)
        bv = b_ref[...].astype(BF16)
        if ta:
            part = _dot_tn(av, bv)
        elif tb:
            part = _dot_nt(av, bv)
        else:
            part = _dot(av, bv)

        @pl.when(k == 0)
        def _():
            acc_ref[...] = part

        @pl.when(k > 0)
        def _():
            acc_ref[...] += part

        @pl.when(k == nk - 1)
        def _():
            r = acc_ref[...]
            if relu2_of is not None:
                r = r * (2.0 * jnp.sqrt(e_ref[...].astype(F32)))
            if relu2:
                r = jnp.square(jnp.maximum(r, 0.0))
            o_ref[...] = r.astype(o_ref.dtype)
            if also_bf16:
                o2_ref[...] = r.astype(BF16)

    in_specs = [a_spec, b_spec]
    args = [a, b]
    if relu2_of is not None:
        in_specs.append(o_spec)
        args.append(relu2_of)
    out_shape = [jax.ShapeDtypeStruct((m, n), out_dtype)]
    out_specs = [o_spec]
    if two:
        out_shape.append(jax.ShapeDtypeStruct((m, n), BF16))
        out_specs.append(o_spec)
    res = pl.pallas_call(
        body, name=name, grid=(m // tm, n // tn, nk), in_specs=in_specs, out_specs=out_specs, out_shape=out_shape,
        scratch_shapes=[pltpu.VMEM((tm, tn), F32)],
        compiler_params=_cparams(("parallel", "parallel", "arbitrary")),
    )(*args)
    return res if two else res[0]


def _rms(x, g):
    r = lax.rsqrt(jnp.mean(x * x, axis=-1, keepdims=True) + EPS)
    return x * r * g


def _rms_bwd(x, g, dy):
    r = lax.rsqrt(jnp.mean(x * x, axis=-1, keepdims=True) + EPS)
    xh = x * r
    gdy = dy * g
    dx = r * (gdy - xh * jnp.mean(xh * gdy, axis=-1, keepdims=True))
    return dx, xh * dy


def _norm_fwd(x, g, *, name, resid=None, out_dtype=BF16):
    s, d = x.shape
    tr = _tile(s, 256)
    row = pl.BlockSpec((tr, d), lambda i: (i, 0))
    gsp = pl.BlockSpec((1, d), lambda i: (0, 0))

    def body(*refs):
        if resid is None:
            x_ref, g_ref, o_ref = refs
            o_ref[...] = _rms(x_ref[...], g_ref[...]).astype(o_ref.dtype)
        else:
            x_ref, g_ref, r_ref, o_ref = refs
            o_ref[...] = (r_ref[...] + _rms(x_ref[...], g_ref[...])).astype(o_ref.dtype)

    args = [x, g.reshape(1, d)] + ([] if resid is None else [resid])
    return pl.pallas_call(
        body, name=name, grid=(s // tr,), in_specs=[row, gsp] + ([] if resid is None else [row]),
        out_specs=row, out_shape=jax.ShapeDtypeStruct((s, d), out_dtype), compiler_params=_cparams(("parallel",)),
    )(*args)


def _norm_bwd(x, g, dy, *, name, add=None, out_dtype=F32):
    s, d = x.shape
    tr = _tile(s, 256)
    row = pl.BlockSpec((tr, d), lambda i: (i, 0))
    gsp = pl.BlockSpec((1, d), lambda i: (0, 0))

    def body(*refs):
        if add is None:
            x_ref, g_ref, dy_ref, dx_ref, dg_ref = refs
        else:
            x_ref, g_ref, dy_ref, add_ref, dx_ref, dg_ref = refs
        dx, gterm = _rms_bwd(x_ref[...], g_ref[...], dy_ref[...].astype(F32))
        if add is not None:
            dx = dx + add_ref[...]
        dx_ref[...] = dx.astype(dx_ref.dtype)

        @pl.when(pl.program_id(0) == 0)
        def _():
            dg_ref[...] = jnp.zeros_like(dg_ref)

        dg_ref[...] += jnp.sum(gterm, axis=0, keepdims=True)

    args = [x, g.reshape(1, d), dy] + ([] if add is None else [add])
    return pl.pallas_call(
        body, name=name, grid=(s // tr,), in_specs=[row, gsp, row] + ([] if add is None else [row]),
        out_specs=[row, gsp], out_shape=[jax.ShapeDtypeStruct((s, d), out_dtype), jax.ShapeDtypeStruct((1, d), F32)],
        compiler_params=_cparams(("arbitrary",)),
    )(*args)


def _loss_head(y, target):
    s, d = y.shape
    tr = _tile(s, 256)
    row = pl.BlockSpec((tr, d), lambda i: (i, 0))
    lsp = pl.BlockSpec((1, LANES), lambda i: (0, 0))

    def body(y_ref, t_ref, l_ref, dy_ref):
        e = y_ref[...] - t_ref[...]
        dy_ref[...] = e * (1.0 / d)

        @pl.when(pl.program_id(0) == 0)
        def _():
            l_ref[...] = jnp.zeros_like(l_ref)

        part = 0.5 * jnp.sum(jnp.mean(e * e, axis=-1, keepdims=True), axis=0, keepdims=True)
        l_ref[...] += jnp.broadcast_to(part, (1, LANES))

    return pl.pallas_call(
        body, name="loss_head", grid=(s // tr,), in_specs=[row, row], out_specs=[lsp, row],
        out_shape=[jax.ShapeDtypeStruct((1, LANES), F32), jax.ShapeDtypeStruct((s, d), F32)],
        compiler_params=_cparams(("arbitrary",)),
    )(y, target)


def _rope_tables(pos_col):
    s = pos_col.shape[0]
    tr = _tile(s, 512)
    f_mla = ROPE_BASE ** (-jnp.arange(ROPE_DIM // 2, dtype=F32) / (ROPE_DIM // 2))
    f_ret = ROPE_BASE ** (-jnp.arange(HEAD // 2, dtype=F32) / (HEAD // 2))
    fm = jnp.concatenate([jnp.zeros((64,), F32), f_mla, f_mla, jnp.zeros((32,), F32)]).reshape(1, LANES)
    fr = jnp.tile(jnp.concatenate([f_ret, f_ret]), 4).reshape(1, 2 * LANES)

    def body(p_ref, fm_ref, fr_ref, cm_ref, sm_ref, cr_ref, sr_ref):
        p = p_ref[...].astype(F32)
        am = p * fm_ref[...]
        ar = p * fr_ref[...]
        cm_ref[...] = jnp.cos(am)
        sm_ref[...] = jnp.sin(am)
        cr_ref[...] = jnp.cos(ar)
        sr_ref[...] = jnp.sin(ar)

    return pl.pallas_call(
        body, name="rope_tables", grid=(s // tr,),
        in_specs=[pl.BlockSpec((tr, 1), lambda i: (i, 0)), pl.BlockSpec((1, LANES), lambda i: (0, 0)),
                  pl.BlockSpec((1, 2 * LANES), lambda i: (0, 0))],
        out_specs=[pl.BlockSpec((tr, LANES), lambda i: (i, 0))] * 2 + [pl.BlockSpec((tr, 2 * LANES), lambda i: (i, 0))] * 2,
        out_shape=[jax.ShapeDtypeStruct((s, LANES), F32)] * 2 + [jax.ShapeDtypeStruct((s, 2 * LANES), F32)] * 2,
        compiler_params=_cparams(("parallel",)),
    )(pos_col, fm, fr)


def _lane(shape):
    return lax.broadcasted_iota(jnp.int32, shape, len(shape) - 1)


def _rot_mla(z):
    l = _lane(z.shape) % LANES
    n = z.shape[-1]
    return jnp.where(l < 80, -pltpu.roll(z, n - 16, 1), pltpu.roll(z, 16, 1))


def _rot_mla_t(y):
    l = _lane(y.shape) % LANES
    n = y.shape[-1]
    return jnp.where((l >= 64) & (l < 80), pltpu.roll(y, n - 16, 1),
                     jnp.where((l >= 80) & (l < 96), -pltpu.roll(y, 16, 1), 0.0))


def _rot_ret(z):
    l = _lane(z.shape) % HEAD
    n = z.shape[-1]
    return jnp.where(l < 32, -pltpu.roll(z, n - 32, 1), pltpu.roll(z, 32, 1))


def _rot_ret_t(y):
    l = _lane(y.shape) % HEAD
    n = y.shape[-1]
    return jnp.where(l < 32, pltpu.roll(y, n - 32, 1), -pltpu.roll(y, 32, 1))


def _log_sigmoid(x):
    return jnp.minimum(x, 0.0) - jnp.log1p(jnp.exp(-jnp.abs(x)))


def _fox_cum(proj, bias_row):
    s = proj.shape[0]
    nb = s // TQ

    def body(x_ref, b_ref, cc_ref, cr_ref, carry_ref):
        @pl.when(pl.program_id(0) == 0)
        def _():
            carry_ref[...] = jnp.zeros_like(carry_ref)

        ls = _log_sigmoid(x_ref[...] + b_ref[...])
        r = lax.broadcasted_iota(jnp.int32, (TQ, TQ), 0)
        c = lax.broadcasted_iota(jnp.int32, (TQ, TQ), 1)
        tri = (c <= r).astype(F32)
        cum = _dot_exact(tri, ls) + carry_ref[...]
        carry_ref[...] = cum[TQ - 1:TQ, :]
        cc_ref[...] = cum
        cr_ref[...] = cum.T[0:8, :]

    return pl.pallas_call(
        body, name="fox_cum", grid=(nb,),
        in_specs=[pl.BlockSpec((TQ, LANES), lambda i: (i, OFF_MISC)), pl.BlockSpec((1, LANES), lambda i: (0, 0))],
        out_specs=[pl.BlockSpec((TQ, LANES), lambda i: (i, 0)), pl.BlockSpec((8, TQ), lambda i: (0, i))],
        out_shape=[jax.ShapeDtypeStruct((s, LANES), F32), jax.ShapeDtypeStruct((8, s), F32)],
        scratch_shapes=[pltpu.VMEM((1, LANES), F32)],
        compiler_params=_cparams(("arbitrary",)),
    )(proj, bias_row)


def _fox_gate_bwd(dck, drs, proj, bias_row, dkr):
    s = proj.shape[0]
    nb = s // TQ

    def body(d_ref, r_ref, x_ref, b_ref, k_ref, o_ref, db_ref, carry_ref):
        @pl.when(pl.program_id(0) == 0)
        def _():
            carry_ref[...] = jnp.zeros_like(carry_ref)
            db_ref[...] = jnp.zeros_like(db_ref)

        rows = jnp.concatenate([d_ref[0], d_ref[1], jnp.zeros((TQ - 16, TQ), F32)], axis=0)
        t = rows.T
        l = _lane((TQ, LANES))
        r0, r1 = r_ref[0], r_ref[1]
        rsum = jnp.where(l == 0, r0[:, 0:1], jnp.where(l == 1, r0[:, HEAD:HEAD + 1],
                         jnp.where(l == 2, r1[:, 0:1], jnp.where(l == 3, r1[:, HEAD:HEAD + 1], 0.0))))
        dcum = rsum - jnp.where(l < 2, t, pltpu.roll(t, LANES - 6, 1))
        r = lax.broadcasted_iota(jnp.int32, (TQ, TQ), 0)
        c = lax.broadcasted_iota(jnp.int32, (TQ, TQ), 1)
        triu = (c >= r).astype(F32)
        rc = _dot_exact(triu, dcum) + carry_ref[...]
        carry_ref[...] = rc[0:1, :]
        f = x_ref[...] + b_ref[...]
        sig_neg = 1.0 / (1.0 + jnp.exp(f))
        df = jnp.where(l < N_HEADS, rc * sig_neg, 0.0)
        db_ref[...] += jnp.sum(df, axis=0, keepdims=True)
        o_ref[...] = (df + k_ref[...]).astype(o_ref.dtype)

    rev = lambda i: nb - 1 - i
    return pl.pallas_call(
        body, name="fox_gate_bwd", grid=(nb,),
        in_specs=[pl.BlockSpec((2, 8, TQ), lambda i: (0, 0, rev(i))), pl.BlockSpec((2, TQ, LANES), lambda i: (0, rev(i), 0)),
                  pl.BlockSpec((TQ, LANES), lambda i: (rev(i), OFF_MISC)),
                  pl.BlockSpec((1, LANES), lambda i: (0, 0)), pl.BlockSpec((TQ, LANES), lambda i: (rev(i), 0))],
        out_specs=[pl.BlockSpec((TQ, LANES), lambda i: (rev(i), 0)), pl.BlockSpec((1, LANES), lambda i: (0, 0))],
        out_shape=[jax.ShapeDtypeStruct((s, LANES), BF16), jax.ShapeDtypeStruct((1, LANES), F32)],
        scratch_shapes=[pltpu.VMEM((1, LANES), F32)],
        compiler_params=_cparams(("arbitrary",)),
    )(dck, drs, proj, bias_row, dkr)


def _mla_prep(proj, cos_m, sin_m, g_q, g_kv, wq, wk, wv):
    s = proj.shape[0]
    tr = _tile(s, 256)

    def body(cq_ref, ckv_ref, misc_ref, cos_ref, sin_ref, gq_ref, gkv_ref, wq_ref, wk_ref, wv_ref,
             q_ref, k_ref, v_ref, cqn_ref, ckvn_ref):
        cos4 = jnp.tile(cos_ref[...], (1, 4))
        sin4 = jnp.tile(sin_ref[...], (1, 4))
        cqn = _rms(cq_ref[...], gq_ref[...]).astype(BF16)
        ckvn = _rms(ckv_ref[...], gkv_ref[...]).astype(BF16)
        cqn_ref[...] = cqn
        ckvn_ref[...] = ckvn
        zq = _dot(cqn, wq_ref[...])
        q_ref[...] = (zq * cos4 + _rot_mla(zq) * sin4).astype(BF16)
        l = _lane((tr, LANES))
        kr = jnp.where((l >= KR_LANE) & (l < KR_LANE + ROPE_DIM), misc_ref[...], 0.0)
        zk = _dot(ckvn, wk_ref[...]) + jnp.tile(kr, (1, 4))
        k_ref[...] = (zk * cos4 + _rot_mla(zk) * sin4).astype(BF16)
        v_ref[...] = _dot(ckvn, wv_ref[...]).astype(BF16)

    full = lambda a: pl.BlockSpec(a.shape, lambda i: (0, 0))
    rowb = lambda w: pl.BlockSpec((tr, w), lambda i: (i, 0))
    gq2, gkv2 = g_q.reshape(1, Q_RANK), g_kv.reshape(1, KV_RANK)
    return pl.pallas_call(
        body, name="mla_prep", grid=(s // tr,),
        in_specs=[pl.BlockSpec((tr, 256), lambda i: (i, OFF_CQ // 2)), pl.BlockSpec((tr, LANES), lambda i: (i, OFF_CKV)),
                  pl.BlockSpec((tr, LANES), lambda i: (i, OFF_MISC)), rowb(LANES), rowb(LANES),
                  full(gq2), full(gkv2), full(wq), full(wk), full(wv)],
        out_specs=[rowb(512), rowb(512), rowb(512), rowb(256), rowb(128)],
        out_shape=[jax.ShapeDtypeStruct((s, 512), BF16), jax.ShapeDtypeStruct((s, 512), BF16), jax.ShapeDtypeStruct((s, 512), BF16),
                   jax.ShapeDtypeStruct((s, 256), BF16), jax.ShapeDtypeStruct((s, 128), BF16)],
        compiler_params=_cparams(("parallel",)),
    )(proj, proj, proj, cos_m, sin_m, gq2, gkv2, wq, wk, wv)


def _mla_prep_bwd(dq, dk, dv, proj, cqn, ckvn, cos_m, sin_m, g_q, g_kv, wq, wk, wv):
    s = proj.shape[0]
    tr = _tile(s, 256)

    def body(dq_ref, dk_ref, dv_ref, cq_ref, ckv_ref, cqn_ref, ckvn_ref, cos_ref, sin_ref, gq_ref, gkv_ref,
             wq_ref, wk_ref, wv_ref, dcq_ref, dckv_ref, dkr_ref, dwq_ref, dwk_ref, dwv_ref, dgq_ref, dgkv_ref):
        @pl.when(pl.program_id(0) == 0)
        def _():
            for r in (dwq_ref, dwk_ref, dwv_ref, dgq_ref, dgkv_ref):
                r[...] = jnp.zeros_like(r)

        cos4 = jnp.tile(cos_ref[...], (1, 4))
        sin4 = jnp.tile(sin_ref[...], (1, 4))
        dqv = dq_ref[...]
        dzq = dqv * cos4 + _rot_mla_t(dqv * sin4)
        dkv_ = dk_ref[...]
        dzk = dkv_ * cos4 + _rot_mla_t(dkv_ * sin4)
        l = _lane((tr, LANES))
        in_rope = (l >= KR_LANE) & (l < KR_LANE + ROPE_DIM)
        dkr = dzk[:, 0:128] + dzk[:, 128:256] + dzk[:, 256:384] + dzk[:, 384:512]
        dkr_ref[...] = jnp.where(in_rope, dkr, 0.0)
        dzq_b = dzq.astype(BF16)
        dzk_b = dzk.astype(BF16)
        dv_b = dv_ref[...].astype(BF16)
        dcqn = _dot_nt(dzq_b, wq_ref[...])
        dckvn = _dot_nt(dzk_b, wk_ref[...]) + _dot_nt(dv_b, wv_ref[...])
        dwq_ref[...] += _dot_tn(cqn_ref[...], dzq_b)
        dwk_ref[...] += _dot_tn(ckvn_ref[...], dzk_b)
        dwv_ref[...] += _dot_tn(ckvn_ref[...], dv_b)
        dcq, gq_term = _rms_bwd(cq_ref[...], gq_ref[...], dcqn)
        dckv, gkv_term = _rms_bwd(ckv_ref[...], gkv_ref[...], dckvn)
        dcq_ref[...] = dcq.astype(BF16)
        dckv_ref[...] = dckv.astype(BF16)
        dgq_ref[...] += jnp.sum(gq_term, axis=0, keepdims=True)
        dgkv_ref[...] += jnp.sum(gkv_term, axis=0, keepdims=True)

    full = lambda shp: pl.BlockSpec(shp, lambda i: (0, 0))
    rowb = lambda w: pl.BlockSpec((tr, w), lambda i: (i, 0))
    gq2, gkv2 = g_q.reshape(1, Q_RANK), g_kv.reshape(1, KV_RANK)
    return pl.pallas_call(
        body, name="mla_prep_bwd", grid=(s // tr,),
        in_specs=[rowb(512), rowb(512), rowb(512),
                  pl.BlockSpec((tr, 256), lambda i: (i, OFF_CQ // 2)), pl.BlockSpec((tr, LANES), lambda i: (i, OFF_CKV)),
                  rowb(256), rowb(128), rowb(LANES), rowb(LANES), full((1, Q_RANK)), full((1, KV_RANK)),
                  full(wq.shape), full(wk.shape), full(wv.shape)],
        out_specs=[rowb(256), rowb(128), rowb(128), full(wq.shape), full(wk.shape), full(wv.shape),
                   full((1, Q_RANK)), full((1, KV_RANK))],
        out_shape=[jax.ShapeDtypeStruct((s, 256), BF16), jax.ShapeDtypeStruct((s, 128), BF16), jax.ShapeDtypeStruct((s, 128), F32),
                   jax.ShapeDtypeStruct(wq.shape, F32), jax.ShapeDtypeStruct(wk.shape, F32), jax.ShapeDtypeStruct(wv.shape, F32),
                   jax.ShapeDtypeStruct((1, Q_RANK), F32), jax.ShapeDtypeStruct((1, KV_RANK), F32)],
        compiler_params=_cparams(("arbitrary",)),
    )(dq, dk, dv, proj, proj, cqn, ckvn, cos_m, sin_m, gq2, gkv2, wq, wk, wv)


def _ret_prep(proj, cos_r, sin_r):
    s = proj.shape[0]
    tr = _tile(s, 256)

    def body(q_ref, k_ref, cos_ref, sin_ref, qo_ref, ko_ref):
        cos, sin = cos_ref[...], sin_ref[...]
        q, k = q_ref[...], k_ref[...]
        qo_ref[...] = (q * cos + _rot_ret(q) * sin).astype(BF16)
        ko_ref[...] = ((k * cos + _rot_ret(k) * sin) * (HEAD ** -0.5)).astype(BF16)

    rowb = pl.BlockSpec((tr, 256), lambda i: (i, 0))
    return pl.pallas_call(
        body, name="ret_prep", grid=(s // tr,),
        in_specs=[pl.BlockSpec((tr, 256), lambda i: (i, OFF_RQ // 2)), pl.BlockSpec((tr, 256), lambda i: (i, OFF_RK // 2)), rowb, rowb],
        out_specs=[rowb, rowb], out_shape=[jax.ShapeDtypeStruct((s, 256), BF16)] * 2,
        compiler_params=_cparams(("parallel",)),
    )(proj, proj, cos_r, sin_r)


def _ret_prep_bwd(dq, dk, cos_r, sin_r):
    s = dq.shape[0]
    tr = _tile(s, 256)

    def body(dq_ref, dk_ref, cos_ref, sin_ref, qo_ref, ko_ref):
        cos, sin = cos_ref[...], sin_ref[...]
        q, k = dq_ref[...], dk_ref[...] * (HEAD ** -0.5)
        qo_ref[...] = (q * cos + _rot_ret_t(q * sin)).astype(BF16)
        ko_ref[...] = (k * cos + _rot_ret_t(k * sin)).astype(BF16)

    rowb = pl.BlockSpec((tr, 256), lambda i: (i, 0))
    return pl.pallas_call(
        body, name="ret_prep_bwd", grid=(s // tr,), in_specs=[rowb] * 4, out_specs=[rowb, rowb],
        out_shape=[jax.ShapeDtypeStruct((s, 256), BF16)] * 2, compiler_params=_cparams(("parallel",)),
    )(dq, dk, cos_r, sin_r)


_LOG_GAMMA = [float(np.log1p(-np.float32(2.0) ** np.float32(-5.0 - h))) for h in range(N_HEADS)]
_MLA_SCALE = float((HEAD + ROPE_DIM) ** -0.5)
_QK_SCALE = float(HEAD ** -0.5)
KEY_BLOCKS = 4


def _split2(x):
    h = x.astype(BF16)
    return h, (x - h.astype(F32)).astype(BF16)


def _dot2(x, u):
    h, lo = _split2(x)
    return _dot(h, u) + _dot(lo, u)


def _head_pick(block, head, axis):
    idx = lax.broadcasted_iota(jnp.int32, block.shape, axis)
    return jnp.sum(jnp.where(idx == head, block, 0.0), axis=axis, keepdims=True)


def _log_gamma_of(head):
    lg = jnp.float32(_LOG_GAMMA[3])
    for h in (2, 1, 0):
        lg = jnp.where(head == h, jnp.float32(_LOG_GAMMA[h]), lg)
    return lg


def _mixer_specs(mode, s, q_off, k_off, v_off):
    nhb = 2
    bw = 2 * LANES if mode == "mla" else LANES
    nsub = KEY_BLOCKS if (s // TQ) % KEY_BLOCKS == 0 else 1
    q_spec = pl.BlockSpec((TQ, bw), lambda p, i: (i, q_off + p))
    k_spec = pl.BlockSpec((s, bw), lambda p, i: (0, k_off + p))
    v_spec = pl.BlockSpec((s, bw), lambda p, i: (0, v_off + p))
    return nhb, N_HEADS // nhb, nsub, q_spec, k_spec, v_spec


def _mixer_geometry(mode, i, nsub):
    w = TQ * nsub
    row = lax.broadcasted_iota(jnp.int32, (TQ, w), 0)
    col = lax.broadcasted_iota(jnp.int32, (TQ, w), 1)
    nfull = i // nsub
    dist = col - row
    if mode in ("fox", "sb"):
        rel = dist
    else:
        rel = col - (row | (CHUNK - 1))

    def visible(c):
        off = c * w - i * TQ
        return (rel + off) < 0 if mode == "sb" else (rel + off) <= 0

    return nfull, dist, visible


def _mixer_fwd(mode, qa, q_off, ka, k_off, va, v_off, *, cum_col=None, cum_row=None):
    s = qa.shape[0]
    nq = s // TQ
    nhb, nblk, nsub, q_spec, k_spec, v_spec = _mixer_specs(mode, s, q_off, k_off, v_off)
    w = TQ * nsub
    softmax = mode in ("fox", "mla")
    has_stat = mode != "ret"

    def body(*refs):
        refs = list(refs)
        q_ref, k_ref, v_ref = refs[:3]
        refs = refs[3:]
        if mode == "fox":
            cc_ref, cr_ref = refs[:2]
            refs = refs[2:]
        o_ref = refs[0]
        st_ref = refs[1] if has_stat else None
        p = pl.program_id(0)
        i = pl.program_id(1)
        nfull, dist, visible = _mixer_geometry(mode, i, nsub)
        lane = _lane((1, LANES))
        heads = [nhb * p + hh for hh in range(nhb)]
        wide = mode == "mla"
        q_scale = _QK_SCALE if mode in ("fox", "sb") else 1.0
        cols = [slice(hh * LANES, (hh + 1) * LANES) if wide else slice(None) for hh in range(nhb)]
        if wide:
            qs = [q_ref[:, cols[hh]] for hh in range(nhb)]
        else:
            qf = q_ref[...].astype(F32) * q_scale
            qs = [jnp.where((lane // HEAD) == hh, qf, 0.0).astype(BF16) for hh in range(nhb)]
        if mode == "fox":
            cqs = [_head_pick(cc_ref[...], h, 1) for h in heads]
        if mode == "ret":
            lgs = [_log_gamma_of(h) for h in heads]
            distf = dist.astype(F32)
        if mode == "sb":
            r1 = lax.broadcasted_iota(jnp.int32, (TQ, TQ), 0)
            c1 = lax.broadcasted_iota(jnp.int32, (TQ, TQ), 1)
            u_after = (r1 > c1).astype(BF16)

        def chunk(c):
            return pl.ds(pl.multiple_of(c * w, w), w)

        def scores(c):
            js = chunk(c)
            return tuple(_dot_nt(qs[hh], k_ref[js, cols[hh]]) for hh in range(nhb))

        def head_step(hh, c, js, sc, vj, carry, last):
            if softmax:
                m, l, acc = carry
                if mode == "fox":
                    ck = _head_pick(cr_ref[:, js], heads[hh], 0)
                    sc = sc + (cqs[hh] - ck)
                else:
                    sc = sc * _MLA_SCALE
                if last:
                    sc = jnp.where(visible(c), sc, NEG)
                m_new = jnp.maximum(m, jnp.max(sc, axis=-1, keepdims=True))
                alpha = jnp.exp(m - m_new)
                pr = jnp.exp(sc - m_new)
                l = alpha * l + jnp.sum(pr, axis=-1, keepdims=True)
                acc = alpha * acc + _dot(pr.astype(BF16), vj)
                return m_new, l, acc
            if mode == "ret":
                off = (i * TQ - c * w).astype(F32)
                if last:
                    dec = jnp.where(visible(c), jnp.exp(lgs[hh] * jnp.abs(off - distf)), 0.0)
                else:
                    dec = jnp.exp(lgs[hh] * (off - distf))
                return carry + _dot((sc * dec).astype(BF16), vj)
            run, acc = carry
            z = sc
            log_beta = jnp.minimum(z, 0.0) - jnp.log1p(jnp.exp(-jnp.abs(z)))
            log_stay = log_beta - z
            if last:
                vis = visible(c)
                log_stay = jnp.where(vis, log_stay, 0.0)
            parts = [None] * nsub
            for b in reversed(range(nsub)):
                ls_b = log_stay[:, b * TQ:(b + 1) * TQ]
                parts[b] = _dot2(ls_b, u_after) + run
                run = run + jnp.sum(ls_b, axis=-1, keepdims=True)
            later = parts[0] if nsub == 1 else jnp.concatenate(parts, axis=1)
            wgt = jnp.exp(log_beta + later)
            if last:
                wgt = jnp.where(vis, wgt, 0.0)
            return run, acc + _dot(wgt.astype(BF16), vj)

        def step(c, c_next, state, last):
            scs, carries = state
            nxt = scores(c_next) if c_next is not None else None
            js = chunk(c)
            return nxt, tuple(head_step(hh, c, js, scs[hh], v_ref[js, cols[hh]], carries[hh], last) for hh in range(nhb))

        zero_acc = jnp.zeros((TQ, LANES), F32)
        zero1 = jnp.zeros((TQ, 1), F32)
        if softmax:
            init = tuple((jnp.full((TQ, 1), NEG, F32), zero1, zero_acc) for _ in range(nhb))
        elif mode == "ret":
            init = tuple(zero_acc for _ in range(nhb))
        else:
            init = tuple((zero1, zero_acc) for _ in range(nhb))
        if mode == "sb":
            state = step(nfull, jnp.maximum(nfull - 1, 0), (scores(nfull), init), True)
            _, carries = lax.fori_loop(0, nfull, lambda t, st: step(nfull - 1 - t, jnp.maximum(nfull - 2 - t, 0), st, False), state)
        else:
            state = lax.fori_loop(0, nfull, lambda c, st: step(c, c + 1, st, False), (scores(0), init))
            _, carries = step(nfull, None, state, True)
        if softmax:
            outs = [acc / l for (m, l, acc) in carries]
            stats = [m + jnp.log(l) for (m, l, acc) in carries]
        elif mode == "ret":
            outs, stats = list(carries), None
        else:
            outs, stats = [acc for (run, acc) in carries], [run for (run, acc) in carries]
        hm0 = (lane // HEAD) == 0
        pick = lambda a: jnp.where(hm0, a[0], a[1])
        if wide:
            for hh in range(nhb):
                o_ref[:, cols[hh]] = outs[hh]
        else:
            o_ref[...] = pick(outs)
        if has_stat:
            st_ref[0] = pick(stats)

    in_specs = [q_spec, k_spec, v_spec]
    args = [qa, ka, va]
    if mode == "fox":
        in_specs += [pl.BlockSpec((TQ, LANES), lambda p, i: (i, 0)), pl.BlockSpec((8, s), lambda p, i: (0, 0))]
        args += [cum_col, cum_row]
    bw = 2 * LANES if mode == "mla" else LANES
    out_specs = [pl.BlockSpec((TQ, bw), lambda p, i: (i, p))]
    out_shape = [jax.ShapeDtypeStruct((s, nblk * bw), F32)]
    if has_stat:
        out_specs.append(pl.BlockSpec((1, TQ, LANES), lambda p, i: (p, i, 0)))
        out_shape.append(jax.ShapeDtypeStruct((nblk, s, LANES), F32))
    res = pl.pallas_call(
        body, name=mode + "_fwd", grid=(nblk, nq), in_specs=in_specs, out_specs=out_specs, out_shape=out_shape,
        compiler_params=_cparams(("parallel", "parallel")),
    )(*args)
    return res if has_stat else (res[0], None)


def _mixer_bwd(mode, qa, q_off, ka, k_off, va, v_off, o, do, *, stat=None, cum_col=None, cum_row=None):
    s = qa.shape[0]
    nq = s // TQ
    nhb, nblk, nsub, q_spec, k_spec, v_spec = _mixer_specs(mode, s, q_off, k_off, v_off)
    w = TQ * nsub
    softmax = mode in ("fox", "mla")
    has_stat = mode != "ret"

    def body(*refs):
        refs = list(refs)
        q_ref, k_ref, v_ref, o_ref, do_ref = refs[:5]
        refs = refs[5:]
        if has_stat:
            st_ref = refs[0]
            refs = refs[1:]
        if mode == "fox":
            cc_ref, cr_ref = refs[:2]
            refs = refs[2:]
        dq_ref, dk_ref, dv_ref = refs[:3]
        dck_ref, drs_ref = refs[3:5] if mode == "fox" else (None, None)
        p = pl.program_id(0)
        i = pl.program_id(1)

        @pl.when(i == 0)
        def _():
            dk_ref[...] = jnp.zeros_like(dk_ref)
            dv_ref[...] = jnp.zeros_like(dv_ref)
            if mode == "fox":
                dck_ref[...] = jnp.zeros_like(dck_ref)

        nfull, dist, visible = _mixer_geometry(mode, i, nsub)
        lane = _lane((1, LANES))
        heads = [nhb * p + hh for hh in range(nhb)]
        dov = do_ref[...]
        wide = mode == "mla"
        q_scale = _QK_SCALE if mode in ("fox", "sb") else 1.0
        cols = [slice(hh * LANES, (hh + 1) * LANES) if wide else slice(None) for hh in range(nhb)]
        if wide:
            prod = dov * o_ref[...]
            qs = [q_ref[:, cols[hh]] for hh in range(nhb)]
            dos = [dov[:, cols[hh]].astype(BF16) for hh in range(nhb)]
            deltas = [jnp.sum(prod[:, cols[hh]], axis=-1, keepdims=True) for hh in range(nhb)]
        else:
            qf = q_ref[...].astype(F32) * q_scale
            prod = dov * o_ref[...]
            hms = [(lane // HEAD) == hh for hh in range(nhb)]
            qs = [jnp.where(hm, qf, 0.0).astype(BF16) for hm in hms]
            dos = [jnp.where(hm, dov, 0.0).astype(BF16) for hm in hms]
            deltas = [jnp.sum(jnp.where(hm, prod, 0.0), axis=-1, keepdims=True) for hm in hms]
        if has_stat:
            st = st_ref[0]
            stats = [st[:, hh * HEAD:hh * HEAD + 1] for hh in range(nhb)]
        if mode == "fox":
            cqs = [_head_pick(cc_ref[...], h, 1) for h in heads]
        if mode == "ret":
            lgs = [_log_gamma_of(h) for h in heads]
            distf = dist.astype(F32)
        if mode == "sb":
            r1 = lax.broadcasted_iota(jnp.int32, (TQ, TQ), 0)
            c1 = lax.broadcasted_iota(jnp.int32, (TQ, TQ), 1)
            u_upto = (r1 <= c1).astype(BF16)
            u_before = (r1 < c1).astype(BF16)

        def chunk(c):
            return pl.ds(pl.multiple_of(c * w, w), w)

        def scores(c):
            js = chunk(c)
            return tuple((_dot_nt(qs[hh], k_ref[js, cols[hh]]), _dot_nt(dos[hh], v_ref[js, cols[hh]])) for hh in range(nhb))

        def emit(hh, js, ds_b, pr_b, dq):
            dk_ref[js, cols[hh]] += _dot_tn(ds_b, qs[hh])
            dv_ref[js, cols[hh]] += _dot_tn(pr_b, dos[hh])
            return dq + _dot(ds_b, k_ref[js, cols[hh]])

        def head_step(hh, c, js, sc_dp, carry, last):
            sc, dp = sc_dp
            if softmax:
                dq, rsum = carry
                if mode == "fox":
                    ck = _head_pick(cr_ref[:, js], heads[hh], 0)
                    sc = sc + (cqs[hh] - ck)
                else:
                    sc = sc * _MLA_SCALE
                if last:
                    sc = jnp.where(visible(c), sc, NEG)
                pr = jnp.exp(sc - stats[hh])
                ds = pr * (dp - deltas[hh])
                if mode == "fox":
                    dck_ref[0, hh:hh + 1, js] += jnp.sum(ds, axis=0, keepdims=True)
                    rsum = rsum + jnp.sum(ds, axis=-1, keepdims=True)
                if mode == "mla":
                    ds = ds * _MLA_SCALE
                return emit(hh, js, ds.astype(BF16), pr.astype(BF16), dq), rsum
            if mode == "ret":
                off = (i * TQ - c * w).astype(F32)
                if last:
                    dec = jnp.where(visible(c), jnp.exp(lgs[hh] * jnp.abs(off - distf)), 0.0)
                else:
                    dec = jnp.exp(lgs[hh] * (off - distf))
                return emit(hh, js, (dp * dec).astype(BF16), (sc * dec).astype(BF16), carry)
            seen, gsum, dq = carry
            z = sc
            log_beta = jnp.minimum(z, 0.0) - jnp.log1p(jnp.exp(-jnp.abs(z)))
            log_stay = log_beta - z
            if last:
                vis = visible(c)
                log_stay = jnp.where(vis, log_stay, 0.0)
            parts = []
            for b in range(nsub):
                ls_b = log_stay[:, b * TQ:(b + 1) * TQ]
                parts.append((stats[hh] - seen) - _dot2(ls_b, u_upto))
                seen = seen + jnp.sum(ls_b, axis=-1, keepdims=True)
            later = parts[0] if nsub == 1 else jnp.concatenate(parts, axis=1)
            wgt = jnp.exp(log_beta + later)
            if last:
                wgt = jnp.where(vis, wgt, 0.0)
            g = dp * wgt
            parts = []
            for b in range(nsub):
                g_b = g[:, b * TQ:(b + 1) * TQ]
                parts.append(gsum + _dot2(g_b, u_before))
                gsum = gsum + jnp.sum(g_b, axis=-1, keepdims=True)
            before = parts[0] if nsub == 1 else jnp.concatenate(parts, axis=1)
            beta = jnp.exp(log_beta)
            dz = g * (1.0 - beta) - beta * before
            if last:
                dz = jnp.where(vis, dz, 0.0)
            return seen, gsum, emit(hh, js, dz.astype(BF16), wgt.astype(BF16), dq)

        def step(c, c_next, state, last):
            scs, carries = state
            nxt = scores(c_next) if c_next is not None else None
            js = chunk(c)
            return nxt, tuple(head_step(hh, c, js, scs[hh], carries[hh], last) for hh in range(nhb))

        zero_acc = jnp.zeros((TQ, LANES), F32)
        zero1 = jnp.zeros((TQ, 1), F32)
        if softmax:
            init = tuple((zero_acc, zero1) for _ in range(nhb))
        elif mode == "ret":
            init = tuple(zero_acc for _ in range(nhb))
        else:
            init = tuple((zero1, zero1, zero_acc) for _ in range(nhb))
        state = lax.fori_loop(0, nfull, lambda c, st: step(c, c + 1, st, False), (scores(0), init))
        _, carries = step(nfull, None, state, True)
        if softmax:
            dqs = [dq for (dq, rsum) in carries]
        elif mode == "ret":
            dqs = list(carries)
        else:
            dqs = [dq for (seen, gsum, dq) in carries]
        hm0 = (lane // HEAD) == 0
        if wide:
            for hh in range(nhb):
                dq_ref[:, cols[hh]] = dqs[hh]
        else:
            dq_ref[...] = jnp.where(hm0, dqs[0], dqs[1]) * q_scale
        if mode == "fox":
            drs_ref[0] = jnp.where(hm0, carries[0][1], carries[1][1])

    bw = 2 * LANES if mode == "mla" else LANES
    pair_blk = pl.BlockSpec((TQ, bw), lambda p, i: (i, p))
    full_blk = pl.BlockSpec((s, bw), lambda p, i: (0, p))
    stat_blk = pl.BlockSpec((1, TQ, LANES), lambda p, i: (p, i, 0))
    in_specs = [q_spec, k_spec, v_spec, pair_blk, pair_blk]
    args = [qa, ka, va, o, do]
    if has_stat:
        in_specs.append(stat_blk)
        args.append(stat)
    if mode == "fox":
        in_specs += [pl.BlockSpec((TQ, LANES), lambda p, i: (i, 0)), pl.BlockSpec((8, s), lambda p, i: (0, 0))]
        args += [cum_col, cum_row]
    out_specs = [pair_blk, full_blk, full_blk]
    out_shape = [jax.ShapeDtypeStruct((s, nblk * bw), F32)] * 3
    if mode == "fox":
        out_specs += [pl.BlockSpec((1, 8, s), lambda p, i: (p, 0, 0)), stat_blk]
        out_shape += [jax.ShapeDtypeStruct((2, 8, s), F32), jax.ShapeDtypeStruct((2, s, LANES), F32)]
    return pl.pallas_call(
        body, name=mode + "_bwd", grid=(nblk, nq), in_specs=in_specs, out_specs=out_specs, out_shape=out_shape,
        compiler_params=_cparams(("parallel", "arbitrary")),
    )(*args)


def _seg_mean_matrix():
    r = lax.broadcasted_iota(jnp.int32, (GROUP, GROUP), 0)
    c = lax.broadcasted_iota(jnp.int32, (GROUP, GROUP), 1)
    return jnp.where((r // HEAD) == (c // HEAD), 1.0 / HEAD, 0.0).astype(F32)


def _sigmoid(x):
    return 1.0 / (1.0 + jnp.exp(-x))


def _mix_post(oa, ob, oc, od, proj, g):
    s = oa.shape[0]
    tr = _tile(s, 256)

    def body(a_ref, b_ref, c_ref, d_ref, rg_ref, g_ref, o_ref):
        gv = g_ref[...]
        o_ref[:, 0:GROUP] = _rms(a_ref[...], gv[:, 0:GROUP]).astype(BF16)
        o_ref[:, GROUP:2 * GROUP] = _rms(b_ref[...], gv[:, GROUP:2 * GROUP]).astype(BF16)
        seg = _seg_mean_matrix()
        c = c_ref[...]
        cen = c - _dot_exact(c, seg)
        n = cen * lax.rsqrt(_dot_exact(cen * cen, seg) + EPS)
        rg = rg_ref[...]
        o_ref[:, 2 * GROUP:3 * GROUP] = (n * gv[:, 2 * GROUP:3 * GROUP] * (rg * _sigmoid(rg))).astype(BF16)
        o_ref[:, 3 * GROUP:] = _rms(d_ref[...], gv[:, 3 * GROUP:]).astype(BF16)

    blk = pl.BlockSpec((tr, GROUP), lambda i: (i, 0))
    return pl.pallas_call(
        body, name="mix_post", grid=(s // tr,),
        in_specs=[blk] * 4 + [pl.BlockSpec((tr, GROUP), lambda i: (i, OFF_RG // 2)), pl.BlockSpec((1, D_MODEL), lambda i: (0, 0))],
        out_specs=pl.BlockSpec((tr, D_MODEL), lambda i: (i, 0)), out_shape=jax.ShapeDtypeStruct((s, D_MODEL), BF16),
        compiler_params=_cparams(("parallel",)),
    )(oa, ob, oc, od, proj, g.reshape(1, D_MODEL))


def _mix_post_bwd(dmixed, oa, ob, oc, od, proj, g):
    s = oa.shape[0]
    tr = _tile(s, 256)

    def body(dm_ref, a_ref, b_ref, c_ref, d_ref, rg_ref, g_ref, da_ref, db_ref, dc_ref, dd_ref, drg_ref, dg_ref):
        @pl.when(pl.program_id(0) == 0)
        def _():
            dg_ref[...] = jnp.zeros_like(dg_ref)

        gv = g_ref[...]
        dm = dm_ref[...]
        for k, (x_ref, dx_ref) in enumerate(((a_ref, da_ref), (b_ref, db_ref), (None, None), (d_ref, dd_ref))):
            if x_ref is None:
                continue
            cols = slice(k * GROUP, (k + 1) * GROUP)
            dx, gterm = _rms_bwd(x_ref[...], gv[:, cols], dm[:, cols])
            dx_ref[...] = dx
            dg_ref[:, cols] += jnp.sum(gterm, axis=0, keepdims=True)
        cols = slice(2 * GROUP, 3 * GROUP)
        seg = _seg_mean_matrix()
        c = c_ref[...]
        cen = c - _dot_exact(c, seg)
        rstd = lax.rsqrt(_dot_exact(cen * cen, seg) + EPS)
        n = cen * rstd
        rg = rg_ref[...]
        sg = _sigmoid(rg)
        gate = rg * sg
        dy = dm[:, cols]
        gc = gv[:, cols]
        dn = dy * gc * gate
        dg_ref[:, cols] += jnp.sum(dy * n * gate, axis=0, keepdims=True)
        drg_ref[...] = (dy * n * gc * (sg * (1.0 + rg * (1.0 - sg)))).astype(BF16)
        dc_ref[...] = rstd * (dn - _dot_exact(dn, seg) - n * _dot_exact(dn * n, seg))

    blk = pl.BlockSpec((tr, GROUP), lambda i: (i, 0))
    gsp = pl.BlockSpec((1, D_MODEL), lambda i: (0, 0))
    return pl.pallas_call(
        body, name="mix_post_bwd", grid=(s // tr,),
        in_specs=[pl.BlockSpec((tr, D_MODEL), lambda i: (i, 0))] + [blk] * 4 + [pl.BlockSpec((tr, GROUP), lambda i: (i, OFF_RG // 2)), gsp],
        out_specs=[blk] * 5 + [gsp],
        out_shape=[jax.ShapeDtypeStruct((s, GROUP), F32)] * 4 + [jax.ShapeDtypeStruct((s, GROUP), BF16), jax.ShapeDtypeStruct((1, D_MODEL), F32)],
        compiler_params=_cparams(("arbitrary",)),
    )(dmixed, oa, ob, oc, od, proj, g.reshape(1, D_MODEL))


def _pack_w_in(w):
    z = lambda n: jnp.zeros((w.shape[0], n), w.dtype)
    misc = jnp.concatenate([w[:, 768:772], z(KR_LANE - N_HEADS), w[:, 1156:1188], z(LANES - KR_LANE - ROPE_DIM)], axis=1)
    return jnp.concatenate([w[:, 0:768], w[:, 772:1028], w[:, 1188:2980], w[:, 1028:1156], misc], axis=1)


def _unpack_dw_in(d):
    m = OFF_MISC * LANES
    return jnp.concatenate([d[:, 0:768], d[:, m:m + N_HEADS], d[:, 768:1024], d[:, OFF_CKV * LANES:m],
                            d[:, m + KR_LANE:m + KR_LANE + ROPE_DIM], d[:, 1024:OFF_CKV * LANES]], axis=1)


def _pack_w_q(w):
    return jnp.pad(w.reshape(Q_RANK, N_HEADS, HEAD + ROPE_DIM), ((0, 0), (0, 0), (0, LANES - HEAD - ROPE_DIM))).reshape(Q_RANK, 4 * LANES)


def _unpack_dw_q(d):
    return d.reshape(Q_RANK, N_HEADS, LANES)[:, :, :HEAD + ROPE_DIM].reshape(Q_RANK, N_HEADS * (HEAD + ROPE_DIM))


def _pack_w_kv(w):
    w4 = w.reshape(KV_RANK, N_HEADS, 2 * HEAD)
    widen = lambda a: jnp.pad(a, ((0, 0), (0, 0), (0, LANES - HEAD))).reshape(KV_RANK, N_HEADS * LANES)
    return widen(w4[:, :, :HEAD]), widen(w4[:, :, HEAD:])


def _unpack_dw_kv(dk, dv):
    narrow = lambda a: a.reshape(KV_RANK, N_HEADS, LANES)[:, :, :HEAD]
    return jnp.concatenate([narrow(dk), narrow(dv)], axis=2).reshape(KV_RANK, 2 * N_HEADS * HEAD)


def _narrow_heads(a):
    return a.reshape(a.shape[0], N_HEADS, LANES)[:, :, :HEAD].reshape(a.shape[0], N_HEADS * HEAD)


def _widen_heads(a):
    return jnp.pad(a.reshape(a.shape[0], N_HEADS, HEAD), ((0, 0), (0, 0), (0, LANES - HEAD))).reshape(a.shape[0], N_HEADS * LANES)


def _layer_fwd(x, lw, tabs, tag):
    cos_m, sin_m, cos_r, sin_r = tabs
    h1 = _norm_fwd(x, lw["g_mix_pre"], name=tag + "pre_norm")
    proj, projb = _matmul(h1, lw["w_in"], name=tag + "in_proj", also_bf16=True)
    bias_row = jnp.pad(lw["b_forget"], (FF_LANE, LANES - N_HEADS - FF_LANE)).reshape(1, LANES)
    cum_col, cum_row = _fox_cum(proj, bias_row)
    oa, lse_a = _mixer_fwd("fox", projb, OFF_FQ, projb, OFF_FK, projb, OFF_FV, cum_col=cum_col, cum_row=cum_row)
    qm, km, vm, cqn, ckvn = _mla_prep(proj, cos_m, sin_m, lw["g_q_lora"], lw["g_kv_lora"], lw["wq"], lw["wk"], lw["wv"])
    ob_wide, lse_b = _mixer_fwd("mla", qm, 0, km, 0, vm, 0)
    ob = _narrow_heads(ob_wide)
    qr, kr = _ret_prep(proj, cos_r, sin_r)
    oc, _ = _mixer_fwd("ret", qr, 0, kr, 0, projb, OFF_RV)
    od, tot_d = _mixer_fwd("sb", projb, OFF_SQ, projb, OFF_SK, projb, OFF_SV)
    mixed = _mix_post(oa, ob, oc, od, proj, lw["g_mix_out"])
    mix = _matmul(mixed, lw["w_out"], name=tag + "out_proj")
    x1 = _norm_fwd(mix, lw["g_mix_post"], name=tag + "mix_post_norm", resid=x, out_dtype=F32)
    h2 = _norm_fwd(x1, lw["g_ffn_pre"], name=tag + "ffn_pre_norm")
    u = _matmul(h2, lw["w_ffn_up"], name=tag + "ffn_up", relu2=True, out_dtype=BF16)
    f = _matmul(u, lw["w_ffn_down"], name=tag + "ffn_down")
    x2 = _norm_fwd(f, lw["g_ffn_post"], name=tag + "ffn_post_norm", resid=x1, out_dtype=F32)
    saved = dict(x=x, h1=h1, proj=proj, projb=projb, bias_row=bias_row, cum_col=cum_col, cum_row=cum_row, oa=oa, lse_a=lse_a,
                 qm=qm, km=km, vm=vm, cqn=cqn, ckvn=ckvn, ob=ob, ob_wide=ob_wide, lse_b=lse_b, qr=qr, kr=kr, oc=oc, od=od, tot_d=tot_d, mixed=mixed,
                 mix=mix, x1=x1, h2=h2, u=u, f=f)
    return x2, saved


def _layer_bwd(dx2, lw, sv, tabs, tag):
    cos_m, sin_m, cos_r, sin_r = tabs
    g = {}
    df, g["g_ffn_post"] = _norm_bwd(sv["f"], lw["g_ffn_post"], dx2, name=tag + "ffn_post_norm_bwd", out_dtype=BF16)
    du_pre = _matmul(df, lw["w_ffn_down"], name=tag + "ffn_down_dx", tb=True, out_dtype=BF16, relu2_of=sv["u"])
    g["w_ffn_down"] = _matmul(sv["u"], df, name=tag + "ffn_down_dw", ta=True)
    dh2 = _matmul(du_pre, lw["w_ffn_up"], name=tag + "ffn_up_dx", tb=True)
    g["w_ffn_up"] = _matmul(sv["h2"], du_pre, name=tag + "ffn_up_dw", ta=True)
    dx1, g["g_ffn_pre"] = _norm_bwd(sv["x1"], lw["g_ffn_pre"], dh2, name=tag + "ffn_pre_norm_bwd", add=dx2)
    dmix, g["g_mix_post"] = _norm_bwd(sv["mix"], lw["g_mix_post"], dx1, name=tag + "mix_post_norm_bwd", out_dtype=BF16)
    dmixed = _matmul(dmix, lw["w_out"], name=tag + "out_proj_dx", tb=True)
    g["w_out"] = _matmul(sv["mixed"], dmix, name=tag + "out_proj_dw", ta=True)
    proj, projb = sv["proj"], sv["projb"]
    doa, dob, doc, dod, drg, g["g_mix_out"] = _mix_post_bwd(dmixed, sv["oa"], sv["ob"], sv["oc"], sv["od"], proj, lw["g_mix_out"])
    dfq, dfk, dfv, dck, drs = _mixer_bwd("fox", projb, OFF_FQ, projb, OFF_FK, projb, OFF_FV, sv["oa"], doa, stat=sv["lse_a"],
                                         cum_col=sv["cum_col"], cum_row=sv["cum_row"])
    dqm, dkm, dvm = _mixer_bwd("mla", sv["qm"], 0, sv["km"], 0, sv["vm"], 0, sv["ob_wide"], _widen_heads(dob), stat=sv["lse_b"])
    dcq, dckv, dkr, dwq, dwk, dwv, g["g_q_lora"], g["g_kv_lora"] = _mla_prep_bwd(
        dqm, dkm, dvm, proj, sv["cqn"], sv["ckvn"], cos_m, sin_m, lw["g_q_lora"], lw["g_kv_lora"], lw["wq"], lw["wk"], lw["wv"])
    dqr, dkr_ret, drv = _mixer_bwd("ret", sv["qr"], 0, sv["kr"], 0, projb, OFF_RV, sv["oc"], doc)
    drq, drk = _ret_prep_bwd(dqr, dkr_ret, cos_r, sin_r)
    dsq, dsk, dsv = _mixer_bwd("sb", projb, OFF_SQ, projb, OFF_SK, projb, OFF_SV, sv["od"], dod, stat=sv["tot_d"])
    dmisc, db_row = _fox_gate_bwd(dck, drs, proj, sv["bias_row"], dkr)
    b = lambda a: a.astype(BF16)
    dproj = jnp.concatenate([b(dfq), b(dfk), b(dfv), dcq, drq, drk, b(drv), drg, b(dsq), b(dsk), b(dsv), dckv, dmisc], axis=1)
    dh1 = _matmul(dproj, lw["w_in"], name=tag + "in_proj_dx", tb=True)
    g["w_in"] = _matmul(sv["h1"], dproj, name=tag + "in_proj_dw", ta=True)
    dx, g["g_mix_pre"] = _norm_bwd(sv["x"], lw["g_mix_pre"], dh1, name=tag + "pre_norm_bwd", add=dx1)
    g["b_forget"] = db_row[0, FF_LANE:FF_LANE + N_HEADS]
    g["wq"], g["wk"], g["wv"] = dwq, dwk, dwv
    return dx, g


def _local_step(x, positions, layers, target):
    s = x.shape[0]
    tabs = _rope_tables(positions.reshape(s, 1))
    saved = []
    for li, lw in enumerate(layers):
        x, sv = _layer_fwd(x, lw, tabs, "l%d_" % li)
        saved.append(sv)
    loss_row, dx = _loss_head(x, target)
    grads = [None] * len(layers)
    for li in reversed(range(len(layers))):
        dx, grads[li] = _layer_bwd(dx, layers[li], saved[li], tabs, "l%d_" % li)
    return loss_row[0, 0], dx, grads


def _adamw(w, g, m, v, *, name):
    r, c = w.shape
    tr = 256 if r % 256 == 0 else r
    blk = pl.BlockSpec((tr, c), lambda i: (i, 0))
    c1 = 1.0 - ADAM_B1 ** ADAM_STEP
    c2 = 1.0 - ADAM_B2 ** ADAM_STEP

    def body(w_ref, g_ref, m_ref, v_ref, d_ref, mo_ref, vo_ref):
        gv = g_ref[...]
        mn = ADAM_B1 * m_ref[...] + (1.0 - ADAM_B1) * gv
        vn = ADAM_B2 * v_ref[...] + (1.0 - ADAM_B2) * jnp.square(gv)
        mo_ref[...] = mn
        vo_ref[...] = vn
        d_ref[...] = -ADAM_LR * ((mn / c1) / (jnp.sqrt(vn / c2) + ADAM_EPS) + ADAM_WD * w_ref[...])

    return pl.pallas_call(
        body, name=name, grid=(r // tr,), in_specs=[blk] * 4, out_specs=[blk] * 3,
        out_shape=[jax.ShapeDtypeStruct((r, c), F32)] * 3, compiler_params=_cparams(("parallel",)),
    )(w, g, m, v)


BIG = ("w_in", "w_q_up", "w_kv_up", "w_out", "w_ffn_up", "w_ffn_down")
SMALL = ("g_mix_pre", "b_forget", "g_q_lora", "g_kv_lora", "g_mix_out", "g_mix_post", "g_ffn_pre", "g_ffn_post")
N_CHIPS = 4
ANY = pl.BlockSpec(memory_space=pl.ANY)


def _mesh_pos():
    return lax.axis_index("x"), lax.axis_index("y"), lax.axis_index("c")


def _other_chips(x, y):
    return [(1 - x, y), (x, 1 - y), (1 - x, 1 - y)]


def _rows_half(ref, half):
    h = ref.shape[-2] // 2
    return ref.at[(slice(None),) * (len(ref.shape) - 2) + (pl.ds(half * h, h), slice(None))]


def _remote(src, dst, send_sem, recv_sem, device):
    return pltpu.make_async_remote_copy(src_ref=src, dst_ref=dst, send_sem=send_sem, recv_sem=recv_sem, device_id=device,
                                        device_id_type=MESH)


def _comm_call(body, name, args, out_shape, n_sems):
    return pl.pallas_call(
        body, name=name, in_specs=[ANY] * len(args), out_specs=[ANY] * len(out_shape), out_shape=out_shape,
        scratch_shapes=[pltpu.SemaphoreType.DMA((n_sems,)), pltpu.SemaphoreType.DMA((n_sems,))],
        compiler_params=pltpu.CompilerParams(has_side_effects=True),
    )(*args)


def _gather_weights(shards):
    n = len(shards)

    def body(*refs):
        ins, outs = refs[:n], refs[n:2 * n]
        ici_send, ici_recv, d2d_send, d2d_recv = refs[2 * n:]
        x, y, c = _mesh_pos()
        mine = 2 * x + y
        peers = _other_chips(x, y)

        def ici(t, j, block):
            px, py = peers[j]
            return _remote(_rows_half(ins[t], c), _rows_half(outs[t].at[block], c), ici_send.at[3 * t + j], ici_recv.at[3 * t + j],
                           (px, py, c))

        def d2d(t, j, block, half):
            region = _rows_half(outs[t].at[block], half)
            return _remote(region, region, d2d_send.at[3 * t + j], d2d_recv.at[3 * t + j], (x, y, 1 - c))

        sends = [ici(t, j, mine) for t in range(n) for j in range(3)]
        for cp in sends:
            cp.start()
        passed = []
        for t in range(n):
            for j, (px, py) in enumerate(peers):
                ici(t, j, 2 * px + py).wait_recv()
                fwd = d2d(t, j, 2 * px + py, c)
                fwd.start()
                passed.append(fwd)
        for t in range(n):
            for j, (px, py) in enumerate(peers):
                d2d(t, j, 2 * px + py, 1 - c).wait_recv()
        for cp in sends + passed:
            cp.wait_send()

    out_shape = [jax.ShapeDtypeStruct((N_CHIPS,) + a.shape, a.dtype) for a in shards]
    return pl.pallas_call(
        body, name="gather_weights", in_specs=[ANY] * n, out_specs=[ANY] * n, out_shape=out_shape,
        scratch_shapes=[pltpu.SemaphoreType.DMA((3 * n,))] * 4,
        compiler_params=pltpu.CompilerParams(has_side_effects=True),
    )(*shards)


def _exchange_halves(gs):
    n = len(gs)

    def body(*refs):
        ins, outs, send_sems, recv_sems = refs[:n], refs[n:2 * n], refs[2 * n], refs[2 * n + 1]
        x, y, c = _mesh_pos()
        cps = [_remote(_rows_half(ins[t], 1 - c), outs[t], send_sems.at[t], recv_sems.at[t], (x, y, 1 - c)) for t in range(n)]
        for cp in cps:
            cp.start()
        for cp in cps:
            cp.wait_recv()
        for cp in cps:
            cp.wait_send()

    out_shape = [jax.ShapeDtypeStruct(g.shape[:2] + (g.shape[2] // 2, g.shape[3]), g.dtype) for g in gs]
    return _comm_call(body, "grad_pair_exchange", gs, out_shape, n)


def _pair_add(g, r, c_idx, *, name):
    nb, d, rows, cols = g.shape
    h = rows // 2
    tr = min(h, 512)
    nt = h // tr

    def body(c_ref, g_ref, r_ref, p_ref, pb_ref):
        s = g_ref[...] + r_ref[...]
        p_ref[...] = s
        pb_ref[...] = s.astype(BF16)

    blk = pl.BlockSpec((1, 1, tr, cols), lambda k, l, i, c_ref: (k, l, i, 0))
    return pl.pallas_call(
        body, name=name,
        grid_spec=pltpu.PrefetchScalarGridSpec(
            num_scalar_prefetch=1, grid=(nb, d, nt),
            in_specs=[pl.BlockSpec((1, 1, tr, cols), lambda k, l, i, c_ref: (k, l, c_ref[0] * nt + i, 0)), blk],
            out_specs=[blk, blk]),
        out_shape=[jax.ShapeDtypeStruct((nb, d, h, cols), F32), jax.ShapeDtypeStruct((nb, d, h, cols), BF16)],
        compiler_params=_cparams(("parallel", "parallel", "parallel")),
    )(c_idx, g, r)


def _exchange_chips(pbs):
    n = len(pbs)

    def body(*refs):
        ins, outs, send_sems, recv_sems = refs[:n], refs[n:2 * n], refs[2 * n], refs[2 * n + 1]
        x, y, c = _mesh_pos()
        cps = [_remote(ins[t].at[2 * px + py], outs[t].at[j], send_sems.at[3 * t + j], recv_sems.at[3 * t + j], (px, py, c))
               for t in range(n) for j, (px, py) in enumerate(_other_chips(x, y))]
        for cp in cps:
            cp.start()
        for cp in cps:
            cp.wait_recv()
        for cp in cps:
            cp.wait_send()

    out_shape = [jax.ShapeDtypeStruct((3,) + p.shape[1:], p.dtype) for p in pbs]
    return _comm_call(body, "grad_chip_exchange", pbs, out_shape, 3 * n)


def _chip_add(p, r, k_idx, *, name):
    _, d, h, cols = p.shape
    tr = min(h, 512)
    nt = h // tr

    def body(k_ref, p_ref, r_ref, o_ref):
        o_ref[0] = ((p_ref[0, 0] + r_ref[0, 0].astype(F32)) + r_ref[1, 0].astype(F32)) + r_ref[2, 0].astype(F32)

    return pl.pallas_call(
        body, name=name,
        grid_spec=pltpu.PrefetchScalarGridSpec(
            num_scalar_prefetch=1, grid=(d, nt),
            in_specs=[pl.BlockSpec((1, 1, tr, cols), lambda l, i, k_ref: (k_ref[0], l, i, 0)),
                      pl.BlockSpec((3, 1, tr, cols), lambda l, i, k_ref: (0, l, i, 0))],
            out_specs=pl.BlockSpec((1, tr, cols), lambda l, i, k_ref: (l, i, 0))),
        out_shape=jax.ShapeDtypeStruct((d, h, cols), F32), compiler_params=_cparams(("parallel", "parallel")),
    )(k_idx, p, r)


def _share_halves(qs):
    n = len(qs)

    def body(*refs):
        ins, outs, send_sems, recv_sems = refs[:n], refs[n:2 * n], refs[2 * n], refs[2 * n + 1]
        x, y, c = _mesh_pos()
        cps = [_remote(ins[t], outs[t], send_sems.at[t], recv_sems.at[t], (x, y, 1 - c)) for t in range(n)]
        for cp in cps:
            cp.start()
        for cp in cps:
            cp.wait_recv()
        for cp in cps:
            cp.wait_send()

    return _comm_call(body, "grad_pair_share", qs, [jax.ShapeDtypeStruct(q.shape, q.dtype) for q in qs], n)


def _all_reduce_small(v):
    r, cols = v.shape
    n_dev = 8

    def body(v_ref, o_ref, buf, send_sems, recv_sems):
        x, y, c = _mesh_pos()
        me = 4 * x + 2 * y + c
        buf[me] = v_ref[...]

        def peer(j):
            return (1 - x if j & 4 else x, 1 - y if j & 2 else y, 1 - c if j & 1 else c)

        def copy(j, slot):
            return pltpu.make_async_remote_copy(src_ref=v_ref, dst_ref=buf.at[slot], send_sem=send_sems.at[j - 1],
                                                recv_sem=recv_sems.at[j - 1], device_id=peer(j), device_id_type=MESH)

        sends = [copy(j, me) for j in range(1, n_dev)]
        for cp in sends:
            cp.start()
        for j in range(1, n_dev):
            px, py, pc = peer(j)
            copy(j, 4 * px + 2 * py + pc).wait_recv()
        for cp in sends:
            cp.wait_send()
        acc = buf[0]
        for d in range(1, n_dev):
            acc = acc + buf[d]
        o_ref[...] = acc

    vm = pl.BlockSpec(memory_space=pltpu.VMEM)
    return pl.pallas_call(
        body, name="small_all_reduce", in_specs=[vm], out_specs=vm, out_shape=jax.ShapeDtypeStruct((r, cols), F32),
        scratch_shapes=[pltpu.VMEM((n_dev, r, cols), F32), pltpu.SemaphoreType.DMA((n_dev - 1,)), pltpu.SemaphoreType.DMA((n_dev - 1,))],
        compiler_params=pltpu.CompilerParams(has_side_effects=True),
    )(v)


_COL_SHARDED = ("w_in", "w_q_up", "w_kv_up", "w_ffn_up")


def _shard_cols(blocks, a, b):
    c = blocks.shape[-1]
    out = []
    while a < b:
        k = a // c
        hi = min(b, (k + 1) * c)
        out.append(blocks[k][:, a - k * c:hi - k * c])
        a = hi
    return out


def _pack_w_in_shards(blocks):
    z = lambda n: [jnp.zeros((blocks.shape[1], n), blocks.dtype)]
    cols = lambda a, b: _shard_cols(blocks, a, b)
    return jnp.concatenate(cols(0, 768) + cols(772, 1028) + cols(1188, 2980) + cols(1028, 1156) + cols(768, 772)
                           + z(KR_LANE - N_HEADS) + cols(1156, 1188) + z(LANES - KR_LANE - ROPE_DIM), axis=1)


def _whole_layer(name, blocks):
    if name in _COL_SHARDED:
        return jnp.concatenate([blocks[k] for k in range(N_CHIPS)], axis=1)
    return blocks.reshape(N_CHIPS * blocks.shape[1], blocks.shape[2])


def _split_layer(name, whole):
    if name in _COL_SHARDED:
        c = whole.shape[1] // N_CHIPS
        return jnp.stack([whole[:, k * c:(k + 1) * c] for k in range(N_CHIPS)])
    return whole.reshape(N_CHIPS, whole.shape[0] // N_CHIPS, whole.shape[1])


def _small_to_rows(d):
    v = jnp.concatenate([d[k].astype(F32).reshape(-1) for k in SMALL])
    rows = -(-v.shape[0] // (8 * LANES)) * 8
    return jnp.pad(v, (0, rows * LANES - v.shape[0])).reshape(rows, LANES)


def _small_from_rows(rows, shapes):
    v = rows.reshape(-1)
    out, o = {}, 0
    for k in SMALL:
        sz = int(np.prod(shapes[k]))
        out[k] = v[o:o + sz].reshape(shapes[k])
        o += sz
    return out


_ARG_NAMES = ("x", "positions", "g_mix_pre", "w_in", "b_forget", "g_q_lora", "w_q_up", "g_kv_lora", "w_kv_up", "g_mix_out", "w_out",
              "g_mix_post", "g_ffn_pre", "w_ffn_up", "w_ffn_down", "g_ffn_post")
_WEIGHTS = _ARG_NAMES[2:]


def kernel(x, positions, g_mix_pre, w_in, b_forget, g_q_lora, w_q_up, g_kv_lora, w_kv_up, g_mix_out, w_out, g_mix_post, g_ffn_pre, w_ffn_up, w_ffn_down, g_ffn_post, loss_target, m_g_mix_pre, m_w_in, m_b_forget, m_g_q_lora, m_w_q_up, m_g_kv_lora, m_w_kv_up, m_g_mix_out, m_w_out, m_g_mix_post, m_g_ffn_pre, m_w_ffn_up, m_w_ffn_down, m_g_ffn_post, v_g_mix_pre, v_w_in, v_b_forget, v_g_q_lora, v_w_q_up, v_g_kv_lora, v_w_kv_up, v_g_mix_out, v_w_out, v_g_mix_post, v_g_ffn_pre, v_w_ffn_up, v_w_ffn_down, v_g_ffn_post):
    w = dict(g_mix_pre=g_mix_pre, w_in=w_in, b_forget=b_forget, g_q_lora=g_q_lora, w_q_up=w_q_up, g_kv_lora=g_kv_lora, w_kv_up=w_kv_up,
             g_mix_out=g_mix_out, w_out=w_out, g_mix_post=g_mix_post, g_ffn_pre=g_ffn_pre, w_ffn_up=w_ffn_up, w_ffn_down=w_ffn_down,
             g_ffn_post=g_ffn_post)
    m = dict(g_mix_pre=m_g_mix_pre, w_in=m_w_in, b_forget=m_b_forget, g_q_lora=m_g_q_lora, w_q_up=m_w_q_up, g_kv_lora=m_g_kv_lora,
             w_kv_up=m_w_kv_up, g_mix_out=m_g_mix_out, w_out=m_w_out, g_mix_post=m_g_mix_post, g_ffn_pre=m_g_ffn_pre,
             w_ffn_up=m_w_ffn_up, w_ffn_down=m_w_ffn_down, g_ffn_post=m_g_ffn_post)
    v = dict(g_mix_pre=v_g_mix_pre, w_in=v_w_in, b_forget=v_b_forget, g_q_lora=v_g_q_lora, w_q_up=v_w_q_up, g_kv_lora=v_g_kv_lora,
             w_kv_up=v_w_kv_up, g_mix_out=v_g_mix_out, w_out=v_w_out, g_mix_post=v_g_mix_post, g_ffn_pre=v_g_ffn_pre,
             w_ffn_up=v_w_ffn_up, w_ffn_down=v_w_ffn_down, g_ffn_post=v_g_ffn_post)
    shard_shapes = {k: w[k].shape for k in BIG}
    small_shapes = {k: w[k].shape for k in SMALL}
    c_idx = lax.axis_index("c").astype(jnp.int32).reshape(1)
    k_idx = (2 * lax.axis_index("x") + lax.axis_index("y")).astype(jnp.int32).reshape(1)

    mine = 2 * lax.axis_index("x") + lax.axis_index("y")
    shards_b = [w[k].astype(BF16) for k in BIG]
    gathered = _gather_weights(shards_b)
    four = {k: lax.dynamic_update_slice(g, s[None], (mine, 0, 0, 0)) for k, g, s in zip(BIG, gathered, shards_b)}
    layers = []
    for l in range(DEPTH):
        wk, wv = _pack_w_kv(_whole_layer("w_kv_up", four["w_kv_up"][:, l]))
        layers.append(dict(
            g_mix_pre=g_mix_pre[l], w_in=_pack_w_in_shards(four["w_in"][:, l]), b_forget=b_forget[l], g_q_lora=g_q_lora[l],
            g_kv_lora=g_kv_lora[l], wq=_pack_w_q(_whole_layer("w_q_up", four["w_q_up"][:, l])), wk=wk, wv=wv, g_mix_out=g_mix_out[l],
            w_out=_whole_layer("w_out", four["w_out"][:, l]), g_mix_post=g_mix_post[l], g_ffn_pre=g_ffn_pre[l],
            w_ffn_up=_whole_layer("w_ffn_up", four["w_ffn_up"][:, l]), w_ffn_down=_whole_layer("w_ffn_down", four["w_ffn_down"][:, l]),
            g_ffn_post=g_ffn_post[l]))

    loss_local, dx, grads = _local_step(x[0], positions[0], layers, loss_target[0])
    loss = lax.psum(loss_local, ("x", "y", "c"))

    whole_grad = dict(
        w_in=lambda l: _unpack_dw_in(grads[l]["w_in"]), w_q_up=lambda l: _unpack_dw_q(grads[l]["wq"]),
        w_kv_up=lambda l: _unpack_dw_kv(grads[l]["wk"], grads[l]["wv"]), w_out=lambda l: grads[l]["w_out"],
        w_ffn_up=lambda l: grads[l]["w_ffn_up"], w_ffn_down=lambda l: grads[l]["w_ffn_down"])
    blocks = [jnp.stack([_split_layer(k, whole_grad[k](l)) for l in range(DEPTH)], axis=1) for k in BIG]
    theirs = _exchange_halves(blocks)
    pair = [_pair_add(g, r, c_idx, name="grad_pair_add_" + k) for k, g, r in zip(BIG, blocks, theirs)]
    partial = _exchange_chips([pb for (_, pb) in pair])
    mine_half = [_chip_add(p, r, k_idx, name="grad_chip_add_" + k) for k, (p, _), r in zip(BIG, pair, partial)]
    sibling_half = _share_halves(mine_half)
    first = lax.axis_index("c") == 0
    g_big = {k: jnp.where(first, jnp.concatenate([q, s], axis=1), jnp.concatenate([s, q], axis=1))
             for k, q, s in zip(BIG, mine_half, sibling_half)}

    g_small_local = {k: jnp.stack([grads[l][k].reshape(small_shapes[k][1:]) for l in range(DEPTH)]) for k in SMALL}
    g_small = _small_from_rows(_all_reduce_small(_small_to_rows(g_small_local)), small_shapes)

    g_all = {**g_big, **g_small}
    delta, new_m, new_v = {}, {}, {}
    for k in BIG:
        d, r, c = shard_shapes[k]
        two_d = lambda a: a.reshape(d * r, c)
        dk, mk, vk = _adamw(two_d(w[k]), two_d(g_all[k]), two_d(m[k]), two_d(v[k]), name="adamw_" + k)
        delta[k], new_m[k], new_v[k] = dk.reshape(d, r, c), mk.reshape(d, r, c), vk.reshape(d, r, c)
    ds, ms, vs = _adamw(_small_to_rows(w), _small_to_rows(g_small), _small_to_rows(m), _small_to_rows(v), name="adamw_small")
    delta.update(_small_from_rows(ds, small_shapes))
    new_m.update(_small_from_rows(ms, small_shapes))
    new_v.update(_small_from_rows(vs, small_shapes))

    grad_x = dx.reshape(x.shape)
    return (loss, grad_x, *[g_all[k] for k in _WEIGHTS], *[delta[k] for k in _WEIGHTS], *[new_m[k] for k in _WEIGHTS],
            *[new_v[k] for k in _WEIGHTS])
```

```python
import functools
import math

import numpy as np
import jax
import jax.numpy as jnp
from jax import lax
from jax.experimental import pallas as pl
from jax.experimental.pallas import tpu as pltpu

F32 = jnp.float32
BF16 = jnp.bfloat16
MESH = pl.DeviceIdType.MESH

D_MODEL = 1024
DEPTH = 2
CHUNK = 64
GROUP = 256
HEAD = 64
N_HEADS = 4
Q_RANK = 256
KV_RANK = 128
ROPE_DIM = 32
D_FF = 4096
D_IN = 2980
D_INP = 3072
ROPE_BASE = 10000.0
EPS = 1e-6
LANES = 128
TQ = 128
NEG = -1e30

ADAM_LR, ADAM_B1, ADAM_B2, ADAM_EPS, ADAM_WD, ADAM_STEP = 0.001, 0.9, 0.999, 1e-08, 0.01, 10

OFF_FQ, OFF_FK, OFF_FV, OFF_CQ = 0, 2, 4, 6
OFF_RQ, OFF_RK, OFF_RV, OFF_RG = 8, 10, 12, 14
OFF_SQ, OFF_SK, OFF_SV = 16, 18, 20
OFF_CKV, OFF_MISC = 22, 23
FF_LANE, KR_LANE = 0, 64

VMEM_LIMIT = 56 * 1024 * 1024


def _tile(dim, pref):
    return pref if dim % pref == 0 else dim


def _cparams(sem, vmem=None):
    return pltpu.CompilerParams(dimension_semantics=sem, vmem_limit_bytes=vmem or VMEM_LIMIT)


def _dot(a, b):
    return jnp.dot(a, b, preferred_element_type=F32)


def _dot_nt(a, b):
    return lax.dot_general(a, b, (((1,), (1,)), ((), ())), preferred_element_type=F32)


def _dot_tn(a, b):
    return lax.dot_general(a, b, (((0,), (0,)), ((), ())), preferred_element_type=F32)


def _dot_exact(a, b):
    return jnp.dot(a, b, precision=lax.Precision.HIGHEST, preferred_element_type=F32)


def _matmul(a, b, *, name, ta=False, tb=False, out_dtype=F32, tm=1024, tn=1024, tk=1024,
            relu2=False, relu2_of=None, also_bf16=False):
    if ta:
        kdim, m = a.shape
    else:
        m, kdim = a.shape
    n = b.shape[0] if tb else b.shape[1]
    tm, tn, tk = _tile(m, tm), _tile(n, tn), _tile(kdim, tk)
    nk = kdim // tk
    a_spec = pl.BlockSpec((tk, tm), lambda i, j, k: (k, i)) if ta else pl.BlockSpec((tm, tk), lambda i, j, k: (i, k))
    b_spec = pl.BlockSpec((tn, tk), lambda i, j, k: (j, k)) if tb else pl.BlockSpec((tk, tn), lambda i, j, k: (k, j))
    o_spec = pl.BlockSpec((tm, tn), lambda i, j, k: (i, j))
    two = also_bf16

    def body(*refs):
        refs = list(refs)
        a_ref, b_ref = refs[0], refs[1]
        e_ref = refs[2] if relu2_of is not None else None
        pos = 3 if relu2_of is not None else 2
        o_ref = refs[pos]
        o2_ref = refs[pos + 1] if two else None
        acc_ref = refs[-1]
        k = pl.program_id(2)
        av = a_ref[...].astype(BF16)
        bv = b_ref[...].astype(BF16)
        if ta:
            part = _dot_tn(av, bv)
        elif tb:
            part = _dot_nt(av, bv)
        else:
            part = _dot(av, bv)

        @pl.when(k == 0)
        def _():
            acc_ref[...] = part

        @pl.when(k > 0)
        def _():
            acc_ref[...] += part

        @pl.when(k == nk - 1)
        def _():
            r = acc_ref[...]
            if relu2_of is not None:
                r = r * (2.0 * jnp.sqrt(e_ref[...].astype(F32)))
            if relu2:
                r = jnp.square(jnp.maximum(r, 0.0))
            o_ref[...] = r.astype(o_ref.dtype)
            if also_bf16:
                o2_ref[...] = r.astype(BF16)

    in_specs = [a_spec, b_spec]
    args = [a, b]
    if relu2_of is not None:
        in_specs.append(o_spec)
        args.append(relu2_of)
    out_shape = [jax.ShapeDtypeStruct((m, n), out_dtype)]
    out_specs = [o_spec]
    if two:
        out_shape.append(jax.ShapeDtypeStruct((m, n), BF16))
        out_specs.append(o_spec)
    res = pl.pallas_call(
        body, name=name, grid=(m // tm, n // tn, nk), in_specs=in_specs, out_specs=out_specs, out_shape=out_shape,
        scratch_shapes=[pltpu.VMEM((tm, tn), F32)],
        compiler_params=_cparams(("parallel", "parallel", "arbitrary")),
    )(*args)
    return res if two else res[0]


def _rms(x, g):
    r = lax.rsqrt(jnp.mean(x * x, axis=-1, keepdims=True) + EPS)
    return x * r * g


def _rms_bwd(x, g, dy):
    r = lax.rsqrt(jnp.mean(x * x, axis=-1, keepdims=True) + EPS)
    xh = x * r
    gdy = dy * g
    dx = r * (gdy - xh * jnp.mean(xh * gdy, axis=-1, keepdims=True))
    return dx, xh * dy


def _norm_fwd(x, g, *, name, resid=None, out_dtype=BF16):
    s, d = x.shape
    tr = _tile(s, 256)
    row = pl.BlockSpec((tr, d), lambda i: (i, 0))
    gsp = pl.BlockSpec((1, d), lambda i: (0, 0))

    def body(*refs):
        if resid is None:
            x_ref, g_ref, o_ref = refs
            o_ref[...] = _rms(x_ref[...], g_ref[...]).astype(o_ref.dtype)
        else:
            x_ref, g_ref, r_ref, o_ref = refs
            o_ref[...] = (r_ref[...] + _rms(x_ref[...], g_ref[...])).astype(o_ref.dtype)

    args = [x, g.reshape(1, d)] + ([] if resid is None else [resid])
    return pl.pallas_call(
        body, name=name, grid=(s // tr,), in_specs=[row, gsp] + ([] if resid is None else [row]),
        out_specs=row, out_shape=jax.ShapeDtypeStruct((s, d), out_dtype), compiler_params=_cparams(("parallel",)),
    )(*args)


def _norm_bwd(x, g, dy, *, name, add=None, out_dtype=F32):
    s, d = x.shape
    tr = _tile(s, 256)
    row = pl.BlockSpec((tr, d), lambda i: (i, 0))
    gsp = pl.BlockSpec((1, d), lambda i: (0, 0))

    def body(*refs):
        if add is None:
            x_ref, g_ref, dy_ref, dx_ref, dg_ref = refs
        else:
            x_ref, g_ref, dy_ref, add_ref, dx_ref, dg_ref = refs
        dx, gterm = _rms_bwd(x_ref[...], g_ref[...], dy_ref[...].astype(F32))
        if add is not None:
            dx = dx + add_ref[...]
        dx_ref[...] = dx.astype(dx_ref.dtype)

        @pl.when(pl.program_id(0) == 0)
        def _():
            dg_ref[...] = jnp.zeros_like(dg_ref)

        dg_ref[...] += jnp.sum(gterm, axis=0, keepdims=True)

    args = [x, g.reshape(1, d), dy] + ([] if add is None else [add])
    return pl.pallas_call(
        body, name=name, grid=(s // tr,), in_specs=[row, gsp, row] + ([] if add is None else [row]),
        out_specs=[row, gsp], out_shape=[jax.ShapeDtypeStruct((s, d), out_dtype), jax.ShapeDtypeStruct((1, d), F32)],
        compiler_params=_cparams(("arbitrary",)),
    )(*args)


def _loss_head(y, target):
    s, d = y.shape
    tr = _tile(s, 256)
    row = pl.BlockSpec((tr, d), lambda i: (i, 0))
    lsp = pl.BlockSpec((1, LANES), lambda i: (0, 0))

    def body(y_ref, t_ref, l_ref, dy_ref):
        e = y_ref[...] - t_ref[...]
        dy_ref[...] = e * (1.0 / d)

        @pl.when(pl.program_id(0) == 0)
        def _():
            l_ref[...] = jnp.zeros_like(l_ref)

        part = 0.5 * jnp.sum(jnp.mean(e * e, axis=-1, keepdims=True), axis=0, keepdims=True)
        l_ref[...] += jnp.broadcast_to(part, (1, LANES))

    return pl.pallas_call(
        body, name="loss_head", grid=(s // tr,), in_specs=[row, row], out_specs=[lsp, row],
        out_shape=[jax.ShapeDtypeStruct((1, LANES), F32), jax.ShapeDtypeStruct((s, d), F32)],
        compiler_params=_cparams(("arbitrary",)),
    )(y, target)


def _rope_tables(pos_col):
    s = pos_col.shape[0]
    tr = _tile(s, 512)
    f_mla = ROPE_BASE ** (-jnp.arange(ROPE_DIM // 2, dtype=F32) / (ROPE_DIM // 2))
    f_ret = ROPE_BASE ** (-jnp.arange(HEAD // 2, dtype=F32) / (HEAD // 2))
    fm = jnp.concatenate([jnp.zeros((64,), F32), f_mla, f_mla, jnp.zeros((32,), F32)]).reshape(1, LANES)
    fr = jnp.tile(jnp.concatenate([f_ret, f_ret]), 4).reshape(1, 2 * LANES)

    def body(p_ref, fm_ref, fr_ref, cm_ref, sm_ref, cr_ref, sr_ref):
        p = p_ref[...].astype(F32)
        am = p * fm_ref[...]
        ar = p * fr_ref[...]
        cm_ref[...] = jnp.cos(am)
        sm_ref[...] = jnp.sin(am)
        cr_ref[...] = jnp.cos(ar)
        sr_ref[...] = jnp.sin(ar)

    return pl.pallas_call(
        body, name="rope_tables", grid=(s // tr,),
        in_specs=[pl.BlockSpec((tr, 1), lambda i: (i, 0)), pl.BlockSpec((1, LANES), lambda i: (0, 0)),
                  pl.BlockSpec((1, 2 * LANES), lambda i: (0, 0))],
        out_specs=[pl.BlockSpec((tr, LANES), lambda i: (i, 0))] * 2 + [pl.BlockSpec((tr, 2 * LANES), lambda i: (i, 0))] * 2,
        out_shape=[jax.ShapeDtypeStruct((s, LANES), F32)] * 2 + [jax.ShapeDtypeStruct((s, 2 * LANES), F32)] * 2,
        compiler_params=_cparams(("parallel",)),
    )(pos_col, fm, fr)


def _lane(shape):
    return lax.broadcasted_iota(jnp.int32, shape, len(shape) - 1)


def _rot_mla(z):
    l = _lane(z.shape) % LANES
    n = z.shape[-1]
    return jnp.where(l < 80, -pltpu.roll(z, n - 16, 1), pltpu.roll(z, 16, 1))


def _rot_mla_t(y):
    l = _lane(y.shape) % LANES
    n = y.shape[-1]
    return jnp.where((l >= 64) & (l < 80), pltpu.roll(y, n - 16, 1),
                     jnp.where((l >= 80) & (l < 96), -pltpu.roll(y, 16, 1), 0.0))


def _rot_ret(z):
    l = _lane(z.shape) % HEAD
    n = z.shape[-1]
    return jnp.where(l < 32, -pltpu.roll(z, n - 32, 1), pltpu.roll(z, 32, 1))


def _rot_ret_t(y):
    l = _lane(y.shape) % HEAD
    n = y.shape[-1]
    return jnp.where(l < 32, pltpu.roll(y, n - 32, 1), -pltpu.roll(y, 32, 1))


def _log_sigmoid(x):
    return jnp.minimum(x, 0.0) - jnp.log1p(jnp.exp(-jnp.abs(x)))


def _fox_cum(proj, bias_row):
    s = proj.shape[0]
    nb = s // TQ

    def body(x_ref, b_ref, cc_ref, cr_ref, carry_ref):
        @pl.when(pl.program_id(0) == 0)
        def _():
            carry_ref[...] = jnp.zeros_like(carry_ref)

        ls = _log_sigmoid(x_ref[...] + b_ref[...])
        r = lax.broadcasted_iota(jnp.int32, (TQ, TQ), 0)
        c = lax.broadcasted_iota(jnp.int32, (TQ, TQ), 1)
        tri = (c <= r).astype(F32)
        cum = _dot_exact(tri, ls) + carry_ref[...]
        carry_ref[...] = cum[TQ - 1:TQ, :]
        cc_ref[...] = cum
        cr_ref[...] = cum.T[0:8, :]

    return pl.pallas_call(
        body, name="fox_cum", grid=(nb,),
        in_specs=[pl.BlockSpec((TQ, LANES), lambda i: (i, OFF_MISC)), pl.BlockSpec((1, LANES), lambda i: (0, 0))],
        out_specs=[pl.BlockSpec((TQ, LANES), lambda i: (i, 0)), pl.BlockSpec((8, TQ), lambda i: (0, i))],
        out_shape=[jax.ShapeDtypeStruct((s, LANES), F32), jax.ShapeDtypeStruct((8, s), F32)],
        scratch_shapes=[pltpu.VMEM((1, LANES), F32)],
        compiler_params=_cparams(("arbitrary",)),
    )(proj, bias_row)


def _fox_gate_bwd(dck, drs, proj, bias_row, dkr):
    s = proj.shape[0]
    nb = s // TQ

    def body(d_ref, r_ref, x_ref, b_ref, k_ref, o_ref, db_ref, carry_ref):
        @pl.when(pl.program_id(0) == 0)
        def _():
            carry_ref[...] = jnp.zeros_like(carry_ref)
            db_ref[...] = jnp.zeros_like(db_ref)

        rows = jnp.concatenate([d_ref[0], d_ref[1], jnp.zeros((TQ - 16, TQ), F32)], axis=0)
        t = rows.T
        l = _lane((TQ, LANES))
        r0, r1 = r_ref[0], r_ref[1]
        rsum = jnp.where(l == 0, r0[:, 0:1], jnp.where(l == 1, r0[:, HEAD:HEAD + 1],
                         jnp.where(l == 2, r1[:, 0:1], jnp.where(l == 3, r1[:, HEAD:HEAD + 1], 0.0))))
        dcum = rsum - jnp.where(l < 2, t, pltpu.roll(t, LANES - 6, 1))
        r = lax.broadcasted_iota(jnp.int32, (TQ, TQ), 0)
        c = lax.broadcasted_iota(jnp.int32, (TQ, TQ), 1)
        triu = (c >= r).astype(F32)
        rc = _dot_exact(triu, dcum) + carry_ref[...]
        carry_ref[...] = rc[0:1, :]
        f = x_ref[...] + b_ref[...]
        sig_neg = 1.0 / (1.0 + jnp.exp(f))
        df = jnp.where(l < N_HEADS, rc * sig_neg, 0.0)
        db_ref[...] += jnp.sum(df, axis=0, keepdims=True)
        o_ref[...] = (df + k_ref[...]).astype(o_ref.dtype)

    rev = lambda i: nb - 1 - i
    return pl.pallas_call(
        body, name="fox_gate_bwd", grid=(nb,),
        in_specs=[pl.BlockSpec((2, 8, TQ), lambda i: (0, 0, rev(i))), pl.BlockSpec((2, TQ, LANES), lambda i: (0, rev(i), 0)),
                  pl.BlockSpec((TQ, LANES), lambda i: (rev(i), OFF_MISC)),
                  pl.BlockSpec((1, LANES), lambda i: (0, 0)), pl.BlockSpec((TQ, LANES), lambda i: (rev(i), 0))],
        out_specs=[pl.BlockSpec((TQ, LANES), lambda i: (rev(i), 0)), pl.BlockSpec((1, LANES), lambda i: (0, 0))],
        out_shape=[jax.ShapeDtypeStruct((s, LANES), BF16), jax.ShapeDtypeStruct((1, LANES), F32)],
        scratch_shapes=[pltpu.VMEM((1, LANES), F32)],
        compiler_params=_cparams(("arbitrary",)),
    )(dck, drs, proj, bias_row, dkr)


def _mla_prep(proj, cos_m, sin_m, g_q, g_kv, wq, wk, wv):
    s = proj.shape[0]
    tr = _tile(s, 256)

    def body(cq_ref, ckv_ref, misc_ref, cos_ref, sin_ref, gq_ref, gkv_ref, wq_ref, wk_ref, wv_ref,
             q_ref, k_ref, v_ref, cqn_ref, ckvn_ref):
        cos4 = jnp.tile(cos_ref[...], (1, 4))
        sin4 = jnp.tile(sin_ref[...], (1, 4))
        cqn = _rms(cq_ref[...], gq_ref[...]).astype(BF16)
        ckvn = _rms(ckv_ref[...], gkv_ref[...]).astype(BF16)
        cqn_ref[...] = cqn
        ckvn_ref[...] = ckvn
        zq = _dot(cqn, wq_ref[...])
        q_ref[...] = (zq * cos4 + _rot_mla(zq) * sin4).astype(BF16)
        l = _lane((tr, LANES))
        kr = jnp.where((l >= KR_LANE) & (l < KR_LANE + ROPE_DIM), misc_ref[...], 0.0)
        zk = _dot(ckvn, wk_ref[...]) + jnp.tile(kr, (1, 4))
        k_ref[...] = (zk * cos4 + _rot_mla(zk) * sin4).astype(BF16)
        v_ref[...] = _dot(ckvn, wv_ref[...]).astype(BF16)

    full = lambda a: pl.BlockSpec(a.shape, lambda i: (0, 0))
    rowb = lambda w: pl.BlockSpec((tr, w), lambda i: (i, 0))
    gq2, gkv2 = g_q.reshape(1, Q_RANK), g_kv.reshape(1, KV_RANK)
    return pl.pallas_call(
        body, name="mla_prep", grid=(s // tr,),
        in_specs=[pl.BlockSpec((tr, 256), lambda i: (i, OFF_CQ // 2)), pl.BlockSpec((tr, LANES), lambda i: (i, OFF_CKV)),
                  pl.BlockSpec((tr, LANES), lambda i: (i, OFF_MISC)), rowb(LANES), rowb(LANES),
                  full(gq2), full(gkv2), full(wq), full(wk), full(wv)],
        out_specs=[rowb(512), rowb(512), rowb(512), rowb(256), rowb(128)],
        out_shape=[jax.ShapeDtypeStruct((s, 512), BF16), jax.ShapeDtypeStruct((s, 512), BF16), jax.ShapeDtypeStruct((s, 512), BF16),
                   jax.ShapeDtypeStruct((s, 256), BF16), jax.ShapeDtypeStruct((s, 128), BF16)],
        compiler_params=_cparams(("parallel",)),
    )(proj, proj, proj, cos_m, sin_m, gq2, gkv2, wq, wk, wv)


def _mla_prep_bwd(dq, dk, dv, proj, cqn, ckvn, cos_m, sin_m, g_q, g_kv, wq, wk, wv):
    s = proj.shape[0]
    tr = _tile(s, 256)

    def body(dq_ref, dk_ref, dv_ref, cq_ref, ckv_ref, cqn_ref, ckvn_ref, cos_ref, sin_ref, gq_ref, gkv_ref,
             wq_ref, wk_ref, wv_ref, dcq_ref, dckv_ref, dkr_ref, dwq_ref, dwk_ref, dwv_ref, dgq_ref, dgkv_ref):
        @pl.when(pl.program_id(0) == 0)
        def _():
            for r in (dwq_ref, dwk_ref, dwv_ref, dgq_ref, dgkv_ref):
                r[...] = jnp.zeros_like(r)

        cos4 = jnp.tile(cos_ref[...], (1, 4))
        sin4 = jnp.tile(sin_ref[...], (1, 4))
        dqv = dq_ref[...]
        dzq = dqv * cos4 + _rot_mla_t(dqv * sin4)
        dkv_ = dk_ref[...]
        dzk = dkv_ * cos4 + _rot_mla_t(dkv_ * sin4)
        l = _lane((tr, LANES))
        in_rope = (l >= KR_LANE) & (l < KR_LANE + ROPE_DIM)
        dkr = dzk[:, 0:128] + dzk[:, 128:256] + dzk[:, 256:384] + dzk[:, 384:512]
        dkr_ref[...] = jnp.where(in_rope, dkr, 0.0)
        dzq_b = dzq.astype(BF16)
        dzk_b = dzk.astype(BF16)
        dv_b = dv_ref[...].astype(BF16)
        dcqn = _dot_nt(dzq_b, wq_ref[...])
        dckvn = _dot_nt(dzk_b, wk_ref[...]) + _dot_nt(dv_b, wv_ref[...])
        dwq_ref[...] += _dot_tn(cqn_ref[...], dzq_b)
        dwk_ref[...] += _dot_tn(ckvn_ref[...], dzk_b)
        dwv_ref[...] += _dot_tn(ckvn_ref[...], dv_b)
        dcq, gq_term = _rms_bwd(cq_ref[...], gq_ref[...], dcqn)
        dckv, gkv_term = _rms_bwd(ckv_ref[...], gkv_ref[...], dckvn)
        dcq_ref[...] = dcq.astype(BF16)
        dckv_ref[...] = dckv.astype(BF16)
        dgq_ref[...] += jnp.sum(gq_term, axis=0, keepdims=True)
        dgkv_ref[...] += jnp.sum(gkv_term, axis=0, keepdims=True)

    full = lambda shp: pl.BlockSpec(shp, lambda i: (0, 0))
    rowb = lambda w: pl.BlockSpec((tr, w), lambda i: (i, 0))
    gq2, gkv2 = g_q.reshape(1, Q_RANK), g_kv.reshape(1, KV_RANK)
    return pl.pallas_call(
        body, name="mla_prep_bwd", grid=(s // tr,),
        in_specs=[rowb(512), rowb(512), rowb(512),
                  pl.BlockSpec((tr, 256), lambda i: (i, OFF_CQ // 2)), pl.BlockSpec((tr, LANES), lambda i: (i, OFF_CKV)),
                  rowb(256), rowb(128), rowb(LANES), rowb(LANES), full((1, Q_RANK)), full((1, KV_RANK)),
                  full(wq.shape), full(wk.shape), full(wv.shape)],
        out_specs=[rowb(256), rowb(128), rowb(128), full(wq.shape), full(wk.shape), full(wv.shape),
                   full((1, Q_RANK)), full((1, KV_RANK))],
        out_shape=[jax.ShapeDtypeStruct((s, 256), BF16), jax.ShapeDtypeStruct((s, 128), BF16), jax.ShapeDtypeStruct((s, 128), F32),
                   jax.ShapeDtypeStruct(wq.shape, F32), jax.ShapeDtypeStruct(wk.shape, F32), jax.ShapeDtypeStruct(wv.shape, F32),
                   jax.ShapeDtypeStruct((1, Q_RANK), F32), jax.ShapeDtypeStruct((1, KV_RANK), F32)],
        compiler_params=_cparams(("arbitrary",)),
    )(dq, dk, dv, proj, proj, cqn, ckvn, cos_m, sin_m, gq2, gkv2, wq, wk, wv)


def _ret_prep(proj, cos_r, sin_r):
    s = proj.shape[0]
    tr = _tile(s, 256)

    def body(q_ref, k_ref, cos_ref, sin_ref, qo_ref, ko_ref):
        cos, sin = cos_ref[...], sin_ref[...]
        q, k = q_ref[...], k_ref[...]
        qo_ref[...] = (q * cos + _rot_ret(q) * sin).astype(BF16)
        ko_ref[...] = ((k * cos + _rot_ret(k) * sin) * (HEAD ** -0.5)).astype(BF16)

    rowb = pl.BlockSpec((tr, 256), lambda i: (i, 0))
    return pl.pallas_call(
        body, name="ret_prep", grid=(s // tr,),
        in_specs=[pl.BlockSpec((tr, 256), lambda i: (i, OFF_RQ // 2)), pl.BlockSpec((tr, 256), lambda i: (i, OFF_RK // 2)), rowb, rowb],
        out_specs=[rowb, rowb], out_shape=[jax.ShapeDtypeStruct((s, 256), BF16)] * 2,
        compiler_params=_cparams(("parallel",)),
    )(proj, proj, cos_r, sin_r)


def _ret_prep_bwd(dq, dk, cos_r, sin_r):
    s = dq.shape[0]
    tr = _tile(s, 256)

    def body(dq_ref, dk_ref, cos_ref, sin_ref, qo_ref, ko_ref):
        cos, sin = cos_ref[...], sin_ref[...]
        q, k = dq_ref[...], dk_ref[...] * (HEAD ** -0.5)
        qo_ref[...] = (q * cos + _rot_ret_t(q * sin)).astype(BF16)
        ko_ref[...] = (k * cos + _rot_ret_t(k * sin)).astype(BF16)

    rowb = pl.BlockSpec((tr, 256), lambda i: (i, 0))
    return pl.pallas_call(
        body, name="ret_prep_bwd", grid=(s // tr,), in_specs=[rowb] * 4, out_specs=[rowb, rowb],
        out_shape=[jax.ShapeDtypeStruct((s, 256), BF16)] * 2, compiler_params=_cparams(("parallel",)),
    )(dq, dk, cos_r, sin_r)


_LOG_GAMMA = [float(np.log1p(-np.float32(2.0) ** np.float32(-5.0 - h))) for h in range(N_HEADS)]
_MLA_SCALE = float((HEAD + ROPE_DIM) ** -0.5)
_QK_SCALE = float(HEAD ** -0.5)
KEY_BLOCKS = 4


def _split2(x):
    h = x.astype(BF16)
    return h, (x - h.astype(F32)).astype(BF16)


def _dot2(x, u):
    h, lo = _split2(x)
    return _dot(h, u) + _dot(lo, u)


def _head_pick(block, head, axis):
    idx = lax.broadcasted_iota(jnp.int32, block.shape, axis)
    return jnp.sum(jnp.where(idx == head, block, 0.0), axis=axis, keepdims=True)


def _log_gamma_of(head):
    lg = jnp.float32(_LOG_GAMMA[3])
    for h in (2, 1, 0):
        lg = jnp.where(head == h, jnp.float32(_LOG_GAMMA[h]), lg)
    return lg


def _mixer_specs(mode, s, q_off, k_off, v_off):
    nhb = 2
    bw = 2 * LANES if mode == "mla" else LANES
    nsub = KEY_BLOCKS if (s // TQ) % KEY_BLOCKS == 0 else 1
    q_spec = pl.BlockSpec((TQ, bw), lambda p, i: (i, q_off + p))
    k_spec = pl.BlockSpec((s, bw), lambda p, i: (0, k_off + p))
    v_spec = pl.BlockSpec((s, bw), lambda p, i: (0, v_off + p))
    return nhb, N_HEADS // nhb, nsub, q_spec, k_spec, v_spec


def _mixer_geometry(mode, i, nsub):
    w = TQ * nsub
    row = lax.broadcasted_iota(jnp.int32, (TQ, w), 0)
    col = lax.broadcasted_iota(jnp.int32, (TQ, w), 1)
    nfull = i // nsub
    dist = col - row
    if mode in ("fox", "sb"):
        rel = dist
    else:
        rel = col - (row | (CHUNK - 1))

    def visible(c):
        off = c * w - i * TQ
        return (rel + off) < 0 if mode == "sb" else (rel + off) <= 0

    return nfull, dist, visible


def _mixer_fwd(mode, qa, q_off, ka, k_off, va, v_off, *, cum_col=None, cum_row=None):
    s = qa.shape[0]
    nq = s // TQ
    nhb, nblk, nsub, q_spec, k_spec, v_spec = _mixer_specs(mode, s, q_off, k_off, v_off)
    w = TQ * nsub
    softmax = mode in ("fox", "mla")
    has_stat = mode != "ret"

    def body(*refs):
        refs = list(refs)
        q_ref, k_ref, v_ref = refs[:3]
        refs = refs[3:]
        if mode == "fox":
            cc_ref, cr_ref = refs[:2]
            refs = refs[2:]
        o_ref = refs[0]
        st_ref = refs[1] if has_stat else None
        p = pl.program_id(0)
        i = pl.program_id(1)
        nfull, dist, visible = _mixer_geometry(mode, i, nsub)
        lane = _lane((1, LANES))
        heads = [nhb * p + hh for hh in range(nhb)]
        wide = mode == "mla"
        q_scale = _QK_SCALE if mode in ("fox", "sb") else 1.0
        cols = [slice(hh * LANES, (hh + 1) * LANES) if wide else slice(None) for hh in range(nhb)]
        if wide:
            qs = [q_ref[:, cols[hh]] for hh in range(nhb)]
        else:
            qf = q_ref[...].astype(F32) * q_scale
            qs = [jnp.where((lane // HEAD) == hh, qf, 0.0).astype(BF16) for hh in range(nhb)]
        if mode == "fox":
            cqs = [_head_pick(cc_ref[...], h, 1) for h in heads]
        if mode == "sb":
            r1 = lax.broadcasted_iota(jnp.int32, (TQ, TQ), 0)
            c1 = lax.broadcasted_iota(jnp.int32, (TQ, TQ), 1)
            u_after = (r1 > c1).astype(BF16)

        def chunk(c):
            return pl.ds(pl.multiple_of(c * w, w), w)

        def scores(c):
            js = chunk(c)
            return tuple(_dot_nt(qs[hh], k_ref[js, cols[hh]]) for hh in range(nhb))

        def head_step(hh, c, js, sc, vj, carry, last):
            if softmax:
                m, l, acc = carry
                if mode == "fox":
                    ck = _head_pick(cr_ref[:, js], heads[hh], 0)
                    sc = sc + (cqs[hh] - ck)
                else:
                    sc = sc * _MLA_SCALE
                if last:
                    sc = jnp.where(visible(c), sc, NEG)
                m_new = jnp.maximum(m, jnp.max(sc, axis=-1, keepdims=True))
                alpha = jnp.exp(m - m_new)
                pr = jnp.exp(sc - m_new)
                l = alpha * l + jnp.sum(pr, axis=-1, keepdims=True)
                acc = alpha * acc + _dot(pr.astype(BF16), vj)
                return m_new, l, acc
            run, acc = carry
            z = sc
            log_beta = jnp.minimum(z, 0.0) - jnp.log1p(jnp.exp(-jnp.abs(z)))
            log_stay = log_beta - z
            if last:
                vis = visible(c)
                log_stay = jnp.where(vis, log_stay, 0.0)
            parts = [None] * nsub
            for b in reversed(range(nsub)):
                ls_b = log_stay[:, b * TQ:(b + 1) * TQ]
                parts[b] = _dot2(ls_b, u_after) + run
                run = run + jnp.sum(ls_b, axis=-1, keepdims=True)
            later = parts[0] if nsub == 1 else jnp.concatenate(parts, axis=1)
            wgt = jnp.exp(log_beta + later)
            if last:
                wgt = jnp.where(vis, wgt, 0.0)
            return run, acc + _dot(wgt.astype(BF16), vj)

        def step(c, c_next, state, last):
            scs, carries = state
            nxt = scores(c_next) if c_next is not None else None
            js = chunk(c)
            return nxt, tuple(head_step(hh, c, js, scs[hh], v_ref[js, cols[hh]], carries[hh], last) for hh in range(nhb))

        zero_acc = jnp.zeros((TQ, LANES), F32)
        zero1 = jnp.zeros((TQ, 1), F32)
        if softmax:
            init = tuple((jnp.full((TQ, 1), NEG, F32), zero1, zero_acc) for _ in range(nhb))
        else:
            init = tuple((zero1, zero_acc) for _ in range(nhb))
        if mode == "sb":
            state = step(nfull, jnp.maximum(nfull - 1, 0), (scores(nfull), init), True)
            _, carries = lax.fori_loop(0, nfull, lambda t, st: step(nfull - 1 - t, jnp.maximum(nfull - 2 - t, 0), st, False), state)
        else:
            state = lax.fori_loop(0, nfull, lambda c, st: step(c, c + 1, st, False), (scores(0), init))
            _, carries = step(nfull, None, state, True)
        if softmax:
            outs = [acc / l for (m, l, acc) in carries]
            stats = [m + jnp.log(l) for (m, l, acc) in carries]
        else:
            outs, stats = [acc for (run, acc) in carries], [run for (run, acc) in carries]
        hm0 = (lane // HEAD) == 0
        pick = lambda a: jnp.where(hm0, a[0], a[1])
        if wide:
            for hh in range(nhb):
                o_ref[:, cols[hh]] = outs[hh]
        else:
            o_ref[...] = pick(outs)
        if has_stat:
            st_ref[0] = pick(stats)

    in_specs = [q_spec, k_spec, v_spec]
    args = [qa, ka, va]
    if mode == "fox":
        in_specs += [pl.BlockSpec((TQ, LANES), lambda p, i: (i, 0)), pl.BlockSpec((8, s), lambda p, i: (0, 0))]
        args += [cum_col, cum_row]
    bw = 2 * LANES if mode == "mla" else LANES
    out_specs = [pl.BlockSpec((TQ, bw), lambda p, i: (i, p))]
    out_shape = [jax.ShapeDtypeStruct((s, nblk * bw), F32)]
    if has_stat:
        out_specs.append(pl.BlockSpec((1, TQ, LANES), lambda p, i: (p, i, 0)))
        out_shape.append(jax.ShapeDtypeStruct((nblk, s, LANES), F32))
    res = pl.pallas_call(
        body, name=mode + "_fwd", grid=(nblk, nq), in_specs=in_specs, out_specs=out_specs, out_shape=out_shape,
        compiler_params=_cparams(("parallel", "parallel")),
    )(*args)
    return res if has_stat else (res[0], None)


def _mixer_bwd(mode, qa, q_off, ka, k_off, va, v_off, o, do, *, stat=None, cum_col=None, cum_row=None):
    s = qa.shape[0]
    nq = s // TQ
    nhb, nblk, nsub, q_spec, k_spec, v_spec = _mixer_specs(mode, s, q_off, k_off, v_off)
    w = TQ * nsub
    softmax = mode in ("fox", "mla")
    has_stat = mode != "ret"

    def body(*refs):
        refs = list(refs)
        q_ref, k_ref, v_ref, o_ref, do_ref = refs[:5]
        refs = refs[5:]
        if has_stat:
            st_ref = refs[0]
            refs = refs[1:]
        if mode == "fox":
            cc_ref, cr_ref = refs[:2]
            refs = refs[2:]
        dq_ref, dk_ref, dv_ref = refs[:3]
        dck_ref, drs_ref = refs[3:5] if mode == "fox" else (None, None)
        p = pl.program_id(0)
        i = pl.program_id(1)

        @pl.when(i == 0)
        def _():
            dk_ref[...] = jnp.zeros_like(dk_ref)
            dv_ref[...] = jnp.zeros_like(dv_ref)
            if mode == "fox":
                dck_ref[...] = jnp.zeros_like(dck_ref)

        nfull, dist, visible = _mixer_geometry(mode, i, nsub)
        lane = _lane((1, LANES))
        heads = [nhb * p + hh for hh in range(nhb)]
        dov = do_ref[...]
        wide = mode == "mla"
        q_scale = _QK_SCALE if mode in ("fox", "sb") else 1.0
        cols = [slice(hh * LANES, (hh + 1) * LANES) if wide else slice(None) for hh in range(nhb)]
        if wide:
            prod = dov * o_ref[...]
            qs = [q_ref[:, cols[hh]] for hh in range(nhb)]
            dos = [dov[:, cols[hh]].astype(BF16) for hh in range(nhb)]
            deltas = [jnp.sum(prod[:, cols[hh]], axis=-1, keepdims=True) for hh in range(nhb)]
        else:
            qf = q_ref[...].astype(F32) * q_scale
            prod = dov * o_ref[...]
            hms = [(lane // HEAD) == hh for hh in range(nhb)]
            qs = [jnp.where(hm, qf, 0.0).astype(BF16) for hm in hms]
            dos = [jnp.where(hm, dov, 0.0).astype(BF16) for hm in hms]
            deltas = [jnp.sum(jnp.where(hm, prod, 0.0), axis=-1, keepdims=True) for hm in hms]
        if has_stat:
            st = st_ref[0]
            stats = [st[:, hh * HEAD:hh * HEAD + 1] for hh in range(nhb)]
        if mode == "fox":
            cqs = [_head_pick(cc_ref[...], h, 1) for h in heads]
        if mode == "sb":
            r1 = lax.broadcasted_iota(jnp.int32, (TQ, TQ), 0)
            c1 = lax.broadcasted_iota(jnp.int32, (TQ, TQ), 1)
            u_upto = (r1 <= c1).astype(BF16)
            u_before = (r1 < c1).astype(BF16)

        def chunk(c):
            return pl.ds(pl.multiple_of(c * w, w), w)

        def scores(c):
            js = chunk(c)
            return tuple((_dot_nt(qs[hh], k_ref[js, cols[hh]]), _dot_nt(dos[hh], v_ref[js, cols[hh]])) for hh in range(nhb))

        def emit(hh, js, ds_b, pr_b, dq):
            dk_ref[js, cols[hh]] += _dot_tn(ds_b, qs[hh])
            dv_ref[js, cols[hh]] += _dot_tn(pr_b, dos[hh])
            return dq + _dot(ds_b, k_ref[js, cols[hh]])

        def head_step(hh, c, js, sc_dp, carry, last):
            sc, dp = sc_dp
            if softmax:
                dq, rsum = carry
                if mode == "fox":
                    ck = _head_pick(cr_ref[:, js], heads[hh], 0)
                    sc = sc + (cqs[hh] - ck)
                else:
                    sc = sc * _MLA_SCALE
                if last:
                    sc = jnp.where(visible(c), sc, NEG)
                pr = jnp.exp(sc - stats[hh])
                ds = pr * (dp - deltas[hh])
                if mode == "fox":
                    dck_ref[0, hh:hh + 1, js] += jnp.sum(ds, axis=0, keepdims=True)
                    rsum = rsum + jnp.sum(ds, axis=-1, keepdims=True)
                if mode == "mla":
                    ds = ds * _MLA_SCALE
                return emit(hh, js, ds.astype(BF16), pr.astype(BF16), dq), rsum
            seen, gsum, dq = carry
            z = sc
            log_beta = jnp.minimum(z, 0.0) - jnp.log1p(jnp.exp(-jnp.abs(z)))
            log_stay = log_beta - z
            if last:
                vis = visible(c)
                log_stay = jnp.where(vis, log_stay, 0.0)
            parts = []
            for b in range(nsub):
                ls_b = log_stay[:, b * TQ:(b + 1) * TQ]
                parts.append((stats[hh] - seen) - _dot2(ls_b, u_upto))
                seen = seen + jnp.sum(ls_b, axis=-1, keepdims=True)
            later = parts[0] if nsub == 1 else jnp.concatenate(parts, axis=1)
            wgt = jnp.exp(log_beta + later)
            if last:
                wgt = jnp.where(vis, wgt, 0.0)
            g = dp * wgt
            parts = []
            for b in range(nsub):
                g_b = g[:, b * TQ:(b + 1) * TQ]
                parts.append(gsum + _dot2(g_b, u_before))
                gsum = gsum + jnp.sum(g_b, axis=-1, keepdims=True)
            before = parts[0] if nsub == 1 else jnp.concatenate(parts, axis=1)
            beta = jnp.exp(log_beta)
            dz = g * (1.0 - beta) - beta * before
            if last:
                dz = jnp.where(vis, dz, 0.0)
            return seen, gsum, emit(hh, js, dz.astype(BF16), wgt.astype(BF16), dq)

        def step(c, c_next, state, last):
            scs, carries = state
            nxt = scores(c_next) if c_next is not None else None
            js = chunk(c)
            return nxt, tuple(head_step(hh, c, js, scs[hh], carries[hh], last) for hh in range(nhb))

        zero_acc = jnp.zeros((TQ, LANES), F32)
        zero1 = jnp.zeros((TQ, 1), F32)
        if softmax:
            init = tuple((zero_acc, zero1) for _ in range(nhb))
        else:
            init = tuple((zero1, zero1, zero_acc) for _ in range(nhb))
        state = lax.fori_loop(0, nfull, lambda c, st: step(c, c + 1, st, False), (scores(0), init))
        _, carries = step(nfull, None, state, True)
        if softmax:
            dqs = [dq for (dq, rsum) in carries]
        else:
            dqs = [dq for (seen, gsum, dq) in carries]
        hm0 = (lane // HEAD) == 0
        if wide:
            for hh in range(nhb):
                dq_ref[:, cols[hh]] = dqs[hh]
        else:
            dq_ref[...] = jnp.where(hm0, dqs[0], dqs[1]) * q_scale
        if mode == "fox":
            drs_ref[0] = jnp.where(hm0, carries[0][1], carries[1][1])

    bw = 2 * LANES if mode == "mla" else LANES
    pair_blk = pl.BlockSpec((TQ, bw), lambda p, i: (i, p))
    full_blk = pl.BlockSpec((s, bw), lambda p, i: (0, p))
    stat_blk = pl.BlockSpec((1, TQ, LANES), lambda p, i: (p, i, 0))
    in_specs = [q_spec, k_spec, v_spec, pair_blk, pair_blk]
    args = [qa, ka, va, o, do]
    if has_stat:
        in_specs.append(stat_blk)
        args.append(stat)
    if mode == "fox":
        in_specs += [pl.BlockSpec((TQ, LANES), lambda p, i: (i, 0)), pl.BlockSpec((8, s), lambda p, i: (0, 0))]
        args += [cum_col, cum_row]
    out_specs = [pair_blk, full_blk, full_blk]
    out_shape = [jax.ShapeDtypeStruct((s, nblk * bw), F32)] * 3
    if mode == "fox":
        out_specs += [pl.BlockSpec((1, 8, s), lambda p, i: (p, 0, 0)), stat_blk]
        out_shape += [jax.ShapeDtypeStruct((2, 8, s), F32), jax.ShapeDtypeStruct((2, s, LANES), F32)]
    return pl.pallas_call(
        body, name=mode + "_bwd", grid=(nblk, nq), in_specs=in_specs, out_specs=out_specs, out_shape=out_shape,
        compiler_params=_cparams(("parallel", "arbitrary")),
    )(*args)


def _ret_geometry(p):
    lane = _lane((1, LANES))
    lg_lane = jnp.where(lane < HEAD, _log_gamma_of(2 * p), _log_gamma_of(2 * p + 1))
    a = lax.broadcasted_iota(jnp.int32, (TQ, 1), 0).astype(F32)
    row = lax.broadcasted_iota(jnp.int32, (TQ, TQ), 0)
    col = lax.broadcasted_iota(jnp.int32, (TQ, TQ), 1)
    same_chunk_or_earlier = (col // CHUNK) <= (row // CHUNK)
    gap = jnp.abs(row - col).astype(F32)
    decays = [jnp.where(same_chunk_or_earlier, jnp.exp(_log_gamma_of(2 * p + hh) * gap), 0.0) for hh in range(2)]
    r = lax.broadcasted_iota(jnp.int32, (LANES, LANES), 0)
    c = lax.broadcasted_iota(jnp.int32, (LANES, LANES), 1)
    own_head = (r // HEAD) == (c // HEAD)
    return lane, lg_lane, a, decays, own_head


def _ret_fwd(qa, ka, va, v_off):
    s = qa.shape[0]
    nq = s // TQ

    def body(q_ref, k_ref, v_ref, o_ref, st_ref, state):
        p = pl.program_id(0)

        @pl.when(pl.program_id(1) == 0)
        def _():
            state[...] = jnp.zeros_like(state)

        lane, lg_lane, a, decays, own_head = _ret_geometry(p)
        q = q_ref[...].astype(F32)
        k = k_ref[...]
        v = v_ref[...]
        s_in = state[...]
        st_ref[0, 0] = s_in
        out = _dot((q * jnp.exp(lg_lane * (a + 1.0))).astype(BF16), s_in.astype(BF16))
        for hh in range(2):
            hm = (lane // HEAD) == hh
            qh = jnp.where(hm, q, 0.0).astype(BF16)
            inner = _dot((_dot_nt(qh, k) * decays[hh]).astype(BF16), v)
            out = out + jnp.where(hm, inner, 0.0)
        o_ref[...] = out
        k_tail = (k.astype(F32) * jnp.exp(lg_lane * (TQ - 1.0 - a))).astype(BF16)
        state[...] = jnp.exp(lg_lane * float(TQ)) * s_in + jnp.where(own_head, _dot_tn(k_tail, v), 0.0)

    blk = lambda off: pl.BlockSpec((TQ, LANES), lambda p, i: (i, off + p))
    return pl.pallas_call(
        body, name="ret_fwd", grid=(2, nq), in_specs=[blk(0), blk(0), blk(v_off)],
        out_specs=[blk(0), pl.BlockSpec((1, 1, LANES, LANES), lambda p, i: (p, i, 0, 0))],
        out_shape=[jax.ShapeDtypeStruct((s, 2 * LANES), F32), jax.ShapeDtypeStruct((2, nq, LANES, LANES), F32)],
        scratch_shapes=[pltpu.VMEM((LANES, LANES), F32)],
        compiler_params=_cparams(("parallel", "arbitrary")),
    )(qa, ka, va)


def _ret_bwd(qa, ka, va, v_off, states, do):
    s = qa.shape[0]
    nq = s // TQ

    def body(q_ref, k_ref, v_ref, st_ref, do_ref, dq_ref, dk_ref, dv_ref, dstate):
        p = pl.program_id(0)

        @pl.when(pl.program_id(1) == 0)
        def _():
            dstate[...] = jnp.zeros_like(dstate)

        lane, lg_lane, a, decays, own_head = _ret_geometry(p)
        q = q_ref[...].astype(F32)
        k = k_ref[...]
        kf = k.astype(F32)
        v = v_ref[...]
        dov = do_ref[...]
        s_in = st_ref[0, 0].astype(BF16)
        ds_next = dstate[...]
        ds_b = ds_next.astype(BF16)
        head_decay = jnp.exp(lg_lane * (a + 1.0))
        tail_decay = jnp.exp(lg_lane * (TQ - 1.0 - a))
        k_tail = (kf * tail_decay).astype(BF16)
        dq = _dot_nt(dov.astype(BF16), s_in) * head_decay
        dk = _dot_nt(v, ds_b) * tail_decay
        dv = _dot(k_tail, ds_b)
        for hh in range(2):
            hm = (lane // HEAD) == hh
            qh = jnp.where(hm, q, 0.0).astype(BF16)
            doh = jnp.where(hm, dov, 0.0).astype(BF16)
            att = (_dot_nt(qh, k) * decays[hh]).astype(BF16)
            datt = (_dot_nt(doh, v) * decays[hh]).astype(BF16)
            dv = dv + _dot_tn(att, doh)
            dk = dk + _dot_tn(datt, qh)
            dq = dq + jnp.where(hm, _dot(datt, k), 0.0)
        dq_ref[...] = dq
        dk_ref[...] = dk
        dv_ref[...] = dv
        q_head = (q * head_decay).astype(BF16)
        dstate[...] = jnp.exp(lg_lane * float(TQ)) * ds_next + jnp.where(own_head, _dot_tn(q_head, dov.astype(BF16)), 0.0)

    blk = lambda off: pl.BlockSpec((TQ, LANES), lambda p, i: (nq - 1 - i, off + p))
    return pl.pallas_call(
        body, name="ret_bwd", grid=(2, nq),
        in_specs=[blk(0), blk(0), blk(v_off), pl.BlockSpec((1, 1, LANES, LANES), lambda p, i: (p, nq - 1 - i, 0, 0)), blk(0)],
        out_specs=[blk(0)] * 3, out_shape=[jax.ShapeDtypeStruct((s, 2 * LANES), F32)] * 3,
        scratch_shapes=[pltpu.VMEM((LANES, LANES), F32)],
        compiler_params=_cparams(("parallel", "arbitrary")),
    )(qa, ka, va, states, do)


def _seg_mean_matrix():
    r = lax.broadcasted_iota(jnp.int32, (GROUP, GROUP), 0)
    c = lax.broadcasted_iota(jnp.int32, (GROUP, GROUP), 1)
    return jnp.where((r // HEAD) == (c // HEAD), 1.0 / HEAD, 0.0).astype(F32)


def _sigmoid(x):
    return 1.0 / (1.0 + jnp.exp(-x))


def _mix_post(oa, ob, oc, od, proj, g):
    s = oa.shape[0]
    tr = _tile(s, 256)

    def body(a_ref, b_ref, c_ref, d_ref, rg_ref, g_ref, o_ref):
        gv = g_ref[...]
        o_ref[:, 0:GROUP] = _rms(a_ref[...], gv[:, 0:GROUP]).astype(BF16)
        o_ref[:, GROUP:2 * GROUP] = _rms(b_ref[...], gv[:, GROUP:2 * GROUP]).astype(BF16)
        seg = _seg_mean_matrix()
        c = c_ref[...]
        cen = c - _dot_exact(c, seg)
        n = cen * lax.rsqrt(_dot_exact(cen * cen, seg) + EPS)
        rg = rg_ref[...]
        o_ref[:, 2 * GROUP:3 * GROUP] = (n * gv[:, 2 * GROUP:3 * GROUP] * (rg * _sigmoid(rg))).astype(BF16)
        o_ref[:, 3 * GROUP:] = _rms(d_ref[...], gv[:, 3 * GROUP:]).astype(BF16)

    blk = pl.BlockSpec((tr, GROUP), lambda i: (i, 0))
    return pl.pallas_call(
        body, name="mix_post", grid=(s // tr,),
        in_specs=[blk] * 4 + [pl.BlockSpec((tr, GROUP), lambda i: (i, OFF_RG // 2)), pl.BlockSpec((1, D_MODEL), lambda i: (0, 0))],
        out_specs=pl.BlockSpec((tr, D_MODEL), lambda i: (i, 0)), out_shape=jax.ShapeDtypeStruct((s, D_MODEL), BF16),
        compiler_params=_cparams(("parallel",)),
    )(oa, ob, oc, od, proj, g.reshape(1, D_MODEL))


def _mix_post_bwd(dmixed, oa, ob, oc, od, proj, g):
    s = oa.shape[0]
    tr = _tile(s, 256)

    def body(dm_ref, a_ref, b_ref, c_ref, d_ref, rg_ref, g_ref, da_ref, db_ref, dc_ref, dd_ref, drg_ref, dg_ref):
        @pl.when(pl.program_id(0) == 0)
        def _():
            dg_ref[...] = jnp.zeros_like(dg_ref)

        gv = g_ref[...]
        dm = dm_ref[...]
        for k, (x_ref, dx_ref) in enumerate(((a_ref, da_ref), (b_ref, db_ref), (None, None), (d_ref, dd_ref))):
            if x_ref is None:
                continue
            cols = slice(k * GROUP, (k + 1) * GROUP)
            dx, gterm = _rms_bwd(x_ref[...], gv[:, cols], dm[:, cols])
            dx_ref[...] = dx
            dg_ref[:, cols] += jnp.sum(gterm, axis=0, keepdims=True)
        cols = slice(2 * GROUP, 3 * GROUP)
        seg = _seg_mean_matrix()
        c = c_ref[...]
        cen = c - _dot_exact(c, seg)
        rstd = lax.rsqrt(_dot_exact(cen * cen, seg) + EPS)
        n = cen * rstd
        rg = rg_ref[...]
        sg = _sigmoid(rg)
        gate = rg * sg
        dy = dm[:, cols]
        gc = gv[:, cols]
        dn = dy * gc * gate
        dg_ref[:, cols] += jnp.sum(dy * n * gate, axis=0, keepdims=True)
        drg_ref[...] = (dy * n * gc * (sg * (1.0 + rg * (1.0 - sg)))).astype(BF16)
        dc_ref[...] = rstd * (dn - _dot_exact(dn, seg) - n * _dot_exact(dn * n, seg))

    blk = pl.BlockSpec((tr, GROUP), lambda i: (i, 0))
    gsp = pl.BlockSpec((1, D_MODEL), lambda i: (0, 0))
    return pl.pallas_call(
        body, name="mix_post_bwd", grid=(s // tr,),
        in_specs=[pl.BlockSpec((tr, D_MODEL), lambda i: (i, 0))] + [blk] * 4 + [pl.BlockSpec((tr, GROUP), lambda i: (i, OFF_RG // 2)), gsp],
        out_specs=[blk] * 5 + [gsp],
        out_shape=[jax.ShapeDtypeStruct((s, GROUP), F32)] * 4 + [jax.ShapeDtypeStruct((s, GROUP), BF16), jax.ShapeDtypeStruct((1, D_MODEL), F32)],
        compiler_params=_cparams(("arbitrary",)),
    )(dmixed, oa, ob, oc, od, proj, g.reshape(1, D_MODEL))


def _pack_w_in(w):
    z = lambda n: jnp.zeros((w.shape[0], n), w.dtype)
    misc = jnp.concatenate([w[:, 768:772], z(KR_LANE - N_HEADS), w[:, 1156:1188], z(LANES - KR_LANE - ROPE_DIM)], axis=1)
    return jnp.concatenate([w[:, 0:768], w[:, 772:1028], w[:, 1188:2980], w[:, 1028:1156], misc], axis=1)


def _unpack_dw_in(d):
    m = OFF_MISC * LANES
    return jnp.concatenate([d[:, 0:768], d[:, m:m + N_HEADS], d[:, 768:1024], d[:, OFF_CKV * LANES:m],
                            d[:, m + KR_LANE:m + KR_LANE + ROPE_DIM], d[:, 1024:OFF_CKV * LANES]], axis=1)


def _pack_w_q(w):
    return jnp.pad(w.reshape(Q_RANK, N_HEADS, HEAD + ROPE_DIM), ((0, 0), (0, 0), (0, LANES - HEAD - ROPE_DIM))).reshape(Q_RANK, 4 * LANES)


def _unpack_dw_q(d):
    return d.reshape(Q_RANK, N_HEADS, LANES)[:, :, :HEAD + ROPE_DIM].reshape(Q_RANK, N_HEADS * (HEAD + ROPE_DIM))


def _pack_w_kv(w):
    w4 = w.reshape(KV_RANK, N_HEADS, 2 * HEAD)
    widen = lambda a: jnp.pad(a, ((0, 0), (0, 0), (0, LANES - HEAD))).reshape(KV_RANK, N_HEADS * LANES)
    return widen(w4[:, :, :HEAD]), widen(w4[:, :, HEAD:])


def _unpack_dw_kv(dk, dv):
    narrow = lambda a: a.reshape(KV_RANK, N_HEADS, LANES)[:, :, :HEAD]
    return jnp.concatenate([narrow(dk), narrow(dv)], axis=2).reshape(KV_RANK, 2 * N_HEADS * HEAD)


def _narrow_heads(a):
    return a.reshape(a.shape[0], N_HEADS, LANES)[:, :, :HEAD].reshape(a.shape[0], N_HEADS * HEAD)


def _widen_heads(a):
    return jnp.pad(a.reshape(a.shape[0], N_HEADS, HEAD), ((0, 0), (0, 0), (0, LANES - HEAD))).reshape(a.shape[0], N_HEADS * LANES)


def _layer_fwd(x, lw, tabs, tag):
    cos_m, sin_m, cos_r, sin_r = tabs
    h1 = _norm_fwd(x, lw["g_mix_pre"], name=tag + "pre_norm")
    proj, projb = _matmul(h1, lw["w_in"], name=tag + "in_proj", also_bf16=True)
    bias_row = jnp.pad(lw["b_forget"], (FF_LANE, LANES - N_HEADS - FF_LANE)).reshape(1, LANES)
    cum_col, cum_row = _fox_cum(proj, bias_row)
    oa, lse_a = _mixer_fwd("fox", projb, OFF_FQ, projb, OFF_FK, projb, OFF_FV, cum_col=cum_col, cum_row=cum_row)
    qm, km, vm, cqn, ckvn = _mla_prep(proj, cos_m, sin_m, lw["g_q_lora"], lw["g_kv_lora"], lw["wq"], lw["wk"], lw["wv"])
    ob_wide, lse_b = _mixer_fwd("mla", qm, 0, km, 0, vm, 0)
    ob = _narrow_heads(ob_wide)
    qr, kr = _ret_prep(proj, cos_r, sin_r)
    oc, ret_states = _ret_fwd(qr, kr, projb, OFF_RV)
    od, tot_d = _mixer_fwd("sb", projb, OFF_SQ, projb, OFF_SK, projb, OFF_SV)
    mixed = _mix_post(oa, ob, oc, od, proj, lw["g_mix_out"])
    mix = _matmul(mixed, lw["w_out"], name=tag + "out_proj")
    x1 = _norm_fwd(mix, lw["g_mix_post"], name=tag + "mix_post_norm", resid=x, out_dtype=F32)
    h2 = _norm_fwd(x1, lw["g_ffn_pre"], name=tag + "ffn_pre_norm")
    u = _matmul(h2, lw["w_ffn_up"], name=tag + "ffn_up", relu2=True, out_dtype=BF16)
    f = _matmul(u, lw["w_ffn_down"], name=tag + "ffn_down")
    x2 = _norm_fwd(f, lw["g_ffn_post"], name=tag + "ffn_post_norm", resid=x1, out_dtype=F32)
    saved = dict(x=x, h1=h1, proj=proj, projb=projb, bias_row=bias_row, cum_col=cum_col, cum_row=cum_row, oa=oa, lse_a=lse_a,
                 qm=qm, km=km, vm=vm, cqn=cqn, ckvn=ckvn, ob=ob, ob_wide=ob_wide, lse_b=lse_b, qr=qr, kr=kr, ret_states=ret_states, oc=oc, od=od, tot_d=tot_d, mixed=mixed,
                 mix=mix, x1=x1, h2=h2, u=u, f=f)
    return x2, saved


def _layer_bwd(dx2, lw, sv, tabs, tag):
    cos_m, sin_m, cos_r, sin_r = tabs
    g = {}
    df, g["g_ffn_post"] = _norm_bwd(sv["f"], lw["g_ffn_post"], dx2, name=tag + "ffn_post_norm_bwd", out_dtype=BF16)
    du_pre = _matmul(df, lw["w_ffn_down"], name=tag + "ffn_down_dx", tb=True, out_dtype=BF16, relu2_of=sv["u"])
    g["w_ffn_down"] = _matmul(sv["u"], df, name=tag + "ffn_down_dw", ta=True)
    dh2 = _matmul(du_pre, lw["w_ffn_up"], name=tag + "ffn_up_dx", tb=True)
    g["w_ffn_up"] = _matmul(sv["h2"], du_pre, name=tag + "ffn_up_dw", ta=True)
    dx1, g["g_ffn_pre"] = _norm_bwd(sv["x1"], lw["g_ffn_pre"], dh2, name=tag + "ffn_pre_norm_bwd", add=dx2)
    dmix, g["g_mix_post"] = _norm_bwd(sv["mix"], lw["g_mix_post"], dx1, name=tag + "mix_post_norm_bwd", out_dtype=BF16)
    dmixed = _matmul(dmix, lw["w_out"], name=tag + "out_proj_dx", tb=True)
    g["w_out"] = _matmul(sv["mixed"], dmix, name=tag + "out_proj_dw", ta=True)
    proj, projb = sv["proj"], sv["projb"]
    doa, dob, doc, dod, drg, g["g_mix_out"] = _mix_post_bwd(dmixed, sv["oa"], sv["ob"], sv["oc"], sv["od"], proj, lw["g_mix_out"])
    dfq, dfk, dfv, dck, drs = _mixer_bwd("fox", projb, OFF_FQ, projb, OFF_FK, projb, OFF_FV, sv["oa"], doa, stat=sv["lse_a"],
                                         cum_col=sv["cum_col"], cum_row=sv["cum_row"])
    dqm, dkm, dvm = _mixer_bwd("mla", sv["qm"], 0, sv["km"], 0, sv["vm"], 0, sv["ob_wide"], _widen_heads(dob), stat=sv["lse_b"])
    dcq, dckv, dkr, dwq, dwk, dwv, g["g_q_lora"], g["g_kv_lora"] = _mla_prep_bwd(
        dqm, dkm, dvm, proj, sv["cqn"], sv["ckvn"], cos_m, sin_m, lw["g_q_lora"], lw["g_kv_lora"], lw["wq"], lw["wk"], lw["wv"])
    dqr, dkr_ret, drv = _ret_bwd(sv["qr"], sv["kr"], projb, OFF_RV, sv["ret_states"], doc)
    drq, drk = _ret_prep_bwd(dqr, dkr_ret, cos_r, sin_r)
    dsq, dsk, dsv = _mixer_bwd("sb", projb, OFF_SQ, projb, OFF_SK, projb, OFF_SV, sv["od"], dod, stat=sv["tot_d"])
    dmisc, db_row = _fox_gate_bwd(dck, drs, proj, sv["bias_row"], dkr)
    b = lambda a: a.astype(BF16)
    dproj = jnp.concatenate([b(dfq), b(dfk), b(dfv), dcq, drq, drk, b(drv), drg, b(dsq), b(dsk), b(dsv), dckv, dmisc], axis=1)
    dh1 = _matmul(dproj, lw["w_in"], name=tag + "in_proj_dx", tb=True)
    g["w_in"] = _matmul(sv["h1"], dproj, name=tag + "in_proj_dw", ta=True)
    dx, g["g_mix_pre"] = _norm_bwd(sv["x"], lw["g_mix_pre"], dh1, name=tag + "pre_norm_bwd", add=dx1)
    g["b_forget"] = db_row[0, FF_LANE:FF_LANE + N_HEADS]
    g["wq"], g["wk"], g["wv"] = dwq, dwk, dwv
    return dx, g


def _local_step(x, positions, layers, target):
    s = x.shape[0]
    tabs = _rope_tables(positions.reshape(s, 1))
    saved = []
    for li, lw in enumerate(layers):
        x, sv = _layer_fwd(x, lw, tabs, "l%d_" % li)
        saved.append(sv)
    loss_row, dx = _loss_head(x, target)
    grads = [None] * len(layers)
    for li in reversed(range(len(layers))):
        dx, grads[li] = _layer_bwd(dx, layers[li], saved[li], tabs, "l%d_" % li)
    return loss_row[0, 0], dx, grads


def _adamw(w, g, m, v, *, name):
    r, c = w.shape
    tr = 256 if r % 256 == 0 else r
    blk = pl.BlockSpec((tr, c), lambda i: (i, 0))
    c1 = 1.0 - ADAM_B1 ** ADAM_STEP
    c2 = 1.0 - ADAM_B2 ** ADAM_STEP

    def body(w_ref, g_ref, m_ref, v_ref, d_ref, mo_ref, vo_ref):
        gv = g_ref[...]
        mn = ADAM_B1 * m_ref[...] + (1.0 - ADAM_B1) * gv
        vn = ADAM_B2 * v_ref[...] + (1.0 - ADAM_B2) * jnp.square(gv)
        mo_ref[...] = mn
        vo_ref[...] = vn
        d_ref[...] = -ADAM_LR * ((mn / c1) / (jnp.sqrt(vn / c2) + ADAM_EPS) + ADAM_WD * w_ref[...])

    return pl.pallas_call(
        body, name=name, grid=(r // tr,), in_specs=[blk] * 4, out_specs=[blk] * 3,
        out_shape=[jax.ShapeDtypeStruct((r, c), F32)] * 3, compiler_params=_cparams(("parallel",)),
    )(w, g, m, v)


BIG = ("w_in", "w_q_up", "w_kv_up", "w_out", "w_ffn_up", "w_ffn_down")
SMALL = ("g_mix_pre", "b_forget", "g_q_lora", "g_kv_lora", "g_mix_out", "g_mix_post", "g_ffn_pre", "g_ffn_post")
N_CHIPS = 4
ANY = pl.BlockSpec(memory_space=pl.ANY)


def _mesh_pos():
    return lax.axis_index("x"), lax.axis_index("y"), lax.axis_index("c")


def _other_chips(x, y):
    return [(1 - x, y), (x, 1 - y), (1 - x, 1 - y)]


def _rows_half(ref, half):
    h = ref.shape[-2] // 2
    return ref.at[(slice(None),) * (len(ref.shape) - 2) + (pl.ds(half * h, h), slice(None))]


def _remote(src, dst, send_sem, recv_sem, device):
    return pltpu.make_async_remote_copy(src_ref=src, dst_ref=dst, send_sem=send_sem, recv_sem=recv_sem, device_id=device,
                                        device_id_type=MESH)


def _comm_call(body, name, args, out_shape, n_sems):
    return pl.pallas_call(
        body, name=name, in_specs=[ANY] * len(args), out_specs=[ANY] * len(out_shape), out_shape=out_shape,
        scratch_shapes=[pltpu.SemaphoreType.DMA((n_sems,)), pltpu.SemaphoreType.DMA((n_sems,))],
        compiler_params=pltpu.CompilerParams(has_side_effects=True),
    )(*args)


def _gather_weights(shards):
    n = len(shards)

    def body(*refs):
        ins, outs = refs[:n], refs[n:2 * n]
        ici_send, ici_recv, d2d_send, d2d_recv = refs[2 * n:]
        x, y, c = _mesh_pos()
        mine = 2 * x + y
        peers = _other_chips(x, y)

        def ici(t, j, block):
            px, py = peers[j]
            return _remote(_rows_half(ins[t], c), _rows_half(outs[t].at[block], c), ici_send.at[3 * t + j], ici_recv.at[3 * t + j],
                           (px, py, c))

        def d2d(t, j, block, half):
            region = _rows_half(outs[t].at[block], half)
            return _remote(region, region, d2d_send.at[3 * t + j], d2d_recv.at[3 * t + j], (x, y, 1 - c))

        sends = [ici(t, j, mine) for t in range(n) for j in range(3)]
        for cp in sends:
            cp.start()
        passed = []
        for t in range(n):
            for j, (px, py) in enumerate(peers):
                ici(t, j, 2 * px + py).wait_recv()
                fwd = d2d(t, j, 2 * px + py, c)
                fwd.start()
                passed.append(fwd)
        for t in range(n):
            for j, (px, py) in enumerate(peers):
                d2d(t, j, 2 * px + py, 1 - c).wait_recv()
        for cp in sends + passed:
            cp.wait_send()

    out_shape = [jax.ShapeDtypeStruct((N_CHIPS,) + a.shape, a.dtype) for a in shards]
    return pl.pallas_call(
        body, name="gather_weights", in_specs=[ANY] * n, out_specs=[ANY] * n, out_shape=out_shape,
        scratch_shapes=[pltpu.SemaphoreType.DMA((3 * n,))] * 4,
        compiler_params=pltpu.CompilerParams(has_side_effects=True),
    )(*shards)


def _exchange_halves(gs):
    n = len(gs)

    def body(*refs):
        ins, outs, send_sems, recv_sems = refs[:n], refs[n:2 * n], refs[2 * n], refs[2 * n + 1]
        x, y, c = _mesh_pos()
        cps = [_remote(_rows_half(ins[t], 1 - c), outs[t], send_sems.at[t], recv_sems.at[t], (x, y, 1 - c)) for t in range(n)]
        for cp in cps:
            cp.start()
        for cp in cps:
            cp.wait_recv()
        for cp in cps:
            cp.wait_send()

    out_shape = [jax.ShapeDtypeStruct(g.shape[:2] + (g.shape[2] // 2, g.shape[3]), g.dtype) for g in gs]
    return _comm_call(body, "grad_pair_exchange", gs, out_shape, n)


def _pair_add(g, r, c_idx, *, name):
    nb, d, rows, cols = g.shape
    h = rows // 2
    tr = min(h, 512)
    nt = h // tr

    def body(c_ref, g_ref, r_ref, p_ref, pb_ref):
        s = g_ref[...] + r_ref[...]
        p_ref[...] = s
        pb_ref[...] = s.astype(BF16)

    blk = pl.BlockSpec((1, 1, tr, cols), lambda k, l, i, c_ref: (k, l, i, 0))
    return pl.pallas_call(
        body, name=name,
        grid_spec=pltpu.PrefetchScalarGridSpec(
            num_scalar_prefetch=1, grid=(nb, d, nt),
            in_specs=[pl.BlockSpec((1, 1, tr, cols), lambda k, l, i, c_ref: (k, l, c_ref[0] * nt + i, 0)), blk],
            out_specs=[blk, blk]),
        out_shape=[jax.ShapeDtypeStruct((nb, d, h, cols), F32), jax.ShapeDtypeStruct((nb, d, h, cols), BF16)],
        compiler_params=_cparams(("parallel", "parallel", "parallel")),
    )(c_idx, g, r)


def _exchange_chips(pbs):
    n = len(pbs)

    def body(*refs):
        ins, outs, send_sems, recv_sems = refs[:n], refs[n:2 * n], refs[2 * n], refs[2 * n + 1]
        x, y, c = _mesh_pos()
        cps = [_remote(ins[t].at[2 * px + py], outs[t].at[j], send_sems.at[3 * t + j], recv_sems.at[3 * t + j], (px, py, c))
               for t in range(n) for j, (px, py) in enumerate(_other_chips(x, y))]
        for cp in cps:
            cp.start()
        for cp in cps:
            cp.wait_recv()
        for cp in cps:
            cp.wait_send()

    out_shape = [jax.ShapeDtypeStruct((3,) + p.shape[1:], p.dtype) for p in pbs]
    return _comm_call(body, "grad_chip_exchange", pbs, out_shape, 3 * n)


def _chip_add(p, r, k_idx, *, name):
    _, d, h, cols = p.shape
    tr = min(h, 512)
    nt = h // tr

    def body(k_ref, p_ref, r_ref, o_ref):
        o_ref[0] = ((p_ref[0, 0] + r_ref[0, 0].astype(F32)) + r_ref[1, 0].astype(F32)) + r_ref[2, 0].astype(F32)

    return pl.pallas_call(
        body, name=name,
        grid_spec=pltpu.PrefetchScalarGridSpec(
            num_scalar_prefetch=1, grid=(d, nt),
            in_specs=[pl.BlockSpec((1, 1, tr, cols), lambda l, i, k_ref: (k_ref[0], l, i, 0)),
                      pl.BlockSpec((3, 1, tr, cols), lambda l, i, k_ref: (0, l, i, 0))],
            out_specs=pl.BlockSpec((1, tr, cols), lambda l, i, k_ref: (l, i, 0))),
        out_shape=jax.ShapeDtypeStruct((d, h, cols), F32), compiler_params=_cparams(("parallel", "parallel")),
    )(k_idx, p, r)


def _share_halves(qs):
    n = len(qs)

    def body(*refs):
        ins, outs, send_sems, recv_sems = refs[:n], refs[n:2 * n], refs[2 * n], refs[2 * n + 1]
        x, y, c = _mesh_pos()
        cps = [_remote(ins[t], outs[t], send_sems.at[t], recv_sems.at[t], (x, y, 1 - c)) for t in range(n)]
        for cp in cps:
            cp.start()
        for cp in cps:
            cp.wait_recv()
        for cp in cps:
            cp.wait_send()

    return _comm_call(body, "grad_pair_share", qs, [jax.ShapeDtypeStruct(q.shape, q.dtype) for q in qs], n)


def _all_reduce_small(v):
    r, cols = v.shape
    n_dev = 8

    def body(v_ref, o_ref, buf, send_sems, recv_sems):
        x, y, c = _mesh_pos()
        me = 4 * x + 2 * y + c
        buf[me] = v_ref[...]

        def peer(j):
            return (1 - x if j & 4 else x, 1 - y if j & 2 else y, 1 - c if j & 1 else c)

        def copy(j, slot):
            return pltpu.make_async_remote_copy(src_ref=v_ref, dst_ref=buf.at[slot], send_sem=send_sems.at[j - 1],
                                                recv_sem=recv_sems.at[j - 1], device_id=peer(j), device_id_type=MESH)

        sends = [copy(j, me) for j in range(1, n_dev)]
        for cp in sends:
            cp.start()
        for j in range(1, n_dev):
            px, py, pc = peer(j)
            copy(j, 4 * px + 2 * py + pc).wait_recv()
        for cp in sends:
            cp.wait_send()
        acc = buf[0]
        for d in range(1, n_dev):
            acc = acc + buf[d]
        o_ref[...] = acc

    vm = pl.BlockSpec(memory_space=pltpu.VMEM)
    return pl.pallas_call(
        body, name="small_all_reduce", in_specs=[vm], out_specs=vm, out_shape=jax.ShapeDtypeStruct((r, cols), F32),
        scratch_shapes=[pltpu.VMEM((n_dev, r, cols), F32), pltpu.SemaphoreType.DMA((n_dev - 1,)), pltpu.SemaphoreType.DMA((n_dev - 1,))],
        compiler_params=pltpu.CompilerParams(has_side_effects=True),
    )(v)


_COL_SHARDED = ("w_in", "w_q_up", "w_kv_up", "w_ffn_up")


def _shard_cols(blocks, a, b):
    c = blocks.shape[-1]
    out = []
    while a < b:
        k = a // c
        hi = min(b, (k + 1) * c)
        out.append(blocks[k][:, a - k * c:hi - k * c])
        a = hi
    return out


def _pack_w_in_shards(blocks):
    z = lambda n: [jnp.zeros((blocks.shape[1], n), blocks.dtype)]
    cols = lambda a, b: _shard_cols(blocks, a, b)
    return jnp.concatenate(cols(0, 768) + cols(772, 1028) + cols(1188, 2980) + cols(1028, 1156) + cols(768, 772)
                           + z(KR_LANE - N_HEADS) + cols(1156, 1188) + z(LANES - KR_LANE - ROPE_DIM), axis=1)


def _whole_layer(name, blocks):
    if name in _COL_SHARDED:
        return jnp.concatenate([blocks[k] for k in range(N_CHIPS)], axis=1)
    return blocks.reshape(N_CHIPS * blocks.shape[1], blocks.shape[2])


def _split_layer(name, whole):
    if name in _COL_SHARDED:
        c = whole.shape[1] // N_CHIPS
        return jnp.stack([whole[:, k * c:(k + 1) * c] for k in range(N_CHIPS)])
    return whole.reshape(N_CHIPS, whole.shape[0] // N_CHIPS, whole.shape[1])


def _small_to_rows(d):
    v = jnp.concatenate([d[k].astype(F32).reshape(-1) for k in SMALL])
    rows = -(-v.shape[0] // (8 * LANES)) * 8
    return jnp.pad(v, (0, rows * LANES - v.shape[0])).reshape(rows, LANES)


def _small_from_rows(rows, shapes):
    v = rows.reshape(-1)
    out, o = {}, 0
    for k in SMALL:
        sz = int(np.prod(shapes[k]))
        out[k] = v[o:o + sz].reshape(shapes[k])
        o += sz
    return out


_ARG_NAMES = ("x", "positions", "g_mix_pre", "w_in", "b_forget", "g_q_lora", "w_q_up", "g_kv_lora", "w_kv_up", "g_mix_out", "w_out",
              "g_mix_post", "g_ffn_pre", "w_ffn_up", "w_ffn_down", "g_ffn_post")
_WEIGHTS = _ARG_NAMES[2:]


def kernel(x, positions, g_mix_pre, w_in, b_forget, g_q_lora, w_q_up, g_kv_lora, w_kv_up, g_mix_out, w_out, g_mix_post, g_ffn_pre, w_ffn_up, w_ffn_down, g_ffn_post, loss_target, m_g_mix_pre, m_w_in, m_b_forget, m_g_q_lora, m_w_q_up, m_g_kv_lora, m_w_kv_up, m_g_mix_out, m_w_out, m_g_mix_post, m_g_ffn_pre, m_w_ffn_up, m_w_ffn_down, m_g_ffn_post, v_g_mix_pre, v_w_in, v_b_forget, v_g_q_lora, v_w_q_up, v_g_kv_lora, v_w_kv_up, v_g_mix_out, v_w_out, v_g_mix_post, v_g_ffn_pre, v_w_ffn_up, v_w_ffn_down, v_g_ffn_post):
    w = dict(g_mix_pre=g_mix_pre, w_in=w_in, b_forget=b_forget, g_q_lora=g_q_lora, w_q_up=w_q_up, g_kv_lora=g_kv_lora, w_kv_up=w_kv_up,
             g_mix_out=g_mix_out, w_out=w_out, g_mix_post=g_mix_post, g_ffn_pre=g_ffn_pre, w_ffn_up=w_ffn_up, w_ffn_down=w_ffn_down,
             g_ffn_post=g_ffn_post)
    m = dict(g_mix_pre=m_g_mix_pre, w_in=m_w_in, b_forget=m_b_forget, g_q_lora=m_g_q_lora, w_q_up=m_w_q_up, g_kv_lora=m_g_kv_lora,
             w_kv_up=m_w_kv_up, g_mix_out=m_g_mix_out, w_out=m_w_out, g_mix_post=m_g_mix_post, g_ffn_pre=m_g_ffn_pre,
             w_ffn_up=m_w_ffn_up, w_ffn_down=m_w_ffn_down, g_ffn_post=m_g_ffn_post)
    v = dict(g_mix_pre=v_g_mix_pre, w_in=v_w_in, b_forget=v_b_forget, g_q_lora=v_g_q_lora, w_q_up=v_w_q_up, g_kv_lora=v_g_kv_lora,
             w_kv_up=v_w_kv_up, g_mix_out=v_g_mix_out, w_out=v_w_out, g_mix_post=v_g_mix_post, g_ffn_pre=v_g_ffn_pre,
             w_ffn_up=v_w_ffn_up, w_ffn_down=v_w_ffn_down, g_ffn_post=v_g_ffn_post)
    shard_shapes = {k: w[k].shape for k in BIG}
    small_shapes = {k: w[k].shape for k in SMALL}
    c_idx = lax.axis_index("c").astype(jnp.int32).reshape(1)
    k_idx = (2 * lax.axis_index("x") + lax.axis_index("y")).astype(jnp.int32).reshape(1)

    mine = 2 * lax.axis_index("x") + lax.axis_index("y")
    shards_b = [w[k].astype(BF16) for k in BIG]
    gathered = _gather_weights(shards_b)
    four = {k: lax.dynamic_update_slice(g, s[None], (mine, 0, 0, 0)) for k, g, s in zip(BIG, gathered, shards_b)}
    layers = []
    for l in range(DEPTH):
        wk, wv = _pack_w_kv(_whole_layer("w_kv_up", four["w_kv_up"][:, l]))
        layers.append(dict(
            g_mix_pre=g_mix_pre[l], w_in=_pack_w_in_shards(four["w_in"][:, l]), b_forget=b_forget[l], g_q_lora=g_q_lora[l],
            g_kv_lora=g_kv_lora[l], wq=_pack_w_q(_whole_layer("w_q_up", four["w_q_up"][:, l])), wk=wk, wv=wv, g_mix_out=g_mix_out[l],
            w_out=_whole_layer("w_out", four["w_out"][:, l]), g_mix_post=g_mix_post[l], g_ffn_pre=g_ffn_pre[l],
            w_ffn_up=_whole_layer("w_ffn_up", four["w_ffn_up"][:, l]), w_ffn_down=_whole_layer("w_ffn_down", four["w_ffn_down"][:, l]),
            g_ffn_post=g_ffn_post[l]))

    loss_local, dx, grads = _local_step(x[0], positions[0], layers, loss_target[0])
    loss = lax.psum(loss_local, ("x", "y", "c"))

    whole_grad = dict(
        w_in=lambda l: _unpack_dw_in(grads[l]["w_in"]), w_q_up=lambda l: _unpack_dw_q(grads[l]["wq"]),
        w_kv_up=lambda l: _unpack_dw_kv(grads[l]["wk"], grads[l]["wv"]), w_out=lambda l: grads[l]["w_out"],
        w_ffn_up=lambda l: grads[l]["w_ffn_up"], w_ffn_down=lambda l: grads[l]["w_ffn_down"])
    blocks = [jnp.stack([_split_layer(k, whole_grad[k](l)) for l in range(DEPTH)], axis=1) for k in BIG]
    theirs = _exchange_halves(blocks)
    pair = [_pair_add(g, r, c_idx, name="grad_pair_add_" + k) for k, g, r in zip(BIG, blocks, theirs)]
    partial = _exchange_chips([pb for (_, pb) in pair])
    mine_half = [_chip_add(p, r, k_idx, name="grad_chip_add_" + k) for k, (p, _), r in zip(BIG, pair, partial)]
    sibling_half = _share_halves(mine_half)
    first = lax.axis_index("c") == 0
    g_big = {k: jnp.where(first, jnp.concatenate([q, s], axis=1), jnp.concatenate([s, q], axis=1))
             for k, q, s in zip(BIG, mine_half, sibling_half)}

    g_small_local = {k: jnp.stack([grads[l][k].reshape(small_shapes[k][1:]) for l in range(DEPTH)]) for k in SMALL}
    g_small = _small_from_rows(_all_reduce_small(_small_to_rows(g_small_local)), small_shapes)

    g_all = {**g_big, **g_small}
    delta, new_m, new_v = {}, {}, {}
    for k in BIG:
        d, r, c = shard_shapes[k]
        two_d = lambda a: a.reshape(d * r, c)
        dk, mk, vk = _adamw(two_d(w[k]), two_d(g_all[k]), two_d(m[k]), two_d(v[k]), name="adamw_" + k)
        delta[k], new_m[k], new_v[k] = dk.reshape(d, r, c), mk.reshape(d, r, c), vk.reshape(d, r, c)
    ds, ms, vs = _adamw(_small_to_rows(w), _small_to_rows(g_small), _small_to_rows(m), _small_to_rows(v), name="adamw_small")
    delta.update(_small_from_rows(ds, small_shapes))
    new_m.update(_small_from_rows(ms, small_shapes))
    new_v.update(_small_from_rows(vs, small_shapes))

    grad_x = dx.reshape(x.shape)
    return (loss, grad_x, *[g_all[k] for k in _WEIGHTS], *[delta[k] for k in _WEIGHTS], *[new_m[k] for k in _WEIGHTS],
            *[new_v[k] for k in _WEIGHTS])
```

```python
import functools
import math

import numpy as np
import jax
import jax.numpy as jnp
from jax import lax
from jax.experimental import pallas as pl
from jax.experimental.pallas import tpu as pltpu

F32 = jnp.float32
BF16 = jnp.bfloat16
MESH = pl.DeviceIdType.MESH

D_MODEL = 1024
DEPTH = 2
CHUNK = 64
GROUP = 256
HEAD = 64
N_HEADS = 4
Q_RANK = 256
KV_RANK = 128
ROPE_DIM = 32
D_FF = 4096
D_IN = 2980
D_INP = 3072
ROPE_BASE = 10000.0
EPS = 1e-6
LANES = 128
TQ = 128
NEG = -1e30

ADAM_LR, ADAM_B1, ADAM_B2, ADAM_EPS, ADAM_WD, ADAM_STEP = 0.001, 0.9, 0.999, 1e-08, 0.01, 10

OFF_FQ, OFF_FK, OFF_FV, OFF_CQ = 0, 2, 4, 6
OFF_RQ, OFF_RK, OFF_RV, OFF_RG = 8, 10, 12, 14
OFF_SQ, OFF_SK, OFF_SV = 16, 18, 20
OFF_CKV, OFF_MISC = 22, 23
FF_LANE, KR_LANE = 0, 64

VMEM_LIMIT = 56 * 1024 * 1024


def _tile(dim, pref):
    return pref if dim % pref == 0 else dim


def _cparams(sem, vmem=None):
    return pltpu.CompilerParams(dimension_semantics=sem, vmem_limit_bytes=vmem or VMEM_LIMIT)


def _dot(a, b):
    return jnp.dot(a, b, preferred_element_type=F32)


def _dot_nt(a, b):
    return lax.dot_general(a, b, (((1,), (1,)), ((), ())), preferred_element_type=F32)


def _dot_tn(a, b):
    return lax.dot_general(a, b, (((0,), (0,)), ((), ())), preferred_element_type=F32)


def _dot_exact(a, b):
    return jnp.dot(a, b, precision=lax.Precision.HIGHEST, preferred_element_type=F32)


def _matmul(a, b, *, name, ta=False, tb=False, out_dtype=F32, tm=1024, tn=1024, tk=1024,
            relu2=False, relu2_of=None, also_bf16=False):
    if ta:
        kdim, m = a.shape
    else:
        m, kdim = a.shape
    n = b.shape[0] if tb else b.shape[1]
    tm, tn, tk = _tile(m, tm), _tile(n, tn), _tile(kdim, tk)
    nk = kdim // tk
    a_spec = pl.BlockSpec((tk, tm), lambda i, j, k: (k, i)) if ta else pl.BlockSpec((tm, tk), lambda i, j, k: (i, k))
    b_spec = pl.BlockSpec((tn, tk), lambda i, j, k: (j, k)) if tb else pl.BlockSpec((tk, tn), lambda i, j, k: (k, j))
    o_spec = pl.BlockSpec((tm, tn), lambda i, j, k: (i, j))
    two = also_bf16

    def body(*refs):
        refs = list(refs)
        a_ref, b_ref = refs[0], refs[1]
        e_ref = refs[2] if relu2_of is not None else None
        pos = 3 if relu2_of is not None else 2
        o_ref = refs[pos]
        o2_ref = refs[pos + 1] if two else None
        acc_ref = refs[-1]
        k = pl.program_id(2)
        av = a_ref[...].astype(BF16)
        bv = b_ref[...].astype(BF16)
        if ta:
            part = _dot_tn(av, bv)
        elif tb:
            part = _dot_nt(av, bv)
        else:
            part = _dot(av, bv)

        @pl.when(k == 0)
        def _():
            acc_ref[...] = part

        @pl.when(k > 0)
        def _():
            acc_ref[...] += part

        @pl.when(k == nk - 1)
        def _():
            r = acc_ref[...]
            if relu2_of is not None:
                r = r * (2.0 * jnp.sqrt(e_ref[...].astype(F32)))
            if relu2:
                r = jnp.square(jnp.maximum(r, 0.0))
            o_ref[...] = r.astype(o_ref.dtype)
            if also_bf16:
                o2_ref[...] = r.astype(BF16)

    in_specs = [a_spec, b_spec]
    args = [a, b]
    if relu2_of is not None:
        in_specs.append(o_spec)
        args.append(relu2_of)
    out_shape = [jax.ShapeDtypeStruct((m, n), out_dtype)]
    out_specs = [o_spec]
    if two:
        out_shape.append(jax.ShapeDtypeStruct((m, n), BF16))
        out_specs.append(o_spec)
    res = pl.pallas_call(
        body, name=name, grid=(m // tm, n // tn, nk), in_specs=in_specs, out_specs=out_specs, out_shape=out_shape,
        scratch_shapes=[pltpu.VMEM((tm, tn), F32)],
        compiler_params=_cparams(("parallel", "parallel", "arbitrary")),
    )(*args)
    return res if two else res[0]


def _rms(x, g):
    r = lax.rsqrt(jnp.mean(x * x, axis=-1, keepdims=True) + EPS)
    return x * r * g


def _rms_bwd(x, g, dy):
    r = lax.rsqrt(jnp.mean(x * x, axis=-1, keepdims=True) + EPS)
    xh = x * r
    gdy = dy * g
    dx = r * (gdy - xh * jnp.mean(xh * gdy, axis=-1, keepdims=True))
    return dx, xh * dy


def _norm_fwd(x, g, *, name, resid=None, out_dtype=BF16):
    s, d = x.shape
    tr = _tile(s, 256)
    row = pl.BlockSpec((tr, d), lambda i: (i, 0))
    gsp = pl.BlockSpec((1, d), lambda i: (0, 0))

    def body(*refs):
        if resid is None:
            x_ref, g_ref, o_ref = refs
            o_ref[...] = _rms(x_ref[...], g_ref[...]).astype(o_ref.dtype)
        else:
            x_ref, g_ref, r_ref, o_ref = refs
            o_ref[...] = (r_ref[...] + _rms(x_ref[...], g_ref[...])).astype(o_ref.dtype)

    args = [x, g.reshape(1, d)] + ([] if resid is None else [resid])
    return pl.pallas_call(
        body, name=name, grid=(s // tr,), in_specs=[row, gsp] + ([] if resid is None else [row]),
        out_specs=row, out_shape=jax.ShapeDtypeStruct((s, d), out_dtype), compiler_params=_cparams(("parallel",)),
    )(*args)


def _norm_bwd(x, g, dy, *, name, add=None, out_dtype=F32):
    s, d = x.shape
    tr = _tile(s, 256)
    row = pl.BlockSpec((tr, d), lambda i: (i, 0))
    gsp = pl.BlockSpec((1, d), lambda i: (0, 0))

    def body(*refs):
        if add is None:
            x_ref, g_ref, dy_ref, dx_ref, dg_ref = refs
        else:
            x_ref, g_ref, dy_ref, add_ref, dx_ref, dg_ref = refs
        dx, gterm = _rms_bwd(x_ref[...], g_ref[...], dy_ref[...].astype(F32))
        if add is not None:
            dx = dx + add_ref[...]
        dx_ref[...] = dx.astype(dx_ref.dtype)

        @pl.when(pl.program_id(0) == 0)
        def _():
            dg_ref[...] = jnp.zeros_like(dg_ref)

        dg_ref[...] += jnp.sum(gterm, axis=0, keepdims=True)

    args = [x, g.reshape(1, d), dy] + ([] if add is None else [add])
    return pl.pallas_call(
        body, name=name, grid=(s // tr,), in_specs=[row, gsp, row] + ([] if add is None else [row]),
        out_specs=[row, gsp], out_shape=[jax.ShapeDtypeStruct((s, d), out_dtype), jax.ShapeDtypeStruct((1, d), F32)],
        compiler_params=_cparams(("arbitrary",)),
    )(*args)


def _loss_head(y, target):
    s, d = y.shape
    tr = _tile(s, 256)
    row = pl.BlockSpec((tr, d), lambda i: (i, 0))
    lsp = pl.BlockSpec((1, LANES), lambda i: (0, 0))

    def body(y_ref, t_ref, l_ref, dy_ref):
        e = y_ref[...] - t_ref[...]
        dy_ref[...] = e * (1.0 / d)

        @pl.when(pl.program_id(0) == 0)
        def _():
            l_ref[...] = jnp.zeros_like(l_ref)

        part = 0.5 * jnp.sum(jnp.mean(e * e, axis=-1, keepdims=True), axis=0, keepdims=True)
        l_ref[...] += jnp.broadcast_to(part, (1, LANES))

    return pl.pallas_call(
        body, name="loss_head", grid=(s // tr,), in_specs=[row, row], out_specs=[lsp, row],
        out_shape=[jax.ShapeDtypeStruct((1, LANES), F32), jax.ShapeDtypeStruct((s, d), F32)],
        compiler_params=_cparams(("arbitrary",)),
    )(y, target)


def _rope_tables(pos_col):
    s = pos_col.shape[0]
    tr = _tile(s, 512)
    f_mla = ROPE_BASE ** (-jnp.arange(ROPE_DIM // 2, dtype=F32) / (ROPE_DIM // 2))
    f_ret = ROPE_BASE ** (-jnp.arange(HEAD // 2, dtype=F32) / (HEAD // 2))
    fm = jnp.concatenate([jnp.zeros((64,), F32), f_mla, f_mla, jnp.zeros((32,), F32)]).reshape(1, LANES)
    fr = jnp.tile(jnp.concatenate([f_ret, f_ret]), 4).reshape(1, 2 * LANES)

    def body(p_ref, fm_ref, fr_ref, cm_ref, sm_ref, cr_ref, sr_ref):
        p = p_ref[...].astype(F32)
        am = p * fm_ref[...]
        ar = p * fr_ref[...]
        cm_ref[...] = jnp.cos(am)
        sm_ref[...] = jnp.sin(am)
        cr_ref[...] = jnp.cos(ar)
        sr_ref[...] = jnp.sin(ar)

    return pl.pallas_call(
        body, name="rope_tables", grid=(s // tr,),
        in_specs=[pl.BlockSpec((tr, 1), lambda i: (i, 0)), pl.BlockSpec((1, LANES), lambda i: (0, 0)),
                  pl.BlockSpec((1, 2 * LANES), lambda i: (0, 0))],
        out_specs=[pl.BlockSpec((tr, LANES), lambda i: (i, 0))] * 2 + [pl.BlockSpec((tr, 2 * LANES), lambda i: (i, 0))] * 2,
        out_shape=[jax.ShapeDtypeStruct((s, LANES), F32)] * 2 + [jax.ShapeDtypeStruct((s, 2 * LANES), F32)] * 2,
        compiler_params=_cparams(("parallel",)),
    )(pos_col, fm, fr)


def _lane(shape):
    return lax.broadcasted_iota(jnp.int32, shape, len(shape) - 1)


def _rot_mla(z):
    l = _lane(z.shape) % LANES
    n = z.shape[-1]
    return jnp.where(l < 80, -pltpu.roll(z, n - 16, 1), pltpu.roll(z, 16, 1))


def _rot_mla_t(y):
    l = _lane(y.shape) % LANES
    n = y.shape[-1]
    return jnp.where((l >= 64) & (l < 80), pltpu.roll(y, n - 16, 1),
                     jnp.where((l >= 80) & (l < 96), -pltpu.roll(y, 16, 1), 0.0))


def _rot_ret(z):
    l = _lane(z.shape) % HEAD
    n = z.shape[-1]
    return jnp.where(l < 32, -pltpu.roll(z, n - 32, 1), pltpu.roll(z, 32, 1))


def _rot_ret_t(y):
    l = _lane(y.shape) % HEAD
    n = y.shape[-1]
    return jnp.where(l < 32, pltpu.roll(y, n - 32, 1), -pltpu.roll(y, 32, 1))


def _log_sigmoid(x):
    return jnp.minimum(x, 0.0) - jnp.log1p(jnp.exp(-jnp.abs(x)))


def _fox_cum(proj, bias_row):
    s = proj.shape[0]
    nb = s // TQ

    def body(x_ref, b_ref, cc_ref, cr_ref, carry_ref):
        @pl.when(pl.program_id(0) == 0)
        def _():
            carry_ref[...] = jnp.zeros_like(carry_ref)

        ls = _log_sigmoid(x_ref[...] + b_ref[...])
        r = lax.broadcasted_iota(jnp.int32, (TQ, TQ), 0)
        c = lax.broadcasted_iota(jnp.int32, (TQ, TQ), 1)
        tri = (c <= r).astype(F32)
        cum = _dot_exact(tri, ls) + carry_ref[...]
        carry_ref[...] = cum[TQ - 1:TQ, :]
        cc_ref[...] = cum
        cr_ref[...] = cum.T[0:8, :]

    return pl.pallas_call(
        body, name="fox_cum", grid=(nb,),
        in_specs=[pl.BlockSpec((TQ, LANES), lambda i: (i, OFF_MISC)), pl.BlockSpec((1, LANES), lambda i: (0, 0))],
        out_specs=[pl.BlockSpec((TQ, LANES), lambda i: (i, 0)), pl.BlockSpec((8, TQ), lambda i: (0, i))],
        out_shape=[jax.ShapeDtypeStruct((s, LANES), F32), jax.ShapeDtypeStruct((8, s), F32)],
        scratch_shapes=[pltpu.VMEM((1, LANES), F32)],
        compiler_params=_cparams(("arbitrary",)),
    )(proj, bias_row)


def _fox_gate_bwd(dck, drs, proj, bias_row, dkr):
    s = proj.shape[0]
    nb = s // TQ

    def body(d_ref, r_ref, x_ref, b_ref, k_ref, o_ref, db_ref, carry_ref):
        @pl.when(pl.program_id(0) == 0)
        def _():
            carry_ref[...] = jnp.zeros_like(carry_ref)
            db_ref[...] = jnp.zeros_like(db_ref)

        rows = jnp.concatenate([d_ref[0], d_ref[1], jnp.zeros((TQ - 16, TQ), F32)], axis=0)
        t = rows.T
        l = _lane((TQ, LANES))
        r0, r1 = r_ref[0], r_ref[1]
        rsum = jnp.where(l == 0, r0[:, 0:1], jnp.where(l == 1, r0[:, HEAD:HEAD + 1],
                         jnp.where(l == 2, r1[:, 0:1], jnp.where(l == 3, r1[:, HEAD:HEAD + 1], 0.0))))
        dcum = rsum - jnp.where(l < 2, t, pltpu.roll(t, LANES - 6, 1))
        r = lax.broadcasted_iota(jnp.int32, (TQ, TQ), 0)
        c = lax.broadcasted_iota(jnp.int32, (TQ, TQ), 1)
        triu = (c >= r).astype(F32)
        rc = _dot_exact(triu, dcum) + carry_ref[...]
        carry_ref[...] = rc[0:1, :]
        f = x_ref[...] + b_ref[...]
        sig_neg = 1.0 / (1.0 + jnp.exp(f))
        df = jnp.where(l < N_HEADS, rc * sig_neg, 0.0)
        db_ref[...] += jnp.sum(df, axis=0, keepdims=True)
        o_ref[...] = (df + k_ref[...]).astype(o_ref.dtype)

    rev = lambda i: nb - 1 - i
    return pl.pallas_call(
        body, name="fox_gate_bwd", grid=(nb,),
        in_specs=[pl.BlockSpec((2, 8, TQ), lambda i: (0, 0, rev(i))), pl.BlockSpec((2, TQ, LANES), lambda i: (0, rev(i), 0)),
                  pl.BlockSpec((TQ, LANES), lambda i: (rev(i), OFF_MISC)),
                  pl.BlockSpec((1, LANES), lambda i: (0, 0)), pl.BlockSpec((TQ, LANES), lambda i: (rev(i), 0))],
        out_specs=[pl.BlockSpec((TQ, LANES), lambda i: (rev(i), 0)), pl.BlockSpec((1, LANES), lambda i: (0, 0))],
        out_shape=[jax.ShapeDtypeStruct((s, LANES), BF16), jax.ShapeDtypeStruct((1, LANES), F32)],
        scratch_shapes=[pltpu.VMEM((1, LANES), F32)],
        compiler_params=_cparams(("arbitrary",)),
    )(dck, drs, proj, bias_row, dkr)


def _mla_prep(proj, cos_m, sin_m, g_q, g_kv, wq, wk, wv):
    s = proj.shape[0]
    tr = _tile(s, 256)

    def body(cq_ref, ckv_ref, misc_ref, cos_ref, sin_ref, gq_ref, gkv_ref, wq_ref, wk_ref, wv_ref,
             q_ref, k_ref, v_ref, cqn_ref, ckvn_ref):
        cos4 = jnp.tile(cos_ref[...], (1, 4))
        sin4 = jnp.tile(sin_ref[...], (1, 4))
        cqn = _rms(cq_ref[...], gq_ref[...]).astype(BF16)
        ckvn = _rms(ckv_ref[...], gkv_ref[...]).astype(BF16)
        cqn_ref[...] = cqn
        ckvn_ref[...] = ckvn
        zq = _dot(cqn, wq_ref[...])
        q_ref[...] = (zq * cos4 + _rot_mla(zq) * sin4).astype(BF16)
        l = _lane((tr, LANES))
        kr = jnp.where((l >= KR_LANE) & (l < KR_LANE + ROPE_DIM), misc_ref[...], 0.0)
        zk = _dot(ckvn, wk_ref[...]) + jnp.tile(kr, (1, 4))
        k_ref[...] = (zk * cos4 + _rot_mla(zk) * sin4).astype(BF16)
        v_ref[...] = _dot(ckvn, wv_ref[...]).astype(BF16)

    full = lambda a: pl.BlockSpec(a.shape, lambda i: (0, 0))
    rowb = lambda w: pl.BlockSpec((tr, w), lambda i: (i, 0))
    gq2, gkv2 = g_q.reshape(1, Q_RANK), g_kv.reshape(1, KV_RANK)
    return pl.pallas_call(
        body, name="mla_prep", grid=(s // tr,),
        in_specs=[pl.BlockSpec((tr, 256), lambda i: (i, OFF_CQ // 2)), pl.BlockSpec((tr, LANES), lambda i: (i, OFF_CKV)),
                  pl.BlockSpec((tr, LANES), lambda i: (i, OFF_MISC)), rowb(LANES), rowb(LANES),
                  full(gq2), full(gkv2), full(wq), full(wk), full(wv)],
        out_specs=[rowb(512), rowb(512), rowb(512), rowb(256), rowb(128)],
        out_shape=[jax.ShapeDtypeStruct((s, 512), BF16), jax.ShapeDtypeStruct((s, 512), BF16), jax.ShapeDtypeStruct((s, 512), BF16),
                   jax.ShapeDtypeStruct((s, 256), BF16), jax.ShapeDtypeStruct((s, 128), BF16)],
        compiler_params=_cparams(("parallel",)),
    )(proj, proj, proj, cos_m, sin_m, gq2, gkv2, wq, wk, wv)


def _mla_prep_bwd(dq, dk, dv, proj, cqn, ckvn, cos_m, sin_m, g_q, g_kv, wq, wk, wv):
    s = proj.shape[0]
    tr = _tile(s, 256)

    def body(dq_ref, dk_ref, dv_ref, cq_ref, ckv_ref, cqn_ref, ckvn_ref, cos_ref, sin_ref, gq_ref, gkv_ref,
             wq_ref, wk_ref, wv_ref, dcq_ref, dckv_ref, dkr_ref, dwq_ref, dwk_ref, dwv_ref, dgq_ref, dgkv_ref):
        @pl.when(pl.program_id(0) == 0)
        def _():
            for r in (dwq_ref, dwk_ref, dwv_ref, dgq_ref, dgkv_ref):
                r[...] = jnp.zeros_like(r)

        cos4 = jnp.tile(cos_ref[...], (1, 4))
        sin4 = jnp.tile(sin_ref[...], (1, 4))
        dqv = dq_ref[...]
        dzq = dqv * cos4 + _rot_mla_t(dqv * sin4)
        dkv_ = dk_ref[...]
        dzk = dkv_ * cos4 + _rot_mla_t(dkv_ * sin4)
        l = _lane((tr, LANES))
        in_rope = (l >= KR_LANE) & (l < KR_LANE + ROPE_DIM)
        dkr = dzk[:, 0:128] + dzk[:, 128:256] + dzk[:, 256:384] + dzk[:, 384:512]
        dkr_ref[...] = jnp.where(in_rope, dkr, 0.0)
        dzq_b = dzq.astype(BF16)
        dzk_b = dzk.astype(BF16)
        dv_b = dv_ref[...].astype(BF16)
        dcqn = _dot_nt(dzq_b, wq_ref[...])
        dckvn = _dot_nt(dzk_b, wk_ref[...]) + _dot_nt(dv_b, wv_ref[...])
        dwq_ref[...] += _dot_tn(cqn_ref[...], dzq_b)
        dwk_ref[...] += _dot_tn(ckvn_ref[...], dzk_b)
        dwv_ref[...] += _dot_tn(ckvn_ref[...], dv_b)
        dcq, gq_term = _rms_bwd(cq_ref[...], gq_ref[...], dcqn)
        dckv, gkv_term = _rms_bwd(ckv_ref[...], gkv_ref[...], dckvn)
        dcq_ref[...] = dcq.astype(BF16)
        dckv_ref[...] = dckv.astype(BF16)
        dgq_ref[...] += jnp.sum(gq_term, axis=0, keepdims=True)
        dgkv_ref[...] += jnp.sum(gkv_term, axis=0, keepdims=True)

    full = lambda shp: pl.BlockSpec(shp, lambda i: (0, 0))
    rowb = lambda w: pl.BlockSpec((tr, w), lambda i: (i, 0))
    gq2, gkv2 = g_q.reshape(1, Q_RANK), g_kv.reshape(1, KV_RANK)
    return pl.pallas_call(
        body, name="mla_prep_bwd", grid=(s // tr,),
        in_specs=[rowb(512), rowb(512), rowb(512),
                  pl.BlockSpec((tr, 256), lambda i: (i, OFF_CQ // 2)), pl.BlockSpec((tr, LANES), lambda i: (i, OFF_CKV)),
                  rowb(256), rowb(128), rowb(LANES), rowb(LANES), full((1, Q_RANK)), full((1, KV_RANK)),
                  full(wq.shape), full(wk.shape), full(wv.shape)],
        out_specs=[rowb(256), rowb(128), rowb(128), full(wq.shape), full(wk.shape), full(wv.shape),
                   full((1, Q_RANK)), full((1, KV_RANK))],
        out_shape=[jax.ShapeDtypeStruct((s, 256), BF16), jax.ShapeDtypeStruct((s, 128), BF16), jax.ShapeDtypeStruct((s, 128), F32),
                   jax.ShapeDtypeStruct(wq.shape, F32), jax.ShapeDtypeStruct(wk.shape, F32), jax.ShapeDtypeStruct(wv.shape, F32),
                   jax.ShapeDtypeStruct((1, Q_RANK), F32), jax.ShapeDtypeStruct((1, KV_RANK), F32)],
        compiler_params=_cparams(("arbitrary",)),
    )(dq, dk, dv, proj, proj, cqn, ckvn, cos_m, sin_m, gq2, gkv2, wq, wk, wv)


def _ret_prep(proj, cos_r, sin_r):
    s = proj.shape[0]
    tr = _tile(s, 256)

    def body(q_ref, k_ref, cos_ref, sin_ref, qo_ref, ko_ref):
        cos, sin = cos_ref[...], sin_ref[...]
        q, k = q_ref[...], k_ref[...]
        qo_ref[...] = (q * cos + _rot_ret(q) * sin).astype(BF16)
        ko_ref[...] = ((k * cos + _rot_ret(k) * sin) * (HEAD ** -0.5)).astype(BF16)

    rowb = pl.BlockSpec((tr, 256), lambda i: (i, 0))
    return pl.pallas_call(
        body, name="ret_prep", grid=(s // tr,),
        in_specs=[pl.BlockSpec((tr, 256), lambda i: (i, OFF_RQ // 2)), pl.BlockSpec((tr, 256), lambda i: (i, OFF_RK // 2)), rowb, rowb],
        out_specs=[rowb, rowb], out_shape=[jax.ShapeDtypeStruct((s, 256), BF16)] * 2,
        compiler_params=_cparams(("parallel",)),
    )(proj, proj, cos_r, sin_r)


def _ret_prep_bwd(dq, dk, cos_r, sin_r):
    s = dq.shape[0]
    tr = _tile(s, 256)

    def body(dq_ref, dk_ref, cos_ref, sin_ref, qo_ref, ko_ref):
        cos, sin = cos_ref[...], sin_ref[...]
        q, k = dq_ref[...], dk_ref[...] * (HEAD ** -0.5)
        qo_ref[...] = (q * cos + _rot_ret_t(q * sin)).astype(BF16)
        ko_ref[...] = (k * cos + _rot_ret_t(k * sin)).astype(BF16)

    rowb = pl.BlockSpec((tr, 256), lambda i: (i, 0))
    return pl.pallas_call(
        body, name="ret_prep_bwd", grid=(s // tr,), in_specs=[rowb] * 4, out_specs=[rowb, rowb],
        out_shape=[jax.ShapeDtypeStruct((s, 256), BF16)] * 2, compiler_params=_cparams(("parallel",)),
    )(dq, dk, cos_r, sin_r)


_LOG_GAMMA = [float(np.log1p(-np.float32(2.0) ** np.float32(-5.0 - h))) for h in range(N_HEADS)]
_MLA_SCALE = float((HEAD + ROPE_DIM) ** -0.5)
_QK_SCALE = float(HEAD ** -0.5)
KEY_BLOCKS = 4
QB = 256


def _split2(x):
    h = x.astype(BF16)
    return h, (x - h.astype(F32)).astype(BF16)


def _dot2(x, u):
    h, lo = _split2(x)
    return _dot(h, u) + _dot(lo, u)


def _head_pick(block, head, axis):
    idx = lax.broadcasted_iota(jnp.int32, block.shape, axis)
    return jnp.sum(jnp.where(idx == head, block, 0.0), axis=axis, keepdims=True)


def _log_gamma_of(head):
    lg = jnp.float32(_LOG_GAMMA[3])
    for h in (2, 1, 0):
        lg = jnp.where(head == h, jnp.float32(_LOG_GAMMA[h]), lg)
    return lg


def _mixer_specs(mode, s, q_off, k_off, v_off):
    nhb = 2
    bw = 2 * LANES if mode == "mla" else LANES
    nsub = KEY_BLOCKS if (s // TQ) % KEY_BLOCKS == 0 else 1
    q_spec = pl.BlockSpec((QB, bw), lambda p, i: (i, q_off + p))
    k_spec = pl.BlockSpec((s, bw), lambda p, i: (0, k_off + p))
    v_spec = pl.BlockSpec((s, bw), lambda p, i: (0, v_off + p))
    return nhb, N_HEADS // nhb, nsub, q_spec, k_spec, v_spec


def _mixer_geometry(mode, i, nsub):
    w = TQ * nsub
    row = lax.broadcasted_iota(jnp.int32, (QB, w), 0)
    col = lax.broadcasted_iota(jnp.int32, (QB, w), 1)
    nfull = (i * QB) // w
    dist = col - row
    if mode in ("fox", "sb"):
        rel = dist
    else:
        rel = col - (row | (CHUNK - 1))

    def visible(c):
        off = c * w - i * QB
        return (rel + off) < 0 if mode == "sb" else (rel + off) <= 0

    return nfull, dist, visible


def _mixer_fwd(mode, qa, q_off, ka, k_off, va, v_off, *, cum_col=None, cum_row=None):
    s = qa.shape[0]
    nq = s // QB
    nhb, nblk, nsub, q_spec, k_spec, v_spec = _mixer_specs(mode, s, q_off, k_off, v_off)
    w = TQ * nsub
    softmax = mode in ("fox", "mla")
    has_stat = mode != "ret"

    def body(*refs):
        refs = list(refs)
        q_ref, k_ref, v_ref = refs[:3]
        refs = refs[3:]
        if mode == "fox":
            cc_ref, cr_ref = refs[:2]
            refs = refs[2:]
        o_ref = refs[0]
        st_ref = refs[1] if has_stat else None
        p = pl.program_id(0)
        i = pl.program_id(1)
        nfull, dist, visible = _mixer_geometry(mode, i, nsub)
        lane = _lane((1, LANES))
        heads = [nhb * p + hh for hh in range(nhb)]
        wide = mode == "mla"
        q_scale = _QK_SCALE if mode in ("fox", "sb") else 1.0
        cols = [slice(hh * LANES, (hh + 1) * LANES) if wide else slice(None) for hh in range(nhb)]
        if wide:
            qs = [q_ref[:, cols[hh]] for hh in range(nhb)]
        else:
            qf = q_ref[...].astype(F32) * q_scale
            qs = [jnp.where((lane // HEAD) == hh, qf, 0.0).astype(BF16) for hh in range(nhb)]
        if mode == "fox":
            cqs = [_head_pick(cc_ref[...], h, 1) for h in heads]
        if mode == "sb":
            r1 = lax.broadcasted_iota(jnp.int32, (TQ, TQ), 0)
            c1 = lax.broadcasted_iota(jnp.int32, (TQ, TQ), 1)
            u_after = (r1 > c1).astype(BF16)

        def chunk(c):
            return pl.ds(pl.multiple_of(c * w, w), w)

        def scores(c):
            js = chunk(c)
            return tuple(_dot_nt(qs[hh], k_ref[js, cols[hh]]) for hh in range(nhb))

        def head_step(hh, c, js, sc, vj, carry, last):
            if softmax:
                m, l, acc = carry
                if mode == "fox":
                    ck = _head_pick(cr_ref[:, js], heads[hh], 0)
                    sc = sc + (cqs[hh] - ck)
                else:
                    sc = sc * _MLA_SCALE
                if last:
                    sc = jnp.where(visible(c), sc, NEG)
                m_new = jnp.maximum(m, jnp.max(sc, axis=-1, keepdims=True))
                alpha = jnp.exp(m - m_new)
                pr = jnp.exp(sc - m_new)
                l = alpha * l + jnp.sum(pr, axis=-1, keepdims=True)
                acc = alpha * acc + _dot(pr.astype(BF16), vj)
                return m_new, l, acc
            run, acc = carry
            z = sc
            log_beta = jnp.minimum(z, 0.0) - jnp.log1p(jnp.exp(-jnp.abs(z)))
            log_stay = log_beta - z
            if last:
                vis = visible(c)
                log_stay = jnp.where(vis, log_stay, 0.0)
            parts = [None] * nsub
            for b in reversed(range(nsub)):
                ls_b = log_stay[:, b * TQ:(b + 1) * TQ]
                parts[b] = _dot2(ls_b, u_after) + run
                run = run + jnp.sum(ls_b, axis=-1, keepdims=True)
            later = parts[0] if nsub == 1 else jnp.concatenate(parts, axis=1)
            wgt = jnp.exp(log_beta + later)
            if last:
                wgt = jnp.where(vis, wgt, 0.0)
            return run, acc + _dot(wgt.astype(BF16), vj)

        def step(c, c_next, state, last):
            scs, carries = state
            nxt = scores(c_next) if c_next is not None else None
            js = chunk(c)
            return nxt, tuple(head_step(hh, c, js, scs[hh], v_ref[js, cols[hh]], carries[hh], last) for hh in range(nhb))

        zero_acc = jnp.zeros((QB, LANES), F32)
        zero1 = jnp.zeros((QB, 1), F32)
        if softmax:
            init = tuple((jnp.full((QB, 1), NEG, F32), zero1, zero_acc) for _ in range(nhb))
        else:
            init = tuple((zero1, zero_acc) for _ in range(nhb))
        if mode == "sb":
            state = step(nfull, jnp.maximum(nfull - 1, 0), (scores(nfull), init), True)
            _, carries = lax.fori_loop(0, nfull, lambda t, st: step(nfull - 1 - t, jnp.maximum(nfull - 2 - t, 0), st, False), state)
        else:
            state = lax.fori_loop(0, nfull, lambda c, st: step(c, c + 1, st, False), (scores(0), init))
            _, carries = step(nfull, None, state, True)
        if softmax:
            outs = [acc / l for (m, l, acc) in carries]
            stats = [m + jnp.log(l) for (m, l, acc) in carries]
        else:
            outs, stats = [acc for (run, acc) in carries], [run for (run, acc) in carries]
        hm0 = (lane // HEAD) == 0
        pick = lambda a: jnp.where(hm0, a[0], a[1])
        if wide:
            for hh in range(nhb):
                o_ref[:, cols[hh]] = outs[hh]
        else:
            o_ref[...] = pick(outs)
        if has_stat:
            st_ref[0] = pick(stats)

    in_specs = [q_spec, k_spec, v_spec]
    args = [qa, ka, va]
    if mode == "fox":
        in_specs += [pl.BlockSpec((QB, LANES), lambda p, i: (i, 0)), pl.BlockSpec((8, s), lambda p, i: (0, 0))]
        args += [cum_col, cum_row]
    bw = 2 * LANES if mode == "mla" else LANES
    out_specs = [pl.BlockSpec((QB, bw), lambda p, i: (i, p))]
    out_shape = [jax.ShapeDtypeStruct((s, nblk * bw), F32)]
    if has_stat:
        out_specs.append(pl.BlockSpec((1, QB, LANES), lambda p, i: (p, i, 0)))
        out_shape.append(jax.ShapeDtypeStruct((nblk, s, LANES), F32))
    res = pl.pallas_call(
        body, name=mode + "_fwd", grid=(nblk, nq), in_specs=in_specs, out_specs=out_specs, out_shape=out_shape,
        compiler_params=_cparams(("parallel", "parallel")),
    )(*args)
    return res if has_stat else (res[0], None)


def _mixer_bwd(mode, qa, q_off, ka, k_off, va, v_off, o, do, *, stat=None, cum_col=None, cum_row=None):
    s = qa.shape[0]
    nq = s // QB
    nhb, nblk, nsub, q_spec, k_spec, v_spec = _mixer_specs(mode, s, q_off, k_off, v_off)
    w = TQ * nsub
    softmax = mode in ("fox", "mla")
    has_stat = mode != "ret"

    def body(*refs):
        refs = list(refs)
        q_ref, k_ref, v_ref, o_ref, do_ref = refs[:5]
        refs = refs[5:]
        if has_stat:
            st_ref = refs[0]
            refs = refs[1:]
        if mode == "fox":
            cc_ref, cr_ref = refs[:2]
            refs = refs[2:]
        dq_ref, dk_ref, dv_ref = refs[:3]
        dck_ref, drs_ref = refs[3:5] if mode == "fox" else (None, None)
        p = pl.program_id(0)
        i = pl.program_id(1)

        @pl.when(i == 0)
        def _():
            dk_ref[...] = jnp.zeros_like(dk_ref)
            dv_ref[...] = jnp.zeros_like(dv_ref)
            if mode == "fox":
                dck_ref[...] = jnp.zeros_like(dck_ref)

        nfull, dist, visible = _mixer_geometry(mode, i, nsub)
        lane = _lane((1, LANES))
        heads = [nhb * p + hh for hh in range(nhb)]
        dov = do_ref[...]
        wide = mode == "mla"
        q_scale = _QK_SCALE if mode in ("fox", "sb") else 1.0
        cols = [slice(hh * LANES, (hh + 1) * LANES) if wide else slice(None) for hh in range(nhb)]
        if wide:
            prod = dov * o_ref[...]
            qs = [q_ref[:, cols[hh]] for hh in range(nhb)]
            dos = [dov[:, cols[hh]].astype(BF16) for hh in range(nhb)]
            deltas = [jnp.sum(prod[:, cols[hh]], axis=-1, keepdims=True) for hh in range(nhb)]
        else:
            qf = q_ref[...].astype(F32) * q_scale
            prod = dov * o_ref[...]
            hms = [(lane // HEAD) == hh for hh in range(nhb)]
            qs = [jnp.where(hm, qf, 0.0).astype(BF16) for hm in hms]
            dos = [jnp.where(hm, dov, 0.0).astype(BF16) for hm in hms]
            deltas = [jnp.sum(jnp.where(hm, prod, 0.0), axis=-1, keepdims=True) for hm in hms]
        if has_stat:
            st = st_ref[0]
            stats = [st[:, hh * HEAD:hh * HEAD + 1] for hh in range(nhb)]
        if mode == "fox":
            cqs = [_head_pick(cc_ref[...], h, 1) for h in heads]
        if mode == "sb":
            r1 = lax.broadcasted_iota(jnp.int32, (TQ, TQ), 0)
            c1 = lax.broadcasted_iota(jnp.int32, (TQ, TQ), 1)
            u_upto = (r1 <= c1).astype(BF16)
            u_before = (r1 < c1).astype(BF16)

        def chunk(c):
            return pl.ds(pl.multiple_of(c * w, w), w)

        def scores(c):
            js = chunk(c)
            return tuple((_dot_nt(qs[hh], k_ref[js, cols[hh]]), _dot_nt(dos[hh], v_ref[js, cols[hh]])) for hh in range(nhb))

        def emit(hh, js, ds_b, pr_b, dq):
            dk_ref[js, cols[hh]] += _dot_tn(ds_b, qs[hh])
            dv_ref[js, cols[hh]] += _dot_tn(pr_b, dos[hh])
            return dq + _dot(ds_b, k_ref[js, cols[hh]])

        def head_step(hh, c, js, sc_dp, carry, last):
            sc, dp = sc_dp
            if softmax:
                dq, rsum = carry
                if mode == "fox":
                    ck = _head_pick(cr_ref[:, js], heads[hh], 0)
                    sc = sc + (cqs[hh] - ck)
                else:
                    sc = sc * _MLA_SCALE
                if last:
                    sc = jnp.where(visible(c), sc, NEG)
                pr = jnp.exp(sc - stats[hh])
                ds = pr * (dp - deltas[hh])
                if mode == "fox":
                    dck_ref[0, hh:hh + 1, js] += jnp.sum(ds, axis=0, keepdims=True)
                    rsum = rsum + jnp.sum(ds, axis=-1, keepdims=True)
                if mode == "mla":
                    ds = ds * _MLA_SCALE
                return emit(hh, js, ds.astype(BF16), pr.astype(BF16), dq), rsum
            seen, gsum, dq = carry
            z = sc
            log_beta = jnp.minimum(z, 0.0) - jnp.log1p(jnp.exp(-jnp.abs(z)))
            log_stay = log_beta - z
            if last:
                vis = visible(c)
                log_stay = jnp.where(vis, log_stay, 0.0)
            parts = []
            for b in range(nsub):
                ls_b = log_stay[:, b * TQ:(b + 1) * TQ]
                parts.append((stats[hh] - seen) - _dot2(ls_b, u_upto))
                seen = seen + jnp.sum(ls_b, axis=-1, keepdims=True)
            later = parts[0] if nsub == 1 else jnp.concatenate(parts, axis=1)
            wgt = jnp.exp(log_beta + later)
            if last:
                wgt = jnp.where(vis, wgt, 0.0)
            g = dp * wgt
            parts = []
            for b in range(nsub):
                g_b = g[:, b * TQ:(b + 1) * TQ]
                parts.append(gsum + _dot2(g_b, u_before))
                gsum = gsum + jnp.sum(g_b, axis=-1, keepdims=True)
            before = parts[0] if nsub == 1 else jnp.concatenate(parts, axis=1)
            beta = jnp.exp(log_beta)
            dz = g * (1.0 - beta) - beta * before
            if last:
                dz = jnp.where(vis, dz, 0.0)
            return seen, gsum, emit(hh, js, dz.astype(BF16), wgt.astype(BF16), dq)

        def step(c, c_next, state, last):
            scs, carries = state
            nxt = scores(c_next) if c_next is not None else None
            js = chunk(c)
            return nxt, tuple(head_step(hh, c, js, scs[hh], carries[hh], last) for hh in range(nhb))

        zero_acc = jnp.zeros((QB, LANES), F32)
        zero1 = jnp.zeros((QB, 1), F32)
        if softmax:
            init = tuple((zero_acc, zero1) for _ in range(nhb))
        else:
            init = tuple((zero1, zero1, zero_acc) for _ in range(nhb))
        state = lax.fori_loop(0, nfull, lambda c, st: step(c, c + 1, st, False), (scores(0), init))
        _, carries = step(nfull, None, state, True)
        if softmax:
            dqs = [dq for (dq, rsum) in carries]
        else:
            dqs = [dq for (seen, gsum, dq) in carries]
        hm0 = (lane // HEAD) == 0
        if wide:
            for hh in range(nhb):
                dq_ref[:, cols[hh]] = dqs[hh]
        else:
            dq_ref[...] = jnp.where(hm0, dqs[0], dqs[1]) * q_scale
        if mode == "fox":
            drs_ref[0] = jnp.where(hm0, carries[0][1], carries[1][1])

    bw = 2 * LANES if mode == "mla" else LANES
    pair_blk = pl.BlockSpec((QB, bw), lambda p, i: (i, p))
    full_blk = pl.BlockSpec((s, bw), lambda p, i: (0, p))
    stat_blk = pl.BlockSpec((1, QB, LANES), lambda p, i: (p, i, 0))
    in_specs = [q_spec, k_spec, v_spec, pair_blk, pair_blk]
    args = [qa, ka, va, o, do]
    if has_stat:
        in_specs.append(stat_blk)
        args.append(stat)
    if mode == "fox":
        in_specs += [pl.BlockSpec((QB, LANES), lambda p, i: (i, 0)), pl.BlockSpec((8, s), lambda p, i: (0, 0))]
        args += [cum_col, cum_row]
    out_specs = [pair_blk, full_blk, full_blk]
    out_shape = [jax.ShapeDtypeStruct((s, nblk * bw), F32)] * 3
    if mode == "fox":
        out_specs += [pl.BlockSpec((1, 8, s), lambda p, i: (p, 0, 0)), stat_blk]
        out_shape += [jax.ShapeDtypeStruct((2, 8, s), F32), jax.ShapeDtypeStruct((2, s, LANES), F32)]
    return pl.pallas_call(
        body, name=mode + "_bwd", grid=(nblk, nq), in_specs=in_specs, out_specs=out_specs, out_shape=out_shape,
        compiler_params=_cparams(("parallel", "arbitrary")),
    )(*args)


def _ret_geometry(p):
    lane = _lane((1, LANES))
    lg_lane = jnp.where(lane < HEAD, _log_gamma_of(2 * p), _log_gamma_of(2 * p + 1))
    a = lax.broadcasted_iota(jnp.int32, (TQ, 1), 0).astype(F32)
    row = lax.broadcasted_iota(jnp.int32, (TQ, TQ), 0)
    col = lax.broadcasted_iota(jnp.int32, (TQ, TQ), 1)
    same_chunk_or_earlier = (col // CHUNK) <= (row // CHUNK)
    gap = jnp.abs(row - col).astype(F32)
    decays = [jnp.where(same_chunk_or_earlier, jnp.exp(_log_gamma_of(2 * p + hh) * gap), 0.0) for hh in range(2)]
    r = lax.broadcasted_iota(jnp.int32, (LANES, LANES), 0)
    c = lax.broadcasted_iota(jnp.int32, (LANES, LANES), 1)
    own_head = (r // HEAD) == (c // HEAD)
    return lane, lg_lane, a, decays, own_head


def _ret_fwd(qa, ka, va, v_off):
    s = qa.shape[0]
    nq = s // TQ

    def body(q_ref, k_ref, v_ref, o_ref, st_ref, state):
        p = pl.program_id(0)

        @pl.when(pl.program_id(1) == 0)
        def _():
            state[...] = jnp.zeros_like(state)

        lane, lg_lane, a, decays, own_head = _ret_geometry(p)
        q = q_ref[...].astype(F32)
        k = k_ref[...]
        v = v_ref[...]
        s_in = state[...]
        st_ref[0, 0] = s_in
        out = _dot((q * jnp.exp(lg_lane * (a + 1.0))).astype(BF16), s_in.astype(BF16))
        for hh in range(2):
            hm = (lane // HEAD) == hh
            qh = jnp.where(hm, q, 0.0).astype(BF16)
            inner = _dot((_dot_nt(qh, k) * decays[hh]).astype(BF16), v)
            out = out + jnp.where(hm, inner, 0.0)
        o_ref[...] = out
        k_tail = (k.astype(F32) * jnp.exp(lg_lane * (TQ - 1.0 - a))).astype(BF16)
        state[...] = jnp.exp(lg_lane * float(TQ)) * s_in + jnp.where(own_head, _dot_tn(k_tail, v), 0.0)

    blk = lambda off: pl.BlockSpec((TQ, LANES), lambda p, i: (i, off + p))
    return pl.pallas_call(
        body, name="ret_fwd", grid=(2, nq), in_specs=[blk(0), blk(0), blk(v_off)],
        out_specs=[blk(0), pl.BlockSpec((1, 1, LANES, LANES), lambda p, i: (p, i, 0, 0))],
        out_shape=[jax.ShapeDtypeStruct((s, 2 * LANES), F32), jax.ShapeDtypeStruct((2, nq, LANES, LANES), F32)],
        scratch_shapes=[pltpu.VMEM((LANES, LANES), F32)],
        compiler_params=_cparams(("parallel", "arbitrary")),
    )(qa, ka, va)


def _ret_bwd(qa, ka, va, v_off, states, do):
    s = qa.shape[0]
    nq = s // TQ

    def body(q_ref, k_ref, v_ref, st_ref, do_ref, dq_ref, dk_ref, dv_ref, dstate):
        p = pl.program_id(0)

        @pl.when(pl.program_id(1) == 0)
        def _():
            dstate[...] = jnp.zeros_like(dstate)

        lane, lg_lane, a, decays, own_head = _ret_geometry(p)
        q = q_ref[...].astype(F32)
        k = k_ref[...]
        kf = k.astype(F32)
        v = v_ref[...]
        dov = do_ref[...]
        s_in = st_ref[0, 0].astype(BF16)
        ds_next = dstate[...]
        ds_b = ds_next.astype(BF16)
        head_decay = jnp.exp(lg_lane * (a + 1.0))
        tail_decay = jnp.exp(lg_lane * (TQ - 1.0 - a))
        k_tail = (kf * tail_decay).astype(BF16)
        dq = _dot_nt(dov.astype(BF16), s_in) * head_decay
        dk = _dot_nt(v, ds_b) * tail_decay
        dv = _dot(k_tail, ds_b)
        for hh in range(2):
            hm = (lane // HEAD) == hh
            qh = jnp.where(hm, q, 0.0).astype(BF16)
            doh = jnp.where(hm, dov, 0.0).astype(BF16)
            att = (_dot_nt(qh, k) * decays[hh]).astype(BF16)
            datt = (_dot_nt(doh, v) * decays[hh]).astype(BF16)
            dv = dv + _dot_tn(att, doh)
            dk = dk + _dot_tn(datt, qh)
            dq = dq + jnp.where(hm, _dot(datt, k), 0.0)
        dq_ref[...] = dq
        dk_ref[...] = dk
        dv_ref[...] = dv
        q_head = (q * head_decay).astype(BF16)
        dstate[...] = jnp.exp(lg_lane * float(TQ)) * ds_next + jnp.where(own_head, _dot_tn(q_head, dov.astype(BF16)), 0.0)

    blk = lambda off: pl.BlockSpec((TQ, LANES), lambda p, i: (nq - 1 - i, off + p))
    return pl.pallas_call(
        body, name="ret_bwd", grid=(2, nq),
        in_specs=[blk(0), blk(0), blk(v_off), pl.BlockSpec((1, 1, LANES, LANES), lambda p, i: (p, nq - 1 - i, 0, 0)), blk(0)],
        out_specs=[blk(0)] * 3, out_shape=[jax.ShapeDtypeStruct((s, 2 * LANES), F32)] * 3,
        scratch_shapes=[pltpu.VMEM((LANES, LANES), F32)],
        compiler_params=_cparams(("parallel", "arbitrary")),
    )(qa, ka, va, states, do)


def _seg_mean_matrix():
    r = lax.broadcasted_iota(jnp.int32, (GROUP, GROUP), 0)
    c = lax.broadcasted_iota(jnp.int32, (GROUP, GROUP), 1)
    return jnp.where((r // HEAD) == (c // HEAD), 1.0 / HEAD, 0.0).astype(F32)


def _sigmoid(x):
    return 1.0 / (1.0 + jnp.exp(-x))


def _mix_post(oa, ob, oc, od, proj, g):
    s = oa.shape[0]
    tr = _tile(s, 256)

    def body(a_ref, b_ref, c_ref, d_ref, rg_ref, g_ref, o_ref):
        gv = g_ref[...]
        o_ref[:, 0:GROUP] = _rms(a_ref[...], gv[:, 0:GROUP]).astype(BF16)
        o_ref[:, GROUP:2 * GROUP] = _rms(b_ref[...], gv[:, GROUP:2 * GROUP]).astype(BF16)
        seg = _seg_mean_matrix()
        c = c_ref[...]
        cen = c - _dot_exact(c, seg)
        n = cen * lax.rsqrt(_dot_exact(cen * cen, seg) + EPS)
        rg = rg_ref[...]
        o_ref[:, 2 * GROUP:3 * GROUP] = (n * gv[:, 2 * GROUP:3 * GROUP] * (rg * _sigmoid(rg))).astype(BF16)
        o_ref[:, 3 * GROUP:] = _rms(d_ref[...], gv[:, 3 * GROUP:]).astype(BF16)

    blk = pl.BlockSpec((tr, GROUP), lambda i: (i, 0))
    return pl.pallas_call(
        body, name="mix_post", grid=(s // tr,),
        in_specs=[blk] * 4 + [pl.BlockSpec((tr, GROUP), lambda i: (i, OFF_RG // 2)), pl.BlockSpec((1, D_MODEL), lambda i: (0, 0))],
        out_specs=pl.BlockSpec((tr, D_MODEL), lambda i: (i, 0)), out_shape=jax.ShapeDtypeStruct((s, D_MODEL), BF16),
        compiler_params=_cparams(("parallel",)),
    )(oa, ob, oc, od, proj, g.reshape(1, D_MODEL))


def _mix_post_bwd(dmixed, oa, ob, oc, od, proj, g):
    s = oa.shape[0]
    tr = _tile(s, 256)

    def body(dm_ref, a_ref, b_ref, c_ref, d_ref, rg_ref, g_ref, da_ref, db_ref, dc_ref, dd_ref, drg_ref, dg_ref):
        @pl.when(pl.program_id(0) == 0)
        def _():
            dg_ref[...] = jnp.zeros_like(dg_ref)

        gv = g_ref[...]
        dm = dm_ref[...]
        for k, (x_ref, dx_ref) in enumerate(((a_ref, da_ref), (b_ref, db_ref), (None, None), (d_ref, dd_ref))):
            if x_ref is None:
                continue
            cols = slice(k * GROUP, (k + 1) * GROUP)
            dx, gterm = _rms_bwd(x_ref[...], gv[:, cols], dm[:, cols])
            dx_ref[...] = dx
            dg_ref[:, cols] += jnp.sum(gterm, axis=0, keepdims=True)
        cols = slice(2 * GROUP, 3 * GROUP)
        seg = _seg_mean_matrix()
        c = c_ref[...]
        cen = c - _dot_exact(c, seg)
        rstd = lax.rsqrt(_dot_exact(cen * cen, seg) + EPS)
        n = cen * rstd
        rg = rg_ref[...]
        sg = _sigmoid(rg)
        gate = rg * sg
        dy = dm[:, cols]
        gc = gv[:, cols]
        dn = dy * gc * gate
        dg_ref[:, cols] += jnp.sum(dy * n * gate, axis=0, keepdims=True)
        drg_ref[...] = (dy * n * gc * (sg * (1.0 + rg * (1.0 - sg)))).astype(BF16)
        dc_ref[...] = rstd * (dn - _dot_exact(dn, seg) - n * _dot_exact(dn * n, seg))

    blk = pl.BlockSpec((tr, GROUP), lambda i: (i, 0))
    gsp = pl.BlockSpec((1, D_MODEL), lambda i: (0, 0))
    return pl.pallas_call(
        body, name="mix_post_bwd", grid=(s // tr,),
        in_specs=[pl.BlockSpec((tr, D_MODEL), lambda i: (i, 0))] + [blk] * 4 + [pl.BlockSpec((tr, GROUP), lambda i: (i, OFF_RG // 2)), gsp],
        out_specs=[blk] * 5 + [gsp],
        out_shape=[jax.ShapeDtypeStruct((s, GROUP), F32)] * 4 + [jax.ShapeDtypeStruct((s, GROUP), BF16), jax.ShapeDtypeStruct((1, D_MODEL), F32)],
        compiler_params=_cparams(("arbitrary",)),
    )(dmixed, oa, ob, oc, od, proj, g.reshape(1, D_MODEL))


def _pack_w_in(w):
    z = lambda n: jnp.zeros((w.shape[0], n), w.dtype)
    misc = jnp.concatenate([w[:, 768:772], z(KR_LANE - N_HEADS), w[:, 1156:1188], z(LANES - KR_LANE - ROPE_DIM)], axis=1)
    return jnp.concatenate([w[:, 0:768], w[:, 772:1028], w[:, 1188:2980], w[:, 1028:1156], misc], axis=1)


def _unpack_dw_in(d):
    m = OFF_MISC * LANES
    return jnp.concatenate([d[:, 0:768], d[:, m:m + N_HEADS], d[:, 768:1024], d[:, OFF_CKV * LANES:m],
                            d[:, m + KR_LANE:m + KR_LANE + ROPE_DIM], d[:, 1024:OFF_CKV * LANES]], axis=1)


def _pack_w_q(w):
    return jnp.pad(w.reshape(Q_RANK, N_HEADS, HEAD + ROPE_DIM), ((0, 0), (0, 0), (0, LANES - HEAD - ROPE_DIM))).reshape(Q_RANK, 4 * LANES)


def _unpack_dw_q(d):
    return d.reshape(Q_RANK, N_HEADS, LANES)[:, :, :HEAD + ROPE_DIM].reshape(Q_RANK, N_HEADS * (HEAD + ROPE_DIM))


def _pack_w_kv(w):
    w4 = w.reshape(KV_RANK, N_HEADS, 2 * HEAD)
    widen = lambda a: jnp.pad(a, ((0, 0), (0, 0), (0, LANES - HEAD))).reshape(KV_RANK, N_HEADS * LANES)
    return widen(w4[:, :, :HEAD]), widen(w4[:, :, HEAD:])


def _unpack_dw_kv(dk, dv):
    narrow = lambda a: a.reshape(KV_RANK, N_HEADS, LANES)[:, :, :HEAD]
    return jnp.concatenate([narrow(dk), narrow(dv)], axis=2).reshape(KV_RANK, 2 * N_HEADS * HEAD)


def _narrow_heads(a):
    return a.reshape(a.shape[0], N_HEADS, LANES)[:, :, :HEAD].reshape(a.shape[0], N_HEADS * HEAD)


def _widen_heads(a):
    return jnp.pad(a.reshape(a.shape[0], N_HEADS, HEAD), ((0, 0), (0, 0), (0, LANES - HEAD))).reshape(a.shape[0], N_HEADS * LANES)


def _layer_fwd(x, lw, tabs, tag):
    cos_m, sin_m, cos_r, sin_r = tabs
    h1 = _norm_fwd(x, lw["g_mix_pre"], name=tag + "pre_norm")
    proj, projb = _matmul(h1, lw["w_in"], name=tag + "in_proj", also_bf16=True)
    bias_row = jnp.pad(lw["b_forget"], (FF_LANE, LANES - N_HEADS - FF_LANE)).reshape(1, LANES)
    cum_col, cum_row = _fox_cum(proj, bias_row)
    oa, lse_a = _mixer_fwd("fox", projb, OFF_FQ, projb, OFF_FK, projb, OFF_FV, cum_col=cum_col, cum_row=cum_row)
    qm, km, vm, cqn, ckvn = _mla_prep(proj, cos_m, sin_m, lw["g_q_lora"], lw["g_kv_lora"], lw["wq"], lw["wk"], lw["wv"])
    ob_wide, lse_b = _mixer_fwd("mla", qm, 0, km, 0, vm, 0)
    ob = _narrow_heads(ob_wide)
    qr, kr = _ret_prep(proj, cos_r, sin_r)
    oc, ret_states = _ret_fwd(qr, kr, projb, OFF_RV)
    od, tot_d = _mixer_fwd("sb", projb, OFF_SQ, projb, OFF_SK, projb, OFF_SV)
    mixed = _mix_post(oa, ob, oc, od, proj, lw["g_mix_out"])
    mix = _matmul(mixed, lw["w_out"], name=tag + "out_proj")
    x1 = _norm_fwd(mix, lw["g_mix_post"], name=tag + "mix_post_norm", resid=x, out_dtype=F32)
    h2 = _norm_fwd(x1, lw["g_ffn_pre"], name=tag + "ffn_pre_norm")
    u = _matmul(h2, lw["w_ffn_up"], name=tag + "ffn_up", relu2=True, out_dtype=BF16)
    f = _matmul(u, lw["w_ffn_down"], name=tag + "ffn_down")
    x2 = _norm_fwd(f, lw["g_ffn_post"], name=tag + "ffn_post_norm", resid=x1, out_dtype=F32)
    saved = dict(x=x, h1=h1, proj=proj, projb=projb, bias_row=bias_row, cum_col=cum_col, cum_row=cum_row, oa=oa, lse_a=lse_a,
                 qm=qm, km=km, vm=vm, cqn=cqn, ckvn=ckvn, ob=ob, ob_wide=ob_wide, lse_b=lse_b, qr=qr, kr=kr, ret_states=ret_states, oc=oc, od=od, tot_d=tot_d, mixed=mixed,
                 mix=mix, x1=x1, h2=h2, u=u, f=f)
    return x2, saved


def _layer_bwd(dx2, lw, sv, tabs, tag):
    cos_m, sin_m, cos_r, sin_r = tabs
    g = {}
    df, g["g_ffn_post"] = _norm_bwd(sv["f"], lw["g_ffn_post"], dx2, name=tag + "ffn_post_norm_bwd", out_dtype=BF16)
    du_pre = _matmul(df, lw["w_ffn_down"], name=tag + "ffn_down_dx", tb=True, out_dtype=BF16, relu2_of=sv["u"])
    g["w_ffn_down"] = _matmul(sv["u"], df, name=tag + "ffn_down_dw", ta=True)
    dh2 = _matmul(du_pre, lw["w_ffn_up"], name=tag + "ffn_up_dx", tb=True)
    g["w_ffn_up"] = _matmul(sv["h2"], du_pre, name=tag + "ffn_up_dw", ta=True)
    dx1, g["g_ffn_pre"] = _norm_bwd(sv["x1"], lw["g_ffn_pre"], dh2, name=tag + "ffn_pre_norm_bwd", add=dx2)
    dmix, g["g_mix_post"] = _norm_bwd(sv["mix"], lw["g_mix_post"], dx1, name=tag + "mix_post_norm_bwd", out_dtype=BF16)
    dmixed = _matmul(dmix, lw["w_out"], name=tag + "out_proj_dx", tb=True)
    g["w_out"] = _matmul(sv["mixed"], dmix, name=tag + "out_proj_dw", ta=True)
    proj, projb = sv["proj"], sv["projb"]
    doa, dob, doc, dod, drg, g["g_mix_out"] = _mix_post_bwd(dmixed, sv["oa"], sv["ob"], sv["oc"], sv["od"], proj, lw["g_mix_out"])
    dfq, dfk, dfv, dck, drs = _mixer_bwd("fox", projb, OFF_FQ, projb, OFF_FK, projb, OFF_FV, sv["oa"], doa, stat=sv["lse_a"],
                                         cum_col=sv["cum_col"], cum_row=sv["cum_row"])
    dqm, dkm, dvm = _mixer_bwd("mla", sv["qm"], 0, sv["km"], 0, sv["vm"], 0, sv["ob_wide"], _widen_heads(dob), stat=sv["lse_b"])
    dcq, dckv, dkr, dwq, dwk, dwv, g["g_q_lora"], g["g_kv_lora"] = _mla_prep_bwd(
        dqm, dkm, dvm, proj, sv["cqn"], sv["ckvn"], cos_m, sin_m, lw["g_q_lora"], lw["g_kv_lora"], lw["wq"], lw["wk"], lw["wv"])
    dqr, dkr_ret, drv = _ret_bwd(sv["qr"], sv["kr"], projb, OFF_RV, sv["ret_states"], doc)
    drq, drk = _ret_prep_bwd(dqr, dkr_ret, cos_r, sin_r)
    dsq, dsk, dsv = _mixer_bwd("sb", projb, OFF_SQ, projb, OFF_SK, projb, OFF_SV, sv["od"], dod, stat=sv["tot_d"])
    dmisc, db_row = _fox_gate_bwd(dck, drs, proj, sv["bias_row"], dkr)
    b = lambda a: a.astype(BF16)
    dproj = jnp.concatenate([b(dfq), b(dfk), b(dfv), dcq, drq, drk, b(drv), drg, b(dsq), b(dsk), b(dsv), dckv, dmisc], axis=1)
    dh1 = _matmul(dproj, lw["w_in"], name=tag + "in_proj_dx", tb=True)
    g["w_in"] = _matmul(sv["h1"], dproj, name=tag + "in_proj_dw", ta=True)
    dx, g["g_mix_pre"] = _norm_bwd(sv["x"], lw["g_mix_pre"], dh1, name=tag + "pre_norm_bwd", add=dx1)
    g["b_forget"] = db_row[0, FF_LANE:FF_LANE + N_HEADS]
    g["wq"], g["wk"], g["wv"] = dwq, dwk, dwv
    return dx, g


def _local_step(x, positions, layers, target):
    s = x.shape[0]
    tabs = _rope_tables(positions.reshape(s, 1))
    saved = []
    for li, lw in enumerate(layers):
        x, sv = _layer_fwd(x, lw, tabs, "l%d_" % li)
        saved.append(sv)
    loss_row, dx = _loss_head(x, target)
    grads = [None] * len(layers)
    for li in reversed(range(len(layers))):
        dx, grads[li] = _layer_bwd(dx, layers[li], saved[li], tabs, "l%d_" % li)
    return loss_row[0, 0], dx, grads


def _adamw(w, g, m, v, *, name):
    r, c = w.shape
    tr = 256 if r % 256 == 0 else r
    blk = pl.BlockSpec((tr, c), lambda i: (i, 0))
    c1 = 1.0 - ADAM_B1 ** ADAM_STEP
    c2 = 1.0 - ADAM_B2 ** ADAM_STEP

    def body(w_ref, g_ref, m_ref, v_ref, d_ref, mo_ref, vo_ref):
        gv = g_ref[...]
        mn = ADAM_B1 * m_ref[...] + (1.0 - ADAM_B1) * gv
        vn = ADAM_B2 * v_ref[...] + (1.0 - ADAM_B2) * jnp.square(gv)
        mo_ref[...] = mn
        vo_ref[...] = vn
        d_ref[...] = -ADAM_LR * ((mn / c1) / (jnp.sqrt(vn / c2) + ADAM_EPS) + ADAM_WD * w_ref[...])

    return pl.pallas_call(
        body, name=name, grid=(r // tr,), in_specs=[blk] * 4, out_specs=[blk] * 3,
        out_shape=[jax.ShapeDtypeStruct((r, c), F32)] * 3, compiler_params=_cparams(("parallel",)),
    )(w, g, m, v)


BIG = ("w_in", "w_q_up", "w_kv_up", "w_out", "w_ffn_up", "w_ffn_down")
SMALL = ("g_mix_pre", "b_forget", "g_q_lora", "g_kv_lora", "g_mix_out", "g_mix_post", "g_ffn_pre", "g_ffn_post")
N_CHIPS = 4
ANY = pl.BlockSpec(memory_space=pl.ANY)


def _mesh_pos():
    return lax.axis_index("x"), lax.axis_index("y"), lax.axis_index("c")


def _other_chips(x, y):
    return [(1 - x, y), (x, 1 - y), (1 - x, 1 - y)]


def _rows_half(ref, half):
    h = ref.shape[-2] // 2
    return ref.at[(slice(None),) * (len(ref.shape) - 2) + (pl.ds(half * h, h), slice(None))]


def _remote(src, dst, send_sem, recv_sem, device):
    return pltpu.make_async_remote_copy(src_ref=src, dst_ref=dst, send_sem=send_sem, recv_sem=recv_sem, device_id=device,
                                        device_id_type=MESH)


def _comm_call(body, name, args, out_shape, n_sems):
    return pl.pallas_call(
        body, name=name, in_specs=[ANY] * len(args), out_specs=[ANY] * len(out_shape), out_shape=out_shape,
        scratch_shapes=[pltpu.SemaphoreType.DMA((n_sems,)), pltpu.SemaphoreType.DMA((n_sems,))],
        compiler_params=pltpu.CompilerParams(has_side_effects=True),
    )(*args)


def _gather_weights(shards):
    n = len(shards)

    def body(*refs):
        ins, outs = refs[:n], refs[n:2 * n]
        ici_send, ici_recv, d2d_send, d2d_recv = refs[2 * n:]
        x, y, c = _mesh_pos()
        mine = 2 * x + y
        peers = _other_chips(x, y)

        def ici(t, j, block):
            px, py = peers[j]
            return _remote(_rows_half(ins[t], c), _rows_half(outs[t].at[block], c), ici_send.at[3 * t + j], ici_recv.at[3 * t + j],
                           (px, py, c))

        def d2d(t, j, block, half):
            region = _rows_half(outs[t].at[block], half)
            return _remote(region, region, d2d_send.at[3 * t + j], d2d_recv.at[3 * t + j], (x, y, 1 - c))

        sends = [ici(t, j, mine) for t in range(n) for j in range(3)]
        for cp in sends:
            cp.start()
        passed = []
        for t in range(n):
            for j, (px, py) in enumerate(peers):
                ici(t, j, 2 * px + py).wait_recv()
                fwd = d2d(t, j, 2 * px + py, c)
                fwd.start()
                passed.append(fwd)
        for t in range(n):
            for j, (px, py) in enumerate(peers):
                d2d(t, j, 2 * px + py, 1 - c).wait_recv()
        for cp in sends + passed:
            cp.wait_send()

    out_shape = [jax.ShapeDtypeStruct((N_CHIPS,) + a.shape, a.dtype) for a in shards]
    return pl.pallas_call(
        body, name="gather_weights", in_specs=[ANY] * n, out_specs=[ANY] * n, out_shape=out_shape,
        scratch_shapes=[pltpu.SemaphoreType.DMA((3 * n,))] * 4,
        compiler_params=pltpu.CompilerParams(has_side_effects=True),
    )(*shards)


def _exchange_halves(gs):
    n = len(gs)

    def body(*refs):
        ins, outs, send_sems, recv_sems = refs[:n], refs[n:2 * n], refs[2 * n], refs[2 * n + 1]
        x, y, c = _mesh_pos()
        cps = [_remote(_rows_half(ins[t], 1 - c), outs[t], send_sems.at[t], recv_sems.at[t], (x, y, 1 - c)) for t in range(n)]
        for cp in cps:
            cp.start()
        for cp in cps:
            cp.wait_recv()
        for cp in cps:
            cp.wait_send()

    out_shape = [jax.ShapeDtypeStruct(g.shape[:2] + (g.shape[2] // 2, g.shape[3]), g.dtype) for g in gs]
    return _comm_call(body, "grad_pair_exchange", gs, out_shape, n)


def _pair_add(g, r, c_idx, *, name):
    nb, d, rows, cols = g.shape
    h = rows // 2
    tr = min(h, 512)
    nt = h // tr

    def body(c_ref, g_ref, r_ref, p_ref, pb_ref):
        s = g_ref[...] + r_ref[...]
        p_ref[...] = s
        pb_ref[...] = s.astype(BF16)

    blk = pl.BlockSpec((1, 1, tr, cols), lambda k, l, i, c_ref: (k, l, i, 0))
    return pl.pallas_call(
        body, name=name,
        grid_spec=pltpu.PrefetchScalarGridSpec(
            num_scalar_prefetch=1, grid=(nb, d, nt),
            in_specs=[pl.BlockSpec((1, 1, tr, cols), lambda k, l, i, c_ref: (k, l, c_ref[0] * nt + i, 0)), blk],
            out_specs=[blk, blk]),
        out_shape=[jax.ShapeDtypeStruct((nb, d, h, cols), F32), jax.ShapeDtypeStruct((nb, d, h, cols), BF16)],
        compiler_params=_cparams(("parallel", "parallel", "parallel")),
    )(c_idx, g, r)


def _exchange_chips(pbs):
    n = len(pbs)

    def body(*refs):
        ins, outs, send_sems, recv_sems = refs[:n], refs[n:2 * n], refs[2 * n], refs[2 * n + 1]
        x, y, c = _mesh_pos()
        cps = [_remote(ins[t].at[2 * px + py], outs[t].at[j], send_sems.at[3 * t + j], recv_sems.at[3 * t + j], (px, py, c))
               for t in range(n) for j, (px, py) in enumerate(_other_chips(x, y))]
        for cp in cps:
            cp.start()
        for cp in cps:
            cp.wait_recv()
        for cp in cps:
            cp.wait_send()

    out_shape = [jax.ShapeDtypeStruct((3,) + p.shape[1:], p.dtype) for p in pbs]
    return _comm_call(body, "grad_chip_exchange", pbs, out_shape, 3 * n)


def _chip_add(p, r, k_idx, *, name):
    _, d, h, cols = p.shape
    tr = min(h, 512)
    nt = h // tr

    def body(k_ref, p_ref, r_ref, o_ref):
        o_ref[0] = ((p_ref[0, 0] + r_ref[0, 0].astype(F32)) + r_ref[1, 0].astype(F32)) + r_ref[2, 0].astype(F32)

    return pl.pallas_call(
        body, name=name,
        grid_spec=pltpu.PrefetchScalarGridSpec(
            num_scalar_prefetch=1, grid=(d, nt),
            in_specs=[pl.BlockSpec((1, 1, tr, cols), lambda l, i, k_ref: (k_ref[0], l, i, 0)),
                      pl.BlockSpec((3, 1, tr, cols), lambda l, i, k_ref: (0, l, i, 0))],
            out_specs=pl.BlockSpec((1, tr, cols), lambda l, i, k_ref: (l, i, 0))),
        out_shape=jax.ShapeDtypeStruct((d, h, cols), F32), compiler_params=_cparams(("parallel", "parallel")),
    )(k_idx, p, r)


def _share_halves(qs):
    n = len(qs)

    def body(*refs):
        ins, outs, send_sems, recv_sems = refs[:n], refs[n:2 * n], refs[2 * n], refs[2 * n + 1]
        x, y, c = _mesh_pos()
        cps = [_remote(ins[t], outs[t], send_sems.at[t], recv_sems.at[t], (x, y, 1 - c)) for t in range(n)]
        for cp in cps:
            cp.start()
        for cp in cps:
            cp.wait_recv()
        for cp in cps:
            cp.wait_send()

    return _comm_call(body, "grad_pair_share", qs, [jax.ShapeDtypeStruct(q.shape, q.dtype) for q in qs], n)


def _all_reduce_small(v):
    r, cols = v.shape
    n_dev = 8

    def body(v_ref, o_ref, buf, send_sems, recv_sems):
        x, y, c = _mesh_pos()
        me = 4 * x + 2 * y + c
        buf[me] = v_ref[...]

        def peer(j):
            return (1 - x if j & 4 else x, 1 - y if j & 2 else y, 1 - c if j & 1 else c)

        def copy(j, slot):
            return pltpu.make_async_remote_copy(src_ref=v_ref, dst_ref=buf.at[slot], send_sem=send_sems.at[j - 1],
                                                recv_sem=recv_sems.at[j - 1], device_id=peer(j), device_id_type=MESH)

        sends = [copy(j, me) for j in range(1, n_dev)]
        for cp in sends:
            cp.start()
        for j in range(1, n_dev):
            px, py, pc = peer(j)
            copy(j, 4 * px + 2 * py + pc).wait_recv()
        for cp in sends:
            cp.wait_send()
        acc = buf[0]
        for d in range(1, n_dev):
            acc = acc + buf[d]
        o_ref[...] = acc

    vm = pl.BlockSpec(memory_space=pltpu.VMEM)
    return pl.pallas_call(
        body, name="small_all_reduce", in_specs=[vm], out_specs=vm, out_shape=jax.ShapeDtypeStruct((r, cols), F32),
        scratch_shapes=[pltpu.VMEM((n_dev, r, cols), F32), pltpu.SemaphoreType.DMA((n_dev - 1,)), pltpu.SemaphoreType.DMA((n_dev - 1,))],
        compiler_params=pltpu.CompilerParams(has_side_effects=True),
    )(v)


_COL_SHARDED = ("w_in", "w_q_up", "w_kv_up", "w_ffn_up")


def _shard_cols(blocks, a, b):
    c = blocks.shape[-1]
    out = []
    while a < b:
        k = a // c
        hi = min(b, (k + 1) * c)
        out.append(blocks[k][:, a - k * c:hi - k * c])
        a = hi
    return out


def _pack_w_in_shards(blocks):
    z = lambda n: [jnp.zeros((blocks.shape[1], n), blocks.dtype)]
    cols = lambda a, b: _shard_cols(blocks, a, b)
    return jnp.concatenate(cols(0, 768) + cols(772, 1028) + cols(1188, 2980) + cols(1028, 1156) + cols(768, 772)
                           + z(KR_LANE - N_HEADS) + cols(1156, 1188) + z(LANES - KR_LANE - ROPE_DIM), axis=1)


def _whole_layer(name, blocks):
    if name in _COL_SHARDED:
        return jnp.concatenate([blocks[k] for k in range(N_CHIPS)], axis=1)
    return blocks.reshape(N_CHIPS * blocks.shape[1], blocks.shape[2])


def _split_layer(name, whole):
    if name in _COL_SHARDED:
        c = whole.shape[1] // N_CHIPS
        return jnp.stack([whole[:, k * c:(k + 1) * c] for k in range(N_CHIPS)])
    return whole.reshape(N_CHIPS, whole.shape[0] // N_CHIPS, whole.shape[1])


def _small_to_rows(d):
    v = jnp.concatenate([d[k].astype(F32).reshape(-1) for k in SMALL])
    rows = -(-v.shape[0] // (8 * LANES)) * 8
    return jnp.pad(v, (0, rows * LANES - v.shape[0])).reshape(rows, LANES)


def _small_from_rows(rows, shapes):
    v = rows.reshape(-1)
    out, o = {}, 0
    for k in SMALL:
        sz = int(np.prod(shapes[k]))
        out[k] = v[o:o + sz].reshape(shapes[k])
        o += sz
    return out


_ARG_NAMES = ("x", "positions", "g_mix_pre", "w_in", "b_forget", "g_q_lora", "w_q_up", "g_kv_lora", "w_kv_up", "g_mix_out", "w_out",
              "g_mix_post", "g_ffn_pre", "w_ffn_up", "w_ffn_down", "g_ffn_post")
_WEIGHTS = _ARG_NAMES[2:]


def kernel(x, positions, g_mix_pre, w_in, b_forget, g_q_lora, w_q_up, g_kv_lora, w_kv_up, g_mix_out, w_out, g_mix_post, g_ffn_pre, w_ffn_up, w_ffn_down, g_ffn_post, loss_target, m_g_mix_pre, m_w_in, m_b_forget, m_g_q_lora, m_w_q_up, m_g_kv_lora, m_w_kv_up, m_g_mix_out, m_w_out, m_g_mix_post, m_g_ffn_pre, m_w_ffn_up, m_w_ffn_down, m_g_ffn_post, v_g_mix_pre, v_w_in, v_b_forget, v_g_q_lora, v_w_q_up, v_g_kv_lora, v_w_kv_up, v_g_mix_out, v_w_out, v_g_mix_post, v_g_ffn_pre, v_w_ffn_up, v_w_ffn_down, v_g_ffn_post):
    w = dict(g_mix_pre=g_mix_pre, w_in=w_in, b_forget=b_forget, g_q_lora=g_q_lora, w_q_up=w_q_up, g_kv_lora=g_kv_lora, w_kv_up=w_kv_up,
             g_mix_out=g_mix_out, w_out=w_out, g_mix_post=g_mix_post, g_ffn_pre=g_ffn_pre, w_ffn_up=w_ffn_up, w_ffn_down=w_ffn_down,
             g_ffn_post=g_ffn_post)
    m = dict(g_mix_pre=m_g_mix_pre, w_in=m_w_in, b_forget=m_b_forget, g_q_lora=m_g_q_lora, w_q_up=m_w_q_up, g_kv_lora=m_g_kv_lora,
             w_kv_up=m_w_kv_up, g_mix_out=m_g_mix_out, w_out=m_w_out, g_mix_post=m_g_mix_post, g_ffn_pre=m_g_ffn_pre,
             w_ffn_up=m_w_ffn_up, w_ffn_down=m_w_ffn_down, g_ffn_post=m_g_ffn_post)
    v = dict(g_mix_pre=v_g_mix_pre, w_in=v_w_in, b_forget=v_b_forget, g_q_lora=v_g_q_lora, w_q_up=v_w_q_up, g_kv_lora=v_g_kv_lora,
             w_kv_up=v_w_kv_up, g_mix_out=v_g_mix_out, w_out=v_w_out, g_mix_post=v_g_mix_post, g_ffn_pre=v_g_ffn_pre,
             w_ffn_up=v_w_ffn_up, w_ffn_down=v_w_ffn_down, g_ffn_post=v_g_ffn_post)
    shard_shapes = {k: w[k].shape for k in BIG}
    small_shapes = {k: w[k].shape for k in SMALL}
    c_idx = lax.axis_index("c").astype(jnp.int32).reshape(1)
    k_idx = (2 * lax.axis_index("x") + lax.axis_index("y")).astype(jnp.int32).reshape(1)

    mine = 2 * lax.axis_index("x") + lax.axis_index("y")
    shards_b = [w[k].astype(BF16) for k in BIG]
    gathered = _gather_weights(shards_b)
    four = {k: lax.dynamic_update_slice(g, s[None], (mine, 0, 0, 0)) for k, g, s in zip(BIG, gathered, shards_b)}
    layers = []
    for l in range(DEPTH):
        wk, wv = _pack_w_kv(_whole_layer("w_kv_up", four["w_kv_up"][:, l]))
        layers.append(dict(
            g_mix_pre=g_mix_pre[l], w_in=_pack_w_in_shards(four["w_in"][:, l]), b_forget=b_forget[l], g_q_lora=g_q_lora[l],
            g_kv_lora=g_kv_lora[l], wq=_pack_w_q(_whole_layer("w_q_up", four["w_q_up"][:, l])), wk=wk, wv=wv, g_mix_out=g_mix_out[l],
            w_out=_whole_layer("w_out", four["w_out"][:, l]), g_mix_post=g_mix_post[l], g_ffn_pre=g_ffn_pre[l],
            w_ffn_up=_whole_layer("w_ffn_up", four["w_ffn_up"][:, l]), w_ffn_down=_whole_layer("w_ffn_down", four["w_ffn_down"][:, l]),
            g_ffn_post=g_ffn_post[l]))

    loss_local, dx, grads = _local_step(x[0], positions[0], layers, loss_target[0])
    loss = lax.psum(loss_local, ("x", "y", "c"))

    whole_grad = dict(
        w_in=lambda l: _unpack_dw_in(grads[l]["w_in"]), w_q_up=lambda l: _unpack_dw_q(grads[l]["wq"]),
        w_kv_up=lambda l: _unpack_dw_kv(grads[l]["wk"], grads[l]["wv"]), w_out=lambda l: grads[l]["w_out"],
        w_ffn_up=lambda l: grads[l]["w_ffn_up"], w_ffn_down=lambda l: grads[l]["w_ffn_down"])
    blocks = [jnp.stack([_split_layer(k, whole_grad[k](l)) for l in range(DEPTH)], axis=1) for k in BIG]
    theirs = _exchange_halves(blocks)
    pair = [_pair_add(g, r, c_idx, name="grad_pair_add_" + k) for k, g, r in zip(BIG, blocks, theirs)]
    partial = _exchange_chips([pb for (_, pb) in pair])
    mine_half = [_chip_add(p, r, k_idx, name="grad_chip_add_" + k) for k, (p, _), r in zip(BIG, pair, partial)]
    sibling_half = _share_halves(mine_half)
    first = lax.axis_index("c") == 0
    g_big = {k: jnp.where(first, jnp.concatenate([q, s], axis=1), jnp.concatenate([s, q], axis=1))
             for k, q, s in zip(BIG, mine_half, sibling_half)}

    g_small_local = {k: jnp.stack([grads[l][k].reshape(small_shapes[k][1:]) for l in range(DEPTH)]) for k in SMALL}
    g_small = _small_from_rows(_all_reduce_small(_small_to_rows(g_small_local)), small_shapes)

    g_all = {**g_big, **g_small}
    delta, new_m, new_v = {}, {}, {}
    for k in BIG:
        d, r, c = shard_shapes[k]
        two_d = lambda a: a.reshape(d * r, c)
        dk, mk, vk = _adamw(two_d(w[k]), two_d(g_all[k]), two_d(m[k]), two_d(v[k]), name="adamw_" + k)
        delta[k], new_m[k], new_v[k] = dk.reshape(d, r, c), mk.reshape(d, r, c), vk.reshape(d, r, c)
    ds, ms, vs = _adamw(_small_to_rows(w), _small_to_rows(g_small), _small_to_rows(m), _small_to_rows(v), name="adamw_small")
    delta.update(_small_from_rows(ds, small_shapes))
    new_m.update(_small_from_rows(ms, small_shapes))
    new_v.update(_small_from_rows(vs, small_shapes))

    grad_x = dx.reshape(x.shape)
    return (loss, grad_x, *[g_all[k] for k in _WEIGHTS], *[delta[k] for k in _WEIGHTS], *[new_m[k] for k in _WEIGHTS],
            *[new_v[k] for k in _WEIGHTS])
```

```python
import functools
import math

import numpy as np
import jax
import jax.numpy as jnp
from jax import lax
from jax.experimental import pallas as pl
from jax.experimental.pallas import tpu as pltpu

F32 = jnp.float32
BF16 = jnp.bfloat16
MESH = pl.DeviceIdType.MESH

D_MODEL = 1024
DEPTH = 2
CHUNK = 64
GROUP = 256
HEAD = 64
N_HEADS = 4
Q_RANK = 256
KV_RANK = 128
ROPE_DIM = 32
D_FF = 4096
D_IN = 2980
D_INP = 3072
ROPE_BASE = 10000.0
EPS = 1e-6
LANES = 128
TQ = 128
NEG = -1e30

ADAM_LR, ADAM_B1, ADAM_B2, ADAM_EPS, ADAM_WD, ADAM_STEP = 0.001, 0.9, 0.999, 1e-08, 0.01, 10

OFF_FQ, OFF_FK, OFF_FV, OFF_CQ = 0, 2, 4, 6
OFF_RQ, OFF_RK, OFF_RV, OFF_RG = 8, 10, 12, 14
OFF_SQ, OFF_SK, OFF_SV = 16, 18, 20
OFF_CKV, OFF_MISC = 22, 23
FF_LANE, KR_LANE = 0, 64

VMEM_LIMIT = 56 * 1024 * 1024


def _tile(dim, pref):
    return pref if dim % pref == 0 else dim


def _cparams(sem, vmem=None):
    return pltpu.CompilerParams(dimension_semantics=sem, vmem_limit_bytes=vmem or VMEM_LIMIT)


def _dot(a, b):
    return jnp.dot(a, b, preferred_element_type=F32)


def _dot_nt(a, b):
    return lax.dot_general(a, b, (((1,), (1,)), ((), ())), preferred_element_type=F32)


def _dot_tn(a, b):
    return lax.dot_general(a, b, (((0,), (0,)), ((), ())), preferred_element_type=F32)


def _dot_exact(a, b):
    return jnp.dot(a, b, precision=lax.Precision.HIGHEST, preferred_element_type=F32)


def _matmul(a, b, *, name, ta=False, tb=False, out_dtype=F32, tm=1024, tn=1024, tk=1024,
            relu2=False, relu2_of=None, also_bf16=False):
    if ta:
        kdim, m = a.shape
    else:
        m, kdim = a.shape
    n = b.shape[0] if tb else b.shape[1]
    tm, tn, tk = _tile(m, tm), _tile(n, tn), _tile(kdim, tk)
    nk = kdim // tk
    a_spec = pl.BlockSpec((tk, tm), lambda i, j, k: (k, i)) if ta else pl.BlockSpec((tm, tk), lambda i, j, k: (i, k))
    b_spec = pl.BlockSpec((tn, tk), lambda i, j, k: (j, k)) if tb else pl.BlockSpec((tk, tn), lambda i, j, k: (k, j))
    o_spec = pl.BlockSpec((tm, tn), lambda i, j, k: (i, j))
    two = also_bf16

    def body(*refs):
        refs = list(refs)
        a_ref, b_ref = refs[0], refs[1]
        e_ref = refs[2] if relu2_of is not None else None
        pos = 3 if relu2_of is not None else 2
        o_ref = refs[pos]
        o2_ref = refs[pos + 1] if two else None
        acc_ref = refs[-1]
        k = pl.program_id(2)
        av = a_ref[...].astype(BF16)
        bv = b_ref[...].astype(BF16)
        if ta:
            part = _dot_tn(av, bv)
        elif tb:
            part = _dot_nt(av, bv)
        else:
            part = _dot(av, bv)

        @pl.when(k == 0)
        def _():
            acc_ref[...] = part

        @pl.when(k > 0)
        def _():
            acc_ref[...] += part

        @pl.when(k == nk - 1)
        def _():
            r = acc_ref[...]
            if relu2_of is not None:
                r = r * (2.0 * jnp.sqrt(e_ref[...].astype(F32)))
            if relu2:
                r = jnp.square(jnp.maximum(r, 0.0))
            o_ref[...] = r.astype(o_ref.dtype)
            if also_bf16:
                o2_ref[...] = r.astype(BF16)

    in_specs = [a_spec, b_spec]
    args = [a, b]
    if relu2_of is not None:
        in_specs.append(o_spec)
        args.append(relu2_of)
    out_shape = [jax.ShapeDtypeStruct((m, n), out_dtype)]
    out_specs = [o_spec]
    if two:
        out_shape.append(jax.ShapeDtypeStruct((m, n), BF16))
        out_specs.append(o_spec)
    res = pl.pallas_call(
        body, name=name, grid=(m // tm, n // tn, nk), in_specs=in_specs, out_specs=out_specs, out_shape=out_shape,
        scratch_shapes=[pltpu.VMEM((tm, tn), F32)],
        compiler_params=_cparams(("parallel", "parallel", "arbitrary")),
    )(*args)
    return res if two else res[0]


def _rms(x, g):
    r = lax.rsqrt(jnp.mean(x * x, axis=-1, keepdims=True) + EPS)
    return x * r * g


def _rms_bwd(x, g, dy):
    r = lax.rsqrt(jnp.mean(x * x, axis=-1, keepdims=True) + EPS)
    xh = x * r
    gdy = dy * g
    dx = r * (gdy - xh * jnp.mean(xh * gdy, axis=-1, keepdims=True))
    return dx, xh * dy


def _norm_fwd(x, g, *, name, resid=None, out_dtype=BF16):
    s, d = x.shape
    tr = _tile(s, 256)
    row = pl.BlockSpec((tr, d), lambda i: (i, 0))
    gsp = pl.BlockSpec((1, d), lambda i: (0, 0))

    def body(*refs):
        if resid is None:
            x_ref, g_ref, o_ref = refs
            o_ref[...] = _rms(x_ref[...], g_ref[...]).astype(o_ref.dtype)
        else:
            x_ref, g_ref, r_ref, o_ref = refs
            o_ref[...] = (r_ref[...] + _rms(x_ref[...], g_ref[...])).astype(o_ref.dtype)

    args = [x, g.reshape(1, d)] + ([] if resid is None else [resid])
    return pl.pallas_call(
        body, name=name, grid=(s // tr,), in_specs=[row, gsp] + ([] if resid is None else [row]),
        out_specs=row, out_shape=jax.ShapeDtypeStruct((s, d), out_dtype), compiler_params=_cparams(("parallel",)),
    )(*args)


def _norm_bwd(x, g, dy, *, name, add=None, out_dtype=F32):
    s, d = x.shape
    tr = _tile(s, 256)
    row = pl.BlockSpec((tr, d), lambda i: (i, 0))
    gsp = pl.BlockSpec((1, d), lambda i: (0, 0))

    def body(*refs):
        if add is None:
            x_ref, g_ref, dy_ref, dx_ref, dg_ref = refs
        else:
            x_ref, g_ref, dy_ref, add_ref, dx_ref, dg_ref = refs
        dx, gterm = _rms_bwd(x_ref[...], g_ref[...], dy_ref[...].astype(F32))
        if add is not None:
            dx = dx + add_ref[...]
        dx_ref[...] = dx.astype(dx_ref.dtype)

        @pl.when(pl.program_id(0) == 0)
        def _():
            dg_ref[...] = jnp.zeros_like(dg_ref)

        dg_ref[...] += jnp.sum(gterm, axis=0, keepdims=True)

    args = [x, g.reshape(1, d), dy] + ([] if add is None else [add])
    return pl.pallas_call(
        body, name=name, grid=(s // tr,), in_specs=[row, gsp, row] + ([] if add is None else [row]),
        out_specs=[row, gsp], out_shape=[jax.ShapeDtypeStruct((s, d), out_dtype), jax.ShapeDtypeStruct((1, d), F32)],
        compiler_params=_cparams(("arbitrary",)),
    )(*args)


def _loss_head(y, target):
    s, d = y.shape
    tr = _tile(s, 256)
    row = pl.BlockSpec((tr, d), lambda i: (i, 0))
    lsp = pl.BlockSpec((1, LANES), lambda i: (0, 0))

    def body(y_ref, t_ref, l_ref, dy_ref):
        e = y_ref[...] - t_ref[...]
        dy_ref[...] = e * (1.0 / d)

        @pl.when(pl.program_id(0) == 0)
        def _():
            l_ref[...] = jnp.zeros_like(l_ref)

        part = 0.5 * jnp.sum(jnp.mean(e * e, axis=-1, keepdims=True), axis=0, keepdims=True)
        l_ref[...] += jnp.broadcast_to(part, (1, LANES))

    return pl.pallas_call(
        body, name="loss_head", grid=(s // tr,), in_specs=[row, row], out_specs=[lsp, row],
        out_shape=[jax.ShapeDtypeStruct((1, LANES), F32), jax.ShapeDtypeStruct((s, d), F32)],
        compiler_params=_cparams(("arbitrary",)),
    )(y, target)


def _rope_tables(pos_col):
    s = pos_col.shape[0]
    tr = _tile(s, 512)
    f_mla = ROPE_BASE ** (-jnp.arange(ROPE_DIM // 2, dtype=F32) / (ROPE_DIM // 2))
    f_ret = ROPE_BASE ** (-jnp.arange(HEAD // 2, dtype=F32) / (HEAD // 2))
    fm = jnp.concatenate([jnp.zeros((64,), F32), f_mla, f_mla, jnp.zeros((32,), F32)]).reshape(1, LANES)
    fr = jnp.tile(jnp.concatenate([f_ret, f_ret]), 4).reshape(1, 2 * LANES)

    def body(p_ref, fm_ref, fr_ref, cm_ref, sm_ref, cr_ref, sr_ref):
        p = p_ref[...].astype(F32)
        am = p * fm_ref[...]
        ar = p * fr_ref[...]
        cm_ref[...] = jnp.cos(am)
        sm_ref[...] = jnp.sin(am)
        cr_ref[...] = jnp.cos(ar)
        sr_ref[...] = jnp.sin(ar)

    return pl.pallas_call(
        body, name="rope_tables", grid=(s // tr,),
        in_specs=[pl.BlockSpec((tr, 1), lambda i: (i, 0)), pl.BlockSpec((1, LANES), lambda i: (0, 0)),
                  pl.BlockSpec((1, 2 * LANES), lambda i: (0, 0))],
        out_specs=[pl.BlockSpec((tr, LANES), lambda i: (i, 0))] * 2 + [pl.BlockSpec((tr, 2 * LANES), lambda i: (i, 0))] * 2,
        out_shape=[jax.ShapeDtypeStruct((s, LANES), F32)] * 2 + [jax.ShapeDtypeStruct((s, 2 * LANES), F32)] * 2,
        compiler_params=_cparams(("parallel",)),
    )(pos_col, fm, fr)


def _lane(shape):
    return lax.broadcasted_iota(jnp.int32, shape, len(shape) - 1)


def _rot_mla(z):
    l = _lane(z.shape) % LANES
    n = z.shape[-1]
    return jnp.where(l < 80, -pltpu.roll(z, n - 16, 1), pltpu.roll(z, 16, 1))


def _rot_mla_t(y):
    l = _lane(y.shape) % LANES
    n = y.shape[-1]
    return jnp.where((l >= 64) & (l < 80), pltpu.roll(y, n - 16, 1),
                     jnp.where((l >= 80) & (l < 96), -pltpu.roll(y, 16, 1), 0.0))


def _rot_ret(z):
    l = _lane(z.shape) % HEAD
    n = z.shape[-1]
    return jnp.where(l < 32, -pltpu.roll(z, n - 32, 1), pltpu.roll(z, 32, 1))


def _rot_ret_t(y):
    l = _lane(y.shape) % HEAD
    n = y.shape[-1]
    return jnp.where(l < 32, pltpu.roll(y, n - 32, 1), -pltpu.roll(y, 32, 1))


def _log_sigmoid(x):
    return jnp.minimum(x, 0.0) - jnp.log1p(jnp.exp(-jnp.abs(x)))


def _fox_cum(proj, bias_row):
    s = proj.shape[0]
    nb = s // TQ

    def body(x_ref, b_ref, cc_ref, cr_ref, carry_ref):
        @pl.when(pl.program_id(0) == 0)
        def _():
            carry_ref[...] = jnp.zeros_like(carry_ref)

        ls = _log_sigmoid(x_ref[...] + b_ref[...])
        r = lax.broadcasted_iota(jnp.int32, (TQ, TQ), 0)
        c = lax.broadcasted_iota(jnp.int32, (TQ, TQ), 1)
        tri = (c <= r).astype(F32)
        cum = _dot_exact(tri, ls) + carry_ref[...]
        carry_ref[...] = cum[TQ - 1:TQ, :]
        cc_ref[...] = cum
        cr_ref[...] = cum.T[0:8, :]

    return pl.pallas_call(
        body, name="fox_cum", grid=(nb,),
        in_specs=[pl.BlockSpec((TQ, LANES), lambda i: (i, OFF_MISC)), pl.BlockSpec((1, LANES), lambda i: (0, 0))],
        out_specs=[pl.BlockSpec((TQ, LANES), lambda i: (i, 0)), pl.BlockSpec((8, TQ), lambda i: (0, i))],
        out_shape=[jax.ShapeDtypeStruct((s, LANES), F32), jax.ShapeDtypeStruct((8, s), F32)],
        scratch_shapes=[pltpu.VMEM((1, LANES), F32)],
        compiler_params=_cparams(("arbitrary",)),
    )(proj, bias_row)


def _fox_gate_bwd(dck, drs, proj, bias_row, dkr):
    s = proj.shape[0]
    nb = s // TQ

    def body(d_ref, r_ref, x_ref, b_ref, k_ref, o_ref, db_ref, carry_ref):
        @pl.when(pl.program_id(0) == 0)
        def _():
            carry_ref[...] = jnp.zeros_like(carry_ref)
            db_ref[...] = jnp.zeros_like(db_ref)

        rows = jnp.concatenate([d_ref[0], d_ref[1], jnp.zeros((TQ - 16, TQ), F32)], axis=0)
        t = rows.T
        l = _lane((TQ, LANES))
        r0, r1 = r_ref[0], r_ref[1]
        rsum = jnp.where(l == 0, r0[:, 0:1], jnp.where(l == 1, r0[:, HEAD:HEAD + 1],
                         jnp.where(l == 2, r1[:, 0:1], jnp.where(l == 3, r1[:, HEAD:HEAD + 1], 0.0))))
        dcum = rsum - jnp.where(l < 2, t, pltpu.roll(t, LANES - 6, 1))
        r = lax.broadcasted_iota(jnp.int32, (TQ, TQ), 0)
        c = lax.broadcasted_iota(jnp.int32, (TQ, TQ), 1)
        triu = (c >= r).astype(F32)
        rc = _dot_exact(triu, dcum) + carry_ref[...]
        carry_ref[...] = rc[0:1, :]
        f = x_ref[...] + b_ref[...]
        sig_neg = 1.0 / (1.0 + jnp.exp(f))
        df = jnp.where(l < N_HEADS, rc * sig_neg, 0.0)
        db_ref[...] += jnp.sum(df, axis=0, keepdims=True)
        o_ref[...] = (df + k_ref[...]).astype(o_ref.dtype)

    rev = lambda i: nb - 1 - i
    return pl.pallas_call(
        body, name="fox_gate_bwd", grid=(nb,),
        in_specs=[pl.BlockSpec((2, 8, TQ), lambda i: (0, 0, rev(i))), pl.BlockSpec((2, TQ, LANES), lambda i: (0, rev(i), 0)),
                  pl.BlockSpec((TQ, LANES), lambda i: (rev(i), OFF_MISC)),
                  pl.BlockSpec((1, LANES), lambda i: (0, 0)), pl.BlockSpec((TQ, LANES), lambda i: (rev(i), 0))],
        out_specs=[pl.BlockSpec((TQ, LANES), lambda i: (rev(i), 0)), pl.BlockSpec((1, LANES), lambda i: (0, 0))],
        out_shape=[jax.ShapeDtypeStruct((s, LANES), BF16), jax.ShapeDtypeStruct((1, LANES), F32)],
        scratch_shapes=[pltpu.VMEM((1, LANES), F32)],
        compiler_params=_cparams(("arbitrary",)),
    )(dck, drs, proj, bias_row, dkr)


def _mla_prep(proj, cos_m, sin_m, g_q, g_kv, wq, wk, wv):
    s = proj.shape[0]
    tr = _tile(s, 256)

    def body(cq_ref, ckv_ref, misc_ref, cos_ref, sin_ref, gq_ref, gkv_ref, wq_ref, wk_ref, wv_ref,
             q_ref, k_ref, v_ref, cqn_ref, ckvn_ref):
        cos4 = jnp.tile(cos_ref[...], (1, 4))
        sin4 = jnp.tile(sin_ref[...], (1, 4))
        cqn = _rms(cq_ref[...], gq_ref[...]).astype(BF16)
        ckvn = _rms(ckv_ref[...], gkv_ref[...]).astype(BF16)
        cqn_ref[...] = cqn
        ckvn_ref[...] = ckvn
        zq = _dot(cqn, wq_ref[...])
        q_ref[...] = (zq * cos4 + _rot_mla(zq) * sin4).astype(BF16)
        l = _lane((tr, LANES))
        kr = jnp.where((l >= KR_LANE) & (l < KR_LANE + ROPE_DIM), misc_ref[...], 0.0)
        zk = _dot(ckvn, wk_ref[...]) + jnp.tile(kr, (1, 4))
        k_ref[...] = (zk * cos4 + _rot_mla(zk) * sin4).astype(BF16)
        v_ref[...] = _dot(ckvn, wv_ref[...]).astype(BF16)

    full = lambda a: pl.BlockSpec(a.shape, lambda i: (0, 0))
    rowb = lambda w: pl.BlockSpec((tr, w), lambda i: (i, 0))
    gq2, gkv2 = g_q.reshape(1, Q_RANK), g_kv.reshape(1, KV_RANK)
    return pl.pallas_call(
        body, name="mla_prep", grid=(s // tr,),
        in_specs=[pl.BlockSpec((tr, 256), lambda i: (i, OFF_CQ // 2)), pl.BlockSpec((tr, LANES), lambda i: (i, OFF_CKV)),
                  pl.BlockSpec((tr, LANES), lambda i: (i, OFF_MISC)), rowb(LANES), rowb(LANES),
                  full(gq2), full(gkv2), full(wq), full(wk), full(wv)],
        out_specs=[rowb(512), rowb(512), rowb(512), rowb(256), rowb(128)],
        out_shape=[jax.ShapeDtypeStruct((s, 512), BF16), jax.ShapeDtypeStruct((s, 512), BF16), jax.ShapeDtypeStruct((s, 512), BF16),
                   jax.ShapeDtypeStruct((s, 256), BF16), jax.ShapeDtypeStruct((s, 128), BF16)],
        compiler_params=_cparams(("parallel",)),
    )(proj, proj, proj, cos_m, sin_m, gq2, gkv2, wq, wk, wv)


def _mla_prep_bwd(dq, dk, dv, proj, cqn, ckvn, cos_m, sin_m, g_q, g_kv, wq, wk, wv):
    s = proj.shape[0]
    tr = _tile(s, 256)

    def body(dq_ref, dk_ref, dv_ref, cq_ref, ckv_ref, cqn_ref, ckvn_ref, cos_ref, sin_ref, gq_ref, gkv_ref,
             wq_ref, wk_ref, wv_ref, dcq_ref, dckv_ref, dkr_ref, dwq_ref, dwk_ref, dwv_ref, dgq_ref, dgkv_ref):
        @pl.when(pl.program_id(0) == 0)
        def _():
            for r in (dwq_ref, dwk_ref, dwv_ref, dgq_ref, dgkv_ref):
                r[...] = jnp.zeros_like(r)

        cos4 = jnp.tile(cos_ref[...], (1, 4))
        sin4 = jnp.tile(sin_ref[...], (1, 4))
        dqv = dq_ref[...]
        dzq = dqv * cos4 + _rot_mla_t(dqv * sin4)
        dkv_ = dk_ref[...]
        dzk = dkv_ * cos4 + _rot_mla_t(dkv_ * sin4)
        l = _lane((tr, LANES))
        in_rope = (l >= KR_LANE) & (l < KR_LANE + ROPE_DIM)
        dkr = dzk[:, 0:128] + dzk[:, 128:256] + dzk[:, 256:384] + dzk[:, 384:512]
        dkr_ref[...] = jnp.where(in_rope, dkr, 0.0)
        dzq_b = dzq.astype(BF16)
        dzk_b = dzk.astype(BF16)
        dv_b = dv_ref[...].astype(BF16)
        dcqn = _dot_nt(dzq_b, wq_ref[...])
        dckvn = _dot_nt(dzk_b, wk_ref[...]) + _dot_nt(dv_b, wv_ref[...])
        dwq_ref[...] += _dot_tn(cqn_ref[...], dzq_b)
        dwk_ref[...] += _dot_tn(ckvn_ref[...], dzk_b)
        dwv_ref[...] += _dot_tn(ckvn_ref[...], dv_b)
        dcq, gq_term = _rms_bwd(cq_ref[...], gq_ref[...], dcqn)
        dckv, gkv_term = _rms_bwd(ckv_ref[...], gkv_ref[...], dckvn)
        dcq_ref[...] = dcq.astype(BF16)
        dckv_ref[...] = dckv.astype(BF16)
        dgq_ref[...] += jnp.sum(gq_term, axis=0, keepdims=True)
        dgkv_ref[...] += jnp.sum(gkv_term, axis=0, keepdims=True)

    full = lambda shp: pl.BlockSpec(shp, lambda i: (0, 0))
    rowb = lambda w: pl.BlockSpec((tr, w), lambda i: (i, 0))
    gq2, gkv2 = g_q.reshape(1, Q_RANK), g_kv.reshape(1, KV_RANK)
    return pl.pallas_call(
        body, name="mla_prep_bwd", grid=(s // tr,),
        in_specs=[rowb(512), rowb(512), rowb(512),
                  pl.BlockSpec((tr, 256), lambda i: (i, OFF_CQ // 2)), pl.BlockSpec((tr, LANES), lambda i: (i, OFF_CKV)),
                  rowb(256), rowb(128), rowb(LANES), rowb(LANES), full((1, Q_RANK)), full((1, KV_RANK)),
                  full(wq.shape), full(wk.shape), full(wv.shape)],
        out_specs=[rowb(256), rowb(128), rowb(128), full(wq.shape), full(wk.shape), full(wv.shape),
                   full((1, Q_RANK)), full((1, KV_RANK))],
        out_shape=[jax.ShapeDtypeStruct((s, 256), BF16), jax.ShapeDtypeStruct((s, 128), BF16), jax.ShapeDtypeStruct((s, 128), F32),
                   jax.ShapeDtypeStruct(wq.shape, F32), jax.ShapeDtypeStruct(wk.shape, F32), jax.ShapeDtypeStruct(wv.shape, F32),
                   jax.ShapeDtypeStruct((1, Q_RANK), F32), jax.ShapeDtypeStruct((1, KV_RANK), F32)],
        compiler_params=_cparams(("arbitrary",)),
    )(dq, dk, dv, proj, proj, cqn, ckvn, cos_m, sin_m, gq2, gkv2, wq, wk, wv)


def _ret_prep(proj, cos_r, sin_r):
    s = proj.shape[0]
    tr = _tile(s, 256)

    def body(q_ref, k_ref, cos_ref, sin_ref, qo_ref, ko_ref):
        cos, sin = cos_ref[...], sin_ref[...]
        q, k = q_ref[...], k_ref[...]
        qo_ref[...] = (q * cos + _rot_ret(q) * sin).astype(BF16)
        ko_ref[...] = ((k * cos + _rot_ret(k) * sin) * (HEAD ** -0.5)).astype(BF16)

    rowb = pl.BlockSpec((tr, 256), lambda i: (i, 0))
    return pl.pallas_call(
        body, name="ret_prep", grid=(s // tr,),
        in_specs=[pl.BlockSpec((tr, 256), lambda i: (i, OFF_RQ // 2)), pl.BlockSpec((tr, 256), lambda i: (i, OFF_RK // 2)), rowb, rowb],
        out_specs=[rowb, rowb], out_shape=[jax.ShapeDtypeStruct((s, 256), BF16)] * 2,
        compiler_params=_cparams(("parallel",)),
    )(proj, proj, cos_r, sin_r)


def _ret_prep_bwd(dq, dk, cos_r, sin_r):
    s = dq.shape[0]
    tr = _tile(s, 256)

    def body(dq_ref, dk_ref, cos_ref, sin_ref, qo_ref, ko_ref):
        cos, sin = cos_ref[...], sin_ref[...]
        q, k = dq_ref[...], dk_ref[...] * (HEAD ** -0.5)
        qo_ref[...] = (q * cos + _rot_ret_t(q * sin)).astype(BF16)
        ko_ref[...] = (k * cos + _rot_ret_t(k * sin)).astype(BF16)

    rowb = pl.BlockSpec((tr, 256), lambda i: (i, 0))
    return pl.pallas_call(
        body, name="ret_prep_bwd", grid=(s // tr,), in_specs=[rowb] * 4, out_specs=[rowb, rowb],
        out_shape=[jax.ShapeDtypeStruct((s, 256), BF16)] * 2, compiler_params=_cparams(("parallel",)),
    )(dq, dk, cos_r, sin_r)


_LOG_GAMMA = [float(np.log1p(-np.float32(2.0) ** np.float32(-5.0 - h))) for h in range(N_HEADS)]
_MLA_SCALE = float((HEAD + ROPE_DIM) ** -0.5)
_QK_SCALE = float(HEAD ** -0.5)
KEY_BLOCKS = 4
QB = 256


def _split2(x):
    h = x.astype(BF16)
    return h, (x - h.astype(F32)).astype(BF16)


def _dot2(x, u):
    h, lo = _split2(x)
    return _dot(h, u) + _dot(lo, u)


def _head_pick(block, head, axis):
    idx = lax.broadcasted_iota(jnp.int32, block.shape, axis)
    return jnp.sum(jnp.where(idx == head, block, 0.0), axis=axis, keepdims=True)


def _log_gamma_of(head):
    lg = jnp.float32(_LOG_GAMMA[3])
    for h in (2, 1, 0):
        lg = jnp.where(head == h, jnp.float32(_LOG_GAMMA[h]), lg)
    return lg


def _mixer_specs(mode, s, q_off, k_off, v_off):
    nhb = 2
    bw = 2 * LANES if mode == "mla" else LANES
    nsub = KEY_BLOCKS if (s // TQ) % KEY_BLOCKS == 0 else 1
    q_spec = pl.BlockSpec((QB, bw), lambda p, i: (i, q_off + p))
    k_spec = pl.BlockSpec((s, bw), lambda p, i: (0, k_off + p))
    v_spec = pl.BlockSpec((s, bw), lambda p, i: (0, v_off + p))
    return nhb, N_HEADS // nhb, nsub, q_spec, k_spec, v_spec


def _mixer_geometry(mode, i, nsub):
    w = TQ * nsub
    row = lax.broadcasted_iota(jnp.int32, (QB, w), 0)
    col = lax.broadcasted_iota(jnp.int32, (QB, w), 1)
    nfull = (i * QB) // w
    dist = col - row
    if mode in ("fox", "sb"):
        rel = dist
    else:
        rel = col - (row | (CHUNK - 1))

    def visible(c):
        off = c * w - i * QB
        return (rel + off) < 0 if mode == "sb" else (rel + off) <= 0

    return nfull, dist, visible


class _SideJob:
    def __init__(self, inputs, out_shape, n_sems, sends, recvs):
        self.inputs, self.out_shape, self.n_sems, self.sends, self.recvs = list(inputs), list(out_shape), n_sems, sends, recvs


def _carry_side_job(body, n_in, n_out, side, n_steps):
    if side is None:
        return body
    si, so = len(side.inputs), len(side.out_shape)

    def at(corner):
        ok = pl.program_id(0) == corner[0]
        for d in range(1, len(n_steps)):
            ok = ok & (pl.program_id(d) == corner[d])
        return ok

    def wrapped(*refs):
        ins, s_ins = refs[:n_in], refs[n_in:n_in + si]
        outs, s_outs = refs[n_in + si:n_in + si + n_out], refs[n_in + si + n_out:n_in + si + n_out + so]
        scratch, send, recv = refs[n_in + si + n_out + so:-2], refs[-2], refs[-1]

        @pl.when(at([0] * len(n_steps)))
        def _():
            for cp in side.sends(s_ins, s_outs, send, recv):
                cp.start()

        body(*ins, *outs, *scratch)

        @pl.when(at([n - 1 for n in n_steps]))
        def _():
            for cp in side.recvs(s_ins, s_outs, send, recv):
                cp.wait_recv()
            for cp in side.sends(s_ins, s_outs, send, recv):
                cp.wait_send()

    return wrapped


def _side_specs(side):
    if side is None:
        return [], [], []
    hbm = pl.BlockSpec(memory_space=pl.ANY)
    return ([hbm] * len(side.inputs), [hbm] * len(side.out_shape),
            [pltpu.SemaphoreType.DMA((side.n_sems,)), pltpu.SemaphoreType.DMA((side.n_sems,))])


def _mixer_fwd(mode, qa, q_off, ka, k_off, va, v_off, *, cum_col=None, cum_row=None, side=None):
    s = qa.shape[0]
    nq = s // QB
    nhb, nblk, nsub, q_spec, k_spec, v_spec = _mixer_specs(mode, s, q_off, k_off, v_off)
    w = TQ * nsub
    softmax = mode in ("fox", "mla")
    has_stat = mode != "ret"

    def body(*refs):
        refs = list(refs)
        q_ref, k_ref, v_ref = refs[:3]
        refs = refs[3:]
        if mode == "fox":
            cc_ref, cr_ref = refs[:2]
            refs = refs[2:]
        o_ref = refs[0]
        st_ref = refs[1] if has_stat else None
        p = pl.program_id(0)
        i = pl.program_id(1)
        nfull, dist, visible = _mixer_geometry(mode, i, nsub)
        lane = _lane((1, LANES))
        heads = [nhb * p + hh for hh in range(nhb)]
        wide = mode == "mla"
        q_scale = _QK_SCALE if mode in ("fox", "sb") else 1.0
        cols = [slice(hh * LANES, (hh + 1) * LANES) if wide else slice(None) for hh in range(nhb)]
        if wide:
            qs = [q_ref[:, cols[hh]] for hh in range(nhb)]
        else:
            qf = q_ref[...].astype(F32) * q_scale
            qs = [jnp.where((lane // HEAD) == hh, qf, 0.0).astype(BF16) for hh in range(nhb)]
        if mode == "fox":
            cqs = [_head_pick(cc_ref[...], h, 1) for h in heads]
        if mode == "sb":
            r1 = lax.broadcasted_iota(jnp.int32, (TQ, TQ), 0)
            c1 = lax.broadcasted_iota(jnp.int32, (TQ, TQ), 1)
            u_after = (r1 > c1).astype(BF16)

        def chunk(c):
            return pl.ds(pl.multiple_of(c * w, w), w)

        def scores(c):
            js = chunk(c)
            return tuple(_dot_nt(qs[hh], k_ref[js, cols[hh]]) for hh in range(nhb))

        def head_step(hh, c, js, sc, vj, carry, last):
            if softmax:
                m, l, acc = carry
                if mode == "fox":
                    ck = _head_pick(cr_ref[:, js], heads[hh], 0)
                    sc = sc + (cqs[hh] - ck)
                else:
                    sc = sc * _MLA_SCALE
                if last:
                    sc = jnp.where(visible(c), sc, NEG)
                m_new = jnp.maximum(m, jnp.max(sc, axis=-1, keepdims=True))
                alpha = jnp.exp(m - m_new)
                pr = jnp.exp(sc - m_new)
                l = alpha * l + jnp.sum(pr, axis=-1, keepdims=True)
                acc = alpha * acc + _dot(pr.astype(BF16), vj)
                return m_new, l, acc
            run, acc = carry
            z = sc
            log_beta = jnp.minimum(z, 0.0) - jnp.log(1.0 + jnp.exp(-jnp.abs(z)))
            log_stay = log_beta - z
            if last:
                vis = visible(c)
                log_stay = jnp.where(vis, log_stay, 0.0)
            parts = [None] * nsub
            for b in reversed(range(nsub)):
                ls_b = log_stay[:, b * TQ:(b + 1) * TQ]
                parts[b] = _dot2(ls_b, u_after) + run
                run = run + jnp.sum(ls_b, axis=-1, keepdims=True)
            later = parts[0] if nsub == 1 else jnp.concatenate(parts, axis=1)
            wgt = jnp.exp(log_beta + later)
            if last:
                wgt = jnp.where(vis, wgt, 0.0)
            return run, acc + _dot(wgt.astype(BF16), vj)

        def step(c, c_next, state, last):
            scs, carries = state
            nxt = scores(c_next) if c_next is not None else None
            js = chunk(c)
            return nxt, tuple(head_step(hh, c, js, scs[hh], v_ref[js, cols[hh]], carries[hh], last) for hh in range(nhb))

        zero_acc = jnp.zeros((QB, LANES), F32)
        zero1 = jnp.zeros((QB, 1), F32)
        if softmax:
            init = tuple((jnp.full((QB, 1), NEG, F32), zero1, zero_acc) for _ in range(nhb))
        else:
            init = tuple((zero1, zero_acc) for _ in range(nhb))
        if mode == "sb":
            state = step(nfull, jnp.maximum(nfull - 1, 0), (scores(nfull), init), True)
            _, carries = lax.fori_loop(0, nfull, lambda t, st: step(nfull - 1 - t, jnp.maximum(nfull - 2 - t, 0), st, False), state)
        else:
            state = lax.fori_loop(0, nfull, lambda c, st: step(c, c + 1, st, False), (scores(0), init))
            _, carries = step(nfull, None, state, True)
        if softmax:
            outs = [acc / l for (m, l, acc) in carries]
            stats = [m + jnp.log(l) for (m, l, acc) in carries]
        else:
            outs, stats = [acc for (run, acc) in carries], [run for (run, acc) in carries]
        hm0 = (lane // HEAD) == 0
        pick = lambda a: jnp.where(hm0, a[0], a[1])
        if wide:
            for hh in range(nhb):
                o_ref[:, cols[hh]] = outs[hh]
        else:
            o_ref[...] = pick(outs)
        if has_stat:
            st_ref[0] = pick(stats)

    in_specs = [q_spec, k_spec, v_spec]
    args = [qa, ka, va]
    if mode == "fox":
        in_specs += [pl.BlockSpec((QB, LANES), lambda p, i: (i, 0)), pl.BlockSpec((8, s), lambda p, i: (0, 0))]
        args += [cum_col, cum_row]
    bw = 2 * LANES if mode == "mla" else LANES
    out_specs = [pl.BlockSpec((QB, bw), lambda p, i: (i, p))]
    out_shape = [jax.ShapeDtypeStruct((s, nblk * bw), F32)]
    out_specs.append(pl.BlockSpec((1, QB, LANES), lambda p, i: (p, i, 0)))
    out_shape.append(jax.ShapeDtypeStruct((nblk, s, LANES), F32))
    side_in, side_out, side_scratch = _side_specs(side)
    res = pl.pallas_call(
        _carry_side_job(body, len(args), len(out_shape), side, (nblk, nq)), name=mode + "_fwd", grid=(nblk, nq),
        in_specs=in_specs + side_in, out_specs=out_specs + side_out,
        out_shape=out_shape + ([] if side is None else side.out_shape), scratch_shapes=side_scratch,
        compiler_params=_cparams(("parallel", "parallel") if side is None else ("arbitrary", "arbitrary")),
    )(*args, *([] if side is None else side.inputs))
    return (res[0], res[1]) if side is None else (res[0], res[1], res[2:])


def _mixer_bwd(mode, qa, q_off, ka, k_off, va, v_off, o, do, *, stat=None, cum_col=None, cum_row=None, side=None):
    s = qa.shape[0]
    nq = s // QB
    nhb, nblk, nsub, q_spec, k_spec, v_spec = _mixer_specs(mode, s, q_off, k_off, v_off)
    w = TQ * nsub
    softmax = mode in ("fox", "mla")
    has_stat = mode != "ret"

    def body(*refs):
        refs = list(refs)
        q_ref, k_ref, v_ref, o_ref, do_ref = refs[:5]
        refs = refs[5:]
        if has_stat:
            st_ref = refs[0]
            refs = refs[1:]
        if mode == "fox":
            cc_ref, cr_ref = refs[:2]
            refs = refs[2:]
        dq_ref, dk_ref, dv_ref = refs[:3]
        dck_ref, drs_ref = refs[3:5] if mode == "fox" else (None, None)
        p = pl.program_id(0)
        i = pl.program_id(1)

        @pl.when(i == 0)
        def _():
            dk_ref[...] = jnp.zeros_like(dk_ref)
            dv_ref[...] = jnp.zeros_like(dv_ref)
            if mode == "fox":
                dck_ref[...] = jnp.zeros_like(dck_ref)

        nfull, dist, visible = _mixer_geometry(mode, i, nsub)
        lane = _lane((1, LANES))
        heads = [nhb * p + hh for hh in range(nhb)]
        dov = do_ref[...]
        wide = mode == "mla"
        q_scale = _QK_SCALE if mode in ("fox", "sb") else 1.0
        cols = [slice(hh * LANES, (hh + 1) * LANES) if wide else slice(None) for hh in range(nhb)]
        if wide:
            prod = dov * o_ref[...]
            qs = [q_ref[:, cols[hh]] for hh in range(nhb)]
            dos = [dov[:, cols[hh]].astype(BF16) for hh in range(nhb)]
            deltas = [jnp.sum(prod[:, cols[hh]], axis=-1, keepdims=True) for hh in range(nhb)]
        else:
            qf = q_ref[...].astype(F32) * q_scale
            prod = dov * o_ref[...]
            hms = [(lane // HEAD) == hh for hh in range(nhb)]
            qs = [jnp.where(hm, qf, 0.0).astype(BF16) for hm in hms]
            dos = [jnp.where(hm, dov, 0.0).astype(BF16) for hm in hms]
            deltas = [jnp.sum(jnp.where(hm, prod, 0.0), axis=-1, keepdims=True) for hm in hms]
        if has_stat:
            st = st_ref[0]
            stats = [st[:, hh * HEAD:hh * HEAD + 1] for hh in range(nhb)]
        if mode == "fox":
            cqs = [_head_pick(cc_ref[...], h, 1) for h in heads]
        if mode == "sb":
            r1 = lax.broadcasted_iota(jnp.int32, (TQ, TQ), 0)
            c1 = lax.broadcasted_iota(jnp.int32, (TQ, TQ), 1)
            u_upto = (r1 <= c1).astype(BF16)
            u_before = (r1 < c1).astype(BF16)

        def chunk(c):
            return pl.ds(pl.multiple_of(c * w, w), w)

        def scores(c):
            js = chunk(c)
            return tuple((_dot_nt(qs[hh], k_ref[js, cols[hh]]), _dot_nt(dos[hh], v_ref[js, cols[hh]])) for hh in range(nhb))

        def emit(hh, js, ds_b, pr_b, dq):
            dk_ref[js, cols[hh]] += _dot_tn(ds_b, qs[hh])
            dv_ref[js, cols[hh]] += _dot_tn(pr_b, dos[hh])
            return dq + _dot(ds_b, k_ref[js, cols[hh]])

        def head_step(hh, c, js, sc_dp, carry, last):
            sc, dp = sc_dp
            if softmax:
                dq, rsum = carry
                if mode == "fox":
                    ck = _head_pick(cr_ref[:, js], heads[hh], 0)
                    sc = sc + (cqs[hh] - ck)
                else:
                    sc = sc * _MLA_SCALE
                if last:
                    sc = jnp.where(visible(c), sc, NEG)
                pr = jnp.exp(sc - stats[hh])
                ds = pr * (dp - deltas[hh])
                if mode == "fox":
                    dck_ref[0, hh:hh + 1, js] += jnp.sum(ds, axis=0, keepdims=True)
                    rsum = rsum + jnp.sum(ds, axis=-1, keepdims=True)
                if mode == "mla":
                    ds = ds * _MLA_SCALE
                return emit(hh, js, ds.astype(BF16), pr.astype(BF16), dq), rsum
            seen, gsum, dq = carry
            z = sc
            log_beta = jnp.minimum(z, 0.0) - jnp.log(1.0 + jnp.exp(-jnp.abs(z)))
            log_stay = log_beta - z
            if last:
                vis = visible(c)
                log_stay = jnp.where(vis, log_stay, 0.0)
            parts = []
            for b in range(nsub):
                ls_b = log_stay[:, b * TQ:(b + 1) * TQ]
                parts.append((stats[hh] - seen) - _dot2(ls_b, u_upto))
                seen = seen + jnp.sum(ls_b, axis=-1, keepdims=True)
            later = parts[0] if nsub == 1 else jnp.concatenate(parts, axis=1)
            wgt = jnp.exp(log_beta + later)
            if last:
                wgt = jnp.where(vis, wgt, 0.0)
            g = dp * wgt
            parts = []
            for b in range(nsub):
                g_b = g[:, b * TQ:(b + 1) * TQ]
                parts.append(gsum + _dot2(g_b, u_before))
                gsum = gsum + jnp.sum(g_b, axis=-1, keepdims=True)
            before = parts[0] if nsub == 1 else jnp.concatenate(parts, axis=1)
            beta = jnp.exp(log_beta)
            dz = g * (1.0 - beta) - beta * before
            if last:
                dz = jnp.where(vis, dz, 0.0)
            return seen, gsum, emit(hh, js, dz.astype(BF16), wgt.astype(BF16), dq)

        def step(c, c_next, state, last):
            scs, carries = state
            nxt = scores(c_next) if c_next is not None else None
            js = chunk(c)
            return nxt, tuple(head_step(hh, c, js, scs[hh], carries[hh], last) for hh in range(nhb))

        zero_acc = jnp.zeros((QB, LANES), F32)
        zero1 = jnp.zeros((QB, 1), F32)
        if softmax:
            init = tuple((zero_acc, zero1) for _ in range(nhb))
        else:
            init = tuple((zero1, zero1, zero_acc) for _ in range(nhb))
        state = lax.fori_loop(0, nfull, lambda c, st: step(c, c + 1, st, False), (scores(0), init))
        _, carries = step(nfull, None, state, True)
        if softmax:
            dqs = [dq for (dq, rsum) in carries]
        else:
            dqs = [dq for (seen, gsum, dq) in carries]
        hm0 = (lane // HEAD) == 0
        if wide:
            for hh in range(nhb):
                dq_ref[:, cols[hh]] = dqs[hh]
        else:
            dq_ref[...] = jnp.where(hm0, dqs[0], dqs[1]) * q_scale
        if mode == "fox":
            drs_ref[0] = jnp.where(hm0, carries[0][1], carries[1][1])

    bw = 2 * LANES if mode == "mla" else LANES
    pair_blk = pl.BlockSpec((QB, bw), lambda p, i: (i, p))
    full_blk = pl.BlockSpec((s, bw), lambda p, i: (0, p))
    stat_blk = pl.BlockSpec((1, QB, LANES), lambda p, i: (p, i, 0))
    in_specs = [q_spec, k_spec, v_spec, pair_blk, pair_blk]
    args = [qa, ka, va, o, do]
    if has_stat:
        in_specs.append(stat_blk)
        args.append(stat)
    if mode == "fox":
        in_specs += [pl.BlockSpec((QB, LANES), lambda p, i: (i, 0)), pl.BlockSpec((8, s), lambda p, i: (0, 0))]
        args += [cum_col, cum_row]
    out_specs = [pair_blk, full_blk, full_blk]
    out_shape = [jax.ShapeDtypeStruct((s, nblk * bw), F32)] * 3
    if mode == "fox":
        out_specs += [pl.BlockSpec((1, 8, s), lambda p, i: (p, 0, 0)), stat_blk]
        out_shape += [jax.ShapeDtypeStruct((2, 8, s), F32), jax.ShapeDtypeStruct((2, s, LANES), F32)]
    side_in, side_out, side_scratch = _side_specs(side)
    res = pl.pallas_call(
        _carry_side_job(body, len(args), len(out_shape), side, (nblk, nq)), name=mode + "_bwd", grid=(nblk, nq),
        in_specs=in_specs + side_in, out_specs=out_specs + side_out,
        out_shape=out_shape + ([] if side is None else side.out_shape), scratch_shapes=side_scratch,
        compiler_params=_cparams(("parallel", "arbitrary") if side is None else ("arbitrary", "arbitrary")),
    )(*args, *([] if side is None else side.inputs))
    return res if side is None else (*res[:len(out_shape)], res[len(out_shape):])


def _ret_geometry(p):
    lane = _lane((1, LANES))
    lg_lane = jnp.where(lane < HEAD, _log_gamma_of(2 * p), _log_gamma_of(2 * p + 1))
    a = lax.broadcasted_iota(jnp.int32, (TQ, 1), 0).astype(F32)
    row = lax.broadcasted_iota(jnp.int32, (TQ, TQ), 0)
    col = lax.broadcasted_iota(jnp.int32, (TQ, TQ), 1)
    same_chunk_or_earlier = (col // CHUNK) <= (row // CHUNK)
    gap = jnp.abs(row - col).astype(F32)
    decays = [jnp.where(same_chunk_or_earlier, jnp.exp(_log_gamma_of(2 * p + hh) * gap), 0.0) for hh in range(2)]
    r = lax.broadcasted_iota(jnp.int32, (LANES, LANES), 0)
    c = lax.broadcasted_iota(jnp.int32, (LANES, LANES), 1)
    own_head = (r // HEAD) == (c // HEAD)
    return lane, lg_lane, a, decays, own_head


def _ret_fwd(qa, ka, va, v_off):
    s = qa.shape[0]
    nq = s // TQ

    def body(q_ref, k_ref, v_ref, o_ref, st_ref, state):
        p = pl.program_id(0)

        @pl.when(pl.program_id(1) == 0)
        def _():
            state[...] = jnp.zeros_like(state)

        lane, lg_lane, a, decays, own_head = _ret_geometry(p)
        q = q_ref[...].astype(F32)
        k = k_ref[...]
        v = v_ref[...]
        s_in = state[...]
        st_ref[0, 0] = s_in
        out = _dot((q * jnp.exp(lg_lane * (a + 1.0))).astype(BF16), s_in.astype(BF16))
        for hh in range(2):
            hm = (lane // HEAD) == hh
            qh = jnp.where(hm, q, 0.0).astype(BF16)
            inner = _dot((_dot_nt(qh, k) * decays[hh]).astype(BF16), v)
            out = out + jnp.where(hm, inner, 0.0)
        o_ref[...] = out
        k_tail = (k.astype(F32) * jnp.exp(lg_lane * (TQ - 1.0 - a))).astype(BF16)
        state[...] = jnp.exp(lg_lane * float(TQ)) * s_in + jnp.where(own_head, _dot_tn(k_tail, v), 0.0)

    blk = lambda off: pl.BlockSpec((TQ, LANES), lambda p, i: (i, off + p))
    return pl.pallas_call(
        body, name="ret_fwd", grid=(2, nq), in_specs=[blk(0), blk(0), blk(v_off)],
        out_specs=[blk(0), pl.BlockSpec((1, 1, LANES, LANES), lambda p, i: (p, i, 0, 0))],
        out_shape=[jax.ShapeDtypeStruct((s, 2 * LANES), F32), jax.ShapeDtypeStruct((2, nq, LANES, LANES), F32)],
        scratch_shapes=[pltpu.VMEM((LANES, LANES), F32)],
        compiler_params=_cparams(("parallel", "arbitrary")),
    )(qa, ka, va)


def _ret_bwd(qa, ka, va, v_off, states, do):
    s = qa.shape[0]
    nq = s // TQ

    def body(q_ref, k_ref, v_ref, st_ref, do_ref, dq_ref, dk_ref, dv_ref, dstate):
        p = pl.program_id(0)

        @pl.when(pl.program_id(1) == 0)
        def _():
            dstate[...] = jnp.zeros_like(dstate)

        lane, lg_lane, a, decays, own_head = _ret_geometry(p)
        q = q_ref[...].astype(F32)
        k = k_ref[...]
        kf = k.astype(F32)
        v = v_ref[...]
        dov = do_ref[...]
        s_in = st_ref[0, 0].astype(BF16)
        ds_next = dstate[...]
        ds_b = ds_next.astype(BF16)
        head_decay = jnp.exp(lg_lane * (a + 1.0))
        tail_decay = jnp.exp(lg_lane * (TQ - 1.0 - a))
        k_tail = (kf * tail_decay).astype(BF16)
        dq = _dot_nt(dov.astype(BF16), s_in) * head_decay
        dk = _dot_nt(v, ds_b) * tail_decay
        dv = _dot(k_tail, ds_b)
        for hh in range(2):
            hm = (lane // HEAD) == hh
            qh = jnp.where(hm, q, 0.0).astype(BF16)
            doh = jnp.where(hm, dov, 0.0).astype(BF16)
            att = (_dot_nt(qh, k) * decays[hh]).astype(BF16)
            datt = (_dot_nt(doh, v) * decays[hh]).astype(BF16)
            dv = dv + _dot_tn(att, doh)
            dk = dk + _dot_tn(datt, qh)
            dq = dq + jnp.where(hm, _dot(datt, k), 0.0)
        dq_ref[...] = dq
        dk_ref[...] = dk
        dv_ref[...] = dv
        q_head = (q * head_decay).astype(BF16)
        dstate[...] = jnp.exp(lg_lane * float(TQ)) * ds_next + jnp.where(own_head, _dot_tn(q_head, dov.astype(BF16)), 0.0)

    blk = lambda off: pl.BlockSpec((TQ, LANES), lambda p, i: (nq - 1 - i, off + p))
    return pl.pallas_call(
        body, name="ret_bwd", grid=(2, nq),
        in_specs=[blk(0), blk(0), blk(v_off), pl.BlockSpec((1, 1, LANES, LANES), lambda p, i: (p, nq - 1 - i, 0, 0)), blk(0)],
        out_specs=[blk(0)] * 3, out_shape=[jax.ShapeDtypeStruct((s, 2 * LANES), F32)] * 3,
        scratch_shapes=[pltpu.VMEM((LANES, LANES), F32)],
        compiler_params=_cparams(("parallel", "arbitrary")),
    )(qa, ka, va, states, do)


def _seg_mean_matrix():
    r = lax.broadcasted_iota(jnp.int32, (GROUP, GROUP), 0)
    c = lax.broadcasted_iota(jnp.int32, (GROUP, GROUP), 1)
    return jnp.where((r // HEAD) == (c // HEAD), 1.0 / HEAD, 0.0).astype(F32)


def _sigmoid(x):
    return 1.0 / (1.0 + jnp.exp(-x))


def _mix_post(oa, ob, oc, od, proj, g):
    s = oa.shape[0]
    tr = _tile(s, 256)

    def body(a_ref, b_ref, c_ref, d_ref, rg_ref, g_ref, o_ref):
        gv = g_ref[...]
        o_ref[:, 0:GROUP] = _rms(a_ref[...], gv[:, 0:GROUP]).astype(BF16)
        o_ref[:, GROUP:2 * GROUP] = _rms(b_ref[...], gv[:, GROUP:2 * GROUP]).astype(BF16)
        seg = _seg_mean_matrix()
        c = c_ref[...]
        cen = c - _dot_exact(c, seg)
        n = cen * lax.rsqrt(_dot_exact(cen * cen, seg) + EPS)
        rg = rg_ref[...]
        o_ref[:, 2 * GROUP:3 * GROUP] = (n * gv[:, 2 * GROUP:3 * GROUP] * (rg * _sigmoid(rg))).astype(BF16)
        o_ref[:, 3 * GROUP:] = _rms(d_ref[...], gv[:, 3 * GROUP:]).astype(BF16)

    blk = pl.BlockSpec((tr, GROUP), lambda i: (i, 0))
    return pl.pallas_call(
        body, name="mix_post", grid=(s // tr,),
        in_specs=[blk] * 4 + [pl.BlockSpec((tr, GROUP), lambda i: (i, OFF_RG // 2)), pl.BlockSpec((1, D_MODEL), lambda i: (0, 0))],
        out_specs=pl.BlockSpec((tr, D_MODEL), lambda i: (i, 0)), out_shape=jax.ShapeDtypeStruct((s, D_MODEL), BF16),
        compiler_params=_cparams(("parallel",)),
    )(oa, ob, oc, od, proj, g.reshape(1, D_MODEL))


def _mix_post_bwd(dmixed, oa, ob, oc, od, proj, g):
    s = oa.shape[0]
    tr = _tile(s, 256)

    def body(dm_ref, a_ref, b_ref, c_ref, d_ref, rg_ref, g_ref, da_ref, db_ref, dc_ref, dd_ref, drg_ref, dg_ref):
        @pl.when(pl.program_id(0) == 0)
        def _():
            dg_ref[...] = jnp.zeros_like(dg_ref)

        gv = g_ref[...]
        dm = dm_ref[...]
        for k, (x_ref, dx_ref) in enumerate(((a_ref, da_ref), (b_ref, db_ref), (None, None), (d_ref, dd_ref))):
            if x_ref is None:
                continue
            cols = slice(k * GROUP, (k + 1) * GROUP)
            dx, gterm = _rms_bwd(x_ref[...], gv[:, cols], dm[:, cols])
            dx_ref[...] = dx
            dg_ref[:, cols] += jnp.sum(gterm, axis=0, keepdims=True)
        cols = slice(2 * GROUP, 3 * GROUP)
        seg = _seg_mean_matrix()
        c = c_ref[...]
        cen = c - _dot_exact(c, seg)
        rstd = lax.rsqrt(_dot_exact(cen * cen, seg) + EPS)
        n = cen * rstd
        rg = rg_ref[...]
        sg = _sigmoid(rg)
        gate = rg * sg
        dy = dm[:, cols]
        gc = gv[:, cols]
        dn = dy * gc * gate
        dg_ref[:, cols] += jnp.sum(dy * n * gate, axis=0, keepdims=True)
        drg_ref[...] = (dy * n * gc * (sg * (1.0 + rg * (1.0 - sg)))).astype(BF16)
        dc_ref[...] = rstd * (dn - _dot_exact(dn, seg) - n * _dot_exact(dn * n, seg))

    blk = pl.BlockSpec((tr, GROUP), lambda i: (i, 0))
    gsp = pl.BlockSpec((1, D_MODEL), lambda i: (0, 0))
    return pl.pallas_call(
        body, name="mix_post_bwd", grid=(s // tr,),
        in_specs=[pl.BlockSpec((tr, D_MODEL), lambda i: (i, 0))] + [blk] * 4 + [pl.BlockSpec((tr, GROUP), lambda i: (i, OFF_RG // 2)), gsp],
        out_specs=[blk] * 5 + [gsp],
        out_shape=[jax.ShapeDtypeStruct((s, GROUP), F32)] * 4 + [jax.ShapeDtypeStruct((s, GROUP), BF16), jax.ShapeDtypeStruct((1, D_MODEL), F32)],
        compiler_params=_cparams(("arbitrary",)),
    )(dmixed, oa, ob, oc, od, proj, g.reshape(1, D_MODEL))


def _pack_w_in(w):
    z = lambda n: jnp.zeros((w.shape[0], n), w.dtype)
    misc = jnp.concatenate([w[:, 768:772], z(KR_LANE - N_HEADS), w[:, 1156:1188], z(LANES - KR_LANE - ROPE_DIM)], axis=1)
    return jnp.concatenate([w[:, 0:768], w[:, 772:1028], w[:, 1188:2980], w[:, 1028:1156], misc], axis=1)


def _unpack_dw_in(d):
    m = OFF_MISC * LANES
    return jnp.concatenate([d[:, 0:768], d[:, m:m + N_HEADS], d[:, 768:1024], d[:, OFF_CKV * LANES:m],
                            d[:, m + KR_LANE:m + KR_LANE + ROPE_DIM], d[:, 1024:OFF_CKV * LANES]], axis=1)


def _pack_w_q(w):
    return jnp.pad(w.reshape(Q_RANK, N_HEADS, HEAD + ROPE_DIM), ((0, 0), (0, 0), (0, LANES - HEAD - ROPE_DIM))).reshape(Q_RANK, 4 * LANES)


def _unpack_dw_q(d):
    return d.reshape(Q_RANK, N_HEADS, LANES)[:, :, :HEAD + ROPE_DIM].reshape(Q_RANK, N_HEADS * (HEAD + ROPE_DIM))


def _pack_w_kv(w):
    w4 = w.reshape(KV_RANK, N_HEADS, 2 * HEAD)
    widen = lambda a: jnp.pad(a, ((0, 0), (0, 0), (0, LANES - HEAD))).reshape(KV_RANK, N_HEADS * LANES)
    return widen(w4[:, :, :HEAD]), widen(w4[:, :, HEAD:])


def _unpack_dw_kv(dk, dv):
    narrow = lambda a: a.reshape(KV_RANK, N_HEADS, LANES)[:, :, :HEAD]
    return jnp.concatenate([narrow(dk), narrow(dv)], axis=2).reshape(KV_RANK, 2 * N_HEADS * HEAD)


def _narrow_heads(a):
    return a.reshape(a.shape[0], N_HEADS, LANES)[:, :, :HEAD].reshape(a.shape[0], N_HEADS * HEAD)


def _widen_heads(a):
    return jnp.pad(a.reshape(a.shape[0], N_HEADS, HEAD), ((0, 0), (0, 0), (0, LANES - HEAD))).reshape(a.shape[0], N_HEADS * LANES)


def _layer_fwd(x, lw, tabs, tag, side=None):
    cos_m, sin_m, cos_r, sin_r = tabs
    h1 = _norm_fwd(x, lw["g_mix_pre"], name=tag + "pre_norm")
    proj, projb = _matmul(h1, lw["w_in"], name=tag + "in_proj", also_bf16=True)
    bias_row = jnp.pad(lw["b_forget"], (FF_LANE, LANES - N_HEADS - FF_LANE)).reshape(1, LANES)
    cum_col, cum_row = _fox_cum(proj, bias_row)
    oa, lse_a = _mixer_fwd("fox", projb, OFF_FQ, projb, OFF_FK, projb, OFF_FV, cum_col=cum_col, cum_row=cum_row)
    qm, km, vm, cqn, ckvn = _mla_prep(proj, cos_m, sin_m, lw["g_q_lora"], lw["g_kv_lora"], lw["wq"], lw["wk"], lw["wv"])
    ob_wide, lse_b = _mixer_fwd("mla", qm, 0, km, 0, vm, 0)
    ob = _narrow_heads(ob_wide)
    qr, kr = _ret_prep(proj, cos_r, sin_r)
    oc, ret_states = _ret_fwd(qr, kr, projb, OFF_RV)
    od, tot_d, *carried = _mixer_fwd("sb", projb, OFF_SQ, projb, OFF_SK, projb, OFF_SV, side=side)
    mixed = _mix_post(oa, ob, oc, od, proj, lw["g_mix_out"])
    mix = _matmul(mixed, lw["w_out"], name=tag + "out_proj")
    x1 = _norm_fwd(mix, lw["g_mix_post"], name=tag + "mix_post_norm", resid=x, out_dtype=F32)
    h2 = _norm_fwd(x1, lw["g_ffn_pre"], name=tag + "ffn_pre_norm")
    u = _matmul(h2, lw["w_ffn_up"], name=tag + "ffn_up", relu2=True, out_dtype=BF16)
    f = _matmul(u, lw["w_ffn_down"], name=tag + "ffn_down")
    x2 = _norm_fwd(f, lw["g_ffn_post"], name=tag + "ffn_post_norm", resid=x1, out_dtype=F32)
    saved = dict(x=x, h1=h1, proj=proj, projb=projb, bias_row=bias_row, cum_col=cum_col, cum_row=cum_row, oa=oa, lse_a=lse_a,
                 qm=qm, km=km, vm=vm, cqn=cqn, ckvn=ckvn, ob=ob, ob_wide=ob_wide, lse_b=lse_b, qr=qr, kr=kr, ret_states=ret_states, oc=oc, od=od, tot_d=tot_d, mixed=mixed,
                 mix=mix, x1=x1, h2=h2, u=u, f=f)
    return x2, saved, (carried[0] if carried else None)


def _layer_bwd(dx2, lw, sv, tabs, tag, side=None):
    cos_m, sin_m, cos_r, sin_r = tabs
    g = {}
    df, g["g_ffn_post"] = _norm_bwd(sv["f"], lw["g_ffn_post"], dx2, name=tag + "ffn_post_norm_bwd", out_dtype=BF16)
    du_pre = _matmul(df, lw["w_ffn_down"], name=tag + "ffn_down_dx", tb=True, out_dtype=BF16, relu2_of=sv["u"])
    g["w_ffn_down"] = _matmul(sv["u"], df, name=tag + "ffn_down_dw", ta=True)
    dh2 = _matmul(du_pre, lw["w_ffn_up"], name=tag + "ffn_up_dx", tb=True)
    g["w_ffn_up"] = _matmul(sv["h2"], du_pre, name=tag + "ffn_up_dw", ta=True)
    dx1, g["g_ffn_pre"] = _norm_bwd(sv["x1"], lw["g_ffn_pre"], dh2, name=tag + "ffn_pre_norm_bwd", add=dx2)
    dmix, g["g_mix_post"] = _norm_bwd(sv["mix"], lw["g_mix_post"], dx1, name=tag + "mix_post_norm_bwd", out_dtype=BF16)
    dmixed = _matmul(dmix, lw["w_out"], name=tag + "out_proj_dx", tb=True)
    g["w_out"] = _matmul(sv["mixed"], dmix, name=tag + "out_proj_dw", ta=True)
    proj, projb = sv["proj"], sv["projb"]
    doa, dob, doc, dod, drg, g["g_mix_out"] = _mix_post_bwd(dmixed, sv["oa"], sv["ob"], sv["oc"], sv["od"], proj, lw["g_mix_out"])
    dfq, dfk, dfv, dck, drs = _mixer_bwd("fox", projb, OFF_FQ, projb, OFF_FK, projb, OFF_FV, sv["oa"], doa, stat=sv["lse_a"],
                                         cum_col=sv["cum_col"], cum_row=sv["cum_row"])
    dqm, dkm, dvm = _mixer_bwd("mla", sv["qm"], 0, sv["km"], 0, sv["vm"], 0, sv["ob_wide"], _widen_heads(dob), stat=sv["lse_b"])
    dcq, dckv, dkr, dwq, dwk, dwv, g["g_q_lora"], g["g_kv_lora"] = _mla_prep_bwd(
        dqm, dkm, dvm, proj, sv["cqn"], sv["ckvn"], cos_m, sin_m, lw["g_q_lora"], lw["g_kv_lora"], lw["wq"], lw["wk"], lw["wv"])
    dqr, dkr_ret, drv = _ret_bwd(sv["qr"], sv["kr"], projb, OFF_RV, sv["ret_states"], doc)
    drq, drk = _ret_prep_bwd(dqr, dkr_ret, cos_r, sin_r)
    dsq, dsk, dsv, *carried = _mixer_bwd("sb", projb, OFF_SQ, projb, OFF_SK, projb, OFF_SV, sv["od"], dod, stat=sv["tot_d"], side=side)
    dmisc, db_row = _fox_gate_bwd(dck, drs, proj, sv["bias_row"], dkr)
    b = lambda a: a.astype(BF16)
    dproj = jnp.concatenate([b(dfq), b(dfk), b(dfv), dcq, drq, drk, b(drv), drg, b(dsq), b(dsk), b(dsv), dckv, dmisc], axis=1)
    dh1 = _matmul(dproj, lw["w_in"], name=tag + "in_proj_dx", tb=True)
    g["w_in"] = _matmul(sv["h1"], dproj, name=tag + "in_proj_dw", ta=True)
    dx, g["g_mix_pre"] = _norm_bwd(sv["x"], lw["g_mix_pre"], dh1, name=tag + "pre_norm_bwd", add=dx1)
    g["b_forget"] = db_row[0, FF_LANE:FF_LANE + N_HEADS]
    g["wq"], g["wk"], g["wv"] = dwq, dwk, dwv
    return dx, g, (carried[0] if carried else None)


def _local_step(x, positions, layers, target):
    s = x.shape[0]
    tabs = _rope_tables(positions.reshape(s, 1))
    saved = []
    for li, lw in enumerate(layers):
        x, sv, _ = _layer_fwd(x, lw, tabs, "l%d_" % li)
        saved.append(sv)
    loss_row, dx = _loss_head(x, target)
    grads = [None] * len(layers)
    for li in reversed(range(len(layers))):
        dx, grads[li], _ = _layer_bwd(dx, layers[li], saved[li], tabs, "l%d_" % li)
    return loss_row[0, 0], dx, grads


def _adamw(w, g, m, v, *, name):
    r, c = w.shape
    tr = 256 if r % 256 == 0 else r
    blk = pl.BlockSpec((tr, c), lambda i: (i, 0))
    c1 = 1.0 - ADAM_B1 ** ADAM_STEP
    c2 = 1.0 - ADAM_B2 ** ADAM_STEP

    def body(w_ref, g_ref, m_ref, v_ref, d_ref, mo_ref, vo_ref):
        gv = g_ref[...]
        mn = ADAM_B1 * m_ref[...] + (1.0 - ADAM_B1) * gv
        vn = ADAM_B2 * v_ref[...] + (1.0 - ADAM_B2) * jnp.square(gv)
        mo_ref[...] = mn
        vo_ref[...] = vn
        d_ref[...] = -ADAM_LR * ((mn / c1) / (jnp.sqrt(vn / c2) + ADAM_EPS) + ADAM_WD * w_ref[...])

    return pl.pallas_call(
        body, name=name, grid=(r // tr,), in_specs=[blk] * 4, out_specs=[blk] * 3,
        out_shape=[jax.ShapeDtypeStruct((r, c), F32)] * 3, compiler_params=_cparams(("parallel",)),
    )(w, g, m, v)


BIG = ("w_in", "w_q_up", "w_kv_up", "w_out", "w_ffn_up", "w_ffn_down")
SMALL = ("g_mix_pre", "b_forget", "g_q_lora", "g_kv_lora", "g_mix_out", "g_mix_post", "g_ffn_pre", "g_ffn_post")
N_CHIPS = 4
ANY = pl.BlockSpec(memory_space=pl.ANY)


def _mesh_pos():
    return lax.axis_index("x"), lax.axis_index("y"), lax.axis_index("c")


def _other_chips(x, y):
    return [(1 - x, y), (x, 1 - y), (1 - x, 1 - y)]


def _rows_half(ref, half):
    h = ref.shape[-2] // 2
    return ref.at[(slice(None),) * (len(ref.shape) - 2) + (pl.ds(half * h, h), slice(None))]


def _remote(src, dst, send_sem, recv_sem, device):
    return pltpu.make_async_remote_copy(src_ref=src, dst_ref=dst, send_sem=send_sem, recv_sem=recv_sem, device_id=device,
                                        device_id_type=MESH)


def _comm_call(body, name, args, out_shape, n_sems):
    return pl.pallas_call(
        body, name=name, in_specs=[ANY] * len(args), out_specs=[ANY] * len(out_shape), out_shape=out_shape,
        scratch_shapes=[pltpu.SemaphoreType.DMA((n_sems,)), pltpu.SemaphoreType.DMA((n_sems,))],
        compiler_params=pltpu.CompilerParams(has_side_effects=True),
    )(*args)


def _run_side_job(side, name):
    si = len(side.inputs)

    def body(*refs):
        args = (refs[:si], refs[si:-2], refs[-2], refs[-1])
        sends = side.sends(*args)
        for cp in sends:
            cp.start()
        for cp in side.recvs(*args):
            cp.wait_recv()
        for cp in sends:
            cp.wait_send()

    return _comm_call(body, name, side.inputs, side.out_shape, side.n_sems)


def _gather_job(shards):
    n = len(shards)

    def copies(own_block, ins, outs, send_sems, recv_sems):
        x, y, c = _mesh_pos()
        return [_remote(_rows_half(ins[t], c), _rows_half(outs[t].at[2 * x + y if own_block else 2 * px + py], c),
                        send_sems.at[3 * t + j], recv_sems.at[3 * t + j], (px, py, c))
                for t in range(n) for j, (px, py) in enumerate(_other_chips(x, y))]

    return _SideJob(shards, [jax.ShapeDtypeStruct((N_CHIPS,) + a.shape, a.dtype) for a in shards], 3 * n,
                    functools.partial(copies, True), functools.partial(copies, False))


def _forward_halves(gathered):
    n = len(gathered)

    def body(*refs):
        bufs, send_sems, recv_sems = refs[n:2 * n], refs[-2], refs[-1]
        x, y, c = _mesh_pos()

        def d2d(t, j, block, half):
            region = _rows_half(bufs[t].at[block], half)
            return _remote(region, region, send_sems.at[3 * t + j], recv_sems.at[3 * t + j], (x, y, 1 - c))

        peers = list(enumerate(_other_chips(x, y)))
        sends = [d2d(t, j, 2 * px + py, c) for t in range(n) for j, (px, py) in peers]
        for cp in sends:
            cp.start()
        for t in range(n):
            for j, (px, py) in peers:
                d2d(t, j, 2 * px + py, 1 - c).wait_recv()
        for cp in sends:
            cp.wait_send()

    return pl.pallas_call(
        body, name="gather_forward", in_specs=[ANY] * n, out_specs=[ANY] * n,
        out_shape=[jax.ShapeDtypeStruct(g.shape, g.dtype) for g in gathered], input_output_aliases={t: t for t in range(n)},
        scratch_shapes=[pltpu.SemaphoreType.DMA((3 * n,)), pltpu.SemaphoreType.DMA((3 * n,))],
        compiler_params=pltpu.CompilerParams(has_side_effects=True),
    )(*gathered)


def _exchange_halves(gs):
    n = len(gs)

    def body(*refs):
        ins, outs, send_sems, recv_sems = refs[:n], refs[n:2 * n], refs[2 * n], refs[2 * n + 1]
        x, y, c = _mesh_pos()
        cps = [_remote(_rows_half(ins[t], 1 - c), outs[t], send_sems.at[t], recv_sems.at[t], (x, y, 1 - c)) for t in range(n)]
        for cp in cps:
            cp.start()
        for cp in cps:
            cp.wait_recv()
        for cp in cps:
            cp.wait_send()

    out_shape = [jax.ShapeDtypeStruct(g.shape[:2] + (g.shape[2] // 2, g.shape[3]), g.dtype) for g in gs]
    return _comm_call(body, "grad_pair_exchange", gs, out_shape, n)


def _pair_add(g, r, c_idx, *, name):
    nb, d, rows, cols = g.shape
    h = rows // 2
    tr = min(h, 512)
    nt = h // tr

    def body(c_ref, g_ref, r_ref, p_ref, pb_ref):
        s = g_ref[...] + r_ref[...]
        p_ref[...] = s
        pb_ref[...] = s.astype(BF16)

    blk = pl.BlockSpec((1, 1, tr, cols), lambda k, l, i, c_ref: (k, l, i, 0))
    return pl.pallas_call(
        body, name=name,
        grid_spec=pltpu.PrefetchScalarGridSpec(
            num_scalar_prefetch=1, grid=(nb, d, nt),
            in_specs=[pl.BlockSpec((1, 1, tr, cols), lambda k, l, i, c_ref: (k, l, c_ref[0] * nt + i, 0)), blk],
            out_specs=[blk, blk]),
        out_shape=[jax.ShapeDtypeStruct((nb, d, h, cols), F32), jax.ShapeDtypeStruct((nb, d, h, cols), BF16)],
        compiler_params=_cparams(("parallel", "parallel", "parallel")),
    )(c_idx, g, r)


def _exchange_chips_job(pbs):
    n = len(pbs)

    def copies(ins, outs, send_sems, recv_sems):
        x, y, c = _mesh_pos()
        return [_remote(ins[t].at[2 * px + py], outs[t].at[j], send_sems.at[3 * t + j], recv_sems.at[3 * t + j], (px, py, c))
                for t in range(n) for j, (px, py) in enumerate(_other_chips(x, y))]

    return _SideJob(pbs, [jax.ShapeDtypeStruct((3,) + p.shape[1:], p.dtype) for p in pbs], 3 * n, copies, copies)


def _chip_add(p, r, k_idx, *, name):
    _, d, h, cols = p.shape
    tr = min(h, 512)
    nt = h // tr

    def body(k_ref, p_ref, r_ref, o_ref):
        o_ref[0] = ((p_ref[0, 0] + r_ref[0, 0].astype(F32)) + r_ref[1, 0].astype(F32)) + r_ref[2, 0].astype(F32)

    return pl.pallas_call(
        body, name=name,
        grid_spec=pltpu.PrefetchScalarGridSpec(
            num_scalar_prefetch=1, grid=(d, nt),
            in_specs=[pl.BlockSpec((1, 1, tr, cols), lambda l, i, k_ref: (k_ref[0], l, i, 0)),
                      pl.BlockSpec((3, 1, tr, cols), lambda l, i, k_ref: (0, l, i, 0))],
            out_specs=pl.BlockSpec((1, tr, cols), lambda l, i, k_ref: (l, i, 0))),
        out_shape=jax.ShapeDtypeStruct((d, h, cols), F32), compiler_params=_cparams(("parallel", "parallel")),
    )(k_idx, p, r)


def _share_halves(qs):
    n = len(qs)

    def body(*refs):
        ins, outs, send_sems, recv_sems = refs[:n], refs[n:2 * n], refs[2 * n], refs[2 * n + 1]
        x, y, c = _mesh_pos()
        cps = [_remote(ins[t], outs[t], send_sems.at[t], recv_sems.at[t], (x, y, 1 - c)) for t in range(n)]
        for cp in cps:
            cp.start()
        for cp in cps:
            cp.wait_recv()
        for cp in cps:
            cp.wait_send()

    return _comm_call(body, "grad_pair_share", qs, [jax.ShapeDtypeStruct(q.shape, q.dtype) for q in qs], n)


def _all_reduce_small(v):
    r, cols = v.shape
    n_dev = 8

    def body(v_ref, o_ref, buf, send_sems, recv_sems):
        x, y, c = _mesh_pos()
        me = 4 * x + 2 * y + c
        buf[me] = v_ref[...]

        def peer(j):
            return (1 - x if j & 4 else x, 1 - y if j & 2 else y, 1 - c if j & 1 else c)

        def copy(j, slot):
            return pltpu.make_async_remote_copy(src_ref=v_ref, dst_ref=buf.at[slot], send_sem=send_sems.at[j - 1],
                                                recv_sem=recv_sems.at[j - 1], device_id=peer(j), device_id_type=MESH)

        sends = [copy(j, me) for j in range(1, n_dev)]
        for cp in sends:
            cp.start()
        for j in range(1, n_dev):
            px, py, pc = peer(j)
            copy(j, 4 * px + 2 * py + pc).wait_recv()
        for cp in sends:
            cp.wait_send()
        acc = buf[0]
        for d in range(1, n_dev):
            acc = acc + buf[d]
        o_ref[...] = acc

    vm = pl.BlockSpec(memory_space=pltpu.VMEM)
    return pl.pallas_call(
        body, name="small_all_reduce", in_specs=[vm], out_specs=vm, out_shape=jax.ShapeDtypeStruct((r, cols), F32),
        scratch_shapes=[pltpu.VMEM((n_dev, r, cols), F32), pltpu.SemaphoreType.DMA((n_dev - 1,)), pltpu.SemaphoreType.DMA((n_dev - 1,))],
        compiler_params=pltpu.CompilerParams(has_side_effects=True),
    )(v)


_COL_SHARDED = ("w_in", "w_q_up", "w_kv_up", "w_ffn_up")


def _shard_cols(blocks, a, b):
    c = blocks.shape[-1]
    out = []
    while a < b:
        k = a // c
        hi = min(b, (k + 1) * c)
        out.append(blocks[k][:, a - k * c:hi - k * c])
        a = hi
    return out


def _pack_w_in_shards(blocks):
    z = lambda n: [jnp.zeros((blocks.shape[1], n), blocks.dtype)]
    cols = lambda a, b: _shard_cols(blocks, a, b)
    return jnp.concatenate(cols(0, 768) + cols(772, 1028) + cols(1188, 2980) + cols(1028, 1156) + cols(768, 772)
                           + z(KR_LANE - N_HEADS) + cols(1156, 1188) + z(LANES - KR_LANE - ROPE_DIM), axis=1)


def _whole_layer(name, blocks):
    if name in _COL_SHARDED:
        return jnp.concatenate([blocks[k] for k in range(N_CHIPS)], axis=1)
    return blocks.reshape(N_CHIPS * blocks.shape[1], blocks.shape[2])


def _split_layer(name, whole):
    if name in _COL_SHARDED:
        c = whole.shape[1] // N_CHIPS
        return jnp.stack([whole[:, k * c:(k + 1) * c] for k in range(N_CHIPS)])
    return whole.reshape(N_CHIPS, whole.shape[0] // N_CHIPS, whole.shape[1])


def _small_to_rows(d):
    v = jnp.concatenate([d[k].astype(F32).reshape(-1) for k in SMALL])
    rows = -(-v.shape[0] // (8 * LANES)) * 8
    return jnp.pad(v, (0, rows * LANES - v.shape[0])).reshape(rows, LANES)


def _small_from_rows(rows, shapes):
    v = rows.reshape(-1)
    out, o = {}, 0
    for k in SMALL:
        sz = int(np.prod(shapes[k]))
        out[k] = v[o:o + sz].reshape(shapes[k])
        o += sz
    return out


_ARG_NAMES = ("x", "positions", "g_mix_pre", "w_in", "b_forget", "g_q_lora", "w_q_up", "g_kv_lora", "w_kv_up", "g_mix_out", "w_out",
              "g_mix_post", "g_ffn_pre", "w_ffn_up", "w_ffn_down", "g_ffn_post")
_WEIGHTS = _ARG_NAMES[2:]


def kernel(x, positions, g_mix_pre, w_in, b_forget, g_q_lora, w_q_up, g_kv_lora, w_kv_up, g_mix_out, w_out, g_mix_post, g_ffn_pre, w_ffn_up, w_ffn_down, g_ffn_post, loss_target, m_g_mix_pre, m_w_in, m_b_forget, m_g_q_lora, m_w_q_up, m_g_kv_lora, m_w_kv_up, m_g_mix_out, m_w_out, m_g_mix_post, m_g_ffn_pre, m_w_ffn_up, m_w_ffn_down, m_g_ffn_post, v_g_mix_pre, v_w_in, v_b_forget, v_g_q_lora, v_w_q_up, v_g_kv_lora, v_w_kv_up, v_g_mix_out, v_w_out, v_g_mix_post, v_g_ffn_pre, v_w_ffn_up, v_w_ffn_down, v_g_ffn_post):
    w = dict(g_mix_pre=g_mix_pre, w_in=w_in, b_forget=b_forget, g_q_lora=g_q_lora, w_q_up=w_q_up, g_kv_lora=g_kv_lora, w_kv_up=w_kv_up,
             g_mix_out=g_mix_out, w_out=w_out, g_mix_post=g_mix_post, g_ffn_pre=g_ffn_pre, w_ffn_up=w_ffn_up, w_ffn_down=w_ffn_down,
             g_ffn_post=g_ffn_post)
    m = dict(g_mix_pre=m_g_mix_pre, w_in=m_w_in, b_forget=m_b_forget, g_q_lora=m_g_q_lora, w_q_up=m_w_q_up, g_kv_lora=m_g_kv_lora,
             w_kv_up=m_w_kv_up, g_mix_out=m_g_mix_out, w_out=m_w_out, g_mix_post=m_g_mix_post, g_ffn_pre=m_g_ffn_pre,
             w_ffn_up=m_w_ffn_up, w_ffn_down=m_w_ffn_down, g_ffn_post=m_g_ffn_post)
    v = dict(g_mix_pre=v_g_mix_pre, w_in=v_w_in, b_forget=v_b_forget, g_q_lora=v_g_q_lora, w_q_up=v_w_q_up, g_kv_lora=v_g_kv_lora,
             w_kv_up=v_w_kv_up, g_mix_out=v_g_mix_out, w_out=v_w_out, g_mix_post=v_g_mix_post, g_ffn_pre=v_g_ffn_pre,
             w_ffn_up=v_w_ffn_up, w_ffn_down=v_w_ffn_down, g_ffn_post=v_g_ffn_post)
    shard_shapes = {k: w[k].shape for k in BIG}
    small_shapes = {k: w[k].shape for k in SMALL}
    c_idx = lax.axis_index("c").astype(jnp.int32).reshape(1)
    k_idx = (2 * lax.axis_index("x") + lax.axis_index("y")).astype(jnp.int32).reshape(1)

    mine = 2 * lax.axis_index("x") + lax.axis_index("y")
    first_core = lax.axis_index("c") == 0
    shards_b = [[w[k][l:l + 1].astype(BF16) for k in BIG] for l in range(DEPTH)]

    def layer_weights(l, gathered):
        four = {k: lax.dynamic_update_slice(g, s[None], (mine, 0, 0, 0))[:, 0]
                for k, g, s in zip(BIG, _forward_halves(gathered), shards_b[l])}
        wk, wv = _pack_w_kv(_whole_layer("w_kv_up", four["w_kv_up"]))
        return dict(
            g_mix_pre=g_mix_pre[l], w_in=_pack_w_in_shards(four["w_in"]), b_forget=b_forget[l], g_q_lora=g_q_lora[l],
            g_kv_lora=g_kv_lora[l], wq=_pack_w_q(_whole_layer("w_q_up", four["w_q_up"])), wk=wk, wv=wv, g_mix_out=g_mix_out[l],
            w_out=_whole_layer("w_out", four["w_out"]), g_mix_post=g_mix_post[l], g_ffn_pre=g_ffn_pre[l],
            w_ffn_up=_whole_layer("w_ffn_up", four["w_ffn_up"]), w_ffn_down=_whole_layer("w_ffn_down", four["w_ffn_down"]),
            g_ffn_post=g_ffn_post[l])

    def grad_blocks(g):
        whole = dict(w_in=_unpack_dw_in(g["w_in"]), w_q_up=_unpack_dw_q(g["wq"]), w_kv_up=_unpack_dw_kv(g["wk"], g["wv"]),
                     w_out=g["w_out"], w_ffn_up=g["w_ffn_up"], w_ffn_down=g["w_ffn_down"])
        blocks = [_split_layer(k, whole[k])[:, None] for k in BIG]
        theirs = _exchange_halves(blocks)
        return [_pair_add(b, r, c_idx, name="grad_pair_add_" + k) for k, b, r in zip(BIG, blocks, theirs)]

    def finish_grads(pair, partial):
        half = [_chip_add(p, r, k_idx, name="grad_chip_add_" + k) for k, (p, _), r in zip(BIG, pair, partial)]
        return [jnp.where(first_core, jnp.concatenate([q, s], axis=1), jnp.concatenate([s, q], axis=1))
                for q, s in zip(half, _share_halves(half))]

    seq = x.shape[1]
    tabs = _rope_tables(positions[0].reshape(seq, 1))
    lw0 = layer_weights(0, _run_side_job(_gather_job(shards_b[0]), "gather_weights_l0"))
    x1, saved0, gathered1 = _layer_fwd(x[0], lw0, tabs, "l0_", side=_gather_job(shards_b[1]))
    lw1 = layer_weights(1, gathered1)
    x2, saved1, _ = _layer_fwd(x1, lw1, tabs, "l1_")
    loss_row, dx = _loss_head(x2, loss_target[0])
    loss = lax.psum(loss_row[0, 0], ("x", "y", "c"))
    dx, grads1, _ = _layer_bwd(dx, lw1, saved1, tabs, "l1_")
    pair1 = grad_blocks(grads1)
    dx, grads0, partial1 = _layer_bwd(dx, lw0, saved0, tabs, "l0_", side=_exchange_chips_job([pb for (_, pb) in pair1]))
    big1 = finish_grads(pair1, partial1)
    pair0 = grad_blocks(grads0)
    big0 = finish_grads(pair0, _run_side_job(_exchange_chips_job([pb for (_, pb) in pair0]), "grad_chip_exchange_l0"))
    g_big = {k: jnp.concatenate([a, b], axis=0) for k, a, b in zip(BIG, big0, big1)}
    grads = [grads0, grads1]

    g_small_local = {k: jnp.stack([grads[l][k].reshape(small_shapes[k][1:]) for l in range(DEPTH)]) for k in SMALL}
    g_small = _small_from_rows(_all_reduce_small(_small_to_rows(g_small_local)), small_shapes)

    g_all = {**g_big, **g_small}
    delta, new_m, new_v = {}, {}, {}
    for k in BIG:
        d, r, c = shard_shapes[k]
        two_d = lambda a: a.reshape(d * r, c)
        dk, mk, vk = _adamw(two_d(w[k]), two_d(g_all[k]), two_d(m[k]), two_d(v[k]), name="adamw_" + k)
        delta[k], new_m[k], new_v[k] = dk.reshape(d, r, c), mk.reshape(d, r, c), vk.reshape(d, r, c)
    ds, ms, vs = _adamw(_small_to_rows(w), _small_to_rows(g_small), _small_to_rows(m), _small_to_rows(v), name="adamw_small")
    delta.update(_small_from_rows(ds, small_shapes))
    new_m.update(_small_from_rows(ms, small_shapes))
    new_v.update(_small_from_rows(vs, small_shapes))

    grad_x = dx.reshape(x.shape)
    return (loss, grad_x, *[g_all[k] for k in _WEIGHTS], *[delta[k] for k in _WEIGHTS], *[new_m[k] for k in _WEIGHTS],
            *[new_v[k] for k in _WEIGHTS])
```

```python
import functools
import math

import numpy as np
import jax
import jax.numpy as jnp
from jax import lax
from jax.experimental import pallas as pl
from jax.experimental.pallas import tpu as pltpu

F32 = jnp.float32
BF16 = jnp.bfloat16
MESH = pl.DeviceIdType.MESH

D_MODEL = 1024
DEPTH = 2
CHUNK = 64
GROUP = 256
HEAD = 64
N_HEADS = 4
Q_RANK = 256
KV_RANK = 128
ROPE_DIM = 32
D_FF = 4096
D_IN = 2980
D_INP = 3072
ROPE_BASE = 10000.0
EPS = 1e-6
LANES = 128
TQ = 128
NEG = -1e30

ADAM_LR, ADAM_B1, ADAM_B2, ADAM_EPS, ADAM_WD, ADAM_STEP = 0.001, 0.9, 0.999, 1e-08, 0.01, 10

OFF_FQ, OFF_FK, OFF_FV, OFF_CQ = 0, 2, 4, 6
OFF_RQ, OFF_RK, OFF_RV, OFF_RG = 8, 10, 12, 14
OFF_SQ, OFF_SK, OFF_SV = 16, 18, 20
OFF_CKV, OFF_MISC = 22, 23
FF_LANE, KR_LANE = 0, 64

VMEM_LIMIT = 56 * 1024 * 1024


def _tile(dim, pref):
    return pref if dim % pref == 0 else dim


def _cparams(sem, vmem=None):
    return pltpu.CompilerParams(dimension_semantics=sem, vmem_limit_bytes=vmem or VMEM_LIMIT)


def _dot(a, b):
    return jnp.dot(a, b, preferred_element_type=F32)


def _dot_nt(a, b):
    return lax.dot_general(a, b, (((1,), (1,)), ((), ())), preferred_element_type=F32)


def _dot_tn(a, b):
    return lax.dot_general(a, b, (((0,), (0,)), ((), ())), preferred_element_type=F32)


def _dot_exact(a, b):
    return jnp.dot(a, b, precision=lax.Precision.HIGHEST, preferred_element_type=F32)


def _matmul(a, b, *, name, ta=False, tb=False, out_dtype=F32, tm=1024, tn=1024, tk=1024,
            relu2=False, relu2_of=None, also_bf16=False):
    if ta:
        kdim, m = a.shape
    else:
        m, kdim = a.shape
    n = b.shape[0] if tb else b.shape[1]
    tm, tn, tk = _tile(m, tm), _tile(n, tn), _tile(kdim, tk)
    nk = kdim // tk
    a_spec = pl.BlockSpec((tk, tm), lambda i, j, k: (k, i)) if ta else pl.BlockSpec((tm, tk), lambda i, j, k: (i, k))
    b_spec = pl.BlockSpec((tn, tk), lambda i, j, k: (j, k)) if tb else pl.BlockSpec((tk, tn), lambda i, j, k: (k, j))
    o_spec = pl.BlockSpec((tm, tn), lambda i, j, k: (i, j))
    two = also_bf16

    def body(*refs):
        refs = list(refs)
        a_ref, b_ref = refs[0], refs[1]
        e_ref = refs[2] if relu2_of is not None else None
        pos = 3 if relu2_of is not None else 2
        o_ref = refs[pos]
        o2_ref = refs[pos + 1] if two else None
        acc_ref = refs[-1]
        k = pl.program_id(2)
        av = a_ref[...].astype(BF16)
        bv = b_ref[...].astype(BF16)
        if ta:
            part = _dot_tn(av, bv)
        elif tb:
            part = _dot_nt(av, bv)
        else:
            part = _dot(av, bv)

        @pl.when(k == 0)
        def _():
            acc_ref[...] = part

        @pl.when(k > 0)
        def _():
            acc_ref[...] += part

        @pl.when(k == nk - 1)
        def _():
            r = acc_ref[...]
            if relu2_of is not None:
                r = r * (2.0 * jnp.sqrt(e_ref[...].astype(F32)))
            if relu2:
                r = jnp.square(jnp.maximum(r, 0.0))
            o_ref[...] = r.astype(o_ref.dtype)
            if also_bf16:
                o2_ref[...] = r.astype(BF16)

    in_specs = [a_spec, b_spec]
    args = [a, b]
    if relu2_of is not None:
        in_specs.append(o_spec)
        args.append(relu2_of)
    out_shape = [jax.ShapeDtypeStruct((m, n), out_dtype)]
    out_specs = [o_spec]
    if two:
        out_shape.append(jax.ShapeDtypeStruct((m, n), BF16))
        out_specs.append(o_spec)
    res = pl.pallas_call(
        body, name=name, grid=(m // tm, n // tn, nk), in_specs=in_specs, out_specs=out_specs, out_shape=out_shape,
        scratch_shapes=[pltpu.VMEM((tm, tn), F32)],
        compiler_params=_cparams(("parallel", "parallel", "arbitrary")),
    )(*args)
    return res if two else res[0]


def _rms(x, g):
    r = lax.rsqrt(jnp.mean(x * x, axis=-1, keepdims=True) + EPS)
    return x * r * g


def _rms_bwd(x, g, dy):
    r = lax.rsqrt(jnp.mean(x * x, axis=-1, keepdims=True) + EPS)
    xh = x * r
    gdy = dy * g
    dx = r * (gdy - xh * jnp.mean(xh * gdy, axis=-1, keepdims=True))
    return dx, xh * dy


def _norm_fwd(x, g, *, name, resid=None, out_dtype=BF16):
    s, d = x.shape
    tr = _tile(s, 256)
    row = pl.BlockSpec((tr, d), lambda i: (i, 0))
    gsp = pl.BlockSpec((1, d), lambda i: (0, 0))

    def body(*refs):
        if resid is None:
            x_ref, g_ref, o_ref = refs
            o_ref[...] = _rms(x_ref[...], g_ref[...]).astype(o_ref.dtype)
        else:
            x_ref, g_ref, r_ref, o_ref = refs
            o_ref[...] = (r_ref[...] + _rms(x_ref[...], g_ref[...])).astype(o_ref.dtype)

    args = [x, g.reshape(1, d)] + ([] if resid is None else [resid])
    return pl.pallas_call(
        body, name=name, grid=(s // tr,), in_specs=[row, gsp] + ([] if resid is None else [row]),
        out_specs=row, out_shape=jax.ShapeDtypeStruct((s, d), out_dtype), compiler_params=_cparams(("parallel",)),
    )(*args)


def _norm_bwd(x, g, dy, *, name, add=None, out_dtype=F32):
    s, d = x.shape
    tr = _tile(s, 256)
    row = pl.BlockSpec((tr, d), lambda i: (i, 0))
    gsp = pl.BlockSpec((1, d), lambda i: (0, 0))

    def body(*refs):
        if add is None:
            x_ref, g_ref, dy_ref, dx_ref, dg_ref = refs
        else:
            x_ref, g_ref, dy_ref, add_ref, dx_ref, dg_ref = refs
        dx, gterm = _rms_bwd(x_ref[...], g_ref[...], dy_ref[...].astype(F32))
        if add is not None:
            dx = dx + add_ref[...]
        dx_ref[...] = dx.astype(dx_ref.dtype)

        @pl.when(pl.program_id(0) == 0)
        def _():
            dg_ref[...] = jnp.zeros_like(dg_ref)

        dg_ref[...] += jnp.sum(gterm, axis=0, keepdims=True)

    args = [x, g.reshape(1, d), dy] + ([] if add is None else [add])
    return pl.pallas_call(
        body, name=name, grid=(s // tr,), in_specs=[row, gsp, row] + ([] if add is None else [row]),
        out_specs=[row, gsp], out_shape=[jax.ShapeDtypeStruct((s, d), out_dtype), jax.ShapeDtypeStruct((1, d), F32)],
        compiler_params=_cparams(("arbitrary",)),
    )(*args)


def _loss_head(y, target):
    s, d = y.shape
    tr = _tile(s, 256)
    row = pl.BlockSpec((tr, d), lambda i: (i, 0))
    lsp = pl.BlockSpec((1, LANES), lambda i: (0, 0))

    def body(y_ref, t_ref, l_ref, dy_ref):
        e = y_ref[...] - t_ref[...]
        dy_ref[...] = e * (1.0 / d)

        @pl.when(pl.program_id(0) == 0)
        def _():
            l_ref[...] = jnp.zeros_like(l_ref)

        part = 0.5 * jnp.sum(jnp.mean(e * e, axis=-1, keepdims=True), axis=0, keepdims=True)
        l_ref[...] += jnp.broadcast_to(part, (1, LANES))

    return pl.pallas_call(
        body, name="loss_head", grid=(s // tr,), in_specs=[row, row], out_specs=[lsp, row],
        out_shape=[jax.ShapeDtypeStruct((1, LANES), F32), jax.ShapeDtypeStruct((s, d), F32)],
        compiler_params=_cparams(("arbitrary",)),
    )(y, target)


def _rope_tables(pos_col):
    s = pos_col.shape[0]
    tr = _tile(s, 512)
    f_mla = ROPE_BASE ** (-jnp.arange(ROPE_DIM // 2, dtype=F32) / (ROPE_DIM // 2))
    f_ret = ROPE_BASE ** (-jnp.arange(HEAD // 2, dtype=F32) / (HEAD // 2))
    fm = jnp.concatenate([jnp.zeros((64,), F32), f_mla, f_mla, jnp.zeros((32,), F32)]).reshape(1, LANES)
    fr = jnp.tile(jnp.concatenate([f_ret, f_ret]), 4).reshape(1, 2 * LANES)

    def body(p_ref, fm_ref, fr_ref, cm_ref, sm_ref, cr_ref, sr_ref):
        p = p_ref[...].astype(F32)
        am = p * fm_ref[...]
        ar = p * fr_ref[...]
        cm_ref[...] = jnp.cos(am)
        sm_ref[...] = jnp.sin(am)
        cr_ref[...] = jnp.cos(ar)
        sr_ref[...] = jnp.sin(ar)

    return pl.pallas_call(
        body, name="rope_tables", grid=(s // tr,),
        in_specs=[pl.BlockSpec((tr, 1), lambda i: (i, 0)), pl.BlockSpec((1, LANES), lambda i: (0, 0)),
                  pl.BlockSpec((1, 2 * LANES), lambda i: (0, 0))],
        out_specs=[pl.BlockSpec((tr, LANES), lambda i: (i, 0))] * 2 + [pl.BlockSpec((tr, 2 * LANES), lambda i: (i, 0))] * 2,
        out_shape=[jax.ShapeDtypeStruct((s, LANES), F32)] * 2 + [jax.ShapeDtypeStruct((s, 2 * LANES), F32)] * 2,
        compiler_params=_cparams(("parallel",)),
    )(pos_col, fm, fr)


def _lane(shape):
    return lax.broadcasted_iota(jnp.int32, shape, len(shape) - 1)


def _rot_mla(z):
    l = _lane(z.shape) % LANES
    n = z.shape[-1]
    return jnp.where(l < 80, -pltpu.roll(z, n - 16, 1), pltpu.roll(z, 16, 1))


def _rot_mla_t(y):
    l = _lane(y.shape) % LANES
    n = y.shape[-1]
    return jnp.where((l >= 64) & (l < 80), pltpu.roll(y, n - 16, 1),
                     jnp.where((l >= 80) & (l < 96), -pltpu.roll(y, 16, 1), 0.0))


def _rot_ret(z):
    l = _lane(z.shape) % HEAD
    n = z.shape[-1]
    return jnp.where(l < 32, -pltpu.roll(z, n - 32, 1), pltpu.roll(z, 32, 1))


def _rot_ret_t(y):
    l = _lane(y.shape) % HEAD
    n = y.shape[-1]
    return jnp.where(l < 32, pltpu.roll(y, n - 32, 1), -pltpu.roll(y, 32, 1))


def _log_sigmoid(x):
    return jnp.minimum(x, 0.0) - jnp.log1p(jnp.exp(-jnp.abs(x)))


def _fox_cum(proj, bias_row):
    s = proj.shape[0]
    nb = s // TQ

    def body(x_ref, b_ref, cc_ref, cr_ref, carry_ref):
        @pl.when(pl.program_id(0) == 0)
        def _():
            carry_ref[...] = jnp.zeros_like(carry_ref)

        ls = _log_sigmoid(x_ref[...] + b_ref[...])
        r = lax.broadcasted_iota(jnp.int32, (TQ, TQ), 0)
        c = lax.broadcasted_iota(jnp.int32, (TQ, TQ), 1)
        tri = (c <= r).astype(F32)
        cum = _dot_exact(tri, ls) + carry_ref[...]
        carry_ref[...] = cum[TQ - 1:TQ, :]
        cc_ref[...] = cum
        cr_ref[...] = cum.T[0:8, :]

    return pl.pallas_call(
        body, name="fox_cum", grid=(nb,),
        in_specs=[pl.BlockSpec((TQ, LANES), lambda i: (i, OFF_MISC)), pl.BlockSpec((1, LANES), lambda i: (0, 0))],
        out_specs=[pl.BlockSpec((TQ, LANES), lambda i: (i, 0)), pl.BlockSpec((8, TQ), lambda i: (0, i))],
        out_shape=[jax.ShapeDtypeStruct((s, LANES), F32), jax.ShapeDtypeStruct((8, s), F32)],
        scratch_shapes=[pltpu.VMEM((1, LANES), F32)],
        compiler_params=_cparams(("arbitrary",)),
    )(proj, bias_row)


def _fox_gate_bwd(dck, drs, proj, bias_row, dkr):
    s = proj.shape[0]
    nb = s // TQ

    def body(d_ref, r_ref, x_ref, b_ref, k_ref, o_ref, db_ref, carry_ref):
        @pl.when(pl.program_id(0) == 0)
        def _():
            carry_ref[...] = jnp.zeros_like(carry_ref)
            db_ref[...] = jnp.zeros_like(db_ref)

        rows = jnp.concatenate([d_ref[0], d_ref[1], jnp.zeros((TQ - 16, TQ), F32)], axis=0)
        t = rows.T
        l = _lane((TQ, LANES))
        r0, r1 = r_ref[0], r_ref[1]
        rsum = jnp.where(l == 0, r0[:, 0:1], jnp.where(l == 1, r0[:, HEAD:HEAD + 1],
                         jnp.where(l == 2, r1[:, 0:1], jnp.where(l == 3, r1[:, HEAD:HEAD + 1], 0.0))))
        dcum = rsum - jnp.where(l < 2, t, pltpu.roll(t, LANES - 6, 1))
        r = lax.broadcasted_iota(jnp.int32, (TQ, TQ), 0)
        c = lax.broadcasted_iota(jnp.int32, (TQ, TQ), 1)
        triu = (c >= r).astype(F32)
        rc = _dot_exact(triu, dcum) + carry_ref[...]
        carry_ref[...] = rc[0:1, :]
        f = x_ref[...] + b_ref[...]
        sig_neg = 1.0 / (1.0 + jnp.exp(f))
        df = jnp.where(l < N_HEADS, rc * sig_neg, 0.0)
        db_ref[...] += jnp.sum(df, axis=0, keepdims=True)
        o_ref[...] = (df + k_ref[...]).astype(o_ref.dtype)

    rev = lambda i: nb - 1 - i
    return pl.pallas_call(
        body, name="fox_gate_bwd", grid=(nb,),
        in_specs=[pl.BlockSpec((2, 8, TQ), lambda i: (0, 0, rev(i))), pl.BlockSpec((2, TQ, LANES), lambda i: (0, rev(i), 0)),
                  pl.BlockSpec((TQ, LANES), lambda i: (rev(i), OFF_MISC)),
                  pl.BlockSpec((1, LANES), lambda i: (0, 0)), pl.BlockSpec((TQ, LANES), lambda i: (rev(i), 0))],
        out_specs=[pl.BlockSpec((TQ, LANES), lambda i: (rev(i), 0)), pl.BlockSpec((1, LANES), lambda i: (0, 0))],
        out_shape=[jax.ShapeDtypeStruct((s, LANES), BF16), jax.ShapeDtypeStruct((1, LANES), F32)],
        scratch_shapes=[pltpu.VMEM((1, LANES), F32)],
        compiler_params=_cparams(("arbitrary",)),
    )(dck, drs, proj, bias_row, dkr)


def _mla_prep(proj, cos_m, sin_m, g_q, g_kv, wq, wk, wv):
    s = proj.shape[0]
    tr = _tile(s, 256)

    def body(cq_ref, ckv_ref, misc_ref, cos_ref, sin_ref, gq_ref, gkv_ref, wq_ref, wk_ref, wv_ref,
             q_ref, k_ref, v_ref, cqn_ref, ckvn_ref):
        cos4 = jnp.tile(cos_ref[...], (1, 4))
        sin4 = jnp.tile(sin_ref[...], (1, 4))
        cqn = _rms(cq_ref[...], gq_ref[...]).astype(BF16)
        ckvn = _rms(ckv_ref[...], gkv_ref[...]).astype(BF16)
        cqn_ref[...] = cqn
        ckvn_ref[...] = ckvn
        zq = _dot(cqn, wq_ref[...])
        q_ref[...] = (zq * cos4 + _rot_mla(zq) * sin4).astype(BF16)
        l = _lane((tr, LANES))
        kr = jnp.where((l >= KR_LANE) & (l < KR_LANE + ROPE_DIM), misc_ref[...], 0.0)
        zk = _dot(ckvn, wk_ref[...]) + jnp.tile(kr, (1, 4))
        k_ref[...] = (zk * cos4 + _rot_mla(zk) * sin4).astype(BF16)
        v_ref[...] = _dot(ckvn, wv_ref[...]).astype(BF16)

    full = lambda a: pl.BlockSpec(a.shape, lambda i: (0, 0))
    rowb = lambda w: pl.BlockSpec((tr, w), lambda i: (i, 0))
    gq2, gkv2 = g_q.reshape(1, Q_RANK), g_kv.reshape(1, KV_RANK)
    return pl.pallas_call(
        body, name="mla_prep", grid=(s // tr,),
        in_specs=[pl.BlockSpec((tr, 256), lambda i: (i, OFF_CQ // 2)), pl.BlockSpec((tr, LANES), lambda i: (i, OFF_CKV)),
                  pl.BlockSpec((tr, LANES), lambda i: (i, OFF_MISC)), rowb(LANES), rowb(LANES),
                  full(gq2), full(gkv2), full(wq), full(wk), full(wv)],
        out_specs=[rowb(512), rowb(512), rowb(512), rowb(256), rowb(128)],
        out_shape=[jax.ShapeDtypeStruct((s, 512), BF16), jax.ShapeDtypeStruct((s, 512), BF16), jax.ShapeDtypeStruct((s, 512), BF16),
                   jax.ShapeDtypeStruct((s, 256), BF16), jax.ShapeDtypeStruct((s, 128), BF16)],
        compiler_params=_cparams(("parallel",)),
    )(proj, proj, proj, cos_m, sin_m, gq2, gkv2, wq, wk, wv)


def _mla_prep_bwd(dq, dk, dv, proj, cqn, ckvn, cos_m, sin_m, g_q, g_kv, wq, wk, wv):
    s = proj.shape[0]
    tr = _tile(s, 256)

    def body(dq_ref, dk_ref, dv_ref, cq_ref, ckv_ref, cqn_ref, ckvn_ref, cos_ref, sin_ref, gq_ref, gkv_ref,
             wq_ref, wk_ref, wv_ref, dcq_ref, dckv_ref, dkr_ref, dwq_ref, dwk_ref, dwv_ref, dgq_ref, dgkv_ref):
        @pl.when(pl.program_id(0) == 0)
        def _():
            for r in (dwq_ref, dwk_ref, dwv_ref, dgq_ref, dgkv_ref):
                r[...] = jnp.zeros_like(r)

        cos4 = jnp.tile(cos_ref[...], (1, 4))
        sin4 = jnp.tile(sin_ref[...], (1, 4))
        dqv = dq_ref[...]
        dzq = dqv * cos4 + _rot_mla_t(dqv * sin4)
        dkv_ = dk_ref[...]
        dzk = dkv_ * cos4 + _rot_mla_t(dkv_ * sin4)
        l = _lane((tr, LANES))
        in_rope = (l >= KR_LANE) & (l < KR_LANE + ROPE_DIM)
        dkr = dzk[:, 0:128] + dzk[:, 128:256] + dzk[:, 256:384] + dzk[:, 384:512]
        dkr_ref[...] = jnp.where(in_rope, dkr, 0.0)
        dzq_b = dzq.astype(BF16)
        dzk_b = dzk.astype(BF16)
        dv_b = dv_ref[...].astype(BF16)
        dcqn = _dot_nt(dzq_b, wq_ref[...])
        dckvn = _dot_nt(dzk_b, wk_ref[...]) + _dot_nt(dv_b, wv_ref[...])
        dwq_ref[...] += _dot_tn(cqn_ref[...], dzq_b)
        dwk_ref[...] += _dot_tn(ckvn_ref[...], dzk_b)
        dwv_ref[...] += _dot_tn(ckvn_ref[...], dv_b)
        dcq, gq_term = _rms_bwd(cq_ref[...], gq_ref[...], dcqn)
        dckv, gkv_term = _rms_bwd(ckv_ref[...], gkv_ref[...], dckvn)
        dcq_ref[...] = dcq.astype(BF16)
        dckv_ref[...] = dckv.astype(BF16)
        dgq_ref[...] += jnp.sum(gq_term, axis=0, keepdims=True)
        dgkv_ref[...] += jnp.sum(gkv_term, axis=0, keepdims=True)

    full = lambda shp: pl.BlockSpec(shp, lambda i: (0, 0))
    rowb = lambda w: pl.BlockSpec((tr, w), lambda i: (i, 0))
    gq2, gkv2 = g_q.reshape(1, Q_RANK), g_kv.reshape(1, KV_RANK)
    return pl.pallas_call(
        body, name="mla_prep_bwd", grid=(s // tr,),
        in_specs=[rowb(512), rowb(512), rowb(512),
                  pl.BlockSpec((tr, 256), lambda i: (i, OFF_CQ // 2)), pl.BlockSpec((tr, LANES), lambda i: (i, OFF_CKV)),
                  rowb(256), rowb(128), rowb(LANES), rowb(LANES), full((1, Q_RANK)), full((1, KV_RANK)),
                  full(wq.shape), full(wk.shape), full(wv.shape)],
        out_specs=[rowb(256), rowb(128), rowb(128), full(wq.shape), full(wk.shape), full(wv.shape),
                   full((1, Q_RANK)), full((1, KV_RANK))],
        out_shape=[jax.ShapeDtypeStruct((s, 256), BF16), jax.ShapeDtypeStruct((s, 128), BF16), jax.ShapeDtypeStruct((s, 128), F32),
                   jax.ShapeDtypeStruct(wq.shape, F32), jax.ShapeDtypeStruct(wk.shape, F32), jax.ShapeDtypeStruct(wv.shape, F32),
                   jax.ShapeDtypeStruct((1, Q_RANK), F32), jax.ShapeDtypeStruct((1, KV_RANK), F32)],
        compiler_params=_cparams(("arbitrary",)),
    )(dq, dk, dv, proj, proj, cqn, ckvn, cos_m, sin_m, gq2, gkv2, wq, wk, wv)


def _ret_prep(proj, cos_r, sin_r):
    s = proj.shape[0]
    tr = _tile(s, 256)

    def body(q_ref, k_ref, cos_ref, sin_ref, qo_ref, ko_ref):
        cos, sin = cos_ref[...], sin_ref[...]
        q, k = q_ref[...], k_ref[...]
        qo_ref[...] = (q * cos + _rot_ret(q) * sin).astype(BF16)
        ko_ref[...] = ((k * cos + _rot_ret(k) * sin) * (HEAD ** -0.5)).astype(BF16)

    rowb = pl.BlockSpec((tr, 256), lambda i: (i, 0))
    return pl.pallas_call(
        body, name="ret_prep", grid=(s // tr,),
        in_specs=[pl.BlockSpec((tr, 256), lambda i: (i, OFF_RQ // 2)), pl.BlockSpec((tr, 256), lambda i: (i, OFF_RK // 2)), rowb, rowb],
        out_specs=[rowb, rowb], out_shape=[jax.ShapeDtypeStruct((s, 256), BF16)] * 2,
        compiler_params=_cparams(("parallel",)),
    )(proj, proj, cos_r, sin_r)


def _ret_prep_bwd(dq, dk, cos_r, sin_r):
    s = dq.shape[0]
    tr = _tile(s, 256)

    def body(dq_ref, dk_ref, cos_ref, sin_ref, qo_ref, ko_ref):
        cos, sin = cos_ref[...], sin_ref[...]
        q, k = dq_ref[...], dk_ref[...] * (HEAD ** -0.5)
        qo_ref[...] = (q * cos + _rot_ret_t(q * sin)).astype(BF16)
        ko_ref[...] = (k * cos + _rot_ret_t(k * sin)).astype(BF16)

    rowb = pl.BlockSpec((tr, 256), lambda i: (i, 0))
    return pl.pallas_call(
        body, name="ret_prep_bwd", grid=(s // tr,), in_specs=[rowb] * 4, out_specs=[rowb, rowb],
        out_shape=[jax.ShapeDtypeStruct((s, 256), BF16)] * 2, compiler_params=_cparams(("parallel",)),
    )(dq, dk, cos_r, sin_r)


_LOG_GAMMA = [float(np.log1p(-np.float32(2.0) ** np.float32(-5.0 - h))) for h in range(N_HEADS)]
_MLA_SCALE = float((HEAD + ROPE_DIM) ** -0.5)
_QK_SCALE = float(HEAD ** -0.5)
KEY_BLOCKS = 4
QB = 256


def _split2(x):
    h = x.astype(BF16)
    return h, (x - h.astype(F32)).astype(BF16)


def _dot2(x, u):
    h, lo = _split2(x)
    return _dot(h, u) + _dot(lo, u)


def _head_pick(block, head, axis):
    idx = lax.broadcasted_iota(jnp.int32, block.shape, axis)
    return jnp.sum(jnp.where(idx == head, block, 0.0), axis=axis, keepdims=True)


def _log_gamma_of(head):
    lg = jnp.float32(_LOG_GAMMA[3])
    for h in (2, 1, 0):
        lg = jnp.where(head == h, jnp.float32(_LOG_GAMMA[h]), lg)
    return lg


def _mixer_specs(mode, s, q_off, k_off, v_off):
    nhb = 2
    bw = 2 * LANES if mode == "mla" else LANES
    nsub = KEY_BLOCKS if (s // TQ) % KEY_BLOCKS == 0 else 1
    q_spec = pl.BlockSpec((QB, bw), lambda p, i: (i, q_off + p))
    k_spec = pl.BlockSpec((s, bw), lambda p, i: (0, k_off + p))
    v_spec = pl.BlockSpec((s, bw), lambda p, i: (0, v_off + p))
    return nhb, N_HEADS // nhb, nsub, q_spec, k_spec, v_spec


def _mixer_geometry(mode, i, nsub):
    w = TQ * nsub
    row = lax.broadcasted_iota(jnp.int32, (QB, w), 0)
    col = lax.broadcasted_iota(jnp.int32, (QB, w), 1)
    nfull = (i * QB) // w
    dist = col - row
    if mode in ("fox", "sb"):
        rel = dist
    else:
        rel = col - (row | (CHUNK - 1))

    def visible(c):
        off = c * w - i * QB
        return (rel + off) < 0 if mode == "sb" else (rel + off) <= 0

    return nfull, dist, visible


class _SideJob:
    def __init__(self, inputs, out_shape, n_sems, sends, recvs):
        self.inputs, self.out_shape, self.n_sems, self.sends, self.recvs = list(inputs), list(out_shape), n_sems, sends, recvs


def _carry_side_job(body, n_in, n_out, side, n_steps):
    if side is None:
        return body
    si, so = len(side.inputs), len(side.out_shape)

    def at(corner):
        ok = pl.program_id(0) == corner[0]
        for d in range(1, len(n_steps)):
            ok = ok & (pl.program_id(d) == corner[d])
        return ok

    def wrapped(*refs):
        ins, s_ins = refs[:n_in], refs[n_in:n_in + si]
        outs, s_outs = refs[n_in + si:n_in + si + n_out], refs[n_in + si + n_out:n_in + si + n_out + so]
        scratch, send, recv = refs[n_in + si + n_out + so:-2], refs[-2], refs[-1]

        @pl.when(at([0] * len(n_steps)))
        def _():
            for cp in side.sends(s_ins, s_outs, send, recv):
                cp.start()

        body(*ins, *outs, *scratch)

        @pl.when(at([n - 1 for n in n_steps]))
        def _():
            for cp in side.recvs(s_ins, s_outs, send, recv):
                cp.wait_recv()
            for cp in side.sends(s_ins, s_outs, send, recv):
                cp.wait_send()

    return wrapped


def _side_specs(side):
    if side is None:
        return [], [], []
    hbm = pl.BlockSpec(memory_space=pl.ANY)
    return ([hbm] * len(side.inputs), [hbm] * len(side.out_shape),
            [pltpu.SemaphoreType.DMA((side.n_sems,)), pltpu.SemaphoreType.DMA((side.n_sems,))])


def _mixer_fwd(mode, qa, q_off, ka, k_off, va, v_off, *, cum_col=None, cum_row=None, side=None):
    s = qa.shape[0]
    nq = s // QB
    nhb, nblk, nsub, q_spec, k_spec, v_spec = _mixer_specs(mode, s, q_off, k_off, v_off)
    w = TQ * nsub
    softmax = mode in ("fox", "mla")
    has_stat = mode != "ret"

    def body(*refs):
        refs = list(refs)
        q_ref, k_ref, v_ref = refs[:3]
        refs = refs[3:]
        if mode == "fox":
            cc_ref, cr_ref = refs[:2]
            refs = refs[2:]
        o_ref = refs[0]
        st_ref = refs[1] if has_stat else None
        p = pl.program_id(0)
        i = pl.program_id(1)
        nfull, dist, visible = _mixer_geometry(mode, i, nsub)
        lane = _lane((1, LANES))
        heads = [nhb * p + hh for hh in range(nhb)]
        wide = mode == "mla"
        q_scale = _QK_SCALE if mode in ("fox", "sb") else 1.0
        cols = [slice(hh * LANES, (hh + 1) * LANES) if wide else slice(None) for hh in range(nhb)]
        if wide:
            qs = [q_ref[:, cols[hh]] for hh in range(nhb)]
        else:
            qf = q_ref[...].astype(F32) * q_scale
            qs = [jnp.where((lane // HEAD) == hh, qf, 0.0).astype(BF16) for hh in range(nhb)]
        if mode == "fox":
            cqs = [_head_pick(cc_ref[...], h, 1) for h in heads]
        if mode == "sb":
            r1 = lax.broadcasted_iota(jnp.int32, (TQ, TQ), 0)
            c1 = lax.broadcasted_iota(jnp.int32, (TQ, TQ), 1)
            u_after = (r1 > c1).astype(BF16)

        def chunk(c):
            return pl.ds(pl.multiple_of(c * w, w), w)

        def scores(c):
            js = chunk(c)
            return tuple(_dot_nt(qs[hh], k_ref[js, cols[hh]]) for hh in range(nhb))

        def head_step(hh, c, js, sc, vj, carry, last):
            if softmax:
                m, l, acc = carry
                if mode == "fox":
                    ck = _head_pick(cr_ref[:, js], heads[hh], 0)
                    sc = sc + (cqs[hh] - ck)
                else:
                    sc = sc * _MLA_SCALE
                if last:
                    sc = jnp.where(visible(c), sc, NEG)
                m_new = jnp.maximum(m, jnp.max(sc, axis=-1, keepdims=True))
                alpha = jnp.exp(m - m_new)
                pr = jnp.exp(sc - m_new)
                l = alpha * l + jnp.sum(pr, axis=-1, keepdims=True)
                acc = alpha * acc + _dot(pr.astype(BF16), vj)
                return m_new, l, acc
            run, acc = carry
            z = sc
            log_beta = jnp.minimum(z, 0.0) - jnp.log(1.0 + jnp.exp(-jnp.abs(z)))
            log_stay = log_beta - z
            if last:
                vis = visible(c)
                log_stay = jnp.where(vis, log_stay, 0.0)
            parts = [None] * nsub
            for b in reversed(range(nsub)):
                ls_b = log_stay[:, b * TQ:(b + 1) * TQ]
                parts[b] = _dot2(ls_b, u_after) + run
                run = run + jnp.sum(ls_b, axis=-1, keepdims=True)
            later = parts[0] if nsub == 1 else jnp.concatenate(parts, axis=1)
            wgt = jnp.exp(log_beta + later)
            if last:
                wgt = jnp.where(vis, wgt, 0.0)
            return run, acc + _dot(wgt.astype(BF16), vj)

        def step(c, c_next, state, last):
            scs, carries = state
            nxt = scores(c_next) if c_next is not None else None
            js = chunk(c)
            return nxt, tuple(head_step(hh, c, js, scs[hh], v_ref[js, cols[hh]], carries[hh], last) for hh in range(nhb))

        zero_acc = jnp.zeros((QB, LANES), F32)
        zero1 = jnp.zeros((QB, 1), F32)
        if softmax:
            init = tuple((jnp.full((QB, 1), NEG, F32), zero1, zero_acc) for _ in range(nhb))
        else:
            init = tuple((zero1, zero_acc) for _ in range(nhb))
        if mode == "sb":
            state = step(nfull, jnp.maximum(nfull - 1, 0), (scores(nfull), init), True)
            _, carries = lax.fori_loop(0, nfull, lambda t, st: step(nfull - 1 - t, jnp.maximum(nfull - 2 - t, 0), st, False), state)
        else:
            state = lax.fori_loop(0, nfull, lambda c, st: step(c, c + 1, st, False), (scores(0), init))
            _, carries = step(nfull, None, state, True)
        if softmax:
            outs = [acc / l for (m, l, acc) in carries]
            stats = [m + jnp.log(l) for (m, l, acc) in carries]
        else:
            outs, stats = [acc for (run, acc) in carries], [run for (run, acc) in carries]
        hm0 = (lane // HEAD) == 0
        pick = lambda a: jnp.where(hm0, a[0], a[1])
        if wide:
            for hh in range(nhb):
                o_ref[:, cols[hh]] = outs[hh]
        else:
            o_ref[...] = pick(outs)
        if has_stat:
            st_ref[0] = pick(stats)

    in_specs = [q_spec, k_spec, v_spec]
    args = [qa, ka, va]
    if mode == "fox":
        in_specs += [pl.BlockSpec((QB, LANES), lambda p, i: (i, 0)), pl.BlockSpec((8, s), lambda p, i: (0, 0))]
        args += [cum_col, cum_row]
    bw = 2 * LANES if mode == "mla" else LANES
    out_specs = [pl.BlockSpec((QB, bw), lambda p, i: (i, p))]
    out_shape = [jax.ShapeDtypeStruct((s, nblk * bw), F32)]
    out_specs.append(pl.BlockSpec((1, QB, LANES), lambda p, i: (p, i, 0)))
    out_shape.append(jax.ShapeDtypeStruct((nblk, s, LANES), F32))
    side_in, side_out, side_scratch = _side_specs(side)
    res = pl.pallas_call(
        _carry_side_job(body, len(args), len(out_shape), side, (nblk, nq)), name=mode + "_fwd", grid=(nblk, nq),
        in_specs=in_specs + side_in, out_specs=out_specs + side_out,
        out_shape=out_shape + ([] if side is None else side.out_shape), scratch_shapes=side_scratch,
        compiler_params=_cparams(("parallel", "parallel") if side is None else ("arbitrary", "arbitrary")),
    )(*args, *([] if side is None else side.inputs))
    return (res[0], res[1]) if side is None else (res[0], res[1], res[2:])


def _mixer_bwd(mode, qa, q_off, ka, k_off, va, v_off, o, do, *, stat=None, cum_col=None, cum_row=None, side=None):
    s = qa.shape[0]
    nq = s // QB
    nhb, nblk, nsub, q_spec, k_spec, v_spec = _mixer_specs(mode, s, q_off, k_off, v_off)
    w = TQ * nsub
    softmax = mode in ("fox", "mla")
    has_stat = mode != "ret"

    def body(*refs):
        refs = list(refs)
        q_ref, k_ref, v_ref, o_ref, do_ref = refs[:5]
        refs = refs[5:]
        if has_stat:
            st_ref = refs[0]
            refs = refs[1:]
        if mode == "fox":
            cc_ref, cr_ref = refs[:2]
            refs = refs[2:]
        dq_ref, dk_ref, dv_ref = refs[:3]
        dck_ref, drs_ref = refs[3:5] if mode == "fox" else (None, None)
        p = pl.program_id(0)
        i = pl.program_id(1)

        @pl.when(i == 0)
        def _():
            dk_ref[...] = jnp.zeros_like(dk_ref)
            dv_ref[...] = jnp.zeros_like(dv_ref)
            if mode == "fox":
                dck_ref[...] = jnp.zeros_like(dck_ref)

        nfull, dist, visible = _mixer_geometry(mode, i, nsub)
        lane = _lane((1, LANES))
        heads = [nhb * p + hh for hh in range(nhb)]
        dov = do_ref[...]
        wide = mode == "mla"
        q_scale = _QK_SCALE if mode in ("fox", "sb") else 1.0
        cols = [slice(hh * LANES, (hh + 1) * LANES) if wide else slice(None) for hh in range(nhb)]
        if wide:
            prod = dov * o_ref[...]
            qs = [q_ref[:, cols[hh]] for hh in range(nhb)]
            dos = [dov[:, cols[hh]].astype(BF16) for hh in range(nhb)]
            deltas = [jnp.sum(prod[:, cols[hh]], axis=-1, keepdims=True) for hh in range(nhb)]
        else:
            qf = q_ref[...].astype(F32) * q_scale
            prod = dov * o_ref[...]
            hms = [(lane // HEAD) == hh for hh in range(nhb)]
            qs = [jnp.where(hm, qf, 0.0).astype(BF16) for hm in hms]
            dos = [jnp.where(hm, dov, 0.0).astype(BF16) for hm in hms]
            deltas = [jnp.sum(jnp.where(hm, prod, 0.0), axis=-1, keepdims=True) for hm in hms]
        if has_stat:
            st = st_ref[0]
            stats = [st[:, hh * HEAD:hh * HEAD + 1] for hh in range(nhb)]
        if mode == "fox":
            cqs = [_head_pick(cc_ref[...], h, 1) for h in heads]
        if mode == "sb":
            r1 = lax.broadcasted_iota(jnp.int32, (TQ, TQ), 0)
            c1 = lax.broadcasted_iota(jnp.int32, (TQ, TQ), 1)
            u_upto = (r1 <= c1).astype(BF16)
            u_before = (r1 < c1).astype(BF16)

        def chunk(c):
            return pl.ds(pl.multiple_of(c * w, w), w)

        def scores(c):
            js = chunk(c)
            return tuple((_dot_nt(qs[hh], k_ref[js, cols[hh]]), _dot_nt(dos[hh], v_ref[js, cols[hh]])) for hh in range(nhb))

        def emit(hh, js, ds_b, pr_b, dq):
            dk_ref[js, cols[hh]] += _dot_tn(ds_b, qs[hh])
            dv_ref[js, cols[hh]] += _dot_tn(pr_b, dos[hh])
            return dq + _dot(ds_b, k_ref[js, cols[hh]])

        def head_step(hh, c, js, sc_dp, carry, last):
            sc, dp = sc_dp
            if softmax:
                dq, rsum = carry
                if mode == "fox":
                    ck = _head_pick(cr_ref[:, js], heads[hh], 0)
                    sc = sc + (cqs[hh] - ck)
                else:
                    sc = sc * _MLA_SCALE
                if last:
                    sc = jnp.where(visible(c), sc, NEG)
                pr = jnp.exp(sc - stats[hh])
                ds = pr * (dp - deltas[hh])
                if mode == "fox":
                    dck_ref[0, hh:hh + 1, js] += jnp.sum(ds, axis=0, keepdims=True)
                    rsum = rsum + jnp.sum(ds, axis=-1, keepdims=True)
                if mode == "mla":
                    ds = ds * _MLA_SCALE
                return emit(hh, js, ds.astype(BF16), pr.astype(BF16), dq), rsum
            seen, gsum, dq = carry
            z = sc
            log_beta = jnp.minimum(z, 0.0) - jnp.log(1.0 + jnp.exp(-jnp.abs(z)))
            log_stay = log_beta - z
            if last:
                vis = visible(c)
                log_stay = jnp.where(vis, log_stay, 0.0)
            parts = []
            for b in range(nsub):
                ls_b = log_stay[:, b * TQ:(b + 1) * TQ]
                parts.append((stats[hh] - seen) - _dot2(ls_b, u_upto))
                seen = seen + jnp.sum(ls_b, axis=-1, keepdims=True)
            later = parts[0] if nsub == 1 else jnp.concatenate(parts, axis=1)
            wgt = jnp.exp(log_beta + later)
            if last:
                wgt = jnp.where(vis, wgt, 0.0)
            g = dp * wgt
            parts = []
            for b in range(nsub):
                g_b = g[:, b * TQ:(b + 1) * TQ]
                parts.append(gsum + _dot2(g_b, u_before))
                gsum = gsum + jnp.sum(g_b, axis=-1, keepdims=True)
            before = parts[0] if nsub == 1 else jnp.concatenate(parts, axis=1)
            beta = jnp.exp(log_beta)
            dz = g * (1.0 - beta) - beta * before
            if last:
                dz = jnp.where(vis, dz, 0.0)
            return seen, gsum, emit(hh, js, dz.astype(BF16), wgt.astype(BF16), dq)

        def step(c, c_next, state, last):
            scs, carries = state
            nxt = scores(c_next) if c_next is not None else None
            js = chunk(c)
            return nxt, tuple(head_step(hh, c, js, scs[hh], carries[hh], last) for hh in range(nhb))

        zero_acc = jnp.zeros((QB, LANES), F32)
        zero1 = jnp.zeros((QB, 1), F32)
        if softmax:
            init = tuple((zero_acc, zero1) for _ in range(nhb))
        else:
            init = tuple((zero1, zero1, zero_acc) for _ in range(nhb))
        state = lax.fori_loop(0, nfull, lambda c, st: step(c, c + 1, st, False), (scores(0), init))
        _, carries = step(nfull, None, state, True)
        if softmax:
            dqs = [dq for (dq, rsum) in carries]
        else:
            dqs = [dq for (seen, gsum, dq) in carries]
        hm0 = (lane // HEAD) == 0
        if wide:
            for hh in range(nhb):
                dq_ref[:, cols[hh]] = dqs[hh]
        else:
            dq_ref[...] = jnp.where(hm0, dqs[0], dqs[1]) * q_scale
        if mode == "fox":
            drs_ref[0] = jnp.where(hm0, carries[0][1], carries[1][1])

    bw = 2 * LANES if mode == "mla" else LANES
    pair_blk = pl.BlockSpec((QB, bw), lambda p, i: (i, p))
    full_blk = pl.BlockSpec((s, bw), lambda p, i: (0, p))
    stat_blk = pl.BlockSpec((1, QB, LANES), lambda p, i: (p, i, 0))
    in_specs = [q_spec, k_spec, v_spec, pair_blk, pair_blk]
    args = [qa, ka, va, o, do]
    if has_stat:
        in_specs.append(stat_blk)
        args.append(stat)
    if mode == "fox":
        in_specs += [pl.BlockSpec((QB, LANES), lambda p, i: (i, 0)), pl.BlockSpec((8, s), lambda p, i: (0, 0))]
        args += [cum_col, cum_row]
    out_specs = [pair_blk, full_blk, full_blk]
    out_shape = [jax.ShapeDtypeStruct((s, nblk * bw), F32)] * 3
    if mode == "fox":
        out_specs += [pl.BlockSpec((1, 8, s), lambda p, i: (p, 0, 0)), stat_blk]
        out_shape += [jax.ShapeDtypeStruct((2, 8, s), F32), jax.ShapeDtypeStruct((2, s, LANES), F32)]
    side_in, side_out, side_scratch = _side_specs(side)
    res = pl.pallas_call(
        _carry_side_job(body, len(args), len(out_shape), side, (nblk, nq)), name=mode + "_bwd", grid=(nblk, nq),
        in_specs=in_specs + side_in, out_specs=out_specs + side_out,
        out_shape=out_shape + ([] if side is None else side.out_shape), scratch_shapes=side_scratch,
        compiler_params=_cparams(("parallel", "arbitrary") if side is None else ("arbitrary", "arbitrary")),
    )(*args, *([] if side is None else side.inputs))
    return res if side is None else (*res[:len(out_shape)], res[len(out_shape):])


def _ret_geometry(p):
    lane = _lane((1, LANES))
    lg_lane = jnp.where(lane < HEAD, _log_gamma_of(2 * p), _log_gamma_of(2 * p + 1))
    a = lax.broadcasted_iota(jnp.int32, (TQ, 1), 0).astype(F32)
    row = lax.broadcasted_iota(jnp.int32, (TQ, TQ), 0)
    col = lax.broadcasted_iota(jnp.int32, (TQ, TQ), 1)
    same_chunk_or_earlier = (col // CHUNK) <= (row // CHUNK)
    gap = jnp.abs(row - col).astype(F32)
    decays = [jnp.where(same_chunk_or_earlier, jnp.exp(_log_gamma_of(2 * p + hh) * gap), 0.0) for hh in range(2)]
    r = lax.broadcasted_iota(jnp.int32, (LANES, LANES), 0)
    c = lax.broadcasted_iota(jnp.int32, (LANES, LANES), 1)
    own_head = (r // HEAD) == (c // HEAD)
    return lane, lg_lane, a, decays, own_head


def _ret_fwd(qa, ka, va, v_off):
    s = qa.shape[0]
    nq = s // TQ

    def body(q_ref, k_ref, v_ref, o_ref, st_ref, state):
        p = pl.program_id(0)

        @pl.when(pl.program_id(1) == 0)
        def _():
            state[...] = jnp.zeros_like(state)

        lane, lg_lane, a, decays, own_head = _ret_geometry(p)
        q = q_ref[...].astype(F32)
        k = k_ref[...]
        v = v_ref[...]
        s_in = state[...]
        st_ref[0, 0] = s_in
        out = _dot((q * jnp.exp(lg_lane * (a + 1.0))).astype(BF16), s_in.astype(BF16))
        for hh in range(2):
            hm = (lane // HEAD) == hh
            qh = jnp.where(hm, q, 0.0).astype(BF16)
            inner = _dot((_dot_nt(qh, k) * decays[hh]).astype(BF16), v)
            out = out + jnp.where(hm, inner, 0.0)
        o_ref[...] = out
        k_tail = (k.astype(F32) * jnp.exp(lg_lane * (TQ - 1.0 - a))).astype(BF16)
        state[...] = jnp.exp(lg_lane * float(TQ)) * s_in + jnp.where(own_head, _dot_tn(k_tail, v), 0.0)

    blk = lambda off: pl.BlockSpec((TQ, LANES), lambda p, i: (i, off + p))
    return pl.pallas_call(
        body, name="ret_fwd", grid=(2, nq), in_specs=[blk(0), blk(0), blk(v_off)],
        out_specs=[blk(0), pl.BlockSpec((1, 1, LANES, LANES), lambda p, i: (p, i, 0, 0))],
        out_shape=[jax.ShapeDtypeStruct((s, 2 * LANES), F32), jax.ShapeDtypeStruct((2, nq, LANES, LANES), F32)],
        scratch_shapes=[pltpu.VMEM((LANES, LANES), F32)],
        compiler_params=_cparams(("parallel", "arbitrary")),
    )(qa, ka, va)


def _ret_bwd(qa, ka, va, v_off, states, do):
    s = qa.shape[0]
    nq = s // TQ

    def body(q_ref, k_ref, v_ref, st_ref, do_ref, dq_ref, dk_ref, dv_ref, dstate):
        p = pl.program_id(0)

        @pl.when(pl.program_id(1) == 0)
        def _():
            dstate[...] = jnp.zeros_like(dstate)

        lane, lg_lane, a, decays, own_head = _ret_geometry(p)
        q = q_ref[...].astype(F32)
        k = k_ref[...]
        kf = k.astype(F32)
        v = v_ref[...]
        dov = do_ref[...]
        s_in = st_ref[0, 0].astype(BF16)
        ds_next = dstate[...]
        ds_b = ds_next.astype(BF16)
        head_decay = jnp.exp(lg_lane * (a + 1.0))
        tail_decay = jnp.exp(lg_lane * (TQ - 1.0 - a))
        k_tail = (kf * tail_decay).astype(BF16)
        dq = _dot_nt(dov.astype(BF16), s_in) * head_decay
        dk = _dot_nt(v, ds_b) * tail_decay
        dv = _dot(k_tail, ds_b)
        for hh in range(2):
            hm = (lane // HEAD) == hh
            qh = jnp.where(hm, q, 0.0).astype(BF16)
            doh = jnp.where(hm, dov, 0.0).astype(BF16)
            att = (_dot_nt(qh, k) * decays[hh]).astype(BF16)
            datt = (_dot_nt(doh, v) * decays[hh]).astype(BF16)
            dv = dv + _dot_tn(att, doh)
            dk = dk + _dot_tn(datt, qh)
            dq = dq + jnp.where(hm, _dot(datt, k), 0.0)
        dq_ref[...] = dq
        dk_ref[...] = dk
        dv_ref[...] = dv
        q_head = (q * head_decay).astype(BF16)
        dstate[...] = jnp.exp(lg_lane * float(TQ)) * ds_next + jnp.where(own_head, _dot_tn(q_head, dov.astype(BF16)), 0.0)

    blk = lambda off: pl.BlockSpec((TQ, LANES), lambda p, i: (nq - 1 - i, off + p))
    return pl.pallas_call(
        body, name="ret_bwd", grid=(2, nq),
        in_specs=[blk(0), blk(0), blk(v_off), pl.BlockSpec((1, 1, LANES, LANES), lambda p, i: (p, nq - 1 - i, 0, 0)), blk(0)],
        out_specs=[blk(0)] * 3, out_shape=[jax.ShapeDtypeStruct((s, 2 * LANES), F32)] * 3,
        scratch_shapes=[pltpu.VMEM((LANES, LANES), F32)],
        compiler_params=_cparams(("parallel", "arbitrary")),
    )(qa, ka, va, states, do)


def _seg_mean_matrix():
    r = lax.broadcasted_iota(jnp.int32, (GROUP, GROUP), 0)
    c = lax.broadcasted_iota(jnp.int32, (GROUP, GROUP), 1)
    return jnp.where((r // HEAD) == (c // HEAD), 1.0 / HEAD, 0.0).astype(F32)


def _sigmoid(x):
    return 1.0 / (1.0 + jnp.exp(-x))


def _mix_post(oa, ob, oc, od, proj, g):
    s = oa.shape[0]
    tr = _tile(s, 256)

    def body(a_ref, b_ref, c_ref, d_ref, rg_ref, g_ref, o_ref):
        gv = g_ref[...]
        o_ref[:, 0:GROUP] = _rms(a_ref[...], gv[:, 0:GROUP]).astype(BF16)
        o_ref[:, GROUP:2 * GROUP] = _rms(b_ref[...], gv[:, GROUP:2 * GROUP]).astype(BF16)
        seg = _seg_mean_matrix()
        c = c_ref[...]
        cen = c - _dot_exact(c, seg)
        n = cen * lax.rsqrt(_dot_exact(cen * cen, seg) + EPS)
        rg = rg_ref[...]
        o_ref[:, 2 * GROUP:3 * GROUP] = (n * gv[:, 2 * GROUP:3 * GROUP] * (rg * _sigmoid(rg))).astype(BF16)
        o_ref[:, 3 * GROUP:] = _rms(d_ref[...], gv[:, 3 * GROUP:]).astype(BF16)

    blk = pl.BlockSpec((tr, GROUP), lambda i: (i, 0))
    return pl.pallas_call(
        body, name="mix_post", grid=(s // tr,),
        in_specs=[blk] * 4 + [pl.BlockSpec((tr, GROUP), lambda i: (i, OFF_RG // 2)), pl.BlockSpec((1, D_MODEL), lambda i: (0, 0))],
        out_specs=pl.BlockSpec((tr, D_MODEL), lambda i: (i, 0)), out_shape=jax.ShapeDtypeStruct((s, D_MODEL), BF16),
        compiler_params=_cparams(("parallel",)),
    )(oa, ob, oc, od, proj, g.reshape(1, D_MODEL))


def _mix_post_bwd(dmixed, oa, ob, oc, od, proj, g):
    s = oa.shape[0]
    tr = _tile(s, 256)

    def body(dm_ref, a_ref, b_ref, c_ref, d_ref, rg_ref, g_ref, da_ref, db_ref, dc_ref, dd_ref, drg_ref, dg_ref):
        @pl.when(pl.program_id(0) == 0)
        def _():
            dg_ref[...] = jnp.zeros_like(dg_ref)

        gv = g_ref[...]
        dm = dm_ref[...]
        for k, (x_ref, dx_ref) in enumerate(((a_ref, da_ref), (b_ref, db_ref), (None, None), (d_ref, dd_ref))):
            if x_ref is None:
                continue
            cols = slice(k * GROUP, (k + 1) * GROUP)
            dx, gterm = _rms_bwd(x_ref[...], gv[:, cols], dm[:, cols])
            dx_ref[...] = dx
            dg_ref[:, cols] += jnp.sum(gterm, axis=0, keepdims=True)
        cols = slice(2 * GROUP, 3 * GROUP)
        seg = _seg_mean_matrix()
        c = c_ref[...]
        cen = c - _dot_exact(c, seg)
        rstd = lax.rsqrt(_dot_exact(cen * cen, seg) + EPS)
        n = cen * rstd
        rg = rg_ref[...]
        sg = _sigmoid(rg)
        gate = rg * sg
        dy = dm[:, cols]
        gc = gv[:, cols]
        dn = dy * gc * gate
        dg_ref[:, cols] += jnp.sum(dy * n * gate, axis=0, keepdims=True)
        drg_ref[...] = (dy * n * gc * (sg * (1.0 + rg * (1.0 - sg)))).astype(BF16)
        dc_ref[...] = rstd * (dn - _dot_exact(dn, seg) - n * _dot_exact(dn * n, seg))

    blk = pl.BlockSpec((tr, GROUP), lambda i: (i, 0))
    gsp = pl.BlockSpec((1, D_MODEL), lambda i: (0, 0))
    return pl.pallas_call(
        body, name="mix_post_bwd", grid=(s // tr,),
        in_specs=[pl.BlockSpec((tr, D_MODEL), lambda i: (i, 0))] + [blk] * 4 + [pl.BlockSpec((tr, GROUP), lambda i: (i, OFF_RG // 2)), gsp],
        out_specs=[blk] * 5 + [gsp],
        out_shape=[jax.ShapeDtypeStruct((s, GROUP), F32)] * 4 + [jax.ShapeDtypeStruct((s, GROUP), BF16), jax.ShapeDtypeStruct((1, D_MODEL), F32)],
        compiler_params=_cparams(("arbitrary",)),
    )(dmixed, oa, ob, oc, od, proj, g.reshape(1, D_MODEL))


def _pack_w_in(w):
    z = lambda n: jnp.zeros((w.shape[0], n), w.dtype)
    misc = jnp.concatenate([w[:, 768:772], z(KR_LANE - N_HEADS), w[:, 1156:1188], z(LANES - KR_LANE - ROPE_DIM)], axis=1)
    return jnp.concatenate([w[:, 0:768], w[:, 772:1028], w[:, 1188:2980], w[:, 1028:1156], misc], axis=1)


def _unpack_dw_in(d):
    m = OFF_MISC * LANES
    return jnp.concatenate([d[:, 0:768], d[:, m:m + N_HEADS], d[:, 768:1024], d[:, OFF_CKV * LANES:m],
                            d[:, m + KR_LANE:m + KR_LANE + ROPE_DIM], d[:, 1024:OFF_CKV * LANES]], axis=1)


def _pack_w_q(w):
    return jnp.pad(w.reshape(Q_RANK, N_HEADS, HEAD + ROPE_DIM), ((0, 0), (0, 0), (0, LANES - HEAD - ROPE_DIM))).reshape(Q_RANK, 4 * LANES)


def _unpack_dw_q(d):
    return d.reshape(Q_RANK, N_HEADS, LANES)[:, :, :HEAD + ROPE_DIM].reshape(Q_RANK, N_HEADS * (HEAD + ROPE_DIM))


def _pack_w_kv(w):
    w4 = w.reshape(KV_RANK, N_HEADS, 2 * HEAD)
    widen = lambda a: jnp.pad(a, ((0, 0), (0, 0), (0, LANES - HEAD))).reshape(KV_RANK, N_HEADS * LANES)
    return widen(w4[:, :, :HEAD]), widen(w4[:, :, HEAD:])


def _unpack_dw_kv(dk, dv):
    narrow = lambda a: a.reshape(KV_RANK, N_HEADS, LANES)[:, :, :HEAD]
    return jnp.concatenate([narrow(dk), narrow(dv)], axis=2).reshape(KV_RANK, 2 * N_HEADS * HEAD)


def _narrow_heads(a):
    return a.reshape(a.shape[0], N_HEADS, LANES)[:, :, :HEAD].reshape(a.shape[0], N_HEADS * HEAD)


def _widen_heads(a):
    return jnp.pad(a.reshape(a.shape[0], N_HEADS, HEAD), ((0, 0), (0, 0), (0, LANES - HEAD))).reshape(a.shape[0], N_HEADS * LANES)


def _layer_fwd(x, lw, tabs, tag, side=None, fox_side=None, late_weights=None):
    cos_m, sin_m, cos_r, sin_r = tabs
    h1 = _norm_fwd(x, lw["g_mix_pre"], name=tag + "pre_norm")
    proj, projb = _matmul(h1, lw["w_in"], name=tag + "in_proj", also_bf16=True)
    bias_row = jnp.pad(lw["b_forget"], (FF_LANE, LANES - N_HEADS - FF_LANE)).reshape(1, LANES)
    cum_col, cum_row = _fox_cum(proj, bias_row)
    oa, lse_a, *fox_carried = _mixer_fwd("fox", projb, OFF_FQ, projb, OFF_FK, projb, OFF_FV, cum_col=cum_col, cum_row=cum_row,
                                         side=fox_side)
    if late_weights is not None:
        lw = {**lw, **late_weights(fox_carried[0])}
    qm, km, vm, cqn, ckvn = _mla_prep(proj, cos_m, sin_m, lw["g_q_lora"], lw["g_kv_lora"], lw["wq"], lw["wk"], lw["wv"])
    ob_wide, lse_b = _mixer_fwd("mla", qm, 0, km, 0, vm, 0)
    ob = _narrow_heads(ob_wide)
    qr, kr = _ret_prep(proj, cos_r, sin_r)
    oc, ret_states = _ret_fwd(qr, kr, projb, OFF_RV)
    od, tot_d, *carried = _mixer_fwd("sb", projb, OFF_SQ, projb, OFF_SK, projb, OFF_SV, side=side)
    mixed = _mix_post(oa, ob, oc, od, proj, lw["g_mix_out"])
    mix = _matmul(mixed, lw["w_out"], name=tag + "out_proj")
    x1 = _norm_fwd(mix, lw["g_mix_post"], name=tag + "mix_post_norm", resid=x, out_dtype=F32)
    h2 = _norm_fwd(x1, lw["g_ffn_pre"], name=tag + "ffn_pre_norm")
    u = _matmul(h2, lw["w_ffn_up"], name=tag + "ffn_up", relu2=True, out_dtype=BF16)
    f = _matmul(u, lw["w_ffn_down"], name=tag + "ffn_down")
    x2 = _norm_fwd(f, lw["g_ffn_post"], name=tag + "ffn_post_norm", resid=x1, out_dtype=F32)
    saved = dict(x=x, h1=h1, proj=proj, projb=projb, bias_row=bias_row, cum_col=cum_col, cum_row=cum_row, oa=oa, lse_a=lse_a,
                 qm=qm, km=km, vm=vm, cqn=cqn, ckvn=ckvn, ob=ob, ob_wide=ob_wide, lse_b=lse_b, qr=qr, kr=kr, ret_states=ret_states, oc=oc, od=od, tot_d=tot_d, mixed=mixed,
                 mix=mix, x1=x1, h2=h2, u=u, f=f)
    return x2, saved, lw, (carried[0] if carried else None)


def _layer_bwd(dx2, lw, sv, tabs, tag, side=None):
    cos_m, sin_m, cos_r, sin_r = tabs
    g = {}
    df, g["g_ffn_post"] = _norm_bwd(sv["f"], lw["g_ffn_post"], dx2, name=tag + "ffn_post_norm_bwd", out_dtype=BF16)
    du_pre = _matmul(df, lw["w_ffn_down"], name=tag + "ffn_down_dx", tb=True, out_dtype=BF16, relu2_of=sv["u"])
    g["w_ffn_down"] = _matmul(sv["u"], df, name=tag + "ffn_down_dw", ta=True)
    dh2 = _matmul(du_pre, lw["w_ffn_up"], name=tag + "ffn_up_dx", tb=True)
    g["w_ffn_up"] = _matmul(sv["h2"], du_pre, name=tag + "ffn_up_dw", ta=True)
    dx1, g["g_ffn_pre"] = _norm_bwd(sv["x1"], lw["g_ffn_pre"], dh2, name=tag + "ffn_pre_norm_bwd", add=dx2)
    dmix, g["g_mix_post"] = _norm_bwd(sv["mix"], lw["g_mix_post"], dx1, name=tag + "mix_post_norm_bwd", out_dtype=BF16)
    dmixed = _matmul(dmix, lw["w_out"], name=tag + "out_proj_dx", tb=True)
    g["w_out"] = _matmul(sv["mixed"], dmix, name=tag + "out_proj_dw", ta=True)
    proj, projb = sv["proj"], sv["projb"]
    doa, dob, doc, dod, drg, g["g_mix_out"] = _mix_post_bwd(dmixed, sv["oa"], sv["ob"], sv["oc"], sv["od"], proj, lw["g_mix_out"])
    dfq, dfk, dfv, dck, drs = _mixer_bwd("fox", projb, OFF_FQ, projb, OFF_FK, projb, OFF_FV, sv["oa"], doa, stat=sv["lse_a"],
                                         cum_col=sv["cum_col"], cum_row=sv["cum_row"])
    dqm, dkm, dvm = _mixer_bwd("mla", sv["qm"], 0, sv["km"], 0, sv["vm"], 0, sv["ob_wide"], _widen_heads(dob), stat=sv["lse_b"])
    dcq, dckv, dkr, dwq, dwk, dwv, g["g_q_lora"], g["g_kv_lora"] = _mla_prep_bwd(
        dqm, dkm, dvm, proj, sv["cqn"], sv["ckvn"], cos_m, sin_m, lw["g_q_lora"], lw["g_kv_lora"], lw["wq"], lw["wk"], lw["wv"])
    dqr, dkr_ret, drv = _ret_bwd(sv["qr"], sv["kr"], projb, OFF_RV, sv["ret_states"], doc)
    drq, drk = _ret_prep_bwd(dqr, dkr_ret, cos_r, sin_r)
    if callable(side):
        side = side(g)
    dsq, dsk, dsv, *carried = _mixer_bwd("sb", projb, OFF_SQ, projb, OFF_SK, projb, OFF_SV, sv["od"], dod, stat=sv["tot_d"], side=side)
    dmisc, db_row = _fox_gate_bwd(dck, drs, proj, sv["bias_row"], dkr)
    b = lambda a: a.astype(BF16)
    dproj = jnp.concatenate([b(dfq), b(dfk), b(dfv), dcq, drq, drk, b(drv), drg, b(dsq), b(dsk), b(dsv), dckv, dmisc], axis=1)
    dh1 = _matmul(dproj, lw["w_in"], name=tag + "in_proj_dx", tb=True)
    g["w_in"] = _matmul(sv["h1"], dproj, name=tag + "in_proj_dw", ta=True)
    dx, g["g_mix_pre"] = _norm_bwd(sv["x"], lw["g_mix_pre"], dh1, name=tag + "pre_norm_bwd", add=dx1)
    g["b_forget"] = db_row[0, FF_LANE:FF_LANE + N_HEADS]
    g["wq"], g["wk"], g["wv"] = dwq, dwk, dwv
    return dx, g, (carried[0] if carried else None)


def _local_step(x, positions, layers, target):
    s = x.shape[0]
    tabs = _rope_tables(positions.reshape(s, 1))
    saved = []
    for li, lw in enumerate(layers):
        x, sv, _, _ = _layer_fwd(x, lw, tabs, "l%d_" % li)
        saved.append(sv)
    loss_row, dx = _loss_head(x, target)
    grads = [None] * len(layers)
    for li in reversed(range(len(layers))):
        dx, grads[li], _ = _layer_bwd(dx, layers[li], saved[li], tabs, "l%d_" % li)
    return loss_row[0, 0], dx, grads


def _adamw(w, g, m, v, *, name):
    r, c = w.shape
    tr = 256 if r % 256 == 0 else r
    blk = pl.BlockSpec((tr, c), lambda i: (i, 0))
    c1 = 1.0 - ADAM_B1 ** ADAM_STEP
    c2 = 1.0 - ADAM_B2 ** ADAM_STEP

    def body(w_ref, g_ref, m_ref, v_ref, d_ref, mo_ref, vo_ref):
        gv = g_ref[...]
        mn = ADAM_B1 * m_ref[...] + (1.0 - ADAM_B1) * gv
        vn = ADAM_B2 * v_ref[...] + (1.0 - ADAM_B2) * jnp.square(gv)
        mo_ref[...] = mn
        vo_ref[...] = vn
        d_ref[...] = -ADAM_LR * ((mn / c1) / (jnp.sqrt(vn / c2) + ADAM_EPS) + ADAM_WD * w_ref[...])

    return pl.pallas_call(
        body, name=name, grid=(r // tr,), in_specs=[blk] * 4, out_specs=[blk] * 3,
        out_shape=[jax.ShapeDtypeStruct((r, c), F32)] * 3, compiler_params=_cparams(("parallel",)),
    )(w, g, m, v)


BIG = ("w_in", "w_q_up", "w_kv_up", "w_out", "w_ffn_up", "w_ffn_down")
SMALL = ("g_mix_pre", "b_forget", "g_q_lora", "g_kv_lora", "g_mix_out", "g_mix_post", "g_ffn_pre", "g_ffn_post")
N_CHIPS = 4
ANY = pl.BlockSpec(memory_space=pl.ANY)


def _mesh_pos():
    return lax.axis_index("x"), lax.axis_index("y"), lax.axis_index("c")


def _other_chips(x, y):
    return [(1 - x, y), (x, 1 - y), (1 - x, 1 - y)]


def _rows_half(ref, half):
    h = ref.shape[-2] // 2
    return ref.at[(slice(None),) * (len(ref.shape) - 2) + (pl.ds(half * h, h), slice(None))]


def _remote(src, dst, send_sem, recv_sem, device):
    return pltpu.make_async_remote_copy(src_ref=src, dst_ref=dst, send_sem=send_sem, recv_sem=recv_sem, device_id=device,
                                        device_id_type=MESH)


def _comm_call(body, name, args, out_shape, n_sems):
    return pl.pallas_call(
        body, name=name, in_specs=[ANY] * len(args), out_specs=[ANY] * len(out_shape), out_shape=out_shape,
        scratch_shapes=[pltpu.SemaphoreType.DMA((n_sems,)), pltpu.SemaphoreType.DMA((n_sems,))],
        compiler_params=pltpu.CompilerParams(has_side_effects=True),
    )(*args)


def _run_side_job(side, name):
    si = len(side.inputs)

    def body(*refs):
        args = (refs[:si], refs[si:-2], refs[-2], refs[-1])
        sends = side.sends(*args)
        for cp in sends:
            cp.start()
        for cp in side.recvs(*args):
            cp.wait_recv()
        for cp in sends:
            cp.wait_send()

    return _comm_call(body, name, side.inputs, side.out_shape, side.n_sems)


def _gather_job(shards):
    n = len(shards)

    def copies(own_block, ins, outs, send_sems, recv_sems):
        x, y, c = _mesh_pos()
        return [_remote(_rows_half(ins[t], c), _rows_half(outs[t].at[2 * x + y if own_block else 2 * px + py], c),
                        send_sems.at[3 * t + j], recv_sems.at[3 * t + j], (px, py, c))
                for t in range(n) for j, (px, py) in enumerate(_other_chips(x, y))]

    return _SideJob(shards, [jax.ShapeDtypeStruct((N_CHIPS,) + a.shape, a.dtype) for a in shards], 3 * n,
                    functools.partial(copies, True), functools.partial(copies, False))


def _forward_halves(gathered):
    n = len(gathered)

    def body(*refs):
        bufs, send_sems, recv_sems = refs[n:2 * n], refs[-2], refs[-1]
        x, y, c = _mesh_pos()

        def d2d(t, j, block, half):
            region = _rows_half(bufs[t].at[block], half)
            return _remote(region, region, send_sems.at[3 * t + j], recv_sems.at[3 * t + j], (x, y, 1 - c))

        peers = list(enumerate(_other_chips(x, y)))
        sends = [d2d(t, j, 2 * px + py, c) for t in range(n) for j, (px, py) in peers]
        for cp in sends:
            cp.start()
        for t in range(n):
            for j, (px, py) in peers:
                d2d(t, j, 2 * px + py, 1 - c).wait_recv()
        for cp in sends:
            cp.wait_send()

    return pl.pallas_call(
        body, name="gather_forward", in_specs=[ANY] * n, out_specs=[ANY] * n,
        out_shape=[jax.ShapeDtypeStruct(g.shape, g.dtype) for g in gathered], input_output_aliases={t: t for t in range(n)},
        scratch_shapes=[pltpu.SemaphoreType.DMA((3 * n,)), pltpu.SemaphoreType.DMA((3 * n,))],
        compiler_params=pltpu.CompilerParams(has_side_effects=True),
    )(*gathered)


def _exchange_halves(gs):
    n = len(gs)

    def body(*refs):
        ins, outs, send_sems, recv_sems = refs[:n], refs[n:2 * n], refs[2 * n], refs[2 * n + 1]
        x, y, c = _mesh_pos()
        cps = [_remote(_rows_half(ins[t], 1 - c), outs[t], send_sems.at[t], recv_sems.at[t], (x, y, 1 - c)) for t in range(n)]
        for cp in cps:
            cp.start()
        for cp in cps:
            cp.wait_recv()
        for cp in cps:
            cp.wait_send()

    out_shape = [jax.ShapeDtypeStruct(g.shape[:2] + (g.shape[2] // 2, g.shape[3]), g.dtype) for g in gs]
    return _comm_call(body, "grad_pair_exchange", gs, out_shape, n)


def _pair_add(g, r, c_idx, *, name):
    nb, d, rows, cols = g.shape
    h = rows // 2
    tr = min(h, 512)
    nt = h // tr

    def body(c_ref, g_ref, r_ref, p_ref, pb_ref):
        s = g_ref[...] + r_ref[...]
        p_ref[...] = s
        pb_ref[...] = s.astype(BF16)

    blk = pl.BlockSpec((1, 1, tr, cols), lambda k, l, i, c_ref: (k, l, i, 0))
    return pl.pallas_call(
        body, name=name,
        grid_spec=pltpu.PrefetchScalarGridSpec(
            num_scalar_prefetch=1, grid=(nb, d, nt),
            in_specs=[pl.BlockSpec((1, 1, tr, cols), lambda k, l, i, c_ref: (k, l, c_ref[0] * nt + i, 0)), blk],
            out_specs=[blk, blk]),
        out_shape=[jax.ShapeDtypeStruct((nb, d, h, cols), F32), jax.ShapeDtypeStruct((nb, d, h, cols), BF16)],
        compiler_params=_cparams(("parallel", "parallel", "parallel")),
    )(c_idx, g, r)


def _exchange_chips_job(pbs):
    n = len(pbs)

    def copies(ins, outs, send_sems, recv_sems):
        x, y, c = _mesh_pos()
        return [_remote(ins[t].at[2 * px + py], outs[t].at[j], send_sems.at[3 * t + j], recv_sems.at[3 * t + j], (px, py, c))
                for t in range(n) for j, (px, py) in enumerate(_other_chips(x, y))]

    return _SideJob(pbs, [jax.ShapeDtypeStruct((3,) + p.shape[1:], p.dtype) for p in pbs], 3 * n, copies, copies)


def _chip_add(p, r, k_idx, *, name):
    _, d, h, cols = p.shape
    tr = min(h, 512)
    nt = h // tr

    def body(k_ref, p_ref, r_ref, o_ref):
        o_ref[0] = ((p_ref[0, 0] + r_ref[0, 0].astype(F32)) + r_ref[1, 0].astype(F32)) + r_ref[2, 0].astype(F32)

    return pl.pallas_call(
        body, name=name,
        grid_spec=pltpu.PrefetchScalarGridSpec(
            num_scalar_prefetch=1, grid=(d, nt),
            in_specs=[pl.BlockSpec((1, 1, tr, cols), lambda l, i, k_ref: (k_ref[0], l, i, 0)),
                      pl.BlockSpec((3, 1, tr, cols), lambda l, i, k_ref: (0, l, i, 0))],
            out_specs=pl.BlockSpec((1, tr, cols), lambda l, i, k_ref: (l, i, 0))),
        out_shape=jax.ShapeDtypeStruct((d, h, cols), F32), compiler_params=_cparams(("parallel", "parallel")),
    )(k_idx, p, r)


def _share_halves(qs):
    n = len(qs)

    def body(*refs):
        ins, outs, send_sems, recv_sems = refs[:n], refs[n:2 * n], refs[2 * n], refs[2 * n + 1]
        x, y, c = _mesh_pos()
        cps = [_remote(ins[t], outs[t], send_sems.at[t], recv_sems.at[t], (x, y, 1 - c)) for t in range(n)]
        for cp in cps:
            cp.start()
        for cp in cps:
            cp.wait_recv()
        for cp in cps:
            cp.wait_send()

    return _comm_call(body, "grad_pair_share", qs, [jax.ShapeDtypeStruct(q.shape, q.dtype) for q in qs], n)


def _all_reduce_small(v):
    r, cols = v.shape
    n_dev = 8

    def body(v_ref, o_ref, buf, send_sems, recv_sems):
        x, y, c = _mesh_pos()
        me = 4 * x + 2 * y + c
        buf[me] = v_ref[...]

        def peer(j):
            return (1 - x if j & 4 else x, 1 - y if j & 2 else y, 1 - c if j & 1 else c)

        def copy(j, slot):
            return pltpu.make_async_remote_copy(src_ref=v_ref, dst_ref=buf.at[slot], send_sem=send_sems.at[j - 1],
                                                recv_sem=recv_sems.at[j - 1], device_id=peer(j), device_id_type=MESH)

        sends = [copy(j, me) for j in range(1, n_dev)]
        for cp in sends:
            cp.start()
        for j in range(1, n_dev):
            px, py, pc = peer(j)
            copy(j, 4 * px + 2 * py + pc).wait_recv()
        for cp in sends:
            cp.wait_send()
        acc = buf[0]
        for d in range(1, n_dev):
            acc = acc + buf[d]
        o_ref[...] = acc

    vm = pl.BlockSpec(memory_space=pltpu.VMEM)
    return pl.pallas_call(
        body, name="small_all_reduce", in_specs=[vm], out_specs=vm, out_shape=jax.ShapeDtypeStruct((r, cols), F32),
        scratch_shapes=[pltpu.VMEM((n_dev, r, cols), F32), pltpu.SemaphoreType.DMA((n_dev - 1,)), pltpu.SemaphoreType.DMA((n_dev - 1,))],
        compiler_params=pltpu.CompilerParams(has_side_effects=True),
    )(v)


_COL_SHARDED = ("w_in", "w_q_up", "w_kv_up", "w_ffn_up")


def _shard_cols(blocks, a, b):
    c = blocks.shape[-1]
    out = []
    while a < b:
        k = a // c
        hi = min(b, (k + 1) * c)
        out.append(blocks[k][:, a - k * c:hi - k * c])
        a = hi
    return out


def _pack_w_in_shards(blocks):
    z = lambda n: [jnp.zeros((blocks.shape[1], n), blocks.dtype)]
    cols = lambda a, b: _shard_cols(blocks, a, b)
    return jnp.concatenate(cols(0, 768) + cols(772, 1028) + cols(1188, 2980) + cols(1028, 1156) + cols(768, 772)
                           + z(KR_LANE - N_HEADS) + cols(1156, 1188) + z(LANES - KR_LANE - ROPE_DIM), axis=1)


def _whole_layer(name, blocks):
    if name in _COL_SHARDED:
        return jnp.concatenate([blocks[k] for k in range(N_CHIPS)], axis=1)
    return blocks.reshape(N_CHIPS * blocks.shape[1], blocks.shape[2])


def _split_layer(name, whole):
    if name in _COL_SHARDED:
        c = whole.shape[1] // N_CHIPS
        return jnp.stack([whole[:, k * c:(k + 1) * c] for k in range(N_CHIPS)])
    return whole.reshape(N_CHIPS, whole.shape[0] // N_CHIPS, whole.shape[1])


def _small_to_rows(d):
    v = jnp.concatenate([d[k].astype(F32).reshape(-1) for k in SMALL])
    rows = -(-v.shape[0] // (8 * LANES)) * 8
    return jnp.pad(v, (0, rows * LANES - v.shape[0])).reshape(rows, LANES)


def _small_from_rows(rows, shapes):
    v = rows.reshape(-1)
    out, o = {}, 0
    for k in SMALL:
        sz = int(np.prod(shapes[k]))
        out[k] = v[o:o + sz].reshape(shapes[k])
        o += sz
    return out


_ARG_NAMES = ("x", "positions", "g_mix_pre", "w_in", "b_forget", "g_q_lora", "w_q_up", "g_kv_lora", "w_kv_up", "g_mix_out", "w_out",
              "g_mix_post", "g_ffn_pre", "w_ffn_up", "w_ffn_down", "g_ffn_post")
_WEIGHTS = _ARG_NAMES[2:]


def kernel(x, positions, g_mix_pre, w_in, b_forget, g_q_lora, w_q_up, g_kv_lora, w_kv_up, g_mix_out, w_out, g_mix_post, g_ffn_pre, w_ffn_up, w_ffn_down, g_ffn_post, loss_target, m_g_mix_pre, m_w_in, m_b_forget, m_g_q_lora, m_w_q_up, m_g_kv_lora, m_w_kv_up, m_g_mix_out, m_w_out, m_g_mix_post, m_g_ffn_pre, m_w_ffn_up, m_w_ffn_down, m_g_ffn_post, v_g_mix_pre, v_w_in, v_b_forget, v_g_q_lora, v_w_q_up, v_g_kv_lora, v_w_kv_up, v_g_mix_out, v_w_out, v_g_mix_post, v_g_ffn_pre, v_w_ffn_up, v_w_ffn_down, v_g_ffn_post):
    w = dict(g_mix_pre=g_mix_pre, w_in=w_in, b_forget=b_forget, g_q_lora=g_q_lora, w_q_up=w_q_up, g_kv_lora=g_kv_lora, w_kv_up=w_kv_up,
             g_mix_out=g_mix_out, w_out=w_out, g_mix_post=g_mix_post, g_ffn_pre=g_ffn_pre, w_ffn_up=w_ffn_up, w_ffn_down=w_ffn_down,
             g_ffn_post=g_ffn_post)
    m = dict(g_mix_pre=m_g_mix_pre, w_in=m_w_in, b_forget=m_b_forget, g_q_lora=m_g_q_lora, w_q_up=m_w_q_up, g_kv_lora=m_g_kv_lora,
             w_kv_up=m_w_kv_up, g_mix_out=m_g_mix_out, w_out=m_w_out, g_mix_post=m_g_mix_post, g_ffn_pre=m_g_ffn_pre,
             w_ffn_up=m_w_ffn_up, w_ffn_down=m_w_ffn_down, g_ffn_post=m_g_ffn_post)
    v = dict(g_mix_pre=v_g_mix_pre, w_in=v_w_in, b_forget=v_b_forget, g_q_lora=v_g_q_lora, w_q_up=v_w_q_up, g_kv_lora=v_g_kv_lora,
             w_kv_up=v_w_kv_up, g_mix_out=v_g_mix_out, w_out=v_w_out, g_mix_post=v_g_mix_post, g_ffn_pre=v_g_ffn_pre,
             w_ffn_up=v_w_ffn_up, w_ffn_down=v_w_ffn_down, g_ffn_post=v_g_ffn_post)
    shard_shapes = {k: w[k].shape for k in BIG}
    small_shapes = {k: w[k].shape for k in SMALL}
    c_idx = lax.axis_index("c").astype(jnp.int32).reshape(1)
    k_idx = (2 * lax.axis_index("x") + lax.axis_index("y")).astype(jnp.int32).reshape(1)

    mine = 2 * lax.axis_index("x") + lax.axis_index("y")
    first_core = lax.axis_index("c") == 0
    shards_b = [{k: w[k][l:l + 1].astype(BF16) for k in BIG} for l in range(DEPTH)]
    gains = [dict(g_mix_pre=g_mix_pre[l], b_forget=b_forget[l], g_q_lora=g_q_lora[l], g_kv_lora=g_kv_lora[l], g_mix_out=g_mix_out[l],
                  g_mix_post=g_mix_post[l], g_ffn_pre=g_ffn_pre[l], g_ffn_post=g_ffn_post[l]) for l in range(DEPTH)]
    FIRST, LATER = ("w_in", "w_q_up", "w_kv_up"), ("w_out", "w_ffn_up", "w_ffn_down")
    EARLY_GRADS, LATE_GRADS = ("w_ffn_down", "w_ffn_up", "w_out"), ("w_in", "w_q_up", "w_kv_up")

    def gather_job(l, names):
        return _gather_job([shards_b[l][k] for k in names])

    def weights_of(l, names, gathered):
        four = {k: lax.dynamic_update_slice(g, shards_b[l][k][None], (mine, 0, 0, 0))[:, 0]
                for k, g in zip(names, _forward_halves(gathered))}
        out = {}
        for k in names:
            if k == "w_in":
                out["w_in"] = _pack_w_in_shards(four[k])
            elif k == "w_q_up":
                out["wq"] = _pack_w_q(_whole_layer(k, four[k]))
            elif k == "w_kv_up":
                out["wk"], out["wv"] = _pack_w_kv(_whole_layer(k, four[k]))
            else:
                out[k] = _whole_layer(k, four[k])
        return out

    def pair_sums(names, g):
        whole = dict(w_in=lambda: _unpack_dw_in(g["w_in"]), w_q_up=lambda: _unpack_dw_q(g["wq"]),
                     w_kv_up=lambda: _unpack_dw_kv(g["wk"], g["wv"]), w_out=lambda: g["w_out"], w_ffn_up=lambda: g["w_ffn_up"],
                     w_ffn_down=lambda: g["w_ffn_down"])
        blocks = [_split_layer(k, whole[k]())[:, None] for k in names]
        theirs = _exchange_halves(blocks)
        return [_pair_add(b, r, c_idx, name="grad_pair_add_" + k) for k, b, r in zip(names, blocks, theirs)]

    def exchange_job(*pairs):
        return _exchange_chips_job([pb for pair in pairs for (_, pb) in pair])

    def finish_grads(names, pair, partial):
        half = [_chip_add(p, r, k_idx, name="grad_chip_add_" + k) for k, (p, _), r in zip(names, pair, partial)]
        return {k: jnp.where(first_core, jnp.concatenate([q, s], axis=1), jnp.concatenate([s, q], axis=1))
                for k, q, s in zip(names, half, _share_halves(half))}

    seq = x.shape[1]
    tabs = _rope_tables(positions[0].reshape(seq, 1))
    first0 = weights_of(0, FIRST, _run_side_job(gather_job(0, FIRST), "gather_weights_l0"))
    x1, saved0, lw0, gathered1 = _layer_fwd(x[0], {**gains[0], **first0}, tabs, "l0_", fox_side=gather_job(0, LATER),
                                            late_weights=lambda got: weights_of(0, LATER, got), side=gather_job(1, BIG))
    lw1 = {**gains[1], **weights_of(1, BIG, gathered1)}
    x2, saved1, _, _ = _layer_fwd(x1, lw1, tabs, "l1_")
    loss_row, dx = _loss_head(x2, loss_target[0])
    loss = lax.psum(loss_row[0, 0], ("x", "y", "c"))
    dx, grads1, _ = _layer_bwd(dx, lw1, saved1, tabs, "l1_")
    pair1 = pair_sums(BIG, grads1)
    early0 = []

    def beside_l0_backward(g):
        early0.extend(pair_sums(EARLY_GRADS, g))
        return exchange_job(pair1, early0)

    dx, grads0, partial = _layer_bwd(dx, lw0, saved0, tabs, "l0_", side=beside_l0_backward)
    big1 = finish_grads(BIG, pair1, partial[:len(BIG)])
    big0 = finish_grads(EARLY_GRADS, early0, partial[len(BIG):])
    late0 = pair_sums(LATE_GRADS, grads0)
    big0.update(finish_grads(LATE_GRADS, late0, _run_side_job(exchange_job(late0), "grad_chip_exchange_l0")))
    g_big = {k: jnp.concatenate([big0[k], big1[k]], axis=0) for k in BIG}
    grads = [grads0, grads1]

    g_small_local = {k: jnp.stack([grads[l][k].reshape(small_shapes[k][1:]) for l in range(DEPTH)]) for k in SMALL}
    g_small = _small_from_rows(_all_reduce_small(_small_to_rows(g_small_local)), small_shapes)

    g_all = {**g_big, **g_small}
    delta, new_m, new_v = {}, {}, {}
    for k in BIG:
        d, r, c = shard_shapes[k]
        two_d = lambda a: a.reshape(d * r, c)
        dk, mk, vk = _adamw(two_d(w[k]), two_d(g_all[k]), two_d(m[k]), two_d(v[k]), name="adamw_" + k)
        delta[k], new_m[k], new_v[k] = dk.reshape(d, r, c), mk.reshape(d, r, c), vk.reshape(d, r, c)
    ds, ms, vs = _adamw(_small_to_rows(w), _small_to_rows(g_small), _small_to_rows(m), _small_to_rows(v), name="adamw_small")
    delta.update(_small_from_rows(ds, small_shapes))
    new_m.update(_small_from_rows(ms, small_shapes))
    new_v.update(_small_from_rows(vs, small_shapes))

    grad_x = dx.reshape(x.shape)
    return (loss, grad_x, *[g_all[k] for k in _WEIGHTS], *[delta[k] for k in _WEIGHTS], *[new_m[k] for k in _WEIGHTS],
            *[new_v[k] for k in _WEIGHTS])
```

```python
import functools
import math

import numpy as np
import jax
import jax.numpy as jnp
from jax import lax
from jax.experimental import pallas as pl
from jax.experimental.pallas import tpu as pltpu

F32 = jnp.float32
BF16 = jnp.bfloat16
MESH = pl.DeviceIdType.MESH

D_MODEL = 1024
DEPTH = 2
CHUNK = 64
GROUP = 256
HEAD = 64
N_HEADS = 4
Q_RANK = 256
KV_RANK = 128
ROPE_DIM = 32
D_FF = 4096
D_IN = 2980
D_INP = 3072
ROPE_BASE = 10000.0
EPS = 1e-6
LANES = 128
TQ = 128
NEG = -1e30

ADAM_LR, ADAM_B1, ADAM_B2, ADAM_EPS, ADAM_WD, ADAM_STEP = 0.001, 0.9, 0.999, 1e-08, 0.01, 10

OFF_FQ, OFF_FK, OFF_FV, OFF_CQ = 0, 2, 4, 6
OFF_RQ, OFF_RK, OFF_RV, OFF_RG = 8, 10, 12, 14
OFF_SQ, OFF_SK, OFF_SV = 16, 18, 20
OFF_CKV, OFF_MISC = 22, 23
FF_LANE, KR_LANE = 0, 64

VMEM_LIMIT = 56 * 1024 * 1024


def _tile(dim, pref):
    return pref if dim % pref == 0 else dim


def _cparams(sem, vmem=None):
    return pltpu.CompilerParams(dimension_semantics=sem, vmem_limit_bytes=vmem or VMEM_LIMIT)


def _dot(a, b):
    return jnp.dot(a, b, preferred_element_type=F32)


def _dot_nt(a, b):
    return lax.dot_general(a, b, (((1,), (1,)), ((), ())), preferred_element_type=F32)


def _dot_tn(a, b):
    return lax.dot_general(a, b, (((0,), (0,)), ((), ())), preferred_element_type=F32)


def _dot_exact(a, b):
    return jnp.dot(a, b, precision=lax.Precision.HIGHEST, preferred_element_type=F32)


def _matmul(a, b, *, name, ta=False, tb=False, out_dtype=F32, tm=1024, tn=1024, tk=1024,
            relu2=False, relu2_of=None, also_bf16=False):
    if ta:
        kdim, m = a.shape
    else:
        m, kdim = a.shape
    n = b.shape[0] if tb else b.shape[1]
    tm, tn, tk = _tile(m, tm), _tile(n, tn), _tile(kdim, tk)
    nk = kdim // tk
    a_spec = pl.BlockSpec((tk, tm), lambda i, j, k: (k, i)) if ta else pl.BlockSpec((tm, tk), lambda i, j, k: (i, k))
    b_spec = pl.BlockSpec((tn, tk), lambda i, j, k: (j, k)) if tb else pl.BlockSpec((tk, tn), lambda i, j, k: (k, j))
    o_spec = pl.BlockSpec((tm, tn), lambda i, j, k: (i, j))
    two = also_bf16

    def body(*refs):
        refs = list(refs)
        a_ref, b_ref = refs[0], refs[1]
        e_ref = refs[2] if relu2_of is not None else None
        pos = 3 if relu2_of is not None else 2
        o_ref = refs[pos]
        o2_ref = refs[pos + 1] if two else None
        acc_ref = refs[-1]
        k = pl.program_id(2)
        av = a_ref[...].astype(BF16)
        bv = b_ref[...].astype(BF16)
        if ta:
            part = _dot_tn(av, bv)
        elif tb:
            part = _dot_nt(av, bv)
        else:
            part = _dot(av, bv)

        @pl.when(k == 0)
        def _():
            acc_ref[...] = part

        @pl.when(k > 0)
        def _():
            acc_ref[...] += part

        @pl.when(k == nk - 1)
        def _():
            r = acc_ref[...]
            if relu2_of is not None:
                r = r * (2.0 * jnp.sqrt(e_ref[...].astype(F32)))
            if relu2:
                r = jnp.square(jnp.maximum(r, 0.0))
            o_ref[...] = r.astype(o_ref.dtype)
            if also_bf16:
                o2_ref[...] = r.astype(BF16)

    in_specs = [a_spec, b_spec]
    args = [a, b]
    if relu2_of is not None:
        in_specs.append(o_spec)
        args.append(relu2_of)
    out_shape = [jax.ShapeDtypeStruct((m, n), out_dtype)]
    out_specs = [o_spec]
    if two:
        out_shape.append(jax.ShapeDtypeStruct((m, n), BF16))
        out_specs.append(o_spec)
    res = pl.pallas_call(
        body, name=name, grid=(m // tm, n // tn, nk), in_specs=in_specs, out_specs=out_specs, out_shape=out_shape,
        scratch_shapes=[pltpu.VMEM((tm, tn), F32)],
        compiler_params=_cparams(("parallel", "parallel", "arbitrary")),
    )(*args)
    return res if two else res[0]


def _rms(x, g):
    r = lax.rsqrt(jnp.mean(x * x, axis=-1, keepdims=True) + EPS)
    return x * r * g


def _rms_bwd(x, g, dy):
    r = lax.rsqrt(jnp.mean(x * x, axis=-1, keepdims=True) + EPS)
    xh = x * r
    gdy = dy * g
    dx = r * (gdy - xh * jnp.mean(xh * gdy, axis=-1, keepdims=True))
    return dx, xh * dy


def _norm_fwd(x, g, *, name, resid=None, out_dtype=BF16):
    s, d = x.shape
    tr = _tile(s, 256)
    row = pl.BlockSpec((tr, d), lambda i: (i, 0))
    gsp = pl.BlockSpec((1, d), lambda i: (0, 0))

    def body(*refs):
        if resid is None:
            x_ref, g_ref, o_ref = refs
            o_ref[...] = _rms(x_ref[...], g_ref[...]).astype(o_ref.dtype)
        else:
            x_ref, g_ref, r_ref, o_ref = refs
            o_ref[...] = (r_ref[...] + _rms(x_ref[...], g_ref[...])).astype(o_ref.dtype)

    args = [x, g.reshape(1, d)] + ([] if resid is None else [resid])
    return pl.pallas_call(
        body, name=name, grid=(s // tr,), in_specs=[row, gsp] + ([] if resid is None else [row]),
        out_specs=row, out_shape=jax.ShapeDtypeStruct((s, d), out_dtype), compiler_params=_cparams(("parallel",)),
    )(*args)


def _norm_bwd(x, g, dy, *, name, add=None, out_dtype=F32):
    s, d = x.shape
    tr = _tile(s, 256)
    row = pl.BlockSpec((tr, d), lambda i: (i, 0))
    gsp = pl.BlockSpec((1, d), lambda i: (0, 0))

    def body(*refs):
        if add is None:
            x_ref, g_ref, dy_ref, dx_ref, dg_ref = refs
        else:
            x_ref, g_ref, dy_ref, add_ref, dx_ref, dg_ref = refs
        dx, gterm = _rms_bwd(x_ref[...], g_ref[...], dy_ref[...].astype(F32))
        if add is not None:
            dx = dx + add_ref[...]
        dx_ref[...] = dx.astype(dx_ref.dtype)

        @pl.when(pl.program_id(0) == 0)
        def _():
            dg_ref[...] = jnp.zeros_like(dg_ref)

        dg_ref[...] += jnp.sum(gterm, axis=0, keepdims=True)

    args = [x, g.reshape(1, d), dy] + ([] if add is None else [add])
    return pl.pallas_call(
        body, name=name, grid=(s // tr,), in_specs=[row, gsp, row] + ([] if add is None else [row]),
        out_specs=[row, gsp], out_shape=[jax.ShapeDtypeStruct((s, d), out_dtype), jax.ShapeDtypeStruct((1, d), F32)],
        compiler_params=_cparams(("arbitrary",)),
    )(*args)


def _loss_head(y, target):
    s, d = y.shape
    tr = _tile(s, 256)
    row = pl.BlockSpec((tr, d), lambda i: (i, 0))
    lsp = pl.BlockSpec((1, LANES), lambda i: (0, 0))

    def body(y_ref, t_ref, l_ref, dy_ref):
        e = y_ref[...] - t_ref[...]
        dy_ref[...] = e * (1.0 / d)

        @pl.when(pl.program_id(0) == 0)
        def _():
            l_ref[...] = jnp.zeros_like(l_ref)

        part = 0.5 * jnp.sum(jnp.mean(e * e, axis=-1, keepdims=True), axis=0, keepdims=True)
        l_ref[...] += jnp.broadcast_to(part, (1, LANES))

    return pl.pallas_call(
        body, name="loss_head", grid=(s // tr,), in_specs=[row, row], out_specs=[lsp, row],
        out_shape=[jax.ShapeDtypeStruct((1, LANES), F32), jax.ShapeDtypeStruct((s, d), F32)],
        compiler_params=_cparams(("arbitrary",)),
    )(y, target)


def _rope_tables(pos_col):
    s = pos_col.shape[0]
    tr = _tile(s, 512)
    f_mla = ROPE_BASE ** (-jnp.arange(ROPE_DIM // 2, dtype=F32) / (ROPE_DIM // 2))
    f_ret = ROPE_BASE ** (-jnp.arange(HEAD // 2, dtype=F32) / (HEAD // 2))
    fm = jnp.concatenate([jnp.zeros((64,), F32), f_mla, f_mla, jnp.zeros((32,), F32)]).reshape(1, LANES)
    fr = jnp.tile(jnp.concatenate([f_ret, f_ret]), 4).reshape(1, 2 * LANES)

    def body(p_ref, fm_ref, fr_ref, cm_ref, sm_ref, cr_ref, sr_ref):
        p = p_ref[...].astype(F32)
        am = p * fm_ref[...]
        ar = p * fr_ref[...]
        cm_ref[...] = jnp.cos(am)
        sm_ref[...] = jnp.sin(am)
        cr_ref[...] = jnp.cos(ar)
        sr_ref[...] = jnp.sin(ar)

    return pl.pallas_call(
        body, name="rope_tables", grid=(s // tr,),
        in_specs=[pl.BlockSpec((tr, 1), lambda i: (i, 0)), pl.BlockSpec((1, LANES), lambda i: (0, 0)),
                  pl.BlockSpec((1, 2 * LANES), lambda i: (0, 0))],
        out_specs=[pl.BlockSpec((tr, LANES), lambda i: (i, 0))] * 2 + [pl.BlockSpec((tr, 2 * LANES), lambda i: (i, 0))] * 2,
        out_shape=[jax.ShapeDtypeStruct((s, LANES), F32)] * 2 + [jax.ShapeDtypeStruct((s, 2 * LANES), F32)] * 2,
        compiler_params=_cparams(("parallel",)),
    )(pos_col, fm, fr)


def _lane(shape):
    return lax.broadcasted_iota(jnp.int32, shape, len(shape) - 1)


def _rot_mla(z):
    l = _lane(z.shape) % LANES
    n = z.shape[-1]
    return jnp.where(l < 80, -pltpu.roll(z, n - 16, 1), pltpu.roll(z, 16, 1))


def _rot_mla_t(y):
    l = _lane(y.shape) % LANES
    n = y.shape[-1]
    return jnp.where((l >= 64) & (l < 80), pltpu.roll(y, n - 16, 1),
                     jnp.where((l >= 80) & (l < 96), -pltpu.roll(y, 16, 1), 0.0))


def _rot_ret(z):
    l = _lane(z.shape) % HEAD
    n = z.shape[-1]
    return jnp.where(l < 32, -pltpu.roll(z, n - 32, 1), pltpu.roll(z, 32, 1))


def _rot_ret_t(y):
    l = _lane(y.shape) % HEAD
    n = y.shape[-1]
    return jnp.where(l < 32, pltpu.roll(y, n - 32, 1), -pltpu.roll(y, 32, 1))


def _log_sigmoid(x):
    return jnp.minimum(x, 0.0) - jnp.log1p(jnp.exp(-jnp.abs(x)))


def _fox_cum(proj, bias_row):
    s = proj.shape[0]
    nb = s // TQ

    def body(x_ref, b_ref, cc_ref, cr_ref, carry_ref):
        @pl.when(pl.program_id(0) == 0)
        def _():
            carry_ref[...] = jnp.zeros_like(carry_ref)

        ls = _log_sigmoid(x_ref[...] + b_ref[...])
        r = lax.broadcasted_iota(jnp.int32, (TQ, TQ), 0)
        c = lax.broadcasted_iota(jnp.int32, (TQ, TQ), 1)
        tri = (c <= r).astype(F32)
        cum = _dot_exact(tri, ls) + carry_ref[...]
        carry_ref[...] = cum[TQ - 1:TQ, :]
        cc_ref[...] = cum
        cr_ref[...] = cum.T[0:8, :]

    return pl.pallas_call(
        body, name="fox_cum", grid=(nb,),
        in_specs=[pl.BlockSpec((TQ, LANES), lambda i: (i, OFF_MISC)), pl.BlockSpec((1, LANES), lambda i: (0, 0))],
        out_specs=[pl.BlockSpec((TQ, LANES), lambda i: (i, 0)), pl.BlockSpec((8, TQ), lambda i: (0, i))],
        out_shape=[jax.ShapeDtypeStruct((s, LANES), F32), jax.ShapeDtypeStruct((8, s), F32)],
        scratch_shapes=[pltpu.VMEM((1, LANES), F32)],
        compiler_params=_cparams(("arbitrary",)),
    )(proj, bias_row)


def _fox_gate_bwd(dck, drs, proj, bias_row, dkr):
    s = proj.shape[0]
    nb = s // TQ

    def body(d_ref, r_ref, x_ref, b_ref, k_ref, o_ref, db_ref, carry_ref):
        @pl.when(pl.program_id(0) == 0)
        def _():
            carry_ref[...] = jnp.zeros_like(carry_ref)
            db_ref[...] = jnp.zeros_like(db_ref)

        rows = jnp.concatenate([d_ref[0], d_ref[1], jnp.zeros((TQ - 16, TQ), F32)], axis=0)
        t = rows.T
        l = _lane((TQ, LANES))
        r0, r1 = r_ref[0], r_ref[1]
        rsum = jnp.where(l == 0, r0[:, 0:1], jnp.where(l == 1, r0[:, HEAD:HEAD + 1],
                         jnp.where(l == 2, r1[:, 0:1], jnp.where(l == 3, r1[:, HEAD:HEAD + 1], 0.0))))
        dcum = rsum - jnp.where(l < 2, t, pltpu.roll(t, LANES - 6, 1))
        r = lax.broadcasted_iota(jnp.int32, (TQ, TQ), 0)
        c = lax.broadcasted_iota(jnp.int32, (TQ, TQ), 1)
        triu = (c >= r).astype(F32)
        rc = _dot_exact(triu, dcum) + carry_ref[...]
        carry_ref[...] = rc[0:1, :]
        f = x_ref[...] + b_ref[...]
        sig_neg = 1.0 / (1.0 + jnp.exp(f))
        df = jnp.where(l < N_HEADS, rc * sig_neg, 0.0)
        db_ref[...] += jnp.sum(df, axis=0, keepdims=True)
        o_ref[...] = (df + k_ref[...]).astype(o_ref.dtype)

    rev = lambda i: nb - 1 - i
    return pl.pallas_call(
        body, name="fox_gate_bwd", grid=(nb,),
        in_specs=[pl.BlockSpec((2, 8, TQ), lambda i: (0, 0, rev(i))), pl.BlockSpec((2, TQ, LANES), lambda i: (0, rev(i), 0)),
                  pl.BlockSpec((TQ, LANES), lambda i: (rev(i), OFF_MISC)),
                  pl.BlockSpec((1, LANES), lambda i: (0, 0)), pl.BlockSpec((TQ, LANES), lambda i: (rev(i), 0))],
        out_specs=[pl.BlockSpec((TQ, LANES), lambda i: (rev(i), 0)), pl.BlockSpec((1, LANES), lambda i: (0, 0))],
        out_shape=[jax.ShapeDtypeStruct((s, LANES), BF16), jax.ShapeDtypeStruct((1, LANES), F32)],
        scratch_shapes=[pltpu.VMEM((1, LANES), F32)],
        compiler_params=_cparams(("arbitrary",)),
    )(dck, drs, proj, bias_row, dkr)


def _mla_prep(proj, cos_m, sin_m, g_q, g_kv, wq, wk, wv):
    s = proj.shape[0]
    tr = _tile(s, 256)

    def body(cq_ref, ckv_ref, misc_ref, cos_ref, sin_ref, gq_ref, gkv_ref, wq_ref, wk_ref, wv_ref,
             q_ref, k_ref, v_ref, cqn_ref, ckvn_ref):
        cos4 = jnp.tile(cos_ref[...], (1, 4))
        sin4 = jnp.tile(sin_ref[...], (1, 4))
        cqn = _rms(cq_ref[...], gq_ref[...]).astype(BF16)
        ckvn = _rms(ckv_ref[...], gkv_ref[...]).astype(BF16)
        cqn_ref[...] = cqn
        ckvn_ref[...] = ckvn
        zq = _dot(cqn, wq_ref[...])
        q_ref[...] = (zq * cos4 + _rot_mla(zq) * sin4).astype(BF16)
        l = _lane((tr, LANES))
        kr = jnp.where((l >= KR_LANE) & (l < KR_LANE + ROPE_DIM), misc_ref[...], 0.0)
        zk = _dot(ckvn, wk_ref[...]) + jnp.tile(kr, (1, 4))
        k_ref[...] = (zk * cos4 + _rot_mla(zk) * sin4).astype(BF16)
        v_ref[...] = _dot(ckvn, wv_ref[...]).astype(BF16)

    full = lambda a: pl.BlockSpec(a.shape, lambda i: (0, 0))
    rowb = lambda w: pl.BlockSpec((tr, w), lambda i: (i, 0))
    gq2, gkv2 = g_q.reshape(1, Q_RANK), g_kv.reshape(1, KV_RANK)
    return pl.pallas_call(
        body, name="mla_prep", grid=(s // tr,),
        in_specs=[pl.BlockSpec((tr, 256), lambda i: (i, OFF_CQ // 2)), pl.BlockSpec((tr, LANES), lambda i: (i, OFF_CKV)),
                  pl.BlockSpec((tr, LANES), lambda i: (i, OFF_MISC)), rowb(LANES), rowb(LANES),
                  full(gq2), full(gkv2), full(wq), full(wk), full(wv)],
        out_specs=[rowb(512), rowb(512), rowb(512), rowb(256), rowb(128)],
        out_shape=[jax.ShapeDtypeStruct((s, 512), BF16), jax.ShapeDtypeStruct((s, 512), BF16), jax.ShapeDtypeStruct((s, 512), BF16),
                   jax.ShapeDtypeStruct((s, 256), BF16), jax.ShapeDtypeStruct((s, 128), BF16)],
        compiler_params=_cparams(("parallel",)),
    )(proj, proj, proj, cos_m, sin_m, gq2, gkv2, wq, wk, wv)


def _mla_prep_bwd(dq, dk, dv, proj, cqn, ckvn, cos_m, sin_m, g_q, g_kv, wq, wk, wv):
    s = proj.shape[0]
    tr = _tile(s, 256)

    def body(dq_ref, dk_ref, dv_ref, cq_ref, ckv_ref, cqn_ref, ckvn_ref, cos_ref, sin_ref, gq_ref, gkv_ref,
             wq_ref, wk_ref, wv_ref, dcq_ref, dckv_ref, dkr_ref, dwq_ref, dwk_ref, dwv_ref, dgq_ref, dgkv_ref):
        @pl.when(pl.program_id(0) == 0)
        def _():
            for r in (dwq_ref, dwk_ref, dwv_ref, dgq_ref, dgkv_ref):
                r[...] = jnp.zeros_like(r)

        cos4 = jnp.tile(cos_ref[...], (1, 4))
        sin4 = jnp.tile(sin_ref[...], (1, 4))
        dqv = dq_ref[...]
        dzq = dqv * cos4 + _rot_mla_t(dqv * sin4)
        dkv_ = dk_ref[...]
        dzk = dkv_ * cos4 + _rot_mla_t(dkv_ * sin4)
        l = _lane((tr, LANES))
        in_rope = (l >= KR_LANE) & (l < KR_LANE + ROPE_DIM)
        dkr = dzk[:, 0:128] + dzk[:, 128:256] + dzk[:, 256:384] + dzk[:, 384:512]
        dkr_ref[...] = jnp.where(in_rope, dkr, 0.0)
        dzq_b = dzq.astype(BF16)
        dzk_b = dzk.astype(BF16)
        dv_b = dv_ref[...].astype(BF16)
        dcqn = _dot_nt(dzq_b, wq_ref[...])
        dckvn = _dot_nt(dzk_b, wk_ref[...]) + _dot_nt(dv_b, wv_ref[...])
        dwq_ref[...] += _dot_tn(cqn_ref[...], dzq_b)
        dwk_ref[...] += _dot_tn(ckvn_ref[...], dzk_b)
        dwv_ref[...] += _dot_tn(ckvn_ref[...], dv_b)
        dcq, gq_term = _rms_bwd(cq_ref[...], gq_ref[...], dcqn)
        dckv, gkv_term = _rms_bwd(ckv_ref[...], gkv_ref[...], dckvn)
        dcq_ref[...] = dcq.astype(BF16)
        dckv_ref[...] = dckv.astype(BF16)
        dgq_ref[...] += jnp.sum(gq_term, axis=0, keepdims=True)
        dgkv_ref[...] += jnp.sum(gkv_term, axis=0, keepdims=True)

    full = lambda shp: pl.BlockSpec(shp, lambda i: (0, 0))
    rowb = lambda w: pl.BlockSpec((tr, w), lambda i: (i, 0))
    gq2, gkv2 = g_q.reshape(1, Q_RANK), g_kv.reshape(1, KV_RANK)
    return pl.pallas_call(
        body, name="mla_prep_bwd", grid=(s // tr,),
        in_specs=[rowb(512), rowb(512), rowb(512),
                  pl.BlockSpec((tr, 256), lambda i: (i, OFF_CQ // 2)), pl.BlockSpec((tr, LANES), lambda i: (i, OFF_CKV)),
                  rowb(256), rowb(128), rowb(LANES), rowb(LANES), full((1, Q_RANK)), full((1, KV_RANK)),
                  full(wq.shape), full(wk.shape), full(wv.shape)],
        out_specs=[rowb(256), rowb(128), rowb(128), full(wq.shape), full(wk.shape), full(wv.shape),
                   full((1, Q_RANK)), full((1, KV_RANK))],
        out_shape=[jax.ShapeDtypeStruct((s, 256), BF16), jax.ShapeDtypeStruct((s, 128), BF16), jax.ShapeDtypeStruct((s, 128), F32),
                   jax.ShapeDtypeStruct(wq.shape, F32), jax.ShapeDtypeStruct(wk.shape, F32), jax.ShapeDtypeStruct(wv.shape, F32),
                   jax.ShapeDtypeStruct((1, Q_RANK), F32), jax.ShapeDtypeStruct((1, KV_RANK), F32)],
        compiler_params=_cparams(("arbitrary",)),
    )(dq, dk, dv, proj, proj, cqn, ckvn, cos_m, sin_m, gq2, gkv2, wq, wk, wv)


def _ret_prep(proj, cos_r, sin_r):
    s = proj.shape[0]
    tr = _tile(s, 256)

    def body(q_ref, k_ref, cos_ref, sin_ref, qo_ref, ko_ref):
        cos, sin = cos_ref[...], sin_ref[...]
        q, k = q_ref[...], k_ref[...]
        qo_ref[...] = (q * cos + _rot_ret(q) * sin).astype(BF16)
        ko_ref[...] = ((k * cos + _rot_ret(k) * sin) * (HEAD ** -0.5)).astype(BF16)

    rowb = pl.BlockSpec((tr, 256), lambda i: (i, 0))
    return pl.pallas_call(
        body, name="ret_prep", grid=(s // tr,),
        in_specs=[pl.BlockSpec((tr, 256), lambda i: (i, OFF_RQ // 2)), pl.BlockSpec((tr, 256), lambda i: (i, OFF_RK // 2)), rowb, rowb],
        out_specs=[rowb, rowb], out_shape=[jax.ShapeDtypeStruct((s, 256), BF16)] * 2,
        compiler_params=_cparams(("parallel",)),
    )(proj, proj, cos_r, sin_r)


def _ret_prep_bwd(dq, dk, cos_r, sin_r):
    s = dq.shape[0]
    tr = _tile(s, 256)

    def body(dq_ref, dk_ref, cos_ref, sin_ref, qo_ref, ko_ref):
        cos, sin = cos_ref[...], sin_ref[...]
        q, k = dq_ref[...], dk_ref[...] * (HEAD ** -0.5)
        qo_ref[...] = (q * cos + _rot_ret_t(q * sin)).astype(BF16)
        ko_ref[...] = (k * cos + _rot_ret_t(k * sin)).astype(BF16)

    rowb = pl.BlockSpec((tr, 256), lambda i: (i, 0))
    return pl.pallas_call(
        body, name="ret_prep_bwd", grid=(s // tr,), in_specs=[rowb] * 4, out_specs=[rowb, rowb],
        out_shape=[jax.ShapeDtypeStruct((s, 256), BF16)] * 2, compiler_params=_cparams(("parallel",)),
    )(dq, dk, cos_r, sin_r)


_LOG_GAMMA = [float(np.log1p(-np.float32(2.0) ** np.float32(-5.0 - h))) for h in range(N_HEADS)]
_MLA_SCALE = float((HEAD + ROPE_DIM) ** -0.5)
_QK_SCALE = float(HEAD ** -0.5)
KEY_BLOCKS = 4
QB = 256


def _split2(x):
    h = x.astype(BF16)
    return h, (x - h.astype(F32)).astype(BF16)


def _dot2(x, u):
    h, lo = _split2(x)
    return _dot(h, u) + _dot(lo, u)


def _head_pick(block, head, axis):
    idx = lax.broadcasted_iota(jnp.int32, block.shape, axis)
    return jnp.sum(jnp.where(idx == head, block, 0.0), axis=axis, keepdims=True)


def _log_gamma_of(head):
    lg = jnp.float32(_LOG_GAMMA[3])
    for h in (2, 1, 0):
        lg = jnp.where(head == h, jnp.float32(_LOG_GAMMA[h]), lg)
    return lg


def _mixer_specs(mode, s, q_off, k_off, v_off):
    nhb = 2
    bw = 2 * LANES if mode == "mla" else LANES
    nsub = KEY_BLOCKS if (s // TQ) % KEY_BLOCKS == 0 else 1
    q_spec = pl.BlockSpec((QB, bw), lambda p, i: (i, q_off + p))
    k_spec = pl.BlockSpec((s, bw), lambda p, i: (0, k_off + p))
    v_spec = pl.BlockSpec((s, bw), lambda p, i: (0, v_off + p))
    return nhb, N_HEADS // nhb, nsub, q_spec, k_spec, v_spec


def _mixer_geometry(mode, i, nsub):
    w = TQ * nsub
    row = lax.broadcasted_iota(jnp.int32, (QB, w), 0)
    col = lax.broadcasted_iota(jnp.int32, (QB, w), 1)
    nfull = (i * QB) // w
    dist = col - row
    if mode in ("fox", "sb"):
        rel = dist
    else:
        rel = col - (row | (CHUNK - 1))

    def visible(c):
        off = c * w - i * QB
        return (rel + off) < 0 if mode == "sb" else (rel + off) <= 0

    return nfull, dist, visible


class _SideJob:
    def __init__(self, inputs, out_shape, n_sems, sends, recvs):
        self.inputs, self.out_shape, self.n_sems, self.sends, self.recvs = list(inputs), list(out_shape), n_sems, sends, recvs


def _carry_side_job(body, n_in, n_out, side, n_steps):
    if side is None:
        return body
    si, so = len(side.inputs), len(side.out_shape)

    def at(corner):
        ok = pl.program_id(0) == corner[0]
        for d in range(1, len(n_steps)):
            ok = ok & (pl.program_id(d) == corner[d])
        return ok

    def wrapped(*refs):
        ins, s_ins = refs[:n_in], refs[n_in:n_in + si]
        outs, s_outs = refs[n_in + si:n_in + si + n_out], refs[n_in + si + n_out:n_in + si + n_out + so]
        scratch, send, recv = refs[n_in + si + n_out + so:-2], refs[-2], refs[-1]

        @pl.when(at([0] * len(n_steps)))
        def _():
            for cp in side.sends(s_ins, s_outs, send, recv):
                cp.start()

        body(*ins, *outs, *scratch)

        @pl.when(at([n - 1 for n in n_steps]))
        def _():
            for cp in side.recvs(s_ins, s_outs, send, recv):
                cp.wait_recv()
            for cp in side.sends(s_ins, s_outs, send, recv):
                cp.wait_send()

    return wrapped


def _side_specs(side):
    if side is None:
        return [], [], []
    hbm = pl.BlockSpec(memory_space=pl.ANY)
    return ([hbm] * len(side.inputs), [hbm] * len(side.out_shape),
            [pltpu.SemaphoreType.DMA((side.n_sems,)), pltpu.SemaphoreType.DMA((side.n_sems,))])


def _mixer_fwd(mode, qa, q_off, ka, k_off, va, v_off, *, cum_col=None, cum_row=None, side=None):
    s = qa.shape[0]
    nq = s // QB
    nhb, nblk, nsub, q_spec, k_spec, v_spec = _mixer_specs(mode, s, q_off, k_off, v_off)
    w = TQ * nsub
    softmax = mode in ("fox", "mla")
    has_stat = mode != "ret"

    def body(*refs):
        refs = list(refs)
        q_ref, k_ref, v_ref = refs[:3]
        refs = refs[3:]
        if mode == "fox":
            cc_ref, cr_ref = refs[:2]
            refs = refs[2:]
        o_ref = refs[0]
        st_ref = refs[1] if has_stat else None
        p = pl.program_id(0)
        i = pl.program_id(1)
        nfull, dist, visible = _mixer_geometry(mode, i, nsub)
        lane = _lane((1, LANES))
        heads = [nhb * p + hh for hh in range(nhb)]
        wide = mode == "mla"
        q_scale = _QK_SCALE if mode in ("fox", "sb") else 1.0
        cols = [slice(hh * LANES, (hh + 1) * LANES) if wide else slice(None) for hh in range(nhb)]
        if wide:
            qs = [q_ref[:, cols[hh]] for hh in range(nhb)]
        else:
            qf = q_ref[...].astype(F32) * q_scale
            qs = [jnp.where((lane // HEAD) == hh, qf, 0.0).astype(BF16) for hh in range(nhb)]
        if mode == "fox":
            cqs = [_head_pick(cc_ref[...], h, 1) for h in heads]
        if mode == "sb":
            r1 = lax.broadcasted_iota(jnp.int32, (TQ, TQ), 0)
            c1 = lax.broadcasted_iota(jnp.int32, (TQ, TQ), 1)
            u_after = (r1 > c1).astype(BF16)

        def chunk(c):
            return pl.ds(pl.multiple_of(c * w, w), w)

        def scores(c):
            js = chunk(c)
            return tuple(_dot_nt(qs[hh], k_ref[js, cols[hh]]) for hh in range(nhb))

        def head_step(hh, c, js, sc, vj, carry, last):
            if softmax:
                m, l, acc = carry
                if mode == "fox":
                    ck = _head_pick(cr_ref[:, js], heads[hh], 0)
                    sc = sc + (cqs[hh] - ck)
                else:
                    sc = sc * _MLA_SCALE
                if last:
                    sc = jnp.where(visible(c), sc, NEG)
                m_new = jnp.maximum(m, jnp.max(sc, axis=-1, keepdims=True))
                alpha = jnp.exp(m - m_new)
                pr = jnp.exp(sc - m_new)
                l = alpha * l + jnp.sum(pr, axis=-1, keepdims=True)
                acc = alpha * acc + _dot(pr.astype(BF16), vj)
                return m_new, l, acc
            run, acc = carry
            z = sc
            log_beta = jnp.minimum(z, 0.0) - jnp.log(1.0 + jnp.exp(-jnp.abs(z)))
            log_stay = log_beta - z
            if last:
                vis = visible(c)
                log_stay = jnp.where(vis, log_stay, 0.0)
            parts = [None] * nsub
            for b in reversed(range(nsub)):
                ls_b = log_stay[:, b * TQ:(b + 1) * TQ]
                parts[b] = _dot2(ls_b, u_after) + run
                run = run + jnp.sum(ls_b, axis=-1, keepdims=True)
            later = parts[0] if nsub == 1 else jnp.concatenate(parts, axis=1)
            wgt = jnp.exp(log_beta + later)
            if last:
                wgt = jnp.where(vis, wgt, 0.0)
            return run, acc + _dot(wgt.astype(BF16), vj)

        def step(c, c_next, state, last):
            scs, carries = state
            nxt = scores(c_next) if c_next is not None else None
            js = chunk(c)
            return nxt, tuple(head_step(hh, c, js, scs[hh], v_ref[js, cols[hh]], carries[hh], last) for hh in range(nhb))

        zero_acc = jnp.zeros((QB, LANES), F32)
        zero1 = jnp.zeros((QB, 1), F32)
        if softmax:
            init = tuple((jnp.full((QB, 1), NEG, F32), zero1, zero_acc) for _ in range(nhb))
        else:
            init = tuple((zero1, zero_acc) for _ in range(nhb))
        if mode == "sb":
            state = step(nfull, jnp.maximum(nfull - 1, 0), (scores(nfull), init), True)
            _, carries = lax.fori_loop(0, nfull, lambda t, st: step(nfull - 1 - t, jnp.maximum(nfull - 2 - t, 0), st, False), state)
        else:
            state = lax.fori_loop(0, nfull, lambda c, st: step(c, c + 1, st, False), (scores(0), init))
            _, carries = step(nfull, None, state, True)
        if softmax:
            outs = [acc / l for (m, l, acc) in carries]
            stats = [m + jnp.log(l) for (m, l, acc) in carries]
        else:
            outs, stats = [acc for (run, acc) in carries], [run for (run, acc) in carries]
        hm0 = (lane // HEAD) == 0
        pick = lambda a: jnp.where(hm0, a[0], a[1])
        if wide:
            for hh in range(nhb):
                o_ref[:, cols[hh]] = outs[hh]
        else:
            o_ref[...] = pick(outs)
        if has_stat:
            st_ref[0] = pick(stats)

    in_specs = [q_spec, k_spec, v_spec]
    args = [qa, ka, va]
    if mode == "fox":
        in_specs += [pl.BlockSpec((QB, LANES), lambda p, i: (i, 0)), pl.BlockSpec((8, s), lambda p, i: (0, 0))]
        args += [cum_col, cum_row]
    bw = 2 * LANES if mode == "mla" else LANES
    out_specs = [pl.BlockSpec((QB, bw), lambda p, i: (i, p))]
    out_shape = [jax.ShapeDtypeStruct((s, nblk * bw), F32)]
    out_specs.append(pl.BlockSpec((1, QB, LANES), lambda p, i: (p, i, 0)))
    out_shape.append(jax.ShapeDtypeStruct((nblk, s, LANES), F32))
    side_in, side_out, side_scratch = _side_specs(side)
    res = pl.pallas_call(
        _carry_side_job(body, len(args), len(out_shape), side, (nblk, nq)), name=mode + "_fwd", grid=(nblk, nq),
        in_specs=in_specs + side_in, out_specs=out_specs + side_out,
        out_shape=out_shape + ([] if side is None else side.out_shape), scratch_shapes=side_scratch,
        compiler_params=_cparams(("parallel", "parallel") if side is None else ("arbitrary", "arbitrary")),
    )(*args, *([] if side is None else side.inputs))
    return (res[0], res[1]) if side is None else (res[0], res[1], res[2:])


def _mixer_bwd(mode, qa, q_off, ka, k_off, va, v_off, o, do, *, stat=None, cum_col=None, cum_row=None, side=None):
    s = qa.shape[0]
    nq = s // QB
    nhb, nblk, nsub, q_spec, k_spec, v_spec = _mixer_specs(mode, s, q_off, k_off, v_off)
    w = TQ * nsub
    softmax = mode in ("fox", "mla")
    has_stat = mode != "ret"

    def body(*refs):
        refs = list(refs)
        q_ref, k_ref, v_ref, o_ref, do_ref = refs[:5]
        refs = refs[5:]
        if has_stat:
            st_ref = refs[0]
            refs = refs[1:]
        if mode == "fox":
            cc_ref, cr_ref = refs[:2]
            refs = refs[2:]
        dq_ref, dk_ref, dv_ref = refs[:3]
        dck_ref, drs_ref = refs[3:5] if mode == "fox" else (None, None)
        p = pl.program_id(0)
        i = pl.program_id(1)

        @pl.when(i == 0)
        def _():
            dk_ref[...] = jnp.zeros_like(dk_ref)
            dv_ref[...] = jnp.zeros_like(dv_ref)
            if mode == "fox":
                dck_ref[...] = jnp.zeros_like(dck_ref)

        nfull, dist, visible = _mixer_geometry(mode, i, nsub)
        lane = _lane((1, LANES))
        heads = [nhb * p + hh for hh in range(nhb)]
        dov = do_ref[...]
        wide = mode == "mla"
        q_scale = _QK_SCALE if mode in ("fox", "sb") else 1.0
        cols = [slice(hh * LANES, (hh + 1) * LANES) if wide else slice(None) for hh in range(nhb)]
        if wide:
            prod = dov * o_ref[...]
            qs = [q_ref[:, cols[hh]] for hh in range(nhb)]
            dos = [dov[:, cols[hh]].astype(BF16) for hh in range(nhb)]
            deltas = [jnp.sum(prod[:, cols[hh]], axis=-1, keepdims=True) for hh in range(nhb)]
        else:
            qf = q_ref[...].astype(F32) * q_scale
            prod = dov * o_ref[...]
            hms = [(lane // HEAD) == hh for hh in range(nhb)]
            qs = [jnp.where(hm, qf, 0.0).astype(BF16) for hm in hms]
            dos = [jnp.where(hm, dov, 0.0).astype(BF16) for hm in hms]
            deltas = [jnp.sum(jnp.where(hm, prod, 0.0), axis=-1, keepdims=True) for hm in hms]
        if has_stat:
            st = st_ref[0]
            stats = [st[:, hh * HEAD:hh * HEAD + 1] for hh in range(nhb)]
        if mode == "fox":
            cqs = [_head_pick(cc_ref[...], h, 1) for h in heads]
        if mode == "sb":
            r1 = lax.broadcasted_iota(jnp.int32, (TQ, TQ), 0)
            c1 = lax.broadcasted_iota(jnp.int32, (TQ, TQ), 1)
            u_upto = (r1 <= c1).astype(BF16)
            u_before = (r1 < c1).astype(BF16)

        def chunk(c):
            return pl.ds(pl.multiple_of(c * w, w), w)

        def scores(c):
            js = chunk(c)
            if mode == "sb":
                return tuple((_dot_nt(qs[hh], k_ref[js, cols[hh]]), None) for hh in range(nhb))
            return tuple((_dot_nt(qs[hh], k_ref[js, cols[hh]]), _dot_nt(dos[hh], v_ref[js, cols[hh]])) for hh in range(nhb))

        def emit(hh, js, ds_b, pr_b, dq):
            dk_ref[js, cols[hh]] += _dot_tn(ds_b, qs[hh])
            dv_ref[js, cols[hh]] += _dot_tn(pr_b, dos[hh])
            return dq + _dot(ds_b, k_ref[js, cols[hh]])

        def head_step(hh, c, js, sc_dp, carry, last):
            sc, dp = sc_dp
            if dp is None:
                dp = _dot_nt(dos[hh], v_ref[js, cols[hh]])
            if softmax:
                dq, rsum = carry
                if mode == "fox":
                    ck = _head_pick(cr_ref[:, js], heads[hh], 0)
                    sc = sc + (cqs[hh] - ck)
                else:
                    sc = sc * _MLA_SCALE
                if last:
                    sc = jnp.where(visible(c), sc, NEG)
                pr = jnp.exp(sc - stats[hh])
                ds = pr * (dp - deltas[hh])
                if mode == "fox":
                    dck_ref[0, hh:hh + 1, js] += jnp.sum(ds, axis=0, keepdims=True)
                    rsum = rsum + jnp.sum(ds, axis=-1, keepdims=True)
                if mode == "mla":
                    ds = ds * _MLA_SCALE
                return emit(hh, js, ds.astype(BF16), pr.astype(BF16), dq), rsum
            seen, gsum, dq = carry
            z = sc
            log_beta = jnp.minimum(z, 0.0) - jnp.log(1.0 + jnp.exp(-jnp.abs(z)))
            log_stay = log_beta - z
            if last:
                vis = visible(c)
                log_stay = jnp.where(vis, log_stay, 0.0)
            parts = []
            for b in range(nsub):
                ls_b = log_stay[:, b * TQ:(b + 1) * TQ]
                parts.append((stats[hh] - seen) - _dot2(ls_b, u_upto))
                seen = seen + jnp.sum(ls_b, axis=-1, keepdims=True)
            later = parts[0] if nsub == 1 else jnp.concatenate(parts, axis=1)
            wgt = jnp.exp(log_beta + later)
            if last:
                wgt = jnp.where(vis, wgt, 0.0)
            g = dp * wgt
            parts = []
            for b in range(nsub):
                g_b = g[:, b * TQ:(b + 1) * TQ]
                parts.append(gsum + _dot2(g_b, u_before))
                gsum = gsum + jnp.sum(g_b, axis=-1, keepdims=True)
            before = parts[0] if nsub == 1 else jnp.concatenate(parts, axis=1)
            beta = jnp.exp(log_beta)
            dz = g * (1.0 - beta) - beta * before
            if last:
                dz = jnp.where(vis, dz, 0.0)
            return seen, gsum, emit(hh, js, dz.astype(BF16), wgt.astype(BF16), dq)

        def step(c, c_next, state, last):
            scs, carries = state
            nxt = scores(c_next) if c_next is not None else None
            js = chunk(c)
            return nxt, tuple(head_step(hh, c, js, scs[hh], carries[hh], last) for hh in range(nhb))

        zero_acc = jnp.zeros((QB, LANES), F32)
        zero1 = jnp.zeros((QB, 1), F32)
        if softmax:
            init = tuple((zero_acc, zero1) for _ in range(nhb))
        else:
            init = tuple((zero1, zero1, zero_acc) for _ in range(nhb))
        state = lax.fori_loop(0, nfull, lambda c, st: step(c, c + 1, st, False), (scores(0), init))
        _, carries = step(nfull, None, state, True)
        if softmax:
            dqs = [dq for (dq, rsum) in carries]
        else:
            dqs = [dq for (seen, gsum, dq) in carries]
        hm0 = (lane // HEAD) == 0
        if wide:
            for hh in range(nhb):
                dq_ref[:, cols[hh]] = dqs[hh]
        else:
            dq_ref[...] = jnp.where(hm0, dqs[0], dqs[1]) * q_scale
        if mode == "fox":
            drs_ref[0] = jnp.where(hm0, carries[0][1], carries[1][1])

    bw = 2 * LANES if mode == "mla" else LANES
    pair_blk = pl.BlockSpec((QB, bw), lambda p, i: (i, p))
    full_blk = pl.BlockSpec((s, bw), lambda p, i: (0, p))
    stat_blk = pl.BlockSpec((1, QB, LANES), lambda p, i: (p, i, 0))
    in_specs = [q_spec, k_spec, v_spec, pair_blk, pair_blk]
    args = [qa, ka, va, o, do]
    if has_stat:
        in_specs.append(stat_blk)
        args.append(stat)
    if mode == "fox":
        in_specs += [pl.BlockSpec((QB, LANES), lambda p, i: (i, 0)), pl.BlockSpec((8, s), lambda p, i: (0, 0))]
        args += [cum_col, cum_row]
    out_specs = [pair_blk, full_blk, full_blk]
    out_shape = [jax.ShapeDtypeStruct((s, nblk * bw), F32)] * 3
    if mode == "fox":
        out_specs += [pl.BlockSpec((1, 8, s), lambda p, i: (p, 0, 0)), stat_blk]
        out_shape += [jax.ShapeDtypeStruct((2, 8, s), F32), jax.ShapeDtypeStruct((2, s, LANES), F32)]
    side_in, side_out, side_scratch = _side_specs(side)
    res = pl.pallas_call(
        _carry_side_job(body, len(args), len(out_shape), side, (nblk, nq)), name=mode + "_bwd", grid=(nblk, nq),
        in_specs=in_specs + side_in, out_specs=out_specs + side_out,
        out_shape=out_shape + ([] if side is None else side.out_shape), scratch_shapes=side_scratch,
        compiler_params=_cparams(("parallel", "arbitrary") if side is None else ("arbitrary", "arbitrary")),
    )(*args, *([] if side is None else side.inputs))
    return res if side is None else (*res[:len(out_shape)], res[len(out_shape):])


def _ret_geometry(p):
    lane = _lane((1, LANES))
    lg_lane = jnp.where(lane < HEAD, _log_gamma_of(2 * p), _log_gamma_of(2 * p + 1))
    a = lax.broadcasted_iota(jnp.int32, (TQ, 1), 0).astype(F32)
    row = lax.broadcasted_iota(jnp.int32, (TQ, TQ), 0)
    col = lax.broadcasted_iota(jnp.int32, (TQ, TQ), 1)
    same_chunk_or_earlier = (col // CHUNK) <= (row // CHUNK)
    gap = jnp.abs(row - col).astype(F32)
    decays = [jnp.where(same_chunk_or_earlier, jnp.exp(_log_gamma_of(2 * p + hh) * gap), 0.0) for hh in range(2)]
    r = lax.broadcasted_iota(jnp.int32, (LANES, LANES), 0)
    c = lax.broadcasted_iota(jnp.int32, (LANES, LANES), 1)
    own_head = (r // HEAD) == (c // HEAD)
    return lane, lg_lane, a, decays, own_head


def _ret_fwd(qa, ka, va, v_off):
    s = qa.shape[0]
    nq = s // TQ

    def body(q_ref, k_ref, v_ref, o_ref, st_ref, state):
        p = pl.program_id(0)

        @pl.when(pl.program_id(1) == 0)
        def _():
            state[...] = jnp.zeros_like(state)

        lane, lg_lane, a, decays, own_head = _ret_geometry(p)
        q = q_ref[...].astype(F32)
        k = k_ref[...]
        v = v_ref[...]
        s_in = state[...]
        st_ref[0, 0] = s_in
        out = _dot((q * jnp.exp(lg_lane * (a + 1.0))).astype(BF16), s_in.astype(BF16))
        for hh in range(2):
            hm = (lane // HEAD) == hh
            qh = jnp.where(hm, q, 0.0).astype(BF16)
            inner = _dot((_dot_nt(qh, k) * decays[hh]).astype(BF16), v)
            out = out + jnp.where(hm, inner, 0.0)
        o_ref[...] = out
        k_tail = (k.astype(F32) * jnp.exp(lg_lane * (TQ - 1.0 - a))).astype(BF16)
        state[...] = jnp.exp(lg_lane * float(TQ)) * s_in + jnp.where(own_head, _dot_tn(k_tail, v), 0.0)

    blk = lambda off: pl.BlockSpec((TQ, LANES), lambda p, i: (i, off + p))
    return pl.pallas_call(
        body, name="ret_fwd", grid=(2, nq), in_specs=[blk(0), blk(0), blk(v_off)],
        out_specs=[blk(0), pl.BlockSpec((1, 1, LANES, LANES), lambda p, i: (p, i, 0, 0))],
        out_shape=[jax.ShapeDtypeStruct((s, 2 * LANES), F32), jax.ShapeDtypeStruct((2, nq, LANES, LANES), F32)],
        scratch_shapes=[pltpu.VMEM((LANES, LANES), F32)],
        compiler_params=_cparams(("parallel", "arbitrary")),
    )(qa, ka, va)


def _ret_bwd(qa, ka, va, v_off, states, do):
    s = qa.shape[0]
    nq = s // TQ

    def body(q_ref, k_ref, v_ref, st_ref, do_ref, dq_ref, dk_ref, dv_ref, dstate):
        p = pl.program_id(0)

        @pl.when(pl.program_id(1) == 0)
        def _():
            dstate[...] = jnp.zeros_like(dstate)

        lane, lg_lane, a, decays, own_head = _ret_geometry(p)
        q = q_ref[...].astype(F32)
        k = k_ref[...]
        kf = k.astype(F32)
        v = v_ref[...]
        dov = do_ref[...]
        s_in = st_ref[0, 0].astype(BF16)
        ds_next = dstate[...]
        ds_b = ds_next.astype(BF16)
        head_decay = jnp.exp(lg_lane * (a + 1.0))
        tail_decay = jnp.exp(lg_lane * (TQ - 1.0 - a))
        k_tail = (kf * tail_decay).astype(BF16)
        dq = _dot_nt(dov.astype(BF16), s_in) * head_decay
        dk = _dot_nt(v, ds_b) * tail_decay
        dv = _dot(k_tail, ds_b)
        for hh in range(2):
            hm = (lane // HEAD) == hh
            qh = jnp.where(hm, q, 0.0).astype(BF16)
            doh = jnp.where(hm, dov, 0.0).astype(BF16)
            att = (_dot_nt(qh, k) * decays[hh]).astype(BF16)
            datt = (_dot_nt(doh, v) * decays[hh]).astype(BF16)
            dv = dv + _dot_tn(att, doh)
            dk = dk + _dot_tn(datt, qh)
            dq = dq + jnp.where(hm, _dot(datt, k), 0.0)
        dq_ref[...] = dq
        dk_ref[...] = dk
        dv_ref[...] = dv
        q_head = (q * head_decay).astype(BF16)
        dstate[...] = jnp.exp(lg_lane * float(TQ)) * ds_next + jnp.where(own_head, _dot_tn(q_head, dov.astype(BF16)), 0.0)

    blk = lambda off: pl.BlockSpec((TQ, LANES), lambda p, i: (nq - 1 - i, off + p))
    return pl.pallas_call(
        body, name="ret_bwd", grid=(2, nq),
        in_specs=[blk(0), blk(0), blk(v_off), pl.BlockSpec((1, 1, LANES, LANES), lambda p, i: (p, nq - 1 - i, 0, 0)), blk(0)],
        out_specs=[blk(0)] * 3, out_shape=[jax.ShapeDtypeStruct((s, 2 * LANES), F32)] * 3,
        scratch_shapes=[pltpu.VMEM((LANES, LANES), F32)],
        compiler_params=_cparams(("parallel", "arbitrary")),
    )(qa, ka, va, states, do)


def _seg_mean_matrix():
    r = lax.broadcasted_iota(jnp.int32, (GROUP, GROUP), 0)
    c = lax.broadcasted_iota(jnp.int32, (GROUP, GROUP), 1)
    return jnp.where((r // HEAD) == (c // HEAD), 1.0 / HEAD, 0.0).astype(F32)


def _sigmoid(x):
    return 1.0 / (1.0 + jnp.exp(-x))


def _mix_post(oa, ob, oc, od, proj, g):
    s = oa.shape[0]
    tr = _tile(s, 256)

    def body(a_ref, b_ref, c_ref, d_ref, rg_ref, g_ref, o_ref):
        gv = g_ref[...]
        o_ref[:, 0:GROUP] = _rms(a_ref[...], gv[:, 0:GROUP]).astype(BF16)
        o_ref[:, GROUP:2 * GROUP] = _rms(b_ref[...], gv[:, GROUP:2 * GROUP]).astype(BF16)
        seg = _seg_mean_matrix()
        c = c_ref[...]
        cen = c - _dot_exact(c, seg)
        n = cen * lax.rsqrt(_dot_exact(cen * cen, seg) + EPS)
        rg = rg_ref[...]
        o_ref[:, 2 * GROUP:3 * GROUP] = (n * gv[:, 2 * GROUP:3 * GROUP] * (rg * _sigmoid(rg))).astype(BF16)
        o_ref[:, 3 * GROUP:] = _rms(d_ref[...], gv[:, 3 * GROUP:]).astype(BF16)

    blk = pl.BlockSpec((tr, GROUP), lambda i: (i, 0))
    return pl.pallas_call(
        body, name="mix_post", grid=(s // tr,),
        in_specs=[blk] * 4 + [pl.BlockSpec((tr, GROUP), lambda i: (i, OFF_RG // 2)), pl.BlockSpec((1, D_MODEL), lambda i: (0, 0))],
        out_specs=pl.BlockSpec((tr, D_MODEL), lambda i: (i, 0)), out_shape=jax.ShapeDtypeStruct((s, D_MODEL), BF16),
        compiler_params=_cparams(("parallel",)),
    )(oa, ob, oc, od, proj, g.reshape(1, D_MODEL))


def _mix_post_bwd(dmixed, oa, ob, oc, od, proj, g):
    s = oa.shape[0]
    tr = _tile(s, 256)

    def body(dm_ref, a_ref, b_ref, c_ref, d_ref, rg_ref, g_ref, da_ref, db_ref, dc_ref, dd_ref, drg_ref, dg_ref):
        @pl.when(pl.program_id(0) == 0)
        def _():
            dg_ref[...] = jnp.zeros_like(dg_ref)

        gv = g_ref[...]
        dm = dm_ref[...]
        for k, (x_ref, dx_ref) in enumerate(((a_ref, da_ref), (b_ref, db_ref), (None, None), (d_ref, dd_ref))):
            if x_ref is None:
                continue
            cols = slice(k * GROUP, (k + 1) * GROUP)
            dx, gterm = _rms_bwd(x_ref[...], gv[:, cols], dm[:, cols])
            dx_ref[...] = dx
            dg_ref[:, cols] += jnp.sum(gterm, axis=0, keepdims=True)
        cols = slice(2 * GROUP, 3 * GROUP)
        seg = _seg_mean_matrix()
        c = c_ref[...]
        cen = c - _dot_exact(c, seg)
        rstd = lax.rsqrt(_dot_exact(cen * cen, seg) + EPS)
        n = cen * rstd
        rg = rg_ref[...]
        sg = _sigmoid(rg)
        gate = rg * sg
        dy = dm[:, cols]
        gc = gv[:, cols]
        dn = dy * gc * gate
        dg_ref[:, cols] += jnp.sum(dy * n * gate, axis=0, keepdims=True)
        drg_ref[...] = (dy * n * gc * (sg * (1.0 + rg * (1.0 - sg)))).astype(BF16)
        dc_ref[...] = rstd * (dn - _dot_exact(dn, seg) - n * _dot_exact(dn * n, seg))

    blk = pl.BlockSpec((tr, GROUP), lambda i: (i, 0))
    gsp = pl.BlockSpec((1, D_MODEL), lambda i: (0, 0))
    return pl.pallas_call(
        body, name="mix_post_bwd", grid=(s // tr,),
        in_specs=[pl.BlockSpec((tr, D_MODEL), lambda i: (i, 0))] + [blk] * 4 + [pl.BlockSpec((tr, GROUP), lambda i: (i, OFF_RG // 2)), gsp],
        out_specs=[blk] * 5 + [gsp],
        out_shape=[jax.ShapeDtypeStruct((s, GROUP), F32)] * 4 + [jax.ShapeDtypeStruct((s, GROUP), BF16), jax.ShapeDtypeStruct((1, D_MODEL), F32)],
        compiler_params=_cparams(("arbitrary",)),
    )(dmixed, oa, ob, oc, od, proj, g.reshape(1, D_MODEL))


def _pack_w_in(w):
    z = lambda n: jnp.zeros((w.shape[0], n), w.dtype)
    misc = jnp.concatenate([w[:, 768:772], z(KR_LANE - N_HEADS), w[:, 1156:1188], z(LANES - KR_LANE - ROPE_DIM)], axis=1)
    return jnp.concatenate([w[:, 0:768], w[:, 772:1028], w[:, 1188:2980], w[:, 1028:1156], misc], axis=1)


def _unpack_dw_in(d):
    m = OFF_MISC * LANES
    return jnp.concatenate([d[:, 0:768], d[:, m:m + N_HEADS], d[:, 768:1024], d[:, OFF_CKV * LANES:m],
                            d[:, m + KR_LANE:m + KR_LANE + ROPE_DIM], d[:, 1024:OFF_CKV * LANES]], axis=1)


def _pack_w_q(w):
    return jnp.pad(w.reshape(Q_RANK, N_HEADS, HEAD + ROPE_DIM), ((0, 0), (0, 0), (0, LANES - HEAD - ROPE_DIM))).reshape(Q_RANK, 4 * LANES)


def _unpack_dw_q(d):
    return d.reshape(Q_RANK, N_HEADS, LANES)[:, :, :HEAD + ROPE_DIM].reshape(Q_RANK, N_HEADS * (HEAD + ROPE_DIM))


def _pack_w_kv(w):
    w4 = w.reshape(KV_RANK, N_HEADS, 2 * HEAD)
    widen = lambda a: jnp.pad(a, ((0, 0), (0, 0), (0, LANES - HEAD))).reshape(KV_RANK, N_HEADS * LANES)
    return widen(w4[:, :, :HEAD]), widen(w4[:, :, HEAD:])


def _unpack_dw_kv(dk, dv):
    narrow = lambda a: a.reshape(KV_RANK, N_HEADS, LANES)[:, :, :HEAD]
    return jnp.concatenate([narrow(dk), narrow(dv)], axis=2).reshape(KV_RANK, 2 * N_HEADS * HEAD)


def _narrow_heads(a):
    return a.reshape(a.shape[0], N_HEADS, LANES)[:, :, :HEAD].reshape(a.shape[0], N_HEADS * HEAD)


def _widen_heads(a):
    return jnp.pad(a.reshape(a.shape[0], N_HEADS, HEAD), ((0, 0), (0, 0), (0, LANES - HEAD))).reshape(a.shape[0], N_HEADS * LANES)


def _layer_fwd(x, lw, tabs, tag, side=None, fox_side=None, late_weights=None):
    cos_m, sin_m, cos_r, sin_r = tabs
    h1 = _norm_fwd(x, lw["g_mix_pre"], name=tag + "pre_norm")
    proj, projb = _matmul(h1, lw["w_in"], name=tag + "in_proj", also_bf16=True)
    bias_row = jnp.pad(lw["b_forget"], (FF_LANE, LANES - N_HEADS - FF_LANE)).reshape(1, LANES)
    cum_col, cum_row = _fox_cum(proj, bias_row)
    oa, lse_a, *fox_carried = _mixer_fwd("fox", projb, OFF_FQ, projb, OFF_FK, projb, OFF_FV, cum_col=cum_col, cum_row=cum_row,
                                         side=fox_side)
    if late_weights is not None:
        lw = {**lw, **late_weights(fox_carried[0])}
    qm, km, vm, cqn, ckvn = _mla_prep(proj, cos_m, sin_m, lw["g_q_lora"], lw["g_kv_lora"], lw["wq"], lw["wk"], lw["wv"])
    ob_wide, lse_b = _mixer_fwd("mla", qm, 0, km, 0, vm, 0)
    ob = _narrow_heads(ob_wide)
    qr, kr = _ret_prep(proj, cos_r, sin_r)
    oc, ret_states = _ret_fwd(qr, kr, projb, OFF_RV)
    od, tot_d, *carried = _mixer_fwd("sb", projb, OFF_SQ, projb, OFF_SK, projb, OFF_SV, side=side)
    mixed = _mix_post(oa, ob, oc, od, proj, lw["g_mix_out"])
    mix = _matmul(mixed, lw["w_out"], name=tag + "out_proj")
    x1 = _norm_fwd(mix, lw["g_mix_post"], name=tag + "mix_post_norm", resid=x, out_dtype=F32)
    h2 = _norm_fwd(x1, lw["g_ffn_pre"], name=tag + "ffn_pre_norm")
    u = _matmul(h2, lw["w_ffn_up"], name=tag + "ffn_up", relu2=True, out_dtype=BF16)
    f = _matmul(u, lw["w_ffn_down"], name=tag + "ffn_down")
    x2 = _norm_fwd(f, lw["g_ffn_post"], name=tag + "ffn_post_norm", resid=x1, out_dtype=F32)
    saved = dict(x=x, h1=h1, proj=proj, projb=projb, bias_row=bias_row, cum_col=cum_col, cum_row=cum_row, oa=oa, lse_a=lse_a,
                 qm=qm, km=km, vm=vm, cqn=cqn, ckvn=ckvn, ob=ob, ob_wide=ob_wide, lse_b=lse_b, qr=qr, kr=kr, ret_states=ret_states, oc=oc, od=od, tot_d=tot_d, mixed=mixed,
                 mix=mix, x1=x1, h2=h2, u=u, f=f)
    return x2, saved, lw, (carried[0] if carried else None)


def _layer_bwd(dx2, lw, sv, tabs, tag, side=None):
    cos_m, sin_m, cos_r, sin_r = tabs
    g = {}
    df, g["g_ffn_post"] = _norm_bwd(sv["f"], lw["g_ffn_post"], dx2, name=tag + "ffn_post_norm_bwd", out_dtype=BF16)
    du_pre = _matmul(df, lw["w_ffn_down"], name=tag + "ffn_down_dx", tb=True, out_dtype=BF16, relu2_of=sv["u"])
    g["w_ffn_down"] = _matmul(sv["u"], df, name=tag + "ffn_down_dw", ta=True)
    dh2 = _matmul(du_pre, lw["w_ffn_up"], name=tag + "ffn_up_dx", tb=True)
    g["w_ffn_up"] = _matmul(sv["h2"], du_pre, name=tag + "ffn_up_dw", ta=True)
    dx1, g["g_ffn_pre"] = _norm_bwd(sv["x1"], lw["g_ffn_pre"], dh2, name=tag + "ffn_pre_norm_bwd", add=dx2)
    dmix, g["g_mix_post"] = _norm_bwd(sv["mix"], lw["g_mix_post"], dx1, name=tag + "mix_post_norm_bwd", out_dtype=BF16)
    dmixed = _matmul(dmix, lw["w_out"], name=tag + "out_proj_dx", tb=True)
    g["w_out"] = _matmul(sv["mixed"], dmix, name=tag + "out_proj_dw", ta=True)
    proj, projb = sv["proj"], sv["projb"]
    doa, dob, doc, dod, drg, g["g_mix_out"] = _mix_post_bwd(dmixed, sv["oa"], sv["ob"], sv["oc"], sv["od"], proj, lw["g_mix_out"])
    dfq, dfk, dfv, dck, drs = _mixer_bwd("fox", projb, OFF_FQ, projb, OFF_FK, projb, OFF_FV, sv["oa"], doa, stat=sv["lse_a"],
                                         cum_col=sv["cum_col"], cum_row=sv["cum_row"])
    dqm, dkm, dvm = _mixer_bwd("mla", sv["qm"], 0, sv["km"], 0, sv["vm"], 0, sv["ob_wide"], _widen_heads(dob), stat=sv["lse_b"])
    dcq, dckv, dkr, dwq, dwk, dwv, g["g_q_lora"], g["g_kv_lora"] = _mla_prep_bwd(
        dqm, dkm, dvm, proj, sv["cqn"], sv["ckvn"], cos_m, sin_m, lw["g_q_lora"], lw["g_kv_lora"], lw["wq"], lw["wk"], lw["wv"])
    dqr, dkr_ret, drv = _ret_bwd(sv["qr"], sv["kr"], projb, OFF_RV, sv["ret_states"], doc)
    drq, drk = _ret_prep_bwd(dqr, dkr_ret, cos_r, sin_r)
    if callable(side):
        side = side(g)
    dsq, dsk, dsv, *carried = _mixer_bwd("sb", projb, OFF_SQ, projb, OFF_SK, projb, OFF_SV, sv["od"], dod, stat=sv["tot_d"], side=side)
    dmisc, db_row = _fox_gate_bwd(dck, drs, proj, sv["bias_row"], dkr)
    b = lambda a: a.astype(BF16)
    dproj = jnp.concatenate([b(dfq), b(dfk), b(dfv), dcq, drq, drk, b(drv), drg, b(dsq), b(dsk), b(dsv), dckv, dmisc], axis=1)
    dh1 = _matmul(dproj, lw["w_in"], name=tag + "in_proj_dx", tb=True)
    g["w_in"] = _matmul(sv["h1"], dproj, name=tag + "in_proj_dw", ta=True)
    dx, g["g_mix_pre"] = _norm_bwd(sv["x"], lw["g_mix_pre"], dh1, name=tag + "pre_norm_bwd", add=dx1)
    g["b_forget"] = db_row[0, FF_LANE:FF_LANE + N_HEADS]
    g["wq"], g["wk"], g["wv"] = dwq, dwk, dwv
    return dx, g, (carried[0] if carried else None)


def _local_step(x, positions, layers, target):
    s = x.shape[0]
    tabs = _rope_tables(positions.reshape(s, 1))
    saved = []
    for li, lw in enumerate(layers):
        x, sv, _, _ = _layer_fwd(x, lw, tabs, "l%d_" % li)
        saved.append(sv)
    loss_row, dx = _loss_head(x, target)
    grads = [None] * len(layers)
    for li in reversed(range(len(layers))):
        dx, grads[li], _ = _layer_bwd(dx, layers[li], saved[li], tabs, "l%d_" % li)
    return loss_row[0, 0], dx, grads


def _adamw(w, g, m, v, *, name):
    r, c = w.shape
    tr = 256 if r % 256 == 0 else r
    blk = pl.BlockSpec((tr, c), lambda i: (i, 0))
    c1 = 1.0 - ADAM_B1 ** ADAM_STEP
    c2 = 1.0 - ADAM_B2 ** ADAM_STEP

    def body(w_ref, g_ref, m_ref, v_ref, d_ref, mo_ref, vo_ref):
        gv = g_ref[...]
        mn = ADAM_B1 * m_ref[...] + (1.0 - ADAM_B1) * gv
        vn = ADAM_B2 * v_ref[...] + (1.0 - ADAM_B2) * jnp.square(gv)
        mo_ref[...] = mn
        vo_ref[...] = vn
        d_ref[...] = -ADAM_LR * ((mn / c1) / (jnp.sqrt(vn / c2) + ADAM_EPS) + ADAM_WD * w_ref[...])

    return pl.pallas_call(
        body, name=name, grid=(r // tr,), in_specs=[blk] * 4, out_specs=[blk] * 3,
        out_shape=[jax.ShapeDtypeStruct((r, c), F32)] * 3, compiler_params=_cparams(("parallel",)),
    )(w, g, m, v)


BIG = ("w_in", "w_q_up", "w_kv_up", "w_out", "w_ffn_up", "w_ffn_down")
SMALL = ("g_mix_pre", "b_forget", "g_q_lora", "g_kv_lora", "g_mix_out", "g_mix_post", "g_ffn_pre", "g_ffn_post")
N_CHIPS = 4
ANY = pl.BlockSpec(memory_space=pl.ANY)


def _mesh_pos():
    return lax.axis_index("x"), lax.axis_index("y"), lax.axis_index("c")


def _other_chips(x, y):
    return [(1 - x, y), (x, 1 - y), (1 - x, 1 - y)]


def _rows_half(ref, half):
    h = ref.shape[-2] // 2
    return ref.at[(slice(None),) * (len(ref.shape) - 2) + (pl.ds(half * h, h), slice(None))]


def _remote(src, dst, send_sem, recv_sem, device):
    return pltpu.make_async_remote_copy(src_ref=src, dst_ref=dst, send_sem=send_sem, recv_sem=recv_sem, device_id=device,
                                        device_id_type=MESH)


def _comm_call(body, name, args, out_shape, n_sems):
    return pl.pallas_call(
        body, name=name, in_specs=[ANY] * len(args), out_specs=[ANY] * len(out_shape), out_shape=out_shape,
        scratch_shapes=[pltpu.SemaphoreType.DMA((n_sems,)), pltpu.SemaphoreType.DMA((n_sems,))],
        compiler_params=pltpu.CompilerParams(has_side_effects=True),
    )(*args)


def _run_side_job(side, name):
    si = len(side.inputs)

    def body(*refs):
        args = (refs[:si], refs[si:-2], refs[-2], refs[-1])
        sends = side.sends(*args)
        for cp in sends:
            cp.start()
        for cp in side.recvs(*args):
            cp.wait_recv()
        for cp in sends:
            cp.wait_send()

    return _comm_call(body, name, side.inputs, side.out_shape, side.n_sems)


def _gather_job(shards):
    n = len(shards)

    def copies(own_block, ins, outs, send_sems, recv_sems):
        x, y, c = _mesh_pos()
        return [_remote(_rows_half(ins[t], c), _rows_half(outs[t].at[2 * x + y if own_block else 2 * px + py], c),
                        send_sems.at[3 * t + j], recv_sems.at[3 * t + j], (px, py, c))
                for t in range(n) for j, (px, py) in enumerate(_other_chips(x, y))]

    return _SideJob(shards, [jax.ShapeDtypeStruct((N_CHIPS,) + a.shape, a.dtype) for a in shards], 3 * n,
                    functools.partial(copies, True), functools.partial(copies, False))


def _forward_halves(gathered):
    n = len(gathered)

    def body(*refs):
        bufs, send_sems, recv_sems = refs[n:2 * n], refs[-2], refs[-1]
        x, y, c = _mesh_pos()

        def d2d(t, j, block, half):
            region = _rows_half(bufs[t].at[block], half)
            return _remote(region, region, send_sems.at[3 * t + j], recv_sems.at[3 * t + j], (x, y, 1 - c))

        peers = list(enumerate(_other_chips(x, y)))
        sends = [d2d(t, j, 2 * px + py, c) for t in range(n) for j, (px, py) in peers]
        for cp in sends:
            cp.start()
        for t in range(n):
            for j, (px, py) in peers:
                d2d(t, j, 2 * px + py, 1 - c).wait_recv()
        for cp in sends:
            cp.wait_send()

    return pl.pallas_call(
        body, name="gather_forward", in_specs=[ANY] * n, out_specs=[ANY] * n,
        out_shape=[jax.ShapeDtypeStruct(g.shape, g.dtype) for g in gathered], input_output_aliases={t: t for t in range(n)},
        scratch_shapes=[pltpu.SemaphoreType.DMA((3 * n,)), pltpu.SemaphoreType.DMA((3 * n,))],
        compiler_params=pltpu.CompilerParams(has_side_effects=True),
    )(*gathered)


def _exchange_halves(gs):
    n = len(gs)

    def body(*refs):
        ins, outs, send_sems, recv_sems = refs[:n], refs[n:2 * n], refs[2 * n], refs[2 * n + 1]
        x, y, c = _mesh_pos()
        cps = [_remote(_rows_half(ins[t], 1 - c), outs[t], send_sems.at[t], recv_sems.at[t], (x, y, 1 - c)) for t in range(n)]
        for cp in cps:
            cp.start()
        for cp in cps:
            cp.wait_recv()
        for cp in cps:
            cp.wait_send()

    out_shape = [jax.ShapeDtypeStruct(g.shape[:2] + (g.shape[2] // 2, g.shape[3]), g.dtype) for g in gs]
    return _comm_call(body, "grad_pair_exchange", gs, out_shape, n)


def _pair_add(g, r, c_idx, *, name):
    nb, d, rows, cols = g.shape
    h = rows // 2
    tr = min(h, 512)
    nt = h // tr

    def body(c_ref, g_ref, r_ref, p_ref, pb_ref):
        s = g_ref[...] + r_ref[...]
        p_ref[...] = s
        pb_ref[...] = s.astype(BF16)

    blk = pl.BlockSpec((1, 1, tr, cols), lambda k, l, i, c_ref: (k, l, i, 0))
    return pl.pallas_call(
        body, name=name,
        grid_spec=pltpu.PrefetchScalarGridSpec(
            num_scalar_prefetch=1, grid=(nb, d, nt),
            in_specs=[pl.BlockSpec((1, 1, tr, cols), lambda k, l, i, c_ref: (k, l, c_ref[0] * nt + i, 0)), blk],
            out_specs=[blk, blk]),
        out_shape=[jax.ShapeDtypeStruct((nb, d, h, cols), F32), jax.ShapeDtypeStruct((nb, d, h, cols), BF16)],
        compiler_params=_cparams(("parallel", "parallel", "parallel")),
    )(c_idx, g, r)


def _exchange_chips_job(pbs):
    n = len(pbs)

    def copies(ins, outs, send_sems, recv_sems):
        x, y, c = _mesh_pos()
        return [_remote(ins[t].at[2 * px + py], outs[t].at[j], send_sems.at[3 * t + j], recv_sems.at[3 * t + j], (px, py, c))
                for t in range(n) for j, (px, py) in enumerate(_other_chips(x, y))]

    return _SideJob(pbs, [jax.ShapeDtypeStruct((3,) + p.shape[1:], p.dtype) for p in pbs], 3 * n, copies, copies)


def _chip_add(p, r, k_idx, *, name):
    _, d, h, cols = p.shape
    tr = min(h, 512)
    nt = h // tr

    def body(k_ref, p_ref, r_ref, o_ref):
        o_ref[0] = ((p_ref[0, 0] + r_ref[0, 0].astype(F32)) + r_ref[1, 0].astype(F32)) + r_ref[2, 0].astype(F32)

    return pl.pallas_call(
        body, name=name,
        grid_spec=pltpu.PrefetchScalarGridSpec(
            num_scalar_prefetch=1, grid=(d, nt),
            in_specs=[pl.BlockSpec((1, 1, tr, cols), lambda l, i, k_ref: (k_ref[0], l, i, 0)),
                      pl.BlockSpec((3, 1, tr, cols), lambda l, i, k_ref: (0, l, i, 0))],
            out_specs=pl.BlockSpec((1, tr, cols), lambda l, i, k_ref: (l, i, 0))),
        out_shape=jax.ShapeDtypeStruct((d, h, cols), F32), compiler_params=_cparams(("parallel", "parallel")),
    )(k_idx, p, r)


def _share_halves(qs):
    n = len(qs)

    def body(*refs):
        ins, outs, send_sems, recv_sems = refs[:n], refs[n:2 * n], refs[2 * n], refs[2 * n + 1]
        x, y, c = _mesh_pos()
        cps = [_remote(ins[t], outs[t], send_sems.at[t], recv_sems.at[t], (x, y, 1 - c)) for t in range(n)]
        for cp in cps:
            cp.start()
        for cp in cps:
            cp.wait_recv()
        for cp in cps:
            cp.wait_send()

    return _comm_call(body, "grad_pair_share", qs, [jax.ShapeDtypeStruct(q.shape, q.dtype) for q in qs], n)


def _all_reduce_small(v):
    r, cols = v.shape
    n_dev = 8

    def body(v_ref, o_ref, buf, send_sems, recv_sems):
        x, y, c = _mesh_pos()
        me = 4 * x + 2 * y + c
        buf[me] = v_ref[...]

        def peer(j):
            return (1 - x if j & 4 else x, 1 - y if j & 2 else y, 1 - c if j & 1 else c)

        def copy(j, slot):
            return pltpu.make_async_remote_copy(src_ref=v_ref, dst_ref=buf.at[slot], send_sem=send_sems.at[j - 1],
                                                recv_sem=recv_sems.at[j - 1], device_id=peer(j), device_id_type=MESH)

        sends = [copy(j, me) for j in range(1, n_dev)]
        for cp in sends:
            cp.start()
        for j in range(1, n_dev):
            px, py, pc = peer(j)
            copy(j, 4 * px + 2 * py + pc).wait_recv()
        for cp in sends:
            cp.wait_send()
        acc = buf[0]
        for d in range(1, n_dev):
            acc = acc + buf[d]
        o_ref[...] = acc

    vm = pl.BlockSpec(memory_space=pltpu.VMEM)
    return pl.pallas_call(
        body, name="small_all_reduce", in_specs=[vm], out_specs=vm, out_shape=jax.ShapeDtypeStruct((r, cols), F32),
        scratch_shapes=[pltpu.VMEM((n_dev, r, cols), F32), pltpu.SemaphoreType.DMA((n_dev - 1,)), pltpu.SemaphoreType.DMA((n_dev - 1,))],
        compiler_params=pltpu.CompilerParams(has_side_effects=True),
    )(v)


_COL_SHARDED = ("w_in", "w_q_up", "w_kv_up", "w_ffn_up")


def _shard_cols(blocks, a, b):
    c = blocks[0].shape[-1]
    out = []
    while a < b:
        k = a // c
        hi = min(b, (k + 1) * c)
        out.append(blocks[k][:, a - k * c:hi - k * c])
        a = hi
    return out


def _pack_w_in_shards(blocks):
    z = lambda n: [jnp.zeros((blocks[0].shape[0], n), blocks[0].dtype)]
    cols = lambda a, b: _shard_cols(blocks, a, b)
    return jnp.concatenate(cols(0, 768) + cols(772, 1028) + cols(1188, 2980) + cols(1028, 1156) + cols(768, 772)
                           + z(KR_LANE - N_HEADS) + cols(1156, 1188) + z(LANES - KR_LANE - ROPE_DIM), axis=1)


def _whole_layer(name, blocks):
    return jnp.concatenate(list(blocks), axis=1 if name in _COL_SHARDED else 0)


def _split_layer(name, whole):
    if name in _COL_SHARDED:
        c = whole.shape[1] // N_CHIPS
        return jnp.stack([whole[:, k * c:(k + 1) * c] for k in range(N_CHIPS)])
    return whole.reshape(N_CHIPS, whole.shape[0] // N_CHIPS, whole.shape[1])


def _small_to_rows(d):
    v = jnp.concatenate([d[k].astype(F32).reshape(-1) for k in SMALL])
    rows = -(-v.shape[0] // (8 * LANES)) * 8
    return jnp.pad(v, (0, rows * LANES - v.shape[0])).reshape(rows, LANES)


def _small_from_rows(rows, shapes):
    v = rows.reshape(-1)
    out, o = {}, 0
    for k in SMALL:
        sz = int(np.prod(shapes[k]))
        out[k] = v[o:o + sz].reshape(shapes[k])
        o += sz
    return out


_ARG_NAMES = ("x", "positions", "g_mix_pre", "w_in", "b_forget", "g_q_lora", "w_q_up", "g_kv_lora", "w_kv_up", "g_mix_out", "w_out",
              "g_mix_post", "g_ffn_pre", "w_ffn_up", "w_ffn_down", "g_ffn_post")
_WEIGHTS = _ARG_NAMES[2:]


def kernel(x, positions, g_mix_pre, w_in, b_forget, g_q_lora, w_q_up, g_kv_lora, w_kv_up, g_mix_out, w_out, g_mix_post, g_ffn_pre, w_ffn_up, w_ffn_down, g_ffn_post, loss_target, m_g_mix_pre, m_w_in, m_b_forget, m_g_q_lora, m_w_q_up, m_g_kv_lora, m_w_kv_up, m_g_mix_out, m_w_out, m_g_mix_post, m_g_ffn_pre, m_w_ffn_up, m_w_ffn_down, m_g_ffn_post, v_g_mix_pre, v_w_in, v_b_forget, v_g_q_lora, v_w_q_up, v_g_kv_lora, v_w_kv_up, v_g_mix_out, v_w_out, v_g_mix_post, v_g_ffn_pre, v_w_ffn_up, v_w_ffn_down, v_g_ffn_post):
    w = dict(g_mix_pre=g_mix_pre, w_in=w_in, b_forget=b_forget, g_q_lora=g_q_lora, w_q_up=w_q_up, g_kv_lora=g_kv_lora, w_kv_up=w_kv_up,
             g_mix_out=g_mix_out, w_out=w_out, g_mix_post=g_mix_post, g_ffn_pre=g_ffn_pre, w_ffn_up=w_ffn_up, w_ffn_down=w_ffn_down,
             g_ffn_post=g_ffn_post)
    m = dict(g_mix_pre=m_g_mix_pre, w_in=m_w_in, b_forget=m_b_forget, g_q_lora=m_g_q_lora, w_q_up=m_w_q_up, g_kv_lora=m_g_kv_lora,
             w_kv_up=m_w_kv_up, g_mix_out=m_g_mix_out, w_out=m_w_out, g_mix_post=m_g_mix_post, g_ffn_pre=m_g_ffn_pre,
             w_ffn_up=m_w_ffn_up, w_ffn_down=m_w_ffn_down, g_ffn_post=m_g_ffn_post)
    v = dict(g_mix_pre=v_g_mix_pre, w_in=v_w_in, b_forget=v_b_forget, g_q_lora=v_g_q_lora, w_q_up=v_w_q_up, g_kv_lora=v_g_kv_lora,
             w_kv_up=v_w_kv_up, g_mix_out=v_g_mix_out, w_out=v_w_out, g_mix_post=v_g_mix_post, g_ffn_pre=v_g_ffn_pre,
             w_ffn_up=v_w_ffn_up, w_ffn_down=v_w_ffn_down, g_ffn_post=v_g_ffn_post)
    shard_shapes = {k: w[k].shape for k in BIG}
    small_shapes = {k: w[k].shape for k in SMALL}
    c_idx = lax.axis_index("c").astype(jnp.int32).reshape(1)
    k_idx = (2 * lax.axis_index("x") + lax.axis_index("y")).astype(jnp.int32).reshape(1)
    first_core = lax.axis_index("c") == 0

    mine = 2 * lax.axis_index("x") + lax.axis_index("y")
    shards_b = [{k: w[k][l:l + 1].astype(BF16) for k in BIG} for l in range(DEPTH)]
    gains = [dict(g_mix_pre=g_mix_pre[l], b_forget=b_forget[l], g_q_lora=g_q_lora[l], g_kv_lora=g_kv_lora[l], g_mix_out=g_mix_out[l],
                  g_mix_post=g_mix_post[l], g_ffn_pre=g_ffn_pre[l], g_ffn_post=g_ffn_post[l]) for l in range(DEPTH)]
    FIRST, LATER = ("w_in", "w_q_up", "w_kv_up"), ("w_out", "w_ffn_up", "w_ffn_down")
    EARLY_GRADS, LATE_GRADS = ("w_ffn_down", "w_ffn_up", "w_out"), ("w_in", "w_q_up", "w_kv_up")

    def gather_job(l, names):
        return _gather_job([shards_b[l][k] for k in names])

    def weights_of(l, names, gathered):
        four = {k: [jnp.where(mine == chip, shards_b[l][k][0], g[chip, 0]) for chip in range(N_CHIPS)]
                for k, g in zip(names, _forward_halves(gathered))}
        out = {}
        for k in names:
            if k == "w_in":
                out["w_in"] = _pack_w_in_shards(four[k])
            elif k == "w_q_up":
                out["wq"] = _pack_w_q(_whole_layer(k, four[k]))
            elif k == "w_kv_up":
                out["wk"], out["wv"] = _pack_w_kv(_whole_layer(k, four[k]))
            else:
                out[k] = _whole_layer(k, four[k])
        return out

    def pair_sums(names, g):
        whole = dict(w_in=lambda: _unpack_dw_in(g["w_in"]), w_q_up=lambda: _unpack_dw_q(g["wq"]),
                     w_kv_up=lambda: _unpack_dw_kv(g["wk"], g["wv"]), w_out=lambda: g["w_out"], w_ffn_up=lambda: g["w_ffn_up"],
                     w_ffn_down=lambda: g["w_ffn_down"])
        blocks = [_split_layer(k, whole[k]())[:, None] for k in names]
        theirs = _exchange_halves(blocks)
        return [_pair_add(b, r, c_idx, name="grad_pair_add_" + k) for k, b, r in zip(names, blocks, theirs)]

    def exchange_job(*pairs):
        return _exchange_chips_job([pb for pair in pairs for (_, pb) in pair])

    def finish_grads(names, pair, partial):
        half = [_chip_add(p, r, k_idx, name="grad_chip_add_" + k) for k, (p, _), r in zip(names, pair, partial)]
        return {k: jnp.where(first_core, jnp.concatenate([q, s], axis=1), jnp.concatenate([s, q], axis=1))
                for k, q, s in zip(names, half, _share_halves(half))}

    seq = x.shape[1]
    tabs = _rope_tables(positions[0].reshape(seq, 1))
    first0 = weights_of(0, FIRST, _run_side_job(gather_job(0, FIRST), "gather_weights_l0"))
    x1, saved0, lw0, gathered1 = _layer_fwd(x[0], {**gains[0], **first0}, tabs, "l0_", fox_side=gather_job(0, LATER),
                                            late_weights=lambda got: weights_of(0, LATER, got), side=gather_job(1, BIG))
    lw1 = {**gains[1], **weights_of(1, BIG, gathered1)}
    x2, saved1, _, _ = _layer_fwd(x1, lw1, tabs, "l1_")
    loss_row, dx = _loss_head(x2, loss_target[0])
    loss = lax.psum(loss_row[0, 0], ("x", "y", "c"))
    dx, grads1, _ = _layer_bwd(dx, lw1, saved1, tabs, "l1_")
    pair1 = pair_sums(BIG, grads1)
    early0 = []

    def beside_l0_backward(g):
        early0.extend(pair_sums(EARLY_GRADS, g))
        return exchange_job(pair1, early0)

    dx, grads0, partial = _layer_bwd(dx, lw0, saved0, tabs, "l0_", side=beside_l0_backward)
    big1 = finish_grads(BIG, pair1, partial[:len(BIG)])
    big0 = finish_grads(EARLY_GRADS, early0, partial[len(BIG):])
    late0 = pair_sums(LATE_GRADS, grads0)
    big0.update(finish_grads(LATE_GRADS, late0, _run_side_job(exchange_job(late0), "grad_chip_exchange_l0")))
    g_big = {k: jnp.concatenate([big0[k], big1[k]], axis=0) for k in BIG}
    grads = [grads0, grads1]

    g_small_local = {k: jnp.stack([grads[l][k].reshape(small_shapes[k][1:]) for l in range(DEPTH)]) for k in SMALL}
    g_small = _small_from_rows(_all_reduce_small(_small_to_rows(g_small_local)), small_shapes)

    g_all = {**g_big, **g_small}
    delta, new_m, new_v = {}, {}, {}
    for k in BIG:
        d, r, c = shard_shapes[k]
        two_d = lambda a: a.reshape(d * r, c)
        dk, mk, vk = _adamw(two_d(w[k]), two_d(g_all[k]), two_d(m[k]), two_d(v[k]), name="adamw_" + k)
        delta[k], new_m[k], new_v[k] = dk.reshape(d, r, c), mk.reshape(d, r, c), vk.reshape(d, r, c)
    ds, ms, vs = _adamw(_small_to_rows(w), _small_to_rows(g_small), _small_to_rows(m), _small_to_rows(v), name="adamw_small")
    delta.update(_small_from_rows(ds, small_shapes))
    new_m.update(_small_from_rows(ms, small_shapes))
    new_v.update(_small_from_rows(vs, small_shapes))

    grad_x = dx.reshape(x.shape)
    return (loss, grad_x, *[g_all[k] for k in _WEIGHTS], *[delta[k] for k in _WEIGHTS], *[new_m[k] for k in _WEIGHTS],
            *[new_v[k] for k in _WEIGHTS])
```

```python
import functools
import math

import numpy as np
import jax
import jax.numpy as jnp
from jax import lax
from jax.experimental import pallas as pl
from jax.experimental.pallas import tpu as pltpu

F32 = jnp.float32
BF16 = jnp.bfloat16
MESH = pl.DeviceIdType.MESH

D_MODEL = 1024
DEPTH = 2
CHUNK = 64
GROUP = 256
HEAD = 64
N_HEADS = 4
Q_RANK = 256
KV_RANK = 128
ROPE_DIM = 32
D_FF = 4096
D_IN = 2980
D_INP = 3072
ROPE_BASE = 10000.0
EPS = 1e-6
LANES = 128
TQ = 128
NEG = -1e30

ADAM_LR, ADAM_B1, ADAM_B2, ADAM_EPS, ADAM_WD, ADAM_STEP = 0.001, 0.9, 0.999, 1e-08, 0.01, 10

OFF_FQ, OFF_FK, OFF_FV, OFF_CQ = 0, 2, 4, 6
OFF_RQ, OFF_RK, OFF_RV, OFF_RG = 8, 10, 12, 14
OFF_SQ, OFF_SK, OFF_SV = 16, 18, 20
OFF_CKV, OFF_MISC = 22, 23
FF_LANE, KR_LANE = 0, 64

VMEM_LIMIT = 56 * 1024 * 1024


def _tile(dim, pref):
    return pref if dim % pref == 0 else dim


def _cparams(sem, vmem=None):
    return pltpu.CompilerParams(dimension_semantics=sem, vmem_limit_bytes=vmem or VMEM_LIMIT)


def _dot(a, b):
    return jnp.dot(a, b, preferred_element_type=F32)


def _dot_nt(a, b):
    return lax.dot_general(a, b, (((1,), (1,)), ((), ())), preferred_element_type=F32)


def _dot_tn(a, b):
    return lax.dot_general(a, b, (((0,), (0,)), ((), ())), preferred_element_type=F32)


def _dot_exact(a, b):
    return jnp.dot(a, b, precision=lax.Precision.HIGHEST, preferred_element_type=F32)


def _matmul(a, b, *, name, ta=False, tb=False, out_dtype=F32, tm=1024, tn=1024, tk=1024,
            relu2=False, relu2_of=None, also_bf16=False, side=None):
    if ta:
        kdim, m = a.shape
    else:
        m, kdim = a.shape
    n = b.shape[0] if tb else b.shape[1]
    tm, tn, tk = _tile(m, tm), _tile(n, tn), _tile(kdim, tk)
    nk = kdim // tk
    a_spec = pl.BlockSpec((tk, tm), lambda i, j, k: (k, i)) if ta else pl.BlockSpec((tm, tk), lambda i, j, k: (i, k))
    b_spec = pl.BlockSpec((tn, tk), lambda i, j, k: (j, k)) if tb else pl.BlockSpec((tk, tn), lambda i, j, k: (k, j))
    o_spec = pl.BlockSpec((tm, tn), lambda i, j, k: (i, j))
    two = also_bf16

    def body(*refs):
        refs = list(refs)
        a_ref, b_ref = refs[0], refs[1]
        e_ref = refs[2] if relu2_of is not None else None
        pos = 3 if relu2_of is not None else 2
        o_ref = refs[pos]
        o2_ref = refs[pos + 1] if two else None
        acc_ref = refs[-1]
        k = pl.program_id(2)
        av = a_ref[...].astype(BF16)
        bv = b_ref[...].astype(BF16)
        if ta:
            part = _dot_tn(av, bv)
        elif tb:
            part = _dot_nt(av, bv)
        else:
            part = _dot(av, bv)

        @pl.when(k == 0)
        def _():
            acc_ref[...] = part

        @pl.when(k > 0)
        def _():
            acc_ref[...] += part

        @pl.when(k == nk - 1)
        def _():
            r = acc_ref[...]
            if relu2_of is not None:
                r = r * (2.0 * jnp.sqrt(e_ref[...].astype(F32)))
            if relu2:
                r = jnp.square(jnp.maximum(r, 0.0))
            o_ref[...] = r.astype(o_ref.dtype)
            if also_bf16:
                o2_ref[...] = r.astype(BF16)

    in_specs = [a_spec, b_spec]
    args = [a, b]
    if relu2_of is not None:
        in_specs.append(o_spec)
        args.append(relu2_of)
    out_shape = [jax.ShapeDtypeStruct((m, n), out_dtype)]
    out_specs = [o_spec]
    if two:
        out_shape.append(jax.ShapeDtypeStruct((m, n), BF16))
        out_specs.append(o_spec)
    grid = (m // tm, n // tn, nk)
    side_in, side_out, side_scratch = _side_specs(side)
    res = pl.pallas_call(
        _carry_side_job(body, len(args), len(out_shape), side, grid), name=name, grid=grid,
        in_specs=in_specs + side_in, out_specs=out_specs + side_out,
        out_shape=out_shape + ([] if side is None else side.out_shape),
        scratch_shapes=[pltpu.VMEM((tm, tn), F32)] + side_scratch,
        compiler_params=_cparams(("parallel", "parallel", "arbitrary") if side is None else ("arbitrary",) * 3),
    )(*args, *([] if side is None else side.inputs))
    main = res[:len(out_shape)]
    main = main if two else main[0]
    return main if side is None else (main, res[len(out_shape):])


def _rms(x, g):
    r = lax.rsqrt(jnp.mean(x * x, axis=-1, keepdims=True) + EPS)
    return x * r * g


def _rms_bwd(x, g, dy):
    r = lax.rsqrt(jnp.mean(x * x, axis=-1, keepdims=True) + EPS)
    xh = x * r
    gdy = dy * g
    dx = r * (gdy - xh * jnp.mean(xh * gdy, axis=-1, keepdims=True))
    return dx, xh * dy


def _norm_fwd(x, g, *, name, resid=None, out_dtype=BF16):
    s, d = x.shape
    tr = _tile(s, 256)
    row = pl.BlockSpec((tr, d), lambda i: (i, 0))
    gsp = pl.BlockSpec((1, d), lambda i: (0, 0))

    def body(*refs):
        if resid is None:
            x_ref, g_ref, o_ref = refs
            o_ref[...] = _rms(x_ref[...], g_ref[...]).astype(o_ref.dtype)
        else:
            x_ref, g_ref, r_ref, o_ref = refs
            o_ref[...] = (r_ref[...] + _rms(x_ref[...], g_ref[...])).astype(o_ref.dtype)

    args = [x, g.reshape(1, d)] + ([] if resid is None else [resid])
    return pl.pallas_call(
        body, name=name, grid=(s // tr,), in_specs=[row, gsp] + ([] if resid is None else [row]),
        out_specs=row, out_shape=jax.ShapeDtypeStruct((s, d), out_dtype), compiler_params=_cparams(("parallel",)),
    )(*args)


def _norm_bwd(x, g, dy, *, name, add=None, out_dtype=F32):
    s, d = x.shape
    tr = _tile(s, 256)
    row = pl.BlockSpec((tr, d), lambda i: (i, 0))
    gsp = pl.BlockSpec((1, d), lambda i: (0, 0))

    def body(*refs):
        if add is None:
            x_ref, g_ref, dy_ref, dx_ref, dg_ref = refs
        else:
            x_ref, g_ref, dy_ref, add_ref, dx_ref, dg_ref = refs
        dx, gterm = _rms_bwd(x_ref[...], g_ref[...], dy_ref[...].astype(F32))
        if add is not None:
            dx = dx + add_ref[...]
        dx_ref[...] = dx.astype(dx_ref.dtype)

        @pl.when(pl.program_id(0) == 0)
        def _():
            dg_ref[...] = jnp.zeros_like(dg_ref)

        dg_ref[...] += jnp.sum(gterm, axis=0, keepdims=True)

    args = [x, g.reshape(1, d), dy] + ([] if add is None else [add])
    return pl.pallas_call(
        body, name=name, grid=(s // tr,), in_specs=[row, gsp, row] + ([] if add is None else [row]),
        out_specs=[row, gsp], out_shape=[jax.ShapeDtypeStruct((s, d), out_dtype), jax.ShapeDtypeStruct((1, d), F32)],
        compiler_params=_cparams(("arbitrary",)),
    )(*args)


def _loss_head(y, target):
    s, d = y.shape
    tr = _tile(s, 256)
    row = pl.BlockSpec((tr, d), lambda i: (i, 0))
    lsp = pl.BlockSpec((1, LANES), lambda i: (0, 0))

    def body(y_ref, t_ref, l_ref, dy_ref):
        e = y_ref[...] - t_ref[...]
        dy_ref[...] = e * (1.0 / d)

        @pl.when(pl.program_id(0) == 0)
        def _():
            l_ref[...] = jnp.zeros_like(l_ref)

        part = 0.5 * jnp.sum(jnp.mean(e * e, axis=-1, keepdims=True), axis=0, keepdims=True)
        l_ref[...] += jnp.broadcast_to(part, (1, LANES))

    return pl.pallas_call(
        body, name="loss_head", grid=(s // tr,), in_specs=[row, row], out_specs=[lsp, row],
        out_shape=[jax.ShapeDtypeStruct((1, LANES), F32), jax.ShapeDtypeStruct((s, d), F32)],
        compiler_params=_cparams(("arbitrary",)),
    )(y, target)


def _rope_tables(pos_col):
    s = pos_col.shape[0]
    tr = _tile(s, 512)
    f_mla = ROPE_BASE ** (-jnp.arange(ROPE_DIM // 2, dtype=F32) / (ROPE_DIM // 2))
    f_ret = ROPE_BASE ** (-jnp.arange(HEAD // 2, dtype=F32) / (HEAD // 2))
    fm = jnp.concatenate([jnp.zeros((64,), F32), f_mla, f_mla, jnp.zeros((32,), F32)]).reshape(1, LANES)
    fr = jnp.tile(jnp.concatenate([f_ret, f_ret]), 4).reshape(1, 2 * LANES)

    def body(p_ref, fm_ref, fr_ref, cm_ref, sm_ref, cr_ref, sr_ref):
        p = p_ref[...].astype(F32)
        am = p * fm_ref[...]
        ar = p * fr_ref[...]
        cm_ref[...] = jnp.cos(am)
        sm_ref[...] = jnp.sin(am)
        cr_ref[...] = jnp.cos(ar)
        sr_ref[...] = jnp.sin(ar)

    return pl.pallas_call(
        body, name="rope_tables", grid=(s // tr,),
        in_specs=[pl.BlockSpec((tr, 1), lambda i: (i, 0)), pl.BlockSpec((1, LANES), lambda i: (0, 0)),
                  pl.BlockSpec((1, 2 * LANES), lambda i: (0, 0))],
        out_specs=[pl.BlockSpec((tr, LANES), lambda i: (i, 0))] * 2 + [pl.BlockSpec((tr, 2 * LANES), lambda i: (i, 0))] * 2,
        out_shape=[jax.ShapeDtypeStruct((s, LANES), F32)] * 2 + [jax.ShapeDtypeStruct((s, 2 * LANES), F32)] * 2,
        compiler_params=_cparams(("parallel",)),
    )(pos_col, fm, fr)


def _lane(shape):
    return lax.broadcasted_iota(jnp.int32, shape, len(shape) - 1)


def _rot_mla(z):
    l = _lane(z.shape) % LANES
    n = z.shape[-1]
    return jnp.where(l < 80, -pltpu.roll(z, n - 16, 1), pltpu.roll(z, 16, 1))


def _rot_mla_t(y):
    l = _lane(y.shape) % LANES
    n = y.shape[-1]
    return jnp.where((l >= 64) & (l < 80), pltpu.roll(y, n - 16, 1),
                     jnp.where((l >= 80) & (l < 96), -pltpu.roll(y, 16, 1), 0.0))


def _rot_ret(z):
    l = _lane(z.shape) % HEAD
    n = z.shape[-1]
    return jnp.where(l < 32, -pltpu.roll(z, n - 32, 1), pltpu.roll(z, 32, 1))


def _rot_ret_t(y):
    l = _lane(y.shape) % HEAD
    n = y.shape[-1]
    return jnp.where(l < 32, pltpu.roll(y, n - 32, 1), -pltpu.roll(y, 32, 1))


def _log_sigmoid(x):
    return jnp.minimum(x, 0.0) - jnp.log1p(jnp.exp(-jnp.abs(x)))


def _fox_cum(proj, bias_row):
    s = proj.shape[0]
    nb = s // TQ

    def body(x_ref, b_ref, cc_ref, cr_ref, carry_ref):
        @pl.when(pl.program_id(0) == 0)
        def _():
            carry_ref[...] = jnp.zeros_like(carry_ref)

        ls = _log_sigmoid(x_ref[...] + b_ref[...])
        r = lax.broadcasted_iota(jnp.int32, (TQ, TQ), 0)
        c = lax.broadcasted_iota(jnp.int32, (TQ, TQ), 1)
        tri = (c <= r).astype(F32)
        cum = _dot_exact(tri, ls) + carry_ref[...]
        carry_ref[...] = cum[TQ - 1:TQ, :]
        cc_ref[...] = cum
        cr_ref[...] = cum.T[0:8, :]

    return pl.pallas_call(
        body, name="fox_cum", grid=(nb,),
        in_specs=[pl.BlockSpec((TQ, LANES), lambda i: (i, OFF_MISC)), pl.BlockSpec((1, LANES), lambda i: (0, 0))],
        out_specs=[pl.BlockSpec((TQ, LANES), lambda i: (i, 0)), pl.BlockSpec((8, TQ), lambda i: (0, i))],
        out_shape=[jax.ShapeDtypeStruct((s, LANES), F32), jax.ShapeDtypeStruct((8, s), F32)],
        scratch_shapes=[pltpu.VMEM((1, LANES), F32)],
        compiler_params=_cparams(("arbitrary",)),
    )(proj, bias_row)


def _fox_gate_bwd(dck, drs, proj, bias_row, dkr):
    s = proj.shape[0]
    nb = s // TQ

    def body(d_ref, r_ref, x_ref, b_ref, k_ref, o_ref, db_ref, carry_ref):
        @pl.when(pl.program_id(0) == 0)
        def _():
            carry_ref[...] = jnp.zeros_like(carry_ref)
            db_ref[...] = jnp.zeros_like(db_ref)

        rows = jnp.concatenate([d_ref[0], d_ref[1], jnp.zeros((TQ - 16, TQ), F32)], axis=0)
        t = rows.T
        l = _lane((TQ, LANES))
        r0, r1 = r_ref[0], r_ref[1]
        rsum = jnp.where(l == 0, r0[:, 0:1], jnp.where(l == 1, r0[:, HEAD:HEAD + 1],
                         jnp.where(l == 2, r1[:, 0:1], jnp.where(l == 3, r1[:, HEAD:HEAD + 1], 0.0))))
        dcum = rsum - jnp.where(l < 2, t, pltpu.roll(t, LANES - 6, 1))
        r = lax.broadcasted_iota(jnp.int32, (TQ, TQ), 0)
        c = lax.broadcasted_iota(jnp.int32, (TQ, TQ), 1)
        triu = (c >= r).astype(F32)
        rc = _dot_exact(triu, dcum) + carry_ref[...]
        carry_ref[...] = rc[0:1, :]
        f = x_ref[...] + b_ref[...]
        sig_neg = 1.0 / (1.0 + jnp.exp(f))
        df = jnp.where(l < N_HEADS, rc * sig_neg, 0.0)
        db_ref[...] += jnp.sum(df, axis=0, keepdims=True)
        o_ref[...] = (df + k_ref[...]).astype(o_ref.dtype)

    rev = lambda i: nb - 1 - i
    return pl.pallas_call(
        body, name="fox_gate_bwd", grid=(nb,),
        in_specs=[pl.BlockSpec((2, 8, TQ), lambda i: (0, 0, rev(i))), pl.BlockSpec((2, TQ, LANES), lambda i: (0, rev(i), 0)),
                  pl.BlockSpec((TQ, LANES), lambda i: (rev(i), OFF_MISC)),
                  pl.BlockSpec((1, LANES), lambda i: (0, 0)), pl.BlockSpec((TQ, LANES), lambda i: (rev(i), 0))],
        out_specs=[pl.BlockSpec((TQ, LANES), lambda i: (rev(i), 0)), pl.BlockSpec((1, LANES), lambda i: (0, 0))],
        out_shape=[jax.ShapeDtypeStruct((s, LANES), BF16), jax.ShapeDtypeStruct((1, LANES), F32)],
        scratch_shapes=[pltpu.VMEM((1, LANES), F32)],
        compiler_params=_cparams(("arbitrary",)),
    )(dck, drs, proj, bias_row, dkr)


def _mla_prep(proj, cos_m, sin_m, g_q, g_kv, wq, wk, wv):
    s = proj.shape[0]
    tr = _tile(s, 256)

    def body(cq_ref, ckv_ref, misc_ref, cos_ref, sin_ref, gq_ref, gkv_ref, wq_ref, wk_ref, wv_ref,
             q_ref, k_ref, v_ref, cqn_ref, ckvn_ref):
        cos4 = jnp.tile(cos_ref[...], (1, 4))
        sin4 = jnp.tile(sin_ref[...], (1, 4))
        cqn = _rms(cq_ref[...], gq_ref[...]).astype(BF16)
        ckvn = _rms(ckv_ref[...], gkv_ref[...]).astype(BF16)
        cqn_ref[...] = cqn
        ckvn_ref[...] = ckvn
        zq = _dot(cqn, wq_ref[...])
        q_ref[...] = (zq * cos4 + _rot_mla(zq) * sin4).astype(BF16)
        l = _lane((tr, LANES))
        kr = jnp.where((l >= KR_LANE) & (l < KR_LANE + ROPE_DIM), misc_ref[...], 0.0)
        zk = _dot(ckvn, wk_ref[...]) + jnp.tile(kr, (1, 4))
        k_ref[...] = (zk * cos4 + _rot_mla(zk) * sin4).astype(BF16)
        v_ref[...] = _dot(ckvn, wv_ref[...]).astype(BF16)

    full = lambda a: pl.BlockSpec(a.shape, lambda i: (0, 0))
    rowb = lambda w: pl.BlockSpec((tr, w), lambda i: (i, 0))
    gq2, gkv2 = g_q.reshape(1, Q_RANK), g_kv.reshape(1, KV_RANK)
    return pl.pallas_call(
        body, name="mla_prep", grid=(s // tr,),
        in_specs=[pl.BlockSpec((tr, 256), lambda i: (i, OFF_CQ // 2)), pl.BlockSpec((tr, LANES), lambda i: (i, OFF_CKV)),
                  pl.BlockSpec((tr, LANES), lambda i: (i, OFF_MISC)), rowb(LANES), rowb(LANES),
                  full(gq2), full(gkv2), full(wq), full(wk), full(wv)],
        out_specs=[rowb(512), rowb(512), rowb(512), rowb(256), rowb(128)],
        out_shape=[jax.ShapeDtypeStruct((s, 512), BF16), jax.ShapeDtypeStruct((s, 512), BF16), jax.ShapeDtypeStruct((s, 512), BF16),
                   jax.ShapeDtypeStruct((s, 256), BF16), jax.ShapeDtypeStruct((s, 128), BF16)],
        compiler_params=_cparams(("parallel",)),
    )(proj, proj, proj, cos_m, sin_m, gq2, gkv2, wq, wk, wv)


def _mla_prep_bwd(dq, dk, dv, proj, cqn, ckvn, cos_m, sin_m, g_q, g_kv, wq, wk, wv):
    s = proj.shape[0]
    tr = _tile(s, 256)

    def body(dq_ref, dk_ref, dv_ref, cq_ref, ckv_ref, cqn_ref, ckvn_ref, cos_ref, sin_ref, gq_ref, gkv_ref,
             wq_ref, wk_ref, wv_ref, dcq_ref, dckv_ref, dkr_ref, dwq_ref, dwk_ref, dwv_ref, dgq_ref, dgkv_ref):
        @pl.when(pl.program_id(0) == 0)
        def _():
            for r in (dwq_ref, dwk_ref, dwv_ref, dgq_ref, dgkv_ref):
                r[...] = jnp.zeros_like(r)

        cos4 = jnp.tile(cos_ref[...], (1, 4))
        sin4 = jnp.tile(sin_ref[...], (1, 4))
        dqv = dq_ref[...]
        dzq = dqv * cos4 + _rot_mla_t(dqv * sin4)
        dkv_ = dk_ref[...]
        dzk = dkv_ * cos4 + _rot_mla_t(dkv_ * sin4)
        l = _lane((tr, LANES))
        in_rope = (l >= KR_LANE) & (l < KR_LANE + ROPE_DIM)
        dkr = dzk[:, 0:128] + dzk[:, 128:256] + dzk[:, 256:384] + dzk[:, 384:512]
        dkr_ref[...] = jnp.where(in_rope, dkr, 0.0)
        dzq_b = dzq.astype(BF16)
        dzk_b = dzk.astype(BF16)
        dv_b = dv_ref[...].astype(BF16)
        dcqn = _dot_nt(dzq_b, wq_ref[...])
        dckvn = _dot_nt(dzk_b, wk_ref[...]) + _dot_nt(dv_b, wv_ref[...])
        dwq_ref[...] += _dot_tn(cqn_ref[...], dzq_b)
        dwk_ref[...] += _dot_tn(ckvn_ref[...], dzk_b)
        dwv_ref[...] += _dot_tn(ckvn_ref[...], dv_b)
        dcq, gq_term = _rms_bwd(cq_ref[...], gq_ref[...], dcqn)
        dckv, gkv_term = _rms_bwd(ckv_ref[...], gkv_ref[...], dckvn)
        dcq_ref[...] = dcq.astype(BF16)
        dckv_ref[...] = dckv.astype(BF16)
        dgq_ref[...] += jnp.sum(gq_term, axis=0, keepdims=True)
        dgkv_ref[...] += jnp.sum(gkv_term, axis=0, keepdims=True)

    full = lambda shp: pl.BlockSpec(shp, lambda i: (0, 0))
    rowb = lambda w: pl.BlockSpec((tr, w), lambda i: (i, 0))
    gq2, gkv2 = g_q.reshape(1, Q_RANK), g_kv.reshape(1, KV_RANK)
    return pl.pallas_call(
        body, name="mla_prep_bwd", grid=(s // tr,),
        in_specs=[rowb(512), rowb(512), rowb(512),
                  pl.BlockSpec((tr, 256), lambda i: (i, OFF_CQ // 2)), pl.BlockSpec((tr, LANES), lambda i: (i, OFF_CKV)),
                  rowb(256), rowb(128), rowb(LANES), rowb(LANES), full((1, Q_RANK)), full((1, KV_RANK)),
                  full(wq.shape), full(wk.shape), full(wv.shape)],
        out_specs=[rowb(256), rowb(128), rowb(128), full(wq.shape), full(wk.shape), full(wv.shape),
                   full((1, Q_RANK)), full((1, KV_RANK))],
        out_shape=[jax.ShapeDtypeStruct((s, 256), BF16), jax.ShapeDtypeStruct((s, 128), BF16), jax.ShapeDtypeStruct((s, 128), F32),
                   jax.ShapeDtypeStruct(wq.shape, F32), jax.ShapeDtypeStruct(wk.shape, F32), jax.ShapeDtypeStruct(wv.shape, F32),
                   jax.ShapeDtypeStruct((1, Q_RANK), F32), jax.ShapeDtypeStruct((1, KV_RANK), F32)],
        compiler_params=_cparams(("arbitrary",)),
    )(dq, dk, dv, proj, proj, cqn, ckvn, cos_m, sin_m, gq2, gkv2, wq, wk, wv)


def _ret_prep(proj, cos_r, sin_r):
    s = proj.shape[0]
    tr = _tile(s, 256)

    def body(q_ref, k_ref, cos_ref, sin_ref, qo_ref, ko_ref):
        cos, sin = cos_ref[...], sin_ref[...]
        q, k = q_ref[...], k_ref[...]
        qo_ref[...] = (q * cos + _rot_ret(q) * sin).astype(BF16)
        ko_ref[...] = ((k * cos + _rot_ret(k) * sin) * (HEAD ** -0.5)).astype(BF16)

    rowb = pl.BlockSpec((tr, 256), lambda i: (i, 0))
    return pl.pallas_call(
        body, name="ret_prep", grid=(s // tr,),
        in_specs=[pl.BlockSpec((tr, 256), lambda i: (i, OFF_RQ // 2)), pl.BlockSpec((tr, 256), lambda i: (i, OFF_RK // 2)), rowb, rowb],
        out_specs=[rowb, rowb], out_shape=[jax.ShapeDtypeStruct((s, 256), BF16)] * 2,
        compiler_params=_cparams(("parallel",)),
    )(proj, proj, cos_r, sin_r)


def _ret_prep_bwd(dq, dk, cos_r, sin_r):
    s = dq.shape[0]
    tr = _tile(s, 256)

    def body(dq_ref, dk_ref, cos_ref, sin_ref, qo_ref, ko_ref):
        cos, sin = cos_ref[...], sin_ref[...]
        q, k = dq_ref[...], dk_ref[...] * (HEAD ** -0.5)
        qo_ref[...] = (q * cos + _rot_ret_t(q * sin)).astype(BF16)
        ko_ref[...] = (k * cos + _rot_ret_t(k * sin)).astype(BF16)

    rowb = pl.BlockSpec((tr, 256), lambda i: (i, 0))
    return pl.pallas_call(
        body, name="ret_prep_bwd", grid=(s // tr,), in_specs=[rowb] * 4, out_specs=[rowb, rowb],
        out_shape=[jax.ShapeDtypeStruct((s, 256), BF16)] * 2, compiler_params=_cparams(("parallel",)),
    )(dq, dk, cos_r, sin_r)


_LOG_GAMMA = [float(np.log1p(-np.float32(2.0) ** np.float32(-5.0 - h))) for h in range(N_HEADS)]
_MLA_SCALE = float((HEAD + ROPE_DIM) ** -0.5)
_QK_SCALE = float(HEAD ** -0.5)
KEY_BLOCKS = 4
QB = 256


def _split2(x):
    h = x.astype(BF16)
    return h, (x - h.astype(F32)).astype(BF16)


def _dot2(x, u):
    h, lo = _split2(x)
    return _dot(h, u) + _dot(lo, u)


def _head_pick(block, head, axis):
    idx = lax.broadcasted_iota(jnp.int32, block.shape, axis)
    return jnp.sum(jnp.where(idx == head, block, 0.0), axis=axis, keepdims=True)


def _log_gamma_of(head):
    lg = jnp.float32(_LOG_GAMMA[3])
    for h in (2, 1, 0):
        lg = jnp.where(head == h, jnp.float32(_LOG_GAMMA[h]), lg)
    return lg


def _mixer_specs(mode, s, q_off, k_off, v_off):
    nhb = 2
    bw = 2 * LANES if mode == "mla" else LANES
    nsub = KEY_BLOCKS if (s // TQ) % KEY_BLOCKS == 0 else 1
    q_spec = pl.BlockSpec((QB, bw), lambda p, i: (i, q_off + p))
    k_spec = pl.BlockSpec((s, bw), lambda p, i: (0, k_off + p))
    v_spec = pl.BlockSpec((s, bw), lambda p, i: (0, v_off + p))
    return nhb, N_HEADS // nhb, nsub, q_spec, k_spec, v_spec


def _mixer_geometry(mode, i, nsub):
    w = TQ * nsub
    row = lax.broadcasted_iota(jnp.int32, (QB, w), 0)
    col = lax.broadcasted_iota(jnp.int32, (QB, w), 1)
    nfull = (i * QB) // w
    dist = col - row
    if mode in ("fox", "sb"):
        rel = dist
    else:
        rel = col - (row | (CHUNK - 1))

    def visible(c):
        off = c * w - i * QB
        return (rel + off) < 0 if mode == "sb" else (rel + off) <= 0

    return nfull, dist, visible


class _SideJob:
    def __init__(self, inputs, out_shape, n_sems, sends, recvs):
        self.inputs, self.out_shape, self.n_sems, self.sends, self.recvs = list(inputs), list(out_shape), n_sems, sends, recvs


def _carry_side_job(body, n_in, n_out, side, n_steps):
    if side is None:
        return body
    si, so = len(side.inputs), len(side.out_shape)

    def at(corner):
        ok = pl.program_id(0) == corner[0]
        for d in range(1, len(n_steps)):
            ok = ok & (pl.program_id(d) == corner[d])
        return ok

    def wrapped(*refs):
        ins, s_ins = refs[:n_in], refs[n_in:n_in + si]
        outs, s_outs = refs[n_in + si:n_in + si + n_out], refs[n_in + si + n_out:n_in + si + n_out + so]
        scratch, send, recv = refs[n_in + si + n_out + so:-2], refs[-2], refs[-1]

        @pl.when(at([0] * len(n_steps)))
        def _():
            for cp in side.sends(s_ins, s_outs, send, recv):
                cp.start()

        body(*ins, *outs, *scratch)

        @pl.when(at([n - 1 for n in n_steps]))
        def _():
            for cp in side.recvs(s_ins, s_outs, send, recv):
                cp.wait_recv()
            for cp in side.sends(s_ins, s_outs, send, recv):
                cp.wait_send()

    return wrapped


def _side_specs(side):
    if side is None:
        return [], [], []
    hbm = pl.BlockSpec(memory_space=pl.ANY)
    return ([hbm] * len(side.inputs), [hbm] * len(side.out_shape),
            [pltpu.SemaphoreType.DMA((side.n_sems,)), pltpu.SemaphoreType.DMA((side.n_sems,))])


def _mixer_fwd(mode, qa, q_off, ka, k_off, va, v_off, *, cum_col=None, cum_row=None, side=None):
    s = qa.shape[0]
    nq = s // QB
    nhb, nblk, nsub, q_spec, k_spec, v_spec = _mixer_specs(mode, s, q_off, k_off, v_off)
    w = TQ * nsub
    softmax = mode in ("fox", "mla")
    has_stat = mode != "ret"

    def body(*refs):
        refs = list(refs)
        q_ref, k_ref, v_ref = refs[:3]
        refs = refs[3:]
        if mode == "fox":
            cc_ref, cr_ref = refs[:2]
            refs = refs[2:]
        o_ref = refs[0]
        st_ref = refs[1] if has_stat else None
        p = pl.program_id(0)
        i = pl.program_id(1)
        nfull, dist, visible = _mixer_geometry(mode, i, nsub)
        lane = _lane((1, LANES))
        heads = [nhb * p + hh for hh in range(nhb)]
        wide = mode == "mla"
        q_scale = _QK_SCALE if mode in ("fox", "sb") else 1.0
        cols = [slice(hh * LANES, (hh + 1) * LANES) if wide else slice(None) for hh in range(nhb)]
        if wide:
            qs = [q_ref[:, cols[hh]] for hh in range(nhb)]
        else:
            qf = q_ref[...].astype(F32) * q_scale
            qs = [jnp.where((lane // HEAD) == hh, qf, 0.0).astype(BF16) for hh in range(nhb)]
        if mode == "fox":
            cqs = [_head_pick(cc_ref[...], h, 1) for h in heads]
        if mode == "sb":
            r1 = lax.broadcasted_iota(jnp.int32, (TQ, TQ), 0)
            c1 = lax.broadcasted_iota(jnp.int32, (TQ, TQ), 1)
            u_after = (r1 > c1).astype(BF16)

        def chunk(c):
            return pl.ds(pl.multiple_of(c * w, w), w)

        def scores(c):
            js = chunk(c)
            return tuple(_dot_nt(qs[hh], k_ref[js, cols[hh]]) for hh in range(nhb))

        def head_step(hh, c, js, sc, vj, carry, last):
            if softmax:
                m, l, acc = carry
                if mode == "fox":
                    ck = _head_pick(cr_ref[:, js], heads[hh], 0)
                    sc = sc + (cqs[hh] - ck)
                else:
                    sc = sc * _MLA_SCALE
                if last:
                    sc = jnp.where(visible(c), sc, NEG)
                m_new = jnp.maximum(m, jnp.max(sc, axis=-1, keepdims=True))
                alpha = jnp.exp(m - m_new)
                pr = jnp.exp(sc - m_new)
                l = alpha * l + jnp.sum(pr, axis=-1, keepdims=True)
                acc = alpha * acc + _dot(pr.astype(BF16), vj)
                return m_new, l, acc
            run, acc = carry
            z = sc
            log_beta = jnp.minimum(z, 0.0) - jnp.log(1.0 + jnp.exp(-jnp.abs(z)))
            log_stay = log_beta - z
            if last:
                vis = visible(c)
                log_stay = jnp.where(vis, log_stay, 0.0)
            parts = [None] * nsub
            for b in reversed(range(nsub)):
                ls_b = log_stay[:, b * TQ:(b + 1) * TQ]
                parts[b] = _dot2(ls_b, u_after) + run
                run = run + jnp.sum(ls_b, axis=-1, keepdims=True)
            later = parts[0] if nsub == 1 else jnp.concatenate(parts, axis=1)
            wgt = jnp.exp(log_beta + later)
            if last:
                wgt = jnp.where(vis, wgt, 0.0)
            return run, acc + _dot(wgt.astype(BF16), vj)

        def step(c, c_next, state, last):
            scs, carries = state
            nxt = scores(c_next) if c_next is not None else None
            js = chunk(c)
            return nxt, tuple(head_step(hh, c, js, scs[hh], v_ref[js, cols[hh]], carries[hh], last) for hh in range(nhb))

        zero_acc = jnp.zeros((QB, LANES), F32)
        zero1 = jnp.zeros((QB, 1), F32)
        if softmax:
            init = tuple((jnp.full((QB, 1), NEG, F32), zero1, zero_acc) for _ in range(nhb))
        else:
            init = tuple((zero1, zero_acc) for _ in range(nhb))
        if mode == "sb":
            state = step(nfull, jnp.maximum(nfull - 1, 0), (scores(nfull), init), True)
            _, carries = lax.fori_loop(0, nfull, lambda t, st: step(nfull - 1 - t, jnp.maximum(nfull - 2 - t, 0), st, False), state)
        else:
            state = lax.fori_loop(0, nfull, lambda c, st: step(c, c + 1, st, False), (scores(0), init))
            _, carries = step(nfull, None, state, True)
        if softmax:
            outs = [acc / l for (m, l, acc) in carries]
            stats = [m + jnp.log(l) for (m, l, acc) in carries]
        else:
            outs, stats = [acc for (run, acc) in carries], [run for (run, acc) in carries]
        hm0 = (lane // HEAD) == 0
        pick = lambda a: jnp.where(hm0, a[0], a[1])
        if wide:
            for hh in range(nhb):
                o_ref[:, cols[hh]] = outs[hh]
        else:
            o_ref[...] = pick(outs)
        if has_stat:
            st_ref[0] = pick(stats)

    in_specs = [q_spec, k_spec, v_spec]
    args = [qa, ka, va]
    if mode == "fox":
        in_specs += [pl.BlockSpec((QB, LANES), lambda p, i: (i, 0)), pl.BlockSpec((8, s), lambda p, i: (0, 0))]
        args += [cum_col, cum_row]
    bw = 2 * LANES if mode == "mla" else LANES
    out_specs = [pl.BlockSpec((QB, bw), lambda p, i: (i, p))]
    out_shape = [jax.ShapeDtypeStruct((s, nblk * bw), F32)]
    out_specs.append(pl.BlockSpec((1, QB, LANES), lambda p, i: (p, i, 0)))
    out_shape.append(jax.ShapeDtypeStruct((nblk, s, LANES), F32))
    side_in, side_out, side_scratch = _side_specs(side)
    res = pl.pallas_call(
        _carry_side_job(body, len(args), len(out_shape), side, (nblk, nq)), name=mode + "_fwd", grid=(nblk, nq),
        in_specs=in_specs + side_in, out_specs=out_specs + side_out,
        out_shape=out_shape + ([] if side is None else side.out_shape), scratch_shapes=side_scratch,
        compiler_params=_cparams(("parallel", "parallel") if side is None else ("arbitrary", "arbitrary")),
    )(*args, *([] if side is None else side.inputs))
    return (res[0], res[1]) if side is None else (res[0], res[1], res[2:])


def _mixer_bwd(mode, qa, q_off, ka, k_off, va, v_off, o, do, *, stat=None, cum_col=None, cum_row=None, side=None):
    s = qa.shape[0]
    nq = s // QB
    nhb, nblk, nsub, q_spec, k_spec, v_spec = _mixer_specs(mode, s, q_off, k_off, v_off)
    w = TQ * nsub
    softmax = mode in ("fox", "mla")
    has_stat = mode != "ret"

    def body(*refs):
        refs = list(refs)
        q_ref, k_ref, v_ref, o_ref, do_ref = refs[:5]
        refs = refs[5:]
        if has_stat:
            st_ref = refs[0]
            refs = refs[1:]
        if mode == "fox":
            cc_ref, cr_ref = refs[:2]
            refs = refs[2:]
        dq_ref, dk_ref, dv_ref = refs[:3]
        dck_ref, drs_ref = refs[3:5] if mode == "fox" else (None, None)
        p = pl.program_id(0)
        i = pl.program_id(1)

        @pl.when(i == 0)
        def _():
            dk_ref[...] = jnp.zeros_like(dk_ref)
            dv_ref[...] = jnp.zeros_like(dv_ref)
            if mode == "fox":
                dck_ref[...] = jnp.zeros_like(dck_ref)

        nfull, dist, visible = _mixer_geometry(mode, i, nsub)
        lane = _lane((1, LANES))
        heads = [nhb * p + hh for hh in range(nhb)]
        dov = do_ref[...]
        wide = mode == "mla"
        q_scale = _QK_SCALE if mode in ("fox", "sb") else 1.0
        cols = [slice(hh * LANES, (hh + 1) * LANES) if wide else slice(None) for hh in range(nhb)]
        if wide:
            prod = dov * o_ref[...]
            qs = [q_ref[:, cols[hh]] for hh in range(nhb)]
            dos = [dov[:, cols[hh]].astype(BF16) for hh in range(nhb)]
            deltas = [jnp.sum(prod[:, cols[hh]], axis=-1, keepdims=True) for hh in range(nhb)]
        else:
            qf = q_ref[...].astype(F32) * q_scale
            prod = dov * o_ref[...]
            hms = [(lane // HEAD) == hh for hh in range(nhb)]
            qs = [jnp.where(hm, qf, 0.0).astype(BF16) for hm in hms]
            dos = [jnp.where(hm, dov, 0.0).astype(BF16) for hm in hms]
            deltas = [jnp.sum(jnp.where(hm, prod, 0.0), axis=-1, keepdims=True) for hm in hms]
        if has_stat:
            st = st_ref[0]
            stats = [st[:, hh * HEAD:hh * HEAD + 1] for hh in range(nhb)]
        if mode == "fox":
            cqs = [_head_pick(cc_ref[...], h, 1) for h in heads]
        if mode == "sb":
            r1 = lax.broadcasted_iota(jnp.int32, (TQ, TQ), 0)
            c1 = lax.broadcasted_iota(jnp.int32, (TQ, TQ), 1)
            u_upto = (r1 <= c1).astype(BF16)
            u_before = (r1 < c1).astype(BF16)

        def chunk(c):
            return pl.ds(pl.multiple_of(c * w, w), w)

        def scores(c):
            js = chunk(c)
            if mode == "sb":
                return tuple((_dot_nt(qs[hh], k_ref[js, cols[hh]]), None) for hh in range(nhb))
            return tuple((_dot_nt(qs[hh], k_ref[js, cols[hh]]), _dot_nt(dos[hh], v_ref[js, cols[hh]])) for hh in range(nhb))

        def emit(hh, js, ds_b, pr_b, dq):
            dk_ref[js, cols[hh]] += _dot_tn(ds_b, qs[hh])
            dv_ref[js, cols[hh]] += _dot_tn(pr_b, dos[hh])
            return dq + _dot(ds_b, k_ref[js, cols[hh]])

        def head_step(hh, c, js, sc_dp, carry, last):
            sc, dp = sc_dp
            if dp is None:
                dp = _dot_nt(dos[hh], v_ref[js, cols[hh]])
            if softmax:
                dq, rsum = carry
                if mode == "fox":
                    ck = _head_pick(cr_ref[:, js], heads[hh], 0)
                    sc = sc + (cqs[hh] - ck)
                else:
                    sc = sc * _MLA_SCALE
                if last:
                    sc = jnp.where(visible(c), sc, NEG)
                pr = jnp.exp(sc - stats[hh])
                ds = pr * (dp - deltas[hh])
                if mode == "fox":
                    dck_ref[0, hh:hh + 1, js] += jnp.sum(ds, axis=0, keepdims=True)
                    rsum = rsum + jnp.sum(ds, axis=-1, keepdims=True)
                if mode == "mla":
                    ds = ds * _MLA_SCALE
                return emit(hh, js, ds.astype(BF16), pr.astype(BF16), dq), rsum
            seen, gsum, dq = carry
            z = sc
            log_beta = jnp.minimum(z, 0.0) - jnp.log(1.0 + jnp.exp(-jnp.abs(z)))
            log_stay = log_beta - z
            if last:
                vis = visible(c)
                log_stay = jnp.where(vis, log_stay, 0.0)
            parts = []
            for b in range(nsub):
                ls_b = log_stay[:, b * TQ:(b + 1) * TQ]
                parts.append((stats[hh] - seen) - _dot2(ls_b, u_upto))
                seen = seen + jnp.sum(ls_b, axis=-1, keepdims=True)
            later = parts[0] if nsub == 1 else jnp.concatenate(parts, axis=1)
            wgt = jnp.exp(log_beta + later)
            if last:
                wgt = jnp.where(vis, wgt, 0.0)
            g = dp * wgt
            parts = []
            for b in range(nsub):
                g_b = g[:, b * TQ:(b + 1) * TQ]
                parts.append(gsum + _dot2(g_b, u_before))
                gsum = gsum + jnp.sum(g_b, axis=-1, keepdims=True)
            before = parts[0] if nsub == 1 else jnp.concatenate(parts, axis=1)
            beta = jnp.exp(log_beta)
            dz = g * (1.0 - beta) - beta * before
            if last:
                dz = jnp.where(vis, dz, 0.0)
            return seen, gsum, emit(hh, js, dz.astype(BF16), wgt.astype(BF16), dq)

        def step(c, c_next, state, last):
            scs, carries = state
            nxt = scores(c_next) if c_next is not None else None
            js = chunk(c)
            return nxt, tuple(head_step(hh, c, js, scs[hh], carries[hh], last) for hh in range(nhb))

        zero_acc = jnp.zeros((QB, LANES), F32)
        zero1 = jnp.zeros((QB, 1), F32)
        if softmax:
            init = tuple((zero_acc, zero1) for _ in range(nhb))
        else:
            init = tuple((zero1, zero1, zero_acc) for _ in range(nhb))
        state = lax.fori_loop(0, nfull, lambda c, st: step(c, c + 1, st, False), (scores(0), init))
        _, carries = step(nfull, None, state, True)
        if softmax:
            dqs = [dq for (dq, rsum) in carries]
        else:
            dqs = [dq for (seen, gsum, dq) in carries]
        hm0 = (lane // HEAD) == 0
        if wide:
            for hh in range(nhb):
                dq_ref[:, cols[hh]] = dqs[hh]
        else:
            dq_ref[...] = jnp.where(hm0, dqs[0], dqs[1]) * q_scale
        if mode == "fox":
            drs_ref[0] = jnp.where(hm0, carries[0][1], carries[1][1])

    bw = 2 * LANES if mode == "mla" else LANES
    pair_blk = pl.BlockSpec((QB, bw), lambda p, i: (i, p))
    full_blk = pl.BlockSpec((s, bw), lambda p, i: (0, p))
    stat_blk = pl.BlockSpec((1, QB, LANES), lambda p, i: (p, i, 0))
    in_specs = [q_spec, k_spec, v_spec, pair_blk, pair_blk]
    args = [qa, ka, va, o, do]
    if has_stat:
        in_specs.append(stat_blk)
        args.append(stat)
    if mode == "fox":
        in_specs += [pl.BlockSpec((QB, LANES), lambda p, i: (i, 0)), pl.BlockSpec((8, s), lambda p, i: (0, 0))]
        args += [cum_col, cum_row]
    out_specs = [pair_blk, full_blk, full_blk]
    out_shape = [jax.ShapeDtypeStruct((s, nblk * bw), F32)] * 3
    if mode == "fox":
        out_specs += [pl.BlockSpec((1, 8, s), lambda p, i: (p, 0, 0)), stat_blk]
        out_shape += [jax.ShapeDtypeStruct((2, 8, s), F32), jax.ShapeDtypeStruct((2, s, LANES), F32)]
    side_in, side_out, side_scratch = _side_specs(side)
    res = pl.pallas_call(
        _carry_side_job(body, len(args), len(out_shape), side, (nblk, nq)), name=mode + "_bwd", grid=(nblk, nq),
        in_specs=in_specs + side_in, out_specs=out_specs + side_out,
        out_shape=out_shape + ([] if side is None else side.out_shape), scratch_shapes=side_scratch,
        compiler_params=_cparams(("parallel", "arbitrary") if side is None else ("arbitrary", "arbitrary")),
    )(*args, *([] if side is None else side.inputs))
    return res if side is None else (*res[:len(out_shape)], res[len(out_shape):])


def _ret_geometry(p):
    lane = _lane((1, LANES))
    lg_lane = jnp.where(lane < HEAD, _log_gamma_of(2 * p), _log_gamma_of(2 * p + 1))
    a = lax.broadcasted_iota(jnp.int32, (TQ, 1), 0).astype(F32)
    row = lax.broadcasted_iota(jnp.int32, (TQ, TQ), 0)
    col = lax.broadcasted_iota(jnp.int32, (TQ, TQ), 1)
    same_chunk_or_earlier = (col // CHUNK) <= (row // CHUNK)
    gap = jnp.abs(row - col).astype(F32)
    decays = [jnp.where(same_chunk_or_earlier, jnp.exp(_log_gamma_of(2 * p + hh) * gap), 0.0) for hh in range(2)]
    r = lax.broadcasted_iota(jnp.int32, (LANES, LANES), 0)
    c = lax.broadcasted_iota(jnp.int32, (LANES, LANES), 1)
    own_head = (r // HEAD) == (c // HEAD)
    return lane, lg_lane, a, decays, own_head


def _ret_fwd(qa, ka, va, v_off):
    s = qa.shape[0]
    nq = s // TQ

    def body(q_ref, k_ref, v_ref, o_ref, st_ref, state):
        p = pl.program_id(0)

        @pl.when(pl.program_id(1) == 0)
        def _():
            state[...] = jnp.zeros_like(state)

        lane, lg_lane, a, decays, own_head = _ret_geometry(p)
        q = q_ref[...].astype(F32)
        k = k_ref[...]
        v = v_ref[...]
        s_in = state[...]
        st_ref[0, 0] = s_in
        out = _dot((q * jnp.exp(lg_lane * (a + 1.0))).astype(BF16), s_in.astype(BF16))
        for hh in range(2):
            hm = (lane // HEAD) == hh
            qh = jnp.where(hm, q, 0.0).astype(BF16)
            inner = _dot((_dot_nt(qh, k) * decays[hh]).astype(BF16), v)
            out = out + jnp.where(hm, inner, 0.0)
        o_ref[...] = out
        k_tail = (k.astype(F32) * jnp.exp(lg_lane * (TQ - 1.0 - a))).astype(BF16)
        state[...] = jnp.exp(lg_lane * float(TQ)) * s_in + jnp.where(own_head, _dot_tn(k_tail, v), 0.0)

    blk = lambda off: pl.BlockSpec((TQ, LANES), lambda p, i: (i, off + p))
    return pl.pallas_call(
        body, name="ret_fwd", grid=(2, nq), in_specs=[blk(0), blk(0), blk(v_off)],
        out_specs=[blk(0), pl.BlockSpec((1, 1, LANES, LANES), lambda p, i: (p, i, 0, 0))],
        out_shape=[jax.ShapeDtypeStruct((s, 2 * LANES), F32), jax.ShapeDtypeStruct((2, nq, LANES, LANES), F32)],
        scratch_shapes=[pltpu.VMEM((LANES, LANES), F32)],
        compiler_params=_cparams(("parallel", "arbitrary")),
    )(qa, ka, va)


def _ret_bwd(qa, ka, va, v_off, states, do):
    s = qa.shape[0]
    nq = s // TQ

    def body(q_ref, k_ref, v_ref, st_ref, do_ref, dq_ref, dk_ref, dv_ref, dstate):
        p = pl.program_id(0)

        @pl.when(pl.program_id(1) == 0)
        def _():
            dstate[...] = jnp.zeros_like(dstate)

        lane, lg_lane, a, decays, own_head = _ret_geometry(p)
        q = q_ref[...].astype(F32)
        k = k_ref[...]
        kf = k.astype(F32)
        v = v_ref[...]
        dov = do_ref[...]
        s_in = st_ref[0, 0].astype(BF16)
        ds_next = dstate[...]
        ds_b = ds_next.astype(BF16)
        head_decay = jnp.exp(lg_lane * (a + 1.0))
        tail_decay = jnp.exp(lg_lane * (TQ - 1.0 - a))
        k_tail = (kf * tail_decay).astype(BF16)
        dq = _dot_nt(dov.astype(BF16), s_in) * head_decay
        dk = _dot_nt(v, ds_b) * tail_decay
        dv = _dot(k_tail, ds_b)
        for hh in range(2):
            hm = (lane // HEAD) == hh
            qh = jnp.where(hm, q, 0.0).astype(BF16)
            doh = jnp.where(hm, dov, 0.0).astype(BF16)
            att = (_dot_nt(qh, k) * decays[hh]).astype(BF16)
            datt = (_dot_nt(doh, v) * decays[hh]).astype(BF16)
            dv = dv + _dot_tn(att, doh)
            dk = dk + _dot_tn(datt, qh)
            dq = dq + jnp.where(hm, _dot(datt, k), 0.0)
        dq_ref[...] = dq
        dk_ref[...] = dk
        dv_ref[...] = dv
        q_head = (q * head_decay).astype(BF16)
        dstate[...] = jnp.exp(lg_lane * float(TQ)) * ds_next + jnp.where(own_head, _dot_tn(q_head, dov.astype(BF16)), 0.0)

    blk = lambda off: pl.BlockSpec((TQ, LANES), lambda p, i: (nq - 1 - i, off + p))
    return pl.pallas_call(
        body, name="ret_bwd", grid=(2, nq),
        in_specs=[blk(0), blk(0), blk(v_off), pl.BlockSpec((1, 1, LANES, LANES), lambda p, i: (p, nq - 1 - i, 0, 0)), blk(0)],
        out_specs=[blk(0)] * 3, out_shape=[jax.ShapeDtypeStruct((s, 2 * LANES), F32)] * 3,
        scratch_shapes=[pltpu.VMEM((LANES, LANES), F32)],
        compiler_params=_cparams(("parallel", "arbitrary")),
    )(qa, ka, va, states, do)


def _seg_mean_matrix():
    r = lax.broadcasted_iota(jnp.int32, (GROUP, GROUP), 0)
    c = lax.broadcasted_iota(jnp.int32, (GROUP, GROUP), 1)
    return jnp.where((r // HEAD) == (c // HEAD), 1.0 / HEAD, 0.0).astype(F32)


def _sigmoid(x):
    return 1.0 / (1.0 + jnp.exp(-x))


def _mix_post(oa, ob, oc, od, proj, g):
    s = oa.shape[0]
    tr = _tile(s, 256)

    def body(a_ref, b_ref, c_ref, d_ref, rg_ref, g_ref, o_ref):
        gv = g_ref[...]
        o_ref[:, 0:GROUP] = _rms(a_ref[...], gv[:, 0:GROUP]).astype(BF16)
        o_ref[:, GROUP:2 * GROUP] = _rms(b_ref[...], gv[:, GROUP:2 * GROUP]).astype(BF16)
        seg = _seg_mean_matrix()
        c = c_ref[...]
        cen = c - _dot_exact(c, seg)
        n = cen * lax.rsqrt(_dot_exact(cen * cen, seg) + EPS)
        rg = rg_ref[...]
        o_ref[:, 2 * GROUP:3 * GROUP] = (n * gv[:, 2 * GROUP:3 * GROUP] * (rg * _sigmoid(rg))).astype(BF16)
        o_ref[:, 3 * GROUP:] = _rms(d_ref[...], gv[:, 3 * GROUP:]).astype(BF16)

    blk = pl.BlockSpec((tr, GROUP), lambda i: (i, 0))
    return pl.pallas_call(
        body, name="mix_post", grid=(s // tr,),
        in_specs=[blk] * 4 + [pl.BlockSpec((tr, GROUP), lambda i: (i, OFF_RG // 2)), pl.BlockSpec((1, D_MODEL), lambda i: (0, 0))],
        out_specs=pl.BlockSpec((tr, D_MODEL), lambda i: (i, 0)), out_shape=jax.ShapeDtypeStruct((s, D_MODEL), BF16),
        compiler_params=_cparams(("parallel",)),
    )(oa, ob, oc, od, proj, g.reshape(1, D_MODEL))


def _mix_post_bwd(dmixed, oa, ob, oc, od, proj, g):
    s = oa.shape[0]
    tr = _tile(s, 256)

    def body(dm_ref, a_ref, b_ref, c_ref, d_ref, rg_ref, g_ref, da_ref, db_ref, dc_ref, dd_ref, drg_ref, dg_ref):
        @pl.when(pl.program_id(0) == 0)
        def _():
            dg_ref[...] = jnp.zeros_like(dg_ref)

        gv = g_ref[...]
        dm = dm_ref[...]
        for k, (x_ref, dx_ref) in enumerate(((a_ref, da_ref), (b_ref, db_ref), (None, None), (d_ref, dd_ref))):
            if x_ref is None:
                continue
            cols = slice(k * GROUP, (k + 1) * GROUP)
            dx, gterm = _rms_bwd(x_ref[...], gv[:, cols], dm[:, cols])
            dx_ref[...] = dx
            dg_ref[:, cols] += jnp.sum(gterm, axis=0, keepdims=True)
        cols = slice(2 * GROUP, 3 * GROUP)
        seg = _seg_mean_matrix()
        c = c_ref[...]
        cen = c - _dot_exact(c, seg)
        rstd = lax.rsqrt(_dot_exact(cen * cen, seg) + EPS)
        n = cen * rstd
        rg = rg_ref[...]
        sg = _sigmoid(rg)
        gate = rg * sg
        dy = dm[:, cols]
        gc = gv[:, cols]
        dn = dy * gc * gate
        dg_ref[:, cols] += jnp.sum(dy * n * gate, axis=0, keepdims=True)
        drg_ref[...] = (dy * n * gc * (sg * (1.0 + rg * (1.0 - sg)))).astype(BF16)
        dc_ref[...] = rstd * (dn - _dot_exact(dn, seg) - n * _dot_exact(dn * n, seg))

    blk = pl.BlockSpec((tr, GROUP), lambda i: (i, 0))
    gsp = pl.BlockSpec((1, D_MODEL), lambda i: (0, 0))
    return pl.pallas_call(
        body, name="mix_post_bwd", grid=(s // tr,),
        in_specs=[pl.BlockSpec((tr, D_MODEL), lambda i: (i, 0))] + [blk] * 4 + [pl.BlockSpec((tr, GROUP), lambda i: (i, OFF_RG // 2)), gsp],
        out_specs=[blk] * 5 + [gsp],
        out_shape=[jax.ShapeDtypeStruct((s, GROUP), F32)] * 4 + [jax.ShapeDtypeStruct((s, GROUP), BF16), jax.ShapeDtypeStruct((1, D_MODEL), F32)],
        compiler_params=_cparams(("arbitrary",)),
    )(dmixed, oa, ob, oc, od, proj, g.reshape(1, D_MODEL))


def _pack_w_in(w):
    z = lambda n: jnp.zeros((w.shape[0], n), w.dtype)
    misc = jnp.concatenate([w[:, 768:772], z(KR_LANE - N_HEADS), w[:, 1156:1188], z(LANES - KR_LANE - ROPE_DIM)], axis=1)
    return jnp.concatenate([w[:, 0:768], w[:, 772:1028], w[:, 1188:2980], w[:, 1028:1156], misc], axis=1)


def _unpack_dw_in(d):
    m = OFF_MISC * LANES
    return jnp.concatenate([d[:, 0:768], d[:, m:m + N_HEADS], d[:, 768:1024], d[:, OFF_CKV * LANES:m],
                            d[:, m + KR_LANE:m + KR_LANE + ROPE_DIM], d[:, 1024:OFF_CKV * LANES]], axis=1)


def _pack_w_q(w):
    return jnp.pad(w.reshape(Q_RANK, N_HEADS, HEAD + ROPE_DIM), ((0, 0), (0, 0), (0, LANES - HEAD - ROPE_DIM))).reshape(Q_RANK, 4 * LANES)


def _unpack_dw_q(d):
    return d.reshape(Q_RANK, N_HEADS, LANES)[:, :, :HEAD + ROPE_DIM].reshape(Q_RANK, N_HEADS * (HEAD + ROPE_DIM))


def _pack_w_kv(w):
    w4 = w.reshape(KV_RANK, N_HEADS, 2 * HEAD)
    widen = lambda a: jnp.pad(a, ((0, 0), (0, 0), (0, LANES - HEAD))).reshape(KV_RANK, N_HEADS * LANES)
    return widen(w4[:, :, :HEAD]), widen(w4[:, :, HEAD:])


def _unpack_dw_kv(dk, dv):
    narrow = lambda a: a.reshape(KV_RANK, N_HEADS, LANES)[:, :, :HEAD]
    return jnp.concatenate([narrow(dk), narrow(dv)], axis=2).reshape(KV_RANK, 2 * N_HEADS * HEAD)


def _narrow_heads(a):
    return a.reshape(a.shape[0], N_HEADS, LANES)[:, :, :HEAD].reshape(a.shape[0], N_HEADS * HEAD)


def _widen_heads(a):
    return jnp.pad(a.reshape(a.shape[0], N_HEADS, HEAD), ((0, 0), (0, 0), (0, LANES - HEAD))).reshape(a.shape[0], N_HEADS * LANES)


def _layer_fwd(x, lw, tabs, tag, side=None, fox_side=None, late_weights=None):
    cos_m, sin_m, cos_r, sin_r = tabs
    h1 = _norm_fwd(x, lw["g_mix_pre"], name=tag + "pre_norm")
    proj, projb = _matmul(h1, lw["w_in"], name=tag + "in_proj", also_bf16=True)
    bias_row = jnp.pad(lw["b_forget"], (FF_LANE, LANES - N_HEADS - FF_LANE)).reshape(1, LANES)
    cum_col, cum_row = _fox_cum(proj, bias_row)
    oa, lse_a, *fox_carried = _mixer_fwd("fox", projb, OFF_FQ, projb, OFF_FK, projb, OFF_FV, cum_col=cum_col, cum_row=cum_row,
                                         side=fox_side)
    if late_weights is not None:
        lw = {**lw, **late_weights(fox_carried[0])}
    qm, km, vm, cqn, ckvn = _mla_prep(proj, cos_m, sin_m, lw["g_q_lora"], lw["g_kv_lora"], lw["wq"], lw["wk"], lw["wv"])
    ob_wide, lse_b = _mixer_fwd("mla", qm, 0, km, 0, vm, 0)
    ob = _narrow_heads(ob_wide)
    qr, kr = _ret_prep(proj, cos_r, sin_r)
    oc, ret_states = _ret_fwd(qr, kr, projb, OFF_RV)
    od, tot_d, *carried = _mixer_fwd("sb", projb, OFF_SQ, projb, OFF_SK, projb, OFF_SV, side=side)
    mixed = _mix_post(oa, ob, oc, od, proj, lw["g_mix_out"])
    mix = _matmul(mixed, lw["w_out"], name=tag + "out_proj")
    x1 = _norm_fwd(mix, lw["g_mix_post"], name=tag + "mix_post_norm", resid=x, out_dtype=F32)
    h2 = _norm_fwd(x1, lw["g_ffn_pre"], name=tag + "ffn_pre_norm")
    u = _matmul(h2, lw["w_ffn_up"], name=tag + "ffn_up", relu2=True, out_dtype=BF16)
    f = _matmul(u, lw["w_ffn_down"], name=tag + "ffn_down")
    x2 = _norm_fwd(f, lw["g_ffn_post"], name=tag + "ffn_post_norm", resid=x1, out_dtype=F32)
    saved = dict(x=x, h1=h1, proj=proj, projb=projb, bias_row=bias_row, cum_col=cum_col, cum_row=cum_row, oa=oa, lse_a=lse_a,
                 qm=qm, km=km, vm=vm, cqn=cqn, ckvn=ckvn, ob=ob, ob_wide=ob_wide, lse_b=lse_b, qr=qr, kr=kr, ret_states=ret_states, oc=oc, od=od, tot_d=tot_d, mixed=mixed,
                 mix=mix, x1=x1, h2=h2, u=u, f=f)
    return x2, saved, lw, (carried[0] if carried else None)


def _layer_bwd(dx2, lw, sv, tabs, tag, side=None, ffn_side=None, fox_side=None):
    cos_m, sin_m, cos_r, sin_r = tabs
    g = {}
    df, g["g_ffn_post"] = _norm_bwd(sv["f"], lw["g_ffn_post"], dx2, name=tag + "ffn_post_norm_bwd", out_dtype=BF16)
    du_pre = _matmul(df, lw["w_ffn_down"], name=tag + "ffn_down_dx", tb=True, out_dtype=BF16, relu2_of=sv["u"], side=ffn_side)
    ffn_carried = None
    if ffn_side is not None:
        du_pre, ffn_carried = du_pre
    g["w_ffn_down"] = _matmul(sv["u"], df, name=tag + "ffn_down_dw", ta=True)
    dh2 = _matmul(du_pre, lw["w_ffn_up"], name=tag + "ffn_up_dx", tb=True)
    g["w_ffn_up"] = _matmul(sv["h2"], du_pre, name=tag + "ffn_up_dw", ta=True)
    dx1, g["g_ffn_pre"] = _norm_bwd(sv["x1"], lw["g_ffn_pre"], dh2, name=tag + "ffn_pre_norm_bwd", add=dx2)
    dmix, g["g_mix_post"] = _norm_bwd(sv["mix"], lw["g_mix_post"], dx1, name=tag + "mix_post_norm_bwd", out_dtype=BF16)
    dmixed = _matmul(dmix, lw["w_out"], name=tag + "out_proj_dx", tb=True)
    g["w_out"] = _matmul(sv["mixed"], dmix, name=tag + "out_proj_dw", ta=True)
    proj, projb = sv["proj"], sv["projb"]
    doa, dob, doc, dod, drg, g["g_mix_out"] = _mix_post_bwd(dmixed, sv["oa"], sv["ob"], sv["oc"], sv["od"], proj, lw["g_mix_out"])
    dfq, dfk, dfv, dck, drs, *fox_carried = _mixer_bwd(
        "fox", projb, OFF_FQ, projb, OFF_FK, projb, OFF_FV, sv["oa"], doa, stat=sv["lse_a"], cum_col=sv["cum_col"],
        cum_row=sv["cum_row"], side=None if fox_side is None else fox_side(g))
    dqm, dkm, dvm = _mixer_bwd("mla", sv["qm"], 0, sv["km"], 0, sv["vm"], 0, sv["ob_wide"], _widen_heads(dob), stat=sv["lse_b"])
    dcq, dckv, dkr, dwq, dwk, dwv, g["g_q_lora"], g["g_kv_lora"] = _mla_prep_bwd(
        dqm, dkm, dvm, proj, sv["cqn"], sv["ckvn"], cos_m, sin_m, lw["g_q_lora"], lw["g_kv_lora"], lw["wq"], lw["wk"], lw["wv"])
    dqr, dkr_ret, drv = _ret_bwd(sv["qr"], sv["kr"], projb, OFF_RV, sv["ret_states"], doc)
    drq, drk = _ret_prep_bwd(dqr, dkr_ret, cos_r, sin_r)
    if callable(side):
        side = side(g, ffn_carried, fox_carried[0] if fox_carried else None)
    dsq, dsk, dsv, *carried = _mixer_bwd("sb", projb, OFF_SQ, projb, OFF_SK, projb, OFF_SV, sv["od"], dod, stat=sv["tot_d"], side=side)
    dmisc, db_row = _fox_gate_bwd(dck, drs, proj, sv["bias_row"], dkr)
    b = lambda a: a.astype(BF16)
    dproj = jnp.concatenate([b(dfq), b(dfk), b(dfv), dcq, drq, drk, b(drv), drg, b(dsq), b(dsk), b(dsv), dckv, dmisc], axis=1)
    dh1 = _matmul(dproj, lw["w_in"], name=tag + "in_proj_dx", tb=True)
    g["w_in"] = _matmul(sv["h1"], dproj, name=tag + "in_proj_dw", ta=True)
    dx, g["g_mix_pre"] = _norm_bwd(sv["x"], lw["g_mix_pre"], dh1, name=tag + "pre_norm_bwd", add=dx1)
    g["b_forget"] = db_row[0, FF_LANE:FF_LANE + N_HEADS]
    g["wq"], g["wk"], g["wv"] = dwq, dwk, dwv
    return dx, g, (carried[0] if carried else None)


def _local_step(x, positions, layers, target):
    s = x.shape[0]
    tabs = _rope_tables(positions.reshape(s, 1))
    saved = []
    for li, lw in enumerate(layers):
        x, sv, _, _ = _layer_fwd(x, lw, tabs, "l%d_" % li)
        saved.append(sv)
    loss_row, dx = _loss_head(x, target)
    grads = [None] * len(layers)
    for li in reversed(range(len(layers))):
        dx, grads[li], _ = _layer_bwd(dx, layers[li], saved[li], tabs, "l%d_" % li)
    return loss_row[0, 0], dx, grads


def _adamw(w, g, m, v, *, name):
    r, c = w.shape
    tr = 256 if r % 256 == 0 else r
    blk = pl.BlockSpec((tr, c), lambda i: (i, 0))
    c1 = 1.0 - ADAM_B1 ** ADAM_STEP
    c2 = 1.0 - ADAM_B2 ** ADAM_STEP

    def body(w_ref, g_ref, m_ref, v_ref, d_ref, mo_ref, vo_ref):
        gv = g_ref[...]
        mn = ADAM_B1 * m_ref[...] + (1.0 - ADAM_B1) * gv
        vn = ADAM_B2 * v_ref[...] + (1.0 - ADAM_B2) * jnp.square(gv)
        mo_ref[...] = mn
        vo_ref[...] = vn
        d_ref[...] = -ADAM_LR * ((mn / c1) / (jnp.sqrt(vn / c2) + ADAM_EPS) + ADAM_WD * w_ref[...])

    return pl.pallas_call(
        body, name=name, grid=(r // tr,), in_specs=[blk] * 4, out_specs=[blk] * 3,
        out_shape=[jax.ShapeDtypeStruct((r, c), F32)] * 3, compiler_params=_cparams(("parallel",)),
    )(w, g, m, v)


BIG = ("w_in", "w_q_up", "w_kv_up", "w_out", "w_ffn_up", "w_ffn_down")
SMALL = ("g_mix_pre", "b_forget", "g_q_lora", "g_kv_lora", "g_mix_out", "g_mix_post", "g_ffn_pre", "g_ffn_post")
N_CHIPS = 4
ANY = pl.BlockSpec(memory_space=pl.ANY)


def _mesh_pos():
    return lax.axis_index("x"), lax.axis_index("y"), lax.axis_index("c")


def _other_chips(x, y):
    return [(1 - x, y), (x, 1 - y), (1 - x, 1 - y)]


def _rows_half(ref, half):
    h = ref.shape[-2] // 2
    return ref.at[(slice(None),) * (len(ref.shape) - 2) + (pl.ds(half * h, h), slice(None))]


def _remote(src, dst, send_sem, recv_sem, device):
    return pltpu.make_async_remote_copy(src_ref=src, dst_ref=dst, send_sem=send_sem, recv_sem=recv_sem, device_id=device,
                                        device_id_type=MESH)


def _comm_call(body, name, args, out_shape, n_sems):
    return pl.pallas_call(
        body, name=name, in_specs=[ANY] * len(args), out_specs=[ANY] * len(out_shape), out_shape=out_shape,
        scratch_shapes=[pltpu.SemaphoreType.DMA((n_sems,)), pltpu.SemaphoreType.DMA((n_sems,))],
        compiler_params=pltpu.CompilerParams(has_side_effects=True),
    )(*args)


def _run_side_job(side, name):
    si = len(side.inputs)

    def body(*refs):
        args = (refs[:si], refs[si:-2], refs[-2], refs[-1])
        sends = side.sends(*args)
        for cp in sends:
            cp.start()
        for cp in side.recvs(*args):
            cp.wait_recv()
        for cp in sends:
            cp.wait_send()

    return _comm_call(body, name, side.inputs, side.out_shape, side.n_sems)


def _gather_job(shards):
    n = len(shards)

    def copies(own_block, ins, outs, send_sems, recv_sems):
        x, y, c = _mesh_pos()
        return [_remote(_rows_half(ins[t], c), _rows_half(outs[t].at[2 * x + y if own_block else 2 * px + py], c),
                        send_sems.at[3 * t + j], recv_sems.at[3 * t + j], (px, py, c))
                for t in range(n) for j, (px, py) in enumerate(_other_chips(x, y))]

    return _SideJob(shards, [jax.ShapeDtypeStruct((N_CHIPS,) + a.shape, a.dtype) for a in shards], 3 * n,
                    functools.partial(copies, True), functools.partial(copies, False))


def _forward_halves(gathered):
    n = len(gathered)

    def body(*refs):
        bufs, send_sems, recv_sems = refs[n:2 * n], refs[-2], refs[-1]
        x, y, c = _mesh_pos()

        def d2d(t, j, block, half):
            region = _rows_half(bufs[t].at[block], half)
            return _remote(region, region, send_sems.at[3 * t + j], recv_sems.at[3 * t + j], (x, y, 1 - c))

        peers = list(enumerate(_other_chips(x, y)))
        sends = [d2d(t, j, 2 * px + py, c) for t in range(n) for j, (px, py) in peers]
        for cp in sends:
            cp.start()
        for t in range(n):
            for j, (px, py) in peers:
                d2d(t, j, 2 * px + py, 1 - c).wait_recv()
        for cp in sends:
            cp.wait_send()

    return pl.pallas_call(
        body, name="gather_forward", in_specs=[ANY] * n, out_specs=[ANY] * n,
        out_shape=[jax.ShapeDtypeStruct(g.shape, g.dtype) for g in gathered], input_output_aliases={t: t for t in range(n)},
        scratch_shapes=[pltpu.SemaphoreType.DMA((3 * n,)), pltpu.SemaphoreType.DMA((3 * n,))],
        compiler_params=pltpu.CompilerParams(has_side_effects=True),
    )(*gathered)


def _exchange_halves_job(gs):
    n = len(gs)

    def copies(ins, outs, send_sems, recv_sems):
        x, y, c = _mesh_pos()
        return [_remote(_rows_half(ins[t], 1 - c), outs[t], send_sems.at[t], recv_sems.at[t], (x, y, 1 - c)) for t in range(n)]

    out_shape = [jax.ShapeDtypeStruct(g.shape[:2] + (g.shape[2] // 2, g.shape[3]), g.dtype) for g in gs]
    return _SideJob(gs, out_shape, n, copies, copies)


def _pair_add(g, r, c_idx, *, name):
    nb, d, rows, cols = g.shape
    h = rows // 2
    tr = min(h, 512)
    nt = h // tr

    def body(c_ref, g_ref, r_ref, p_ref, pb_ref):
        s = g_ref[...] + r_ref[...]
        p_ref[...] = s
        pb_ref[...] = s.astype(BF16)

    blk = pl.BlockSpec((1, 1, tr, cols), lambda k, l, i, c_ref: (k, l, i, 0))
    return pl.pallas_call(
        body, name=name,
        grid_spec=pltpu.PrefetchScalarGridSpec(
            num_scalar_prefetch=1, grid=(nb, d, nt),
            in_specs=[pl.BlockSpec((1, 1, tr, cols), lambda k, l, i, c_ref: (k, l, c_ref[0] * nt + i, 0)), blk],
            out_specs=[blk, blk]),
        out_shape=[jax.ShapeDtypeStruct((nb, d, h, cols), F32), jax.ShapeDtypeStruct((nb, d, h, cols), BF16)],
        compiler_params=_cparams(("parallel", "parallel", "parallel")),
    )(c_idx, g, r)


def _exchange_chips_job(pbs):
    n = len(pbs)

    def copies(ins, outs, send_sems, recv_sems):
        x, y, c = _mesh_pos()
        return [_remote(ins[t].at[2 * px + py], outs[t].at[j], send_sems.at[3 * t + j], recv_sems.at[3 * t + j], (px, py, c))
                for t in range(n) for j, (px, py) in enumerate(_other_chips(x, y))]

    return _SideJob(pbs, [jax.ShapeDtypeStruct((3,) + p.shape[1:], p.dtype) for p in pbs], 3 * n, copies, copies)


def _chip_add(p, r, k_idx, *, name):
    _, d, h, cols = p.shape
    tr = min(h, 512)
    nt = h // tr

    def body(k_ref, p_ref, r_ref, o_ref):
        o_ref[0] = ((p_ref[0, 0] + r_ref[0, 0].astype(F32)) + r_ref[1, 0].astype(F32)) + r_ref[2, 0].astype(F32)

    return pl.pallas_call(
        body, name=name,
        grid_spec=pltpu.PrefetchScalarGridSpec(
            num_scalar_prefetch=1, grid=(d, nt),
            in_specs=[pl.BlockSpec((1, 1, tr, cols), lambda l, i, k_ref: (k_ref[0], l, i, 0)),
                      pl.BlockSpec((3, 1, tr, cols), lambda l, i, k_ref: (0, l, i, 0))],
            out_specs=pl.BlockSpec((1, tr, cols), lambda l, i, k_ref: (l, i, 0))),
        out_shape=jax.ShapeDtypeStruct((d, h, cols), F32), compiler_params=_cparams(("parallel", "parallel")),
    )(k_idx, p, r)


def _share_halves(qs):
    n = len(qs)

    def body(*refs):
        ins, outs, send_sems, recv_sems = refs[:n], refs[n:2 * n], refs[2 * n], refs[2 * n + 1]
        x, y, c = _mesh_pos()
        cps = [_remote(ins[t], outs[t], send_sems.at[t], recv_sems.at[t], (x, y, 1 - c)) for t in range(n)]
        for cp in cps:
            cp.start()
        for cp in cps:
            cp.wait_recv()
        for cp in cps:
            cp.wait_send()

    return _comm_call(body, "grad_pair_share", qs, [jax.ShapeDtypeStruct(q.shape, q.dtype) for q in qs], n)


def _all_reduce_small(v):
    r, cols = v.shape
    n_dev = 8

    def body(v_ref, o_ref, buf, send_sems, recv_sems):
        x, y, c = _mesh_pos()
        me = 4 * x + 2 * y + c
        buf[me] = v_ref[...]

        def peer(j):
            return (1 - x if j & 4 else x, 1 - y if j & 2 else y, 1 - c if j & 1 else c)

        def copy(j, slot):
            return pltpu.make_async_remote_copy(src_ref=v_ref, dst_ref=buf.at[slot], send_sem=send_sems.at[j - 1],
                                                recv_sem=recv_sems.at[j - 1], device_id=peer(j), device_id_type=MESH)

        sends = [copy(j, me) for j in range(1, n_dev)]
        for cp in sends:
            cp.start()
        for j in range(1, n_dev):
            px, py, pc = peer(j)
            copy(j, 4 * px + 2 * py + pc).wait_recv()
        for cp in sends:
            cp.wait_send()
        acc = buf[0]
        for d in range(1, n_dev):
            acc = acc + buf[d]
        o_ref[...] = acc

    vm = pl.BlockSpec(memory_space=pltpu.VMEM)
    return pl.pallas_call(
        body, name="small_all_reduce", in_specs=[vm], out_specs=vm, out_shape=jax.ShapeDtypeStruct((r, cols), F32),
        scratch_shapes=[pltpu.VMEM((n_dev, r, cols), F32), pltpu.SemaphoreType.DMA((n_dev - 1,)), pltpu.SemaphoreType.DMA((n_dev - 1,))],
        compiler_params=pltpu.CompilerParams(has_side_effects=True),
    )(v)


_COL_SHARDED = ("w_in", "w_q_up", "w_kv_up", "w_ffn_up")


def _shard_cols(blocks, a, b):
    c = blocks[0].shape[-1]
    out = []
    while a < b:
        k = a // c
        hi = min(b, (k + 1) * c)
        out.append(blocks[k][:, a - k * c:hi - k * c])
        a = hi
    return out


def _pack_w_in_shards(blocks):
    z = lambda n: [jnp.zeros((blocks[0].shape[0], n), blocks[0].dtype)]
    cols = lambda a, b: _shard_cols(blocks, a, b)
    return jnp.concatenate(cols(0, 768) + cols(772, 1028) + cols(1188, 2980) + cols(1028, 1156) + cols(768, 772)
                           + z(KR_LANE - N_HEADS) + cols(1156, 1188) + z(LANES - KR_LANE - ROPE_DIM), axis=1)


def _whole_layer(name, blocks):
    return jnp.concatenate(list(blocks), axis=1 if name in _COL_SHARDED else 0)


def _split_layer(name, whole):
    if name in _COL_SHARDED:
        c = whole.shape[1] // N_CHIPS
        return jnp.stack([whole[:, k * c:(k + 1) * c] for k in range(N_CHIPS)])
    return whole.reshape(N_CHIPS, whole.shape[0] // N_CHIPS, whole.shape[1])


def _small_to_rows(d):
    v = jnp.concatenate([d[k].astype(F32).reshape(-1) for k in SMALL])
    rows = -(-v.shape[0] // (8 * LANES)) * 8
    return jnp.pad(v, (0, rows * LANES - v.shape[0])).reshape(rows, LANES)


def _small_from_rows(rows, shapes):
    v = rows.reshape(-1)
    out, o = {}, 0
    for k in SMALL:
        sz = int(np.prod(shapes[k]))
        out[k] = v[o:o + sz].reshape(shapes[k])
        o += sz
    return out


_ARG_NAMES = ("x", "positions", "g_mix_pre", "w_in", "b_forget", "g_q_lora", "w_q_up", "g_kv_lora", "w_kv_up", "g_mix_out", "w_out",
              "g_mix_post", "g_ffn_pre", "w_ffn_up", "w_ffn_down", "g_ffn_post")
_WEIGHTS = _ARG_NAMES[2:]


def kernel(x, positions, g_mix_pre, w_in, b_forget, g_q_lora, w_q_up, g_kv_lora, w_kv_up, g_mix_out, w_out, g_mix_post, g_ffn_pre, w_ffn_up, w_ffn_down, g_ffn_post, loss_target, m_g_mix_pre, m_w_in, m_b_forget, m_g_q_lora, m_w_q_up, m_g_kv_lora, m_w_kv_up, m_g_mix_out, m_w_out, m_g_mix_post, m_g_ffn_pre, m_w_ffn_up, m_w_ffn_down, m_g_ffn_post, v_g_mix_pre, v_w_in, v_b_forget, v_g_q_lora, v_w_q_up, v_g_kv_lora, v_w_kv_up, v_g_mix_out, v_w_out, v_g_mix_post, v_g_ffn_pre, v_w_ffn_up, v_w_ffn_down, v_g_ffn_post):
    w = dict(g_mix_pre=g_mix_pre, w_in=w_in, b_forget=b_forget, g_q_lora=g_q_lora, w_q_up=w_q_up, g_kv_lora=g_kv_lora, w_kv_up=w_kv_up,
             g_mix_out=g_mix_out, w_out=w_out, g_mix_post=g_mix_post, g_ffn_pre=g_ffn_pre, w_ffn_up=w_ffn_up, w_ffn_down=w_ffn_down,
             g_ffn_post=g_ffn_post)
    m = dict(g_mix_pre=m_g_mix_pre, w_in=m_w_in, b_forget=m_b_forget, g_q_lora=m_g_q_lora, w_q_up=m_w_q_up, g_kv_lora=m_g_kv_lora,
             w_kv_up=m_w_kv_up, g_mix_out=m_g_mix_out, w_out=m_w_out, g_mix_post=m_g_mix_post, g_ffn_pre=m_g_ffn_pre,
             w_ffn_up=m_w_ffn_up, w_ffn_down=m_w_ffn_down, g_ffn_post=m_g_ffn_post)
    v = dict(g_mix_pre=v_g_mix_pre, w_in=v_w_in, b_forget=v_b_forget, g_q_lora=v_g_q_lora, w_q_up=v_w_q_up, g_kv_lora=v_g_kv_lora,
             w_kv_up=v_w_kv_up, g_mix_out=v_g_mix_out, w_out=v_w_out, g_mix_post=v_g_mix_post, g_ffn_pre=v_g_ffn_pre,
             w_ffn_up=v_w_ffn_up, w_ffn_down=v_w_ffn_down, g_ffn_post=v_g_ffn_post)
    shard_shapes = {k: w[k].shape for k in BIG}
    small_shapes = {k: w[k].shape for k in SMALL}
    c_idx = lax.axis_index("c").astype(jnp.int32).reshape(1)
    k_idx = (2 * lax.axis_index("x") + lax.axis_index("y")).astype(jnp.int32).reshape(1)
    first_core = lax.axis_index("c") == 0

    mine = 2 * lax.axis_index("x") + lax.axis_index("y")
    shards_b = [{k: w[k][l:l + 1].astype(BF16) for k in BIG} for l in range(DEPTH)]
    gains = [dict(g_mix_pre=g_mix_pre[l], b_forget=b_forget[l], g_q_lora=g_q_lora[l], g_kv_lora=g_kv_lora[l], g_mix_out=g_mix_out[l],
                  g_mix_post=g_mix_post[l], g_ffn_pre=g_ffn_pre[l], g_ffn_post=g_ffn_post[l]) for l in range(DEPTH)]
    FIRST, LATER = ("w_in", "w_q_up", "w_kv_up"), ("w_out", "w_ffn_up", "w_ffn_down")
    EARLY_GRADS, LATE_GRADS = ("w_ffn_down", "w_ffn_up", "w_out"), ("w_in", "w_q_up", "w_kv_up")

    def gather_job(l, names):
        return _gather_job([shards_b[l][k] for k in names])

    def weights_of(l, names, gathered):
        four = {k: [jnp.where(mine == chip, shards_b[l][k][0], g[chip, 0]) for chip in range(N_CHIPS)]
                for k, g in zip(names, _forward_halves(gathered))}
        out = {}
        for k in names:
            if k == "w_in":
                out["w_in"] = _pack_w_in_shards(four[k])
            elif k == "w_q_up":
                out["wq"] = _pack_w_q(_whole_layer(k, four[k]))
            elif k == "w_kv_up":
                out["wk"], out["wv"] = _pack_w_kv(_whole_layer(k, four[k]))
            else:
                out[k] = _whole_layer(k, four[k])
        return out

    def grad_blocks(names, g):
        whole = dict(w_in=lambda: _unpack_dw_in(g["w_in"]), w_q_up=lambda: _unpack_dw_q(g["wq"]),
                     w_kv_up=lambda: _unpack_dw_kv(g["wk"], g["wv"]), w_out=lambda: g["w_out"], w_ffn_up=lambda: g["w_ffn_up"],
                     w_ffn_down=lambda: g["w_ffn_down"])
        return [_split_layer(k, whole[k]())[:, None] for k in names]

    def pair_sums(names, blocks, theirs):
        return [_pair_add(b, r, c_idx, name="grad_pair_add_" + k) for k, b, r in zip(names, blocks, theirs)]

    def exchange_job(*pairs):
        return _exchange_chips_job([pb for pair in pairs for (_, pb) in pair])

    def finish_grads(names, pair, partial):
        half = [_chip_add(p, r, k_idx, name="grad_chip_add_" + k) for k, (p, _), r in zip(names, pair, partial)]
        return {k: jnp.where(first_core, jnp.concatenate([q, s], axis=1), jnp.concatenate([s, q], axis=1))
                for k, q, s in zip(names, half, _share_halves(half))}

    seq = x.shape[1]
    tabs = _rope_tables(positions[0].reshape(seq, 1))
    first0 = weights_of(0, FIRST, _run_side_job(gather_job(0, FIRST), "gather_weights_l0"))
    x1, saved0, lw0, gathered1 = _layer_fwd(x[0], {**gains[0], **first0}, tabs, "l0_", fox_side=gather_job(0, LATER),
                                            late_weights=lambda got: weights_of(0, LATER, got), side=gather_job(1, BIG))
    lw1 = {**gains[1], **weights_of(1, BIG, gathered1)}
    x2, saved1, _, _ = _layer_fwd(x1, lw1, tabs, "l1_")
    loss_row, dx = _loss_head(x2, loss_target[0])
    loss = lax.psum(loss_row[0, 0], ("x", "y", "c"))
    dx, grads1, _ = _layer_bwd(dx, lw1, saved1, tabs, "l1_")
    blocks1 = grad_blocks(BIG, grads1)
    early_blocks0, pair1, early0 = [], [], []

    def beside_l0_fox_backward(g):
        early_blocks0.extend(grad_blocks(EARLY_GRADS, g))
        return _exchange_halves_job(early_blocks0)

    def beside_l0_sb_backward(g, theirs1, theirs_early0):
        pair1.extend(pair_sums(BIG, blocks1, theirs1))
        early0.extend(pair_sums(EARLY_GRADS, early_blocks0, theirs_early0))
        return exchange_job(pair1, early0)

    dx, grads0, partial = _layer_bwd(dx, lw0, saved0, tabs, "l0_", ffn_side=_exchange_halves_job(blocks1),
                                     fox_side=beside_l0_fox_backward, side=beside_l0_sb_backward)
    big1 = finish_grads(BIG, pair1, partial[:len(BIG)])
    big0 = finish_grads(EARLY_GRADS, early0, partial[len(BIG):])
    late_blocks0 = grad_blocks(LATE_GRADS, grads0)
    late0 = pair_sums(LATE_GRADS, late_blocks0, _run_side_job(_exchange_halves_job(late_blocks0), "grad_pair_exchange_l0"))
    big0.update(finish_grads(LATE_GRADS, late0, _run_side_job(exchange_job(late0), "grad_chip_exchange_l0")))
    g_big = {k: jnp.concatenate([big0[k], big1[k]], axis=0) for k in BIG}
    grads = [grads0, grads1]

    g_small_local = {k: jnp.stack([grads[l][k].reshape(small_shapes[k][1:]) for l in range(DEPTH)]) for k in SMALL}
    g_small = _small_from_rows(_all_reduce_small(_small_to_rows(g_small_local)), small_shapes)

    g_all = {**g_big, **g_small}
    delta, new_m, new_v = {}, {}, {}
    for k in BIG:
        d, r, c = shard_shapes[k]
        two_d = lambda a: a.reshape(d * r, c)
        dk, mk, vk = _adamw(two_d(w[k]), two_d(g_all[k]), two_d(m[k]), two_d(v[k]), name="adamw_" + k)
        delta[k], new_m[k], new_v[k] = dk.reshape(d, r, c), mk.reshape(d, r, c), vk.reshape(d, r, c)
    ds, ms, vs = _adamw(_small_to_rows(w), _small_to_rows(g_small), _small_to_rows(m), _small_to_rows(v), name="adamw_small")
    delta.update(_small_from_rows(ds, small_shapes))
    new_m.update(_small_from_rows(ms, small_shapes))
    new_v.update(_small_from_rows(vs, small_shapes))

    grad_x = dx.reshape(x.shape)
    return (loss, grad_x, *[g_all[k] for k in _WEIGHTS], *[delta[k] for k in _WEIGHTS], *[new_m[k] for k in _WEIGHTS],
            *[new_v[k] for k in _WEIGHTS])
```

```python
import functools
import math

import numpy as np
import jax
import jax.numpy as jnp
from jax import lax
from jax.experimental import pallas as pl
from jax.experimental.pallas import tpu as pltpu

F32 = jnp.float32
BF16 = jnp.bfloat16
MESH = pl.DeviceIdType.MESH

D_MODEL = 1024
DEPTH = 2
CHUNK = 64
GROUP = 256
HEAD = 64
N_HEADS = 4
Q_RANK = 256
KV_RANK = 128
ROPE_DIM = 32
D_FF = 4096
D_IN = 2980
D_INP = 3072
ROPE_BASE = 10000.0
EPS = 1e-6
LANES = 128
TQ = 128
NEG = -1e30

ADAM_LR, ADAM_B1, ADAM_B2, ADAM_EPS, ADAM_WD, ADAM_STEP = 0.001, 0.9, 0.999, 1e-08, 0.01, 10

OFF_FQ, OFF_FK, OFF_FV, OFF_CQ = 0, 2, 4, 6
OFF_RQ, OFF_RK, OFF_RV, OFF_RG = 8, 10, 12, 14
OFF_SQ, OFF_SK, OFF_SV = 16, 18, 20
OFF_CKV, OFF_MISC = 22, 23
FF_LANE, KR_LANE = 0, 64

VMEM_LIMIT = 56 * 1024 * 1024


def _tile(dim, pref):
    return pref if dim % pref == 0 else dim


def _cparams(sem, vmem=None):
    return pltpu.CompilerParams(dimension_semantics=sem, vmem_limit_bytes=vmem or VMEM_LIMIT)


def _dot(a, b):
    return jnp.dot(a, b, preferred_element_type=F32)


def _dot_nt(a, b):
    return lax.dot_general(a, b, (((1,), (1,)), ((), ())), preferred_element_type=F32)


def _dot_tn(a, b):
    return lax.dot_general(a, b, (((0,), (0,)), ((), ())), preferred_element_type=F32)


def _dot_exact(a, b):
    return jnp.dot(a, b, precision=lax.Precision.HIGHEST, preferred_element_type=F32)


def _matmul(a, b, *, name, ta=False, tb=False, out_dtype=F32, tm=1024, tn=1024, tk=1024,
            relu2=False, relu2_of=None, also_bf16=False, side=None, col_blocks=False):
    if ta:
        kdim, m = a.shape
    else:
        m, kdim = a.shape
    if col_blocks and not ta:
        n = b.shape[1] if tb else b.shape[0] * b.shape[2]
        if tb:
            kdim = b.shape[0] * b.shape[2]
    else:
        n = b.shape[0] if tb else b.shape[1]
    tm, tn, tk = _tile(m, tm), _tile(n, tn), _tile(kdim, tk)
    nk = kdim // tk
    a_spec = pl.BlockSpec((tk, tm), lambda i, j, k: (k, i)) if ta else pl.BlockSpec((tm, tk), lambda i, j, k: (i, k))
    b_spec = pl.BlockSpec((tn, tk), lambda i, j, k: (j, k)) if tb else pl.BlockSpec((tk, tn), lambda i, j, k: (k, j))
    o_spec = pl.BlockSpec((tm, tn), lambda i, j, k: (i, j))
    if col_blocks and ta:
        o_spec = pl.BlockSpec((None, tm, tn), lambda i, j, k: (j, i, 0))
    elif col_blocks and tb:
        assert b.shape[2] == tk
        b_spec = pl.BlockSpec((None, tn, tk), lambda i, j, k: (k, j, 0))
    elif col_blocks:
        assert b.shape[2] == tn
        b_spec = pl.BlockSpec((None, tk, tn), lambda i, j, k: (j, k, 0))
    two = also_bf16

    def body(*refs):
        refs = list(refs)
        a_ref, b_ref = refs[0], refs[1]
        e_ref = refs[2] if relu2_of is not None else None
        pos = 3 if relu2_of is not None else 2
        o_ref = refs[pos]
        o2_ref = refs[pos + 1] if two else None
        acc_ref = refs[-1]
        k = pl.program_id(2)
        av = a_ref[...].astype(BF16)
        bv = b_ref[...].astype(BF16)
        if ta:
            part = _dot_tn(av, bv)
        elif tb:
            part = _dot_nt(av, bv)
        else:
            part = _dot(av, bv)

        @pl.when(k == 0)
        def _():
            acc_ref[...] = part

        @pl.when(k > 0)
        def _():
            acc_ref[...] += part

        @pl.when(k == nk - 1)
        def _():
            r = acc_ref[...]
            if relu2_of is not None:
                r = r * (2.0 * jnp.sqrt(e_ref[...].astype(F32)))
            if relu2:
                r = jnp.square(jnp.maximum(r, 0.0))
            o_ref[...] = r.astype(o_ref.dtype)
            if also_bf16:
                o2_ref[...] = r.astype(BF16)

    in_specs = [a_spec, b_spec]
    args = [a, b]
    if relu2_of is not None:
        in_specs.append(o_spec)
        args.append(relu2_of)
    out_shape = [jax.ShapeDtypeStruct((n // tn, m, tn) if (col_blocks and ta) else (m, n), out_dtype)]
    out_specs = [o_spec]
    if two:
        out_shape.append(jax.ShapeDtypeStruct((m, n), BF16))
        out_specs.append(o_spec)
    grid = (m // tm, n // tn, nk)
    side_in, side_out, side_scratch = _side_specs(side)
    res = pl.pallas_call(
        _carry_side_job(body, len(args), len(out_shape), side, grid), name=name, grid=grid,
        in_specs=in_specs + side_in, out_specs=out_specs + side_out,
        out_shape=out_shape + ([] if side is None else side.out_shape),
        scratch_shapes=[pltpu.VMEM((tm, tn), F32)] + side_scratch,
        compiler_params=_cparams(("parallel", "parallel", "arbitrary") if side is None else ("arbitrary",) * 3),
    )(*args, *([] if side is None else side.inputs))
    main = res[:len(out_shape)]
    main = main if two else main[0]
    return main if side is None else (main, res[len(out_shape):])


def _rms(x, g):
    r = lax.rsqrt(jnp.mean(x * x, axis=-1, keepdims=True) + EPS)
    return x * r * g


def _rms_bwd(x, g, dy):
    r = lax.rsqrt(jnp.mean(x * x, axis=-1, keepdims=True) + EPS)
    xh = x * r
    gdy = dy * g
    dx = r * (gdy - xh * jnp.mean(xh * gdy, axis=-1, keepdims=True))
    return dx, xh * dy


def _norm_fwd(x, g, *, name, resid=None, out_dtype=BF16):
    s, d = x.shape
    tr = _tile(s, 256)
    row = pl.BlockSpec((tr, d), lambda i: (i, 0))
    gsp = pl.BlockSpec((1, d), lambda i: (0, 0))

    def body(*refs):
        if resid is None:
            x_ref, g_ref, o_ref = refs
            o_ref[...] = _rms(x_ref[...], g_ref[...]).astype(o_ref.dtype)
        else:
            x_ref, g_ref, r_ref, o_ref = refs
            o_ref[...] = (r_ref[...] + _rms(x_ref[...], g_ref[...])).astype(o_ref.dtype)

    args = [x, g.reshape(1, d)] + ([] if resid is None else [resid])
    return pl.pallas_call(
        body, name=name, grid=(s // tr,), in_specs=[row, gsp] + ([] if resid is None else [row]),
        out_specs=row, out_shape=jax.ShapeDtypeStruct((s, d), out_dtype), compiler_params=_cparams(("parallel",)),
    )(*args)


def _norm_bwd(x, g, dy, *, name, add=None, out_dtype=F32):
    s, d = x.shape
    tr = _tile(s, 256)
    row = pl.BlockSpec((tr, d), lambda i: (i, 0))
    gsp = pl.BlockSpec((1, d), lambda i: (0, 0))

    def body(*refs):
        if add is None:
            x_ref, g_ref, dy_ref, dx_ref, dg_ref = refs
        else:
            x_ref, g_ref, dy_ref, add_ref, dx_ref, dg_ref = refs
        dx, gterm = _rms_bwd(x_ref[...], g_ref[...], dy_ref[...].astype(F32))
        if add is not None:
            dx = dx + add_ref[...]
        dx_ref[...] = dx.astype(dx_ref.dtype)

        @pl.when(pl.program_id(0) == 0)
        def _():
            dg_ref[...] = jnp.zeros_like(dg_ref)

        dg_ref[...] += jnp.sum(gterm, axis=0, keepdims=True)

    args = [x, g.reshape(1, d), dy] + ([] if add is None else [add])
    return pl.pallas_call(
        body, name=name, grid=(s // tr,), in_specs=[row, gsp, row] + ([] if add is None else [row]),
        out_specs=[row, gsp], out_shape=[jax.ShapeDtypeStruct((s, d), out_dtype), jax.ShapeDtypeStruct((1, d), F32)],
        compiler_params=_cparams(("arbitrary",)),
    )(*args)


def _loss_head(y, target):
    s, d = y.shape
    tr = _tile(s, 256)
    row = pl.BlockSpec((tr, d), lambda i: (i, 0))
    lsp = pl.BlockSpec((1, LANES), lambda i: (0, 0))

    def body(y_ref, t_ref, l_ref, dy_ref):
        e = y_ref[...] - t_ref[...]
        dy_ref[...] = e * (1.0 / d)

        @pl.when(pl.program_id(0) == 0)
        def _():
            l_ref[...] = jnp.zeros_like(l_ref)

        part = 0.5 * jnp.sum(jnp.mean(e * e, axis=-1, keepdims=True), axis=0, keepdims=True)
        l_ref[...] += jnp.broadcast_to(part, (1, LANES))

    return pl.pallas_call(
        body, name="loss_head", grid=(s // tr,), in_specs=[row, row], out_specs=[lsp, row],
        out_shape=[jax.ShapeDtypeStruct((1, LANES), F32), jax.ShapeDtypeStruct((s, d), F32)],
        compiler_params=_cparams(("arbitrary",)),
    )(y, target)


def _rope_tables(pos_col):
    s = pos_col.shape[0]
    tr = _tile(s, 512)
    f_mla = ROPE_BASE ** (-jnp.arange(ROPE_DIM // 2, dtype=F32) / (ROPE_DIM // 2))
    f_ret = ROPE_BASE ** (-jnp.arange(HEAD // 2, dtype=F32) / (HEAD // 2))
    fm = jnp.concatenate([jnp.zeros((64,), F32), f_mla, f_mla, jnp.zeros((32,), F32)]).reshape(1, LANES)
    fr = jnp.tile(jnp.concatenate([f_ret, f_ret]), 4).reshape(1, 2 * LANES)

    def body(p_ref, fm_ref, fr_ref, cm_ref, sm_ref, cr_ref, sr_ref):
        p = p_ref[...].astype(F32)
        am = p * fm_ref[...]
        ar = p * fr_ref[...]
        cm_ref[...] = jnp.cos(am)
        sm_ref[...] = jnp.sin(am)
        cr_ref[...] = jnp.cos(ar)
        sr_ref[...] = jnp.sin(ar)

    return pl.pallas_call(
        body, name="rope_tables", grid=(s // tr,),
        in_specs=[pl.BlockSpec((tr, 1), lambda i: (i, 0)), pl.BlockSpec((1, LANES), lambda i: (0, 0)),
                  pl.BlockSpec((1, 2 * LANES), lambda i: (0, 0))],
        out_specs=[pl.BlockSpec((tr, LANES), lambda i: (i, 0))] * 2 + [pl.BlockSpec((tr, 2 * LANES), lambda i: (i, 0))] * 2,
        out_shape=[jax.ShapeDtypeStruct((s, LANES), F32)] * 2 + [jax.ShapeDtypeStruct((s, 2 * LANES), F32)] * 2,
        compiler_params=_cparams(("parallel",)),
    )(pos_col, fm, fr)


def _lane(shape):
    return lax.broadcasted_iota(jnp.int32, shape, len(shape) - 1)


def _rot_mla(z):
    l = _lane(z.shape) % LANES
    n = z.shape[-1]
    return jnp.where(l < 80, -pltpu.roll(z, n - 16, 1), pltpu.roll(z, 16, 1))


def _rot_mla_t(y):
    l = _lane(y.shape) % LANES
    n = y.shape[-1]
    return jnp.where((l >= 64) & (l < 80), pltpu.roll(y, n - 16, 1),
                     jnp.where((l >= 80) & (l < 96), -pltpu.roll(y, 16, 1), 0.0))


def _rot_ret(z):
    l = _lane(z.shape) % HEAD
    n = z.shape[-1]
    return jnp.where(l < 32, -pltpu.roll(z, n - 32, 1), pltpu.roll(z, 32, 1))


def _rot_ret_t(y):
    l = _lane(y.shape) % HEAD
    n = y.shape[-1]
    return jnp.where(l < 32, pltpu.roll(y, n - 32, 1), -pltpu.roll(y, 32, 1))


def _log_sigmoid(x):
    return jnp.minimum(x, 0.0) - jnp.log1p(jnp.exp(-jnp.abs(x)))


def _fox_cum(proj, bias_row):
    s = proj.shape[0]
    nb = s // TQ

    def body(x_ref, b_ref, cc_ref, cr_ref, carry_ref):
        @pl.when(pl.program_id(0) == 0)
        def _():
            carry_ref[...] = jnp.zeros_like(carry_ref)

        ls = _log_sigmoid(x_ref[...] + b_ref[...])
        r = lax.broadcasted_iota(jnp.int32, (TQ, TQ), 0)
        c = lax.broadcasted_iota(jnp.int32, (TQ, TQ), 1)
        tri = (c <= r).astype(F32)
        cum = _dot_exact(tri, ls) + carry_ref[...]
        carry_ref[...] = cum[TQ - 1:TQ, :]
        cc_ref[...] = cum
        cr_ref[...] = cum.T[0:8, :]

    return pl.pallas_call(
        body, name="fox_cum", grid=(nb,),
        in_specs=[pl.BlockSpec((TQ, LANES), lambda i: (i, OFF_MISC)), pl.BlockSpec((1, LANES), lambda i: (0, 0))],
        out_specs=[pl.BlockSpec((TQ, LANES), lambda i: (i, 0)), pl.BlockSpec((8, TQ), lambda i: (0, i))],
        out_shape=[jax.ShapeDtypeStruct((s, LANES), F32), jax.ShapeDtypeStruct((8, s), F32)],
        scratch_shapes=[pltpu.VMEM((1, LANES), F32)],
        compiler_params=_cparams(("arbitrary",)),
    )(proj, bias_row)


def _fox_gate_bwd(dck, drs, proj, bias_row, dkr):
    s = proj.shape[0]
    nb = s // TQ

    def body(d_ref, r_ref, x_ref, b_ref, k_ref, o_ref, db_ref, carry_ref):
        @pl.when(pl.program_id(0) == 0)
        def _():
            carry_ref[...] = jnp.zeros_like(carry_ref)
            db_ref[...] = jnp.zeros_like(db_ref)

        rows = jnp.concatenate([d_ref[0], d_ref[1], jnp.zeros((TQ - 16, TQ), F32)], axis=0)
        t = rows.T
        l = _lane((TQ, LANES))
        r0, r1 = r_ref[0], r_ref[1]
        rsum = jnp.where(l == 0, r0[:, 0:1], jnp.where(l == 1, r0[:, HEAD:HEAD + 1],
                         jnp.where(l == 2, r1[:, 0:1], jnp.where(l == 3, r1[:, HEAD:HEAD + 1], 0.0))))
        dcum = rsum - jnp.where(l < 2, t, pltpu.roll(t, LANES - 6, 1))
        r = lax.broadcasted_iota(jnp.int32, (TQ, TQ), 0)
        c = lax.broadcasted_iota(jnp.int32, (TQ, TQ), 1)
        triu = (c >= r).astype(F32)
        rc = _dot_exact(triu, dcum) + carry_ref[...]
        carry_ref[...] = rc[0:1, :]
        f = x_ref[...] + b_ref[...]
        sig_neg = 1.0 / (1.0 + jnp.exp(f))
        df = jnp.where(l < N_HEADS, rc * sig_neg, 0.0)
        db_ref[...] += jnp.sum(df, axis=0, keepdims=True)
        o_ref[...] = (df + k_ref[...]).astype(o_ref.dtype)

    rev = lambda i: nb - 1 - i
    return pl.pallas_call(
        body, name="fox_gate_bwd", grid=(nb,),
        in_specs=[pl.BlockSpec((2, 8, TQ), lambda i: (0, 0, rev(i))), pl.BlockSpec((2, TQ, LANES), lambda i: (0, rev(i), 0)),
                  pl.BlockSpec((TQ, LANES), lambda i: (rev(i), OFF_MISC)),
                  pl.BlockSpec((1, LANES), lambda i: (0, 0)), pl.BlockSpec((TQ, LANES), lambda i: (rev(i), 0))],
        out_specs=[pl.BlockSpec((TQ, LANES), lambda i: (rev(i), 0)), pl.BlockSpec((1, LANES), lambda i: (0, 0))],
        out_shape=[jax.ShapeDtypeStruct((s, LANES), BF16), jax.ShapeDtypeStruct((1, LANES), F32)],
        scratch_shapes=[pltpu.VMEM((1, LANES), F32)],
        compiler_params=_cparams(("arbitrary",)),
    )(dck, drs, proj, bias_row, dkr)


def _mla_prep(proj, cos_m, sin_m, g_q, g_kv, wq, wk, wv):
    s = proj.shape[0]
    tr = _tile(s, 256)

    def body(cq_ref, ckv_ref, misc_ref, cos_ref, sin_ref, gq_ref, gkv_ref, wq_ref, wk_ref, wv_ref,
             q_ref, k_ref, v_ref, cqn_ref, ckvn_ref):
        cos4 = jnp.tile(cos_ref[...], (1, 4))
        sin4 = jnp.tile(sin_ref[...], (1, 4))
        cqn = _rms(cq_ref[...], gq_ref[...]).astype(BF16)
        ckvn = _rms(ckv_ref[...], gkv_ref[...]).astype(BF16)
        cqn_ref[...] = cqn
        ckvn_ref[...] = ckvn
        zq = _dot(cqn, wq_ref[...])
        q_ref[...] = (zq * cos4 + _rot_mla(zq) * sin4).astype(BF16)
        l = _lane((tr, LANES))
        kr = jnp.where((l >= KR_LANE) & (l < KR_LANE + ROPE_DIM), misc_ref[...], 0.0)
        zk = _dot(ckvn, wk_ref[...]) + jnp.tile(kr, (1, 4))
        k_ref[...] = (zk * cos4 + _rot_mla(zk) * sin4).astype(BF16)
        v_ref[...] = _dot(ckvn, wv_ref[...]).astype(BF16)

    full = lambda a: pl.BlockSpec(a.shape, lambda i: (0, 0))
    rowb = lambda w: pl.BlockSpec((tr, w), lambda i: (i, 0))
    gq2, gkv2 = g_q.reshape(1, Q_RANK), g_kv.reshape(1, KV_RANK)
    return pl.pallas_call(
        body, name="mla_prep", grid=(s // tr,),
        in_specs=[pl.BlockSpec((tr, 256), lambda i: (i, OFF_CQ // 2)), pl.BlockSpec((tr, LANES), lambda i: (i, OFF_CKV)),
                  pl.BlockSpec((tr, LANES), lambda i: (i, OFF_MISC)), rowb(LANES), rowb(LANES),
                  full(gq2), full(gkv2), full(wq), full(wk), full(wv)],
        out_specs=[rowb(512), rowb(512), rowb(512), rowb(256), rowb(128)],
        out_shape=[jax.ShapeDtypeStruct((s, 512), BF16), jax.ShapeDtypeStruct((s, 512), BF16), jax.ShapeDtypeStruct((s, 512), BF16),
                   jax.ShapeDtypeStruct((s, 256), BF16), jax.ShapeDtypeStruct((s, 128), BF16)],
        compiler_params=_cparams(("parallel",)),
    )(proj, proj, proj, cos_m, sin_m, gq2, gkv2, wq, wk, wv)


def _mla_prep_bwd(dq, dk, dv, proj, cqn, ckvn, cos_m, sin_m, g_q, g_kv, wq, wk, wv):
    s = proj.shape[0]
    tr = _tile(s, 256)

    def body(dq_ref, dk_ref, dv_ref, cq_ref, ckv_ref, cqn_ref, ckvn_ref, cos_ref, sin_ref, gq_ref, gkv_ref,
             wq_ref, wk_ref, wv_ref, dcq_ref, dckv_ref, dkr_ref, dwq_ref, dwk_ref, dwv_ref, dgq_ref, dgkv_ref):
        @pl.when(pl.program_id(0) == 0)
        def _():
            for r in (dwq_ref, dwk_ref, dwv_ref, dgq_ref, dgkv_ref):
                r[...] = jnp.zeros_like(r)

        cos4 = jnp.tile(cos_ref[...], (1, 4))
        sin4 = jnp.tile(sin_ref[...], (1, 4))
        dqv = dq_ref[...]
        dzq = dqv * cos4 + _rot_mla_t(dqv * sin4)
        dkv_ = dk_ref[...]
        dzk = dkv_ * cos4 + _rot_mla_t(dkv_ * sin4)
        l = _lane((tr, LANES))
        in_rope = (l >= KR_LANE) & (l < KR_LANE + ROPE_DIM)
        dkr = dzk[:, 0:128] + dzk[:, 128:256] + dzk[:, 256:384] + dzk[:, 384:512]
        dkr_ref[...] = jnp.where(in_rope, dkr, 0.0)
        dzq_b = dzq.astype(BF16)
        dzk_b = dzk.astype(BF16)
        dv_b = dv_ref[...].astype(BF16)
        dcqn = _dot_nt(dzq_b, wq_ref[...])
        dckvn = _dot_nt(dzk_b, wk_ref[...]) + _dot_nt(dv_b, wv_ref[...])
        dwq_ref[...] += _dot_tn(cqn_ref[...], dzq_b)
        dwk_ref[...] += _dot_tn(ckvn_ref[...], dzk_b)
        dwv_ref[...] += _dot_tn(ckvn_ref[...], dv_b)
        dcq, gq_term = _rms_bwd(cq_ref[...], gq_ref[...], dcqn)
        dckv, gkv_term = _rms_bwd(ckv_ref[...], gkv_ref[...], dckvn)
        dcq_ref[...] = dcq.astype(BF16)
        dckv_ref[...] = dckv.astype(BF16)
        dgq_ref[...] += jnp.sum(gq_term, axis=0, keepdims=True)
        dgkv_ref[...] += jnp.sum(gkv_term, axis=0, keepdims=True)

    full = lambda shp: pl.BlockSpec(shp, lambda i: (0, 0))
    rowb = lambda w: pl.BlockSpec((tr, w), lambda i: (i, 0))
    gq2, gkv2 = g_q.reshape(1, Q_RANK), g_kv.reshape(1, KV_RANK)
    return pl.pallas_call(
        body, name="mla_prep_bwd", grid=(s // tr,),
        in_specs=[rowb(512), rowb(512), rowb(512),
                  pl.BlockSpec((tr, 256), lambda i: (i, OFF_CQ // 2)), pl.BlockSpec((tr, LANES), lambda i: (i, OFF_CKV)),
                  rowb(256), rowb(128), rowb(LANES), rowb(LANES), full((1, Q_RANK)), full((1, KV_RANK)),
                  full(wq.shape), full(wk.shape), full(wv.shape)],
        out_specs=[rowb(256), rowb(128), rowb(128), full(wq.shape), full(wk.shape), full(wv.shape),
                   full((1, Q_RANK)), full((1, KV_RANK))],
        out_shape=[jax.ShapeDtypeStruct((s, 256), BF16), jax.ShapeDtypeStruct((s, 128), BF16), jax.ShapeDtypeStruct((s, 128), F32),
                   jax.ShapeDtypeStruct(wq.shape, F32), jax.ShapeDtypeStruct(wk.shape, F32), jax.ShapeDtypeStruct(wv.shape, F32),
                   jax.ShapeDtypeStruct((1, Q_RANK), F32), jax.ShapeDtypeStruct((1, KV_RANK), F32)],
        compiler_params=_cparams(("arbitrary",)),
    )(dq, dk, dv, proj, proj, cqn, ckvn, cos_m, sin_m, gq2, gkv2, wq, wk, wv)


def _ret_prep(proj, cos_r, sin_r):
    s = proj.shape[0]
    tr = _tile(s, 256)

    def body(q_ref, k_ref, cos_ref, sin_ref, qo_ref, ko_ref):
        cos, sin = cos_ref[...], sin_ref[...]
        q, k = q_ref[...], k_ref[...]
        qo_ref[...] = (q * cos + _rot_ret(q) * sin).astype(BF16)
        ko_ref[...] = ((k * cos + _rot_ret(k) * sin) * (HEAD ** -0.5)).astype(BF16)

    rowb = pl.BlockSpec((tr, 256), lambda i: (i, 0))
    return pl.pallas_call(
        body, name="ret_prep", grid=(s // tr,),
        in_specs=[pl.BlockSpec((tr, 256), lambda i: (i, OFF_RQ // 2)), pl.BlockSpec((tr, 256), lambda i: (i, OFF_RK // 2)), rowb, rowb],
        out_specs=[rowb, rowb], out_shape=[jax.ShapeDtypeStruct((s, 256), BF16)] * 2,
        compiler_params=_cparams(("parallel",)),
    )(proj, proj, cos_r, sin_r)


def _ret_prep_bwd(dq, dk, cos_r, sin_r):
    s = dq.shape[0]
    tr = _tile(s, 256)

    def body(dq_ref, dk_ref, cos_ref, sin_ref, qo_ref, ko_ref):
        cos, sin = cos_ref[...], sin_ref[...]
        q, k = dq_ref[...], dk_ref[...] * (HEAD ** -0.5)
        qo_ref[...] = (q * cos + _rot_ret_t(q * sin)).astype(BF16)
        ko_ref[...] = (k * cos + _rot_ret_t(k * sin)).astype(BF16)

    rowb = pl.BlockSpec((tr, 256), lambda i: (i, 0))
    return pl.pallas_call(
        body, name="ret_prep_bwd", grid=(s // tr,), in_specs=[rowb] * 4, out_specs=[rowb, rowb],
        out_shape=[jax.ShapeDtypeStruct((s, 256), BF16)] * 2, compiler_params=_cparams(("parallel",)),
    )(dq, dk, cos_r, sin_r)


_LOG_GAMMA = [float(np.log1p(-np.float32(2.0) ** np.float32(-5.0 - h))) for h in range(N_HEADS)]
_MLA_SCALE = float((HEAD + ROPE_DIM) ** -0.5)
_QK_SCALE = float(HEAD ** -0.5)
KEY_BLOCKS = 4
QB = 256


def _split2(x):
    h = x.astype(BF16)
    return h, (x - h.astype(F32)).astype(BF16)


def _dot2(x, u):
    h, lo = _split2(x)
    return _dot(h, u) + _dot(lo, u)


def _head_pick(block, head, axis):
    idx = lax.broadcasted_iota(jnp.int32, block.shape, axis)
    return jnp.sum(jnp.where(idx == head, block, 0.0), axis=axis, keepdims=True)


def _log_gamma_of(head):
    lg = jnp.float32(_LOG_GAMMA[3])
    for h in (2, 1, 0):
        lg = jnp.where(head == h, jnp.float32(_LOG_GAMMA[h]), lg)
    return lg


def _mixer_specs(mode, s, q_off, k_off, v_off):
    nhb = 2
    bw = 2 * LANES if mode == "mla" else LANES
    nsub = KEY_BLOCKS if (s // TQ) % KEY_BLOCKS == 0 else 1
    q_spec = pl.BlockSpec((QB, bw), lambda p, i: (i, q_off + p))
    k_spec = pl.BlockSpec((s, bw), lambda p, i: (0, k_off + p))
    v_spec = pl.BlockSpec((s, bw), lambda p, i: (0, v_off + p))
    return nhb, N_HEADS // nhb, nsub, q_spec, k_spec, v_spec


def _mixer_geometry(mode, i, nsub):
    w = TQ * nsub
    row = lax.broadcasted_iota(jnp.int32, (QB, w), 0)
    col = lax.broadcasted_iota(jnp.int32, (QB, w), 1)
    nfull = (i * QB) // w
    dist = col - row
    if mode in ("fox", "sb"):
        rel = dist
    else:
        rel = col - (row | (CHUNK - 1))

    def visible(c):
        off = c * w - i * QB
        return (rel + off) < 0 if mode == "sb" else (rel + off) <= 0

    return nfull, dist, visible


class _SideJob:
    def __init__(self, inputs, out_shape, n_sems, sends, recvs):
        self.inputs, self.out_shape, self.n_sems, self.sends, self.recvs = list(inputs), list(out_shape), n_sems, sends, recvs


def _carry_side_job(body, n_in, n_out, side, n_steps):
    if side is None:
        return body
    si, so = len(side.inputs), len(side.out_shape)

    def at(corner):
        ok = pl.program_id(0) == corner[0]
        for d in range(1, len(n_steps)):
            ok = ok & (pl.program_id(d) == corner[d])
        return ok

    def wrapped(*refs):
        ins, s_ins = refs[:n_in], refs[n_in:n_in + si]
        outs, s_outs = refs[n_in + si:n_in + si + n_out], refs[n_in + si + n_out:n_in + si + n_out + so]
        scratch, send, recv = refs[n_in + si + n_out + so:-2], refs[-2], refs[-1]

        @pl.when(at([0] * len(n_steps)))
        def _():
            for cp in side.sends(s_ins, s_outs, send, recv):
                cp.start()

        body(*ins, *outs, *scratch)

        @pl.when(at([n - 1 for n in n_steps]))
        def _():
            for cp in side.recvs(s_ins, s_outs, send, recv):
                cp.wait_recv()
            for cp in side.sends(s_ins, s_outs, send, recv):
                cp.wait_send()

    return wrapped


def _side_specs(side):
    if side is None:
        return [], [], []
    hbm = pl.BlockSpec(memory_space=pl.ANY)
    return ([hbm] * len(side.inputs), [hbm] * len(side.out_shape),
            [pltpu.SemaphoreType.DMA((side.n_sems,)), pltpu.SemaphoreType.DMA((side.n_sems,))])


def _mixer_fwd(mode, qa, q_off, ka, k_off, va, v_off, *, cum_col=None, cum_row=None, side=None):
    s = qa.shape[0]
    nq = s // QB
    nhb, nblk, nsub, q_spec, k_spec, v_spec = _mixer_specs(mode, s, q_off, k_off, v_off)
    w = TQ * nsub
    softmax = mode in ("fox", "mla")

    def body(*refs):
        refs = list(refs)
        q_ref, k_ref, v_ref = refs[:3]
        refs = refs[3:]
        if mode == "fox":
            cc_ref, cr_ref = refs[:2]
            refs = refs[2:]
        o_ref = refs[0]
        st_ref = refs[1]
        p = pl.program_id(0)
        i = pl.program_id(1)
        nfull, dist, visible = _mixer_geometry(mode, i, nsub)
        lane = _lane((1, LANES))
        heads = [nhb * p + hh for hh in range(nhb)]
        wide = mode == "mla"
        q_scale = _QK_SCALE if mode in ("fox", "sb") else 1.0
        cols = [slice(hh * LANES, (hh + 1) * LANES) if wide else slice(None) for hh in range(nhb)]
        if wide:
            qs = [q_ref[:, cols[hh]] for hh in range(nhb)]
        else:
            qf = q_ref[...].astype(F32) * q_scale
            qs = [jnp.where((lane // HEAD) == hh, qf, 0.0).astype(BF16) for hh in range(nhb)]
        if mode == "fox":
            cqs = [_head_pick(cc_ref[...], h, 1) for h in heads]
        if mode == "sb":
            r1 = lax.broadcasted_iota(jnp.int32, (TQ, TQ), 0)
            c1 = lax.broadcasted_iota(jnp.int32, (TQ, TQ), 1)
            u_after = (r1 > c1).astype(BF16)

        def chunk(c):
            return pl.ds(pl.multiple_of(c * w, w), w)

        def scores(c):
            js = chunk(c)
            return tuple(_dot_nt(qs[hh], k_ref[js, cols[hh]]) for hh in range(nhb))

        def head_step(hh, c, js, sc, vj, carry, last):
            if softmax:
                m, l, acc = carry
                if mode == "fox":
                    ck = _head_pick(cr_ref[:, js], heads[hh], 0)
                    sc = sc + (cqs[hh] - ck)
                else:
                    sc = sc * _MLA_SCALE
                if last:
                    sc = jnp.where(visible(c), sc, NEG)
                m_new = jnp.maximum(m, jnp.max(sc, axis=-1, keepdims=True))
                alpha = jnp.exp(m - m_new)
                pr = jnp.exp(sc - m_new)
                l = alpha * l + jnp.sum(pr, axis=-1, keepdims=True)
                acc = alpha * acc + _dot(pr.astype(BF16), vj)
                return m_new, l, acc
            run, acc = carry
            z = sc
            log_beta = jnp.minimum(z, 0.0) - jnp.log(1.0 + jnp.exp(-jnp.abs(z)))
            log_stay = log_beta - z
            if last:
                vis = visible(c)
                log_stay = jnp.where(vis, log_stay, 0.0)
            parts = [None] * nsub
            for b in reversed(range(nsub)):
                ls_b = log_stay[:, b * TQ:(b + 1) * TQ]
                parts[b] = _dot2(ls_b, u_after) + run
                run = run + jnp.sum(ls_b, axis=-1, keepdims=True)
            later = parts[0] if nsub == 1 else jnp.concatenate(parts, axis=1)
            wgt = jnp.exp(log_beta + later)
            if last:
                wgt = jnp.where(vis, wgt, 0.0)
            return run, acc + _dot(wgt.astype(BF16), vj)

        def step(c, c_next, state, last):
            scs, carries = state
            nxt = scores(c_next) if c_next is not None else None
            js = chunk(c)
            return nxt, tuple(head_step(hh, c, js, scs[hh], v_ref[js, cols[hh]], carries[hh], last) for hh in range(nhb))

        zero_acc = jnp.zeros((QB, LANES), F32)
        zero1 = jnp.zeros((QB, 1), F32)
        if softmax:
            init = tuple((jnp.full((QB, 1), NEG, F32), zero1, zero_acc) for _ in range(nhb))
        else:
            init = tuple((zero1, zero_acc) for _ in range(nhb))
        if mode == "sb":
            state = step(nfull, jnp.maximum(nfull - 1, 0), (scores(nfull), init), True)
            _, carries = lax.fori_loop(0, nfull, lambda t, st: step(nfull - 1 - t, jnp.maximum(nfull - 2 - t, 0), st, False), state)
        else:
            state = lax.fori_loop(0, nfull, lambda c, st: step(c, c + 1, st, False), (scores(0), init))
            _, carries = step(nfull, None, state, True)
        if softmax:
            outs = [acc / l for (m, l, acc) in carries]
            stats = [m + jnp.log(l) for (m, l, acc) in carries]
        else:
            outs, stats = [acc for (run, acc) in carries], [run for (run, acc) in carries]
        hm0 = (lane // HEAD) == 0
        pick = lambda a: jnp.where(hm0, a[0], a[1])
        if wide:
            for hh in range(nhb):
                o_ref[:, cols[hh]] = outs[hh]
        else:
            o_ref[...] = pick(outs)
        st_ref[0] = pick(stats)

    in_specs = [q_spec, k_spec, v_spec]
    args = [qa, ka, va]
    if mode == "fox":
        in_specs += [pl.BlockSpec((QB, LANES), lambda p, i: (i, 0)), pl.BlockSpec((8, s), lambda p, i: (0, 0))]
        args += [cum_col, cum_row]
    bw = 2 * LANES if mode == "mla" else LANES
    out_specs = [pl.BlockSpec((QB, bw), lambda p, i: (i, p))]
    out_shape = [jax.ShapeDtypeStruct((s, nblk * bw), F32)]
    out_specs.append(pl.BlockSpec((1, QB, LANES), lambda p, i: (p, i, 0)))
    out_shape.append(jax.ShapeDtypeStruct((nblk, s, LANES), F32))
    side_in, side_out, side_scratch = _side_specs(side)
    res = pl.pallas_call(
        _carry_side_job(body, len(args), len(out_shape), side, (nblk, nq)), name=mode + "_fwd", grid=(nblk, nq),
        in_specs=in_specs + side_in, out_specs=out_specs + side_out,
        out_shape=out_shape + ([] if side is None else side.out_shape), scratch_shapes=side_scratch,
        compiler_params=_cparams(("parallel", "parallel") if side is None else ("arbitrary", "arbitrary")),
    )(*args, *([] if side is None else side.inputs))
    return (res[0], res[1]) if side is None else (res[0], res[1], res[2:])


def _mixer_bwd(mode, qa, q_off, ka, k_off, va, v_off, o, do, *, stat=None, cum_col=None, cum_row=None, side=None):
    s = qa.shape[0]
    nq = s // QB
    nhb, nblk, nsub, q_spec, k_spec, v_spec = _mixer_specs(mode, s, q_off, k_off, v_off)
    w = TQ * nsub
    softmax = mode in ("fox", "mla")

    def body(*refs):
        refs = list(refs)
        q_ref, k_ref, v_ref, o_ref, do_ref = refs[:5]
        refs = refs[5:]
        st_ref = refs[0]
        refs = refs[1:]
        if mode == "fox":
            cc_ref, cr_ref = refs[:2]
            refs = refs[2:]
        dq_ref, dk_ref, dv_ref = refs[:3]
        dck_ref, drs_ref = refs[3:5] if mode == "fox" else (None, None)
        p = pl.program_id(0)
        i = pl.program_id(1)

        @pl.when(i == 0)
        def _():
            dk_ref[...] = jnp.zeros_like(dk_ref)
            dv_ref[...] = jnp.zeros_like(dv_ref)
            if mode == "fox":
                dck_ref[...] = jnp.zeros_like(dck_ref)

        nfull, dist, visible = _mixer_geometry(mode, i, nsub)
        lane = _lane((1, LANES))
        heads = [nhb * p + hh for hh in range(nhb)]
        dov = do_ref[...]
        wide = mode == "mla"
        q_scale = _QK_SCALE if mode in ("fox", "sb") else 1.0
        cols = [slice(hh * LANES, (hh + 1) * LANES) if wide else slice(None) for hh in range(nhb)]
        if wide:
            prod = dov * o_ref[...]
            qs = [q_ref[:, cols[hh]] for hh in range(nhb)]
            dos = [dov[:, cols[hh]].astype(BF16) for hh in range(nhb)]
            deltas = [jnp.sum(prod[:, cols[hh]], axis=-1, keepdims=True) for hh in range(nhb)]
        else:
            qf = q_ref[...].astype(F32) * q_scale
            prod = dov * o_ref[...]
            hms = [(lane // HEAD) == hh for hh in range(nhb)]
            qs = [jnp.where(hm, qf, 0.0).astype(BF16) for hm in hms]
            dos = [jnp.where(hm, dov, 0.0).astype(BF16) for hm in hms]
            deltas = [jnp.sum(jnp.where(hm, prod, 0.0), axis=-1, keepdims=True) for hm in hms]
        st = st_ref[0]
        stats = [st[:, hh * HEAD:hh * HEAD + 1] for hh in range(nhb)]
        if mode == "fox":
            cqs = [_head_pick(cc_ref[...], h, 1) for h in heads]
        if mode == "sb":
            r1 = lax.broadcasted_iota(jnp.int32, (TQ, TQ), 0)
            c1 = lax.broadcasted_iota(jnp.int32, (TQ, TQ), 1)
            u_upto = (r1 <= c1).astype(BF16)
            u_before = (r1 < c1).astype(BF16)

        def chunk(c):
            return pl.ds(pl.multiple_of(c * w, w), w)

        def scores(c):
            js = chunk(c)
            if mode == "sb":
                return tuple((_dot_nt(qs[hh], k_ref[js, cols[hh]]), None) for hh in range(nhb))
            return tuple((_dot_nt(qs[hh], k_ref[js, cols[hh]]), _dot_nt(dos[hh], v_ref[js, cols[hh]])) for hh in range(nhb))

        def emit(hh, js, ds_b, pr_b, dq):
            dk_ref[js, cols[hh]] += _dot_tn(ds_b, qs[hh])
            dv_ref[js, cols[hh]] += _dot_tn(pr_b, dos[hh])
            return dq + _dot(ds_b, k_ref[js, cols[hh]])

        def head_step(hh, c, js, sc_dp, carry, last):
            sc, dp = sc_dp
            if dp is None:
                dp = _dot_nt(dos[hh], v_ref[js, cols[hh]])
            if softmax:
                dq, rsum = carry
                if mode == "fox":
                    ck = _head_pick(cr_ref[:, js], heads[hh], 0)
                    sc = sc + (cqs[hh] - ck)
                else:
                    sc = sc * _MLA_SCALE
                if last:
                    sc = jnp.where(visible(c), sc, NEG)
                pr = jnp.exp(sc - stats[hh])
                ds = pr * (dp - deltas[hh])
                if mode == "fox":
                    dck_ref[0, hh:hh + 1, js] += jnp.sum(ds, axis=0, keepdims=True)
                    rsum = rsum + jnp.sum(ds, axis=-1, keepdims=True)
                if mode == "mla":
                    ds = ds * _MLA_SCALE
                return emit(hh, js, ds.astype(BF16), pr.astype(BF16), dq), rsum
            seen, gsum, dq = carry
            z = sc
            log_beta = jnp.minimum(z, 0.0) - jnp.log(1.0 + jnp.exp(-jnp.abs(z)))
            log_stay = log_beta - z
            if last:
                vis = visible(c)
                log_stay = jnp.where(vis, log_stay, 0.0)
            parts = []
            for b in range(nsub):
                ls_b = log_stay[:, b * TQ:(b + 1) * TQ]
                parts.append((stats[hh] - seen) - _dot2(ls_b, u_upto))
                seen = seen + jnp.sum(ls_b, axis=-1, keepdims=True)
            later = parts[0] if nsub == 1 else jnp.concatenate(parts, axis=1)
            wgt = jnp.exp(log_beta + later)
            if last:
                wgt = jnp.where(vis, wgt, 0.0)
            g = dp * wgt
            parts = []
            for b in range(nsub):
                g_b = g[:, b * TQ:(b + 1) * TQ]
                parts.append(gsum + _dot2(g_b, u_before))
                gsum = gsum + jnp.sum(g_b, axis=-1, keepdims=True)
            before = parts[0] if nsub == 1 else jnp.concatenate(parts, axis=1)
            beta = jnp.exp(log_beta)
            dz = g * (1.0 - beta) - beta * before
            if last:
                dz = jnp.where(vis, dz, 0.0)
            return seen, gsum, emit(hh, js, dz.astype(BF16), wgt.astype(BF16), dq)

        def step(c, c_next, state, last):
            scs, carries = state
            nxt = scores(c_next) if c_next is not None else None
            js = chunk(c)
            return nxt, tuple(head_step(hh, c, js, scs[hh], carries[hh], last) for hh in range(nhb))

        zero_acc = jnp.zeros((QB, LANES), F32)
        zero1 = jnp.zeros((QB, 1), F32)
        if softmax:
            init = tuple((zero_acc, zero1) for _ in range(nhb))
        else:
            init = tuple((zero1, zero1, zero_acc) for _ in range(nhb))
        state = lax.fori_loop(0, nfull, lambda c, st: step(c, c + 1, st, False), (scores(0), init))
        _, carries = step(nfull, None, state, True)
        if softmax:
            dqs = [dq for (dq, rsum) in carries]
        else:
            dqs = [dq for (seen, gsum, dq) in carries]
        hm0 = (lane // HEAD) == 0
        if wide:
            for hh in range(nhb):
                dq_ref[:, cols[hh]] = dqs[hh]
        else:
            dq_ref[...] = jnp.where(hm0, dqs[0], dqs[1]) * q_scale
        if mode == "fox":
            drs_ref[0] = jnp.where(hm0, carries[0][1], carries[1][1])

    bw = 2 * LANES if mode == "mla" else LANES
    pair_blk = pl.BlockSpec((QB, bw), lambda p, i: (i, p))
    full_blk = pl.BlockSpec((s, bw), lambda p, i: (0, p))
    stat_blk = pl.BlockSpec((1, QB, LANES), lambda p, i: (p, i, 0))
    in_specs = [q_spec, k_spec, v_spec, pair_blk, pair_blk]
    args = [qa, ka, va, o, do]
    in_specs.append(stat_blk)
    args.append(stat)
    if mode == "fox":
        in_specs += [pl.BlockSpec((QB, LANES), lambda p, i: (i, 0)), pl.BlockSpec((8, s), lambda p, i: (0, 0))]
        args += [cum_col, cum_row]
    out_specs = [pair_blk, full_blk, full_blk]
    out_shape = [jax.ShapeDtypeStruct((s, nblk * bw), F32)] * 3
    if mode == "fox":
        out_specs += [pl.BlockSpec((1, 8, s), lambda p, i: (p, 0, 0)), stat_blk]
        out_shape += [jax.ShapeDtypeStruct((2, 8, s), F32), jax.ShapeDtypeStruct((2, s, LANES), F32)]
    side_in, side_out, side_scratch = _side_specs(side)
    res = pl.pallas_call(
        _carry_side_job(body, len(args), len(out_shape), side, (nblk, nq)), name=mode + "_bwd", grid=(nblk, nq),
        in_specs=in_specs + side_in, out_specs=out_specs + side_out,
        out_shape=out_shape + ([] if side is None else side.out_shape), scratch_shapes=side_scratch,
        compiler_params=_cparams(("parallel", "arbitrary") if side is None else ("arbitrary", "arbitrary")),
    )(*args, *([] if side is None else side.inputs))
    return res if side is None else (*res[:len(out_shape)], res[len(out_shape):])


def _ret_geometry(p):
    lane = _lane((1, LANES))
    lg_lane = jnp.where(lane < HEAD, _log_gamma_of(2 * p), _log_gamma_of(2 * p + 1))
    a = lax.broadcasted_iota(jnp.int32, (TQ, 1), 0).astype(F32)
    row = lax.broadcasted_iota(jnp.int32, (TQ, TQ), 0)
    col = lax.broadcasted_iota(jnp.int32, (TQ, TQ), 1)
    same_chunk_or_earlier = (col // CHUNK) <= (row // CHUNK)
    gap = jnp.abs(row - col).astype(F32)
    decays = [jnp.where(same_chunk_or_earlier, jnp.exp(_log_gamma_of(2 * p + hh) * gap), 0.0) for hh in range(2)]
    r = lax.broadcasted_iota(jnp.int32, (LANES, LANES), 0)
    c = lax.broadcasted_iota(jnp.int32, (LANES, LANES), 1)
    own_head = (r // HEAD) == (c // HEAD)
    return lane, lg_lane, a, decays, own_head


def _ret_fwd(qa, ka, va, v_off):
    s = qa.shape[0]
    nq = s // TQ

    def body(q_ref, k_ref, v_ref, o_ref, st_ref, state):
        p = pl.program_id(0)

        @pl.when(pl.program_id(1) == 0)
        def _():
            state[...] = jnp.zeros_like(state)

        lane, lg_lane, a, decays, own_head = _ret_geometry(p)
        q = q_ref[...].astype(F32)
        k = k_ref[...]
        v = v_ref[...]
        s_in = state[...]
        st_ref[0, 0] = s_in
        out = _dot((q * jnp.exp(lg_lane * (a + 1.0))).astype(BF16), s_in.astype(BF16))
        for hh in range(2):
            hm = (lane // HEAD) == hh
            qh = jnp.where(hm, q, 0.0).astype(BF16)
            inner = _dot((_dot_nt(qh, k) * decays[hh]).astype(BF16), v)
            out = out + jnp.where(hm, inner, 0.0)
        o_ref[...] = out
        k_tail = (k.astype(F32) * jnp.exp(lg_lane * (TQ - 1.0 - a))).astype(BF16)
        state[...] = jnp.exp(lg_lane * float(TQ)) * s_in + jnp.where(own_head, _dot_tn(k_tail, v), 0.0)

    blk = lambda off: pl.BlockSpec((TQ, LANES), lambda p, i: (i, off + p))
    return pl.pallas_call(
        body, name="ret_fwd", grid=(2, nq), in_specs=[blk(0), blk(0), blk(v_off)],
        out_specs=[blk(0), pl.BlockSpec((1, 1, LANES, LANES), lambda p, i: (p, i, 0, 0))],
        out_shape=[jax.ShapeDtypeStruct((s, 2 * LANES), F32), jax.ShapeDtypeStruct((2, nq, LANES, LANES), F32)],
        scratch_shapes=[pltpu.VMEM((LANES, LANES), F32)],
        compiler_params=_cparams(("parallel", "arbitrary")),
    )(qa, ka, va)


def _ret_bwd(qa, ka, va, v_off, states, do):
    s = qa.shape[0]
    nq = s // TQ

    def body(q_ref, k_ref, v_ref, st_ref, do_ref, dq_ref, dk_ref, dv_ref, dstate):
        p = pl.program_id(0)

        @pl.when(pl.program_id(1) == 0)
        def _():
            dstate[...] = jnp.zeros_like(dstate)

        lane, lg_lane, a, decays, own_head = _ret_geometry(p)
        q = q_ref[...].astype(F32)
        k = k_ref[...]
        kf = k.astype(F32)
        v = v_ref[...]
        dov = do_ref[...]
        s_in = st_ref[0, 0].astype(BF16)
        ds_next = dstate[...]
        ds_b = ds_next.astype(BF16)
        head_decay = jnp.exp(lg_lane * (a + 1.0))
        tail_decay = jnp.exp(lg_lane * (TQ - 1.0 - a))
        k_tail = (kf * tail_decay).astype(BF16)
        dq = _dot_nt(dov.astype(BF16), s_in) * head_decay
        dk = _dot_nt(v, ds_b) * tail_decay
        dv = _dot(k_tail, ds_b)
        for hh in range(2):
            hm = (lane // HEAD) == hh
            qh = jnp.where(hm, q, 0.0).astype(BF16)
            doh = jnp.where(hm, dov, 0.0).astype(BF16)
            att = (_dot_nt(qh, k) * decays[hh]).astype(BF16)
            datt = (_dot_nt(doh, v) * decays[hh]).astype(BF16)
            dv = dv + _dot_tn(att, doh)
            dk = dk + _dot_tn(datt, qh)
            dq = dq + jnp.where(hm, _dot(datt, k), 0.0)
        dq_ref[...] = dq
        dk_ref[...] = dk
        dv_ref[...] = dv
        q_head = (q * head_decay).astype(BF16)
        dstate[...] = jnp.exp(lg_lane * float(TQ)) * ds_next + jnp.where(own_head, _dot_tn(q_head, dov.astype(BF16)), 0.0)

    blk = lambda off: pl.BlockSpec((TQ, LANES), lambda p, i: (nq - 1 - i, off + p))
    return pl.pallas_call(
        body, name="ret_bwd", grid=(2, nq),
        in_specs=[blk(0), blk(0), blk(v_off), pl.BlockSpec((1, 1, LANES, LANES), lambda p, i: (p, nq - 1 - i, 0, 0)), blk(0)],
        out_specs=[blk(0)] * 3, out_shape=[jax.ShapeDtypeStruct((s, 2 * LANES), F32)] * 3,
        scratch_shapes=[pltpu.VMEM((LANES, LANES), F32)],
        compiler_params=_cparams(("parallel", "arbitrary")),
    )(qa, ka, va, states, do)


def _seg_mean_matrix():
    r = lax.broadcasted_iota(jnp.int32, (GROUP, GROUP), 0)
    c = lax.broadcasted_iota(jnp.int32, (GROUP, GROUP), 1)
    return jnp.where((r // HEAD) == (c // HEAD), 1.0 / HEAD, 0.0).astype(F32)


def _sigmoid(x):
    return 1.0 / (1.0 + jnp.exp(-x))


def _mix_post(oa, ob, oc, od, proj, g):
    s = oa.shape[0]
    tr = _tile(s, 256)

    def body(a_ref, b_ref, c_ref, d_ref, rg_ref, g_ref, o_ref):
        gv = g_ref[...]
        o_ref[:, 0:GROUP] = _rms(a_ref[...], gv[:, 0:GROUP]).astype(BF16)
        o_ref[:, GROUP:2 * GROUP] = _rms(b_ref[...], gv[:, GROUP:2 * GROUP]).astype(BF16)
        seg = _seg_mean_matrix()
        c = c_ref[...]
        cen = c - _dot_exact(c, seg)
        n = cen * lax.rsqrt(_dot_exact(cen * cen, seg) + EPS)
        rg = rg_ref[...]
        o_ref[:, 2 * GROUP:3 * GROUP] = (n * gv[:, 2 * GROUP:3 * GROUP] * (rg * _sigmoid(rg))).astype(BF16)
        o_ref[:, 3 * GROUP:] = _rms(d_ref[...], gv[:, 3 * GROUP:]).astype(BF16)

    blk = pl.BlockSpec((tr, GROUP), lambda i: (i, 0))
    return pl.pallas_call(
        body, name="mix_post", grid=(s // tr,),
        in_specs=[blk] * 4 + [pl.BlockSpec((tr, GROUP), lambda i: (i, OFF_RG // 2)), pl.BlockSpec((1, D_MODEL), lambda i: (0, 0))],
        out_specs=pl.BlockSpec((tr, D_MODEL), lambda i: (i, 0)), out_shape=jax.ShapeDtypeStruct((s, D_MODEL), BF16),
        compiler_params=_cparams(("parallel",)),
    )(oa, ob, oc, od, proj, g.reshape(1, D_MODEL))


def _mix_post_bwd(dmixed, oa, ob, oc, od, proj, g):
    s = oa.shape[0]
    tr = _tile(s, 256)

    def body(dm_ref, a_ref, b_ref, c_ref, d_ref, rg_ref, g_ref, da_ref, db_ref, dc_ref, dd_ref, drg_ref, dg_ref):
        @pl.when(pl.program_id(0) == 0)
        def _():
            dg_ref[...] = jnp.zeros_like(dg_ref)

        gv = g_ref[...]
        dm = dm_ref[...]
        for k, (x_ref, dx_ref) in enumerate(((a_ref, da_ref), (b_ref, db_ref), (None, None), (d_ref, dd_ref))):
            if x_ref is None:
                continue
            cols = slice(k * GROUP, (k + 1) * GROUP)
            dx, gterm = _rms_bwd(x_ref[...], gv[:, cols], dm[:, cols])
            dx_ref[...] = dx
            dg_ref[:, cols] += jnp.sum(gterm, axis=0, keepdims=True)
        cols = slice(2 * GROUP, 3 * GROUP)
        seg = _seg_mean_matrix()
        c = c_ref[...]
        cen = c - _dot_exact(c, seg)
        rstd = lax.rsqrt(_dot_exact(cen * cen, seg) + EPS)
        n = cen * rstd
        rg = rg_ref[...]
        sg = _sigmoid(rg)
        gate = rg * sg
        dy = dm[:, cols]
        gc = gv[:, cols]
        dn = dy * gc * gate
        dg_ref[:, cols] += jnp.sum(dy * n * gate, axis=0, keepdims=True)
        drg_ref[...] = (dy * n * gc * (sg * (1.0 + rg * (1.0 - sg)))).astype(BF16)
        dc_ref[...] = rstd * (dn - _dot_exact(dn, seg) - n * _dot_exact(dn * n, seg))

    blk = pl.BlockSpec((tr, GROUP), lambda i: (i, 0))
    gsp = pl.BlockSpec((1, D_MODEL), lambda i: (0, 0))
    return pl.pallas_call(
        body, name="mix_post_bwd", grid=(s // tr,),
        in_specs=[pl.BlockSpec((tr, D_MODEL), lambda i: (i, 0))] + [blk] * 4 + [pl.BlockSpec((tr, GROUP), lambda i: (i, OFF_RG // 2)), gsp],
        out_specs=[blk] * 5 + [gsp],
        out_shape=[jax.ShapeDtypeStruct((s, GROUP), F32)] * 4 + [jax.ShapeDtypeStruct((s, GROUP), BF16), jax.ShapeDtypeStruct((1, D_MODEL), F32)],
        compiler_params=_cparams(("arbitrary",)),
    )(dmixed, oa, ob, oc, od, proj, g.reshape(1, D_MODEL))


def _pack_w_in(w):
    z = lambda n: jnp.zeros((w.shape[0], n), w.dtype)
    misc = jnp.concatenate([w[:, 768:772], z(KR_LANE - N_HEADS), w[:, 1156:1188], z(LANES - KR_LANE - ROPE_DIM)], axis=1)
    return jnp.concatenate([w[:, 0:768], w[:, 772:1028], w[:, 1188:2980], w[:, 1028:1156], misc], axis=1)


def _unpack_dw_in(d):
    m = OFF_MISC * LANES
    return jnp.concatenate([d[:, 0:768], d[:, m:m + N_HEADS], d[:, 768:1024], d[:, OFF_CKV * LANES:m],
                            d[:, m + KR_LANE:m + KR_LANE + ROPE_DIM], d[:, 1024:OFF_CKV * LANES]], axis=1)


def _pack_w_q(w):
    return jnp.pad(w.reshape(Q_RANK, N_HEADS, HEAD + ROPE_DIM), ((0, 0), (0, 0), (0, LANES - HEAD - ROPE_DIM))).reshape(Q_RANK, 4 * LANES)


def _unpack_dw_q(d):
    return d.reshape(Q_RANK, N_HEADS, LANES)[:, :, :HEAD + ROPE_DIM].reshape(Q_RANK, N_HEADS * (HEAD + ROPE_DIM))


def _pack_w_kv(w):
    w4 = w.reshape(KV_RANK, N_HEADS, 2 * HEAD)
    widen = lambda a: jnp.pad(a, ((0, 0), (0, 0), (0, LANES - HEAD))).reshape(KV_RANK, N_HEADS * LANES)
    return widen(w4[:, :, :HEAD]), widen(w4[:, :, HEAD:])


def _unpack_dw_kv(dk, dv):
    narrow = lambda a: a.reshape(KV_RANK, N_HEADS, LANES)[:, :, :HEAD]
    return jnp.concatenate([narrow(dk), narrow(dv)], axis=2).reshape(KV_RANK, 2 * N_HEADS * HEAD)


def _narrow_heads(a):
    return a.reshape(a.shape[0], N_HEADS, LANES)[:, :, :HEAD].reshape(a.shape[0], N_HEADS * HEAD)


def _widen_heads(a):
    return jnp.pad(a.reshape(a.shape[0], N_HEADS, HEAD), ((0, 0), (0, 0), (0, LANES - HEAD))).reshape(a.shape[0], N_HEADS * LANES)


def _layer_fwd(x, lw, tabs, tag, side=None, fox_side=None, late_weights=None):
    cos_m, sin_m, cos_r, sin_r = tabs
    h1 = _norm_fwd(x, lw["g_mix_pre"], name=tag + "pre_norm")
    proj, projb = _matmul(h1, lw["w_in"], name=tag + "in_proj", also_bf16=True)
    bias_row = jnp.pad(lw["b_forget"], (FF_LANE, LANES - N_HEADS - FF_LANE)).reshape(1, LANES)
    cum_col, cum_row = _fox_cum(proj, bias_row)
    oa, lse_a, *fox_carried = _mixer_fwd("fox", projb, OFF_FQ, projb, OFF_FK, projb, OFF_FV, cum_col=cum_col, cum_row=cum_row,
                                         side=fox_side)
    if late_weights is not None:
        lw = {**lw, **late_weights(fox_carried[0])}
    qm, km, vm, cqn, ckvn = _mla_prep(proj, cos_m, sin_m, lw["g_q_lora"], lw["g_kv_lora"], lw["wq"], lw["wk"], lw["wv"])
    ob_wide, lse_b = _mixer_fwd("mla", qm, 0, km, 0, vm, 0)
    ob = _narrow_heads(ob_wide)
    qr, kr = _ret_prep(proj, cos_r, sin_r)
    oc, ret_states = _ret_fwd(qr, kr, projb, OFF_RV)
    od, tot_d, *carried = _mixer_fwd("sb", projb, OFF_SQ, projb, OFF_SK, projb, OFF_SV, side=side)
    mixed = _mix_post(oa, ob, oc, od, proj, lw["g_mix_out"])
    mix = _matmul(mixed, lw["w_out"], name=tag + "out_proj")
    x1 = _norm_fwd(mix, lw["g_mix_post"], name=tag + "mix_post_norm", resid=x, out_dtype=F32)
    h2 = _norm_fwd(x1, lw["g_ffn_pre"], name=tag + "ffn_pre_norm")
    u = _matmul(h2, lw["w_ffn_up"], name=tag + "ffn_up", relu2=True, out_dtype=BF16, col_blocks=True)
    f = _matmul(u, lw["w_ffn_down"], name=tag + "ffn_down")
    x2 = _norm_fwd(f, lw["g_ffn_post"], name=tag + "ffn_post_norm", resid=x1, out_dtype=F32)
    saved = dict(x=x, h1=h1, proj=proj, projb=projb, bias_row=bias_row, cum_col=cum_col, cum_row=cum_row, oa=oa, lse_a=lse_a,
                 qm=qm, km=km, vm=vm, cqn=cqn, ckvn=ckvn, ob=ob, ob_wide=ob_wide, lse_b=lse_b, qr=qr, kr=kr, ret_states=ret_states, oc=oc, od=od, tot_d=tot_d, mixed=mixed,
                 mix=mix, x1=x1, h2=h2, u=u, f=f)
    return x2, saved, lw, (carried[0] if carried else None)


def _layer_bwd(dx2, lw, sv, tabs, tag, side=None, ffn_side=None, fox_side=None):
    cos_m, sin_m, cos_r, sin_r = tabs
    g = {}
    df, g["g_ffn_post"] = _norm_bwd(sv["f"], lw["g_ffn_post"], dx2, name=tag + "ffn_post_norm_bwd", out_dtype=BF16)
    du_pre = _matmul(df, lw["w_ffn_down"], name=tag + "ffn_down_dx", tb=True, out_dtype=BF16, relu2_of=sv["u"], side=ffn_side)
    ffn_carried = None
    if ffn_side is not None:
        du_pre, ffn_carried = du_pre
    g["w_ffn_down"] = _matmul(sv["u"], df, name=tag + "ffn_down_dw", ta=True)
    dh2 = _matmul(du_pre, lw["w_ffn_up"], name=tag + "ffn_up_dx", tb=True, col_blocks=True)
    g["w_ffn_up"] = _matmul(sv["h2"], du_pre, name=tag + "ffn_up_dw", ta=True, col_blocks=True)
    dx1, g["g_ffn_pre"] = _norm_bwd(sv["x1"], lw["g_ffn_pre"], dh2, name=tag + "ffn_pre_norm_bwd", add=dx2)
    dmix, g["g_mix_post"] = _norm_bwd(sv["mix"], lw["g_mix_post"], dx1, name=tag + "mix_post_norm_bwd", out_dtype=BF16)
    dmixed = _matmul(dmix, lw["w_out"], name=tag + "out_proj_dx", tb=True)
    g["w_out"] = _matmul(sv["mixed"], dmix, name=tag + "out_proj_dw", ta=True)
    proj, projb = sv["proj"], sv["projb"]
    doa, dob, doc, dod, drg, g["g_mix_out"] = _mix_post_bwd(dmixed, sv["oa"], sv["ob"], sv["oc"], sv["od"], proj, lw["g_mix_out"])
    dfq, dfk, dfv, dck, drs, *fox_carried = _mixer_bwd(
        "fox", projb, OFF_FQ, projb, OFF_FK, projb, OFF_FV, sv["oa"], doa, stat=sv["lse_a"], cum_col=sv["cum_col"],
        cum_row=sv["cum_row"], side=None if fox_side is None else fox_side(g))
    dqm, dkm, dvm = _mixer_bwd("mla", sv["qm"], 0, sv["km"], 0, sv["vm"], 0, sv["ob_wide"], _widen_heads(dob), stat=sv["lse_b"])
    dcq, dckv, dkr, dwq, dwk, dwv, g["g_q_lora"], g["g_kv_lora"] = _mla_prep_bwd(
        dqm, dkm, dvm, proj, sv["cqn"], sv["ckvn"], cos_m, sin_m, lw["g_q_lora"], lw["g_kv_lora"], lw["wq"], lw["wk"], lw["wv"])
    dqr, dkr_ret, drv = _ret_bwd(sv["qr"], sv["kr"], projb, OFF_RV, sv["ret_states"], doc)
    drq, drk = _ret_prep_bwd(dqr, dkr_ret, cos_r, sin_r)
    if callable(side):
        side = side(g, ffn_carried, fox_carried[0] if fox_carried else None)
    dsq, dsk, dsv, *carried = _mixer_bwd("sb", projb, OFF_SQ, projb, OFF_SK, projb, OFF_SV, sv["od"], dod, stat=sv["tot_d"], side=side)
    dmisc, db_row = _fox_gate_bwd(dck, drs, proj, sv["bias_row"], dkr)
    b = lambda a: a.astype(BF16)
    dproj = jnp.concatenate([b(dfq), b(dfk), b(dfv), dcq, drq, drk, b(drv), drg, b(dsq), b(dsk), b(dsv), dckv, dmisc], axis=1)
    dh1 = _matmul(dproj, lw["w_in"], name=tag + "in_proj_dx", tb=True)
    g["w_in"] = _matmul(sv["h1"], dproj, name=tag + "in_proj_dw", ta=True)
    dx, g["g_mix_pre"] = _norm_bwd(sv["x"], lw["g_mix_pre"], dh1, name=tag + "pre_norm_bwd", add=dx1)
    g["b_forget"] = db_row[0, FF_LANE:FF_LANE + N_HEADS]
    g["wq"], g["wk"], g["wv"] = dwq, dwk, dwv
    return dx, g, (carried[0] if carried else None)


def _local_step(x, positions, layers, target):
    s = x.shape[0]
    tabs = _rope_tables(positions.reshape(s, 1))
    saved = []
    for li, lw in enumerate(layers):
        x, sv, _, _ = _layer_fwd(x, lw, tabs, "l%d_" % li)
        saved.append(sv)
    loss_row, dx = _loss_head(x, target)
    grads = [None] * len(layers)
    for li in reversed(range(len(layers))):
        dx, grads[li], _ = _layer_bwd(dx, layers[li], saved[li], tabs, "l%d_" % li)
    return loss_row[0, 0], dx, grads


def _adamw(w, g, m, v, *, name):
    r, c = w.shape
    tr = 256 if r % 256 == 0 else r
    blk = pl.BlockSpec((tr, c), lambda i: (i, 0))
    c1 = 1.0 - ADAM_B1 ** ADAM_STEP
    c2 = 1.0 - ADAM_B2 ** ADAM_STEP

    def body(w_ref, g_ref, m_ref, v_ref, d_ref, mo_ref, vo_ref):
        gv = g_ref[...]
        mn = ADAM_B1 * m_ref[...] + (1.0 - ADAM_B1) * gv
        vn = ADAM_B2 * v_ref[...] + (1.0 - ADAM_B2) * jnp.square(gv)
        mo_ref[...] = mn
        vo_ref[...] = vn
        d_ref[...] = -ADAM_LR * ((mn / c1) / (jnp.sqrt(vn / c2) + ADAM_EPS) + ADAM_WD * w_ref[...])

    return pl.pallas_call(
        body, name=name, grid=(r // tr,), in_specs=[blk] * 4, out_specs=[blk] * 3,
        out_shape=[jax.ShapeDtypeStruct((r, c), F32)] * 3, compiler_params=_cparams(("parallel",)),
    )(w, g, m, v)


BIG = ("w_in", "w_q_up", "w_kv_up", "w_out", "w_ffn_up", "w_ffn_down")
SMALL = ("g_mix_pre", "b_forget", "g_q_lora", "g_kv_lora", "g_mix_out", "g_mix_post", "g_ffn_pre", "g_ffn_post")
N_CHIPS = 4
ANY = pl.BlockSpec(memory_space=pl.ANY)


def _mesh_pos():
    return lax.axis_index("x"), lax.axis_index("y"), lax.axis_index("c")


def _other_chips(x, y):
    return [(1 - x, y), (x, 1 - y), (1 - x, 1 - y)]


def _rows_half(ref, half):
    h = ref.shape[-2] // 2
    return ref.at[(slice(None),) * (len(ref.shape) - 2) + (pl.ds(half * h, h), slice(None))]


def _remote(src, dst, send_sem, recv_sem, device):
    return pltpu.make_async_remote_copy(src_ref=src, dst_ref=dst, send_sem=send_sem, recv_sem=recv_sem, device_id=device,
                                        device_id_type=MESH)


def _comm_call(body, name, args, out_shape, n_sems):
    return pl.pallas_call(
        body, name=name, in_specs=[ANY] * len(args), out_specs=[ANY] * len(out_shape), out_shape=out_shape,
        scratch_shapes=[pltpu.SemaphoreType.DMA((n_sems,)), pltpu.SemaphoreType.DMA((n_sems,))],
        compiler_params=pltpu.CompilerParams(has_side_effects=True),
    )(*args)


def _run_side_job(side, name):
    si = len(side.inputs)

    def body(*refs):
        args = (refs[:si], refs[si:-2], refs[-2], refs[-1])
        sends = side.sends(*args)
        for cp in sends:
            cp.start()
        for cp in side.recvs(*args):
            cp.wait_recv()
        for cp in sends:
            cp.wait_send()

    return _comm_call(body, name, side.inputs, side.out_shape, side.n_sems)


def _gather_job(shards):
    n = len(shards)

    def copies(own_block, ins, outs, send_sems, recv_sems):
        x, y, c = _mesh_pos()
        return [_remote(_rows_half(ins[t], c), _rows_half(outs[t].at[2 * x + y if own_block else 2 * px + py], c),
                        send_sems.at[3 * t + j], recv_sems.at[3 * t + j], (px, py, c))
                for t in range(n) for j, (px, py) in enumerate(_other_chips(x, y))]

    return _SideJob(shards, [jax.ShapeDtypeStruct((N_CHIPS,) + a.shape, a.dtype) for a in shards], 3 * n,
                    functools.partial(copies, True), functools.partial(copies, False))


def _forward_halves(gathered):
    n = len(gathered)

    def body(*refs):
        bufs, send_sems, recv_sems = refs[n:2 * n], refs[-2], refs[-1]
        x, y, c = _mesh_pos()

        def d2d(t, j, block, half):
            region = _rows_half(bufs[t].at[block], half)
            return _remote(region, region, send_sems.at[3 * t + j], recv_sems.at[3 * t + j], (x, y, 1 - c))

        peers = list(enumerate(_other_chips(x, y)))
        sends = [d2d(t, j, 2 * px + py, c) for t in range(n) for j, (px, py) in peers]
        for cp in sends:
            cp.start()
        for t in range(n):
            for j, (px, py) in peers:
                d2d(t, j, 2 * px + py, 1 - c).wait_recv()
        for cp in sends:
            cp.wait_send()

    return pl.pallas_call(
        body, name="gather_forward", in_specs=[ANY] * n, out_specs=[ANY] * n,
        out_shape=[jax.ShapeDtypeStruct(g.shape, g.dtype) for g in gathered], input_output_aliases={t: t for t in range(n)},
        scratch_shapes=[pltpu.SemaphoreType.DMA((3 * n,)), pltpu.SemaphoreType.DMA((3 * n,))],
        compiler_params=pltpu.CompilerParams(has_side_effects=True),
    )(*gathered)


def _exchange_halves_job(gs):
    n = len(gs)

    def copies(ins, outs, send_sems, recv_sems):
        x, y, c = _mesh_pos()
        return [_remote(_rows_half(ins[t], 1 - c), outs[t], send_sems.at[t], recv_sems.at[t], (x, y, 1 - c)) for t in range(n)]

    out_shape = [jax.ShapeDtypeStruct(g.shape[:2] + (g.shape[2] // 2, g.shape[3]), g.dtype) for g in gs]
    return _SideJob(gs, out_shape, n, copies, copies)


def _pair_add(g, r, c_idx, *, name):
    nb, d, rows, cols = g.shape
    h = rows // 2
    tr = min(h, 512)
    nt = h // tr

    def body(c_ref, g_ref, r_ref, p_ref, pb_ref):
        s = g_ref[...] + r_ref[...]
        p_ref[...] = s
        pb_ref[...] = s.astype(BF16)

    blk = pl.BlockSpec((1, 1, tr, cols), lambda k, l, i, c_ref: (k, l, i, 0))
    return pl.pallas_call(
        body, name=name,
        grid_spec=pltpu.PrefetchScalarGridSpec(
            num_scalar_prefetch=1, grid=(nb, d, nt),
            in_specs=[pl.BlockSpec((1, 1, tr, cols), lambda k, l, i, c_ref: (k, l, c_ref[0] * nt + i, 0)), blk],
            out_specs=[blk, blk]),
        out_shape=[jax.ShapeDtypeStruct((nb, d, h, cols), F32), jax.ShapeDtypeStruct((nb, d, h, cols), BF16)],
        compiler_params=_cparams(("parallel", "parallel", "parallel")),
    )(c_idx, g, r)


def _exchange_chips_job(pbs):
    n = len(pbs)

    def copies(ins, outs, send_sems, recv_sems):
        x, y, c = _mesh_pos()
        return [_remote(ins[t].at[2 * px + py], outs[t].at[j], send_sems.at[3 * t + j], recv_sems.at[3 * t + j], (px, py, c))
                for t in range(n) for j, (px, py) in enumerate(_other_chips(x, y))]

    return _SideJob(pbs, [jax.ShapeDtypeStruct((3,) + p.shape[1:], p.dtype) for p in pbs], 3 * n, copies, copies)


def _chip_add(p, r, k_idx, *, name):
    _, d, h, cols = p.shape
    tr = min(h, 512)
    nt = h // tr

    def body(k_ref, p_ref, r_ref, o_ref):
        o_ref[0] = ((p_ref[0, 0] + r_ref[0, 0].astype(F32)) + r_ref[1, 0].astype(F32)) + r_ref[2, 0].astype(F32)

    return pl.pallas_call(
        body, name=name,
        grid_spec=pltpu.PrefetchScalarGridSpec(
            num_scalar_prefetch=1, grid=(d, nt),
            in_specs=[pl.BlockSpec((1, 1, tr, cols), lambda l, i, k_ref: (k_ref[0], l, i, 0)),
                      pl.BlockSpec((3, 1, tr, cols), lambda l, i, k_ref: (0, l, i, 0))],
            out_specs=pl.BlockSpec((1, tr, cols), lambda l, i, k_ref: (l, i, 0))),
        out_shape=jax.ShapeDtypeStruct((d, h, cols), F32), compiler_params=_cparams(("parallel", "parallel")),
    )(k_idx, p, r)


def _share_halves(qs):
    n = len(qs)

    def body(*refs):
        ins, outs, send_sems, recv_sems = refs[:n], refs[n:2 * n], refs[2 * n], refs[2 * n + 1]
        x, y, c = _mesh_pos()
        cps = [_remote(ins[t], outs[t], send_sems.at[t], recv_sems.at[t], (x, y, 1 - c)) for t in range(n)]
        for cp in cps:
            cp.start()
        for cp in cps:
            cp.wait_recv()
        for cp in cps:
            cp.wait_send()

    return _comm_call(body, "grad_pair_share", qs, [jax.ShapeDtypeStruct(q.shape, q.dtype) for q in qs], n)


def _all_reduce_small(v):
    r, cols = v.shape
    n_dev = 8

    def body(v_ref, o_ref, buf, send_sems, recv_sems):
        x, y, c = _mesh_pos()
        me = 4 * x + 2 * y + c
        buf[me] = v_ref[...]

        def peer(j):
            return (1 - x if j & 4 else x, 1 - y if j & 2 else y, 1 - c if j & 1 else c)

        def copy(j, slot):
            return pltpu.make_async_remote_copy(src_ref=v_ref, dst_ref=buf.at[slot], send_sem=send_sems.at[j - 1],
                                                recv_sem=recv_sems.at[j - 1], device_id=peer(j), device_id_type=MESH)

        sends = [copy(j, me) for j in range(1, n_dev)]
        for cp in sends:
            cp.start()
        for j in range(1, n_dev):
            px, py, pc = peer(j)
            copy(j, 4 * px + 2 * py + pc).wait_recv()
        for cp in sends:
            cp.wait_send()
        acc = buf[0]
        for d in range(1, n_dev):
            acc = acc + buf[d]
        o_ref[...] = acc

    vm = pl.BlockSpec(memory_space=pltpu.VMEM)
    return pl.pallas_call(
        body, name="small_all_reduce", in_specs=[vm], out_specs=vm, out_shape=jax.ShapeDtypeStruct((r, cols), F32),
        scratch_shapes=[pltpu.VMEM((n_dev, r, cols), F32), pltpu.SemaphoreType.DMA((n_dev - 1,)), pltpu.SemaphoreType.DMA((n_dev - 1,))],
        compiler_params=pltpu.CompilerParams(has_side_effects=True),
    )(v)


_COL_SHARDED = ("w_in", "w_q_up", "w_kv_up", "w_ffn_up")


def _shard_cols(blocks, a, b):
    c = blocks[0].shape[-1]
    out = []
    while a < b:
        k = a // c
        hi = min(b, (k + 1) * c)
        out.append(blocks[k][:, a - k * c:hi - k * c])
        a = hi
    return out


def _pack_w_in_shards(blocks):
    z = lambda n: [jnp.zeros((blocks[0].shape[0], n), blocks[0].dtype)]
    cols = lambda a, b: _shard_cols(blocks, a, b)
    return jnp.concatenate(cols(0, 768) + cols(772, 1028) + cols(1188, 2980) + cols(1028, 1156) + cols(768, 772)
                           + z(KR_LANE - N_HEADS) + cols(1156, 1188) + z(LANES - KR_LANE - ROPE_DIM), axis=1)


def _whole_layer(name, blocks):
    if name in _COL_SHARDED:
        return jnp.concatenate([blocks[k] for k in range(N_CHIPS)], axis=1)
    return blocks.reshape(N_CHIPS * blocks.shape[1], blocks.shape[2])


def _split_layer(name, whole):
    if name in _COL_SHARDED:
        c = whole.shape[1] // N_CHIPS
        return jnp.stack([whole[:, k * c:(k + 1) * c] for k in range(N_CHIPS)])
    return whole.reshape(N_CHIPS, whole.shape[0] // N_CHIPS, whole.shape[1])


def _small_to_rows(d):
    v = jnp.concatenate([d[k].astype(F32).reshape(-1) for k in SMALL])
    rows = -(-v.shape[0] // (8 * LANES)) * 8
    return jnp.pad(v, (0, rows * LANES - v.shape[0])).reshape(rows, LANES)


def _small_from_rows(rows, shapes):
    v = rows.reshape(-1)
    out, o = {}, 0
    for k in SMALL:
        sz = int(np.prod(shapes[k]))
        out[k] = v[o:o + sz].reshape(shapes[k])
        o += sz
    return out


_ARG_NAMES = ("x", "positions", "g_mix_pre", "w_in", "b_forget", "g_q_lora", "w_q_up", "g_kv_lora", "w_kv_up", "g_mix_out", "w_out",
              "g_mix_post", "g_ffn_pre", "w_ffn_up", "w_ffn_down", "g_ffn_post")
_WEIGHTS = _ARG_NAMES[2:]


def kernel(x, positions, g_mix_pre, w_in, b_forget, g_q_lora, w_q_up, g_kv_lora, w_kv_up, g_mix_out, w_out, g_mix_post, g_ffn_pre, w_ffn_up, w_ffn_down, g_ffn_post, loss_target, m_g_mix_pre, m_w_in, m_b_forget, m_g_q_lora, m_w_q_up, m_g_kv_lora, m_w_kv_up, m_g_mix_out, m_w_out, m_g_mix_post, m_g_ffn_pre, m_w_ffn_up, m_w_ffn_down, m_g_ffn_post, v_g_mix_pre, v_w_in, v_b_forget, v_g_q_lora, v_w_q_up, v_g_kv_lora, v_w_kv_up, v_g_mix_out, v_w_out, v_g_mix_post, v_g_ffn_pre, v_w_ffn_up, v_w_ffn_down, v_g_ffn_post):
    w = dict(g_mix_pre=g_mix_pre, w_in=w_in, b_forget=b_forget, g_q_lora=g_q_lora, w_q_up=w_q_up, g_kv_lora=g_kv_lora, w_kv_up=w_kv_up,
             g_mix_out=g_mix_out, w_out=w_out, g_mix_post=g_mix_post, g_ffn_pre=g_ffn_pre, w_ffn_up=w_ffn_up, w_ffn_down=w_ffn_down,
             g_ffn_post=g_ffn_post)
    m = dict(g_mix_pre=m_g_mix_pre, w_in=m_w_in, b_forget=m_b_forget, g_q_lora=m_g_q_lora, w_q_up=m_w_q_up, g_kv_lora=m_g_kv_lora,
             w_kv_up=m_w_kv_up, g_mix_out=m_g_mix_out, w_out=m_w_out, g_mix_post=m_g_mix_post, g_ffn_pre=m_g_ffn_pre,
             w_ffn_up=m_w_ffn_up, w_ffn_down=m_w_ffn_down, g_ffn_post=m_g_ffn_post)
    v = dict(g_mix_pre=v_g_mix_pre, w_in=v_w_in, b_forget=v_b_forget, g_q_lora=v_g_q_lora, w_q_up=v_w_q_up, g_kv_lora=v_g_kv_lora,
             w_kv_up=v_w_kv_up, g_mix_out=v_g_mix_out, w_out=v_w_out, g_mix_post=v_g_mix_post, g_ffn_pre=v_g_ffn_pre,
             w_ffn_up=v_w_ffn_up, w_ffn_down=v_w_ffn_down, g_ffn_post=v_g_ffn_post)
    shard_shapes = {k: w[k].shape for k in BIG}
    small_shapes = {k: w[k].shape for k in SMALL}
    c_idx = lax.axis_index("c").astype(jnp.int32).reshape(1)
    k_idx = (2 * lax.axis_index("x") + lax.axis_index("y")).astype(jnp.int32).reshape(1)
    first_core = lax.axis_index("c") == 0

    mine = 2 * lax.axis_index("x") + lax.axis_index("y")
    shards_b = [{k: w[k][l:l + 1].astype(BF16) for k in BIG} for l in range(DEPTH)]
    gains = [dict(g_mix_pre=g_mix_pre[l], b_forget=b_forget[l], g_q_lora=g_q_lora[l], g_kv_lora=g_kv_lora[l], g_mix_out=g_mix_out[l],
                  g_mix_post=g_mix_post[l], g_ffn_pre=g_ffn_pre[l], g_ffn_post=g_ffn_post[l]) for l in range(DEPTH)]
    FIRST, LATER = ("w_in", "w_q_up", "w_kv_up"), ("w_out", "w_ffn_up", "w_ffn_down")
    EARLY_GRADS, LATE_GRADS = ("w_ffn_down", "w_ffn_up", "w_out"), ("w_in", "w_q_up", "w_kv_up")

    def gather_job(l, names):
        return _gather_job([shards_b[l][k] for k in names])

    def weights_of(l, names, gathered):
        four = {k: lax.dynamic_update_slice(g, shards_b[l][k][None], (mine, 0, 0, 0))[:, 0]
                for k, g in zip(names, _forward_halves(gathered))}
        out = {}
        for k in names:
            if k == "w_in":
                out["w_in"] = _pack_w_in_shards(four[k])
            elif k == "w_q_up":
                out["wq"] = _pack_w_q(_whole_layer(k, four[k]))
            elif k == "w_kv_up":
                out["wk"], out["wv"] = _pack_w_kv(_whole_layer(k, four[k]))
            elif k == "w_ffn_up":
                out[k] = four[k]
            else:
                out[k] = _whole_layer(k, four[k])
        return out

    def grad_blocks(names, g):
        whole = dict(w_in=lambda: _unpack_dw_in(g["w_in"]), w_q_up=lambda: _unpack_dw_q(g["wq"]),
                     w_kv_up=lambda: _unpack_dw_kv(g["wk"], g["wv"]), w_out=lambda: g["w_out"], w_ffn_down=lambda: g["w_ffn_down"])
        return [(g[k] if k == "w_ffn_up" else _split_layer(k, whole[k]()))[:, None] for k in names]

    def pair_sums(names, blocks, theirs):
        return [_pair_add(b, r, c_idx, name="grad_pair_add_" + k) for k, b, r in zip(names, blocks, theirs)]

    def exchange_job(*pairs):
        return _exchange_chips_job([pb for pair in pairs for (_, pb) in pair])

    def finish_grads(names, pair, partial):
        half = [_chip_add(p, r, k_idx, name="grad_chip_add_" + k) for k, (p, _), r in zip(names, pair, partial)]
        return {k: jnp.where(first_core, jnp.concatenate([q, s], axis=1), jnp.concatenate([s, q], axis=1))
                for k, q, s in zip(names, half, _share_halves(half))}

    seq = x.shape[1]
    tabs = _rope_tables(positions[0].reshape(seq, 1))
    first0 = weights_of(0, FIRST, _run_side_job(gather_job(0, FIRST), "gather_weights_l0"))
    x1, saved0, lw0, gathered1 = _layer_fwd(x[0], {**gains[0], **first0}, tabs, "l0_", fox_side=gather_job(0, LATER),
                                            late_weights=lambda got: weights_of(0, LATER, got), side=gather_job(1, BIG))
    lw1 = {**gains[1], **weights_of(1, BIG, gathered1)}
    x2, saved1, _, _ = _layer_fwd(x1, lw1, tabs, "l1_")
    loss_row, dx = _loss_head(x2, loss_target[0])
    loss = lax.psum(loss_row[0, 0], ("x", "y", "c"))
    dx, grads1, _ = _layer_bwd(dx, lw1, saved1, tabs, "l1_")
    blocks1 = grad_blocks(BIG, grads1)
    early_blocks0, pair1, early0 = [], [], []

    def beside_l0_fox_backward(g):
        early_blocks0.extend(grad_blocks(EARLY_GRADS, g))
        return _exchange_halves_job(early_blocks0)

    def beside_l0_sb_backward(g, theirs1, theirs_early0):
        pair1.extend(pair_sums(BIG, blocks1, theirs1))
        early0.extend(pair_sums(EARLY_GRADS, early_blocks0, theirs_early0))
        return exchange_job(pair1, early0)

    dx, grads0, partial = _layer_bwd(dx, lw0, saved0, tabs, "l0_", ffn_side=_exchange_halves_job(blocks1),
                                     fox_side=beside_l0_fox_backward, side=beside_l0_sb_backward)
    big1 = finish_grads(BIG, pair1, partial[:len(BIG)])
    big0 = finish_grads(EARLY_GRADS, early0, partial[len(BIG):])
    late_blocks0 = grad_blocks(LATE_GRADS, grads0)
    late0 = pair_sums(LATE_GRADS, late_blocks0, _run_side_job(_exchange_halves_job(late_blocks0), "grad_pair_exchange_l0"))
    big0.update(finish_grads(LATE_GRADS, late0, _run_side_job(exchange_job(late0), "grad_chip_exchange_l0")))
    g_big = {k: jnp.concatenate([big0[k], big1[k]], axis=0) for k in BIG}
    grads = [grads0, grads1]

    g_small_local = {k: jnp.stack([grads[l][k].reshape(small_shapes[k][1:]) for l in range(DEPTH)]) for k in SMALL}
    g_small = _small_from_rows(_all_reduce_small(_small_to_rows(g_small_local)), small_shapes)

    g_all = {**g_big, **g_small}
    delta, new_m, new_v = {}, {}, {}
    for k in BIG:
        d, r, c = shard_shapes[k]
        two_d = lambda a: a.reshape(d * r, c)
        dk, mk, vk = _adamw(two_d(w[k]), two_d(g_all[k]), two_d(m[k]), two_d(v[k]), name="adamw_" + k)
        delta[k], new_m[k], new_v[k] = dk.reshape(d, r, c), mk.reshape(d, r, c), vk.reshape(d, r, c)
    ds, ms, vs = _adamw(_small_to_rows(w), _small_to_rows(g_small), _small_to_rows(m), _small_to_rows(v), name="adamw_small")
    delta.update(_small_from_rows(ds, small_shapes))
    new_m.update(_small_from_rows(ms, small_shapes))
    new_v.update(_small_from_rows(vs, small_shapes))

    grad_x = dx.reshape(x.shape)
    return (loss, grad_x, *[g_all[k] for k in _WEIGHTS], *[delta[k] for k in _WEIGHTS], *[new_m[k] for k in _WEIGHTS],
            *[new_v[k] for k in _WEIGHTS])
```

```python
import functools
import math

import numpy as np
import jax
import jax.numpy as jnp
from jax import lax
from jax.experimental import pallas as pl
from jax.experimental.pallas import tpu as pltpu

F32 = jnp.float32
BF16 = jnp.bfloat16
MESH = pl.DeviceIdType.MESH

D_MODEL = 1024
DEPTH = 2
CHUNK = 64
GROUP = 256
HEAD = 64
N_HEADS = 4
Q_RANK = 256
KV_RANK = 128
ROPE_DIM = 32
D_FF = 4096
D_IN = 2980
D_INP = 3072
ROPE_BASE = 10000.0
EPS = 1e-6
LANES = 128
TQ = 128
NEG = -1e30

ADAM_LR, ADAM_B1, ADAM_B2, ADAM_EPS, ADAM_WD, ADAM_STEP = 0.001, 0.9, 0.999, 1e-08, 0.01, 10

OFF_FQ, OFF_FK, OFF_FV, OFF_CQ = 0, 2, 4, 6
OFF_RQ, OFF_RK, OFF_RV, OFF_RG = 8, 10, 12, 14
OFF_SQ, OFF_SK, OFF_SV = 16, 18, 20
OFF_CKV, OFF_MISC = 22, 23
FF_LANE, KR_LANE = 0, 64

VMEM_LIMIT = 56 * 1024 * 1024


def _tile(dim, pref):
    return pref if dim % pref == 0 else dim


def _cparams(sem, vmem=None):
    return pltpu.CompilerParams(dimension_semantics=sem, vmem_limit_bytes=vmem or VMEM_LIMIT)


def _dot(a, b):
    return jnp.dot(a, b, preferred_element_type=F32)


def _dot_nt(a, b):
    return lax.dot_general(a, b, (((1,), (1,)), ((), ())), preferred_element_type=F32)


def _dot_tn(a, b):
    return lax.dot_general(a, b, (((0,), (0,)), ((), ())), preferred_element_type=F32)


def _dot_exact(a, b):
    return jnp.dot(a, b, precision=lax.Precision.HIGHEST, preferred_element_type=F32)


def _matmul(a, b, *, name, ta=False, tb=False, out_dtype=F32, tm=1024, tn=1024, tk=1024,
            relu2=False, relu2_of=None, also_bf16=False, side=None, col_blocks=False):
    if ta:
        kdim, m = a.shape
    else:
        m, kdim = a.shape
    if col_blocks and not ta:
        n = b.shape[1] if tb else b.shape[0] * b.shape[2]
        if tb:
            kdim = b.shape[0] * b.shape[2]
    else:
        n = b.shape[0] if tb else b.shape[1]
    tm, tn, tk = _tile(m, tm), _tile(n, tn), _tile(kdim, tk)
    nk = kdim // tk
    a_spec = pl.BlockSpec((tk, tm), lambda i, j, k: (k, i)) if ta else pl.BlockSpec((tm, tk), lambda i, j, k: (i, k))
    b_spec = pl.BlockSpec((tn, tk), lambda i, j, k: (j, k)) if tb else pl.BlockSpec((tk, tn), lambda i, j, k: (k, j))
    o_spec = pl.BlockSpec((tm, tn), lambda i, j, k: (i, j))
    if col_blocks and ta:
        o_spec = pl.BlockSpec((None, tm, tn), lambda i, j, k: (j, i, 0))
    elif col_blocks and tb:
        assert b.shape[2] == tk
        b_spec = pl.BlockSpec((None, tn, tk), lambda i, j, k: (k, j, 0))
    elif col_blocks:
        assert b.shape[2] == tn
        b_spec = pl.BlockSpec((None, tk, tn), lambda i, j, k: (j, k, 0))
    two = also_bf16

    def body(*refs):
        refs = list(refs)
        a_ref, b_ref = refs[0], refs[1]
        e_ref = refs[2] if relu2_of is not None else None
        pos = 3 if relu2_of is not None else 2
        o_ref = refs[pos]
        o2_ref = refs[pos + 1] if two else None
        acc_ref = refs[-1]
        k = pl.program_id(2)
        av = a_ref[...].astype(BF16)
        bv = b_ref[...].astype(BF16)
        if ta:
            part = _dot_tn(av, bv)
        elif tb:
            part = _dot_nt(av, bv)
        else:
            part = _dot(av, bv)

        @pl.when(k == 0)
        def _():
            acc_ref[...] = part

        @pl.when(k > 0)
        def _():
            acc_ref[...] += part

        @pl.when(k == nk - 1)
        def _():
            r = acc_ref[...]
            if relu2_of is not None:
                r = r * (2.0 * jnp.sqrt(e_ref[...].astype(F32)))
            if relu2:
                r = jnp.square(jnp.maximum(r, 0.0))
            o_ref[...] = r.astype(o_ref.dtype)
            if also_bf16:
                o2_ref[...] = r.astype(BF16)

    in_specs = [a_spec, b_spec]
    args = [a, b]
    if relu2_of is not None:
        in_specs.append(o_spec)
        args.append(relu2_of)
    out_shape = [jax.ShapeDtypeStruct((n // tn, m, tn) if (col_blocks and ta) else (m, n), out_dtype)]
    out_specs = [o_spec]
    if two:
        out_shape.append(jax.ShapeDtypeStruct((m, n), BF16))
        out_specs.append(o_spec)
    grid = (m // tm, n // tn, nk)
    side_in, side_out, side_scratch = _side_specs(side)
    res = pl.pallas_call(
        _carry_side_job(body, len(args), len(out_shape), side, grid), name=name, grid=grid,
        in_specs=in_specs + side_in, out_specs=out_specs + side_out,
        out_shape=out_shape + ([] if side is None else side.out_shape),
        scratch_shapes=[pltpu.VMEM((tm, tn), F32)] + side_scratch,
        compiler_params=_cparams(("parallel", "parallel", "arbitrary") if side is None else ("arbitrary",) * 3),
    )(*args, *([] if side is None else side.inputs))
    main = res[:len(out_shape)]
    main = main if two else main[0]
    return main if side is None else (main, res[len(out_shape):])


def _rms(x, g):
    r = lax.rsqrt(jnp.mean(x * x, axis=-1, keepdims=True) + EPS)
    return x * r * g


def _rms_bwd(x, g, dy):
    r = lax.rsqrt(jnp.mean(x * x, axis=-1, keepdims=True) + EPS)
    xh = x * r
    gdy = dy * g
    dx = r * (gdy - xh * jnp.mean(xh * gdy, axis=-1, keepdims=True))
    return dx, xh * dy


def _norm_fwd(x, g, *, name, resid=None, out_dtype=BF16, next_gain=None):
    s, d = x.shape
    tr = _tile(s, 256)
    row = pl.BlockSpec((tr, d), lambda i: (i, 0))
    gsp = pl.BlockSpec((1, d), lambda i: (0, 0))

    def body(*refs):
        refs = list(refs)
        x_ref, g_ref = refs[:2]
        y = _rms(x_ref[...], g_ref[...])
        pos = 2
        if resid is not None:
            y = refs[pos][...] + y
            pos += 1
        if next_gain is None:
            refs[pos][...] = y.astype(refs[pos].dtype)
        else:
            refs[pos + 1][...] = y.astype(refs[pos + 1].dtype)
            refs[pos + 2][...] = _rms(y, refs[pos][...]).astype(BF16)

    args = [x, g.reshape(1, d)] + ([] if resid is None else [resid]) + ([] if next_gain is None else [next_gain.reshape(1, d)])
    in_specs = [row, gsp] + ([] if resid is None else [row]) + ([] if next_gain is None else [gsp])
    first = jax.ShapeDtypeStruct((s, d), out_dtype)
    if next_gain is None:
        out_specs, out_shape = row, first
    else:
        out_specs, out_shape = [row, row], [first, jax.ShapeDtypeStruct((s, d), BF16)]
    return pl.pallas_call(
        body, name=name, grid=(s // tr,), in_specs=in_specs, out_specs=out_specs, out_shape=out_shape,
        compiler_params=_cparams(("parallel",)),
    )(*args)


def _norm_bwd(x, g, dy, *, name, add=None, out_dtype=F32, then=None):
    s, d = x.shape
    tr = _tile(s, 256)
    row = pl.BlockSpec((tr, d), lambda i: (i, 0))
    gsp = pl.BlockSpec((1, d), lambda i: (0, 0))
    n_in = 3 + (add is not None) + (2 if then is not None else 0)

    def body(*refs):
        ins, outs = refs[:n_in], refs[n_in:]
        x_ref, g_ref, dy_ref = ins[:3]
        dx, gterm = _rms_bwd(x_ref[...], g_ref[...], dy_ref[...].astype(F32))
        if add is not None:
            dx = dx + ins[3][...]
        outs[0][...] = dx.astype(outs[0].dtype)
        terms = [(outs[1], gterm)]
        if then is not None:
            dx2, gterm2 = _rms_bwd(ins[-2][...], ins[-1][...], dx)
            outs[2][...] = dx2.astype(BF16)
            terms.append((outs[3], gterm2))

        @pl.when(pl.program_id(0) == 0)
        def _():
            for dg_ref, _ in terms:
                dg_ref[...] = jnp.zeros_like(dg_ref)

        for dg_ref, term in terms:
            dg_ref[...] += jnp.sum(term, axis=0, keepdims=True)

    args = [x, g.reshape(1, d), dy] + ([] if add is None else [add]) + ([] if then is None else [then[0], then[1].reshape(1, d)])
    in_specs = [row, gsp, row] + ([] if add is None else [row]) + ([] if then is None else [row, gsp])
    out_specs = [row, gsp] + ([] if then is None else [row, gsp])
    out_shape = [jax.ShapeDtypeStruct((s, d), out_dtype), jax.ShapeDtypeStruct((1, d), F32)]
    if then is not None:
        out_shape += [jax.ShapeDtypeStruct((s, d), BF16), jax.ShapeDtypeStruct((1, d), F32)]
    return pl.pallas_call(
        body, name=name, grid=(s // tr,), in_specs=in_specs, out_specs=out_specs, out_shape=out_shape,
        compiler_params=_cparams(("arbitrary",)),
    )(*args)


def _loss_head(y, target):
    s, d = y.shape
    tr = _tile(s, 256)
    row = pl.BlockSpec((tr, d), lambda i: (i, 0))
    lsp = pl.BlockSpec((1, LANES), lambda i: (0, 0))

    def body(y_ref, t_ref, l_ref, dy_ref):
        e = y_ref[...] - t_ref[...]
        dy_ref[...] = e * (1.0 / d)

        @pl.when(pl.program_id(0) == 0)
        def _():
            l_ref[...] = jnp.zeros_like(l_ref)

        part = 0.5 * jnp.sum(jnp.mean(e * e, axis=-1, keepdims=True), axis=0, keepdims=True)
        l_ref[...] += jnp.broadcast_to(part, (1, LANES))

    return pl.pallas_call(
        body, name="loss_head", grid=(s // tr,), in_specs=[row, row], out_specs=[lsp, row],
        out_shape=[jax.ShapeDtypeStruct((1, LANES), F32), jax.ShapeDtypeStruct((s, d), F32)],
        compiler_params=_cparams(("arbitrary",)),
    )(y, target)


def _rope_tables(pos_col):
    s = pos_col.shape[0]
    tr = _tile(s, 512)
    f_mla = ROPE_BASE ** (-jnp.arange(ROPE_DIM // 2, dtype=F32) / (ROPE_DIM // 2))
    f_ret = ROPE_BASE ** (-jnp.arange(HEAD // 2, dtype=F32) / (HEAD // 2))
    fm = jnp.concatenate([jnp.zeros((64,), F32), f_mla, f_mla, jnp.zeros((32,), F32)]).reshape(1, LANES)
    fr = jnp.tile(jnp.concatenate([f_ret, f_ret]), 4).reshape(1, 2 * LANES)

    def body(p_ref, fm_ref, fr_ref, cm_ref, sm_ref, cr_ref, sr_ref):
        p = p_ref[...].astype(F32)
        am = p * fm_ref[...]
        ar = p * fr_ref[...]
        cm_ref[...] = jnp.cos(am)
        sm_ref[...] = jnp.sin(am)
        cr_ref[...] = jnp.cos(ar)
        sr_ref[...] = jnp.sin(ar)

    return pl.pallas_call(
        body, name="rope_tables", grid=(s // tr,),
        in_specs=[pl.BlockSpec((tr, 1), lambda i: (i, 0)), pl.BlockSpec((1, LANES), lambda i: (0, 0)),
                  pl.BlockSpec((1, 2 * LANES), lambda i: (0, 0))],
        out_specs=[pl.BlockSpec((tr, LANES), lambda i: (i, 0))] * 2 + [pl.BlockSpec((tr, 2 * LANES), lambda i: (i, 0))] * 2,
        out_shape=[jax.ShapeDtypeStruct((s, LANES), F32)] * 2 + [jax.ShapeDtypeStruct((s, 2 * LANES), F32)] * 2,
        compiler_params=_cparams(("parallel",)),
    )(pos_col, fm, fr)


def _lane(shape):
    return lax.broadcasted_iota(jnp.int32, shape, len(shape) - 1)


def _rot_mla(z):
    l = _lane(z.shape) % LANES
    n = z.shape[-1]
    return jnp.where(l < 80, -pltpu.roll(z, n - 16, 1), pltpu.roll(z, 16, 1))


def _rot_mla_t(y):
    l = _lane(y.shape) % LANES
    n = y.shape[-1]
    return jnp.where((l >= 64) & (l < 80), pltpu.roll(y, n - 16, 1),
                     jnp.where((l >= 80) & (l < 96), -pltpu.roll(y, 16, 1), 0.0))


def _rot_ret(z):
    l = _lane(z.shape) % HEAD
    n = z.shape[-1]
    return jnp.where(l < 32, -pltpu.roll(z, n - 32, 1), pltpu.roll(z, 32, 1))


def _rot_ret_t(y):
    l = _lane(y.shape) % HEAD
    n = y.shape[-1]
    return jnp.where(l < 32, pltpu.roll(y, n - 32, 1), -pltpu.roll(y, 32, 1))


def _log_sigmoid(x):
    return jnp.minimum(x, 0.0) - jnp.log1p(jnp.exp(-jnp.abs(x)))


def _fox_cum(proj, bias_row):
    s = proj.shape[0]
    nb = s // TQ

    def body(x_ref, b_ref, cc_ref, cr_ref, carry_ref):
        @pl.when(pl.program_id(0) == 0)
        def _():
            carry_ref[...] = jnp.zeros_like(carry_ref)

        ls = _log_sigmoid(x_ref[...] + b_ref[...])
        r = lax.broadcasted_iota(jnp.int32, (TQ, TQ), 0)
        c = lax.broadcasted_iota(jnp.int32, (TQ, TQ), 1)
        tri = (c <= r).astype(F32)
        cum = _dot_exact(tri, ls) + carry_ref[...]
        carry_ref[...] = cum[TQ - 1:TQ, :]
        cc_ref[...] = cum
        cr_ref[...] = cum.T[0:8, :]

    return pl.pallas_call(
        body, name="fox_cum", grid=(nb,),
        in_specs=[pl.BlockSpec((TQ, LANES), lambda i: (i, OFF_MISC)), pl.BlockSpec((1, LANES), lambda i: (0, 0))],
        out_specs=[pl.BlockSpec((TQ, LANES), lambda i: (i, 0)), pl.BlockSpec((8, TQ), lambda i: (0, i))],
        out_shape=[jax.ShapeDtypeStruct((s, LANES), F32), jax.ShapeDtypeStruct((8, s), F32)],
        scratch_shapes=[pltpu.VMEM((1, LANES), F32)],
        compiler_params=_cparams(("arbitrary",)),
    )(proj, bias_row)


def _fox_gate_bwd(dck, drs, proj, bias_row, dkr):
    s = proj.shape[0]
    nb = s // TQ

    def body(d_ref, r_ref, x_ref, b_ref, k_ref, o_ref, db_ref, carry_ref):
        @pl.when(pl.program_id(0) == 0)
        def _():
            carry_ref[...] = jnp.zeros_like(carry_ref)
            db_ref[...] = jnp.zeros_like(db_ref)

        rows = jnp.concatenate([d_ref[0], d_ref[1], jnp.zeros((TQ - 16, TQ), F32)], axis=0)
        t = rows.T
        l = _lane((TQ, LANES))
        r0, r1 = r_ref[0], r_ref[1]
        rsum = jnp.where(l == 0, r0[:, 0:1], jnp.where(l == 1, r0[:, HEAD:HEAD + 1],
                         jnp.where(l == 2, r1[:, 0:1], jnp.where(l == 3, r1[:, HEAD:HEAD + 1], 0.0))))
        dcum = rsum - jnp.where(l < 2, t, pltpu.roll(t, LANES - 6, 1))
        r = lax.broadcasted_iota(jnp.int32, (TQ, TQ), 0)
        c = lax.broadcasted_iota(jnp.int32, (TQ, TQ), 1)
        triu = (c >= r).astype(F32)
        rc = _dot_exact(triu, dcum) + carry_ref[...]
        carry_ref[...] = rc[0:1, :]
        f = x_ref[...] + b_ref[...]
        sig_neg = 1.0 / (1.0 + jnp.exp(f))
        df = jnp.where(l < N_HEADS, rc * sig_neg, 0.0)
        db_ref[...] += jnp.sum(df, axis=0, keepdims=True)
        o_ref[...] = (df + k_ref[...]).astype(o_ref.dtype)

    rev = lambda i: nb - 1 - i
    return pl.pallas_call(
        body, name="fox_gate_bwd", grid=(nb,),
        in_specs=[pl.BlockSpec((2, 8, TQ), lambda i: (0, 0, rev(i))), pl.BlockSpec((2, TQ, LANES), lambda i: (0, rev(i), 0)),
                  pl.BlockSpec((TQ, LANES), lambda i: (rev(i), OFF_MISC)),
                  pl.BlockSpec((1, LANES), lambda i: (0, 0)), pl.BlockSpec((TQ, LANES), lambda i: (rev(i), 0))],
        out_specs=[pl.BlockSpec((TQ, LANES), lambda i: (rev(i), 0)), pl.BlockSpec((1, LANES), lambda i: (0, 0))],
        out_shape=[jax.ShapeDtypeStruct((s, LANES), BF16), jax.ShapeDtypeStruct((1, LANES), F32)],
        scratch_shapes=[pltpu.VMEM((1, LANES), F32)],
        compiler_params=_cparams(("arbitrary",)),
    )(dck, drs, proj, bias_row, dkr)


def _mla_prep(proj, cos_m, sin_m, g_q, g_kv, wq, wk, wv):
    s = proj.shape[0]
    tr = _tile(s, 256)

    def body(cq_ref, ckv_ref, misc_ref, cos_ref, sin_ref, gq_ref, gkv_ref, wq_ref, wk_ref, wv_ref,
             q_ref, k_ref, v_ref, cqn_ref, ckvn_ref):
        cos4 = jnp.tile(cos_ref[...], (1, 4))
        sin4 = jnp.tile(sin_ref[...], (1, 4))
        cqn = _rms(cq_ref[...], gq_ref[...]).astype(BF16)
        ckvn = _rms(ckv_ref[...], gkv_ref[...]).astype(BF16)
        cqn_ref[...] = cqn
        ckvn_ref[...] = ckvn
        zq = _dot(cqn, wq_ref[...])
        q_ref[...] = (zq * cos4 + _rot_mla(zq) * sin4).astype(BF16)
        l = _lane((tr, LANES))
        kr = jnp.where((l >= KR_LANE) & (l < KR_LANE + ROPE_DIM), misc_ref[...], 0.0)
        zk = _dot(ckvn, wk_ref[...]) + jnp.tile(kr, (1, 4))
        k_ref[...] = (zk * cos4 + _rot_mla(zk) * sin4).astype(BF16)
        v_ref[...] = _dot(ckvn, wv_ref[...]).astype(BF16)

    full = lambda a: pl.BlockSpec(a.shape, lambda i: (0, 0))
    rowb = lambda w: pl.BlockSpec((tr, w), lambda i: (i, 0))
    gq2, gkv2 = g_q.reshape(1, Q_RANK), g_kv.reshape(1, KV_RANK)
    return pl.pallas_call(
        body, name="mla_prep", grid=(s // tr,),
        in_specs=[pl.BlockSpec((tr, 256), lambda i: (i, OFF_CQ // 2)), pl.BlockSpec((tr, LANES), lambda i: (i, OFF_CKV)),
                  pl.BlockSpec((tr, LANES), lambda i: (i, OFF_MISC)), rowb(LANES), rowb(LANES),
                  full(gq2), full(gkv2), full(wq), full(wk), full(wv)],
        out_specs=[rowb(512), rowb(512), rowb(512), rowb(256), rowb(128)],
        out_shape=[jax.ShapeDtypeStruct((s, 512), BF16), jax.ShapeDtypeStruct((s, 512), BF16), jax.ShapeDtypeStruct((s, 512), BF16),
                   jax.ShapeDtypeStruct((s, 256), BF16), jax.ShapeDtypeStruct((s, 128), BF16)],
        compiler_params=_cparams(("parallel",)),
    )(proj, proj, proj, cos_m, sin_m, gq2, gkv2, wq, wk, wv)


def _mla_prep_bwd(dq, dk, dv, proj, cqn, ckvn, cos_m, sin_m, g_q, g_kv, wq, wk, wv):
    s = proj.shape[0]
    tr = _tile(s, 256)

    def body(dq_ref, dk_ref, dv_ref, cq_ref, ckv_ref, cqn_ref, ckvn_ref, cos_ref, sin_ref, gq_ref, gkv_ref,
             wq_ref, wk_ref, wv_ref, dcq_ref, dckv_ref, dkr_ref, dwq_ref, dwk_ref, dwv_ref, dgq_ref, dgkv_ref):
        @pl.when(pl.program_id(0) == 0)
        def _():
            for r in (dwq_ref, dwk_ref, dwv_ref, dgq_ref, dgkv_ref):
                r[...] = jnp.zeros_like(r)

        cos4 = jnp.tile(cos_ref[...], (1, 4))
        sin4 = jnp.tile(sin_ref[...], (1, 4))
        dqv = dq_ref[...]
        dzq = dqv * cos4 + _rot_mla_t(dqv * sin4)
        dkv_ = dk_ref[...]
        dzk = dkv_ * cos4 + _rot_mla_t(dkv_ * sin4)
        l = _lane((tr, LANES))
        in_rope = (l >= KR_LANE) & (l < KR_LANE + ROPE_DIM)
        dkr = dzk[:, 0:128] + dzk[:, 128:256] + dzk[:, 256:384] + dzk[:, 384:512]
        dkr_ref[...] = jnp.where(in_rope, dkr, 0.0)
        dzq_b = dzq.astype(BF16)
        dzk_b = dzk.astype(BF16)
        dv_b = dv_ref[...].astype(BF16)
        dcqn = _dot_nt(dzq_b, wq_ref[...])
        dckvn = _dot_nt(dzk_b, wk_ref[...]) + _dot_nt(dv_b, wv_ref[...])
        dwq_ref[...] += _dot_tn(cqn_ref[...], dzq_b)
        dwk_ref[...] += _dot_tn(ckvn_ref[...], dzk_b)
        dwv_ref[...] += _dot_tn(ckvn_ref[...], dv_b)
        dcq, gq_term = _rms_bwd(cq_ref[...], gq_ref[...], dcqn)
        dckv, gkv_term = _rms_bwd(ckv_ref[...], gkv_ref[...], dckvn)
        dcq_ref[...] = dcq.astype(BF16)
        dckv_ref[...] = dckv.astype(BF16)
        dgq_ref[...] += jnp.sum(gq_term, axis=0, keepdims=True)
        dgkv_ref[...] += jnp.sum(gkv_term, axis=0, keepdims=True)

    full = lambda shp: pl.BlockSpec(shp, lambda i: (0, 0))
    rowb = lambda w: pl.BlockSpec((tr, w), lambda i: (i, 0))
    gq2, gkv2 = g_q.reshape(1, Q_RANK), g_kv.reshape(1, KV_RANK)
    return pl.pallas_call(
        body, name="mla_prep_bwd", grid=(s // tr,),
        in_specs=[rowb(512), rowb(512), rowb(512),
                  pl.BlockSpec((tr, 256), lambda i: (i, OFF_CQ // 2)), pl.BlockSpec((tr, LANES), lambda i: (i, OFF_CKV)),
                  rowb(256), rowb(128), rowb(LANES), rowb(LANES), full((1, Q_RANK)), full((1, KV_RANK)),
                  full(wq.shape), full(wk.shape), full(wv.shape)],
        out_specs=[rowb(256), rowb(128), rowb(128), full(wq.shape), full(wk.shape), full(wv.shape),
                   full((1, Q_RANK)), full((1, KV_RANK))],
        out_shape=[jax.ShapeDtypeStruct((s, 256), BF16), jax.ShapeDtypeStruct((s, 128), BF16), jax.ShapeDtypeStruct((s, 128), F32),
                   jax.ShapeDtypeStruct(wq.shape, F32), jax.ShapeDtypeStruct(wk.shape, F32), jax.ShapeDtypeStruct(wv.shape, F32),
                   jax.ShapeDtypeStruct((1, Q_RANK), F32), jax.ShapeDtypeStruct((1, KV_RANK), F32)],
        compiler_params=_cparams(("arbitrary",)),
    )(dq, dk, dv, proj, proj, cqn, ckvn, cos_m, sin_m, gq2, gkv2, wq, wk, wv)


def _ret_prep(proj, cos_r, sin_r):
    s = proj.shape[0]
    tr = _tile(s, 256)

    def body(q_ref, k_ref, cos_ref, sin_ref, qo_ref, ko_ref):
        cos, sin = cos_ref[...], sin_ref[...]
        q, k = q_ref[...], k_ref[...]
        qo_ref[...] = (q * cos + _rot_ret(q) * sin).astype(BF16)
        ko_ref[...] = ((k * cos + _rot_ret(k) * sin) * (HEAD ** -0.5)).astype(BF16)

    rowb = pl.BlockSpec((tr, 256), lambda i: (i, 0))
    return pl.pallas_call(
        body, name="ret_prep", grid=(s // tr,),
        in_specs=[pl.BlockSpec((tr, 256), lambda i: (i, OFF_RQ // 2)), pl.BlockSpec((tr, 256), lambda i: (i, OFF_RK // 2)), rowb, rowb],
        out_specs=[rowb, rowb], out_shape=[jax.ShapeDtypeStruct((s, 256), BF16)] * 2,
        compiler_params=_cparams(("parallel",)),
    )(proj, proj, cos_r, sin_r)


def _ret_prep_bwd(dq, dk, cos_r, sin_r):
    s = dq.shape[0]
    tr = _tile(s, 256)

    def body(dq_ref, dk_ref, cos_ref, sin_ref, qo_ref, ko_ref):
        cos, sin = cos_ref[...], sin_ref[...]
        q, k = dq_ref[...], dk_ref[...] * (HEAD ** -0.5)
        qo_ref[...] = (q * cos + _rot_ret_t(q * sin)).astype(BF16)
        ko_ref[...] = (k * cos + _rot_ret_t(k * sin)).astype(BF16)

    rowb = pl.BlockSpec((tr, 256), lambda i: (i, 0))
    return pl.pallas_call(
        body, name="ret_prep_bwd", grid=(s // tr,), in_specs=[rowb] * 4, out_specs=[rowb, rowb],
        out_shape=[jax.ShapeDtypeStruct((s, 256), BF16)] * 2, compiler_params=_cparams(("parallel",)),
    )(dq, dk, cos_r, sin_r)


_LOG_GAMMA = [float(np.log1p(-np.float32(2.0) ** np.float32(-5.0 - h))) for h in range(N_HEADS)]
_MLA_SCALE = float((HEAD + ROPE_DIM) ** -0.5)
_QK_SCALE = float(HEAD ** -0.5)
KEY_BLOCKS = 4
QB = 256


def _split2(x):
    h = x.astype(BF16)
    return h, (x - h.astype(F32)).astype(BF16)


def _dot2(x, u):
    h, lo = _split2(x)
    return _dot(h, u) + _dot(lo, u)


def _head_pick(block, head, axis):
    idx = lax.broadcasted_iota(jnp.int32, block.shape, axis)
    return jnp.sum(jnp.where(idx == head, block, 0.0), axis=axis, keepdims=True)


def _log_gamma_of(head):
    lg = jnp.float32(_LOG_GAMMA[3])
    for h in (2, 1, 0):
        lg = jnp.where(head == h, jnp.float32(_LOG_GAMMA[h]), lg)
    return lg


def _mixer_specs(mode, s, q_off, k_off, v_off):
    nhb = 2
    bw = 2 * LANES if mode == "mla" else LANES
    nsub = KEY_BLOCKS if (s // TQ) % KEY_BLOCKS == 0 else 1
    q_spec = pl.BlockSpec((QB, bw), lambda p, i: (i, q_off + p))
    k_spec = pl.BlockSpec((s, bw), lambda p, i: (0, k_off + p))
    v_spec = pl.BlockSpec((s, bw), lambda p, i: (0, v_off + p))
    return nhb, N_HEADS // nhb, nsub, q_spec, k_spec, v_spec


def _mixer_geometry(mode, i, nsub):
    w = TQ * nsub
    row = lax.broadcasted_iota(jnp.int32, (QB, w), 0)
    col = lax.broadcasted_iota(jnp.int32, (QB, w), 1)
    nfull = (i * QB) // w
    dist = col - row
    if mode in ("fox", "sb"):
        rel = dist
    else:
        rel = col - (row | (CHUNK - 1))

    def visible(c):
        off = c * w - i * QB
        return (rel + off) < 0 if mode == "sb" else (rel + off) <= 0

    return nfull, dist, visible


class _SideJob:
    def __init__(self, inputs, out_shape, n_sems, sends, recvs):
        self.inputs, self.out_shape, self.n_sems, self.sends, self.recvs = list(inputs), list(out_shape), n_sems, sends, recvs


def _carry_side_job(body, n_in, n_out, side, n_steps):
    if side is None:
        return body
    si, so = len(side.inputs), len(side.out_shape)

    def at(corner):
        ok = pl.program_id(0) == corner[0]
        for d in range(1, len(n_steps)):
            ok = ok & (pl.program_id(d) == corner[d])
        return ok

    def wrapped(*refs):
        ins, s_ins = refs[:n_in], refs[n_in:n_in + si]
        outs, s_outs = refs[n_in + si:n_in + si + n_out], refs[n_in + si + n_out:n_in + si + n_out + so]
        scratch, send, recv = refs[n_in + si + n_out + so:-2], refs[-2], refs[-1]

        @pl.when(at([0] * len(n_steps)))
        def _():
            for cp in side.sends(s_ins, s_outs, send, recv):
                cp.start()

        body(*ins, *outs, *scratch)

        @pl.when(at([n - 1 for n in n_steps]))
        def _():
            for cp in side.recvs(s_ins, s_outs, send, recv):
                cp.wait_recv()
            for cp in side.sends(s_ins, s_outs, send, recv):
                cp.wait_send()

    return wrapped


def _side_specs(side):
    if side is None:
        return [], [], []
    hbm = pl.BlockSpec(memory_space=pl.ANY)
    return ([hbm] * len(side.inputs), [hbm] * len(side.out_shape),
            [pltpu.SemaphoreType.DMA((side.n_sems,)), pltpu.SemaphoreType.DMA((side.n_sems,))])


def _mixer_fwd(mode, qa, q_off, ka, k_off, va, v_off, *, cum_col=None, cum_row=None, side=None):
    s = qa.shape[0]
    nq = s // QB
    nhb, nblk, nsub, q_spec, k_spec, v_spec = _mixer_specs(mode, s, q_off, k_off, v_off)
    w = TQ * nsub
    softmax = mode in ("fox", "mla")

    def body(*refs):
        refs = list(refs)
        q_ref, k_ref, v_ref = refs[:3]
        refs = refs[3:]
        if mode == "fox":
            cc_ref, cr_ref = refs[:2]
            refs = refs[2:]
        o_ref = refs[0]
        st_ref = refs[1]
        p = pl.program_id(0)
        i = pl.program_id(1)
        nfull, dist, visible = _mixer_geometry(mode, i, nsub)
        lane = _lane((1, LANES))
        heads = [nhb * p + hh for hh in range(nhb)]
        wide = mode == "mla"
        q_scale = _QK_SCALE if mode in ("fox", "sb") else 1.0
        cols = [slice(hh * LANES, (hh + 1) * LANES) if wide else slice(None) for hh in range(nhb)]
        if wide:
            qs = [q_ref[:, cols[hh]] for hh in range(nhb)]
        else:
            qf = q_ref[...].astype(F32) * q_scale
            qs = [jnp.where((lane // HEAD) == hh, qf, 0.0).astype(BF16) for hh in range(nhb)]
        if mode == "fox":
            cqs = [_head_pick(cc_ref[...], h, 1) for h in heads]
        if mode == "sb":
            r1 = lax.broadcasted_iota(jnp.int32, (TQ, TQ), 0)
            c1 = lax.broadcasted_iota(jnp.int32, (TQ, TQ), 1)
            u_after = (r1 > c1).astype(BF16)

        def chunk(c):
            return pl.ds(pl.multiple_of(c * w, w), w)

        def scores(c):
            js = chunk(c)
            return tuple(_dot_nt(qs[hh], k_ref[js, cols[hh]]) for hh in range(nhb))

        def head_step(hh, c, js, sc, vj, carry, last):
            if softmax:
                m, l, acc = carry
                if mode == "fox":
                    ck = _head_pick(cr_ref[:, js], heads[hh], 0)
                    sc = sc + (cqs[hh] - ck)
                else:
                    sc = sc * _MLA_SCALE
                if last:
                    sc = jnp.where(visible(c), sc, NEG)
                m_new = jnp.maximum(m, jnp.max(sc, axis=-1, keepdims=True))
                alpha = jnp.exp(m - m_new)
                pr = jnp.exp(sc - m_new)
                l = alpha * l + jnp.sum(pr, axis=-1, keepdims=True)
                acc = alpha * acc + _dot(pr.astype(BF16), vj)
                return m_new, l, acc
            run, acc = carry
            z = sc
            log_beta = jnp.minimum(z, 0.0) - jnp.log(1.0 + jnp.exp(-jnp.abs(z)))
            log_stay = log_beta - z
            if last:
                vis = visible(c)
                log_stay = jnp.where(vis, log_stay, 0.0)
            parts = [None] * nsub
            for b in reversed(range(nsub)):
                ls_b = log_stay[:, b * TQ:(b + 1) * TQ]
                parts[b] = _dot2(ls_b, u_after) + run
                run = run + jnp.sum(ls_b, axis=-1, keepdims=True)
            later = parts[0] if nsub == 1 else jnp.concatenate(parts, axis=1)
            wgt = jnp.exp(log_beta + later)
            if last:
                wgt = jnp.where(vis, wgt, 0.0)
            return run, acc + _dot(wgt.astype(BF16), vj)

        def step(c, c_next, state, last):
            scs, carries = state
            nxt = scores(c_next) if c_next is not None else None
            js = chunk(c)
            return nxt, tuple(head_step(hh, c, js, scs[hh], v_ref[js, cols[hh]], carries[hh], last) for hh in range(nhb))

        zero_acc = jnp.zeros((QB, LANES), F32)
        zero1 = jnp.zeros((QB, 1), F32)
        if softmax:
            init = tuple((jnp.full((QB, 1), NEG, F32), zero1, zero_acc) for _ in range(nhb))
        else:
            init = tuple((zero1, zero_acc) for _ in range(nhb))
        if mode == "sb":
            state = step(nfull, jnp.maximum(nfull - 1, 0), (scores(nfull), init), True)
            _, carries = lax.fori_loop(0, nfull, lambda t, st: step(nfull - 1 - t, jnp.maximum(nfull - 2 - t, 0), st, False), state)
        else:
            state = lax.fori_loop(0, nfull, lambda c, st: step(c, c + 1, st, False), (scores(0), init))
            _, carries = step(nfull, None, state, True)
        if softmax:
            outs = [acc / l for (m, l, acc) in carries]
            stats = [m + jnp.log(l) for (m, l, acc) in carries]
        else:
            outs, stats = [acc for (run, acc) in carries], [run for (run, acc) in carries]
        hm0 = (lane // HEAD) == 0
        pick = lambda a: jnp.where(hm0, a[0], a[1])
        if wide:
            for hh in range(nhb):
                o_ref[:, cols[hh]] = outs[hh]
        else:
            o_ref[...] = pick(outs)
        st_ref[0] = pick(stats)

    in_specs = [q_spec, k_spec, v_spec]
    args = [qa, ka, va]
    if mode == "fox":
        in_specs += [pl.BlockSpec((QB, LANES), lambda p, i: (i, 0)), pl.BlockSpec((8, s), lambda p, i: (0, 0))]
        args += [cum_col, cum_row]
    bw = 2 * LANES if mode == "mla" else LANES
    out_specs = [pl.BlockSpec((QB, bw), lambda p, i: (i, p))]
    out_shape = [jax.ShapeDtypeStruct((s, nblk * bw), F32)]
    out_specs.append(pl.BlockSpec((1, QB, LANES), lambda p, i: (p, i, 0)))
    out_shape.append(jax.ShapeDtypeStruct((nblk, s, LANES), F32))
    side_in, side_out, side_scratch = _side_specs(side)
    res = pl.pallas_call(
        _carry_side_job(body, len(args), len(out_shape), side, (nblk, nq)), name=mode + "_fwd", grid=(nblk, nq),
        in_specs=in_specs + side_in, out_specs=out_specs + side_out,
        out_shape=out_shape + ([] if side is None else side.out_shape), scratch_shapes=side_scratch,
        compiler_params=_cparams(("parallel", "parallel") if side is None else ("arbitrary", "arbitrary")),
    )(*args, *([] if side is None else side.inputs))
    return (res[0], res[1]) if side is None else (res[0], res[1], res[2:])


def _mixer_bwd(mode, qa, q_off, ka, k_off, va, v_off, o, do, *, stat=None, cum_col=None, cum_row=None, side=None):
    s = qa.shape[0]
    nq = s // QB
    nhb, nblk, nsub, q_spec, k_spec, v_spec = _mixer_specs(mode, s, q_off, k_off, v_off)
    w = TQ * nsub
    softmax = mode in ("fox", "mla")

    def body(*refs):
        refs = list(refs)
        q_ref, k_ref, v_ref, o_ref, do_ref = refs[:5]
        refs = refs[5:]
        st_ref = refs[0]
        refs = refs[1:]
        if mode == "fox":
            cc_ref, cr_ref = refs[:2]
            refs = refs[2:]
        dq_ref, dk_ref, dv_ref = refs[:3]
        dck_ref, drs_ref = refs[3:5] if mode == "fox" else (None, None)
        p = pl.program_id(0)
        i = pl.program_id(1)

        @pl.when(i == 0)
        def _():
            dk_ref[...] = jnp.zeros_like(dk_ref)
            dv_ref[...] = jnp.zeros_like(dv_ref)
            if mode == "fox":
                dck_ref[...] = jnp.zeros_like(dck_ref)

        nfull, dist, visible = _mixer_geometry(mode, i, nsub)
        lane = _lane((1, LANES))
        heads = [nhb * p + hh for hh in range(nhb)]
        dov = do_ref[...]
        wide = mode == "mla"
        q_scale = _QK_SCALE if mode in ("fox", "sb") else 1.0
        cols = [slice(hh * LANES, (hh + 1) * LANES) if wide else slice(None) for hh in range(nhb)]
        if wide:
            prod = dov * o_ref[...]
            qs = [q_ref[:, cols[hh]] for hh in range(nhb)]
            dos = [dov[:, cols[hh]].astype(BF16) for hh in range(nhb)]
            deltas = [jnp.sum(prod[:, cols[hh]], axis=-1, keepdims=True) for hh in range(nhb)]
        else:
            qf = q_ref[...].astype(F32) * q_scale
            prod = dov * o_ref[...]
            hms = [(lane // HEAD) == hh for hh in range(nhb)]
            qs = [jnp.where(hm, qf, 0.0).astype(BF16) for hm in hms]
            dos = [jnp.where(hm, dov, 0.0).astype(BF16) for hm in hms]
            deltas = [jnp.sum(jnp.where(hm, prod, 0.0), axis=-1, keepdims=True) for hm in hms]
        st = st_ref[0]
        stats = [st[:, hh * HEAD:hh * HEAD + 1] for hh in range(nhb)]
        if mode == "fox":
            cqs = [_head_pick(cc_ref[...], h, 1) for h in heads]
        if mode == "sb":
            r1 = lax.broadcasted_iota(jnp.int32, (TQ, TQ), 0)
            c1 = lax.broadcasted_iota(jnp.int32, (TQ, TQ), 1)
            u_upto = (r1 <= c1).astype(BF16)
            u_before = (r1 < c1).astype(BF16)

        def chunk(c):
            return pl.ds(pl.multiple_of(c * w, w), w)

        def scores(c):
            js = chunk(c)
            if mode == "sb":
                return tuple((_dot_nt(qs[hh], k_ref[js, cols[hh]]), None) for hh in range(nhb))
            return tuple((_dot_nt(qs[hh], k_ref[js, cols[hh]]), _dot_nt(dos[hh], v_ref[js, cols[hh]])) for hh in range(nhb))

        def emit(hh, js, ds_b, pr_b, dq):
            dk_ref[js, cols[hh]] += _dot_tn(ds_b, qs[hh])
            dv_ref[js, cols[hh]] += _dot_tn(pr_b, dos[hh])
            return dq + _dot(ds_b, k_ref[js, cols[hh]])

        def head_step(hh, c, js, sc_dp, carry, last):
            sc, dp = sc_dp
            if dp is None:
                dp = _dot_nt(dos[hh], v_ref[js, cols[hh]])
            if softmax:
                dq, rsum = carry
                if mode == "fox":
                    ck = _head_pick(cr_ref[:, js], heads[hh], 0)
                    sc = sc + (cqs[hh] - ck)
                else:
                    sc = sc * _MLA_SCALE
                if last:
                    sc = jnp.where(visible(c), sc, NEG)
                pr = jnp.exp(sc - stats[hh])
                ds = pr * (dp - deltas[hh])
                if mode == "fox":
                    dck_ref[0, hh:hh + 1, js] += jnp.sum(ds, axis=0, keepdims=True)
                    rsum = rsum + jnp.sum(ds, axis=-1, keepdims=True)
                if mode == "mla":
                    ds = ds * _MLA_SCALE
                return emit(hh, js, ds.astype(BF16), pr.astype(BF16), dq), rsum
            seen, gsum, dq = carry
            z = sc
            log_beta = jnp.minimum(z, 0.0) - jnp.log(1.0 + jnp.exp(-jnp.abs(z)))
            log_stay = log_beta - z
            if last:
                vis = visible(c)
                log_stay = jnp.where(vis, log_stay, 0.0)
            parts = []
            for b in range(nsub):
                ls_b = log_stay[:, b * TQ:(b + 1) * TQ]
                parts.append((stats[hh] - seen) - _dot2(ls_b, u_upto))
                seen = seen + jnp.sum(ls_b, axis=-1, keepdims=True)
            later = parts[0] if nsub == 1 else jnp.concatenate(parts, axis=1)
            wgt = jnp.exp(log_beta + later)
            if last:
                wgt = jnp.where(vis, wgt, 0.0)
            g = dp * wgt
            parts = []
            for b in range(nsub):
                g_b = g[:, b * TQ:(b + 1) * TQ]
                parts.append(gsum + _dot2(g_b, u_before))
                gsum = gsum + jnp.sum(g_b, axis=-1, keepdims=True)
            before = parts[0] if nsub == 1 else jnp.concatenate(parts, axis=1)
            beta = jnp.exp(log_beta)
            dz = g * (1.0 - beta) - beta * before
            if last:
                dz = jnp.where(vis, dz, 0.0)
            return seen, gsum, emit(hh, js, dz.astype(BF16), wgt.astype(BF16), dq)

        def step(c, c_next, state, last):
            scs, carries = state
            nxt = scores(c_next) if c_next is not None else None
            js = chunk(c)
            return nxt, tuple(head_step(hh, c, js, scs[hh], carries[hh], last) for hh in range(nhb))

        zero_acc = jnp.zeros((QB, LANES), F32)
        zero1 = jnp.zeros((QB, 1), F32)
        if softmax:
            init = tuple((zero_acc, zero1) for _ in range(nhb))
        else:
            init = tuple((zero1, zero1, zero_acc) for _ in range(nhb))
        state = lax.fori_loop(0, nfull, lambda c, st: step(c, c + 1, st, False), (scores(0), init))
        _, carries = step(nfull, None, state, True)
        if softmax:
            dqs = [dq for (dq, rsum) in carries]
        else:
            dqs = [dq for (seen, gsum, dq) in carries]
        hm0 = (lane // HEAD) == 0
        if wide:
            for hh in range(nhb):
                dq_ref[:, cols[hh]] = dqs[hh]
        else:
            dq_ref[...] = jnp.where(hm0, dqs[0], dqs[1]) * q_scale
        if mode == "fox":
            drs_ref[0] = jnp.where(hm0, carries[0][1], carries[1][1])

    bw = 2 * LANES if mode == "mla" else LANES
    pair_blk = pl.BlockSpec((QB, bw), lambda p, i: (i, p))
    full_blk = pl.BlockSpec((s, bw), lambda p, i: (0, p))
    stat_blk = pl.BlockSpec((1, QB, LANES), lambda p, i: (p, i, 0))
    in_specs = [q_spec, k_spec, v_spec, pair_blk, pair_blk]
    args = [qa, ka, va, o, do]
    in_specs.append(stat_blk)
    args.append(stat)
    if mode == "fox":
        in_specs += [pl.BlockSpec((QB, LANES), lambda p, i: (i, 0)), pl.BlockSpec((8, s), lambda p, i: (0, 0))]
        args += [cum_col, cum_row]
    out_specs = [pair_blk, full_blk, full_blk]
    out_shape = [jax.ShapeDtypeStruct((s, nblk * bw), F32)] * 3
    if mode == "fox":
        out_specs += [pl.BlockSpec((1, 8, s), lambda p, i: (p, 0, 0)), stat_blk]
        out_shape += [jax.ShapeDtypeStruct((2, 8, s), F32), jax.ShapeDtypeStruct((2, s, LANES), F32)]
    side_in, side_out, side_scratch = _side_specs(side)
    res = pl.pallas_call(
        _carry_side_job(body, len(args), len(out_shape), side, (nblk, nq)), name=mode + "_bwd", grid=(nblk, nq),
        in_specs=in_specs + side_in, out_specs=out_specs + side_out,
        out_shape=out_shape + ([] if side is None else side.out_shape), scratch_shapes=side_scratch,
        compiler_params=_cparams(("parallel", "arbitrary") if side is None else ("arbitrary", "arbitrary")),
    )(*args, *([] if side is None else side.inputs))
    return res if side is None else (*res[:len(out_shape)], res[len(out_shape):])


def _ret_geometry(p):
    lane = _lane((1, LANES))
    lg_lane = jnp.where(lane < HEAD, _log_gamma_of(2 * p), _log_gamma_of(2 * p + 1))
    a = lax.broadcasted_iota(jnp.int32, (TQ, 1), 0).astype(F32)
    row = lax.broadcasted_iota(jnp.int32, (TQ, TQ), 0)
    col = lax.broadcasted_iota(jnp.int32, (TQ, TQ), 1)
    same_chunk_or_earlier = (col // CHUNK) <= (row // CHUNK)
    gap = jnp.abs(row - col).astype(F32)
    decays = [jnp.where(same_chunk_or_earlier, jnp.exp(_log_gamma_of(2 * p + hh) * gap), 0.0) for hh in range(2)]
    r = lax.broadcasted_iota(jnp.int32, (LANES, LANES), 0)
    c = lax.broadcasted_iota(jnp.int32, (LANES, LANES), 1)
    own_head = (r // HEAD) == (c // HEAD)
    return lane, lg_lane, a, decays, own_head


def _ret_fwd(qa, ka, va, v_off):
    s = qa.shape[0]
    nq = s // TQ

    def body(q_ref, k_ref, v_ref, o_ref, st_ref, state):
        p = pl.program_id(0)

        @pl.when(pl.program_id(1) == 0)
        def _():
            state[...] = jnp.zeros_like(state)

        lane, lg_lane, a, decays, own_head = _ret_geometry(p)
        q = q_ref[...].astype(F32)
        k = k_ref[...]
        v = v_ref[...]
        s_in = state[...]
        st_ref[0, 0] = s_in
        out = _dot((q * jnp.exp(lg_lane * (a + 1.0))).astype(BF16), s_in.astype(BF16))
        for hh in range(2):
            hm = (lane // HEAD) == hh
            qh = jnp.where(hm, q, 0.0).astype(BF16)
            inner = _dot((_dot_nt(qh, k) * decays[hh]).astype(BF16), v)
            out = out + jnp.where(hm, inner, 0.0)
        o_ref[...] = out
        k_tail = (k.astype(F32) * jnp.exp(lg_lane * (TQ - 1.0 - a))).astype(BF16)
        state[...] = jnp.exp(lg_lane * float(TQ)) * s_in + jnp.where(own_head, _dot_tn(k_tail, v), 0.0)

    blk = lambda off: pl.BlockSpec((TQ, LANES), lambda p, i: (i, off + p))
    return pl.pallas_call(
        body, name="ret_fwd", grid=(2, nq), in_specs=[blk(0), blk(0), blk(v_off)],
        out_specs=[blk(0), pl.BlockSpec((1, 1, LANES, LANES), lambda p, i: (p, i, 0, 0))],
        out_shape=[jax.ShapeDtypeStruct((s, 2 * LANES), F32), jax.ShapeDtypeStruct((2, nq, LANES, LANES), F32)],
        scratch_shapes=[pltpu.VMEM((LANES, LANES), F32)],
        compiler_params=_cparams(("parallel", "arbitrary")),
    )(qa, ka, va)


def _ret_bwd(qa, ka, va, v_off, states, do):
    s = qa.shape[0]
    nq = s // TQ

    def body(q_ref, k_ref, v_ref, st_ref, do_ref, dq_ref, dk_ref, dv_ref, dstate):
        p = pl.program_id(0)

        @pl.when(pl.program_id(1) == 0)
        def _():
            dstate[...] = jnp.zeros_like(dstate)

        lane, lg_lane, a, decays, own_head = _ret_geometry(p)
        q = q_ref[...].astype(F32)
        k = k_ref[...]
        kf = k.astype(F32)
        v = v_ref[...]
        dov = do_ref[...]
        s_in = st_ref[0, 0].astype(BF16)
        ds_next = dstate[...]
        ds_b = ds_next.astype(BF16)
        head_decay = jnp.exp(lg_lane * (a + 1.0))
        tail_decay = jnp.exp(lg_lane * (TQ - 1.0 - a))
        k_tail = (kf * tail_decay).astype(BF16)
        dq = _dot_nt(dov.astype(BF16), s_in) * head_decay
        dk = _dot_nt(v, ds_b) * tail_decay
        dv = _dot(k_tail, ds_b)
        for hh in range(2):
            hm = (lane // HEAD) == hh
            qh = jnp.where(hm, q, 0.0).astype(BF16)
            doh = jnp.where(hm, dov, 0.0).astype(BF16)
            att = (_dot_nt(qh, k) * decays[hh]).astype(BF16)
            datt = (_dot_nt(doh, v) * decays[hh]).astype(BF16)
            dv = dv + _dot_tn(att, doh)
            dk = dk + _dot_tn(datt, qh)
            dq = dq + jnp.where(hm, _dot(datt, k), 0.0)
        dq_ref[...] = dq
        dk_ref[...] = dk
        dv_ref[...] = dv
        q_head = (q * head_decay).astype(BF16)
        dstate[...] = jnp.exp(lg_lane * float(TQ)) * ds_next + jnp.where(own_head, _dot_tn(q_head, dov.astype(BF16)), 0.0)

    blk = lambda off: pl.BlockSpec((TQ, LANES), lambda p, i: (nq - 1 - i, off + p))
    return pl.pallas_call(
        body, name="ret_bwd", grid=(2, nq),
        in_specs=[blk(0), blk(0), blk(v_off), pl.BlockSpec((1, 1, LANES, LANES), lambda p, i: (p, nq - 1 - i, 0, 0)), blk(0)],
        out_specs=[blk(0)] * 3, out_shape=[jax.ShapeDtypeStruct((s, 2 * LANES), F32)] * 3,
        scratch_shapes=[pltpu.VMEM((LANES, LANES), F32)],
        compiler_params=_cparams(("parallel", "arbitrary")),
    )(qa, ka, va, states, do)


def _seg_mean_matrix():
    r = lax.broadcasted_iota(jnp.int32, (GROUP, GROUP), 0)
    c = lax.broadcasted_iota(jnp.int32, (GROUP, GROUP), 1)
    return jnp.where((r // HEAD) == (c // HEAD), 1.0 / HEAD, 0.0).astype(F32)


def _sigmoid(x):
    return 1.0 / (1.0 + jnp.exp(-x))


def _mix_post(oa, ob, oc, od, proj, g):
    s = oa.shape[0]
    tr = _tile(s, 256)

    def body(a_ref, b_ref, c_ref, d_ref, rg_ref, g_ref, o_ref):
        gv = g_ref[...]
        o_ref[:, 0:GROUP] = _rms(a_ref[...], gv[:, 0:GROUP]).astype(BF16)
        o_ref[:, GROUP:2 * GROUP] = _rms(b_ref[...], gv[:, GROUP:2 * GROUP]).astype(BF16)
        seg = _seg_mean_matrix()
        c = c_ref[...]
        cen = c - _dot_exact(c, seg)
        n = cen * lax.rsqrt(_dot_exact(cen * cen, seg) + EPS)
        rg = rg_ref[...]
        o_ref[:, 2 * GROUP:3 * GROUP] = (n * gv[:, 2 * GROUP:3 * GROUP] * (rg * _sigmoid(rg))).astype(BF16)
        o_ref[:, 3 * GROUP:] = _rms(d_ref[...], gv[:, 3 * GROUP:]).astype(BF16)

    blk = pl.BlockSpec((tr, GROUP), lambda i: (i, 0))
    return pl.pallas_call(
        body, name="mix_post", grid=(s // tr,),
        in_specs=[blk] * 4 + [pl.BlockSpec((tr, GROUP), lambda i: (i, OFF_RG // 2)), pl.BlockSpec((1, D_MODEL), lambda i: (0, 0))],
        out_specs=pl.BlockSpec((tr, D_MODEL), lambda i: (i, 0)), out_shape=jax.ShapeDtypeStruct((s, D_MODEL), BF16),
        compiler_params=_cparams(("parallel",)),
    )(oa, ob, oc, od, proj, g.reshape(1, D_MODEL))


def _mix_post_bwd(dmixed, oa, ob, oc, od, proj, g):
    s = oa.shape[0]
    tr = _tile(s, 256)

    def body(dm_ref, a_ref, b_ref, c_ref, d_ref, rg_ref, g_ref, da_ref, db_ref, dc_ref, dd_ref, drg_ref, dg_ref):
        @pl.when(pl.program_id(0) == 0)
        def _():
            dg_ref[...] = jnp.zeros_like(dg_ref)

        gv = g_ref[...]
        dm = dm_ref[...]
        for k, (x_ref, dx_ref) in enumerate(((a_ref, da_ref), (b_ref, db_ref), (None, None), (d_ref, dd_ref))):
            if x_ref is None:
                continue
            cols = slice(k * GROUP, (k + 1) * GROUP)
            dx, gterm = _rms_bwd(x_ref[...], gv[:, cols], dm[:, cols])
            dx_ref[...] = dx
            dg_ref[:, cols] += jnp.sum(gterm, axis=0, keepdims=True)
        cols = slice(2 * GROUP, 3 * GROUP)
        seg = _seg_mean_matrix()
        c = c_ref[...]
        cen = c - _dot_exact(c, seg)
        rstd = lax.rsqrt(_dot_exact(cen * cen, seg) + EPS)
        n = cen * rstd
        rg = rg_ref[...]
        sg = _sigmoid(rg)
        gate = rg * sg
        dy = dm[:, cols]
        gc = gv[:, cols]
        dn = dy * gc * gate
        dg_ref[:, cols] += jnp.sum(dy * n * gate, axis=0, keepdims=True)
        drg_ref[...] = (dy * n * gc * (sg * (1.0 + rg * (1.0 - sg)))).astype(BF16)
        dc_ref[...] = rstd * (dn - _dot_exact(dn, seg) - n * _dot_exact(dn * n, seg))

    blk = pl.BlockSpec((tr, GROUP), lambda i: (i, 0))
    gsp = pl.BlockSpec((1, D_MODEL), lambda i: (0, 0))
    return pl.pallas_call(
        body, name="mix_post_bwd", grid=(s // tr,),
        in_specs=[pl.BlockSpec((tr, D_MODEL), lambda i: (i, 0))] + [blk] * 4 + [pl.BlockSpec((tr, GROUP), lambda i: (i, OFF_RG // 2)), gsp],
        out_specs=[blk] * 5 + [gsp],
        out_shape=[jax.ShapeDtypeStruct((s, GROUP), F32)] * 4 + [jax.ShapeDtypeStruct((s, GROUP), BF16), jax.ShapeDtypeStruct((1, D_MODEL), F32)],
        compiler_params=_cparams(("arbitrary",)),
    )(dmixed, oa, ob, oc, od, proj, g.reshape(1, D_MODEL))


def _pack_w_in(w):
    z = lambda n: jnp.zeros((w.shape[0], n), w.dtype)
    misc = jnp.concatenate([w[:, 768:772], z(KR_LANE - N_HEADS), w[:, 1156:1188], z(LANES - KR_LANE - ROPE_DIM)], axis=1)
    return jnp.concatenate([w[:, 0:768], w[:, 772:1028], w[:, 1188:2980], w[:, 1028:1156], misc], axis=1)


def _unpack_dw_in(d):
    m = OFF_MISC * LANES
    return jnp.concatenate([d[:, 0:768], d[:, m:m + N_HEADS], d[:, 768:1024], d[:, OFF_CKV * LANES:m],
                            d[:, m + KR_LANE:m + KR_LANE + ROPE_DIM], d[:, 1024:OFF_CKV * LANES]], axis=1)


def _pack_w_q(w):
    return jnp.pad(w.reshape(Q_RANK, N_HEADS, HEAD + ROPE_DIM), ((0, 0), (0, 0), (0, LANES - HEAD - ROPE_DIM))).reshape(Q_RANK, 4 * LANES)


def _unpack_dw_q(d):
    return d.reshape(Q_RANK, N_HEADS, LANES)[:, :, :HEAD + ROPE_DIM].reshape(Q_RANK, N_HEADS * (HEAD + ROPE_DIM))


def _pack_w_kv(w):
    w4 = w.reshape(KV_RANK, N_HEADS, 2 * HEAD)
    widen = lambda a: jnp.pad(a, ((0, 0), (0, 0), (0, LANES - HEAD))).reshape(KV_RANK, N_HEADS * LANES)
    return widen(w4[:, :, :HEAD]), widen(w4[:, :, HEAD:])


def _unpack_dw_kv(dk, dv):
    narrow = lambda a: a.reshape(KV_RANK, N_HEADS, LANES)[:, :, :HEAD]
    return jnp.concatenate([narrow(dk), narrow(dv)], axis=2).reshape(KV_RANK, 2 * N_HEADS * HEAD)


def _narrow_heads(a):
    return a.reshape(a.shape[0], N_HEADS, LANES)[:, :, :HEAD].reshape(a.shape[0], N_HEADS * HEAD)


def _widen_heads(a):
    return jnp.pad(a.reshape(a.shape[0], N_HEADS, HEAD), ((0, 0), (0, 0), (0, LANES - HEAD))).reshape(a.shape[0], N_HEADS * LANES)


def _layer_fwd(x, lw, tabs, tag, side=None, fox_side=None, late_weights=None, h1=None, next_gain=None):
    cos_m, sin_m, cos_r, sin_r = tabs
    if h1 is None:
        h1 = _norm_fwd(x, lw["g_mix_pre"], name=tag + "pre_norm")
    proj, projb = _matmul(h1, lw["w_in"], name=tag + "in_proj", also_bf16=True)
    bias_row = jnp.pad(lw["b_forget"], (FF_LANE, LANES - N_HEADS - FF_LANE)).reshape(1, LANES)
    cum_col, cum_row = _fox_cum(proj, bias_row)
    oa, lse_a, *fox_carried = _mixer_fwd("fox", projb, OFF_FQ, projb, OFF_FK, projb, OFF_FV, cum_col=cum_col, cum_row=cum_row,
                                         side=fox_side)
    if late_weights is not None:
        lw = {**lw, **late_weights(fox_carried[0])}
    qm, km, vm, cqn, ckvn = _mla_prep(proj, cos_m, sin_m, lw["g_q_lora"], lw["g_kv_lora"], lw["wq"], lw["wk"], lw["wv"])
    ob_wide, lse_b = _mixer_fwd("mla", qm, 0, km, 0, vm, 0)
    ob = _narrow_heads(ob_wide)
    qr, kr = _ret_prep(proj, cos_r, sin_r)
    oc, ret_states = _ret_fwd(qr, kr, projb, OFF_RV)
    od, tot_d, *carried = _mixer_fwd("sb", projb, OFF_SQ, projb, OFF_SK, projb, OFF_SV, side=side)
    mixed = _mix_post(oa, ob, oc, od, proj, lw["g_mix_out"])
    mix = _matmul(mixed, lw["w_out"], name=tag + "out_proj")
    x1, h2 = _norm_fwd(mix, lw["g_mix_post"], name=tag + "mix_post_norm", resid=x, out_dtype=F32, next_gain=lw["g_ffn_pre"])
    u = _matmul(h2, lw["w_ffn_up"], name=tag + "ffn_up", relu2=True, out_dtype=BF16, col_blocks=True)
    f = _matmul(u, lw["w_ffn_down"], name=tag + "ffn_down")
    x2 = _norm_fwd(f, lw["g_ffn_post"], name=tag + "ffn_post_norm", resid=x1, out_dtype=F32, next_gain=next_gain)
    h_next = None
    if next_gain is not None:
        x2, h_next = x2
    saved = dict(x=x, h1=h1, proj=proj, projb=projb, bias_row=bias_row, cum_col=cum_col, cum_row=cum_row, oa=oa, lse_a=lse_a,
                 qm=qm, km=km, vm=vm, cqn=cqn, ckvn=ckvn, ob=ob, ob_wide=ob_wide, lse_b=lse_b, qr=qr, kr=kr, ret_states=ret_states, oc=oc, od=od, tot_d=tot_d, mixed=mixed,
                 mix=mix, x1=x1, h2=h2, u=u, f=f)
    return x2, saved, lw, (carried[0] if carried else None), h_next


def _layer_bwd(dx2, lw, sv, tabs, tag, side=None, ffn_side=None, fox_side=None, post_given=None, then_prev=None):
    cos_m, sin_m, cos_r, sin_r = tabs
    g = {}
    if post_given is None:
        df, g["g_ffn_post"] = _norm_bwd(sv["f"], lw["g_ffn_post"], dx2, name=tag + "ffn_post_norm_bwd", out_dtype=BF16)
    else:
        df, g["g_ffn_post"] = post_given
    du_pre = _matmul(df, lw["w_ffn_down"], name=tag + "ffn_down_dx", tb=True, out_dtype=BF16, relu2_of=sv["u"], side=ffn_side)
    ffn_carried = None
    if ffn_side is not None:
        du_pre, ffn_carried = du_pre
    g["w_ffn_down"] = _matmul(sv["u"], df, name=tag + "ffn_down_dw", ta=True)
    dh2 = _matmul(du_pre, lw["w_ffn_up"], name=tag + "ffn_up_dx", tb=True, col_blocks=True)
    g["w_ffn_up"] = _matmul(sv["h2"], du_pre, name=tag + "ffn_up_dw", ta=True, col_blocks=True)
    dx1, g["g_ffn_pre"], dmix, g["g_mix_post"] = _norm_bwd(sv["x1"], lw["g_ffn_pre"], dh2, name=tag + "ffn_pre_norm_bwd", add=dx2,
                                                           then=(sv["mix"], lw["g_mix_post"]))
    dmixed = _matmul(dmix, lw["w_out"], name=tag + "out_proj_dx", tb=True)
    g["w_out"] = _matmul(sv["mixed"], dmix, name=tag + "out_proj_dw", ta=True)
    proj, projb = sv["proj"], sv["projb"]
    doa, dob, doc, dod, drg, g["g_mix_out"] = _mix_post_bwd(dmixed, sv["oa"], sv["ob"], sv["oc"], sv["od"], proj, lw["g_mix_out"])
    dfq, dfk, dfv, dck, drs, *fox_carried = _mixer_bwd(
        "fox", projb, OFF_FQ, projb, OFF_FK, projb, OFF_FV, sv["oa"], doa, stat=sv["lse_a"], cum_col=sv["cum_col"],
        cum_row=sv["cum_row"], side=None if fox_side is None else fox_side(g))
    dqm, dkm, dvm = _mixer_bwd("mla", sv["qm"], 0, sv["km"], 0, sv["vm"], 0, sv["ob_wide"], _widen_heads(dob), stat=sv["lse_b"])
    dcq, dckv, dkr, dwq, dwk, dwv, g["g_q_lora"], g["g_kv_lora"] = _mla_prep_bwd(
        dqm, dkm, dvm, proj, sv["cqn"], sv["ckvn"], cos_m, sin_m, lw["g_q_lora"], lw["g_kv_lora"], lw["wq"], lw["wk"], lw["wv"])
    dqr, dkr_ret, drv = _ret_bwd(sv["qr"], sv["kr"], projb, OFF_RV, sv["ret_states"], doc)
    drq, drk = _ret_prep_bwd(dqr, dkr_ret, cos_r, sin_r)
    if callable(side):
        side = side(g, ffn_carried, fox_carried[0] if fox_carried else None)
    dsq, dsk, dsv, *carried = _mixer_bwd("sb", projb, OFF_SQ, projb, OFF_SK, projb, OFF_SV, sv["od"], dod, stat=sv["tot_d"], side=side)
    dmisc, db_row = _fox_gate_bwd(dck, drs, proj, sv["bias_row"], dkr)
    b = lambda a: a.astype(BF16)
    dproj = jnp.concatenate([b(dfq), b(dfk), b(dfv), dcq, drq, drk, b(drv), drg, b(dsq), b(dsk), b(dsv), dckv, dmisc], axis=1)
    dh1 = _matmul(dproj, lw["w_in"], name=tag + "in_proj_dx", tb=True)
    g["w_in"] = _matmul(sv["h1"], dproj, name=tag + "in_proj_dw", ta=True)
    dx, g["g_mix_pre"], *prev_post = _norm_bwd(sv["x"], lw["g_mix_pre"], dh1, name=tag + "pre_norm_bwd", add=dx1, then=then_prev)
    g["b_forget"] = db_row[0, FF_LANE:FF_LANE + N_HEADS]
    g["wq"], g["wk"], g["wv"] = dwq, dwk, dwv
    return dx, g, (carried[0] if carried else None), (tuple(prev_post) if prev_post else None)


def _local_step(x, positions, layers, target):
    s = x.shape[0]
    tabs = _rope_tables(positions.reshape(s, 1))
    saved, h1 = [], None
    for li, lw in enumerate(layers):
        nxt = layers[li + 1]["g_mix_pre"] if li + 1 < len(layers) else None
        x, sv, _, _, h1 = _layer_fwd(x, lw, tabs, "l%d_" % li, h1=h1, next_gain=nxt)
        saved.append(sv)
    loss_row, dx = _loss_head(x, target)
    grads, post = [None] * len(layers), None
    for li in reversed(range(len(layers))):
        prev = (saved[li - 1]["f"], layers[li - 1]["g_ffn_post"]) if li > 0 else None
        dx, grads[li], _, post = _layer_bwd(dx, layers[li], saved[li], tabs, "l%d_" % li, post_given=post, then_prev=prev)
    return loss_row[0, 0], dx, grads


def _adamw(w, g, m, v, *, name):
    r, c = w.shape
    tr = 256 if r % 256 == 0 else r
    blk = pl.BlockSpec((tr, c), lambda i: (i, 0))
    c1 = 1.0 - ADAM_B1 ** ADAM_STEP
    c2 = 1.0 - ADAM_B2 ** ADAM_STEP

    def body(w_ref, g_ref, m_ref, v_ref, d_ref, mo_ref, vo_ref):
        gv = g_ref[...]
        mn = ADAM_B1 * m_ref[...] + (1.0 - ADAM_B1) * gv
        vn = ADAM_B2 * v_ref[...] + (1.0 - ADAM_B2) * jnp.square(gv)
        mo_ref[...] = mn
        vo_ref[...] = vn
        d_ref[...] = -ADAM_LR * ((mn / c1) / (jnp.sqrt(vn / c2) + ADAM_EPS) + ADAM_WD * w_ref[...])

    return pl.pallas_call(
        body, name=name, grid=(r // tr,), in_specs=[blk] * 4, out_specs=[blk] * 3,
        out_shape=[jax.ShapeDtypeStruct((r, c), F32)] * 3, compiler_params=_cparams(("parallel",)),
    )(w, g, m, v)


BIG = ("w_in", "w_q_up", "w_kv_up", "w_out", "w_ffn_up", "w_ffn_down")
SMALL = ("g_mix_pre", "b_forget", "g_q_lora", "g_kv_lora", "g_mix_out", "g_mix_post", "g_ffn_pre", "g_ffn_post")
N_CHIPS = 4
ANY = pl.BlockSpec(memory_space=pl.ANY)


def _mesh_pos():
    return lax.axis_index("x"), lax.axis_index("y"), lax.axis_index("c")


def _other_chips(x, y):
    return [(1 - x, y), (x, 1 - y), (1 - x, 1 - y)]


def _rows_half(ref, half):
    h = ref.shape[-2] // 2
    return ref.at[(slice(None),) * (len(ref.shape) - 2) + (pl.ds(half * h, h), slice(None))]


def _remote(src, dst, send_sem, recv_sem, device):
    return pltpu.make_async_remote_copy(src_ref=src, dst_ref=dst, send_sem=send_sem, recv_sem=recv_sem, device_id=device,
                                        device_id_type=MESH)


def _comm_call(body, name, args, out_shape, n_sems):
    return pl.pallas_call(
        body, name=name, in_specs=[ANY] * len(args), out_specs=[ANY] * len(out_shape), out_shape=out_shape,
        scratch_shapes=[pltpu.SemaphoreType.DMA((n_sems,)), pltpu.SemaphoreType.DMA((n_sems,))],
        compiler_params=pltpu.CompilerParams(has_side_effects=True),
    )(*args)


def _run_side_job(side, name):
    si = len(side.inputs)

    def body(*refs):
        args = (refs[:si], refs[si:-2], refs[-2], refs[-1])
        sends = side.sends(*args)
        for cp in sends:
            cp.start()
        for cp in side.recvs(*args):
            cp.wait_recv()
        for cp in sends:
            cp.wait_send()

    return _comm_call(body, name, side.inputs, side.out_shape, side.n_sems)


def _gather_job(shards):
    n = len(shards)

    def copies(own_block, ins, outs, send_sems, recv_sems):
        x, y, c = _mesh_pos()
        return [_remote(_rows_half(ins[t], c), _rows_half(outs[t].at[2 * x + y if own_block else 2 * px + py], c),
                        send_sems.at[3 * t + j], recv_sems.at[3 * t + j], (px, py, c))
                for t in range(n) for j, (px, py) in enumerate(_other_chips(x, y))]

    return _SideJob(shards, [jax.ShapeDtypeStruct((N_CHIPS,) + a.shape, a.dtype) for a in shards], 3 * n,
                    functools.partial(copies, True), functools.partial(copies, False))


def _forward_halves(gathered):
    n = len(gathered)

    def body(*refs):
        bufs, send_sems, recv_sems = refs[n:2 * n], refs[-2], refs[-1]
        x, y, c = _mesh_pos()

        def d2d(t, j, block, half):
            region = _rows_half(bufs[t].at[block], half)
            return _remote(region, region, send_sems.at[3 * t + j], recv_sems.at[3 * t + j], (x, y, 1 - c))

        peers = list(enumerate(_other_chips(x, y)))
        sends = [d2d(t, j, 2 * px + py, c) for t in range(n) for j, (px, py) in peers]
        for cp in sends:
            cp.start()
        for t in range(n):
            for j, (px, py) in peers:
                d2d(t, j, 2 * px + py, 1 - c).wait_recv()
        for cp in sends:
            cp.wait_send()

    return pl.pallas_call(
        body, name="gather_forward", in_specs=[ANY] * n, out_specs=[ANY] * n,
        out_shape=[jax.ShapeDtypeStruct(g.shape, g.dtype) for g in gathered], input_output_aliases={t: t for t in range(n)},
        scratch_shapes=[pltpu.SemaphoreType.DMA((3 * n,)), pltpu.SemaphoreType.DMA((3 * n,))],
        compiler_params=pltpu.CompilerParams(has_side_effects=True),
    )(*gathered)


def _exchange_halves_job(gs):
    n = len(gs)

    def copies(ins, outs, send_sems, recv_sems):
        x, y, c = _mesh_pos()
        return [_remote(_rows_half(ins[t], 1 - c), outs[t], send_sems.at[t], recv_sems.at[t], (x, y, 1 - c)) for t in range(n)]

    out_shape = [jax.ShapeDtypeStruct(g.shape[:2] + (g.shape[2] // 2, g.shape[3]), g.dtype) for g in gs]
    return _SideJob(gs, out_shape, n, copies, copies)


def _pair_add(g, r, c_idx, *, name):
    nb, d, rows, cols = g.shape
    h = rows // 2
    tr = min(h, 512)
    nt = h // tr

    def body(c_ref, g_ref, r_ref, p_ref, pb_ref):
        s = g_ref[...] + r_ref[...]
        p_ref[...] = s
        pb_ref[...] = s.astype(BF16)

    blk = pl.BlockSpec((1, 1, tr, cols), lambda k, l, i, c_ref: (k, l, i, 0))
    return pl.pallas_call(
        body, name=name,
        grid_spec=pltpu.PrefetchScalarGridSpec(
            num_scalar_prefetch=1, grid=(nb, d, nt),
            in_specs=[pl.BlockSpec((1, 1, tr, cols), lambda k, l, i, c_ref: (k, l, c_ref[0] * nt + i, 0)), blk],
            out_specs=[blk, blk]),
        out_shape=[jax.ShapeDtypeStruct((nb, d, h, cols), F32), jax.ShapeDtypeStruct((nb, d, h, cols), BF16)],
        compiler_params=_cparams(("parallel", "parallel", "parallel")),
    )(c_idx, g, r)


def _exchange_chips_job(pbs):
    n = len(pbs)

    def copies(ins, outs, send_sems, recv_sems):
        x, y, c = _mesh_pos()
        return [_remote(ins[t].at[2 * px + py], outs[t].at[j], send_sems.at[3 * t + j], recv_sems.at[3 * t + j], (px, py, c))
                for t in range(n) for j, (px, py) in enumerate(_other_chips(x, y))]

    return _SideJob(pbs, [jax.ShapeDtypeStruct((3,) + p.shape[1:], p.dtype) for p in pbs], 3 * n, copies, copies)


def _chip_add(p, r, k_idx, *, name):
    _, d, h, cols = p.shape
    tr = min(h, 512)
    nt = h // tr

    def body(k_ref, p_ref, r_ref, o_ref):
        o_ref[0] = ((p_ref[0, 0] + r_ref[0, 0].astype(F32)) + r_ref[1, 0].astype(F32)) + r_ref[2, 0].astype(F32)

    return pl.pallas_call(
        body, name=name,
        grid_spec=pltpu.PrefetchScalarGridSpec(
            num_scalar_prefetch=1, grid=(d, nt),
            in_specs=[pl.BlockSpec((1, 1, tr, cols), lambda l, i, k_ref: (k_ref[0], l, i, 0)),
                      pl.BlockSpec((3, 1, tr, cols), lambda l, i, k_ref: (0, l, i, 0))],
            out_specs=pl.BlockSpec((1, tr, cols), lambda l, i, k_ref: (l, i, 0))),
        out_shape=jax.ShapeDtypeStruct((d, h, cols), F32), compiler_params=_cparams(("parallel", "parallel")),
    )(k_idx, p, r)


def _share_halves(qs):
    n = len(qs)

    def body(*refs):
        ins, outs, send_sems, recv_sems = refs[:n], refs[n:2 * n], refs[2 * n], refs[2 * n + 1]
        x, y, c = _mesh_pos()
        cps = [_remote(ins[t], outs[t], send_sems.at[t], recv_sems.at[t], (x, y, 1 - c)) for t in range(n)]
        for cp in cps:
            cp.start()
        for cp in cps:
            cp.wait_recv()
        for cp in cps:
            cp.wait_send()

    return _comm_call(body, "grad_pair_share", qs, [jax.ShapeDtypeStruct(q.shape, q.dtype) for q in qs], n)


def _all_reduce_small(v):
    r, cols = v.shape
    n_dev = 8

    def body(v_ref, o_ref, buf, send_sems, recv_sems):
        x, y, c = _mesh_pos()
        me = 4 * x + 2 * y + c
        buf[me] = v_ref[...]

        def peer(j):
            return (1 - x if j & 4 else x, 1 - y if j & 2 else y, 1 - c if j & 1 else c)

        def copy(j, slot):
            return pltpu.make_async_remote_copy(src_ref=v_ref, dst_ref=buf.at[slot], send_sem=send_sems.at[j - 1],
                                                recv_sem=recv_sems.at[j - 1], device_id=peer(j), device_id_type=MESH)

        sends = [copy(j, me) for j in range(1, n_dev)]
        for cp in sends:
            cp.start()
        for j in range(1, n_dev):
            px, py, pc = peer(j)
            copy(j, 4 * px + 2 * py + pc).wait_recv()
        for cp in sends:
            cp.wait_send()
        acc = buf[0]
        for d in range(1, n_dev):
            acc = acc + buf[d]
        o_ref[...] = acc

    vm = pl.BlockSpec(memory_space=pltpu.VMEM)
    return pl.pallas_call(
        body, name="small_all_reduce", in_specs=[vm], out_specs=vm, out_shape=jax.ShapeDtypeStruct((r, cols), F32),
        scratch_shapes=[pltpu.VMEM((n_dev, r, cols), F32), pltpu.SemaphoreType.DMA((n_dev - 1,)), pltpu.SemaphoreType.DMA((n_dev - 1,))],
        compiler_params=pltpu.CompilerParams(has_side_effects=True),
    )(v)


_COL_SHARDED = ("w_in", "w_q_up", "w_kv_up", "w_ffn_up")


def _shard_cols(blocks, a, b):
    c = blocks[0].shape[-1]
    out = []
    while a < b:
        k = a // c
        hi = min(b, (k + 1) * c)
        out.append(blocks[k][:, a - k * c:hi - k * c])
        a = hi
    return out


def _pack_w_in_shards(blocks):
    z = lambda n: [jnp.zeros((blocks[0].shape[0], n), blocks[0].dtype)]
    cols = lambda a, b: _shard_cols(blocks, a, b)
    return jnp.concatenate(cols(0, 768) + cols(772, 1028) + cols(1188, 2980) + cols(1028, 1156) + cols(768, 772)
                           + z(KR_LANE - N_HEADS) + cols(1156, 1188) + z(LANES - KR_LANE - ROPE_DIM), axis=1)


def _whole_layer(name, blocks):
    if name in _COL_SHARDED:
        return jnp.concatenate([blocks[k] for k in range(N_CHIPS)], axis=1)
    return blocks.reshape(N_CHIPS * blocks.shape[1], blocks.shape[2])


def _split_layer(name, whole):
    if name in _COL_SHARDED:
        c = whole.shape[1] // N_CHIPS
        return jnp.stack([whole[:, k * c:(k + 1) * c] for k in range(N_CHIPS)])
    return whole.reshape(N_CHIPS, whole.shape[0] // N_CHIPS, whole.shape[1])


def _small_to_rows(d):
    v = jnp.concatenate([d[k].astype(F32).reshape(-1) for k in SMALL])
    rows = -(-v.shape[0] // (8 * LANES)) * 8
    return jnp.pad(v, (0, rows * LANES - v.shape[0])).reshape(rows, LANES)


def _small_from_rows(rows, shapes):
    v = rows.reshape(-1)
    out, o = {}, 0
    for k in SMALL:
        sz = int(np.prod(shapes[k]))
        out[k] = v[o:o + sz].reshape(shapes[k])
        o += sz
    return out


_ARG_NAMES = ("x", "positions", "g_mix_pre", "w_in", "b_forget", "g_q_lora", "w_q_up", "g_kv_lora", "w_kv_up", "g_mix_out", "w_out",
              "g_mix_post", "g_ffn_pre", "w_ffn_up", "w_ffn_down", "g_ffn_post")
_WEIGHTS = _ARG_NAMES[2:]


def kernel(x, positions, g_mix_pre, w_in, b_forget, g_q_lora, w_q_up, g_kv_lora, w_kv_up, g_mix_out, w_out, g_mix_post, g_ffn_pre, w_ffn_up, w_ffn_down, g_ffn_post, loss_target, m_g_mix_pre, m_w_in, m_b_forget, m_g_q_lora, m_w_q_up, m_g_kv_lora, m_w_kv_up, m_g_mix_out, m_w_out, m_g_mix_post, m_g_ffn_pre, m_w_ffn_up, m_w_ffn_down, m_g_ffn_post, v_g_mix_pre, v_w_in, v_b_forget, v_g_q_lora, v_w_q_up, v_g_kv_lora, v_w_kv_up, v_g_mix_out, v_w_out, v_g_mix_post, v_g_ffn_pre, v_w_ffn_up, v_w_ffn_down, v_g_ffn_post):
    w = dict(g_mix_pre=g_mix_pre, w_in=w_in, b_forget=b_forget, g_q_lora=g_q_lora, w_q_up=w_q_up, g_kv_lora=g_kv_lora, w_kv_up=w_kv_up,
             g_mix_out=g_mix_out, w_out=w_out, g_mix_post=g_mix_post, g_ffn_pre=g_ffn_pre, w_ffn_up=w_ffn_up, w_ffn_down=w_ffn_down,
             g_ffn_post=g_ffn_post)
    m = dict(g_mix_pre=m_g_mix_pre, w_in=m_w_in, b_forget=m_b_forget, g_q_lora=m_g_q_lora, w_q_up=m_w_q_up, g_kv_lora=m_g_kv_lora,
             w_kv_up=m_w_kv_up, g_mix_out=m_g_mix_out, w_out=m_w_out, g_mix_post=m_g_mix_post, g_ffn_pre=m_g_ffn_pre,
             w_ffn_up=m_w_ffn_up, w_ffn_down=m_w_ffn_down, g_ffn_post=m_g_ffn_post)
    v = dict(g_mix_pre=v_g_mix_pre, w_in=v_w_in, b_forget=v_b_forget, g_q_lora=v_g_q_lora, w_q_up=v_w_q_up, g_kv_lora=v_g_kv_lora,
             w_kv_up=v_w_kv_up, g_mix_out=v_g_mix_out, w_out=v_w_out, g_mix_post=v_g_mix_post, g_ffn_pre=v_g_ffn_pre,
             w_ffn_up=v_w_ffn_up, w_ffn_down=v_w_ffn_down, g_ffn_post=v_g_ffn_post)
    shard_shapes = {k: w[k].shape for k in BIG}
    small_shapes = {k: w[k].shape for k in SMALL}
    c_idx = lax.axis_index("c").astype(jnp.int32).reshape(1)
    k_idx = (2 * lax.axis_index("x") + lax.axis_index("y")).astype(jnp.int32).reshape(1)
    first_core = lax.axis_index("c") == 0

    mine = 2 * lax.axis_index("x") + lax.axis_index("y")
    shards_b = [{k: w[k][l:l + 1].astype(BF16) for k in BIG} for l in range(DEPTH)]
    gains = [dict(g_mix_pre=g_mix_pre[l], b_forget=b_forget[l], g_q_lora=g_q_lora[l], g_kv_lora=g_kv_lora[l], g_mix_out=g_mix_out[l],
                  g_mix_post=g_mix_post[l], g_ffn_pre=g_ffn_pre[l], g_ffn_post=g_ffn_post[l]) for l in range(DEPTH)]
    FIRST, LATER = ("w_in", "w_q_up", "w_kv_up"), ("w_out", "w_ffn_up", "w_ffn_down")
    EARLY_GRADS, LATE_GRADS = ("w_ffn_down", "w_ffn_up", "w_out"), ("w_in", "w_q_up", "w_kv_up")

    def gather_job(l, names):
        return _gather_job([shards_b[l][k] for k in names])

    def weights_of(l, names, gathered):
        four = {k: lax.dynamic_update_slice(g, shards_b[l][k][None], (mine, 0, 0, 0))[:, 0]
                for k, g in zip(names, _forward_halves(gathered))}
        out = {}
        for k in names:
            if k == "w_in":
                out["w_in"] = _pack_w_in_shards(four[k])
            elif k == "w_q_up":
                out["wq"] = _pack_w_q(_whole_layer(k, four[k]))
            elif k == "w_kv_up":
                out["wk"], out["wv"] = _pack_w_kv(_whole_layer(k, four[k]))
            elif k == "w_ffn_up":
                out[k] = four[k]
            else:
                out[k] = _whole_layer(k, four[k])
        return out

    def grad_blocks(names, g):
        whole = dict(w_in=lambda: _unpack_dw_in(g["w_in"]), w_q_up=lambda: _unpack_dw_q(g["wq"]),
                     w_kv_up=lambda: _unpack_dw_kv(g["wk"], g["wv"]), w_out=lambda: g["w_out"], w_ffn_down=lambda: g["w_ffn_down"])
        return [(g[k] if k == "w_ffn_up" else _split_layer(k, whole[k]()))[:, None] for k in names]

    def pair_sums(names, blocks, theirs):
        return [_pair_add(b, r, c_idx, name="grad_pair_add_" + k) for k, b, r in zip(names, blocks, theirs)]

    def exchange_job(*pairs):
        return _exchange_chips_job([pb for pair in pairs for (_, pb) in pair])

    def finish_grads(names, pair, partial):
        half = [_chip_add(p, r, k_idx, name="grad_chip_add_" + k) for k, (p, _), r in zip(names, pair, partial)]
        return {k: jnp.where(first_core, jnp.concatenate([q, s], axis=1), jnp.concatenate([s, q], axis=1))
                for k, q, s in zip(names, half, _share_halves(half))}

    seq = x.shape[1]
    tabs = _rope_tables(positions[0].reshape(seq, 1))
    first0 = weights_of(0, FIRST, _run_side_job(gather_job(0, FIRST), "gather_weights_l0"))
    x1, saved0, lw0, gathered1, h1 = _layer_fwd(x[0], {**gains[0], **first0}, tabs, "l0_", fox_side=gather_job(0, LATER),
                                                late_weights=lambda got: weights_of(0, LATER, got), side=gather_job(1, BIG),
                                                next_gain=gains[1]["g_mix_pre"])
    lw1 = {**gains[1], **weights_of(1, BIG, gathered1)}
    x2, saved1, _, _, _ = _layer_fwd(x1, lw1, tabs, "l1_", h1=h1)
    loss_row, dx = _loss_head(x2, loss_target[0])
    loss = lax.psum(loss_row[0, 0], ("x", "y", "c"))
    dx, grads1, _, post0 = _layer_bwd(dx, lw1, saved1, tabs, "l1_", then_prev=(saved0["f"], lw0["g_ffn_post"]))
    blocks1 = grad_blocks(BIG, grads1)
    early_blocks0, pair1, early0 = [], [], []

    def beside_l0_fox_backward(g):
        early_blocks0.extend(grad_blocks(EARLY_GRADS, g))
        return _exchange_halves_job(early_blocks0)

    def beside_l0_sb_backward(g, theirs1, theirs_early0):
        pair1.extend(pair_sums(BIG, blocks1, theirs1))
        early0.extend(pair_sums(EARLY_GRADS, early_blocks0, theirs_early0))
        return exchange_job(pair1, early0)

    dx, grads0, partial, _ = _layer_bwd(dx, lw0, saved0, tabs, "l0_", ffn_side=_exchange_halves_job(blocks1),
                                        fox_side=beside_l0_fox_backward, side=beside_l0_sb_backward, post_given=post0)
    big1 = finish_grads(BIG, pair1, partial[:len(BIG)])
    big0 = finish_grads(EARLY_GRADS, early0, partial[len(BIG):])
    late_blocks0 = grad_blocks(LATE_GRADS, grads0)
    late0 = pair_sums(LATE_GRADS, late_blocks0, _run_side_job(_exchange_halves_job(late_blocks0), "grad_pair_exchange_l0"))
    big0.update(finish_grads(LATE_GRADS, late0, _run_side_job(exchange_job(late0), "grad_chip_exchange_l0")))
    g_big = {k: jnp.concatenate([big0[k], big1[k]], axis=0) for k in BIG}
    grads = [grads0, grads1]

    g_small_local = {k: jnp.stack([grads[l][k].reshape(small_shapes[k][1:]) for l in range(DEPTH)]) for k in SMALL}
    g_small = _small_from_rows(_all_reduce_small(_small_to_rows(g_small_local)), small_shapes)

    g_all = {**g_big, **g_small}
    delta, new_m, new_v = {}, {}, {}
    for k in BIG:
        d, r, c = shard_shapes[k]
        two_d = lambda a: a.reshape(d * r, c)
        dk, mk, vk = _adamw(two_d(w[k]), two_d(g_all[k]), two_d(m[k]), two_d(v[k]), name="adamw_" + k)
        delta[k], new_m[k], new_v[k] = dk.reshape(d, r, c), mk.reshape(d, r, c), vk.reshape(d, r, c)
    ds, ms, vs = _adamw(_small_to_rows(w), _small_to_rows(g_small), _small_to_rows(m), _small_to_rows(v), name="adamw_small")
    delta.update(_small_from_rows(ds, small_shapes))
    new_m.update(_small_from_rows(ms, small_shapes))
    new_v.update(_small_from_rows(vs, small_shapes))

    grad_x = dx.reshape(x.shape)
    return (loss, grad_x, *[g_all[k] for k in _WEIGHTS], *[delta[k] for k in _WEIGHTS], *[new_m[k] for k in _WEIGHTS],
            *[new_v[k] for k in _WEIGHTS])
```

```python
import functools
import math

import numpy as np
import jax
import jax.numpy as jnp
from jax import lax
from jax.experimental import pallas as pl
from jax.experimental.pallas import tpu as pltpu

F32 = jnp.float32
BF16 = jnp.bfloat16
MESH = pl.DeviceIdType.MESH

D_MODEL = 1024
DEPTH = 2
CHUNK = 64
GROUP = 256
HEAD = 64
N_HEADS = 4
Q_RANK = 256
KV_RANK = 128
ROPE_DIM = 32
D_FF = 4096
D_IN = 2980
D_INP = 3072
ROPE_BASE = 10000.0
EPS = 1e-6
LANES = 128
TQ = 128
NEG = -1e30

ADAM_LR, ADAM_B1, ADAM_B2, ADAM_EPS, ADAM_WD, ADAM_STEP = 0.001, 0.9, 0.999, 1e-08, 0.01, 10

OFF_FQ, OFF_FK, OFF_FV, OFF_CQ = 0, 2, 4, 6
OFF_RQ, OFF_RK, OFF_RV, OFF_RG = 8, 10, 12, 14
OFF_SQ, OFF_SK, OFF_SV = 16, 18, 20
OFF_CKV, OFF_MISC = 22, 23
FF_LANE, KR_LANE = 0, 64

VMEM_LIMIT = 56 * 1024 * 1024


def _tile(dim, pref):
    return pref if dim % pref == 0 else dim


def _cparams(sem, vmem=None):
    return pltpu.CompilerParams(dimension_semantics=sem, vmem_limit_bytes=vmem or VMEM_LIMIT)


def _dot(a, b):
    return jnp.dot(a, b, preferred_element_type=F32)


def _dot_nt(a, b):
    return lax.dot_general(a, b, (((1,), (1,)), ((), ())), preferred_element_type=F32)


def _dot_tn(a, b):
    return lax.dot_general(a, b, (((0,), (0,)), ((), ())), preferred_element_type=F32)


def _dot_exact(a, b):
    return jnp.dot(a, b, precision=lax.Precision.HIGHEST, preferred_element_type=F32)


def _matmul(a, b, *, name, ta=False, tb=False, out_dtype=F32, tm=1024, tn=1024, tk=1024,
            relu2=False, relu2_of=None, also_bf16=False, side=None, col_blocks=False):
    if ta:
        kdim, m = a.shape
    else:
        m, kdim = a.shape
    if col_blocks and not ta:
        n = b.shape[1] if tb else b.shape[0] * b.shape[2]
        if tb:
            kdim = b.shape[0] * b.shape[2]
    else:
        n = b.shape[0] if tb else b.shape[1]
    tm, tn, tk = _tile(m, tm), _tile(n, tn), _tile(kdim, tk)
    nk = kdim // tk
    a_spec = pl.BlockSpec((tk, tm), lambda i, j, k: (k, i)) if ta else pl.BlockSpec((tm, tk), lambda i, j, k: (i, k))
    b_spec = pl.BlockSpec((tn, tk), lambda i, j, k: (j, k)) if tb else pl.BlockSpec((tk, tn), lambda i, j, k: (k, j))
    o_spec = pl.BlockSpec((tm, tn), lambda i, j, k: (i, j))
    if col_blocks and ta:
        o_spec = pl.BlockSpec((None, tm, tn), lambda i, j, k: (j, i, 0))
    elif col_blocks and tb:
        assert b.shape[2] == tk
        b_spec = pl.BlockSpec((None, tn, tk), lambda i, j, k: (k, j, 0))
    elif col_blocks:
        assert b.shape[2] == tn
        b_spec = pl.BlockSpec((None, tk, tn), lambda i, j, k: (j, k, 0))
    two = also_bf16

    def body(*refs):
        refs = list(refs)
        a_ref, b_ref = refs[0], refs[1]
        e_ref = refs[2] if relu2_of is not None else None
        pos = 3 if relu2_of is not None else 2
        o_ref = refs[pos]
        o2_ref = refs[pos + 1] if two else None
        acc_ref = refs[-1]
        k = pl.program_id(2)
        av = a_ref[...].astype(BF16)
        bv = b_ref[...].astype(BF16)
        if ta:
            part = _dot_tn(av, bv)
        elif tb:
            part = _dot_nt(av, bv)
        else:
            part = _dot(av, bv)

        @pl.when(k == 0)
        def _():
            acc_ref[...] = part

        @pl.when(k > 0)
        def _():
            acc_ref[...] += part

        @pl.when(k == nk - 1)
        def _():
            r = acc_ref[...]
            if relu2_of is not None:
                r = r * (2.0 * jnp.sqrt(e_ref[...].astype(F32)))
            if relu2:
                r = jnp.square(jnp.maximum(r, 0.0))
            o_ref[...] = r.astype(o_ref.dtype)
            if also_bf16:
                o2_ref[...] = r.astype(BF16)

    in_specs = [a_spec, b_spec]
    args = [a, b]
    if relu2_of is not None:
        in_specs.append(o_spec)
        args.append(relu2_of)
    out_shape = [jax.ShapeDtypeStruct((n // tn, m, tn) if (col_blocks and ta) else (m, n), out_dtype)]
    out_specs = [o_spec]
    if two:
        out_shape.append(jax.ShapeDtypeStruct((m, n), BF16))
        out_specs.append(o_spec)
    grid = (m // tm, n // tn, nk)
    side_in, side_out, side_scratch = _side_specs(side)
    res = pl.pallas_call(
        _carry_side_job(body, len(args), len(out_shape), side, grid), name=name, grid=grid,
        in_specs=in_specs + side_in, out_specs=out_specs + side_out,
        out_shape=out_shape + ([] if side is None else side.out_shape),
        scratch_shapes=[pltpu.VMEM((tm, tn), F32)] + side_scratch,
        compiler_params=_cparams(("parallel", "parallel", "arbitrary") if side is None else ("arbitrary",) * 3),
    )(*args, *([] if side is None else side.inputs))
    main = res[:len(out_shape)]
    main = main if two else main[0]
    return main if side is None else (main, res[len(out_shape):])


def _rms(x, g):
    r = lax.rsqrt(jnp.mean(x * x, axis=-1, keepdims=True) + EPS)
    return x * r * g


def _rms_bwd(x, g, dy):
    r = lax.rsqrt(jnp.mean(x * x, axis=-1, keepdims=True) + EPS)
    xh = x * r
    gdy = dy * g
    dx = r * (gdy - xh * jnp.mean(xh * gdy, axis=-1, keepdims=True))
    return dx, xh * dy


def _norm_fwd(x, g, *, name, resid=None, out_dtype=BF16, next_gain=None):
    s, d = x.shape
    tr = _tile(s, 256)
    row = pl.BlockSpec((tr, d), lambda i: (i, 0))
    gsp = pl.BlockSpec((1, d), lambda i: (0, 0))

    def body(*refs):
        refs = list(refs)
        x_ref, g_ref = refs[:2]
        y = _rms(x_ref[...], g_ref[...])
        pos = 2
        if resid is not None:
            y = refs[pos][...] + y
            pos += 1
        if next_gain is None:
            refs[pos][...] = y.astype(refs[pos].dtype)
        else:
            refs[pos + 1][...] = y.astype(refs[pos + 1].dtype)
            refs[pos + 2][...] = _rms(y, refs[pos][...]).astype(BF16)

    args = [x, g.reshape(1, d)] + ([] if resid is None else [resid]) + ([] if next_gain is None else [next_gain.reshape(1, d)])
    in_specs = [row, gsp] + ([] if resid is None else [row]) + ([] if next_gain is None else [gsp])
    first = jax.ShapeDtypeStruct((s, d), out_dtype)
    if next_gain is None:
        out_specs, out_shape = row, first
    else:
        out_specs, out_shape = [row, row], [first, jax.ShapeDtypeStruct((s, d), BF16)]
    return pl.pallas_call(
        body, name=name, grid=(s // tr,), in_specs=in_specs, out_specs=out_specs, out_shape=out_shape,
        compiler_params=_cparams(("parallel",)),
    )(*args)


def _norm_bwd(x, g, dy, *, name, add=None, out_dtype=F32, then=None):
    s, d = x.shape
    tr = _tile(s, 256)
    row = pl.BlockSpec((tr, d), lambda i: (i, 0))
    gsp = pl.BlockSpec((1, d), lambda i: (0, 0))
    n_in = 3 + (add is not None) + (2 if then is not None else 0)

    def body(*refs):
        ins, outs = refs[:n_in], refs[n_in:]
        x_ref, g_ref, dy_ref = ins[:3]
        dx, gterm = _rms_bwd(x_ref[...], g_ref[...], dy_ref[...].astype(F32))
        if add is not None:
            dx = dx + ins[3][...]
        outs[0][...] = dx.astype(outs[0].dtype)
        terms = [(outs[1], gterm)]
        if then is not None:
            dx2, gterm2 = _rms_bwd(ins[-2][...], ins[-1][...], dx)
            outs[2][...] = dx2.astype(BF16)
            terms.append((outs[3], gterm2))

        @pl.when(pl.program_id(0) == 0)
        def _():
            for dg_ref, _ in terms:
                dg_ref[...] = jnp.zeros_like(dg_ref)

        for dg_ref, term in terms:
            dg_ref[...] += jnp.sum(term, axis=0, keepdims=True)

    args = [x, g.reshape(1, d), dy] + ([] if add is None else [add]) + ([] if then is None else [then[0], then[1].reshape(1, d)])
    in_specs = [row, gsp, row] + ([] if add is None else [row]) + ([] if then is None else [row, gsp])
    out_specs = [row, gsp] + ([] if then is None else [row, gsp])
    out_shape = [jax.ShapeDtypeStruct((s, d), out_dtype), jax.ShapeDtypeStruct((1, d), F32)]
    if then is not None:
        out_shape += [jax.ShapeDtypeStruct((s, d), BF16), jax.ShapeDtypeStruct((1, d), F32)]
    return pl.pallas_call(
        body, name=name, grid=(s // tr,), in_specs=in_specs, out_specs=out_specs, out_shape=out_shape,
        compiler_params=_cparams(("arbitrary",)),
    )(*args)


def _loss_head(f, g, resid, target):
    s, d = f.shape
    tr = _tile(s, 256)
    row = pl.BlockSpec((tr, d), lambda i: (i, 0))
    gsp = pl.BlockSpec((1, d), lambda i: (0, 0))
    lsp = pl.BlockSpec((1, LANES), lambda i: (0, 0))

    def body(f_ref, g_ref, r_ref, t_ref, l_ref, dy_ref, df_ref, dg_ref):
        fv, gv = f_ref[...], g_ref[...]
        e = (r_ref[...] + _rms(fv, gv)) - t_ref[...]
        dy = e * (1.0 / d)
        dy_ref[...] = dy
        df, gterm = _rms_bwd(fv, gv, dy)
        df_ref[...] = df.astype(BF16)

        @pl.when(pl.program_id(0) == 0)
        def _():
            l_ref[...] = jnp.zeros_like(l_ref)
            dg_ref[...] = jnp.zeros_like(dg_ref)

        part = 0.5 * jnp.sum(jnp.mean(e * e, axis=-1, keepdims=True), axis=0, keepdims=True)
        l_ref[...] += jnp.broadcast_to(part, (1, LANES))
        dg_ref[...] += jnp.sum(gterm, axis=0, keepdims=True)

    return pl.pallas_call(
        body, name="loss_head", grid=(s // tr,), in_specs=[row, gsp, row, row], out_specs=[lsp, row, row, gsp],
        out_shape=[jax.ShapeDtypeStruct((1, LANES), F32), jax.ShapeDtypeStruct((s, d), F32), jax.ShapeDtypeStruct((s, d), BF16),
                   jax.ShapeDtypeStruct((1, d), F32)],
        compiler_params=_cparams(("arbitrary",)),
    )(f, g.reshape(1, d), resid, target)


def _rope_tables(pos_col):
    s = pos_col.shape[0]
    tr = _tile(s, 512)
    f_mla = ROPE_BASE ** (-jnp.arange(ROPE_DIM // 2, dtype=F32) / (ROPE_DIM // 2))
    f_ret = ROPE_BASE ** (-jnp.arange(HEAD // 2, dtype=F32) / (HEAD // 2))
    fm = jnp.concatenate([jnp.zeros((64,), F32), f_mla, f_mla, jnp.zeros((32,), F32)]).reshape(1, LANES)
    fr = jnp.tile(jnp.concatenate([f_ret, f_ret]), 4).reshape(1, 2 * LANES)

    def body(p_ref, fm_ref, fr_ref, cm_ref, sm_ref, cr_ref, sr_ref):
        p = p_ref[...].astype(F32)
        am = p * fm_ref[...]
        ar = p * fr_ref[...]
        cm_ref[...] = jnp.cos(am)
        sm_ref[...] = jnp.sin(am)
        cr_ref[...] = jnp.cos(ar)
        sr_ref[...] = jnp.sin(ar)

    return pl.pallas_call(
        body, name="rope_tables", grid=(s // tr,),
        in_specs=[pl.BlockSpec((tr, 1), lambda i: (i, 0)), pl.BlockSpec((1, LANES), lambda i: (0, 0)),
                  pl.BlockSpec((1, 2 * LANES), lambda i: (0, 0))],
        out_specs=[pl.BlockSpec((tr, LANES), lambda i: (i, 0))] * 2 + [pl.BlockSpec((tr, 2 * LANES), lambda i: (i, 0))] * 2,
        out_shape=[jax.ShapeDtypeStruct((s, LANES), F32)] * 2 + [jax.ShapeDtypeStruct((s, 2 * LANES), F32)] * 2,
        compiler_params=_cparams(("parallel",)),
    )(pos_col, fm, fr)


def _lane(shape):
    return lax.broadcasted_iota(jnp.int32, shape, len(shape) - 1)


def _rot_mla(z):
    l = _lane(z.shape) % LANES
    n = z.shape[-1]
    return jnp.where(l < 80, -pltpu.roll(z, n - 16, 1), pltpu.roll(z, 16, 1))


def _rot_mla_t(y):
    l = _lane(y.shape) % LANES
    n = y.shape[-1]
    return jnp.where((l >= 64) & (l < 80), pltpu.roll(y, n - 16, 1),
                     jnp.where((l >= 80) & (l < 96), -pltpu.roll(y, 16, 1), 0.0))


def _rot_ret(z):
    l = _lane(z.shape) % HEAD
    n = z.shape[-1]
    return jnp.where(l < 32, -pltpu.roll(z, n - 32, 1), pltpu.roll(z, 32, 1))


def _rot_ret_t(y):
    l = _lane(y.shape) % HEAD
    n = y.shape[-1]
    return jnp.where(l < 32, pltpu.roll(y, n - 32, 1), -pltpu.roll(y, 32, 1))


def _log_sigmoid(x):
    return jnp.minimum(x, 0.0) - jnp.log1p(jnp.exp(-jnp.abs(x)))


def _fox_cum(proj, bias_row):
    s = proj.shape[0]
    nb = s // TQ

    def body(x_ref, b_ref, cc_ref, cr_ref, carry_ref):
        @pl.when(pl.program_id(0) == 0)
        def _():
            carry_ref[...] = jnp.zeros_like(carry_ref)

        ls = _log_sigmoid(x_ref[...] + b_ref[...])
        r = lax.broadcasted_iota(jnp.int32, (TQ, TQ), 0)
        c = lax.broadcasted_iota(jnp.int32, (TQ, TQ), 1)
        tri = (c <= r).astype(F32)
        cum = _dot_exact(tri, ls) + carry_ref[...]
        carry_ref[...] = cum[TQ - 1:TQ, :]
        cc_ref[...] = cum
        cr_ref[...] = cum.T[0:8, :]

    return pl.pallas_call(
        body, name="fox_cum", grid=(nb,),
        in_specs=[pl.BlockSpec((TQ, LANES), lambda i: (i, OFF_MISC)), pl.BlockSpec((1, LANES), lambda i: (0, 0))],
        out_specs=[pl.BlockSpec((TQ, LANES), lambda i: (i, 0)), pl.BlockSpec((8, TQ), lambda i: (0, i))],
        out_shape=[jax.ShapeDtypeStruct((s, LANES), F32), jax.ShapeDtypeStruct((8, s), F32)],
        scratch_shapes=[pltpu.VMEM((1, LANES), F32)],
        compiler_params=_cparams(("arbitrary",)),
    )(proj, bias_row)


def _fox_gate_bwd(dck, drs, proj, bias_row, dkr):
    s = proj.shape[0]
    nb = s // TQ

    def body(d_ref, r_ref, x_ref, b_ref, k_ref, o_ref, db_ref, carry_ref):
        @pl.when(pl.program_id(0) == 0)
        def _():
            carry_ref[...] = jnp.zeros_like(carry_ref)
            db_ref[...] = jnp.zeros_like(db_ref)

        rows = jnp.concatenate([d_ref[0], d_ref[1], jnp.zeros((TQ - 16, TQ), F32)], axis=0)
        t = rows.T
        l = _lane((TQ, LANES))
        r0, r1 = r_ref[0], r_ref[1]
        rsum = jnp.where(l == 0, r0[:, 0:1], jnp.where(l == 1, r0[:, HEAD:HEAD + 1],
                         jnp.where(l == 2, r1[:, 0:1], jnp.where(l == 3, r1[:, HEAD:HEAD + 1], 0.0))))
        dcum = rsum - jnp.where(l < 2, t, pltpu.roll(t, LANES - 6, 1))
        r = lax.broadcasted_iota(jnp.int32, (TQ, TQ), 0)
        c = lax.broadcasted_iota(jnp.int32, (TQ, TQ), 1)
        triu = (c >= r).astype(F32)
        rc = _dot_exact(triu, dcum) + carry_ref[...]
        carry_ref[...] = rc[0:1, :]
        f = x_ref[...] + b_ref[...]
        sig_neg = 1.0 / (1.0 + jnp.exp(f))
        df = jnp.where(l < N_HEADS, rc * sig_neg, 0.0)
        db_ref[...] += jnp.sum(df, axis=0, keepdims=True)
        o_ref[...] = (df + k_ref[...]).astype(o_ref.dtype)

    rev = lambda i: nb - 1 - i
    return pl.pallas_call(
        body, name="fox_gate_bwd", grid=(nb,),
        in_specs=[pl.BlockSpec((2, 8, TQ), lambda i: (0, 0, rev(i))), pl.BlockSpec((2, TQ, LANES), lambda i: (0, rev(i), 0)),
                  pl.BlockSpec((TQ, LANES), lambda i: (rev(i), OFF_MISC)),
                  pl.BlockSpec((1, LANES), lambda i: (0, 0)), pl.BlockSpec((TQ, LANES), lambda i: (rev(i), 0))],
        out_specs=[pl.BlockSpec((TQ, LANES), lambda i: (rev(i), 0)), pl.BlockSpec((1, LANES), lambda i: (0, 0))],
        out_shape=[jax.ShapeDtypeStruct((s, LANES), BF16), jax.ShapeDtypeStruct((1, LANES), F32)],
        scratch_shapes=[pltpu.VMEM((1, LANES), F32)],
        compiler_params=_cparams(("arbitrary",)),
    )(dck, drs, proj, bias_row, dkr)


def _mla_prep(proj, cos_m, sin_m, g_q, g_kv, wq, wk, wv):
    s = proj.shape[0]
    tr = _tile(s, 256)

    def body(cq_ref, ckv_ref, misc_ref, cos_ref, sin_ref, gq_ref, gkv_ref, wq_ref, wk_ref, wv_ref,
             q_ref, k_ref, v_ref, cqn_ref, ckvn_ref):
        cos4 = jnp.tile(cos_ref[...], (1, 4))
        sin4 = jnp.tile(sin_ref[...], (1, 4))
        cqn = _rms(cq_ref[...], gq_ref[...]).astype(BF16)
        ckvn = _rms(ckv_ref[...], gkv_ref[...]).astype(BF16)
        cqn_ref[...] = cqn
        ckvn_ref[...] = ckvn
        zq = _dot(cqn, wq_ref[...])
        q_ref[...] = (zq * cos4 + _rot_mla(zq) * sin4).astype(BF16)
        l = _lane((tr, LANES))
        kr = jnp.where((l >= KR_LANE) & (l < KR_LANE + ROPE_DIM), misc_ref[...], 0.0)
        zk = _dot(ckvn, wk_ref[...]) + jnp.tile(kr, (1, 4))
        k_ref[...] = (zk * cos4 + _rot_mla(zk) * sin4).astype(BF16)
        v_ref[...] = _dot(ckvn, wv_ref[...]).astype(BF16)

    full = lambda a: pl.BlockSpec(a.shape, lambda i: (0, 0))
    rowb = lambda w: pl.BlockSpec((tr, w), lambda i: (i, 0))
    gq2, gkv2 = g_q.reshape(1, Q_RANK), g_kv.reshape(1, KV_RANK)
    return pl.pallas_call(
        body, name="mla_prep", grid=(s // tr,),
        in_specs=[pl.BlockSpec((tr, 256), lambda i: (i, OFF_CQ // 2)), pl.BlockSpec((tr, LANES), lambda i: (i, OFF_CKV)),
                  pl.BlockSpec((tr, LANES), lambda i: (i, OFF_MISC)), rowb(LANES), rowb(LANES),
                  full(gq2), full(gkv2), full(wq), full(wk), full(wv)],
        out_specs=[rowb(512), rowb(512), rowb(512), rowb(256), rowb(128)],
        out_shape=[jax.ShapeDtypeStruct((s, 512), BF16), jax.ShapeDtypeStruct((s, 512), BF16), jax.ShapeDtypeStruct((s, 512), BF16),
                   jax.ShapeDtypeStruct((s, 256), BF16), jax.ShapeDtypeStruct((s, 128), BF16)],
        compiler_params=_cparams(("parallel",)),
    )(proj, proj, proj, cos_m, sin_m, gq2, gkv2, wq, wk, wv)


def _mla_prep_bwd(dq, dk, dv, proj, cqn, ckvn, cos_m, sin_m, g_q, g_kv, wq, wk, wv):
    s = proj.shape[0]
    tr = _tile(s, 256)

    def body(dq_ref, dk_ref, dv_ref, cq_ref, ckv_ref, cqn_ref, ckvn_ref, cos_ref, sin_ref, gq_ref, gkv_ref,
             wq_ref, wk_ref, wv_ref, dcq_ref, dckv_ref, dkr_ref, dwq_ref, dwk_ref, dwv_ref, dgq_ref, dgkv_ref):
        @pl.when(pl.program_id(0) == 0)
        def _():
            for r in (dwq_ref, dwk_ref, dwv_ref, dgq_ref, dgkv_ref):
                r[...] = jnp.zeros_like(r)

        cos4 = jnp.tile(cos_ref[...], (1, 4))
        sin4 = jnp.tile(sin_ref[...], (1, 4))
        dqv = dq_ref[...]
        dzq = dqv * cos4 + _rot_mla_t(dqv * sin4)
        dkv_ = dk_ref[...]
        dzk = dkv_ * cos4 + _rot_mla_t(dkv_ * sin4)
        l = _lane((tr, LANES))
        in_rope = (l >= KR_LANE) & (l < KR_LANE + ROPE_DIM)
        dkr = dzk[:, 0:128] + dzk[:, 128:256] + dzk[:, 256:384] + dzk[:, 384:512]
        dkr_ref[...] = jnp.where(in_rope, dkr, 0.0)
        dzq_b = dzq.astype(BF16)
        dzk_b = dzk.astype(BF16)
        dv_b = dv_ref[...].astype(BF16)
        dcqn = _dot_nt(dzq_b, wq_ref[...])
        dckvn = _dot_nt(dzk_b, wk_ref[...]) + _dot_nt(dv_b, wv_ref[...])
        dwq_ref[...] += _dot_tn(cqn_ref[...], dzq_b)
        dwk_ref[...] += _dot_tn(ckvn_ref[...], dzk_b)
        dwv_ref[...] += _dot_tn(ckvn_ref[...], dv_b)
        dcq, gq_term = _rms_bwd(cq_ref[...], gq_ref[...], dcqn)
        dckv, gkv_term = _rms_bwd(ckv_ref[...], gkv_ref[...], dckvn)
        dcq_ref[...] = dcq.astype(BF16)
        dckv_ref[...] = dckv.astype(BF16)
        dgq_ref[...] += jnp.sum(gq_term, axis=0, keepdims=True)
        dgkv_ref[...] += jnp.sum(gkv_term, axis=0, keepdims=True)

    full = lambda shp: pl.BlockSpec(shp, lambda i: (0, 0))
    rowb = lambda w: pl.BlockSpec((tr, w), lambda i: (i, 0))
    gq2, gkv2 = g_q.reshape(1, Q_RANK), g_kv.reshape(1, KV_RANK)
    return pl.pallas_call(
        body, name="mla_prep_bwd", grid=(s // tr,),
        in_specs=[rowb(512), rowb(512), rowb(512),
                  pl.BlockSpec((tr, 256), lambda i: (i, OFF_CQ // 2)), pl.BlockSpec((tr, LANES), lambda i: (i, OFF_CKV)),
                  rowb(256), rowb(128), rowb(LANES), rowb(LANES), full((1, Q_RANK)), full((1, KV_RANK)),
                  full(wq.shape), full(wk.shape), full(wv.shape)],
        out_specs=[rowb(256), rowb(128), rowb(128), full(wq.shape), full(wk.shape), full(wv.shape),
                   full((1, Q_RANK)), full((1, KV_RANK))],
        out_shape=[jax.ShapeDtypeStruct((s, 256), BF16), jax.ShapeDtypeStruct((s, 128), BF16), jax.ShapeDtypeStruct((s, 128), F32),
                   jax.ShapeDtypeStruct(wq.shape, F32), jax.ShapeDtypeStruct(wk.shape, F32), jax.ShapeDtypeStruct(wv.shape, F32),
                   jax.ShapeDtypeStruct((1, Q_RANK), F32), jax.ShapeDtypeStruct((1, KV_RANK), F32)],
        compiler_params=_cparams(("arbitrary",)),
    )(dq, dk, dv, proj, proj, cqn, ckvn, cos_m, sin_m, gq2, gkv2, wq, wk, wv)


def _ret_prep(proj, cos_r, sin_r):
    s = proj.shape[0]
    tr = _tile(s, 256)

    def body(q_ref, k_ref, cos_ref, sin_ref, qo_ref, ko_ref):
        cos, sin = cos_ref[...], sin_ref[...]
        q, k = q_ref[...], k_ref[...]
        qo_ref[...] = (q * cos + _rot_ret(q) * sin).astype(BF16)
        ko_ref[...] = ((k * cos + _rot_ret(k) * sin) * (HEAD ** -0.5)).astype(BF16)

    rowb = pl.BlockSpec((tr, 256), lambda i: (i, 0))
    return pl.pallas_call(
        body, name="ret_prep", grid=(s // tr,),
        in_specs=[pl.BlockSpec((tr, 256), lambda i: (i, OFF_RQ // 2)), pl.BlockSpec((tr, 256), lambda i: (i, OFF_RK // 2)), rowb, rowb],
        out_specs=[rowb, rowb], out_shape=[jax.ShapeDtypeStruct((s, 256), BF16)] * 2,
        compiler_params=_cparams(("parallel",)),
    )(proj, proj, cos_r, sin_r)


def _ret_prep_bwd(dq, dk, cos_r, sin_r):
    s = dq.shape[0]
    tr = _tile(s, 256)

    def body(dq_ref, dk_ref, cos_ref, sin_ref, qo_ref, ko_ref):
        cos, sin = cos_ref[...], sin_ref[...]
        q, k = dq_ref[...], dk_ref[...] * (HEAD ** -0.5)
        qo_ref[...] = (q * cos + _rot_ret_t(q * sin)).astype(BF16)
        ko_ref[...] = (k * cos + _rot_ret_t(k * sin)).astype(BF16)

    rowb = pl.BlockSpec((tr, 256), lambda i: (i, 0))
    return pl.pallas_call(
        body, name="ret_prep_bwd", grid=(s // tr,), in_specs=[rowb] * 4, out_specs=[rowb, rowb],
        out_shape=[jax.ShapeDtypeStruct((s, 256), BF16)] * 2, compiler_params=_cparams(("parallel",)),
    )(dq, dk, cos_r, sin_r)


_LOG_GAMMA = [float(np.log1p(-np.float32(2.0) ** np.float32(-5.0 - h))) for h in range(N_HEADS)]
_MLA_SCALE = float((HEAD + ROPE_DIM) ** -0.5)
_QK_SCALE = float(HEAD ** -0.5)
KEY_BLOCKS = 4
QB = 256


def _split2(x):
    h = x.astype(BF16)
    return h, (x - h.astype(F32)).astype(BF16)


def _dot2(x, u):
    h, lo = _split2(x)
    return _dot(h, u) + _dot(lo, u)


def _head_pick(block, head, axis):
    idx = lax.broadcasted_iota(jnp.int32, block.shape, axis)
    return jnp.sum(jnp.where(idx == head, block, 0.0), axis=axis, keepdims=True)


def _log_gamma_of(head):
    lg = jnp.float32(_LOG_GAMMA[3])
    for h in (2, 1, 0):
        lg = jnp.where(head == h, jnp.float32(_LOG_GAMMA[h]), lg)
    return lg


def _mixer_specs(mode, s, q_off, k_off, v_off):
    nhb = 2
    bw = 2 * LANES if mode == "mla" else LANES
    nsub = KEY_BLOCKS if (s // TQ) % KEY_BLOCKS == 0 else 1
    q_spec = pl.BlockSpec((QB, bw), lambda p, i: (i, q_off + p))
    k_spec = pl.BlockSpec((s, bw), lambda p, i: (0, k_off + p))
    v_spec = pl.BlockSpec((s, bw), lambda p, i: (0, v_off + p))
    return nhb, N_HEADS // nhb, nsub, q_spec, k_spec, v_spec


def _mixer_geometry(mode, i, nsub):
    w = TQ * nsub
    row = lax.broadcasted_iota(jnp.int32, (QB, w), 0)
    col = lax.broadcasted_iota(jnp.int32, (QB, w), 1)
    nfull = (i * QB) // w
    dist = col - row
    if mode in ("fox", "sb"):
        rel = dist
    else:
        rel = col - (row | (CHUNK - 1))

    def visible(c):
        off = c * w - i * QB
        return (rel + off) < 0 if mode == "sb" else (rel + off) <= 0

    return nfull, dist, visible


class _SideJob:
    def __init__(self, inputs, out_shape, n_sems, sends, recvs):
        self.inputs, self.out_shape, self.n_sems, self.sends, self.recvs = list(inputs), list(out_shape), n_sems, sends, recvs


def _carry_side_job(body, n_in, n_out, side, n_steps):
    if side is None:
        return body
    si, so = len(side.inputs), len(side.out_shape)

    def at(corner):
        ok = pl.program_id(0) == corner[0]
        for d in range(1, len(n_steps)):
            ok = ok & (pl.program_id(d) == corner[d])
        return ok

    def wrapped(*refs):
        ins, s_ins = refs[:n_in], refs[n_in:n_in + si]
        outs, s_outs = refs[n_in + si:n_in + si + n_out], refs[n_in + si + n_out:n_in + si + n_out + so]
        scratch, send, recv = refs[n_in + si + n_out + so:-2], refs[-2], refs[-1]

        @pl.when(at([0] * len(n_steps)))
        def _():
            for cp in side.sends(s_ins, s_outs, send, recv):
                cp.start()

        body(*ins, *outs, *scratch)

        @pl.when(at([n - 1 for n in n_steps]))
        def _():
            for cp in side.recvs(s_ins, s_outs, send, recv):
                cp.wait_recv()
            for cp in side.sends(s_ins, s_outs, send, recv):
                cp.wait_send()

    return wrapped


def _side_specs(side):
    if side is None:
        return [], [], []
    hbm = pl.BlockSpec(memory_space=pl.ANY)
    return ([hbm] * len(side.inputs), [hbm] * len(side.out_shape),
            [pltpu.SemaphoreType.DMA((side.n_sems,)), pltpu.SemaphoreType.DMA((side.n_sems,))])


def _mixer_fwd(mode, qa, q_off, ka, k_off, va, v_off, *, cum_col=None, cum_row=None, side=None):
    s = qa.shape[0]
    nq = s // QB
    nhb, nblk, nsub, q_spec, k_spec, v_spec = _mixer_specs(mode, s, q_off, k_off, v_off)
    w = TQ * nsub
    softmax = mode in ("fox", "mla")

    def body(*refs):
        refs = list(refs)
        q_ref, k_ref, v_ref = refs[:3]
        refs = refs[3:]
        if mode == "fox":
            cc_ref, cr_ref = refs[:2]
            refs = refs[2:]
        o_ref = refs[0]
        st_ref = refs[1]
        p = pl.program_id(0)
        i = pl.program_id(1)
        nfull, dist, visible = _mixer_geometry(mode, i, nsub)
        lane = _lane((1, LANES))
        heads = [nhb * p + hh for hh in range(nhb)]
        wide = mode == "mla"
        q_scale = _QK_SCALE if mode in ("fox", "sb") else 1.0
        cols = [slice(hh * LANES, (hh + 1) * LANES) if wide else slice(None) for hh in range(nhb)]
        if wide:
            qs = [q_ref[:, cols[hh]] for hh in range(nhb)]
        else:
            qf = q_ref[...].astype(F32) * q_scale
            qs = [jnp.where((lane // HEAD) == hh, qf, 0.0).astype(BF16) for hh in range(nhb)]
        if mode == "fox":
            cqs = [_head_pick(cc_ref[...], h, 1) for h in heads]
        if mode == "sb":
            r1 = lax.broadcasted_iota(jnp.int32, (TQ, TQ), 0)
            c1 = lax.broadcasted_iota(jnp.int32, (TQ, TQ), 1)
            u_after = (r1 > c1).astype(BF16)

        def chunk(c):
            return pl.ds(pl.multiple_of(c * w, w), w)

        def scores(c):
            js = chunk(c)
            return tuple(_dot_nt(qs[hh], k_ref[js, cols[hh]]) for hh in range(nhb))

        def head_step(hh, c, js, sc, vj, carry, last):
            if softmax:
                m, l, acc = carry
                if mode == "fox":
                    ck = _head_pick(cr_ref[:, js], heads[hh], 0)
                    sc = sc + (cqs[hh] - ck)
                else:
                    sc = sc * _MLA_SCALE
                if last:
                    sc = jnp.where(visible(c), sc, NEG)
                m_new = jnp.maximum(m, jnp.max(sc, axis=-1, keepdims=True))
                alpha = jnp.exp(m - m_new)
                pr = jnp.exp(sc - m_new)
                l = alpha * l + jnp.sum(pr, axis=-1, keepdims=True)
                acc = alpha * acc + _dot(pr.astype(BF16), vj)
                return m_new, l, acc
            run, acc = carry
            z = sc
            log_beta = jnp.minimum(z, 0.0) - jnp.log(1.0 + jnp.exp(-jnp.abs(z)))
            log_stay = log_beta - z
            if last:
                vis = visible(c)
                log_stay = jnp.where(vis, log_stay, 0.0)
            parts = [None] * nsub
            for b in reversed(range(nsub)):
                ls_b = log_stay[:, b * TQ:(b + 1) * TQ]
                parts[b] = _dot2(ls_b, u_after) + run
                run = run + jnp.sum(ls_b, axis=-1, keepdims=True)
            later = parts[0] if nsub == 1 else jnp.concatenate(parts, axis=1)
            wgt = jnp.exp(log_beta + later)
            if last:
                wgt = jnp.where(vis, wgt, 0.0)
            return run, acc + _dot(wgt.astype(BF16), vj)

        def step(c, c_next, state, last):
            scs, carries = state
            nxt = scores(c_next) if c_next is not None else None
            js = chunk(c)
            return nxt, tuple(head_step(hh, c, js, scs[hh], v_ref[js, cols[hh]], carries[hh], last) for hh in range(nhb))

        zero_acc = jnp.zeros((QB, LANES), F32)
        zero1 = jnp.zeros((QB, 1), F32)
        if softmax:
            init = tuple((jnp.full((QB, 1), NEG, F32), zero1, zero_acc) for _ in range(nhb))
        else:
            init = tuple((zero1, zero_acc) for _ in range(nhb))
        if mode == "sb":
            state = step(nfull, jnp.maximum(nfull - 1, 0), (scores(nfull), init), True)
            _, carries = lax.fori_loop(0, nfull, lambda t, st: step(nfull - 1 - t, jnp.maximum(nfull - 2 - t, 0), st, False), state)
        else:
            state = lax.fori_loop(0, nfull, lambda c, st: step(c, c + 1, st, False), (scores(0), init))
            _, carries = step(nfull, None, state, True)
        if softmax:
            outs = [acc / l for (m, l, acc) in carries]
            stats = [m + jnp.log(l) for (m, l, acc) in carries]
        else:
            outs, stats = [acc for (run, acc) in carries], [run for (run, acc) in carries]
        hm0 = (lane // HEAD) == 0
        pick = lambda a: jnp.where(hm0, a[0], a[1])
        if wide:
            for hh in range(nhb):
                o_ref[:, cols[hh]] = outs[hh]
        else:
            o_ref[...] = pick(outs)
        st_ref[0] = pick(stats)

    in_specs = [q_spec, k_spec, v_spec]
    args = [qa, ka, va]
    if mode == "fox":
        in_specs += [pl.BlockSpec((QB, LANES), lambda p, i: (i, 0)), pl.BlockSpec((8, s), lambda p, i: (0, 0))]
        args += [cum_col, cum_row]
    bw = 2 * LANES if mode == "mla" else LANES
    out_specs = [pl.BlockSpec((QB, bw), lambda p, i: (i, p))]
    out_shape = [jax.ShapeDtypeStruct((s, nblk * bw), F32)]
    out_specs.append(pl.BlockSpec((1, QB, LANES), lambda p, i: (p, i, 0)))
    out_shape.append(jax.ShapeDtypeStruct((nblk, s, LANES), F32))
    side_in, side_out, side_scratch = _side_specs(side)
    res = pl.pallas_call(
        _carry_side_job(body, len(args), len(out_shape), side, (nblk, nq)), name=mode + "_fwd", grid=(nblk, nq),
        in_specs=in_specs + side_in, out_specs=out_specs + side_out,
        out_shape=out_shape + ([] if side is None else side.out_shape), scratch_shapes=side_scratch,
        compiler_params=_cparams(("parallel", "parallel") if side is None else ("arbitrary", "arbitrary")),
    )(*args, *([] if side is None else side.inputs))
    return (res[0], res[1]) if side is None else (res[0], res[1], res[2:])


def _mixer_bwd(mode, qa, q_off, ka, k_off, va, v_off, o, do, *, stat=None, cum_col=None, cum_row=None, side=None):
    s = qa.shape[0]
    nq = s // QB
    nhb, nblk, nsub, q_spec, k_spec, v_spec = _mixer_specs(mode, s, q_off, k_off, v_off)
    w = TQ * nsub
    softmax = mode in ("fox", "mla")

    def body(*refs):
        refs = list(refs)
        q_ref, k_ref, v_ref, o_ref, do_ref = refs[:5]
        refs = refs[5:]
        st_ref = refs[0]
        refs = refs[1:]
        if mode == "fox":
            cc_ref, cr_ref = refs[:2]
            refs = refs[2:]
        dq_ref, dk_ref, dv_ref = refs[:3]
        dck_ref, drs_ref = refs[3:5] if mode == "fox" else (None, None)
        p = pl.program_id(0)
        i = pl.program_id(1)

        @pl.when(i == 0)
        def _():
            dk_ref[...] = jnp.zeros_like(dk_ref)
            dv_ref[...] = jnp.zeros_like(dv_ref)
            if mode == "fox":
                dck_ref[...] = jnp.zeros_like(dck_ref)

        nfull, dist, visible = _mixer_geometry(mode, i, nsub)
        lane = _lane((1, LANES))
        heads = [nhb * p + hh for hh in range(nhb)]
        dov = do_ref[...]
        wide = mode == "mla"
        q_scale = _QK_SCALE if mode in ("fox", "sb") else 1.0
        cols = [slice(hh * LANES, (hh + 1) * LANES) if wide else slice(None) for hh in range(nhb)]
        if wide:
            prod = dov * o_ref[...]
            qs = [q_ref[:, cols[hh]] for hh in range(nhb)]
            dos = [dov[:, cols[hh]].astype(BF16) for hh in range(nhb)]
            deltas = [jnp.sum(prod[:, cols[hh]], axis=-1, keepdims=True) for hh in range(nhb)]
        else:
            qf = q_ref[...].astype(F32) * q_scale
            prod = dov * o_ref[...]
            hms = [(lane // HEAD) == hh for hh in range(nhb)]
            qs = [jnp.where(hm, qf, 0.0).astype(BF16) for hm in hms]
            dos = [jnp.where(hm, dov, 0.0).astype(BF16) for hm in hms]
            deltas = [jnp.sum(jnp.where(hm, prod, 0.0), axis=-1, keepdims=True) for hm in hms]
        st = st_ref[0]
        stats = [st[:, hh * HEAD:hh * HEAD + 1] for hh in range(nhb)]
        if mode == "fox":
            cqs = [_head_pick(cc_ref[...], h, 1) for h in heads]
        if mode == "sb":
            r1 = lax.broadcasted_iota(jnp.int32, (TQ, TQ), 0)
            c1 = lax.broadcasted_iota(jnp.int32, (TQ, TQ), 1)
            u_upto = (r1 <= c1).astype(BF16)
            u_before = (r1 < c1).astype(BF16)

        def chunk(c):
            return pl.ds(pl.multiple_of(c * w, w), w)

        def scores(c):
            js = chunk(c)
            if mode == "sb":
                return tuple((_dot_nt(qs[hh], k_ref[js, cols[hh]]), None) for hh in range(nhb))
            return tuple((_dot_nt(qs[hh], k_ref[js, cols[hh]]), _dot_nt(dos[hh], v_ref[js, cols[hh]])) for hh in range(nhb))

        def emit(hh, js, ds_b, pr_b, dq):
            dk_ref[js, cols[hh]] += _dot_tn(ds_b, qs[hh])
            dv_ref[js, cols[hh]] += _dot_tn(pr_b, dos[hh])
            return dq + _dot(ds_b, k_ref[js, cols[hh]])

        def head_step(hh, c, js, sc_dp, carry, last):
            sc, dp = sc_dp
            if dp is None:
                dp = _dot_nt(dos[hh], v_ref[js, cols[hh]])
            if softmax:
                dq, rsum = carry
                if mode == "fox":
                    ck = _head_pick(cr_ref[:, js], heads[hh], 0)
                    sc = sc + (cqs[hh] - ck)
                else:
                    sc = sc * _MLA_SCALE
                if last:
                    sc = jnp.where(visible(c), sc, NEG)
                pr = jnp.exp(sc - stats[hh])
                ds = pr * (dp - deltas[hh])
                if mode == "fox":
                    dck_ref[0, hh:hh + 1, js] += jnp.sum(ds, axis=0, keepdims=True)
                    rsum = rsum + jnp.sum(ds, axis=-1, keepdims=True)
                if mode == "mla":
                    ds = ds * _MLA_SCALE
                return emit(hh, js, ds.astype(BF16), pr.astype(BF16), dq), rsum
            seen, gsum, dq = carry
            z = sc
            log_beta = jnp.minimum(z, 0.0) - jnp.log(1.0 + jnp.exp(-jnp.abs(z)))
            log_stay = log_beta - z
            if last:
                vis = visible(c)
                log_stay = jnp.where(vis, log_stay, 0.0)
            parts = []
            for b in range(nsub):
                ls_b = log_stay[:, b * TQ:(b + 1) * TQ]
                parts.append((stats[hh] - seen) - _dot2(ls_b, u_upto))
                seen = seen + jnp.sum(ls_b, axis=-1, keepdims=True)
            later = parts[0] if nsub == 1 else jnp.concatenate(parts, axis=1)
            wgt = jnp.exp(log_beta + later)
            if last:
                wgt = jnp.where(vis, wgt, 0.0)
            g = dp * wgt
            parts = []
            for b in range(nsub):
                g_b = g[:, b * TQ:(b + 1) * TQ]
                parts.append(gsum + _dot2(g_b, u_before))
                gsum = gsum + jnp.sum(g_b, axis=-1, keepdims=True)
            before = parts[0] if nsub == 1 else jnp.concatenate(parts, axis=1)
            beta = jnp.exp(log_beta)
            dz = g * (1.0 - beta) - beta * before
            if last:
                dz = jnp.where(vis, dz, 0.0)
            return seen, gsum, emit(hh, js, dz.astype(BF16), wgt.astype(BF16), dq)

        def step(c, c_next, state, last):
            scs, carries = state
            nxt = scores(c_next) if c_next is not None else None
            js = chunk(c)
            return nxt, tuple(head_step(hh, c, js, scs[hh], carries[hh], last) for hh in range(nhb))

        zero_acc = jnp.zeros((QB, LANES), F32)
        zero1 = jnp.zeros((QB, 1), F32)
        if softmax:
            init = tuple((zero_acc, zero1) for _ in range(nhb))
        else:
            init = tuple((zero1, zero1, zero_acc) for _ in range(nhb))
        state = lax.fori_loop(0, nfull, lambda c, st: step(c, c + 1, st, False), (scores(0), init))
        _, carries = step(nfull, None, state, True)
        if softmax:
            dqs = [dq for (dq, rsum) in carries]
        else:
            dqs = [dq for (seen, gsum, dq) in carries]
        hm0 = (lane // HEAD) == 0
        if wide:
            for hh in range(nhb):
                dq_ref[:, cols[hh]] = dqs[hh]
        else:
            dq_ref[...] = jnp.where(hm0, dqs[0], dqs[1]) * q_scale
        if mode == "fox":
            drs_ref[0] = jnp.where(hm0, carries[0][1], carries[1][1])

    bw = 2 * LANES if mode == "mla" else LANES
    pair_blk = pl.BlockSpec((QB, bw), lambda p, i: (i, p))
    full_blk = pl.BlockSpec((s, bw), lambda p, i: (0, p))
    stat_blk = pl.BlockSpec((1, QB, LANES), lambda p, i: (p, i, 0))
    in_specs = [q_spec, k_spec, v_spec, pair_blk, pair_blk]
    args = [qa, ka, va, o, do]
    in_specs.append(stat_blk)
    args.append(stat)
    if mode == "fox":
        in_specs += [pl.BlockSpec((QB, LANES), lambda p, i: (i, 0)), pl.BlockSpec((8, s), lambda p, i: (0, 0))]
        args += [cum_col, cum_row]
    out_specs = [pair_blk, full_blk, full_blk]
    out_shape = [jax.ShapeDtypeStruct((s, nblk * bw), F32)] * 3
    if mode == "fox":
        out_specs += [pl.BlockSpec((1, 8, s), lambda p, i: (p, 0, 0)), stat_blk]
        out_shape += [jax.ShapeDtypeStruct((2, 8, s), F32), jax.ShapeDtypeStruct((2, s, LANES), F32)]
    side_in, side_out, side_scratch = _side_specs(side)
    res = pl.pallas_call(
        _carry_side_job(body, len(args), len(out_shape), side, (nblk, nq)), name=mode + "_bwd", grid=(nblk, nq),
        in_specs=in_specs + side_in, out_specs=out_specs + side_out,
        out_shape=out_shape + ([] if side is None else side.out_shape), scratch_shapes=side_scratch,
        compiler_params=_cparams(("parallel", "arbitrary") if side is None else ("arbitrary", "arbitrary")),
    )(*args, *([] if side is None else side.inputs))
    return res if side is None else (*res[:len(out_shape)], res[len(out_shape):])


def _ret_geometry(p):
    lane = _lane((1, LANES))
    lg_lane = jnp.where(lane < HEAD, _log_gamma_of(2 * p), _log_gamma_of(2 * p + 1))
    a = lax.broadcasted_iota(jnp.int32, (TQ, 1), 0).astype(F32)
    row = lax.broadcasted_iota(jnp.int32, (TQ, TQ), 0)
    col = lax.broadcasted_iota(jnp.int32, (TQ, TQ), 1)
    same_chunk_or_earlier = (col // CHUNK) <= (row // CHUNK)
    gap = jnp.abs(row - col).astype(F32)
    decays = [jnp.where(same_chunk_or_earlier, jnp.exp(_log_gamma_of(2 * p + hh) * gap), 0.0) for hh in range(2)]
    r = lax.broadcasted_iota(jnp.int32, (LANES, LANES), 0)
    c = lax.broadcasted_iota(jnp.int32, (LANES, LANES), 1)
    own_head = (r // HEAD) == (c // HEAD)
    return lane, lg_lane, a, decays, own_head


def _ret_fwd(qa, ka, va, v_off):
    s = qa.shape[0]
    nq = s // TQ

    def body(q_ref, k_ref, v_ref, o_ref, st_ref, state):
        p = pl.program_id(0)

        @pl.when(pl.program_id(1) == 0)
        def _():
            state[...] = jnp.zeros_like(state)

        lane, lg_lane, a, decays, own_head = _ret_geometry(p)
        q = q_ref[...].astype(F32)
        k = k_ref[...]
        v = v_ref[...]
        s_in = state[...]
        st_ref[0, 0] = s_in
        out = _dot((q * jnp.exp(lg_lane * (a + 1.0))).astype(BF16), s_in.astype(BF16))
        for hh in range(2):
            hm = (lane // HEAD) == hh
            qh = jnp.where(hm, q, 0.0).astype(BF16)
            inner = _dot((_dot_nt(qh, k) * decays[hh]).astype(BF16), v)
            out = out + jnp.where(hm, inner, 0.0)
        o_ref[...] = out
        k_tail = (k.astype(F32) * jnp.exp(lg_lane * (TQ - 1.0 - a))).astype(BF16)
        state[...] = jnp.exp(lg_lane * float(TQ)) * s_in + jnp.where(own_head, _dot_tn(k_tail, v), 0.0)

    blk = lambda off: pl.BlockSpec((TQ, LANES), lambda p, i: (i, off + p))
    return pl.pallas_call(
        body, name="ret_fwd", grid=(2, nq), in_specs=[blk(0), blk(0), blk(v_off)],
        out_specs=[blk(0), pl.BlockSpec((1, 1, LANES, LANES), lambda p, i: (p, i, 0, 0))],
        out_shape=[jax.ShapeDtypeStruct((s, 2 * LANES), F32), jax.ShapeDtypeStruct((2, nq, LANES, LANES), F32)],
        scratch_shapes=[pltpu.VMEM((LANES, LANES), F32)],
        compiler_params=_cparams(("parallel", "arbitrary")),
    )(qa, ka, va)


def _ret_bwd(qa, ka, va, v_off, states, do):
    s = qa.shape[0]
    nq = s // TQ

    def body(q_ref, k_ref, v_ref, st_ref, do_ref, dq_ref, dk_ref, dv_ref, dstate):
        p = pl.program_id(0)

        @pl.when(pl.program_id(1) == 0)
        def _():
            dstate[...] = jnp.zeros_like(dstate)

        lane, lg_lane, a, decays, own_head = _ret_geometry(p)
        q = q_ref[...].astype(F32)
        k = k_ref[...]
        kf = k.astype(F32)
        v = v_ref[...]
        dov = do_ref[...]
        s_in = st_ref[0, 0].astype(BF16)
        ds_next = dstate[...]
        ds_b = ds_next.astype(BF16)
        head_decay = jnp.exp(lg_lane * (a + 1.0))
        tail_decay = jnp.exp(lg_lane * (TQ - 1.0 - a))
        k_tail = (kf * tail_decay).astype(BF16)
        dq = _dot_nt(dov.astype(BF16), s_in) * head_decay
        dk = _dot_nt(v, ds_b) * tail_decay
        dv = _dot(k_tail, ds_b)
        for hh in range(2):
            hm = (lane // HEAD) == hh
            qh = jnp.where(hm, q, 0.0).astype(BF16)
            doh = jnp.where(hm, dov, 0.0).astype(BF16)
            att = (_dot_nt(qh, k) * decays[hh]).astype(BF16)
            datt = (_dot_nt(doh, v) * decays[hh]).astype(BF16)
            dv = dv + _dot_tn(att, doh)
            dk = dk + _dot_tn(datt, qh)
            dq = dq + jnp.where(hm, _dot(datt, k), 0.0)
        dq_ref[...] = dq
        dk_ref[...] = dk
        dv_ref[...] = dv
        q_head = (q * head_decay).astype(BF16)
        dstate[...] = jnp.exp(lg_lane * float(TQ)) * ds_next + jnp.where(own_head, _dot_tn(q_head, dov.astype(BF16)), 0.0)

    blk = lambda off: pl.BlockSpec((TQ, LANES), lambda p, i: (nq - 1 - i, off + p))
    return pl.pallas_call(
        body, name="ret_bwd", grid=(2, nq),
        in_specs=[blk(0), blk(0), blk(v_off), pl.BlockSpec((1, 1, LANES, LANES), lambda p, i: (p, nq - 1 - i, 0, 0)), blk(0)],
        out_specs=[blk(0)] * 3, out_shape=[jax.ShapeDtypeStruct((s, 2 * LANES), F32)] * 3,
        scratch_shapes=[pltpu.VMEM((LANES, LANES), F32)],
        compiler_params=_cparams(("parallel", "arbitrary")),
    )(qa, ka, va, states, do)


def _seg_mean_matrix():
    r = lax.broadcasted_iota(jnp.int32, (GROUP, GROUP), 0)
    c = lax.broadcasted_iota(jnp.int32, (GROUP, GROUP), 1)
    return jnp.where((r // HEAD) == (c // HEAD), 1.0 / HEAD, 0.0).astype(F32)


def _sigmoid(x):
    return 1.0 / (1.0 + jnp.exp(-x))


def _mix_post(oa, ob, oc, od, proj, g):
    s = oa.shape[0]
    tr = _tile(s, 256)

    def body(a_ref, b_ref, c_ref, d_ref, rg_ref, g_ref, o_ref):
        gv = g_ref[...]
        o_ref[:, 0:GROUP] = _rms(a_ref[...], gv[:, 0:GROUP]).astype(BF16)
        o_ref[:, GROUP:2 * GROUP] = _rms(b_ref[...], gv[:, GROUP:2 * GROUP]).astype(BF16)
        seg = _seg_mean_matrix()
        c = c_ref[...]
        cen = c - _dot_exact(c, seg)
        n = cen * lax.rsqrt(_dot_exact(cen * cen, seg) + EPS)
        rg = rg_ref[...]
        o_ref[:, 2 * GROUP:3 * GROUP] = (n * gv[:, 2 * GROUP:3 * GROUP] * (rg * _sigmoid(rg))).astype(BF16)
        o_ref[:, 3 * GROUP:] = _rms(d_ref[...], gv[:, 3 * GROUP:]).astype(BF16)

    blk = pl.BlockSpec((tr, GROUP), lambda i: (i, 0))
    return pl.pallas_call(
        body, name="mix_post", grid=(s // tr,),
        in_specs=[blk] * 4 + [pl.BlockSpec((tr, GROUP), lambda i: (i, OFF_RG // 2)), pl.BlockSpec((1, D_MODEL), lambda i: (0, 0))],
        out_specs=pl.BlockSpec((tr, D_MODEL), lambda i: (i, 0)), out_shape=jax.ShapeDtypeStruct((s, D_MODEL), BF16),
        compiler_params=_cparams(("parallel",)),
    )(oa, ob, oc, od, proj, g.reshape(1, D_MODEL))


def _mix_post_bwd(dmixed, oa, ob, oc, od, proj, g):
    s = oa.shape[0]
    tr = _tile(s, 256)

    def body(dm_ref, a_ref, b_ref, c_ref, d_ref, rg_ref, g_ref, da_ref, db_ref, dc_ref, dd_ref, drg_ref, dg_ref):
        @pl.when(pl.program_id(0) == 0)
        def _():
            dg_ref[...] = jnp.zeros_like(dg_ref)

        gv = g_ref[...]
        dm = dm_ref[...]
        for k, (x_ref, dx_ref) in enumerate(((a_ref, da_ref), (b_ref, db_ref), (None, None), (d_ref, dd_ref))):
            if x_ref is None:
                continue
            cols = slice(k * GROUP, (k + 1) * GROUP)
            dx, gterm = _rms_bwd(x_ref[...], gv[:, cols], dm[:, cols])
            dx_ref[...] = dx
            dg_ref[:, cols] += jnp.sum(gterm, axis=0, keepdims=True)
        cols = slice(2 * GROUP, 3 * GROUP)
        seg = _seg_mean_matrix()
        c = c_ref[...]
        cen = c - _dot_exact(c, seg)
        rstd = lax.rsqrt(_dot_exact(cen * cen, seg) + EPS)
        n = cen * rstd
        rg = rg_ref[...]
        sg = _sigmoid(rg)
        gate = rg * sg
        dy = dm[:, cols]
        gc = gv[:, cols]
        dn = dy * gc * gate
        dg_ref[:, cols] += jnp.sum(dy * n * gate, axis=0, keepdims=True)
        drg_ref[...] = (dy * n * gc * (sg * (1.0 + rg * (1.0 - sg)))).astype(BF16)
        dc_ref[...] = rstd * (dn - _dot_exact(dn, seg) - n * _dot_exact(dn * n, seg))

    blk = pl.BlockSpec((tr, GROUP), lambda i: (i, 0))
    gsp = pl.BlockSpec((1, D_MODEL), lambda i: (0, 0))
    return pl.pallas_call(
        body, name="mix_post_bwd", grid=(s // tr,),
        in_specs=[pl.BlockSpec((tr, D_MODEL), lambda i: (i, 0))] + [blk] * 4 + [pl.BlockSpec((tr, GROUP), lambda i: (i, OFF_RG // 2)), gsp],
        out_specs=[blk] * 5 + [gsp],
        out_shape=[jax.ShapeDtypeStruct((s, GROUP), F32)] * 4 + [jax.ShapeDtypeStruct((s, GROUP), BF16), jax.ShapeDtypeStruct((1, D_MODEL), F32)],
        compiler_params=_cparams(("arbitrary",)),
    )(dmixed, oa, ob, oc, od, proj, g.reshape(1, D_MODEL))


def _pack_w_in(w):
    z = lambda n: jnp.zeros((w.shape[0], n), w.dtype)
    misc = jnp.concatenate([w[:, 768:772], z(KR_LANE - N_HEADS), w[:, 1156:1188], z(LANES - KR_LANE - ROPE_DIM)], axis=1)
    return jnp.concatenate([w[:, 0:768], w[:, 772:1028], w[:, 1188:2980], w[:, 1028:1156], misc], axis=1)


def _unpack_dw_in(d):
    m = OFF_MISC * LANES
    return jnp.concatenate([d[:, 0:768], d[:, m:m + N_HEADS], d[:, 768:1024], d[:, OFF_CKV * LANES:m],
                            d[:, m + KR_LANE:m + KR_LANE + ROPE_DIM], d[:, 1024:OFF_CKV * LANES]], axis=1)


def _pack_w_q(w):
    return jnp.pad(w.reshape(Q_RANK, N_HEADS, HEAD + ROPE_DIM), ((0, 0), (0, 0), (0, LANES - HEAD - ROPE_DIM))).reshape(Q_RANK, 4 * LANES)


def _unpack_dw_q(d):
    return d.reshape(Q_RANK, N_HEADS, LANES)[:, :, :HEAD + ROPE_DIM].reshape(Q_RANK, N_HEADS * (HEAD + ROPE_DIM))


def _pack_w_kv(w):
    w4 = w.reshape(KV_RANK, N_HEADS, 2 * HEAD)
    widen = lambda a: jnp.pad(a, ((0, 0), (0, 0), (0, LANES - HEAD))).reshape(KV_RANK, N_HEADS * LANES)
    return widen(w4[:, :, :HEAD]), widen(w4[:, :, HEAD:])


def _unpack_dw_kv(dk, dv):
    narrow = lambda a: a.reshape(KV_RANK, N_HEADS, LANES)[:, :, :HEAD]
    return jnp.concatenate([narrow(dk), narrow(dv)], axis=2).reshape(KV_RANK, 2 * N_HEADS * HEAD)


def _narrow_heads(a):
    return a.reshape(a.shape[0], N_HEADS, LANES)[:, :, :HEAD].reshape(a.shape[0], N_HEADS * HEAD)


def _widen_heads(a):
    return jnp.pad(a.reshape(a.shape[0], N_HEADS, HEAD), ((0, 0), (0, 0), (0, LANES - HEAD))).reshape(a.shape[0], N_HEADS * LANES)


def _layer_fwd(x, lw, tabs, tag, side=None, fox_side=None, late_weights=None, h1=None, next_gain=None):
    cos_m, sin_m, cos_r, sin_r = tabs
    if h1 is None:
        h1 = _norm_fwd(x, lw["g_mix_pre"], name=tag + "pre_norm")
    proj, projb = _matmul(h1, lw["w_in"], name=tag + "in_proj", also_bf16=True)
    bias_row = jnp.pad(lw["b_forget"], (FF_LANE, LANES - N_HEADS - FF_LANE)).reshape(1, LANES)
    cum_col, cum_row = _fox_cum(proj, bias_row)
    oa, lse_a, *fox_carried = _mixer_fwd("fox", projb, OFF_FQ, projb, OFF_FK, projb, OFF_FV, cum_col=cum_col, cum_row=cum_row,
                                         side=fox_side)
    if late_weights is not None:
        lw = {**lw, **late_weights(fox_carried[0])}
    qm, km, vm, cqn, ckvn = _mla_prep(proj, cos_m, sin_m, lw["g_q_lora"], lw["g_kv_lora"], lw["wq"], lw["wk"], lw["wv"])
    ob_wide, lse_b = _mixer_fwd("mla", qm, 0, km, 0, vm, 0)
    ob = _narrow_heads(ob_wide)
    qr, kr = _ret_prep(proj, cos_r, sin_r)
    oc, ret_states = _ret_fwd(qr, kr, projb, OFF_RV)
    od, tot_d, *carried = _mixer_fwd("sb", projb, OFF_SQ, projb, OFF_SK, projb, OFF_SV, side=side)
    mixed = _mix_post(oa, ob, oc, od, proj, lw["g_mix_out"])
    mix = _matmul(mixed, lw["w_out"], name=tag + "out_proj")
    x1, h2 = _norm_fwd(mix, lw["g_mix_post"], name=tag + "mix_post_norm", resid=x, out_dtype=F32, next_gain=lw["g_ffn_pre"])
    u = _matmul(h2, lw["w_ffn_up"], name=tag + "ffn_up", relu2=True, out_dtype=BF16, col_blocks=True)
    f = _matmul(u, lw["w_ffn_down"], name=tag + "ffn_down")
    x2, h_next = None, None
    if next_gain is not None:
        x2, h_next = _norm_fwd(f, lw["g_ffn_post"], name=tag + "ffn_post_norm", resid=x1, out_dtype=F32, next_gain=next_gain)
    saved = dict(x=x, h1=h1, proj=proj, projb=projb, bias_row=bias_row, cum_col=cum_col, cum_row=cum_row, oa=oa, lse_a=lse_a,
                 qm=qm, km=km, vm=vm, cqn=cqn, ckvn=ckvn, ob=ob, ob_wide=ob_wide, lse_b=lse_b, qr=qr, kr=kr, ret_states=ret_states, oc=oc, od=od, tot_d=tot_d, mixed=mixed,
                 mix=mix, x1=x1, h2=h2, u=u, f=f)
    return x2, saved, lw, (carried[0] if carried else None), h_next


def _layer_bwd(dx2, lw, sv, tabs, tag, side=None, ffn_side=None, fox_side=None, post_given=None, then_prev=None):
    cos_m, sin_m, cos_r, sin_r = tabs
    g = {}
    if post_given is None:
        df, g["g_ffn_post"] = _norm_bwd(sv["f"], lw["g_ffn_post"], dx2, name=tag + "ffn_post_norm_bwd", out_dtype=BF16)
    else:
        df, g["g_ffn_post"] = post_given
    du_pre = _matmul(df, lw["w_ffn_down"], name=tag + "ffn_down_dx", tb=True, out_dtype=BF16, relu2_of=sv["u"], side=ffn_side)
    ffn_carried = None
    if ffn_side is not None:
        du_pre, ffn_carried = du_pre
    g["w_ffn_down"] = _matmul(sv["u"], df, name=tag + "ffn_down_dw", ta=True)
    dh2 = _matmul(du_pre, lw["w_ffn_up"], name=tag + "ffn_up_dx", tb=True, col_blocks=True)
    g["w_ffn_up"] = _matmul(sv["h2"], du_pre, name=tag + "ffn_up_dw", ta=True, col_blocks=True)
    dx1, g["g_ffn_pre"], dmix, g["g_mix_post"] = _norm_bwd(sv["x1"], lw["g_ffn_pre"], dh2, name=tag + "ffn_pre_norm_bwd", add=dx2,
                                                           then=(sv["mix"], lw["g_mix_post"]))
    dmixed = _matmul(dmix, lw["w_out"], name=tag + "out_proj_dx", tb=True)
    g["w_out"] = _matmul(sv["mixed"], dmix, name=tag + "out_proj_dw", ta=True)
    proj, projb = sv["proj"], sv["projb"]
    doa, dob, doc, dod, drg, g["g_mix_out"] = _mix_post_bwd(dmixed, sv["oa"], sv["ob"], sv["oc"], sv["od"], proj, lw["g_mix_out"])
    dfq, dfk, dfv, dck, drs, *fox_carried = _mixer_bwd(
        "fox", projb, OFF_FQ, projb, OFF_FK, projb, OFF_FV, sv["oa"], doa, stat=sv["lse_a"], cum_col=sv["cum_col"],
        cum_row=sv["cum_row"], side=None if fox_side is None else fox_side(g))
    dqm, dkm, dvm = _mixer_bwd("mla", sv["qm"], 0, sv["km"], 0, sv["vm"], 0, sv["ob_wide"], _widen_heads(dob), stat=sv["lse_b"])
    dcq, dckv, dkr, dwq, dwk, dwv, g["g_q_lora"], g["g_kv_lora"] = _mla_prep_bwd(
        dqm, dkm, dvm, proj, sv["cqn"], sv["ckvn"], cos_m, sin_m, lw["g_q_lora"], lw["g_kv_lora"], lw["wq"], lw["wk"], lw["wv"])
    dqr, dkr_ret, drv = _ret_bwd(sv["qr"], sv["kr"], projb, OFF_RV, sv["ret_states"], doc)
    drq, drk = _ret_prep_bwd(dqr, dkr_ret, cos_r, sin_r)
    if callable(side):
        side = side(g, ffn_carried, fox_carried[0] if fox_carried else None)
    dsq, dsk, dsv, *carried = _mixer_bwd("sb", projb, OFF_SQ, projb, OFF_SK, projb, OFF_SV, sv["od"], dod, stat=sv["tot_d"], side=side)
    dmisc, db_row = _fox_gate_bwd(dck, drs, proj, sv["bias_row"], dkr)
    b = lambda a: a.astype(BF16)
    dproj = jnp.concatenate([b(dfq), b(dfk), b(dfv), dcq, drq, drk, b(drv), drg, b(dsq), b(dsk), b(dsv), dckv, dmisc], axis=1)
    dh1 = _matmul(dproj, lw["w_in"], name=tag + "in_proj_dx", tb=True)
    g["w_in"] = _matmul(sv["h1"], dproj, name=tag + "in_proj_dw", ta=True)
    dx, g["g_mix_pre"], *prev_post = _norm_bwd(sv["x"], lw["g_mix_pre"], dh1, name=tag + "pre_norm_bwd", add=dx1, then=then_prev)
    g["b_forget"] = db_row[0, FF_LANE:FF_LANE + N_HEADS]
    g["wq"], g["wk"], g["wv"] = dwq, dwk, dwv
    return dx, g, (carried[0] if carried else None), (tuple(prev_post) if prev_post else None)


def _local_step(x, positions, layers, target):
    s = x.shape[0]
    tabs = _rope_tables(positions.reshape(s, 1))
    saved, h1 = [], None
    for li, lw in enumerate(layers):
        nxt = layers[li + 1]["g_mix_pre"] if li + 1 < len(layers) else None
        x, sv, _, _, h1 = _layer_fwd(x, lw, tabs, "l%d_" % li, h1=h1, next_gain=nxt)
        saved.append(sv)
    loss_row, dx, df, dg = _loss_head(saved[-1]["f"], layers[-1]["g_ffn_post"], saved[-1]["x1"], target)
    grads, post = [None] * len(layers), (df, dg)
    for li in reversed(range(len(layers))):
        prev = (saved[li - 1]["f"], layers[li - 1]["g_ffn_post"]) if li > 0 else None
        dx, grads[li], _, post = _layer_bwd(dx, layers[li], saved[li], tabs, "l%d_" % li, post_given=post, then_prev=prev)
    return loss_row[0, 0], dx, grads


def _adamw(w, g, m, v, *, name):
    r, c = w.shape
    tr = 256 if r % 256 == 0 else r
    blk = pl.BlockSpec((tr, c), lambda i: (i, 0))
    c1 = 1.0 - ADAM_B1 ** ADAM_STEP
    c2 = 1.0 - ADAM_B2 ** ADAM_STEP

    def body(w_ref, g_ref, m_ref, v_ref, d_ref, mo_ref, vo_ref):
        gv = g_ref[...]
        mn = ADAM_B1 * m_ref[...] + (1.0 - ADAM_B1) * gv
        vn = ADAM_B2 * v_ref[...] + (1.0 - ADAM_B2) * jnp.square(gv)
        mo_ref[...] = mn
        vo_ref[...] = vn
        d_ref[...] = -ADAM_LR * ((mn / c1) / (jnp.sqrt(vn / c2) + ADAM_EPS) + ADAM_WD * w_ref[...])

    return pl.pallas_call(
        body, name=name, grid=(r // tr,), in_specs=[blk] * 4, out_specs=[blk] * 3,
        out_shape=[jax.ShapeDtypeStruct((r, c), F32)] * 3, compiler_params=_cparams(("parallel",)),
    )(w, g, m, v)


BIG = ("w_in", "w_q_up", "w_kv_up", "w_out", "w_ffn_up", "w_ffn_down")
SMALL = ("g_mix_pre", "b_forget", "g_q_lora", "g_kv_lora", "g_mix_out", "g_mix_post", "g_ffn_pre", "g_ffn_post")
N_CHIPS = 4
ANY = pl.BlockSpec(memory_space=pl.ANY)


def _mesh_pos():
    return lax.axis_index("x"), lax.axis_index("y"), lax.axis_index("c")


def _other_chips(x, y):
    return [(1 - x, y), (x, 1 - y), (1 - x, 1 - y)]


def _rows_half(ref, half):
    h = ref.shape[-2] // 2
    return ref.at[(slice(None),) * (len(ref.shape) - 2) + (pl.ds(half * h, h), slice(None))]


def _remote(src, dst, send_sem, recv_sem, device):
    return pltpu.make_async_remote_copy(src_ref=src, dst_ref=dst, send_sem=send_sem, recv_sem=recv_sem, device_id=device,
                                        device_id_type=MESH)


def _comm_call(body, name, args, out_shape, n_sems):
    return pl.pallas_call(
        body, name=name, in_specs=[ANY] * len(args), out_specs=[ANY] * len(out_shape), out_shape=out_shape,
        scratch_shapes=[pltpu.SemaphoreType.DMA((n_sems,)), pltpu.SemaphoreType.DMA((n_sems,))],
        compiler_params=pltpu.CompilerParams(has_side_effects=True),
    )(*args)


def _run_side_job(side, name):
    si = len(side.inputs)

    def body(*refs):
        args = (refs[:si], refs[si:-2], refs[-2], refs[-1])
        sends = side.sends(*args)
        for cp in sends:
            cp.start()
        for cp in side.recvs(*args):
            cp.wait_recv()
        for cp in sends:
            cp.wait_send()

    return _comm_call(body, name, side.inputs, side.out_shape, side.n_sems)


def _gather_job(shards):
    n = len(shards)

    def copies(own_block, ins, outs, send_sems, recv_sems):
        x, y, c = _mesh_pos()
        return [_remote(_rows_half(ins[t], c), _rows_half(outs[t].at[2 * x + y if own_block else 2 * px + py], c),
                        send_sems.at[3 * t + j], recv_sems.at[3 * t + j], (px, py, c))
                for t in range(n) for j, (px, py) in enumerate(_other_chips(x, y))]

    return _SideJob(shards, [jax.ShapeDtypeStruct((N_CHIPS,) + a.shape, a.dtype) for a in shards], 3 * n,
                    functools.partial(copies, True), functools.partial(copies, False))


def _forward_halves(gathered):
    n = len(gathered)

    def body(*refs):
        bufs, send_sems, recv_sems = refs[n:2 * n], refs[-2], refs[-1]
        x, y, c = _mesh_pos()

        def d2d(t, j, block, half):
            region = _rows_half(bufs[t].at[block], half)
            return _remote(region, region, send_sems.at[3 * t + j], recv_sems.at[3 * t + j], (x, y, 1 - c))

        peers = list(enumerate(_other_chips(x, y)))
        sends = [d2d(t, j, 2 * px + py, c) for t in range(n) for j, (px, py) in peers]
        for cp in sends:
            cp.start()
        for t in range(n):
            for j, (px, py) in peers:
                d2d(t, j, 2 * px + py, 1 - c).wait_recv()
        for cp in sends:
            cp.wait_send()

    return pl.pallas_call(
        body, name="gather_forward", in_specs=[ANY] * n, out_specs=[ANY] * n,
        out_shape=[jax.ShapeDtypeStruct(g.shape, g.dtype) for g in gathered], input_output_aliases={t: t for t in range(n)},
        scratch_shapes=[pltpu.SemaphoreType.DMA((3 * n,)), pltpu.SemaphoreType.DMA((3 * n,))],
        compiler_params=pltpu.CompilerParams(has_side_effects=True),
    )(*gathered)


def _exchange_halves_job(gs):
    n = len(gs)

    def copies(ins, outs, send_sems, recv_sems):
        x, y, c = _mesh_pos()
        return [_remote(_rows_half(ins[t], 1 - c), outs[t], send_sems.at[t], recv_sems.at[t], (x, y, 1 - c)) for t in range(n)]

    out_shape = [jax.ShapeDtypeStruct(g.shape[:2] + (g.shape[2] // 2, g.shape[3]), g.dtype) for g in gs]
    return _SideJob(gs, out_shape, n, copies, copies)


def _pair_add(g, r, c_idx, *, name):
    nb, d, rows, cols = g.shape
    h = rows // 2
    tr = min(h, 512)
    nt = h // tr

    def body(c_ref, g_ref, r_ref, p_ref, pb_ref):
        s = g_ref[...] + r_ref[...]
        p_ref[...] = s
        pb_ref[...] = s.astype(BF16)

    blk = pl.BlockSpec((1, 1, tr, cols), lambda k, l, i, c_ref: (k, l, i, 0))
    return pl.pallas_call(
        body, name=name,
        grid_spec=pltpu.PrefetchScalarGridSpec(
            num_scalar_prefetch=1, grid=(nb, d, nt),
            in_specs=[pl.BlockSpec((1, 1, tr, cols), lambda k, l, i, c_ref: (k, l, c_ref[0] * nt + i, 0)), blk],
            out_specs=[blk, blk]),
        out_shape=[jax.ShapeDtypeStruct((nb, d, h, cols), F32), jax.ShapeDtypeStruct((nb, d, h, cols), BF16)],
        compiler_params=_cparams(("parallel", "parallel", "parallel")),
    )(c_idx, g, r)


def _exchange_chips_job(pbs):
    n = len(pbs)

    def copies(ins, outs, send_sems, recv_sems):
        x, y, c = _mesh_pos()
        return [_remote(ins[t].at[2 * px + py], outs[t].at[j], send_sems.at[3 * t + j], recv_sems.at[3 * t + j], (px, py, c))
                for t in range(n) for j, (px, py) in enumerate(_other_chips(x, y))]

    return _SideJob(pbs, [jax.ShapeDtypeStruct((3,) + p.shape[1:], p.dtype) for p in pbs], 3 * n, copies, copies)


def _chip_add(p, r, k_idx, *, name):
    _, d, h, cols = p.shape
    tr = min(h, 512)
    nt = h // tr

    def body(k_ref, p_ref, r_ref, o_ref):
        o_ref[0] = ((p_ref[0, 0] + r_ref[0, 0].astype(F32)) + r_ref[1, 0].astype(F32)) + r_ref[2, 0].astype(F32)

    return pl.pallas_call(
        body, name=name,
        grid_spec=pltpu.PrefetchScalarGridSpec(
            num_scalar_prefetch=1, grid=(d, nt),
            in_specs=[pl.BlockSpec((1, 1, tr, cols), lambda l, i, k_ref: (k_ref[0], l, i, 0)),
                      pl.BlockSpec((3, 1, tr, cols), lambda l, i, k_ref: (0, l, i, 0))],
            out_specs=pl.BlockSpec((1, tr, cols), lambda l, i, k_ref: (l, i, 0))),
        out_shape=jax.ShapeDtypeStruct((d, h, cols), F32), compiler_params=_cparams(("parallel", "parallel")),
    )(k_idx, p, r)


def _share_halves(qs):
    n = len(qs)

    def body(*refs):
        ins, outs, send_sems, recv_sems = refs[:n], refs[n:2 * n], refs[2 * n], refs[2 * n + 1]
        x, y, c = _mesh_pos()
        cps = [_remote(ins[t], outs[t], send_sems.at[t], recv_sems.at[t], (x, y, 1 - c)) for t in range(n)]
        for cp in cps:
            cp.start()
        for cp in cps:
            cp.wait_recv()
        for cp in cps:
            cp.wait_send()

    return _comm_call(body, "grad_pair_share", qs, [jax.ShapeDtypeStruct(q.shape, q.dtype) for q in qs], n)


def _all_reduce_small(v):
    r, cols = v.shape
    n_dev = 8

    def body(v_ref, o_ref, buf, send_sems, recv_sems):
        x, y, c = _mesh_pos()
        me = 4 * x + 2 * y + c
        buf[me] = v_ref[...]

        def peer(j):
            return (1 - x if j & 4 else x, 1 - y if j & 2 else y, 1 - c if j & 1 else c)

        def copy(j, slot):
            return pltpu.make_async_remote_copy(src_ref=v_ref, dst_ref=buf.at[slot], send_sem=send_sems.at[j - 1],
                                                recv_sem=recv_sems.at[j - 1], device_id=peer(j), device_id_type=MESH)

        sends = [copy(j, me) for j in range(1, n_dev)]
        for cp in sends:
            cp.start()
        for j in range(1, n_dev):
            px, py, pc = peer(j)
            copy(j, 4 * px + 2 * py + pc).wait_recv()
        for cp in sends:
            cp.wait_send()
        acc = buf[0]
        for d in range(1, n_dev):
            acc = acc + buf[d]
        o_ref[...] = acc

    vm = pl.BlockSpec(memory_space=pltpu.VMEM)
    return pl.pallas_call(
        body, name="small_all_reduce", in_specs=[vm], out_specs=vm, out_shape=jax.ShapeDtypeStruct((r, cols), F32),
        scratch_shapes=[pltpu.VMEM((n_dev, r, cols), F32), pltpu.SemaphoreType.DMA((n_dev - 1,)), pltpu.SemaphoreType.DMA((n_dev - 1,))],
        compiler_params=pltpu.CompilerParams(has_side_effects=True),
    )(v)


_COL_SHARDED = ("w_in", "w_q_up", "w_kv_up", "w_ffn_up")


def _shard_cols(blocks, a, b):
    c = blocks[0].shape[-1]
    out = []
    while a < b:
        k = a // c
        hi = min(b, (k + 1) * c)
        out.append(blocks[k][:, a - k * c:hi - k * c])
        a = hi
    return out


def _pack_w_in_shards(blocks):
    z = lambda n: [jnp.zeros((blocks[0].shape[0], n), blocks[0].dtype)]
    cols = lambda a, b: _shard_cols(blocks, a, b)
    return jnp.concatenate(cols(0, 768) + cols(772, 1028) + cols(1188, 2980) + cols(1028, 1156) + cols(768, 772)
                           + z(KR_LANE - N_HEADS) + cols(1156, 1188) + z(LANES - KR_LANE - ROPE_DIM), axis=1)


def _whole_layer(name, blocks):
    if name in _COL_SHARDED:
        return jnp.concatenate([blocks[k] for k in range(N_CHIPS)], axis=1)
    return blocks.reshape(N_CHIPS * blocks.shape[1], blocks.shape[2])


def _split_layer(name, whole):
    if name in _COL_SHARDED:
        c = whole.shape[1] // N_CHIPS
        return jnp.stack([whole[:, k * c:(k + 1) * c] for k in range(N_CHIPS)])
    return whole.reshape(N_CHIPS, whole.shape[0] // N_CHIPS, whole.shape[1])


def _small_to_rows(d):
    v = jnp.concatenate([d[k].astype(F32).reshape(-1) for k in SMALL])
    rows = -(-v.shape[0] // (8 * LANES)) * 8
    return jnp.pad(v, (0, rows * LANES - v.shape[0])).reshape(rows, LANES)


def _small_from_rows(rows, shapes):
    v = rows.reshape(-1)
    out, o = {}, 0
    for k in SMALL:
        sz = int(np.prod(shapes[k]))
        out[k] = v[o:o + sz].reshape(shapes[k])
        o += sz
    return out


_ARG_NAMES = ("x", "positions", "g_mix_pre", "w_in", "b_forget", "g_q_lora", "w_q_up", "g_kv_lora", "w_kv_up", "g_mix_out", "w_out",
              "g_mix_post", "g_ffn_pre", "w_ffn_up", "w_ffn_down", "g_ffn_post")
_WEIGHTS = _ARG_NAMES[2:]


def kernel(x, positions, g_mix_pre, w_in, b_forget, g_q_lora, w_q_up, g_kv_lora, w_kv_up, g_mix_out, w_out, g_mix_post, g_ffn_pre, w_ffn_up, w_ffn_down, g_ffn_post, loss_target, m_g_mix_pre, m_w_in, m_b_forget, m_g_q_lora, m_w_q_up, m_g_kv_lora, m_w_kv_up, m_g_mix_out, m_w_out, m_g_mix_post, m_g_ffn_pre, m_w_ffn_up, m_w_ffn_down, m_g_ffn_post, v_g_mix_pre, v_w_in, v_b_forget, v_g_q_lora, v_w_q_up, v_g_kv_lora, v_w_kv_up, v_g_mix_out, v_w_out, v_g_mix_post, v_g_ffn_pre, v_w_ffn_up, v_w_ffn_down, v_g_ffn_post):
    w = dict(g_mix_pre=g_mix_pre, w_in=w_in, b_forget=b_forget, g_q_lora=g_q_lora, w_q_up=w_q_up, g_kv_lora=g_kv_lora, w_kv_up=w_kv_up,
             g_mix_out=g_mix_out, w_out=w_out, g_mix_post=g_mix_post, g_ffn_pre=g_ffn_pre, w_ffn_up=w_ffn_up, w_ffn_down=w_ffn_down,
             g_ffn_post=g_ffn_post)
    m = dict(g_mix_pre=m_g_mix_pre, w_in=m_w_in, b_forget=m_b_forget, g_q_lora=m_g_q_lora, w_q_up=m_w_q_up, g_kv_lora=m_g_kv_lora,
             w_kv_up=m_w_kv_up, g_mix_out=m_g_mix_out, w_out=m_w_out, g_mix_post=m_g_mix_post, g_ffn_pre=m_g_ffn_pre,
             w_ffn_up=m_w_ffn_up, w_ffn_down=m_w_ffn_down, g_ffn_post=m_g_ffn_post)
    v = dict(g_mix_pre=v_g_mix_pre, w_in=v_w_in, b_forget=v_b_forget, g_q_lora=v_g_q_lora, w_q_up=v_w_q_up, g_kv_lora=v_g_kv_lora,
             w_kv_up=v_w_kv_up, g_mix_out=v_g_mix_out, w_out=v_w_out, g_mix_post=v_g_mix_post, g_ffn_pre=v_g_ffn_pre,
             w_ffn_up=v_w_ffn_up, w_ffn_down=v_w_ffn_down, g_ffn_post=v_g_ffn_post)
    shard_shapes = {k: w[k].shape for k in BIG}
    small_shapes = {k: w[k].shape for k in SMALL}
    c_idx = lax.axis_index("c").astype(jnp.int32).reshape(1)
    k_idx = (2 * lax.axis_index("x") + lax.axis_index("y")).astype(jnp.int32).reshape(1)
    first_core = lax.axis_index("c") == 0

    mine = 2 * lax.axis_index("x") + lax.axis_index("y")
    shards_b = [{k: w[k][l:l + 1].astype(BF16) for k in BIG} for l in range(DEPTH)]
    gains = [dict(g_mix_pre=g_mix_pre[l], b_forget=b_forget[l], g_q_lora=g_q_lora[l], g_kv_lora=g_kv_lora[l], g_mix_out=g_mix_out[l],
                  g_mix_post=g_mix_post[l], g_ffn_pre=g_ffn_pre[l], g_ffn_post=g_ffn_post[l]) for l in range(DEPTH)]
    FIRST, LATER = ("w_in", "w_q_up", "w_kv_up"), ("w_out", "w_ffn_up", "w_ffn_down")
    EARLY_GRADS, LATE_GRADS = ("w_ffn_down", "w_ffn_up", "w_out"), ("w_in", "w_q_up", "w_kv_up")

    def gather_job(l, names):
        return _gather_job([shards_b[l][k] for k in names])

    def weights_of(l, names, gathered):
        four = {k: lax.dynamic_update_slice(g, shards_b[l][k][None], (mine, 0, 0, 0))[:, 0]
                for k, g in zip(names, _forward_halves(gathered))}
        out = {}
        for k in names:
            if k == "w_in":
                out["w_in"] = _pack_w_in_shards(four[k])
            elif k == "w_q_up":
                out["wq"] = _pack_w_q(_whole_layer(k, four[k]))
            elif k == "w_kv_up":
                out["wk"], out["wv"] = _pack_w_kv(_whole_layer(k, four[k]))
            elif k == "w_ffn_up":
                out[k] = four[k]
            else:
                out[k] = _whole_layer(k, four[k])
        return out

    def grad_blocks(names, g):
        whole = dict(w_in=lambda: _unpack_dw_in(g["w_in"]), w_q_up=lambda: _unpack_dw_q(g["wq"]),
                     w_kv_up=lambda: _unpack_dw_kv(g["wk"], g["wv"]), w_out=lambda: g["w_out"], w_ffn_down=lambda: g["w_ffn_down"])
        return [(g[k] if k == "w_ffn_up" else _split_layer(k, whole[k]()))[:, None] for k in names]

    def pair_sums(names, blocks, theirs):
        return [_pair_add(b, r, c_idx, name="grad_pair_add_" + k) for k, b, r in zip(names, blocks, theirs)]

    def exchange_job(*pairs):
        return _exchange_chips_job([pb for pair in pairs for (_, pb) in pair])

    def finish_grads(names, pair, partial):
        half = [_chip_add(p, r, k_idx, name="grad_chip_add_" + k) for k, (p, _), r in zip(names, pair, partial)]
        return {k: jnp.where(first_core, jnp.concatenate([q, s], axis=1), jnp.concatenate([s, q], axis=1))
                for k, q, s in zip(names, half, _share_halves(half))}

    seq = x.shape[1]
    tabs = _rope_tables(positions[0].reshape(seq, 1))
    first0 = weights_of(0, FIRST, _run_side_job(gather_job(0, FIRST), "gather_weights_l0"))
    x1, saved0, lw0, gathered1, h1 = _layer_fwd(x[0], {**gains[0], **first0}, tabs, "l0_", fox_side=gather_job(0, LATER),
                                                late_weights=lambda got: weights_of(0, LATER, got), side=gather_job(1, BIG),
                                                next_gain=gains[1]["g_mix_pre"])
    lw1 = {**gains[1], **weights_of(1, BIG, gathered1)}
    _, saved1, _, _, _ = _layer_fwd(x1, lw1, tabs, "l1_", h1=h1)
    loss_row, dx, df1, dg1 = _loss_head(saved1["f"], lw1["g_ffn_post"], saved1["x1"], loss_target[0])
    loss = lax.psum(loss_row[0, 0], ("x", "y", "c"))
    dx, grads1, _, post0 = _layer_bwd(dx, lw1, saved1, tabs, "l1_", post_given=(df1, dg1),
                                      then_prev=(saved0["f"], lw0["g_ffn_post"]))
    blocks1 = grad_blocks(BIG, grads1)
    early_blocks0, pair1, early0 = [], [], []

    def beside_l0_fox_backward(g):
        early_blocks0.extend(grad_blocks(EARLY_GRADS, g))
        return _exchange_halves_job(early_blocks0)

    def beside_l0_sb_backward(g, theirs1, theirs_early0):
        pair1.extend(pair_sums(BIG, blocks1, theirs1))
        early0.extend(pair_sums(EARLY_GRADS, early_blocks0, theirs_early0))
        return exchange_job(pair1, early0)

    dx, grads0, partial, _ = _layer_bwd(dx, lw0, saved0, tabs, "l0_", ffn_side=_exchange_halves_job(blocks1),
                                        fox_side=beside_l0_fox_backward, side=beside_l0_sb_backward, post_given=post0)
    big1 = finish_grads(BIG, pair1, partial[:len(BIG)])
    big0 = finish_grads(EARLY_GRADS, early0, partial[len(BIG):])
    late_blocks0 = grad_blocks(LATE_GRADS, grads0)
    late0 = pair_sums(LATE_GRADS, late_blocks0, _run_side_job(_exchange_halves_job(late_blocks0), "grad_pair_exchange_l0"))
    big0.update(finish_grads(LATE_GRADS, late0, _run_side_job(exchange_job(late0), "grad_chip_exchange_l0")))
    g_big = {k: jnp.concatenate([big0[k], big1[k]], axis=0) for k in BIG}
    grads = [grads0, grads1]

    g_small_local = {k: jnp.stack([grads[l][k].reshape(small_shapes[k][1:]) for l in range(DEPTH)]) for k in SMALL}
    g_small = _small_from_rows(_all_reduce_small(_small_to_rows(g_small_local)), small_shapes)

    g_all = {**g_big, **g_small}
    delta, new_m, new_v = {}, {}, {}
    for k in BIG:
        d, r, c = shard_shapes[k]
        two_d = lambda a: a.reshape(d * r, c)
        dk, mk, vk = _adamw(two_d(w[k]), two_d(g_all[k]), two_d(m[k]), two_d(v[k]), name="adamw_" + k)
        delta[k], new_m[k], new_v[k] = dk.reshape(d, r, c), mk.reshape(d, r, c), vk.reshape(d, r, c)
    ds, ms, vs = _adamw(_small_to_rows(w), _small_to_rows(g_small), _small_to_rows(m), _small_to_rows(v), name="adamw_small")
    delta.update(_small_from_rows(ds, small_shapes))
    new_m.update(_small_from_rows(ms, small_shapes))
    new_v.update(_small_from_rows(vs, small_shapes))

    grad_x = dx.reshape(x.shape)
    return (loss, grad_x, *[g_all[k] for k in _WEIGHTS], *[delta[k] for k in _WEIGHTS], *[new_m[k] for k in _WEIGHTS],
            *[new_v[k] for k in _WEIGHTS])
```

```python
import functools
import math

import numpy as np
import jax
import jax.numpy as jnp
from jax import lax
from jax.experimental import pallas as pl
from jax.experimental.pallas import tpu as pltpu

F32 = jnp.float32
BF16 = jnp.bfloat16
MESH = pl.DeviceIdType.MESH

D_MODEL = 1024
DEPTH = 2
CHUNK = 64
GROUP = 256
HEAD = 64
N_HEADS = 4
Q_RANK = 256
KV_RANK = 128
ROPE_DIM = 32
D_FF = 4096
D_IN = 2980
D_INP = 3072
ROPE_BASE = 10000.0
EPS = 1e-6
LANES = 128
TQ = 128
GATE_ROWS = 512
NEG = -1e30

ADAM_LR, ADAM_B1, ADAM_B2, ADAM_EPS, ADAM_WD, ADAM_STEP = 0.001, 0.9, 0.999, 1e-08, 0.01, 10

OFF_FQ, OFF_FK, OFF_FV, OFF_CQ = 0, 2, 4, 6
OFF_RQ, OFF_RK, OFF_RV, OFF_RG = 8, 10, 12, 14
OFF_SQ, OFF_SK, OFF_SV = 16, 18, 20
OFF_CKV, OFF_MISC = 22, 23
FF_LANE, KR_LANE = 0, 64

VMEM_LIMIT = 56 * 1024 * 1024


def _tile(dim, pref):
    return pref if dim % pref == 0 else dim


def _cparams(sem, vmem=None):
    return pltpu.CompilerParams(dimension_semantics=sem, vmem_limit_bytes=vmem or VMEM_LIMIT)


def _dot(a, b):
    return jnp.dot(a, b, preferred_element_type=F32)


def _dot_nt(a, b):
    return lax.dot_general(a, b, (((1,), (1,)), ((), ())), preferred_element_type=F32)


def _dot_tn(a, b):
    return lax.dot_general(a, b, (((0,), (0,)), ((), ())), preferred_element_type=F32)


def _dot_exact(a, b):
    return jnp.dot(a, b, precision=lax.Precision.HIGHEST, preferred_element_type=F32)


def _matmul(a, b, *, name, ta=False, tb=False, out_dtype=F32, tm=1024, tn=1024, tk=1024,
            relu2=False, relu2_of=None, also_bf16=False, side=None, col_blocks=False):
    if ta:
        kdim, m = a.shape
    else:
        m, kdim = a.shape
    if col_blocks and not ta:
        n = b.shape[1] if tb else b.shape[0] * b.shape[2]
        if tb:
            kdim = b.shape[0] * b.shape[2]
    else:
        n = b.shape[0] if tb else b.shape[1]
    tm, tn, tk = _tile(m, tm), _tile(n, tn), _tile(kdim, tk)
    nk = kdim // tk
    a_spec = pl.BlockSpec((tk, tm), lambda i, j, k: (k, i)) if ta else pl.BlockSpec((tm, tk), lambda i, j, k: (i, k))
    b_spec = pl.BlockSpec((tn, tk), lambda i, j, k: (j, k)) if tb else pl.BlockSpec((tk, tn), lambda i, j, k: (k, j))
    o_spec = pl.BlockSpec((tm, tn), lambda i, j, k: (i, j))
    if col_blocks and ta:
        o_spec = pl.BlockSpec((None, tm, tn), lambda i, j, k: (j, i, 0))
    elif col_blocks and tb:
        assert b.shape[2] == tk
        b_spec = pl.BlockSpec((None, tn, tk), lambda i, j, k: (k, j, 0))
    elif col_blocks:
        assert b.shape[2] == tn
        b_spec = pl.BlockSpec((None, tk, tn), lambda i, j, k: (j, k, 0))
    two = also_bf16

    def body(*refs):
        refs = list(refs)
        a_ref, b_ref = refs[0], refs[1]
        e_ref = refs[2] if relu2_of is not None else None
        pos = 3 if relu2_of is not None else 2
        o_ref = refs[pos]
        o2_ref = refs[pos + 1] if two else None
        acc_ref = refs[-1]
        k = pl.program_id(2)
        av = a_ref[...].astype(BF16)
        bv = b_ref[...].astype(BF16)
        if ta:
            part = _dot_tn(av, bv)
        elif tb:
            part = _dot_nt(av, bv)
        else:
            part = _dot(av, bv)

        @pl.when(k == 0)
        def _():
            acc_ref[...] = part

        @pl.when(k > 0)
        def _():
            acc_ref[...] += part

        @pl.when(k == nk - 1)
        def _():
            r = acc_ref[...]
            if relu2_of is not None:
                r = r * (2.0 * jnp.sqrt(e_ref[...].astype(F32)))
            if relu2:
                r = jnp.square(jnp.maximum(r, 0.0))
            o_ref[...] = r.astype(o_ref.dtype)
            if also_bf16:
                o2_ref[...] = r.astype(BF16)

    in_specs = [a_spec, b_spec]
    args = [a, b]
    if relu2_of is not None:
        in_specs.append(o_spec)
        args.append(relu2_of)
    out_shape = [jax.ShapeDtypeStruct((n // tn, m, tn) if (col_blocks and ta) else (m, n), out_dtype)]
    out_specs = [o_spec]
    if two:
        out_shape.append(jax.ShapeDtypeStruct((m, n), BF16))
        out_specs.append(o_spec)
    grid = (m // tm, n // tn, nk)
    side_in, side_out, side_scratch = _side_specs(side)
    res = pl.pallas_call(
        _carry_side_job(body, len(args), len(out_shape), side, grid), name=name, grid=grid,
        in_specs=in_specs + side_in, out_specs=out_specs + side_out,
        out_shape=out_shape + ([] if side is None else side.out_shape),
        scratch_shapes=[pltpu.VMEM((tm, tn), F32)] + side_scratch,
        compiler_params=_cparams(("parallel", "parallel", "arbitrary") if side is None else ("arbitrary",) * 3),
    )(*args, *([] if side is None else side.inputs))
    main = res[:len(out_shape)]
    main = main if two else main[0]
    return main if side is None else (main, res[len(out_shape):])


def _rms(x, g):
    r = lax.rsqrt(jnp.mean(x * x, axis=-1, keepdims=True) + EPS)
    return x * r * g


def _rms_bwd(x, g, dy):
    r = lax.rsqrt(jnp.mean(x * x, axis=-1, keepdims=True) + EPS)
    xh = x * r
    gdy = dy * g
    dx = r * (gdy - xh * jnp.mean(xh * gdy, axis=-1, keepdims=True))
    return dx, xh * dy


def _norm_fwd(x, g, *, name, resid=None, out_dtype=BF16, next_gain=None):
    s, d = x.shape
    tr = _tile(s, 256)
    row = pl.BlockSpec((tr, d), lambda i: (i, 0))
    gsp = pl.BlockSpec((1, d), lambda i: (0, 0))

    def body(*refs):
        refs = list(refs)
        x_ref, g_ref = refs[:2]
        y = _rms(x_ref[...], g_ref[...])
        pos = 2
        if resid is not None:
            y = refs[pos][...] + y
            pos += 1
        if next_gain is None:
            refs[pos][...] = y.astype(refs[pos].dtype)
        else:
            refs[pos + 1][...] = y.astype(refs[pos + 1].dtype)
            refs[pos + 2][...] = _rms(y, refs[pos][...]).astype(BF16)

    args = [x, g.reshape(1, d)] + ([] if resid is None else [resid]) + ([] if next_gain is None else [next_gain.reshape(1, d)])
    in_specs = [row, gsp] + ([] if resid is None else [row]) + ([] if next_gain is None else [gsp])
    first = jax.ShapeDtypeStruct((s, d), out_dtype)
    if next_gain is None:
        out_specs, out_shape = row, first
    else:
        out_specs, out_shape = [row, row], [first, jax.ShapeDtypeStruct((s, d), BF16)]
    return pl.pallas_call(
        body, name=name, grid=(s // tr,), in_specs=in_specs, out_specs=out_specs, out_shape=out_shape,
        compiler_params=_cparams(("parallel",)),
    )(*args)


def _norm_bwd(x, g, dy, *, name, add=None, out_dtype=F32, then=None):
    s, d = x.shape
    tr = _tile(s, 256)
    row = pl.BlockSpec((tr, d), lambda i: (i, 0))
    gsp = pl.BlockSpec((1, d), lambda i: (0, 0))
    n_in = 3 + (add is not None) + (2 if then is not None else 0)

    def body(*refs):
        ins, outs = refs[:n_in], refs[n_in:]
        x_ref, g_ref, dy_ref = ins[:3]
        dx, gterm = _rms_bwd(x_ref[...], g_ref[...], dy_ref[...].astype(F32))
        if add is not None:
            dx = dx + ins[3][...]
        outs[0][...] = dx.astype(outs[0].dtype)
        terms = [(outs[1], gterm)]
        if then is not None:
            dx2, gterm2 = _rms_bwd(ins[-2][...], ins[-1][...], dx)
            outs[2][...] = dx2.astype(BF16)
            terms.append((outs[3], gterm2))

        @pl.when(pl.program_id(0) == 0)
        def _():
            for dg_ref, _ in terms:
                dg_ref[...] = jnp.zeros_like(dg_ref)

        for dg_ref, term in terms:
            dg_ref[...] += jnp.sum(term, axis=0, keepdims=True)

    args = [x, g.reshape(1, d), dy] + ([] if add is None else [add]) + ([] if then is None else [then[0], then[1].reshape(1, d)])
    in_specs = [row, gsp, row] + ([] if add is None else [row]) + ([] if then is None else [row, gsp])
    out_specs = [row, gsp] + ([] if then is None else [row, gsp])
    out_shape = [jax.ShapeDtypeStruct((s, d), out_dtype), jax.ShapeDtypeStruct((1, d), F32)]
    if then is not None:
        out_shape += [jax.ShapeDtypeStruct((s, d), BF16), jax.ShapeDtypeStruct((1, d), F32)]
    return pl.pallas_call(
        body, name=name, grid=(s // tr,), in_specs=in_specs, out_specs=out_specs, out_shape=out_shape,
        compiler_params=_cparams(("arbitrary",)),
    )(*args)


def _loss_head(f, g, resid, target):
    s, d = f.shape
    tr = _tile(s, 256)
    row = pl.BlockSpec((tr, d), lambda i: (i, 0))
    gsp = pl.BlockSpec((1, d), lambda i: (0, 0))
    lsp = pl.BlockSpec((1, LANES), lambda i: (0, 0))

    def body(f_ref, g_ref, r_ref, t_ref, l_ref, dy_ref, df_ref, dg_ref):
        fv, gv = f_ref[...], g_ref[...]
        e = (r_ref[...] + _rms(fv, gv)) - t_ref[...]
        dy = e * (1.0 / d)
        dy_ref[...] = dy
        df, gterm = _rms_bwd(fv, gv, dy)
        df_ref[...] = df.astype(BF16)

        @pl.when(pl.program_id(0) == 0)
        def _():
            l_ref[...] = jnp.zeros_like(l_ref)
            dg_ref[...] = jnp.zeros_like(dg_ref)

        part = 0.5 * jnp.sum(jnp.mean(e * e, axis=-1, keepdims=True), axis=0, keepdims=True)
        l_ref[...] += jnp.broadcast_to(part, (1, LANES))
        dg_ref[...] += jnp.sum(gterm, axis=0, keepdims=True)

    return pl.pallas_call(
        body, name="loss_head", grid=(s // tr,), in_specs=[row, gsp, row, row], out_specs=[lsp, row, row, gsp],
        out_shape=[jax.ShapeDtypeStruct((1, LANES), F32), jax.ShapeDtypeStruct((s, d), F32), jax.ShapeDtypeStruct((s, d), BF16),
                   jax.ShapeDtypeStruct((1, d), F32)],
        compiler_params=_cparams(("arbitrary",)),
    )(f, g.reshape(1, d), resid, target)


def _rope_tables(pos_col):
    s = pos_col.shape[0]
    tr = _tile(s, 512)
    f_mla = ROPE_BASE ** (-jnp.arange(ROPE_DIM // 2, dtype=F32) / (ROPE_DIM // 2))
    f_ret = ROPE_BASE ** (-jnp.arange(HEAD // 2, dtype=F32) / (HEAD // 2))
    fm = jnp.concatenate([jnp.zeros((64,), F32), f_mla, f_mla, jnp.zeros((32,), F32)]).reshape(1, LANES)
    fr = jnp.tile(jnp.concatenate([f_ret, f_ret]), 2).reshape(1, LANES)

    def body(p_ref, fm_ref, fr_ref, cm_ref, sm_ref, cr_ref, sr_ref):
        p = p_ref[...].astype(F32)
        am = p * fm_ref[...]
        ar = p * fr_ref[...]
        cm_ref[...] = jnp.cos(am)
        sm_ref[...] = jnp.sin(am)
        cr_ref[...] = jnp.tile(jnp.cos(ar), (1, 2))
        sr_ref[...] = jnp.tile(jnp.sin(ar), (1, 2))

    return pl.pallas_call(
        body, name="rope_tables", grid=(s // tr,),
        in_specs=[pl.BlockSpec((tr, 1), lambda i: (i, 0)), pl.BlockSpec((1, LANES), lambda i: (0, 0)),
                  pl.BlockSpec((1, LANES), lambda i: (0, 0))],
        out_specs=[pl.BlockSpec((tr, LANES), lambda i: (i, 0))] * 2 + [pl.BlockSpec((tr, 2 * LANES), lambda i: (i, 0))] * 2,
        out_shape=[jax.ShapeDtypeStruct((s, LANES), F32)] * 2 + [jax.ShapeDtypeStruct((s, 2 * LANES), F32)] * 2,
        compiler_params=_cparams(("parallel",)),
    )(pos_col, fm, fr)


def _lane(shape):
    return lax.broadcasted_iota(jnp.int32, shape, len(shape) - 1)


def _rot_mla(z):
    l = _lane(z.shape) % LANES
    n = z.shape[-1]
    return jnp.where(l < 80, -pltpu.roll(z, n - 16, 1), pltpu.roll(z, 16, 1))


def _rot_mla_t(y):
    l = _lane(y.shape) % LANES
    n = y.shape[-1]
    return jnp.where((l >= 64) & (l < 80), pltpu.roll(y, n - 16, 1),
                     jnp.where((l >= 80) & (l < 96), -pltpu.roll(y, 16, 1), 0.0))


def _rot_ret(z):
    l = _lane(z.shape) % HEAD
    n = z.shape[-1]
    return jnp.where(l < 32, -pltpu.roll(z, n - 32, 1), pltpu.roll(z, 32, 1))


def _rot_ret_t(y):
    l = _lane(y.shape) % HEAD
    n = y.shape[-1]
    return jnp.where(l < 32, pltpu.roll(y, n - 32, 1), -pltpu.roll(y, 32, 1))


def _log_sigmoid(x):
    return jnp.minimum(x, 0.0) - jnp.log1p(jnp.exp(-jnp.abs(x)))


def _fox_cum(proj, bias_row):
    s = proj.shape[0]
    fb = _tile(s, GATE_ROWS)
    nb = s // fb

    def body(x_ref, b_ref, cc_ref, cr_ref, carry_ref):
        @pl.when(pl.program_id(0) == 0)
        def _():
            carry_ref[...] = jnp.zeros_like(carry_ref)

        ls = _log_sigmoid(x_ref[...] + b_ref[...])
        r = lax.broadcasted_iota(jnp.int32, (fb, fb), 0)
        c = lax.broadcasted_iota(jnp.int32, (fb, fb), 1)
        tri = (c <= r).astype(F32)
        cum = _dot_exact(tri, ls) + carry_ref[...]
        carry_ref[...] = cum[fb - 1:fb, :]
        cc_ref[...] = cum
        cr_ref[...] = cum.T[0:8, :]

    return pl.pallas_call(
        body, name="fox_cum", grid=(nb,),
        in_specs=[pl.BlockSpec((fb, LANES), lambda i: (i, OFF_MISC)), pl.BlockSpec((1, LANES), lambda i: (0, 0))],
        out_specs=[pl.BlockSpec((fb, LANES), lambda i: (i, 0)), pl.BlockSpec((8, fb), lambda i: (0, i))],
        out_shape=[jax.ShapeDtypeStruct((s, LANES), F32), jax.ShapeDtypeStruct((8, s), F32)],
        scratch_shapes=[pltpu.VMEM((1, LANES), F32)],
        compiler_params=_cparams(("arbitrary",)),
    )(proj, bias_row)


def _fox_gate_bwd(dck, drs, proj, bias_row, dkr):
    s = proj.shape[0]
    fb = _tile(s, GATE_ROWS)
    nb = s // fb

    def body(d_ref, r_ref, x_ref, b_ref, k_ref, o_ref, db_ref, carry_ref):
        @pl.when(pl.program_id(0) == 0)
        def _():
            carry_ref[...] = jnp.zeros_like(carry_ref)
            db_ref[...] = jnp.zeros_like(db_ref)

        rows = jnp.concatenate([d_ref[0], d_ref[1], jnp.zeros((LANES - 16, fb), F32)], axis=0)
        t = rows.T
        l = _lane((fb, LANES))
        r0, r1 = r_ref[0], r_ref[1]
        rsum = jnp.where(l == 0, r0[:, 0:1], jnp.where(l == 1, r0[:, HEAD:HEAD + 1],
                         jnp.where(l == 2, r1[:, 0:1], jnp.where(l == 3, r1[:, HEAD:HEAD + 1], 0.0))))
        dcum = rsum - jnp.where(l < 2, t, pltpu.roll(t, LANES - 6, 1))
        r = lax.broadcasted_iota(jnp.int32, (fb, fb), 0)
        c = lax.broadcasted_iota(jnp.int32, (fb, fb), 1)
        triu = (c >= r).astype(F32)
        rc = _dot_exact(triu, dcum) + carry_ref[...]
        carry_ref[...] = rc[0:1, :]
        f = x_ref[...] + b_ref[...]
        sig_neg = 1.0 / (1.0 + jnp.exp(f))
        df = jnp.where(l < N_HEADS, rc * sig_neg, 0.0)
        db_ref[...] += jnp.sum(df, axis=0, keepdims=True)
        o_ref[...] = (df + k_ref[...]).astype(o_ref.dtype)

    rev = lambda i: nb - 1 - i
    return pl.pallas_call(
        body, name="fox_gate_bwd", grid=(nb,),
        in_specs=[pl.BlockSpec((2, 8, fb), lambda i: (0, 0, rev(i))), pl.BlockSpec((2, fb, LANES), lambda i: (0, rev(i), 0)),
                  pl.BlockSpec((fb, LANES), lambda i: (rev(i), OFF_MISC)),
                  pl.BlockSpec((1, LANES), lambda i: (0, 0)), pl.BlockSpec((fb, LANES), lambda i: (rev(i), 0))],
        out_specs=[pl.BlockSpec((fb, LANES), lambda i: (rev(i), 0)), pl.BlockSpec((1, LANES), lambda i: (0, 0))],
        out_shape=[jax.ShapeDtypeStruct((s, LANES), BF16), jax.ShapeDtypeStruct((1, LANES), F32)],
        scratch_shapes=[pltpu.VMEM((1, LANES), F32)],
        compiler_params=_cparams(("arbitrary",)),
    )(dck, drs, proj, bias_row, dkr)


def _mla_prep(proj, cos_m, sin_m, g_q, g_kv, wq, wk, wv):
    s = proj.shape[0]
    tr = _tile(s, 256)

    def body(cq_ref, ckv_ref, misc_ref, cos_ref, sin_ref, gq_ref, gkv_ref, wq_ref, wk_ref, wv_ref,
             q_ref, k_ref, v_ref, cqn_ref, ckvn_ref):
        cos4 = jnp.tile(cos_ref[...], (1, 4))
        sin4 = jnp.tile(sin_ref[...], (1, 4))
        cqn = _rms(cq_ref[...], gq_ref[...]).astype(BF16)
        ckvn = _rms(ckv_ref[...], gkv_ref[...]).astype(BF16)
        cqn_ref[...] = cqn
        ckvn_ref[...] = ckvn
        zq = _dot(cqn, wq_ref[...])
        q_ref[...] = (zq * cos4 + _rot_mla(zq) * sin4).astype(BF16)
        l = _lane((tr, LANES))
        kr = jnp.where((l >= KR_LANE) & (l < KR_LANE + ROPE_DIM), misc_ref[...], 0.0)
        zk = _dot(ckvn, wk_ref[...]) + jnp.tile(kr, (1, 4))
        k_ref[...] = (zk * cos4 + _rot_mla(zk) * sin4).astype(BF16)
        v_ref[...] = _dot(ckvn, wv_ref[...]).astype(BF16)

    full = lambda a: pl.BlockSpec(a.shape, lambda i: (0, 0))
    rowb = lambda w: pl.BlockSpec((tr, w), lambda i: (i, 0))
    gq2, gkv2 = g_q.reshape(1, Q_RANK), g_kv.reshape(1, KV_RANK)
    return pl.pallas_call(
        body, name="mla_prep", grid=(s // tr,),
        in_specs=[pl.BlockSpec((tr, 256), lambda i: (i, OFF_CQ // 2)), pl.BlockSpec((tr, LANES), lambda i: (i, OFF_CKV)),
                  pl.BlockSpec((tr, LANES), lambda i: (i, OFF_MISC)), rowb(LANES), rowb(LANES),
                  full(gq2), full(gkv2), full(wq), full(wk), full(wv)],
        out_specs=[rowb(512), rowb(512), rowb(512), rowb(256), rowb(128)],
        out_shape=[jax.ShapeDtypeStruct((s, 512), BF16), jax.ShapeDtypeStruct((s, 512), BF16), jax.ShapeDtypeStruct((s, 512), BF16),
                   jax.ShapeDtypeStruct((s, 256), BF16), jax.ShapeDtypeStruct((s, 128), BF16)],
        compiler_params=_cparams(("parallel",)),
    )(proj, proj, proj, cos_m, sin_m, gq2, gkv2, wq, wk, wv)


def _mla_prep_bwd(dq, dk, dv, proj, cqn, ckvn, cos_m, sin_m, g_q, g_kv, wq, wk, wv):
    s = proj.shape[0]
    tr = _tile(s, 256)

    def body(dq_ref, dk_ref, dv_ref, cq_ref, ckv_ref, cqn_ref, ckvn_ref, cos_ref, sin_ref, gq_ref, gkv_ref,
             wq_ref, wk_ref, wv_ref, dcq_ref, dckv_ref, dkr_ref, dwq_ref, dwk_ref, dwv_ref, dgq_ref, dgkv_ref):
        @pl.when(pl.program_id(0) == 0)
        def _():
            for r in (dwq_ref, dwk_ref, dwv_ref, dgq_ref, dgkv_ref):
                r[...] = jnp.zeros_like(r)

        cos4 = jnp.tile(cos_ref[...], (1, 4))
        sin4 = jnp.tile(sin_ref[...], (1, 4))
        dqv = dq_ref[...]
        dzq = dqv * cos4 + _rot_mla_t(dqv * sin4)
        dkv_ = dk_ref[...]
        dzk = dkv_ * cos4 + _rot_mla_t(dkv_ * sin4)
        l = _lane((tr, LANES))
        in_rope = (l >= KR_LANE) & (l < KR_LANE + ROPE_DIM)
        dkr = dzk[:, 0:128] + dzk[:, 128:256] + dzk[:, 256:384] + dzk[:, 384:512]
        dkr_ref[...] = jnp.where(in_rope, dkr, 0.0)
        dzq_b = dzq.astype(BF16)
        dzk_b = dzk.astype(BF16)
        dv_b = dv_ref[...].astype(BF16)
        dcqn = _dot_nt(dzq_b, wq_ref[...])
        dckvn = _dot_nt(dzk_b, wk_ref[...]) + _dot_nt(dv_b, wv_ref[...])
        dwq_ref[...] += _dot_tn(cqn_ref[...], dzq_b)
        dwk_ref[...] += _dot_tn(ckvn_ref[...], dzk_b)
        dwv_ref[...] += _dot_tn(ckvn_ref[...], dv_b)
        dcq, gq_term = _rms_bwd(cq_ref[...], gq_ref[...], dcqn)
        dckv, gkv_term = _rms_bwd(ckv_ref[...], gkv_ref[...], dckvn)
        dcq_ref[...] = dcq.astype(BF16)
        dckv_ref[...] = dckv.astype(BF16)
        dgq_ref[...] += jnp.sum(gq_term, axis=0, keepdims=True)
        dgkv_ref[...] += jnp.sum(gkv_term, axis=0, keepdims=True)

    full = lambda shp: pl.BlockSpec(shp, lambda i: (0, 0))
    rowb = lambda w: pl.BlockSpec((tr, w), lambda i: (i, 0))
    gq2, gkv2 = g_q.reshape(1, Q_RANK), g_kv.reshape(1, KV_RANK)
    return pl.pallas_call(
        body, name="mla_prep_bwd", grid=(s // tr,),
        in_specs=[rowb(512), rowb(512), rowb(512),
                  pl.BlockSpec((tr, 256), lambda i: (i, OFF_CQ // 2)), pl.BlockSpec((tr, LANES), lambda i: (i, OFF_CKV)),
                  rowb(256), rowb(128), rowb(LANES), rowb(LANES), full((1, Q_RANK)), full((1, KV_RANK)),
                  full(wq.shape), full(wk.shape), full(wv.shape)],
        out_specs=[rowb(256), rowb(128), rowb(128), full(wq.shape), full(wk.shape), full(wv.shape),
                   full((1, Q_RANK)), full((1, KV_RANK))],
        out_shape=[jax.ShapeDtypeStruct((s, 256), BF16), jax.ShapeDtypeStruct((s, 128), BF16), jax.ShapeDtypeStruct((s, 128), F32),
                   jax.ShapeDtypeStruct(wq.shape, F32), jax.ShapeDtypeStruct(wk.shape, F32), jax.ShapeDtypeStruct(wv.shape, F32),
                   jax.ShapeDtypeStruct((1, Q_RANK), F32), jax.ShapeDtypeStruct((1, KV_RANK), F32)],
        compiler_params=_cparams(("arbitrary",)),
    )(dq, dk, dv, proj, proj, cqn, ckvn, cos_m, sin_m, gq2, gkv2, wq, wk, wv)


def _ret_prep(proj, cos_r, sin_r):
    s = proj.shape[0]
    tr = _tile(s, 256)

    def body(q_ref, k_ref, cos_ref, sin_ref, qo_ref, ko_ref):
        cos, sin = cos_ref[...], sin_ref[...]
        q, k = q_ref[...], k_ref[...]
        qo_ref[...] = (q * cos + _rot_ret(q) * sin).astype(BF16)
        ko_ref[...] = ((k * cos + _rot_ret(k) * sin) * (HEAD ** -0.5)).astype(BF16)

    rowb = pl.BlockSpec((tr, 256), lambda i: (i, 0))
    return pl.pallas_call(
        body, name="ret_prep", grid=(s // tr,),
        in_specs=[pl.BlockSpec((tr, 256), lambda i: (i, OFF_RQ // 2)), pl.BlockSpec((tr, 256), lambda i: (i, OFF_RK // 2)), rowb, rowb],
        out_specs=[rowb, rowb], out_shape=[jax.ShapeDtypeStruct((s, 256), BF16)] * 2,
        compiler_params=_cparams(("parallel",)),
    )(proj, proj, cos_r, sin_r)


def _ret_prep_bwd(dq, dk, cos_r, sin_r):
    s = dq.shape[0]
    tr = _tile(s, 256)

    def body(dq_ref, dk_ref, cos_ref, sin_ref, qo_ref, ko_ref):
        cos, sin = cos_ref[...], sin_ref[...]
        q, k = dq_ref[...], dk_ref[...] * (HEAD ** -0.5)
        qo_ref[...] = (q * cos + _rot_ret_t(q * sin)).astype(BF16)
        ko_ref[...] = (k * cos + _rot_ret_t(k * sin)).astype(BF16)

    rowb = pl.BlockSpec((tr, 256), lambda i: (i, 0))
    return pl.pallas_call(
        body, name="ret_prep_bwd", grid=(s // tr,), in_specs=[rowb] * 4, out_specs=[rowb, rowb],
        out_shape=[jax.ShapeDtypeStruct((s, 256), BF16)] * 2, compiler_params=_cparams(("parallel",)),
    )(dq, dk, cos_r, sin_r)


_LOG_GAMMA = [float(np.log1p(-np.float32(2.0) ** np.float32(-5.0 - h))) for h in range(N_HEADS)]
_MLA_SCALE = float((HEAD + ROPE_DIM) ** -0.5)
_QK_SCALE = float(HEAD ** -0.5)
KEY_BLOCKS = 4
QB = 256


def _split2(x):
    h = x.astype(BF16)
    return h, (x - h.astype(F32)).astype(BF16)


def _dot2(x, u):
    h, lo = _split2(x)
    return _dot(h, u) + _dot(lo, u)


def _head_pick(block, head, axis):
    idx = lax.broadcasted_iota(jnp.int32, block.shape, axis)
    return jnp.sum(jnp.where(idx == head, block, 0.0), axis=axis, keepdims=True)


def _log_gamma_of(head):
    lg = jnp.float32(_LOG_GAMMA[3])
    for h in (2, 1, 0):
        lg = jnp.where(head == h, jnp.float32(_LOG_GAMMA[h]), lg)
    return lg


def _mixer_specs(mode, s, q_off, k_off, v_off):
    nhb = 2
    bw = 2 * LANES if mode == "mla" else LANES
    nsub = KEY_BLOCKS if (s // TQ) % KEY_BLOCKS == 0 else 1
    q_spec = pl.BlockSpec((QB, bw), lambda p, i: (i, q_off + p))
    k_spec = pl.BlockSpec((s, bw), lambda p, i: (0, k_off + p))
    v_spec = pl.BlockSpec((s, bw), lambda p, i: (0, v_off + p))
    return nhb, N_HEADS // nhb, nsub, q_spec, k_spec, v_spec


def _mixer_geometry(mode, i, nsub):
    w = TQ * nsub
    row = lax.broadcasted_iota(jnp.int32, (QB, w), 0)
    col = lax.broadcasted_iota(jnp.int32, (QB, w), 1)
    nfull = (i * QB) // w
    dist = col - row
    if mode in ("fox", "sb"):
        rel = dist
    else:
        rel = col - (row | (CHUNK - 1))

    def visible(c):
        off = c * w - i * QB
        return (rel + off) < 0 if mode == "sb" else (rel + off) <= 0

    return nfull, dist, visible


class _SideJob:
    def __init__(self, inputs, out_shape, n_sems, sends, recvs):
        self.inputs, self.out_shape, self.n_sems, self.sends, self.recvs = list(inputs), list(out_shape), n_sems, sends, recvs


def _carry_side_job(body, n_in, n_out, side, n_steps):
    if side is None:
        return body
    si, so = len(side.inputs), len(side.out_shape)

    def at(corner):
        ok = pl.program_id(0) == corner[0]
        for d in range(1, len(n_steps)):
            ok = ok & (pl.program_id(d) == corner[d])
        return ok

    def wrapped(*refs):
        ins, s_ins = refs[:n_in], refs[n_in:n_in + si]
        outs, s_outs = refs[n_in + si:n_in + si + n_out], refs[n_in + si + n_out:n_in + si + n_out + so]
        scratch, send, recv = refs[n_in + si + n_out + so:-2], refs[-2], refs[-1]

        @pl.when(at([0] * len(n_steps)))
        def _():
            for cp in side.sends(s_ins, s_outs, send, recv):
                cp.start()

        body(*ins, *outs, *scratch)

        @pl.when(at([n - 1 for n in n_steps]))
        def _():
            for cp in side.recvs(s_ins, s_outs, send, recv):
                cp.wait_recv()
            for cp in side.sends(s_ins, s_outs, send, recv):
                cp.wait_send()

    return wrapped


def _side_specs(side):
    if side is None:
        return [], [], []
    hbm = pl.BlockSpec(memory_space=pl.ANY)
    return ([hbm] * len(side.inputs), [hbm] * len(side.out_shape),
            [pltpu.SemaphoreType.DMA((side.n_sems,)), pltpu.SemaphoreType.DMA((side.n_sems,))])


def _mixer_fwd(mode, qa, q_off, ka, k_off, va, v_off, *, cum_col=None, cum_row=None, side=None):
    s = qa.shape[0]
    nq = s // QB
    nhb, nblk, nsub, q_spec, k_spec, v_spec = _mixer_specs(mode, s, q_off, k_off, v_off)
    w = TQ * nsub
    softmax = mode in ("fox", "mla")

    def body(*refs):
        refs = list(refs)
        q_ref, k_ref, v_ref = refs[:3]
        refs = refs[3:]
        if mode == "fox":
            cc_ref, cr_ref = refs[:2]
            refs = refs[2:]
        o_ref = refs[0]
        st_ref = refs[1]
        p = pl.program_id(0)
        i = pl.program_id(1)
        nfull, dist, visible = _mixer_geometry(mode, i, nsub)
        lane = _lane((1, LANES))
        heads = [nhb * p + hh for hh in range(nhb)]
        wide = mode == "mla"
        q_scale = _QK_SCALE if mode in ("fox", "sb") else 1.0
        cols = [slice(hh * LANES, (hh + 1) * LANES) if wide else slice(None) for hh in range(nhb)]
        if wide:
            qs = [q_ref[:, cols[hh]] for hh in range(nhb)]
        else:
            qf = q_ref[...].astype(F32) * q_scale
            qs = [jnp.where((lane // HEAD) == hh, qf, 0.0).astype(BF16) for hh in range(nhb)]
        if mode == "fox":
            cqs = [_head_pick(cc_ref[...], h, 1) for h in heads]
        if mode == "sb":
            r1 = lax.broadcasted_iota(jnp.int32, (TQ, TQ), 0)
            c1 = lax.broadcasted_iota(jnp.int32, (TQ, TQ), 1)
            u_after = (r1 > c1).astype(BF16)

        def chunk(c):
            return pl.ds(pl.multiple_of(c * w, w), w)

        def scores(c):
            js = chunk(c)
            return tuple(_dot_nt(qs[hh], k_ref[js, cols[hh]]) for hh in range(nhb))

        def head_step(hh, c, js, sc, vj, carry, last):
            if softmax:
                m, l, acc = carry
                if mode == "fox":
                    ck = _head_pick(cr_ref[:, js], heads[hh], 0)
                    sc = sc + (cqs[hh] - ck)
                else:
                    sc = sc * _MLA_SCALE
                if last:
                    sc = jnp.where(visible(c), sc, NEG)
                m_new = jnp.maximum(m, jnp.max(sc, axis=-1, keepdims=True))
                alpha = jnp.exp(m - m_new)
                pr = jnp.exp(sc - m_new)
                l = alpha * l + jnp.sum(pr, axis=-1, keepdims=True)
                acc = alpha * acc + _dot(pr.astype(BF16), vj)
                return m_new, l, acc
            run, acc = carry
            z = sc
            log_beta = jnp.minimum(z, 0.0) - jnp.log(1.0 + jnp.exp(-jnp.abs(z)))
            log_stay = log_beta - z
            if last:
                vis = visible(c)
                log_stay = jnp.where(vis, log_stay, 0.0)
            parts = [None] * nsub
            for b in reversed(range(nsub)):
                ls_b = log_stay[:, b * TQ:(b + 1) * TQ]
                parts[b] = _dot2(ls_b, u_after) + run
                run = run + jnp.sum(ls_b, axis=-1, keepdims=True)
            later = parts[0] if nsub == 1 else jnp.concatenate(parts, axis=1)
            wgt = jnp.exp(log_beta + later)
            if last:
                wgt = jnp.where(vis, wgt, 0.0)
            return run, acc + _dot(wgt.astype(BF16), vj)

        def step(c, c_next, state, last):
            scs, carries = state
            nxt = scores(c_next) if c_next is not None else None
            js = chunk(c)
            return nxt, tuple(head_step(hh, c, js, scs[hh], v_ref[js, cols[hh]], carries[hh], last) for hh in range(nhb))

        zero_acc = jnp.zeros((QB, LANES), F32)
        zero1 = jnp.zeros((QB, 1), F32)
        if softmax:
            init = tuple((jnp.full((QB, 1), NEG, F32), zero1, zero_acc) for _ in range(nhb))
        else:
            init = tuple((zero1, zero_acc) for _ in range(nhb))
        if mode == "sb":
            state = step(nfull, jnp.maximum(nfull - 1, 0), (scores(nfull), init), True)
            _, carries = lax.fori_loop(0, nfull, lambda t, st: step(nfull - 1 - t, jnp.maximum(nfull - 2 - t, 0), st, False), state)
        else:
            state = lax.fori_loop(0, nfull, lambda c, st: step(c, c + 1, st, False), (scores(0), init))
            _, carries = step(nfull, None, state, True)
        if softmax:
            outs = [acc / l for (m, l, acc) in carries]
            stats = [m + jnp.log(l) for (m, l, acc) in carries]
        else:
            outs, stats = [acc for (run, acc) in carries], [run for (run, acc) in carries]
        hm0 = (lane // HEAD) == 0
        pick = lambda a: jnp.where(hm0, a[0], a[1])
        if wide:
            for hh in range(nhb):
                o_ref[:, cols[hh]] = outs[hh]
        else:
            o_ref[...] = pick(outs)
        st_ref[0] = pick(stats)

    in_specs = [q_spec, k_spec, v_spec]
    args = [qa, ka, va]
    if mode == "fox":
        in_specs += [pl.BlockSpec((QB, LANES), lambda p, i: (i, 0)), pl.BlockSpec((8, s), lambda p, i: (0, 0))]
        args += [cum_col, cum_row]
    bw = 2 * LANES if mode == "mla" else LANES
    out_specs = [pl.BlockSpec((QB, bw), lambda p, i: (i, p))]
    out_shape = [jax.ShapeDtypeStruct((s, nblk * bw), F32)]
    out_specs.append(pl.BlockSpec((1, QB, LANES), lambda p, i: (p, i, 0)))
    out_shape.append(jax.ShapeDtypeStruct((nblk, s, LANES), F32))
    side_in, side_out, side_scratch = _side_specs(side)
    res = pl.pallas_call(
        _carry_side_job(body, len(args), len(out_shape), side, (nblk, nq)), name=mode + "_fwd", grid=(nblk, nq),
        in_specs=in_specs + side_in, out_specs=out_specs + side_out,
        out_shape=out_shape + ([] if side is None else side.out_shape), scratch_shapes=side_scratch,
        compiler_params=_cparams(("parallel", "parallel") if side is None else ("arbitrary", "arbitrary")),
    )(*args, *([] if side is None else side.inputs))
    return (res[0], res[1]) if side is None else (res[0], res[1], res[2:])


def _mixer_bwd(mode, qa, q_off, ka, k_off, va, v_off, o, do, *, stat=None, cum_col=None, cum_row=None, side=None):
    s = qa.shape[0]
    nq = s // QB
    nhb, nblk, nsub, q_spec, k_spec, v_spec = _mixer_specs(mode, s, q_off, k_off, v_off)
    w = TQ * nsub
    softmax = mode in ("fox", "mla")

    def body(*refs):
        refs = list(refs)
        q_ref, k_ref, v_ref, o_ref, do_ref = refs[:5]
        refs = refs[5:]
        st_ref = refs[0]
        refs = refs[1:]
        if mode == "fox":
            cc_ref, cr_ref = refs[:2]
            refs = refs[2:]
        dq_ref, dk_ref, dv_ref = refs[:3]
        dck_ref, drs_ref = refs[3:5] if mode == "fox" else (None, None)
        p = pl.program_id(0)
        i = pl.program_id(1)

        @pl.when(i == 0)
        def _():
            dk_ref[...] = jnp.zeros_like(dk_ref)
            dv_ref[...] = jnp.zeros_like(dv_ref)
            if mode == "fox":
                dck_ref[...] = jnp.zeros_like(dck_ref)

        nfull, dist, visible = _mixer_geometry(mode, i, nsub)
        lane = _lane((1, LANES))
        heads = [nhb * p + hh for hh in range(nhb)]
        dov = do_ref[...]
        wide = mode == "mla"
        q_scale = _QK_SCALE if mode in ("fox", "sb") else 1.0
        cols = [slice(hh * LANES, (hh + 1) * LANES) if wide else slice(None) for hh in range(nhb)]
        if wide:
            prod = dov * o_ref[...]
            qs = [q_ref[:, cols[hh]] for hh in range(nhb)]
            dos = [dov[:, cols[hh]].astype(BF16) for hh in range(nhb)]
            deltas = [jnp.sum(prod[:, cols[hh]], axis=-1, keepdims=True) for hh in range(nhb)]
        else:
            qf = q_ref[...].astype(F32) * q_scale
            prod = dov * o_ref[...]
            hms = [(lane // HEAD) == hh for hh in range(nhb)]
            qs = [jnp.where(hm, qf, 0.0).astype(BF16) for hm in hms]
            dos = [jnp.where(hm, dov, 0.0).astype(BF16) for hm in hms]
            deltas = [jnp.sum(jnp.where(hm, prod, 0.0), axis=-1, keepdims=True) for hm in hms]
        st = st_ref[0]
        stats = [st[:, hh * HEAD:hh * HEAD + 1] for hh in range(nhb)]
        if mode == "fox":
            cqs = [_head_pick(cc_ref[...], h, 1) for h in heads]
        if mode == "sb":
            r1 = lax.broadcasted_iota(jnp.int32, (TQ, TQ), 0)
            c1 = lax.broadcasted_iota(jnp.int32, (TQ, TQ), 1)
            u_upto = (r1 <= c1).astype(BF16)
            u_before = (r1 < c1).astype(BF16)

        def chunk(c):
            return pl.ds(pl.multiple_of(c * w, w), w)

        def scores(c):
            js = chunk(c)
            if mode == "sb":
                return tuple((_dot_nt(qs[hh], k_ref[js, cols[hh]]), None) for hh in range(nhb))
            return tuple((_dot_nt(qs[hh], k_ref[js, cols[hh]]), _dot_nt(dos[hh], v_ref[js, cols[hh]])) for hh in range(nhb))

        def emit(hh, js, ds_b, pr_b, dq):
            dk_ref[js, cols[hh]] += _dot_tn(ds_b, qs[hh])
            dv_ref[js, cols[hh]] += _dot_tn(pr_b, dos[hh])
            return dq + _dot(ds_b, k_ref[js, cols[hh]])

        def head_step(hh, c, js, sc_dp, carry, last):
            sc, dp = sc_dp
            if dp is None:
                dp = _dot_nt(dos[hh], v_ref[js, cols[hh]])
            if softmax:
                dq, rsum = carry
                if mode == "fox":
                    ck = _head_pick(cr_ref[:, js], heads[hh], 0)
                    sc = sc + (cqs[hh] - ck)
                else:
                    sc = sc * _MLA_SCALE
                if last:
                    sc = jnp.where(visible(c), sc, NEG)
                pr = jnp.exp(sc - stats[hh])
                ds = pr * (dp - deltas[hh])
                if mode == "fox":
                    dck_ref[0, hh:hh + 1, js] += jnp.sum(ds, axis=0, keepdims=True)
                    rsum = rsum + jnp.sum(ds, axis=-1, keepdims=True)
                if mode == "mla":
                    ds = ds * _MLA_SCALE
                return emit(hh, js, ds.astype(BF16), pr.astype(BF16), dq), rsum
            seen, gsum, dq = carry
            z = sc
            log_beta = jnp.minimum(z, 0.0) - jnp.log(1.0 + jnp.exp(-jnp.abs(z)))
            log_stay = log_beta - z
            if last:
                vis = visible(c)
                log_stay = jnp.where(vis, log_stay, 0.0)
            parts = []
            for b in range(nsub):
                ls_b = log_stay[:, b * TQ:(b + 1) * TQ]
                parts.append((stats[hh] - seen) - _dot2(ls_b, u_upto))
                seen = seen + jnp.sum(ls_b, axis=-1, keepdims=True)
            later = parts[0] if nsub == 1 else jnp.concatenate(parts, axis=1)
            wgt = jnp.exp(log_beta + later)
            if last:
                wgt = jnp.where(vis, wgt, 0.0)
            g = dp * wgt
            parts = []
            for b in range(nsub):
                g_b = g[:, b * TQ:(b + 1) * TQ]
                parts.append(gsum + _dot2(g_b, u_before))
                gsum = gsum + jnp.sum(g_b, axis=-1, keepdims=True)
            before = parts[0] if nsub == 1 else jnp.concatenate(parts, axis=1)
            beta = jnp.exp(log_beta)
            dz = g * (1.0 - beta) - beta * before
            if last:
                dz = jnp.where(vis, dz, 0.0)
            return seen, gsum, emit(hh, js, dz.astype(BF16), wgt.astype(BF16), dq)

        def step(c, c_next, state, last):
            scs, carries = state
            nxt = scores(c_next) if c_next is not None else None
            js = chunk(c)
            return nxt, tuple(head_step(hh, c, js, scs[hh], carries[hh], last) for hh in range(nhb))

        zero_acc = jnp.zeros((QB, LANES), F32)
        zero1 = jnp.zeros((QB, 1), F32)
        if softmax:
            init = tuple((zero_acc, zero1) for _ in range(nhb))
        else:
            init = tuple((zero1, zero1, zero_acc) for _ in range(nhb))
        state = lax.fori_loop(0, nfull, lambda c, st: step(c, c + 1, st, False), (scores(0), init))
        _, carries = step(nfull, None, state, True)
        if softmax:
            dqs = [dq for (dq, rsum) in carries]
        else:
            dqs = [dq for (seen, gsum, dq) in carries]
        hm0 = (lane // HEAD) == 0
        if wide:
            for hh in range(nhb):
                dq_ref[:, cols[hh]] = dqs[hh]
        else:
            dq_ref[...] = jnp.where(hm0, dqs[0], dqs[1]) * q_scale
        if mode == "fox":
            drs_ref[0] = jnp.where(hm0, carries[0][1], carries[1][1])

    bw = 2 * LANES if mode == "mla" else LANES
    pair_blk = pl.BlockSpec((QB, bw), lambda p, i: (i, p))
    full_blk = pl.BlockSpec((s, bw), lambda p, i: (0, p))
    stat_blk = pl.BlockSpec((1, QB, LANES), lambda p, i: (p, i, 0))
    in_specs = [q_spec, k_spec, v_spec, pair_blk, pair_blk]
    args = [qa, ka, va, o, do]
    in_specs.append(stat_blk)
    args.append(stat)
    if mode == "fox":
        in_specs += [pl.BlockSpec((QB, LANES), lambda p, i: (i, 0)), pl.BlockSpec((8, s), lambda p, i: (0, 0))]
        args += [cum_col, cum_row]
    out_specs = [pair_blk, full_blk, full_blk]
    out_shape = [jax.ShapeDtypeStruct((s, nblk * bw), F32)] * 3
    if mode == "fox":
        out_specs += [pl.BlockSpec((1, 8, s), lambda p, i: (p, 0, 0)), stat_blk]
        out_shape += [jax.ShapeDtypeStruct((2, 8, s), F32), jax.ShapeDtypeStruct((2, s, LANES), F32)]
    side_in, side_out, side_scratch = _side_specs(side)
    res = pl.pallas_call(
        _carry_side_job(body, len(args), len(out_shape), side, (nblk, nq)), name=mode + "_bwd", grid=(nblk, nq),
        in_specs=in_specs + side_in, out_specs=out_specs + side_out,
        out_shape=out_shape + ([] if side is None else side.out_shape), scratch_shapes=side_scratch,
        compiler_params=_cparams(("parallel", "arbitrary") if side is None else ("arbitrary", "arbitrary")),
    )(*args, *([] if side is None else side.inputs))
    return res if side is None else (*res[:len(out_shape)], res[len(out_shape):])


def _ret_geometry(p):
    lane = _lane((1, LANES))
    lg_lane = jnp.where(lane < HEAD, _log_gamma_of(2 * p), _log_gamma_of(2 * p + 1))
    a = lax.broadcasted_iota(jnp.int32, (TQ, 1), 0).astype(F32)
    row = lax.broadcasted_iota(jnp.int32, (TQ, TQ), 0)
    col = lax.broadcasted_iota(jnp.int32, (TQ, TQ), 1)
    same_chunk_or_earlier = (col // CHUNK) <= (row // CHUNK)
    gap = jnp.abs(row - col).astype(F32)
    decays = [jnp.where(same_chunk_or_earlier, jnp.exp(_log_gamma_of(2 * p + hh) * gap), 0.0) for hh in range(2)]
    r = lax.broadcasted_iota(jnp.int32, (LANES, LANES), 0)
    c = lax.broadcasted_iota(jnp.int32, (LANES, LANES), 1)
    own_head = (r // HEAD) == (c // HEAD)
    return lane, lg_lane, a, decays, own_head


def _ret_fwd(qa, ka, va, v_off):
    s = qa.shape[0]
    nq = s // TQ

    def body(q_ref, k_ref, v_ref, o_ref, st_ref, state):
        p = pl.program_id(0)

        @pl.when(pl.program_id(1) == 0)
        def _():
            state[...] = jnp.zeros_like(state)

        lane, lg_lane, a, decays, own_head = _ret_geometry(p)
        q = q_ref[...].astype(F32)
        k = k_ref[...]
        v = v_ref[...]
        s_in = state[...]
        st_ref[0, 0] = s_in
        out = _dot((q * jnp.exp(lg_lane * (a + 1.0))).astype(BF16), s_in.astype(BF16))
        for hh in range(2):
            hm = (lane // HEAD) == hh
            qh = jnp.where(hm, q, 0.0).astype(BF16)
            inner = _dot((_dot_nt(qh, k) * decays[hh]).astype(BF16), v)
            out = out + jnp.where(hm, inner, 0.0)
        o_ref[...] = out
        k_tail = (k.astype(F32) * jnp.exp(lg_lane * (TQ - 1.0 - a))).astype(BF16)
        state[...] = jnp.exp(lg_lane * float(TQ)) * s_in + jnp.where(own_head, _dot_tn(k_tail, v), 0.0)

    blk = lambda off: pl.BlockSpec((TQ, LANES), lambda p, i: (i, off + p))
    return pl.pallas_call(
        body, name="ret_fwd", grid=(2, nq), in_specs=[blk(0), blk(0), blk(v_off)],
        out_specs=[blk(0), pl.BlockSpec((1, 1, LANES, LANES), lambda p, i: (p, i, 0, 0))],
        out_shape=[jax.ShapeDtypeStruct((s, 2 * LANES), F32), jax.ShapeDtypeStruct((2, nq, LANES, LANES), F32)],
        scratch_shapes=[pltpu.VMEM((LANES, LANES), F32)],
        compiler_params=_cparams(("parallel", "arbitrary")),
    )(qa, ka, va)


def _ret_bwd(qa, ka, va, v_off, states, do):
    s = qa.shape[0]
    nq = s // TQ

    def body(q_ref, k_ref, v_ref, st_ref, do_ref, dq_ref, dk_ref, dv_ref, dstate):
        p = pl.program_id(0)

        @pl.when(pl.program_id(1) == 0)
        def _():
            dstate[...] = jnp.zeros_like(dstate)

        lane, lg_lane, a, decays, own_head = _ret_geometry(p)
        q = q_ref[...].astype(F32)
        k = k_ref[...]
        kf = k.astype(F32)
        v = v_ref[...]
        dov = do_ref[...]
        s_in = st_ref[0, 0].astype(BF16)
        ds_next = dstate[...]
        ds_b = ds_next.astype(BF16)
        head_decay = jnp.exp(lg_lane * (a + 1.0))
        tail_decay = jnp.exp(lg_lane * (TQ - 1.0 - a))
        k_tail = (kf * tail_decay).astype(BF16)
        dq = _dot_nt(dov.astype(BF16), s_in) * head_decay
        dk = _dot_nt(v, ds_b) * tail_decay
        dv = _dot(k_tail, ds_b)
        for hh in range(2):
            hm = (lane // HEAD) == hh
            qh = jnp.where(hm, q, 0.0).astype(BF16)
            doh = jnp.where(hm, dov, 0.0).astype(BF16)
            att = (_dot_nt(qh, k) * decays[hh]).astype(BF16)
            datt = (_dot_nt(doh, v) * decays[hh]).astype(BF16)
            dv = dv + _dot_tn(att, doh)
            dk = dk + _dot_tn(datt, qh)
            dq = dq + jnp.where(hm, _dot(datt, k), 0.0)
        dq_ref[...] = dq
        dk_ref[...] = dk
        dv_ref[...] = dv
        q_head = (q * head_decay).astype(BF16)
        dstate[...] = jnp.exp(lg_lane * float(TQ)) * ds_next + jnp.where(own_head, _dot_tn(q_head, dov.astype(BF16)), 0.0)

    blk = lambda off: pl.BlockSpec((TQ, LANES), lambda p, i: (nq - 1 - i, off + p))
    return pl.pallas_call(
        body, name="ret_bwd", grid=(2, nq),
        in_specs=[blk(0), blk(0), blk(v_off), pl.BlockSpec((1, 1, LANES, LANES), lambda p, i: (p, nq - 1 - i, 0, 0)), blk(0)],
        out_specs=[blk(0)] * 3, out_shape=[jax.ShapeDtypeStruct((s, 2 * LANES), F32)] * 3,
        scratch_shapes=[pltpu.VMEM((LANES, LANES), F32)],
        compiler_params=_cparams(("parallel", "arbitrary")),
    )(qa, ka, va, states, do)


def _seg_mean_matrix():
    r = lax.broadcasted_iota(jnp.int32, (GROUP, GROUP), 0)
    c = lax.broadcasted_iota(jnp.int32, (GROUP, GROUP), 1)
    return jnp.where((r // HEAD) == (c // HEAD), 1.0 / HEAD, 0.0).astype(BF16)


def _seg_mean(x, seg):
    h = x.astype(BF16)
    r = x - h.astype(F32)
    m = r.astype(BF16)
    lo = (r - m.astype(F32)).astype(BF16)
    return _dot(h, seg) + _dot(m, seg) + _dot(lo, seg)


def _sigmoid(x):
    return 1.0 / (1.0 + jnp.exp(-x))


def _mix_post(oa, ob, oc, od, proj, g):
    s = oa.shape[0]
    tr = _tile(s, 256)

    def body(a_ref, b_ref, c_ref, d_ref, rg_ref, g_ref, o_ref):
        gv = g_ref[...]
        o_ref[:, 0:GROUP] = _rms(a_ref[...], gv[:, 0:GROUP]).astype(BF16)
        o_ref[:, GROUP:2 * GROUP] = _rms(b_ref[...], gv[:, GROUP:2 * GROUP]).astype(BF16)
        seg = _seg_mean_matrix()
        c = c_ref[...]
        cen = c - _seg_mean(c, seg)
        n = cen * lax.rsqrt(_seg_mean(cen * cen, seg) + EPS)
        rg = rg_ref[...]
        o_ref[:, 2 * GROUP:3 * GROUP] = (n * gv[:, 2 * GROUP:3 * GROUP] * (rg * _sigmoid(rg))).astype(BF16)
        o_ref[:, 3 * GROUP:] = _rms(d_ref[...], gv[:, 3 * GROUP:]).astype(BF16)

    blk = pl.BlockSpec((tr, GROUP), lambda i: (i, 0))
    return pl.pallas_call(
        body, name="mix_post", grid=(s // tr,),
        in_specs=[blk] * 4 + [pl.BlockSpec((tr, GROUP), lambda i: (i, OFF_RG // 2)), pl.BlockSpec((1, D_MODEL), lambda i: (0, 0))],
        out_specs=pl.BlockSpec((tr, D_MODEL), lambda i: (i, 0)), out_shape=jax.ShapeDtypeStruct((s, D_MODEL), BF16),
        compiler_params=_cparams(("parallel",)),
    )(oa, ob, oc, od, proj, g.reshape(1, D_MODEL))


def _mix_post_bwd(dmixed, oa, ob, oc, od, proj, g):
    s = oa.shape[0]
    tr = _tile(s, 256)

    def body(dm_ref, a_ref, b_ref, c_ref, d_ref, rg_ref, g_ref, da_ref, db_ref, dc_ref, dd_ref, drg_ref, dg_ref):
        @pl.when(pl.program_id(0) == 0)
        def _():
            dg_ref[...] = jnp.zeros_like(dg_ref)

        gv = g_ref[...]
        dm = dm_ref[...]
        for k, (x_ref, dx_ref) in enumerate(((a_ref, da_ref), (b_ref, db_ref), (None, None), (d_ref, dd_ref))):
            if x_ref is None:
                continue
            cols = slice(k * GROUP, (k + 1) * GROUP)
            dx, gterm = _rms_bwd(x_ref[...], gv[:, cols], dm[:, cols])
            dx_ref[...] = dx
            dg_ref[:, cols] += jnp.sum(gterm, axis=0, keepdims=True)
        cols = slice(2 * GROUP, 3 * GROUP)
        seg = _seg_mean_matrix()
        c = c_ref[...]
        cen = c - _seg_mean(c, seg)
        rstd = lax.rsqrt(_seg_mean(cen * cen, seg) + EPS)
        n = cen * rstd
        rg = rg_ref[...]
        sg = _sigmoid(rg)
        gate = rg * sg
        dy = dm[:, cols]
        gc = gv[:, cols]
        dn = dy * gc * gate
        dg_ref[:, cols] += jnp.sum(dy * n * gate, axis=0, keepdims=True)
        drg_ref[...] = (dy * n * gc * (sg * (1.0 + rg * (1.0 - sg)))).astype(BF16)
        dc_ref[...] = rstd * (dn - _seg_mean(dn, seg) - n * _seg_mean(dn * n, seg))

    blk = pl.BlockSpec((tr, GROUP), lambda i: (i, 0))
    gsp = pl.BlockSpec((1, D_MODEL), lambda i: (0, 0))
    return pl.pallas_call(
        body, name="mix_post_bwd", grid=(s // tr,),
        in_specs=[pl.BlockSpec((tr, D_MODEL), lambda i: (i, 0))] + [blk] * 4 + [pl.BlockSpec((tr, GROUP), lambda i: (i, OFF_RG // 2)), gsp],
        out_specs=[blk] * 5 + [gsp],
        out_shape=[jax.ShapeDtypeStruct((s, GROUP), F32)] * 4 + [jax.ShapeDtypeStruct((s, GROUP), BF16), jax.ShapeDtypeStruct((1, D_MODEL), F32)],
        compiler_params=_cparams(("arbitrary",)),
    )(dmixed, oa, ob, oc, od, proj, g.reshape(1, D_MODEL))


def _pack_w_in(w):
    z = lambda n: jnp.zeros((w.shape[0], n), w.dtype)
    misc = jnp.concatenate([w[:, 768:772], z(KR_LANE - N_HEADS), w[:, 1156:1188], z(LANES - KR_LANE - ROPE_DIM)], axis=1)
    return jnp.concatenate([w[:, 0:768], w[:, 772:1028], w[:, 1188:2980], w[:, 1028:1156], misc], axis=1)


def _unpack_dw_in(d):
    m = OFF_MISC * LANES
    return jnp.concatenate([d[:, 0:768], d[:, m:m + N_HEADS], d[:, 768:1024], d[:, OFF_CKV * LANES:m],
                            d[:, m + KR_LANE:m + KR_LANE + ROPE_DIM], d[:, 1024:OFF_CKV * LANES]], axis=1)


def _pack_w_q(w):
    return jnp.pad(w.reshape(Q_RANK, N_HEADS, HEAD + ROPE_DIM), ((0, 0), (0, 0), (0, LANES - HEAD - ROPE_DIM))).reshape(Q_RANK, 4 * LANES)


def _unpack_dw_q(d):
    return d.reshape(Q_RANK, N_HEADS, LANES)[:, :, :HEAD + ROPE_DIM].reshape(Q_RANK, N_HEADS * (HEAD + ROPE_DIM))


def _pack_w_kv(w):
    w4 = w.reshape(KV_RANK, N_HEADS, 2 * HEAD)
    widen = lambda a: jnp.pad(a, ((0, 0), (0, 0), (0, LANES - HEAD))).reshape(KV_RANK, N_HEADS * LANES)
    return widen(w4[:, :, :HEAD]), widen(w4[:, :, HEAD:])


def _unpack_dw_kv(dk, dv):
    narrow = lambda a: a.reshape(KV_RANK, N_HEADS, LANES)[:, :, :HEAD]
    return jnp.concatenate([narrow(dk), narrow(dv)], axis=2).reshape(KV_RANK, 2 * N_HEADS * HEAD)


def _narrow_heads(a):
    return a.reshape(a.shape[0], N_HEADS, LANES)[:, :, :HEAD].reshape(a.shape[0], N_HEADS * HEAD)


def _widen_heads(a):
    return jnp.pad(a.reshape(a.shape[0], N_HEADS, HEAD), ((0, 0), (0, 0), (0, LANES - HEAD))).reshape(a.shape[0], N_HEADS * LANES)


def _layer_fwd(x, lw, tabs, tag, side=None, fox_side=None, late_weights=None, h1=None, next_gain=None):
    cos_m, sin_m, cos_r, sin_r = tabs
    if h1 is None:
        h1 = _norm_fwd(x, lw["g_mix_pre"], name=tag + "pre_norm")
    proj, projb = _matmul(h1, lw["w_in"], name=tag + "in_proj", also_bf16=True)
    bias_row = jnp.pad(lw["b_forget"], (FF_LANE, LANES - N_HEADS - FF_LANE)).reshape(1, LANES)
    cum_col, cum_row = _fox_cum(proj, bias_row)
    oa, lse_a, *fox_carried = _mixer_fwd("fox", projb, OFF_FQ, projb, OFF_FK, projb, OFF_FV, cum_col=cum_col, cum_row=cum_row,
                                         side=fox_side)
    if late_weights is not None:
        lw = {**lw, **late_weights(fox_carried[0])}
    qm, km, vm, cqn, ckvn = _mla_prep(proj, cos_m, sin_m, lw["g_q_lora"], lw["g_kv_lora"], lw["wq"], lw["wk"], lw["wv"])
    ob_wide, lse_b = _mixer_fwd("mla", qm, 0, km, 0, vm, 0)
    ob = _narrow_heads(ob_wide)
    qr, kr = _ret_prep(proj, cos_r, sin_r)
    oc, ret_states = _ret_fwd(qr, kr, projb, OFF_RV)
    od, tot_d, *carried = _mixer_fwd("sb", projb, OFF_SQ, projb, OFF_SK, projb, OFF_SV, side=side)
    mixed = _mix_post(oa, ob, oc, od, proj, lw["g_mix_out"])
    mix = _matmul(mixed, lw["w_out"], name=tag + "out_proj")
    x1, h2 = _norm_fwd(mix, lw["g_mix_post"], name=tag + "mix_post_norm", resid=x, out_dtype=F32, next_gain=lw["g_ffn_pre"])
    u = _matmul(h2, lw["w_ffn_up"], name=tag + "ffn_up", relu2=True, out_dtype=BF16, col_blocks=True)
    f = _matmul(u, lw["w_ffn_down"], name=tag + "ffn_down")
    x2, h_next = None, None
    if next_gain is not None:
        x2, h_next = _norm_fwd(f, lw["g_ffn_post"], name=tag + "ffn_post_norm", resid=x1, out_dtype=F32, next_gain=next_gain)
    saved = dict(x=x, h1=h1, proj=proj, projb=projb, bias_row=bias_row, cum_col=cum_col, cum_row=cum_row, oa=oa, lse_a=lse_a,
                 qm=qm, km=km, vm=vm, cqn=cqn, ckvn=ckvn, ob=ob, ob_wide=ob_wide, lse_b=lse_b, qr=qr, kr=kr, ret_states=ret_states, oc=oc, od=od, tot_d=tot_d, mixed=mixed,
                 mix=mix, x1=x1, h2=h2, u=u, f=f)
    return x2, saved, lw, (carried[0] if carried else None), h_next


def _layer_bwd(dx2, lw, sv, tabs, tag, side=None, ffn_side=None, fox_side=None, post_given=None, then_prev=None):
    cos_m, sin_m, cos_r, sin_r = tabs
    g = {}
    if post_given is None:
        df, g["g_ffn_post"] = _norm_bwd(sv["f"], lw["g_ffn_post"], dx2, name=tag + "ffn_post_norm_bwd", out_dtype=BF16)
    else:
        df, g["g_ffn_post"] = post_given
    du_pre = _matmul(df, lw["w_ffn_down"], name=tag + "ffn_down_dx", tb=True, out_dtype=BF16, relu2_of=sv["u"], side=ffn_side)
    ffn_carried = None
    if ffn_side is not None:
        du_pre, ffn_carried = du_pre
    g["w_ffn_down"] = _matmul(sv["u"], df, name=tag + "ffn_down_dw", ta=True)
    dh2 = _matmul(du_pre, lw["w_ffn_up"], name=tag + "ffn_up_dx", tb=True, col_blocks=True)
    g["w_ffn_up"] = _matmul(sv["h2"], du_pre, name=tag + "ffn_up_dw", ta=True, col_blocks=True)
    dx1, g["g_ffn_pre"], dmix, g["g_mix_post"] = _norm_bwd(sv["x1"], lw["g_ffn_pre"], dh2, name=tag + "ffn_pre_norm_bwd", add=dx2,
                                                           then=(sv["mix"], lw["g_mix_post"]))
    dmixed = _matmul(dmix, lw["w_out"], name=tag + "out_proj_dx", tb=True)
    g["w_out"] = _matmul(sv["mixed"], dmix, name=tag + "out_proj_dw", ta=True)
    proj, projb = sv["proj"], sv["projb"]
    doa, dob, doc, dod, drg, g["g_mix_out"] = _mix_post_bwd(dmixed, sv["oa"], sv["ob"], sv["oc"], sv["od"], proj, lw["g_mix_out"])
    dfq, dfk, dfv, dck, drs, *fox_carried = _mixer_bwd(
        "fox", projb, OFF_FQ, projb, OFF_FK, projb, OFF_FV, sv["oa"], doa, stat=sv["lse_a"], cum_col=sv["cum_col"],
        cum_row=sv["cum_row"], side=None if fox_side is None else fox_side(g))
    dqm, dkm, dvm = _mixer_bwd("mla", sv["qm"], 0, sv["km"], 0, sv["vm"], 0, sv["ob_wide"], _widen_heads(dob), stat=sv["lse_b"])
    dcq, dckv, dkr, dwq, dwk, dwv, g["g_q_lora"], g["g_kv_lora"] = _mla_prep_bwd(
        dqm, dkm, dvm, proj, sv["cqn"], sv["ckvn"], cos_m, sin_m, lw["g_q_lora"], lw["g_kv_lora"], lw["wq"], lw["wk"], lw["wv"])
    dqr, dkr_ret, drv = _ret_bwd(sv["qr"], sv["kr"], projb, OFF_RV, sv["ret_states"], doc)
    drq, drk = _ret_prep_bwd(dqr, dkr_ret, cos_r, sin_r)
    if callable(side):
        side = side(g, ffn_carried, fox_carried[0] if fox_carried else None)
    dsq, dsk, dsv, *carried = _mixer_bwd("sb", projb, OFF_SQ, projb, OFF_SK, projb, OFF_SV, sv["od"], dod, stat=sv["tot_d"], side=side)
    dmisc, db_row = _fox_gate_bwd(dck, drs, proj, sv["bias_row"], dkr)
    b = lambda a: a.astype(BF16)
    dproj = jnp.concatenate([b(dfq), b(dfk), b(dfv), dcq, drq, drk, b(drv), drg, b(dsq), b(dsk), b(dsv), dckv, dmisc], axis=1)
    dh1 = _matmul(dproj, lw["w_in"], name=tag + "in_proj_dx", tb=True)
    g["w_in"] = _matmul(sv["h1"], dproj, name=tag + "in_proj_dw", ta=True)
    dx, g["g_mix_pre"], *prev_post = _norm_bwd(sv["x"], lw["g_mix_pre"], dh1, name=tag + "pre_norm_bwd", add=dx1, then=then_prev)
    g["b_forget"] = db_row[0, FF_LANE:FF_LANE + N_HEADS]
    g["wq"], g["wk"], g["wv"] = dwq, dwk, dwv
    return dx, g, (carried[0] if carried else None), (tuple(prev_post) if prev_post else None)


def _local_step(x, positions, layers, target):
    s = x.shape[0]
    tabs = _rope_tables(positions.reshape(s, 1))
    saved, h1 = [], None
    for li, lw in enumerate(layers):
        nxt = layers[li + 1]["g_mix_pre"] if li + 1 < len(layers) else None
        x, sv, _, _, h1 = _layer_fwd(x, lw, tabs, "l%d_" % li, h1=h1, next_gain=nxt)
        saved.append(sv)
    loss_row, dx, df, dg = _loss_head(saved[-1]["f"], layers[-1]["g_ffn_post"], saved[-1]["x1"], target)
    grads, post = [None] * len(layers), (df, dg)
    for li in reversed(range(len(layers))):
        prev = (saved[li - 1]["f"], layers[li - 1]["g_ffn_post"]) if li > 0 else None
        dx, grads[li], _, post = _layer_bwd(dx, layers[li], saved[li], tabs, "l%d_" % li, post_given=post, then_prev=prev)
    return loss_row[0, 0], dx, grads


def _adamw(w, g, m, v, *, name):
    r, c = w.shape
    tr = 256 if r % 256 == 0 else r
    blk = pl.BlockSpec((tr, c), lambda i: (i, 0))
    c1 = 1.0 - ADAM_B1 ** ADAM_STEP
    c2 = 1.0 - ADAM_B2 ** ADAM_STEP

    def body(w_ref, g_ref, m_ref, v_ref, d_ref, mo_ref, vo_ref):
        gv = g_ref[...]
        mn = ADAM_B1 * m_ref[...] + (1.0 - ADAM_B1) * gv
        vn = ADAM_B2 * v_ref[...] + (1.0 - ADAM_B2) * jnp.square(gv)
        mo_ref[...] = mn
        vo_ref[...] = vn
        d_ref[...] = -ADAM_LR * ((mn / c1) / (jnp.sqrt(vn / c2) + ADAM_EPS) + ADAM_WD * w_ref[...])

    return pl.pallas_call(
        body, name=name, grid=(r // tr,), in_specs=[blk] * 4, out_specs=[blk] * 3,
        out_shape=[jax.ShapeDtypeStruct((r, c), F32)] * 3, compiler_params=_cparams(("parallel",)),
    )(w, g, m, v)


BIG = ("w_in", "w_q_up", "w_kv_up", "w_out", "w_ffn_up", "w_ffn_down")
SMALL = ("g_mix_pre", "b_forget", "g_q_lora", "g_kv_lora", "g_mix_out", "g_mix_post", "g_ffn_pre", "g_ffn_post")
N_CHIPS = 4
ANY = pl.BlockSpec(memory_space=pl.ANY)


def _mesh_pos():
    return lax.axis_index("x"), lax.axis_index("y"), lax.axis_index("c")


def _other_chips(x, y):
    return [(1 - x, y), (x, 1 - y), (1 - x, 1 - y)]


def _rows_half(ref, half):
    h = ref.shape[-2] // 2
    return ref.at[(slice(None),) * (len(ref.shape) - 2) + (pl.ds(half * h, h), slice(None))]


def _remote(src, dst, send_sem, recv_sem, device):
    return pltpu.make_async_remote_copy(src_ref=src, dst_ref=dst, send_sem=send_sem, recv_sem=recv_sem, device_id=device,
                                        device_id_type=MESH)


def _comm_call(body, name, args, out_shape, n_sems):
    return pl.pallas_call(
        body, name=name, in_specs=[ANY] * len(args), out_specs=[ANY] * len(out_shape), out_shape=out_shape,
        scratch_shapes=[pltpu.SemaphoreType.DMA((n_sems,)), pltpu.SemaphoreType.DMA((n_sems,))],
        compiler_params=pltpu.CompilerParams(has_side_effects=True),
    )(*args)


def _run_side_job(side, name):
    si = len(side.inputs)

    def body(*refs):
        args = (refs[:si], refs[si:-2], refs[-2], refs[-1])
        sends = side.sends(*args)
        for cp in sends:
            cp.start()
        for cp in side.recvs(*args):
            cp.wait_recv()
        for cp in sends:
            cp.wait_send()

    return _comm_call(body, name, side.inputs, side.out_shape, side.n_sems)


def _gather_job(shards):
    n = len(shards)

    def copies(own_block, ins, outs, send_sems, recv_sems):
        x, y, c = _mesh_pos()
        return [_remote(_rows_half(ins[t], c), _rows_half(outs[t].at[2 * x + y if own_block else 2 * px + py], c),
                        send_sems.at[3 * t + j], recv_sems.at[3 * t + j], (px, py, c))
                for t in range(n) for j, (px, py) in enumerate(_other_chips(x, y))]

    return _SideJob(shards, [jax.ShapeDtypeStruct((N_CHIPS,) + a.shape, a.dtype) for a in shards], 3 * n,
                    functools.partial(copies, True), functools.partial(copies, False))


def _forward_halves(gathered):
    n = len(gathered)

    def body(*refs):
        bufs, send_sems, recv_sems = refs[n:2 * n], refs[-2], refs[-1]
        x, y, c = _mesh_pos()

        def d2d(t, j, block, half):
            region = _rows_half(bufs[t].at[block], half)
            return _remote(region, region, send_sems.at[3 * t + j], recv_sems.at[3 * t + j], (x, y, 1 - c))

        peers = list(enumerate(_other_chips(x, y)))
        sends = [d2d(t, j, 2 * px + py, c) for t in range(n) for j, (px, py) in peers]
        for cp in sends:
            cp.start()
        for t in range(n):
            for j, (px, py) in peers:
                d2d(t, j, 2 * px + py, 1 - c).wait_recv()
        for cp in sends:
            cp.wait_send()

    return pl.pallas_call(
        body, name="gather_forward", in_specs=[ANY] * n, out_specs=[ANY] * n,
        out_shape=[jax.ShapeDtypeStruct(g.shape, g.dtype) for g in gathered], input_output_aliases={t: t for t in range(n)},
        scratch_shapes=[pltpu.SemaphoreType.DMA((3 * n,)), pltpu.SemaphoreType.DMA((3 * n,))],
        compiler_params=pltpu.CompilerParams(has_side_effects=True),
    )(*gathered)


def _exchange_halves_job(gs):
    n = len(gs)

    def copies(ins, outs, send_sems, recv_sems):
        x, y, c = _mesh_pos()
        return [_remote(_rows_half(ins[t], 1 - c), outs[t], send_sems.at[t], recv_sems.at[t], (x, y, 1 - c)) for t in range(n)]

    out_shape = [jax.ShapeDtypeStruct(g.shape[:2] + (g.shape[2] // 2, g.shape[3]), g.dtype) for g in gs]
    return _SideJob(gs, out_shape, n, copies, copies)


def _pair_add(g, r, c_idx, *, name):
    nb, d, rows, cols = g.shape
    h = rows // 2
    tr = min(h, 512)
    nt = h // tr

    def body(c_ref, g_ref, r_ref, p_ref, pb_ref):
        s = g_ref[...] + r_ref[...]
        p_ref[...] = s
        pb_ref[...] = s.astype(BF16)

    blk = pl.BlockSpec((1, 1, tr, cols), lambda k, l, i, c_ref: (k, l, i, 0))
    return pl.pallas_call(
        body, name=name,
        grid_spec=pltpu.PrefetchScalarGridSpec(
            num_scalar_prefetch=1, grid=(nb, d, nt),
            in_specs=[pl.BlockSpec((1, 1, tr, cols), lambda k, l, i, c_ref: (k, l, c_ref[0] * nt + i, 0)), blk],
            out_specs=[blk, blk]),
        out_shape=[jax.ShapeDtypeStruct((nb, d, h, cols), F32), jax.ShapeDtypeStruct((nb, d, h, cols), BF16)],
        compiler_params=_cparams(("parallel", "parallel", "parallel")),
    )(c_idx, g, r)


def _exchange_chips_job(pbs):
    n = len(pbs)

    def copies(ins, outs, send_sems, recv_sems):
        x, y, c = _mesh_pos()
        return [_remote(ins[t].at[2 * px + py], outs[t].at[j], send_sems.at[3 * t + j], recv_sems.at[3 * t + j], (px, py, c))
                for t in range(n) for j, (px, py) in enumerate(_other_chips(x, y))]

    return _SideJob(pbs, [jax.ShapeDtypeStruct((3,) + p.shape[1:], p.dtype) for p in pbs], 3 * n, copies, copies)


def _chip_add(p, r, k_idx, *, name):
    _, d, h, cols = p.shape
    tr = min(h, 512)
    nt = h // tr

    def body(k_ref, p_ref, r_ref, o_ref):
        o_ref[0] = ((p_ref[0, 0] + r_ref[0, 0].astype(F32)) + r_ref[1, 0].astype(F32)) + r_ref[2, 0].astype(F32)

    return pl.pallas_call(
        body, name=name,
        grid_spec=pltpu.PrefetchScalarGridSpec(
            num_scalar_prefetch=1, grid=(d, nt),
            in_specs=[pl.BlockSpec((1, 1, tr, cols), lambda l, i, k_ref: (k_ref[0], l, i, 0)),
                      pl.BlockSpec((3, 1, tr, cols), lambda l, i, k_ref: (0, l, i, 0))],
            out_specs=pl.BlockSpec((1, tr, cols), lambda l, i, k_ref: (l, i, 0))),
        out_shape=jax.ShapeDtypeStruct((d, h, cols), F32), compiler_params=_cparams(("parallel", "parallel")),
    )(k_idx, p, r)


def _share_halves(qs):
    n = len(qs)

    def body(*refs):
        ins, outs, send_sems, recv_sems = refs[:n], refs[n:2 * n], refs[2 * n], refs[2 * n + 1]
        x, y, c = _mesh_pos()
        cps = [_remote(ins[t], outs[t], send_sems.at[t], recv_sems.at[t], (x, y, 1 - c)) for t in range(n)]
        for cp in cps:
            cp.start()
        for cp in cps:
            cp.wait_recv()
        for cp in cps:
            cp.wait_send()

    return _comm_call(body, "grad_pair_share", qs, [jax.ShapeDtypeStruct(q.shape, q.dtype) for q in qs], n)


def _all_reduce_small(v):
    r, cols = v.shape
    n_dev = 8

    def body(v_ref, o_ref, buf, send_sems, recv_sems):
        x, y, c = _mesh_pos()
        me = 4 * x + 2 * y + c
        buf[me] = v_ref[...]

        def peer(j):
            return (1 - x if j & 4 else x, 1 - y if j & 2 else y, 1 - c if j & 1 else c)

        def copy(j, slot):
            return pltpu.make_async_remote_copy(src_ref=v_ref, dst_ref=buf.at[slot], send_sem=send_sems.at[j - 1],
                                                recv_sem=recv_sems.at[j - 1], device_id=peer(j), device_id_type=MESH)

        sends = [copy(j, me) for j in range(1, n_dev)]
        for cp in sends:
            cp.start()
        for j in range(1, n_dev):
            px, py, pc = peer(j)
            copy(j, 4 * px + 2 * py + pc).wait_recv()
        for cp in sends:
            cp.wait_send()
        acc = buf[0]
        for d in range(1, n_dev):
            acc = acc + buf[d]
        o_ref[...] = acc

    vm = pl.BlockSpec(memory_space=pltpu.VMEM)
    return pl.pallas_call(
        body, name="small_all_reduce", in_specs=[vm], out_specs=vm, out_shape=jax.ShapeDtypeStruct((r, cols), F32),
        scratch_shapes=[pltpu.VMEM((n_dev, r, cols), F32), pltpu.SemaphoreType.DMA((n_dev - 1,)), pltpu.SemaphoreType.DMA((n_dev - 1,))],
        compiler_params=pltpu.CompilerParams(has_side_effects=True),
    )(v)


_COL_SHARDED = ("w_in", "w_q_up", "w_kv_up", "w_ffn_up")


def _shard_cols(blocks, a, b):
    c = blocks[0].shape[-1]
    out = []
    while a < b:
        k = a // c
        hi = min(b, (k + 1) * c)
        out.append(blocks[k][:, a - k * c:hi - k * c])
        a = hi
    return out


def _pack_w_in_shards(blocks):
    z = lambda n: [jnp.zeros((blocks[0].shape[0], n), blocks[0].dtype)]
    cols = lambda a, b: _shard_cols(blocks, a, b)
    return jnp.concatenate(cols(0, 768) + cols(772, 1028) + cols(1188, 2980) + cols(1028, 1156) + cols(768, 772)
                           + z(KR_LANE - N_HEADS) + cols(1156, 1188) + z(LANES - KR_LANE - ROPE_DIM), axis=1)


def _whole_layer(name, blocks):
    if name in _COL_SHARDED:
        return jnp.concatenate([blocks[k] for k in range(N_CHIPS)], axis=1)
    return blocks.reshape(N_CHIPS * blocks.shape[1], blocks.shape[2])


def _split_layer(name, whole):
    if name in _COL_SHARDED:
        c = whole.shape[1] // N_CHIPS
        return jnp.stack([whole[:, k * c:(k + 1) * c] for k in range(N_CHIPS)])
    return whole.reshape(N_CHIPS, whole.shape[0] // N_CHIPS, whole.shape[1])


def _small_to_rows(d):
    v = jnp.concatenate([d[k].astype(F32).reshape(-1) for k in SMALL])
    rows = -(-v.shape[0] // (8 * LANES)) * 8
    return jnp.pad(v, (0, rows * LANES - v.shape[0])).reshape(rows, LANES)


def _small_from_rows(rows, shapes):
    v = rows.reshape(-1)
    out, o = {}, 0
    for k in SMALL:
        sz = int(np.prod(shapes[k]))
        out[k] = v[o:o + sz].reshape(shapes[k])
        o += sz
    return out


_ARG_NAMES = ("x", "positions", "g_mix_pre", "w_in", "b_forget", "g_q_lora", "w_q_up", "g_kv_lora", "w_kv_up", "g_mix_out", "w_out",
              "g_mix_post", "g_ffn_pre", "w_ffn_up", "w_ffn_down", "g_ffn_post")
_WEIGHTS = _ARG_NAMES[2:]


def kernel(x, positions, g_mix_pre, w_in, b_forget, g_q_lora, w_q_up, g_kv_lora, w_kv_up, g_mix_out, w_out, g_mix_post, g_ffn_pre, w_ffn_up, w_ffn_down, g_ffn_post, loss_target, m_g_mix_pre, m_w_in, m_b_forget, m_g_q_lora, m_w_q_up, m_g_kv_lora, m_w_kv_up, m_g_mix_out, m_w_out, m_g_mix_post, m_g_ffn_pre, m_w_ffn_up, m_w_ffn_down, m_g_ffn_post, v_g_mix_pre, v_w_in, v_b_forget, v_g_q_lora, v_w_q_up, v_g_kv_lora, v_w_kv_up, v_g_mix_out, v_w_out, v_g_mix_post, v_g_ffn_pre, v_w_ffn_up, v_w_ffn_down, v_g_ffn_post):
    w = dict(g_mix_pre=g_mix_pre, w_in=w_in, b_forget=b_forget, g_q_lora=g_q_lora, w_q_up=w_q_up, g_kv_lora=g_kv_lora, w_kv_up=w_kv_up,
             g_mix_out=g_mix_out, w_out=w_out, g_mix_post=g_mix_post, g_ffn_pre=g_ffn_pre, w_ffn_up=w_ffn_up, w_ffn_down=w_ffn_down,
             g_ffn_post=g_ffn_post)
    m = dict(g_mix_pre=m_g_mix_pre, w_in=m_w_in, b_forget=m_b_forget, g_q_lora=m_g_q_lora, w_q_up=m_w_q_up, g_kv_lora=m_g_kv_lora,
             w_kv_up=m_w_kv_up, g_mix_out=m_g_mix_out, w_out=m_w_out, g_mix_post=m_g_mix_post, g_ffn_pre=m_g_ffn_pre,
             w_ffn_up=m_w_ffn_up, w_ffn_down=m_w_ffn_down, g_ffn_post=m_g_ffn_post)
    v = dict(g_mix_pre=v_g_mix_pre, w_in=v_w_in, b_forget=v_b_forget, g_q_lora=v_g_q_lora, w_q_up=v_w_q_up, g_kv_lora=v_g_kv_lora,
             w_kv_up=v_w_kv_up, g_mix_out=v_g_mix_out, w_out=v_w_out, g_mix_post=v_g_mix_post, g_ffn_pre=v_g_ffn_pre,
             w_ffn_up=v_w_ffn_up, w_ffn_down=v_w_ffn_down, g_ffn_post=v_g_ffn_post)
    shard_shapes = {k: w[k].shape for k in BIG}
    small_shapes = {k: w[k].shape for k in SMALL}
    c_idx = lax.axis_index("c").astype(jnp.int32).reshape(1)
    k_idx = (2 * lax.axis_index("x") + lax.axis_index("y")).astype(jnp.int32).reshape(1)
    first_core = lax.axis_index("c") == 0

    mine = 2 * lax.axis_index("x") + lax.axis_index("y")
    shards_b = [{k: w[k][l:l + 1].astype(BF16) for k in BIG} for l in range(DEPTH)]
    gains = [dict(g_mix_pre=g_mix_pre[l], b_forget=b_forget[l], g_q_lora=g_q_lora[l], g_kv_lora=g_kv_lora[l], g_mix_out=g_mix_out[l],
                  g_mix_post=g_mix_post[l], g_ffn_pre=g_ffn_pre[l], g_ffn_post=g_ffn_post[l]) for l in range(DEPTH)]
    FIRST, LATER = ("w_in", "w_q_up", "w_kv_up"), ("w_out", "w_ffn_up", "w_ffn_down")
    EARLY_GRADS, LATE_GRADS = ("w_ffn_down", "w_ffn_up", "w_out"), ("w_in", "w_q_up", "w_kv_up")

    def gather_job(l, names):
        return _gather_job([shards_b[l][k] for k in names])

    def weights_of(l, names, gathered):
        four = {k: lax.dynamic_update_slice(g, shards_b[l][k][None], (mine, 0, 0, 0))[:, 0]
                for k, g in zip(names, _forward_halves(gathered))}
        out = {}
        for k in names:
            if k == "w_in":
                out["w_in"] = _pack_w_in_shards(four[k])
            elif k == "w_q_up":
                out["wq"] = _pack_w_q(_whole_layer(k, four[k]))
            elif k == "w_kv_up":
                out["wk"], out["wv"] = _pack_w_kv(_whole_layer(k, four[k]))
            elif k == "w_ffn_up":
                out[k] = four[k]
            else:
                out[k] = _whole_layer(k, four[k])
        return out

    def grad_blocks(names, g):
        whole = dict(w_in=lambda: _unpack_dw_in(g["w_in"]), w_q_up=lambda: _unpack_dw_q(g["wq"]),
                     w_kv_up=lambda: _unpack_dw_kv(g["wk"], g["wv"]), w_out=lambda: g["w_out"], w_ffn_down=lambda: g["w_ffn_down"])
        return [(g[k] if k == "w_ffn_up" else _split_layer(k, whole[k]()))[:, None] for k in names]

    def pair_sums(names, blocks, theirs):
        return [_pair_add(b, r, c_idx, name="grad_pair_add_" + k) for k, b, r in zip(names, blocks, theirs)]

    def exchange_job(*pairs):
        return _exchange_chips_job([pb for pair in pairs for (_, pb) in pair])

    def finish_grads(names, pair, partial):
        half = [_chip_add(p, r, k_idx, name="grad_chip_add_" + k) for k, (p, _), r in zip(names, pair, partial)]
        return {k: jnp.where(first_core, jnp.concatenate([q, s], axis=1), jnp.concatenate([s, q], axis=1))
                for k, q, s in zip(names, half, _share_halves(half))}

    seq = x.shape[1]
    tabs = _rope_tables(positions[0].reshape(seq, 1))
    first0 = weights_of(0, FIRST, _run_side_job(gather_job(0, FIRST), "gather_weights_l0"))
    x1, saved0, lw0, gathered1, h1 = _layer_fwd(x[0], {**gains[0], **first0}, tabs, "l0_", fox_side=gather_job(0, LATER),
                                                late_weights=lambda got: weights_of(0, LATER, got), side=gather_job(1, BIG),
                                                next_gain=gains[1]["g_mix_pre"])
    lw1 = {**gains[1], **weights_of(1, BIG, gathered1)}
    _, saved1, _, _, _ = _layer_fwd(x1, lw1, tabs, "l1_", h1=h1)
    loss_row, dx, df1, dg1 = _loss_head(saved1["f"], lw1["g_ffn_post"], saved1["x1"], loss_target[0])
    loss = lax.psum(loss_row[0, 0], ("x", "y", "c"))
    dx, grads1, _, post0 = _layer_bwd(dx, lw1, saved1, tabs, "l1_", post_given=(df1, dg1),
                                      then_prev=(saved0["f"], lw0["g_ffn_post"]))
    blocks1 = grad_blocks(BIG, grads1)
    early_blocks0, pair1, early0 = [], [], []

    def beside_l0_fox_backward(g):
        early_blocks0.extend(grad_blocks(EARLY_GRADS, g))
        return _exchange_halves_job(early_blocks0)

    def beside_l0_sb_backward(g, theirs1, theirs_early0):
        pair1.extend(pair_sums(BIG, blocks1, theirs1))
        early0.extend(pair_sums(EARLY_GRADS, early_blocks0, theirs_early0))
        return exchange_job(pair1, early0)

    dx, grads0, partial, _ = _layer_bwd(dx, lw0, saved0, tabs, "l0_", ffn_side=_exchange_halves_job(blocks1),
                                        fox_side=beside_l0_fox_backward, side=beside_l0_sb_backward, post_given=post0)
    big1 = finish_grads(BIG, pair1, partial[:len(BIG)])
    big0 = finish_grads(EARLY_GRADS, early0, partial[len(BIG):])
    late_blocks0 = grad_blocks(LATE_GRADS, grads0)
    late0 = pair_sums(LATE_GRADS, late_blocks0, _run_side_job(_exchange_halves_job(late_blocks0), "grad_pair_exchange_l0"))
    big0.update(finish_grads(LATE_GRADS, late0, _run_side_job(exchange_job(late0), "grad_chip_exchange_l0")))
    g_big = {k: jnp.concatenate([big0[k], big1[k]], axis=0) for k in BIG}
    grads = [grads0, grads1]

    g_small_local = {k: jnp.stack([grads[l][k].reshape(small_shapes[k][1:]) for l in range(DEPTH)]) for k in SMALL}
    g_small = _small_from_rows(_all_reduce_small(_small_to_rows(g_small_local)), small_shapes)

    g_all = {**g_big, **g_small}
    delta, new_m, new_v = {}, {}, {}
    for k in BIG:
        d, r, c = shard_shapes[k]
        two_d = lambda a: a.reshape(d * r, c)
        dk, mk, vk = _adamw(two_d(w[k]), two_d(g_all[k]), two_d(m[k]), two_d(v[k]), name="adamw_" + k)
        delta[k], new_m[k], new_v[k] = dk.reshape(d, r, c), mk.reshape(d, r, c), vk.reshape(d, r, c)
    ds, ms, vs = _adamw(_small_to_rows(w), _small_to_rows(g_small), _small_to_rows(m), _small_to_rows(v), name="adamw_small")
    delta.update(_small_from_rows(ds, small_shapes))
    new_m.update(_small_from_rows(ms, small_shapes))
    new_v.update(_small_from_rows(vs, small_shapes))

    grad_x = dx.reshape(x.shape)
    return (loss, grad_x, *[g_all[k] for k in _WEIGHTS], *[delta[k] for k in _WEIGHTS], *[new_m[k] for k in _WEIGHTS],
            *[new_v[k] for k in _WEIGHTS])
```

```python
import functools
import math

import numpy as np
import jax
import jax.numpy as jnp
from jax import lax
from jax.experimental import pallas as pl
from jax.experimental.pallas import tpu as pltpu

F32 = jnp.float32
BF16 = jnp.bfloat16
MESH = pl.DeviceIdType.MESH

D_MODEL = 1024
DEPTH = 2
CHUNK = 64
GROUP = 256
HEAD = 64
N_HEADS = 4
Q_RANK = 256
KV_RANK = 128
ROPE_DIM = 32
D_FF = 4096
D_IN = 2980
D_INP = 3072
ROPE_BASE = 10000.0
EPS = 1e-6
LANES = 128
TQ = 128
GATE_ROWS = 1024
NEG = -1e30

ADAM_LR, ADAM_B1, ADAM_B2, ADAM_EPS, ADAM_WD, ADAM_STEP = 0.001, 0.9, 0.999, 1e-08, 0.01, 10

OFF_FQ, OFF_FK, OFF_FV, OFF_CQ = 0, 2, 4, 6
OFF_RQ, OFF_RK, OFF_RV, OFF_RG = 8, 10, 12, 14
OFF_SQ, OFF_SK, OFF_SV = 16, 18, 20
OFF_CKV, OFF_MISC = 22, 23
FF_LANE, KR_LANE = 0, 64

VMEM_LIMIT = 56 * 1024 * 1024


def _tile(dim, pref):
    return pref if dim % pref == 0 else dim


def _cparams(sem, vmem=None):
    return pltpu.CompilerParams(dimension_semantics=sem, vmem_limit_bytes=vmem or VMEM_LIMIT)


def _dot(a, b):
    return jnp.dot(a, b, preferred_element_type=F32)


def _dot_nt(a, b):
    return lax.dot_general(a, b, (((1,), (1,)), ((), ())), preferred_element_type=F32)


def _dot_tn(a, b):
    return lax.dot_general(a, b, (((0,), (0,)), ((), ())), preferred_element_type=F32)


def _dot_exact(a, b):
    return jnp.dot(a, b, precision=lax.Precision.HIGHEST, preferred_element_type=F32)


def _matmul(a, b, *, name, ta=False, tb=False, out_dtype=F32, tm=1024, tn=1024, tk=1024,
            relu2=False, relu2_of=None, also_bf16=False, side=None, col_blocks=False):
    if ta:
        kdim, m = a.shape
    else:
        m, kdim = a.shape
    if col_blocks and not ta:
        n = b.shape[1] if tb else b.shape[0] * b.shape[2]
        if tb:
            kdim = b.shape[0] * b.shape[2]
    else:
        n = b.shape[0] if tb else b.shape[1]
    tm, tn, tk = _tile(m, tm), _tile(n, tn), _tile(kdim, tk)
    nk = kdim // tk
    a_spec = pl.BlockSpec((tk, tm), lambda i, j, k: (k, i)) if ta else pl.BlockSpec((tm, tk), lambda i, j, k: (i, k))
    b_spec = pl.BlockSpec((tn, tk), lambda i, j, k: (j, k)) if tb else pl.BlockSpec((tk, tn), lambda i, j, k: (k, j))
    o_spec = pl.BlockSpec((tm, tn), lambda i, j, k: (i, j))
    if col_blocks and ta:
        o_spec = pl.BlockSpec((None, tm, tn), lambda i, j, k: (j, i, 0))
    elif col_blocks and tb:
        assert b.shape[2] == tk
        b_spec = pl.BlockSpec((None, tn, tk), lambda i, j, k: (k, j, 0))
    elif col_blocks:
        assert b.shape[2] == tn
        b_spec = pl.BlockSpec((None, tk, tn), lambda i, j, k: (j, k, 0))
    two = also_bf16

    def body(*refs):
        refs = list(refs)
        a_ref, b_ref = refs[0], refs[1]
        e_ref = refs[2] if relu2_of is not None else None
        pos = 3 if relu2_of is not None else 2
        o_ref = refs[pos]
        o2_ref = refs[pos + 1] if two else None
        acc_ref = refs[-1]
        k = pl.program_id(2)
        av = a_ref[...].astype(BF16)
        bv = b_ref[...].astype(BF16)
        if ta:
            part = _dot_tn(av, bv)
        elif tb:
            part = _dot_nt(av, bv)
        else:
            part = _dot(av, bv)

        @pl.when(k == 0)
        def _():
            acc_ref[...] = part

        @pl.when(k > 0)
        def _():
            acc_ref[...] += part

        @pl.when(k == nk - 1)
        def _():
            r = acc_ref[...]
            if relu2_of is not None:
                r = r * (2.0 * jnp.sqrt(e_ref[...].astype(F32)))
            if relu2:
                r = jnp.square(jnp.maximum(r, 0.0))
            o_ref[...] = r.astype(o_ref.dtype)
            if also_bf16:
                o2_ref[...] = r.astype(BF16)

    in_specs = [a_spec, b_spec]
    args = [a, b]
    if relu2_of is not None:
        in_specs.append(o_spec)
        args.append(relu2_of)
    out_shape = [jax.ShapeDtypeStruct((n // tn, m, tn) if (col_blocks and ta) else (m, n), out_dtype)]
    out_specs = [o_spec]
    if two:
        out_shape.append(jax.ShapeDtypeStruct((m, n), BF16))
        out_specs.append(o_spec)
    grid = (m // tm, n // tn, nk)
    side_in, side_out, side_scratch = _side_specs(side)
    res = pl.pallas_call(
        _carry_side_job(body, len(args), len(out_shape), side, grid), name=name, grid=grid,
        in_specs=in_specs + side_in, out_specs=out_specs + side_out,
        out_shape=out_shape + ([] if side is None else side.out_shape),
        scratch_shapes=[pltpu.VMEM((tm, tn), F32)] + side_scratch,
        compiler_params=_cparams(("parallel", "parallel", "arbitrary") if side is None else ("arbitrary",) * 3),
    )(*args, *([] if side is None else side.inputs))
    main = res[:len(out_shape)]
    main = main if two else main[0]
    return main if side is None else (main, res[len(out_shape):])


def _rms(x, g):
    r = lax.rsqrt(jnp.mean(x * x, axis=-1, keepdims=True) + EPS)
    return x * r * g


def _rms_bwd(x, g, dy):
    r = lax.rsqrt(jnp.mean(x * x, axis=-1, keepdims=True) + EPS)
    xh = x * r
    gdy = dy * g
    dx = r * (gdy - xh * jnp.mean(xh * gdy, axis=-1, keepdims=True))
    return dx, xh * dy


def _norm_fwd(x, g, *, name, resid=None, out_dtype=BF16, next_gain=None):
    s, d = x.shape
    tr = _tile(s, 256)
    row = pl.BlockSpec((tr, d), lambda i: (i, 0))
    gsp = pl.BlockSpec((1, d), lambda i: (0, 0))

    def body(*refs):
        refs = list(refs)
        x_ref, g_ref = refs[:2]
        y = _rms(x_ref[...], g_ref[...])
        pos = 2
        if resid is not None:
            y = refs[pos][...] + y
            pos += 1
        if next_gain is None:
            refs[pos][...] = y.astype(refs[pos].dtype)
        else:
            refs[pos + 1][...] = y.astype(refs[pos + 1].dtype)
            refs[pos + 2][...] = _rms(y, refs[pos][...]).astype(BF16)

    args = [x, g.reshape(1, d)] + ([] if resid is None else [resid]) + ([] if next_gain is None else [next_gain.reshape(1, d)])
    in_specs = [row, gsp] + ([] if resid is None else [row]) + ([] if next_gain is None else [gsp])
    first = jax.ShapeDtypeStruct((s, d), out_dtype)
    if next_gain is None:
        out_specs, out_shape = row, first
    else:
        out_specs, out_shape = [row, row], [first, jax.ShapeDtypeStruct((s, d), BF16)]
    return pl.pallas_call(
        body, name=name, grid=(s // tr,), in_specs=in_specs, out_specs=out_specs, out_shape=out_shape,
        compiler_params=_cparams(("parallel",)),
    )(*args)


def _norm_bwd(x, g, dy, *, name, add=None, out_dtype=F32, then=None):
    s, d = x.shape
    tr = _tile(s, 256)
    row = pl.BlockSpec((tr, d), lambda i: (i, 0))
    gsp = pl.BlockSpec((1, d), lambda i: (0, 0))
    n_in = 3 + (add is not None) + (2 if then is not None else 0)

    def body(*refs):
        ins, outs = refs[:n_in], refs[n_in:]
        x_ref, g_ref, dy_ref = ins[:3]
        dx, gterm = _rms_bwd(x_ref[...], g_ref[...], dy_ref[...].astype(F32))
        if add is not None:
            dx = dx + ins[3][...]
        outs[0][...] = dx.astype(outs[0].dtype)
        terms = [(outs[1], gterm)]
        if then is not None:
            dx2, gterm2 = _rms_bwd(ins[-2][...], ins[-1][...], dx)
            outs[2][...] = dx2.astype(BF16)
            terms.append((outs[3], gterm2))

        @pl.when(pl.program_id(0) == 0)
        def _():
            for dg_ref, _ in terms:
                dg_ref[...] = jnp.zeros_like(dg_ref)

        for dg_ref, term in terms:
            dg_ref[...] += jnp.sum(term, axis=0, keepdims=True)

    args = [x, g.reshape(1, d), dy] + ([] if add is None else [add]) + ([] if then is None else [then[0], then[1].reshape(1, d)])
    in_specs = [row, gsp, row] + ([] if add is None else [row]) + ([] if then is None else [row, gsp])
    out_specs = [row, gsp] + ([] if then is None else [row, gsp])
    out_shape = [jax.ShapeDtypeStruct((s, d), out_dtype), jax.ShapeDtypeStruct((1, d), F32)]
    if then is not None:
        out_shape += [jax.ShapeDtypeStruct((s, d), BF16), jax.ShapeDtypeStruct((1, d), F32)]
    return pl.pallas_call(
        body, name=name, grid=(s // tr,), in_specs=in_specs, out_specs=out_specs, out_shape=out_shape,
        compiler_params=_cparams(("arbitrary",)),
    )(*args)


def _loss_head(f, g, resid, target):
    s, d = f.shape
    tr = _tile(s, 256)
    row = pl.BlockSpec((tr, d), lambda i: (i, 0))
    gsp = pl.BlockSpec((1, d), lambda i: (0, 0))
    lsp = pl.BlockSpec((1, LANES), lambda i: (0, 0))

    def body(f_ref, g_ref, r_ref, t_ref, l_ref, dy_ref, df_ref, dg_ref):
        fv, gv = f_ref[...], g_ref[...]
        e = (r_ref[...] + _rms(fv, gv)) - t_ref[...]
        dy = e * (1.0 / d)
        dy_ref[...] = dy
        df, gterm = _rms_bwd(fv, gv, dy)
        df_ref[...] = df.astype(BF16)

        @pl.when(pl.program_id(0) == 0)
        def _():
            l_ref[...] = jnp.zeros_like(l_ref)
            dg_ref[...] = jnp.zeros_like(dg_ref)

        part = 0.5 * jnp.sum(jnp.mean(e * e, axis=-1, keepdims=True), axis=0, keepdims=True)
        l_ref[...] += jnp.broadcast_to(part, (1, LANES))
        dg_ref[...] += jnp.sum(gterm, axis=0, keepdims=True)

    return pl.pallas_call(
        body, name="loss_head", grid=(s // tr,), in_specs=[row, gsp, row, row], out_specs=[lsp, row, row, gsp],
        out_shape=[jax.ShapeDtypeStruct((1, LANES), F32), jax.ShapeDtypeStruct((s, d), F32), jax.ShapeDtypeStruct((s, d), BF16),
                   jax.ShapeDtypeStruct((1, d), F32)],
        compiler_params=_cparams(("arbitrary",)),
    )(f, g.reshape(1, d), resid, target)


def _rope_tables(pos_col):
    s = pos_col.shape[0]
    tr = _tile(s, 512)
    f_mla = ROPE_BASE ** (-jnp.arange(ROPE_DIM // 2, dtype=F32) / (ROPE_DIM // 2))
    f_ret = ROPE_BASE ** (-jnp.arange(HEAD // 2, dtype=F32) / (HEAD // 2))
    fm = jnp.concatenate([jnp.zeros((64,), F32), f_mla, f_mla, jnp.zeros((32,), F32)]).reshape(1, LANES)
    fr = jnp.tile(jnp.concatenate([f_ret, f_ret]), 2).reshape(1, LANES)

    def body(p_ref, fm_ref, fr_ref, cm_ref, sm_ref, cr_ref, sr_ref):
        p = p_ref[...].astype(F32)
        am = p * fm_ref[...]
        ar = p * fr_ref[...]
        cm_ref[...] = jnp.cos(am)
        sm_ref[...] = jnp.sin(am)
        cr_ref[...] = jnp.tile(jnp.cos(ar), (1, 2))
        sr_ref[...] = jnp.tile(jnp.sin(ar), (1, 2))

    return pl.pallas_call(
        body, name="rope_tables", grid=(s // tr,),
        in_specs=[pl.BlockSpec((tr, 1), lambda i: (i, 0)), pl.BlockSpec((1, LANES), lambda i: (0, 0)),
                  pl.BlockSpec((1, LANES), lambda i: (0, 0))],
        out_specs=[pl.BlockSpec((tr, LANES), lambda i: (i, 0))] * 2 + [pl.BlockSpec((tr, 2 * LANES), lambda i: (i, 0))] * 2,
        out_shape=[jax.ShapeDtypeStruct((s, LANES), F32)] * 2 + [jax.ShapeDtypeStruct((s, 2 * LANES), F32)] * 2,
        compiler_params=_cparams(("parallel",)),
    )(pos_col, fm, fr)


def _lane(shape):
    return lax.broadcasted_iota(jnp.int32, shape, len(shape) - 1)


def _rot_mla(z):
    l = _lane(z.shape) % LANES
    n = z.shape[-1]
    return jnp.where(l < 80, -pltpu.roll(z, n - 16, 1), pltpu.roll(z, 16, 1))


def _rot_mla_t(y):
    l = _lane(y.shape) % LANES
    n = y.shape[-1]
    return jnp.where((l >= 64) & (l < 80), pltpu.roll(y, n - 16, 1),
                     jnp.where((l >= 80) & (l < 96), -pltpu.roll(y, 16, 1), 0.0))


def _rot_ret(z):
    l = _lane(z.shape) % HEAD
    n = z.shape[-1]
    return jnp.where(l < 32, -pltpu.roll(z, n - 32, 1), pltpu.roll(z, 32, 1))


def _rot_ret_t(y):
    l = _lane(y.shape) % HEAD
    n = y.shape[-1]
    return jnp.where(l < 32, pltpu.roll(y, n - 32, 1), -pltpu.roll(y, 32, 1))


def _log_sigmoid(x):
    return jnp.minimum(x, 0.0) - jnp.log1p(jnp.exp(-jnp.abs(x)))


def _fox_cum(proj, bias_row):
    s = proj.shape[0]
    fb = _tile(s, GATE_ROWS)
    nb = s // fb

    def body(x_ref, b_ref, cc_ref, cr_ref, carry_ref):
        @pl.when(pl.program_id(0) == 0)
        def _():
            carry_ref[...] = jnp.zeros_like(carry_ref)

        ls = _log_sigmoid(x_ref[...] + b_ref[...])
        r = lax.broadcasted_iota(jnp.int32, (fb, fb), 0)
        c = lax.broadcasted_iota(jnp.int32, (fb, fb), 1)
        tri = (c <= r).astype(F32)
        cum = _dot_exact(tri, ls) + carry_ref[...]
        carry_ref[...] = cum[fb - 1:fb, :]
        cc_ref[...] = cum
        cr_ref[...] = cum.T[0:8, :]

    return pl.pallas_call(
        body, name="fox_cum", grid=(nb,),
        in_specs=[pl.BlockSpec((fb, LANES), lambda i: (i, OFF_MISC)), pl.BlockSpec((1, LANES), lambda i: (0, 0))],
        out_specs=[pl.BlockSpec((fb, LANES), lambda i: (i, 0)), pl.BlockSpec((8, fb), lambda i: (0, i))],
        out_shape=[jax.ShapeDtypeStruct((s, LANES), F32), jax.ShapeDtypeStruct((8, s), F32)],
        scratch_shapes=[pltpu.VMEM((1, LANES), F32)],
        compiler_params=_cparams(("arbitrary",)),
    )(proj, bias_row)


def _fox_gate_bwd(dck, drs, proj, bias_row, dkr):
    s = proj.shape[0]
    fb = _tile(s, GATE_ROWS)
    nb = s // fb

    def body(d_ref, r_ref, x_ref, b_ref, k_ref, o_ref, db_ref, carry_ref):
        @pl.when(pl.program_id(0) == 0)
        def _():
            carry_ref[...] = jnp.zeros_like(carry_ref)
            db_ref[...] = jnp.zeros_like(db_ref)

        rows = jnp.concatenate([d_ref[0], d_ref[1], jnp.zeros((LANES - 16, fb), F32)], axis=0)
        t = rows.T
        l = _lane((fb, LANES))
        r0, r1 = r_ref[0], r_ref[1]
        rsum = jnp.where(l == 0, r0[:, 0:1], jnp.where(l == 1, r0[:, HEAD:HEAD + 1],
                         jnp.where(l == 2, r1[:, 0:1], jnp.where(l == 3, r1[:, HEAD:HEAD + 1], 0.0))))
        dcum = rsum - jnp.where(l < 2, t, pltpu.roll(t, LANES - 6, 1))
        r = lax.broadcasted_iota(jnp.int32, (fb, fb), 0)
        c = lax.broadcasted_iota(jnp.int32, (fb, fb), 1)
        triu = (c >= r).astype(F32)
        rc = _dot_exact(triu, dcum) + carry_ref[...]
        carry_ref[...] = rc[0:1, :]
        f = x_ref[...] + b_ref[...]
        sig_neg = 1.0 / (1.0 + jnp.exp(f))
        df = jnp.where(l < N_HEADS, rc * sig_neg, 0.0)
        db_ref[...] += jnp.sum(df, axis=0, keepdims=True)
        o_ref[...] = (df + k_ref[...]).astype(o_ref.dtype)

    rev = lambda i: nb - 1 - i
    return pl.pallas_call(
        body, name="fox_gate_bwd", grid=(nb,),
        in_specs=[pl.BlockSpec((2, 8, fb), lambda i: (0, 0, rev(i))), pl.BlockSpec((2, fb, LANES), lambda i: (0, rev(i), 0)),
                  pl.BlockSpec((fb, LANES), lambda i: (rev(i), OFF_MISC)),
                  pl.BlockSpec((1, LANES), lambda i: (0, 0)), pl.BlockSpec((fb, LANES), lambda i: (rev(i), 0))],
        out_specs=[pl.BlockSpec((fb, LANES), lambda i: (rev(i), 0)), pl.BlockSpec((1, LANES), lambda i: (0, 0))],
        out_shape=[jax.ShapeDtypeStruct((s, LANES), BF16), jax.ShapeDtypeStruct((1, LANES), F32)],
        scratch_shapes=[pltpu.VMEM((1, LANES), F32)],
        compiler_params=_cparams(("arbitrary",)),
    )(dck, drs, proj, bias_row, dkr)


def _mla_prep(proj, cos_m, sin_m, g_q, g_kv, wq, wk, wv):
    s = proj.shape[0]
    tr = _tile(s, 512)

    def body(cq_ref, ckv_ref, misc_ref, cos_ref, sin_ref, gq_ref, gkv_ref, wq_ref, wk_ref, wv_ref,
             q_ref, k_ref, v_ref, cqn_ref, ckvn_ref):
        cos4 = jnp.tile(cos_ref[...], (1, 4))
        sin4 = jnp.tile(sin_ref[...], (1, 4))
        cqn = _rms(cq_ref[...], gq_ref[...]).astype(BF16)
        ckvn = _rms(ckv_ref[...], gkv_ref[...]).astype(BF16)
        cqn_ref[...] = cqn
        ckvn_ref[...] = ckvn
        zq = _dot(cqn, wq_ref[...])
        q_ref[...] = (zq * cos4 + _rot_mla(zq) * sin4).astype(BF16)
        l = _lane((tr, LANES))
        kr = jnp.where((l >= KR_LANE) & (l < KR_LANE + ROPE_DIM), misc_ref[...], 0.0)
        zk = _dot(ckvn, wk_ref[...]) + jnp.tile(kr, (1, 4))
        k_ref[...] = (zk * cos4 + _rot_mla(zk) * sin4).astype(BF16)
        v_ref[...] = _dot(ckvn, wv_ref[...]).astype(BF16)

    full = lambda a: pl.BlockSpec(a.shape, lambda i: (0, 0))
    rowb = lambda w: pl.BlockSpec((tr, w), lambda i: (i, 0))
    gq2, gkv2 = g_q.reshape(1, Q_RANK), g_kv.reshape(1, KV_RANK)
    return pl.pallas_call(
        body, name="mla_prep", grid=(s // tr,),
        in_specs=[pl.BlockSpec((tr, 256), lambda i: (i, OFF_CQ // 2)), pl.BlockSpec((tr, LANES), lambda i: (i, OFF_CKV)),
                  pl.BlockSpec((tr, LANES), lambda i: (i, OFF_MISC)), rowb(LANES), rowb(LANES),
                  full(gq2), full(gkv2), full(wq), full(wk), full(wv)],
        out_specs=[rowb(512), rowb(512), rowb(512), rowb(256), rowb(128)],
        out_shape=[jax.ShapeDtypeStruct((s, 512), BF16), jax.ShapeDtypeStruct((s, 512), BF16), jax.ShapeDtypeStruct((s, 512), BF16),
                   jax.ShapeDtypeStruct((s, 256), BF16), jax.ShapeDtypeStruct((s, 128), BF16)],
        compiler_params=_cparams(("parallel",)),
    )(proj, proj, proj, cos_m, sin_m, gq2, gkv2, wq, wk, wv)


def _mla_prep_bwd(dq, dk, dv, proj, cqn, ckvn, cos_m, sin_m, g_q, g_kv, wq, wk, wv):
    s = proj.shape[0]
    tr = _tile(s, 512)

    def body(dq_ref, dk_ref, dv_ref, cq_ref, ckv_ref, cqn_ref, ckvn_ref, cos_ref, sin_ref, gq_ref, gkv_ref,
             wq_ref, wk_ref, wv_ref, dcq_ref, dckv_ref, dkr_ref, dwq_ref, dwk_ref, dwv_ref, dgq_ref, dgkv_ref):
        @pl.when(pl.program_id(0) == 0)
        def _():
            for r in (dwq_ref, dwk_ref, dwv_ref, dgq_ref, dgkv_ref):
                r[...] = jnp.zeros_like(r)

        cos4 = jnp.tile(cos_ref[...], (1, 4))
        sin4 = jnp.tile(sin_ref[...], (1, 4))
        dqv = dq_ref[...]
        dzq = dqv * cos4 + _rot_mla_t(dqv * sin4)
        dkv_ = dk_ref[...]
        dzk = dkv_ * cos4 + _rot_mla_t(dkv_ * sin4)
        l = _lane((tr, LANES))
        in_rope = (l >= KR_LANE) & (l < KR_LANE + ROPE_DIM)
        dkr = dzk[:, 0:128] + dzk[:, 128:256] + dzk[:, 256:384] + dzk[:, 384:512]
        dkr_ref[...] = jnp.where(in_rope, dkr, 0.0)
        dzq_b = dzq.astype(BF16)
        dzk_b = dzk.astype(BF16)
        dv_b = dv_ref[...].astype(BF16)
        dcqn = _dot_nt(dzq_b, wq_ref[...])
        dckvn = _dot_nt(dzk_b, wk_ref[...]) + _dot_nt(dv_b, wv_ref[...])
        dwq_ref[...] += _dot_tn(cqn_ref[...], dzq_b)
        dwk_ref[...] += _dot_tn(ckvn_ref[...], dzk_b)
        dwv_ref[...] += _dot_tn(ckvn_ref[...], dv_b)
        dcq, gq_term = _rms_bwd(cq_ref[...], gq_ref[...], dcqn)
        dckv, gkv_term = _rms_bwd(ckv_ref[...], gkv_ref[...], dckvn)
        dcq_ref[...] = dcq.astype(BF16)
        dckv_ref[...] = dckv.astype(BF16)
        dgq_ref[...] += jnp.sum(gq_term, axis=0, keepdims=True)
        dgkv_ref[...] += jnp.sum(gkv_term, axis=0, keepdims=True)

    full = lambda shp: pl.BlockSpec(shp, lambda i: (0, 0))
    rowb = lambda w: pl.BlockSpec((tr, w), lambda i: (i, 0))
    gq2, gkv2 = g_q.reshape(1, Q_RANK), g_kv.reshape(1, KV_RANK)
    return pl.pallas_call(
        body, name="mla_prep_bwd", grid=(s // tr,),
        in_specs=[rowb(512), rowb(512), rowb(512),
                  pl.BlockSpec((tr, 256), lambda i: (i, OFF_CQ // 2)), pl.BlockSpec((tr, LANES), lambda i: (i, OFF_CKV)),
                  rowb(256), rowb(128), rowb(LANES), rowb(LANES), full((1, Q_RANK)), full((1, KV_RANK)),
                  full(wq.shape), full(wk.shape), full(wv.shape)],
        out_specs=[rowb(256), rowb(128), rowb(128), full(wq.shape), full(wk.shape), full(wv.shape),
                   full((1, Q_RANK)), full((1, KV_RANK))],
        out_shape=[jax.ShapeDtypeStruct((s, 256), BF16), jax.ShapeDtypeStruct((s, 128), BF16), jax.ShapeDtypeStruct((s, 128), F32),
                   jax.ShapeDtypeStruct(wq.shape, F32), jax.ShapeDtypeStruct(wk.shape, F32), jax.ShapeDtypeStruct(wv.shape, F32),
                   jax.ShapeDtypeStruct((1, Q_RANK), F32), jax.ShapeDtypeStruct((1, KV_RANK), F32)],
        compiler_params=_cparams(("arbitrary",)),
    )(dq, dk, dv, proj, proj, cqn, ckvn, cos_m, sin_m, gq2, gkv2, wq, wk, wv)


def _ret_prep(proj, cos_r, sin_r):
    s = proj.shape[0]
    tr = _tile(s, 512)

    def body(q_ref, k_ref, cos_ref, sin_ref, qo_ref, ko_ref):
        cos, sin = cos_ref[...], sin_ref[...]
        q, k = q_ref[...], k_ref[...]
        qo_ref[...] = (q * cos + _rot_ret(q) * sin).astype(BF16)
        ko_ref[...] = ((k * cos + _rot_ret(k) * sin) * (HEAD ** -0.5)).astype(BF16)

    rowb = pl.BlockSpec((tr, 256), lambda i: (i, 0))
    return pl.pallas_call(
        body, name="ret_prep", grid=(s // tr,),
        in_specs=[pl.BlockSpec((tr, 256), lambda i: (i, OFF_RQ // 2)), pl.BlockSpec((tr, 256), lambda i: (i, OFF_RK // 2)), rowb, rowb],
        out_specs=[rowb, rowb], out_shape=[jax.ShapeDtypeStruct((s, 256), BF16)] * 2,
        compiler_params=_cparams(("parallel",)),
    )(proj, proj, cos_r, sin_r)


def _ret_prep_bwd(dq, dk, cos_r, sin_r):
    s = dq.shape[0]
    tr = _tile(s, 512)

    def body(dq_ref, dk_ref, cos_ref, sin_ref, qo_ref, ko_ref):
        cos, sin = cos_ref[...], sin_ref[...]
        q, k = dq_ref[...], dk_ref[...] * (HEAD ** -0.5)
        qo_ref[...] = (q * cos + _rot_ret_t(q * sin)).astype(BF16)
        ko_ref[...] = (k * cos + _rot_ret_t(k * sin)).astype(BF16)

    rowb = pl.BlockSpec((tr, 256), lambda i: (i, 0))
    return pl.pallas_call(
        body, name="ret_prep_bwd", grid=(s // tr,), in_specs=[rowb] * 4, out_specs=[rowb, rowb],
        out_shape=[jax.ShapeDtypeStruct((s, 256), BF16)] * 2, compiler_params=_cparams(("parallel",)),
    )(dq, dk, cos_r, sin_r)


_LOG_GAMMA = [float(np.log1p(-np.float32(2.0) ** np.float32(-5.0 - h))) for h in range(N_HEADS)]
_MLA_SCALE = float((HEAD + ROPE_DIM) ** -0.5)
_QK_SCALE = float(HEAD ** -0.5)
KEY_BLOCKS = 4
QB = 256


def _split2(x):
    h = x.astype(BF16)
    return h, (x - h.astype(F32)).astype(BF16)


def _dot2(x, u):
    h, lo = _split2(x)
    return _dot(h, u) + _dot(lo, u)


def _head_pick(block, head, axis):
    idx = lax.broadcasted_iota(jnp.int32, block.shape, axis)
    return jnp.sum(jnp.where(idx == head, block, 0.0), axis=axis, keepdims=True)


def _log_gamma_of(head):
    lg = jnp.float32(_LOG_GAMMA[3])
    for h in (2, 1, 0):
        lg = jnp.where(head == h, jnp.float32(_LOG_GAMMA[h]), lg)
    return lg


def _mixer_specs(mode, s, q_off, k_off, v_off):
    nhb = 2
    bw = 2 * LANES if mode == "mla" else LANES
    nsub = KEY_BLOCKS if (s // TQ) % KEY_BLOCKS == 0 else 1
    q_spec = pl.BlockSpec((QB, bw), lambda p, i: (i, q_off + p))
    k_spec = pl.BlockSpec((s, bw), lambda p, i: (0, k_off + p))
    v_spec = pl.BlockSpec((s, bw), lambda p, i: (0, v_off + p))
    return nhb, N_HEADS // nhb, nsub, q_spec, k_spec, v_spec


def _mixer_geometry(mode, i, nsub):
    w = TQ * nsub
    row = lax.broadcasted_iota(jnp.int32, (QB, w), 0)
    col = lax.broadcasted_iota(jnp.int32, (QB, w), 1)
    nfull = (i * QB) // w
    dist = col - row
    if mode in ("fox", "sb"):
        rel = dist
    else:
        rel = col - (row | (CHUNK - 1))

    def visible(c):
        off = c * w - i * QB
        return (rel + off) < 0 if mode == "sb" else (rel + off) <= 0

    return nfull, dist, visible


class _SideJob:
    def __init__(self, inputs, out_shape, n_sems, sends, recvs):
        self.inputs, self.out_shape, self.n_sems, self.sends, self.recvs = list(inputs), list(out_shape), n_sems, sends, recvs


def _carry_side_job(body, n_in, n_out, side, n_steps):
    if side is None:
        return body
    si, so = len(side.inputs), len(side.out_shape)

    def at(corner):
        ok = pl.program_id(0) == corner[0]
        for d in range(1, len(n_steps)):
            ok = ok & (pl.program_id(d) == corner[d])
        return ok

    def wrapped(*refs):
        ins, s_ins = refs[:n_in], refs[n_in:n_in + si]
        outs, s_outs = refs[n_in + si:n_in + si + n_out], refs[n_in + si + n_out:n_in + si + n_out + so]
        scratch, send, recv = refs[n_in + si + n_out + so:-2], refs[-2], refs[-1]

        @pl.when(at([0] * len(n_steps)))
        def _():
            for cp in side.sends(s_ins, s_outs, send, recv):
                cp.start()

        body(*ins, *outs, *scratch)

        @pl.when(at([n - 1 for n in n_steps]))
        def _():
            for cp in side.recvs(s_ins, s_outs, send, recv):
                cp.wait_recv()
            for cp in side.sends(s_ins, s_outs, send, recv):
                cp.wait_send()

    return wrapped


def _side_specs(side):
    if side is None:
        return [], [], []
    hbm = pl.BlockSpec(memory_space=pl.ANY)
    return ([hbm] * len(side.inputs), [hbm] * len(side.out_shape),
            [pltpu.SemaphoreType.DMA((side.n_sems,)), pltpu.SemaphoreType.DMA((side.n_sems,))])


def _mixer_fwd(mode, qa, q_off, ka, k_off, va, v_off, *, cum_col=None, cum_row=None, side=None):
    s = qa.shape[0]
    nq = s // QB
    nhb, nblk, nsub, q_spec, k_spec, v_spec = _mixer_specs(mode, s, q_off, k_off, v_off)
    w = TQ * nsub
    softmax = mode in ("fox", "mla")

    def body(*refs):
        refs = list(refs)
        q_ref, k_ref, v_ref = refs[:3]
        refs = refs[3:]
        if mode == "fox":
            cc_ref, cr_ref = refs[:2]
            refs = refs[2:]
        o_ref = refs[0]
        st_ref = refs[1]
        p = pl.program_id(0)
        i = pl.program_id(1)
        nfull, dist, visible = _mixer_geometry(mode, i, nsub)
        lane = _lane((1, LANES))
        heads = [nhb * p + hh for hh in range(nhb)]
        wide = mode == "mla"
        q_scale = _QK_SCALE if mode in ("fox", "sb") else 1.0
        cols = [slice(hh * LANES, (hh + 1) * LANES) if wide else slice(None) for hh in range(nhb)]
        if wide:
            qs = [q_ref[:, cols[hh]] for hh in range(nhb)]
        else:
            qf = q_ref[...].astype(F32) * q_scale
            qs = [jnp.where((lane // HEAD) == hh, qf, 0.0).astype(BF16) for hh in range(nhb)]
        if mode == "fox":
            cqs = [_head_pick(cc_ref[...], h, 1) for h in heads]
        if mode == "sb":
            r1 = lax.broadcasted_iota(jnp.int32, (TQ, TQ), 0)
            c1 = lax.broadcasted_iota(jnp.int32, (TQ, TQ), 1)
            u_after = (r1 > c1).astype(BF16)

        def chunk(c):
            return pl.ds(pl.multiple_of(c * w, w), w)

        def scores(c):
            js = chunk(c)
            return tuple(_dot_nt(qs[hh], k_ref[js, cols[hh]]) for hh in range(nhb))

        def head_step(hh, c, js, sc, vj, carry, last):
            if softmax:
                m, l, acc = carry
                if mode == "fox":
                    ck = _head_pick(cr_ref[:, js], heads[hh], 0)
                    sc = sc + (cqs[hh] - ck)
                else:
                    sc = sc * _MLA_SCALE
                if last:
                    sc = jnp.where(visible(c), sc, NEG)
                m_new = jnp.maximum(m, jnp.max(sc, axis=-1, keepdims=True))
                alpha = jnp.exp(m - m_new)
                pr = jnp.exp(sc - m_new)
                l = alpha * l + jnp.sum(pr, axis=-1, keepdims=True)
                acc = alpha * acc + _dot(pr.astype(BF16), vj)
                return m_new, l, acc
            run, acc = carry
            z = sc
            log_beta = jnp.minimum(z, 0.0) - jnp.log(1.0 + jnp.exp(-jnp.abs(z)))
            log_stay = log_beta - z
            if last:
                vis = visible(c)
                log_stay = jnp.where(vis, log_stay, 0.0)
            parts = [None] * nsub
            for b in reversed(range(nsub)):
                ls_b = log_stay[:, b * TQ:(b + 1) * TQ]
                parts[b] = _dot2(ls_b, u_after) + run
                run = run + jnp.sum(ls_b, axis=-1, keepdims=True)
            later = parts[0] if nsub == 1 else jnp.concatenate(parts, axis=1)
            wgt = jnp.exp(log_beta + later)
            if last:
                wgt = jnp.where(vis, wgt, 0.0)
            return run, acc + _dot(wgt.astype(BF16), vj)

        def step(c, c_next, state, last):
            scs, carries = state
            nxt = scores(c_next) if c_next is not None else None
            js = chunk(c)
            return nxt, tuple(head_step(hh, c, js, scs[hh], v_ref[js, cols[hh]], carries[hh], last) for hh in range(nhb))

        zero_acc = jnp.zeros((QB, LANES), F32)
        zero1 = jnp.zeros((QB, 1), F32)
        if softmax:
            init = tuple((jnp.full((QB, 1), NEG, F32), zero1, zero_acc) for _ in range(nhb))
        else:
            init = tuple((zero1, zero_acc) for _ in range(nhb))
        if mode == "sb":
            state = step(nfull, jnp.maximum(nfull - 1, 0), (scores(nfull), init), True)
            _, carries = lax.fori_loop(0, nfull, lambda t, st: step(nfull - 1 - t, jnp.maximum(nfull - 2 - t, 0), st, False), state)
        else:
            state = lax.fori_loop(0, nfull, lambda c, st: step(c, c + 1, st, False), (scores(0), init))
            _, carries = step(nfull, None, state, True)
        if softmax:
            outs = [acc / l for (m, l, acc) in carries]
            stats = [m + jnp.log(l) for (m, l, acc) in carries]
        else:
            outs, stats = [acc for (run, acc) in carries], [run for (run, acc) in carries]
        hm0 = (lane // HEAD) == 0
        pick = lambda a: jnp.where(hm0, a[0], a[1])
        if wide:
            for hh in range(nhb):
                o_ref[:, cols[hh]] = outs[hh]
        else:
            o_ref[...] = pick(outs)
        st_ref[0] = pick(stats)

    in_specs = [q_spec, k_spec, v_spec]
    args = [qa, ka, va]
    if mode == "fox":
        in_specs += [pl.BlockSpec((QB, LANES), lambda p, i: (i, 0)), pl.BlockSpec((8, s), lambda p, i: (0, 0))]
        args += [cum_col, cum_row]
    bw = 2 * LANES if mode == "mla" else LANES
    out_specs = [pl.BlockSpec((QB, bw), lambda p, i: (i, p))]
    out_shape = [jax.ShapeDtypeStruct((s, nblk * bw), F32)]
    out_specs.append(pl.BlockSpec((1, QB, LANES), lambda p, i: (p, i, 0)))
    out_shape.append(jax.ShapeDtypeStruct((nblk, s, LANES), F32))
    side_in, side_out, side_scratch = _side_specs(side)
    res = pl.pallas_call(
        _carry_side_job(body, len(args), len(out_shape), side, (nblk, nq)), name=mode + "_fwd", grid=(nblk, nq),
        in_specs=in_specs + side_in, out_specs=out_specs + side_out,
        out_shape=out_shape + ([] if side is None else side.out_shape), scratch_shapes=side_scratch,
        compiler_params=_cparams(("parallel", "parallel") if side is None else ("arbitrary", "arbitrary")),
    )(*args, *([] if side is None else side.inputs))
    return (res[0], res[1]) if side is None else (res[0], res[1], res[2:])


def _mixer_bwd(mode, qa, q_off, ka, k_off, va, v_off, o, do, *, stat=None, cum_col=None, cum_row=None, side=None):
    s = qa.shape[0]
    nq = s // QB
    nhb, nblk, nsub, q_spec, k_spec, v_spec = _mixer_specs(mode, s, q_off, k_off, v_off)
    w = TQ * nsub
    softmax = mode in ("fox", "mla")

    def body(*refs):
        refs = list(refs)
        q_ref, k_ref, v_ref, o_ref, do_ref = refs[:5]
        refs = refs[5:]
        st_ref = refs[0]
        refs = refs[1:]
        if mode == "fox":
            cc_ref, cr_ref = refs[:2]
            refs = refs[2:]
        dq_ref, dk_ref, dv_ref = refs[:3]
        dck_ref, drs_ref = refs[3:5] if mode == "fox" else (None, None)
        p = pl.program_id(0)
        i = pl.program_id(1)

        @pl.when(i == 0)
        def _():
            dk_ref[...] = jnp.zeros_like(dk_ref)
            dv_ref[...] = jnp.zeros_like(dv_ref)
            if mode == "fox":
                dck_ref[...] = jnp.zeros_like(dck_ref)

        nfull, dist, visible = _mixer_geometry(mode, i, nsub)
        lane = _lane((1, LANES))
        heads = [nhb * p + hh for hh in range(nhb)]
        dov = do_ref[...]
        wide = mode == "mla"
        q_scale = _QK_SCALE if mode in ("fox", "sb") else 1.0
        cols = [slice(hh * LANES, (hh + 1) * LANES) if wide else slice(None) for hh in range(nhb)]
        if wide:
            prod = dov * o_ref[...]
            qs = [q_ref[:, cols[hh]] for hh in range(nhb)]
            dos = [dov[:, cols[hh]].astype(BF16) for hh in range(nhb)]
            deltas = [jnp.sum(prod[:, cols[hh]], axis=-1, keepdims=True) for hh in range(nhb)]
        else:
            qf = q_ref[...].astype(F32) * q_scale
            prod = dov * o_ref[...]
            hms = [(lane // HEAD) == hh for hh in range(nhb)]
            qs = [jnp.where(hm, qf, 0.0).astype(BF16) for hm in hms]
            dos = [jnp.where(hm, dov, 0.0).astype(BF16) for hm in hms]
            deltas = [jnp.sum(jnp.where(hm, prod, 0.0), axis=-1, keepdims=True) for hm in hms]
        st = st_ref[0]
        stats = [st[:, hh * HEAD:hh * HEAD + 1] for hh in range(nhb)]
        if mode == "fox":
            cqs = [_head_pick(cc_ref[...], h, 1) for h in heads]
        if mode == "sb":
            r1 = lax.broadcasted_iota(jnp.int32, (TQ, TQ), 0)
            c1 = lax.broadcasted_iota(jnp.int32, (TQ, TQ), 1)
            u_upto = (r1 <= c1).astype(BF16)
            u_before = (r1 < c1).astype(BF16)

        def chunk(c):
            return pl.ds(pl.multiple_of(c * w, w), w)

        def scores(c):
            js = chunk(c)
            if mode == "sb":
                return tuple((_dot_nt(qs[hh], k_ref[js, cols[hh]]), None) for hh in range(nhb))
            return tuple((_dot_nt(qs[hh], k_ref[js, cols[hh]]), _dot_nt(dos[hh], v_ref[js, cols[hh]])) for hh in range(nhb))

        def emit(hh, js, ds_b, pr_b, dq):
            dk_ref[js, cols[hh]] += _dot_tn(ds_b, qs[hh])
            dv_ref[js, cols[hh]] += _dot_tn(pr_b, dos[hh])
            return dq + _dot(ds_b, k_ref[js, cols[hh]])

        def head_step(hh, c, js, sc_dp, carry, last):
            sc, dp = sc_dp
            if dp is None:
                dp = _dot_nt(dos[hh], v_ref[js, cols[hh]])
            if softmax:
                dq, rsum = carry
                if mode == "fox":
                    ck = _head_pick(cr_ref[:, js], heads[hh], 0)
                    sc = sc + (cqs[hh] - ck)
                else:
                    sc = sc * _MLA_SCALE
                if last:
                    sc = jnp.where(visible(c), sc, NEG)
                pr = jnp.exp(sc - stats[hh])
                ds = pr * (dp - deltas[hh])
                if mode == "fox":
                    dck_ref[0, hh:hh + 1, js] += jnp.sum(ds, axis=0, keepdims=True)
                    rsum = rsum + jnp.sum(ds, axis=-1, keepdims=True)
                if mode == "mla":
                    ds = ds * _MLA_SCALE
                return emit(hh, js, ds.astype(BF16), pr.astype(BF16), dq), rsum
            seen, gsum, dq = carry
            z = sc
            log_beta = jnp.minimum(z, 0.0) - jnp.log(1.0 + jnp.exp(-jnp.abs(z)))
            log_stay = log_beta - z
            if last:
                vis = visible(c)
                log_stay = jnp.where(vis, log_stay, 0.0)
            parts = []
            for b in range(nsub):
                ls_b = log_stay[:, b * TQ:(b + 1) * TQ]
                parts.append((stats[hh] - seen) - _dot2(ls_b, u_upto))
                seen = seen + jnp.sum(ls_b, axis=-1, keepdims=True)
            later = parts[0] if nsub == 1 else jnp.concatenate(parts, axis=1)
            wgt = jnp.exp(log_beta + later)
            if last:
                wgt = jnp.where(vis, wgt, 0.0)
            g = dp * wgt
            parts = []
            for b in range(nsub):
                g_b = g[:, b * TQ:(b + 1) * TQ]
                parts.append(gsum + _dot2(g_b, u_before))
                gsum = gsum + jnp.sum(g_b, axis=-1, keepdims=True)
            before = parts[0] if nsub == 1 else jnp.concatenate(parts, axis=1)
            beta = jnp.exp(log_beta)
            dz = g * (1.0 - beta) - beta * before
            if last:
                dz = jnp.where(vis, dz, 0.0)
            return seen, gsum, emit(hh, js, dz.astype(BF16), wgt.astype(BF16), dq)

        def step(c, c_next, state, last):
            scs, carries = state
            nxt = scores(c_next) if c_next is not None else None
            js = chunk(c)
            return nxt, tuple(head_step(hh, c, js, scs[hh], carries[hh], last) for hh in range(nhb))

        zero_acc = jnp.zeros((QB, LANES), F32)
        zero1 = jnp.zeros((QB, 1), F32)
        if softmax:
            init = tuple((zero_acc, zero1) for _ in range(nhb))
        else:
            init = tuple((zero1, zero1, zero_acc) for _ in range(nhb))
        state = lax.fori_loop(0, nfull, lambda c, st: step(c, c + 1, st, False), (scores(0), init))
        _, carries = step(nfull, None, state, True)
        if softmax:
            dqs = [dq for (dq, rsum) in carries]
        else:
            dqs = [dq for (seen, gsum, dq) in carries]
        hm0 = (lane // HEAD) == 0
        if wide:
            for hh in range(nhb):
                dq_ref[:, cols[hh]] = dqs[hh]
        else:
            dq_ref[...] = jnp.where(hm0, dqs[0], dqs[1]) * q_scale
        if mode == "fox":
            drs_ref[0] = jnp.where(hm0, carries[0][1], carries[1][1])

    bw = 2 * LANES if mode == "mla" else LANES
    pair_blk = pl.BlockSpec((QB, bw), lambda p, i: (i, p))
    full_blk = pl.BlockSpec((s, bw), lambda p, i: (0, p))
    stat_blk = pl.BlockSpec((1, QB, LANES), lambda p, i: (p, i, 0))
    in_specs = [q_spec, k_spec, v_spec, pair_blk, pair_blk]
    args = [qa, ka, va, o, do]
    in_specs.append(stat_blk)
    args.append(stat)
    if mode == "fox":
        in_specs += [pl.BlockSpec((QB, LANES), lambda p, i: (i, 0)), pl.BlockSpec((8, s), lambda p, i: (0, 0))]
        args += [cum_col, cum_row]
    out_specs = [pair_blk, full_blk, full_blk]
    out_shape = [jax.ShapeDtypeStruct((s, nblk * bw), F32)] * 3
    if mode == "fox":
        out_specs += [pl.BlockSpec((1, 8, s), lambda p, i: (p, 0, 0)), stat_blk]
        out_shape += [jax.ShapeDtypeStruct((2, 8, s), F32), jax.ShapeDtypeStruct((2, s, LANES), F32)]
    side_in, side_out, side_scratch = _side_specs(side)
    res = pl.pallas_call(
        _carry_side_job(body, len(args), len(out_shape), side, (nblk, nq)), name=mode + "_bwd", grid=(nblk, nq),
        in_specs=in_specs + side_in, out_specs=out_specs + side_out,
        out_shape=out_shape + ([] if side is None else side.out_shape), scratch_shapes=side_scratch,
        compiler_params=_cparams(("parallel", "arbitrary") if side is None else ("arbitrary", "arbitrary")),
    )(*args, *([] if side is None else side.inputs))
    return res if side is None else (*res[:len(out_shape)], res[len(out_shape):])


def _ret_geometry(p):
    lane = _lane((1, LANES))
    lg_lane = jnp.where(lane < HEAD, _log_gamma_of(2 * p), _log_gamma_of(2 * p + 1))
    a = lax.broadcasted_iota(jnp.int32, (TQ, 1), 0).astype(F32)
    row = lax.broadcasted_iota(jnp.int32, (TQ, TQ), 0)
    col = lax.broadcasted_iota(jnp.int32, (TQ, TQ), 1)
    same_chunk_or_earlier = (col // CHUNK) <= (row // CHUNK)
    gap = jnp.abs(row - col).astype(F32)
    decays = [jnp.where(same_chunk_or_earlier, jnp.exp(_log_gamma_of(2 * p + hh) * gap), 0.0) for hh in range(2)]
    r = lax.broadcasted_iota(jnp.int32, (LANES, LANES), 0)
    c = lax.broadcasted_iota(jnp.int32, (LANES, LANES), 1)
    own_head = (r // HEAD) == (c // HEAD)
    return lane, lg_lane, a, decays, own_head


def _ret_fwd(qa, ka, va, v_off):
    s = qa.shape[0]
    nq = s // TQ

    def body(q_ref, k_ref, v_ref, o_ref, st_ref, state):
        p = pl.program_id(0)

        @pl.when(pl.program_id(1) == 0)
        def _():
            state[...] = jnp.zeros_like(state)

        lane, lg_lane, a, decays, own_head = _ret_geometry(p)
        q = q_ref[...].astype(F32)
        k = k_ref[...]
        v = v_ref[...]
        s_in = state[...]
        st_ref[0, 0] = s_in
        out = _dot((q * jnp.exp(lg_lane * (a + 1.0))).astype(BF16), s_in.astype(BF16))
        for hh in range(2):
            hm = (lane // HEAD) == hh
            qh = jnp.where(hm, q, 0.0).astype(BF16)
            inner = _dot((_dot_nt(qh, k) * decays[hh]).astype(BF16), v)
            out = out + jnp.where(hm, inner, 0.0)
        o_ref[...] = out
        k_tail = (k.astype(F32) * jnp.exp(lg_lane * (TQ - 1.0 - a))).astype(BF16)
        state[...] = jnp.exp(lg_lane * float(TQ)) * s_in + jnp.where(own_head, _dot_tn(k_tail, v), 0.0)

    blk = lambda off: pl.BlockSpec((TQ, LANES), lambda p, i: (i, off + p))
    return pl.pallas_call(
        body, name="ret_fwd", grid=(2, nq), in_specs=[blk(0), blk(0), blk(v_off)],
        out_specs=[blk(0), pl.BlockSpec((1, 1, LANES, LANES), lambda p, i: (p, i, 0, 0))],
        out_shape=[jax.ShapeDtypeStruct((s, 2 * LANES), F32), jax.ShapeDtypeStruct((2, nq, LANES, LANES), F32)],
        scratch_shapes=[pltpu.VMEM((LANES, LANES), F32)],
        compiler_params=_cparams(("parallel", "arbitrary")),
    )(qa, ka, va)


def _ret_bwd(qa, ka, va, v_off, states, do):
    s = qa.shape[0]
    nq = s // TQ

    def body(q_ref, k_ref, v_ref, st_ref, do_ref, dq_ref, dk_ref, dv_ref, dstate):
        p = pl.program_id(0)

        @pl.when(pl.program_id(1) == 0)
        def _():
            dstate[...] = jnp.zeros_like(dstate)

        lane, lg_lane, a, decays, own_head = _ret_geometry(p)
        q = q_ref[...].astype(F32)
        k = k_ref[...]
        kf = k.astype(F32)
        v = v_ref[...]
        dov = do_ref[...]
        s_in = st_ref[0, 0].astype(BF16)
        ds_next = dstate[...]
        ds_b = ds_next.astype(BF16)
        head_decay = jnp.exp(lg_lane * (a + 1.0))
        tail_decay = jnp.exp(lg_lane * (TQ - 1.0 - a))
        k_tail = (kf * tail_decay).astype(BF16)
        dq = _dot_nt(dov.astype(BF16), s_in) * head_decay
        dk = _dot_nt(v, ds_b) * tail_decay
        dv = _dot(k_tail, ds_b)
        for hh in range(2):
            hm = (lane // HEAD) == hh
            qh = jnp.where(hm, q, 0.0).astype(BF16)
            doh = jnp.where(hm, dov, 0.0).astype(BF16)
            att = (_dot_nt(qh, k) * decays[hh]).astype(BF16)
            datt = (_dot_nt(doh, v) * decays[hh]).astype(BF16)
            dv = dv + _dot_tn(att, doh)
            dk = dk + _dot_tn(datt, qh)
            dq = dq + jnp.where(hm, _dot(datt, k), 0.0)
        dq_ref[...] = dq
        dk_ref[...] = dk
        dv_ref[...] = dv
        q_head = (q * head_decay).astype(BF16)
        dstate[...] = jnp.exp(lg_lane * float(TQ)) * ds_next + jnp.where(own_head, _dot_tn(q_head, dov.astype(BF16)), 0.0)

    blk = lambda off: pl.BlockSpec((TQ, LANES), lambda p, i: (nq - 1 - i, off + p))
    return pl.pallas_call(
        body, name="ret_bwd", grid=(2, nq),
        in_specs=[blk(0), blk(0), blk(v_off), pl.BlockSpec((1, 1, LANES, LANES), lambda p, i: (p, nq - 1 - i, 0, 0)), blk(0)],
        out_specs=[blk(0)] * 3, out_shape=[jax.ShapeDtypeStruct((s, 2 * LANES), F32)] * 3,
        scratch_shapes=[pltpu.VMEM((LANES, LANES), F32)],
        compiler_params=_cparams(("parallel", "arbitrary")),
    )(qa, ka, va, states, do)


def _seg_mean_matrix():
    r = lax.broadcasted_iota(jnp.int32, (GROUP, GROUP), 0)
    c = lax.broadcasted_iota(jnp.int32, (GROUP, GROUP), 1)
    return jnp.where((r // HEAD) == (c // HEAD), 1.0 / HEAD, 0.0).astype(BF16)


def _seg_mean(x, seg):
    h = x.astype(BF16)
    r = x - h.astype(F32)
    m = r.astype(BF16)
    lo = (r - m.astype(F32)).astype(BF16)
    return _dot(h, seg) + _dot(m, seg) + _dot(lo, seg)


def _sigmoid(x):
    return 1.0 / (1.0 + jnp.exp(-x))


def _mix_post(oa, ob, oc, od, proj, g):
    s = oa.shape[0]
    tr = _tile(s, 256)

    def body(a_ref, b_ref, c_ref, d_ref, rg_ref, g_ref, o_ref):
        gv = g_ref[...]
        o_ref[:, 0:GROUP] = _rms(a_ref[...], gv[:, 0:GROUP]).astype(BF16)
        o_ref[:, GROUP:2 * GROUP] = _rms(b_ref[...], gv[:, GROUP:2 * GROUP]).astype(BF16)
        seg = _seg_mean_matrix()
        c = c_ref[...]
        cen = c - _seg_mean(c, seg)
        n = cen * lax.rsqrt(_seg_mean(cen * cen, seg) + EPS)
        rg = rg_ref[...]
        o_ref[:, 2 * GROUP:3 * GROUP] = (n * gv[:, 2 * GROUP:3 * GROUP] * (rg * _sigmoid(rg))).astype(BF16)
        o_ref[:, 3 * GROUP:] = _rms(d_ref[...], gv[:, 3 * GROUP:]).astype(BF16)

    blk = pl.BlockSpec((tr, GROUP), lambda i: (i, 0))
    return pl.pallas_call(
        body, name="mix_post", grid=(s // tr,),
        in_specs=[blk] * 4 + [pl.BlockSpec((tr, GROUP), lambda i: (i, OFF_RG // 2)), pl.BlockSpec((1, D_MODEL), lambda i: (0, 0))],
        out_specs=pl.BlockSpec((tr, D_MODEL), lambda i: (i, 0)), out_shape=jax.ShapeDtypeStruct((s, D_MODEL), BF16),
        compiler_params=_cparams(("parallel",)),
    )(oa, ob, oc, od, proj, g.reshape(1, D_MODEL))


def _mix_post_bwd(dmixed, oa, ob, oc, od, proj, g):
    s = oa.shape[0]
    tr = _tile(s, 256)

    def body(dm_ref, a_ref, b_ref, c_ref, d_ref, rg_ref, g_ref, da_ref, db_ref, dc_ref, dd_ref, drg_ref, dg_ref):
        @pl.when(pl.program_id(0) == 0)
        def _():
            dg_ref[...] = jnp.zeros_like(dg_ref)

        gv = g_ref[...]
        dm = dm_ref[...]
        for k, (x_ref, dx_ref) in enumerate(((a_ref, da_ref), (b_ref, db_ref), (None, None), (d_ref, dd_ref))):
            if x_ref is None:
                continue
            cols = slice(k * GROUP, (k + 1) * GROUP)
            dx, gterm = _rms_bwd(x_ref[...], gv[:, cols], dm[:, cols])
            dx_ref[...] = dx
            dg_ref[:, cols] += jnp.sum(gterm, axis=0, keepdims=True)
        cols = slice(2 * GROUP, 3 * GROUP)
        seg = _seg_mean_matrix()
        c = c_ref[...]
        cen = c - _seg_mean(c, seg)
        rstd = lax.rsqrt(_seg_mean(cen * cen, seg) + EPS)
        n = cen * rstd
        rg = rg_ref[...]
        sg = _sigmoid(rg)
        gate = rg * sg
        dy = dm[:, cols]
        gc = gv[:, cols]
        dn = dy * gc * gate
        dg_ref[:, cols] += jnp.sum(dy * n * gate, axis=0, keepdims=True)
        drg_ref[...] = (dy * n * gc * (sg * (1.0 + rg * (1.0 - sg)))).astype(BF16)
        dc_ref[...] = rstd * (dn - _seg_mean(dn, seg) - n * _seg_mean(dn * n, seg))

    blk = pl.BlockSpec((tr, GROUP), lambda i: (i, 0))
    gsp = pl.BlockSpec((1, D_MODEL), lambda i: (0, 0))
    return pl.pallas_call(
        body, name="mix_post_bwd", grid=(s // tr,),
        in_specs=[pl.BlockSpec((tr, D_MODEL), lambda i: (i, 0))] + [blk] * 4 + [pl.BlockSpec((tr, GROUP), lambda i: (i, OFF_RG // 2)), gsp],
        out_specs=[blk] * 5 + [gsp],
        out_shape=[jax.ShapeDtypeStruct((s, GROUP), F32)] * 4 + [jax.ShapeDtypeStruct((s, GROUP), BF16), jax.ShapeDtypeStruct((1, D_MODEL), F32)],
        compiler_params=_cparams(("arbitrary",)),
    )(dmixed, oa, ob, oc, od, proj, g.reshape(1, D_MODEL))


def _pack_w_in(w):
    z = lambda n: jnp.zeros((w.shape[0], n), w.dtype)
    misc = jnp.concatenate([w[:, 768:772], z(KR_LANE - N_HEADS), w[:, 1156:1188], z(LANES - KR_LANE - ROPE_DIM)], axis=1)
    return jnp.concatenate([w[:, 0:768], w[:, 772:1028], w[:, 1188:2980], w[:, 1028:1156], misc], axis=1)


def _unpack_dw_in(d):
    m = OFF_MISC * LANES
    return jnp.concatenate([d[:, 0:768], d[:, m:m + N_HEADS], d[:, 768:1024], d[:, OFF_CKV * LANES:m],
                            d[:, m + KR_LANE:m + KR_LANE + ROPE_DIM], d[:, 1024:OFF_CKV * LANES]], axis=1)


def _pack_w_q(w):
    return jnp.pad(w.reshape(Q_RANK, N_HEADS, HEAD + ROPE_DIM), ((0, 0), (0, 0), (0, LANES - HEAD - ROPE_DIM))).reshape(Q_RANK, 4 * LANES)


def _unpack_dw_q(d):
    return d.reshape(Q_RANK, N_HEADS, LANES)[:, :, :HEAD + ROPE_DIM].reshape(Q_RANK, N_HEADS * (HEAD + ROPE_DIM))


def _pack_w_kv(w):
    w4 = w.reshape(KV_RANK, N_HEADS, 2 * HEAD)
    widen = lambda a: jnp.pad(a, ((0, 0), (0, 0), (0, LANES - HEAD))).reshape(KV_RANK, N_HEADS * LANES)
    return widen(w4[:, :, :HEAD]), widen(w4[:, :, HEAD:])


def _unpack_dw_kv(dk, dv):
    narrow = lambda a: a.reshape(KV_RANK, N_HEADS, LANES)[:, :, :HEAD]
    return jnp.concatenate([narrow(dk), narrow(dv)], axis=2).reshape(KV_RANK, 2 * N_HEADS * HEAD)


def _narrow_heads(a):
    return a.reshape(a.shape[0], N_HEADS, LANES)[:, :, :HEAD].reshape(a.shape[0], N_HEADS * HEAD)


def _widen_heads(a):
    return jnp.pad(a.reshape(a.shape[0], N_HEADS, HEAD), ((0, 0), (0, 0), (0, LANES - HEAD))).reshape(a.shape[0], N_HEADS * LANES)


def _layer_fwd(x, lw, tabs, tag, side=None, fox_side=None, late_weights=None, h1=None, next_gain=None):
    cos_m, sin_m, cos_r, sin_r = tabs
    if h1 is None:
        h1 = _norm_fwd(x, lw["g_mix_pre"], name=tag + "pre_norm")
    proj, projb = _matmul(h1, lw["w_in"], name=tag + "in_proj", also_bf16=True)
    bias_row = jnp.pad(lw["b_forget"], (FF_LANE, LANES - N_HEADS - FF_LANE)).reshape(1, LANES)
    cum_col, cum_row = _fox_cum(proj, bias_row)
    oa, lse_a, *fox_carried = _mixer_fwd("fox", projb, OFF_FQ, projb, OFF_FK, projb, OFF_FV, cum_col=cum_col, cum_row=cum_row,
                                         side=fox_side)
    if late_weights is not None:
        lw = {**lw, **late_weights(fox_carried[0])}
    qm, km, vm, cqn, ckvn = _mla_prep(proj, cos_m, sin_m, lw["g_q_lora"], lw["g_kv_lora"], lw["wq"], lw["wk"], lw["wv"])
    ob_wide, lse_b = _mixer_fwd("mla", qm, 0, km, 0, vm, 0)
    ob = _narrow_heads(ob_wide)
    qr, kr = _ret_prep(proj, cos_r, sin_r)
    oc, ret_states = _ret_fwd(qr, kr, projb, OFF_RV)
    od, tot_d, *carried = _mixer_fwd("sb", projb, OFF_SQ, projb, OFF_SK, projb, OFF_SV, side=side)
    mixed = _mix_post(oa, ob, oc, od, proj, lw["g_mix_out"])
    mix = _matmul(mixed, lw["w_out"], name=tag + "out_proj")
    x1, h2 = _norm_fwd(mix, lw["g_mix_post"], name=tag + "mix_post_norm", resid=x, out_dtype=F32, next_gain=lw["g_ffn_pre"])
    u = _matmul(h2, lw["w_ffn_up"], name=tag + "ffn_up", relu2=True, out_dtype=BF16, col_blocks=True)
    f = _matmul(u, lw["w_ffn_down"], name=tag + "ffn_down")
    x2, h_next = None, None
    if next_gain is not None:
        x2, h_next = _norm_fwd(f, lw["g_ffn_post"], name=tag + "ffn_post_norm", resid=x1, out_dtype=F32, next_gain=next_gain)
    saved = dict(x=x, h1=h1, proj=proj, projb=projb, bias_row=bias_row, cum_col=cum_col, cum_row=cum_row, oa=oa, lse_a=lse_a,
                 qm=qm, km=km, vm=vm, cqn=cqn, ckvn=ckvn, ob=ob, ob_wide=ob_wide, lse_b=lse_b, qr=qr, kr=kr, ret_states=ret_states, oc=oc, od=od, tot_d=tot_d, mixed=mixed,
                 mix=mix, x1=x1, h2=h2, u=u, f=f)
    return x2, saved, lw, (carried[0] if carried else None), h_next


def _layer_bwd(dx2, lw, sv, tabs, tag, side=None, ffn_side=None, fox_side=None, post_given=None, then_prev=None):
    cos_m, sin_m, cos_r, sin_r = tabs
    g = {}
    if post_given is None:
        df, g["g_ffn_post"] = _norm_bwd(sv["f"], lw["g_ffn_post"], dx2, name=tag + "ffn_post_norm_bwd", out_dtype=BF16)
    else:
        df, g["g_ffn_post"] = post_given
    du_pre = _matmul(df, lw["w_ffn_down"], name=tag + "ffn_down_dx", tb=True, out_dtype=BF16, relu2_of=sv["u"], side=ffn_side)
    ffn_carried = None
    if ffn_side is not None:
        du_pre, ffn_carried = du_pre
    g["w_ffn_down"] = _matmul(sv["u"], df, name=tag + "ffn_down_dw", ta=True)
    dh2 = _matmul(du_pre, lw["w_ffn_up"], name=tag + "ffn_up_dx", tb=True, col_blocks=True)
    g["w_ffn_up"] = _matmul(sv["h2"], du_pre, name=tag + "ffn_up_dw", ta=True, col_blocks=True)
    dx1, g["g_ffn_pre"], dmix, g["g_mix_post"] = _norm_bwd(sv["x1"], lw["g_ffn_pre"], dh2, name=tag + "ffn_pre_norm_bwd", add=dx2,
                                                           then=(sv["mix"], lw["g_mix_post"]))
    dmixed = _matmul(dmix, lw["w_out"], name=tag + "out_proj_dx", tb=True)
    g["w_out"] = _matmul(sv["mixed"], dmix, name=tag + "out_proj_dw", ta=True)
    proj, projb = sv["proj"], sv["projb"]
    doa, dob, doc, dod, drg, g["g_mix_out"] = _mix_post_bwd(dmixed, sv["oa"], sv["ob"], sv["oc"], sv["od"], proj, lw["g_mix_out"])
    dfq, dfk, dfv, dck, drs, *fox_carried = _mixer_bwd(
        "fox", projb, OFF_FQ, projb, OFF_FK, projb, OFF_FV, sv["oa"], doa, stat=sv["lse_a"], cum_col=sv["cum_col"],
        cum_row=sv["cum_row"], side=None if fox_side is None else fox_side(g))
    dqm, dkm, dvm = _mixer_bwd("mla", sv["qm"], 0, sv["km"], 0, sv["vm"], 0, sv["ob_wide"], _widen_heads(dob), stat=sv["lse_b"])
    dcq, dckv, dkr, dwq, dwk, dwv, g["g_q_lora"], g["g_kv_lora"] = _mla_prep_bwd(
        dqm, dkm, dvm, proj, sv["cqn"], sv["ckvn"], cos_m, sin_m, lw["g_q_lora"], lw["g_kv_lora"], lw["wq"], lw["wk"], lw["wv"])
    dqr, dkr_ret, drv = _ret_bwd(sv["qr"], sv["kr"], projb, OFF_RV, sv["ret_states"], doc)
    drq, drk = _ret_prep_bwd(dqr, dkr_ret, cos_r, sin_r)
    if callable(side):
        side = side(g, ffn_carried, fox_carried[0] if fox_carried else None)
    dsq, dsk, dsv, *carried = _mixer_bwd("sb", projb, OFF_SQ, projb, OFF_SK, projb, OFF_SV, sv["od"], dod, stat=sv["tot_d"], side=side)
    dmisc, db_row = _fox_gate_bwd(dck, drs, proj, sv["bias_row"], dkr)
    b = lambda a: a.astype(BF16)
    dproj = jnp.concatenate([b(dfq), b(dfk), b(dfv), dcq, drq, drk, b(drv), drg, b(dsq), b(dsk), b(dsv), dckv, dmisc], axis=1)
    dh1 = _matmul(dproj, lw["w_in"], name=tag + "in_proj_dx", tb=True)
    g["w_in"] = _matmul(sv["h1"], dproj, name=tag + "in_proj_dw", ta=True)
    dx, g["g_mix_pre"], *prev_post = _norm_bwd(sv["x"], lw["g_mix_pre"], dh1, name=tag + "pre_norm_bwd", add=dx1, then=then_prev)
    g["b_forget"] = db_row[0, FF_LANE:FF_LANE + N_HEADS]
    g["wq"], g["wk"], g["wv"] = dwq, dwk, dwv
    return dx, g, (carried[0] if carried else None), (tuple(prev_post) if prev_post else None)


def _local_step(x, positions, layers, target):
    s = x.shape[0]
    tabs = _rope_tables(positions.reshape(s, 1))
    saved, h1 = [], None
    for li, lw in enumerate(layers):
        nxt = layers[li + 1]["g_mix_pre"] if li + 1 < len(layers) else None
        x, sv, _, _, h1 = _layer_fwd(x, lw, tabs, "l%d_" % li, h1=h1, next_gain=nxt)
        saved.append(sv)
    loss_row, dx, df, dg = _loss_head(saved[-1]["f"], layers[-1]["g_ffn_post"], saved[-1]["x1"], target)
    grads, post = [None] * len(layers), (df, dg)
    for li in reversed(range(len(layers))):
        prev = (saved[li - 1]["f"], layers[li - 1]["g_ffn_post"]) if li > 0 else None
        dx, grads[li], _, post = _layer_bwd(dx, layers[li], saved[li], tabs, "l%d_" % li, post_given=post, then_prev=prev)
    return loss_row[0, 0], dx, grads


def _adamw(w, g, m, v, *, name):
    d, r, c = w.shape
    tr = 256 if r % 256 == 0 else r
    blk = pl.BlockSpec((None, tr, c), lambda l, i: (l, i, 0))
    c1 = 1.0 - ADAM_B1 ** ADAM_STEP
    c2 = 1.0 - ADAM_B2 ** ADAM_STEP

    def body(w_ref, g_ref, m_ref, v_ref, d_ref, mo_ref, vo_ref):
        gv = g_ref[...]
        mn = ADAM_B1 * m_ref[...] + (1.0 - ADAM_B1) * gv
        vn = ADAM_B2 * v_ref[...] + (1.0 - ADAM_B2) * jnp.square(gv)
        mo_ref[...] = mn
        vo_ref[...] = vn
        d_ref[...] = -ADAM_LR * ((mn / c1) / (jnp.sqrt(vn / c2) + ADAM_EPS) + ADAM_WD * w_ref[...])

    return pl.pallas_call(
        body, name=name, grid=(d, r // tr), in_specs=[blk] * 4, out_specs=[blk] * 3,
        out_shape=[jax.ShapeDtypeStruct((d, r, c), F32)] * 3, compiler_params=_cparams(("parallel", "parallel")),
    )(w, g, m, v)


BIG = ("w_in", "w_q_up", "w_kv_up", "w_out", "w_ffn_up", "w_ffn_down")
SMALL = ("g_mix_pre", "b_forget", "g_q_lora", "g_kv_lora", "g_mix_out", "g_mix_post", "g_ffn_pre", "g_ffn_post")
N_CHIPS = 4
ANY = pl.BlockSpec(memory_space=pl.ANY)


def _mesh_pos():
    return lax.axis_index("x"), lax.axis_index("y"), lax.axis_index("c")


def _other_chips(x, y):
    return [(1 - x, y), (x, 1 - y), (1 - x, 1 - y)]


def _rows_half(ref, half):
    h = ref.shape[-2] // 2
    return ref.at[(slice(None),) * (len(ref.shape) - 2) + (pl.ds(half * h, h), slice(None))]


def _remote(src, dst, send_sem, recv_sem, device):
    return pltpu.make_async_remote_copy(src_ref=src, dst_ref=dst, send_sem=send_sem, recv_sem=recv_sem, device_id=device,
                                        device_id_type=MESH)


def _comm_call(body, name, args, out_shape, n_sems):
    return pl.pallas_call(
        body, name=name, in_specs=[ANY] * len(args), out_specs=[ANY] * len(out_shape), out_shape=out_shape,
        scratch_shapes=[pltpu.SemaphoreType.DMA((n_sems,)), pltpu.SemaphoreType.DMA((n_sems,))],
        compiler_params=pltpu.CompilerParams(has_side_effects=True),
    )(*args)


def _run_side_job(side, name):
    si = len(side.inputs)

    def body(*refs):
        args = (refs[:si], refs[si:-2], refs[-2], refs[-1])
        sends = side.sends(*args)
        for cp in sends:
            cp.start()
        for cp in side.recvs(*args):
            cp.wait_recv()
        for cp in sends:
            cp.wait_send()

    return _comm_call(body, name, side.inputs, side.out_shape, side.n_sems)


def _gather_job(shards):
    n = len(shards)

    def copies(own_block, ins, outs, send_sems, recv_sems):
        x, y, c = _mesh_pos()
        return [_remote(_rows_half(ins[t], c), _rows_half(outs[t].at[2 * x + y if own_block else 2 * px + py], c),
                        send_sems.at[3 * t + j], recv_sems.at[3 * t + j], (px, py, c))
                for t in range(n) for j, (px, py) in enumerate(_other_chips(x, y))]

    return _SideJob(shards, [jax.ShapeDtypeStruct((N_CHIPS,) + a.shape, a.dtype) for a in shards], 3 * n,
                    functools.partial(copies, True), functools.partial(copies, False))


def _forward_halves(gathered):
    n = len(gathered)

    def body(*refs):
        bufs, send_sems, recv_sems = refs[n:2 * n], refs[-2], refs[-1]
        x, y, c = _mesh_pos()

        def d2d(t, j, block, half):
            region = _rows_half(bufs[t].at[block], half)
            return _remote(region, region, send_sems.at[3 * t + j], recv_sems.at[3 * t + j], (x, y, 1 - c))

        peers = list(enumerate(_other_chips(x, y)))
        sends = [d2d(t, j, 2 * px + py, c) for t in range(n) for j, (px, py) in peers]
        for cp in sends:
            cp.start()
        for t in range(n):
            for j, (px, py) in peers:
                d2d(t, j, 2 * px + py, 1 - c).wait_recv()
        for cp in sends:
            cp.wait_send()

    return pl.pallas_call(
        body, name="gather_forward", in_specs=[ANY] * n, out_specs=[ANY] * n,
        out_shape=[jax.ShapeDtypeStruct(g.shape, g.dtype) for g in gathered], input_output_aliases={t: t for t in range(n)},
        scratch_shapes=[pltpu.SemaphoreType.DMA((3 * n,)), pltpu.SemaphoreType.DMA((3 * n,))],
        compiler_params=pltpu.CompilerParams(has_side_effects=True),
    )(*gathered)


def _exchange_halves_job(gs):
    n = len(gs)

    def copies(ins, outs, send_sems, recv_sems):
        x, y, c = _mesh_pos()
        return [_remote(_rows_half(ins[t], 1 - c), outs[t], send_sems.at[t], recv_sems.at[t], (x, y, 1 - c)) for t in range(n)]

    out_shape = [jax.ShapeDtypeStruct(g.shape[:2] + (g.shape[2] // 2, g.shape[3]), g.dtype) for g in gs]
    return _SideJob(gs, out_shape, n, copies, copies)


def _pair_add(g, r, c_idx, *, name):
    nb, d, rows, cols = g.shape
    h = rows // 2
    tr = min(h, 512)
    nt = h // tr

    def body(c_ref, g_ref, r_ref, p_ref, pb_ref):
        s = g_ref[...] + r_ref[...]
        p_ref[...] = s
        pb_ref[...] = s.astype(BF16)

    blk = pl.BlockSpec((1, 1, tr, cols), lambda k, l, i, c_ref: (k, l, i, 0))
    return pl.pallas_call(
        body, name=name,
        grid_spec=pltpu.PrefetchScalarGridSpec(
            num_scalar_prefetch=1, grid=(nb, d, nt),
            in_specs=[pl.BlockSpec((1, 1, tr, cols), lambda k, l, i, c_ref: (k, l, c_ref[0] * nt + i, 0)), blk],
            out_specs=[blk, blk]),
        out_shape=[jax.ShapeDtypeStruct((nb, d, h, cols), F32), jax.ShapeDtypeStruct((nb, d, h, cols), BF16)],
        compiler_params=_cparams(("parallel", "parallel", "parallel")),
    )(c_idx, g, r)


def _exchange_chips_job(pbs):
    n = len(pbs)

    def copies(ins, outs, send_sems, recv_sems):
        x, y, c = _mesh_pos()
        return [_remote(ins[t].at[2 * px + py], outs[t].at[j], send_sems.at[3 * t + j], recv_sems.at[3 * t + j], (px, py, c))
                for t in range(n) for j, (px, py) in enumerate(_other_chips(x, y))]

    return _SideJob(pbs, [jax.ShapeDtypeStruct((3,) + p.shape[1:], p.dtype) for p in pbs], 3 * n, copies, copies)


def _chip_add(p, r, k_idx, *, name):
    _, d, h, cols = p.shape
    tr = min(h, 512)
    nt = h // tr

    def body(k_ref, p_ref, r_ref, o_ref):
        o_ref[0] = ((p_ref[0, 0] + r_ref[0, 0].astype(F32)) + r_ref[1, 0].astype(F32)) + r_ref[2, 0].astype(F32)

    return pl.pallas_call(
        body, name=name,
        grid_spec=pltpu.PrefetchScalarGridSpec(
            num_scalar_prefetch=1, grid=(d, nt),
            in_specs=[pl.BlockSpec((1, 1, tr, cols), lambda l, i, k_ref: (k_ref[0], l, i, 0)),
                      pl.BlockSpec((3, 1, tr, cols), lambda l, i, k_ref: (0, l, i, 0))],
            out_specs=pl.BlockSpec((1, tr, cols), lambda l, i, k_ref: (l, i, 0))),
        out_shape=jax.ShapeDtypeStruct((d, h, cols), F32), compiler_params=_cparams(("parallel", "parallel")),
    )(k_idx, p, r)


def _share_halves(qs):
    n = len(qs)

    def body(*refs):
        ins, outs, send_sems, recv_sems = refs[:n], refs[n:2 * n], refs[2 * n], refs[2 * n + 1]
        x, y, c = _mesh_pos()
        cps = [_remote(ins[t], outs[t], send_sems.at[t], recv_sems.at[t], (x, y, 1 - c)) for t in range(n)]
        for cp in cps:
            cp.start()
        for cp in cps:
            cp.wait_recv()
        for cp in cps:
            cp.wait_send()

    return _comm_call(body, "grad_pair_share", qs, [jax.ShapeDtypeStruct(q.shape, q.dtype) for q in qs], n)


def _all_reduce_small(v):
    r, cols = v.shape
    n_dev = 8

    def body(v_ref, o_ref, buf, send_sems, recv_sems):
        x, y, c = _mesh_pos()
        me = 4 * x + 2 * y + c
        buf[me] = v_ref[...]

        def peer(j):
            return (1 - x if j & 4 else x, 1 - y if j & 2 else y, 1 - c if j & 1 else c)

        def copy(j, slot):
            return pltpu.make_async_remote_copy(src_ref=v_ref, dst_ref=buf.at[slot], send_sem=send_sems.at[j - 1],
                                                recv_sem=recv_sems.at[j - 1], device_id=peer(j), device_id_type=MESH)

        sends = [copy(j, me) for j in range(1, n_dev)]
        for cp in sends:
            cp.start()
        for j in range(1, n_dev):
            px, py, pc = peer(j)
            copy(j, 4 * px + 2 * py + pc).wait_recv()
        for cp in sends:
            cp.wait_send()
        acc = buf[0]
        for d in range(1, n_dev):
            acc = acc + buf[d]
        o_ref[...] = acc

    vm = pl.BlockSpec(memory_space=pltpu.VMEM)
    return pl.pallas_call(
        body, name="small_all_reduce", in_specs=[vm], out_specs=vm, out_shape=jax.ShapeDtypeStruct((r, cols), F32),
        scratch_shapes=[pltpu.VMEM((n_dev, r, cols), F32), pltpu.SemaphoreType.DMA((n_dev - 1,)), pltpu.SemaphoreType.DMA((n_dev - 1,))],
        compiler_params=pltpu.CompilerParams(has_side_effects=True),
    )(v)


_COL_SHARDED = ("w_in", "w_q_up", "w_kv_up", "w_ffn_up")


def _shard_cols(blocks, a, b):
    c = blocks[0].shape[-1]
    out = []
    while a < b:
        k = a // c
        hi = min(b, (k + 1) * c)
        out.append(blocks[k][:, a - k * c:hi - k * c])
        a = hi
    return out


def _pack_w_in_shards(blocks):
    z = lambda n: [jnp.zeros((blocks[0].shape[0], n), blocks[0].dtype)]
    cols = lambda a, b: _shard_cols(blocks, a, b)
    return jnp.concatenate(cols(0, 768) + cols(772, 1028) + cols(1188, 2980) + cols(1028, 1156) + cols(768, 772)
                           + z(KR_LANE - N_HEADS) + cols(1156, 1188) + z(LANES - KR_LANE - ROPE_DIM), axis=1)


def _whole_layer(name, blocks):
    if name in _COL_SHARDED:
        return jnp.concatenate([blocks[k] for k in range(N_CHIPS)], axis=1)
    return blocks.reshape(N_CHIPS * blocks.shape[1], blocks.shape[2])


def _split_layer(name, whole):
    if name in _COL_SHARDED:
        c = whole.shape[1] // N_CHIPS
        return jnp.stack([whole[:, k * c:(k + 1) * c] for k in range(N_CHIPS)])
    return whole.reshape(N_CHIPS, whole.shape[0] // N_CHIPS, whole.shape[1])


def _small_to_rows(d):
    v = jnp.concatenate([d[k].astype(F32).reshape(-1) for k in SMALL])
    rows = -(-v.shape[0] // (8 * LANES)) * 8
    return jnp.pad(v, (0, rows * LANES - v.shape[0])).reshape(rows, LANES)


def _small_from_rows(rows, shapes):
    v = rows.reshape(-1)
    out, o = {}, 0
    for k in SMALL:
        sz = int(np.prod(shapes[k]))
        out[k] = v[o:o + sz].reshape(shapes[k])
        o += sz
    return out


_ARG_NAMES = ("x", "positions", "g_mix_pre", "w_in", "b_forget", "g_q_lora", "w_q_up", "g_kv_lora", "w_kv_up", "g_mix_out", "w_out",
              "g_mix_post", "g_ffn_pre", "w_ffn_up", "w_ffn_down", "g_ffn_post")
_WEIGHTS = _ARG_NAMES[2:]


def kernel(x, positions, g_mix_pre, w_in, b_forget, g_q_lora, w_q_up, g_kv_lora, w_kv_up, g_mix_out, w_out, g_mix_post, g_ffn_pre, w_ffn_up, w_ffn_down, g_ffn_post, loss_target, m_g_mix_pre, m_w_in, m_b_forget, m_g_q_lora, m_w_q_up, m_g_kv_lora, m_w_kv_up, m_g_mix_out, m_w_out, m_g_mix_post, m_g_ffn_pre, m_w_ffn_up, m_w_ffn_down, m_g_ffn_post, v_g_mix_pre, v_w_in, v_b_forget, v_g_q_lora, v_w_q_up, v_g_kv_lora, v_w_kv_up, v_g_mix_out, v_w_out, v_g_mix_post, v_g_ffn_pre, v_w_ffn_up, v_w_ffn_down, v_g_ffn_post):
    w = dict(g_mix_pre=g_mix_pre, w_in=w_in, b_forget=b_forget, g_q_lora=g_q_lora, w_q_up=w_q_up, g_kv_lora=g_kv_lora, w_kv_up=w_kv_up,
             g_mix_out=g_mix_out, w_out=w_out, g_mix_post=g_mix_post, g_ffn_pre=g_ffn_pre, w_ffn_up=w_ffn_up, w_ffn_down=w_ffn_down,
             g_ffn_post=g_ffn_post)
    m = dict(g_mix_pre=m_g_mix_pre, w_in=m_w_in, b_forget=m_b_forget, g_q_lora=m_g_q_lora, w_q_up=m_w_q_up, g_kv_lora=m_g_kv_lora,
             w_kv_up=m_w_kv_up, g_mix_out=m_g_mix_out, w_out=m_w_out, g_mix_post=m_g_mix_post, g_ffn_pre=m_g_ffn_pre,
             w_ffn_up=m_w_ffn_up, w_ffn_down=m_w_ffn_down, g_ffn_post=m_g_ffn_post)
    v = dict(g_mix_pre=v_g_mix_pre, w_in=v_w_in, b_forget=v_b_forget, g_q_lora=v_g_q_lora, w_q_up=v_w_q_up, g_kv_lora=v_g_kv_lora,
             w_kv_up=v_w_kv_up, g_mix_out=v_g_mix_out, w_out=v_w_out, g_mix_post=v_g_mix_post, g_ffn_pre=v_g_ffn_pre,
             w_ffn_up=v_w_ffn_up, w_ffn_down=v_w_ffn_down, g_ffn_post=v_g_ffn_post)
    small_shapes = {k: w[k].shape for k in SMALL}
    c_idx = lax.axis_index("c").astype(jnp.int32).reshape(1)
    k_idx = (2 * lax.axis_index("x") + lax.axis_index("y")).astype(jnp.int32).reshape(1)
    first_core = lax.axis_index("c") == 0

    mine = 2 * lax.axis_index("x") + lax.axis_index("y")
    shards_b = [{k: w[k][l:l + 1].astype(BF16) for k in BIG} for l in range(DEPTH)]
    gains = [dict(g_mix_pre=g_mix_pre[l], b_forget=b_forget[l], g_q_lora=g_q_lora[l], g_kv_lora=g_kv_lora[l], g_mix_out=g_mix_out[l],
                  g_mix_post=g_mix_post[l], g_ffn_pre=g_ffn_pre[l], g_ffn_post=g_ffn_post[l]) for l in range(DEPTH)]
    FIRST, LATER = ("w_in", "w_q_up", "w_kv_up"), ("w_out", "w_ffn_up", "w_ffn_down")
    EARLY_GRADS, LATE_GRADS = ("w_ffn_down", "w_ffn_up", "w_out"), ("w_in", "w_q_up", "w_kv_up")

    def gather_job(l, names):
        return _gather_job([shards_b[l][k] for k in names])

    def weights_of(l, names, gathered):
        four = {k: lax.dynamic_update_slice(g, shards_b[l][k][None], (mine, 0, 0, 0))[:, 0]
                for k, g in zip(names, _forward_halves(gathered))}
        out = {}
        for k in names:
            if k == "w_in":
                out["w_in"] = _pack_w_in_shards(four[k])
            elif k == "w_q_up":
                out["wq"] = _pack_w_q(_whole_layer(k, four[k]))
            elif k == "w_kv_up":
                out["wk"], out["wv"] = _pack_w_kv(_whole_layer(k, four[k]))
            elif k == "w_ffn_up":
                out[k] = four[k]
            else:
                out[k] = _whole_layer(k, four[k])
        return out

    def grad_blocks(names, g):
        whole = dict(w_in=lambda: _unpack_dw_in(g["w_in"]), w_q_up=lambda: _unpack_dw_q(g["wq"]),
                     w_kv_up=lambda: _unpack_dw_kv(g["wk"], g["wv"]), w_out=lambda: g["w_out"], w_ffn_down=lambda: g["w_ffn_down"])
        return [(g[k] if k == "w_ffn_up" else _split_layer(k, whole[k]()))[:, None] for k in names]

    def pair_sums(names, blocks, theirs):
        return [_pair_add(b, r, c_idx, name="grad_pair_add_" + k) for k, b, r in zip(names, blocks, theirs)]

    def exchange_job(*pairs):
        return _exchange_chips_job([pb for pair in pairs for (_, pb) in pair])

    def finish_grads(names, pair, partial):
        half = [_chip_add(p, r, k_idx, name="grad_chip_add_" + k) for k, (p, _), r in zip(names, pair, partial)]
        return {k: jnp.where(first_core, jnp.concatenate([q, s], axis=1), jnp.concatenate([s, q], axis=1))
                for k, q, s in zip(names, half, _share_halves(half))}

    seq = x.shape[1]
    tabs = _rope_tables(positions[0].reshape(seq, 1))
    first0 = weights_of(0, FIRST, _run_side_job(gather_job(0, FIRST), "gather_weights_l0"))
    x1, saved0, lw0, gathered1, h1 = _layer_fwd(x[0], {**gains[0], **first0}, tabs, "l0_", fox_side=gather_job(0, LATER),
                                                late_weights=lambda got: weights_of(0, LATER, got), side=gather_job(1, BIG),
                                                next_gain=gains[1]["g_mix_pre"])
    lw1 = {**gains[1], **weights_of(1, BIG, gathered1)}
    _, saved1, _, _, _ = _layer_fwd(x1, lw1, tabs, "l1_", h1=h1)
    loss_row, dx, df1, dg1 = _loss_head(saved1["f"], lw1["g_ffn_post"], saved1["x1"], loss_target[0])
    loss = lax.psum(loss_row[0, 0], ("x", "y", "c"))
    dx, grads1, _, post0 = _layer_bwd(dx, lw1, saved1, tabs, "l1_", post_given=(df1, dg1),
                                      then_prev=(saved0["f"], lw0["g_ffn_post"]))
    blocks1 = grad_blocks(BIG, grads1)
    early_blocks0, pair1, early0 = [], [], []

    def beside_l0_fox_backward(g):
        early_blocks0.extend(grad_blocks(EARLY_GRADS, g))
        return _exchange_halves_job(early_blocks0)

    def beside_l0_sb_backward(g, theirs1, theirs_early0):
        pair1.extend(pair_sums(BIG, blocks1, theirs1))
        early0.extend(pair_sums(EARLY_GRADS, early_blocks0, theirs_early0))
        return exchange_job(pair1, early0)

    dx, grads0, partial, _ = _layer_bwd(dx, lw0, saved0, tabs, "l0_", ffn_side=_exchange_halves_job(blocks1),
                                        fox_side=beside_l0_fox_backward, side=beside_l0_sb_backward, post_given=post0)
    big1 = finish_grads(BIG, pair1, partial[:len(BIG)])
    big0 = finish_grads(EARLY_GRADS, early0, partial[len(BIG):])
    late_blocks0 = grad_blocks(LATE_GRADS, grads0)
    late0 = pair_sums(LATE_GRADS, late_blocks0, _run_side_job(_exchange_halves_job(late_blocks0), "grad_pair_exchange_l0"))
    big0.update(finish_grads(LATE_GRADS, late0, _run_side_job(exchange_job(late0), "grad_chip_exchange_l0")))
    g_big = {k: jnp.concatenate([big0[k], big1[k]], axis=0) for k in BIG}
    grads = [grads0, grads1]

    g_small_local = {k: jnp.stack([grads[l][k].reshape(small_shapes[k][1:]) for l in range(DEPTH)]) for k in SMALL}
    g_small = _small_from_rows(_all_reduce_small(_small_to_rows(g_small_local)), small_shapes)

    g_all = {**g_big, **g_small}
    delta, new_m, new_v = {}, {}, {}
    for k in BIG:
        delta[k], new_m[k], new_v[k] = _adamw(w[k], g_all[k], m[k], v[k], name="adamw_" + k)
    ds, ms, vs = _adamw(*[_small_to_rows(t)[None] for t in (w, g_small, m, v)], name="adamw_small")
    delta.update(_small_from_rows(ds, small_shapes))
    new_m.update(_small_from_rows(ms, small_shapes))
    new_v.update(_small_from_rows(vs, small_shapes))

    grad_x = dx.reshape(x.shape)
    return (loss, grad_x, *[g_all[k] for k in _WEIGHTS], *[delta[k] for k in _WEIGHTS], *[new_m[k] for k in _WEIGHTS],
            *[new_v[k] for k in _WEIGHTS])
```

```python
import functools
import math

import numpy as np
import jax
import jax.numpy as jnp
from jax import lax
from jax.experimental import pallas as pl
from jax.experimental.pallas import tpu as pltpu

F32 = jnp.float32
BF16 = jnp.bfloat16
MESH = pl.DeviceIdType.MESH

D_MODEL = 1024
DEPTH = 2
CHUNK = 64
GROUP = 256
HEAD = 64
N_HEADS = 4
Q_RANK = 256
KV_RANK = 128
ROPE_DIM = 32
D_FF = 4096
D_IN = 2980
D_INP = 3072
ROPE_BASE = 10000.0
EPS = 1e-6
LANES = 128
TQ = 128
GATE_ROWS = 512
CONTRACT_TILE = 4096
NEG = -1e30

ADAM_LR, ADAM_B1, ADAM_B2, ADAM_EPS, ADAM_WD, ADAM_STEP = 0.001, 0.9, 0.999, 1e-08, 0.01, 10

OFF_FQ, OFF_FK, OFF_FV, OFF_CQ = 0, 2, 4, 6
OFF_RQ, OFF_RK, OFF_RV, OFF_RG = 8, 10, 12, 14
OFF_SQ, OFF_SK, OFF_SV = 16, 18, 20
OFF_CKV, OFF_MISC = 22, 23
FF_LANE, KR_LANE = 0, 64

VMEM_LIMIT = 56 * 1024 * 1024


def _tile(dim, pref):
    return pref if dim % pref == 0 else dim


def _cparams(sem, vmem=None):
    return pltpu.CompilerParams(dimension_semantics=sem, vmem_limit_bytes=vmem or VMEM_LIMIT)


def _dot(a, b):
    return jnp.dot(a, b, preferred_element_type=F32)


def _dot_nt(a, b):
    return lax.dot_general(a, b, (((1,), (1,)), ((), ())), preferred_element_type=F32)


def _dot_tn(a, b):
    return lax.dot_general(a, b, (((0,), (0,)), ((), ())), preferred_element_type=F32)


def _dot_exact(a, b):
    return jnp.dot(a, b, precision=lax.Precision.HIGHEST, preferred_element_type=F32)


def _matmul(a, b, *, name, ta=False, tb=False, out_dtype=F32, tm=1024, tn=1024, tk=CONTRACT_TILE,
            relu2=False, relu2_of=None, also_bf16=False, side=None, col_blocks=False):
    if ta:
        kdim, m = a.shape
    else:
        m, kdim = a.shape
    if col_blocks and not ta:
        n = b.shape[1] if tb else b.shape[0] * b.shape[2]
        if tb:
            kdim = b.shape[0] * b.shape[2]
    else:
        n = b.shape[0] if tb else b.shape[1]
    tm, tn, tk = _tile(m, tm), _tile(n, tn), _tile(kdim, tk)
    nk = kdim // tk
    a_spec = pl.BlockSpec((tk, tm), lambda i, j, k: (k, i)) if ta else pl.BlockSpec((tm, tk), lambda i, j, k: (i, k))
    b_spec = pl.BlockSpec((tn, tk), lambda i, j, k: (j, k)) if tb else pl.BlockSpec((tk, tn), lambda i, j, k: (k, j))
    o_spec = pl.BlockSpec((tm, tn), lambda i, j, k: (i, j))
    if col_blocks and ta:
        o_spec = pl.BlockSpec((None, tm, tn), lambda i, j, k: (j, i, 0))
    elif col_blocks and tb:
        assert tk == kdim
        b_spec = pl.BlockSpec((b.shape[0], tn, b.shape[2]), lambda i, j, k: (0, j, 0))
    elif col_blocks:
        assert b.shape[2] == tn
        b_spec = pl.BlockSpec((None, tk, tn), lambda i, j, k: (j, k, 0))
    two = also_bf16

    def body(*refs):
        refs = list(refs)
        a_ref, b_ref = refs[0], refs[1]
        e_ref = refs[2] if relu2_of is not None else None
        pos = 3 if relu2_of is not None else 2
        o_ref = refs[pos]
        o2_ref = refs[pos + 1] if two else None
        acc_ref = refs[-1]
        k = pl.program_id(2)
        av = a_ref[...].astype(BF16)
        if col_blocks and tb:
            bv = jnp.concatenate([b_ref[q] for q in range(b.shape[0])], axis=1).astype(BF16)
        else:
            bv = b_ref[...].astype(BF16)
        if ta:
            part = _dot_tn(av, bv)
        elif tb:
            part = _dot_nt(av, bv)
        else:
            part = _dot(av, bv)

        @pl.when(k == 0)
        def _():
            acc_ref[...] = part

        @pl.when(k > 0)
        def _():
            acc_ref[...] += part

        @pl.when(k == nk - 1)
        def _():
            r = acc_ref[...]
            if relu2_of is not None:
                r = r * (2.0 * jnp.sqrt(e_ref[...].astype(F32)))
            if relu2:
                r = jnp.square(jnp.maximum(r, 0.0))
            o_ref[...] = r.astype(o_ref.dtype)
            if also_bf16:
                o2_ref[...] = r.astype(BF16)

    in_specs = [a_spec, b_spec]
    args = [a, b]
    if relu2_of is not None:
        in_specs.append(o_spec)
        args.append(relu2_of)
    out_shape = [jax.ShapeDtypeStruct((n // tn, m, tn) if (col_blocks and ta) else (m, n), out_dtype)]
    out_specs = [o_spec]
    if two:
        out_shape.append(jax.ShapeDtypeStruct((m, n), BF16))
        out_specs.append(o_spec)
    grid = (m // tm, n // tn, nk)
    side_in, side_out, side_scratch = _side_specs(side)
    res = pl.pallas_call(
        _carry_side_job(body, len(args), len(out_shape), side, grid), name=name, grid=grid,
        in_specs=in_specs + side_in, out_specs=out_specs + side_out,
        out_shape=out_shape + ([] if side is None else side.out_shape),
        scratch_shapes=[pltpu.VMEM((tm, tn), F32)] + side_scratch,
        compiler_params=_cparams(("parallel", "parallel", "arbitrary") if side is None else ("arbitrary",) * 3),
    )(*args, *([] if side is None else side.inputs))
    main = res[:len(out_shape)]
    main = main if two else main[0]
    return main if side is None else (main, res[len(out_shape):])


def _rms(x, g):
    r = lax.rsqrt(jnp.mean(x * x, axis=-1, keepdims=True) + EPS)
    return x * r * g


def _rms_bwd(x, g, dy):
    r = lax.rsqrt(jnp.mean(x * x, axis=-1, keepdims=True) + EPS)
    xh = x * r
    gdy = dy * g
    dx = r * (gdy - xh * jnp.mean(xh * gdy, axis=-1, keepdims=True))
    return dx, xh * dy


def _norm_fwd(x, g, *, name, resid=None, out_dtype=BF16, next_gain=None):
    s, d = x.shape
    tr = _tile(s, 256)
    row = pl.BlockSpec((tr, d), lambda i: (i, 0))
    gsp = pl.BlockSpec((1, d), lambda i: (0, 0))

    def body(*refs):
        refs = list(refs)
        x_ref, g_ref = refs[:2]
        y = _rms(x_ref[...], g_ref[...])
        pos = 2
        if resid is not None:
            y = refs[pos][...] + y
            pos += 1
        if next_gain is None:
            refs[pos][...] = y.astype(refs[pos].dtype)
        else:
            refs[pos + 1][...] = y.astype(refs[pos + 1].dtype)
            refs[pos + 2][...] = _rms(y, refs[pos][...]).astype(BF16)

    args = [x, g.reshape(1, d)] + ([] if resid is None else [resid]) + ([] if next_gain is None else [next_gain.reshape(1, d)])
    in_specs = [row, gsp] + ([] if resid is None else [row]) + ([] if next_gain is None else [gsp])
    first = jax.ShapeDtypeStruct((s, d), out_dtype)
    if next_gain is None:
        out_specs, out_shape = row, first
    else:
        out_specs, out_shape = [row, row], [first, jax.ShapeDtypeStruct((s, d), BF16)]
    return pl.pallas_call(
        body, name=name, grid=(s // tr,), in_specs=in_specs, out_specs=out_specs, out_shape=out_shape,
        compiler_params=_cparams(("parallel",)),
    )(*args)


def _norm_bwd(x, g, dy, *, name, add=None, out_dtype=F32, then=None):
    s, d = x.shape
    tr = _tile(s, 256)
    row = pl.BlockSpec((tr, d), lambda i: (i, 0))
    gsp = pl.BlockSpec((1, d), lambda i: (0, 0))
    n_in = 3 + (add is not None) + (2 if then is not None else 0)

    def body(*refs):
        ins, outs = refs[:n_in], refs[n_in:]
        x_ref, g_ref, dy_ref = ins[:3]
        dx, gterm = _rms_bwd(x_ref[...], g_ref[...], dy_ref[...].astype(F32))
        if add is not None:
            dx = dx + ins[3][...]
        outs[0][...] = dx.astype(outs[0].dtype)
        terms = [(outs[1], gterm)]
        if then is not None:
            dx2, gterm2 = _rms_bwd(ins[-2][...], ins[-1][...], dx)
            outs[2][...] = dx2.astype(BF16)
            terms.append((outs[3], gterm2))

        @pl.when(pl.program_id(0) == 0)
        def _():
            for dg_ref, _ in terms:
                dg_ref[...] = jnp.zeros_like(dg_ref)

        for dg_ref, term in terms:
            dg_ref[...] += jnp.sum(term, axis=0, keepdims=True)

    args = [x, g.reshape(1, d), dy] + ([] if add is None else [add]) + ([] if then is None else [then[0], then[1].reshape(1, d)])
    in_specs = [row, gsp, row] + ([] if add is None else [row]) + ([] if then is None else [row, gsp])
    out_specs = [row, gsp] + ([] if then is None else [row, gsp])
    out_shape = [jax.ShapeDtypeStruct((s, d), out_dtype), jax.ShapeDtypeStruct((1, d), F32)]
    if then is not None:
        out_shape += [jax.ShapeDtypeStruct((s, d), BF16), jax.ShapeDtypeStruct((1, d), F32)]
    return pl.pallas_call(
        body, name=name, grid=(s // tr,), in_specs=in_specs, out_specs=out_specs, out_shape=out_shape,
        compiler_params=_cparams(("arbitrary",)),
    )(*args)


def _loss_head(f, g, resid, target):
    s, d = f.shape
    tr = _tile(s, 256)
    row = pl.BlockSpec((tr, d), lambda i: (i, 0))
    gsp = pl.BlockSpec((1, d), lambda i: (0, 0))
    lsp = pl.BlockSpec((1, LANES), lambda i: (0, 0))

    def body(f_ref, g_ref, r_ref, t_ref, l_ref, dy_ref, df_ref, dg_ref):
        fv, gv = f_ref[...], g_ref[...]
        e = (r_ref[...] + _rms(fv, gv)) - t_ref[...]
        dy = e * (1.0 / d)
        dy_ref[...] = dy
        df, gterm = _rms_bwd(fv, gv, dy)
        df_ref[...] = df.astype(BF16)

        @pl.when(pl.program_id(0) == 0)
        def _():
            l_ref[...] = jnp.zeros_like(l_ref)
            dg_ref[...] = jnp.zeros_like(dg_ref)

        part = 0.5 * jnp.sum(jnp.mean(e * e, axis=-1, keepdims=True), axis=0, keepdims=True)
        l_ref[...] += jnp.broadcast_to(part, (1, LANES))
        dg_ref[...] += jnp.sum(gterm, axis=0, keepdims=True)

    return pl.pallas_call(
        body, name="loss_head", grid=(s // tr,), in_specs=[row, gsp, row, row], out_specs=[lsp, row, row, gsp],
        out_shape=[jax.ShapeDtypeStruct((1, LANES), F32), jax.ShapeDtypeStruct((s, d), F32), jax.ShapeDtypeStruct((s, d), BF16),
                   jax.ShapeDtypeStruct((1, d), F32)],
        compiler_params=_cparams(("arbitrary",)),
    )(f, g.reshape(1, d), resid, target)


def _rope_tables(pos_col):
    s = pos_col.shape[0]
    tr = _tile(s, 512)
    f_mla = ROPE_BASE ** (-jnp.arange(ROPE_DIM // 2, dtype=F32) / (ROPE_DIM // 2))
    f_ret = ROPE_BASE ** (-jnp.arange(HEAD // 2, dtype=F32) / (HEAD // 2))
    fm = jnp.concatenate([jnp.zeros((64,), F32), f_mla, f_mla, jnp.zeros((32,), F32)]).reshape(1, LANES)
    fr = jnp.tile(jnp.concatenate([f_ret, f_ret]), 2).reshape(1, LANES)

    def body(p_ref, fm_ref, fr_ref, cm_ref, sm_ref, cr_ref, sr_ref):
        p = p_ref[...].astype(F32)
        am = p * fm_ref[...]
        ar = p * fr_ref[...]
        cm_ref[...] = jnp.cos(am)
        sm_ref[...] = jnp.sin(am)
        cr_ref[...] = jnp.tile(jnp.cos(ar), (1, 2))
        sr_ref[...] = jnp.tile(jnp.sin(ar), (1, 2))

    return pl.pallas_call(
        body, name="rope_tables", grid=(s // tr,),
        in_specs=[pl.BlockSpec((tr, 1), lambda i: (i, 0)), pl.BlockSpec((1, LANES), lambda i: (0, 0)),
                  pl.BlockSpec((1, LANES), lambda i: (0, 0))],
        out_specs=[pl.BlockSpec((tr, LANES), lambda i: (i, 0))] * 2 + [pl.BlockSpec((tr, 2 * LANES), lambda i: (i, 0))] * 2,
        out_shape=[jax.ShapeDtypeStruct((s, LANES), F32)] * 2 + [jax.ShapeDtypeStruct((s, 2 * LANES), F32)] * 2,
        compiler_params=_cparams(("parallel",)),
    )(pos_col, fm, fr)


def _lane(shape):
    return lax.broadcasted_iota(jnp.int32, shape, len(shape) - 1)


def _rot_mla(z):
    l = _lane(z.shape) % LANES
    n = z.shape[-1]
    return jnp.where(l < 80, -pltpu.roll(z, n - 16, 1), pltpu.roll(z, 16, 1))


def _rot_mla_t(y):
    l = _lane(y.shape) % LANES
    n = y.shape[-1]
    return jnp.where((l >= 64) & (l < 80), pltpu.roll(y, n - 16, 1),
                     jnp.where((l >= 80) & (l < 96), -pltpu.roll(y, 16, 1), 0.0))


def _rot_ret(z):
    l = _lane(z.shape) % HEAD
    n = z.shape[-1]
    return jnp.where(l < 32, -pltpu.roll(z, n - 32, 1), pltpu.roll(z, 32, 1))


def _rot_ret_t(y):
    l = _lane(y.shape) % HEAD
    n = y.shape[-1]
    return jnp.where(l < 32, pltpu.roll(y, n - 32, 1), -pltpu.roll(y, 32, 1))


def _log_sigmoid(x):
    return jnp.minimum(x, 0.0) - jnp.log1p(jnp.exp(-jnp.abs(x)))


def _fox_cum(proj, bias_row):
    s = proj.shape[0]
    fb = _tile(s, GATE_ROWS)
    nb = s // fb

    def body(x_ref, b_ref, cc_ref, cr_ref, carry_ref):
        @pl.when(pl.program_id(0) == 0)
        def _():
            carry_ref[...] = jnp.zeros_like(carry_ref)

        ls = _log_sigmoid(x_ref[...] + b_ref[...])
        r = lax.broadcasted_iota(jnp.int32, (fb, fb), 0)
        c = lax.broadcasted_iota(jnp.int32, (fb, fb), 1)
        tri = (c <= r).astype(F32)
        cum = _dot_exact(tri, ls) + carry_ref[...]
        carry_ref[...] = cum[fb - 1:fb, :]
        cc_ref[...] = cum
        cr_ref[...] = cum.T[0:8, :]

    return pl.pallas_call(
        body, name="fox_cum", grid=(nb,),
        in_specs=[pl.BlockSpec((fb, LANES), lambda i: (i, OFF_MISC)), pl.BlockSpec((1, LANES), lambda i: (0, 0))],
        out_specs=[pl.BlockSpec((fb, LANES), lambda i: (i, 0)), pl.BlockSpec((8, fb), lambda i: (0, i))],
        out_shape=[jax.ShapeDtypeStruct((s, LANES), F32), jax.ShapeDtypeStruct((8, s), F32)],
        scratch_shapes=[pltpu.VMEM((1, LANES), F32)],
        compiler_params=_cparams(("arbitrary",)),
    )(proj, bias_row)


def _fox_gate_bwd(dck, drs, proj, bias_row, dkr):
    s = proj.shape[0]
    fb = _tile(s, GATE_ROWS)
    nb = s // fb

    def body(d_ref, r_ref, x_ref, b_ref, k_ref, o_ref, db_ref, carry_ref):
        @pl.when(pl.program_id(0) == 0)
        def _():
            carry_ref[...] = jnp.zeros_like(carry_ref)
            db_ref[...] = jnp.zeros_like(db_ref)

        rows = jnp.concatenate([d_ref[0], d_ref[1], jnp.zeros((LANES - 16, fb), F32)], axis=0)
        t = rows.T
        l = _lane((fb, LANES))
        r0, r1 = r_ref[0], r_ref[1]
        rsum = jnp.where(l == 0, r0[:, 0:1], jnp.where(l == 1, r0[:, HEAD:HEAD + 1],
                         jnp.where(l == 2, r1[:, 0:1], jnp.where(l == 3, r1[:, HEAD:HEAD + 1], 0.0))))
        dcum = rsum - jnp.where(l < 2, t, pltpu.roll(t, LANES - 6, 1))
        r = lax.broadcasted_iota(jnp.int32, (fb, fb), 0)
        c = lax.broadcasted_iota(jnp.int32, (fb, fb), 1)
        triu = (c >= r).astype(F32)
        rc = _dot_exact(triu, dcum) + carry_ref[...]
        carry_ref[...] = rc[0:1, :]
        f = x_ref[...] + b_ref[...]
        sig_neg = 1.0 / (1.0 + jnp.exp(f))
        df = jnp.where(l < N_HEADS, rc * sig_neg, 0.0)
        db_ref[...] += jnp.sum(df, axis=0, keepdims=True)
        o_ref[...] = (df + k_ref[...]).astype(o_ref.dtype)

    rev = lambda i: nb - 1 - i
    return pl.pallas_call(
        body, name="fox_gate_bwd", grid=(nb,),
        in_specs=[pl.BlockSpec((2, 8, fb), lambda i: (0, 0, rev(i))), pl.BlockSpec((2, fb, LANES), lambda i: (0, rev(i), 0)),
                  pl.BlockSpec((fb, LANES), lambda i: (rev(i), OFF_MISC)),
                  pl.BlockSpec((1, LANES), lambda i: (0, 0)), pl.BlockSpec((fb, LANES), lambda i: (rev(i), 0))],
        out_specs=[pl.BlockSpec((fb, LANES), lambda i: (rev(i), 0)), pl.BlockSpec((1, LANES), lambda i: (0, 0))],
        out_shape=[jax.ShapeDtypeStruct((s, LANES), BF16), jax.ShapeDtypeStruct((1, LANES), F32)],
        scratch_shapes=[pltpu.VMEM((1, LANES), F32)],
        compiler_params=_cparams(("arbitrary",)),
    )(dck, drs, proj, bias_row, dkr)


def _mla_prep(proj, cos_m, sin_m, g_q, g_kv, wq, wk, wv):
    s = proj.shape[0]
    tr = _tile(s, 512)

    def body(cq_ref, ckv_ref, misc_ref, cos_ref, sin_ref, gq_ref, gkv_ref, wq_ref, wk_ref, wv_ref,
             q_ref, k_ref, v_ref, cqn_ref, ckvn_ref):
        cos4 = jnp.tile(cos_ref[...], (1, 4))
        sin4 = jnp.tile(sin_ref[...], (1, 4))
        cqn = _rms(cq_ref[...], gq_ref[...]).astype(BF16)
        ckvn = _rms(ckv_ref[...], gkv_ref[...]).astype(BF16)
        cqn_ref[...] = cqn
        ckvn_ref[...] = ckvn
        zq = _dot(cqn, wq_ref[...])
        q_ref[...] = (zq * cos4 + _rot_mla(zq) * sin4).astype(BF16)
        l = _lane((tr, LANES))
        kr = jnp.where((l >= KR_LANE) & (l < KR_LANE + ROPE_DIM), misc_ref[...], 0.0)
        zk = _dot(ckvn, wk_ref[...]) + jnp.tile(kr, (1, 4))
        k_ref[...] = (zk * cos4 + _rot_mla(zk) * sin4).astype(BF16)
        v_ref[...] = _dot(ckvn, wv_ref[...]).astype(BF16)

    full = lambda a: pl.BlockSpec(a.shape, lambda i: (0, 0))
    rowb = lambda w: pl.BlockSpec((tr, w), lambda i: (i, 0))
    gq2, gkv2 = g_q.reshape(1, Q_RANK), g_kv.reshape(1, KV_RANK)
    return pl.pallas_call(
        body, name="mla_prep", grid=(s // tr,),
        in_specs=[pl.BlockSpec((tr, 256), lambda i: (i, OFF_CQ // 2)), pl.BlockSpec((tr, LANES), lambda i: (i, OFF_CKV)),
                  pl.BlockSpec((tr, LANES), lambda i: (i, OFF_MISC)), rowb(LANES), rowb(LANES),
                  full(gq2), full(gkv2), full(wq), full(wk), full(wv)],
        out_specs=[rowb(512), rowb(512), rowb(512), rowb(256), rowb(128)],
        out_shape=[jax.ShapeDtypeStruct((s, 512), BF16), jax.ShapeDtypeStruct((s, 512), BF16), jax.ShapeDtypeStruct((s, 512), BF16),
                   jax.ShapeDtypeStruct((s, 256), BF16), jax.ShapeDtypeStruct((s, 128), BF16)],
        compiler_params=_cparams(("parallel",)),
    )(proj, proj, proj, cos_m, sin_m, gq2, gkv2, wq, wk, wv)


def _mla_prep_bwd(dq, dk, dv, proj, cqn, ckvn, cos_m, sin_m, g_q, g_kv, wq, wk, wv):
    s = proj.shape[0]
    tr = _tile(s, 512)

    def body(dq_ref, dk_ref, dv_ref, cq_ref, ckv_ref, cqn_ref, ckvn_ref, cos_ref, sin_ref, gq_ref, gkv_ref,
             wq_ref, wk_ref, wv_ref, dcq_ref, dckv_ref, dkr_ref, dwq_ref, dwk_ref, dwv_ref, dgq_ref, dgkv_ref):
        @pl.when(pl.program_id(0) == 0)
        def _():
            for r in (dwq_ref, dwk_ref, dwv_ref, dgq_ref, dgkv_ref):
                r[...] = jnp.zeros_like(r)

        cos4 = jnp.tile(cos_ref[...], (1, 4))
        sin4 = jnp.tile(sin_ref[...], (1, 4))
        dqv = dq_ref[...]
        dzq = dqv * cos4 + _rot_mla_t(dqv * sin4)
        dkv_ = dk_ref[...]
        dzk = dkv_ * cos4 + _rot_mla_t(dkv_ * sin4)
        l = _lane((tr, LANES))
        in_rope = (l >= KR_LANE) & (l < KR_LANE + ROPE_DIM)
        dkr = dzk[:, 0:128] + dzk[:, 128:256] + dzk[:, 256:384] + dzk[:, 384:512]
        dkr_ref[...] = jnp.where(in_rope, dkr, 0.0)
        dzq_b = dzq.astype(BF16)
        dzk_b = dzk.astype(BF16)
        dv_b = dv_ref[...].astype(BF16)
        dcqn = _dot_nt(dzq_b, wq_ref[...])
        dckvn = _dot_nt(dzk_b, wk_ref[...]) + _dot_nt(dv_b, wv_ref[...])
        dwq_ref[...] += _dot_tn(cqn_ref[...], dzq_b)
        dwk_ref[...] += _dot_tn(ckvn_ref[...], dzk_b)
        dwv_ref[...] += _dot_tn(ckvn_ref[...], dv_b)
        dcq, gq_term = _rms_bwd(cq_ref[...], gq_ref[...], dcqn)
        dckv, gkv_term = _rms_bwd(ckv_ref[...], gkv_ref[...], dckvn)
        dcq_ref[...] = dcq.astype(BF16)
        dckv_ref[...] = dckv.astype(BF16)
        dgq_ref[...] += jnp.sum(gq_term, axis=0, keepdims=True)
        dgkv_ref[...] += jnp.sum(gkv_term, axis=0, keepdims=True)

    full = lambda shp: pl.BlockSpec(shp, lambda i: (0, 0))
    rowb = lambda w: pl.BlockSpec((tr, w), lambda i: (i, 0))
    gq2, gkv2 = g_q.reshape(1, Q_RANK), g_kv.reshape(1, KV_RANK)
    return pl.pallas_call(
        body, name="mla_prep_bwd", grid=(s // tr,),
        in_specs=[rowb(512), rowb(512), rowb(512),
                  pl.BlockSpec((tr, 256), lambda i: (i, OFF_CQ // 2)), pl.BlockSpec((tr, LANES), lambda i: (i, OFF_CKV)),
                  rowb(256), rowb(128), rowb(LANES), rowb(LANES), full((1, Q_RANK)), full((1, KV_RANK)),
                  full(wq.shape), full(wk.shape), full(wv.shape)],
        out_specs=[rowb(256), rowb(128), rowb(128), full(wq.shape), full(wk.shape), full(wv.shape),
                   full((1, Q_RANK)), full((1, KV_RANK))],
        out_shape=[jax.ShapeDtypeStruct((s, 256), BF16), jax.ShapeDtypeStruct((s, 128), BF16), jax.ShapeDtypeStruct((s, 128), F32),
                   jax.ShapeDtypeStruct(wq.shape, F32), jax.ShapeDtypeStruct(wk.shape, F32), jax.ShapeDtypeStruct(wv.shape, F32),
                   jax.ShapeDtypeStruct((1, Q_RANK), F32), jax.ShapeDtypeStruct((1, KV_RANK), F32)],
        compiler_params=_cparams(("arbitrary",)),
    )(dq, dk, dv, proj, proj, cqn, ckvn, cos_m, sin_m, gq2, gkv2, wq, wk, wv)


def _ret_prep(proj, cos_r, sin_r):
    s = proj.shape[0]
    tr = _tile(s, 512)

    def body(q_ref, k_ref, cos_ref, sin_ref, qo_ref, ko_ref):
        cos, sin = cos_ref[...], sin_ref[...]
        q, k = q_ref[...], k_ref[...]
        qo_ref[...] = (q * cos + _rot_ret(q) * sin).astype(BF16)
        ko_ref[...] = ((k * cos + _rot_ret(k) * sin) * (HEAD ** -0.5)).astype(BF16)

    rowb = pl.BlockSpec((tr, 256), lambda i: (i, 0))
    return pl.pallas_call(
        body, name="ret_prep", grid=(s // tr,),
        in_specs=[pl.BlockSpec((tr, 256), lambda i: (i, OFF_RQ // 2)), pl.BlockSpec((tr, 256), lambda i: (i, OFF_RK // 2)), rowb, rowb],
        out_specs=[rowb, rowb], out_shape=[jax.ShapeDtypeStruct((s, 256), BF16)] * 2,
        compiler_params=_cparams(("parallel",)),
    )(proj, proj, cos_r, sin_r)


def _ret_prep_bwd(dq, dk, cos_r, sin_r):
    s = dq.shape[0]
    tr = _tile(s, 512)

    def body(dq_ref, dk_ref, cos_ref, sin_ref, qo_ref, ko_ref):
        cos, sin = cos_ref[...], sin_ref[...]
        q, k = dq_ref[...], dk_ref[...] * (HEAD ** -0.5)
        qo_ref[...] = (q * cos + _rot_ret_t(q * sin)).astype(BF16)
        ko_ref[...] = (k * cos + _rot_ret_t(k * sin)).astype(BF16)

    rowb = pl.BlockSpec((tr, 256), lambda i: (i, 0))
    return pl.pallas_call(
        body, name="ret_prep_bwd", grid=(s // tr,), in_specs=[rowb] * 4, out_specs=[rowb, rowb],
        out_shape=[jax.ShapeDtypeStruct((s, 256), BF16)] * 2, compiler_params=_cparams(("parallel",)),
    )(dq, dk, cos_r, sin_r)


_LOG_GAMMA = [float(np.log1p(-np.float32(2.0) ** np.float32(-5.0 - h))) for h in range(N_HEADS)]
_MLA_SCALE = float((HEAD + ROPE_DIM) ** -0.5)
_QK_SCALE = float(HEAD ** -0.5)
KEY_BLOCKS = 4
QB = 256


def _split2(x):
    h = x.astype(BF16)
    return h, (x - h.astype(F32)).astype(BF16)


def _dot2(x, u):
    h, lo = _split2(x)
    return _dot(h, u) + _dot(lo, u)


def _head_pick(block, head, axis):
    idx = lax.broadcasted_iota(jnp.int32, block.shape, axis)
    return jnp.sum(jnp.where(idx == head, block, 0.0), axis=axis, keepdims=True)


def _log_gamma_of(head):
    lg = jnp.float32(_LOG_GAMMA[3])
    for h in (2, 1, 0):
        lg = jnp.where(head == h, jnp.float32(_LOG_GAMMA[h]), lg)
    return lg


def _mixer_specs(mode, s, q_off, k_off, v_off):
    nhb = 2
    bw = 2 * LANES if mode == "mla" else LANES
    nsub = KEY_BLOCKS if (s // TQ) % KEY_BLOCKS == 0 else 1
    q_spec = pl.BlockSpec((QB, bw), lambda p, i: (i, q_off + p))
    k_spec = pl.BlockSpec((s, bw), lambda p, i: (0, k_off + p))
    v_spec = pl.BlockSpec((s, bw), lambda p, i: (0, v_off + p))
    return nhb, N_HEADS // nhb, nsub, q_spec, k_spec, v_spec


def _mixer_geometry(mode, i, nsub):
    w = TQ * nsub
    row = lax.broadcasted_iota(jnp.int32, (QB, w), 0)
    col = lax.broadcasted_iota(jnp.int32, (QB, w), 1)
    nfull = (i * QB) // w
    dist = col - row
    if mode in ("fox", "sb"):
        rel = dist
    else:
        rel = col - (row | (CHUNK - 1))

    def visible(c):
        off = c * w - i * QB
        return (rel + off) < 0 if mode == "sb" else (rel + off) <= 0

    return nfull, dist, visible


class _SideJob:
    def __init__(self, inputs, out_shape, n_sems, sends, recvs):
        self.inputs, self.out_shape, self.n_sems, self.sends, self.recvs = list(inputs), list(out_shape), n_sems, sends, recvs


def _carry_side_job(body, n_in, n_out, side, n_steps):
    if side is None:
        return body
    si, so = len(side.inputs), len(side.out_shape)

    def at(corner):
        ok = pl.program_id(0) == corner[0]
        for d in range(1, len(n_steps)):
            ok = ok & (pl.program_id(d) == corner[d])
        return ok

    def wrapped(*refs):
        ins, s_ins = refs[:n_in], refs[n_in:n_in + si]
        outs, s_outs = refs[n_in + si:n_in + si + n_out], refs[n_in + si + n_out:n_in + si + n_out + so]
        scratch, send, recv = refs[n_in + si + n_out + so:-2], refs[-2], refs[-1]

        @pl.when(at([0] * len(n_steps)))
        def _():
            for cp in side.sends(s_ins, s_outs, send, recv):
                cp.start()

        body(*ins, *outs, *scratch)

        @pl.when(at([n - 1 for n in n_steps]))
        def _():
            for cp in side.recvs(s_ins, s_outs, send, recv):
                cp.wait_recv()
            for cp in side.sends(s_ins, s_outs, send, recv):
                cp.wait_send()

    return wrapped


def _side_specs(side):
    if side is None:
        return [], [], []
    hbm = pl.BlockSpec(memory_space=pl.ANY)
    return ([hbm] * len(side.inputs), [hbm] * len(side.out_shape),
            [pltpu.SemaphoreType.DMA((side.n_sems,)), pltpu.SemaphoreType.DMA((side.n_sems,))])


def _mixer_fwd(mode, qa, q_off, ka, k_off, va, v_off, *, cum_col=None, cum_row=None, side=None):
    s = qa.shape[0]
    nq = s // QB
    nhb, nblk, nsub, q_spec, k_spec, v_spec = _mixer_specs(mode, s, q_off, k_off, v_off)
    w = TQ * nsub
    softmax = mode in ("fox", "mla")

    def body(*refs):
        refs = list(refs)
        q_ref, k_ref, v_ref = refs[:3]
        refs = refs[3:]
        if mode == "fox":
            cc_ref, cr_ref = refs[:2]
            refs = refs[2:]
        o_ref = refs[0]
        st_ref = refs[1]
        p = pl.program_id(0)
        i = pl.program_id(1)
        nfull, dist, visible = _mixer_geometry(mode, i, nsub)
        lane = _lane((1, LANES))
        heads = [nhb * p + hh for hh in range(nhb)]
        wide = mode == "mla"
        q_scale = _QK_SCALE if mode in ("fox", "sb") else 1.0
        cols = [slice(hh * LANES, (hh + 1) * LANES) if wide else slice(None) for hh in range(nhb)]
        if wide:
            qs = [q_ref[:, cols[hh]] for hh in range(nhb)]
        else:
            qf = q_ref[...].astype(F32) * q_scale
            qs = [jnp.where((lane // HEAD) == hh, qf, 0.0).astype(BF16) for hh in range(nhb)]
        if mode == "fox":
            cqs = [_head_pick(cc_ref[...], h, 1) for h in heads]
        if mode == "sb":
            r1 = lax.broadcasted_iota(jnp.int32, (TQ, TQ), 0)
            c1 = lax.broadcasted_iota(jnp.int32, (TQ, TQ), 1)
            u_after = (r1 > c1).astype(BF16)

        def chunk(c):
            return pl.ds(pl.multiple_of(c * w, w), w)

        def scores(c):
            js = chunk(c)
            return tuple(_dot_nt(qs[hh], k_ref[js, cols[hh]]) for hh in range(nhb))

        def head_step(hh, c, js, sc, vj, carry, last):
            if softmax:
                m, l, acc = carry
                if mode == "fox":
                    ck = _head_pick(cr_ref[:, js], heads[hh], 0)
                    sc = sc + (cqs[hh] - ck)
                else:
                    sc = sc * _MLA_SCALE
                if last:
                    sc = jnp.where(visible(c), sc, NEG)
                m_new = jnp.maximum(m, jnp.max(sc, axis=-1, keepdims=True))
                alpha = jnp.exp(m - m_new)
                pr = jnp.exp(sc - m_new)
                l = alpha * l + jnp.sum(pr, axis=-1, keepdims=True)
                acc = alpha * acc + _dot(pr.astype(BF16), vj)
                return m_new, l, acc
            run, acc = carry
            z = sc
            log_beta = jnp.minimum(z, 0.0) - jnp.log(1.0 + jnp.exp(-jnp.abs(z)))
            log_stay = log_beta - z
            if last:
                vis = visible(c)
                log_stay = jnp.where(vis, log_stay, 0.0)
            parts = [None] * nsub
            for b in reversed(range(nsub)):
                ls_b = log_stay[:, b * TQ:(b + 1) * TQ]
                parts[b] = _dot2(ls_b, u_after) + run
                run = run + jnp.sum(ls_b, axis=-1, keepdims=True)
            later = parts[0] if nsub == 1 else jnp.concatenate(parts, axis=1)
            wgt = jnp.exp(log_beta + later)
            if last:
                wgt = jnp.where(vis, wgt, 0.0)
            return run, acc + _dot(wgt.astype(BF16), vj)

        def step(c, c_next, state, last):
            scs, carries = state
            nxt = scores(c_next) if c_next is not None else None
            js = chunk(c)
            return nxt, tuple(head_step(hh, c, js, scs[hh], v_ref[js, cols[hh]], carries[hh], last) for hh in range(nhb))

        zero_acc = jnp.zeros((QB, LANES), F32)
        zero1 = jnp.zeros((QB, 1), F32)
        if softmax:
            init = tuple((jnp.full((QB, 1), NEG, F32), zero1, zero_acc) for _ in range(nhb))
        else:
            init = tuple((zero1, zero_acc) for _ in range(nhb))
        if mode == "sb":
            state = step(nfull, jnp.maximum(nfull - 1, 0), (scores(nfull), init), True)
            _, carries = lax.fori_loop(0, nfull, lambda t, st: step(nfull - 1 - t, jnp.maximum(nfull - 2 - t, 0), st, False), state)
        else:
            state = lax.fori_loop(0, nfull, lambda c, st: step(c, c + 1, st, False), (scores(0), init))
            _, carries = step(nfull, None, state, True)
        if softmax:
            outs = [acc / l for (m, l, acc) in carries]
            stats = [m + jnp.log(l) for (m, l, acc) in carries]
        else:
            outs, stats = [acc for (run, acc) in carries], [run for (run, acc) in carries]
        hm0 = (lane // HEAD) == 0
        pick = lambda a: jnp.where(hm0, a[0], a[1])
        if wide:
            for hh in range(nhb):
                o_ref[:, cols[hh]] = outs[hh]
        else:
            o_ref[...] = pick(outs)
        st_ref[0] = pick(stats)

    in_specs = [q_spec, k_spec, v_spec]
    args = [qa, ka, va]
    if mode == "fox":
        in_specs += [pl.BlockSpec((QB, LANES), lambda p, i: (i, 0)), pl.BlockSpec((8, s), lambda p, i: (0, 0))]
        args += [cum_col, cum_row]
    bw = 2 * LANES if mode == "mla" else LANES
    out_specs = [pl.BlockSpec((QB, bw), lambda p, i: (i, p))]
    out_shape = [jax.ShapeDtypeStruct((s, nblk * bw), F32)]
    out_specs.append(pl.BlockSpec((1, QB, LANES), lambda p, i: (p, i, 0)))
    out_shape.append(jax.ShapeDtypeStruct((nblk, s, LANES), F32))
    side_in, side_out, side_scratch = _side_specs(side)
    res = pl.pallas_call(
        _carry_side_job(body, len(args), len(out_shape), side, (nblk, nq)), name=mode + "_fwd", grid=(nblk, nq),
        in_specs=in_specs + side_in, out_specs=out_specs + side_out,
        out_shape=out_shape + ([] if side is None else side.out_shape), scratch_shapes=side_scratch,
        compiler_params=_cparams(("parallel", "parallel") if side is None else ("arbitrary", "arbitrary")),
    )(*args, *([] if side is None else side.inputs))
    return (res[0], res[1]) if side is None else (res[0], res[1], res[2:])


def _mixer_bwd(mode, qa, q_off, ka, k_off, va, v_off, o, do, *, stat=None, cum_col=None, cum_row=None, side=None):
    s = qa.shape[0]
    nq = s // QB
    nhb, nblk, nsub, q_spec, k_spec, v_spec = _mixer_specs(mode, s, q_off, k_off, v_off)
    w = TQ * nsub
    softmax = mode in ("fox", "mla")

    def body(*refs):
        refs = list(refs)
        q_ref, k_ref, v_ref, o_ref, do_ref = refs[:5]
        refs = refs[5:]
        st_ref = refs[0]
        refs = refs[1:]
        if mode == "fox":
            cc_ref, cr_ref = refs[:2]
            refs = refs[2:]
        dq_ref, dk_ref, dv_ref = refs[:3]
        dck_ref, drs_ref = refs[3:5] if mode == "fox" else (None, None)
        p = pl.program_id(0)
        i = pl.program_id(1)

        @pl.when(i == 0)
        def _():
            dk_ref[...] = jnp.zeros_like(dk_ref)
            dv_ref[...] = jnp.zeros_like(dv_ref)
            if mode == "fox":
                dck_ref[...] = jnp.zeros_like(dck_ref)

        nfull, dist, visible = _mixer_geometry(mode, i, nsub)
        lane = _lane((1, LANES))
        heads = [nhb * p + hh for hh in range(nhb)]
        dov = do_ref[...]
        wide = mode == "mla"
        q_scale = _QK_SCALE if mode in ("fox", "sb") else 1.0
        cols = [slice(hh * LANES, (hh + 1) * LANES) if wide else slice(None) for hh in range(nhb)]
        if wide:
            prod = dov * o_ref[...]
            qs = [q_ref[:, cols[hh]] for hh in range(nhb)]
            dos = [dov[:, cols[hh]].astype(BF16) for hh in range(nhb)]
            deltas = [jnp.sum(prod[:, cols[hh]], axis=-1, keepdims=True) for hh in range(nhb)]
        else:
            qf = q_ref[...].astype(F32) * q_scale
            prod = dov * o_ref[...]
            hms = [(lane // HEAD) == hh for hh in range(nhb)]
            qs = [jnp.where(hm, qf, 0.0).astype(BF16) for hm in hms]
            dos = [jnp.where(hm, dov, 0.0).astype(BF16) for hm in hms]
            deltas = [jnp.sum(jnp.where(hm, prod, 0.0), axis=-1, keepdims=True) for hm in hms]
        st = st_ref[0]
        stats = [st[:, hh * HEAD:hh * HEAD + 1] for hh in range(nhb)]
        if mode == "fox":
            cqs = [_head_pick(cc_ref[...], h, 1) for h in heads]
        if mode == "sb":
            r1 = lax.broadcasted_iota(jnp.int32, (TQ, TQ), 0)
            c1 = lax.broadcasted_iota(jnp.int32, (TQ, TQ), 1)
            u_upto = (r1 <= c1).astype(BF16)
            u_before = (r1 < c1).astype(BF16)

        def chunk(c):
            return pl.ds(pl.multiple_of(c * w, w), w)

        def scores(c):
            js = chunk(c)
            if mode == "sb":
                return tuple((_dot_nt(qs[hh], k_ref[js, cols[hh]]), None) for hh in range(nhb))
            return tuple((_dot_nt(qs[hh], k_ref[js, cols[hh]]), _dot_nt(dos[hh], v_ref[js, cols[hh]])) for hh in range(nhb))

        def emit(hh, js, ds_b, pr_b, dq):
            dk_ref[js, cols[hh]] += _dot_tn(ds_b, qs[hh])
            dv_ref[js, cols[hh]] += _dot_tn(pr_b, dos[hh])
            return dq + _dot(ds_b, k_ref[js, cols[hh]])

        def head_step(hh, c, js, sc_dp, carry, last):
            sc, dp = sc_dp
            if dp is None:
                dp = _dot_nt(dos[hh], v_ref[js, cols[hh]])
            if softmax:
                dq, rsum = carry
                if mode == "fox":
                    ck = _head_pick(cr_ref[:, js], heads[hh], 0)
                    sc = sc + (cqs[hh] - ck)
                else:
                    sc = sc * _MLA_SCALE
                if last:
                    sc = jnp.where(visible(c), sc, NEG)
                pr = jnp.exp(sc - stats[hh])
                ds = pr * (dp - deltas[hh])
                if mode == "fox":
                    dck_ref[0, hh:hh + 1, js] += jnp.sum(ds, axis=0, keepdims=True)
                    rsum = rsum + jnp.sum(ds, axis=-1, keepdims=True)
                if mode == "mla":
                    ds = ds * _MLA_SCALE
                return emit(hh, js, ds.astype(BF16), pr.astype(BF16), dq), rsum
            seen, gsum, dq = carry
            z = sc
            log_beta = jnp.minimum(z, 0.0) - jnp.log(1.0 + jnp.exp(-jnp.abs(z)))
            log_stay = log_beta - z
            if last:
                vis = visible(c)
                log_stay = jnp.where(vis, log_stay, 0.0)
            parts = []
            for b in range(nsub):
                ls_b = log_stay[:, b * TQ:(b + 1) * TQ]
                parts.append((stats[hh] - seen) - _dot2(ls_b, u_upto))
                seen = seen + jnp.sum(ls_b, axis=-1, keepdims=True)
            later = parts[0] if nsub == 1 else jnp.concatenate(parts, axis=1)
            wgt = jnp.exp(log_beta + later)
            if last:
                wgt = jnp.where(vis, wgt, 0.0)
            g = dp * wgt
            parts = []
            for b in range(nsub):
                g_b = g[:, b * TQ:(b + 1) * TQ]
                parts.append(gsum + _dot2(g_b, u_before))
                gsum = gsum + jnp.sum(g_b, axis=-1, keepdims=True)
            before = parts[0] if nsub == 1 else jnp.concatenate(parts, axis=1)
            beta = jnp.exp(log_beta)
            dz = g * (1.0 - beta) - beta * before
            if last:
                dz = jnp.where(vis, dz, 0.0)
            return seen, gsum, emit(hh, js, dz.astype(BF16), wgt.astype(BF16), dq)

        def step(c, c_next, state, last):
            scs, carries = state
            nxt = scores(c_next) if c_next is not None else None
            js = chunk(c)
            return nxt, tuple(head_step(hh, c, js, scs[hh], carries[hh], last) for hh in range(nhb))

        zero_acc = jnp.zeros((QB, LANES), F32)
        zero1 = jnp.zeros((QB, 1), F32)
        if softmax:
            init = tuple((zero_acc, zero1) for _ in range(nhb))
        else:
            init = tuple((zero1, zero1, zero_acc) for _ in range(nhb))
        state = lax.fori_loop(0, nfull, lambda c, st: step(c, c + 1, st, False), (scores(0), init))
        _, carries = step(nfull, None, state, True)
        if softmax:
            dqs = [dq for (dq, rsum) in carries]
        else:
            dqs = [dq for (seen, gsum, dq) in carries]
        hm0 = (lane // HEAD) == 0
        if wide:
            for hh in range(nhb):
                dq_ref[:, cols[hh]] = dqs[hh]
        else:
            dq_ref[...] = jnp.where(hm0, dqs[0], dqs[1]) * q_scale
        if mode == "fox":
            drs_ref[0] = jnp.where(hm0, carries[0][1], carries[1][1])

    bw = 2 * LANES if mode == "mla" else LANES
    pair_blk = pl.BlockSpec((QB, bw), lambda p, i: (i, p))
    full_blk = pl.BlockSpec((s, bw), lambda p, i: (0, p))
    stat_blk = pl.BlockSpec((1, QB, LANES), lambda p, i: (p, i, 0))
    in_specs = [q_spec, k_spec, v_spec, pair_blk, pair_blk]
    args = [qa, ka, va, o, do]
    in_specs.append(stat_blk)
    args.append(stat)
    if mode == "fox":
        in_specs += [pl.BlockSpec((QB, LANES), lambda p, i: (i, 0)), pl.BlockSpec((8, s), lambda p, i: (0, 0))]
        args += [cum_col, cum_row]
    out_specs = [pair_blk, full_blk, full_blk]
    out_shape = [jax.ShapeDtypeStruct((s, nblk * bw), F32)] * 3
    if mode == "fox":
        out_specs += [pl.BlockSpec((1, 8, s), lambda p, i: (p, 0, 0)), stat_blk]
        out_shape += [jax.ShapeDtypeStruct((2, 8, s), F32), jax.ShapeDtypeStruct((2, s, LANES), F32)]
    side_in, side_out, side_scratch = _side_specs(side)
    res = pl.pallas_call(
        _carry_side_job(body, len(args), len(out_shape), side, (nblk, nq)), name=mode + "_bwd", grid=(nblk, nq),
        in_specs=in_specs + side_in, out_specs=out_specs + side_out,
        out_shape=out_shape + ([] if side is None else side.out_shape), scratch_shapes=side_scratch,
        compiler_params=_cparams(("parallel", "arbitrary") if side is None else ("arbitrary", "arbitrary")),
    )(*args, *([] if side is None else side.inputs))
    return res if side is None else (*res[:len(out_shape)], res[len(out_shape):])


def _ret_geometry(p):
    lane = _lane((1, LANES))
    lg_lane = jnp.where(lane < HEAD, _log_gamma_of(2 * p), _log_gamma_of(2 * p + 1))
    a = lax.broadcasted_iota(jnp.int32, (TQ, 1), 0).astype(F32)
    row = lax.broadcasted_iota(jnp.int32, (TQ, TQ), 0)
    col = lax.broadcasted_iota(jnp.int32, (TQ, TQ), 1)
    same_chunk_or_earlier = (col // CHUNK) <= (row // CHUNK)
    gap = jnp.abs(row - col).astype(F32)
    decays = [jnp.where(same_chunk_or_earlier, jnp.exp(_log_gamma_of(2 * p + hh) * gap), 0.0) for hh in range(2)]
    r = lax.broadcasted_iota(jnp.int32, (LANES, LANES), 0)
    c = lax.broadcasted_iota(jnp.int32, (LANES, LANES), 1)
    own_head = (r // HEAD) == (c // HEAD)
    return lane, lg_lane, a, decays, own_head


def _ret_fwd(qa, ka, va, v_off):
    s = qa.shape[0]
    nq = s // TQ

    def body(q_ref, k_ref, v_ref, o_ref, st_ref, state):
        p = pl.program_id(0)

        @pl.when(pl.program_id(1) == 0)
        def _():
            state[...] = jnp.zeros_like(state)

        lane, lg_lane, a, decays, own_head = _ret_geometry(p)
        q = q_ref[...].astype(F32)
        k = k_ref[...]
        v = v_ref[...]
        s_in = state[...]
        st_ref[0, 0] = s_in
        out = _dot((q * jnp.exp(lg_lane * (a + 1.0))).astype(BF16), s_in.astype(BF16))
        for hh in range(2):
            hm = (lane // HEAD) == hh
            qh = jnp.where(hm, q, 0.0).astype(BF16)
            inner = _dot((_dot_nt(qh, k) * decays[hh]).astype(BF16), v)
            out = out + jnp.where(hm, inner, 0.0)
        o_ref[...] = out
        k_tail = (k.astype(F32) * jnp.exp(lg_lane * (TQ - 1.0 - a))).astype(BF16)
        state[...] = jnp.exp(lg_lane * float(TQ)) * s_in + jnp.where(own_head, _dot_tn(k_tail, v), 0.0)

    blk = lambda off: pl.BlockSpec((TQ, LANES), lambda p, i: (i, off + p))
    return pl.pallas_call(
        body, name="ret_fwd", grid=(2, nq), in_specs=[blk(0), blk(0), blk(v_off)],
        out_specs=[blk(0), pl.BlockSpec((1, 1, LANES, LANES), lambda p, i: (p, i, 0, 0))],
        out_shape=[jax.ShapeDtypeStruct((s, 2 * LANES), F32), jax.ShapeDtypeStruct((2, nq, LANES, LANES), F32)],
        scratch_shapes=[pltpu.VMEM((LANES, LANES), F32)],
        compiler_params=_cparams(("parallel", "arbitrary")),
    )(qa, ka, va)


def _ret_bwd(qa, ka, va, v_off, states, do):
    s = qa.shape[0]
    nq = s // TQ

    def body(q_ref, k_ref, v_ref, st_ref, do_ref, dq_ref, dk_ref, dv_ref, dstate):
        p = pl.program_id(0)

        @pl.when(pl.program_id(1) == 0)
        def _():
            dstate[...] = jnp.zeros_like(dstate)

        lane, lg_lane, a, decays, own_head = _ret_geometry(p)
        q = q_ref[...].astype(F32)
        k = k_ref[...]
        kf = k.astype(F32)
        v = v_ref[...]
        dov = do_ref[...]
        s_in = st_ref[0, 0].astype(BF16)
        ds_next = dstate[...]
        ds_b = ds_next.astype(BF16)
        head_decay = jnp.exp(lg_lane * (a + 1.0))
        tail_decay = jnp.exp(lg_lane * (TQ - 1.0 - a))
        k_tail = (kf * tail_decay).astype(BF16)
        dq = _dot_nt(dov.astype(BF16), s_in) * head_decay
        dk = _dot_nt(v, ds_b) * tail_decay
        dv = _dot(k_tail, ds_b)
        for hh in range(2):
            hm = (lane // HEAD) == hh
            qh = jnp.where(hm, q, 0.0).astype(BF16)
            doh = jnp.where(hm, dov, 0.0).astype(BF16)
            att = (_dot_nt(qh, k) * decays[hh]).astype(BF16)
            datt = (_dot_nt(doh, v) * decays[hh]).astype(BF16)
            dv = dv + _dot_tn(att, doh)
            dk = dk + _dot_tn(datt, qh)
            dq = dq + jnp.where(hm, _dot(datt, k), 0.0)
        dq_ref[...] = dq
        dk_ref[...] = dk
        dv_ref[...] = dv
        q_head = (q * head_decay).astype(BF16)
        dstate[...] = jnp.exp(lg_lane * float(TQ)) * ds_next + jnp.where(own_head, _dot_tn(q_head, dov.astype(BF16)), 0.0)

    blk = lambda off: pl.BlockSpec((TQ, LANES), lambda p, i: (nq - 1 - i, off + p))
    return pl.pallas_call(
        body, name="ret_bwd", grid=(2, nq),
        in_specs=[blk(0), blk(0), blk(v_off), pl.BlockSpec((1, 1, LANES, LANES), lambda p, i: (p, nq - 1 - i, 0, 0)), blk(0)],
        out_specs=[blk(0)] * 3, out_shape=[jax.ShapeDtypeStruct((s, 2 * LANES), F32)] * 3,
        scratch_shapes=[pltpu.VMEM((LANES, LANES), F32)],
        compiler_params=_cparams(("parallel", "arbitrary")),
    )(qa, ka, va, states, do)


def _seg_mean_matrix():
    r = lax.broadcasted_iota(jnp.int32, (GROUP, GROUP), 0)
    c = lax.broadcasted_iota(jnp.int32, (GROUP, GROUP), 1)
    return jnp.where((r // HEAD) == (c // HEAD), 1.0 / HEAD, 0.0).astype(BF16)


def _seg_mean(x, seg):
    h = x.astype(BF16)
    r = x - h.astype(F32)
    m = r.astype(BF16)
    lo = (r - m.astype(F32)).astype(BF16)
    return _dot(h, seg) + _dot(m, seg) + _dot(lo, seg)


def _sigmoid(x):
    return 1.0 / (1.0 + jnp.exp(-x))


def _mix_post(oa, ob, oc, od, proj, g):
    s = oa.shape[0]
    tr = _tile(s, 256)

    def body(a_ref, b_ref, c_ref, d_ref, rg_ref, g_ref, o_ref):
        gv = g_ref[...]
        o_ref[:, 0:GROUP] = _rms(a_ref[...], gv[:, 0:GROUP]).astype(BF16)
        o_ref[:, GROUP:2 * GROUP] = _rms(b_ref[...], gv[:, GROUP:2 * GROUP]).astype(BF16)
        seg = _seg_mean_matrix()
        c = c_ref[...]
        cen = c - _seg_mean(c, seg)
        n = cen * lax.rsqrt(_seg_mean(cen * cen, seg) + EPS)
        rg = rg_ref[...]
        o_ref[:, 2 * GROUP:3 * GROUP] = (n * gv[:, 2 * GROUP:3 * GROUP] * (rg * _sigmoid(rg))).astype(BF16)
        o_ref[:, 3 * GROUP:] = _rms(d_ref[...], gv[:, 3 * GROUP:]).astype(BF16)

    blk = pl.BlockSpec((tr, GROUP), lambda i: (i, 0))
    return pl.pallas_call(
        body, name="mix_post", grid=(s // tr,),
        in_specs=[blk] * 4 + [pl.BlockSpec((tr, GROUP), lambda i: (i, OFF_RG // 2)), pl.BlockSpec((1, D_MODEL), lambda i: (0, 0))],
        out_specs=pl.BlockSpec((tr, D_MODEL), lambda i: (i, 0)), out_shape=jax.ShapeDtypeStruct((s, D_MODEL), BF16),
        compiler_params=_cparams(("parallel",)),
    )(oa, ob, oc, od, proj, g.reshape(1, D_MODEL))


def _mix_post_bwd(dmixed, oa, ob, oc, od, proj, g):
    s = oa.shape[0]
    tr = _tile(s, 256)

    def body(dm_ref, a_ref, b_ref, c_ref, d_ref, rg_ref, g_ref, da_ref, db_ref, dc_ref, dd_ref, drg_ref, dg_ref):
        @pl.when(pl.program_id(0) == 0)
        def _():
            dg_ref[...] = jnp.zeros_like(dg_ref)

        gv = g_ref[...]
        dm = dm_ref[...]
        for k, (x_ref, dx_ref) in enumerate(((a_ref, da_ref), (b_ref, db_ref), (None, None), (d_ref, dd_ref))):
            if x_ref is None:
                continue
            cols = slice(k * GROUP, (k + 1) * GROUP)
            dx, gterm = _rms_bwd(x_ref[...], gv[:, cols], dm[:, cols])
            dx_ref[...] = dx
            dg_ref[:, cols] += jnp.sum(gterm, axis=0, keepdims=True)
        cols = slice(2 * GROUP, 3 * GROUP)
        seg = _seg_mean_matrix()
        c = c_ref[...]
        cen = c - _seg_mean(c, seg)
        rstd = lax.rsqrt(_seg_mean(cen * cen, seg) + EPS)
        n = cen * rstd
        rg = rg_ref[...]
        sg = _sigmoid(rg)
        gate = rg * sg
        dy = dm[:, cols]
        gc = gv[:, cols]
        dn = dy * gc * gate
        dg_ref[:, cols] += jnp.sum(dy * n * gate, axis=0, keepdims=True)
        drg_ref[...] = (dy * n * gc * (sg * (1.0 + rg * (1.0 - sg)))).astype(BF16)
        dc_ref[...] = rstd * (dn - _seg_mean(dn, seg) - n * _seg_mean(dn * n, seg))

    blk = pl.BlockSpec((tr, GROUP), lambda i: (i, 0))
    gsp = pl.BlockSpec((1, D_MODEL), lambda i: (0, 0))
    return pl.pallas_call(
        body, name="mix_post_bwd", grid=(s // tr,),
        in_specs=[pl.BlockSpec((tr, D_MODEL), lambda i: (i, 0))] + [blk] * 4 + [pl.BlockSpec((tr, GROUP), lambda i: (i, OFF_RG // 2)), gsp],
        out_specs=[blk] * 5 + [gsp],
        out_shape=[jax.ShapeDtypeStruct((s, GROUP), F32)] * 4 + [jax.ShapeDtypeStruct((s, GROUP), BF16), jax.ShapeDtypeStruct((1, D_MODEL), F32)],
        compiler_params=_cparams(("arbitrary",)),
    )(dmixed, oa, ob, oc, od, proj, g.reshape(1, D_MODEL))


def _pack_w_in(w):
    z = lambda n: jnp.zeros((w.shape[0], n), w.dtype)
    misc = jnp.concatenate([w[:, 768:772], z(KR_LANE - N_HEADS), w[:, 1156:1188], z(LANES - KR_LANE - ROPE_DIM)], axis=1)
    return jnp.concatenate([w[:, 0:768], w[:, 772:1028], w[:, 1188:2980], w[:, 1028:1156], misc], axis=1)


def _unpack_dw_in(d):
    m = OFF_MISC * LANES
    return jnp.concatenate([d[:, 0:768], d[:, m:m + N_HEADS], d[:, 768:1024], d[:, OFF_CKV * LANES:m],
                            d[:, m + KR_LANE:m + KR_LANE + ROPE_DIM], d[:, 1024:OFF_CKV * LANES]], axis=1)


def _pack_w_q(w):
    return jnp.pad(w.reshape(Q_RANK, N_HEADS, HEAD + ROPE_DIM), ((0, 0), (0, 0), (0, LANES - HEAD - ROPE_DIM))).reshape(Q_RANK, 4 * LANES)


def _unpack_dw_q(d):
    return d.reshape(Q_RANK, N_HEADS, LANES)[:, :, :HEAD + ROPE_DIM].reshape(Q_RANK, N_HEADS * (HEAD + ROPE_DIM))


def _pack_w_kv(w):
    w4 = w.reshape(KV_RANK, N_HEADS, 2 * HEAD)
    widen = lambda a: jnp.pad(a, ((0, 0), (0, 0), (0, LANES - HEAD))).reshape(KV_RANK, N_HEADS * LANES)
    return widen(w4[:, :, :HEAD]), widen(w4[:, :, HEAD:])


def _unpack_dw_kv(dk, dv):
    narrow = lambda a: a.reshape(KV_RANK, N_HEADS, LANES)[:, :, :HEAD]
    return jnp.concatenate([narrow(dk), narrow(dv)], axis=2).reshape(KV_RANK, 2 * N_HEADS * HEAD)


def _narrow_heads(a):
    return a.reshape(a.shape[0], N_HEADS, LANES)[:, :, :HEAD].reshape(a.shape[0], N_HEADS * HEAD)


def _widen_heads(a):
    return jnp.pad(a.reshape(a.shape[0], N_HEADS, HEAD), ((0, 0), (0, 0), (0, LANES - HEAD))).reshape(a.shape[0], N_HEADS * LANES)


def _layer_fwd(x, lw, tabs, tag, side=None, fox_side=None, late_weights=None, h1=None, next_gain=None):
    cos_m, sin_m, cos_r, sin_r = tabs
    if h1 is None:
        h1 = _norm_fwd(x, lw["g_mix_pre"], name=tag + "pre_norm")
    proj, projb = _matmul(h1, lw["w_in"], name=tag + "in_proj", also_bf16=True)
    bias_row = jnp.pad(lw["b_forget"], (FF_LANE, LANES - N_HEADS - FF_LANE)).reshape(1, LANES)
    cum_col, cum_row = _fox_cum(proj, bias_row)
    oa, lse_a, *fox_carried = _mixer_fwd("fox", projb, OFF_FQ, projb, OFF_FK, projb, OFF_FV, cum_col=cum_col, cum_row=cum_row,
                                         side=fox_side)
    if late_weights is not None:
        lw = {**lw, **late_weights(fox_carried[0])}
    qm, km, vm, cqn, ckvn = _mla_prep(proj, cos_m, sin_m, lw["g_q_lora"], lw["g_kv_lora"], lw["wq"], lw["wk"], lw["wv"])
    ob_wide, lse_b = _mixer_fwd("mla", qm, 0, km, 0, vm, 0)
    ob = _narrow_heads(ob_wide)
    qr, kr = _ret_prep(proj, cos_r, sin_r)
    oc, ret_states = _ret_fwd(qr, kr, projb, OFF_RV)
    od, tot_d, *carried = _mixer_fwd("sb", projb, OFF_SQ, projb, OFF_SK, projb, OFF_SV, side=side)
    mixed = _mix_post(oa, ob, oc, od, proj, lw["g_mix_out"])
    mix = _matmul(mixed, lw["w_out"], name=tag + "out_proj")
    x1, h2 = _norm_fwd(mix, lw["g_mix_post"], name=tag + "mix_post_norm", resid=x, out_dtype=F32, next_gain=lw["g_ffn_pre"])
    u = _matmul(h2, lw["w_ffn_up"], name=tag + "ffn_up", relu2=True, out_dtype=BF16, col_blocks=True)
    f = _matmul(u, lw["w_ffn_down"], name=tag + "ffn_down")
    x2, h_next = None, None
    if next_gain is not None:
        x2, h_next = _norm_fwd(f, lw["g_ffn_post"], name=tag + "ffn_post_norm", resid=x1, out_dtype=F32, next_gain=next_gain)
    saved = dict(x=x, h1=h1, proj=proj, projb=projb, bias_row=bias_row, cum_col=cum_col, cum_row=cum_row, oa=oa, lse_a=lse_a,
                 qm=qm, km=km, vm=vm, cqn=cqn, ckvn=ckvn, ob=ob, ob_wide=ob_wide, lse_b=lse_b, qr=qr, kr=kr, ret_states=ret_states, oc=oc, od=od, tot_d=tot_d, mixed=mixed,
                 mix=mix, x1=x1, h2=h2, u=u, f=f)
    return x2, saved, lw, (carried[0] if carried else None), h_next


def _layer_bwd(dx2, lw, sv, tabs, tag, side=None, ffn_side=None, fox_side=None, post_given=None, then_prev=None):
    cos_m, sin_m, cos_r, sin_r = tabs
    g = {}
    if post_given is None:
        df, g["g_ffn_post"] = _norm_bwd(sv["f"], lw["g_ffn_post"], dx2, name=tag + "ffn_post_norm_bwd", out_dtype=BF16)
    else:
        df, g["g_ffn_post"] = post_given
    du_pre = _matmul(df, lw["w_ffn_down"], name=tag + "ffn_down_dx", tb=True, out_dtype=BF16, relu2_of=sv["u"], side=ffn_side)
    ffn_carried = None
    if ffn_side is not None:
        du_pre, ffn_carried = du_pre
    g["w_ffn_down"] = _matmul(sv["u"], df, name=tag + "ffn_down_dw", ta=True)
    dh2 = _matmul(du_pre, lw["w_ffn_up"], name=tag + "ffn_up_dx", tb=True, col_blocks=True)
    g["w_ffn_up"] = _matmul(sv["h2"], du_pre, name=tag + "ffn_up_dw", ta=True, col_blocks=True)
    dx1, g["g_ffn_pre"], dmix, g["g_mix_post"] = _norm_bwd(sv["x1"], lw["g_ffn_pre"], dh2, name=tag + "ffn_pre_norm_bwd", add=dx2,
                                                           then=(sv["mix"], lw["g_mix_post"]))
    dmixed = _matmul(dmix, lw["w_out"], name=tag + "out_proj_dx", tb=True)
    g["w_out"] = _matmul(sv["mixed"], dmix, name=tag + "out_proj_dw", ta=True)
    proj, projb = sv["proj"], sv["projb"]
    doa, dob, doc, dod, drg, g["g_mix_out"] = _mix_post_bwd(dmixed, sv["oa"], sv["ob"], sv["oc"], sv["od"], proj, lw["g_mix_out"])
    dfq, dfk, dfv, dck, drs, *fox_carried = _mixer_bwd(
        "fox", projb, OFF_FQ, projb, OFF_FK, projb, OFF_FV, sv["oa"], doa, stat=sv["lse_a"], cum_col=sv["cum_col"],
        cum_row=sv["cum_row"], side=None if fox_side is None else fox_side(g))
    dqm, dkm, dvm = _mixer_bwd("mla", sv["qm"], 0, sv["km"], 0, sv["vm"], 0, sv["ob_wide"], _widen_heads(dob), stat=sv["lse_b"])
    dcq, dckv, dkr, dwq, dwk, dwv, g["g_q_lora"], g["g_kv_lora"] = _mla_prep_bwd(
        dqm, dkm, dvm, proj, sv["cqn"], sv["ckvn"], cos_m, sin_m, lw["g_q_lora"], lw["g_kv_lora"], lw["wq"], lw["wk"], lw["wv"])
    dqr, dkr_ret, drv = _ret_bwd(sv["qr"], sv["kr"], projb, OFF_RV, sv["ret_states"], doc)
    drq, drk = _ret_prep_bwd(dqr, dkr_ret, cos_r, sin_r)
    if callable(side):
        side = side(g, ffn_carried, fox_carried[0] if fox_carried else None)
    dsq, dsk, dsv, *carried = _mixer_bwd("sb", projb, OFF_SQ, projb, OFF_SK, projb, OFF_SV, sv["od"], dod, stat=sv["tot_d"], side=side)
    dmisc, db_row = _fox_gate_bwd(dck, drs, proj, sv["bias_row"], dkr)
    b = lambda a: a.astype(BF16)
    dproj = jnp.concatenate([b(dfq), b(dfk), b(dfv), dcq, drq, drk, b(drv), drg, b(dsq), b(dsk), b(dsv), dckv, dmisc], axis=1)
    dh1 = _matmul(dproj, lw["w_in"], name=tag + "in_proj_dx", tb=True)
    g["w_in"] = _matmul(sv["h1"], dproj, name=tag + "in_proj_dw", ta=True)
    dx, g["g_mix_pre"], *prev_post = _norm_bwd(sv["x"], lw["g_mix_pre"], dh1, name=tag + "pre_norm_bwd", add=dx1, then=then_prev)
    g["b_forget"] = db_row[0, FF_LANE:FF_LANE + N_HEADS]
    g["wq"], g["wk"], g["wv"] = dwq, dwk, dwv
    return dx, g, (carried[0] if carried else None), (tuple(prev_post) if prev_post else None)


def _local_step(x, positions, layers, target):
    s = x.shape[0]
    tabs = _rope_tables(positions.reshape(s, 1))
    saved, h1 = [], None
    for li, lw in enumerate(layers):
        nxt = layers[li + 1]["g_mix_pre"] if li + 1 < len(layers) else None
        x, sv, _, _, h1 = _layer_fwd(x, lw, tabs, "l%d_" % li, h1=h1, next_gain=nxt)
        saved.append(sv)
    loss_row, dx, df, dg = _loss_head(saved[-1]["f"], layers[-1]["g_ffn_post"], saved[-1]["x1"], target)
    grads, post = [None] * len(layers), (df, dg)
    for li in reversed(range(len(layers))):
        prev = (saved[li - 1]["f"], layers[li - 1]["g_ffn_post"]) if li > 0 else None
        dx, grads[li], _, post = _layer_bwd(dx, layers[li], saved[li], tabs, "l%d_" % li, post_given=post, then_prev=prev)
    return loss_row[0, 0], dx, grads


def _adamw(w, g, m, v, *, name):
    d, r, c = w.shape
    tr = 256 if r % 256 == 0 else r
    blk = pl.BlockSpec((None, tr, c), lambda l, i: (l, i, 0))
    c1 = 1.0 - ADAM_B1 ** ADAM_STEP
    c2 = 1.0 - ADAM_B2 ** ADAM_STEP

    def body(w_ref, g_ref, m_ref, v_ref, d_ref, mo_ref, vo_ref):
        gv = g_ref[...]
        mn = ADAM_B1 * m_ref[...] + (1.0 - ADAM_B1) * gv
        vn = ADAM_B2 * v_ref[...] + (1.0 - ADAM_B2) * jnp.square(gv)
        mo_ref[...] = mn
        vo_ref[...] = vn
        d_ref[...] = -ADAM_LR * ((mn / c1) / (jnp.sqrt(vn / c2) + ADAM_EPS) + ADAM_WD * w_ref[...])

    return pl.pallas_call(
        body, name=name, grid=(d, r // tr), in_specs=[blk] * 4, out_specs=[blk] * 3,
        out_shape=[jax.ShapeDtypeStruct((d, r, c), F32)] * 3, compiler_params=_cparams(("parallel", "parallel")),
    )(w, g, m, v)


BIG = ("w_in", "w_q_up", "w_kv_up", "w_out", "w_ffn_up", "w_ffn_down")
SMALL = ("g_mix_pre", "b_forget", "g_q_lora", "g_kv_lora", "g_mix_out", "g_mix_post", "g_ffn_pre", "g_ffn_post")
N_CHIPS = 4
ANY = pl.BlockSpec(memory_space=pl.ANY)


def _mesh_pos():
    return lax.axis_index("x"), lax.axis_index("y"), lax.axis_index("c")


def _other_chips(x, y):
    return [(1 - x, y), (x, 1 - y), (1 - x, 1 - y)]


def _rows_half(ref, half):
    h = ref.shape[-2] // 2
    return ref.at[(slice(None),) * (len(ref.shape) - 2) + (pl.ds(half * h, h), slice(None))]


def _remote(src, dst, send_sem, recv_sem, device):
    return pltpu.make_async_remote_copy(src_ref=src, dst_ref=dst, send_sem=send_sem, recv_sem=recv_sem, device_id=device,
                                        device_id_type=MESH)


def _comm_call(body, name, args, out_shape, n_sems):
    return pl.pallas_call(
        body, name=name, in_specs=[ANY] * len(args), out_specs=[ANY] * len(out_shape), out_shape=out_shape,
        scratch_shapes=[pltpu.SemaphoreType.DMA((n_sems,)), pltpu.SemaphoreType.DMA((n_sems,))],
        compiler_params=pltpu.CompilerParams(has_side_effects=True),
    )(*args)


def _run_side_job(side, name):
    si = len(side.inputs)

    def body(*refs):
        args = (refs[:si], refs[si:-2], refs[-2], refs[-1])
        sends = side.sends(*args)
        for cp in sends:
            cp.start()
        for cp in side.recvs(*args):
            cp.wait_recv()
        for cp in sends:
            cp.wait_send()

    return _comm_call(body, name, side.inputs, side.out_shape, side.n_sems)


def _gather_job(shards):
    n = len(shards)

    def copies(own_block, ins, outs, send_sems, recv_sems):
        x, y, c = _mesh_pos()
        return [_remote(_rows_half(ins[t], c), _rows_half(outs[t].at[2 * x + y if own_block else 2 * px + py], c),
                        send_sems.at[3 * t + j], recv_sems.at[3 * t + j], (px, py, c))
                for t in range(n) for j, (px, py) in enumerate(_other_chips(x, y))]

    return _SideJob(shards, [jax.ShapeDtypeStruct((N_CHIPS,) + a.shape, a.dtype) for a in shards], 3 * n,
                    functools.partial(copies, True), functools.partial(copies, False))


def _forward_halves(gathered):
    n = len(gathered)

    def body(*refs):
        bufs, send_sems, recv_sems = refs[n:2 * n], refs[-2], refs[-1]
        x, y, c = _mesh_pos()

        def d2d(t, j, block, half):
            region = _rows_half(bufs[t].at[block], half)
            return _remote(region, region, send_sems.at[3 * t + j], recv_sems.at[3 * t + j], (x, y, 1 - c))

        peers = list(enumerate(_other_chips(x, y)))
        sends = [d2d(t, j, 2 * px + py, c) for t in range(n) for j, (px, py) in peers]
        for cp in sends:
            cp.start()
        for t in range(n):
            for j, (px, py) in peers:
                d2d(t, j, 2 * px + py, 1 - c).wait_recv()
        for cp in sends:
            cp.wait_send()

    return pl.pallas_call(
        body, name="gather_forward", in_specs=[ANY] * n, out_specs=[ANY] * n,
        out_shape=[jax.ShapeDtypeStruct(g.shape, g.dtype) for g in gathered], input_output_aliases={t: t for t in range(n)},
        scratch_shapes=[pltpu.SemaphoreType.DMA((3 * n,)), pltpu.SemaphoreType.DMA((3 * n,))],
        compiler_params=pltpu.CompilerParams(has_side_effects=True),
    )(*gathered)


def _exchange_halves_job(gs):
    n = len(gs)

    def copies(ins, outs, send_sems, recv_sems):
        x, y, c = _mesh_pos()
        return [_remote(_rows_half(ins[t], 1 - c), outs[t], send_sems.at[t], recv_sems.at[t], (x, y, 1 - c)) for t in range(n)]

    out_shape = [jax.ShapeDtypeStruct(g.shape[:2] + (g.shape[2] // 2, g.shape[3]), g.dtype) for g in gs]
    return _SideJob(gs, out_shape, n, copies, copies)


def _pair_add(g, r, c_idx, *, name):
    nb, d, rows, cols = g.shape
    h = rows // 2
    tr = min(h, 512)
    nt = h // tr

    def body(c_ref, g_ref, r_ref, p_ref, pb_ref):
        s = g_ref[...] + r_ref[...]
        p_ref[...] = s
        pb_ref[...] = s.astype(BF16)

    blk = pl.BlockSpec((1, 1, tr, cols), lambda k, l, i, c_ref: (k, l, i, 0))
    return pl.pallas_call(
        body, name=name,
        grid_spec=pltpu.PrefetchScalarGridSpec(
            num_scalar_prefetch=1, grid=(nb, d, nt),
            in_specs=[pl.BlockSpec((1, 1, tr, cols), lambda k, l, i, c_ref: (k, l, c_ref[0] * nt + i, 0)), blk],
            out_specs=[blk, blk]),
        out_shape=[jax.ShapeDtypeStruct((nb, d, h, cols), F32), jax.ShapeDtypeStruct((nb, d, h, cols), BF16)],
        compiler_params=_cparams(("parallel", "parallel", "parallel")),
    )(c_idx, g, r)


def _exchange_chips_job(pbs):
    n = len(pbs)

    def copies(ins, outs, send_sems, recv_sems):
        x, y, c = _mesh_pos()
        return [_remote(ins[t].at[2 * px + py], outs[t].at[j], send_sems.at[3 * t + j], recv_sems.at[3 * t + j], (px, py, c))
                for t in range(n) for j, (px, py) in enumerate(_other_chips(x, y))]

    return _SideJob(pbs, [jax.ShapeDtypeStruct((3,) + p.shape[1:], p.dtype) for p in pbs], 3 * n, copies, copies)


def _chip_add(p, r, k_idx, *, name):
    _, d, h, cols = p.shape
    tr = min(h, 512)
    nt = h // tr

    def body(k_ref, p_ref, r_ref, o_ref):
        o_ref[0] = ((p_ref[0, 0] + r_ref[0, 0].astype(F32)) + r_ref[1, 0].astype(F32)) + r_ref[2, 0].astype(F32)

    return pl.pallas_call(
        body, name=name,
        grid_spec=pltpu.PrefetchScalarGridSpec(
            num_scalar_prefetch=1, grid=(d, nt),
            in_specs=[pl.BlockSpec((1, 1, tr, cols), lambda l, i, k_ref: (k_ref[0], l, i, 0)),
                      pl.BlockSpec((3, 1, tr, cols), lambda l, i, k_ref: (0, l, i, 0))],
            out_specs=pl.BlockSpec((1, tr, cols), lambda l, i, k_ref: (l, i, 0))),
        out_shape=jax.ShapeDtypeStruct((d, h, cols), F32), compiler_params=_cparams(("parallel", "parallel")),
    )(k_idx, p, r)


def _share_halves(qs):
    n = len(qs)

    def body(*refs):
        ins, outs, send_sems, recv_sems = refs[:n], refs[n:2 * n], refs[2 * n], refs[2 * n + 1]
        x, y, c = _mesh_pos()
        cps = [_remote(ins[t], outs[t], send_sems.at[t], recv_sems.at[t], (x, y, 1 - c)) for t in range(n)]
        for cp in cps:
            cp.start()
        for cp in cps:
            cp.wait_recv()
        for cp in cps:
            cp.wait_send()

    return _comm_call(body, "grad_pair_share", qs, [jax.ShapeDtypeStruct(q.shape, q.dtype) for q in qs], n)


def _all_reduce_small(v):
    r, cols = v.shape
    n_dev = 8

    def body(v_ref, o_ref, buf, send_sems, recv_sems):
        x, y, c = _mesh_pos()
        me = 4 * x + 2 * y + c
        buf[me] = v_ref[...]

        def peer(j):
            return (1 - x if j & 4 else x, 1 - y if j & 2 else y, 1 - c if j & 1 else c)

        def copy(j, slot):
            return pltpu.make_async_remote_copy(src_ref=v_ref, dst_ref=buf.at[slot], send_sem=send_sems.at[j - 1],
                                                recv_sem=recv_sems.at[j - 1], device_id=peer(j), device_id_type=MESH)

        sends = [copy(j, me) for j in range(1, n_dev)]
        for cp in sends:
            cp.start()
        for j in range(1, n_dev):
            px, py, pc = peer(j)
            copy(j, 4 * px + 2 * py + pc).wait_recv()
        for cp in sends:
            cp.wait_send()
        acc = buf[0]
        for d in range(1, n_dev):
            acc = acc + buf[d]
        o_ref[...] = acc

    vm = pl.BlockSpec(memory_space=pltpu.VMEM)
    return pl.pallas_call(
        body, name="small_all_reduce", in_specs=[vm], out_specs=vm, out_shape=jax.ShapeDtypeStruct((r, cols), F32),
        scratch_shapes=[pltpu.VMEM((n_dev, r, cols), F32), pltpu.SemaphoreType.DMA((n_dev - 1,)), pltpu.SemaphoreType.DMA((n_dev - 1,))],
        compiler_params=pltpu.CompilerParams(has_side_effects=True),
    )(v)


_COL_SHARDED = ("w_in", "w_q_up", "w_kv_up", "w_ffn_up")


def _shard_cols(blocks, a, b):
    c = blocks[0].shape[-1]
    out = []
    while a < b:
        k = a // c
        hi = min(b, (k + 1) * c)
        out.append(blocks[k][:, a - k * c:hi - k * c])
        a = hi
    return out


def _pack_w_in_shards(blocks):
    z = lambda n: [jnp.zeros((blocks[0].shape[0], n), blocks[0].dtype)]
    cols = lambda a, b: _shard_cols(blocks, a, b)
    return jnp.concatenate(cols(0, 768) + cols(772, 1028) + cols(1188, 2980) + cols(1028, 1156) + cols(768, 772)
                           + z(KR_LANE - N_HEADS) + cols(1156, 1188) + z(LANES - KR_LANE - ROPE_DIM), axis=1)


def _whole_layer(name, blocks):
    if name in _COL_SHARDED:
        return jnp.concatenate([blocks[k] for k in range(N_CHIPS)], axis=1)
    return blocks.reshape(N_CHIPS * blocks.shape[1], blocks.shape[2])


def _split_layer(name, whole):
    if name in _COL_SHARDED:
        c = whole.shape[1] // N_CHIPS
        return jnp.stack([whole[:, k * c:(k + 1) * c] for k in range(N_CHIPS)])
    return whole.reshape(N_CHIPS, whole.shape[0] // N_CHIPS, whole.shape[1])


def _small_to_rows(d):
    v = jnp.concatenate([d[k].astype(F32).reshape(-1) for k in SMALL])
    rows = -(-v.shape[0] // (8 * LANES)) * 8
    return jnp.pad(v, (0, rows * LANES - v.shape[0])).reshape(rows, LANES)


def _small_from_rows(rows, shapes):
    v = rows.reshape(-1)
    out, o = {}, 0
    for k in SMALL:
        sz = int(np.prod(shapes[k]))
        out[k] = v[o:o + sz].reshape(shapes[k])
        o += sz
    return out


_ARG_NAMES = ("x", "positions", "g_mix_pre", "w_in", "b_forget", "g_q_lora", "w_q_up", "g_kv_lora", "w_kv_up", "g_mix_out", "w_out",
              "g_mix_post", "g_ffn_pre", "w_ffn_up", "w_ffn_down", "g_ffn_post")
_WEIGHTS = _ARG_NAMES[2:]


def kernel(x, positions, g_mix_pre, w_in, b_forget, g_q_lora, w_q_up, g_kv_lora, w_kv_up, g_mix_out, w_out, g_mix_post, g_ffn_pre, w_ffn_up, w_ffn_down, g_ffn_post, loss_target, m_g_mix_pre, m_w_in, m_b_forget, m_g_q_lora, m_w_q_up, m_g_kv_lora, m_w_kv_up, m_g_mix_out, m_w_out, m_g_mix_post, m_g_ffn_pre, m_w_ffn_up, m_w_ffn_down, m_g_ffn_post, v_g_mix_pre, v_w_in, v_b_forget, v_g_q_lora, v_w_q_up, v_g_kv_lora, v_w_kv_up, v_g_mix_out, v_w_out, v_g_mix_post, v_g_ffn_pre, v_w_ffn_up, v_w_ffn_down, v_g_ffn_post):
    w = dict(g_mix_pre=g_mix_pre, w_in=w_in, b_forget=b_forget, g_q_lora=g_q_lora, w_q_up=w_q_up, g_kv_lora=g_kv_lora, w_kv_up=w_kv_up,
             g_mix_out=g_mix_out, w_out=w_out, g_mix_post=g_mix_post, g_ffn_pre=g_ffn_pre, w_ffn_up=w_ffn_up, w_ffn_down=w_ffn_down,
             g_ffn_post=g_ffn_post)
    m = dict(g_mix_pre=m_g_mix_pre, w_in=m_w_in, b_forget=m_b_forget, g_q_lora=m_g_q_lora, w_q_up=m_w_q_up, g_kv_lora=m_g_kv_lora,
             w_kv_up=m_w_kv_up, g_mix_out=m_g_mix_out, w_out=m_w_out, g_mix_post=m_g_mix_post, g_ffn_pre=m_g_ffn_pre,
             w_ffn_up=m_w_ffn_up, w_ffn_down=m_w_ffn_down, g_ffn_post=m_g_ffn_post)
    v = dict(g_mix_pre=v_g_mix_pre, w_in=v_w_in, b_forget=v_b_forget, g_q_lora=v_g_q_lora, w_q_up=v_w_q_up, g_kv_lora=v_g_kv_lora,
             w_kv_up=v_w_kv_up, g_mix_out=v_g_mix_out, w_out=v_w_out, g_mix_post=v_g_mix_post, g_ffn_pre=v_g_ffn_pre,
             w_ffn_up=v_w_ffn_up, w_ffn_down=v_w_ffn_down, g_ffn_post=v_g_ffn_post)
    small_shapes = {k: w[k].shape for k in SMALL}
    c_idx = lax.axis_index("c").astype(jnp.int32).reshape(1)
    k_idx = (2 * lax.axis_index("x") + lax.axis_index("y")).astype(jnp.int32).reshape(1)
    first_core = lax.axis_index("c") == 0

    mine = 2 * lax.axis_index("x") + lax.axis_index("y")
    shards_b = [{k: w[k][l:l + 1].astype(BF16) for k in BIG} for l in range(DEPTH)]
    gains = [dict(g_mix_pre=g_mix_pre[l], b_forget=b_forget[l], g_q_lora=g_q_lora[l], g_kv_lora=g_kv_lora[l], g_mix_out=g_mix_out[l],
                  g_mix_post=g_mix_post[l], g_ffn_pre=g_ffn_pre[l], g_ffn_post=g_ffn_post[l]) for l in range(DEPTH)]
    FIRST, LATER = ("w_in", "w_q_up", "w_kv_up"), ("w_out", "w_ffn_up", "w_ffn_down")
    EARLY_GRADS, LATE_GRADS = ("w_ffn_down", "w_ffn_up", "w_out"), ("w_in", "w_q_up", "w_kv_up")

    def gather_job(l, names):
        return _gather_job([shards_b[l][k] for k in names])

    def weights_of(l, names, gathered):
        four = {k: lax.dynamic_update_slice(g, shards_b[l][k][None], (mine, 0, 0, 0))[:, 0]
                for k, g in zip(names, _forward_halves(gathered))}
        out = {}
        for k in names:
            if k == "w_in":
                out["w_in"] = _pack_w_in_shards(four[k])
            elif k == "w_q_up":
                out["wq"] = _pack_w_q(_whole_layer(k, four[k]))
            elif k == "w_kv_up":
                out["wk"], out["wv"] = _pack_w_kv(_whole_layer(k, four[k]))
            elif k == "w_ffn_up":
                out[k] = four[k]
            else:
                out[k] = _whole_layer(k, four[k])
        return out

    def grad_blocks(names, g):
        whole = dict(w_in=lambda: _unpack_dw_in(g["w_in"]), w_q_up=lambda: _unpack_dw_q(g["wq"]),
                     w_kv_up=lambda: _unpack_dw_kv(g["wk"], g["wv"]), w_out=lambda: g["w_out"], w_ffn_down=lambda: g["w_ffn_down"])
        return [(g[k] if k == "w_ffn_up" else _split_layer(k, whole[k]()))[:, None] for k in names]

    def pair_sums(names, blocks, theirs):
        return [_pair_add(b, r, c_idx, name="grad_pair_add_" + k) for k, b, r in zip(names, blocks, theirs)]

    def exchange_job(*pairs):
        return _exchange_chips_job([pb for pair in pairs for (_, pb) in pair])

    def finish_grads(names, pair, partial):
        half = [_chip_add(p, r, k_idx, name="grad_chip_add_" + k) for k, (p, _), r in zip(names, pair, partial)]
        return {k: jnp.where(first_core, jnp.concatenate([q, s], axis=1), jnp.concatenate([s, q], axis=1))
                for k, q, s in zip(names, half, _share_halves(half))}

    seq = x.shape[1]
    tabs = _rope_tables(positions[0].reshape(seq, 1))
    first0 = weights_of(0, FIRST, _run_side_job(gather_job(0, FIRST), "gather_weights_l0"))
    x1, saved0, lw0, gathered1, h1 = _layer_fwd(x[0], {**gains[0], **first0}, tabs, "l0_", fox_side=gather_job(0, LATER),
                                                late_weights=lambda got: weights_of(0, LATER, got), side=gather_job(1, BIG),
                                                next_gain=gains[1]["g_mix_pre"])
    lw1 = {**gains[1], **weights_of(1, BIG, gathered1)}
    _, saved1, _, _, _ = _layer_fwd(x1, lw1, tabs, "l1_", h1=h1)
    loss_row, dx, df1, dg1 = _loss_head(saved1["f"], lw1["g_ffn_post"], saved1["x1"], loss_target[0])
    loss = lax.psum(loss_row[0, 0], ("x", "y", "c"))
    dx, grads1, _, post0 = _layer_bwd(dx, lw1, saved1, tabs, "l1_", post_given=(df1, dg1),
                                      then_prev=(saved0["f"], lw0["g_ffn_post"]))
    blocks1 = grad_blocks(BIG, grads1)
    early_blocks0, pair1, early0 = [], [], []

    def beside_l0_fox_backward(g):
        early_blocks0.extend(grad_blocks(EARLY_GRADS, g))
        return _exchange_halves_job(early_blocks0)

    def beside_l0_sb_backward(g, theirs1, theirs_early0):
        pair1.extend(pair_sums(BIG, blocks1, theirs1))
        early0.extend(pair_sums(EARLY_GRADS, early_blocks0, theirs_early0))
        return exchange_job(pair1, early0)

    dx, grads0, partial, _ = _layer_bwd(dx, lw0, saved0, tabs, "l0_", ffn_side=_exchange_halves_job(blocks1),
                                        fox_side=beside_l0_fox_backward, side=beside_l0_sb_backward, post_given=post0)
    big1 = finish_grads(BIG, pair1, partial[:len(BIG)])
    big0 = finish_grads(EARLY_GRADS, early0, partial[len(BIG):])
    late_blocks0 = grad_blocks(LATE_GRADS, grads0)
    late0 = pair_sums(LATE_GRADS, late_blocks0, _run_side_job(_exchange_halves_job(late_blocks0), "grad_pair_exchange_l0"))
    big0.update(finish_grads(LATE_GRADS, late0, _run_side_job(exchange_job(late0), "grad_chip_exchange_l0")))
    g_big = {k: jnp.concatenate([big0[k], big1[k]], axis=0) for k in BIG}
    grads = [grads0, grads1]

    g_small_local = {k: jnp.stack([grads[l][k].reshape(small_shapes[k][1:]) for l in range(DEPTH)]) for k in SMALL}
    g_small = _small_from_rows(_all_reduce_small(_small_to_rows(g_small_local)), small_shapes)

    g_all = {**g_big, **g_small}
    delta, new_m, new_v = {}, {}, {}
    for k in BIG:
        delta[k], new_m[k], new_v[k] = _adamw(w[k], g_all[k], m[k], v[k], name="adamw_" + k)
    ds, ms, vs = _adamw(*[_small_to_rows(t)[None] for t in (w, g_small, m, v)], name="adamw_small")
    delta.update(_small_from_rows(ds, small_shapes))
    new_m.update(_small_from_rows(ms, small_shapes))
    new_v.update(_small_from_rows(vs, small_shapes))

    grad_x = dx.reshape(x.shape)
    return (loss, grad_x, *[g_all[k] for k in _WEIGHTS], *[delta[k] for k in _WEIGHTS], *[new_m[k] for k in _WEIGHTS],
            *[new_v[k] for k in _WEIGHTS])
```

```python
import functools
import math

import numpy as np
import jax
import jax.numpy as jnp
from jax import lax
from jax.experimental import pallas as pl
from jax.experimental.pallas import tpu as pltpu

F32 = jnp.float32
BF16 = jnp.bfloat16
MESH = pl.DeviceIdType.MESH

D_MODEL = 1024
DEPTH = 2
CHUNK = 64
GROUP = 256
HEAD = 64
N_HEADS = 4
Q_RANK = 256
KV_RANK = 128
ROPE_DIM = 32
D_FF = 4096
D_IN = 2980
D_INP = 3072
ROPE_BASE = 10000.0
EPS = 1e-6
LANES = 128
TQ = 128
GATE_ROWS = 512
CONTRACT_TILE = 4096
NEG = -1e30

ADAM_LR, ADAM_B1, ADAM_B2, ADAM_EPS, ADAM_WD, ADAM_STEP = 0.001, 0.9, 0.999, 1e-08, 0.01, 10

OFF_FQ, OFF_FK, OFF_FV, OFF_CQ = 0, 2, 4, 6
OFF_RQ, OFF_RK, OFF_RV, OFF_RG = 8, 10, 12, 14
OFF_SQ, OFF_SK, OFF_SV = 16, 18, 20
OFF_CKV, OFF_MISC = 22, 23
FF_LANE, KR_LANE = 0, 64

VMEM_LIMIT = 56 * 1024 * 1024


def _tile(dim, pref):
    return pref if dim % pref == 0 else dim


def _cparams(sem, vmem=None):
    return pltpu.CompilerParams(dimension_semantics=sem, vmem_limit_bytes=vmem or VMEM_LIMIT)


def _dot(a, b):
    return jnp.dot(a, b, preferred_element_type=F32)


def _dot_nt(a, b):
    return lax.dot_general(a, b, (((1,), (1,)), ((), ())), preferred_element_type=F32)


def _dot_tn(a, b):
    return lax.dot_general(a, b, (((0,), (0,)), ((), ())), preferred_element_type=F32)


def _dot_exact(a, b):
    return jnp.dot(a, b, precision=lax.Precision.HIGHEST, preferred_element_type=F32)


def _matmul(a, b, *, name, ta=False, tb=False, out_dtype=F32, tm=1024, tn=1024, tk=CONTRACT_TILE,
            relu2=False, relu2_of=None, also_bf16=False, side=None, col_blocks=False):
    if ta:
        kdim, m = a.shape
    else:
        m, kdim = a.shape
    if col_blocks and not ta:
        n = b.shape[1] if tb else b.shape[0] * b.shape[2]
        if tb:
            kdim = b.shape[0] * b.shape[2]
    else:
        n = b.shape[0] if tb else b.shape[1]
    tm, tn, tk = _tile(m, tm), _tile(n, tn), _tile(kdim, tk)
    nk = kdim // tk
    a_spec = pl.BlockSpec((tk, tm), lambda i, j, k: (k, i)) if ta else pl.BlockSpec((tm, tk), lambda i, j, k: (i, k))
    b_spec = pl.BlockSpec((tn, tk), lambda i, j, k: (j, k)) if tb else pl.BlockSpec((tk, tn), lambda i, j, k: (k, j))
    o_spec = pl.BlockSpec((tm, tn), lambda i, j, k: (i, j))
    if col_blocks and ta:
        o_spec = pl.BlockSpec((None, tm, tn), lambda i, j, k: (j, i, 0))
    elif col_blocks and tb:
        assert tk == kdim
        b_spec = pl.BlockSpec((b.shape[0], tn, b.shape[2]), lambda i, j, k: (0, j, 0))
    elif col_blocks:
        assert b.shape[2] == tn
        b_spec = pl.BlockSpec((None, tk, tn), lambda i, j, k: (j, k, 0))
    two = also_bf16

    def body(*refs):
        refs = list(refs)
        a_ref, b_ref = refs[0], refs[1]
        e_ref = refs[2] if relu2_of is not None else None
        pos = 3 if relu2_of is not None else 2
        o_ref = refs[pos]
        o2_ref = refs[pos + 1] if two else None
        acc_ref = refs[-1]
        k = pl.program_id(2)
        av = a_ref[...].astype(BF16)
        if col_blocks and tb:
            bv = jnp.concatenate([b_ref[q] for q in range(b.shape[0])], axis=1).astype(BF16)
        else:
            bv = b_ref[...].astype(BF16)
        if ta:
            part = _dot_tn(av, bv)
        elif tb:
            part = _dot_nt(av, bv)
        else:
            part = _dot(av, bv)

        @pl.when(k == 0)
        def _():
            acc_ref[...] = part

        @pl.when(k > 0)
        def _():
            acc_ref[...] += part

        @pl.when(k == nk - 1)
        def _():
            r = acc_ref[...]
            if relu2_of is not None:
                r = r * (2.0 * jnp.sqrt(e_ref[...].astype(F32)))
            if relu2:
                r = jnp.square(jnp.maximum(r, 0.0))
            o_ref[...] = r.astype(o_ref.dtype)
            if also_bf16:
                o2_ref[...] = r.astype(BF16)

    in_specs = [a_spec, b_spec]
    args = [a, b]
    if relu2_of is not None:
        in_specs.append(o_spec)
        args.append(relu2_of)
    out_shape = [jax.ShapeDtypeStruct((n // tn, m, tn) if (col_blocks and ta) else (m, n), out_dtype)]
    out_specs = [o_spec]
    if two:
        out_shape.append(jax.ShapeDtypeStruct((m, n), BF16))
        out_specs.append(o_spec)
    grid = (m // tm, n // tn, nk)
    side_in, side_out, side_scratch = _side_specs(side)
    res = pl.pallas_call(
        _carry_side_job(body, len(args), len(out_shape), side, grid), name=name, grid=grid,
        in_specs=in_specs + side_in, out_specs=out_specs + side_out,
        out_shape=out_shape + ([] if side is None else side.out_shape),
        scratch_shapes=[pltpu.VMEM((tm, tn), F32)] + side_scratch,
        compiler_params=_cparams(("parallel", "parallel", "arbitrary") if side is None else ("arbitrary",) * 3),
    )(*args, *([] if side is None else side.inputs))
    main = res[:len(out_shape)]
    main = main if two else main[0]
    return main if side is None else (main, res[len(out_shape):])


def _rms(x, g):
    r = lax.rsqrt(jnp.mean(x * x, axis=-1, keepdims=True) + EPS)
    return x * r * g


def _rms_bwd(x, g, dy):
    r = lax.rsqrt(jnp.mean(x * x, axis=-1, keepdims=True) + EPS)
    xh = x * r
    gdy = dy * g
    dx = r * (gdy - xh * jnp.mean(xh * gdy, axis=-1, keepdims=True))
    return dx, xh * dy


def _norm_fwd(x, g, *, name, resid=None, out_dtype=BF16, next_gain=None):
    s, d = x.shape
    tr = _tile(s, 256)
    row = pl.BlockSpec((tr, d), lambda i: (i, 0))
    gsp = pl.BlockSpec((1, d), lambda i: (0, 0))

    def body(*refs):
        refs = list(refs)
        x_ref, g_ref = refs[:2]
        y = _rms(x_ref[...], g_ref[...])
        pos = 2
        if resid is not None:
            y = refs[pos][...] + y
            pos += 1
        if next_gain is None:
            refs[pos][...] = y.astype(refs[pos].dtype)
        else:
            refs[pos + 1][...] = y.astype(refs[pos + 1].dtype)
            refs[pos + 2][...] = _rms(y, refs[pos][...]).astype(BF16)

    args = [x, g.reshape(1, d)] + ([] if resid is None else [resid]) + ([] if next_gain is None else [next_gain.reshape(1, d)])
    in_specs = [row, gsp] + ([] if resid is None else [row]) + ([] if next_gain is None else [gsp])
    first = jax.ShapeDtypeStruct((s, d), out_dtype)
    if next_gain is None:
        out_specs, out_shape = row, first
    else:
        out_specs, out_shape = [row, row], [first, jax.ShapeDtypeStruct((s, d), BF16)]
    return pl.pallas_call(
        body, name=name, grid=(s // tr,), in_specs=in_specs, out_specs=out_specs, out_shape=out_shape,
        compiler_params=_cparams(("parallel",)),
    )(*args)


def _norm_bwd(x, g, dy, *, name, add=None, out_dtype=F32, then=None):
    s, d = x.shape
    tr = _tile(s, 256)
    row = pl.BlockSpec((tr, d), lambda i: (i, 0))
    gsp = pl.BlockSpec((1, d), lambda i: (0, 0))
    n_in = 3 + (add is not None) + (2 if then is not None else 0)

    def body(*refs):
        ins, outs = refs[:n_in], refs[n_in:]
        x_ref, g_ref, dy_ref = ins[:3]
        dx, gterm = _rms_bwd(x_ref[...], g_ref[...], dy_ref[...].astype(F32))
        if add is not None:
            dx = dx + ins[3][...]
        outs[0][...] = dx.astype(outs[0].dtype)
        terms = [(outs[1], gterm)]
        if then is not None:
            dx2, gterm2 = _rms_bwd(ins[-2][...], ins[-1][...], dx)
            outs[2][...] = dx2.astype(BF16)
            terms.append((outs[3], gterm2))

        @pl.when(pl.program_id(0) == 0)
        def _():
            for dg_ref, _ in terms:
                dg_ref[...] = jnp.zeros_like(dg_ref)

        for dg_ref, term in terms:
            dg_ref[...] += jnp.sum(term, axis=0, keepdims=True)

    args = [x, g.reshape(1, d), dy] + ([] if add is None else [add]) + ([] if then is None else [then[0], then[1].reshape(1, d)])
    in_specs = [row, gsp, row] + ([] if add is None else [row]) + ([] if then is None else [row, gsp])
    out_specs = [row, gsp] + ([] if then is None else [row, gsp])
    out_shape = [jax.ShapeDtypeStruct((s, d), out_dtype), jax.ShapeDtypeStruct((1, d), F32)]
    if then is not None:
        out_shape += [jax.ShapeDtypeStruct((s, d), BF16), jax.ShapeDtypeStruct((1, d), F32)]
    return pl.pallas_call(
        body, name=name, grid=(s // tr,), in_specs=in_specs, out_specs=out_specs, out_shape=out_shape,
        compiler_params=_cparams(("arbitrary",)),
    )(*args)


def _loss_head(f, g, resid, target):
    s, d = f.shape
    tr = _tile(s, 256)
    row = pl.BlockSpec((tr, d), lambda i: (i, 0))
    gsp = pl.BlockSpec((1, d), lambda i: (0, 0))
    lsp = pl.BlockSpec((1, LANES), lambda i: (0, 0))

    def body(f_ref, g_ref, r_ref, t_ref, l_ref, dy_ref, df_ref, dg_ref):
        fv, gv = f_ref[...], g_ref[...]
        e = (r_ref[...] + _rms(fv, gv)) - t_ref[...]
        dy = e * (1.0 / d)
        dy_ref[...] = dy
        df, gterm = _rms_bwd(fv, gv, dy)
        df_ref[...] = df.astype(BF16)

        @pl.when(pl.program_id(0) == 0)
        def _():
            l_ref[...] = jnp.zeros_like(l_ref)
            dg_ref[...] = jnp.zeros_like(dg_ref)

        part = 0.5 * jnp.sum(jnp.mean(e * e, axis=-1, keepdims=True), axis=0, keepdims=True)
        l_ref[...] += jnp.broadcast_to(part, (1, LANES))
        dg_ref[...] += jnp.sum(gterm, axis=0, keepdims=True)

    return pl.pallas_call(
        body, name="loss_head", grid=(s // tr,), in_specs=[row, gsp, row, row], out_specs=[lsp, row, row, gsp],
        out_shape=[jax.ShapeDtypeStruct((1, LANES), F32), jax.ShapeDtypeStruct((s, d), F32), jax.ShapeDtypeStruct((s, d), BF16),
                   jax.ShapeDtypeStruct((1, d), F32)],
        compiler_params=_cparams(("arbitrary",)),
    )(f, g.reshape(1, d), resid, target)


def _rope_tables(pos_col):
    s = pos_col.shape[0]
    tr = _tile(s, 512)
    f_mla = ROPE_BASE ** (-jnp.arange(ROPE_DIM // 2, dtype=F32) / (ROPE_DIM // 2))
    f_ret = ROPE_BASE ** (-jnp.arange(HEAD // 2, dtype=F32) / (HEAD // 2))
    fm = jnp.concatenate([jnp.zeros((64,), F32), f_mla, f_mla, jnp.zeros((32,), F32)]).reshape(1, LANES)
    fr = jnp.tile(jnp.concatenate([f_ret, f_ret]), 2).reshape(1, LANES)

    def body(p_ref, fm_ref, fr_ref, cm_ref, sm_ref, cr_ref, sr_ref):
        p = p_ref[...].astype(F32)
        am = p * fm_ref[...]
        ar = p * fr_ref[...]
        cm_ref[...] = jnp.cos(am)
        sm_ref[...] = jnp.sin(am)
        cr_ref[...] = jnp.tile(jnp.cos(ar), (1, 2))
        sr_ref[...] = jnp.tile(jnp.sin(ar), (1, 2))

    return pl.pallas_call(
        body, name="rope_tables", grid=(s // tr,),
        in_specs=[pl.BlockSpec((tr, 1), lambda i: (i, 0)), pl.BlockSpec((1, LANES), lambda i: (0, 0)),
                  pl.BlockSpec((1, LANES), lambda i: (0, 0))],
        out_specs=[pl.BlockSpec((tr, LANES), lambda i: (i, 0))] * 2 + [pl.BlockSpec((tr, 2 * LANES), lambda i: (i, 0))] * 2,
        out_shape=[jax.ShapeDtypeStruct((s, LANES), F32)] * 2 + [jax.ShapeDtypeStruct((s, 2 * LANES), F32)] * 2,
        compiler_params=_cparams(("parallel",)),
    )(pos_col, fm, fr)


def _lane(shape):
    return lax.broadcasted_iota(jnp.int32, shape, len(shape) - 1)


def _rot_mla(z):
    l = _lane(z.shape) % LANES
    n = z.shape[-1]
    return jnp.where(l < 80, -pltpu.roll(z, n - 16, 1), pltpu.roll(z, 16, 1))


def _rot_mla_t(y):
    l = _lane(y.shape) % LANES
    n = y.shape[-1]
    return jnp.where((l >= 64) & (l < 80), pltpu.roll(y, n - 16, 1),
                     jnp.where((l >= 80) & (l < 96), -pltpu.roll(y, 16, 1), 0.0))


def _rot_ret(z):
    l = _lane(z.shape) % HEAD
    n = z.shape[-1]
    return jnp.where(l < 32, -pltpu.roll(z, n - 32, 1), pltpu.roll(z, 32, 1))


def _rot_ret_t(y):
    l = _lane(y.shape) % HEAD
    n = y.shape[-1]
    return jnp.where(l < 32, pltpu.roll(y, n - 32, 1), -pltpu.roll(y, 32, 1))


def _log_sigmoid(x):
    return jnp.minimum(x, 0.0) - jnp.log1p(jnp.exp(-jnp.abs(x)))


def _fox_cum(proj, bias_row):
    s = proj.shape[0]
    fb = _tile(s, GATE_ROWS)
    nb = s // fb

    def body(x_ref, b_ref, cc_ref, cr_ref, carry_ref):
        @pl.when(pl.program_id(0) == 0)
        def _():
            carry_ref[...] = jnp.zeros_like(carry_ref)

        ls = _log_sigmoid(x_ref[...] + b_ref[...])
        r = lax.broadcasted_iota(jnp.int32, (fb, fb), 0)
        c = lax.broadcasted_iota(jnp.int32, (fb, fb), 1)
        tri = (c <= r).astype(F32)
        cum = _dot_exact(tri, ls) + carry_ref[...]
        carry_ref[...] = cum[fb - 1:fb, :]
        cc_ref[...] = cum
        cr_ref[...] = cum.T[0:8, :]

    return pl.pallas_call(
        body, name="fox_cum", grid=(nb,),
        in_specs=[pl.BlockSpec((fb, LANES), lambda i: (i, OFF_MISC)), pl.BlockSpec((1, LANES), lambda i: (0, 0))],
        out_specs=[pl.BlockSpec((fb, LANES), lambda i: (i, 0)), pl.BlockSpec((8, fb), lambda i: (0, i))],
        out_shape=[jax.ShapeDtypeStruct((s, LANES), F32), jax.ShapeDtypeStruct((8, s), F32)],
        scratch_shapes=[pltpu.VMEM((1, LANES), F32)],
        compiler_params=_cparams(("arbitrary",)),
    )(proj, bias_row)


def _fox_gate_bwd(dck, drs, proj, bias_row, dkr):
    s = proj.shape[0]
    fb = _tile(s, GATE_ROWS)
    nb = s // fb

    def body(d_ref, r_ref, x_ref, b_ref, k_ref, o_ref, db_ref, carry_ref):
        @pl.when(pl.program_id(0) == 0)
        def _():
            carry_ref[...] = jnp.zeros_like(carry_ref)
            db_ref[...] = jnp.zeros_like(db_ref)

        rows = jnp.concatenate([d_ref[0], d_ref[1], jnp.zeros((LANES - 16, fb), F32)], axis=0)
        t = rows.T
        l = _lane((fb, LANES))
        r0, r1 = r_ref[0], r_ref[1]
        rsum = jnp.where(l == 0, r0[:, 0:1], jnp.where(l == 1, r0[:, HEAD:HEAD + 1],
                         jnp.where(l == 2, r1[:, 0:1], jnp.where(l == 3, r1[:, HEAD:HEAD + 1], 0.0))))
        dcum = rsum - jnp.where(l < 2, t, pltpu.roll(t, LANES - 6, 1))
        r = lax.broadcasted_iota(jnp.int32, (fb, fb), 0)
        c = lax.broadcasted_iota(jnp.int32, (fb, fb), 1)
        triu = (c >= r).astype(F32)
        rc = _dot_exact(triu, dcum) + carry_ref[...]
        carry_ref[...] = rc[0:1, :]
        f = x_ref[...] + b_ref[...]
        sig_neg = 1.0 / (1.0 + jnp.exp(f))
        df = jnp.where(l < N_HEADS, rc * sig_neg, 0.0)
        db_ref[...] += jnp.sum(df, axis=0, keepdims=True)
        o_ref[...] = (df + k_ref[...]).astype(o_ref.dtype)

    rev = lambda i: nb - 1 - i
    return pl.pallas_call(
        body, name="fox_gate_bwd", grid=(nb,),
        in_specs=[pl.BlockSpec((2, 8, fb), lambda i: (0, 0, rev(i))), pl.BlockSpec((2, fb, LANES), lambda i: (0, rev(i), 0)),
                  pl.BlockSpec((fb, LANES), lambda i: (rev(i), OFF_MISC)),
                  pl.BlockSpec((1, LANES), lambda i: (0, 0)), pl.BlockSpec((fb, LANES), lambda i: (rev(i), 0))],
        out_specs=[pl.BlockSpec((fb, LANES), lambda i: (rev(i), 0)), pl.BlockSpec((1, LANES), lambda i: (0, 0))],
        out_shape=[jax.ShapeDtypeStruct((s, LANES), BF16), jax.ShapeDtypeStruct((1, LANES), F32)],
        scratch_shapes=[pltpu.VMEM((1, LANES), F32)],
        compiler_params=_cparams(("arbitrary",)),
    )(dck, drs, proj, bias_row, dkr)


def _mla_prep(proj, cos_m, sin_m, g_q, g_kv, wq, wk, wv):
    s = proj.shape[0]
    tr = _tile(s, 512)

    def body(cq_ref, ckv_ref, misc_ref, cos_ref, sin_ref, gq_ref, gkv_ref, wq_ref, wk_ref, wv_ref,
             q_ref, k_ref, v_ref, cqn_ref, ckvn_ref):
        cos4 = jnp.tile(cos_ref[...], (1, 4))
        sin4 = jnp.tile(sin_ref[...], (1, 4))
        cqn = _rms(cq_ref[...], gq_ref[...]).astype(BF16)
        ckvn = _rms(ckv_ref[...], gkv_ref[...]).astype(BF16)
        cqn_ref[...] = cqn
        ckvn_ref[...] = ckvn
        zq = _dot(cqn, wq_ref[...])
        q_ref[...] = (zq * cos4 + _rot_mla(zq) * sin4).astype(BF16)
        l = _lane((tr, LANES))
        kr = jnp.where((l >= KR_LANE) & (l < KR_LANE + ROPE_DIM), misc_ref[...], 0.0)
        zk = _dot(ckvn, wk_ref[...]) + jnp.tile(kr, (1, 4))
        k_ref[...] = (zk * cos4 + _rot_mla(zk) * sin4).astype(BF16)
        v_ref[...] = _dot(ckvn, wv_ref[...]).astype(BF16)

    full = lambda a: pl.BlockSpec(a.shape, lambda i: (0, 0))
    rowb = lambda w: pl.BlockSpec((tr, w), lambda i: (i, 0))
    gq2, gkv2 = g_q.reshape(1, Q_RANK), g_kv.reshape(1, KV_RANK)
    return pl.pallas_call(
        body, name="mla_prep", grid=(s // tr,),
        in_specs=[pl.BlockSpec((tr, 256), lambda i: (i, OFF_CQ // 2)), pl.BlockSpec((tr, LANES), lambda i: (i, OFF_CKV)),
                  pl.BlockSpec((tr, LANES), lambda i: (i, OFF_MISC)), rowb(LANES), rowb(LANES),
                  full(gq2), full(gkv2), full(wq), full(wk), full(wv)],
        out_specs=[rowb(512), rowb(512), rowb(512), rowb(256), rowb(128)],
        out_shape=[jax.ShapeDtypeStruct((s, 512), BF16), jax.ShapeDtypeStruct((s, 512), BF16), jax.ShapeDtypeStruct((s, 512), BF16),
                   jax.ShapeDtypeStruct((s, 256), BF16), jax.ShapeDtypeStruct((s, 128), BF16)],
        compiler_params=_cparams(("parallel",)),
    )(proj, proj, proj, cos_m, sin_m, gq2, gkv2, wq, wk, wv)


def _mla_prep_bwd(dq, dk, dv, proj, cqn, ckvn, cos_m, sin_m, g_q, g_kv, wq, wk, wv):
    s = proj.shape[0]
    tr = _tile(s, 512)

    def body(dq_ref, dk_ref, dv_ref, cq_ref, ckv_ref, cqn_ref, ckvn_ref, cos_ref, sin_ref, gq_ref, gkv_ref,
             wq_ref, wk_ref, wv_ref, dcq_ref, dckv_ref, dkr_ref, dwq_ref, dwk_ref, dwv_ref, dgq_ref, dgkv_ref):
        @pl.when(pl.program_id(0) == 0)
        def _():
            for r in (dwq_ref, dwk_ref, dwv_ref, dgq_ref, dgkv_ref):
                r[...] = jnp.zeros_like(r)

        cos4 = jnp.tile(cos_ref[...], (1, 4))
        sin4 = jnp.tile(sin_ref[...], (1, 4))
        dqv = dq_ref[...]
        dzq = dqv * cos4 + _rot_mla_t(dqv * sin4)
        dkv_ = dk_ref[...]
        dzk = dkv_ * cos4 + _rot_mla_t(dkv_ * sin4)
        l = _lane((tr, LANES))
        in_rope = (l >= KR_LANE) & (l < KR_LANE + ROPE_DIM)
        dkr = dzk[:, 0:128] + dzk[:, 128:256] + dzk[:, 256:384] + dzk[:, 384:512]
        dkr_ref[...] = jnp.where(in_rope, dkr, 0.0)
        dzq_b = dzq.astype(BF16)
        dzk_b = dzk.astype(BF16)
        dv_b = dv_ref[...].astype(BF16)
        dcqn = _dot_nt(dzq_b, wq_ref[...])
        dckvn = _dot_nt(dzk_b, wk_ref[...]) + _dot_nt(dv_b, wv_ref[...])
        dwq_ref[...] += _dot_tn(cqn_ref[...], dzq_b)
        dwk_ref[...] += _dot_tn(ckvn_ref[...], dzk_b)
        dwv_ref[...] += _dot_tn(ckvn_ref[...], dv_b)
        dcq, gq_term = _rms_bwd(cq_ref[...], gq_ref[...], dcqn)
        dckv, gkv_term = _rms_bwd(ckv_ref[...], gkv_ref[...], dckvn)
        dcq_ref[...] = dcq.astype(BF16)
        dckv_ref[...] = dckv.astype(BF16)
        dgq_ref[...] += jnp.sum(gq_term, axis=0, keepdims=True)
        dgkv_ref[...] += jnp.sum(gkv_term, axis=0, keepdims=True)

    full = lambda shp: pl.BlockSpec(shp, lambda i: (0, 0))
    rowb = lambda w: pl.BlockSpec((tr, w), lambda i: (i, 0))
    gq2, gkv2 = g_q.reshape(1, Q_RANK), g_kv.reshape(1, KV_RANK)
    return pl.pallas_call(
        body, name="mla_prep_bwd", grid=(s // tr,),
        in_specs=[rowb(512), rowb(512), rowb(512),
                  pl.BlockSpec((tr, 256), lambda i: (i, OFF_CQ // 2)), pl.BlockSpec((tr, LANES), lambda i: (i, OFF_CKV)),
                  rowb(256), rowb(128), rowb(LANES), rowb(LANES), full((1, Q_RANK)), full((1, KV_RANK)),
                  full(wq.shape), full(wk.shape), full(wv.shape)],
        out_specs=[rowb(256), rowb(128), rowb(128), full(wq.shape), full(wk.shape), full(wv.shape),
                   full((1, Q_RANK)), full((1, KV_RANK))],
        out_shape=[jax.ShapeDtypeStruct((s, 256), BF16), jax.ShapeDtypeStruct((s, 128), BF16), jax.ShapeDtypeStruct((s, 128), F32),
                   jax.ShapeDtypeStruct(wq.shape, F32), jax.ShapeDtypeStruct(wk.shape, F32), jax.ShapeDtypeStruct(wv.shape, F32),
                   jax.ShapeDtypeStruct((1, Q_RANK), F32), jax.ShapeDtypeStruct((1, KV_RANK), F32)],
        compiler_params=_cparams(("arbitrary",)),
    )(dq, dk, dv, proj, proj, cqn, ckvn, cos_m, sin_m, gq2, gkv2, wq, wk, wv)


def _ret_prep(proj, cos_r, sin_r):
    s = proj.shape[0]
    tr = _tile(s, 512)

    def body(q_ref, k_ref, cos_ref, sin_ref, qo_ref, ko_ref):
        cos, sin = cos_ref[...], sin_ref[...]
        q, k = q_ref[...], k_ref[...]
        qo_ref[...] = (q * cos + _rot_ret(q) * sin).astype(BF16)
        ko_ref[...] = ((k * cos + _rot_ret(k) * sin) * (HEAD ** -0.5)).astype(BF16)

    rowb = pl.BlockSpec((tr, 256), lambda i: (i, 0))
    return pl.pallas_call(
        body, name="ret_prep", grid=(s // tr,),
        in_specs=[pl.BlockSpec((tr, 256), lambda i: (i, OFF_RQ // 2)), pl.BlockSpec((tr, 256), lambda i: (i, OFF_RK // 2)), rowb, rowb],
        out_specs=[rowb, rowb], out_shape=[jax.ShapeDtypeStruct((s, 256), BF16)] * 2,
        compiler_params=_cparams(("parallel",)),
    )(proj, proj, cos_r, sin_r)


def _ret_prep_bwd(dq, dk, cos_r, sin_r):
    s = dq.shape[0]
    tr = _tile(s, 512)

    def body(dq_ref, dk_ref, cos_ref, sin_ref, qo_ref, ko_ref):
        cos, sin = cos_ref[...], sin_ref[...]
        q, k = dq_ref[...], dk_ref[...] * (HEAD ** -0.5)
        qo_ref[...] = (q * cos + _rot_ret_t(q * sin)).astype(BF16)
        ko_ref[...] = (k * cos + _rot_ret_t(k * sin)).astype(BF16)

    rowb = pl.BlockSpec((tr, 256), lambda i: (i, 0))
    return pl.pallas_call(
        body, name="ret_prep_bwd", grid=(s // tr,), in_specs=[rowb] * 4, out_specs=[rowb, rowb],
        out_shape=[jax.ShapeDtypeStruct((s, 256), BF16)] * 2, compiler_params=_cparams(("parallel",)),
    )(dq, dk, cos_r, sin_r)


_LOG_GAMMA = [float(np.log1p(-np.float32(2.0) ** np.float32(-5.0 - h))) for h in range(N_HEADS)]
_MLA_SCALE = float((HEAD + ROPE_DIM) ** -0.5)
_QK_SCALE = float(HEAD ** -0.5)
KEY_BLOCKS = 4
QB = 512


def _split2(x):
    h = x.astype(BF16)
    return h, (x - h.astype(F32)).astype(BF16)


def _dot2(x, u):
    h, lo = _split2(x)
    return _dot(h, u) + _dot(lo, u)


def _head_pick(block, head, axis):
    idx = lax.broadcasted_iota(jnp.int32, block.shape, axis)
    return jnp.sum(jnp.where(idx == head, block, 0.0), axis=axis, keepdims=True)


def _log_gamma_of(head):
    lg = jnp.float32(_LOG_GAMMA[3])
    for h in (2, 1, 0):
        lg = jnp.where(head == h, jnp.float32(_LOG_GAMMA[h]), lg)
    return lg


def _mixer_specs(mode, s, q_off, k_off, v_off):
    nhb = 2
    bw = 2 * LANES if mode == "mla" else LANES
    nsub = KEY_BLOCKS if (s // TQ) % KEY_BLOCKS == 0 else 1
    q_spec = pl.BlockSpec((QB, bw), lambda p, i: (i, q_off + p))
    k_spec = pl.BlockSpec((s, bw), lambda p, i: (0, k_off + p))
    v_spec = pl.BlockSpec((s, bw), lambda p, i: (0, v_off + p))
    return nhb, N_HEADS // nhb, nsub, q_spec, k_spec, v_spec


def _mixer_geometry(mode, i, nsub):
    w = TQ * nsub
    row = lax.broadcasted_iota(jnp.int32, (QB, w), 0)
    col = lax.broadcasted_iota(jnp.int32, (QB, w), 1)
    nfull = (i * QB) // w
    dist = col - row
    if mode in ("fox", "sb"):
        rel = dist
    else:
        rel = col - (row | (CHUNK - 1))

    def visible(c):
        off = c * w - i * QB
        return (rel + off) < 0 if mode == "sb" else (rel + off) <= 0

    return nfull, dist, visible


class _SideJob:
    def __init__(self, inputs, out_shape, n_sems, sends, recvs):
        self.inputs, self.out_shape, self.n_sems, self.sends, self.recvs = list(inputs), list(out_shape), n_sems, sends, recvs


def _carry_side_job(body, n_in, n_out, side, n_steps):
    if side is None:
        return body
    si, so = len(side.inputs), len(side.out_shape)

    def at(corner):
        ok = pl.program_id(0) == corner[0]
        for d in range(1, len(n_steps)):
            ok = ok & (pl.program_id(d) == corner[d])
        return ok

    def wrapped(*refs):
        ins, s_ins = refs[:n_in], refs[n_in:n_in + si]
        outs, s_outs = refs[n_in + si:n_in + si + n_out], refs[n_in + si + n_out:n_in + si + n_out + so]
        scratch, send, recv = refs[n_in + si + n_out + so:-2], refs[-2], refs[-1]

        @pl.when(at([0] * len(n_steps)))
        def _():
            for cp in side.sends(s_ins, s_outs, send, recv):
                cp.start()

        body(*ins, *outs, *scratch)

        @pl.when(at([n - 1 for n in n_steps]))
        def _():
            for cp in side.recvs(s_ins, s_outs, send, recv):
                cp.wait_recv()
            for cp in side.sends(s_ins, s_outs, send, recv):
                cp.wait_send()

    return wrapped


def _side_specs(side):
    if side is None:
        return [], [], []
    hbm = pl.BlockSpec(memory_space=pl.ANY)
    return ([hbm] * len(side.inputs), [hbm] * len(side.out_shape),
            [pltpu.SemaphoreType.DMA((side.n_sems,)), pltpu.SemaphoreType.DMA((side.n_sems,))])


def _mixer_fwd(mode, qa, q_off, ka, k_off, va, v_off, *, cum_col=None, cum_row=None, side=None):
    s = qa.shape[0]
    nq = s // QB
    nhb, nblk, nsub, q_spec, k_spec, v_spec = _mixer_specs(mode, s, q_off, k_off, v_off)
    w = TQ * nsub
    softmax = mode in ("fox", "mla")

    def body(*refs):
        refs = list(refs)
        q_ref, k_ref, v_ref = refs[:3]
        refs = refs[3:]
        if mode == "fox":
            cc_ref, cr_ref = refs[:2]
            refs = refs[2:]
        o_ref = refs[0]
        st_ref = refs[1]
        p = pl.program_id(0)
        i = pl.program_id(1)
        nfull, dist, visible = _mixer_geometry(mode, i, nsub)
        lane = _lane((1, LANES))
        heads = [nhb * p + hh for hh in range(nhb)]
        wide = mode == "mla"
        q_scale = _QK_SCALE if mode in ("fox", "sb") else 1.0
        cols = [slice(hh * LANES, (hh + 1) * LANES) if wide else slice(None) for hh in range(nhb)]
        if wide:
            qs = [q_ref[:, cols[hh]] for hh in range(nhb)]
        else:
            qf = q_ref[...].astype(F32) * q_scale
            qs = [jnp.where((lane // HEAD) == hh, qf, 0.0).astype(BF16) for hh in range(nhb)]
        if mode == "fox":
            cqs = [_head_pick(cc_ref[...], h, 1) for h in heads]
        if mode == "sb":
            r1 = lax.broadcasted_iota(jnp.int32, (TQ, TQ), 0)
            c1 = lax.broadcasted_iota(jnp.int32, (TQ, TQ), 1)
            u_after = (r1 > c1).astype(BF16)

        def chunk(c):
            return pl.ds(pl.multiple_of(c * w, w), w)

        def scores(c):
            js = chunk(c)
            return tuple(_dot_nt(qs[hh], k_ref[js, cols[hh]]) for hh in range(nhb))

        def head_step(hh, c, js, sc, vj, carry, last):
            if softmax:
                m, l, acc = carry
                if mode == "fox":
                    ck = _head_pick(cr_ref[:, js], heads[hh], 0)
                    sc = sc + (cqs[hh] - ck)
                else:
                    sc = sc * _MLA_SCALE
                if last:
                    sc = jnp.where(visible(c), sc, NEG)
                m_new = jnp.maximum(m, jnp.max(sc, axis=-1, keepdims=True))
                alpha = jnp.exp(m - m_new)
                pr = jnp.exp(sc - m_new)
                l = alpha * l + jnp.sum(pr, axis=-1, keepdims=True)
                acc = alpha * acc + _dot(pr.astype(BF16), vj)
                return m_new, l, acc
            run, acc = carry
            z = sc
            log_beta = jnp.minimum(z, 0.0) - jnp.log(1.0 + jnp.exp(-jnp.abs(z)))
            log_stay = log_beta - z
            if last:
                vis = visible(c)
                log_stay = jnp.where(vis, log_stay, 0.0)
            parts = [None] * nsub
            for b in reversed(range(nsub)):
                ls_b = log_stay[:, b * TQ:(b + 1) * TQ]
                parts[b] = _dot2(ls_b, u_after) + run
                run = run + jnp.sum(ls_b, axis=-1, keepdims=True)
            later = parts[0] if nsub == 1 else jnp.concatenate(parts, axis=1)
            wgt = jnp.exp(log_beta + later)
            if last:
                wgt = jnp.where(vis, wgt, 0.0)
            return run, acc + _dot(wgt.astype(BF16), vj)

        def step(c, c_next, state, last):
            scs, carries = state
            nxt = scores(c_next) if c_next is not None else None
            js = chunk(c)
            return nxt, tuple(head_step(hh, c, js, scs[hh], v_ref[js, cols[hh]], carries[hh], last) for hh in range(nhb))

        zero_acc = jnp.zeros((QB, LANES), F32)
        zero1 = jnp.zeros((QB, 1), F32)
        if softmax:
            init = tuple((jnp.full((QB, 1), NEG, F32), zero1, zero_acc) for _ in range(nhb))
        else:
            init = tuple((zero1, zero_acc) for _ in range(nhb))
        if mode == "sb":
            state = step(nfull, jnp.maximum(nfull - 1, 0), (scores(nfull), init), True)
            _, carries = lax.fori_loop(0, nfull, lambda t, st: step(nfull - 1 - t, jnp.maximum(nfull - 2 - t, 0), st, False), state)
        else:
            state = lax.fori_loop(0, nfull, lambda c, st: step(c, c + 1, st, False), (scores(0), init))
            _, carries = step(nfull, None, state, True)
        if softmax:
            outs = [acc / l for (m, l, acc) in carries]
            stats = [m + jnp.log(l) for (m, l, acc) in carries]
        else:
            outs, stats = [acc for (run, acc) in carries], [run for (run, acc) in carries]
        hm0 = (lane // HEAD) == 0
        pick = lambda a: jnp.where(hm0, a[0], a[1])
        if wide:
            for hh in range(nhb):
                o_ref[:, cols[hh]] = outs[hh]
        else:
            o_ref[...] = pick(outs)
        st_ref[0] = pick(stats)

    in_specs = [q_spec, k_spec, v_spec]
    args = [qa, ka, va]
    if mode == "fox":
        in_specs += [pl.BlockSpec((QB, LANES), lambda p, i: (i, 0)), pl.BlockSpec((8, s), lambda p, i: (0, 0))]
        args += [cum_col, cum_row]
    bw = 2 * LANES if mode == "mla" else LANES
    out_specs = [pl.BlockSpec((QB, bw), lambda p, i: (i, p))]
    out_shape = [jax.ShapeDtypeStruct((s, nblk * bw), F32)]
    out_specs.append(pl.BlockSpec((1, QB, LANES), lambda p, i: (p, i, 0)))
    out_shape.append(jax.ShapeDtypeStruct((nblk, s, LANES), F32))
    side_in, side_out, side_scratch = _side_specs(side)
    res = pl.pallas_call(
        _carry_side_job(body, len(args), len(out_shape), side, (nblk, nq)), name=mode + "_fwd", grid=(nblk, nq),
        in_specs=in_specs + side_in, out_specs=out_specs + side_out,
        out_shape=out_shape + ([] if side is None else side.out_shape), scratch_shapes=side_scratch,
        compiler_params=_cparams(("parallel", "parallel") if side is None else ("arbitrary", "arbitrary")),
    )(*args, *([] if side is None else side.inputs))
    return (res[0], res[1]) if side is None else (res[0], res[1], res[2:])


def _mixer_bwd(mode, qa, q_off, ka, k_off, va, v_off, o, do, *, stat=None, cum_col=None, cum_row=None, side=None):
    s = qa.shape[0]
    nq = s // QB
    nhb, nblk, nsub, q_spec, k_spec, v_spec = _mixer_specs(mode, s, q_off, k_off, v_off)
    w = TQ * nsub
    softmax = mode in ("fox", "mla")

    def body(*refs):
        refs = list(refs)
        q_ref, k_ref, v_ref, o_ref, do_ref = refs[:5]
        refs = refs[5:]
        st_ref = refs[0]
        refs = refs[1:]
        if mode == "fox":
            cc_ref, cr_ref = refs[:2]
            refs = refs[2:]
        dq_ref, dk_ref, dv_ref = refs[:3]
        dck_ref, drs_ref = refs[3:5] if mode == "fox" else (None, None)
        p = pl.program_id(0)
        i = pl.program_id(1)

        @pl.when(i == 0)
        def _():
            dk_ref[...] = jnp.zeros_like(dk_ref)
            dv_ref[...] = jnp.zeros_like(dv_ref)
            if mode == "fox":
                dck_ref[...] = jnp.zeros_like(dck_ref)

        nfull, dist, visible = _mixer_geometry(mode, i, nsub)
        lane = _lane((1, LANES))
        heads = [nhb * p + hh for hh in range(nhb)]
        dov = do_ref[...]
        wide = mode == "mla"
        q_scale = _QK_SCALE if mode in ("fox", "sb") else 1.0
        cols = [slice(hh * LANES, (hh + 1) * LANES) if wide else slice(None) for hh in range(nhb)]
        if wide:
            prod = dov * o_ref[...]
            qs = [q_ref[:, cols[hh]] for hh in range(nhb)]
            dos = [dov[:, cols[hh]].astype(BF16) for hh in range(nhb)]
            deltas = [jnp.sum(prod[:, cols[hh]], axis=-1, keepdims=True) for hh in range(nhb)]
        else:
            qf = q_ref[...].astype(F32) * q_scale
            prod = dov * o_ref[...]
            hms = [(lane // HEAD) == hh for hh in range(nhb)]
            qs = [jnp.where(hm, qf, 0.0).astype(BF16) for hm in hms]
            dos = [jnp.where(hm, dov, 0.0).astype(BF16) for hm in hms]
            deltas = [jnp.sum(jnp.where(hm, prod, 0.0), axis=-1, keepdims=True) for hm in hms]
        st = st_ref[0]
        stats = [st[:, hh * HEAD:hh * HEAD + 1] for hh in range(nhb)]
        if mode == "fox":
            cqs = [_head_pick(cc_ref[...], h, 1) for h in heads]
        if mode == "sb":
            r1 = lax.broadcasted_iota(jnp.int32, (TQ, TQ), 0)
            c1 = lax.broadcasted_iota(jnp.int32, (TQ, TQ), 1)
            u_upto = (r1 <= c1).astype(BF16)
            u_before = (r1 < c1).astype(BF16)

        def chunk(c):
            return pl.ds(pl.multiple_of(c * w, w), w)

        def scores(c):
            js = chunk(c)
            if mode == "sb":
                return tuple((_dot_nt(qs[hh], k_ref[js, cols[hh]]), None) for hh in range(nhb))
            return tuple((_dot_nt(qs[hh], k_ref[js, cols[hh]]), _dot_nt(dos[hh], v_ref[js, cols[hh]])) for hh in range(nhb))

        def emit(hh, js, ds_b, pr_b, dq):
            dk_ref[js, cols[hh]] += _dot_tn(ds_b, qs[hh])
            dv_ref[js, cols[hh]] += _dot_tn(pr_b, dos[hh])
            return dq + _dot(ds_b, k_ref[js, cols[hh]])

        def head_step(hh, c, js, sc_dp, carry, last):
            sc, dp = sc_dp
            if dp is None:
                dp = _dot_nt(dos[hh], v_ref[js, cols[hh]])
            if softmax:
                dq, rsum = carry
                if mode == "fox":
                    ck = _head_pick(cr_ref[:, js], heads[hh], 0)
                    sc = sc + (cqs[hh] - ck)
                else:
                    sc = sc * _MLA_SCALE
                if last:
                    sc = jnp.where(visible(c), sc, NEG)
                pr = jnp.exp(sc - stats[hh])
                ds = pr * (dp - deltas[hh])
                if mode == "fox":
                    dck_ref[0, hh:hh + 1, js] += jnp.sum(ds, axis=0, keepdims=True)
                    rsum = rsum + jnp.sum(ds, axis=-1, keepdims=True)
                if mode == "mla":
                    ds = ds * _MLA_SCALE
                return emit(hh, js, ds.astype(BF16), pr.astype(BF16), dq), rsum
            seen, gsum, dq = carry
            z = sc
            log_beta = jnp.minimum(z, 0.0) - jnp.log(1.0 + jnp.exp(-jnp.abs(z)))
            log_stay = log_beta - z
            if last:
                vis = visible(c)
                log_stay = jnp.where(vis, log_stay, 0.0)
            parts = []
            for b in range(nsub):
                ls_b = log_stay[:, b * TQ:(b + 1) * TQ]
                parts.append((stats[hh] - seen) - _dot2(ls_b, u_upto))
                seen = seen + jnp.sum(ls_b, axis=-1, keepdims=True)
            later = parts[0] if nsub == 1 else jnp.concatenate(parts, axis=1)
            wgt = jnp.exp(log_beta + later)
            if last:
                wgt = jnp.where(vis, wgt, 0.0)
            g = dp * wgt
            parts = []
            for b in range(nsub):
                g_b = g[:, b * TQ:(b + 1) * TQ]
                parts.append(gsum + _dot2(g_b, u_before))
                gsum = gsum + jnp.sum(g_b, axis=-1, keepdims=True)
            before = parts[0] if nsub == 1 else jnp.concatenate(parts, axis=1)
            beta = jnp.exp(log_beta)
            dz = g * (1.0 - beta) - beta * before
            if last:
                dz = jnp.where(vis, dz, 0.0)
            return seen, gsum, emit(hh, js, dz.astype(BF16), wgt.astype(BF16), dq)

        def step(c, c_next, state, last):
            scs, carries = state
            nxt = scores(c_next) if c_next is not None else None
            js = chunk(c)
            return nxt, tuple(head_step(hh, c, js, scs[hh], carries[hh], last) for hh in range(nhb))

        zero_acc = jnp.zeros((QB, LANES), F32)
        zero1 = jnp.zeros((QB, 1), F32)
        if softmax:
            init = tuple((zero_acc, zero1) for _ in range(nhb))
        else:
            init = tuple((zero1, zero1, zero_acc) for _ in range(nhb))
        state = lax.fori_loop(0, nfull, lambda c, st: step(c, c + 1, st, False), (scores(0), init))
        _, carries = step(nfull, None, state, True)
        if softmax:
            dqs = [dq for (dq, rsum) in carries]
        else:
            dqs = [dq for (seen, gsum, dq) in carries]
        hm0 = (lane // HEAD) == 0
        if wide:
            for hh in range(nhb):
                dq_ref[:, cols[hh]] = dqs[hh]
        else:
            dq_ref[...] = jnp.where(hm0, dqs[0], dqs[1]) * q_scale
        if mode == "fox":
            drs_ref[0] = jnp.where(hm0, carries[0][1], carries[1][1])

    bw = 2 * LANES if mode == "mla" else LANES
    pair_blk = pl.BlockSpec((QB, bw), lambda p, i: (i, p))
    full_blk = pl.BlockSpec((s, bw), lambda p, i: (0, p))
    stat_blk = pl.BlockSpec((1, QB, LANES), lambda p, i: (p, i, 0))
    in_specs = [q_spec, k_spec, v_spec, pair_blk, pair_blk]
    args = [qa, ka, va, o, do]
    in_specs.append(stat_blk)
    args.append(stat)
    if mode == "fox":
        in_specs += [pl.BlockSpec((QB, LANES), lambda p, i: (i, 0)), pl.BlockSpec((8, s), lambda p, i: (0, 0))]
        args += [cum_col, cum_row]
    out_specs = [pair_blk, full_blk, full_blk]
    out_shape = [jax.ShapeDtypeStruct((s, nblk * bw), F32)] * 3
    if mode == "fox":
        out_specs += [pl.BlockSpec((1, 8, s), lambda p, i: (p, 0, 0)), stat_blk]
        out_shape += [jax.ShapeDtypeStruct((2, 8, s), F32), jax.ShapeDtypeStruct((2, s, LANES), F32)]
    side_in, side_out, side_scratch = _side_specs(side)
    res = pl.pallas_call(
        _carry_side_job(body, len(args), len(out_shape), side, (nblk, nq)), name=mode + "_bwd", grid=(nblk, nq),
        in_specs=in_specs + side_in, out_specs=out_specs + side_out,
        out_shape=out_shape + ([] if side is None else side.out_shape), scratch_shapes=side_scratch,
        compiler_params=_cparams(("parallel", "arbitrary") if side is None else ("arbitrary", "arbitrary")),
    )(*args, *([] if side is None else side.inputs))
    return res if side is None else (*res[:len(out_shape)], res[len(out_shape):])


def _ret_geometry(p):
    lane = _lane((1, LANES))
    lg_lane = jnp.where(lane < HEAD, _log_gamma_of(2 * p), _log_gamma_of(2 * p + 1))
    a = lax.broadcasted_iota(jnp.int32, (TQ, 1), 0).astype(F32)
    row = lax.broadcasted_iota(jnp.int32, (TQ, TQ), 0)
    col = lax.broadcasted_iota(jnp.int32, (TQ, TQ), 1)
    same_chunk_or_earlier = (col // CHUNK) <= (row // CHUNK)
    gap = jnp.abs(row - col).astype(F32)
    decays = [jnp.where(same_chunk_or_earlier, jnp.exp(_log_gamma_of(2 * p + hh) * gap), 0.0) for hh in range(2)]
    r = lax.broadcasted_iota(jnp.int32, (LANES, LANES), 0)
    c = lax.broadcasted_iota(jnp.int32, (LANES, LANES), 1)
    own_head = (r // HEAD) == (c // HEAD)
    return lane, lg_lane, a, decays, own_head


def _ret_fwd(qa, ka, va, v_off):
    s = qa.shape[0]
    nq = s // TQ

    def body(q_ref, k_ref, v_ref, o_ref, st_ref, state):
        p = pl.program_id(0)

        @pl.when(pl.program_id(1) == 0)
        def _():
            state[...] = jnp.zeros_like(state)

        lane, lg_lane, a, decays, own_head = _ret_geometry(p)
        q = q_ref[...].astype(F32)
        k = k_ref[...]
        v = v_ref[...]
        s_in = state[...]
        st_ref[0, 0] = s_in
        out = _dot((q * jnp.exp(lg_lane * (a + 1.0))).astype(BF16), s_in.astype(BF16))
        for hh in range(2):
            hm = (lane // HEAD) == hh
            qh = jnp.where(hm, q, 0.0).astype(BF16)
            inner = _dot((_dot_nt(qh, k) * decays[hh]).astype(BF16), v)
            out = out + jnp.where(hm, inner, 0.0)
        o_ref[...] = out
        k_tail = (k.astype(F32) * jnp.exp(lg_lane * (TQ - 1.0 - a))).astype(BF16)
        state[...] = jnp.exp(lg_lane * float(TQ)) * s_in + jnp.where(own_head, _dot_tn(k_tail, v), 0.0)

    blk = lambda off: pl.BlockSpec((TQ, LANES), lambda p, i: (i, off + p))
    return pl.pallas_call(
        body, name="ret_fwd", grid=(2, nq), in_specs=[blk(0), blk(0), blk(v_off)],
        out_specs=[blk(0), pl.BlockSpec((1, 1, LANES, LANES), lambda p, i: (p, i, 0, 0))],
        out_shape=[jax.ShapeDtypeStruct((s, 2 * LANES), F32), jax.ShapeDtypeStruct((2, nq, LANES, LANES), F32)],
        scratch_shapes=[pltpu.VMEM((LANES, LANES), F32)],
        compiler_params=_cparams(("parallel", "arbitrary")),
    )(qa, ka, va)


def _ret_bwd(qa, ka, va, v_off, states, do):
    s = qa.shape[0]
    nq = s // TQ

    def body(q_ref, k_ref, v_ref, st_ref, do_ref, dq_ref, dk_ref, dv_ref, dstate):
        p = pl.program_id(0)

        @pl.when(pl.program_id(1) == 0)
        def _():
            dstate[...] = jnp.zeros_like(dstate)

        lane, lg_lane, a, decays, own_head = _ret_geometry(p)
        q = q_ref[...].astype(F32)
        k = k_ref[...]
        kf = k.astype(F32)
        v = v_ref[...]
        dov = do_ref[...]
        s_in = st_ref[0, 0].astype(BF16)
        ds_next = dstate[...]
        ds_b = ds_next.astype(BF16)
        head_decay = jnp.exp(lg_lane * (a + 1.0))
        tail_decay = jnp.exp(lg_lane * (TQ - 1.0 - a))
        k_tail = (kf * tail_decay).astype(BF16)
        dq = _dot_nt(dov.astype(BF16), s_in) * head_decay
        dk = _dot_nt(v, ds_b) * tail_decay
        dv = _dot(k_tail, ds_b)
        for hh in range(2):
            hm = (lane // HEAD) == hh
            qh = jnp.where(hm, q, 0.0).astype(BF16)
            doh = jnp.where(hm, dov, 0.0).astype(BF16)
            att = (_dot_nt(qh, k) * decays[hh]).astype(BF16)
            datt = (_dot_nt(doh, v) * decays[hh]).astype(BF16)
            dv = dv + _dot_tn(att, doh)
            dk = dk + _dot_tn(datt, qh)
            dq = dq + jnp.where(hm, _dot(datt, k), 0.0)
        dq_ref[...] = dq
        dk_ref[...] = dk
        dv_ref[...] = dv
        q_head = (q * head_decay).astype(BF16)
        dstate[...] = jnp.exp(lg_lane * float(TQ)) * ds_next + jnp.where(own_head, _dot_tn(q_head, dov.astype(BF16)), 0.0)

    blk = lambda off: pl.BlockSpec((TQ, LANES), lambda p, i: (nq - 1 - i, off + p))
    return pl.pallas_call(
        body, name="ret_bwd", grid=(2, nq),
        in_specs=[blk(0), blk(0), blk(v_off), pl.BlockSpec((1, 1, LANES, LANES), lambda p, i: (p, nq - 1 - i, 0, 0)), blk(0)],
        out_specs=[blk(0)] * 3, out_shape=[jax.ShapeDtypeStruct((s, 2 * LANES), F32)] * 3,
        scratch_shapes=[pltpu.VMEM((LANES, LANES), F32)],
        compiler_params=_cparams(("parallel", "arbitrary")),
    )(qa, ka, va, states, do)


def _seg_mean_matrix():
    r = lax.broadcasted_iota(jnp.int32, (GROUP, GROUP), 0)
    c = lax.broadcasted_iota(jnp.int32, (GROUP, GROUP), 1)
    return jnp.where((r // HEAD) == (c // HEAD), 1.0 / HEAD, 0.0).astype(BF16)


def _seg_mean(x, seg):
    h = x.astype(BF16)
    r = x - h.astype(F32)
    m = r.astype(BF16)
    lo = (r - m.astype(F32)).astype(BF16)
    return _dot(h, seg) + _dot(m, seg) + _dot(lo, seg)


def _sigmoid(x):
    return 1.0 / (1.0 + jnp.exp(-x))


def _mix_post(oa, ob, oc, od, proj, g):
    s = oa.shape[0]
    tr = _tile(s, 256)

    def body(a_ref, b_ref, c_ref, d_ref, rg_ref, g_ref, o_ref):
        gv = g_ref[...]
        o_ref[:, 0:GROUP] = _rms(a_ref[...], gv[:, 0:GROUP]).astype(BF16)
        o_ref[:, GROUP:2 * GROUP] = _rms(b_ref[...], gv[:, GROUP:2 * GROUP]).astype(BF16)
        seg = _seg_mean_matrix()
        c = c_ref[...]
        cen = c - _seg_mean(c, seg)
        n = cen * lax.rsqrt(_seg_mean(cen * cen, seg) + EPS)
        rg = rg_ref[...]
        o_ref[:, 2 * GROUP:3 * GROUP] = (n * gv[:, 2 * GROUP:3 * GROUP] * (rg * _sigmoid(rg))).astype(BF16)
        o_ref[:, 3 * GROUP:] = _rms(d_ref[...], gv[:, 3 * GROUP:]).astype(BF16)

    blk = pl.BlockSpec((tr, GROUP), lambda i: (i, 0))
    return pl.pallas_call(
        body, name="mix_post", grid=(s // tr,),
        in_specs=[blk] * 4 + [pl.BlockSpec((tr, GROUP), lambda i: (i, OFF_RG // 2)), pl.BlockSpec((1, D_MODEL), lambda i: (0, 0))],
        out_specs=pl.BlockSpec((tr, D_MODEL), lambda i: (i, 0)), out_shape=jax.ShapeDtypeStruct((s, D_MODEL), BF16),
        compiler_params=_cparams(("parallel",)),
    )(oa, ob, oc, od, proj, g.reshape(1, D_MODEL))


def _mix_post_bwd(dmixed, oa, ob, oc, od, proj, g):
    s = oa.shape[0]
    tr = _tile(s, 256)

    def body(dm_ref, a_ref, b_ref, c_ref, d_ref, rg_ref, g_ref, da_ref, db_ref, dc_ref, dd_ref, drg_ref, dg_ref):
        @pl.when(pl.program_id(0) == 0)
        def _():
            dg_ref[...] = jnp.zeros_like(dg_ref)

        gv = g_ref[...]
        dm = dm_ref[...]
        for k, (x_ref, dx_ref) in enumerate(((a_ref, da_ref), (b_ref, db_ref), (None, None), (d_ref, dd_ref))):
            if x_ref is None:
                continue
            cols = slice(k * GROUP, (k + 1) * GROUP)
            dx, gterm = _rms_bwd(x_ref[...], gv[:, cols], dm[:, cols])
            dx_ref[...] = dx
            dg_ref[:, cols] += jnp.sum(gterm, axis=0, keepdims=True)
        cols = slice(2 * GROUP, 3 * GROUP)
        seg = _seg_mean_matrix()
        c = c_ref[...]
        cen = c - _seg_mean(c, seg)
        rstd = lax.rsqrt(_seg_mean(cen * cen, seg) + EPS)
        n = cen * rstd
        rg = rg_ref[...]
        sg = _sigmoid(rg)
        gate = rg * sg
        dy = dm[:, cols]
        gc = gv[:, cols]
        dn = dy * gc * gate
        dg_ref[:, cols] += jnp.sum(dy * n * gate, axis=0, keepdims=True)
        drg_ref[...] = (dy * n * gc * (sg * (1.0 + rg * (1.0 - sg)))).astype(BF16)
        dc_ref[...] = rstd * (dn - _seg_mean(dn, seg) - n * _seg_mean(dn * n, seg))

    blk = pl.BlockSpec((tr, GROUP), lambda i: (i, 0))
    gsp = pl.BlockSpec((1, D_MODEL), lambda i: (0, 0))
    return pl.pallas_call(
        body, name="mix_post_bwd", grid=(s // tr,),
        in_specs=[pl.BlockSpec((tr, D_MODEL), lambda i: (i, 0))] + [blk] * 4 + [pl.BlockSpec((tr, GROUP), lambda i: (i, OFF_RG // 2)), gsp],
        out_specs=[blk] * 5 + [gsp],
        out_shape=[jax.ShapeDtypeStruct((s, GROUP), F32)] * 4 + [jax.ShapeDtypeStruct((s, GROUP), BF16), jax.ShapeDtypeStruct((1, D_MODEL), F32)],
        compiler_params=_cparams(("arbitrary",)),
    )(dmixed, oa, ob, oc, od, proj, g.reshape(1, D_MODEL))


def _pack_w_in(w):
    z = lambda n: jnp.zeros((w.shape[0], n), w.dtype)
    misc = jnp.concatenate([w[:, 768:772], z(KR_LANE - N_HEADS), w[:, 1156:1188], z(LANES - KR_LANE - ROPE_DIM)], axis=1)
    return jnp.concatenate([w[:, 0:768], w[:, 772:1028], w[:, 1188:2980], w[:, 1028:1156], misc], axis=1)


def _unpack_dw_in(d):
    m = OFF_MISC * LANES
    return jnp.concatenate([d[:, 0:768], d[:, m:m + N_HEADS], d[:, 768:1024], d[:, OFF_CKV * LANES:m],
                            d[:, m + KR_LANE:m + KR_LANE + ROPE_DIM], d[:, 1024:OFF_CKV * LANES]], axis=1)


def _pack_w_q(w):
    return jnp.pad(w.reshape(Q_RANK, N_HEADS, HEAD + ROPE_DIM), ((0, 0), (0, 0), (0, LANES - HEAD - ROPE_DIM))).reshape(Q_RANK, 4 * LANES)


def _unpack_dw_q(d):
    return d.reshape(Q_RANK, N_HEADS, LANES)[:, :, :HEAD + ROPE_DIM].reshape(Q_RANK, N_HEADS * (HEAD + ROPE_DIM))


def _pack_w_kv(w):
    w4 = w.reshape(KV_RANK, N_HEADS, 2 * HEAD)
    widen = lambda a: jnp.pad(a, ((0, 0), (0, 0), (0, LANES - HEAD))).reshape(KV_RANK, N_HEADS * LANES)
    return widen(w4[:, :, :HEAD]), widen(w4[:, :, HEAD:])


def _unpack_dw_kv(dk, dv):
    narrow = lambda a: a.reshape(KV_RANK, N_HEADS, LANES)[:, :, :HEAD]
    return jnp.concatenate([narrow(dk), narrow(dv)], axis=2).reshape(KV_RANK, 2 * N_HEADS * HEAD)


def _narrow_heads(a):
    return a.reshape(a.shape[0], N_HEADS, LANES)[:, :, :HEAD].reshape(a.shape[0], N_HEADS * HEAD)


def _widen_heads(a):
    return jnp.pad(a.reshape(a.shape[0], N_HEADS, HEAD), ((0, 0), (0, 0), (0, LANES - HEAD))).reshape(a.shape[0], N_HEADS * LANES)


def _layer_fwd(x, lw, tabs, tag, side=None, fox_side=None, late_weights=None, h1=None, next_gain=None):
    cos_m, sin_m, cos_r, sin_r = tabs
    if h1 is None:
        h1 = _norm_fwd(x, lw["g_mix_pre"], name=tag + "pre_norm")
    proj, projb = _matmul(h1, lw["w_in"], name=tag + "in_proj", also_bf16=True)
    bias_row = jnp.pad(lw["b_forget"], (FF_LANE, LANES - N_HEADS - FF_LANE)).reshape(1, LANES)
    cum_col, cum_row = _fox_cum(proj, bias_row)
    oa, lse_a, *fox_carried = _mixer_fwd("fox", projb, OFF_FQ, projb, OFF_FK, projb, OFF_FV, cum_col=cum_col, cum_row=cum_row,
                                         side=fox_side)
    if late_weights is not None:
        lw = {**lw, **late_weights(fox_carried[0])}
    qm, km, vm, cqn, ckvn = _mla_prep(proj, cos_m, sin_m, lw["g_q_lora"], lw["g_kv_lora"], lw["wq"], lw["wk"], lw["wv"])
    ob_wide, lse_b = _mixer_fwd("mla", qm, 0, km, 0, vm, 0)
    ob = _narrow_heads(ob_wide)
    qr, kr = _ret_prep(proj, cos_r, sin_r)
    oc, ret_states = _ret_fwd(qr, kr, projb, OFF_RV)
    od, tot_d, *carried = _mixer_fwd("sb", projb, OFF_SQ, projb, OFF_SK, projb, OFF_SV, side=side)
    mixed = _mix_post(oa, ob, oc, od, proj, lw["g_mix_out"])
    mix = _matmul(mixed, lw["w_out"], name=tag + "out_proj")
    x1, h2 = _norm_fwd(mix, lw["g_mix_post"], name=tag + "mix_post_norm", resid=x, out_dtype=F32, next_gain=lw["g_ffn_pre"])
    u = _matmul(h2, lw["w_ffn_up"], name=tag + "ffn_up", relu2=True, out_dtype=BF16, col_blocks=True)
    f = _matmul(u, lw["w_ffn_down"], name=tag + "ffn_down")
    x2, h_next = None, None
    if next_gain is not None:
        x2, h_next = _norm_fwd(f, lw["g_ffn_post"], name=tag + "ffn_post_norm", resid=x1, out_dtype=F32, next_gain=next_gain)
    saved = dict(x=x, h1=h1, proj=proj, projb=projb, bias_row=bias_row, cum_col=cum_col, cum_row=cum_row, oa=oa, lse_a=lse_a,
                 qm=qm, km=km, vm=vm, cqn=cqn, ckvn=ckvn, ob=ob, ob_wide=ob_wide, lse_b=lse_b, qr=qr, kr=kr, ret_states=ret_states, oc=oc, od=od, tot_d=tot_d, mixed=mixed,
                 mix=mix, x1=x1, h2=h2, u=u, f=f)
    return x2, saved, lw, (carried[0] if carried else None), h_next


def _layer_bwd(dx2, lw, sv, tabs, tag, side=None, ffn_side=None, fox_side=None, post_given=None, then_prev=None):
    cos_m, sin_m, cos_r, sin_r = tabs
    g = {}
    if post_given is None:
        df, g["g_ffn_post"] = _norm_bwd(sv["f"], lw["g_ffn_post"], dx2, name=tag + "ffn_post_norm_bwd", out_dtype=BF16)
    else:
        df, g["g_ffn_post"] = post_given
    du_pre = _matmul(df, lw["w_ffn_down"], name=tag + "ffn_down_dx", tb=True, out_dtype=BF16, relu2_of=sv["u"], side=ffn_side)
    ffn_carried = None
    if ffn_side is not None:
        du_pre, ffn_carried = du_pre
    g["w_ffn_down"] = _matmul(sv["u"], df, name=tag + "ffn_down_dw", ta=True)
    dh2 = _matmul(du_pre, lw["w_ffn_up"], name=tag + "ffn_up_dx", tb=True, col_blocks=True)
    g["w_ffn_up"] = _matmul(sv["h2"], du_pre, name=tag + "ffn_up_dw", ta=True, col_blocks=True)
    dx1, g["g_ffn_pre"], dmix, g["g_mix_post"] = _norm_bwd(sv["x1"], lw["g_ffn_pre"], dh2, name=tag + "ffn_pre_norm_bwd", add=dx2,
                                                           then=(sv["mix"], lw["g_mix_post"]))
    dmixed = _matmul(dmix, lw["w_out"], name=tag + "out_proj_dx", tb=True)
    g["w_out"] = _matmul(sv["mixed"], dmix, name=tag + "out_proj_dw", ta=True)
    proj, projb = sv["proj"], sv["projb"]
    doa, dob, doc, dod, drg, g["g_mix_out"] = _mix_post_bwd(dmixed, sv["oa"], sv["ob"], sv["oc"], sv["od"], proj, lw["g_mix_out"])
    dfq, dfk, dfv, dck, drs, *fox_carried = _mixer_bwd(
        "fox", projb, OFF_FQ, projb, OFF_FK, projb, OFF_FV, sv["oa"], doa, stat=sv["lse_a"], cum_col=sv["cum_col"],
        cum_row=sv["cum_row"], side=None if fox_side is None else fox_side(g))
    dqm, dkm, dvm = _mixer_bwd("mla", sv["qm"], 0, sv["km"], 0, sv["vm"], 0, sv["ob_wide"], _widen_heads(dob), stat=sv["lse_b"])
    dcq, dckv, dkr, dwq, dwk, dwv, g["g_q_lora"], g["g_kv_lora"] = _mla_prep_bwd(
        dqm, dkm, dvm, proj, sv["cqn"], sv["ckvn"], cos_m, sin_m, lw["g_q_lora"], lw["g_kv_lora"], lw["wq"], lw["wk"], lw["wv"])
    dqr, dkr_ret, drv = _ret_bwd(sv["qr"], sv["kr"], projb, OFF_RV, sv["ret_states"], doc)
    drq, drk = _ret_prep_bwd(dqr, dkr_ret, cos_r, sin_r)
    if callable(side):
        side = side(g, ffn_carried, fox_carried[0] if fox_carried else None)
    dsq, dsk, dsv, *carried = _mixer_bwd("sb", projb, OFF_SQ, projb, OFF_SK, projb, OFF_SV, sv["od"], dod, stat=sv["tot_d"], side=side)
    dmisc, db_row = _fox_gate_bwd(dck, drs, proj, sv["bias_row"], dkr)
    b = lambda a: a.astype(BF16)
    dproj = jnp.concatenate([b(dfq), b(dfk), b(dfv), dcq, drq, drk, b(drv), drg, b(dsq), b(dsk), b(dsv), dckv, dmisc], axis=1)
    dh1 = _matmul(dproj, lw["w_in"], name=tag + "in_proj_dx", tb=True)
    g["w_in"] = _matmul(sv["h1"], dproj, name=tag + "in_proj_dw", ta=True)
    dx, g["g_mix_pre"], *prev_post = _norm_bwd(sv["x"], lw["g_mix_pre"], dh1, name=tag + "pre_norm_bwd", add=dx1, then=then_prev)
    g["b_forget"] = db_row[0, FF_LANE:FF_LANE + N_HEADS]
    g["wq"], g["wk"], g["wv"] = dwq, dwk, dwv
    return dx, g, (carried[0] if carried else None), (tuple(prev_post) if prev_post else None)


def _local_step(x, positions, layers, target):
    s = x.shape[0]
    tabs = _rope_tables(positions.reshape(s, 1))
    saved, h1 = [], None
    for li, lw in enumerate(layers):
        nxt = layers[li + 1]["g_mix_pre"] if li + 1 < len(layers) else None
        x, sv, _, _, h1 = _layer_fwd(x, lw, tabs, "l%d_" % li, h1=h1, next_gain=nxt)
        saved.append(sv)
    loss_row, dx, df, dg = _loss_head(saved[-1]["f"], layers[-1]["g_ffn_post"], saved[-1]["x1"], target)
    grads, post = [None] * len(layers), (df, dg)
    for li in reversed(range(len(layers))):
        prev = (saved[li - 1]["f"], layers[li - 1]["g_ffn_post"]) if li > 0 else None
        dx, grads[li], _, post = _layer_bwd(dx, layers[li], saved[li], tabs, "l%d_" % li, post_given=post, then_prev=prev)
    return loss_row[0, 0], dx, grads


def _adamw(w, g, m, v, *, name):
    d, r, c = w.shape
    tr = 256 if r % 256 == 0 else r
    blk = pl.BlockSpec((None, tr, c), lambda l, i: (l, i, 0))
    c1 = 1.0 - ADAM_B1 ** ADAM_STEP
    c2 = 1.0 - ADAM_B2 ** ADAM_STEP

    def body(w_ref, g_ref, m_ref, v_ref, d_ref, mo_ref, vo_ref):
        gv = g_ref[...]
        mn = ADAM_B1 * m_ref[...] + (1.0 - ADAM_B1) * gv
        vn = ADAM_B2 * v_ref[...] + (1.0 - ADAM_B2) * jnp.square(gv)
        mo_ref[...] = mn
        vo_ref[...] = vn
        d_ref[...] = -ADAM_LR * ((mn / c1) / (jnp.sqrt(vn / c2) + ADAM_EPS) + ADAM_WD * w_ref[...])

    return pl.pallas_call(
        body, name=name, grid=(d, r // tr), in_specs=[blk] * 4, out_specs=[blk] * 3,
        out_shape=[jax.ShapeDtypeStruct((d, r, c), F32)] * 3, compiler_params=_cparams(("parallel", "parallel")),
    )(w, g, m, v)


BIG = ("w_in", "w_q_up", "w_kv_up", "w_out", "w_ffn_up", "w_ffn_down")
SMALL = ("g_mix_pre", "b_forget", "g_q_lora", "g_kv_lora", "g_mix_out", "g_mix_post", "g_ffn_pre", "g_ffn_post")
N_CHIPS = 4
ANY = pl.BlockSpec(memory_space=pl.ANY)


def _mesh_pos():
    return lax.axis_index("x"), lax.axis_index("y"), lax.axis_index("c")


def _other_chips(x, y):
    return [(1 - x, y), (x, 1 - y), (1 - x, 1 - y)]


def _rows_half(ref, half):
    h = ref.shape[-2] // 2
    return ref.at[(slice(None),) * (len(ref.shape) - 2) + (pl.ds(half * h, h), slice(None))]


def _remote(src, dst, send_sem, recv_sem, device):
    return pltpu.make_async_remote_copy(src_ref=src, dst_ref=dst, send_sem=send_sem, recv_sem=recv_sem, device_id=device,
                                        device_id_type=MESH)


def _comm_call(body, name, args, out_shape, n_sems):
    return pl.pallas_call(
        body, name=name, in_specs=[ANY] * len(args), out_specs=[ANY] * len(out_shape), out_shape=out_shape,
        scratch_shapes=[pltpu.SemaphoreType.DMA((n_sems,)), pltpu.SemaphoreType.DMA((n_sems,))],
        compiler_params=pltpu.CompilerParams(has_side_effects=True),
    )(*args)


def _run_side_job(side, name):
    si = len(side.inputs)

    def body(*refs):
        args = (refs[:si], refs[si:-2], refs[-2], refs[-1])
        sends = side.sends(*args)
        for cp in sends:
            cp.start()
        for cp in side.recvs(*args):
            cp.wait_recv()
        for cp in sends:
            cp.wait_send()

    return _comm_call(body, name, side.inputs, side.out_shape, side.n_sems)


def _gather_job(shards):
    n = len(shards)

    def copies(own_block, ins, outs, send_sems, recv_sems):
        x, y, c = _mesh_pos()
        return [_remote(_rows_half(ins[t], c), _rows_half(outs[t].at[2 * x + y if own_block else 2 * px + py], c),
                        send_sems.at[3 * t + j], recv_sems.at[3 * t + j], (px, py, c))
                for t in range(n) for j, (px, py) in enumerate(_other_chips(x, y))]

    return _SideJob(shards, [jax.ShapeDtypeStruct((N_CHIPS,) + a.shape, a.dtype) for a in shards], 3 * n,
                    functools.partial(copies, True), functools.partial(copies, False))


def _forward_halves(gathered):
    n = len(gathered)

    def body(*refs):
        bufs, send_sems, recv_sems = refs[n:2 * n], refs[-2], refs[-1]
        x, y, c = _mesh_pos()

        def d2d(t, j, block, half):
            region = _rows_half(bufs[t].at[block], half)
            return _remote(region, region, send_sems.at[3 * t + j], recv_sems.at[3 * t + j], (x, y, 1 - c))

        peers = list(enumerate(_other_chips(x, y)))
        sends = [d2d(t, j, 2 * px + py, c) for t in range(n) for j, (px, py) in peers]
        for cp in sends:
            cp.start()
        for t in range(n):
            for j, (px, py) in peers:
                d2d(t, j, 2 * px + py, 1 - c).wait_recv()
        for cp in sends:
            cp.wait_send()

    return pl.pallas_call(
        body, name="gather_forward", in_specs=[ANY] * n, out_specs=[ANY] * n,
        out_shape=[jax.ShapeDtypeStruct(g.shape, g.dtype) for g in gathered], input_output_aliases={t: t for t in range(n)},
        scratch_shapes=[pltpu.SemaphoreType.DMA((3 * n,)), pltpu.SemaphoreType.DMA((3 * n,))],
        compiler_params=pltpu.CompilerParams(has_side_effects=True),
    )(*gathered)


def _exchange_halves_job(gs):
    n = len(gs)

    def copies(ins, outs, send_sems, recv_sems):
        x, y, c = _mesh_pos()
        return [_remote(_rows_half(ins[t], 1 - c), outs[t], send_sems.at[t], recv_sems.at[t], (x, y, 1 - c)) for t in range(n)]

    out_shape = [jax.ShapeDtypeStruct(g.shape[:2] + (g.shape[2] // 2, g.shape[3]), g.dtype) for g in gs]
    return _SideJob(gs, out_shape, n, copies, copies)


def _pair_add(g, r, c_idx, *, name):
    nb, d, rows, cols = g.shape
    h = rows // 2
    tr = min(h, 512)
    nt = h // tr

    def body(c_ref, g_ref, r_ref, p_ref, pb_ref):
        s = g_ref[...] + r_ref[...]
        p_ref[...] = s
        pb_ref[...] = s.astype(BF16)

    blk = pl.BlockSpec((1, 1, tr, cols), lambda k, l, i, c_ref: (k, l, i, 0))
    return pl.pallas_call(
        body, name=name,
        grid_spec=pltpu.PrefetchScalarGridSpec(
            num_scalar_prefetch=1, grid=(nb, d, nt),
            in_specs=[pl.BlockSpec((1, 1, tr, cols), lambda k, l, i, c_ref: (k, l, c_ref[0] * nt + i, 0)), blk],
            out_specs=[blk, blk]),
        out_shape=[jax.ShapeDtypeStruct((nb, d, h, cols), F32), jax.ShapeDtypeStruct((nb, d, h, cols), BF16)],
        compiler_params=_cparams(("parallel", "parallel", "parallel")),
    )(c_idx, g, r)


def _exchange_chips_job(pbs):
    n = len(pbs)

    def copies(ins, outs, send_sems, recv_sems):
        x, y, c = _mesh_pos()
        return [_remote(ins[t].at[2 * px + py], outs[t].at[j], send_sems.at[3 * t + j], recv_sems.at[3 * t + j], (px, py, c))
                for t in range(n) for j, (px, py) in enumerate(_other_chips(x, y))]

    return _SideJob(pbs, [jax.ShapeDtypeStruct((3,) + p.shape[1:], p.dtype) for p in pbs], 3 * n, copies, copies)


def _chip_add(p, r, k_idx, *, name):
    _, d, h, cols = p.shape
    tr = min(h, 512)
    nt = h // tr

    def body(k_ref, p_ref, r_ref, o_ref):
        o_ref[0] = ((p_ref[0, 0] + r_ref[0, 0].astype(F32)) + r_ref[1, 0].astype(F32)) + r_ref[2, 0].astype(F32)

    return pl.pallas_call(
        body, name=name,
        grid_spec=pltpu.PrefetchScalarGridSpec(
            num_scalar_prefetch=1, grid=(d, nt),
            in_specs=[pl.BlockSpec((1, 1, tr, cols), lambda l, i, k_ref: (k_ref[0], l, i, 0)),
                      pl.BlockSpec((3, 1, tr, cols), lambda l, i, k_ref: (0, l, i, 0))],
            out_specs=pl.BlockSpec((1, tr, cols), lambda l, i, k_ref: (l, i, 0))),
        out_shape=jax.ShapeDtypeStruct((d, h, cols), F32), compiler_params=_cparams(("parallel", "parallel")),
    )(k_idx, p, r)


def _share_halves(qs):
    n = len(qs)

    def body(*refs):
        ins, outs, send_sems, recv_sems = refs[:n], refs[n:2 * n], refs[2 * n], refs[2 * n + 1]
        x, y, c = _mesh_pos()
        cps = [_remote(ins[t], outs[t], send_sems.at[t], recv_sems.at[t], (x, y, 1 - c)) for t in range(n)]
        for cp in cps:
            cp.start()
        for cp in cps:
            cp.wait_recv()
        for cp in cps:
            cp.wait_send()

    return _comm_call(body, "grad_pair_share", qs, [jax.ShapeDtypeStruct(q.shape, q.dtype) for q in qs], n)


def _all_reduce_small(v):
    r, cols = v.shape
    n_dev = 8

    def body(v_ref, o_ref, buf, send_sems, recv_sems):
        x, y, c = _mesh_pos()
        me = 4 * x + 2 * y + c
        buf[me] = v_ref[...]

        def peer(j):
            return (1 - x if j & 4 else x, 1 - y if j & 2 else y, 1 - c if j & 1 else c)

        def copy(j, slot):
            return pltpu.make_async_remote_copy(src_ref=v_ref, dst_ref=buf.at[slot], send_sem=send_sems.at[j - 1],
                                                recv_sem=recv_sems.at[j - 1], device_id=peer(j), device_id_type=MESH)

        sends = [copy(j, me) for j in range(1, n_dev)]
        for cp in sends:
            cp.start()
        for j in range(1, n_dev):
            px, py, pc = peer(j)
            copy(j, 4 * px + 2 * py + pc).wait_recv()
        for cp in sends:
            cp.wait_send()
        acc = buf[0]
        for d in range(1, n_dev):
            acc = acc + buf[d]
        o_ref[...] = acc

    vm = pl.BlockSpec(memory_space=pltpu.VMEM)
    return pl.pallas_call(
        body, name="small_all_reduce", in_specs=[vm], out_specs=vm, out_shape=jax.ShapeDtypeStruct((r, cols), F32),
        scratch_shapes=[pltpu.VMEM((n_dev, r, cols), F32), pltpu.SemaphoreType.DMA((n_dev - 1,)), pltpu.SemaphoreType.DMA((n_dev - 1,))],
        compiler_params=pltpu.CompilerParams(has_side_effects=True),
    )(v)


_COL_SHARDED = ("w_in", "w_q_up", "w_kv_up", "w_ffn_up")


def _shard_cols(blocks, a, b):
    c = blocks[0].shape[-1]
    out = []
    while a < b:
        k = a // c
        hi = min(b, (k + 1) * c)
        out.append(blocks[k][:, a - k * c:hi - k * c])
        a = hi
    return out


def _pack_w_in_shards(blocks):
    z = lambda n: [jnp.zeros((blocks[0].shape[0], n), blocks[0].dtype)]
    cols = lambda a, b: _shard_cols(blocks, a, b)
    return jnp.concatenate(cols(0, 768) + cols(772, 1028) + cols(1188, 2980) + cols(1028, 1156) + cols(768, 772)
                           + z(KR_LANE - N_HEADS) + cols(1156, 1188) + z(LANES - KR_LANE - ROPE_DIM), axis=1)


def _whole_layer(name, blocks):
    if name in _COL_SHARDED:
        return jnp.concatenate([blocks[k] for k in range(N_CHIPS)], axis=1)
    return blocks.reshape(N_CHIPS * blocks.shape[1], blocks.shape[2])


def _split_layer(name, whole):
    if name in _COL_SHARDED:
        c = whole.shape[1] // N_CHIPS
        return jnp.stack([whole[:, k * c:(k + 1) * c] for k in range(N_CHIPS)])
    return whole.reshape(N_CHIPS, whole.shape[0] // N_CHIPS, whole.shape[1])


def _small_to_rows(d):
    v = jnp.concatenate([d[k].astype(F32).reshape(-1) for k in SMALL])
    rows = -(-v.shape[0] // (8 * LANES)) * 8
    return jnp.pad(v, (0, rows * LANES - v.shape[0])).reshape(rows, LANES)


def _small_from_rows(rows, shapes):
    v = rows.reshape(-1)
    out, o = {}, 0
    for k in SMALL:
        sz = int(np.prod(shapes[k]))
        out[k] = v[o:o + sz].reshape(shapes[k])
        o += sz
    return out


_ARG_NAMES = ("x", "positions", "g_mix_pre", "w_in", "b_forget", "g_q_lora", "w_q_up", "g_kv_lora", "w_kv_up", "g_mix_out", "w_out",
              "g_mix_post", "g_ffn_pre", "w_ffn_up", "w_ffn_down", "g_ffn_post")
_WEIGHTS = _ARG_NAMES[2:]


def kernel(x, positions, g_mix_pre, w_in, b_forget, g_q_lora, w_q_up, g_kv_lora, w_kv_up, g_mix_out, w_out, g_mix_post, g_ffn_pre, w_ffn_up, w_ffn_down, g_ffn_post, loss_target, m_g_mix_pre, m_w_in, m_b_forget, m_g_q_lora, m_w_q_up, m_g_kv_lora, m_w_kv_up, m_g_mix_out, m_w_out, m_g_mix_post, m_g_ffn_pre, m_w_ffn_up, m_w_ffn_down, m_g_ffn_post, v_g_mix_pre, v_w_in, v_b_forget, v_g_q_lora, v_w_q_up, v_g_kv_lora, v_w_kv_up, v_g_mix_out, v_w_out, v_g_mix_post, v_g_ffn_pre, v_w_ffn_up, v_w_ffn_down, v_g_ffn_post):
    w = dict(g_mix_pre=g_mix_pre, w_in=w_in, b_forget=b_forget, g_q_lora=g_q_lora, w_q_up=w_q_up, g_kv_lora=g_kv_lora, w_kv_up=w_kv_up,
             g_mix_out=g_mix_out, w_out=w_out, g_mix_post=g_mix_post, g_ffn_pre=g_ffn_pre, w_ffn_up=w_ffn_up, w_ffn_down=w_ffn_down,
             g_ffn_post=g_ffn_post)
    m = dict(g_mix_pre=m_g_mix_pre, w_in=m_w_in, b_forget=m_b_forget, g_q_lora=m_g_q_lora, w_q_up=m_w_q_up, g_kv_lora=m_g_kv_lora,
             w_kv_up=m_w_kv_up, g_mix_out=m_g_mix_out, w_out=m_w_out, g_mix_post=m_g_mix_post, g_ffn_pre=m_g_ffn_pre,
             w_ffn_up=m_w_ffn_up, w_ffn_down=m_w_ffn_down, g_ffn_post=m_g_ffn_post)
    v = dict(g_mix_pre=v_g_mix_pre, w_in=v_w_in, b_forget=v_b_forget, g_q_lora=v_g_q_lora, w_q_up=v_w_q_up, g_kv_lora=v_g_kv_lora,
             w_kv_up=v_w_kv_up, g_mix_out=v_g_mix_out, w_out=v_w_out, g_mix_post=v_g_mix_post, g_ffn_pre=v_g_ffn_pre,
             w_ffn_up=v_w_ffn_up, w_ffn_down=v_w_ffn_down, g_ffn_post=v_g_ffn_post)
    small_shapes = {k: w[k].shape for k in SMALL}
    c_idx = lax.axis_index("c").astype(jnp.int32).reshape(1)
    k_idx = (2 * lax.axis_index("x") + lax.axis_index("y")).astype(jnp.int32).reshape(1)
    first_core = lax.axis_index("c") == 0

    mine = 2 * lax.axis_index("x") + lax.axis_index("y")
    shards_b = [{k: w[k][l:l + 1].astype(BF16) for k in BIG} for l in range(DEPTH)]
    gains = [dict(g_mix_pre=g_mix_pre[l], b_forget=b_forget[l], g_q_lora=g_q_lora[l], g_kv_lora=g_kv_lora[l], g_mix_out=g_mix_out[l],
                  g_mix_post=g_mix_post[l], g_ffn_pre=g_ffn_pre[l], g_ffn_post=g_ffn_post[l]) for l in range(DEPTH)]
    FIRST, LATER = ("w_in", "w_q_up", "w_kv_up"), ("w_out", "w_ffn_up", "w_ffn_down")
    EARLY_GRADS, LATE_GRADS = ("w_ffn_down", "w_ffn_up", "w_out"), ("w_in", "w_q_up", "w_kv_up")

    def gather_job(l, names):
        return _gather_job([shards_b[l][k] for k in names])

    def weights_of(l, names, gathered):
        four = {k: lax.dynamic_update_slice(g, shards_b[l][k][None], (mine, 0, 0, 0))[:, 0]
                for k, g in zip(names, _forward_halves(gathered))}
        out = {}
        for k in names:
            if k == "w_in":
                out["w_in"] = _pack_w_in_shards(four[k])
            elif k == "w_q_up":
                out["wq"] = _pack_w_q(_whole_layer(k, four[k]))
            elif k == "w_kv_up":
                out["wk"], out["wv"] = _pack_w_kv(_whole_layer(k, four[k]))
            elif k == "w_ffn_up":
                out[k] = four[k]
            else:
                out[k] = _whole_layer(k, four[k])
        return out

    def grad_blocks(names, g):
        whole = dict(w_in=lambda: _unpack_dw_in(g["w_in"]), w_q_up=lambda: _unpack_dw_q(g["wq"]),
                     w_kv_up=lambda: _unpack_dw_kv(g["wk"], g["wv"]), w_out=lambda: g["w_out"], w_ffn_down=lambda: g["w_ffn_down"])
        return [(g[k] if k == "w_ffn_up" else _split_layer(k, whole[k]()))[:, None] for k in names]

    def pair_sums(names, blocks, theirs):
        return [_pair_add(b, r, c_idx, name="grad_pair_add_" + k) for k, b, r in zip(names, blocks, theirs)]

    def exchange_job(*pairs):
        return _exchange_chips_job([pb for pair in pairs for (_, pb) in pair])

    def finish_grads(names, pair, partial):
        half = [_chip_add(p, r, k_idx, name="grad_chip_add_" + k) for k, (p, _), r in zip(names, pair, partial)]
        return {k: jnp.where(first_core, jnp.concatenate([q, s], axis=1), jnp.concatenate([s, q], axis=1))
                for k, q, s in zip(names, half, _share_halves(half))}

    seq = x.shape[1]
    tabs = _rope_tables(positions[0].reshape(seq, 1))
    first0 = weights_of(0, FIRST, _run_side_job(gather_job(0, FIRST), "gather_weights_l0"))
    x1, saved0, lw0, gathered1, h1 = _layer_fwd(x[0], {**gains[0], **first0}, tabs, "l0_", fox_side=gather_job(0, LATER),
                                                late_weights=lambda got: weights_of(0, LATER, got), side=gather_job(1, BIG),
                                                next_gain=gains[1]["g_mix_pre"])
    lw1 = {**gains[1], **weights_of(1, BIG, gathered1)}
    _, saved1, _, _, _ = _layer_fwd(x1, lw1, tabs, "l1_", h1=h1)
    loss_row, dx, df1, dg1 = _loss_head(saved1["f"], lw1["g_ffn_post"], saved1["x1"], loss_target[0])
    loss = lax.psum(loss_row[0, 0], ("x", "y", "c"))
    dx, grads1, _, post0 = _layer_bwd(dx, lw1, saved1, tabs, "l1_", post_given=(df1, dg1),
                                      then_prev=(saved0["f"], lw0["g_ffn_post"]))
    blocks1 = grad_blocks(BIG, grads1)
    early_blocks0, pair1, early0 = [], [], []

    def beside_l0_fox_backward(g):
        early_blocks0.extend(grad_blocks(EARLY_GRADS, g))
        return _exchange_halves_job(early_blocks0)

    def beside_l0_sb_backward(g, theirs1, theirs_early0):
        pair1.extend(pair_sums(BIG, blocks1, theirs1))
        early0.extend(pair_sums(EARLY_GRADS, early_blocks0, theirs_early0))
        return exchange_job(pair1, early0)

    dx, grads0, partial, _ = _layer_bwd(dx, lw0, saved0, tabs, "l0_", ffn_side=_exchange_halves_job(blocks1),
                                        fox_side=beside_l0_fox_backward, side=beside_l0_sb_backward, post_given=post0)
    big1 = finish_grads(BIG, pair1, partial[:len(BIG)])
    big0 = finish_grads(EARLY_GRADS, early0, partial[len(BIG):])
    late_blocks0 = grad_blocks(LATE_GRADS, grads0)
    late0 = pair_sums(LATE_GRADS, late_blocks0, _run_side_job(_exchange_halves_job(late_blocks0), "grad_pair_exchange_l0"))
    big0.update(finish_grads(LATE_GRADS, late0, _run_side_job(exchange_job(late0), "grad_chip_exchange_l0")))
    g_big = {k: jnp.concatenate([big0[k], big1[k]], axis=0) for k in BIG}
    grads = [grads0, grads1]

    g_small_local = {k: jnp.stack([grads[l][k].reshape(small_shapes[k][1:]) for l in range(DEPTH)]) for k in SMALL}
    g_small = _small_from_rows(_all_reduce_small(_small_to_rows(g_small_local)), small_shapes)

    g_all = {**g_big, **g_small}
    delta, new_m, new_v = {}, {}, {}
    for k in BIG:
        delta[k], new_m[k], new_v[k] = _adamw(w[k], g_all[k], m[k], v[k], name="adamw_" + k)
    ds, ms, vs = _adamw(*[_small_to_rows(t)[None] for t in (w, g_small, m, v)], name="adamw_small")
    delta.update(_small_from_rows(ds, small_shapes))
    new_m.update(_small_from_rows(ms, small_shapes))
    new_v.update(_small_from_rows(vs, small_shapes))

    grad_x = dx.reshape(x.shape)
    return (loss, grad_x, *[g_all[k] for k in _WEIGHTS], *[delta[k] for k in _WEIGHTS], *[new_m[k] for k in _WEIGHTS],
            *[new_v[k] for k in _WEIGHTS])
```

```python
import functools
import math

import numpy as np
import jax
import jax.numpy as jnp
from jax import lax
from jax.experimental import pallas as pl
from jax.experimental.pallas import tpu as pltpu
from jax.experimental.pallas import tpu_sc as plsc

F32 = jnp.float32
BF16 = jnp.bfloat16
MESH = pl.DeviceIdType.MESH

D_MODEL = 1024
DEPTH = 2
CHUNK = 64
GROUP = 256
HEAD = 64
N_HEADS = 4
Q_RANK = 256
KV_RANK = 128
ROPE_DIM = 32
D_FF = 4096
D_IN = 2980
D_INP = 3072
ROPE_BASE = 10000.0
EPS = 1e-6
LANES = 128
TQ = 128
GATE_ROWS = 512
CONTRACT_TILE = 4096
NEG = -1e30

ADAM_LR, ADAM_B1, ADAM_B2, ADAM_EPS, ADAM_WD, ADAM_STEP = 0.001, 0.9, 0.999, 1e-08, 0.01, 10

OFF_FQ, OFF_FK, OFF_FV, OFF_CQ = 0, 2, 4, 6
OFF_RQ, OFF_RK, OFF_RV, OFF_RG = 8, 10, 12, 14
OFF_SQ, OFF_SK, OFF_SV = 16, 18, 20
OFF_CKV, OFF_MISC = 22, 23
FF_LANE, KR_LANE = 0, 64

VMEM_LIMIT = 56 * 1024 * 1024


def _tile(dim, pref):
    return pref if dim % pref == 0 else dim


def _cparams(sem, vmem=None):
    return pltpu.CompilerParams(dimension_semantics=sem, vmem_limit_bytes=vmem or VMEM_LIMIT)


def _dot(a, b):
    return jnp.dot(a, b, preferred_element_type=F32)


def _dot_nt(a, b):
    return lax.dot_general(a, b, (((1,), (1,)), ((), ())), preferred_element_type=F32)


def _dot_tn(a, b):
    return lax.dot_general(a, b, (((0,), (0,)), ((), ())), preferred_element_type=F32)


def _dot_exact(a, b):
    return jnp.dot(a, b, precision=lax.Precision.HIGHEST, preferred_element_type=F32)


def _matmul(a, b, *, name, ta=False, tb=False, out_dtype=F32, tm=1024, tn=1024, tk=CONTRACT_TILE,
            relu2=False, relu2_of=None, also_bf16=False, side=None, col_blocks=False):
    if ta:
        kdim, m = a.shape
    else:
        m, kdim = a.shape
    if col_blocks and not ta:
        n = b.shape[1] if tb else b.shape[0] * b.shape[2]
        if tb:
            kdim = b.shape[0] * b.shape[2]
    else:
        n = b.shape[0] if tb else b.shape[1]
    tm, tn, tk = _tile(m, tm), _tile(n, tn), _tile(kdim, tk)
    nk = kdim // tk
    a_spec = pl.BlockSpec((tk, tm), lambda i, j, k: (k, i)) if ta else pl.BlockSpec((tm, tk), lambda i, j, k: (i, k))
    b_spec = pl.BlockSpec((tn, tk), lambda i, j, k: (j, k)) if tb else pl.BlockSpec((tk, tn), lambda i, j, k: (k, j))
    o_spec = pl.BlockSpec((tm, tn), lambda i, j, k: (i, j))
    if col_blocks and ta:
        o_spec = pl.BlockSpec((None, tm, tn), lambda i, j, k: (j, i, 0))
    elif col_blocks and tb:
        assert tk == kdim
        b_spec = pl.BlockSpec((b.shape[0], tn, b.shape[2]), lambda i, j, k: (0, j, 0))
    elif col_blocks:
        assert b.shape[2] == tn
        b_spec = pl.BlockSpec((None, tk, tn), lambda i, j, k: (j, k, 0))
    two = also_bf16

    def body(*refs):
        refs = list(refs)
        a_ref, b_ref = refs[0], refs[1]
        e_ref = refs[2] if relu2_of is not None else None
        pos = 3 if relu2_of is not None else 2
        o_ref = refs[pos]
        o2_ref = refs[pos + 1] if two else None
        acc_ref = refs[-1]
        k = pl.program_id(2)
        av = a_ref[...].astype(BF16)
        if col_blocks and tb:
            bv = jnp.concatenate([b_ref[q] for q in range(b.shape[0])], axis=1).astype(BF16)
        else:
            bv = b_ref[...].astype(BF16)
        if ta:
            part = _dot_tn(av, bv)
        elif tb:
            part = _dot_nt(av, bv)
        else:
            part = _dot(av, bv)

        @pl.when(k == 0)
        def _():
            acc_ref[...] = part

        @pl.when(k > 0)
        def _():
            acc_ref[...] += part

        @pl.when(k == nk - 1)
        def _():
            r = acc_ref[...]
            if relu2_of is not None:
                r = r * (2.0 * jnp.sqrt(e_ref[...].astype(F32)))
            if relu2:
                r = jnp.square(jnp.maximum(r, 0.0))
            o_ref[...] = r.astype(o_ref.dtype)
            if also_bf16:
                o2_ref[...] = r.astype(BF16)

    in_specs = [a_spec, b_spec]
    args = [a, b]
    if relu2_of is not None:
        in_specs.append(o_spec)
        args.append(relu2_of)
    out_shape = [jax.ShapeDtypeStruct((n // tn, m, tn) if (col_blocks and ta) else (m, n), out_dtype)]
    out_specs = [o_spec]
    if two:
        out_shape.append(jax.ShapeDtypeStruct((m, n), BF16))
        out_specs.append(o_spec)
    grid = (m // tm, n // tn, nk)
    side_in, side_out, side_scratch = _side_specs(side)
    res = pl.pallas_call(
        _carry_side_job(body, len(args), len(out_shape), side, grid), name=name, grid=grid,
        in_specs=in_specs + side_in, out_specs=out_specs + side_out,
        out_shape=out_shape + ([] if side is None else side.out_shape),
        scratch_shapes=[pltpu.VMEM((tm, tn), F32)] + side_scratch,
        compiler_params=_cparams(("parallel", "parallel", "arbitrary") if side is None else ("arbitrary",) * 3),
    )(*args, *([] if side is None else side.inputs))
    main = res[:len(out_shape)]
    main = main if two else main[0]
    return main if side is None else (main, res[len(out_shape):])


def _rms(x, g):
    r = lax.rsqrt(jnp.mean(x * x, axis=-1, keepdims=True) + EPS)
    return x * r * g


def _rms_bwd(x, g, dy):
    r = lax.rsqrt(jnp.mean(x * x, axis=-1, keepdims=True) + EPS)
    xh = x * r
    gdy = dy * g
    dx = r * (gdy - xh * jnp.mean(xh * gdy, axis=-1, keepdims=True))
    return dx, xh * dy


def _norm_fwd(x, g, *, name, resid=None, out_dtype=BF16, next_gain=None):
    s, d = x.shape
    tr = _tile(s, 256)
    row = pl.BlockSpec((tr, d), lambda i: (i, 0))
    gsp = pl.BlockSpec((1, d), lambda i: (0, 0))

    def body(*refs):
        refs = list(refs)
        x_ref, g_ref = refs[:2]
        y = _rms(x_ref[...], g_ref[...])
        pos = 2
        if resid is not None:
            y = refs[pos][...] + y
            pos += 1
        if next_gain is None:
            refs[pos][...] = y.astype(refs[pos].dtype)
        else:
            refs[pos + 1][...] = y.astype(refs[pos + 1].dtype)
            refs[pos + 2][...] = _rms(y, refs[pos][...]).astype(BF16)

    args = [x, g.reshape(1, d)] + ([] if resid is None else [resid]) + ([] if next_gain is None else [next_gain.reshape(1, d)])
    in_specs = [row, gsp] + ([] if resid is None else [row]) + ([] if next_gain is None else [gsp])
    first = jax.ShapeDtypeStruct((s, d), out_dtype)
    if next_gain is None:
        out_specs, out_shape = row, first
    else:
        out_specs, out_shape = [row, row], [first, jax.ShapeDtypeStruct((s, d), BF16)]
    return pl.pallas_call(
        body, name=name, grid=(s // tr,), in_specs=in_specs, out_specs=out_specs, out_shape=out_shape,
        compiler_params=_cparams(("parallel",)),
    )(*args)


def _norm_bwd(x, g, dy, *, name, add=None, out_dtype=F32, then=None):
    s, d = x.shape
    tr = _tile(s, 256)
    row = pl.BlockSpec((tr, d), lambda i: (i, 0))
    gsp = pl.BlockSpec((1, d), lambda i: (0, 0))
    n_in = 3 + (add is not None) + (2 if then is not None else 0)

    def body(*refs):
        ins, outs = refs[:n_in], refs[n_in:]
        x_ref, g_ref, dy_ref = ins[:3]
        dx, gterm = _rms_bwd(x_ref[...], g_ref[...], dy_ref[...].astype(F32))
        if add is not None:
            dx = dx + ins[3][...]
        outs[0][...] = dx.astype(outs[0].dtype)
        terms = [(outs[1], gterm)]
        if then is not None:
            dx2, gterm2 = _rms_bwd(ins[-2][...], ins[-1][...], dx)
            outs[2][...] = dx2.astype(BF16)
            terms.append((outs[3], gterm2))

        @pl.when(pl.program_id(0) == 0)
        def _():
            for dg_ref, _ in terms:
                dg_ref[...] = jnp.zeros_like(dg_ref)

        for dg_ref, term in terms:
            dg_ref[...] += jnp.sum(term, axis=0, keepdims=True)

    args = [x, g.reshape(1, d), dy] + ([] if add is None else [add]) + ([] if then is None else [then[0], then[1].reshape(1, d)])
    in_specs = [row, gsp, row] + ([] if add is None else [row]) + ([] if then is None else [row, gsp])
    out_specs = [row, gsp] + ([] if then is None else [row, gsp])
    out_shape = [jax.ShapeDtypeStruct((s, d), out_dtype), jax.ShapeDtypeStruct((1, d), F32)]
    if then is not None:
        out_shape += [jax.ShapeDtypeStruct((s, d), BF16), jax.ShapeDtypeStruct((1, d), F32)]
    return pl.pallas_call(
        body, name=name, grid=(s // tr,), in_specs=in_specs, out_specs=out_specs, out_shape=out_shape,
        compiler_params=_cparams(("arbitrary",)),
    )(*args)


def _loss_head(f, g, resid, target):
    s, d = f.shape
    tr = _tile(s, 256)
    row = pl.BlockSpec((tr, d), lambda i: (i, 0))
    gsp = pl.BlockSpec((1, d), lambda i: (0, 0))
    lsp = pl.BlockSpec((1, LANES), lambda i: (0, 0))

    def body(f_ref, g_ref, r_ref, t_ref, l_ref, dy_ref, df_ref, dg_ref):
        fv, gv = f_ref[...], g_ref[...]
        e = (r_ref[...] + _rms(fv, gv)) - t_ref[...]
        dy = e * (1.0 / d)
        dy_ref[...] = dy
        df, gterm = _rms_bwd(fv, gv, dy)
        df_ref[...] = df.astype(BF16)

        @pl.when(pl.program_id(0) == 0)
        def _():
            l_ref[...] = jnp.zeros_like(l_ref)
            dg_ref[...] = jnp.zeros_like(dg_ref)

        part = 0.5 * jnp.sum(jnp.mean(e * e, axis=-1, keepdims=True), axis=0, keepdims=True)
        l_ref[...] += jnp.broadcast_to(part, (1, LANES))
        dg_ref[...] += jnp.sum(gterm, axis=0, keepdims=True)

    return pl.pallas_call(
        body, name="loss_head", grid=(s // tr,), in_specs=[row, gsp, row, row], out_specs=[lsp, row, row, gsp],
        out_shape=[jax.ShapeDtypeStruct((1, LANES), F32), jax.ShapeDtypeStruct((s, d), F32), jax.ShapeDtypeStruct((s, d), BF16),
                   jax.ShapeDtypeStruct((1, d), F32)],
        compiler_params=_cparams(("arbitrary",)),
    )(f, g.reshape(1, d), resid, target)


def _rope_tables(pos_col):
    s = pos_col.shape[0]
    tr = _tile(s, 512)
    f_mla = ROPE_BASE ** (-jnp.arange(ROPE_DIM // 2, dtype=F32) / (ROPE_DIM // 2))
    f_ret = ROPE_BASE ** (-jnp.arange(HEAD // 2, dtype=F32) / (HEAD // 2))
    fm = jnp.concatenate([jnp.zeros((64,), F32), f_mla, f_mla, jnp.zeros((32,), F32)]).reshape(1, LANES)
    fr = jnp.tile(jnp.concatenate([f_ret, f_ret]), 2).reshape(1, LANES)

    def body(p_ref, fm_ref, fr_ref, cm_ref, sm_ref, cr_ref, sr_ref):
        p = p_ref[...].astype(F32)
        am = p * fm_ref[...]
        ar = p * fr_ref[...]
        cm_ref[...] = jnp.cos(am)
        sm_ref[...] = jnp.sin(am)
        cr_ref[...] = jnp.tile(jnp.cos(ar), (1, 2))
        sr_ref[...] = jnp.tile(jnp.sin(ar), (1, 2))

    return pl.pallas_call(
        body, name="rope_tables", grid=(s // tr,),
        in_specs=[pl.BlockSpec((tr, 1), lambda i: (i, 0)), pl.BlockSpec((1, LANES), lambda i: (0, 0)),
                  pl.BlockSpec((1, LANES), lambda i: (0, 0))],
        out_specs=[pl.BlockSpec((tr, LANES), lambda i: (i, 0))] * 2 + [pl.BlockSpec((tr, 2 * LANES), lambda i: (i, 0))] * 2,
        out_shape=[jax.ShapeDtypeStruct((s, LANES), F32)] * 2 + [jax.ShapeDtypeStruct((s, 2 * LANES), F32)] * 2,
        compiler_params=_cparams(("parallel",)),
    )(pos_col, fm, fr)


def _lane(shape):
    return lax.broadcasted_iota(jnp.int32, shape, len(shape) - 1)


def _rot_mla(z):
    l = _lane(z.shape) % LANES
    n = z.shape[-1]
    return jnp.where(l < 80, -pltpu.roll(z, n - 16, 1), pltpu.roll(z, 16, 1))


def _rot_mla_t(y):
    l = _lane(y.shape) % LANES
    n = y.shape[-1]
    return jnp.where((l >= 64) & (l < 80), pltpu.roll(y, n - 16, 1),
                     jnp.where((l >= 80) & (l < 96), -pltpu.roll(y, 16, 1), 0.0))


def _rot_ret(z):
    l = _lane(z.shape) % HEAD
    n = z.shape[-1]
    return jnp.where(l < 32, -pltpu.roll(z, n - 32, 1), pltpu.roll(z, 32, 1))


def _rot_ret_t(y):
    l = _lane(y.shape) % HEAD
    n = y.shape[-1]
    return jnp.where(l < 32, pltpu.roll(y, n - 32, 1), -pltpu.roll(y, 32, 1))


def _log_sigmoid(x):
    return jnp.minimum(x, 0.0) - jnp.log1p(jnp.exp(-jnp.abs(x)))


def _fox_cum(proj, bias_row):
    s = proj.shape[0]
    fb = _tile(s, GATE_ROWS)
    nb = s // fb

    def body(x_ref, b_ref, cc_ref, cr_ref, carry_ref):
        @pl.when(pl.program_id(0) == 0)
        def _():
            carry_ref[...] = jnp.zeros_like(carry_ref)

        ls = _log_sigmoid(x_ref[...] + b_ref[...])
        r = lax.broadcasted_iota(jnp.int32, (fb, fb), 0)
        c = lax.broadcasted_iota(jnp.int32, (fb, fb), 1)
        tri = (c <= r).astype(F32)
        cum = _dot_exact(tri, ls) + carry_ref[...]
        carry_ref[...] = cum[fb - 1:fb, :]
        cc_ref[...] = cum
        cr_ref[...] = cum.T[0:8, :]

    return pl.pallas_call(
        body, name="fox_cum", grid=(nb,),
        in_specs=[pl.BlockSpec((fb, LANES), lambda i: (i, OFF_MISC)), pl.BlockSpec((1, LANES), lambda i: (0, 0))],
        out_specs=[pl.BlockSpec((fb, LANES), lambda i: (i, 0)), pl.BlockSpec((8, fb), lambda i: (0, i))],
        out_shape=[jax.ShapeDtypeStruct((s, LANES), F32), jax.ShapeDtypeStruct((8, s), F32)],
        scratch_shapes=[pltpu.VMEM((1, LANES), F32)],
        compiler_params=_cparams(("arbitrary",)),
    )(proj, bias_row)


def _fox_gate_bwd(dck, drs, proj, bias_row, dkr):
    s = proj.shape[0]
    fb = _tile(s, GATE_ROWS)
    nb = s // fb

    def body(d_ref, r_ref, x_ref, b_ref, k_ref, o_ref, db_ref, carry_ref):
        @pl.when(pl.program_id(0) == 0)
        def _():
            carry_ref[...] = jnp.zeros_like(carry_ref)
            db_ref[...] = jnp.zeros_like(db_ref)

        rows = jnp.concatenate([d_ref[0], d_ref[1], jnp.zeros((LANES - 16, fb), F32)], axis=0)
        t = rows.T
        l = _lane((fb, LANES))
        r0, r1 = r_ref[0], r_ref[1]
        rsum = jnp.where(l == 0, r0[:, 0:1], jnp.where(l == 1, r0[:, HEAD:HEAD + 1],
                         jnp.where(l == 2, r1[:, 0:1], jnp.where(l == 3, r1[:, HEAD:HEAD + 1], 0.0))))
        dcum = rsum - jnp.where(l < 2, t, pltpu.roll(t, LANES - 6, 1))
        r = lax.broadcasted_iota(jnp.int32, (fb, fb), 0)
        c = lax.broadcasted_iota(jnp.int32, (fb, fb), 1)
        triu = (c >= r).astype(F32)
        rc = _dot_exact(triu, dcum) + carry_ref[...]
        carry_ref[...] = rc[0:1, :]
        f = x_ref[...] + b_ref[...]
        sig_neg = 1.0 / (1.0 + jnp.exp(f))
        df = jnp.where(l < N_HEADS, rc * sig_neg, 0.0)
        db_ref[...] += jnp.sum(df, axis=0, keepdims=True)
        o_ref[...] = (df + k_ref[...]).astype(o_ref.dtype)

    rev = lambda i: nb - 1 - i
    return pl.pallas_call(
        body, name="fox_gate_bwd", grid=(nb,),
        in_specs=[pl.BlockSpec((2, 8, fb), lambda i: (0, 0, rev(i))), pl.BlockSpec((2, fb, LANES), lambda i: (0, rev(i), 0)),
                  pl.BlockSpec((fb, LANES), lambda i: (rev(i), OFF_MISC)),
                  pl.BlockSpec((1, LANES), lambda i: (0, 0)), pl.BlockSpec((fb, LANES), lambda i: (rev(i), 0))],
        out_specs=[pl.BlockSpec((fb, LANES), lambda i: (rev(i), 0)), pl.BlockSpec((1, LANES), lambda i: (0, 0))],
        out_shape=[jax.ShapeDtypeStruct((s, LANES), BF16), jax.ShapeDtypeStruct((1, LANES), F32)],
        scratch_shapes=[pltpu.VMEM((1, LANES), F32)],
        compiler_params=_cparams(("arbitrary",)),
    )(dck, drs, proj, bias_row, dkr)


def _mla_prep(proj, cos_m, sin_m, g_q, g_kv, wq, wk, wv):
    s = proj.shape[0]
    tr = _tile(s, 512)

    def body(cq_ref, ckv_ref, misc_ref, cos_ref, sin_ref, gq_ref, gkv_ref, wq_ref, wk_ref, wv_ref,
             q_ref, k_ref, v_ref, cqn_ref, ckvn_ref):
        cos4 = jnp.tile(cos_ref[...], (1, 4))
        sin4 = jnp.tile(sin_ref[...], (1, 4))
        cqn = _rms(cq_ref[...], gq_ref[...]).astype(BF16)
        ckvn = _rms(ckv_ref[...], gkv_ref[...]).astype(BF16)
        cqn_ref[...] = cqn
        ckvn_ref[...] = ckvn
        zq = _dot(cqn, wq_ref[...])
        q_ref[...] = (zq * cos4 + _rot_mla(zq) * sin4).astype(BF16)
        l = _lane((tr, LANES))
        kr = jnp.where((l >= KR_LANE) & (l < KR_LANE + ROPE_DIM), misc_ref[...], 0.0)
        zk = _dot(ckvn, wk_ref[...]) + jnp.tile(kr, (1, 4))
        k_ref[...] = (zk * cos4 + _rot_mla(zk) * sin4).astype(BF16)
        v_ref[...] = _dot(ckvn, wv_ref[...]).astype(BF16)

    full = lambda a: pl.BlockSpec(a.shape, lambda i: (0, 0))
    rowb = lambda w: pl.BlockSpec((tr, w), lambda i: (i, 0))
    gq2, gkv2 = g_q.reshape(1, Q_RANK), g_kv.reshape(1, KV_RANK)
    return pl.pallas_call(
        body, name="mla_prep", grid=(s // tr,),
        in_specs=[pl.BlockSpec((tr, 256), lambda i: (i, OFF_CQ // 2)), pl.BlockSpec((tr, LANES), lambda i: (i, OFF_CKV)),
                  pl.BlockSpec((tr, LANES), lambda i: (i, OFF_MISC)), rowb(LANES), rowb(LANES),
                  full(gq2), full(gkv2), full(wq), full(wk), full(wv)],
        out_specs=[rowb(512), rowb(512), rowb(512), rowb(256), rowb(128)],
        out_shape=[jax.ShapeDtypeStruct((s, 512), BF16), jax.ShapeDtypeStruct((s, 512), BF16), jax.ShapeDtypeStruct((s, 512), BF16),
                   jax.ShapeDtypeStruct((s, 256), BF16), jax.ShapeDtypeStruct((s, 128), BF16)],
        compiler_params=_cparams(("parallel",)),
    )(proj, proj, proj, cos_m, sin_m, gq2, gkv2, wq, wk, wv)


def _mla_prep_bwd(dq, dk, dv, proj, cqn, ckvn, cos_m, sin_m, g_q, g_kv, wq, wk, wv):
    s = proj.shape[0]
    tr = _tile(s, 512)

    def body(dq_ref, dk_ref, dv_ref, cq_ref, ckv_ref, cqn_ref, ckvn_ref, cos_ref, sin_ref, gq_ref, gkv_ref,
             wq_ref, wk_ref, wv_ref, dcq_ref, dckv_ref, dkr_ref, dwq_ref, dwk_ref, dwv_ref, dgq_ref, dgkv_ref):
        @pl.when(pl.program_id(0) == 0)
        def _():
            for r in (dwq_ref, dwk_ref, dwv_ref, dgq_ref, dgkv_ref):
                r[...] = jnp.zeros_like(r)

        cos4 = jnp.tile(cos_ref[...], (1, 4))
        sin4 = jnp.tile(sin_ref[...], (1, 4))
        dqv = dq_ref[...]
        dzq = dqv * cos4 + _rot_mla_t(dqv * sin4)
        dkv_ = dk_ref[...]
        dzk = dkv_ * cos4 + _rot_mla_t(dkv_ * sin4)
        l = _lane((tr, LANES))
        in_rope = (l >= KR_LANE) & (l < KR_LANE + ROPE_DIM)
        dkr = dzk[:, 0:128] + dzk[:, 128:256] + dzk[:, 256:384] + dzk[:, 384:512]
        dkr_ref[...] = jnp.where(in_rope, dkr, 0.0)
        dzq_b = dzq.astype(BF16)
        dzk_b = dzk.astype(BF16)
        dv_b = dv_ref[...].astype(BF16)
        dcqn = _dot_nt(dzq_b, wq_ref[...])
        dckvn = _dot_nt(dzk_b, wk_ref[...]) + _dot_nt(dv_b, wv_ref[...])
        dwq_ref[...] += _dot_tn(cqn_ref[...], dzq_b)
        dwk_ref[...] += _dot_tn(ckvn_ref[...], dzk_b)
        dwv_ref[...] += _dot_tn(ckvn_ref[...], dv_b)
        dcq, gq_term = _rms_bwd(cq_ref[...], gq_ref[...], dcqn)
        dckv, gkv_term = _rms_bwd(ckv_ref[...], gkv_ref[...], dckvn)
        dcq_ref[...] = dcq.astype(BF16)
        dckv_ref[...] = dckv.astype(BF16)
        dgq_ref[...] += jnp.sum(gq_term, axis=0, keepdims=True)
        dgkv_ref[...] += jnp.sum(gkv_term, axis=0, keepdims=True)

    full = lambda shp: pl.BlockSpec(shp, lambda i: (0, 0))
    rowb = lambda w: pl.BlockSpec((tr, w), lambda i: (i, 0))
    gq2, gkv2 = g_q.reshape(1, Q_RANK), g_kv.reshape(1, KV_RANK)
    return pl.pallas_call(
        body, name="mla_prep_bwd", grid=(s // tr,),
        in_specs=[rowb(512), rowb(512), rowb(512),
                  pl.BlockSpec((tr, 256), lambda i: (i, OFF_CQ // 2)), pl.BlockSpec((tr, LANES), lambda i: (i, OFF_CKV)),
                  rowb(256), rowb(128), rowb(LANES), rowb(LANES), full((1, Q_RANK)), full((1, KV_RANK)),
                  full(wq.shape), full(wk.shape), full(wv.shape)],
        out_specs=[rowb(256), rowb(128), rowb(128), full(wq.shape), full(wk.shape), full(wv.shape),
                   full((1, Q_RANK)), full((1, KV_RANK))],
        out_shape=[jax.ShapeDtypeStruct((s, 256), BF16), jax.ShapeDtypeStruct((s, 128), BF16), jax.ShapeDtypeStruct((s, 128), F32),
                   jax.ShapeDtypeStruct(wq.shape, F32), jax.ShapeDtypeStruct(wk.shape, F32), jax.ShapeDtypeStruct(wv.shape, F32),
                   jax.ShapeDtypeStruct((1, Q_RANK), F32), jax.ShapeDtypeStruct((1, KV_RANK), F32)],
        compiler_params=_cparams(("arbitrary",)),
    )(dq, dk, dv, proj, proj, cqn, ckvn, cos_m, sin_m, gq2, gkv2, wq, wk, wv)


def _ret_prep(proj, cos_r, sin_r):
    s = proj.shape[0]
    tr = _tile(s, 512)

    def body(q_ref, k_ref, cos_ref, sin_ref, qo_ref, ko_ref):
        cos, sin = cos_ref[...], sin_ref[...]
        q, k = q_ref[...], k_ref[...]
        qo_ref[...] = (q * cos + _rot_ret(q) * sin).astype(BF16)
        ko_ref[...] = ((k * cos + _rot_ret(k) * sin) * (HEAD ** -0.5)).astype(BF16)

    rowb = pl.BlockSpec((tr, 256), lambda i: (i, 0))
    return pl.pallas_call(
        body, name="ret_prep", grid=(s // tr,),
        in_specs=[pl.BlockSpec((tr, 256), lambda i: (i, OFF_RQ // 2)), pl.BlockSpec((tr, 256), lambda i: (i, OFF_RK // 2)), rowb, rowb],
        out_specs=[rowb, rowb], out_shape=[jax.ShapeDtypeStruct((s, 256), BF16)] * 2,
        compiler_params=_cparams(("parallel",)),
    )(proj, proj, cos_r, sin_r)


def _ret_prep_bwd(dq, dk, cos_r, sin_r):
    s = dq.shape[0]
    tr = _tile(s, 512)

    def body(dq_ref, dk_ref, cos_ref, sin_ref, qo_ref, ko_ref):
        cos, sin = cos_ref[...], sin_ref[...]
        q, k = dq_ref[...], dk_ref[...] * (HEAD ** -0.5)
        qo_ref[...] = (q * cos + _rot_ret_t(q * sin)).astype(BF16)
        ko_ref[...] = (k * cos + _rot_ret_t(k * sin)).astype(BF16)

    rowb = pl.BlockSpec((tr, 256), lambda i: (i, 0))
    return pl.pallas_call(
        body, name="ret_prep_bwd", grid=(s // tr,), in_specs=[rowb] * 4, out_specs=[rowb, rowb],
        out_shape=[jax.ShapeDtypeStruct((s, 256), BF16)] * 2, compiler_params=_cparams(("parallel",)),
    )(dq, dk, cos_r, sin_r)


_LOG_GAMMA = [float(np.log1p(-np.float32(2.0) ** np.float32(-5.0 - h))) for h in range(N_HEADS)]
_MLA_SCALE = float((HEAD + ROPE_DIM) ** -0.5)
_QK_SCALE = float(HEAD ** -0.5)
KEY_BLOCKS = 4
QB = 512


def _split2(x):
    h = x.astype(BF16)
    return h, (x - h.astype(F32)).astype(BF16)


def _dot2(x, u):
    h, lo = _split2(x)
    return _dot(h, u) + _dot(lo, u)


def _head_pick(block, head, axis):
    idx = lax.broadcasted_iota(jnp.int32, block.shape, axis)
    return jnp.sum(jnp.where(idx == head, block, 0.0), axis=axis, keepdims=True)


def _log_gamma_of(head):
    lg = jnp.float32(_LOG_GAMMA[3])
    for h in (2, 1, 0):
        lg = jnp.where(head == h, jnp.float32(_LOG_GAMMA[h]), lg)
    return lg


def _mixer_specs(mode, s, q_off, k_off, v_off):
    nhb = 2
    bw = 2 * LANES if mode == "mla" else LANES
    nsub = KEY_BLOCKS if (s // TQ) % KEY_BLOCKS == 0 else 1
    q_spec = pl.BlockSpec((QB, bw), lambda p, i: (i, q_off + p))
    k_spec = pl.BlockSpec((s, bw), lambda p, i: (0, k_off + p))
    v_spec = pl.BlockSpec((s, bw), lambda p, i: (0, v_off + p))
    return nhb, N_HEADS // nhb, nsub, q_spec, k_spec, v_spec


def _mixer_geometry(mode, i, nsub):
    w = TQ * nsub
    row = lax.broadcasted_iota(jnp.int32, (QB, w), 0)
    col = lax.broadcasted_iota(jnp.int32, (QB, w), 1)
    nfull = (i * QB) // w
    dist = col - row
    if mode in ("fox", "sb"):
        rel = dist
    else:
        rel = col - (row | (CHUNK - 1))

    def visible(c):
        off = c * w - i * QB
        return (rel + off) < 0 if mode == "sb" else (rel + off) <= 0

    return nfull, dist, visible


class _SideJob:
    def __init__(self, inputs, out_shape, n_sems, sends, recvs):
        self.inputs, self.out_shape, self.n_sems, self.sends, self.recvs = list(inputs), list(out_shape), n_sems, sends, recvs


def _carry_side_job(body, n_in, n_out, side, n_steps):
    if side is None:
        return body
    si, so = len(side.inputs), len(side.out_shape)

    def at(corner):
        ok = pl.program_id(0) == corner[0]
        for d in range(1, len(n_steps)):
            ok = ok & (pl.program_id(d) == corner[d])
        return ok

    def wrapped(*refs):
        ins, s_ins = refs[:n_in], refs[n_in:n_in + si]
        outs, s_outs = refs[n_in + si:n_in + si + n_out], refs[n_in + si + n_out:n_in + si + n_out + so]
        scratch, send, recv = refs[n_in + si + n_out + so:-2], refs[-2], refs[-1]

        @pl.when(at([0] * len(n_steps)))
        def _():
            for cp in side.sends(s_ins, s_outs, send, recv):
                cp.start()

        body(*ins, *outs, *scratch)

        @pl.when(at([n - 1 for n in n_steps]))
        def _():
            for cp in side.recvs(s_ins, s_outs, send, recv):
                cp.wait_recv()
            for cp in side.sends(s_ins, s_outs, send, recv):
                cp.wait_send()

    return wrapped


def _side_specs(side):
    if side is None:
        return [], [], []
    hbm = pl.BlockSpec(memory_space=pl.ANY)
    return ([hbm] * len(side.inputs), [hbm] * len(side.out_shape),
            [pltpu.SemaphoreType.DMA((side.n_sems,)), pltpu.SemaphoreType.DMA((side.n_sems,))])


def _mixer_fwd(mode, qa, q_off, ka, k_off, va, v_off, *, cum_col=None, cum_row=None, side=None):
    s = qa.shape[0]
    nq = s // QB
    nhb, nblk, nsub, q_spec, k_spec, v_spec = _mixer_specs(mode, s, q_off, k_off, v_off)
    w = TQ * nsub
    softmax = mode in ("fox", "mla")

    def body(*refs):
        refs = list(refs)
        q_ref, k_ref, v_ref = refs[:3]
        refs = refs[3:]
        if mode == "fox":
            cc_ref, cr_ref = refs[:2]
            refs = refs[2:]
        o_ref = refs[0]
        st_ref = refs[1]
        p = pl.program_id(0)
        i = pl.program_id(1)
        nfull, dist, visible = _mixer_geometry(mode, i, nsub)
        lane = _lane((1, LANES))
        heads = [nhb * p + hh for hh in range(nhb)]
        wide = mode == "mla"
        q_scale = _QK_SCALE if mode in ("fox", "sb") else 1.0
        cols = [slice(hh * LANES, (hh + 1) * LANES) if wide else slice(None) for hh in range(nhb)]
        if wide:
            qs = [q_ref[:, cols[hh]] for hh in range(nhb)]
        else:
            qf = q_ref[...].astype(F32) * q_scale
            qs = [jnp.where((lane // HEAD) == hh, qf, 0.0).astype(BF16) for hh in range(nhb)]
        if mode == "fox":
            cqs = [_head_pick(cc_ref[...], h, 1) for h in heads]
        if mode == "sb":
            r1 = lax.broadcasted_iota(jnp.int32, (TQ, TQ), 0)
            c1 = lax.broadcasted_iota(jnp.int32, (TQ, TQ), 1)
            u_after = (r1 > c1).astype(BF16)

        def chunk(c):
            return pl.ds(pl.multiple_of(c * w, w), w)

        def scores(c):
            js = chunk(c)
            return tuple(_dot_nt(qs[hh], k_ref[js, cols[hh]]) for hh in range(nhb))

        def head_step(hh, c, js, sc, vj, carry, last):
            if softmax:
                m, l, acc = carry
                if mode == "fox":
                    ck = _head_pick(cr_ref[:, js], heads[hh], 0)
                    sc = sc + (cqs[hh] - ck)
                else:
                    sc = sc * _MLA_SCALE
                if last:
                    sc = jnp.where(visible(c), sc, NEG)
                m_new = jnp.maximum(m, jnp.max(sc, axis=-1, keepdims=True))
                alpha = jnp.exp(m - m_new)
                pr = jnp.exp(sc - m_new)
                l = alpha * l + jnp.sum(pr, axis=-1, keepdims=True)
                acc = alpha * acc + _dot(pr.astype(BF16), vj)
                return m_new, l, acc
            run, acc = carry
            z = sc
            log_beta = jnp.minimum(z, 0.0) - jnp.log(1.0 + jnp.exp(-jnp.abs(z)))
            log_stay = log_beta - z
            if last:
                vis = visible(c)
                log_stay = jnp.where(vis, log_stay, 0.0)
            parts = [None] * nsub
            for b in reversed(range(nsub)):
                ls_b = log_stay[:, b * TQ:(b + 1) * TQ]
                parts[b] = _dot2(ls_b, u_after) + run
                run = run + jnp.sum(ls_b, axis=-1, keepdims=True)
            later = parts[0] if nsub == 1 else jnp.concatenate(parts, axis=1)
            wgt = jnp.exp(log_beta + later)
            if last:
                wgt = jnp.where(vis, wgt, 0.0)
            return run, acc + _dot(wgt.astype(BF16), vj)

        def step(c, c_next, state, last):
            scs, carries = state
            nxt = scores(c_next) if c_next is not None else None
            js = chunk(c)
            return nxt, tuple(head_step(hh, c, js, scs[hh], v_ref[js, cols[hh]], carries[hh], last) for hh in range(nhb))

        zero_acc = jnp.zeros((QB, LANES), F32)
        zero1 = jnp.zeros((QB, 1), F32)
        if softmax:
            init = tuple((jnp.full((QB, 1), NEG, F32), zero1, zero_acc) for _ in range(nhb))
        else:
            init = tuple((zero1, zero_acc) for _ in range(nhb))
        if mode == "sb":
            state = step(nfull, jnp.maximum(nfull - 1, 0), (scores(nfull), init), True)
            _, carries = lax.fori_loop(0, nfull, lambda t, st: step(nfull - 1 - t, jnp.maximum(nfull - 2 - t, 0), st, False), state)
        else:
            state = lax.fori_loop(0, nfull, lambda c, st: step(c, c + 1, st, False), (scores(0), init))
            _, carries = step(nfull, None, state, True)
        if softmax:
            outs = [acc / l for (m, l, acc) in carries]
            stats = [m + jnp.log(l) for (m, l, acc) in carries]
        else:
            outs, stats = [acc for (run, acc) in carries], [run for (run, acc) in carries]
        hm0 = (lane // HEAD) == 0
        pick = lambda a: jnp.where(hm0, a[0], a[1])
        if wide:
            for hh in range(nhb):
                o_ref[:, cols[hh]] = outs[hh]
        else:
            o_ref[...] = pick(outs)
        st_ref[0] = pick(stats)

    in_specs = [q_spec, k_spec, v_spec]
    args = [qa, ka, va]
    if mode == "fox":
        in_specs += [pl.BlockSpec((QB, LANES), lambda p, i: (i, 0)), pl.BlockSpec((8, s), lambda p, i: (0, 0))]
        args += [cum_col, cum_row]
    bw = 2 * LANES if mode == "mla" else LANES
    out_specs = [pl.BlockSpec((QB, bw), lambda p, i: (i, p))]
    out_shape = [jax.ShapeDtypeStruct((s, nblk * bw), F32)]
    out_specs.append(pl.BlockSpec((1, QB, LANES), lambda p, i: (p, i, 0)))
    out_shape.append(jax.ShapeDtypeStruct((nblk, s, LANES), F32))
    side_in, side_out, side_scratch = _side_specs(side)
    res = pl.pallas_call(
        _carry_side_job(body, len(args), len(out_shape), side, (nblk, nq)), name=mode + "_fwd", grid=(nblk, nq),
        in_specs=in_specs + side_in, out_specs=out_specs + side_out,
        out_shape=out_shape + ([] if side is None else side.out_shape), scratch_shapes=side_scratch,
        compiler_params=_cparams(("parallel", "parallel") if side is None else ("arbitrary", "arbitrary")),
    )(*args, *([] if side is None else side.inputs))
    return (res[0], res[1]) if side is None else (res[0], res[1], res[2:])


def _mixer_bwd(mode, qa, q_off, ka, k_off, va, v_off, o, do, *, stat=None, cum_col=None, cum_row=None, side=None):
    s = qa.shape[0]
    nq = s // QB
    nhb, nblk, nsub, q_spec, k_spec, v_spec = _mixer_specs(mode, s, q_off, k_off, v_off)
    w = TQ * nsub
    softmax = mode in ("fox", "mla")

    def body(*refs):
        refs = list(refs)
        q_ref, k_ref, v_ref, o_ref, do_ref = refs[:5]
        refs = refs[5:]
        st_ref = refs[0]
        refs = refs[1:]
        if mode == "fox":
            cc_ref, cr_ref = refs[:2]
            refs = refs[2:]
        dq_ref, dk_ref, dv_ref = refs[:3]
        dck_ref, drs_ref = refs[3:5] if mode == "fox" else (None, None)
        p = pl.program_id(0)
        i = pl.program_id(1)

        @pl.when(i == 0)
        def _():
            dk_ref[...] = jnp.zeros_like(dk_ref)
            dv_ref[...] = jnp.zeros_like(dv_ref)
            if mode == "fox":
                dck_ref[...] = jnp.zeros_like(dck_ref)

        nfull, dist, visible = _mixer_geometry(mode, i, nsub)
        lane = _lane((1, LANES))
        heads = [nhb * p + hh for hh in range(nhb)]
        dov = do_ref[...]
        wide = mode == "mla"
        q_scale = _QK_SCALE if mode in ("fox", "sb") else 1.0
        cols = [slice(hh * LANES, (hh + 1) * LANES) if wide else slice(None) for hh in range(nhb)]
        if wide:
            prod = dov * o_ref[...]
            qs = [q_ref[:, cols[hh]] for hh in range(nhb)]
            dos = [dov[:, cols[hh]].astype(BF16) for hh in range(nhb)]
            deltas = [jnp.sum(prod[:, cols[hh]], axis=-1, keepdims=True) for hh in range(nhb)]
        else:
            qf = q_ref[...].astype(F32) * q_scale
            prod = dov * o_ref[...]
            hms = [(lane // HEAD) == hh for hh in range(nhb)]
            qs = [jnp.where(hm, qf, 0.0).astype(BF16) for hm in hms]
            dos = [jnp.where(hm, dov, 0.0).astype(BF16) for hm in hms]
            deltas = [jnp.sum(jnp.where(hm, prod, 0.0), axis=-1, keepdims=True) for hm in hms]
        st = st_ref[0]
        stats = [st[:, hh * HEAD:hh * HEAD + 1] for hh in range(nhb)]
        if mode == "fox":
            cqs = [_head_pick(cc_ref[...], h, 1) for h in heads]
        if mode == "sb":
            r1 = lax.broadcasted_iota(jnp.int32, (TQ, TQ), 0)
            c1 = lax.broadcasted_iota(jnp.int32, (TQ, TQ), 1)
            u_upto = (r1 <= c1).astype(BF16)
            u_before = (r1 < c1).astype(BF16)

        def chunk(c):
            return pl.ds(pl.multiple_of(c * w, w), w)

        def scores(c):
            js = chunk(c)
            if mode == "sb":
                return tuple((_dot_nt(qs[hh], k_ref[js, cols[hh]]), None) for hh in range(nhb))
            return tuple((_dot_nt(qs[hh], k_ref[js, cols[hh]]), _dot_nt(dos[hh], v_ref[js, cols[hh]])) for hh in range(nhb))

        def emit(hh, js, ds_b, pr_b, dq):
            dk_ref[js, cols[hh]] += _dot_tn(ds_b, qs[hh])
            dv_ref[js, cols[hh]] += _dot_tn(pr_b, dos[hh])
            return dq + _dot(ds_b, k_ref[js, cols[hh]])

        def head_step(hh, c, js, sc_dp, carry, last):
            sc, dp = sc_dp
            if dp is None:
                dp = _dot_nt(dos[hh], v_ref[js, cols[hh]])
            if softmax:
                dq, rsum = carry
                if mode == "fox":
                    ck = _head_pick(cr_ref[:, js], heads[hh], 0)
                    sc = sc + (cqs[hh] - ck)
                else:
                    sc = sc * _MLA_SCALE
                if last:
                    sc = jnp.where(visible(c), sc, NEG)
                pr = jnp.exp(sc - stats[hh])
                ds = pr * (dp - deltas[hh])
                if mode == "fox":
                    dck_ref[0, hh:hh + 1, js] += jnp.sum(ds, axis=0, keepdims=True)
                    rsum = rsum + jnp.sum(ds, axis=-1, keepdims=True)
                if mode == "mla":
                    ds = ds * _MLA_SCALE
                return emit(hh, js, ds.astype(BF16), pr.astype(BF16), dq), rsum
            seen, gsum, dq = carry
            z = sc
            log_beta = jnp.minimum(z, 0.0) - jnp.log(1.0 + jnp.exp(-jnp.abs(z)))
            log_stay = log_beta - z
            if last:
                vis = visible(c)
                log_stay = jnp.where(vis, log_stay, 0.0)
            parts = []
            for b in range(nsub):
                ls_b = log_stay[:, b * TQ:(b + 1) * TQ]
                parts.append((stats[hh] - seen) - _dot2(ls_b, u_upto))
                seen = seen + jnp.sum(ls_b, axis=-1, keepdims=True)
            later = parts[0] if nsub == 1 else jnp.concatenate(parts, axis=1)
            wgt = jnp.exp(log_beta + later)
            if last:
                wgt = jnp.where(vis, wgt, 0.0)
            g = dp * wgt
            parts = []
            for b in range(nsub):
                g_b = g[:, b * TQ:(b + 1) * TQ]
                parts.append(gsum + _dot2(g_b, u_before))
                gsum = gsum + jnp.sum(g_b, axis=-1, keepdims=True)
            before = parts[0] if nsub == 1 else jnp.concatenate(parts, axis=1)
            beta = jnp.exp(log_beta)
            dz = g * (1.0 - beta) - beta * before
            if last:
                dz = jnp.where(vis, dz, 0.0)
            return seen, gsum, emit(hh, js, dz.astype(BF16), wgt.astype(BF16), dq)

        def step(c, c_next, state, last):
            scs, carries = state
            nxt = scores(c_next) if c_next is not None else None
            js = chunk(c)
            return nxt, tuple(head_step(hh, c, js, scs[hh], carries[hh], last) for hh in range(nhb))

        zero_acc = jnp.zeros((QB, LANES), F32)
        zero1 = jnp.zeros((QB, 1), F32)
        if softmax:
            init = tuple((zero_acc, zero1) for _ in range(nhb))
        else:
            init = tuple((zero1, zero1, zero_acc) for _ in range(nhb))
        state = lax.fori_loop(0, nfull, lambda c, st: step(c, c + 1, st, False), (scores(0), init))
        _, carries = step(nfull, None, state, True)
        if softmax:
            dqs = [dq for (dq, rsum) in carries]
        else:
            dqs = [dq for (seen, gsum, dq) in carries]
        hm0 = (lane // HEAD) == 0
        if wide:
            for hh in range(nhb):
                dq_ref[:, cols[hh]] = dqs[hh]
        else:
            dq_ref[...] = jnp.where(hm0, dqs[0], dqs[1]) * q_scale
        if mode == "fox":
            drs_ref[0] = jnp.where(hm0, carries[0][1], carries[1][1])

    bw = 2 * LANES if mode == "mla" else LANES
    pair_blk = pl.BlockSpec((QB, bw), lambda p, i: (i, p))
    full_blk = pl.BlockSpec((s, bw), lambda p, i: (0, p))
    stat_blk = pl.BlockSpec((1, QB, LANES), lambda p, i: (p, i, 0))
    in_specs = [q_spec, k_spec, v_spec, pair_blk, pair_blk]
    args = [qa, ka, va, o, do]
    in_specs.append(stat_blk)
    args.append(stat)
    if mode == "fox":
        in_specs += [pl.BlockSpec((QB, LANES), lambda p, i: (i, 0)), pl.BlockSpec((8, s), lambda p, i: (0, 0))]
        args += [cum_col, cum_row]
    out_specs = [pair_blk, full_blk, full_blk]
    out_shape = [jax.ShapeDtypeStruct((s, nblk * bw), F32)] * 3
    if mode == "fox":
        out_specs += [pl.BlockSpec((1, 8, s), lambda p, i: (p, 0, 0)), stat_blk]
        out_shape += [jax.ShapeDtypeStruct((2, 8, s), F32), jax.ShapeDtypeStruct((2, s, LANES), F32)]
    side_in, side_out, side_scratch = _side_specs(side)
    res = pl.pallas_call(
        _carry_side_job(body, len(args), len(out_shape), side, (nblk, nq)), name=mode + "_bwd", grid=(nblk, nq),
        in_specs=in_specs + side_in, out_specs=out_specs + side_out,
        out_shape=out_shape + ([] if side is None else side.out_shape), scratch_shapes=side_scratch,
        compiler_params=_cparams(("parallel", "arbitrary") if side is None else ("arbitrary", "arbitrary")),
    )(*args, *([] if side is None else side.inputs))
    return res if side is None else (*res[:len(out_shape)], res[len(out_shape):])


def _ret_geometry(p):
    lane = _lane((1, LANES))
    lg_lane = jnp.where(lane < HEAD, _log_gamma_of(2 * p), _log_gamma_of(2 * p + 1))
    a = lax.broadcasted_iota(jnp.int32, (TQ, 1), 0).astype(F32)
    row = lax.broadcasted_iota(jnp.int32, (TQ, TQ), 0)
    col = lax.broadcasted_iota(jnp.int32, (TQ, TQ), 1)
    same_chunk_or_earlier = (col // CHUNK) <= (row // CHUNK)
    gap = jnp.abs(row - col).astype(F32)
    decays = [jnp.where(same_chunk_or_earlier, jnp.exp(_log_gamma_of(2 * p + hh) * gap), 0.0) for hh in range(2)]
    r = lax.broadcasted_iota(jnp.int32, (LANES, LANES), 0)
    c = lax.broadcasted_iota(jnp.int32, (LANES, LANES), 1)
    own_head = (r // HEAD) == (c // HEAD)
    return lane, lg_lane, a, decays, own_head


def _ret_fwd(qa, ka, va, v_off):
    s = qa.shape[0]
    nq = s // TQ

    def body(q_ref, k_ref, v_ref, o_ref, st_ref, state):
        p = pl.program_id(0)

        @pl.when(pl.program_id(1) == 0)
        def _():
            state[...] = jnp.zeros_like(state)

        lane, lg_lane, a, decays, own_head = _ret_geometry(p)
        q = q_ref[...].astype(F32)
        k = k_ref[...]
        v = v_ref[...]
        s_in = state[...]
        st_ref[0, 0] = s_in
        out = _dot((q * jnp.exp(lg_lane * (a + 1.0))).astype(BF16), s_in.astype(BF16))
        for hh in range(2):
            hm = (lane // HEAD) == hh
            qh = jnp.where(hm, q, 0.0).astype(BF16)
            inner = _dot((_dot_nt(qh, k) * decays[hh]).astype(BF16), v)
            out = out + jnp.where(hm, inner, 0.0)
        o_ref[...] = out
        k_tail = (k.astype(F32) * jnp.exp(lg_lane * (TQ - 1.0 - a))).astype(BF16)
        state[...] = jnp.exp(lg_lane * float(TQ)) * s_in + jnp.where(own_head, _dot_tn(k_tail, v), 0.0)

    blk = lambda off: pl.BlockSpec((TQ, LANES), lambda p, i: (i, off + p))
    return pl.pallas_call(
        body, name="ret_fwd", grid=(2, nq), in_specs=[blk(0), blk(0), blk(v_off)],
        out_specs=[blk(0), pl.BlockSpec((1, 1, LANES, LANES), lambda p, i: (p, i, 0, 0))],
        out_shape=[jax.ShapeDtypeStruct((s, 2 * LANES), F32), jax.ShapeDtypeStruct((2, nq, LANES, LANES), F32)],
        scratch_shapes=[pltpu.VMEM((LANES, LANES), F32)],
        compiler_params=_cparams(("parallel", "arbitrary")),
    )(qa, ka, va)


def _ret_bwd(qa, ka, va, v_off, states, do):
    s = qa.shape[0]
    nq = s // TQ

    def body(q_ref, k_ref, v_ref, st_ref, do_ref, dq_ref, dk_ref, dv_ref, dstate):
        p = pl.program_id(0)

        @pl.when(pl.program_id(1) == 0)
        def _():
            dstate[...] = jnp.zeros_like(dstate)

        lane, lg_lane, a, decays, own_head = _ret_geometry(p)
        q = q_ref[...].astype(F32)
        k = k_ref[...]
        kf = k.astype(F32)
        v = v_ref[...]
        dov = do_ref[...]
        s_in = st_ref[0, 0].astype(BF16)
        ds_next = dstate[...]
        ds_b = ds_next.astype(BF16)
        head_decay = jnp.exp(lg_lane * (a + 1.0))
        tail_decay = jnp.exp(lg_lane * (TQ - 1.0 - a))
        k_tail = (kf * tail_decay).astype(BF16)
        dq = _dot_nt(dov.astype(BF16), s_in) * head_decay
        dk = _dot_nt(v, ds_b) * tail_decay
        dv = _dot(k_tail, ds_b)
        for hh in range(2):
            hm = (lane // HEAD) == hh
            qh = jnp.where(hm, q, 0.0).astype(BF16)
            doh = jnp.where(hm, dov, 0.0).astype(BF16)
            att = (_dot_nt(qh, k) * decays[hh]).astype(BF16)
            datt = (_dot_nt(doh, v) * decays[hh]).astype(BF16)
            dv = dv + _dot_tn(att, doh)
            dk = dk + _dot_tn(datt, qh)
            dq = dq + jnp.where(hm, _dot(datt, k), 0.0)
        dq_ref[...] = dq
        dk_ref[...] = dk
        dv_ref[...] = dv
        q_head = (q * head_decay).astype(BF16)
        dstate[...] = jnp.exp(lg_lane * float(TQ)) * ds_next + jnp.where(own_head, _dot_tn(q_head, dov.astype(BF16)), 0.0)

    blk = lambda off: pl.BlockSpec((TQ, LANES), lambda p, i: (nq - 1 - i, off + p))
    return pl.pallas_call(
        body, name="ret_bwd", grid=(2, nq),
        in_specs=[blk(0), blk(0), blk(v_off), pl.BlockSpec((1, 1, LANES, LANES), lambda p, i: (p, nq - 1 - i, 0, 0)), blk(0)],
        out_specs=[blk(0)] * 3, out_shape=[jax.ShapeDtypeStruct((s, 2 * LANES), F32)] * 3,
        scratch_shapes=[pltpu.VMEM((LANES, LANES), F32)],
        compiler_params=_cparams(("parallel", "arbitrary")),
    )(qa, ka, va, states, do)


def _seg_mean_matrix():
    r = lax.broadcasted_iota(jnp.int32, (GROUP, GROUP), 0)
    c = lax.broadcasted_iota(jnp.int32, (GROUP, GROUP), 1)
    return jnp.where((r // HEAD) == (c // HEAD), 1.0 / HEAD, 0.0).astype(BF16)


def _seg_mean(x, seg):
    h = x.astype(BF16)
    r = x - h.astype(F32)
    m = r.astype(BF16)
    lo = (r - m.astype(F32)).astype(BF16)
    return _dot(h, seg) + _dot(m, seg) + _dot(lo, seg)


def _sigmoid(x):
    return 1.0 / (1.0 + jnp.exp(-x))


def _mix_post(oa, ob, oc, od, proj, g):
    s = oa.shape[0]
    tr = _tile(s, 256)

    def body(a_ref, b_ref, c_ref, d_ref, rg_ref, g_ref, o_ref):
        gv = g_ref[...]
        o_ref[:, 0:GROUP] = _rms(a_ref[...], gv[:, 0:GROUP]).astype(BF16)
        o_ref[:, GROUP:2 * GROUP] = _rms(b_ref[...], gv[:, GROUP:2 * GROUP]).astype(BF16)
        seg = _seg_mean_matrix()
        c = c_ref[...]
        cen = c - _seg_mean(c, seg)
        n = cen * lax.rsqrt(_seg_mean(cen * cen, seg) + EPS)
        rg = rg_ref[...]
        o_ref[:, 2 * GROUP:3 * GROUP] = (n * gv[:, 2 * GROUP:3 * GROUP] * (rg * _sigmoid(rg))).astype(BF16)
        o_ref[:, 3 * GROUP:] = _rms(d_ref[...], gv[:, 3 * GROUP:]).astype(BF16)

    blk = pl.BlockSpec((tr, GROUP), lambda i: (i, 0))
    return pl.pallas_call(
        body, name="mix_post", grid=(s // tr,),
        in_specs=[blk] * 4 + [pl.BlockSpec((tr, GROUP), lambda i: (i, OFF_RG // 2)), pl.BlockSpec((1, D_MODEL), lambda i: (0, 0))],
        out_specs=pl.BlockSpec((tr, D_MODEL), lambda i: (i, 0)), out_shape=jax.ShapeDtypeStruct((s, D_MODEL), BF16),
        compiler_params=_cparams(("parallel",)),
    )(oa, ob, oc, od, proj, g.reshape(1, D_MODEL))


def _mix_post_bwd(dmixed, oa, ob, oc, od, proj, g):
    s = oa.shape[0]
    tr = _tile(s, 256)

    def body(dm_ref, a_ref, b_ref, c_ref, d_ref, rg_ref, g_ref, da_ref, db_ref, dc_ref, dd_ref, drg_ref, dg_ref):
        @pl.when(pl.program_id(0) == 0)
        def _():
            dg_ref[...] = jnp.zeros_like(dg_ref)

        gv = g_ref[...]
        dm = dm_ref[...]
        for k, (x_ref, dx_ref) in enumerate(((a_ref, da_ref), (b_ref, db_ref), (None, None), (d_ref, dd_ref))):
            if x_ref is None:
                continue
            cols = slice(k * GROUP, (k + 1) * GROUP)
            dx, gterm = _rms_bwd(x_ref[...], gv[:, cols], dm[:, cols])
            dx_ref[...] = dx
            dg_ref[:, cols] += jnp.sum(gterm, axis=0, keepdims=True)
        cols = slice(2 * GROUP, 3 * GROUP)
        seg = _seg_mean_matrix()
        c = c_ref[...]
        cen = c - _seg_mean(c, seg)
        rstd = lax.rsqrt(_seg_mean(cen * cen, seg) + EPS)
        n = cen * rstd
        rg = rg_ref[...]
        sg = _sigmoid(rg)
        gate = rg * sg
        dy = dm[:, cols]
        gc = gv[:, cols]
        dn = dy * gc * gate
        dg_ref[:, cols] += jnp.sum(dy * n * gate, axis=0, keepdims=True)
        drg_ref[...] = (dy * n * gc * (sg * (1.0 + rg * (1.0 - sg)))).astype(BF16)
        dc_ref[...] = rstd * (dn - _seg_mean(dn, seg) - n * _seg_mean(dn * n, seg))

    blk = pl.BlockSpec((tr, GROUP), lambda i: (i, 0))
    gsp = pl.BlockSpec((1, D_MODEL), lambda i: (0, 0))
    return pl.pallas_call(
        body, name="mix_post_bwd", grid=(s // tr,),
        in_specs=[pl.BlockSpec((tr, D_MODEL), lambda i: (i, 0))] + [blk] * 4 + [pl.BlockSpec((tr, GROUP), lambda i: (i, OFF_RG // 2)), gsp],
        out_specs=[blk] * 5 + [gsp],
        out_shape=[jax.ShapeDtypeStruct((s, GROUP), F32)] * 4 + [jax.ShapeDtypeStruct((s, GROUP), BF16), jax.ShapeDtypeStruct((1, D_MODEL), F32)],
        compiler_params=_cparams(("arbitrary",)),
    )(dmixed, oa, ob, oc, od, proj, g.reshape(1, D_MODEL))


def _pack_w_in(w):
    z = lambda n: jnp.zeros((w.shape[0], n), w.dtype)
    misc = jnp.concatenate([w[:, 768:772], z(KR_LANE - N_HEADS), w[:, 1156:1188], z(LANES - KR_LANE - ROPE_DIM)], axis=1)
    return jnp.concatenate([w[:, 0:768], w[:, 772:1028], w[:, 1188:2980], w[:, 1028:1156], misc], axis=1)


def _unpack_dw_in(d):
    m = OFF_MISC * LANES
    return jnp.concatenate([d[:, 0:768], d[:, m:m + N_HEADS], d[:, 768:1024], d[:, OFF_CKV * LANES:m],
                            d[:, m + KR_LANE:m + KR_LANE + ROPE_DIM], d[:, 1024:OFF_CKV * LANES]], axis=1)


def _pack_w_q(w):
    return jnp.pad(w.reshape(Q_RANK, N_HEADS, HEAD + ROPE_DIM), ((0, 0), (0, 0), (0, LANES - HEAD - ROPE_DIM))).reshape(Q_RANK, 4 * LANES)


def _unpack_dw_q(d):
    return d.reshape(Q_RANK, N_HEADS, LANES)[:, :, :HEAD + ROPE_DIM].reshape(Q_RANK, N_HEADS * (HEAD + ROPE_DIM))


def _pack_w_kv(w):
    w4 = w.reshape(KV_RANK, N_HEADS, 2 * HEAD)
    widen = lambda a: jnp.pad(a, ((0, 0), (0, 0), (0, LANES - HEAD))).reshape(KV_RANK, N_HEADS * LANES)
    return widen(w4[:, :, :HEAD]), widen(w4[:, :, HEAD:])


def _unpack_dw_kv(dk, dv):
    narrow = lambda a: a.reshape(KV_RANK, N_HEADS, LANES)[:, :, :HEAD]
    return jnp.concatenate([narrow(dk), narrow(dv)], axis=2).reshape(KV_RANK, 2 * N_HEADS * HEAD)


def _narrow_heads(a):
    return a.reshape(a.shape[0], N_HEADS, LANES)[:, :, :HEAD].reshape(a.shape[0], N_HEADS * HEAD)


def _widen_heads(a):
    return jnp.pad(a.reshape(a.shape[0], N_HEADS, HEAD), ((0, 0), (0, 0), (0, LANES - HEAD))).reshape(a.shape[0], N_HEADS * LANES)


def _layer_fwd(x, lw, tabs, tag, side=None, fox_side=None, late_weights=None, h1=None, next_gain=None):
    cos_m, sin_m, cos_r, sin_r = tabs
    if h1 is None:
        h1 = _norm_fwd(x, lw["g_mix_pre"], name=tag + "pre_norm")
    proj, projb = _matmul(h1, lw["w_in"], name=tag + "in_proj", also_bf16=True)
    bias_row = jnp.pad(lw["b_forget"], (FF_LANE, LANES - N_HEADS - FF_LANE)).reshape(1, LANES)
    cum_col, cum_row = _fox_cum(proj, bias_row)
    oa, lse_a, *fox_carried = _mixer_fwd("fox", projb, OFF_FQ, projb, OFF_FK, projb, OFF_FV, cum_col=cum_col, cum_row=cum_row,
                                         side=fox_side)
    if late_weights is not None:
        lw = {**lw, **late_weights(fox_carried[0])}
    qm, km, vm, cqn, ckvn = _mla_prep(proj, cos_m, sin_m, lw["g_q_lora"], lw["g_kv_lora"], lw["wq"], lw["wk"], lw["wv"])
    ob_wide, lse_b = _mixer_fwd("mla", qm, 0, km, 0, vm, 0)
    ob = _narrow_heads(ob_wide)
    qr, kr = _ret_prep(proj, cos_r, sin_r)
    oc, ret_states = _ret_fwd(qr, kr, projb, OFF_RV)
    od, tot_d, *carried = _mixer_fwd("sb", projb, OFF_SQ, projb, OFF_SK, projb, OFF_SV, side=side)
    mixed = _mix_post(oa, ob, oc, od, proj, lw["g_mix_out"])
    mix = _matmul(mixed, lw["w_out"], name=tag + "out_proj")
    x1, h2 = _norm_fwd(mix, lw["g_mix_post"], name=tag + "mix_post_norm", resid=x, out_dtype=F32, next_gain=lw["g_ffn_pre"])
    u = _matmul(h2, lw["w_ffn_up"], name=tag + "ffn_up", relu2=True, out_dtype=BF16, col_blocks=True)
    f = _matmul(u, lw["w_ffn_down"], name=tag + "ffn_down")
    x2, h_next = None, None
    if next_gain is not None:
        x2, h_next = _norm_fwd(f, lw["g_ffn_post"], name=tag + "ffn_post_norm", resid=x1, out_dtype=F32, next_gain=next_gain)
    saved = dict(x=x, h1=h1, proj=proj, projb=projb, bias_row=bias_row, cum_col=cum_col, cum_row=cum_row, oa=oa, lse_a=lse_a,
                 qm=qm, km=km, vm=vm, cqn=cqn, ckvn=ckvn, ob=ob, ob_wide=ob_wide, lse_b=lse_b, qr=qr, kr=kr, ret_states=ret_states, oc=oc, od=od, tot_d=tot_d, mixed=mixed,
                 mix=mix, x1=x1, h2=h2, u=u, f=f)
    return x2, saved, lw, (carried[0] if carried else None), h_next


def _layer_bwd(dx2, lw, sv, tabs, tag, side=None, ffn_side=None, fox_side=None, post_given=None, then_prev=None):
    cos_m, sin_m, cos_r, sin_r = tabs
    g = {}
    if post_given is None:
        df, g["g_ffn_post"] = _norm_bwd(sv["f"], lw["g_ffn_post"], dx2, name=tag + "ffn_post_norm_bwd", out_dtype=BF16)
    else:
        df, g["g_ffn_post"] = post_given
    du_pre = _matmul(df, lw["w_ffn_down"], name=tag + "ffn_down_dx", tb=True, out_dtype=BF16, relu2_of=sv["u"], side=ffn_side)
    ffn_carried = None
    if ffn_side is not None:
        du_pre, ffn_carried = du_pre
    g["w_ffn_down"] = _matmul(sv["u"], df, name=tag + "ffn_down_dw", ta=True)
    dh2 = _matmul(du_pre, lw["w_ffn_up"], name=tag + "ffn_up_dx", tb=True, col_blocks=True)
    g["w_ffn_up"] = _matmul(sv["h2"], du_pre, name=tag + "ffn_up_dw", ta=True, col_blocks=True)
    dx1, g["g_ffn_pre"], dmix, g["g_mix_post"] = _norm_bwd(sv["x1"], lw["g_ffn_pre"], dh2, name=tag + "ffn_pre_norm_bwd", add=dx2,
                                                           then=(sv["mix"], lw["g_mix_post"]))
    dmixed = _matmul(dmix, lw["w_out"], name=tag + "out_proj_dx", tb=True)
    g["w_out"] = _matmul(sv["mixed"], dmix, name=tag + "out_proj_dw", ta=True)
    proj, projb = sv["proj"], sv["projb"]
    doa, dob, doc, dod, drg, g["g_mix_out"] = _mix_post_bwd(dmixed, sv["oa"], sv["ob"], sv["oc"], sv["od"], proj, lw["g_mix_out"])
    dfq, dfk, dfv, dck, drs, *fox_carried = _mixer_bwd(
        "fox", projb, OFF_FQ, projb, OFF_FK, projb, OFF_FV, sv["oa"], doa, stat=sv["lse_a"], cum_col=sv["cum_col"],
        cum_row=sv["cum_row"], side=None if fox_side is None else fox_side(g))
    dqm, dkm, dvm = _mixer_bwd("mla", sv["qm"], 0, sv["km"], 0, sv["vm"], 0, sv["ob_wide"], _widen_heads(dob), stat=sv["lse_b"])
    dcq, dckv, dkr, dwq, dwk, dwv, g["g_q_lora"], g["g_kv_lora"] = _mla_prep_bwd(
        dqm, dkm, dvm, proj, sv["cqn"], sv["ckvn"], cos_m, sin_m, lw["g_q_lora"], lw["g_kv_lora"], lw["wq"], lw["wk"], lw["wv"])
    dqr, dkr_ret, drv = _ret_bwd(sv["qr"], sv["kr"], projb, OFF_RV, sv["ret_states"], doc)
    drq, drk = _ret_prep_bwd(dqr, dkr_ret, cos_r, sin_r)
    if callable(side):
        side = side(g, ffn_carried, fox_carried[0] if fox_carried else None)
    dsq, dsk, dsv, *carried = _mixer_bwd("sb", projb, OFF_SQ, projb, OFF_SK, projb, OFF_SV, sv["od"], dod, stat=sv["tot_d"], side=side)
    dmisc, db_row = _fox_gate_bwd(dck, drs, proj, sv["bias_row"], dkr)
    b = lambda a: a.astype(BF16)
    dproj = jnp.concatenate([b(dfq), b(dfk), b(dfv), dcq, drq, drk, b(drv), drg, b(dsq), b(dsk), b(dsv), dckv, dmisc], axis=1)
    dh1 = _matmul(dproj, lw["w_in"], name=tag + "in_proj_dx", tb=True)
    g["w_in"] = _matmul(sv["h1"], dproj, name=tag + "in_proj_dw", ta=True)
    dx, g["g_mix_pre"], *prev_post = _norm_bwd(sv["x"], lw["g_mix_pre"], dh1, name=tag + "pre_norm_bwd", add=dx1, then=then_prev)
    g["b_forget"] = db_row[0, FF_LANE:FF_LANE + N_HEADS]
    g["wq"], g["wk"], g["wv"] = dwq, dwk, dwv
    return dx, g, (carried[0] if carried else None), (tuple(prev_post) if prev_post else None)


def _local_step(x, positions, layers, target):
    s = x.shape[0]
    tabs = _rope_tables(positions.reshape(s, 1))
    saved, h1 = [], None
    for li, lw in enumerate(layers):
        nxt = layers[li + 1]["g_mix_pre"] if li + 1 < len(layers) else None
        x, sv, _, _, h1 = _layer_fwd(x, lw, tabs, "l%d_" % li, h1=h1, next_gain=nxt)
        saved.append(sv)
    loss_row, dx, df, dg = _loss_head(saved[-1]["f"], layers[-1]["g_ffn_post"], saved[-1]["x1"], target)
    grads, post = [None] * len(layers), (df, dg)
    for li in reversed(range(len(layers))):
        prev = (saved[li - 1]["f"], layers[li - 1]["g_ffn_post"]) if li > 0 else None
        dx, grads[li], _, post = _layer_bwd(dx, layers[li], saved[li], tabs, "l%d_" % li, post_given=post, then_prev=prev)
    return loss_row[0, 0], dx, grads


def _adamw(w, g, m, v, *, name):
    d, r, c = w.shape
    tr = 256 if r % 256 == 0 else r
    blk = pl.BlockSpec((None, tr, c), lambda l, i: (l, i, 0))
    c1 = 1.0 - ADAM_B1 ** ADAM_STEP
    c2 = 1.0 - ADAM_B2 ** ADAM_STEP

    def body(w_ref, g_ref, m_ref, v_ref, d_ref, mo_ref, vo_ref):
        gv = g_ref[...]
        mn = ADAM_B1 * m_ref[...] + (1.0 - ADAM_B1) * gv
        vn = ADAM_B2 * v_ref[...] + (1.0 - ADAM_B2) * jnp.square(gv)
        mo_ref[...] = mn
        vo_ref[...] = vn
        d_ref[...] = -ADAM_LR * ((mn / c1) / (jnp.sqrt(vn / c2) + ADAM_EPS) + ADAM_WD * w_ref[...])

    return pl.pallas_call(
        body, name=name, grid=(d, r // tr), in_specs=[blk] * 4, out_specs=[blk] * 3,
        out_shape=[jax.ShapeDtypeStruct((d, r, c), F32)] * 3, compiler_params=_cparams(("parallel", "parallel")),
    )(w, g, m, v)


SC_TILES = 32
SC_ROWS = 8


def _adamw_sparsecore(ws, gs, ms, vs, *, name):
    n = len(ws)
    c = ws[0].shape[2]
    c1 = 1.0 - ADAM_B1 ** ADAM_STEP
    c2 = 1.0 - ADAM_B2 ** ADAM_STEP
    pieces = [(t, l, r0) for t in range(n) for l in range(ws[t].shape[0]) for r0 in range(0, ws[t].shape[1] // SC_TILES, SC_ROWS)]

    def body(*refs):
        ins, outs = refs[:4 * n], refs[4 * n:7 * n]
        bufs, sem_in, sem_out = refs[7 * n:7 * n + 8], refs[7 * n + 8], refs[7 * n + 9]
        tile = lax.axis_index("subcore") * 2 + lax.axis_index("core")

        def window(k):
            t, l, r0 = pieces[k]
            return t, (l, pl.ds(tile * (ws[t].shape[1] // SC_TILES) + r0, SC_ROWS))

        def loads(k):
            t, at = window(k)
            return [pltpu.make_async_copy(ins[j * n + t].at[at], bufs[4 * (k % 2) + j], sem_in.at[k % 2]) for j in range(4)]

        def stores(k):
            t, at = window(k)
            return [pltpu.make_async_copy(bufs[4 * (k % 2) + j], outs[(j - 1) * n + t].at[at], sem_out.at[k % 2]) for j in (1, 2, 3)]

        def update(k):
            gb, wb, mb, vb = bufs[4 * (k % 2):4 * (k % 2) + 4]

            @pl.loop(0, SC_ROWS)
            def _(rr):
                @pl.loop(0, c, step=16)
                def _(i):
                    s = (rr, pl.ds(i, 16))
                    gv = gb[s]
                    mn = ADAM_B1 * mb[s] + (1.0 - ADAM_B1) * gv
                    vn = ADAM_B2 * vb[s] + (1.0 - ADAM_B2) * (gv * gv)
                    mb[s] = mn
                    vb[s] = vn
                    wb[s] = -ADAM_LR * ((mn / c1) / (jnp.sqrt(vn / c2) + ADAM_EPS) + ADAM_WD * wb[s])

        for cp in loads(0):
            cp.start()
        for k in range(len(pieces)):
            if k + 1 < len(pieces):
                if k >= 1:
                    for cp in stores(k - 1):
                        cp.wait()
                for cp in loads(k + 1):
                    cp.start()
            for cp in loads(k):
                cp.wait()
            update(k)
            for cp in stores(k):
                cp.start()
        for k in range(max(len(pieces) - 2, 0), len(pieces)):
            for cp in stores(k):
                cp.wait()

    out = pl.kernel(
        body, name=name, out_type=[jax.ShapeDtypeStruct(t.shape, F32) for t in ws] * 3,
        mesh=plsc.VectorSubcoreMesh(core_axis_name="core", subcore_axis_name="subcore"),
        scratch_types=[pltpu.VMEM((SC_ROWS, c), F32)] * 8 + [pltpu.SemaphoreType.DMA((2,)), pltpu.SemaphoreType.DMA((2,))],
    )(*gs, *ws, *ms, *vs)
    return out[:n], out[n:2 * n], out[2 * n:]


BIG = ("w_in", "w_q_up", "w_kv_up", "w_out", "w_ffn_up", "w_ffn_down")
SMALL = ("g_mix_pre", "b_forget", "g_q_lora", "g_kv_lora", "g_mix_out", "g_mix_post", "g_ffn_pre", "g_ffn_post")
N_CHIPS = 4
ANY = pl.BlockSpec(memory_space=pl.ANY)


def _mesh_pos():
    return lax.axis_index("x"), lax.axis_index("y"), lax.axis_index("c")


def _other_chips(x, y):
    return [(1 - x, y), (x, 1 - y), (1 - x, 1 - y)]


def _rows_half(ref, half):
    h = ref.shape[-2] // 2
    return ref.at[(slice(None),) * (len(ref.shape) - 2) + (pl.ds(half * h, h), slice(None))]


def _remote(src, dst, send_sem, recv_sem, device):
    return pltpu.make_async_remote_copy(src_ref=src, dst_ref=dst, send_sem=send_sem, recv_sem=recv_sem, device_id=device,
                                        device_id_type=MESH)


def _comm_call(body, name, args, out_shape, n_sems):
    return pl.pallas_call(
        body, name=name, in_specs=[ANY] * len(args), out_specs=[ANY] * len(out_shape), out_shape=out_shape,
        scratch_shapes=[pltpu.SemaphoreType.DMA((n_sems,)), pltpu.SemaphoreType.DMA((n_sems,))],
        compiler_params=pltpu.CompilerParams(has_side_effects=True),
    )(*args)


def _run_side_job(side, name):
    si = len(side.inputs)

    def body(*refs):
        args = (refs[:si], refs[si:-2], refs[-2], refs[-1])
        sends = side.sends(*args)
        for cp in sends:
            cp.start()
        for cp in side.recvs(*args):
            cp.wait_recv()
        for cp in sends:
            cp.wait_send()

    return _comm_call(body, name, side.inputs, side.out_shape, side.n_sems)


def _gather_job(shards):
    n = len(shards)

    def copies(own_block, ins, outs, send_sems, recv_sems):
        x, y, c = _mesh_pos()
        return [_remote(_rows_half(ins[t], c), _rows_half(outs[t].at[2 * x + y if own_block else 2 * px + py], c),
                        send_sems.at[3 * t + j], recv_sems.at[3 * t + j], (px, py, c))
                for t in range(n) for j, (px, py) in enumerate(_other_chips(x, y))]

    return _SideJob(shards, [jax.ShapeDtypeStruct((N_CHIPS,) + a.shape, a.dtype) for a in shards], 3 * n,
                    functools.partial(copies, True), functools.partial(copies, False))


def _forward_halves(gathered):
    n = len(gathered)

    def body(*refs):
        bufs, send_sems, recv_sems = refs[n:2 * n], refs[-2], refs[-1]
        x, y, c = _mesh_pos()

        def d2d(t, j, block, half):
            region = _rows_half(bufs[t].at[block], half)
            return _remote(region, region, send_sems.at[3 * t + j], recv_sems.at[3 * t + j], (x, y, 1 - c))

        peers = list(enumerate(_other_chips(x, y)))
        sends = [d2d(t, j, 2 * px + py, c) for t in range(n) for j, (px, py) in peers]
        for cp in sends:
            cp.start()
        for t in range(n):
            for j, (px, py) in peers:
                d2d(t, j, 2 * px + py, 1 - c).wait_recv()
        for cp in sends:
            cp.wait_send()

    return pl.pallas_call(
        body, name="gather_forward", in_specs=[ANY] * n, out_specs=[ANY] * n,
        out_shape=[jax.ShapeDtypeStruct(g.shape, g.dtype) for g in gathered], input_output_aliases={t: t for t in range(n)},
        scratch_shapes=[pltpu.SemaphoreType.DMA((3 * n,)), pltpu.SemaphoreType.DMA((3 * n,))],
        compiler_params=pltpu.CompilerParams(has_side_effects=True),
    )(*gathered)


def _exchange_halves_job(gs):
    n = len(gs)

    def copies(ins, outs, send_sems, recv_sems):
        x, y, c = _mesh_pos()
        return [_remote(_rows_half(ins[t], 1 - c), outs[t], send_sems.at[t], recv_sems.at[t], (x, y, 1 - c)) for t in range(n)]

    out_shape = [jax.ShapeDtypeStruct(g.shape[:2] + (g.shape[2] // 2, g.shape[3]), g.dtype) for g in gs]
    return _SideJob(gs, out_shape, n, copies, copies)


def _pair_add(g, r, c_idx, *, name):
    nb, d, rows, cols = g.shape
    h = rows // 2
    tr = min(h, 512)
    nt = h // tr

    def body(c_ref, g_ref, r_ref, p_ref, pb_ref):
        s = g_ref[...] + r_ref[...]
        p_ref[...] = s
        pb_ref[...] = s.astype(BF16)

    blk = pl.BlockSpec((1, 1, tr, cols), lambda k, l, i, c_ref: (k, l, i, 0))
    return pl.pallas_call(
        body, name=name,
        grid_spec=pltpu.PrefetchScalarGridSpec(
            num_scalar_prefetch=1, grid=(nb, d, nt),
            in_specs=[pl.BlockSpec((1, 1, tr, cols), lambda k, l, i, c_ref: (k, l, c_ref[0] * nt + i, 0)), blk],
            out_specs=[blk, blk]),
        out_shape=[jax.ShapeDtypeStruct((nb, d, h, cols), F32), jax.ShapeDtypeStruct((nb, d, h, cols), BF16)],
        compiler_params=_cparams(("parallel", "parallel", "parallel")),
    )(c_idx, g, r)


def _exchange_chips_job(pbs):
    n = len(pbs)

    def copies(ins, outs, send_sems, recv_sems):
        x, y, c = _mesh_pos()
        return [_remote(ins[t].at[2 * px + py], outs[t].at[j], send_sems.at[3 * t + j], recv_sems.at[3 * t + j], (px, py, c))
                for t in range(n) for j, (px, py) in enumerate(_other_chips(x, y))]

    return _SideJob(pbs, [jax.ShapeDtypeStruct((3,) + p.shape[1:], p.dtype) for p in pbs], 3 * n, copies, copies)


def _chip_add(p, r, k_idx, *, name):
    _, d, h, cols = p.shape
    tr = min(h, 512)
    nt = h // tr

    def body(k_ref, p_ref, r_ref, o_ref):
        o_ref[0] = ((p_ref[0, 0] + r_ref[0, 0].astype(F32)) + r_ref[1, 0].astype(F32)) + r_ref[2, 0].astype(F32)

    return pl.pallas_call(
        body, name=name,
        grid_spec=pltpu.PrefetchScalarGridSpec(
            num_scalar_prefetch=1, grid=(d, nt),
            in_specs=[pl.BlockSpec((1, 1, tr, cols), lambda l, i, k_ref: (k_ref[0], l, i, 0)),
                      pl.BlockSpec((3, 1, tr, cols), lambda l, i, k_ref: (0, l, i, 0))],
            out_specs=pl.BlockSpec((1, tr, cols), lambda l, i, k_ref: (l, i, 0))),
        out_shape=jax.ShapeDtypeStruct((d, h, cols), F32), compiler_params=_cparams(("parallel", "parallel")),
    )(k_idx, p, r)


def _share_halves(qs):
    n = len(qs)

    def body(*refs):
        ins, outs, send_sems, recv_sems = refs[:n], refs[n:2 * n], refs[2 * n], refs[2 * n + 1]
        x, y, c = _mesh_pos()
        cps = [_remote(ins[t], outs[t], send_sems.at[t], recv_sems.at[t], (x, y, 1 - c)) for t in range(n)]
        for cp in cps:
            cp.start()
        for cp in cps:
            cp.wait_recv()
        for cp in cps:
            cp.wait_send()

    return _comm_call(body, "grad_pair_share", qs, [jax.ShapeDtypeStruct(q.shape, q.dtype) for q in qs], n)


def _all_reduce_small(v):
    r, cols = v.shape
    n_dev = 8

    def body(v_ref, o_ref, buf, send_sems, recv_sems):
        x, y, c = _mesh_pos()
        me = 4 * x + 2 * y + c
        buf[me] = v_ref[...]

        def peer(j):
            return (1 - x if j & 4 else x, 1 - y if j & 2 else y, 1 - c if j & 1 else c)

        def copy(j, slot):
            return pltpu.make_async_remote_copy(src_ref=v_ref, dst_ref=buf.at[slot], send_sem=send_sems.at[j - 1],
                                                recv_sem=recv_sems.at[j - 1], device_id=peer(j), device_id_type=MESH)

        sends = [copy(j, me) for j in range(1, n_dev)]
        for cp in sends:
            cp.start()
        for j in range(1, n_dev):
            px, py, pc = peer(j)
            copy(j, 4 * px + 2 * py + pc).wait_recv()
        for cp in sends:
            cp.wait_send()
        acc = buf[0]
        for d in range(1, n_dev):
            acc = acc + buf[d]
        o_ref[...] = acc

    vm = pl.BlockSpec(memory_space=pltpu.VMEM)
    return pl.pallas_call(
        body, name="small_all_reduce", in_specs=[vm], out_specs=vm, out_shape=jax.ShapeDtypeStruct((r, cols), F32),
        scratch_shapes=[pltpu.VMEM((n_dev, r, cols), F32), pltpu.SemaphoreType.DMA((n_dev - 1,)), pltpu.SemaphoreType.DMA((n_dev - 1,))],
        compiler_params=pltpu.CompilerParams(has_side_effects=True),
    )(v)


_COL_SHARDED = ("w_in", "w_q_up", "w_kv_up", "w_ffn_up")


def _shard_cols(blocks, a, b):
    c = blocks[0].shape[-1]
    out = []
    while a < b:
        k = a // c
        hi = min(b, (k + 1) * c)
        out.append(blocks[k][:, a - k * c:hi - k * c])
        a = hi
    return out


def _pack_w_in_shards(blocks):
    z = lambda n: [jnp.zeros((blocks[0].shape[0], n), blocks[0].dtype)]
    cols = lambda a, b: _shard_cols(blocks, a, b)
    return jnp.concatenate(cols(0, 768) + cols(772, 1028) + cols(1188, 2980) + cols(1028, 1156) + cols(768, 772)
                           + z(KR_LANE - N_HEADS) + cols(1156, 1188) + z(LANES - KR_LANE - ROPE_DIM), axis=1)


def _whole_layer(name, blocks):
    if name in _COL_SHARDED:
        return jnp.concatenate([blocks[k] for k in range(N_CHIPS)], axis=1)
    return blocks.reshape(N_CHIPS * blocks.shape[1], blocks.shape[2])


def _split_layer(name, whole):
    if name in _COL_SHARDED:
        c = whole.shape[1] // N_CHIPS
        return jnp.stack([whole[:, k * c:(k + 1) * c] for k in range(N_CHIPS)])
    return whole.reshape(N_CHIPS, whole.shape[0] // N_CHIPS, whole.shape[1])


def _small_to_rows(d):
    v = jnp.concatenate([d[k].astype(F32).reshape(-1) for k in SMALL])
    rows = -(-v.shape[0] // (8 * LANES)) * 8
    return jnp.pad(v, (0, rows * LANES - v.shape[0])).reshape(rows, LANES)


def _small_from_rows(rows, shapes):
    v = rows.reshape(-1)
    out, o = {}, 0
    for k in SMALL:
        sz = int(np.prod(shapes[k]))
        out[k] = v[o:o + sz].reshape(shapes[k])
        o += sz
    return out


_ARG_NAMES = ("x", "positions", "g_mix_pre", "w_in", "b_forget", "g_q_lora", "w_q_up", "g_kv_lora", "w_kv_up", "g_mix_out", "w_out",
              "g_mix_post", "g_ffn_pre", "w_ffn_up", "w_ffn_down", "g_ffn_post")
_WEIGHTS = _ARG_NAMES[2:]


def kernel(x, positions, g_mix_pre, w_in, b_forget, g_q_lora, w_q_up, g_kv_lora, w_kv_up, g_mix_out, w_out, g_mix_post, g_ffn_pre, w_ffn_up, w_ffn_down, g_ffn_post, loss_target, m_g_mix_pre, m_w_in, m_b_forget, m_g_q_lora, m_w_q_up, m_g_kv_lora, m_w_kv_up, m_g_mix_out, m_w_out, m_g_mix_post, m_g_ffn_pre, m_w_ffn_up, m_w_ffn_down, m_g_ffn_post, v_g_mix_pre, v_w_in, v_b_forget, v_g_q_lora, v_w_q_up, v_g_kv_lora, v_w_kv_up, v_g_mix_out, v_w_out, v_g_mix_post, v_g_ffn_pre, v_w_ffn_up, v_w_ffn_down, v_g_ffn_post):
    w = dict(g_mix_pre=g_mix_pre, w_in=w_in, b_forget=b_forget, g_q_lora=g_q_lora, w_q_up=w_q_up, g_kv_lora=g_kv_lora, w_kv_up=w_kv_up,
             g_mix_out=g_mix_out, w_out=w_out, g_mix_post=g_mix_post, g_ffn_pre=g_ffn_pre, w_ffn_up=w_ffn_up, w_ffn_down=w_ffn_down,
             g_ffn_post=g_ffn_post)
    m = dict(g_mix_pre=m_g_mix_pre, w_in=m_w_in, b_forget=m_b_forget, g_q_lora=m_g_q_lora, w_q_up=m_w_q_up, g_kv_lora=m_g_kv_lora,
             w_kv_up=m_w_kv_up, g_mix_out=m_g_mix_out, w_out=m_w_out, g_mix_post=m_g_mix_post, g_ffn_pre=m_g_ffn_pre,
             w_ffn_up=m_w_ffn_up, w_ffn_down=m_w_ffn_down, g_ffn_post=m_g_ffn_post)
    v = dict(g_mix_pre=v_g_mix_pre, w_in=v_w_in, b_forget=v_b_forget, g_q_lora=v_g_q_lora, w_q_up=v_w_q_up, g_kv_lora=v_g_kv_lora,
             w_kv_up=v_w_kv_up, g_mix_out=v_g_mix_out, w_out=v_w_out, g_mix_post=v_g_mix_post, g_ffn_pre=v_g_ffn_pre,
             w_ffn_up=v_w_ffn_up, w_ffn_down=v_w_ffn_down, g_ffn_post=v_g_ffn_post)
    small_shapes = {k: w[k].shape for k in SMALL}
    c_idx = lax.axis_index("c").astype(jnp.int32).reshape(1)
    k_idx = (2 * lax.axis_index("x") + lax.axis_index("y")).astype(jnp.int32).reshape(1)
    first_core = lax.axis_index("c") == 0

    mine = 2 * lax.axis_index("x") + lax.axis_index("y")
    shards_b = [{k: w[k][l:l + 1].astype(BF16) for k in BIG} for l in range(DEPTH)]
    gains = [dict(g_mix_pre=g_mix_pre[l], b_forget=b_forget[l], g_q_lora=g_q_lora[l], g_kv_lora=g_kv_lora[l], g_mix_out=g_mix_out[l],
                  g_mix_post=g_mix_post[l], g_ffn_pre=g_ffn_pre[l], g_ffn_post=g_ffn_post[l]) for l in range(DEPTH)]
    FIRST, LATER = ("w_in", "w_q_up", "w_kv_up"), ("w_out", "w_ffn_up", "w_ffn_down")
    EARLY_GRADS, LATE_GRADS = ("w_ffn_down", "w_ffn_up", "w_out"), ("w_in", "w_q_up", "w_kv_up")
    SC_ADAMW = EARLY_GRADS

    def gather_job(l, names):
        return _gather_job([shards_b[l][k] for k in names])

    def weights_of(l, names, gathered):
        four = {k: lax.dynamic_update_slice(g, shards_b[l][k][None], (mine, 0, 0, 0))[:, 0]
                for k, g in zip(names, _forward_halves(gathered))}
        out = {}
        for k in names:
            if k == "w_in":
                out["w_in"] = _pack_w_in_shards(four[k])
            elif k == "w_q_up":
                out["wq"] = _pack_w_q(_whole_layer(k, four[k]))
            elif k == "w_kv_up":
                out["wk"], out["wv"] = _pack_w_kv(_whole_layer(k, four[k]))
            elif k == "w_ffn_up":
                out[k] = four[k]
            else:
                out[k] = _whole_layer(k, four[k])
        return out

    def grad_blocks(names, g):
        whole = dict(w_in=lambda: _unpack_dw_in(g["w_in"]), w_q_up=lambda: _unpack_dw_q(g["wq"]),
                     w_kv_up=lambda: _unpack_dw_kv(g["wk"], g["wv"]), w_out=lambda: g["w_out"], w_ffn_down=lambda: g["w_ffn_down"])
        return [(g[k] if k == "w_ffn_up" else _split_layer(k, whole[k]()))[:, None] for k in names]

    def pair_sums(names, blocks, theirs):
        return [_pair_add(b, r, c_idx, name="grad_pair_add_" + k) for k, b, r in zip(names, blocks, theirs)]

    def exchange_job(*pairs):
        return _exchange_chips_job([pb for pair in pairs for (_, pb) in pair])

    def finish_grads(names, pair, partial):
        half = [_chip_add(p, r, k_idx, name="grad_chip_add_" + k) for k, (p, _), r in zip(names, pair, partial)]
        return {k: jnp.where(first_core, jnp.concatenate([q, s], axis=1), jnp.concatenate([s, q], axis=1))
                for k, q, s in zip(names, half, _share_halves(half))}

    seq = x.shape[1]
    tabs = _rope_tables(positions[0].reshape(seq, 1))
    first0 = weights_of(0, FIRST, _run_side_job(gather_job(0, FIRST), "gather_weights_l0"))
    x1, saved0, lw0, gathered1, h1 = _layer_fwd(x[0], {**gains[0], **first0}, tabs, "l0_", fox_side=gather_job(0, LATER),
                                                late_weights=lambda got: weights_of(0, LATER, got), side=gather_job(1, BIG),
                                                next_gain=gains[1]["g_mix_pre"])
    lw1 = {**gains[1], **weights_of(1, BIG, gathered1)}
    _, saved1, _, _, _ = _layer_fwd(x1, lw1, tabs, "l1_", h1=h1)
    loss_row, dx, df1, dg1 = _loss_head(saved1["f"], lw1["g_ffn_post"], saved1["x1"], loss_target[0])
    loss = lax.psum(loss_row[0, 0], ("x", "y", "c"))
    dx, grads1, _, post0 = _layer_bwd(dx, lw1, saved1, tabs, "l1_", post_given=(df1, dg1),
                                      then_prev=(saved0["f"], lw0["g_ffn_post"]))
    blocks1 = grad_blocks(BIG, grads1)
    early_blocks0, pair1, early0 = [], [], []

    def beside_l0_fox_backward(g):
        early_blocks0.extend(grad_blocks(EARLY_GRADS, g))
        return _exchange_halves_job(early_blocks0)

    def beside_l0_sb_backward(g, theirs1, theirs_early0):
        pair1.extend(pair_sums(BIG, blocks1, theirs1))
        early0.extend(pair_sums(EARLY_GRADS, early_blocks0, theirs_early0))
        return exchange_job(pair1, early0)

    dx, grads0, partial, _ = _layer_bwd(dx, lw0, saved0, tabs, "l0_", ffn_side=_exchange_halves_job(blocks1),
                                        fox_side=beside_l0_fox_backward, side=beside_l0_sb_backward, post_given=post0)
    big1 = finish_grads(BIG, pair1, partial[:len(BIG)])
    big0 = finish_grads(EARLY_GRADS, early0, partial[len(BIG):])
    g_early = {k: jnp.concatenate([big0[k], big1[k]], axis=0) for k in SC_ADAMW}
    sc_delta, sc_m, sc_v = _adamw_sparsecore([w[k] for k in SC_ADAMW], [g_early[k] for k in SC_ADAMW], [m[k] for k in SC_ADAMW],
                                             [v[k] for k in SC_ADAMW], name="adamw_sparsecore")
    late_blocks0 = grad_blocks(LATE_GRADS, grads0)
    late0 = pair_sums(LATE_GRADS, late_blocks0, _run_side_job(_exchange_halves_job(late_blocks0), "grad_pair_exchange_l0"))
    big0.update(finish_grads(LATE_GRADS, late0, _run_side_job(exchange_job(late0), "grad_chip_exchange_l0")))
    g_big = {k: g_early[k] if k in SC_ADAMW else jnp.concatenate([big0[k], big1[k]], axis=0) for k in BIG}
    grads = [grads0, grads1]

    g_small_local = {k: jnp.stack([grads[l][k].reshape(small_shapes[k][1:]) for l in range(DEPTH)]) for k in SMALL}
    g_small = _small_from_rows(_all_reduce_small(_small_to_rows(g_small_local)), small_shapes)

    g_all = {**g_big, **g_small}
    delta, new_m, new_v = {}, {}, {}
    for k in BIG:
        if k in SC_ADAMW:
            i = SC_ADAMW.index(k)
            delta[k], new_m[k], new_v[k] = sc_delta[i], sc_m[i], sc_v[i]
        else:
            delta[k], new_m[k], new_v[k] = _adamw(w[k], g_all[k], m[k], v[k], name="adamw_" + k)
    ds, ms, vs = _adamw(*[_small_to_rows(t)[None] for t in (w, g_small, m, v)], name="adamw_small")
    delta.update(_small_from_rows(ds, small_shapes))
    new_m.update(_small_from_rows(ms, small_shapes))
    new_v.update(_small_from_rows(vs, small_shapes))

    grad_x = dx.reshape(x.shape)
    return (loss, grad_x, *[g_all[k] for k in _WEIGHTS], *[delta[k] for k in _WEIGHTS], *[new_m[k] for k in _WEIGHTS],
            *[new_v[k] for k in _WEIGHTS])
```

```python
import functools
import math

import numpy as np
import jax
import jax.numpy as jnp
from jax import lax
from jax.experimental import pallas as pl
from jax.experimental.pallas import tpu as pltpu
from jax.experimental.pallas import tpu_sc as plsc

F32 = jnp.float32
BF16 = jnp.bfloat16
MESH = pl.DeviceIdType.MESH

D_MODEL = 1024
DEPTH = 2
CHUNK = 64
GROUP = 256
HEAD = 64
N_HEADS = 4
Q_RANK = 256
KV_RANK = 128
ROPE_DIM = 32
D_FF = 4096
D_IN = 2980
D_INP = 3072
ROPE_BASE = 10000.0
EPS = 1e-6
LANES = 128
TQ = 128
GATE_ROWS = 512
CONTRACT_TILE = 4096
NEG = -1e30

ADAM_LR, ADAM_B1, ADAM_B2, ADAM_EPS, ADAM_WD, ADAM_STEP = 0.001, 0.9, 0.999, 1e-08, 0.01, 10

OFF_FQ, OFF_FK, OFF_FV, OFF_CQ = 0, 2, 4, 6
OFF_RQ, OFF_RK, OFF_RV, OFF_RG = 8, 10, 12, 14
OFF_SQ, OFF_SK, OFF_SV = 16, 18, 20
OFF_CKV, OFF_MISC = 22, 23
FF_LANE, KR_LANE = 0, 64

VMEM_LIMIT = 56 * 1024 * 1024


def _tile(dim, pref):
    return pref if dim % pref == 0 else dim


def _cparams(sem, vmem=None):
    return pltpu.CompilerParams(dimension_semantics=sem, vmem_limit_bytes=vmem or VMEM_LIMIT)


def _dot(a, b):
    return jnp.dot(a, b, preferred_element_type=F32)


def _dot_nt(a, b):
    return lax.dot_general(a, b, (((1,), (1,)), ((), ())), preferred_element_type=F32)


def _dot_tn(a, b):
    return lax.dot_general(a, b, (((0,), (0,)), ((), ())), preferred_element_type=F32)


def _dot_exact(a, b):
    return jnp.dot(a, b, precision=lax.Precision.HIGHEST, preferred_element_type=F32)


def _matmul(a, b, *, name, ta=False, tb=False, out_dtype=F32, tm=1024, tn=1024, tk=CONTRACT_TILE,
            relu2=False, relu2_of=None, also_bf16=False, side=None, col_blocks=False):
    if ta:
        kdim, m = a.shape
    else:
        m, kdim = a.shape
    if col_blocks and not ta:
        n = b.shape[1] if tb else b.shape[0] * b.shape[2]
        if tb:
            kdim = b.shape[0] * b.shape[2]
    else:
        n = b.shape[0] if tb else b.shape[1]
    tm, tn, tk = _tile(m, tm), _tile(n, tn), _tile(kdim, tk)
    nk = kdim // tk
    a_spec = pl.BlockSpec((tk, tm), lambda i, j, k: (k, i)) if ta else pl.BlockSpec((tm, tk), lambda i, j, k: (i, k))
    b_spec = pl.BlockSpec((tn, tk), lambda i, j, k: (j, k)) if tb else pl.BlockSpec((tk, tn), lambda i, j, k: (k, j))
    o_spec = pl.BlockSpec((tm, tn), lambda i, j, k: (i, j))
    if col_blocks and ta:
        o_spec = pl.BlockSpec((None, tm, tn), lambda i, j, k: (j, i, 0))
    elif col_blocks and tb:
        assert tk == kdim
        b_spec = pl.BlockSpec((b.shape[0], tn, b.shape[2]), lambda i, j, k: (0, j, 0))
    elif col_blocks:
        assert b.shape[2] == tn
        b_spec = pl.BlockSpec((None, tk, tn), lambda i, j, k: (j, k, 0))
    two = also_bf16

    def body(*refs):
        refs = list(refs)
        a_ref, b_ref = refs[0], refs[1]
        e_ref = refs[2] if relu2_of is not None else None
        pos = 3 if relu2_of is not None else 2
        o_ref = refs[pos]
        o2_ref = refs[pos + 1] if two else None
        acc_ref = refs[-1]
        k = pl.program_id(2)
        av = a_ref[...].astype(BF16)
        if col_blocks and tb:
            bv = jnp.concatenate([b_ref[q] for q in range(b.shape[0])], axis=1).astype(BF16)
        else:
            bv = b_ref[...].astype(BF16)
        if ta:
            part = _dot_tn(av, bv)
        elif tb:
            part = _dot_nt(av, bv)
        else:
            part = _dot(av, bv)

        @pl.when(k == 0)
        def _():
            acc_ref[...] = part

        @pl.when(k > 0)
        def _():
            acc_ref[...] += part

        @pl.when(k == nk - 1)
        def _():
            r = acc_ref[...]
            if relu2_of is not None:
                r = r * (2.0 * jnp.sqrt(e_ref[...].astype(F32)))
            if relu2:
                r = jnp.square(jnp.maximum(r, 0.0))
            o_ref[...] = r.astype(o_ref.dtype)
            if also_bf16:
                o2_ref[...] = r.astype(BF16)

    in_specs = [a_spec, b_spec]
    args = [a, b]
    if relu2_of is not None:
        in_specs.append(o_spec)
        args.append(relu2_of)
    out_shape = [jax.ShapeDtypeStruct((n // tn, m, tn) if (col_blocks and ta) else (m, n), out_dtype)]
    out_specs = [o_spec]
    if two:
        out_shape.append(jax.ShapeDtypeStruct((m, n), BF16))
        out_specs.append(o_spec)
    grid = (m // tm, n // tn, nk)
    side_in, side_out, side_scratch = _side_specs(side)
    res = pl.pallas_call(
        _carry_side_job(body, len(args), len(out_shape), side, grid), name=name, grid=grid,
        in_specs=in_specs + side_in, out_specs=out_specs + side_out,
        out_shape=out_shape + ([] if side is None else side.out_shape),
        scratch_shapes=[pltpu.VMEM((tm, tn), F32)] + side_scratch,
        compiler_params=_cparams(("parallel", "parallel", "arbitrary") if side is None else ("arbitrary",) * 3),
    )(*args, *([] if side is None else side.inputs))
    main = res[:len(out_shape)]
    main = main if two else main[0]
    return main if side is None else (main, res[len(out_shape):])


def _rms(x, g):
    r = lax.rsqrt(jnp.mean(x * x, axis=-1, keepdims=True) + EPS)
    return x * r * g


def _rms_bwd(x, g, dy):
    r = lax.rsqrt(jnp.mean(x * x, axis=-1, keepdims=True) + EPS)
    xh = x * r
    gdy = dy * g
    dx = r * (gdy - xh * jnp.mean(xh * gdy, axis=-1, keepdims=True))
    return dx, xh * dy


def _norm_fwd(x, g, *, name, resid=None, out_dtype=BF16, next_gain=None):
    s, d = x.shape
    tr = _tile(s, 256)
    row = pl.BlockSpec((tr, d), lambda i: (i, 0))
    gsp = pl.BlockSpec((1, d), lambda i: (0, 0))

    def body(*refs):
        refs = list(refs)
        x_ref, g_ref = refs[:2]
        y = _rms(x_ref[...], g_ref[...])
        pos = 2
        if resid is not None:
            y = refs[pos][...] + y
            pos += 1
        if next_gain is None:
            refs[pos][...] = y.astype(refs[pos].dtype)
        else:
            refs[pos + 1][...] = y.astype(refs[pos + 1].dtype)
            refs[pos + 2][...] = _rms(y, refs[pos][...]).astype(BF16)

    args = [x, g.reshape(1, d)] + ([] if resid is None else [resid]) + ([] if next_gain is None else [next_gain.reshape(1, d)])
    in_specs = [row, gsp] + ([] if resid is None else [row]) + ([] if next_gain is None else [gsp])
    first = jax.ShapeDtypeStruct((s, d), out_dtype)
    if next_gain is None:
        out_specs, out_shape = row, first
    else:
        out_specs, out_shape = [row, row], [first, jax.ShapeDtypeStruct((s, d), BF16)]
    return pl.pallas_call(
        body, name=name, grid=(s // tr,), in_specs=in_specs, out_specs=out_specs, out_shape=out_shape,
        compiler_params=_cparams(("parallel",)),
    )(*args)


def _norm_bwd(x, g, dy, *, name, add=None, out_dtype=F32, then=None):
    s, d = x.shape
    tr = _tile(s, 256)
    row = pl.BlockSpec((tr, d), lambda i: (i, 0))
    gsp = pl.BlockSpec((1, d), lambda i: (0, 0))
    n_in = 3 + (add is not None) + (2 if then is not None else 0)

    def body(*refs):
        ins, outs = refs[:n_in], refs[n_in:]
        x_ref, g_ref, dy_ref = ins[:3]
        dx, gterm = _rms_bwd(x_ref[...], g_ref[...], dy_ref[...].astype(F32))
        if add is not None:
            dx = dx + ins[3][...]
        outs[0][...] = dx.astype(outs[0].dtype)
        terms = [(outs[1], gterm)]
        if then is not None:
            dx2, gterm2 = _rms_bwd(ins[-2][...], ins[-1][...], dx)
            outs[2][...] = dx2.astype(BF16)
            terms.append((outs[3], gterm2))

        @pl.when(pl.program_id(0) == 0)
        def _():
            for dg_ref, _ in terms:
                dg_ref[...] = jnp.zeros_like(dg_ref)

        for dg_ref, term in terms:
            dg_ref[...] += jnp.sum(term, axis=0, keepdims=True)

    args = [x, g.reshape(1, d), dy] + ([] if add is None else [add]) + ([] if then is None else [then[0], then[1].reshape(1, d)])
    in_specs = [row, gsp, row] + ([] if add is None else [row]) + ([] if then is None else [row, gsp])
    out_specs = [row, gsp] + ([] if then is None else [row, gsp])
    out_shape = [jax.ShapeDtypeStruct((s, d), out_dtype), jax.ShapeDtypeStruct((1, d), F32)]
    if then is not None:
        out_shape += [jax.ShapeDtypeStruct((s, d), BF16), jax.ShapeDtypeStruct((1, d), F32)]
    return pl.pallas_call(
        body, name=name, grid=(s // tr,), in_specs=in_specs, out_specs=out_specs, out_shape=out_shape,
        compiler_params=_cparams(("arbitrary",)),
    )(*args)


def _loss_head(f, g, resid, target):
    s, d = f.shape
    tr = _tile(s, 256)
    row = pl.BlockSpec((tr, d), lambda i: (i, 0))
    gsp = pl.BlockSpec((1, d), lambda i: (0, 0))
    lsp = pl.BlockSpec((1, LANES), lambda i: (0, 0))

    def body(f_ref, g_ref, r_ref, t_ref, l_ref, dy_ref, df_ref, dg_ref):
        fv, gv = f_ref[...], g_ref[...]
        e = (r_ref[...] + _rms(fv, gv)) - t_ref[...]
        dy = e * (1.0 / d)
        dy_ref[...] = dy
        df, gterm = _rms_bwd(fv, gv, dy)
        df_ref[...] = df.astype(BF16)

        @pl.when(pl.program_id(0) == 0)
        def _():
            l_ref[...] = jnp.zeros_like(l_ref)
            dg_ref[...] = jnp.zeros_like(dg_ref)

        part = 0.5 * jnp.sum(jnp.mean(e * e, axis=-1, keepdims=True), axis=0, keepdims=True)
        l_ref[...] += jnp.broadcast_to(part, (1, LANES))
        dg_ref[...] += jnp.sum(gterm, axis=0, keepdims=True)

    return pl.pallas_call(
        body, name="loss_head", grid=(s // tr,), in_specs=[row, gsp, row, row], out_specs=[lsp, row, row, gsp],
        out_shape=[jax.ShapeDtypeStruct((1, LANES), F32), jax.ShapeDtypeStruct((s, d), F32), jax.ShapeDtypeStruct((s, d), BF16),
                   jax.ShapeDtypeStruct((1, d), F32)],
        compiler_params=_cparams(("arbitrary",)),
    )(f, g.reshape(1, d), resid, target)


def _rope_tables(pos_col):
    s = pos_col.shape[0]
    tr = _tile(s, 512)
    f_mla = ROPE_BASE ** (-jnp.arange(ROPE_DIM // 2, dtype=F32) / (ROPE_DIM // 2))
    f_ret = ROPE_BASE ** (-jnp.arange(HEAD // 2, dtype=F32) / (HEAD // 2))
    fm = jnp.concatenate([jnp.zeros((64,), F32), f_mla, f_mla, jnp.zeros((32,), F32)]).reshape(1, LANES)
    fr = jnp.tile(jnp.concatenate([f_ret, f_ret]), 2).reshape(1, LANES)

    def body(p_ref, fm_ref, fr_ref, cm_ref, sm_ref, cr_ref, sr_ref):
        p = p_ref[...].astype(F32)
        am = p * fm_ref[...]
        ar = p * fr_ref[...]
        cm_ref[...] = jnp.cos(am)
        sm_ref[...] = jnp.sin(am)
        cr_ref[...] = jnp.tile(jnp.cos(ar), (1, 2))
        sr_ref[...] = jnp.tile(jnp.sin(ar), (1, 2))

    return pl.pallas_call(
        body, name="rope_tables", grid=(s // tr,),
        in_specs=[pl.BlockSpec((tr, 1), lambda i: (i, 0)), pl.BlockSpec((1, LANES), lambda i: (0, 0)),
                  pl.BlockSpec((1, LANES), lambda i: (0, 0))],
        out_specs=[pl.BlockSpec((tr, LANES), lambda i: (i, 0))] * 2 + [pl.BlockSpec((tr, 2 * LANES), lambda i: (i, 0))] * 2,
        out_shape=[jax.ShapeDtypeStruct((s, LANES), F32)] * 2 + [jax.ShapeDtypeStruct((s, 2 * LANES), F32)] * 2,
        compiler_params=_cparams(("parallel",)),
    )(pos_col, fm, fr)


def _lane(shape):
    return lax.broadcasted_iota(jnp.int32, shape, len(shape) - 1)


def _rot_mla(z):
    l = _lane(z.shape) % LANES
    n = z.shape[-1]
    return jnp.where(l < 80, -pltpu.roll(z, n - 16, 1), pltpu.roll(z, 16, 1))


def _rot_mla_t(y):
    l = _lane(y.shape) % LANES
    n = y.shape[-1]
    return jnp.where((l >= 64) & (l < 80), pltpu.roll(y, n - 16, 1),
                     jnp.where((l >= 80) & (l < 96), -pltpu.roll(y, 16, 1), 0.0))


def _rot_ret(z):
    l = _lane(z.shape) % HEAD
    n = z.shape[-1]
    return jnp.where(l < 32, -pltpu.roll(z, n - 32, 1), pltpu.roll(z, 32, 1))


def _rot_ret_t(y):
    l = _lane(y.shape) % HEAD
    n = y.shape[-1]
    return jnp.where(l < 32, pltpu.roll(y, n - 32, 1), -pltpu.roll(y, 32, 1))


def _log_sigmoid(x):
    return jnp.minimum(x, 0.0) - jnp.log1p(jnp.exp(-jnp.abs(x)))


def _fox_cum(proj, bias_row):
    s = proj.shape[0]
    fb = _tile(s, GATE_ROWS)
    nb = s // fb

    def body(x_ref, b_ref, cc_ref, cr_ref, carry_ref):
        @pl.when(pl.program_id(0) == 0)
        def _():
            carry_ref[...] = jnp.zeros_like(carry_ref)

        ls = _log_sigmoid(x_ref[...] + b_ref[...])
        r = lax.broadcasted_iota(jnp.int32, (fb, fb), 0)
        c = lax.broadcasted_iota(jnp.int32, (fb, fb), 1)
        tri = (c <= r).astype(F32)
        cum = _dot_exact(tri, ls) + carry_ref[...]
        carry_ref[...] = cum[fb - 1:fb, :]
        cc_ref[...] = cum
        cr_ref[...] = cum.T[0:8, :]

    return pl.pallas_call(
        body, name="fox_cum", grid=(nb,),
        in_specs=[pl.BlockSpec((fb, LANES), lambda i: (i, OFF_MISC)), pl.BlockSpec((1, LANES), lambda i: (0, 0))],
        out_specs=[pl.BlockSpec((fb, LANES), lambda i: (i, 0)), pl.BlockSpec((8, fb), lambda i: (0, i))],
        out_shape=[jax.ShapeDtypeStruct((s, LANES), F32), jax.ShapeDtypeStruct((8, s), F32)],
        scratch_shapes=[pltpu.VMEM((1, LANES), F32)],
        compiler_params=_cparams(("arbitrary",)),
    )(proj, bias_row)


def _fox_gate_bwd(dck, drs, proj, bias_row, dkr):
    s = proj.shape[0]
    fb = _tile(s, GATE_ROWS)
    nb = s // fb

    def body(d_ref, r_ref, x_ref, b_ref, k_ref, o_ref, db_ref, carry_ref):
        @pl.when(pl.program_id(0) == 0)
        def _():
            carry_ref[...] = jnp.zeros_like(carry_ref)
            db_ref[...] = jnp.zeros_like(db_ref)

        rows = jnp.concatenate([d_ref[0], d_ref[1], jnp.zeros((LANES - 16, fb), F32)], axis=0)
        t = rows.T
        l = _lane((fb, LANES))
        r0, r1 = r_ref[0], r_ref[1]
        rsum = jnp.where(l == 0, r0[:, 0:1], jnp.where(l == 1, r0[:, HEAD:HEAD + 1],
                         jnp.where(l == 2, r1[:, 0:1], jnp.where(l == 3, r1[:, HEAD:HEAD + 1], 0.0))))
        dcum = rsum - jnp.where(l < 2, t, pltpu.roll(t, LANES - 6, 1))
        r = lax.broadcasted_iota(jnp.int32, (fb, fb), 0)
        c = lax.broadcasted_iota(jnp.int32, (fb, fb), 1)
        triu = (c >= r).astype(F32)
        rc = _dot_exact(triu, dcum) + carry_ref[...]
        carry_ref[...] = rc[0:1, :]
        f = x_ref[...] + b_ref[...]
        sig_neg = 1.0 / (1.0 + jnp.exp(f))
        df = jnp.where(l < N_HEADS, rc * sig_neg, 0.0)
        db_ref[...] += jnp.sum(df, axis=0, keepdims=True)
        o_ref[...] = (df + k_ref[...]).astype(o_ref.dtype)

    rev = lambda i: nb - 1 - i
    return pl.pallas_call(
        body, name="fox_gate_bwd", grid=(nb,),
        in_specs=[pl.BlockSpec((2, 8, fb), lambda i: (0, 0, rev(i))), pl.BlockSpec((2, fb, LANES), lambda i: (0, rev(i), 0)),
                  pl.BlockSpec((fb, LANES), lambda i: (rev(i), OFF_MISC)),
                  pl.BlockSpec((1, LANES), lambda i: (0, 0)), pl.BlockSpec((fb, LANES), lambda i: (rev(i), 0))],
        out_specs=[pl.BlockSpec((fb, LANES), lambda i: (rev(i), 0)), pl.BlockSpec((1, LANES), lambda i: (0, 0))],
        out_shape=[jax.ShapeDtypeStruct((s, LANES), BF16), jax.ShapeDtypeStruct((1, LANES), F32)],
        scratch_shapes=[pltpu.VMEM((1, LANES), F32)],
        compiler_params=_cparams(("arbitrary",)),
    )(dck, drs, proj, bias_row, dkr)


def _mla_prep(proj, cos_m, sin_m, g_q, g_kv, wq, wk, wv):
    s = proj.shape[0]
    tr = _tile(s, 512)

    def body(cq_ref, ckv_ref, misc_ref, cos_ref, sin_ref, gq_ref, gkv_ref, wq_ref, wk_ref, wv_ref,
             q_ref, k_ref, v_ref, cqn_ref, ckvn_ref):
        cos4 = jnp.tile(cos_ref[...], (1, 4))
        sin4 = jnp.tile(sin_ref[...], (1, 4))
        cqn = _rms(cq_ref[...], gq_ref[...]).astype(BF16)
        ckvn = _rms(ckv_ref[...], gkv_ref[...]).astype(BF16)
        cqn_ref[...] = cqn
        ckvn_ref[...] = ckvn
        zq = _dot(cqn, wq_ref[...])
        q_ref[...] = (zq * cos4 + _rot_mla(zq) * sin4).astype(BF16)
        l = _lane((tr, LANES))
        kr = jnp.where((l >= KR_LANE) & (l < KR_LANE + ROPE_DIM), misc_ref[...], 0.0)
        zk = _dot(ckvn, wk_ref[...]) + jnp.tile(kr, (1, 4))
        k_ref[...] = (zk * cos4 + _rot_mla(zk) * sin4).astype(BF16)
        v_ref[...] = _dot(ckvn, wv_ref[...]).astype(BF16)

    full = lambda a: pl.BlockSpec(a.shape, lambda i: (0, 0))
    rowb = lambda w: pl.BlockSpec((tr, w), lambda i: (i, 0))
    gq2, gkv2 = g_q.reshape(1, Q_RANK), g_kv.reshape(1, KV_RANK)
    return pl.pallas_call(
        body, name="mla_prep", grid=(s // tr,),
        in_specs=[pl.BlockSpec((tr, 256), lambda i: (i, OFF_CQ // 2)), pl.BlockSpec((tr, LANES), lambda i: (i, OFF_CKV)),
                  pl.BlockSpec((tr, LANES), lambda i: (i, OFF_MISC)), rowb(LANES), rowb(LANES),
                  full(gq2), full(gkv2), full(wq), full(wk), full(wv)],
        out_specs=[rowb(512), rowb(512), rowb(512), rowb(256), rowb(128)],
        out_shape=[jax.ShapeDtypeStruct((s, 512), BF16), jax.ShapeDtypeStruct((s, 512), BF16), jax.ShapeDtypeStruct((s, 512), BF16),
                   jax.ShapeDtypeStruct((s, 256), BF16), jax.ShapeDtypeStruct((s, 128), BF16)],
        compiler_params=_cparams(("parallel",)),
    )(proj, proj, proj, cos_m, sin_m, gq2, gkv2, wq, wk, wv)


def _mla_prep_bwd(dq, dk, dv, proj, cqn, ckvn, cos_m, sin_m, g_q, g_kv, wq, wk, wv):
    s = proj.shape[0]
    tr = _tile(s, 512)

    def body(dq_ref, dk_ref, dv_ref, cq_ref, ckv_ref, cqn_ref, ckvn_ref, cos_ref, sin_ref, gq_ref, gkv_ref,
             wq_ref, wk_ref, wv_ref, dcq_ref, dckv_ref, dkr_ref, dwq_ref, dwk_ref, dwv_ref, dgq_ref, dgkv_ref):
        @pl.when(pl.program_id(0) == 0)
        def _():
            for r in (dwq_ref, dwk_ref, dwv_ref, dgq_ref, dgkv_ref):
                r[...] = jnp.zeros_like(r)

        cos4 = jnp.tile(cos_ref[...], (1, 4))
        sin4 = jnp.tile(sin_ref[...], (1, 4))
        dqv = dq_ref[...]
        dzq = dqv * cos4 + _rot_mla_t(dqv * sin4)
        dkv_ = dk_ref[...]
        dzk = dkv_ * cos4 + _rot_mla_t(dkv_ * sin4)
        l = _lane((tr, LANES))
        in_rope = (l >= KR_LANE) & (l < KR_LANE + ROPE_DIM)
        dkr = dzk[:, 0:128] + dzk[:, 128:256] + dzk[:, 256:384] + dzk[:, 384:512]
        dkr_ref[...] = jnp.where(in_rope, dkr, 0.0)
        dzq_b = dzq.astype(BF16)
        dzk_b = dzk.astype(BF16)
        dv_b = dv_ref[...].astype(BF16)
        dcqn = _dot_nt(dzq_b, wq_ref[...])
        dckvn = _dot_nt(dzk_b, wk_ref[...]) + _dot_nt(dv_b, wv_ref[...])
        dwq_ref[...] += _dot_tn(cqn_ref[...], dzq_b)
        dwk_ref[...] += _dot_tn(ckvn_ref[...], dzk_b)
        dwv_ref[...] += _dot_tn(ckvn_ref[...], dv_b)
        dcq, gq_term = _rms_bwd(cq_ref[...], gq_ref[...], dcqn)
        dckv, gkv_term = _rms_bwd(ckv_ref[...], gkv_ref[...], dckvn)
        dcq_ref[...] = dcq.astype(BF16)
        dckv_ref[...] = dckv.astype(BF16)
        dgq_ref[...] += jnp.sum(gq_term, axis=0, keepdims=True)
        dgkv_ref[...] += jnp.sum(gkv_term, axis=0, keepdims=True)

    full = lambda shp: pl.BlockSpec(shp, lambda i: (0, 0))
    rowb = lambda w: pl.BlockSpec((tr, w), lambda i: (i, 0))
    gq2, gkv2 = g_q.reshape(1, Q_RANK), g_kv.reshape(1, KV_RANK)
    return pl.pallas_call(
        body, name="mla_prep_bwd", grid=(s // tr,),
        in_specs=[rowb(512), rowb(512), rowb(512),
                  pl.BlockSpec((tr, 256), lambda i: (i, OFF_CQ // 2)), pl.BlockSpec((tr, LANES), lambda i: (i, OFF_CKV)),
                  rowb(256), rowb(128), rowb(LANES), rowb(LANES), full((1, Q_RANK)), full((1, KV_RANK)),
                  full(wq.shape), full(wk.shape), full(wv.shape)],
        out_specs=[rowb(256), rowb(128), rowb(128), full(wq.shape), full(wk.shape), full(wv.shape),
                   full((1, Q_RANK)), full((1, KV_RANK))],
        out_shape=[jax.ShapeDtypeStruct((s, 256), BF16), jax.ShapeDtypeStruct((s, 128), BF16), jax.ShapeDtypeStruct((s, 128), F32),
                   jax.ShapeDtypeStruct(wq.shape, F32), jax.ShapeDtypeStruct(wk.shape, F32), jax.ShapeDtypeStruct(wv.shape, F32),
                   jax.ShapeDtypeStruct((1, Q_RANK), F32), jax.ShapeDtypeStruct((1, KV_RANK), F32)],
        compiler_params=_cparams(("arbitrary",)),
    )(dq, dk, dv, proj, proj, cqn, ckvn, cos_m, sin_m, gq2, gkv2, wq, wk, wv)


def _ret_prep(proj, cos_r, sin_r):
    s = proj.shape[0]
    tr = _tile(s, 512)

    def body(q_ref, k_ref, cos_ref, sin_ref, qo_ref, ko_ref):
        cos, sin = cos_ref[...], sin_ref[...]
        q, k = q_ref[...], k_ref[...]
        qo_ref[...] = (q * cos + _rot_ret(q) * sin).astype(BF16)
        ko_ref[...] = ((k * cos + _rot_ret(k) * sin) * (HEAD ** -0.5)).astype(BF16)

    rowb = pl.BlockSpec((tr, 256), lambda i: (i, 0))
    return pl.pallas_call(
        body, name="ret_prep", grid=(s // tr,),
        in_specs=[pl.BlockSpec((tr, 256), lambda i: (i, OFF_RQ // 2)), pl.BlockSpec((tr, 256), lambda i: (i, OFF_RK // 2)), rowb, rowb],
        out_specs=[rowb, rowb], out_shape=[jax.ShapeDtypeStruct((s, 256), BF16)] * 2,
        compiler_params=_cparams(("parallel",)),
    )(proj, proj, cos_r, sin_r)


def _ret_prep_bwd(dq, dk, cos_r, sin_r):
    s = dq.shape[0]
    tr = _tile(s, 512)

    def body(dq_ref, dk_ref, cos_ref, sin_ref, qo_ref, ko_ref):
        cos, sin = cos_ref[...], sin_ref[...]
        q, k = dq_ref[...], dk_ref[...] * (HEAD ** -0.5)
        qo_ref[...] = (q * cos + _rot_ret_t(q * sin)).astype(BF16)
        ko_ref[...] = (k * cos + _rot_ret_t(k * sin)).astype(BF16)

    rowb = pl.BlockSpec((tr, 256), lambda i: (i, 0))
    return pl.pallas_call(
        body, name="ret_prep_bwd", grid=(s // tr,), in_specs=[rowb] * 4, out_specs=[rowb, rowb],
        out_shape=[jax.ShapeDtypeStruct((s, 256), BF16)] * 2, compiler_params=_cparams(("parallel",)),
    )(dq, dk, cos_r, sin_r)


_LOG_GAMMA = [float(np.log1p(-np.float32(2.0) ** np.float32(-5.0 - h))) for h in range(N_HEADS)]
_MLA_SCALE = float((HEAD + ROPE_DIM) ** -0.5)
_QK_SCALE = float(HEAD ** -0.5)
KEY_BLOCKS = 4
QB = 512


def _split2(x):
    h = x.astype(BF16)
    return h, (x - h.astype(F32)).astype(BF16)


def _dot2(x, u):
    h, lo = _split2(x)
    return _dot(h, u) + _dot(lo, u)


def _head_pick(block, head, axis):
    idx = lax.broadcasted_iota(jnp.int32, block.shape, axis)
    return jnp.sum(jnp.where(idx == head, block, 0.0), axis=axis, keepdims=True)


def _log_gamma_of(head):
    lg = jnp.float32(_LOG_GAMMA[3])
    for h in (2, 1, 0):
        lg = jnp.where(head == h, jnp.float32(_LOG_GAMMA[h]), lg)
    return lg


def _mixer_specs(mode, s, q_off, k_off, v_off):
    nhb = 2
    bw = 2 * LANES if mode == "mla" else LANES
    nsub = KEY_BLOCKS if (s // TQ) % KEY_BLOCKS == 0 else 1
    q_spec = pl.BlockSpec((QB, bw), lambda p, i: (i, q_off + p))
    k_spec = pl.BlockSpec((s, bw), lambda p, i: (0, k_off + p))
    v_spec = pl.BlockSpec((s, bw), lambda p, i: (0, v_off + p))
    return nhb, N_HEADS // nhb, nsub, q_spec, k_spec, v_spec


def _mixer_geometry(mode, i, nsub):
    w = TQ * nsub
    row = lax.broadcasted_iota(jnp.int32, (QB, w), 0)
    col = lax.broadcasted_iota(jnp.int32, (QB, w), 1)
    nfull = (i * QB) // w
    dist = col - row
    if mode in ("fox", "sb"):
        rel = dist
    else:
        rel = col - (row | (CHUNK - 1))

    def visible(c):
        off = c * w - i * QB
        return (rel + off) < 0 if mode == "sb" else (rel + off) <= 0

    return nfull, dist, visible


class _SideJob:
    def __init__(self, inputs, out_shape, n_sems, sends, recvs):
        self.inputs, self.out_shape, self.n_sems, self.sends, self.recvs = list(inputs), list(out_shape), n_sems, sends, recvs


def _carry_side_job(body, n_in, n_out, side, n_steps):
    if side is None:
        return body
    si, so = len(side.inputs), len(side.out_shape)

    def at(corner):
        ok = pl.program_id(0) == corner[0]
        for d in range(1, len(n_steps)):
            ok = ok & (pl.program_id(d) == corner[d])
        return ok

    def wrapped(*refs):
        ins, s_ins = refs[:n_in], refs[n_in:n_in + si]
        outs, s_outs = refs[n_in + si:n_in + si + n_out], refs[n_in + si + n_out:n_in + si + n_out + so]
        scratch, send, recv = refs[n_in + si + n_out + so:-2], refs[-2], refs[-1]

        @pl.when(at([0] * len(n_steps)))
        def _():
            for cp in side.sends(s_ins, s_outs, send, recv):
                cp.start()

        body(*ins, *outs, *scratch)

        @pl.when(at([n - 1 for n in n_steps]))
        def _():
            for cp in side.recvs(s_ins, s_outs, send, recv):
                cp.wait_recv()
            for cp in side.sends(s_ins, s_outs, send, recv):
                cp.wait_send()

    return wrapped


def _side_specs(side):
    if side is None:
        return [], [], []
    hbm = pl.BlockSpec(memory_space=pl.ANY)
    return ([hbm] * len(side.inputs), [hbm] * len(side.out_shape),
            [pltpu.SemaphoreType.DMA((side.n_sems,)), pltpu.SemaphoreType.DMA((side.n_sems,))])


def _mixer_fwd(mode, qa, q_off, ka, k_off, va, v_off, *, cum_col=None, cum_row=None, side=None):
    s = qa.shape[0]
    nq = s // QB
    nhb, nblk, nsub, q_spec, k_spec, v_spec = _mixer_specs(mode, s, q_off, k_off, v_off)
    w = TQ * nsub
    softmax = mode in ("fox", "mla")

    def body(*refs):
        refs = list(refs)
        q_ref, k_ref, v_ref = refs[:3]
        refs = refs[3:]
        if mode == "fox":
            cc_ref, cr_ref = refs[:2]
            refs = refs[2:]
        o_ref = refs[0]
        st_ref = refs[1]
        p = pl.program_id(0)
        i = pl.program_id(1)
        nfull, dist, visible = _mixer_geometry(mode, i, nsub)
        lane = _lane((1, LANES))
        heads = [nhb * p + hh for hh in range(nhb)]
        wide = mode == "mla"
        q_scale = _QK_SCALE if mode in ("fox", "sb") else 1.0
        cols = [slice(hh * LANES, (hh + 1) * LANES) if wide else slice(None) for hh in range(nhb)]
        if wide:
            qs = [q_ref[:, cols[hh]] for hh in range(nhb)]
        else:
            qf = q_ref[...].astype(F32) * q_scale
            qs = [jnp.where((lane // HEAD) == hh, qf, 0.0).astype(BF16) for hh in range(nhb)]
        if mode == "fox":
            cqs = [_head_pick(cc_ref[...], h, 1) for h in heads]
        if mode == "sb":
            r1 = lax.broadcasted_iota(jnp.int32, (TQ, TQ), 0)
            c1 = lax.broadcasted_iota(jnp.int32, (TQ, TQ), 1)
            u_after = (r1 > c1).astype(BF16)

        def chunk(c):
            return pl.ds(pl.multiple_of(c * w, w), w)

        def scores(c):
            js = chunk(c)
            return tuple(_dot_nt(qs[hh], k_ref[js, cols[hh]]) for hh in range(nhb))

        def head_step(hh, c, js, sc, vj, carry, last):
            if softmax:
                m, l, acc = carry
                if mode == "fox":
                    ck = _head_pick(cr_ref[:, js], heads[hh], 0)
                    sc = sc + (cqs[hh] - ck)
                else:
                    sc = sc * _MLA_SCALE
                if last:
                    sc = jnp.where(visible(c), sc, NEG)
                m_new = jnp.maximum(m, jnp.max(sc, axis=-1, keepdims=True))
                alpha = jnp.exp(m - m_new)
                pr = jnp.exp(sc - m_new)
                l = alpha * l + jnp.sum(pr, axis=-1, keepdims=True)
                acc = alpha * acc + _dot(pr.astype(BF16), vj)
                return m_new, l, acc
            run, acc = carry
            z = sc
            log_beta = jnp.minimum(z, 0.0) - jnp.log(1.0 + jnp.exp(-jnp.abs(z)))
            log_stay = log_beta - z
            if last:
                vis = visible(c)
                log_stay = jnp.where(vis, log_stay, 0.0)
            parts = [None] * nsub
            for b in reversed(range(nsub)):
                ls_b = log_stay[:, b * TQ:(b + 1) * TQ]
                parts[b] = _dot2(ls_b, u_after) + run
                run = run + jnp.sum(ls_b, axis=-1, keepdims=True)
            later = parts[0] if nsub == 1 else jnp.concatenate(parts, axis=1)
            wgt = jnp.exp(log_beta + later)
            if last:
                wgt = jnp.where(vis, wgt, 0.0)
            return run, acc + _dot(wgt.astype(BF16), vj)

        def step(c, c_next, state, last):
            scs, carries = state
            nxt = scores(c_next) if c_next is not None else None
            js = chunk(c)
            return nxt, tuple(head_step(hh, c, js, scs[hh], v_ref[js, cols[hh]], carries[hh], last) for hh in range(nhb))

        zero_acc = jnp.zeros((QB, LANES), F32)
        zero1 = jnp.zeros((QB, 1), F32)
        if softmax:
            init = tuple((jnp.full((QB, 1), NEG, F32), zero1, zero_acc) for _ in range(nhb))
        else:
            init = tuple((zero1, zero_acc) for _ in range(nhb))
        if mode == "sb":
            state = step(nfull, jnp.maximum(nfull - 1, 0), (scores(nfull), init), True)
            _, carries = lax.fori_loop(0, nfull, lambda t, st: step(nfull - 1 - t, jnp.maximum(nfull - 2 - t, 0), st, False), state)
        else:
            state = lax.fori_loop(0, nfull, lambda c, st: step(c, c + 1, st, False), (scores(0), init))
            _, carries = step(nfull, None, state, True)
        if softmax:
            outs = [acc / l for (m, l, acc) in carries]
            stats = [m + jnp.log(l) for (m, l, acc) in carries]
        else:
            outs, stats = [acc for (run, acc) in carries], [run for (run, acc) in carries]
        hm0 = (lane // HEAD) == 0
        pick = lambda a: jnp.where(hm0, a[0], a[1])
        if wide:
            for hh in range(nhb):
                o_ref[:, cols[hh]] = outs[hh]
        else:
            o_ref[...] = pick(outs)
        st_ref[0] = pick(stats)

    in_specs = [q_spec, k_spec, v_spec]
    args = [qa, ka, va]
    if mode == "fox":
        in_specs += [pl.BlockSpec((QB, LANES), lambda p, i: (i, 0)), pl.BlockSpec((8, s), lambda p, i: (0, 0))]
        args += [cum_col, cum_row]
    bw = 2 * LANES if mode == "mla" else LANES
    out_specs = [pl.BlockSpec((QB, bw), lambda p, i: (i, p))]
    out_shape = [jax.ShapeDtypeStruct((s, nblk * bw), F32)]
    out_specs.append(pl.BlockSpec((1, QB, LANES), lambda p, i: (p, i, 0)))
    out_shape.append(jax.ShapeDtypeStruct((nblk, s, LANES), F32))
    side_in, side_out, side_scratch = _side_specs(side)
    res = pl.pallas_call(
        _carry_side_job(body, len(args), len(out_shape), side, (nblk, nq)), name=mode + "_fwd", grid=(nblk, nq),
        in_specs=in_specs + side_in, out_specs=out_specs + side_out,
        out_shape=out_shape + ([] if side is None else side.out_shape), scratch_shapes=side_scratch,
        compiler_params=_cparams(("parallel", "parallel") if side is None else ("arbitrary", "arbitrary")),
    )(*args, *([] if side is None else side.inputs))
    return (res[0], res[1]) if side is None else (res[0], res[1], res[2:])


def _mixer_bwd(mode, qa, q_off, ka, k_off, va, v_off, o, do, *, stat=None, cum_col=None, cum_row=None, side=None):
    s = qa.shape[0]
    nq = s // QB
    nhb, nblk, nsub, q_spec, k_spec, v_spec = _mixer_specs(mode, s, q_off, k_off, v_off)
    w = TQ * nsub
    softmax = mode in ("fox", "mla")

    def body(*refs):
        refs = list(refs)
        q_ref, k_ref, v_ref, o_ref, do_ref = refs[:5]
        refs = refs[5:]
        st_ref = refs[0]
        refs = refs[1:]
        if mode == "fox":
            cc_ref, cr_ref = refs[:2]
            refs = refs[2:]
        dq_ref, dk_ref, dv_ref = refs[:3]
        dck_ref, drs_ref = refs[3:5] if mode == "fox" else (None, None)
        p = pl.program_id(0)
        i = pl.program_id(1)

        @pl.when(i == 0)
        def _():
            dk_ref[...] = jnp.zeros_like(dk_ref)
            dv_ref[...] = jnp.zeros_like(dv_ref)
            if mode == "fox":
                dck_ref[...] = jnp.zeros_like(dck_ref)

        nfull, dist, visible = _mixer_geometry(mode, i, nsub)
        lane = _lane((1, LANES))
        heads = [nhb * p + hh for hh in range(nhb)]
        dov = do_ref[...]
        wide = mode == "mla"
        q_scale = _QK_SCALE if mode in ("fox", "sb") else 1.0
        cols = [slice(hh * LANES, (hh + 1) * LANES) if wide else slice(None) for hh in range(nhb)]
        if wide:
            prod = dov * o_ref[...]
            qs = [q_ref[:, cols[hh]] for hh in range(nhb)]
            dos = [dov[:, cols[hh]].astype(BF16) for hh in range(nhb)]
            deltas = [jnp.sum(prod[:, cols[hh]], axis=-1, keepdims=True) for hh in range(nhb)]
        else:
            qf = q_ref[...].astype(F32) * q_scale
            prod = dov * o_ref[...]
            hms = [(lane // HEAD) == hh for hh in range(nhb)]
            qs = [jnp.where(hm, qf, 0.0).astype(BF16) for hm in hms]
            dos = [jnp.where(hm, dov, 0.0).astype(BF16) for hm in hms]
            deltas = [jnp.sum(jnp.where(hm, prod, 0.0), axis=-1, keepdims=True) for hm in hms]
        st = st_ref[0]
        stats = [st[:, hh * HEAD:hh * HEAD + 1] for hh in range(nhb)]
        if mode == "fox":
            cqs = [_head_pick(cc_ref[...], h, 1) for h in heads]
        if mode == "sb":
            r1 = lax.broadcasted_iota(jnp.int32, (TQ, TQ), 0)
            c1 = lax.broadcasted_iota(jnp.int32, (TQ, TQ), 1)
            u_upto = (r1 <= c1).astype(BF16)
            u_before = (r1 < c1).astype(BF16)

        def chunk(c):
            return pl.ds(pl.multiple_of(c * w, w), w)

        def scores(c):
            js = chunk(c)
            if mode == "sb":
                return tuple((_dot_nt(qs[hh], k_ref[js, cols[hh]]), None) for hh in range(nhb))
            return tuple((_dot_nt(qs[hh], k_ref[js, cols[hh]]), _dot_nt(dos[hh], v_ref[js, cols[hh]])) for hh in range(nhb))

        def emit(hh, js, ds_b, pr_b, dq):
            dk_ref[js, cols[hh]] += _dot_tn(ds_b, qs[hh])
            dv_ref[js, cols[hh]] += _dot_tn(pr_b, dos[hh])
            return dq + _dot(ds_b, k_ref[js, cols[hh]])

        def head_step(hh, c, js, sc_dp, carry, last):
            sc, dp = sc_dp
            if dp is None:
                dp = _dot_nt(dos[hh], v_ref[js, cols[hh]])
            if softmax:
                dq, rsum = carry
                if mode == "fox":
                    ck = _head_pick(cr_ref[:, js], heads[hh], 0)
                    sc = sc + (cqs[hh] - ck)
                else:
                    sc = sc * _MLA_SCALE
                if last:
                    sc = jnp.where(visible(c), sc, NEG)
                pr = jnp.exp(sc - stats[hh])
                ds = pr * (dp - deltas[hh])
                if mode == "fox":
                    dck_ref[0, hh:hh + 1, js] += jnp.sum(ds, axis=0, keepdims=True)
                    rsum = rsum + jnp.sum(ds, axis=-1, keepdims=True)
                if mode == "mla":
                    ds = ds * _MLA_SCALE
                return emit(hh, js, ds.astype(BF16), pr.astype(BF16), dq), rsum
            seen, gsum, dq = carry
            z = sc
            log_beta = jnp.minimum(z, 0.0) - jnp.log(1.0 + jnp.exp(-jnp.abs(z)))
            log_stay = log_beta - z
            if last:
                vis = visible(c)
                log_stay = jnp.where(vis, log_stay, 0.0)
            parts = []
            for b in range(nsub):
                ls_b = log_stay[:, b * TQ:(b + 1) * TQ]
                parts.append((stats[hh] - seen) - _dot2(ls_b, u_upto))
                seen = seen + jnp.sum(ls_b, axis=-1, keepdims=True)
            later = parts[0] if nsub == 1 else jnp.concatenate(parts, axis=1)
            wgt = jnp.exp(log_beta + later)
            if last:
                wgt = jnp.where(vis, wgt, 0.0)
            g = dp * wgt
            parts = []
            for b in range(nsub):
                g_b = g[:, b * TQ:(b + 1) * TQ]
                parts.append(gsum + _dot2(g_b, u_before))
                gsum = gsum + jnp.sum(g_b, axis=-1, keepdims=True)
            before = parts[0] if nsub == 1 else jnp.concatenate(parts, axis=1)
            beta = jnp.exp(log_beta)
            dz = g * (1.0 - beta) - beta * before
            if last:
                dz = jnp.where(vis, dz, 0.0)
            return seen, gsum, emit(hh, js, dz.astype(BF16), wgt.astype(BF16), dq)

        def step(c, c_next, state, last):
            scs, carries = state
            nxt = scores(c_next) if c_next is not None else None
            js = chunk(c)
            return nxt, tuple(head_step(hh, c, js, scs[hh], carries[hh], last) for hh in range(nhb))

        zero_acc = jnp.zeros((QB, LANES), F32)
        zero1 = jnp.zeros((QB, 1), F32)
        if softmax:
            init = tuple((zero_acc, zero1) for _ in range(nhb))
        else:
            init = tuple((zero1, zero1, zero_acc) for _ in range(nhb))
        state = lax.fori_loop(0, nfull, lambda c, st: step(c, c + 1, st, False), (scores(0), init))
        _, carries = step(nfull, None, state, True)
        if softmax:
            dqs = [dq for (dq, rsum) in carries]
        else:
            dqs = [dq for (seen, gsum, dq) in carries]
        hm0 = (lane // HEAD) == 0
        if wide:
            for hh in range(nhb):
                dq_ref[:, cols[hh]] = dqs[hh]
        else:
            dq_ref[...] = jnp.where(hm0, dqs[0], dqs[1]) * q_scale
        if mode == "fox":
            drs_ref[0] = jnp.where(hm0, carries[0][1], carries[1][1])

    bw = 2 * LANES if mode == "mla" else LANES
    pair_blk = pl.BlockSpec((QB, bw), lambda p, i: (i, p))
    full_blk = pl.BlockSpec((s, bw), lambda p, i: (0, p))
    stat_blk = pl.BlockSpec((1, QB, LANES), lambda p, i: (p, i, 0))
    in_specs = [q_spec, k_spec, v_spec, pair_blk, pair_blk]
    args = [qa, ka, va, o, do]
    in_specs.append(stat_blk)
    args.append(stat)
    if mode == "fox":
        in_specs += [pl.BlockSpec((QB, LANES), lambda p, i: (i, 0)), pl.BlockSpec((8, s), lambda p, i: (0, 0))]
        args += [cum_col, cum_row]
    out_specs = [pair_blk, full_blk, full_blk]
    out_shape = [jax.ShapeDtypeStruct((s, nblk * bw), F32)] * 3
    if mode == "fox":
        out_specs += [pl.BlockSpec((1, 8, s), lambda p, i: (p, 0, 0)), stat_blk]
        out_shape += [jax.ShapeDtypeStruct((2, 8, s), F32), jax.ShapeDtypeStruct((2, s, LANES), F32)]
    side_in, side_out, side_scratch = _side_specs(side)
    res = pl.pallas_call(
        _carry_side_job(body, len(args), len(out_shape), side, (nblk, nq)), name=mode + "_bwd", grid=(nblk, nq),
        in_specs=in_specs + side_in, out_specs=out_specs + side_out,
        out_shape=out_shape + ([] if side is None else side.out_shape), scratch_shapes=side_scratch,
        compiler_params=_cparams(("parallel", "arbitrary") if side is None else ("arbitrary", "arbitrary")),
    )(*args, *([] if side is None else side.inputs))
    return res if side is None else (*res[:len(out_shape)], res[len(out_shape):])


def _ret_geometry(p):
    lane = _lane((1, LANES))
    lg_lane = jnp.where(lane < HEAD, _log_gamma_of(2 * p), _log_gamma_of(2 * p + 1))
    a = lax.broadcasted_iota(jnp.int32, (TQ, 1), 0).astype(F32)
    row = lax.broadcasted_iota(jnp.int32, (TQ, TQ), 0)
    col = lax.broadcasted_iota(jnp.int32, (TQ, TQ), 1)
    same_chunk_or_earlier = (col // CHUNK) <= (row // CHUNK)
    gap = jnp.abs(row - col).astype(F32)
    decays = [jnp.where(same_chunk_or_earlier, jnp.exp(_log_gamma_of(2 * p + hh) * gap), 0.0) for hh in range(2)]
    r = lax.broadcasted_iota(jnp.int32, (LANES, LANES), 0)
    c = lax.broadcasted_iota(jnp.int32, (LANES, LANES), 1)
    own_head = (r // HEAD) == (c // HEAD)
    return lane, lg_lane, a, decays, own_head


def _ret_fwd(qa, ka, va, v_off):
    s = qa.shape[0]
    nq = s // TQ

    def body(q_ref, k_ref, v_ref, o_ref, st_ref, state):
        p = pl.program_id(0)

        @pl.when(pl.program_id(1) == 0)
        def _():
            state[...] = jnp.zeros_like(state)

        lane, lg_lane, a, decays, own_head = _ret_geometry(p)
        q = q_ref[...].astype(F32)
        k = k_ref[...]
        v = v_ref[...]
        s_in = state[...]
        st_ref[0, 0] = s_in
        out = _dot((q * jnp.exp(lg_lane * (a + 1.0))).astype(BF16), s_in.astype(BF16))
        for hh in range(2):
            hm = (lane // HEAD) == hh
            qh = jnp.where(hm, q, 0.0).astype(BF16)
            inner = _dot((_dot_nt(qh, k) * decays[hh]).astype(BF16), v)
            out = out + jnp.where(hm, inner, 0.0)
        o_ref[...] = out
        k_tail = (k.astype(F32) * jnp.exp(lg_lane * (TQ - 1.0 - a))).astype(BF16)
        state[...] = jnp.exp(lg_lane * float(TQ)) * s_in + jnp.where(own_head, _dot_tn(k_tail, v), 0.0)

    blk = lambda off: pl.BlockSpec((TQ, LANES), lambda p, i: (i, off + p))
    return pl.pallas_call(
        body, name="ret_fwd", grid=(2, nq), in_specs=[blk(0), blk(0), blk(v_off)],
        out_specs=[blk(0), pl.BlockSpec((1, 1, LANES, LANES), lambda p, i: (p, i, 0, 0))],
        out_shape=[jax.ShapeDtypeStruct((s, 2 * LANES), F32), jax.ShapeDtypeStruct((2, nq, LANES, LANES), F32)],
        scratch_shapes=[pltpu.VMEM((LANES, LANES), F32)],
        compiler_params=_cparams(("parallel", "arbitrary")),
    )(qa, ka, va)


def _ret_bwd(qa, ka, va, v_off, states, do):
    s = qa.shape[0]
    nq = s // TQ

    def body(q_ref, k_ref, v_ref, st_ref, do_ref, dq_ref, dk_ref, dv_ref, dstate):
        p = pl.program_id(0)

        @pl.when(pl.program_id(1) == 0)
        def _():
            dstate[...] = jnp.zeros_like(dstate)

        lane, lg_lane, a, decays, own_head = _ret_geometry(p)
        q = q_ref[...].astype(F32)
        k = k_ref[...]
        kf = k.astype(F32)
        v = v_ref[...]
        dov = do_ref[...]
        s_in = st_ref[0, 0].astype(BF16)
        ds_next = dstate[...]
        ds_b = ds_next.astype(BF16)
        head_decay = jnp.exp(lg_lane * (a + 1.0))
        tail_decay = jnp.exp(lg_lane * (TQ - 1.0 - a))
        k_tail = (kf * tail_decay).astype(BF16)
        dq = _dot_nt(dov.astype(BF16), s_in) * head_decay
        dk = _dot_nt(v, ds_b) * tail_decay
        dv = _dot(k_tail, ds_b)
        for hh in range(2):
            hm = (lane // HEAD) == hh
            qh = jnp.where(hm, q, 0.0).astype(BF16)
            doh = jnp.where(hm, dov, 0.0).astype(BF16)
            att = (_dot_nt(qh, k) * decays[hh]).astype(BF16)
            datt = (_dot_nt(doh, v) * decays[hh]).astype(BF16)
            dv = dv + _dot_tn(att, doh)
            dk = dk + _dot_tn(datt, qh)
            dq = dq + jnp.where(hm, _dot(datt, k), 0.0)
        dq_ref[...] = dq
        dk_ref[...] = dk
        dv_ref[...] = dv
        q_head = (q * head_decay).astype(BF16)
        dstate[...] = jnp.exp(lg_lane * float(TQ)) * ds_next + jnp.where(own_head, _dot_tn(q_head, dov.astype(BF16)), 0.0)

    blk = lambda off: pl.BlockSpec((TQ, LANES), lambda p, i: (nq - 1 - i, off + p))
    return pl.pallas_call(
        body, name="ret_bwd", grid=(2, nq),
        in_specs=[blk(0), blk(0), blk(v_off), pl.BlockSpec((1, 1, LANES, LANES), lambda p, i: (p, nq - 1 - i, 0, 0)), blk(0)],
        out_specs=[blk(0)] * 3, out_shape=[jax.ShapeDtypeStruct((s, 2 * LANES), F32)] * 3,
        scratch_shapes=[pltpu.VMEM((LANES, LANES), F32)],
        compiler_params=_cparams(("parallel", "arbitrary")),
    )(qa, ka, va, states, do)


def _seg_mean_matrix():
    r = lax.broadcasted_iota(jnp.int32, (GROUP, GROUP), 0)
    c = lax.broadcasted_iota(jnp.int32, (GROUP, GROUP), 1)
    return jnp.where((r // HEAD) == (c // HEAD), 1.0 / HEAD, 0.0).astype(BF16)


def _seg_mean(x, seg):
    h = x.astype(BF16)
    r = x - h.astype(F32)
    m = r.astype(BF16)
    lo = (r - m.astype(F32)).astype(BF16)
    return _dot(h, seg) + _dot(m, seg) + _dot(lo, seg)


def _sigmoid(x):
    return 1.0 / (1.0 + jnp.exp(-x))


def _mix_post(oa, ob, oc, od, proj, g):
    s = oa.shape[0]
    tr = _tile(s, 256)

    def body(a_ref, b_ref, c_ref, d_ref, rg_ref, g_ref, o_ref):
        gv = g_ref[...]
        o_ref[:, 0:GROUP] = _rms(a_ref[...], gv[:, 0:GROUP]).astype(BF16)
        o_ref[:, GROUP:2 * GROUP] = _rms(b_ref[...], gv[:, GROUP:2 * GROUP]).astype(BF16)
        seg = _seg_mean_matrix()
        c = c_ref[...]
        cen = c - _seg_mean(c, seg)
        n = cen * lax.rsqrt(_seg_mean(cen * cen, seg) + EPS)
        rg = rg_ref[...]
        o_ref[:, 2 * GROUP:3 * GROUP] = (n * gv[:, 2 * GROUP:3 * GROUP] * (rg * _sigmoid(rg))).astype(BF16)
        o_ref[:, 3 * GROUP:] = _rms(d_ref[...], gv[:, 3 * GROUP:]).astype(BF16)

    blk = pl.BlockSpec((tr, GROUP), lambda i: (i, 0))
    return pl.pallas_call(
        body, name="mix_post", grid=(s // tr,),
        in_specs=[blk] * 4 + [pl.BlockSpec((tr, GROUP), lambda i: (i, OFF_RG // 2)), pl.BlockSpec((1, D_MODEL), lambda i: (0, 0))],
        out_specs=pl.BlockSpec((tr, D_MODEL), lambda i: (i, 0)), out_shape=jax.ShapeDtypeStruct((s, D_MODEL), BF16),
        compiler_params=_cparams(("parallel",)),
    )(oa, ob, oc, od, proj, g.reshape(1, D_MODEL))


def _mix_post_bwd(dmixed, oa, ob, oc, od, proj, g):
    s = oa.shape[0]
    tr = _tile(s, 256)

    def body(dm_ref, a_ref, b_ref, c_ref, d_ref, rg_ref, g_ref, da_ref, db_ref, dc_ref, dd_ref, drg_ref, dg_ref):
        @pl.when(pl.program_id(0) == 0)
        def _():
            dg_ref[...] = jnp.zeros_like(dg_ref)

        gv = g_ref[...]
        dm = dm_ref[...]
        for k, (x_ref, dx_ref) in enumerate(((a_ref, da_ref), (b_ref, db_ref), (None, None), (d_ref, dd_ref))):
            if x_ref is None:
                continue
            cols = slice(k * GROUP, (k + 1) * GROUP)
            dx, gterm = _rms_bwd(x_ref[...], gv[:, cols], dm[:, cols])
            dx_ref[...] = dx
            dg_ref[:, cols] += jnp.sum(gterm, axis=0, keepdims=True)
        cols = slice(2 * GROUP, 3 * GROUP)
        seg = _seg_mean_matrix()
        c = c_ref[...]
        cen = c - _seg_mean(c, seg)
        rstd = lax.rsqrt(_seg_mean(cen * cen, seg) + EPS)
        n = cen * rstd
        rg = rg_ref[...]
        sg = _sigmoid(rg)
        gate = rg * sg
        dy = dm[:, cols]
        gc = gv[:, cols]
        dn = dy * gc * gate
        dg_ref[:, cols] += jnp.sum(dy * n * gate, axis=0, keepdims=True)
        drg_ref[...] = (dy * n * gc * (sg * (1.0 + rg * (1.0 - sg)))).astype(BF16)
        dc_ref[...] = rstd * (dn - _seg_mean(dn, seg) - n * _seg_mean(dn * n, seg))

    blk = pl.BlockSpec((tr, GROUP), lambda i: (i, 0))
    gsp = pl.BlockSpec((1, D_MODEL), lambda i: (0, 0))
    return pl.pallas_call(
        body, name="mix_post_bwd", grid=(s // tr,),
        in_specs=[pl.BlockSpec((tr, D_MODEL), lambda i: (i, 0))] + [blk] * 4 + [pl.BlockSpec((tr, GROUP), lambda i: (i, OFF_RG // 2)), gsp],
        out_specs=[blk] * 5 + [gsp],
        out_shape=[jax.ShapeDtypeStruct((s, GROUP), F32)] * 4 + [jax.ShapeDtypeStruct((s, GROUP), BF16), jax.ShapeDtypeStruct((1, D_MODEL), F32)],
        compiler_params=_cparams(("arbitrary",)),
    )(dmixed, oa, ob, oc, od, proj, g.reshape(1, D_MODEL))


def _pack_w_in(w):
    z = lambda n: jnp.zeros((w.shape[0], n), w.dtype)
    misc = jnp.concatenate([w[:, 768:772], z(KR_LANE - N_HEADS), w[:, 1156:1188], z(LANES - KR_LANE - ROPE_DIM)], axis=1)
    return jnp.concatenate([w[:, 0:768], w[:, 772:1028], w[:, 1188:2980], w[:, 1028:1156], misc], axis=1)


def _unpack_dw_in(d):
    m = OFF_MISC * LANES
    return jnp.concatenate([d[:, 0:768], d[:, m:m + N_HEADS], d[:, 768:1024], d[:, OFF_CKV * LANES:m],
                            d[:, m + KR_LANE:m + KR_LANE + ROPE_DIM], d[:, 1024:OFF_CKV * LANES]], axis=1)


def _pack_w_q(w):
    return jnp.pad(w.reshape(Q_RANK, N_HEADS, HEAD + ROPE_DIM), ((0, 0), (0, 0), (0, LANES - HEAD - ROPE_DIM))).reshape(Q_RANK, 4 * LANES)


def _unpack_dw_q(d):
    return d.reshape(Q_RANK, N_HEADS, LANES)[:, :, :HEAD + ROPE_DIM].reshape(Q_RANK, N_HEADS * (HEAD + ROPE_DIM))


def _pack_w_kv(w):
    w4 = w.reshape(KV_RANK, N_HEADS, 2 * HEAD)
    widen = lambda a: jnp.pad(a, ((0, 0), (0, 0), (0, LANES - HEAD))).reshape(KV_RANK, N_HEADS * LANES)
    return widen(w4[:, :, :HEAD]), widen(w4[:, :, HEAD:])


def _unpack_dw_kv(dk, dv):
    narrow = lambda a: a.reshape(KV_RANK, N_HEADS, LANES)[:, :, :HEAD]
    return jnp.concatenate([narrow(dk), narrow(dv)], axis=2).reshape(KV_RANK, 2 * N_HEADS * HEAD)


def _narrow_heads(a):
    return a.reshape(a.shape[0], N_HEADS, LANES)[:, :, :HEAD].reshape(a.shape[0], N_HEADS * HEAD)


def _widen_heads(a):
    return jnp.pad(a.reshape(a.shape[0], N_HEADS, HEAD), ((0, 0), (0, 0), (0, LANES - HEAD))).reshape(a.shape[0], N_HEADS * LANES)


def _layer_fwd(x, lw, tabs, tag, side=None, fox_side=None, late_weights=None, h1=None, next_gain=None):
    cos_m, sin_m, cos_r, sin_r = tabs
    if h1 is None:
        h1 = _norm_fwd(x, lw["g_mix_pre"], name=tag + "pre_norm")
    proj, projb = _matmul(h1, lw["w_in"], name=tag + "in_proj", also_bf16=True)
    bias_row = jnp.pad(lw["b_forget"], (FF_LANE, LANES - N_HEADS - FF_LANE)).reshape(1, LANES)
    cum_col, cum_row = _fox_cum(proj, bias_row)
    oa, lse_a, *fox_carried = _mixer_fwd("fox", projb, OFF_FQ, projb, OFF_FK, projb, OFF_FV, cum_col=cum_col, cum_row=cum_row,
                                         side=fox_side)
    if late_weights is not None:
        lw = {**lw, **late_weights(fox_carried[0])}
    qm, km, vm, cqn, ckvn = _mla_prep(proj, cos_m, sin_m, lw["g_q_lora"], lw["g_kv_lora"], lw["wq"], lw["wk"], lw["wv"])
    ob_wide, lse_b = _mixer_fwd("mla", qm, 0, km, 0, vm, 0)
    ob = _narrow_heads(ob_wide)
    qr, kr = _ret_prep(proj, cos_r, sin_r)
    oc, ret_states = _ret_fwd(qr, kr, projb, OFF_RV)
    od, tot_d, *carried = _mixer_fwd("sb", projb, OFF_SQ, projb, OFF_SK, projb, OFF_SV, side=side)
    mixed = _mix_post(oa, ob, oc, od, proj, lw["g_mix_out"])
    mix = _matmul(mixed, lw["w_out"], name=tag + "out_proj")
    x1, h2 = _norm_fwd(mix, lw["g_mix_post"], name=tag + "mix_post_norm", resid=x, out_dtype=F32, next_gain=lw["g_ffn_pre"])
    u = _matmul(h2, lw["w_ffn_up"], name=tag + "ffn_up", relu2=True, out_dtype=BF16, col_blocks=True)
    f = _matmul(u, lw["w_ffn_down"], name=tag + "ffn_down")
    x2, h_next = None, None
    if next_gain is not None:
        x2, h_next = _norm_fwd(f, lw["g_ffn_post"], name=tag + "ffn_post_norm", resid=x1, out_dtype=F32, next_gain=next_gain)
    saved = dict(x=x, h1=h1, proj=proj, projb=projb, bias_row=bias_row, cum_col=cum_col, cum_row=cum_row, oa=oa, lse_a=lse_a,
                 qm=qm, km=km, vm=vm, cqn=cqn, ckvn=ckvn, ob=ob, ob_wide=ob_wide, lse_b=lse_b, qr=qr, kr=kr, ret_states=ret_states, oc=oc, od=od, tot_d=tot_d, mixed=mixed,
                 mix=mix, x1=x1, h2=h2, u=u, f=f)
    return x2, saved, lw, (carried[0] if carried else None), h_next


def _layer_bwd(dx2, lw, sv, tabs, tag, side=None, ffn_side=None, fox_side=None, post_given=None, then_prev=None):
    cos_m, sin_m, cos_r, sin_r = tabs
    g = {}
    if post_given is None:
        df, g["g_ffn_post"] = _norm_bwd(sv["f"], lw["g_ffn_post"], dx2, name=tag + "ffn_post_norm_bwd", out_dtype=BF16)
    else:
        df, g["g_ffn_post"] = post_given
    du_pre = _matmul(df, lw["w_ffn_down"], name=tag + "ffn_down_dx", tb=True, out_dtype=BF16, relu2_of=sv["u"], side=ffn_side)
    ffn_carried = None
    if ffn_side is not None:
        du_pre, ffn_carried = du_pre
    g["w_ffn_down"] = _matmul(sv["u"], df, name=tag + "ffn_down_dw", ta=True)
    dh2 = _matmul(du_pre, lw["w_ffn_up"], name=tag + "ffn_up_dx", tb=True, col_blocks=True)
    g["w_ffn_up"] = _matmul(sv["h2"], du_pre, name=tag + "ffn_up_dw", ta=True, col_blocks=True)
    dx1, g["g_ffn_pre"], dmix, g["g_mix_post"] = _norm_bwd(sv["x1"], lw["g_ffn_pre"], dh2, name=tag + "ffn_pre_norm_bwd", add=dx2,
                                                           then=(sv["mix"], lw["g_mix_post"]))
    dmixed = _matmul(dmix, lw["w_out"], name=tag + "out_proj_dx", tb=True)
    g["w_out"] = _matmul(sv["mixed"], dmix, name=tag + "out_proj_dw", ta=True)
    proj, projb = sv["proj"], sv["projb"]
    doa, dob, doc, dod, drg, g["g_mix_out"] = _mix_post_bwd(dmixed, sv["oa"], sv["ob"], sv["oc"], sv["od"], proj, lw["g_mix_out"])
    dfq, dfk, dfv, dck, drs, *fox_carried = _mixer_bwd(
        "fox", projb, OFF_FQ, projb, OFF_FK, projb, OFF_FV, sv["oa"], doa, stat=sv["lse_a"], cum_col=sv["cum_col"],
        cum_row=sv["cum_row"], side=None if fox_side is None else fox_side(g))
    dqm, dkm, dvm = _mixer_bwd("mla", sv["qm"], 0, sv["km"], 0, sv["vm"], 0, sv["ob_wide"], _widen_heads(dob), stat=sv["lse_b"])
    dcq, dckv, dkr, dwq, dwk, dwv, g["g_q_lora"], g["g_kv_lora"] = _mla_prep_bwd(
        dqm, dkm, dvm, proj, sv["cqn"], sv["ckvn"], cos_m, sin_m, lw["g_q_lora"], lw["g_kv_lora"], lw["wq"], lw["wk"], lw["wv"])
    dqr, dkr_ret, drv = _ret_bwd(sv["qr"], sv["kr"], projb, OFF_RV, sv["ret_states"], doc)
    drq, drk = _ret_prep_bwd(dqr, dkr_ret, cos_r, sin_r)
    if callable(side):
        side = side(g, ffn_carried, fox_carried[0] if fox_carried else None)
    dsq, dsk, dsv, *carried = _mixer_bwd("sb", projb, OFF_SQ, projb, OFF_SK, projb, OFF_SV, sv["od"], dod, stat=sv["tot_d"], side=side)
    dmisc, db_row = _fox_gate_bwd(dck, drs, proj, sv["bias_row"], dkr)
    b = lambda a: a.astype(BF16)
    dproj = jnp.concatenate([b(dfq), b(dfk), b(dfv), dcq, drq, drk, b(drv), drg, b(dsq), b(dsk), b(dsv), dckv, dmisc], axis=1)
    dh1 = _matmul(dproj, lw["w_in"], name=tag + "in_proj_dx", tb=True)
    g["w_in"] = _matmul(sv["h1"], dproj, name=tag + "in_proj_dw", ta=True)
    dx, g["g_mix_pre"], *prev_post = _norm_bwd(sv["x"], lw["g_mix_pre"], dh1, name=tag + "pre_norm_bwd", add=dx1, then=then_prev)
    g["b_forget"] = db_row[0, FF_LANE:FF_LANE + N_HEADS]
    g["wq"], g["wk"], g["wv"] = dwq, dwk, dwv
    return dx, g, (carried[0] if carried else None), (tuple(prev_post) if prev_post else None)


def _local_step(x, positions, layers, target):
    s = x.shape[0]
    tabs = _rope_tables(positions.reshape(s, 1))
    saved, h1 = [], None
    for li, lw in enumerate(layers):
        nxt = layers[li + 1]["g_mix_pre"] if li + 1 < len(layers) else None
        x, sv, _, _, h1 = _layer_fwd(x, lw, tabs, "l%d_" % li, h1=h1, next_gain=nxt)
        saved.append(sv)
    loss_row, dx, df, dg = _loss_head(saved[-1]["f"], layers[-1]["g_ffn_post"], saved[-1]["x1"], target)
    grads, post = [None] * len(layers), (df, dg)
    for li in reversed(range(len(layers))):
        prev = (saved[li - 1]["f"], layers[li - 1]["g_ffn_post"]) if li > 0 else None
        dx, grads[li], _, post = _layer_bwd(dx, layers[li], saved[li], tabs, "l%d_" % li, post_given=post, then_prev=prev)
    return loss_row[0, 0], dx, grads


def _adamw(w, g, m, v, *, name):
    d, r, c = w.shape
    tr = 256 if r % 256 == 0 else r
    blk = pl.BlockSpec((None, tr, c), lambda l, i: (l, i, 0))
    c1 = 1.0 - ADAM_B1 ** ADAM_STEP
    c2 = 1.0 - ADAM_B2 ** ADAM_STEP

    def body(w_ref, g_ref, m_ref, v_ref, d_ref, mo_ref, vo_ref):
        gv = g_ref[...]
        mn = ADAM_B1 * m_ref[...] + (1.0 - ADAM_B1) * gv
        vn = ADAM_B2 * v_ref[...] + (1.0 - ADAM_B2) * jnp.square(gv)
        mo_ref[...] = mn
        vo_ref[...] = vn
        d_ref[...] = -ADAM_LR * ((mn / c1) / (jnp.sqrt(vn / c2) + ADAM_EPS) + ADAM_WD * w_ref[...])

    return pl.pallas_call(
        body, name=name, grid=(d, r // tr), in_specs=[blk] * 4, out_specs=[blk] * 3,
        out_shape=[jax.ShapeDtypeStruct((d, r, c), F32)] * 3, compiler_params=_cparams(("parallel", "parallel")),
    )(w, g, m, v)


SC_TILES = 32
SC_ROWS = 8


def _adamw_sparsecore(ws, gs, ms, vs, *, name):
    n = len(ws)
    c = ws[0].shape[2]
    c1 = 1.0 - ADAM_B1 ** ADAM_STEP
    c2 = 1.0 - ADAM_B2 ** ADAM_STEP
    pieces = [(t, l, r0) for t in range(n) for l in range(ws[t].shape[0]) for r0 in range(0, ws[t].shape[1] // SC_TILES, SC_ROWS)]

    def body(*refs):
        ins, outs = refs[:4 * n], refs[4 * n:7 * n]
        bufs, sem_in, sem_out = refs[7 * n:7 * n + 8], refs[7 * n + 8], refs[7 * n + 9]
        tile = lax.axis_index("subcore") * 2 + lax.axis_index("core")

        def window(k):
            t, l, r0 = pieces[k]
            return t, (l, pl.ds(tile * (ws[t].shape[1] // SC_TILES) + r0, SC_ROWS))

        def loads(k):
            t, at = window(k)
            return [pltpu.make_async_copy(ins[j * n + t].at[at], bufs[4 * (k % 2) + j], sem_in.at[k % 2]) for j in range(4)]

        def stores(k):
            t, at = window(k)
            return [pltpu.make_async_copy(bufs[4 * (k % 2) + j], outs[(j - 1) * n + t].at[at], sem_out.at[k % 2]) for j in (1, 2, 3)]

        def update(k):
            gb, wb, mb, vb = bufs[4 * (k % 2):4 * (k % 2) + 4]

            def adam(gv, wv, mv, vv):
                mn = ADAM_B1 * mv + (1.0 - ADAM_B1) * gv
                vn = ADAM_B2 * vv + (1.0 - ADAM_B2) * (gv * gv)
                return -ADAM_LR * ((mn / c1) / (jnp.sqrt(vn / c2) + ADAM_EPS) + ADAM_WD * wv), mn, vn

            @pl.loop(0, SC_ROWS)
            def _(rr):
                last = (rr, pl.ds(c - 16, 16))
                if c % 16:
                    end = adam(gb[last], wb[last], mb[last], vb[last])

                @pl.loop(0, c // 16 * 16, step=16)
                def _(i):
                    s = (rr, pl.ds(i, 16))
                    wb[s], mb[s], vb[s] = adam(gb[s], wb[s], mb[s], vb[s])

                if c % 16:
                    wb[last], mb[last], vb[last] = end

        for cp in loads(0):
            cp.start()
        for k in range(len(pieces)):
            if k + 1 < len(pieces):
                if k >= 1:
                    for cp in stores(k - 1):
                        cp.wait()
                for cp in loads(k + 1):
                    cp.start()
            for cp in loads(k):
                cp.wait()
            update(k)
            for cp in stores(k):
                cp.start()
        for k in range(max(len(pieces) - 2, 0), len(pieces)):
            for cp in stores(k):
                cp.wait()

    out = pl.kernel(
        body, name=name, out_type=[jax.ShapeDtypeStruct(t.shape, F32) for t in ws] * 3,
        mesh=plsc.VectorSubcoreMesh(core_axis_name="core", subcore_axis_name="subcore"),
        scratch_types=[pltpu.VMEM((SC_ROWS, c), F32)] * 8 + [pltpu.SemaphoreType.DMA((2,)), pltpu.SemaphoreType.DMA((2,))],
    )(*gs, *ws, *ms, *vs)
    return out[:n], out[n:2 * n], out[2 * n:]


BIG = ("w_in", "w_q_up", "w_kv_up", "w_out", "w_ffn_up", "w_ffn_down")
SMALL = ("g_mix_pre", "b_forget", "g_q_lora", "g_kv_lora", "g_mix_out", "g_mix_post", "g_ffn_pre", "g_ffn_post")
N_CHIPS = 4
ANY = pl.BlockSpec(memory_space=pl.ANY)


def _mesh_pos():
    return lax.axis_index("x"), lax.axis_index("y"), lax.axis_index("c")


def _other_chips(x, y):
    return [(1 - x, y), (x, 1 - y), (1 - x, 1 - y)]


def _rows_half(ref, half):
    h = ref.shape[-2] // 2
    return ref.at[(slice(None),) * (len(ref.shape) - 2) + (pl.ds(half * h, h), slice(None))]


def _remote(src, dst, send_sem, recv_sem, device):
    return pltpu.make_async_remote_copy(src_ref=src, dst_ref=dst, send_sem=send_sem, recv_sem=recv_sem, device_id=device,
                                        device_id_type=MESH)


def _comm_call(body, name, args, out_shape, n_sems):
    return pl.pallas_call(
        body, name=name, in_specs=[ANY] * len(args), out_specs=[ANY] * len(out_shape), out_shape=out_shape,
        scratch_shapes=[pltpu.SemaphoreType.DMA((n_sems,)), pltpu.SemaphoreType.DMA((n_sems,))],
        compiler_params=pltpu.CompilerParams(has_side_effects=True),
    )(*args)


def _run_side_job(side, name):
    si = len(side.inputs)

    def body(*refs):
        args = (refs[:si], refs[si:-2], refs[-2], refs[-1])
        sends = side.sends(*args)
        for cp in sends:
            cp.start()
        for cp in side.recvs(*args):
            cp.wait_recv()
        for cp in sends:
            cp.wait_send()

    return _comm_call(body, name, side.inputs, side.out_shape, side.n_sems)


def _gather_job(shards):
    n = len(shards)

    def copies(own_block, ins, outs, send_sems, recv_sems):
        x, y, c = _mesh_pos()
        return [_remote(_rows_half(ins[t], c), _rows_half(outs[t].at[2 * x + y if own_block else 2 * px + py], c),
                        send_sems.at[3 * t + j], recv_sems.at[3 * t + j], (px, py, c))
                for t in range(n) for j, (px, py) in enumerate(_other_chips(x, y))]

    return _SideJob(shards, [jax.ShapeDtypeStruct((N_CHIPS,) + a.shape, a.dtype) for a in shards], 3 * n,
                    functools.partial(copies, True), functools.partial(copies, False))


def _forward_halves(gathered):
    n = len(gathered)

    def body(*refs):
        bufs, send_sems, recv_sems = refs[n:2 * n], refs[-2], refs[-1]
        x, y, c = _mesh_pos()

        def d2d(t, j, block, half):
            region = _rows_half(bufs[t].at[block], half)
            return _remote(region, region, send_sems.at[3 * t + j], recv_sems.at[3 * t + j], (x, y, 1 - c))

        peers = list(enumerate(_other_chips(x, y)))
        sends = [d2d(t, j, 2 * px + py, c) for t in range(n) for j, (px, py) in peers]
        for cp in sends:
            cp.start()
        for t in range(n):
            for j, (px, py) in peers:
                d2d(t, j, 2 * px + py, 1 - c).wait_recv()
        for cp in sends:
            cp.wait_send()

    return pl.pallas_call(
        body, name="gather_forward", in_specs=[ANY] * n, out_specs=[ANY] * n,
        out_shape=[jax.ShapeDtypeStruct(g.shape, g.dtype) for g in gathered], input_output_aliases={t: t for t in range(n)},
        scratch_shapes=[pltpu.SemaphoreType.DMA((3 * n,)), pltpu.SemaphoreType.DMA((3 * n,))],
        compiler_params=pltpu.CompilerParams(has_side_effects=True),
    )(*gathered)


def _exchange_halves_job(gs):
    n = len(gs)

    def copies(ins, outs, send_sems, recv_sems):
        x, y, c = _mesh_pos()
        return [_remote(_rows_half(ins[t], 1 - c), outs[t], send_sems.at[t], recv_sems.at[t], (x, y, 1 - c)) for t in range(n)]

    out_shape = [jax.ShapeDtypeStruct(g.shape[:2] + (g.shape[2] // 2, g.shape[3]), g.dtype) for g in gs]
    return _SideJob(gs, out_shape, n, copies, copies)


def _pair_add(g, r, c_idx, *, name):
    nb, d, rows, cols = g.shape
    h = rows // 2
    tr = min(h, 512)
    nt = h // tr

    def body(c_ref, g_ref, r_ref, p_ref, pb_ref):
        s = g_ref[...] + r_ref[...]
        p_ref[...] = s
        pb_ref[...] = s.astype(BF16)

    blk = pl.BlockSpec((1, 1, tr, cols), lambda k, l, i, c_ref: (k, l, i, 0))
    return pl.pallas_call(
        body, name=name,
        grid_spec=pltpu.PrefetchScalarGridSpec(
            num_scalar_prefetch=1, grid=(nb, d, nt),
            in_specs=[pl.BlockSpec((1, 1, tr, cols), lambda k, l, i, c_ref: (k, l, c_ref[0] * nt + i, 0)), blk],
            out_specs=[blk, blk]),
        out_shape=[jax.ShapeDtypeStruct((nb, d, h, cols), F32), jax.ShapeDtypeStruct((nb, d, h, cols), BF16)],
        compiler_params=_cparams(("parallel", "parallel", "parallel")),
    )(c_idx, g, r)


def _exchange_chips_job(pbs):
    n = len(pbs)

    def copies(ins, outs, send_sems, recv_sems):
        x, y, c = _mesh_pos()
        return [_remote(ins[t].at[2 * px + py], outs[t].at[j], send_sems.at[3 * t + j], recv_sems.at[3 * t + j], (px, py, c))
                for t in range(n) for j, (px, py) in enumerate(_other_chips(x, y))]

    return _SideJob(pbs, [jax.ShapeDtypeStruct((3,) + p.shape[1:], p.dtype) for p in pbs], 3 * n, copies, copies)


def _chip_add(p, r, k_idx, *, name):
    _, d, h, cols = p.shape
    tr = min(h, 512)
    nt = h // tr

    def body(k_ref, p_ref, r_ref, o_ref):
        o_ref[0] = ((p_ref[0, 0] + r_ref[0, 0].astype(F32)) + r_ref[1, 0].astype(F32)) + r_ref[2, 0].astype(F32)

    return pl.pallas_call(
        body, name=name,
        grid_spec=pltpu.PrefetchScalarGridSpec(
            num_scalar_prefetch=1, grid=(d, nt),
            in_specs=[pl.BlockSpec((1, 1, tr, cols), lambda l, i, k_ref: (k_ref[0], l, i, 0)),
                      pl.BlockSpec((3, 1, tr, cols), lambda l, i, k_ref: (0, l, i, 0))],
            out_specs=pl.BlockSpec((1, tr, cols), lambda l, i, k_ref: (l, i, 0))),
        out_shape=jax.ShapeDtypeStruct((d, h, cols), F32), compiler_params=_cparams(("parallel", "parallel")),
    )(k_idx, p, r)


def _share_halves(qs):
    n = len(qs)

    def body(*refs):
        ins, outs, send_sems, recv_sems = refs[:n], refs[n:2 * n], refs[2 * n], refs[2 * n + 1]
        x, y, c = _mesh_pos()
        cps = [_remote(ins[t], outs[t], send_sems.at[t], recv_sems.at[t], (x, y, 1 - c)) for t in range(n)]
        for cp in cps:
            cp.start()
        for cp in cps:
            cp.wait_recv()
        for cp in cps:
            cp.wait_send()

    return _comm_call(body, "grad_pair_share", qs, [jax.ShapeDtypeStruct(q.shape, q.dtype) for q in qs], n)


def _all_reduce_small(v):
    r, cols = v.shape
    n_dev = 8

    def body(v_ref, o_ref, buf, send_sems, recv_sems):
        x, y, c = _mesh_pos()
        me = 4 * x + 2 * y + c
        buf[me] = v_ref[...]

        def peer(j):
            return (1 - x if j & 4 else x, 1 - y if j & 2 else y, 1 - c if j & 1 else c)

        def copy(j, slot):
            return pltpu.make_async_remote_copy(src_ref=v_ref, dst_ref=buf.at[slot], send_sem=send_sems.at[j - 1],
                                                recv_sem=recv_sems.at[j - 1], device_id=peer(j), device_id_type=MESH)

        sends = [copy(j, me) for j in range(1, n_dev)]
        for cp in sends:
            cp.start()
        for j in range(1, n_dev):
            px, py, pc = peer(j)
            copy(j, 4 * px + 2 * py + pc).wait_recv()
        for cp in sends:
            cp.wait_send()
        acc = buf[0]
        for d in range(1, n_dev):
            acc = acc + buf[d]
        o_ref[...] = acc

    vm = pl.BlockSpec(memory_space=pltpu.VMEM)
    return pl.pallas_call(
        body, name="small_all_reduce", in_specs=[vm], out_specs=vm, out_shape=jax.ShapeDtypeStruct((r, cols), F32),
        scratch_shapes=[pltpu.VMEM((n_dev, r, cols), F32), pltpu.SemaphoreType.DMA((n_dev - 1,)), pltpu.SemaphoreType.DMA((n_dev - 1,))],
        compiler_params=pltpu.CompilerParams(has_side_effects=True),
    )(v)


_COL_SHARDED = ("w_in", "w_q_up", "w_kv_up", "w_ffn_up")


def _shard_cols(blocks, a, b):
    c = blocks[0].shape[-1]
    out = []
    while a < b:
        k = a // c
        hi = min(b, (k + 1) * c)
        out.append(blocks[k][:, a - k * c:hi - k * c])
        a = hi
    return out


def _pack_w_in_shards(blocks):
    z = lambda n: [jnp.zeros((blocks[0].shape[0], n), blocks[0].dtype)]
    cols = lambda a, b: _shard_cols(blocks, a, b)
    return jnp.concatenate(cols(0, 768) + cols(772, 1028) + cols(1188, 2980) + cols(1028, 1156) + cols(768, 772)
                           + z(KR_LANE - N_HEADS) + cols(1156, 1188) + z(LANES - KR_LANE - ROPE_DIM), axis=1)


def _whole_layer(name, blocks):
    if name in _COL_SHARDED:
        return jnp.concatenate([blocks[k] for k in range(N_CHIPS)], axis=1)
    return blocks.reshape(N_CHIPS * blocks.shape[1], blocks.shape[2])


def _split_layer(name, whole):
    if name in _COL_SHARDED:
        c = whole.shape[1] // N_CHIPS
        return jnp.stack([whole[:, k * c:(k + 1) * c] for k in range(N_CHIPS)])
    return whole.reshape(N_CHIPS, whole.shape[0] // N_CHIPS, whole.shape[1])


def _small_to_rows(d):
    v = jnp.concatenate([d[k].astype(F32).reshape(-1) for k in SMALL])
    rows = -(-v.shape[0] // (8 * LANES)) * 8
    return jnp.pad(v, (0, rows * LANES - v.shape[0])).reshape(rows, LANES)


def _small_from_rows(rows, shapes):
    v = rows.reshape(-1)
    out, o = {}, 0
    for k in SMALL:
        sz = int(np.prod(shapes[k]))
        out[k] = v[o:o + sz].reshape(shapes[k])
        o += sz
    return out


_ARG_NAMES = ("x", "positions", "g_mix_pre", "w_in", "b_forget", "g_q_lora", "w_q_up", "g_kv_lora", "w_kv_up", "g_mix_out", "w_out",
              "g_mix_post", "g_ffn_pre", "w_ffn_up", "w_ffn_down", "g_ffn_post")
_WEIGHTS = _ARG_NAMES[2:]


def kernel(x, positions, g_mix_pre, w_in, b_forget, g_q_lora, w_q_up, g_kv_lora, w_kv_up, g_mix_out, w_out, g_mix_post, g_ffn_pre, w_ffn_up, w_ffn_down, g_ffn_post, loss_target, m_g_mix_pre, m_w_in, m_b_forget, m_g_q_lora, m_w_q_up, m_g_kv_lora, m_w_kv_up, m_g_mix_out, m_w_out, m_g_mix_post, m_g_ffn_pre, m_w_ffn_up, m_w_ffn_down, m_g_ffn_post, v_g_mix_pre, v_w_in, v_b_forget, v_g_q_lora, v_w_q_up, v_g_kv_lora, v_w_kv_up, v_g_mix_out, v_w_out, v_g_mix_post, v_g_ffn_pre, v_w_ffn_up, v_w_ffn_down, v_g_ffn_post):
    w = dict(g_mix_pre=g_mix_pre, w_in=w_in, b_forget=b_forget, g_q_lora=g_q_lora, w_q_up=w_q_up, g_kv_lora=g_kv_lora, w_kv_up=w_kv_up,
             g_mix_out=g_mix_out, w_out=w_out, g_mix_post=g_mix_post, g_ffn_pre=g_ffn_pre, w_ffn_up=w_ffn_up, w_ffn_down=w_ffn_down,
             g_ffn_post=g_ffn_post)
    m = dict(g_mix_pre=m_g_mix_pre, w_in=m_w_in, b_forget=m_b_forget, g_q_lora=m_g_q_lora, w_q_up=m_w_q_up, g_kv_lora=m_g_kv_lora,
             w_kv_up=m_w_kv_up, g_mix_out=m_g_mix_out, w_out=m_w_out, g_mix_post=m_g_mix_post, g_ffn_pre=m_g_ffn_pre,
             w_ffn_up=m_w_ffn_up, w_ffn_down=m_w_ffn_down, g_ffn_post=m_g_ffn_post)
    v = dict(g_mix_pre=v_g_mix_pre, w_in=v_w_in, b_forget=v_b_forget, g_q_lora=v_g_q_lora, w_q_up=v_w_q_up, g_kv_lora=v_g_kv_lora,
             w_kv_up=v_w_kv_up, g_mix_out=v_g_mix_out, w_out=v_w_out, g_mix_post=v_g_mix_post, g_ffn_pre=v_g_ffn_pre,
             w_ffn_up=v_w_ffn_up, w_ffn_down=v_w_ffn_down, g_ffn_post=v_g_ffn_post)
    small_shapes = {k: w[k].shape for k in SMALL}
    c_idx = lax.axis_index("c").astype(jnp.int32).reshape(1)
    k_idx = (2 * lax.axis_index("x") + lax.axis_index("y")).astype(jnp.int32).reshape(1)
    first_core = lax.axis_index("c") == 0

    mine = 2 * lax.axis_index("x") + lax.axis_index("y")
    shards_b = [{k: w[k][l:l + 1].astype(BF16) for k in BIG} for l in range(DEPTH)]
    gains = [dict(g_mix_pre=g_mix_pre[l], b_forget=b_forget[l], g_q_lora=g_q_lora[l], g_kv_lora=g_kv_lora[l], g_mix_out=g_mix_out[l],
                  g_mix_post=g_mix_post[l], g_ffn_pre=g_ffn_pre[l], g_ffn_post=g_ffn_post[l]) for l in range(DEPTH)]
    FIRST, LATER = ("w_in", "w_q_up", "w_kv_up"), ("w_out", "w_ffn_up", "w_ffn_down")
    EARLY_GRADS, LATE_GRADS = ("w_ffn_down", "w_ffn_up", "w_out"), ("w_in", "w_q_up", "w_kv_up")
    SC_ADAMW = EARLY_GRADS

    def gather_job(l, names):
        return _gather_job([shards_b[l][k] for k in names])

    def weights_of(l, names, gathered):
        four = {k: lax.dynamic_update_slice(g, shards_b[l][k][None], (mine, 0, 0, 0))[:, 0]
                for k, g in zip(names, _forward_halves(gathered))}
        out = {}
        for k in names:
            if k == "w_in":
                out["w_in"] = _pack_w_in_shards(four[k])
            elif k == "w_q_up":
                out["wq"] = _pack_w_q(_whole_layer(k, four[k]))
            elif k == "w_kv_up":
                out["wk"], out["wv"] = _pack_w_kv(_whole_layer(k, four[k]))
            elif k == "w_ffn_up":
                out[k] = four[k]
            else:
                out[k] = _whole_layer(k, four[k])
        return out

    def grad_blocks(names, g):
        whole = dict(w_in=lambda: _unpack_dw_in(g["w_in"]), w_q_up=lambda: _unpack_dw_q(g["wq"]),
                     w_kv_up=lambda: _unpack_dw_kv(g["wk"], g["wv"]), w_out=lambda: g["w_out"], w_ffn_down=lambda: g["w_ffn_down"])
        return [(g[k] if k == "w_ffn_up" else _split_layer(k, whole[k]()))[:, None] for k in names]

    def pair_sums(names, blocks, theirs):
        return [_pair_add(b, r, c_idx, name="grad_pair_add_" + k) for k, b, r in zip(names, blocks, theirs)]

    def exchange_job(*pairs):
        return _exchange_chips_job([pb for pair in pairs for (_, pb) in pair])

    def finish_grads(names, pair, partial):
        half = [_chip_add(p, r, k_idx, name="grad_chip_add_" + k) for k, (p, _), r in zip(names, pair, partial)]
        return {k: jnp.where(first_core, jnp.concatenate([q, s], axis=1), jnp.concatenate([s, q], axis=1))
                for k, q, s in zip(names, half, _share_halves(half))}

    seq = x.shape[1]
    tabs = _rope_tables(positions[0].reshape(seq, 1))
    first0 = weights_of(0, FIRST, _run_side_job(gather_job(0, FIRST), "gather_weights_l0"))
    x1, saved0, lw0, gathered1, h1 = _layer_fwd(x[0], {**gains[0], **first0}, tabs, "l0_", fox_side=gather_job(0, LATER),
                                                late_weights=lambda got: weights_of(0, LATER, got), side=gather_job(1, BIG),
                                                next_gain=gains[1]["g_mix_pre"])
    lw1 = {**gains[1], **weights_of(1, BIG, gathered1)}
    _, saved1, _, _, _ = _layer_fwd(x1, lw1, tabs, "l1_", h1=h1)
    loss_row, dx, df1, dg1 = _loss_head(saved1["f"], lw1["g_ffn_post"], saved1["x1"], loss_target[0])
    loss = lax.psum(loss_row[0, 0], ("x", "y", "c"))
    dx, grads1, _, post0 = _layer_bwd(dx, lw1, saved1, tabs, "l1_", post_given=(df1, dg1),
                                      then_prev=(saved0["f"], lw0["g_ffn_post"]))
    blocks1 = grad_blocks(BIG, grads1)
    early_blocks0, pair1, early0 = [], [], []

    def beside_l0_fox_backward(g):
        early_blocks0.extend(grad_blocks(EARLY_GRADS, g))
        return _exchange_halves_job(early_blocks0)

    def beside_l0_sb_backward(g, theirs1, theirs_early0):
        pair1.extend(pair_sums(BIG, blocks1, theirs1))
        early0.extend(pair_sums(EARLY_GRADS, early_blocks0, theirs_early0))
        return exchange_job(pair1, early0)

    dx, grads0, partial, _ = _layer_bwd(dx, lw0, saved0, tabs, "l0_", ffn_side=_exchange_halves_job(blocks1),
                                        fox_side=beside_l0_fox_backward, side=beside_l0_sb_backward, post_given=post0)
    big1 = finish_grads(BIG, pair1, partial[:len(BIG)])
    big0 = finish_grads(EARLY_GRADS, early0, partial[len(BIG):])
    g_early = {k: jnp.concatenate([big0[k], big1[k]], axis=0) for k in SC_ADAMW}
    sc_delta, sc_m, sc_v = _adamw_sparsecore([w[k] for k in SC_ADAMW], [g_early[k] for k in SC_ADAMW], [m[k] for k in SC_ADAMW],
                                             [v[k] for k in SC_ADAMW], name="adamw_sparsecore")
    late_blocks0 = grad_blocks(LATE_GRADS, grads0)
    late0 = pair_sums(LATE_GRADS, late_blocks0, _run_side_job(_exchange_halves_job(late_blocks0), "grad_pair_exchange_l0"))
    big0.update(finish_grads(LATE_GRADS, late0, _run_side_job(exchange_job(late0), "grad_chip_exchange_l0")))
    g_big = {k: g_early[k] if k in SC_ADAMW else jnp.concatenate([big0[k], big1[k]], axis=0) for k in BIG}
    sc_late = _adamw_sparsecore([w["w_in"]], [g_big["w_in"]], [m["w_in"]], [v["w_in"]], name="adamw_sparsecore_w_in")
    grads = [grads0, grads1]

    g_small_local = {k: jnp.stack([grads[l][k].reshape(small_shapes[k][1:]) for l in range(DEPTH)]) for k in SMALL}
    g_small = _small_from_rows(_all_reduce_small(_small_to_rows(g_small_local)), small_shapes)

    g_all = {**g_big, **g_small}
    delta, new_m, new_v = {}, {}, {}
    for k in BIG:
        if k in SC_ADAMW:
            i = SC_ADAMW.index(k)
            delta[k], new_m[k], new_v[k] = sc_delta[i], sc_m[i], sc_v[i]
        elif k == "w_in":
            delta[k], new_m[k], new_v[k] = [t[0] for t in sc_late]
        else:
            delta[k], new_m[k], new_v[k] = _adamw(w[k], g_all[k], m[k], v[k], name="adamw_" + k)
    ds, ms, vs = _adamw(*[_small_to_rows(t)[None] for t in (w, g_small, m, v)], name="adamw_small")
    delta.update(_small_from_rows(ds, small_shapes))
    new_m.update(_small_from_rows(ms, small_shapes))
    new_v.update(_small_from_rows(vs, small_shapes))

    grad_x = dx.reshape(x.shape)
    return (loss, grad_x, *[g_all[k] for k in _WEIGHTS], *[delta[k] for k in _WEIGHTS], *[new_m[k] for k in _WEIGHTS],
            *[new_v[k] for k in _WEIGHTS])
```

```python
import functools
import math

import numpy as np
import jax
import jax.numpy as jnp
from jax import lax
from jax.experimental import pallas as pl
from jax.experimental.pallas import tpu as pltpu
from jax.experimental.pallas import tpu_sc as plsc

F32 = jnp.float32
BF16 = jnp.bfloat16
MESH = pl.DeviceIdType.MESH

D_MODEL = 1024
DEPTH = 2
CHUNK = 64
GROUP = 256
HEAD = 64
N_HEADS = 4
Q_RANK = 256
KV_RANK = 128
ROPE_DIM = 32
D_FF = 4096
D_IN = 2980
D_INP = 3072
ROPE_BASE = 10000.0
EPS = 1e-6
LANES = 128
TQ = 128
GATE_ROWS = 512
CONTRACT_TILE = 4096
NEG = -1e30

ADAM_LR, ADAM_B1, ADAM_B2, ADAM_EPS, ADAM_WD, ADAM_STEP = 0.001, 0.9, 0.999, 1e-08, 0.01, 10

OFF_FQ, OFF_FK, OFF_FV, OFF_CQ = 0, 2, 4, 6
OFF_RQ, OFF_RK, OFF_RV, OFF_RG = 8, 10, 12, 14
OFF_SQ, OFF_SK, OFF_SV = 16, 18, 20
OFF_CKV, OFF_MISC = 22, 23
FF_LANE, KR_LANE = 0, 64

VMEM_LIMIT = 56 * 1024 * 1024


def _tile(dim, pref):
    return pref if dim % pref == 0 else dim


def _cparams(sem, vmem=None):
    return pltpu.CompilerParams(dimension_semantics=sem, vmem_limit_bytes=vmem or VMEM_LIMIT)


def _dot(a, b):
    return jnp.dot(a, b, preferred_element_type=F32)


def _dot_nt(a, b):
    return lax.dot_general(a, b, (((1,), (1,)), ((), ())), preferred_element_type=F32)


def _dot_tn(a, b):
    return lax.dot_general(a, b, (((0,), (0,)), ((), ())), preferred_element_type=F32)


def _dot_exact(a, b):
    return jnp.dot(a, b, precision=lax.Precision.HIGHEST, preferred_element_type=F32)


def _matmul(a, b, *, name, ta=False, tb=False, out_dtype=F32, tm=1024, tn=1024, tk=CONTRACT_TILE,
            relu2=False, relu2_of=None, also_bf16=False, side=None, col_blocks=False):
    if ta:
        kdim, m = a.shape
    else:
        m, kdim = a.shape
    if col_blocks and not ta:
        n = b.shape[1] if tb else b.shape[0] * b.shape[2]
        if tb:
            kdim = b.shape[0] * b.shape[2]
    else:
        n = b.shape[0] if tb else b.shape[1]
    tm, tn, tk = _tile(m, tm), _tile(n, tn), _tile(kdim, tk)
    nk = kdim // tk
    a_spec = pl.BlockSpec((tk, tm), lambda i, j, k: (k, i)) if ta else pl.BlockSpec((tm, tk), lambda i, j, k: (i, k))
    b_spec = pl.BlockSpec((tn, tk), lambda i, j, k: (j, k)) if tb else pl.BlockSpec((tk, tn), lambda i, j, k: (k, j))
    o_spec = pl.BlockSpec((tm, tn), lambda i, j, k: (i, j))
    if col_blocks and ta:
        o_spec = pl.BlockSpec((None, tm, tn), lambda i, j, k: (j, i, 0))
    elif col_blocks and tb:
        assert tk == kdim
        b_spec = pl.BlockSpec((b.shape[0], tn, b.shape[2]), lambda i, j, k: (0, j, 0))
    elif col_blocks:
        assert b.shape[2] == tn
        b_spec = pl.BlockSpec((None, tk, tn), lambda i, j, k: (j, k, 0))
    two = also_bf16

    def body(*refs):
        refs = list(refs)
        a_ref, b_ref = refs[0], refs[1]
        e_ref = refs[2] if relu2_of is not None else None
        pos = 3 if relu2_of is not None else 2
        o_ref = refs[pos]
        o2_ref = refs[pos + 1] if two else None
        acc_ref = refs[-1]
        k = pl.program_id(2)
        av = a_ref[...].astype(BF16)
        if col_blocks and tb:
            bv = jnp.concatenate([b_ref[q] for q in range(b.shape[0])], axis=1).astype(BF16)
        else:
            bv = b_ref[...].astype(BF16)
        if ta:
            part = _dot_tn(av, bv)
        elif tb:
            part = _dot_nt(av, bv)
        else:
            part = _dot(av, bv)

        @pl.when(k == 0)
        def _():
            acc_ref[...] = part

        @pl.when(k > 0)
        def _():
            acc_ref[...] += part

        @pl.when(k == nk - 1)
        def _():
            r = acc_ref[...]
            if relu2_of is not None:
                r = r * (2.0 * jnp.sqrt(e_ref[...].astype(F32)))
            if relu2:
                r = jnp.square(jnp.maximum(r, 0.0))
            o_ref[...] = r.astype(o_ref.dtype)
            if also_bf16:
                o2_ref[...] = r.astype(BF16)

    in_specs = [a_spec, b_spec]
    args = [a, b]
    if relu2_of is not None:
        in_specs.append(o_spec)
        args.append(relu2_of)
    out_shape = [jax.ShapeDtypeStruct((n // tn, m, tn) if (col_blocks and ta) else (m, n), out_dtype)]
    out_specs = [o_spec]
    if two:
        out_shape.append(jax.ShapeDtypeStruct((m, n), BF16))
        out_specs.append(o_spec)
    grid = (m // tm, n // tn, nk)
    side_in, side_out, side_scratch = _side_specs(side)
    res = pl.pallas_call(
        _carry_side_job(body, len(args), len(out_shape), side, grid), name=name, grid=grid,
        in_specs=in_specs + side_in, out_specs=out_specs + side_out,
        out_shape=out_shape + ([] if side is None else side.out_shape),
        scratch_shapes=[pltpu.VMEM((tm, tn), F32)] + side_scratch,
        compiler_params=_cparams(("parallel", "parallel", "arbitrary") if side is None else ("arbitrary",) * 3),
    )(*args, *([] if side is None else side.inputs))
    main = res[:len(out_shape)]
    main = main if two else main[0]
    return main if side is None else (main, res[len(out_shape):])


def _rms(x, g):
    r = lax.rsqrt(jnp.mean(x * x, axis=-1, keepdims=True) + EPS)
    return x * r * g


def _rms_bwd(x, g, dy):
    r = lax.rsqrt(jnp.mean(x * x, axis=-1, keepdims=True) + EPS)
    xh = x * r
    gdy = dy * g
    dx = r * (gdy - xh * jnp.mean(xh * gdy, axis=-1, keepdims=True))
    return dx, xh * dy


def _norm_fwd(x, g, *, name, resid=None, out_dtype=BF16, next_gain=None):
    s, d = x.shape
    tr = _tile(s, 256)
    row = pl.BlockSpec((tr, d), lambda i: (i, 0))
    gsp = pl.BlockSpec((1, d), lambda i: (0, 0))

    def body(*refs):
        refs = list(refs)
        x_ref, g_ref = refs[:2]
        y = _rms(x_ref[...], g_ref[...])
        pos = 2
        if resid is not None:
            y = refs[pos][...] + y
            pos += 1
        if next_gain is None:
            refs[pos][...] = y.astype(refs[pos].dtype)
        else:
            refs[pos + 1][...] = y.astype(refs[pos + 1].dtype)
            refs[pos + 2][...] = _rms(y, refs[pos][...]).astype(BF16)

    args = [x, g.reshape(1, d)] + ([] if resid is None else [resid]) + ([] if next_gain is None else [next_gain.reshape(1, d)])
    in_specs = [row, gsp] + ([] if resid is None else [row]) + ([] if next_gain is None else [gsp])
    first = jax.ShapeDtypeStruct((s, d), out_dtype)
    if next_gain is None:
        out_specs, out_shape = row, first
    else:
        out_specs, out_shape = [row, row], [first, jax.ShapeDtypeStruct((s, d), BF16)]
    return pl.pallas_call(
        body, name=name, grid=(s // tr,), in_specs=in_specs, out_specs=out_specs, out_shape=out_shape,
        compiler_params=_cparams(("parallel",)),
    )(*args)


def _norm_bwd(x, g, dy, *, name, add=None, out_dtype=F32, then=None):
    s, d = x.shape
    tr = _tile(s, 256)
    row = pl.BlockSpec((tr, d), lambda i: (i, 0))
    gsp = pl.BlockSpec((1, d), lambda i: (0, 0))
    n_in = 3 + (add is not None) + (2 if then is not None else 0)

    def body(*refs):
        ins, outs = refs[:n_in], refs[n_in:]
        x_ref, g_ref, dy_ref = ins[:3]
        dx, gterm = _rms_bwd(x_ref[...], g_ref[...], dy_ref[...].astype(F32))
        if add is not None:
            dx = dx + ins[3][...]
        outs[0][...] = dx.astype(outs[0].dtype)
        terms = [(outs[1], gterm)]
        if then is not None:
            dx2, gterm2 = _rms_bwd(ins[-2][...], ins[-1][...], dx)
            outs[2][...] = dx2.astype(BF16)
            terms.append((outs[3], gterm2))

        @pl.when(pl.program_id(0) == 0)
        def _():
            for dg_ref, _ in terms:
                dg_ref[...] = jnp.zeros_like(dg_ref)

        for dg_ref, term in terms:
            dg_ref[...] += jnp.sum(term, axis=0, keepdims=True)

    args = [x, g.reshape(1, d), dy] + ([] if add is None else [add]) + ([] if then is None else [then[0], then[1].reshape(1, d)])
    in_specs = [row, gsp, row] + ([] if add is None else [row]) + ([] if then is None else [row, gsp])
    out_specs = [row, gsp] + ([] if then is None else [row, gsp])
    out_shape = [jax.ShapeDtypeStruct((s, d), out_dtype), jax.ShapeDtypeStruct((1, d), F32)]
    if then is not None:
        out_shape += [jax.ShapeDtypeStruct((s, d), BF16), jax.ShapeDtypeStruct((1, d), F32)]
    return pl.pallas_call(
        body, name=name, grid=(s // tr,), in_specs=in_specs, out_specs=out_specs, out_shape=out_shape,
        compiler_params=_cparams(("arbitrary",)),
    )(*args)


def _loss_head(f, g, resid, target):
    s, d = f.shape
    tr = _tile(s, 256)
    row = pl.BlockSpec((tr, d), lambda i: (i, 0))
    gsp = pl.BlockSpec((1, d), lambda i: (0, 0))
    lsp = pl.BlockSpec((1, LANES), lambda i: (0, 0))

    def body(f_ref, g_ref, r_ref, t_ref, l_ref, dy_ref, df_ref, dg_ref):
        fv, gv = f_ref[...], g_ref[...]
        e = (r_ref[...] + _rms(fv, gv)) - t_ref[...]
        dy = e * (1.0 / d)
        dy_ref[...] = dy
        df, gterm = _rms_bwd(fv, gv, dy)
        df_ref[...] = df.astype(BF16)

        @pl.when(pl.program_id(0) == 0)
        def _():
            l_ref[...] = jnp.zeros_like(l_ref)
            dg_ref[...] = jnp.zeros_like(dg_ref)

        part = 0.5 * jnp.sum(jnp.mean(e * e, axis=-1, keepdims=True), axis=0, keepdims=True)
        l_ref[...] += jnp.broadcast_to(part, (1, LANES))
        dg_ref[...] += jnp.sum(gterm, axis=0, keepdims=True)

    return pl.pallas_call(
        body, name="loss_head", grid=(s // tr,), in_specs=[row, gsp, row, row], out_specs=[lsp, row, row, gsp],
        out_shape=[jax.ShapeDtypeStruct((1, LANES), F32), jax.ShapeDtypeStruct((s, d), F32), jax.ShapeDtypeStruct((s, d), BF16),
                   jax.ShapeDtypeStruct((1, d), F32)],
        compiler_params=_cparams(("arbitrary",)),
    )(f, g.reshape(1, d), resid, target)


def _rope_tables(pos_col):
    s = pos_col.shape[0]
    tr = _tile(s, 512)
    f_mla = ROPE_BASE ** (-jnp.arange(ROPE_DIM // 2, dtype=F32) / (ROPE_DIM // 2))
    f_ret = ROPE_BASE ** (-jnp.arange(HEAD // 2, dtype=F32) / (HEAD // 2))
    fm = jnp.concatenate([jnp.zeros((64,), F32), f_mla, f_mla, jnp.zeros((32,), F32)]).reshape(1, LANES)
    fr = jnp.tile(jnp.concatenate([f_ret, f_ret]), 2).reshape(1, LANES)

    def body(p_ref, fm_ref, fr_ref, cm_ref, sm_ref, cr_ref, sr_ref):
        p = p_ref[...].astype(F32)
        am = p * fm_ref[...]
        ar = p * fr_ref[...]
        cm_ref[...] = jnp.cos(am)
        sm_ref[...] = jnp.sin(am)
        cr_ref[...] = jnp.tile(jnp.cos(ar), (1, 2))
        sr_ref[...] = jnp.tile(jnp.sin(ar), (1, 2))

    return pl.pallas_call(
        body, name="rope_tables", grid=(s // tr,),
        in_specs=[pl.BlockSpec((tr, 1), lambda i: (i, 0)), pl.BlockSpec((1, LANES), lambda i: (0, 0)),
                  pl.BlockSpec((1, LANES), lambda i: (0, 0))],
        out_specs=[pl.BlockSpec((tr, LANES), lambda i: (i, 0))] * 2 + [pl.BlockSpec((tr, 2 * LANES), lambda i: (i, 0))] * 2,
        out_shape=[jax.ShapeDtypeStruct((s, LANES), F32)] * 2 + [jax.ShapeDtypeStruct((s, 2 * LANES), F32)] * 2,
        compiler_params=_cparams(("parallel",)),
    )(pos_col, fm, fr)


def _lane(shape):
    return lax.broadcasted_iota(jnp.int32, shape, len(shape) - 1)


def _rot_mla(z):
    l = _lane(z.shape) % LANES
    n = z.shape[-1]
    return jnp.where(l < 80, -pltpu.roll(z, n - 16, 1), pltpu.roll(z, 16, 1))


def _rot_mla_t(y):
    l = _lane(y.shape) % LANES
    n = y.shape[-1]
    return jnp.where((l >= 64) & (l < 80), pltpu.roll(y, n - 16, 1),
                     jnp.where((l >= 80) & (l < 96), -pltpu.roll(y, 16, 1), 0.0))


def _rot_ret(z):
    l = _lane(z.shape) % HEAD
    n = z.shape[-1]
    return jnp.where(l < 32, -pltpu.roll(z, n - 32, 1), pltpu.roll(z, 32, 1))


def _rot_ret_t(y):
    l = _lane(y.shape) % HEAD
    n = y.shape[-1]
    return jnp.where(l < 32, pltpu.roll(y, n - 32, 1), -pltpu.roll(y, 32, 1))


def _log_sigmoid(x):
    return jnp.minimum(x, 0.0) - jnp.log1p(jnp.exp(-jnp.abs(x)))


def _fox_cum(proj, bias_row):
    s = proj.shape[0]
    fb = _tile(s, GATE_ROWS)
    nb = s // fb

    def body(x_ref, b_ref, cc_ref, cr_ref, carry_ref):
        @pl.when(pl.program_id(0) == 0)
        def _():
            carry_ref[...] = jnp.zeros_like(carry_ref)

        ls = _log_sigmoid(x_ref[...] + b_ref[...])
        r = lax.broadcasted_iota(jnp.int32, (fb, fb), 0)
        c = lax.broadcasted_iota(jnp.int32, (fb, fb), 1)
        tri = (c <= r).astype(F32)
        cum = _dot_exact(tri, ls) + carry_ref[...]
        carry_ref[...] = cum[fb - 1:fb, :]
        cc_ref[...] = cum
        cr_ref[...] = cum.T[0:8, :]

    return pl.pallas_call(
        body, name="fox_cum", grid=(nb,),
        in_specs=[pl.BlockSpec((fb, LANES), lambda i: (i, OFF_MISC)), pl.BlockSpec((1, LANES), lambda i: (0, 0))],
        out_specs=[pl.BlockSpec((fb, LANES), lambda i: (i, 0)), pl.BlockSpec((8, fb), lambda i: (0, i))],
        out_shape=[jax.ShapeDtypeStruct((s, LANES), F32), jax.ShapeDtypeStruct((8, s), F32)],
        scratch_shapes=[pltpu.VMEM((1, LANES), F32)],
        compiler_params=_cparams(("arbitrary",)),
    )(proj, bias_row)


def _fox_gate_bwd(dck, drs, proj, bias_row, dkr):
    s = proj.shape[0]
    fb = _tile(s, GATE_ROWS)
    nb = s // fb

    def body(d_ref, r_ref, x_ref, b_ref, k_ref, o_ref, db_ref, carry_ref):
        @pl.when(pl.program_id(0) == 0)
        def _():
            carry_ref[...] = jnp.zeros_like(carry_ref)
            db_ref[...] = jnp.zeros_like(db_ref)

        rows = jnp.concatenate([d_ref[0], d_ref[1], jnp.zeros((LANES - 16, fb), F32)], axis=0)
        t = rows.T
        l = _lane((fb, LANES))
        r0, r1 = r_ref[0], r_ref[1]
        rsum = jnp.where(l == 0, r0[:, 0:1], jnp.where(l == 1, r0[:, HEAD:HEAD + 1],
                         jnp.where(l == 2, r1[:, 0:1], jnp.where(l == 3, r1[:, HEAD:HEAD + 1], 0.0))))
        dcum = rsum - jnp.where(l < 2, t, pltpu.roll(t, LANES - 6, 1))
        r = lax.broadcasted_iota(jnp.int32, (fb, fb), 0)
        c = lax.broadcasted_iota(jnp.int32, (fb, fb), 1)
        triu = (c >= r).astype(F32)
        rc = _dot_exact(triu, dcum) + carry_ref[...]
        carry_ref[...] = rc[0:1, :]
        f = x_ref[...] + b_ref[...]
        sig_neg = 1.0 / (1.0 + jnp.exp(f))
        df = jnp.where(l < N_HEADS, rc * sig_neg, 0.0)
        db_ref[...] += jnp.sum(df, axis=0, keepdims=True)
        o_ref[...] = (df + k_ref[...]).astype(o_ref.dtype)

    rev = lambda i: nb - 1 - i
    return pl.pallas_call(
        body, name="fox_gate_bwd", grid=(nb,),
        in_specs=[pl.BlockSpec((2, 8, fb), lambda i: (0, 0, rev(i))), pl.BlockSpec((2, fb, LANES), lambda i: (0, rev(i), 0)),
                  pl.BlockSpec((fb, LANES), lambda i: (rev(i), OFF_MISC)),
                  pl.BlockSpec((1, LANES), lambda i: (0, 0)), pl.BlockSpec((fb, LANES), lambda i: (rev(i), 0))],
        out_specs=[pl.BlockSpec((fb, LANES), lambda i: (rev(i), 0)), pl.BlockSpec((1, LANES), lambda i: (0, 0))],
        out_shape=[jax.ShapeDtypeStruct((s, LANES), BF16), jax.ShapeDtypeStruct((1, LANES), F32)],
        scratch_shapes=[pltpu.VMEM((1, LANES), F32)],
        compiler_params=_cparams(("arbitrary",)),
    )(dck, drs, proj, bias_row, dkr)


def _mla_prep(proj, cos_m, sin_m, g_q, g_kv, wq, wk, wv):
    s = proj.shape[0]
    tr = _tile(s, 512)

    def body(cq_ref, ckv_ref, misc_ref, cos_ref, sin_ref, gq_ref, gkv_ref, wq_ref, wk_ref, wv_ref,
             q_ref, k_ref, v_ref, cqn_ref, ckvn_ref):
        cos4 = jnp.tile(cos_ref[...], (1, 4))
        sin4 = jnp.tile(sin_ref[...], (1, 4))
        cqn = _rms(cq_ref[...], gq_ref[...]).astype(BF16)
        ckvn = _rms(ckv_ref[...], gkv_ref[...]).astype(BF16)
        cqn_ref[...] = cqn
        ckvn_ref[...] = ckvn
        zq = _dot(cqn, wq_ref[...])
        q_ref[...] = (zq * cos4 + _rot_mla(zq) * sin4).astype(BF16)
        l = _lane((tr, LANES))
        kr = jnp.where((l >= KR_LANE) & (l < KR_LANE + ROPE_DIM), misc_ref[...], 0.0)
        zk = _dot(ckvn, wk_ref[...]) + jnp.tile(kr, (1, 4))
        k_ref[...] = (zk * cos4 + _rot_mla(zk) * sin4).astype(BF16)
        v_ref[...] = _dot(ckvn, wv_ref[...]).astype(BF16)

    full = lambda a: pl.BlockSpec(a.shape, lambda i: (0, 0))
    rowb = lambda w: pl.BlockSpec((tr, w), lambda i: (i, 0))
    gq2, gkv2 = g_q.reshape(1, Q_RANK), g_kv.reshape(1, KV_RANK)
    return pl.pallas_call(
        body, name="mla_prep", grid=(s // tr,),
        in_specs=[pl.BlockSpec((tr, 256), lambda i: (i, OFF_CQ // 2)), pl.BlockSpec((tr, LANES), lambda i: (i, OFF_CKV)),
                  pl.BlockSpec((tr, LANES), lambda i: (i, OFF_MISC)), rowb(LANES), rowb(LANES),
                  full(gq2), full(gkv2), full(wq), full(wk), full(wv)],
        out_specs=[rowb(512), rowb(512), rowb(512), rowb(256), rowb(128)],
        out_shape=[jax.ShapeDtypeStruct((s, 512), BF16), jax.ShapeDtypeStruct((s, 512), BF16), jax.ShapeDtypeStruct((s, 512), BF16),
                   jax.ShapeDtypeStruct((s, 256), BF16), jax.ShapeDtypeStruct((s, 128), BF16)],
        compiler_params=_cparams(("parallel",)),
    )(proj, proj, proj, cos_m, sin_m, gq2, gkv2, wq, wk, wv)


def _mla_prep_bwd(dq, dk, dv, proj, cqn, ckvn, cos_m, sin_m, g_q, g_kv, wq, wk, wv):
    s = proj.shape[0]
    tr = _tile(s, 512)

    def body(dq_ref, dk_ref, dv_ref, cq_ref, ckv_ref, cqn_ref, ckvn_ref, cos_ref, sin_ref, gq_ref, gkv_ref,
             wq_ref, wk_ref, wv_ref, dcq_ref, dckv_ref, dkr_ref, dwq_ref, dwk_ref, dwv_ref, dgq_ref, dgkv_ref):
        @pl.when(pl.program_id(0) == 0)
        def _():
            for r in (dwq_ref, dwk_ref, dwv_ref, dgq_ref, dgkv_ref):
                r[...] = jnp.zeros_like(r)

        cos4 = jnp.tile(cos_ref[...], (1, 4))
        sin4 = jnp.tile(sin_ref[...], (1, 4))
        dqv = dq_ref[...]
        dzq = dqv * cos4 + _rot_mla_t(dqv * sin4)
        dkv_ = dk_ref[...]
        dzk = dkv_ * cos4 + _rot_mla_t(dkv_ * sin4)
        l = _lane((tr, LANES))
        in_rope = (l >= KR_LANE) & (l < KR_LANE + ROPE_DIM)
        dkr = dzk[:, 0:128] + dzk[:, 128:256] + dzk[:, 256:384] + dzk[:, 384:512]
        dkr_ref[...] = jnp.where(in_rope, dkr, 0.0)
        dzq_b = dzq.astype(BF16)
        dzk_b = dzk.astype(BF16)
        dv_b = dv_ref[...].astype(BF16)
        dcqn = _dot_nt(dzq_b, wq_ref[...])
        dckvn = _dot_nt(dzk_b, wk_ref[...]) + _dot_nt(dv_b, wv_ref[...])
        dwq_ref[...] += _dot_tn(cqn_ref[...], dzq_b)
        dwk_ref[...] += _dot_tn(ckvn_ref[...], dzk_b)
        dwv_ref[...] += _dot_tn(ckvn_ref[...], dv_b)
        dcq, gq_term = _rms_bwd(cq_ref[...], gq_ref[...], dcqn)
        dckv, gkv_term = _rms_bwd(ckv_ref[...], gkv_ref[...], dckvn)
        dcq_ref[...] = dcq.astype(BF16)
        dckv_ref[...] = dckv.astype(BF16)
        dgq_ref[...] += jnp.sum(gq_term, axis=0, keepdims=True)
        dgkv_ref[...] += jnp.sum(gkv_term, axis=0, keepdims=True)

    full = lambda shp: pl.BlockSpec(shp, lambda i: (0, 0))
    rowb = lambda w: pl.BlockSpec((tr, w), lambda i: (i, 0))
    gq2, gkv2 = g_q.reshape(1, Q_RANK), g_kv.reshape(1, KV_RANK)
    return pl.pallas_call(
        body, name="mla_prep_bwd", grid=(s // tr,),
        in_specs=[rowb(512), rowb(512), rowb(512),
                  pl.BlockSpec((tr, 256), lambda i: (i, OFF_CQ // 2)), pl.BlockSpec((tr, LANES), lambda i: (i, OFF_CKV)),
                  rowb(256), rowb(128), rowb(LANES), rowb(LANES), full((1, Q_RANK)), full((1, KV_RANK)),
                  full(wq.shape), full(wk.shape), full(wv.shape)],
        out_specs=[rowb(256), rowb(128), rowb(128), full(wq.shape), full(wk.shape), full(wv.shape),
                   full((1, Q_RANK)), full((1, KV_RANK))],
        out_shape=[jax.ShapeDtypeStruct((s, 256), BF16), jax.ShapeDtypeStruct((s, 128), BF16), jax.ShapeDtypeStruct((s, 128), F32),
                   jax.ShapeDtypeStruct(wq.shape, F32), jax.ShapeDtypeStruct(wk.shape, F32), jax.ShapeDtypeStruct(wv.shape, F32),
                   jax.ShapeDtypeStruct((1, Q_RANK), F32), jax.ShapeDtypeStruct((1, KV_RANK), F32)],
        compiler_params=_cparams(("arbitrary",)),
    )(dq, dk, dv, proj, proj, cqn, ckvn, cos_m, sin_m, gq2, gkv2, wq, wk, wv)


def _ret_prep(proj, cos_r, sin_r):
    s = proj.shape[0]
    tr = _tile(s, 512)

    def body(q_ref, k_ref, cos_ref, sin_ref, qo_ref, ko_ref):
        cos, sin = cos_ref[...], sin_ref[...]
        q, k = q_ref[...], k_ref[...]
        qo_ref[...] = (q * cos + _rot_ret(q) * sin).astype(BF16)
        ko_ref[...] = ((k * cos + _rot_ret(k) * sin) * (HEAD ** -0.5)).astype(BF16)

    rowb = pl.BlockSpec((tr, 256), lambda i: (i, 0))
    return pl.pallas_call(
        body, name="ret_prep", grid=(s // tr,),
        in_specs=[pl.BlockSpec((tr, 256), lambda i: (i, OFF_RQ // 2)), pl.BlockSpec((tr, 256), lambda i: (i, OFF_RK // 2)), rowb, rowb],
        out_specs=[rowb, rowb], out_shape=[jax.ShapeDtypeStruct((s, 256), BF16)] * 2,
        compiler_params=_cparams(("parallel",)),
    )(proj, proj, cos_r, sin_r)


def _ret_prep_bwd(dq, dk, cos_r, sin_r):
    s = dq.shape[0]
    tr = _tile(s, 512)

    def body(dq_ref, dk_ref, cos_ref, sin_ref, qo_ref, ko_ref):
        cos, sin = cos_ref[...], sin_ref[...]
        q, k = dq_ref[...], dk_ref[...] * (HEAD ** -0.5)
        qo_ref[...] = (q * cos + _rot_ret_t(q * sin)).astype(BF16)
        ko_ref[...] = (k * cos + _rot_ret_t(k * sin)).astype(BF16)

    rowb = pl.BlockSpec((tr, 256), lambda i: (i, 0))
    return pl.pallas_call(
        body, name="ret_prep_bwd", grid=(s // tr,), in_specs=[rowb] * 4, out_specs=[rowb, rowb],
        out_shape=[jax.ShapeDtypeStruct((s, 256), BF16)] * 2, compiler_params=_cparams(("parallel",)),
    )(dq, dk, cos_r, sin_r)


_LOG_GAMMA = [float(np.log1p(-np.float32(2.0) ** np.float32(-5.0 - h))) for h in range(N_HEADS)]
_MLA_SCALE = float((HEAD + ROPE_DIM) ** -0.5)
_QK_SCALE = float(HEAD ** -0.5)
KEY_BLOCKS = 4
QB = 512


def _split2(x):
    h = x.astype(BF16)
    return h, (x - h.astype(F32)).astype(BF16)


def _dot2(x, u):
    h, lo = _split2(x)
    return _dot(h, u) + _dot(lo, u)


def _head_pick(block, head, axis):
    idx = lax.broadcasted_iota(jnp.int32, block.shape, axis)
    return jnp.sum(jnp.where(idx == head, block, 0.0), axis=axis, keepdims=True)


def _log_gamma_of(head):
    lg = jnp.float32(_LOG_GAMMA[3])
    for h in (2, 1, 0):
        lg = jnp.where(head == h, jnp.float32(_LOG_GAMMA[h]), lg)
    return lg


def _mixer_specs(mode, s, q_off, k_off, v_off):
    nhb = 2
    bw = 2 * LANES if mode == "mla" else LANES
    nsub = KEY_BLOCKS if (s // TQ) % KEY_BLOCKS == 0 else 1
    q_spec = pl.BlockSpec((QB, bw), lambda p, i: (i, q_off + p))
    k_spec = pl.BlockSpec((s, bw), lambda p, i: (0, k_off + p))
    v_spec = pl.BlockSpec((s, bw), lambda p, i: (0, v_off + p))
    return nhb, N_HEADS // nhb, nsub, q_spec, k_spec, v_spec


def _mixer_geometry(mode, i, nsub):
    w = TQ * nsub
    row = lax.broadcasted_iota(jnp.int32, (QB, w), 0)
    col = lax.broadcasted_iota(jnp.int32, (QB, w), 1)
    nfull = (i * QB) // w
    dist = col - row
    if mode in ("fox", "sb"):
        rel = dist
    else:
        rel = col - (row | (CHUNK - 1))

    def visible(c):
        off = c * w - i * QB
        return (rel + off) < 0 if mode == "sb" else (rel + off) <= 0

    return nfull, dist, visible


class _SideJob:
    def __init__(self, inputs, out_shape, n_sems, sends, recvs):
        self.inputs, self.out_shape, self.n_sems, self.sends, self.recvs = list(inputs), list(out_shape), n_sems, sends, recvs


def _carry_side_job(body, n_in, n_out, side, n_steps):
    if side is None:
        return body
    si, so = len(side.inputs), len(side.out_shape)

    def at(corner):
        ok = pl.program_id(0) == corner[0]
        for d in range(1, len(n_steps)):
            ok = ok & (pl.program_id(d) == corner[d])
        return ok

    def wrapped(*refs):
        ins, s_ins = refs[:n_in], refs[n_in:n_in + si]
        outs, s_outs = refs[n_in + si:n_in + si + n_out], refs[n_in + si + n_out:n_in + si + n_out + so]
        scratch, send, recv = refs[n_in + si + n_out + so:-2], refs[-2], refs[-1]

        @pl.when(at([0] * len(n_steps)))
        def _():
            for cp in side.sends(s_ins, s_outs, send, recv):
                cp.start()

        body(*ins, *outs, *scratch)

        @pl.when(at([n - 1 for n in n_steps]))
        def _():
            for cp in side.recvs(s_ins, s_outs, send, recv):
                cp.wait_recv()
            for cp in side.sends(s_ins, s_outs, send, recv):
                cp.wait_send()

    return wrapped


def _side_specs(side):
    if side is None:
        return [], [], []
    hbm = pl.BlockSpec(memory_space=pl.ANY)
    return ([hbm] * len(side.inputs), [hbm] * len(side.out_shape),
            [pltpu.SemaphoreType.DMA((side.n_sems,)), pltpu.SemaphoreType.DMA((side.n_sems,))])


def _mixer_fwd(mode, qa, q_off, ka, k_off, va, v_off, *, cum_col=None, cum_row=None, side=None):
    s = qa.shape[0]
    nq = s // QB
    nhb, nblk, nsub, q_spec, k_spec, v_spec = _mixer_specs(mode, s, q_off, k_off, v_off)
    w = TQ * nsub
    softmax = mode in ("fox", "mla")

    def body(*refs):
        refs = list(refs)
        q_ref, k_ref, v_ref = refs[:3]
        refs = refs[3:]
        if mode == "fox":
            cc_ref, cr_ref = refs[:2]
            refs = refs[2:]
        o_ref = refs[0]
        st_ref = refs[1]
        p = pl.program_id(0)
        i = pl.program_id(1)
        nfull, dist, visible = _mixer_geometry(mode, i, nsub)
        lane = _lane((1, LANES))
        heads = [nhb * p + hh for hh in range(nhb)]
        wide = mode == "mla"
        q_scale = _QK_SCALE if mode in ("fox", "sb") else 1.0
        cols = [slice(hh * LANES, (hh + 1) * LANES) if wide else slice(None) for hh in range(nhb)]
        if wide:
            qs = [q_ref[:, cols[hh]] for hh in range(nhb)]
        else:
            qf = q_ref[...].astype(F32) * q_scale
            qs = [jnp.where((lane // HEAD) == hh, qf, 0.0).astype(BF16) for hh in range(nhb)]
        if mode == "fox":
            cqs = [_head_pick(cc_ref[...], h, 1) for h in heads]
        if mode == "sb":
            r1 = lax.broadcasted_iota(jnp.int32, (TQ, TQ), 0)
            c1 = lax.broadcasted_iota(jnp.int32, (TQ, TQ), 1)
            u_after = (r1 > c1).astype(BF16)

        def chunk(c):
            return pl.ds(pl.multiple_of(c * w, w), w)

        def scores(c):
            js = chunk(c)
            return tuple(_dot_nt(qs[hh], k_ref[js, cols[hh]]) for hh in range(nhb))

        def head_step(hh, c, js, sc, vj, carry, last):
            if softmax:
                m, l, acc = carry
                if mode == "fox":
                    ck = _head_pick(cr_ref[:, js], heads[hh], 0)
                    sc = sc + (cqs[hh] - ck)
                else:
                    sc = sc * _MLA_SCALE
                if last:
                    sc = jnp.where(visible(c), sc, NEG)
                m_new = jnp.maximum(m, jnp.max(sc, axis=-1, keepdims=True))
                alpha = jnp.exp(m - m_new)
                pr = jnp.exp(sc - m_new)
                l = alpha * l + jnp.sum(pr, axis=-1, keepdims=True)
                acc = alpha * acc + _dot(pr.astype(BF16), vj)
                return m_new, l, acc
            run, acc = carry
            z = sc
            log_beta = jnp.minimum(z, 0.0) - jnp.log(1.0 + jnp.exp(-jnp.abs(z)))
            log_stay = log_beta - z
            if last:
                vis = visible(c)
                log_stay = jnp.where(vis, log_stay, 0.0)
            parts = [None] * nsub
            for b in reversed(range(nsub)):
                ls_b = log_stay[:, b * TQ:(b + 1) * TQ]
                parts[b] = _dot2(ls_b, u_after) + run
                run = run + jnp.sum(ls_b, axis=-1, keepdims=True)
            later = parts[0] if nsub == 1 else jnp.concatenate(parts, axis=1)
            wgt = jnp.exp(log_beta + later)
            if last:
                wgt = jnp.where(vis, wgt, 0.0)
            return run, acc + _dot(wgt.astype(BF16), vj)

        def step(c, c_next, state, last):
            scs, carries = state
            nxt = scores(c_next) if c_next is not None else None
            js = chunk(c)
            return nxt, tuple(head_step(hh, c, js, scs[hh], v_ref[js, cols[hh]], carries[hh], last) for hh in range(nhb))

        zero_acc = jnp.zeros((QB, LANES), F32)
        zero1 = jnp.zeros((QB, 1), F32)
        if softmax:
            init = tuple((jnp.full((QB, 1), NEG, F32), zero1, zero_acc) for _ in range(nhb))
        else:
            init = tuple((zero1, zero_acc) for _ in range(nhb))
        if mode == "sb":
            state = step(nfull, jnp.maximum(nfull - 1, 0), (scores(nfull), init), True)
            _, carries = lax.fori_loop(0, nfull, lambda t, st: step(nfull - 1 - t, jnp.maximum(nfull - 2 - t, 0), st, False), state)
        else:
            state = lax.fori_loop(0, nfull, lambda c, st: step(c, c + 1, st, False), (scores(0), init))
            _, carries = step(nfull, None, state, True)
        if softmax:
            outs = [acc / l for (m, l, acc) in carries]
            stats = [m + jnp.log(l) for (m, l, acc) in carries]
        else:
            outs, stats = [acc for (run, acc) in carries], [run for (run, acc) in carries]
        hm0 = (lane // HEAD) == 0
        pick = lambda a: jnp.where(hm0, a[0], a[1])
        if wide:
            for hh in range(nhb):
                o_ref[:, cols[hh]] = outs[hh]
        else:
            o_ref[...] = pick(outs)
        st_ref[0] = pick(stats)

    in_specs = [q_spec, k_spec, v_spec]
    args = [qa, ka, va]
    if mode == "fox":
        in_specs += [pl.BlockSpec((QB, LANES), lambda p, i: (i, 0)), pl.BlockSpec((8, s), lambda p, i: (0, 0))]
        args += [cum_col, cum_row]
    bw = 2 * LANES if mode == "mla" else LANES
    out_specs = [pl.BlockSpec((QB, bw), lambda p, i: (i, p))]
    out_shape = [jax.ShapeDtypeStruct((s, nblk * bw), F32)]
    out_specs.append(pl.BlockSpec((1, QB, LANES), lambda p, i: (p, i, 0)))
    out_shape.append(jax.ShapeDtypeStruct((nblk, s, LANES), F32))
    side_in, side_out, side_scratch = _side_specs(side)
    res = pl.pallas_call(
        _carry_side_job(body, len(args), len(out_shape), side, (nblk, nq)), name=mode + "_fwd", grid=(nblk, nq),
        in_specs=in_specs + side_in, out_specs=out_specs + side_out,
        out_shape=out_shape + ([] if side is None else side.out_shape), scratch_shapes=side_scratch,
        compiler_params=_cparams(("parallel", "parallel") if side is None else ("arbitrary", "arbitrary")),
    )(*args, *([] if side is None else side.inputs))
    return (res[0], res[1]) if side is None else (res[0], res[1], res[2:])


def _mixer_bwd(mode, qa, q_off, ka, k_off, va, v_off, o, do, *, stat=None, cum_col=None, cum_row=None, side=None):
    s = qa.shape[0]
    nq = s // QB
    nhb, nblk, nsub, q_spec, k_spec, v_spec = _mixer_specs(mode, s, q_off, k_off, v_off)
    w = TQ * nsub
    softmax = mode in ("fox", "mla")

    def body(*refs):
        refs = list(refs)
        q_ref, k_ref, v_ref, o_ref, do_ref = refs[:5]
        refs = refs[5:]
        st_ref = refs[0]
        refs = refs[1:]
        if mode == "fox":
            cc_ref, cr_ref = refs[:2]
            refs = refs[2:]
        dq_ref, dk_ref, dv_ref = refs[:3]
        dck_ref, drs_ref = refs[3:5] if mode == "fox" else (None, None)
        p = pl.program_id(0)
        i = pl.program_id(1)

        @pl.when(i == 0)
        def _():
            dk_ref[...] = jnp.zeros_like(dk_ref)
            dv_ref[...] = jnp.zeros_like(dv_ref)
            if mode == "fox":
                dck_ref[...] = jnp.zeros_like(dck_ref)

        nfull, dist, visible = _mixer_geometry(mode, i, nsub)
        lane = _lane((1, LANES))
        heads = [nhb * p + hh for hh in range(nhb)]
        dov = do_ref[...]
        wide = mode == "mla"
        q_scale = _QK_SCALE if mode in ("fox", "sb") else 1.0
        cols = [slice(hh * LANES, (hh + 1) * LANES) if wide else slice(None) for hh in range(nhb)]
        if wide:
            prod = dov * o_ref[...]
            qs = [q_ref[:, cols[hh]] for hh in range(nhb)]
            dos = [dov[:, cols[hh]].astype(BF16) for hh in range(nhb)]
            deltas = [jnp.sum(prod[:, cols[hh]], axis=-1, keepdims=True) for hh in range(nhb)]
        else:
            qf = q_ref[...].astype(F32) * q_scale
            prod = dov * o_ref[...]
            hms = [(lane // HEAD) == hh for hh in range(nhb)]
            qs = [jnp.where(hm, qf, 0.0).astype(BF16) for hm in hms]
            dos = [jnp.where(hm, dov, 0.0).astype(BF16) for hm in hms]
            deltas = [jnp.sum(jnp.where(hm, prod, 0.0), axis=-1, keepdims=True) for hm in hms]
        st = st_ref[0]
        stats = [st[:, hh * HEAD:hh * HEAD + 1] for hh in range(nhb)]
        if mode == "fox":
            cqs = [_head_pick(cc_ref[...], h, 1) for h in heads]
        if mode == "sb":
            r1 = lax.broadcasted_iota(jnp.int32, (TQ, TQ), 0)
            c1 = lax.broadcasted_iota(jnp.int32, (TQ, TQ), 1)
            u_upto = (r1 <= c1).astype(BF16)
            u_before = (r1 < c1).astype(BF16)

        def chunk(c):
            return pl.ds(pl.multiple_of(c * w, w), w)

        def scores(c):
            js = chunk(c)
            if mode == "sb":
                return tuple((_dot_nt(qs[hh], k_ref[js, cols[hh]]), None) for hh in range(nhb))
            return tuple((_dot_nt(qs[hh], k_ref[js, cols[hh]]), _dot_nt(dos[hh], v_ref[js, cols[hh]])) for hh in range(nhb))

        def emit(hh, js, ds_b, pr_b, dq):
            dk_ref[js, cols[hh]] += _dot_tn(ds_b, qs[hh])
            dv_ref[js, cols[hh]] += _dot_tn(pr_b, dos[hh])
            return dq + _dot(ds_b, k_ref[js, cols[hh]])

        def head_step(hh, c, js, sc_dp, carry, last):
            sc, dp = sc_dp
            if dp is None:
                dp = _dot_nt(dos[hh], v_ref[js, cols[hh]])
            if softmax:
                dq, rsum = carry
                if mode == "fox":
                    ck = _head_pick(cr_ref[:, js], heads[hh], 0)
                    sc = sc + (cqs[hh] - ck)
                else:
                    sc = sc * _MLA_SCALE
                if last:
                    sc = jnp.where(visible(c), sc, NEG)
                pr = jnp.exp(sc - stats[hh])
                ds = pr * (dp - deltas[hh])
                if mode == "fox":
                    dck_ref[0, hh:hh + 1, js] += jnp.sum(ds, axis=0, keepdims=True)
                    rsum = rsum + jnp.sum(ds, axis=-1, keepdims=True)
                if mode == "mla":
                    ds = ds * _MLA_SCALE
                return emit(hh, js, ds.astype(BF16), pr.astype(BF16), dq), rsum
            seen, gsum, dq = carry
            z = sc
            log_beta = jnp.minimum(z, 0.0) - jnp.log(1.0 + jnp.exp(-jnp.abs(z)))
            log_stay = log_beta - z
            if last:
                vis = visible(c)
                log_stay = jnp.where(vis, log_stay, 0.0)
            parts = []
            for b in range(nsub):
                ls_b = log_stay[:, b * TQ:(b + 1) * TQ]
                parts.append((stats[hh] - seen) - _dot2(ls_b, u_upto))
                seen = seen + jnp.sum(ls_b, axis=-1, keepdims=True)
            later = parts[0] if nsub == 1 else jnp.concatenate(parts, axis=1)
            wgt = jnp.exp(log_beta + later)
            if last:
                wgt = jnp.where(vis, wgt, 0.0)
            g = dp * wgt
            parts = []
            for b in range(nsub):
                g_b = g[:, b * TQ:(b + 1) * TQ]
                parts.append(gsum + _dot2(g_b, u_before))
                gsum = gsum + jnp.sum(g_b, axis=-1, keepdims=True)
            before = parts[0] if nsub == 1 else jnp.concatenate(parts, axis=1)
            beta = jnp.exp(log_beta)
            dz = g * (1.0 - beta) - beta * before
            if last:
                dz = jnp.where(vis, dz, 0.0)
            return seen, gsum, emit(hh, js, dz.astype(BF16), wgt.astype(BF16), dq)

        def step(c, c_next, state, last):
            scs, carries = state
            nxt = scores(c_next) if c_next is not None else None
            js = chunk(c)
            return nxt, tuple(head_step(hh, c, js, scs[hh], carries[hh], last) for hh in range(nhb))

        zero_acc = jnp.zeros((QB, LANES), F32)
        zero1 = jnp.zeros((QB, 1), F32)
        if softmax:
            init = tuple((zero_acc, zero1) for _ in range(nhb))
        else:
            init = tuple((zero1, zero1, zero_acc) for _ in range(nhb))
        state = lax.fori_loop(0, nfull, lambda c, st: step(c, c + 1, st, False), (scores(0), init))
        _, carries = step(nfull, None, state, True)
        if softmax:
            dqs = [dq for (dq, rsum) in carries]
        else:
            dqs = [dq for (seen, gsum, dq) in carries]
        hm0 = (lane // HEAD) == 0
        if wide:
            for hh in range(nhb):
                dq_ref[:, cols[hh]] = dqs[hh]
        else:
            dq_ref[...] = jnp.where(hm0, dqs[0], dqs[1]) * q_scale
        if mode == "fox":
            drs_ref[0] = jnp.where(hm0, carries[0][1], carries[1][1])

    bw = 2 * LANES if mode == "mla" else LANES
    pair_blk = pl.BlockSpec((QB, bw), lambda p, i: (i, p))
    full_blk = pl.BlockSpec((s, bw), lambda p, i: (0, p))
    stat_blk = pl.BlockSpec((1, QB, LANES), lambda p, i: (p, i, 0))
    in_specs = [q_spec, k_spec, v_spec, pair_blk, pair_blk]
    args = [qa, ka, va, o, do]
    in_specs.append(stat_blk)
    args.append(stat)
    if mode == "fox":
        in_specs += [pl.BlockSpec((QB, LANES), lambda p, i: (i, 0)), pl.BlockSpec((8, s), lambda p, i: (0, 0))]
        args += [cum_col, cum_row]
    out_specs = [pair_blk, full_blk, full_blk]
    out_shape = [jax.ShapeDtypeStruct((s, nblk * bw), F32)] * 3
    if mode == "fox":
        out_specs += [pl.BlockSpec((1, 8, s), lambda p, i: (p, 0, 0)), stat_blk]
        out_shape += [jax.ShapeDtypeStruct((2, 8, s), F32), jax.ShapeDtypeStruct((2, s, LANES), F32)]
    side_in, side_out, side_scratch = _side_specs(side)
    res = pl.pallas_call(
        _carry_side_job(body, len(args), len(out_shape), side, (nblk, nq)), name=mode + "_bwd", grid=(nblk, nq),
        in_specs=in_specs + side_in, out_specs=out_specs + side_out,
        out_shape=out_shape + ([] if side is None else side.out_shape), scratch_shapes=side_scratch,
        compiler_params=_cparams(("parallel", "arbitrary") if side is None else ("arbitrary", "arbitrary")),
    )(*args, *([] if side is None else side.inputs))
    return res if side is None else (*res[:len(out_shape)], res[len(out_shape):])


def _ret_geometry(p):
    lane = _lane((1, LANES))
    lg_lane = jnp.where(lane < HEAD, _log_gamma_of(2 * p), _log_gamma_of(2 * p + 1))
    a = lax.broadcasted_iota(jnp.int32, (TQ, 1), 0).astype(F32)
    row = lax.broadcasted_iota(jnp.int32, (TQ, TQ), 0)
    col = lax.broadcasted_iota(jnp.int32, (TQ, TQ), 1)
    same_chunk_or_earlier = (col // CHUNK) <= (row // CHUNK)
    gap = jnp.abs(row - col).astype(F32)
    decays = [jnp.where(same_chunk_or_earlier, jnp.exp(_log_gamma_of(2 * p + hh) * gap), 0.0) for hh in range(2)]
    r = lax.broadcasted_iota(jnp.int32, (LANES, LANES), 0)
    c = lax.broadcasted_iota(jnp.int32, (LANES, LANES), 1)
    own_head = (r // HEAD) == (c // HEAD)
    return lane, lg_lane, a, decays, own_head


def _ret_fwd(qa, ka, va, v_off):
    s = qa.shape[0]
    nq = s // TQ

    def body(q_ref, k_ref, v_ref, o_ref, st_ref, state):
        p = pl.program_id(0)

        @pl.when(pl.program_id(1) == 0)
        def _():
            state[...] = jnp.zeros_like(state)

        lane, lg_lane, a, decays, own_head = _ret_geometry(p)
        q = q_ref[...].astype(F32)
        k = k_ref[...]
        v = v_ref[...]
        s_in = state[...]
        st_ref[0, 0] = s_in
        out = _dot((q * jnp.exp(lg_lane * (a + 1.0))).astype(BF16), s_in.astype(BF16))
        for hh in range(2):
            hm = (lane // HEAD) == hh
            qh = jnp.where(hm, q, 0.0).astype(BF16)
            inner = _dot((_dot_nt(qh, k) * decays[hh]).astype(BF16), v)
            out = out + jnp.where(hm, inner, 0.0)
        o_ref[...] = out
        k_tail = (k.astype(F32) * jnp.exp(lg_lane * (TQ - 1.0 - a))).astype(BF16)
        state[...] = jnp.exp(lg_lane * float(TQ)) * s_in + jnp.where(own_head, _dot_tn(k_tail, v), 0.0)

    blk = lambda off: pl.BlockSpec((TQ, LANES), lambda p, i: (i, off + p))
    return pl.pallas_call(
        body, name="ret_fwd", grid=(2, nq), in_specs=[blk(0), blk(0), blk(v_off)],
        out_specs=[blk(0), pl.BlockSpec((1, 1, LANES, LANES), lambda p, i: (p, i, 0, 0))],
        out_shape=[jax.ShapeDtypeStruct((s, 2 * LANES), F32), jax.ShapeDtypeStruct((2, nq, LANES, LANES), F32)],
        scratch_shapes=[pltpu.VMEM((LANES, LANES), F32)],
        compiler_params=_cparams(("parallel", "arbitrary")),
    )(qa, ka, va)


def _ret_bwd(qa, ka, va, v_off, states, do):
    s = qa.shape[0]
    nq = s // TQ

    def body(q_ref, k_ref, v_ref, st_ref, do_ref, dq_ref, dk_ref, dv_ref, dstate):
        p = pl.program_id(0)

        @pl.when(pl.program_id(1) == 0)
        def _():
            dstate[...] = jnp.zeros_like(dstate)

        lane, lg_lane, a, decays, own_head = _ret_geometry(p)
        q = q_ref[...].astype(F32)
        k = k_ref[...]
        kf = k.astype(F32)
        v = v_ref[...]
        dov = do_ref[...]
        s_in = st_ref[0, 0].astype(BF16)
        ds_next = dstate[...]
        ds_b = ds_next.astype(BF16)
        head_decay = jnp.exp(lg_lane * (a + 1.0))
        tail_decay = jnp.exp(lg_lane * (TQ - 1.0 - a))
        k_tail = (kf * tail_decay).astype(BF16)
        dq = _dot_nt(dov.astype(BF16), s_in) * head_decay
        dk = _dot_nt(v, ds_b) * tail_decay
        dv = _dot(k_tail, ds_b)
        for hh in range(2):
            hm = (lane // HEAD) == hh
            qh = jnp.where(hm, q, 0.0).astype(BF16)
            doh = jnp.where(hm, dov, 0.0).astype(BF16)
            att = (_dot_nt(qh, k) * decays[hh]).astype(BF16)
            datt = (_dot_nt(doh, v) * decays[hh]).astype(BF16)
            dv = dv + _dot_tn(att, doh)
            dk = dk + _dot_tn(datt, qh)
            dq = dq + jnp.where(hm, _dot(datt, k), 0.0)
        dq_ref[...] = dq
        dk_ref[...] = dk
        dv_ref[...] = dv
        q_head = (q * head_decay).astype(BF16)
        dstate[...] = jnp.exp(lg_lane * float(TQ)) * ds_next + jnp.where(own_head, _dot_tn(q_head, dov.astype(BF16)), 0.0)

    blk = lambda off: pl.BlockSpec((TQ, LANES), lambda p, i: (nq - 1 - i, off + p))
    return pl.pallas_call(
        body, name="ret_bwd", grid=(2, nq),
        in_specs=[blk(0), blk(0), blk(v_off), pl.BlockSpec((1, 1, LANES, LANES), lambda p, i: (p, nq - 1 - i, 0, 0)), blk(0)],
        out_specs=[blk(0)] * 3, out_shape=[jax.ShapeDtypeStruct((s, 2 * LANES), F32)] * 3,
        scratch_shapes=[pltpu.VMEM((LANES, LANES), F32)],
        compiler_params=_cparams(("parallel", "arbitrary")),
    )(qa, ka, va, states, do)


def _seg_mean_matrix():
    r = lax.broadcasted_iota(jnp.int32, (GROUP, GROUP), 0)
    c = lax.broadcasted_iota(jnp.int32, (GROUP, GROUP), 1)
    return jnp.where((r // HEAD) == (c // HEAD), 1.0 / HEAD, 0.0).astype(BF16)


def _seg_mean(x, seg):
    h = x.astype(BF16)
    r = x - h.astype(F32)
    m = r.astype(BF16)
    lo = (r - m.astype(F32)).astype(BF16)
    return _dot(h, seg) + _dot(m, seg) + _dot(lo, seg)


def _sigmoid(x):
    return 1.0 / (1.0 + jnp.exp(-x))


def _mix_post(oa, ob, oc, od, proj, g):
    s = oa.shape[0]
    tr = _tile(s, 256)

    def body(a_ref, b_ref, c_ref, d_ref, rg_ref, g_ref, o_ref):
        gv = g_ref[...]
        o_ref[:, 0:GROUP] = _rms(a_ref[...], gv[:, 0:GROUP]).astype(BF16)
        o_ref[:, GROUP:2 * GROUP] = _rms(b_ref[...], gv[:, GROUP:2 * GROUP]).astype(BF16)
        seg = _seg_mean_matrix()
        c = c_ref[...]
        cen = c - _seg_mean(c, seg)
        n = cen * lax.rsqrt(_seg_mean(cen * cen, seg) + EPS)
        rg = rg_ref[...]
        o_ref[:, 2 * GROUP:3 * GROUP] = (n * gv[:, 2 * GROUP:3 * GROUP] * (rg * _sigmoid(rg))).astype(BF16)
        o_ref[:, 3 * GROUP:] = _rms(d_ref[...], gv[:, 3 * GROUP:]).astype(BF16)

    blk = pl.BlockSpec((tr, GROUP), lambda i: (i, 0))
    return pl.pallas_call(
        body, name="mix_post", grid=(s // tr,),
        in_specs=[blk] * 4 + [pl.BlockSpec((tr, GROUP), lambda i: (i, OFF_RG // 2)), pl.BlockSpec((1, D_MODEL), lambda i: (0, 0))],
        out_specs=pl.BlockSpec((tr, D_MODEL), lambda i: (i, 0)), out_shape=jax.ShapeDtypeStruct((s, D_MODEL), BF16),
        compiler_params=_cparams(("parallel",)),
    )(oa, ob, oc, od, proj, g.reshape(1, D_MODEL))


def _mix_post_bwd(dmixed, oa, ob, oc, od, proj, g):
    s = oa.shape[0]
    tr = _tile(s, 256)

    def body(dm_ref, a_ref, b_ref, c_ref, d_ref, rg_ref, g_ref, da_ref, db_ref, dc_ref, dd_ref, drg_ref, dg_ref):
        @pl.when(pl.program_id(0) == 0)
        def _():
            dg_ref[...] = jnp.zeros_like(dg_ref)

        gv = g_ref[...]
        dm = dm_ref[...]
        for k, (x_ref, dx_ref) in enumerate(((a_ref, da_ref), (b_ref, db_ref), (None, None), (d_ref, dd_ref))):
            if x_ref is None:
                continue
            cols = slice(k * GROUP, (k + 1) * GROUP)
            dx, gterm = _rms_bwd(x_ref[...], gv[:, cols], dm[:, cols])
            dx_ref[...] = dx
            dg_ref[:, cols] += jnp.sum(gterm, axis=0, keepdims=True)
        cols = slice(2 * GROUP, 3 * GROUP)
        seg = _seg_mean_matrix()
        c = c_ref[...]
        cen = c - _seg_mean(c, seg)
        rstd = lax.rsqrt(_seg_mean(cen * cen, seg) + EPS)
        n = cen * rstd
        rg = rg_ref[...]
        sg = _sigmoid(rg)
        gate = rg * sg
        dy = dm[:, cols]
        gc = gv[:, cols]
        dn = dy * gc * gate
        dg_ref[:, cols] += jnp.sum(dy * n * gate, axis=0, keepdims=True)
        drg_ref[...] = (dy * n * gc * (sg * (1.0 + rg * (1.0 - sg)))).astype(BF16)
        dc_ref[...] = rstd * (dn - _seg_mean(dn, seg) - n * _seg_mean(dn * n, seg))

    blk = pl.BlockSpec((tr, GROUP), lambda i: (i, 0))
    gsp = pl.BlockSpec((1, D_MODEL), lambda i: (0, 0))
    return pl.pallas_call(
        body, name="mix_post_bwd", grid=(s // tr,),
        in_specs=[pl.BlockSpec((tr, D_MODEL), lambda i: (i, 0))] + [blk] * 4 + [pl.BlockSpec((tr, GROUP), lambda i: (i, OFF_RG // 2)), gsp],
        out_specs=[blk] * 5 + [gsp],
        out_shape=[jax.ShapeDtypeStruct((s, GROUP), F32)] * 4 + [jax.ShapeDtypeStruct((s, GROUP), BF16), jax.ShapeDtypeStruct((1, D_MODEL), F32)],
        compiler_params=_cparams(("arbitrary",)),
    )(dmixed, oa, ob, oc, od, proj, g.reshape(1, D_MODEL))


def _pack_w_in(w):
    z = lambda n: jnp.zeros((w.shape[0], n), w.dtype)
    misc = jnp.concatenate([w[:, 768:772], z(KR_LANE - N_HEADS), w[:, 1156:1188], z(LANES - KR_LANE - ROPE_DIM)], axis=1)
    return jnp.concatenate([w[:, 0:768], w[:, 772:1028], w[:, 1188:2980], w[:, 1028:1156], misc], axis=1)


def _unpack_dw_in(d):
    m = OFF_MISC * LANES
    return jnp.concatenate([d[:, 0:768], d[:, m:m + N_HEADS], d[:, 768:1024], d[:, OFF_CKV * LANES:m],
                            d[:, m + KR_LANE:m + KR_LANE + ROPE_DIM], d[:, 1024:OFF_CKV * LANES]], axis=1)


def _pack_w_q(w):
    return jnp.pad(w.reshape(Q_RANK, N_HEADS, HEAD + ROPE_DIM), ((0, 0), (0, 0), (0, LANES - HEAD - ROPE_DIM))).reshape(Q_RANK, 4 * LANES)


def _unpack_dw_q(d):
    return d.reshape(Q_RANK, N_HEADS, LANES)[:, :, :HEAD + ROPE_DIM].reshape(Q_RANK, N_HEADS * (HEAD + ROPE_DIM))


def _pack_w_kv(w):
    w4 = w.reshape(KV_RANK, N_HEADS, 2 * HEAD)
    widen = lambda a: jnp.pad(a, ((0, 0), (0, 0), (0, LANES - HEAD))).reshape(KV_RANK, N_HEADS * LANES)
    return widen(w4[:, :, :HEAD]), widen(w4[:, :, HEAD:])


def _unpack_dw_kv(dk, dv):
    narrow = lambda a: a.reshape(KV_RANK, N_HEADS, LANES)[:, :, :HEAD]
    return jnp.concatenate([narrow(dk), narrow(dv)], axis=2).reshape(KV_RANK, 2 * N_HEADS * HEAD)


def _narrow_heads(a):
    return a.reshape(a.shape[0], N_HEADS, LANES)[:, :, :HEAD].reshape(a.shape[0], N_HEADS * HEAD)


def _widen_heads(a):
    return jnp.pad(a.reshape(a.shape[0], N_HEADS, HEAD), ((0, 0), (0, 0), (0, LANES - HEAD))).reshape(a.shape[0], N_HEADS * LANES)


def _layer_fwd(x, lw, tabs, tag, side=None, fox_side=None, late_weights=None, h1=None, next_gain=None):
    cos_m, sin_m, cos_r, sin_r = tabs
    if h1 is None:
        h1 = _norm_fwd(x, lw["g_mix_pre"], name=tag + "pre_norm")
    proj, projb = _matmul(h1, lw["w_in"], name=tag + "in_proj", also_bf16=True)
    bias_row = jnp.pad(lw["b_forget"], (FF_LANE, LANES - N_HEADS - FF_LANE)).reshape(1, LANES)
    cum_col, cum_row = _fox_cum(proj, bias_row)
    oa, lse_a, *fox_carried = _mixer_fwd("fox", projb, OFF_FQ, projb, OFF_FK, projb, OFF_FV, cum_col=cum_col, cum_row=cum_row,
                                         side=fox_side)
    if late_weights is not None:
        lw = {**lw, **late_weights(fox_carried[0])}
    qm, km, vm, cqn, ckvn = _mla_prep(proj, cos_m, sin_m, lw["g_q_lora"], lw["g_kv_lora"], lw["wq"], lw["wk"], lw["wv"])
    ob_wide, lse_b = _mixer_fwd("mla", qm, 0, km, 0, vm, 0)
    ob = _narrow_heads(ob_wide)
    qr, kr = _ret_prep(proj, cos_r, sin_r)
    oc, ret_states = _ret_fwd(qr, kr, projb, OFF_RV)
    od, tot_d, *carried = _mixer_fwd("sb", projb, OFF_SQ, projb, OFF_SK, projb, OFF_SV, side=side)
    mixed = _mix_post(oa, ob, oc, od, proj, lw["g_mix_out"])
    mix = _matmul(mixed, lw["w_out"], name=tag + "out_proj")
    x1, h2 = _norm_fwd(mix, lw["g_mix_post"], name=tag + "mix_post_norm", resid=x, out_dtype=F32, next_gain=lw["g_ffn_pre"])
    u = _matmul(h2, lw["w_ffn_up"], name=tag + "ffn_up", relu2=True, out_dtype=BF16, col_blocks=True)
    f = _matmul(u, lw["w_ffn_down"], name=tag + "ffn_down")
    x2, h_next = None, None
    if next_gain is not None:
        x2, h_next = _norm_fwd(f, lw["g_ffn_post"], name=tag + "ffn_post_norm", resid=x1, out_dtype=F32, next_gain=next_gain)
    saved = dict(x=x, h1=h1, proj=proj, projb=projb, bias_row=bias_row, cum_col=cum_col, cum_row=cum_row, oa=oa, lse_a=lse_a,
                 qm=qm, km=km, vm=vm, cqn=cqn, ckvn=ckvn, ob=ob, ob_wide=ob_wide, lse_b=lse_b, qr=qr, kr=kr, ret_states=ret_states, oc=oc, od=od, tot_d=tot_d, mixed=mixed,
                 mix=mix, x1=x1, h2=h2, u=u, f=f)
    return x2, saved, lw, (carried[0] if carried else None), h_next


def _layer_bwd(dx2, lw, sv, tabs, tag, side=None, ffn_side=None, fox_side=None, post_given=None, then_prev=None):
    cos_m, sin_m, cos_r, sin_r = tabs
    g = {}
    if post_given is None:
        df, g["g_ffn_post"] = _norm_bwd(sv["f"], lw["g_ffn_post"], dx2, name=tag + "ffn_post_norm_bwd", out_dtype=BF16)
    else:
        df, g["g_ffn_post"] = post_given
    du_pre = _matmul(df, lw["w_ffn_down"], name=tag + "ffn_down_dx", tb=True, out_dtype=BF16, relu2_of=sv["u"], side=ffn_side)
    ffn_carried = None
    if ffn_side is not None:
        du_pre, ffn_carried = du_pre
    g["w_ffn_down"] = _matmul(sv["u"], df, name=tag + "ffn_down_dw", ta=True)
    dh2 = _matmul(du_pre, lw["w_ffn_up"], name=tag + "ffn_up_dx", tb=True, col_blocks=True)
    g["w_ffn_up"] = _matmul(sv["h2"], du_pre, name=tag + "ffn_up_dw", ta=True, col_blocks=True)
    dx1, g["g_ffn_pre"], dmix, g["g_mix_post"] = _norm_bwd(sv["x1"], lw["g_ffn_pre"], dh2, name=tag + "ffn_pre_norm_bwd", add=dx2,
                                                           then=(sv["mix"], lw["g_mix_post"]))
    dmixed = _matmul(dmix, lw["w_out"], name=tag + "out_proj_dx", tb=True)
    g["w_out"] = _matmul(sv["mixed"], dmix, name=tag + "out_proj_dw", ta=True)
    proj, projb = sv["proj"], sv["projb"]
    doa, dob, doc, dod, drg, g["g_mix_out"] = _mix_post_bwd(dmixed, sv["oa"], sv["ob"], sv["oc"], sv["od"], proj, lw["g_mix_out"])
    dfq, dfk, dfv, dck, drs, *fox_carried = _mixer_bwd(
        "fox", projb, OFF_FQ, projb, OFF_FK, projb, OFF_FV, sv["oa"], doa, stat=sv["lse_a"], cum_col=sv["cum_col"],
        cum_row=sv["cum_row"], side=None if fox_side is None else fox_side(g))
    dqm, dkm, dvm = _mixer_bwd("mla", sv["qm"], 0, sv["km"], 0, sv["vm"], 0, sv["ob_wide"], _widen_heads(dob), stat=sv["lse_b"])
    dcq, dckv, dkr, dwq, dwk, dwv, g["g_q_lora"], g["g_kv_lora"] = _mla_prep_bwd(
        dqm, dkm, dvm, proj, sv["cqn"], sv["ckvn"], cos_m, sin_m, lw["g_q_lora"], lw["g_kv_lora"], lw["wq"], lw["wk"], lw["wv"])
    dqr, dkr_ret, drv = _ret_bwd(sv["qr"], sv["kr"], projb, OFF_RV, sv["ret_states"], doc)
    drq, drk = _ret_prep_bwd(dqr, dkr_ret, cos_r, sin_r)
    if callable(side):
        side = side(g, ffn_carried, fox_carried[0] if fox_carried else None)
    dsq, dsk, dsv, *carried = _mixer_bwd("sb", projb, OFF_SQ, projb, OFF_SK, projb, OFF_SV, sv["od"], dod, stat=sv["tot_d"], side=side)
    dmisc, db_row = _fox_gate_bwd(dck, drs, proj, sv["bias_row"], dkr)
    b = lambda a: a.astype(BF16)
    dproj = jnp.concatenate([b(dfq), b(dfk), b(dfv), dcq, drq, drk, b(drv), drg, b(dsq), b(dsk), b(dsv), dckv, dmisc], axis=1)
    dh1 = _matmul(dproj, lw["w_in"], name=tag + "in_proj_dx", tb=True)
    g["w_in"] = _matmul(sv["h1"], dproj, name=tag + "in_proj_dw", ta=True)
    dx, g["g_mix_pre"], *prev_post = _norm_bwd(sv["x"], lw["g_mix_pre"], dh1, name=tag + "pre_norm_bwd", add=dx1, then=then_prev)
    g["b_forget"] = db_row[0, FF_LANE:FF_LANE + N_HEADS]
    g["wq"], g["wk"], g["wv"] = dwq, dwk, dwv
    return dx, g, (carried[0] if carried else None), (tuple(prev_post) if prev_post else None)


def _local_step(x, positions, layers, target):
    s = x.shape[0]
    tabs = _rope_tables(positions.reshape(s, 1))
    saved, h1 = [], None
    for li, lw in enumerate(layers):
        nxt = layers[li + 1]["g_mix_pre"] if li + 1 < len(layers) else None
        x, sv, _, _, h1 = _layer_fwd(x, lw, tabs, "l%d_" % li, h1=h1, next_gain=nxt)
        saved.append(sv)
    loss_row, dx, df, dg = _loss_head(saved[-1]["f"], layers[-1]["g_ffn_post"], saved[-1]["x1"], target)
    grads, post = [None] * len(layers), (df, dg)
    for li in reversed(range(len(layers))):
        prev = (saved[li - 1]["f"], layers[li - 1]["g_ffn_post"]) if li > 0 else None
        dx, grads[li], _, post = _layer_bwd(dx, layers[li], saved[li], tabs, "l%d_" % li, post_given=post, then_prev=prev)
    return loss_row[0, 0], dx, grads


def _adamw(w, g, m, v, *, name):
    d, r, c = w.shape
    tr = 256 if r % 256 == 0 else r
    blk = pl.BlockSpec((None, tr, c), lambda l, i: (l, i, 0))
    c1 = 1.0 - ADAM_B1 ** ADAM_STEP
    c2 = 1.0 - ADAM_B2 ** ADAM_STEP

    def body(w_ref, g_ref, m_ref, v_ref, d_ref, mo_ref, vo_ref):
        gv = g_ref[...]
        mn = ADAM_B1 * m_ref[...] + (1.0 - ADAM_B1) * gv
        vn = ADAM_B2 * v_ref[...] + (1.0 - ADAM_B2) * jnp.square(gv)
        mo_ref[...] = mn
        vo_ref[...] = vn
        d_ref[...] = -ADAM_LR * ((mn / c1) / (jnp.sqrt(vn / c2) + ADAM_EPS) + ADAM_WD * w_ref[...])

    return pl.pallas_call(
        body, name=name, grid=(d, r // tr), in_specs=[blk] * 4, out_specs=[blk] * 3,
        out_shape=[jax.ShapeDtypeStruct((d, r, c), F32)] * 3, compiler_params=_cparams(("parallel", "parallel")),
    )(w, g, m, v)


def _adamw_lead(w, g, m, v, *, name, steps):
    a, b, c = w.shape
    blk = pl.BlockSpec((a // steps, b, c), lambda i: (i, 0, 0))
    c1 = 1.0 - ADAM_B1 ** ADAM_STEP
    c2 = 1.0 - ADAM_B2 ** ADAM_STEP

    def body(w_ref, g_ref, m_ref, v_ref, d_ref, mo_ref, vo_ref):
        gv = g_ref[...]
        mn = ADAM_B1 * m_ref[...] + (1.0 - ADAM_B1) * gv
        vn = ADAM_B2 * v_ref[...] + (1.0 - ADAM_B2) * jnp.square(gv)
        mo_ref[...] = mn
        vo_ref[...] = vn
        d_ref[...] = -ADAM_LR * ((mn / c1) / (jnp.sqrt(vn / c2) + ADAM_EPS) + ADAM_WD * w_ref[...])

    return pl.pallas_call(
        body, name=name, grid=(steps,), in_specs=[blk] * 4, out_specs=[blk] * 3,
        out_shape=[jax.ShapeDtypeStruct((a, b, c), F32)] * 3, compiler_params=_cparams(("parallel",)),
    )(w, g, m, v)


SC_TILES = 32
SC_ROWS = 8


def _adamw_sparsecore(ws, gs, ms, vs, *, name):
    n = len(ws)
    c = ws[0].shape[2]
    c1 = 1.0 - ADAM_B1 ** ADAM_STEP
    c2 = 1.0 - ADAM_B2 ** ADAM_STEP
    pieces = [(t, l, r0) for t in range(n) for l in range(ws[t].shape[0]) for r0 in range(0, ws[t].shape[1] // SC_TILES, SC_ROWS)]

    def body(*refs):
        ins, outs = refs[:4 * n], refs[4 * n:7 * n]
        bufs, sem_in, sem_out = refs[7 * n:7 * n + 8], refs[7 * n + 8], refs[7 * n + 9]
        tile = lax.axis_index("subcore") * 2 + lax.axis_index("core")

        def window(k):
            t, l, r0 = pieces[k]
            return t, (l, pl.ds(tile * (ws[t].shape[1] // SC_TILES) + r0, SC_ROWS))

        def loads(k):
            t, at = window(k)
            return [pltpu.make_async_copy(ins[j * n + t].at[at], bufs[4 * (k % 2) + j], sem_in.at[k % 2]) for j in range(4)]

        def stores(k):
            t, at = window(k)
            return [pltpu.make_async_copy(bufs[4 * (k % 2) + j], outs[(j - 1) * n + t].at[at], sem_out.at[k % 2]) for j in (1, 2, 3)]

        def update(k):
            gb, wb, mb, vb = bufs[4 * (k % 2):4 * (k % 2) + 4]

            def adam(gv, wv, mv, vv):
                mn = ADAM_B1 * mv + (1.0 - ADAM_B1) * gv
                vn = ADAM_B2 * vv + (1.0 - ADAM_B2) * (gv * gv)
                return -ADAM_LR * ((mn / c1) / (jnp.sqrt(vn / c2) + ADAM_EPS) + ADAM_WD * wv), mn, vn

            @pl.loop(0, SC_ROWS)
            def _(rr):
                last = (rr, pl.ds(c - 16, 16))
                if c % 16:
                    end = adam(gb[last], wb[last], mb[last], vb[last])

                @pl.loop(0, c // 16 * 16, step=16)
                def _(i):
                    s = (rr, pl.ds(i, 16))
                    wb[s], mb[s], vb[s] = adam(gb[s], wb[s], mb[s], vb[s])

                if c % 16:
                    wb[last], mb[last], vb[last] = end

        for cp in loads(0):
            cp.start()
        for k in range(len(pieces)):
            if k + 1 < len(pieces):
                if k >= 1:
                    for cp in stores(k - 1):
                        cp.wait()
                for cp in loads(k + 1):
                    cp.start()
            for cp in loads(k):
                cp.wait()
            update(k)
            for cp in stores(k):
                cp.start()
        for k in range(max(len(pieces) - 2, 0), len(pieces)):
            for cp in stores(k):
                cp.wait()

    out = pl.kernel(
        body, name=name, out_type=[jax.ShapeDtypeStruct(t.shape, F32) for t in ws] * 3,
        mesh=plsc.VectorSubcoreMesh(core_axis_name="core", subcore_axis_name="subcore"),
        scratch_types=[pltpu.VMEM((SC_ROWS, c), F32)] * 8 + [pltpu.SemaphoreType.DMA((2,)), pltpu.SemaphoreType.DMA((2,))],
    )(*gs, *ws, *ms, *vs)
    return out[:n], out[n:2 * n], out[2 * n:]


BIG = ("w_in", "w_q_up", "w_kv_up", "w_out", "w_ffn_up", "w_ffn_down")
SMALL = ("g_mix_pre", "b_forget", "g_q_lora", "g_kv_lora", "g_mix_out", "g_mix_post", "g_ffn_pre", "g_ffn_post")
N_CHIPS = 4
ANY = pl.BlockSpec(memory_space=pl.ANY)


def _mesh_pos():
    return lax.axis_index("x"), lax.axis_index("y"), lax.axis_index("c")


def _other_chips(x, y):
    return [(1 - x, y), (x, 1 - y), (1 - x, 1 - y)]


def _rows_half(ref, half):
    h = ref.shape[-2] // 2
    return ref.at[(slice(None),) * (len(ref.shape) - 2) + (pl.ds(half * h, h), slice(None))]


def _remote(src, dst, send_sem, recv_sem, device):
    return pltpu.make_async_remote_copy(src_ref=src, dst_ref=dst, send_sem=send_sem, recv_sem=recv_sem, device_id=device,
                                        device_id_type=MESH)


def _comm_call(body, name, args, out_shape, n_sems):
    return pl.pallas_call(
        body, name=name, in_specs=[ANY] * len(args), out_specs=[ANY] * len(out_shape), out_shape=out_shape,
        scratch_shapes=[pltpu.SemaphoreType.DMA((n_sems,)), pltpu.SemaphoreType.DMA((n_sems,))],
        compiler_params=pltpu.CompilerParams(has_side_effects=True),
    )(*args)


def _run_side_job(side, name):
    si = len(side.inputs)

    def body(*refs):
        args = (refs[:si], refs[si:-2], refs[-2], refs[-1])
        sends = side.sends(*args)
        for cp in sends:
            cp.start()
        for cp in side.recvs(*args):
            cp.wait_recv()
        for cp in sends:
            cp.wait_send()

    return _comm_call(body, name, side.inputs, side.out_shape, side.n_sems)


def _gather_job(shards):
    n = len(shards)

    def copies(own_block, ins, outs, send_sems, recv_sems):
        x, y, c = _mesh_pos()
        return [_remote(_rows_half(ins[t], c), _rows_half(outs[t].at[2 * x + y if own_block else 2 * px + py], c),
                        send_sems.at[3 * t + j], recv_sems.at[3 * t + j], (px, py, c))
                for t in range(n) for j, (px, py) in enumerate(_other_chips(x, y))]

    return _SideJob(shards, [jax.ShapeDtypeStruct((N_CHIPS,) + a.shape, a.dtype) for a in shards], 3 * n,
                    functools.partial(copies, True), functools.partial(copies, False))


def _forward_halves(gathered):
    n = len(gathered)

    def body(*refs):
        bufs, send_sems, recv_sems = refs[n:2 * n], refs[-2], refs[-1]
        x, y, c = _mesh_pos()

        def d2d(t, j, block, half):
            region = _rows_half(bufs[t].at[block], half)
            return _remote(region, region, send_sems.at[3 * t + j], recv_sems.at[3 * t + j], (x, y, 1 - c))

        peers = list(enumerate(_other_chips(x, y)))
        sends = [d2d(t, j, 2 * px + py, c) for t in range(n) for j, (px, py) in peers]
        for cp in sends:
            cp.start()
        for t in range(n):
            for j, (px, py) in peers:
                d2d(t, j, 2 * px + py, 1 - c).wait_recv()
        for cp in sends:
            cp.wait_send()

    return pl.pallas_call(
        body, name="gather_forward", in_specs=[ANY] * n, out_specs=[ANY] * n,
        out_shape=[jax.ShapeDtypeStruct(g.shape, g.dtype) for g in gathered], input_output_aliases={t: t for t in range(n)},
        scratch_shapes=[pltpu.SemaphoreType.DMA((3 * n,)), pltpu.SemaphoreType.DMA((3 * n,))],
        compiler_params=pltpu.CompilerParams(has_side_effects=True),
    )(*gathered)


def _exchange_halves_job(gs):
    n = len(gs)

    def copies(ins, outs, send_sems, recv_sems):
        x, y, c = _mesh_pos()
        return [_remote(_rows_half(ins[t], 1 - c), outs[t], send_sems.at[t], recv_sems.at[t], (x, y, 1 - c)) for t in range(n)]

    out_shape = [jax.ShapeDtypeStruct(g.shape[:2] + (g.shape[2] // 2, g.shape[3]), g.dtype) for g in gs]
    return _SideJob(gs, out_shape, n, copies, copies)


def _pair_add(g, r, c_idx, *, name):
    nb, d, rows, cols = g.shape
    h = rows // 2
    tr = min(h, 512)
    nt = h // tr

    def body(c_ref, g_ref, r_ref, p_ref, pb_ref):
        s = g_ref[...] + r_ref[...]
        p_ref[...] = s
        pb_ref[...] = s.astype(BF16)

    blk = pl.BlockSpec((1, 1, tr, cols), lambda k, l, i, c_ref: (k, l, i, 0))
    return pl.pallas_call(
        body, name=name,
        grid_spec=pltpu.PrefetchScalarGridSpec(
            num_scalar_prefetch=1, grid=(nb, d, nt),
            in_specs=[pl.BlockSpec((1, 1, tr, cols), lambda k, l, i, c_ref: (k, l, c_ref[0] * nt + i, 0)), blk],
            out_specs=[blk, blk]),
        out_shape=[jax.ShapeDtypeStruct((nb, d, h, cols), F32), jax.ShapeDtypeStruct((nb, d, h, cols), BF16)],
        compiler_params=_cparams(("parallel", "parallel", "parallel")),
    )(c_idx, g, r)


def _exchange_chips_job(pbs):
    n = len(pbs)

    def copies(ins, outs, send_sems, recv_sems):
        x, y, c = _mesh_pos()
        return [_remote(ins[t].at[2 * px + py], outs[t].at[j], send_sems.at[3 * t + j], recv_sems.at[3 * t + j], (px, py, c))
                for t in range(n) for j, (px, py) in enumerate(_other_chips(x, y))]

    return _SideJob(pbs, [jax.ShapeDtypeStruct((3,) + p.shape[1:], p.dtype) for p in pbs], 3 * n, copies, copies)


def _chip_add(p, r, k_idx, *, name):
    _, d, h, cols = p.shape
    tr = min(h, 512)
    nt = h // tr

    def body(k_ref, p_ref, r_ref, o_ref):
        o_ref[0] = ((p_ref[0, 0] + r_ref[0, 0].astype(F32)) + r_ref[1, 0].astype(F32)) + r_ref[2, 0].astype(F32)

    return pl.pallas_call(
        body, name=name,
        grid_spec=pltpu.PrefetchScalarGridSpec(
            num_scalar_prefetch=1, grid=(d, nt),
            in_specs=[pl.BlockSpec((1, 1, tr, cols), lambda l, i, k_ref: (k_ref[0], l, i, 0)),
                      pl.BlockSpec((3, 1, tr, cols), lambda l, i, k_ref: (0, l, i, 0))],
            out_specs=pl.BlockSpec((1, tr, cols), lambda l, i, k_ref: (l, i, 0))),
        out_shape=jax.ShapeDtypeStruct((d, h, cols), F32), compiler_params=_cparams(("parallel", "parallel")),
    )(k_idx, p, r)


def _share_halves(qs):
    n = len(qs)

    def body(*refs):
        ins, outs, send_sems, recv_sems = refs[:n], refs[n:2 * n], refs[2 * n], refs[2 * n + 1]
        x, y, c = _mesh_pos()
        cps = [_remote(ins[t], outs[t], send_sems.at[t], recv_sems.at[t], (x, y, 1 - c)) for t in range(n)]
        for cp in cps:
            cp.start()
        for cp in cps:
            cp.wait_recv()
        for cp in cps:
            cp.wait_send()

    return _comm_call(body, "grad_pair_share", qs, [jax.ShapeDtypeStruct(q.shape, q.dtype) for q in qs], n)


def _all_reduce_small(v):
    r, cols = v.shape
    n_dev = 8

    def body(v_ref, o_ref, buf, send_sems, recv_sems):
        x, y, c = _mesh_pos()
        me = 4 * x + 2 * y + c
        buf[me] = v_ref[...]

        def peer(j):
            return (1 - x if j & 4 else x, 1 - y if j & 2 else y, 1 - c if j & 1 else c)

        def copy(j, slot):
            return pltpu.make_async_remote_copy(src_ref=v_ref, dst_ref=buf.at[slot], send_sem=send_sems.at[j - 1],
                                                recv_sem=recv_sems.at[j - 1], device_id=peer(j), device_id_type=MESH)

        sends = [copy(j, me) for j in range(1, n_dev)]
        for cp in sends:
            cp.start()
        for j in range(1, n_dev):
            px, py, pc = peer(j)
            copy(j, 4 * px + 2 * py + pc).wait_recv()
        for cp in sends:
            cp.wait_send()
        acc = buf[0]
        for d in range(1, n_dev):
            acc = acc + buf[d]
        o_ref[...] = acc

    vm = pl.BlockSpec(memory_space=pltpu.VMEM)
    return pl.pallas_call(
        body, name="small_all_reduce", in_specs=[vm], out_specs=vm, out_shape=jax.ShapeDtypeStruct((r, cols), F32),
        scratch_shapes=[pltpu.VMEM((n_dev, r, cols), F32), pltpu.SemaphoreType.DMA((n_dev - 1,)), pltpu.SemaphoreType.DMA((n_dev - 1,))],
        compiler_params=pltpu.CompilerParams(has_side_effects=True),
    )(v)


_COL_SHARDED = ("w_in", "w_q_up", "w_kv_up", "w_ffn_up")


def _shard_cols(blocks, a, b):
    c = blocks[0].shape[-1]
    out = []
    while a < b:
        k = a // c
        hi = min(b, (k + 1) * c)
        out.append(blocks[k][:, a - k * c:hi - k * c])
        a = hi
    return out


def _pack_w_in_shards(blocks):
    z = lambda n: [jnp.zeros((blocks[0].shape[0], n), blocks[0].dtype)]
    cols = lambda a, b: _shard_cols(blocks, a, b)
    return jnp.concatenate(cols(0, 768) + cols(772, 1028) + cols(1188, 2980) + cols(1028, 1156) + cols(768, 772)
                           + z(KR_LANE - N_HEADS) + cols(1156, 1188) + z(LANES - KR_LANE - ROPE_DIM), axis=1)


def _whole_layer(name, blocks):
    if name in _COL_SHARDED:
        return jnp.concatenate([blocks[k] for k in range(N_CHIPS)], axis=1)
    return blocks.reshape(N_CHIPS * blocks.shape[1], blocks.shape[2])


def _split_layer(name, whole):
    if name in _COL_SHARDED:
        c = whole.shape[1] // N_CHIPS
        return jnp.stack([whole[:, k * c:(k + 1) * c] for k in range(N_CHIPS)])
    return whole.reshape(N_CHIPS, whole.shape[0] // N_CHIPS, whole.shape[1])


def _small_to_rows(d):
    v = jnp.concatenate([d[k].astype(F32).reshape(-1) for k in SMALL])
    rows = -(-v.shape[0] // (8 * LANES)) * 8
    return jnp.pad(v, (0, rows * LANES - v.shape[0])).reshape(rows, LANES)


def _small_from_rows(rows, shapes):
    v = rows.reshape(-1)
    out, o = {}, 0
    for k in SMALL:
        sz = int(np.prod(shapes[k]))
        out[k] = v[o:o + sz].reshape(shapes[k])
        o += sz
    return out


_ARG_NAMES = ("x", "positions", "g_mix_pre", "w_in", "b_forget", "g_q_lora", "w_q_up", "g_kv_lora", "w_kv_up", "g_mix_out", "w_out",
              "g_mix_post", "g_ffn_pre", "w_ffn_up", "w_ffn_down", "g_ffn_post")
_WEIGHTS = _ARG_NAMES[2:]


def kernel(x, positions, g_mix_pre, w_in, b_forget, g_q_lora, w_q_up, g_kv_lora, w_kv_up, g_mix_out, w_out, g_mix_post, g_ffn_pre, w_ffn_up, w_ffn_down, g_ffn_post, loss_target, m_g_mix_pre, m_w_in, m_b_forget, m_g_q_lora, m_w_q_up, m_g_kv_lora, m_w_kv_up, m_g_mix_out, m_w_out, m_g_mix_post, m_g_ffn_pre, m_w_ffn_up, m_w_ffn_down, m_g_ffn_post, v_g_mix_pre, v_w_in, v_b_forget, v_g_q_lora, v_w_q_up, v_g_kv_lora, v_w_kv_up, v_g_mix_out, v_w_out, v_g_mix_post, v_g_ffn_pre, v_w_ffn_up, v_w_ffn_down, v_g_ffn_post):
    w = dict(g_mix_pre=g_mix_pre, w_in=w_in, b_forget=b_forget, g_q_lora=g_q_lora, w_q_up=w_q_up, g_kv_lora=g_kv_lora, w_kv_up=w_kv_up,
             g_mix_out=g_mix_out, w_out=w_out, g_mix_post=g_mix_post, g_ffn_pre=g_ffn_pre, w_ffn_up=w_ffn_up, w_ffn_down=w_ffn_down,
             g_ffn_post=g_ffn_post)
    m = dict(g_mix_pre=m_g_mix_pre, w_in=m_w_in, b_forget=m_b_forget, g_q_lora=m_g_q_lora, w_q_up=m_w_q_up, g_kv_lora=m_g_kv_lora,
             w_kv_up=m_w_kv_up, g_mix_out=m_g_mix_out, w_out=m_w_out, g_mix_post=m_g_mix_post, g_ffn_pre=m_g_ffn_pre,
             w_ffn_up=m_w_ffn_up, w_ffn_down=m_w_ffn_down, g_ffn_post=m_g_ffn_post)
    v = dict(g_mix_pre=v_g_mix_pre, w_in=v_w_in, b_forget=v_b_forget, g_q_lora=v_g_q_lora, w_q_up=v_w_q_up, g_kv_lora=v_g_kv_lora,
             w_kv_up=v_w_kv_up, g_mix_out=v_g_mix_out, w_out=v_w_out, g_mix_post=v_g_mix_post, g_ffn_pre=v_g_ffn_pre,
             w_ffn_up=v_w_ffn_up, w_ffn_down=v_w_ffn_down, g_ffn_post=v_g_ffn_post)
    small_shapes = {k: w[k].shape for k in SMALL}
    c_idx = lax.axis_index("c").astype(jnp.int32).reshape(1)
    k_idx = (2 * lax.axis_index("x") + lax.axis_index("y")).astype(jnp.int32).reshape(1)
    first_core = lax.axis_index("c") == 0

    mine = 2 * lax.axis_index("x") + lax.axis_index("y")
    shards_b = [{k: w[k][l:l + 1].astype(BF16) for k in BIG} for l in range(DEPTH)]
    gains = [dict(g_mix_pre=g_mix_pre[l], b_forget=b_forget[l], g_q_lora=g_q_lora[l], g_kv_lora=g_kv_lora[l], g_mix_out=g_mix_out[l],
                  g_mix_post=g_mix_post[l], g_ffn_pre=g_ffn_pre[l], g_ffn_post=g_ffn_post[l]) for l in range(DEPTH)]
    FIRST, LATER = ("w_in", "w_q_up", "w_kv_up"), ("w_out", "w_ffn_up", "w_ffn_down")
    EARLY_GRADS, LATE_GRADS = ("w_ffn_down", "w_ffn_up", "w_out"), ("w_in", "w_q_up", "w_kv_up")
    SC_ADAMW = EARLY_GRADS

    def gather_job(l, names):
        return _gather_job([shards_b[l][k] for k in names])

    def weights_of(l, names, gathered):
        four = {k: lax.dynamic_update_slice(g, shards_b[l][k][None], (mine, 0, 0, 0))[:, 0]
                for k, g in zip(names, _forward_halves(gathered))}
        out = {}
        for k in names:
            if k == "w_in":
                out["w_in"] = _pack_w_in_shards(four[k])
            elif k == "w_q_up":
                out["wq"] = _pack_w_q(_whole_layer(k, four[k]))
            elif k == "w_kv_up":
                out["wk"], out["wv"] = _pack_w_kv(_whole_layer(k, four[k]))
            elif k == "w_ffn_up":
                out[k] = four[k]
            else:
                out[k] = _whole_layer(k, four[k])
        return out

    def grad_blocks(names, g):
        whole = dict(w_in=lambda: _unpack_dw_in(g["w_in"]), w_q_up=lambda: _unpack_dw_q(g["wq"]),
                     w_kv_up=lambda: _unpack_dw_kv(g["wk"], g["wv"]), w_out=lambda: g["w_out"], w_ffn_down=lambda: g["w_ffn_down"])
        return [(g[k] if k == "w_ffn_up" else _split_layer(k, whole[k]()))[:, None] for k in names]

    def pair_sums(names, blocks, theirs):
        return [_pair_add(b, r, c_idx, name="grad_pair_add_" + k) for k, b, r in zip(names, blocks, theirs)]

    def exchange_job(*pairs):
        return _exchange_chips_job([pb for pair in pairs for (_, pb) in pair])

    def finish_grads(names, pair, partial):
        half = [_chip_add(p, r, k_idx, name="grad_chip_add_" + k) for k, (p, _), r in zip(names, pair, partial)]
        return {k: jnp.where(first_core, jnp.concatenate([q, s], axis=1), jnp.concatenate([s, q], axis=1))
                for k, q, s in zip(names, half, _share_halves(half))}

    seq = x.shape[1]
    tabs = _rope_tables(positions[0].reshape(seq, 1))
    first0 = weights_of(0, FIRST, _run_side_job(gather_job(0, FIRST), "gather_weights_l0"))
    x1, saved0, lw0, gathered1, h1 = _layer_fwd(x[0], {**gains[0], **first0}, tabs, "l0_", fox_side=gather_job(0, LATER),
                                                late_weights=lambda got: weights_of(0, LATER, got), side=gather_job(1, BIG),
                                                next_gain=gains[1]["g_mix_pre"])
    lw1 = {**gains[1], **weights_of(1, BIG, gathered1)}
    _, saved1, _, _, _ = _layer_fwd(x1, lw1, tabs, "l1_", h1=h1)
    loss_row, dx, df1, dg1 = _loss_head(saved1["f"], lw1["g_ffn_post"], saved1["x1"], loss_target[0])
    loss = lax.psum(loss_row[0, 0], ("x", "y", "c"))
    dx, grads1, _, post0 = _layer_bwd(dx, lw1, saved1, tabs, "l1_", post_given=(df1, dg1),
                                      then_prev=(saved0["f"], lw0["g_ffn_post"]))
    blocks1 = grad_blocks(BIG, grads1)
    early_blocks0, pair1, early0 = [], [], []

    def beside_l0_fox_backward(g):
        early_blocks0.extend(grad_blocks(EARLY_GRADS, g))
        return _exchange_halves_job(early_blocks0)

    def beside_l0_sb_backward(g, theirs1, theirs_early0):
        pair1.extend(pair_sums(BIG, blocks1, theirs1))
        early0.extend(pair_sums(EARLY_GRADS, early_blocks0, theirs_early0))
        return exchange_job(pair1, early0)

    dx, grads0, partial, _ = _layer_bwd(dx, lw0, saved0, tabs, "l0_", ffn_side=_exchange_halves_job(blocks1),
                                        fox_side=beside_l0_fox_backward, side=beside_l0_sb_backward, post_given=post0)
    big1 = finish_grads(BIG, pair1, partial[:len(BIG)])
    big0 = finish_grads(EARLY_GRADS, early0, partial[len(BIG):])
    g_early = {k: jnp.concatenate([big0[k], big1[k]], axis=0) for k in SC_ADAMW}
    sc_delta, sc_m, sc_v = _adamw_sparsecore([w[k] for k in SC_ADAMW], [g_early[k] for k in SC_ADAMW], [m[k] for k in SC_ADAMW],
                                             [v[k] for k in SC_ADAMW], name="adamw_sparsecore")
    late_blocks0 = grad_blocks(LATE_GRADS, grads0)
    late0 = pair_sums(LATE_GRADS, late_blocks0, _run_side_job(_exchange_halves_job(late_blocks0), "grad_pair_exchange_l0"))
    big0.update(finish_grads(LATE_GRADS, late0, _run_side_job(exchange_job(late0), "grad_chip_exchange_l0")))
    g_big = {k: g_early[k] if k in SC_ADAMW else jnp.concatenate([big0[k], big1[k]], axis=0) for k in BIG}
    w_in_lead = _adamw_lead(*[jnp.transpose(t["w_in"], (2, 0, 1)) for t in (w, g_big, m, v)], name="adamw_w_in", steps=5)
    sc_late = [[jnp.transpose(t, (1, 2, 0))] for t in w_in_lead]
    grads = [grads0, grads1]

    g_small_local = {k: jnp.stack([grads[l][k].reshape(small_shapes[k][1:]) for l in range(DEPTH)]) for k in SMALL}
    g_small = _small_from_rows(_all_reduce_small(_small_to_rows(g_small_local)), small_shapes)

    g_all = {**g_big, **g_small}
    delta, new_m, new_v = {}, {}, {}
    for k in BIG:
        if k in SC_ADAMW:
            i = SC_ADAMW.index(k)
            delta[k], new_m[k], new_v[k] = sc_delta[i], sc_m[i], sc_v[i]
        elif k == "w_in":
            delta[k], new_m[k], new_v[k] = [t[0] for t in sc_late]
        else:
            delta[k], new_m[k], new_v[k] = _adamw(w[k], g_all[k], m[k], v[k], name="adamw_" + k)
    ds, ms, vs = _adamw(*[_small_to_rows(t)[None] for t in (w, g_small, m, v)], name="adamw_small")
    delta.update(_small_from_rows(ds, small_shapes))
    new_m.update(_small_from_rows(ms, small_shapes))
    new_v.update(_small_from_rows(vs, small_shapes))

    grad_x = dx.reshape(x.shape)
    return (loss, grad_x, *[g_all[k] for k in _WEIGHTS], *[delta[k] for k in _WEIGHTS], *[new_m[k] for k in _WEIGHTS],
            *[new_v[k] for k in _WEIGHTS])
```

```python
import functools
import math

import numpy as np
import jax
import jax.numpy as jnp
from jax import lax
from jax.experimental import pallas as pl
from jax.experimental.pallas import tpu as pltpu
from jax.experimental.pallas import tpu_sc as plsc

F32 = jnp.float32
BF16 = jnp.bfloat16
MESH = pl.DeviceIdType.MESH

D_MODEL = 1024
DEPTH = 2
CHUNK = 64
GROUP = 256
HEAD = 64
N_HEADS = 4
Q_RANK = 256
KV_RANK = 128
ROPE_DIM = 32
D_FF = 4096
D_IN = 2980
D_INP = 3072
ROPE_BASE = 10000.0
EPS = 1e-6
LANES = 128
TQ = 128
GATE_ROWS = 512
CONTRACT_TILE = 4096
NEG = -1e30

ADAM_LR, ADAM_B1, ADAM_B2, ADAM_EPS, ADAM_WD, ADAM_STEP = 0.001, 0.9, 0.999, 1e-08, 0.01, 10

OFF_FQ, OFF_FK, OFF_FV, OFF_CQ = 0, 2, 4, 6
OFF_RQ, OFF_RK, OFF_RV, OFF_RG = 8, 10, 12, 14
OFF_SQ, OFF_SK, OFF_SV = 16, 18, 20
OFF_CKV, OFF_MISC = 22, 23
FF_LANE, KR_LANE = 0, 64

VMEM_LIMIT = 56 * 1024 * 1024


def _tile(dim, pref):
    return pref if dim % pref == 0 else dim


def _cparams(sem, vmem=None):
    return pltpu.CompilerParams(dimension_semantics=sem, vmem_limit_bytes=vmem or VMEM_LIMIT)


def _dot(a, b):
    return jnp.dot(a, b, preferred_element_type=F32)


def _dot_nt(a, b):
    return lax.dot_general(a, b, (((1,), (1,)), ((), ())), preferred_element_type=F32)


def _dot_tn(a, b):
    return lax.dot_general(a, b, (((0,), (0,)), ((), ())), preferred_element_type=F32)


def _dot_exact(a, b):
    return jnp.dot(a, b, precision=lax.Precision.HIGHEST, preferred_element_type=F32)


def _matmul(a, b, *, name, ta=False, tb=False, out_dtype=F32, tm=1024, tn=1024, tk=CONTRACT_TILE,
            relu2=False, relu2_of=None, also_bf16=False, side=None, col_blocks=False):
    if ta:
        kdim, m = a.shape
    else:
        m, kdim = a.shape
    if col_blocks and not ta:
        n = b.shape[1] if tb else b.shape[0] * b.shape[2]
        if tb:
            kdim = b.shape[0] * b.shape[2]
    else:
        n = b.shape[0] if tb else b.shape[1]
    tm, tn, tk = _tile(m, tm), _tile(n, tn), _tile(kdim, tk)
    nk = kdim // tk
    a_spec = pl.BlockSpec((tk, tm), lambda i, j, k: (k, i)) if ta else pl.BlockSpec((tm, tk), lambda i, j, k: (i, k))
    b_spec = pl.BlockSpec((tn, tk), lambda i, j, k: (j, k)) if tb else pl.BlockSpec((tk, tn), lambda i, j, k: (k, j))
    o_spec = pl.BlockSpec((tm, tn), lambda i, j, k: (i, j))
    if col_blocks and ta:
        o_spec = pl.BlockSpec((None, tm, tn), lambda i, j, k: (j, i, 0))
    elif col_blocks and tb:
        assert tk == kdim
        b_spec = pl.BlockSpec((b.shape[0], tn, b.shape[2]), lambda i, j, k: (0, j, 0))
    elif col_blocks:
        assert b.shape[2] == tn
        b_spec = pl.BlockSpec((None, tk, tn), lambda i, j, k: (j, k, 0))
    two = also_bf16

    def body(*refs):
        refs = list(refs)
        a_ref, b_ref = refs[0], refs[1]
        e_ref = refs[2] if relu2_of is not None else None
        pos = 3 if relu2_of is not None else 2
        o_ref = refs[pos]
        o2_ref = refs[pos + 1] if two else None
        acc_ref = refs[-1]
        k = pl.program_id(2)
        av = a_ref[...].astype(BF16)
        if col_blocks and tb:
            bv = jnp.concatenate([b_ref[q] for q in range(b.shape[0])], axis=1).astype(BF16)
        else:
            bv = b_ref[...].astype(BF16)
        if ta:
            part = _dot_tn(av, bv)
        elif tb:
            part = _dot_nt(av, bv)
        else:
            part = _dot(av, bv)

        @pl.when(k == 0)
        def _():
            acc_ref[...] = part

        @pl.when(k > 0)
        def _():
            acc_ref[...] += part

        @pl.when(k == nk - 1)
        def _():
            r = acc_ref[...]
            if relu2_of is not None:
                r = r * (2.0 * jnp.sqrt(e_ref[...].astype(F32)))
            if relu2:
                r = jnp.square(jnp.maximum(r, 0.0))
            o_ref[...] = r.astype(o_ref.dtype)
            if also_bf16:
                o2_ref[...] = r.astype(BF16)

    in_specs = [a_spec, b_spec]
    args = [a, b]
    if relu2_of is not None:
        in_specs.append(o_spec)
        args.append(relu2_of)
    out_shape = [jax.ShapeDtypeStruct((n // tn, m, tn) if (col_blocks and ta) else (m, n), out_dtype)]
    out_specs = [o_spec]
    if two:
        out_shape.append(jax.ShapeDtypeStruct((m, n), BF16))
        out_specs.append(o_spec)
    grid = (m // tm, n // tn, nk)
    side_in, side_out, side_scratch = _side_specs(side)
    res = pl.pallas_call(
        _carry_side_job(body, len(args), len(out_shape), side, grid), name=name, grid=grid,
        in_specs=in_specs + side_in, out_specs=out_specs + side_out,
        out_shape=out_shape + ([] if side is None else side.out_shape),
        scratch_shapes=[pltpu.VMEM((tm, tn), F32)] + side_scratch,
        compiler_params=_cparams(("parallel", "parallel", "arbitrary") if side is None else ("arbitrary",) * 3),
    )(*args, *([] if side is None else side.inputs))
    main = res[:len(out_shape)]
    main = main if two else main[0]
    return main if side is None else (main, res[len(out_shape):])


def _rms(x, g):
    r = lax.rsqrt(jnp.mean(x * x, axis=-1, keepdims=True) + EPS)
    return x * r * g


def _rms_bwd(x, g, dy):
    r = lax.rsqrt(jnp.mean(x * x, axis=-1, keepdims=True) + EPS)
    xh = x * r
    gdy = dy * g
    dx = r * (gdy - xh * jnp.mean(xh * gdy, axis=-1, keepdims=True))
    return dx, xh * dy


def _norm_fwd(x, g, *, name, resid=None, out_dtype=BF16, next_gain=None):
    s, d = x.shape
    tr = _tile(s, 256)
    row = pl.BlockSpec((tr, d), lambda i: (i, 0))
    gsp = pl.BlockSpec((1, d), lambda i: (0, 0))

    def body(*refs):
        refs = list(refs)
        x_ref, g_ref = refs[:2]
        y = _rms(x_ref[...], g_ref[...])
        pos = 2
        if resid is not None:
            y = refs[pos][...] + y
            pos += 1
        if next_gain is None:
            refs[pos][...] = y.astype(refs[pos].dtype)
        else:
            refs[pos + 1][...] = y.astype(refs[pos + 1].dtype)
            refs[pos + 2][...] = _rms(y, refs[pos][...]).astype(BF16)

    args = [x, g.reshape(1, d)] + ([] if resid is None else [resid]) + ([] if next_gain is None else [next_gain.reshape(1, d)])
    in_specs = [row, gsp] + ([] if resid is None else [row]) + ([] if next_gain is None else [gsp])
    first = jax.ShapeDtypeStruct((s, d), out_dtype)
    if next_gain is None:
        out_specs, out_shape = row, first
    else:
        out_specs, out_shape = [row, row], [first, jax.ShapeDtypeStruct((s, d), BF16)]
    return pl.pallas_call(
        body, name=name, grid=(s // tr,), in_specs=in_specs, out_specs=out_specs, out_shape=out_shape,
        compiler_params=_cparams(("parallel",)),
    )(*args)


def _norm_bwd(x, g, dy, *, name, add=None, out_dtype=F32, then=None):
    s, d = x.shape
    tr = _tile(s, 256)
    row = pl.BlockSpec((tr, d), lambda i: (i, 0))
    gsp = pl.BlockSpec((1, d), lambda i: (0, 0))
    n_in = 3 + (add is not None) + (2 if then is not None else 0)

    def body(*refs):
        ins, outs = refs[:n_in], refs[n_in:]
        x_ref, g_ref, dy_ref = ins[:3]
        dx, gterm = _rms_bwd(x_ref[...], g_ref[...], dy_ref[...].astype(F32))
        if add is not None:
            dx = dx + ins[3][...]
        outs[0][...] = dx.astype(outs[0].dtype)
        terms = [(outs[1], gterm)]
        if then is not None:
            dx2, gterm2 = _rms_bwd(ins[-2][...], ins[-1][...], dx)
            outs[2][...] = dx2.astype(BF16)
            terms.append((outs[3], gterm2))

        @pl.when(pl.program_id(0) == 0)
        def _():
            for dg_ref, _ in terms:
                dg_ref[...] = jnp.zeros_like(dg_ref)

        for dg_ref, term in terms:
            dg_ref[...] += jnp.sum(term, axis=0, keepdims=True)

    args = [x, g.reshape(1, d), dy] + ([] if add is None else [add]) + ([] if then is None else [then[0], then[1].reshape(1, d)])
    in_specs = [row, gsp, row] + ([] if add is None else [row]) + ([] if then is None else [row, gsp])
    out_specs = [row, gsp] + ([] if then is None else [row, gsp])
    out_shape = [jax.ShapeDtypeStruct((s, d), out_dtype), jax.ShapeDtypeStruct((1, d), F32)]
    if then is not None:
        out_shape += [jax.ShapeDtypeStruct((s, d), BF16), jax.ShapeDtypeStruct((1, d), F32)]
    return pl.pallas_call(
        body, name=name, grid=(s // tr,), in_specs=in_specs, out_specs=out_specs, out_shape=out_shape,
        compiler_params=_cparams(("arbitrary",)),
    )(*args)


def _loss_head(f, g, resid, target):
    s, d = f.shape
    tr = _tile(s, 256)
    row = pl.BlockSpec((tr, d), lambda i: (i, 0))
    gsp = pl.BlockSpec((1, d), lambda i: (0, 0))
    lsp = pl.BlockSpec((1, LANES), lambda i: (0, 0))

    def body(f_ref, g_ref, r_ref, t_ref, l_ref, dy_ref, df_ref, dg_ref):
        fv, gv = f_ref[...], g_ref[...]
        e = (r_ref[...] + _rms(fv, gv)) - t_ref[...]
        dy = e * (1.0 / d)
        dy_ref[...] = dy
        df, gterm = _rms_bwd(fv, gv, dy)
        df_ref[...] = df.astype(BF16)

        @pl.when(pl.program_id(0) == 0)
        def _():
            l_ref[...] = jnp.zeros_like(l_ref)
            dg_ref[...] = jnp.zeros_like(dg_ref)

        part = 0.5 * jnp.sum(jnp.mean(e * e, axis=-1, keepdims=True), axis=0, keepdims=True)
        l_ref[...] += jnp.broadcast_to(part, (1, LANES))
        dg_ref[...] += jnp.sum(gterm, axis=0, keepdims=True)

    return pl.pallas_call(
        body, name="loss_head", grid=(s // tr,), in_specs=[row, gsp, row, row], out_specs=[lsp, row, row, gsp],
        out_shape=[jax.ShapeDtypeStruct((1, LANES), F32), jax.ShapeDtypeStruct((s, d), F32), jax.ShapeDtypeStruct((s, d), BF16),
                   jax.ShapeDtypeStruct((1, d), F32)],
        compiler_params=_cparams(("arbitrary",)),
    )(f, g.reshape(1, d), resid, target)


def _rope_tables(pos_col):
    s = pos_col.shape[0]
    tr = _tile(s, 512)
    f_mla = ROPE_BASE ** (-jnp.arange(ROPE_DIM // 2, dtype=F32) / (ROPE_DIM // 2))
    f_ret = ROPE_BASE ** (-jnp.arange(HEAD // 2, dtype=F32) / (HEAD // 2))
    fm = jnp.concatenate([jnp.zeros((64,), F32), f_mla, f_mla, jnp.zeros((32,), F32)]).reshape(1, LANES)
    fr = jnp.tile(jnp.concatenate([f_ret, f_ret]), 2).reshape(1, LANES)

    def body(p_ref, fm_ref, fr_ref, cm_ref, sm_ref, cr_ref, sr_ref):
        p = p_ref[...].astype(F32)
        am = p * fm_ref[...]
        ar = p * fr_ref[...]
        cm_ref[...] = jnp.cos(am)
        sm_ref[...] = jnp.sin(am)
        cr_ref[...] = jnp.tile(jnp.cos(ar), (1, 2))
        sr_ref[...] = jnp.tile(jnp.sin(ar), (1, 2))

    return pl.pallas_call(
        body, name="rope_tables", grid=(s // tr,),
        in_specs=[pl.BlockSpec((tr, 1), lambda i: (i, 0)), pl.BlockSpec((1, LANES), lambda i: (0, 0)),
                  pl.BlockSpec((1, LANES), lambda i: (0, 0))],
        out_specs=[pl.BlockSpec((tr, LANES), lambda i: (i, 0))] * 2 + [pl.BlockSpec((tr, 2 * LANES), lambda i: (i, 0))] * 2,
        out_shape=[jax.ShapeDtypeStruct((s, LANES), F32)] * 2 + [jax.ShapeDtypeStruct((s, 2 * LANES), F32)] * 2,
        compiler_params=_cparams(("parallel",)),
    )(pos_col, fm, fr)


def _lane(shape):
    return lax.broadcasted_iota(jnp.int32, shape, len(shape) - 1)


def _rot_mla(z):
    l = _lane(z.shape) % LANES
    n = z.shape[-1]
    return jnp.where(l < 80, -pltpu.roll(z, n - 16, 1), pltpu.roll(z, 16, 1))


def _rot_mla_t(y):
    l = _lane(y.shape) % LANES
    n = y.shape[-1]
    return jnp.where((l >= 64) & (l < 80), pltpu.roll(y, n - 16, 1),
                     jnp.where((l >= 80) & (l < 96), -pltpu.roll(y, 16, 1), 0.0))


def _rot_ret(z):
    l = _lane(z.shape) % HEAD
    n = z.shape[-1]
    return jnp.where(l < 32, -pltpu.roll(z, n - 32, 1), pltpu.roll(z, 32, 1))


def _rot_ret_t(y):
    l = _lane(y.shape) % HEAD
    n = y.shape[-1]
    return jnp.where(l < 32, pltpu.roll(y, n - 32, 1), -pltpu.roll(y, 32, 1))


def _log_sigmoid(x):
    return jnp.minimum(x, 0.0) - jnp.log1p(jnp.exp(-jnp.abs(x)))


def _fox_cum(proj, bias_row):
    s = proj.shape[0]
    fb = _tile(s, GATE_ROWS)
    nb = s // fb

    def body(x_ref, b_ref, cc_ref, cr_ref, carry_ref):
        @pl.when(pl.program_id(0) == 0)
        def _():
            carry_ref[...] = jnp.zeros_like(carry_ref)

        ls = _log_sigmoid(x_ref[...] + b_ref[...])
        r = lax.broadcasted_iota(jnp.int32, (fb, fb), 0)
        c = lax.broadcasted_iota(jnp.int32, (fb, fb), 1)
        tri = (c <= r).astype(F32)
        cum = _dot_exact(tri, ls) + carry_ref[...]
        carry_ref[...] = cum[fb - 1:fb, :]
        cc_ref[...] = cum
        cr_ref[...] = cum.T[0:8, :]

    return pl.pallas_call(
        body, name="fox_cum", grid=(nb,),
        in_specs=[pl.BlockSpec((fb, LANES), lambda i: (i, OFF_MISC)), pl.BlockSpec((1, LANES), lambda i: (0, 0))],
        out_specs=[pl.BlockSpec((fb, LANES), lambda i: (i, 0)), pl.BlockSpec((8, fb), lambda i: (0, i))],
        out_shape=[jax.ShapeDtypeStruct((s, LANES), F32), jax.ShapeDtypeStruct((8, s), F32)],
        scratch_shapes=[pltpu.VMEM((1, LANES), F32)],
        compiler_params=_cparams(("arbitrary",)),
    )(proj, bias_row)


def _fox_gate_bwd(dck, drs, proj, bias_row, dkr):
    s = proj.shape[0]
    fb = _tile(s, GATE_ROWS)
    nb = s // fb

    def body(d_ref, r_ref, x_ref, b_ref, k_ref, o_ref, db_ref, carry_ref):
        @pl.when(pl.program_id(0) == 0)
        def _():
            carry_ref[...] = jnp.zeros_like(carry_ref)
            db_ref[...] = jnp.zeros_like(db_ref)

        rows = jnp.concatenate([d_ref[0], d_ref[1], jnp.zeros((LANES - 16, fb), F32)], axis=0)
        t = rows.T
        l = _lane((fb, LANES))
        r0, r1 = r_ref[0], r_ref[1]
        rsum = jnp.where(l == 0, r0[:, 0:1], jnp.where(l == 1, r0[:, HEAD:HEAD + 1],
                         jnp.where(l == 2, r1[:, 0:1], jnp.where(l == 3, r1[:, HEAD:HEAD + 1], 0.0))))
        dcum = rsum - jnp.where(l < 2, t, pltpu.roll(t, LANES - 6, 1))
        r = lax.broadcasted_iota(jnp.int32, (fb, fb), 0)
        c = lax.broadcasted_iota(jnp.int32, (fb, fb), 1)
        triu = (c >= r).astype(F32)
        rc = _dot_exact(triu, dcum) + carry_ref[...]
        carry_ref[...] = rc[0:1, :]
        f = x_ref[...] + b_ref[...]
        sig_neg = 1.0 / (1.0 + jnp.exp(f))
        df = jnp.where(l < N_HEADS, rc * sig_neg, 0.0)
        db_ref[...] += jnp.sum(df, axis=0, keepdims=True)
        o_ref[...] = (df + k_ref[...]).astype(o_ref.dtype)

    rev = lambda i: nb - 1 - i
    return pl.pallas_call(
        body, name="fox_gate_bwd", grid=(nb,),
        in_specs=[pl.BlockSpec((2, 8, fb), lambda i: (0, 0, rev(i))), pl.BlockSpec((2, fb, LANES), lambda i: (0, rev(i), 0)),
                  pl.BlockSpec((fb, LANES), lambda i: (rev(i), OFF_MISC)),
                  pl.BlockSpec((1, LANES), lambda i: (0, 0)), pl.BlockSpec((fb, LANES), lambda i: (rev(i), 0))],
        out_specs=[pl.BlockSpec((fb, LANES), lambda i: (rev(i), 0)), pl.BlockSpec((1, LANES), lambda i: (0, 0))],
        out_shape=[jax.ShapeDtypeStruct((s, LANES), BF16), jax.ShapeDtypeStruct((1, LANES), F32)],
        scratch_shapes=[pltpu.VMEM((1, LANES), F32)],
        compiler_params=_cparams(("arbitrary",)),
    )(dck, drs, proj, bias_row, dkr)


def _mla_prep(proj, cos_m, sin_m, g_q, g_kv, wq, wk, wv):
    s = proj.shape[0]
    tr = _tile(s, 512)

    def body(cq_ref, ckv_ref, misc_ref, cos_ref, sin_ref, gq_ref, gkv_ref, wq_ref, wk_ref, wv_ref,
             q_ref, k_ref, v_ref, cqn_ref, ckvn_ref):
        cos4 = jnp.tile(cos_ref[...], (1, 4))
        sin4 = jnp.tile(sin_ref[...], (1, 4))
        cqn = _rms(cq_ref[...], gq_ref[...]).astype(BF16)
        ckvn = _rms(ckv_ref[...], gkv_ref[...]).astype(BF16)
        cqn_ref[...] = cqn
        ckvn_ref[...] = ckvn
        zq = _dot(cqn, wq_ref[...])
        q_ref[...] = (zq * cos4 + _rot_mla(zq) * sin4).astype(BF16)
        l = _lane((tr, LANES))
        kr = jnp.where((l >= KR_LANE) & (l < KR_LANE + ROPE_DIM), misc_ref[...], 0.0)
        zk = _dot(ckvn, wk_ref[...]) + jnp.tile(kr, (1, 4))
        k_ref[...] = (zk * cos4 + _rot_mla(zk) * sin4).astype(BF16)
        v_ref[...] = _dot(ckvn, wv_ref[...]).astype(BF16)

    full = lambda a: pl.BlockSpec(a.shape, lambda i: (0, 0))
    rowb = lambda w: pl.BlockSpec((tr, w), lambda i: (i, 0))
    gq2, gkv2 = g_q.reshape(1, Q_RANK), g_kv.reshape(1, KV_RANK)
    return pl.pallas_call(
        body, name="mla_prep", grid=(s // tr,),
        in_specs=[pl.BlockSpec((tr, 256), lambda i: (i, OFF_CQ // 2)), pl.BlockSpec((tr, LANES), lambda i: (i, OFF_CKV)),
                  pl.BlockSpec((tr, LANES), lambda i: (i, OFF_MISC)), rowb(LANES), rowb(LANES),
                  full(gq2), full(gkv2), full(wq), full(wk), full(wv)],
        out_specs=[rowb(512), rowb(512), rowb(512), rowb(256), rowb(128)],
        out_shape=[jax.ShapeDtypeStruct((s, 512), BF16), jax.ShapeDtypeStruct((s, 512), BF16), jax.ShapeDtypeStruct((s, 512), BF16),
                   jax.ShapeDtypeStruct((s, 256), BF16), jax.ShapeDtypeStruct((s, 128), BF16)],
        compiler_params=_cparams(("parallel",)),
    )(proj, proj, proj, cos_m, sin_m, gq2, gkv2, wq, wk, wv)


def _mla_prep_bwd(dq, dk, dv, proj, cqn, ckvn, cos_m, sin_m, g_q, g_kv, wq, wk, wv):
    s = proj.shape[0]
    tr = _tile(s, 512)

    def body(dq_ref, dk_ref, dv_ref, cq_ref, ckv_ref, cqn_ref, ckvn_ref, cos_ref, sin_ref, gq_ref, gkv_ref,
             wq_ref, wk_ref, wv_ref, dcq_ref, dckv_ref, dkr_ref, dwq_ref, dwk_ref, dwv_ref, dgq_ref, dgkv_ref):
        @pl.when(pl.program_id(0) == 0)
        def _():
            for r in (dwq_ref, dwk_ref, dwv_ref, dgq_ref, dgkv_ref):
                r[...] = jnp.zeros_like(r)

        cos4 = jnp.tile(cos_ref[...], (1, 4))
        sin4 = jnp.tile(sin_ref[...], (1, 4))
        dqv = dq_ref[...]
        dzq = dqv * cos4 + _rot_mla_t(dqv * sin4)
        dkv_ = dk_ref[...]
        dzk = dkv_ * cos4 + _rot_mla_t(dkv_ * sin4)
        l = _lane((tr, LANES))
        in_rope = (l >= KR_LANE) & (l < KR_LANE + ROPE_DIM)
        dkr = dzk[:, 0:128] + dzk[:, 128:256] + dzk[:, 256:384] + dzk[:, 384:512]
        dkr_ref[...] = jnp.where(in_rope, dkr, 0.0)
        dzq_b = dzq.astype(BF16)
        dzk_b = dzk.astype(BF16)
        dv_b = dv_ref[...].astype(BF16)
        dcqn = _dot_nt(dzq_b, wq_ref[...])
        dckvn = _dot_nt(dzk_b, wk_ref[...]) + _dot_nt(dv_b, wv_ref[...])
        dwq_ref[...] += _dot_tn(cqn_ref[...], dzq_b)
        dwk_ref[...] += _dot_tn(ckvn_ref[...], dzk_b)
        dwv_ref[...] += _dot_tn(ckvn_ref[...], dv_b)
        dcq, gq_term = _rms_bwd(cq_ref[...], gq_ref[...], dcqn)
        dckv, gkv_term = _rms_bwd(ckv_ref[...], gkv_ref[...], dckvn)
        dcq_ref[...] = dcq.astype(BF16)
        dckv_ref[...] = dckv.astype(BF16)
        dgq_ref[...] += jnp.sum(gq_term, axis=0, keepdims=True)
        dgkv_ref[...] += jnp.sum(gkv_term, axis=0, keepdims=True)

    full = lambda shp: pl.BlockSpec(shp, lambda i: (0, 0))
    rowb = lambda w: pl.BlockSpec((tr, w), lambda i: (i, 0))
    gq2, gkv2 = g_q.reshape(1, Q_RANK), g_kv.reshape(1, KV_RANK)
    return pl.pallas_call(
        body, name="mla_prep_bwd", grid=(s // tr,),
        in_specs=[rowb(512), rowb(512), rowb(512),
                  pl.BlockSpec((tr, 256), lambda i: (i, OFF_CQ // 2)), pl.BlockSpec((tr, LANES), lambda i: (i, OFF_CKV)),
                  rowb(256), rowb(128), rowb(LANES), rowb(LANES), full((1, Q_RANK)), full((1, KV_RANK)),
                  full(wq.shape), full(wk.shape), full(wv.shape)],
        out_specs=[rowb(256), rowb(128), rowb(128), full(wq.shape), full(wk.shape), full(wv.shape),
                   full((1, Q_RANK)), full((1, KV_RANK))],
        out_shape=[jax.ShapeDtypeStruct((s, 256), BF16), jax.ShapeDtypeStruct((s, 128), BF16), jax.ShapeDtypeStruct((s, 128), F32),
                   jax.ShapeDtypeStruct(wq.shape, F32), jax.ShapeDtypeStruct(wk.shape, F32), jax.ShapeDtypeStruct(wv.shape, F32),
                   jax.ShapeDtypeStruct((1, Q_RANK), F32), jax.ShapeDtypeStruct((1, KV_RANK), F32)],
        compiler_params=_cparams(("arbitrary",)),
    )(dq, dk, dv, proj, proj, cqn, ckvn, cos_m, sin_m, gq2, gkv2, wq, wk, wv)


def _ret_prep(proj, cos_r, sin_r):
    s = proj.shape[0]
    tr = _tile(s, 512)

    def body(q_ref, k_ref, cos_ref, sin_ref, qo_ref, ko_ref):
        cos, sin = cos_ref[...], sin_ref[...]
        q, k = q_ref[...], k_ref[...]
        qo_ref[...] = (q * cos + _rot_ret(q) * sin).astype(BF16)
        ko_ref[...] = ((k * cos + _rot_ret(k) * sin) * (HEAD ** -0.5)).astype(BF16)

    rowb = pl.BlockSpec((tr, 256), lambda i: (i, 0))
    return pl.pallas_call(
        body, name="ret_prep", grid=(s // tr,),
        in_specs=[pl.BlockSpec((tr, 256), lambda i: (i, OFF_RQ // 2)), pl.BlockSpec((tr, 256), lambda i: (i, OFF_RK // 2)), rowb, rowb],
        out_specs=[rowb, rowb], out_shape=[jax.ShapeDtypeStruct((s, 256), BF16)] * 2,
        compiler_params=_cparams(("parallel",)),
    )(proj, proj, cos_r, sin_r)


def _ret_prep_bwd(dq, dk, cos_r, sin_r):
    s = dq.shape[0]
    tr = _tile(s, 512)

    def body(dq_ref, dk_ref, cos_ref, sin_ref, qo_ref, ko_ref):
        cos, sin = cos_ref[...], sin_ref[...]
        q, k = dq_ref[...], dk_ref[...] * (HEAD ** -0.5)
        qo_ref[...] = (q * cos + _rot_ret_t(q * sin)).astype(BF16)
        ko_ref[...] = (k * cos + _rot_ret_t(k * sin)).astype(BF16)

    rowb = pl.BlockSpec((tr, 256), lambda i: (i, 0))
    return pl.pallas_call(
        body, name="ret_prep_bwd", grid=(s // tr,), in_specs=[rowb] * 4, out_specs=[rowb, rowb],
        out_shape=[jax.ShapeDtypeStruct((s, 256), BF16)] * 2, compiler_params=_cparams(("parallel",)),
    )(dq, dk, cos_r, sin_r)


_LOG_GAMMA = [float(np.log1p(-np.float32(2.0) ** np.float32(-5.0 - h))) for h in range(N_HEADS)]
_MLA_SCALE = float((HEAD + ROPE_DIM) ** -0.5)
_QK_SCALE = float(HEAD ** -0.5)
KEY_BLOCKS = 4
QB = 512


def _split2(x):
    h = x.astype(BF16)
    return h, (x - h.astype(F32)).astype(BF16)


def _dot2(x, u):
    h, lo = _split2(x)
    return _dot(h, u) + _dot(lo, u)


def _head_pick(block, head, axis):
    idx = lax.broadcasted_iota(jnp.int32, block.shape, axis)
    return jnp.sum(jnp.where(idx == head, block, 0.0), axis=axis, keepdims=True)


def _log_gamma_of(head):
    lg = jnp.float32(_LOG_GAMMA[3])
    for h in (2, 1, 0):
        lg = jnp.where(head == h, jnp.float32(_LOG_GAMMA[h]), lg)
    return lg


def _mixer_specs(mode, s, q_off, k_off, v_off):
    nhb = 2
    bw = 2 * LANES if mode == "mla" else LANES
    nsub = KEY_BLOCKS if (s // TQ) % KEY_BLOCKS == 0 else 1
    q_spec = pl.BlockSpec((QB, bw), lambda p, i: (i, q_off + p))
    k_spec = pl.BlockSpec((s, bw), lambda p, i: (0, k_off + p))
    v_spec = pl.BlockSpec((s, bw), lambda p, i: (0, v_off + p))
    return nhb, N_HEADS // nhb, nsub, q_spec, k_spec, v_spec


def _mixer_geometry(mode, i, nsub):
    w = TQ * nsub
    row = lax.broadcasted_iota(jnp.int32, (QB, w), 0)
    col = lax.broadcasted_iota(jnp.int32, (QB, w), 1)
    nfull = (i * QB) // w
    dist = col - row
    if mode in ("fox", "sb"):
        rel = dist
    else:
        rel = col - (row | (CHUNK - 1))

    def visible(c):
        off = c * w - i * QB
        return (rel + off) < 0 if mode == "sb" else (rel + off) <= 0

    return nfull, dist, visible


class _SideJob:
    def __init__(self, inputs, out_shape, n_sems, sends, recvs):
        self.inputs, self.out_shape, self.n_sems, self.sends, self.recvs = list(inputs), list(out_shape), n_sems, sends, recvs


def _carry_side_job(body, n_in, n_out, side, n_steps):
    if side is None:
        return body
    si, so = len(side.inputs), len(side.out_shape)

    def at(corner):
        ok = pl.program_id(0) == corner[0]
        for d in range(1, len(n_steps)):
            ok = ok & (pl.program_id(d) == corner[d])
        return ok

    def wrapped(*refs):
        ins, s_ins = refs[:n_in], refs[n_in:n_in + si]
        outs, s_outs = refs[n_in + si:n_in + si + n_out], refs[n_in + si + n_out:n_in + si + n_out + so]
        scratch, send, recv = refs[n_in + si + n_out + so:-2], refs[-2], refs[-1]

        @pl.when(at([0] * len(n_steps)))
        def _():
            for cp in side.sends(s_ins, s_outs, send, recv):
                cp.start()

        body(*ins, *outs, *scratch)

        @pl.when(at([n - 1 for n in n_steps]))
        def _():
            for cp in side.recvs(s_ins, s_outs, send, recv):
                cp.wait_recv()
            for cp in side.sends(s_ins, s_outs, send, recv):
                cp.wait_send()

    return wrapped


def _side_specs(side):
    if side is None:
        return [], [], []
    hbm = pl.BlockSpec(memory_space=pl.ANY)
    return ([hbm] * len(side.inputs), [hbm] * len(side.out_shape),
            [pltpu.SemaphoreType.DMA((side.n_sems,)), pltpu.SemaphoreType.DMA((side.n_sems,))])


def _mixer_fwd(mode, qa, q_off, ka, k_off, va, v_off, *, cum_col=None, cum_row=None, side=None):
    s = qa.shape[0]
    nq = s // QB
    nhb, nblk, nsub, q_spec, k_spec, v_spec = _mixer_specs(mode, s, q_off, k_off, v_off)
    w = TQ * nsub
    softmax = mode in ("fox", "mla")

    def body(*refs):
        refs = list(refs)
        q_ref, k_ref, v_ref = refs[:3]
        refs = refs[3:]
        if mode == "fox":
            cc_ref, cr_ref = refs[:2]
            refs = refs[2:]
        o_ref = refs[0]
        st_ref = refs[1]
        p = pl.program_id(0)
        i = pl.program_id(1)
        nfull, dist, visible = _mixer_geometry(mode, i, nsub)
        lane = _lane((1, LANES))
        heads = [nhb * p + hh for hh in range(nhb)]
        wide = mode == "mla"
        q_scale = _QK_SCALE if mode in ("fox", "sb") else 1.0
        cols = [slice(hh * LANES, (hh + 1) * LANES) if wide else slice(None) for hh in range(nhb)]
        if wide:
            qs = [q_ref[:, cols[hh]] for hh in range(nhb)]
        else:
            qf = q_ref[...].astype(F32) * q_scale
            qs = [jnp.where((lane // HEAD) == hh, qf, 0.0).astype(BF16) for hh in range(nhb)]
        if mode == "fox":
            cqs = [_head_pick(cc_ref[...], h, 1) for h in heads]
        if mode == "sb":
            r1 = lax.broadcasted_iota(jnp.int32, (TQ, TQ), 0)
            c1 = lax.broadcasted_iota(jnp.int32, (TQ, TQ), 1)
            u_after = (r1 > c1).astype(BF16)

        def chunk(c):
            return pl.ds(pl.multiple_of(c * w, w), w)

        def scores(c):
            js = chunk(c)
            return tuple(_dot_nt(qs[hh], k_ref[js, cols[hh]]) for hh in range(nhb))

        def head_step(hh, c, js, sc, vj, carry, last):
            if softmax:
                m, l, acc = carry
                if mode == "fox":
                    ck = _head_pick(cr_ref[:, js], heads[hh], 0)
                    sc = sc + (cqs[hh] - ck)
                else:
                    sc = sc * _MLA_SCALE
                if last:
                    sc = jnp.where(visible(c), sc, NEG)
                m_new = jnp.maximum(m, jnp.max(sc, axis=-1, keepdims=True))
                alpha = jnp.exp(m - m_new)
                pr = jnp.exp(sc - m_new)
                l = alpha * l + jnp.sum(pr, axis=-1, keepdims=True)
                acc = alpha * acc + _dot(pr.astype(BF16), vj)
                return m_new, l, acc
            run, acc = carry
            z = sc
            log_beta = jnp.minimum(z, 0.0) - jnp.log(1.0 + jnp.exp(-jnp.abs(z)))
            log_stay = log_beta - z
            if last:
                vis = visible(c)
                log_stay = jnp.where(vis, log_stay, 0.0)
            parts = [None] * nsub
            for b in reversed(range(nsub)):
                ls_b = log_stay[:, b * TQ:(b + 1) * TQ]
                parts[b] = _dot2(ls_b, u_after) + run
                run = run + jnp.sum(ls_b, axis=-1, keepdims=True)
            later = parts[0] if nsub == 1 else jnp.concatenate(parts, axis=1)
            wgt = jnp.exp(log_beta + later)
            if last:
                wgt = jnp.where(vis, wgt, 0.0)
            return run, acc + _dot(wgt.astype(BF16), vj)

        def step(c, c_next, state, last):
            scs, carries = state
            nxt = scores(c_next) if c_next is not None else None
            js = chunk(c)
            return nxt, tuple(head_step(hh, c, js, scs[hh], v_ref[js, cols[hh]], carries[hh], last) for hh in range(nhb))

        zero_acc = jnp.zeros((QB, LANES), F32)
        zero1 = jnp.zeros((QB, 1), F32)
        if softmax:
            init = tuple((jnp.full((QB, 1), NEG, F32), zero1, zero_acc) for _ in range(nhb))
        else:
            init = tuple((zero1, zero_acc) for _ in range(nhb))
        if mode == "sb":
            state = step(nfull, jnp.maximum(nfull - 1, 0), (scores(nfull), init), True)
            _, carries = lax.fori_loop(0, nfull, lambda t, st: step(nfull - 1 - t, jnp.maximum(nfull - 2 - t, 0), st, False), state)
        else:
            state = lax.fori_loop(0, nfull, lambda c, st: step(c, c + 1, st, False), (scores(0), init))
            _, carries = step(nfull, None, state, True)
        if softmax:
            outs = [acc / l for (m, l, acc) in carries]
            stats = [m + jnp.log(l) for (m, l, acc) in carries]
        else:
            outs, stats = [acc for (run, acc) in carries], [run for (run, acc) in carries]
        hm0 = (lane // HEAD) == 0
        pick = lambda a: jnp.where(hm0, a[0], a[1])
        if wide:
            for hh in range(nhb):
                o_ref[:, cols[hh]] = outs[hh]
        else:
            o_ref[...] = pick(outs)
        st_ref[0] = pick(stats)

    in_specs = [q_spec, k_spec, v_spec]
    args = [qa, ka, va]
    if mode == "fox":
        in_specs += [pl.BlockSpec((QB, LANES), lambda p, i: (i, 0)), pl.BlockSpec((8, s), lambda p, i: (0, 0))]
        args += [cum_col, cum_row]
    bw = 2 * LANES if mode == "mla" else LANES
    out_specs = [pl.BlockSpec((QB, bw), lambda p, i: (i, p))]
    out_shape = [jax.ShapeDtypeStruct((s, nblk * bw), F32)]
    out_specs.append(pl.BlockSpec((1, QB, LANES), lambda p, i: (p, i, 0)))
    out_shape.append(jax.ShapeDtypeStruct((nblk, s, LANES), F32))
    side_in, side_out, side_scratch = _side_specs(side)
    res = pl.pallas_call(
        _carry_side_job(body, len(args), len(out_shape), side, (nblk, nq)), name=mode + "_fwd", grid=(nblk, nq),
        in_specs=in_specs + side_in, out_specs=out_specs + side_out,
        out_shape=out_shape + ([] if side is None else side.out_shape), scratch_shapes=side_scratch,
        compiler_params=_cparams(("parallel", "parallel") if side is None else ("arbitrary", "arbitrary")),
    )(*args, *([] if side is None else side.inputs))
    return (res[0], res[1]) if side is None else (res[0], res[1], res[2:])


def _mixer_bwd(mode, qa, q_off, ka, k_off, va, v_off, o, do, *, stat=None, cum_col=None, cum_row=None, side=None):
    s = qa.shape[0]
    nq = s // QB
    nhb, nblk, nsub, q_spec, k_spec, v_spec = _mixer_specs(mode, s, q_off, k_off, v_off)
    w = TQ * nsub
    softmax = mode in ("fox", "mla")

    def body(*refs):
        refs = list(refs)
        q_ref, k_ref, v_ref, o_ref, do_ref = refs[:5]
        refs = refs[5:]
        st_ref = refs[0]
        refs = refs[1:]
        if mode == "fox":
            cc_ref, cr_ref = refs[:2]
            refs = refs[2:]
        dq_ref, dk_ref, dv_ref = refs[:3]
        dck_ref, drs_ref = refs[3:5] if mode == "fox" else (None, None)
        p = pl.program_id(0)
        i = pl.program_id(1)

        @pl.when(i == 0)
        def _():
            dk_ref[...] = jnp.zeros_like(dk_ref)
            dv_ref[...] = jnp.zeros_like(dv_ref)
            if mode == "fox":
                dck_ref[...] = jnp.zeros_like(dck_ref)

        nfull, dist, visible = _mixer_geometry(mode, i, nsub)
        lane = _lane((1, LANES))
        heads = [nhb * p + hh for hh in range(nhb)]
        dov = do_ref[...]
        wide = mode == "mla"
        q_scale = _QK_SCALE if mode in ("fox", "sb") else 1.0
        cols = [slice(hh * LANES, (hh + 1) * LANES) if wide else slice(None) for hh in range(nhb)]
        if wide:
            prod = dov * o_ref[...]
            qs = [q_ref[:, cols[hh]] for hh in range(nhb)]
            dos = [dov[:, cols[hh]].astype(BF16) for hh in range(nhb)]
            deltas = [jnp.sum(prod[:, cols[hh]], axis=-1, keepdims=True) for hh in range(nhb)]
        else:
            qf = q_ref[...].astype(F32) * q_scale
            prod = dov * o_ref[...]
            hms = [(lane // HEAD) == hh for hh in range(nhb)]
            qs = [jnp.where(hm, qf, 0.0).astype(BF16) for hm in hms]
            dos = [jnp.where(hm, dov, 0.0).astype(BF16) for hm in hms]
            deltas = [jnp.sum(jnp.where(hm, prod, 0.0), axis=-1, keepdims=True) for hm in hms]
        st = st_ref[0]
        stats = [st[:, hh * HEAD:hh * HEAD + 1] for hh in range(nhb)]
        if mode == "fox":
            cqs = [_head_pick(cc_ref[...], h, 1) for h in heads]
        if mode == "sb":
            r1 = lax.broadcasted_iota(jnp.int32, (TQ, TQ), 0)
            c1 = lax.broadcasted_iota(jnp.int32, (TQ, TQ), 1)
            u_upto = (r1 <= c1).astype(BF16)
            u_before = (r1 < c1).astype(BF16)

        def chunk(c):
            return pl.ds(pl.multiple_of(c * w, w), w)

        def scores(c):
            js = chunk(c)
            if mode == "sb":
                return tuple((_dot_nt(qs[hh], k_ref[js, cols[hh]]), None) for hh in range(nhb))
            return tuple((_dot_nt(qs[hh], k_ref[js, cols[hh]]), _dot_nt(dos[hh], v_ref[js, cols[hh]])) for hh in range(nhb))

        def emit(hh, js, ds_b, pr_b, dq):
            dk_ref[js, cols[hh]] += _dot_tn(ds_b, qs[hh])
            dv_ref[js, cols[hh]] += _dot_tn(pr_b, dos[hh])
            return dq + _dot(ds_b, k_ref[js, cols[hh]])

        def head_step(hh, c, js, sc_dp, carry, last):
            sc, dp = sc_dp
            if dp is None:
                dp = _dot_nt(dos[hh], v_ref[js, cols[hh]])
            if softmax:
                dq, rsum = carry
                if mode == "fox":
                    ck = _head_pick(cr_ref[:, js], heads[hh], 0)
                    sc = sc + (cqs[hh] - ck)
                else:
                    sc = sc * _MLA_SCALE
                if last:
                    sc = jnp.where(visible(c), sc, NEG)
                pr = jnp.exp(sc - stats[hh])
                ds = pr * (dp - deltas[hh])
                if mode == "fox":
                    dck_ref[0, hh:hh + 1, js] += jnp.sum(ds, axis=0, keepdims=True)
                    rsum = rsum + jnp.sum(ds, axis=-1, keepdims=True)
                if mode == "mla":
                    ds = ds * _MLA_SCALE
                return emit(hh, js, ds.astype(BF16), pr.astype(BF16), dq), rsum
            seen, gsum, dq = carry
            z = sc
            log_beta = jnp.minimum(z, 0.0) - jnp.log(1.0 + jnp.exp(-jnp.abs(z)))
            log_stay = log_beta - z
            if last:
                vis = visible(c)
                log_stay = jnp.where(vis, log_stay, 0.0)
            parts = []
            for b in range(nsub):
                ls_b = log_stay[:, b * TQ:(b + 1) * TQ]
                parts.append((stats[hh] - seen) - _dot2(ls_b, u_upto))
                seen = seen + jnp.sum(ls_b, axis=-1, keepdims=True)
            later = parts[0] if nsub == 1 else jnp.concatenate(parts, axis=1)
            wgt = jnp.exp(log_beta + later)
            if last:
                wgt = jnp.where(vis, wgt, 0.0)
            g = dp * wgt
            parts = []
            for b in range(nsub):
                g_b = g[:, b * TQ:(b + 1) * TQ]
                parts.append(gsum + _dot2(g_b, u_before))
                gsum = gsum + jnp.sum(g_b, axis=-1, keepdims=True)
            before = parts[0] if nsub == 1 else jnp.concatenate(parts, axis=1)
            beta = jnp.exp(log_beta)
            dz = g * (1.0 - beta) - beta * before
            if last:
                dz = jnp.where(vis, dz, 0.0)
            return seen, gsum, emit(hh, js, dz.astype(BF16), wgt.astype(BF16), dq)

        def step(c, c_next, state, last):
            scs, carries = state
            nxt = scores(c_next) if c_next is not None else None
            js = chunk(c)
            return nxt, tuple(head_step(hh, c, js, scs[hh], carries[hh], last) for hh in range(nhb))

        zero_acc = jnp.zeros((QB, LANES), F32)
        zero1 = jnp.zeros((QB, 1), F32)
        if softmax:
            init = tuple((zero_acc, zero1) for _ in range(nhb))
        else:
            init = tuple((zero1, zero1, zero_acc) for _ in range(nhb))
        state = lax.fori_loop(0, nfull, lambda c, st: step(c, c + 1, st, False), (scores(0), init))
        _, carries = step(nfull, None, state, True)
        if softmax:
            dqs = [dq for (dq, rsum) in carries]
        else:
            dqs = [dq for (seen, gsum, dq) in carries]
        hm0 = (lane // HEAD) == 0
        if wide:
            for hh in range(nhb):
                dq_ref[:, cols[hh]] = dqs[hh]
        else:
            dq_ref[...] = jnp.where(hm0, dqs[0], dqs[1]) * q_scale
        if mode == "fox":
            drs_ref[0] = jnp.where(hm0, carries[0][1], carries[1][1])

    bw = 2 * LANES if mode == "mla" else LANES
    pair_blk = pl.BlockSpec((QB, bw), lambda p, i: (i, p))
    full_blk = pl.BlockSpec((s, bw), lambda p, i: (0, p))
    stat_blk = pl.BlockSpec((1, QB, LANES), lambda p, i: (p, i, 0))
    in_specs = [q_spec, k_spec, v_spec, pair_blk, pair_blk]
    args = [qa, ka, va, o, do]
    in_specs.append(stat_blk)
    args.append(stat)
    if mode == "fox":
        in_specs += [pl.BlockSpec((QB, LANES), lambda p, i: (i, 0)), pl.BlockSpec((8, s), lambda p, i: (0, 0))]
        args += [cum_col, cum_row]
    out_specs = [pair_blk, full_blk, full_blk]
    out_shape = [jax.ShapeDtypeStruct((s, nblk * bw), F32)] * 3
    if mode == "fox":
        out_specs += [pl.BlockSpec((1, 8, s), lambda p, i: (p, 0, 0)), stat_blk]
        out_shape += [jax.ShapeDtypeStruct((2, 8, s), F32), jax.ShapeDtypeStruct((2, s, LANES), F32)]
    side_in, side_out, side_scratch = _side_specs(side)
    res = pl.pallas_call(
        _carry_side_job(body, len(args), len(out_shape), side, (nblk, nq)), name=mode + "_bwd", grid=(nblk, nq),
        in_specs=in_specs + side_in, out_specs=out_specs + side_out,
        out_shape=out_shape + ([] if side is None else side.out_shape), scratch_shapes=side_scratch,
        compiler_params=_cparams(("parallel", "arbitrary") if side is None else ("arbitrary", "arbitrary")),
    )(*args, *([] if side is None else side.inputs))
    return res if side is None else (*res[:len(out_shape)], res[len(out_shape):])


def _ret_geometry(p):
    lane = _lane((1, LANES))
    lg_lane = jnp.where(lane < HEAD, _log_gamma_of(2 * p), _log_gamma_of(2 * p + 1))
    a = lax.broadcasted_iota(jnp.int32, (TQ, 1), 0).astype(F32)
    row = lax.broadcasted_iota(jnp.int32, (TQ, TQ), 0)
    col = lax.broadcasted_iota(jnp.int32, (TQ, TQ), 1)
    same_chunk_or_earlier = (col // CHUNK) <= (row // CHUNK)
    gap = jnp.abs(row - col).astype(F32)
    decays = [jnp.where(same_chunk_or_earlier, jnp.exp(_log_gamma_of(2 * p + hh) * gap), 0.0) for hh in range(2)]
    r = lax.broadcasted_iota(jnp.int32, (LANES, LANES), 0)
    c = lax.broadcasted_iota(jnp.int32, (LANES, LANES), 1)
    own_head = (r // HEAD) == (c // HEAD)
    return lane, lg_lane, a, decays, own_head


def _ret_fwd(qa, ka, va, v_off):
    s = qa.shape[0]
    nq = s // TQ

    def body(q_ref, k_ref, v_ref, o_ref, st_ref, state):
        p = pl.program_id(0)

        @pl.when(pl.program_id(1) == 0)
        def _():
            state[...] = jnp.zeros_like(state)

        lane, lg_lane, a, decays, own_head = _ret_geometry(p)
        q = q_ref[...].astype(F32)
        k = k_ref[...]
        v = v_ref[...]
        s_in = state[...]
        st_ref[0, 0] = s_in
        out = _dot((q * jnp.exp(lg_lane * (a + 1.0))).astype(BF16), s_in.astype(BF16))
        for hh in range(2):
            hm = (lane // HEAD) == hh
            qh = jnp.where(hm, q, 0.0).astype(BF16)
            inner = _dot((_dot_nt(qh, k) * decays[hh]).astype(BF16), v)
            out = out + jnp.where(hm, inner, 0.0)
        o_ref[...] = out
        k_tail = (k.astype(F32) * jnp.exp(lg_lane * (TQ - 1.0 - a))).astype(BF16)
        state[...] = jnp.exp(lg_lane * float(TQ)) * s_in + jnp.where(own_head, _dot_tn(k_tail, v), 0.0)

    blk = lambda off: pl.BlockSpec((TQ, LANES), lambda p, i: (i, off + p))
    return pl.pallas_call(
        body, name="ret_fwd", grid=(2, nq), in_specs=[blk(0), blk(0), blk(v_off)],
        out_specs=[blk(0), pl.BlockSpec((1, 1, LANES, LANES), lambda p, i: (p, i, 0, 0))],
        out_shape=[jax.ShapeDtypeStruct((s, 2 * LANES), F32), jax.ShapeDtypeStruct((2, nq, LANES, LANES), F32)],
        scratch_shapes=[pltpu.VMEM((LANES, LANES), F32)],
        compiler_params=_cparams(("parallel", "arbitrary")),
    )(qa, ka, va)


def _ret_bwd(qa, ka, va, v_off, states, do):
    s = qa.shape[0]
    nq = s // TQ

    def body(q_ref, k_ref, v_ref, st_ref, do_ref, dq_ref, dk_ref, dv_ref, dstate):
        p = pl.program_id(0)

        @pl.when(pl.program_id(1) == 0)
        def _():
            dstate[...] = jnp.zeros_like(dstate)

        lane, lg_lane, a, decays, own_head = _ret_geometry(p)
        q = q_ref[...].astype(F32)
        k = k_ref[...]
        kf = k.astype(F32)
        v = v_ref[...]
        dov = do_ref[...]
        s_in = st_ref[0, 0].astype(BF16)
        ds_next = dstate[...]
        ds_b = ds_next.astype(BF16)
        head_decay = jnp.exp(lg_lane * (a + 1.0))
        tail_decay = jnp.exp(lg_lane * (TQ - 1.0 - a))
        k_tail = (kf * tail_decay).astype(BF16)
        dq = _dot_nt(dov.astype(BF16), s_in) * head_decay
        dk = _dot_nt(v, ds_b) * tail_decay
        dv = _dot(k_tail, ds_b)
        for hh in range(2):
            hm = (lane // HEAD) == hh
            qh = jnp.where(hm, q, 0.0).astype(BF16)
            doh = jnp.where(hm, dov, 0.0).astype(BF16)
            att = (_dot_nt(qh, k) * decays[hh]).astype(BF16)
            datt = (_dot_nt(doh, v) * decays[hh]).astype(BF16)
            dv = dv + _dot_tn(att, doh)
            dk = dk + _dot_tn(datt, qh)
            dq = dq + jnp.where(hm, _dot(datt, k), 0.0)
        dq_ref[...] = dq
        dk_ref[...] = dk
        dv_ref[...] = dv
        q_head = (q * head_decay).astype(BF16)
        dstate[...] = jnp.exp(lg_lane * float(TQ)) * ds_next + jnp.where(own_head, _dot_tn(q_head, dov.astype(BF16)), 0.0)

    blk = lambda off: pl.BlockSpec((TQ, LANES), lambda p, i: (nq - 1 - i, off + p))
    return pl.pallas_call(
        body, name="ret_bwd", grid=(2, nq),
        in_specs=[blk(0), blk(0), blk(v_off), pl.BlockSpec((1, 1, LANES, LANES), lambda p, i: (p, nq - 1 - i, 0, 0)), blk(0)],
        out_specs=[blk(0)] * 3, out_shape=[jax.ShapeDtypeStruct((s, 2 * LANES), F32)] * 3,
        scratch_shapes=[pltpu.VMEM((LANES, LANES), F32)],
        compiler_params=_cparams(("parallel", "arbitrary")),
    )(qa, ka, va, states, do)


def _seg_mean_matrix():
    r = lax.broadcasted_iota(jnp.int32, (GROUP, GROUP), 0)
    c = lax.broadcasted_iota(jnp.int32, (GROUP, GROUP), 1)
    return jnp.where((r // HEAD) == (c // HEAD), 1.0 / HEAD, 0.0).astype(BF16)


def _seg_mean(x, seg):
    h = x.astype(BF16)
    r = x - h.astype(F32)
    m = r.astype(BF16)
    lo = (r - m.astype(F32)).astype(BF16)
    return _dot(h, seg) + _dot(m, seg) + _dot(lo, seg)


def _sigmoid(x):
    return 1.0 / (1.0 + jnp.exp(-x))


def _mix_post(oa, ob, oc, od, proj, g):
    s = oa.shape[0]
    tr = _tile(s, 256)

    def body(a_ref, b_ref, c_ref, d_ref, rg_ref, g_ref, o_ref):
        gv = g_ref[...]
        o_ref[:, 0:GROUP] = _rms(a_ref[...], gv[:, 0:GROUP]).astype(BF16)
        o_ref[:, GROUP:2 * GROUP] = _rms(b_ref[...], gv[:, GROUP:2 * GROUP]).astype(BF16)
        seg = _seg_mean_matrix()
        c = c_ref[...]
        cen = c - _seg_mean(c, seg)
        n = cen * lax.rsqrt(_seg_mean(cen * cen, seg) + EPS)
        rg = rg_ref[...]
        o_ref[:, 2 * GROUP:3 * GROUP] = (n * gv[:, 2 * GROUP:3 * GROUP] * (rg * _sigmoid(rg))).astype(BF16)
        o_ref[:, 3 * GROUP:] = _rms(d_ref[...], gv[:, 3 * GROUP:]).astype(BF16)

    blk = pl.BlockSpec((tr, GROUP), lambda i: (i, 0))
    return pl.pallas_call(
        body, name="mix_post", grid=(s // tr,),
        in_specs=[blk] * 4 + [pl.BlockSpec((tr, GROUP), lambda i: (i, OFF_RG // 2)), pl.BlockSpec((1, D_MODEL), lambda i: (0, 0))],
        out_specs=pl.BlockSpec((tr, D_MODEL), lambda i: (i, 0)), out_shape=jax.ShapeDtypeStruct((s, D_MODEL), BF16),
        compiler_params=_cparams(("parallel",)),
    )(oa, ob, oc, od, proj, g.reshape(1, D_MODEL))


def _mix_post_bwd(dmixed, oa, ob, oc, od, proj, g):
    s = oa.shape[0]
    tr = _tile(s, 256)

    def body(dm_ref, a_ref, b_ref, c_ref, d_ref, rg_ref, g_ref, da_ref, db_ref, dc_ref, dd_ref, drg_ref, dg_ref):
        @pl.when(pl.program_id(0) == 0)
        def _():
            dg_ref[...] = jnp.zeros_like(dg_ref)

        gv = g_ref[...]
        dm = dm_ref[...]
        for k, (x_ref, dx_ref) in enumerate(((a_ref, da_ref), (b_ref, db_ref), (None, None), (d_ref, dd_ref))):
            if x_ref is None:
                continue
            cols = slice(k * GROUP, (k + 1) * GROUP)
            dx, gterm = _rms_bwd(x_ref[...], gv[:, cols], dm[:, cols])
            dx_ref[...] = dx
            dg_ref[:, cols] += jnp.sum(gterm, axis=0, keepdims=True)
        cols = slice(2 * GROUP, 3 * GROUP)
        seg = _seg_mean_matrix()
        c = c_ref[...]
        cen = c - _seg_mean(c, seg)
        rstd = lax.rsqrt(_seg_mean(cen * cen, seg) + EPS)
        n = cen * rstd
        rg = rg_ref[...]
        sg = _sigmoid(rg)
        gate = rg * sg
        dy = dm[:, cols]
        gc = gv[:, cols]
        dn = dy * gc * gate
        dg_ref[:, cols] += jnp.sum(dy * n * gate, axis=0, keepdims=True)
        drg_ref[...] = (dy * n * gc * (sg * (1.0 + rg * (1.0 - sg)))).astype(BF16)
        dc_ref[...] = rstd * (dn - _seg_mean(dn, seg) - n * _seg_mean(dn * n, seg))

    blk = pl.BlockSpec((tr, GROUP), lambda i: (i, 0))
    gsp = pl.BlockSpec((1, D_MODEL), lambda i: (0, 0))
    return pl.pallas_call(
        body, name="mix_post_bwd", grid=(s // tr,),
        in_specs=[pl.BlockSpec((tr, D_MODEL), lambda i: (i, 0))] + [blk] * 4 + [pl.BlockSpec((tr, GROUP), lambda i: (i, OFF_RG // 2)), gsp],
        out_specs=[blk] * 5 + [gsp],
        out_shape=[jax.ShapeDtypeStruct((s, GROUP), F32)] * 4 + [jax.ShapeDtypeStruct((s, GROUP), BF16), jax.ShapeDtypeStruct((1, D_MODEL), F32)],
        compiler_params=_cparams(("arbitrary",)),
    )(dmixed, oa, ob, oc, od, proj, g.reshape(1, D_MODEL))


def _pack_w_in(w):
    z = lambda n: jnp.zeros((w.shape[0], n), w.dtype)
    misc = jnp.concatenate([w[:, 768:772], z(KR_LANE - N_HEADS), w[:, 1156:1188], z(LANES - KR_LANE - ROPE_DIM)], axis=1)
    return jnp.concatenate([w[:, 0:768], w[:, 772:1028], w[:, 1188:2980], w[:, 1028:1156], misc], axis=1)


def _unpack_dw_in(d):
    m = OFF_MISC * LANES
    return jnp.concatenate([d[:, 0:768], d[:, m:m + N_HEADS], d[:, 768:1024], d[:, OFF_CKV * LANES:m],
                            d[:, m + KR_LANE:m + KR_LANE + ROPE_DIM], d[:, 1024:OFF_CKV * LANES]], axis=1)


def _pack_w_q(w):
    return jnp.pad(w.reshape(Q_RANK, N_HEADS, HEAD + ROPE_DIM), ((0, 0), (0, 0), (0, LANES - HEAD - ROPE_DIM))).reshape(Q_RANK, 4 * LANES)


def _unpack_dw_q(d):
    return d.reshape(Q_RANK, N_HEADS, LANES)[:, :, :HEAD + ROPE_DIM].reshape(Q_RANK, N_HEADS * (HEAD + ROPE_DIM))


def _pack_w_kv(w):
    w4 = w.reshape(KV_RANK, N_HEADS, 2 * HEAD)
    widen = lambda a: jnp.pad(a, ((0, 0), (0, 0), (0, LANES - HEAD))).reshape(KV_RANK, N_HEADS * LANES)
    return widen(w4[:, :, :HEAD]), widen(w4[:, :, HEAD:])


def _unpack_dw_kv(dk, dv):
    narrow = lambda a: a.reshape(KV_RANK, N_HEADS, LANES)[:, :, :HEAD]
    return jnp.concatenate([narrow(dk), narrow(dv)], axis=2).reshape(KV_RANK, 2 * N_HEADS * HEAD)


def _narrow_heads(a):
    return a.reshape(a.shape[0], N_HEADS, LANES)[:, :, :HEAD].reshape(a.shape[0], N_HEADS * HEAD)


def _widen_heads(a):
    return jnp.pad(a.reshape(a.shape[0], N_HEADS, HEAD), ((0, 0), (0, 0), (0, LANES - HEAD))).reshape(a.shape[0], N_HEADS * LANES)


def _layer_fwd(x, lw, tabs, tag, side=None, fox_side=None, late_weights=None, h1=None, next_gain=None):
    cos_m, sin_m, cos_r, sin_r = tabs
    if h1 is None:
        h1 = _norm_fwd(x, lw["g_mix_pre"], name=tag + "pre_norm")
    proj, projb = _matmul(h1, lw["w_in"], name=tag + "in_proj", also_bf16=True)
    bias_row = jnp.pad(lw["b_forget"], (FF_LANE, LANES - N_HEADS - FF_LANE)).reshape(1, LANES)
    cum_col, cum_row = _fox_cum(proj, bias_row)
    oa, lse_a, *fox_carried = _mixer_fwd("fox", projb, OFF_FQ, projb, OFF_FK, projb, OFF_FV, cum_col=cum_col, cum_row=cum_row,
                                         side=fox_side)
    if late_weights is not None:
        lw = {**lw, **late_weights(fox_carried[0])}
    qm, km, vm, cqn, ckvn = _mla_prep(proj, cos_m, sin_m, lw["g_q_lora"], lw["g_kv_lora"], lw["wq"], lw["wk"], lw["wv"])
    ob_wide, lse_b = _mixer_fwd("mla", qm, 0, km, 0, vm, 0)
    ob = _narrow_heads(ob_wide)
    qr, kr = _ret_prep(proj, cos_r, sin_r)
    oc, ret_states = _ret_fwd(qr, kr, projb, OFF_RV)
    od, tot_d, *carried = _mixer_fwd("sb", projb, OFF_SQ, projb, OFF_SK, projb, OFF_SV, side=side)
    mixed = _mix_post(oa, ob, oc, od, proj, lw["g_mix_out"])
    mix = _matmul(mixed, lw["w_out"], name=tag + "out_proj")
    x1, h2 = _norm_fwd(mix, lw["g_mix_post"], name=tag + "mix_post_norm", resid=x, out_dtype=F32, next_gain=lw["g_ffn_pre"])
    u = _matmul(h2, lw["w_ffn_up"], name=tag + "ffn_up", relu2=True, out_dtype=BF16, col_blocks=True)
    f = _matmul(u, lw["w_ffn_down"], name=tag + "ffn_down")
    x2, h_next = None, None
    if next_gain is not None:
        x2, h_next = _norm_fwd(f, lw["g_ffn_post"], name=tag + "ffn_post_norm", resid=x1, out_dtype=F32, next_gain=next_gain)
    saved = dict(x=x, h1=h1, proj=proj, projb=projb, bias_row=bias_row, cum_col=cum_col, cum_row=cum_row, oa=oa, lse_a=lse_a,
                 qm=qm, km=km, vm=vm, cqn=cqn, ckvn=ckvn, ob=ob, ob_wide=ob_wide, lse_b=lse_b, qr=qr, kr=kr, ret_states=ret_states, oc=oc, od=od, tot_d=tot_d, mixed=mixed,
                 mix=mix, x1=x1, h2=h2, u=u, f=f)
    return x2, saved, lw, (carried[0] if carried else None), h_next


def _layer_bwd(dx2, lw, sv, tabs, tag, side=None, ffn_side=None, fox_side=None, post_given=None, then_prev=None):
    cos_m, sin_m, cos_r, sin_r = tabs
    g = {}
    if post_given is None:
        df, g["g_ffn_post"] = _norm_bwd(sv["f"], lw["g_ffn_post"], dx2, name=tag + "ffn_post_norm_bwd", out_dtype=BF16)
    else:
        df, g["g_ffn_post"] = post_given
    du_pre = _matmul(df, lw["w_ffn_down"], name=tag + "ffn_down_dx", tb=True, out_dtype=BF16, relu2_of=sv["u"], side=ffn_side)
    ffn_carried = None
    if ffn_side is not None:
        du_pre, ffn_carried = du_pre
    g["w_ffn_down"] = _matmul(sv["u"], df, name=tag + "ffn_down_dw", ta=True)
    dh2 = _matmul(du_pre, lw["w_ffn_up"], name=tag + "ffn_up_dx", tb=True, col_blocks=True)
    g["w_ffn_up"] = _matmul(sv["h2"], du_pre, name=tag + "ffn_up_dw", ta=True, col_blocks=True)
    dx1, g["g_ffn_pre"], dmix, g["g_mix_post"] = _norm_bwd(sv["x1"], lw["g_ffn_pre"], dh2, name=tag + "ffn_pre_norm_bwd", add=dx2,
                                                           then=(sv["mix"], lw["g_mix_post"]))
    dmixed = _matmul(dmix, lw["w_out"], name=tag + "out_proj_dx", tb=True)
    g["w_out"] = _matmul(sv["mixed"], dmix, name=tag + "out_proj_dw", ta=True)
    proj, projb = sv["proj"], sv["projb"]
    doa, dob, doc, dod, drg, g["g_mix_out"] = _mix_post_bwd(dmixed, sv["oa"], sv["ob"], sv["oc"], sv["od"], proj, lw["g_mix_out"])
    dfq, dfk, dfv, dck, drs, *fox_carried = _mixer_bwd(
        "fox", projb, OFF_FQ, projb, OFF_FK, projb, OFF_FV, sv["oa"], doa, stat=sv["lse_a"], cum_col=sv["cum_col"],
        cum_row=sv["cum_row"], side=None if fox_side is None else fox_side(g))
    dqm, dkm, dvm = _mixer_bwd("mla", sv["qm"], 0, sv["km"], 0, sv["vm"], 0, sv["ob_wide"], _widen_heads(dob), stat=sv["lse_b"])
    dcq, dckv, dkr, dwq, dwk, dwv, g["g_q_lora"], g["g_kv_lora"] = _mla_prep_bwd(
        dqm, dkm, dvm, proj, sv["cqn"], sv["ckvn"], cos_m, sin_m, lw["g_q_lora"], lw["g_kv_lora"], lw["wq"], lw["wk"], lw["wv"])
    dqr, dkr_ret, drv = _ret_bwd(sv["qr"], sv["kr"], projb, OFF_RV, sv["ret_states"], doc)
    drq, drk = _ret_prep_bwd(dqr, dkr_ret, cos_r, sin_r)
    if callable(side):
        side = side(g, ffn_carried, fox_carried[0] if fox_carried else None)
    dsq, dsk, dsv, *carried = _mixer_bwd("sb", projb, OFF_SQ, projb, OFF_SK, projb, OFF_SV, sv["od"], dod, stat=sv["tot_d"], side=side)
    dmisc, db_row = _fox_gate_bwd(dck, drs, proj, sv["bias_row"], dkr)
    b = lambda a: a.astype(BF16)
    dproj = jnp.concatenate([b(dfq), b(dfk), b(dfv), dcq, drq, drk, b(drv), drg, b(dsq), b(dsk), b(dsv), dckv, dmisc], axis=1)
    dh1 = _matmul(dproj, lw["w_in"], name=tag + "in_proj_dx", tb=True)
    g["w_in"] = _matmul(sv["h1"], dproj, name=tag + "in_proj_dw", ta=True)
    dx, g["g_mix_pre"], *prev_post = _norm_bwd(sv["x"], lw["g_mix_pre"], dh1, name=tag + "pre_norm_bwd", add=dx1, then=then_prev)
    g["b_forget"] = db_row[0, FF_LANE:FF_LANE + N_HEADS]
    g["wq"], g["wk"], g["wv"] = dwq, dwk, dwv
    return dx, g, (carried[0] if carried else None), (tuple(prev_post) if prev_post else None)


def _local_step(x, positions, layers, target):
    s = x.shape[0]
    tabs = _rope_tables(positions.reshape(s, 1))
    saved, h1 = [], None
    for li, lw in enumerate(layers):
        nxt = layers[li + 1]["g_mix_pre"] if li + 1 < len(layers) else None
        x, sv, _, _, h1 = _layer_fwd(x, lw, tabs, "l%d_" % li, h1=h1, next_gain=nxt)
        saved.append(sv)
    loss_row, dx, df, dg = _loss_head(saved[-1]["f"], layers[-1]["g_ffn_post"], saved[-1]["x1"], target)
    grads, post = [None] * len(layers), (df, dg)
    for li in reversed(range(len(layers))):
        prev = (saved[li - 1]["f"], layers[li - 1]["g_ffn_post"]) if li > 0 else None
        dx, grads[li], _, post = _layer_bwd(dx, layers[li], saved[li], tabs, "l%d_" % li, post_given=post, then_prev=prev)
    return loss_row[0, 0], dx, grads


def _adamw(w, g, m, v, *, name):
    d, r, c = w.shape
    tr = 256 if r % 256 == 0 else r
    blk = pl.BlockSpec((None, tr, c), lambda l, i: (l, i, 0))
    c1 = 1.0 - ADAM_B1 ** ADAM_STEP
    c2 = 1.0 - ADAM_B2 ** ADAM_STEP

    def body(w_ref, g_ref, m_ref, v_ref, d_ref, mo_ref, vo_ref):
        gv = g_ref[...]
        mn = ADAM_B1 * m_ref[...] + (1.0 - ADAM_B1) * gv
        vn = ADAM_B2 * v_ref[...] + (1.0 - ADAM_B2) * jnp.square(gv)
        mo_ref[...] = mn
        vo_ref[...] = vn
        d_ref[...] = -ADAM_LR * ((mn / c1) / (jnp.sqrt(vn / c2) + ADAM_EPS) + ADAM_WD * w_ref[...])

    return pl.pallas_call(
        body, name=name, grid=(d, r // tr), in_specs=[blk] * 4, out_specs=[blk] * 3,
        out_shape=[jax.ShapeDtypeStruct((d, r, c), F32)] * 3, compiler_params=_cparams(("parallel", "parallel")),
    )(w, g, m, v)


def _adamw_lead(w, g, m, v, *, name, steps):
    a, b, c = w.shape
    blk = pl.BlockSpec((a // steps, b, c), lambda i: (i, 0, 0))
    c1 = 1.0 - ADAM_B1 ** ADAM_STEP
    c2 = 1.0 - ADAM_B2 ** ADAM_STEP

    def body(w_ref, g_ref, m_ref, v_ref, d_ref, mo_ref, vo_ref):
        gv = g_ref[...]
        mn = ADAM_B1 * m_ref[...] + (1.0 - ADAM_B1) * gv
        vn = ADAM_B2 * v_ref[...] + (1.0 - ADAM_B2) * jnp.square(gv)
        mo_ref[...] = mn
        vo_ref[...] = vn
        d_ref[...] = -ADAM_LR * ((mn / c1) / (jnp.sqrt(vn / c2) + ADAM_EPS) + ADAM_WD * w_ref[...])

    return pl.pallas_call(
        body, name=name, grid=(steps,), in_specs=[blk] * 4, out_specs=[blk] * 3,
        out_shape=[jax.ShapeDtypeStruct((a, b, c), F32)] * 3, compiler_params=_cparams(("parallel",)),
    )(w, g, m, v)


SC_TILES = 32
SC_ROWS = 8


def _adamw_sparsecore(ws, gs, ms, vs, *, name):
    n = len(ws)
    c = ws[0].shape[2]
    c1 = 1.0 - ADAM_B1 ** ADAM_STEP
    c2 = 1.0 - ADAM_B2 ** ADAM_STEP
    pieces = [(t, l, r0) for t in range(n) for l in range(ws[t].shape[0]) for r0 in range(0, ws[t].shape[1] // SC_TILES, SC_ROWS)]

    def body(*refs):
        ins, outs = refs[:4 * n], refs[4 * n:7 * n]
        bufs, sem_in, sem_out = refs[7 * n:7 * n + 8], refs[7 * n + 8], refs[7 * n + 9]
        tile = lax.axis_index("subcore") * 2 + lax.axis_index("core")

        def window(k):
            t, l, r0 = pieces[k]
            return t, (l, pl.ds(tile * (ws[t].shape[1] // SC_TILES) + r0, SC_ROWS))

        def loads(k):
            t, at = window(k)
            return [pltpu.make_async_copy(ins[j * n + t].at[at], bufs[4 * (k % 2) + j], sem_in.at[k % 2]) for j in range(4)]

        def stores(k):
            t, at = window(k)
            return [pltpu.make_async_copy(bufs[4 * (k % 2) + j], outs[(j - 1) * n + t].at[at], sem_out.at[k % 2]) for j in (1, 2, 3)]

        def update(k):
            gb, wb, mb, vb = bufs[4 * (k % 2):4 * (k % 2) + 4]

            def adam(gv, wv, mv, vv):
                mn = ADAM_B1 * mv + (1.0 - ADAM_B1) * gv
                vn = ADAM_B2 * vv + (1.0 - ADAM_B2) * (gv * gv)
                return -ADAM_LR * ((mn / c1) / (jnp.sqrt(vn / c2) + ADAM_EPS) + ADAM_WD * wv), mn, vn

            @pl.loop(0, SC_ROWS)
            def _(rr):
                last = (rr, pl.ds(c - 16, 16))
                if c % 16:
                    end = adam(gb[last], wb[last], mb[last], vb[last])

                @pl.loop(0, c // 16 * 16, step=16)
                def _(i):
                    s = (rr, pl.ds(i, 16))
                    wb[s], mb[s], vb[s] = adam(gb[s], wb[s], mb[s], vb[s])

                if c % 16:
                    wb[last], mb[last], vb[last] = end

        for cp in loads(0):
            cp.start()
        for k in range(len(pieces)):
            if k + 1 < len(pieces):
                if k >= 1:
                    for cp in stores(k - 1):
                        cp.wait()
                for cp in loads(k + 1):
                    cp.start()
            for cp in loads(k):
                cp.wait()
            update(k)
            for cp in stores(k):
                cp.start()
        for k in range(max(len(pieces) - 2, 0), len(pieces)):
            for cp in stores(k):
                cp.wait()

    out = pl.kernel(
        body, name=name, out_type=[jax.ShapeDtypeStruct(t.shape, F32) for t in ws] * 3,
        mesh=plsc.VectorSubcoreMesh(core_axis_name="core", subcore_axis_name="subcore"),
        scratch_types=[pltpu.VMEM((SC_ROWS, c), F32)] * 8 + [pltpu.SemaphoreType.DMA((2,)), pltpu.SemaphoreType.DMA((2,))],
    )(*gs, *ws, *ms, *vs)
    return out[:n], out[n:2 * n], out[2 * n:]


BIG = ("w_in", "w_q_up", "w_kv_up", "w_out", "w_ffn_up", "w_ffn_down")
SMALL = ("g_mix_pre", "b_forget", "g_q_lora", "g_kv_lora", "g_mix_out", "g_mix_post", "g_ffn_pre", "g_ffn_post")
N_CHIPS = 4
ANY = pl.BlockSpec(memory_space=pl.ANY)


def _mesh_pos():
    return lax.axis_index("x"), lax.axis_index("y"), lax.axis_index("c")


def _other_chips(x, y):
    return [(1 - x, y), (x, 1 - y), (1 - x, 1 - y)]


def _rows_half(ref, half):
    h = ref.shape[-2] // 2
    return ref.at[(slice(None),) * (len(ref.shape) - 2) + (pl.ds(half * h, h), slice(None))]


def _remote(src, dst, send_sem, recv_sem, device):
    return pltpu.make_async_remote_copy(src_ref=src, dst_ref=dst, send_sem=send_sem, recv_sem=recv_sem, device_id=device,
                                        device_id_type=MESH)


def _comm_call(body, name, args, out_shape, n_sems):
    return pl.pallas_call(
        body, name=name, in_specs=[ANY] * len(args), out_specs=[ANY] * len(out_shape), out_shape=out_shape,
        scratch_shapes=[pltpu.SemaphoreType.DMA((n_sems,)), pltpu.SemaphoreType.DMA((n_sems,))],
        compiler_params=pltpu.CompilerParams(has_side_effects=True),
    )(*args)


def _run_side_job(side, name):
    si = len(side.inputs)

    def body(*refs):
        args = (refs[:si], refs[si:-2], refs[-2], refs[-1])
        sends = side.sends(*args)
        for cp in sends:
            cp.start()
        for cp in side.recvs(*args):
            cp.wait_recv()
        for cp in sends:
            cp.wait_send()

    return _comm_call(body, name, side.inputs, side.out_shape, side.n_sems)


def _gather_job(shards):
    n = len(shards)

    def copies(own_block, ins, outs, send_sems, recv_sems):
        x, y, c = _mesh_pos()
        return [_remote(_rows_half(ins[t], c), _rows_half(outs[t].at[2 * x + y if own_block else 2 * px + py], c),
                        send_sems.at[3 * t + j], recv_sems.at[3 * t + j], (px, py, c))
                for t in range(n) for j, (px, py) in enumerate(_other_chips(x, y))]

    return _SideJob(shards, [jax.ShapeDtypeStruct((N_CHIPS,) + a.shape, a.dtype) for a in shards], 3 * n,
                    functools.partial(copies, True), functools.partial(copies, False))


def _forward_halves(gathered):
    n = len(gathered)

    def body(*refs):
        bufs, send_sems, recv_sems = refs[n:2 * n], refs[-2], refs[-1]
        x, y, c = _mesh_pos()

        def d2d(t, j, block, half):
            region = _rows_half(bufs[t].at[block], half)
            return _remote(region, region, send_sems.at[3 * t + j], recv_sems.at[3 * t + j], (x, y, 1 - c))

        peers = list(enumerate(_other_chips(x, y)))
        sends = [d2d(t, j, 2 * px + py, c) for t in range(n) for j, (px, py) in peers]
        for cp in sends:
            cp.start()
        for t in range(n):
            for j, (px, py) in peers:
                d2d(t, j, 2 * px + py, 1 - c).wait_recv()
        for cp in sends:
            cp.wait_send()

    return pl.pallas_call(
        body, name="gather_forward", in_specs=[ANY] * n, out_specs=[ANY] * n,
        out_shape=[jax.ShapeDtypeStruct(g.shape, g.dtype) for g in gathered], input_output_aliases={t: t for t in range(n)},
        scratch_shapes=[pltpu.SemaphoreType.DMA((3 * n,)), pltpu.SemaphoreType.DMA((3 * n,))],
        compiler_params=pltpu.CompilerParams(has_side_effects=True),
    )(*gathered)


def _exchange_halves_job(gs):
    n = len(gs)

    def copies(ins, outs, send_sems, recv_sems):
        x, y, c = _mesh_pos()
        return [_remote(_rows_half(ins[t], 1 - c), outs[t], send_sems.at[t], recv_sems.at[t], (x, y, 1 - c)) for t in range(n)]

    out_shape = [jax.ShapeDtypeStruct(g.shape[:2] + (g.shape[2] // 2, g.shape[3]), g.dtype) for g in gs]
    return _SideJob(gs, out_shape, n, copies, copies)


def _pair_add(g, r, c_idx, *, name):
    nb, d, rows, cols = g.shape
    h = rows // 2
    tr = min(h, 512)
    nt = h // tr

    def body(c_ref, g_ref, r_ref, p_ref, pb_ref):
        s = g_ref[...] + r_ref[...]
        p_ref[...] = s
        pb_ref[...] = s.astype(BF16)

    blk = pl.BlockSpec((1, 1, tr, cols), lambda k, l, i, c_ref: (k, l, i, 0))
    return pl.pallas_call(
        body, name=name,
        grid_spec=pltpu.PrefetchScalarGridSpec(
            num_scalar_prefetch=1, grid=(nb, d, nt),
            in_specs=[pl.BlockSpec((1, 1, tr, cols), lambda k, l, i, c_ref: (k, l, c_ref[0] * nt + i, 0)), blk],
            out_specs=[blk, blk]),
        out_shape=[jax.ShapeDtypeStruct((nb, d, h, cols), F32), jax.ShapeDtypeStruct((nb, d, h, cols), BF16)],
        compiler_params=_cparams(("parallel", "parallel", "parallel")),
    )(c_idx, g, r)


def _exchange_chips_job(pbs):
    n = len(pbs)

    def copies(ins, outs, send_sems, recv_sems):
        x, y, c = _mesh_pos()
        return [_remote(ins[t].at[2 * px + py], outs[t].at[j], send_sems.at[3 * t + j], recv_sems.at[3 * t + j], (px, py, c))
                for t in range(n) for j, (px, py) in enumerate(_other_chips(x, y))]

    return _SideJob(pbs, [jax.ShapeDtypeStruct((3,) + p.shape[1:], p.dtype) for p in pbs], 3 * n, copies, copies)


def _chip_add(p, r, k_idx, *, name):
    _, d, h, cols = p.shape
    tr = min(h, 512)
    nt = h // tr

    def body(k_ref, p_ref, r_ref, o_ref):
        o_ref[0] = ((p_ref[0, 0] + r_ref[0, 0].astype(F32)) + r_ref[1, 0].astype(F32)) + r_ref[2, 0].astype(F32)

    return pl.pallas_call(
        body, name=name,
        grid_spec=pltpu.PrefetchScalarGridSpec(
            num_scalar_prefetch=1, grid=(d, nt),
            in_specs=[pl.BlockSpec((1, 1, tr, cols), lambda l, i, k_ref: (k_ref[0], l, i, 0)),
                      pl.BlockSpec((3, 1, tr, cols), lambda l, i, k_ref: (0, l, i, 0))],
            out_specs=pl.BlockSpec((1, tr, cols), lambda l, i, k_ref: (l, i, 0))),
        out_shape=jax.ShapeDtypeStruct((d, h, cols), F32), compiler_params=_cparams(("parallel", "parallel")),
    )(k_idx, p, r)


def _share_halves(qs):
    n = len(qs)

    def body(*refs):
        ins, outs, send_sems, recv_sems = refs[:n], refs[n:2 * n], refs[2 * n], refs[2 * n + 1]
        x, y, c = _mesh_pos()
        cps = [_remote(ins[t], outs[t], send_sems.at[t], recv_sems.at[t], (x, y, 1 - c)) for t in range(n)]
        for cp in cps:
            cp.start()
        for cp in cps:
            cp.wait_recv()
        for cp in cps:
            cp.wait_send()

    return _comm_call(body, "grad_pair_share", qs, [jax.ShapeDtypeStruct(q.shape, q.dtype) for q in qs], n)


def _all_reduce_small(v):
    r, cols = v.shape
    n_dev = 8

    def body(v_ref, o_ref, buf, send_sems, recv_sems):
        x, y, c = _mesh_pos()
        me = 4 * x + 2 * y + c
        buf[me] = v_ref[...]

        def peer(j):
            return (1 - x if j & 4 else x, 1 - y if j & 2 else y, 1 - c if j & 1 else c)

        def copy(j, slot):
            return pltpu.make_async_remote_copy(src_ref=v_ref, dst_ref=buf.at[slot], send_sem=send_sems.at[j - 1],
                                                recv_sem=recv_sems.at[j - 1], device_id=peer(j), device_id_type=MESH)

        sends = [copy(j, me) for j in range(1, n_dev)]
        for cp in sends:
            cp.start()
        for j in range(1, n_dev):
            px, py, pc = peer(j)
            copy(j, 4 * px + 2 * py + pc).wait_recv()
        for cp in sends:
            cp.wait_send()
        acc = buf[0]
        for d in range(1, n_dev):
            acc = acc + buf[d]
        o_ref[...] = acc

    vm = pl.BlockSpec(memory_space=pltpu.VMEM)
    return pl.pallas_call(
        body, name="small_all_reduce", in_specs=[vm], out_specs=vm, out_shape=jax.ShapeDtypeStruct((r, cols), F32),
        scratch_shapes=[pltpu.VMEM((n_dev, r, cols), F32), pltpu.SemaphoreType.DMA((n_dev - 1,)), pltpu.SemaphoreType.DMA((n_dev - 1,))],
        compiler_params=pltpu.CompilerParams(has_side_effects=True),
    )(v)


_COL_SHARDED = ("w_in", "w_q_up", "w_kv_up", "w_ffn_up")


def _shard_cols(blocks, a, b):
    c = blocks[0].shape[-1]
    out = []
    while a < b:
        k = a // c
        hi = min(b, (k + 1) * c)
        out.append(blocks[k][:, a - k * c:hi - k * c])
        a = hi
    return out


def _pack_w_in_shards(blocks):
    z = lambda n: [jnp.zeros((blocks[0].shape[0], n), blocks[0].dtype)]
    cols = lambda a, b: _shard_cols(blocks, a, b)
    return jnp.concatenate(cols(0, 768) + cols(772, 1028) + cols(1188, 2980) + cols(1028, 1156) + cols(768, 772)
                           + z(KR_LANE - N_HEADS) + cols(1156, 1188) + z(LANES - KR_LANE - ROPE_DIM), axis=1)


def _whole_layer(name, blocks):
    if name in _COL_SHARDED:
        return jnp.concatenate([blocks[k] for k in range(N_CHIPS)], axis=1)
    return blocks.reshape(N_CHIPS * blocks.shape[1], blocks.shape[2])


def _split_layer(name, whole):
    if name in _COL_SHARDED:
        c = whole.shape[1] // N_CHIPS
        return jnp.stack([whole[:, k * c:(k + 1) * c] for k in range(N_CHIPS)])
    return whole.reshape(N_CHIPS, whole.shape[0] // N_CHIPS, whole.shape[1])


def _small_to_rows(d):
    v = jnp.concatenate([d[k].astype(F32).reshape(-1) for k in SMALL])
    rows = -(-v.shape[0] // (8 * LANES)) * 8
    return jnp.pad(v, (0, rows * LANES - v.shape[0])).reshape(rows, LANES)


def _small_from_rows(rows, shapes):
    v = rows.reshape(-1)
    out, o = {}, 0
    for k in SMALL:
        sz = int(np.prod(shapes[k]))
        out[k] = v[o:o + sz].reshape(shapes[k])
        o += sz
    return out


_ARG_NAMES = ("x", "positions", "g_mix_pre", "w_in", "b_forget", "g_q_lora", "w_q_up", "g_kv_lora", "w_kv_up", "g_mix_out", "w_out",
              "g_mix_post", "g_ffn_pre", "w_ffn_up", "w_ffn_down", "g_ffn_post")
_WEIGHTS = _ARG_NAMES[2:]


def kernel(x, positions, g_mix_pre, w_in, b_forget, g_q_lora, w_q_up, g_kv_lora, w_kv_up, g_mix_out, w_out, g_mix_post, g_ffn_pre, w_ffn_up, w_ffn_down, g_ffn_post, loss_target, m_g_mix_pre, m_w_in, m_b_forget, m_g_q_lora, m_w_q_up, m_g_kv_lora, m_w_kv_up, m_g_mix_out, m_w_out, m_g_mix_post, m_g_ffn_pre, m_w_ffn_up, m_w_ffn_down, m_g_ffn_post, v_g_mix_pre, v_w_in, v_b_forget, v_g_q_lora, v_w_q_up, v_g_kv_lora, v_w_kv_up, v_g_mix_out, v_w_out, v_g_mix_post, v_g_ffn_pre, v_w_ffn_up, v_w_ffn_down, v_g_ffn_post):
    w = dict(g_mix_pre=g_mix_pre, w_in=w_in, b_forget=b_forget, g_q_lora=g_q_lora, w_q_up=w_q_up, g_kv_lora=g_kv_lora, w_kv_up=w_kv_up,
             g_mix_out=g_mix_out, w_out=w_out, g_mix_post=g_mix_post, g_ffn_pre=g_ffn_pre, w_ffn_up=w_ffn_up, w_ffn_down=w_ffn_down,
             g_ffn_post=g_ffn_post)
    m = dict(g_mix_pre=m_g_mix_pre, w_in=m_w_in, b_forget=m_b_forget, g_q_lora=m_g_q_lora, w_q_up=m_w_q_up, g_kv_lora=m_g_kv_lora,
             w_kv_up=m_w_kv_up, g_mix_out=m_g_mix_out, w_out=m_w_out, g_mix_post=m_g_mix_post, g_ffn_pre=m_g_ffn_pre,
             w_ffn_up=m_w_ffn_up, w_ffn_down=m_w_ffn_down, g_ffn_post=m_g_ffn_post)
    v = dict(g_mix_pre=v_g_mix_pre, w_in=v_w_in, b_forget=v_b_forget, g_q_lora=v_g_q_lora, w_q_up=v_w_q_up, g_kv_lora=v_g_kv_lora,
             w_kv_up=v_w_kv_up, g_mix_out=v_g_mix_out, w_out=v_w_out, g_mix_post=v_g_mix_post, g_ffn_pre=v_g_ffn_pre,
             w_ffn_up=v_w_ffn_up, w_ffn_down=v_w_ffn_down, g_ffn_post=v_g_ffn_post)
    small_shapes = {k: w[k].shape for k in SMALL}
    c_idx = lax.axis_index("c").astype(jnp.int32).reshape(1)
    k_idx = (2 * lax.axis_index("x") + lax.axis_index("y")).astype(jnp.int32).reshape(1)
    first_core = lax.axis_index("c") == 0

    mine = 2 * lax.axis_index("x") + lax.axis_index("y")
    shards_b = [{k: w[k][l:l + 1].astype(BF16) for k in BIG} for l in range(DEPTH)]
    gains = [dict(g_mix_pre=g_mix_pre[l], b_forget=b_forget[l], g_q_lora=g_q_lora[l], g_kv_lora=g_kv_lora[l], g_mix_out=g_mix_out[l],
                  g_mix_post=g_mix_post[l], g_ffn_pre=g_ffn_pre[l], g_ffn_post=g_ffn_post[l]) for l in range(DEPTH)]
    FIRST, LATER = ("w_in", "w_q_up", "w_kv_up"), ("w_out", "w_ffn_up", "w_ffn_down")
    EARLY_GRADS, LATE_GRADS = ("w_ffn_down", "w_ffn_up", "w_out"), ("w_in", "w_q_up", "w_kv_up")
    SC_ADAMW = EARLY_GRADS

    def gather_job(l, names):
        return _gather_job([shards_b[l][k] for k in names])

    def weights_of(l, names, gathered):
        four = {k: lax.dynamic_update_slice(g, shards_b[l][k][None], (mine, 0, 0, 0))[:, 0]
                for k, g in zip(names, _forward_halves(gathered))}
        out = {}
        for k in names:
            if k == "w_in":
                out["w_in"] = _pack_w_in_shards(four[k])
            elif k == "w_q_up":
                out["wq"] = _pack_w_q(_whole_layer(k, four[k]))
            elif k == "w_kv_up":
                out["wk"], out["wv"] = _pack_w_kv(_whole_layer(k, four[k]))
            elif k == "w_ffn_up":
                out[k] = four[k]
            else:
                out[k] = _whole_layer(k, four[k])
        return out

    def grad_blocks(names, g):
        whole = dict(w_in=lambda: _unpack_dw_in(g["w_in"]), w_q_up=lambda: _unpack_dw_q(g["wq"]),
                     w_kv_up=lambda: _unpack_dw_kv(g["wk"], g["wv"]), w_out=lambda: g["w_out"], w_ffn_down=lambda: g["w_ffn_down"])
        return [(g[k] if k == "w_ffn_up" else _split_layer(k, whole[k]()))[:, None] for k in names]

    def pair_sums(names, blocks, theirs):
        return [_pair_add(b, r, c_idx, name="grad_pair_add_" + k) for k, b, r in zip(names, blocks, theirs)]

    def exchange_job(*pairs):
        return _exchange_chips_job([pb for pair in pairs for (_, pb) in pair])

    def finish_grads(names, pair, partial):
        half = [_chip_add(p, r, k_idx, name="grad_chip_add_" + k) for k, (p, _), r in zip(names, pair, partial)]
        return {k: jnp.where(first_core, jnp.concatenate([q, s], axis=1), jnp.concatenate([s, q], axis=1))
                for k, q, s in zip(names, half, _share_halves(half))}

    seq = x.shape[1]
    tabs = _rope_tables(positions[0].reshape(seq, 1))
    first0 = weights_of(0, FIRST, _run_side_job(gather_job(0, FIRST), "gather_weights_l0"))
    x1, saved0, lw0, gathered1, h1 = _layer_fwd(x[0], {**gains[0], **first0}, tabs, "l0_", fox_side=gather_job(0, LATER),
                                                late_weights=lambda got: weights_of(0, LATER, got), side=gather_job(1, BIG),
                                                next_gain=gains[1]["g_mix_pre"])
    lw1 = {**gains[1], **weights_of(1, BIG, gathered1)}
    _, saved1, _, _, _ = _layer_fwd(x1, lw1, tabs, "l1_", h1=h1)
    loss_row, dx, df1, dg1 = _loss_head(saved1["f"], lw1["g_ffn_post"], saved1["x1"], loss_target[0])
    loss = lax.psum(loss_row[0, 0], ("x", "y", "c"))
    dx, grads1, _, post0 = _layer_bwd(dx, lw1, saved1, tabs, "l1_", post_given=(df1, dg1),
                                      then_prev=(saved0["f"], lw0["g_ffn_post"]))
    blocks1 = grad_blocks(BIG, grads1)
    early_blocks0, pair1, early0 = [], [], []

    def beside_l0_fox_backward(g):
        early_blocks0.extend(grad_blocks(EARLY_GRADS, g))
        return _exchange_halves_job(early_blocks0)

    def beside_l0_sb_backward(g, theirs1, theirs_early0):
        pair1.extend(pair_sums(BIG, blocks1, theirs1))
        early0.extend(pair_sums(EARLY_GRADS, early_blocks0, theirs_early0))
        return exchange_job(pair1, early0)

    dx, grads0, partial, _ = _layer_bwd(dx, lw0, saved0, tabs, "l0_", ffn_side=_exchange_halves_job(blocks1),
                                        fox_side=beside_l0_fox_backward, side=beside_l0_sb_backward, post_given=post0)
    big1 = finish_grads(BIG, pair1, partial[:len(BIG)])
    big0 = finish_grads(EARLY_GRADS, early0, partial[len(BIG):])
    g_early = {k: jnp.concatenate([big0[k], big1[k]], axis=0) for k in SC_ADAMW}
    sc_delta, sc_m, sc_v = _adamw_sparsecore([w[k] for k in SC_ADAMW], [g_early[k] for k in SC_ADAMW], [m[k] for k in SC_ADAMW],
                                             [v[k] for k in SC_ADAMW], name="adamw_sparsecore")
    late_blocks0 = grad_blocks(LATE_GRADS, grads0)
    late0 = pair_sums(LATE_GRADS, late_blocks0, _run_side_job(_exchange_halves_job(late_blocks0), "grad_pair_exchange_l0"))
    big0.update(finish_grads(LATE_GRADS, late0, _run_side_job(exchange_job(late0), "grad_chip_exchange_l0")))
    g_big = {k: g_early[k] if k in SC_ADAMW else jnp.concatenate([big0[k], big1[k]], axis=0) for k in BIG}
    w_in_lead = _adamw_lead(*[jnp.transpose(t["w_in"], (2, 0, 1)) for t in (w, g_big, m, v)], name="adamw_w_in", steps=5)
    sc_late = [[jnp.transpose(t, (1, 2, 0))] for t in w_in_lead]
    grads = [grads0, grads1]

    g_small_local = {k: jnp.stack([grads[l][k].reshape(small_shapes[k][1:]) for l in range(DEPTH)]) for k in SMALL}
    g_small = _small_from_rows(_all_reduce_small(_small_to_rows(g_small_local)), small_shapes)

    g_all = {**g_big, **g_small}
    delta, new_m, new_v = {}, {}, {}
    for k in BIG:
        if k in SC_ADAMW:
            i = SC_ADAMW.index(k)
            delta[k], new_m[k], new_v[k] = sc_delta[i], sc_m[i], sc_v[i]
        elif k == "w_in":
            delta[k], new_m[k], new_v[k] = [t[0] for t in sc_late]
        elif k == "w_q_up":
            out = _adamw(*[jnp.swapaxes(t[k], 1, 2) for t in (w, g_all, m, v)], name="adamw_" + k)
            delta[k], new_m[k], new_v[k] = [jnp.swapaxes(t, 1, 2) for t in out]
        else:
            delta[k], new_m[k], new_v[k] = _adamw(w[k], g_all[k], m[k], v[k], name="adamw_" + k)
    ds, ms, vs = _adamw(*[_small_to_rows(t)[None] for t in (w, g_small, m, v)], name="adamw_small")
    delta.update(_small_from_rows(ds, small_shapes))
    new_m.update(_small_from_rows(ms, small_shapes))
    new_v.update(_small_from_rows(vs, small_shapes))

    grad_x = dx.reshape(x.shape)
    return (loss, grad_x, *[g_all[k] for k in _WEIGHTS], *[delta[k] for k in _WEIGHTS], *[new_m[k] for k in _WEIGHTS],
            *[new_v[k] for k in _WEIGHTS])
```

```python
import functools
import math

import numpy as np
import jax
import jax.numpy as jnp
from jax import lax
from jax.experimental import pallas as pl
from jax.experimental.pallas import tpu as pltpu
from jax.experimental.pallas import tpu_sc as plsc

F32 = jnp.float32
BF16 = jnp.bfloat16
MESH = pl.DeviceIdType.MESH

D_MODEL = 1024
DEPTH = 2
CHUNK = 64
GROUP = 256
HEAD = 64
N_HEADS = 4
Q_RANK = 256
KV_RANK = 128
ROPE_DIM = 32
D_FF = 4096
D_IN = 2980
D_INP = 3072
ROPE_BASE = 10000.0
EPS = 1e-6
LANES = 128
TQ = 128
GATE_ROWS = 512
CONTRACT_TILE = 4096
NEG = -1e30

ADAM_LR, ADAM_B1, ADAM_B2, ADAM_EPS, ADAM_WD, ADAM_STEP = 0.001, 0.9, 0.999, 1e-08, 0.01, 10

OFF_FQ, OFF_FK, OFF_FV, OFF_CQ = 0, 2, 4, 6
OFF_RQ, OFF_RK, OFF_RV, OFF_RG = 8, 10, 12, 14
OFF_SQ, OFF_SK, OFF_SV = 16, 18, 20
OFF_CKV, OFF_MISC = 22, 23
FF_LANE, KR_LANE = 0, 64

VMEM_LIMIT = 56 * 1024 * 1024


def _tile(dim, pref):
    return pref if dim % pref == 0 else dim


def _cparams(sem, vmem=None):
    return pltpu.CompilerParams(dimension_semantics=sem, vmem_limit_bytes=vmem or VMEM_LIMIT)


def _dot(a, b):
    return jnp.dot(a, b, preferred_element_type=F32)


def _dot_nt(a, b):
    return lax.dot_general(a, b, (((1,), (1,)), ((), ())), preferred_element_type=F32)


def _dot_tn(a, b):
    return lax.dot_general(a, b, (((0,), (0,)), ((), ())), preferred_element_type=F32)


def _dot_exact(a, b):
    return jnp.dot(a, b, precision=lax.Precision.HIGHEST, preferred_element_type=F32)


def _matmul(a, b, *, name, ta=False, tb=False, out_dtype=F32, tm=1024, tn=1024, tk=CONTRACT_TILE,
            relu2=False, relu2_of=None, also_bf16=False, side=None, col_blocks=False):
    if ta:
        kdim, m = a.shape
    else:
        m, kdim = a.shape
    if col_blocks and not ta:
        n = b.shape[1] if tb else b.shape[0] * b.shape[2]
        if tb:
            kdim = b.shape[0] * b.shape[2]
    else:
        n = b.shape[0] if tb else b.shape[1]
    tm, tn, tk = _tile(m, tm), _tile(n, tn), _tile(kdim, tk)
    nk = kdim // tk
    a_spec = pl.BlockSpec((tk, tm), lambda i, j, k: (k, i)) if ta else pl.BlockSpec((tm, tk), lambda i, j, k: (i, k))
    b_spec = pl.BlockSpec((tn, tk), lambda i, j, k: (j, k)) if tb else pl.BlockSpec((tk, tn), lambda i, j, k: (k, j))
    o_spec = pl.BlockSpec((tm, tn), lambda i, j, k: (i, j))
    if col_blocks and ta:
        o_spec = pl.BlockSpec((None, tm, tn), lambda i, j, k: (j, i, 0))
    elif col_blocks and tb:
        assert tk == kdim
        b_spec = pl.BlockSpec((b.shape[0], tn, b.shape[2]), lambda i, j, k: (0, j, 0))
    elif col_blocks:
        assert b.shape[2] == tn
        b_spec = pl.BlockSpec((None, tk, tn), lambda i, j, k: (j, k, 0))
    two = also_bf16

    def body(*refs):
        refs = list(refs)
        a_ref, b_ref = refs[0], refs[1]
        e_ref = refs[2] if relu2_of is not None else None
        pos = 3 if relu2_of is not None else 2
        o_ref = refs[pos]
        o2_ref = refs[pos + 1] if two else None
        acc_ref = refs[-1]
        k = pl.program_id(2)
        av = a_ref[...].astype(BF16)
        if col_blocks and tb:
            bv = jnp.concatenate([b_ref[q] for q in range(b.shape[0])], axis=1).astype(BF16)
        else:
            bv = b_ref[...].astype(BF16)
        if ta:
            part = _dot_tn(av, bv)
        elif tb:
            part = _dot_nt(av, bv)
        else:
            part = _dot(av, bv)

        @pl.when(k == 0)
        def _():
            acc_ref[...] = part

        @pl.when(k > 0)
        def _():
            acc_ref[...] += part

        @pl.when(k == nk - 1)
        def _():
            r = acc_ref[...]
            if relu2_of is not None:
                r = r * (2.0 * jnp.sqrt(e_ref[...].astype(F32)))
            if relu2:
                r = jnp.square(jnp.maximum(r, 0.0))
            o_ref[...] = r.astype(o_ref.dtype)
            if also_bf16:
                o2_ref[...] = r.astype(BF16)

    in_specs = [a_spec, b_spec]
    args = [a, b]
    if relu2_of is not None:
        in_specs.append(o_spec)
        args.append(relu2_of)
    out_shape = [jax.ShapeDtypeStruct((n // tn, m, tn) if (col_blocks and ta) else (m, n), out_dtype)]
    out_specs = [o_spec]
    if two:
        out_shape.append(jax.ShapeDtypeStruct((m, n), BF16))
        out_specs.append(o_spec)
    grid = (m // tm, n // tn, nk)
    side_in, side_out, side_scratch = _side_specs(side)
    res = pl.pallas_call(
        _carry_side_job(body, len(args), len(out_shape), side, grid), name=name, grid=grid,
        in_specs=in_specs + side_in, out_specs=out_specs + side_out,
        out_shape=out_shape + ([] if side is None else side.out_shape),
        scratch_shapes=[pltpu.VMEM((tm, tn), F32)] + side_scratch,
        compiler_params=_cparams(("parallel", "parallel", "arbitrary") if side is None else ("arbitrary",) * 3),
    )(*args, *([] if side is None else side.inputs))
    main = res[:len(out_shape)]
    main = main if two else main[0]
    return main if side is None else (main, res[len(out_shape):])


def _rms(x, g):
    r = lax.rsqrt(jnp.mean(x * x, axis=-1, keepdims=True) + EPS)
    return x * r * g


def _rms_bwd(x, g, dy):
    r = lax.rsqrt(jnp.mean(x * x, axis=-1, keepdims=True) + EPS)
    xh = x * r
    gdy = dy * g
    dx = r * (gdy - xh * jnp.mean(xh * gdy, axis=-1, keepdims=True))
    return dx, xh * dy


def _norm_fwd(x, g, *, name, resid=None, out_dtype=BF16, next_gain=None):
    s, d = x.shape
    tr = _tile(s, 256)
    row = pl.BlockSpec((tr, d), lambda i: (i, 0))
    gsp = pl.BlockSpec((1, d), lambda i: (0, 0))

    def body(*refs):
        refs = list(refs)
        x_ref, g_ref = refs[:2]
        y = _rms(x_ref[...], g_ref[...])
        pos = 2
        if resid is not None:
            y = refs[pos][...] + y
            pos += 1
        if next_gain is None:
            refs[pos][...] = y.astype(refs[pos].dtype)
        else:
            refs[pos + 1][...] = y.astype(refs[pos + 1].dtype)
            refs[pos + 2][...] = _rms(y, refs[pos][...]).astype(BF16)

    args = [x, g.reshape(1, d)] + ([] if resid is None else [resid]) + ([] if next_gain is None else [next_gain.reshape(1, d)])
    in_specs = [row, gsp] + ([] if resid is None else [row]) + ([] if next_gain is None else [gsp])
    first = jax.ShapeDtypeStruct((s, d), out_dtype)
    if next_gain is None:
        out_specs, out_shape = row, first
    else:
        out_specs, out_shape = [row, row], [first, jax.ShapeDtypeStruct((s, d), BF16)]
    return pl.pallas_call(
        body, name=name, grid=(s // tr,), in_specs=in_specs, out_specs=out_specs, out_shape=out_shape,
        compiler_params=_cparams(("parallel",)),
    )(*args)


def _norm_bwd(x, g, dy, *, name, add=None, out_dtype=F32, then=None):
    s, d = x.shape
    tr = _tile(s, 256)
    row = pl.BlockSpec((tr, d), lambda i: (i, 0))
    gsp = pl.BlockSpec((1, d), lambda i: (0, 0))
    n_in = 3 + (add is not None) + (2 if then is not None else 0)

    def body(*refs):
        ins, outs = refs[:n_in], refs[n_in:]
        x_ref, g_ref, dy_ref = ins[:3]
        dx, gterm = _rms_bwd(x_ref[...], g_ref[...], dy_ref[...].astype(F32))
        if add is not None:
            dx = dx + ins[3][...]
        outs[0][...] = dx.astype(outs[0].dtype)
        terms = [(outs[1], gterm)]
        if then is not None:
            dx2, gterm2 = _rms_bwd(ins[-2][...], ins[-1][...], dx)
            outs[2][...] = dx2.astype(BF16)
            terms.append((outs[3], gterm2))

        @pl.when(pl.program_id(0) == 0)
        def _():
            for dg_ref, _ in terms:
                dg_ref[...] = jnp.zeros_like(dg_ref)

        for dg_ref, term in terms:
            dg_ref[...] += jnp.sum(term, axis=0, keepdims=True)

    args = [x, g.reshape(1, d), dy] + ([] if add is None else [add]) + ([] if then is None else [then[0], then[1].reshape(1, d)])
    in_specs = [row, gsp, row] + ([] if add is None else [row]) + ([] if then is None else [row, gsp])
    out_specs = [row, gsp] + ([] if then is None else [row, gsp])
    out_shape = [jax.ShapeDtypeStruct((s, d), out_dtype), jax.ShapeDtypeStruct((1, d), F32)]
    if then is not None:
        out_shape += [jax.ShapeDtypeStruct((s, d), BF16), jax.ShapeDtypeStruct((1, d), F32)]
    return pl.pallas_call(
        body, name=name, grid=(s // tr,), in_specs=in_specs, out_specs=out_specs, out_shape=out_shape,
        compiler_params=_cparams(("arbitrary",)),
    )(*args)


def _loss_head(f, g, resid, target):
    s, d = f.shape
    tr = _tile(s, 256)
    row = pl.BlockSpec((tr, d), lambda i: (i, 0))
    gsp = pl.BlockSpec((1, d), lambda i: (0, 0))
    lsp = pl.BlockSpec((1, LANES), lambda i: (0, 0))

    def body(f_ref, g_ref, r_ref, t_ref, l_ref, dy_ref, df_ref, dg_ref):
        fv, gv = f_ref[...], g_ref[...]
        e = (r_ref[...] + _rms(fv, gv)) - t_ref[...]
        dy = e * (1.0 / d)
        dy_ref[...] = dy
        df, gterm = _rms_bwd(fv, gv, dy)
        df_ref[...] = df.astype(BF16)

        @pl.when(pl.program_id(0) == 0)
        def _():
            l_ref[...] = jnp.zeros_like(l_ref)
            dg_ref[...] = jnp.zeros_like(dg_ref)

        part = 0.5 * jnp.sum(jnp.mean(e * e, axis=-1, keepdims=True), axis=0, keepdims=True)
        l_ref[...] += jnp.broadcast_to(part, (1, LANES))
        dg_ref[...] += jnp.sum(gterm, axis=0, keepdims=True)

    return pl.pallas_call(
        body, name="loss_head", grid=(s // tr,), in_specs=[row, gsp, row, row], out_specs=[lsp, row, row, gsp],
        out_shape=[jax.ShapeDtypeStruct((1, LANES), F32), jax.ShapeDtypeStruct((s, d), F32), jax.ShapeDtypeStruct((s, d), BF16),
                   jax.ShapeDtypeStruct((1, d), F32)],
        compiler_params=_cparams(("arbitrary",)),
    )(f, g.reshape(1, d), resid, target)


def _rope_tables(pos_col):
    s = pos_col.shape[0]
    tr = _tile(s, 512)
    f_mla = ROPE_BASE ** (-jnp.arange(ROPE_DIM // 2, dtype=F32) / (ROPE_DIM // 2))
    f_ret = ROPE_BASE ** (-jnp.arange(HEAD // 2, dtype=F32) / (HEAD // 2))
    fm = jnp.concatenate([jnp.zeros((64,), F32), f_mla, f_mla, jnp.zeros((32,), F32)]).reshape(1, LANES)
    fr = jnp.tile(jnp.concatenate([f_ret, f_ret]), 2).reshape(1, LANES)

    def body(p_ref, fm_ref, fr_ref, cm_ref, sm_ref, cr_ref, sr_ref):
        p = p_ref[...].astype(F32)
        am = p * fm_ref[...]
        ar = p * fr_ref[...]
        cm_ref[...] = jnp.cos(am)
        sm_ref[...] = jnp.sin(am)
        cr_ref[...] = jnp.tile(jnp.cos(ar), (1, 2))
        sr_ref[...] = jnp.tile(jnp.sin(ar), (1, 2))

    return pl.pallas_call(
        body, name="rope_tables", grid=(s // tr,),
        in_specs=[pl.BlockSpec((tr, 1), lambda i: (i, 0)), pl.BlockSpec((1, LANES), lambda i: (0, 0)),
                  pl.BlockSpec((1, LANES), lambda i: (0, 0))],
        out_specs=[pl.BlockSpec((tr, LANES), lambda i: (i, 0))] * 2 + [pl.BlockSpec((tr, 2 * LANES), lambda i: (i, 0))] * 2,
        out_shape=[jax.ShapeDtypeStruct((s, LANES), F32)] * 2 + [jax.ShapeDtypeStruct((s, 2 * LANES), F32)] * 2,
        compiler_params=_cparams(("parallel",)),
    )(pos_col, fm, fr)


def _lane(shape):
    return lax.broadcasted_iota(jnp.int32, shape, len(shape) - 1)


def _rot_mla(z):
    l = _lane(z.shape) % LANES
    n = z.shape[-1]
    return jnp.where(l < 80, -pltpu.roll(z, n - 16, 1), pltpu.roll(z, 16, 1))


def _rot_mla_t(y):
    l = _lane(y.shape) % LANES
    n = y.shape[-1]
    return jnp.where((l >= 64) & (l < 80), pltpu.roll(y, n - 16, 1),
                     jnp.where((l >= 80) & (l < 96), -pltpu.roll(y, 16, 1), 0.0))


def _rot_ret(z):
    l = _lane(z.shape) % HEAD
    n = z.shape[-1]
    return jnp.where(l < 32, -pltpu.roll(z, n - 32, 1), pltpu.roll(z, 32, 1))


def _rot_ret_t(y):
    l = _lane(y.shape) % HEAD
    n = y.shape[-1]
    return jnp.where(l < 32, pltpu.roll(y, n - 32, 1), -pltpu.roll(y, 32, 1))


def _log_sigmoid(x):
    return jnp.minimum(x, 0.0) - jnp.log1p(jnp.exp(-jnp.abs(x)))


def _fox_cum(proj, bias_row):
    s = proj.shape[0]
    fb = _tile(s, GATE_ROWS)
    nb = s // fb

    def body(x_ref, b_ref, cc_ref, cr_ref, carry_ref):
        @pl.when(pl.program_id(0) == 0)
        def _():
            carry_ref[...] = jnp.zeros_like(carry_ref)

        ls = _log_sigmoid(x_ref[...] + b_ref[...])
        r = lax.broadcasted_iota(jnp.int32, (fb, fb), 0)
        c = lax.broadcasted_iota(jnp.int32, (fb, fb), 1)
        tri = (c <= r).astype(F32)
        cum = _dot_exact(tri, ls) + carry_ref[...]
        carry_ref[...] = cum[fb - 1:fb, :]
        cc_ref[...] = cum
        cr_ref[...] = cum.T[0:8, :]

    return pl.pallas_call(
        body, name="fox_cum", grid=(nb,),
        in_specs=[pl.BlockSpec((fb, LANES), lambda i: (i, OFF_MISC)), pl.BlockSpec((1, LANES), lambda i: (0, 0))],
        out_specs=[pl.BlockSpec((fb, LANES), lambda i: (i, 0)), pl.BlockSpec((8, fb), lambda i: (0, i))],
        out_shape=[jax.ShapeDtypeStruct((s, LANES), F32), jax.ShapeDtypeStruct((8, s), F32)],
        scratch_shapes=[pltpu.VMEM((1, LANES), F32)],
        compiler_params=_cparams(("arbitrary",)),
    )(proj, bias_row)


def _fox_gate_bwd(dck, drs, proj, bias_row, dkr):
    s = proj.shape[0]
    fb = _tile(s, GATE_ROWS)
    nb = s // fb

    def body(d_ref, r_ref, x_ref, b_ref, k_ref, o_ref, db_ref, carry_ref):
        @pl.when(pl.program_id(0) == 0)
        def _():
            carry_ref[...] = jnp.zeros_like(carry_ref)
            db_ref[...] = jnp.zeros_like(db_ref)

        rows = jnp.concatenate([d_ref[0], d_ref[1], jnp.zeros((LANES - 16, fb), F32)], axis=0)
        t = rows.T
        l = _lane((fb, LANES))
        r0, r1 = r_ref[0], r_ref[1]
        rsum = jnp.where(l == 0, r0[:, 0:1], jnp.where(l == 1, r0[:, HEAD:HEAD + 1],
                         jnp.where(l == 2, r1[:, 0:1], jnp.where(l == 3, r1[:, HEAD:HEAD + 1], 0.0))))
        dcum = rsum - jnp.where(l < 2, t, pltpu.roll(t, LANES - 6, 1))
        r = lax.broadcasted_iota(jnp.int32, (fb, fb), 0)
        c = lax.broadcasted_iota(jnp.int32, (fb, fb), 1)
        triu = (c >= r).astype(F32)
        rc = _dot_exact(triu, dcum) + carry_ref[...]
        carry_ref[...] = rc[0:1, :]
        f = x_ref[...] + b_ref[...]
        sig_neg = 1.0 / (1.0 + jnp.exp(f))
        df = jnp.where(l < N_HEADS, rc * sig_neg, 0.0)
        db_ref[...] += jnp.sum(df, axis=0, keepdims=True)
        o_ref[...] = (df + k_ref[...]).astype(o_ref.dtype)

    rev = lambda i: nb - 1 - i
    return pl.pallas_call(
        body, name="fox_gate_bwd", grid=(nb,),
        in_specs=[pl.BlockSpec((2, 8, fb), lambda i: (0, 0, rev(i))), pl.BlockSpec((2, fb, LANES), lambda i: (0, rev(i), 0)),
                  pl.BlockSpec((fb, LANES), lambda i: (rev(i), OFF_MISC)),
                  pl.BlockSpec((1, LANES), lambda i: (0, 0)), pl.BlockSpec((fb, LANES), lambda i: (rev(i), 0))],
        out_specs=[pl.BlockSpec((fb, LANES), lambda i: (rev(i), 0)), pl.BlockSpec((1, LANES), lambda i: (0, 0))],
        out_shape=[jax.ShapeDtypeStruct((s, LANES), BF16), jax.ShapeDtypeStruct((1, LANES), F32)],
        scratch_shapes=[pltpu.VMEM((1, LANES), F32)],
        compiler_params=_cparams(("arbitrary",)),
    )(dck, drs, proj, bias_row, dkr)


def _mla_prep(proj, cos_m, sin_m, g_q, g_kv, wq, wk, wv):
    s = proj.shape[0]
    tr = _tile(s, 512)

    def body(cq_ref, ckv_ref, misc_ref, cos_ref, sin_ref, gq_ref, gkv_ref, wq_ref, wk_ref, wv_ref,
             q_ref, k_ref, v_ref, cqn_ref, ckvn_ref):
        cos4 = jnp.tile(cos_ref[...], (1, 4))
        sin4 = jnp.tile(sin_ref[...], (1, 4))
        cqn = _rms(cq_ref[...], gq_ref[...]).astype(BF16)
        ckvn = _rms(ckv_ref[...], gkv_ref[...]).astype(BF16)
        cqn_ref[...] = cqn
        ckvn_ref[...] = ckvn
        zq = _dot(cqn, wq_ref[...])
        q_ref[...] = (zq * cos4 + _rot_mla(zq) * sin4).astype(BF16)
        l = _lane((tr, LANES))
        kr = jnp.where((l >= KR_LANE) & (l < KR_LANE + ROPE_DIM), misc_ref[...], 0.0)
        zk = _dot(ckvn, wk_ref[...]) + jnp.tile(kr, (1, 4))
        k_ref[...] = (zk * cos4 + _rot_mla(zk) * sin4).astype(BF16)
        v_ref[...] = _dot(ckvn, wv_ref[...]).astype(BF16)

    full = lambda a: pl.BlockSpec(a.shape, lambda i: (0, 0))
    rowb = lambda w: pl.BlockSpec((tr, w), lambda i: (i, 0))
    gq2, gkv2 = g_q.reshape(1, Q_RANK), g_kv.reshape(1, KV_RANK)
    return pl.pallas_call(
        body, name="mla_prep", grid=(s // tr,),
        in_specs=[pl.BlockSpec((tr, 256), lambda i: (i, OFF_CQ // 2)), pl.BlockSpec((tr, LANES), lambda i: (i, OFF_CKV)),
                  pl.BlockSpec((tr, LANES), lambda i: (i, OFF_MISC)), rowb(LANES), rowb(LANES),
                  full(gq2), full(gkv2), full(wq), full(wk), full(wv)],
        out_specs=[rowb(512), rowb(512), rowb(512), rowb(256), rowb(128)],
        out_shape=[jax.ShapeDtypeStruct((s, 512), BF16), jax.ShapeDtypeStruct((s, 512), BF16), jax.ShapeDtypeStruct((s, 512), BF16),
                   jax.ShapeDtypeStruct((s, 256), BF16), jax.ShapeDtypeStruct((s, 128), BF16)],
        compiler_params=_cparams(("parallel",)),
    )(proj, proj, proj, cos_m, sin_m, gq2, gkv2, wq, wk, wv)


def _mla_prep_bwd(dq, dk, dv, proj, cqn, ckvn, cos_m, sin_m, g_q, g_kv, wq, wk, wv):
    s = proj.shape[0]
    tr = _tile(s, 512)

    def body(dq_ref, dk_ref, dv_ref, cq_ref, ckv_ref, cqn_ref, ckvn_ref, cos_ref, sin_ref, gq_ref, gkv_ref,
             wq_ref, wk_ref, wv_ref, dcq_ref, dckv_ref, dkr_ref, dwq_ref, dwk_ref, dwv_ref, dgq_ref, dgkv_ref):
        @pl.when(pl.program_id(0) == 0)
        def _():
            for r in (dwq_ref, dwk_ref, dwv_ref, dgq_ref, dgkv_ref):
                r[...] = jnp.zeros_like(r)

        cos4 = jnp.tile(cos_ref[...], (1, 4))
        sin4 = jnp.tile(sin_ref[...], (1, 4))
        dqv = dq_ref[...]
        dzq = dqv * cos4 + _rot_mla_t(dqv * sin4)
        dkv_ = dk_ref[...]
        dzk = dkv_ * cos4 + _rot_mla_t(dkv_ * sin4)
        l = _lane((tr, LANES))
        in_rope = (l >= KR_LANE) & (l < KR_LANE + ROPE_DIM)
        dkr = dzk[:, 0:128] + dzk[:, 128:256] + dzk[:, 256:384] + dzk[:, 384:512]
        dkr_ref[...] = jnp.where(in_rope, dkr, 0.0)
        dzq_b = dzq.astype(BF16)
        dzk_b = dzk.astype(BF16)
        dv_b = dv_ref[...].astype(BF16)
        dcqn = _dot_nt(dzq_b, wq_ref[...])
        dckvn = _dot_nt(dzk_b, wk_ref[...]) + _dot_nt(dv_b, wv_ref[...])
        dwq_ref[...] += _dot_tn(cqn_ref[...], dzq_b)
        dwk_ref[...] += _dot_tn(ckvn_ref[...], dzk_b)
        dwv_ref[...] += _dot_tn(ckvn_ref[...], dv_b)
        dcq, gq_term = _rms_bwd(cq_ref[...], gq_ref[...], dcqn)
        dckv, gkv_term = _rms_bwd(ckv_ref[...], gkv_ref[...], dckvn)
        dcq_ref[...] = dcq.astype(BF16)
        dckv_ref[...] = dckv.astype(BF16)
        dgq_ref[...] += jnp.sum(gq_term, axis=0, keepdims=True)
        dgkv_ref[...] += jnp.sum(gkv_term, axis=0, keepdims=True)

    full = lambda shp: pl.BlockSpec(shp, lambda i: (0, 0))
    rowb = lambda w: pl.BlockSpec((tr, w), lambda i: (i, 0))
    gq2, gkv2 = g_q.reshape(1, Q_RANK), g_kv.reshape(1, KV_RANK)
    return pl.pallas_call(
        body, name="mla_prep_bwd", grid=(s // tr,),
        in_specs=[rowb(512), rowb(512), rowb(512),
                  pl.BlockSpec((tr, 256), lambda i: (i, OFF_CQ // 2)), pl.BlockSpec((tr, LANES), lambda i: (i, OFF_CKV)),
                  rowb(256), rowb(128), rowb(LANES), rowb(LANES), full((1, Q_RANK)), full((1, KV_RANK)),
                  full(wq.shape), full(wk.shape), full(wv.shape)],
        out_specs=[rowb(256), rowb(128), rowb(128), full(wq.shape), full(wk.shape), full(wv.shape),
                   full((1, Q_RANK)), full((1, KV_RANK))],
        out_shape=[jax.ShapeDtypeStruct((s, 256), BF16), jax.ShapeDtypeStruct((s, 128), BF16), jax.ShapeDtypeStruct((s, 128), F32),
                   jax.ShapeDtypeStruct(wq.shape, F32), jax.ShapeDtypeStruct(wk.shape, F32), jax.ShapeDtypeStruct(wv.shape, F32),
                   jax.ShapeDtypeStruct((1, Q_RANK), F32), jax.ShapeDtypeStruct((1, KV_RANK), F32)],
        compiler_params=_cparams(("arbitrary",)),
    )(dq, dk, dv, proj, proj, cqn, ckvn, cos_m, sin_m, gq2, gkv2, wq, wk, wv)


def _ret_prep(proj, cos_r, sin_r):
    s = proj.shape[0]
    tr = _tile(s, 512)

    def body(q_ref, k_ref, cos_ref, sin_ref, qo_ref, ko_ref):
        cos, sin = cos_ref[...], sin_ref[...]
        q, k = q_ref[...], k_ref[...]
        qo_ref[...] = (q * cos + _rot_ret(q) * sin).astype(BF16)
        ko_ref[...] = ((k * cos + _rot_ret(k) * sin) * (HEAD ** -0.5)).astype(BF16)

    rowb = pl.BlockSpec((tr, 256), lambda i: (i, 0))
    return pl.pallas_call(
        body, name="ret_prep", grid=(s // tr,),
        in_specs=[pl.BlockSpec((tr, 256), lambda i: (i, OFF_RQ // 2)), pl.BlockSpec((tr, 256), lambda i: (i, OFF_RK // 2)), rowb, rowb],
        out_specs=[rowb, rowb], out_shape=[jax.ShapeDtypeStruct((s, 256), BF16)] * 2,
        compiler_params=_cparams(("parallel",)),
    )(proj, proj, cos_r, sin_r)


def _ret_prep_bwd(dq, dk, cos_r, sin_r):
    s = dq.shape[0]
    tr = _tile(s, 512)

    def body(dq_ref, dk_ref, cos_ref, sin_ref, qo_ref, ko_ref):
        cos, sin = cos_ref[...], sin_ref[...]
        q, k = dq_ref[...], dk_ref[...] * (HEAD ** -0.5)
        qo_ref[...] = (q * cos + _rot_ret_t(q * sin)).astype(BF16)
        ko_ref[...] = (k * cos + _rot_ret_t(k * sin)).astype(BF16)

    rowb = pl.BlockSpec((tr, 256), lambda i: (i, 0))
    return pl.pallas_call(
        body, name="ret_prep_bwd", grid=(s // tr,), in_specs=[rowb] * 4, out_specs=[rowb, rowb],
        out_shape=[jax.ShapeDtypeStruct((s, 256), BF16)] * 2, compiler_params=_cparams(("parallel",)),
    )(dq, dk, cos_r, sin_r)


_LOG_GAMMA = [float(np.log1p(-np.float32(2.0) ** np.float32(-5.0 - h))) for h in range(N_HEADS)]
_MLA_SCALE = float((HEAD + ROPE_DIM) ** -0.5)
_QK_SCALE = float(HEAD ** -0.5)
KEY_BLOCKS = 4
QB = 512


def _split2(x):
    h = x.astype(BF16)
    return h, (x - h.astype(F32)).astype(BF16)


def _dot2(x, u):
    h, lo = _split2(x)
    return _dot(h, u) + _dot(lo, u)


def _head_pick(block, head, axis):
    idx = lax.broadcasted_iota(jnp.int32, block.shape, axis)
    return jnp.sum(jnp.where(idx == head, block, 0.0), axis=axis, keepdims=True)


def _log_gamma_of(head):
    lg = jnp.float32(_LOG_GAMMA[3])
    for h in (2, 1, 0):
        lg = jnp.where(head == h, jnp.float32(_LOG_GAMMA[h]), lg)
    return lg


def _mixer_specs(mode, s, q_off, k_off, v_off):
    nhb = 2
    bw = 2 * LANES if mode == "mla" else LANES
    nsub = KEY_BLOCKS if (s // TQ) % KEY_BLOCKS == 0 else 1
    q_spec = pl.BlockSpec((QB, bw), lambda p, i: (i, q_off + p))
    k_spec = pl.BlockSpec((s, bw), lambda p, i: (0, k_off + p))
    v_spec = pl.BlockSpec((s, bw), lambda p, i: (0, v_off + p))
    return nhb, N_HEADS // nhb, nsub, q_spec, k_spec, v_spec


def _mixer_geometry(mode, i, nsub):
    w = TQ * nsub
    row = lax.broadcasted_iota(jnp.int32, (QB, w), 0)
    col = lax.broadcasted_iota(jnp.int32, (QB, w), 1)
    nfull = (i * QB) // w
    dist = col - row
    if mode in ("fox", "sb"):
        rel = dist
    else:
        rel = col - (row | (CHUNK - 1))

    def visible(c):
        off = c * w - i * QB
        return (rel + off) < 0 if mode == "sb" else (rel + off) <= 0

    return nfull, dist, visible


class _SideJob:
    def __init__(self, inputs, out_shape, n_sems, sends, recvs):
        self.inputs, self.out_shape, self.n_sems, self.sends, self.recvs = list(inputs), list(out_shape), n_sems, sends, recvs


def _carry_side_job(body, n_in, n_out, side, n_steps):
    if side is None:
        return body
    si, so = len(side.inputs), len(side.out_shape)

    def at(corner):
        ok = pl.program_id(0) == corner[0]
        for d in range(1, len(n_steps)):
            ok = ok & (pl.program_id(d) == corner[d])
        return ok

    def wrapped(*refs):
        ins, s_ins = refs[:n_in], refs[n_in:n_in + si]
        outs, s_outs = refs[n_in + si:n_in + si + n_out], refs[n_in + si + n_out:n_in + si + n_out + so]
        scratch, send, recv = refs[n_in + si + n_out + so:-2], refs[-2], refs[-1]

        @pl.when(at([0] * len(n_steps)))
        def _():
            for cp in side.sends(s_ins, s_outs, send, recv):
                cp.start()

        body(*ins, *outs, *scratch)

        @pl.when(at([n - 1 for n in n_steps]))
        def _():
            for cp in side.recvs(s_ins, s_outs, send, recv):
                cp.wait_recv()
            for cp in side.sends(s_ins, s_outs, send, recv):
                cp.wait_send()

    return wrapped


def _side_specs(side):
    if side is None:
        return [], [], []
    hbm = pl.BlockSpec(memory_space=pl.ANY)
    return ([hbm] * len(side.inputs), [hbm] * len(side.out_shape),
            [pltpu.SemaphoreType.DMA((side.n_sems,)), pltpu.SemaphoreType.DMA((side.n_sems,))])


def _mixer_fwd(mode, qa, q_off, ka, k_off, va, v_off, *, cum_col=None, cum_row=None, side=None):
    s = qa.shape[0]
    nq = s // QB
    nhb, nblk, nsub, q_spec, k_spec, v_spec = _mixer_specs(mode, s, q_off, k_off, v_off)
    w = TQ * nsub
    softmax = mode in ("fox", "mla")

    def body(*refs):
        refs = list(refs)
        q_ref, k_ref, v_ref = refs[:3]
        refs = refs[3:]
        if mode == "fox":
            cc_ref, cr_ref = refs[:2]
            refs = refs[2:]
        o_ref = refs[0]
        st_ref = refs[1]
        p = pl.program_id(0)
        i = pl.program_id(1)
        nfull, dist, visible = _mixer_geometry(mode, i, nsub)
        lane = _lane((1, LANES))
        heads = [nhb * p + hh for hh in range(nhb)]
        wide = mode == "mla"
        q_scale = _QK_SCALE if mode in ("fox", "sb") else 1.0
        cols = [slice(hh * LANES, (hh + 1) * LANES) if wide else slice(None) for hh in range(nhb)]
        if wide:
            qs = [q_ref[:, cols[hh]] for hh in range(nhb)]
        else:
            qf = q_ref[...].astype(F32) * q_scale
            qs = [jnp.where((lane // HEAD) == hh, qf, 0.0).astype(BF16) for hh in range(nhb)]
        if mode == "fox":
            cqs = [_head_pick(cc_ref[...], h, 1) for h in heads]
        if mode == "sb":
            r1 = lax.broadcasted_iota(jnp.int32, (TQ, TQ), 0)
            c1 = lax.broadcasted_iota(jnp.int32, (TQ, TQ), 1)
            u_after = (r1 > c1).astype(BF16)

        def chunk(c):
            return pl.ds(pl.multiple_of(c * w, w), w)

        def scores(c):
            js = chunk(c)
            return tuple(_dot_nt(qs[hh], k_ref[js, cols[hh]]) for hh in range(nhb))

        def head_step(hh, c, js, sc, vj, carry, last):
            if softmax:
                m, l, acc = carry
                if mode == "fox":
                    ck = _head_pick(cr_ref[:, js], heads[hh], 0)
                    sc = sc + (cqs[hh] - ck)
                else:
                    sc = sc * _MLA_SCALE
                if last:
                    sc = jnp.where(visible(c), sc, NEG)
                m_new = jnp.maximum(m, jnp.max(sc, axis=-1, keepdims=True))
                alpha = jnp.exp(m - m_new)
                pr = jnp.exp(sc - m_new)
                l = alpha * l + jnp.sum(pr, axis=-1, keepdims=True)
                acc = alpha * acc + _dot(pr.astype(BF16), vj)
                return m_new, l, acc
            run, acc = carry
            z = sc
            log_beta = jnp.minimum(z, 0.0) - jnp.log(1.0 + jnp.exp(-jnp.abs(z)))
            log_stay = log_beta - z
            if last:
                vis = visible(c)
                log_stay = jnp.where(vis, log_stay, 0.0)
            parts = [None] * nsub
            for b in reversed(range(nsub)):
                ls_b = log_stay[:, b * TQ:(b + 1) * TQ]
                parts[b] = _dot2(ls_b, u_after) + run
                run = run + jnp.sum(ls_b, axis=-1, keepdims=True)
            later = parts[0] if nsub == 1 else jnp.concatenate(parts, axis=1)
            wgt = jnp.exp(log_beta + later)
            if last:
                wgt = jnp.where(vis, wgt, 0.0)
            return run, acc + _dot(wgt.astype(BF16), vj)

        def step(c, c_next, state, last):
            scs, carries = state
            nxt = scores(c_next) if c_next is not None else None
            js = chunk(c)
            return nxt, tuple(head_step(hh, c, js, scs[hh], v_ref[js, cols[hh]], carries[hh], last) for hh in range(nhb))

        zero_acc = jnp.zeros((QB, LANES), F32)
        zero1 = jnp.zeros((QB, 1), F32)
        if softmax:
            init = tuple((jnp.full((QB, 1), NEG, F32), zero1, zero_acc) for _ in range(nhb))
        else:
            init = tuple((zero1, zero_acc) for _ in range(nhb))
        if mode == "sb":
            state = step(nfull, jnp.maximum(nfull - 1, 0), (scores(nfull), init), True)
            _, carries = lax.fori_loop(0, nfull, lambda t, st: step(nfull - 1 - t, jnp.maximum(nfull - 2 - t, 0), st, False), state)
        else:
            state = lax.fori_loop(0, nfull, lambda c, st: step(c, c + 1, st, False), (scores(0), init))
            _, carries = step(nfull, None, state, True)
        if softmax:
            outs = [acc / l for (m, l, acc) in carries]
            stats = [m + jnp.log(l) for (m, l, acc) in carries]
        else:
            outs, stats = [acc for (run, acc) in carries], [run for (run, acc) in carries]
        hm0 = (lane // HEAD) == 0
        pick = lambda a: jnp.where(hm0, a[0], a[1])
        if wide:
            for hh in range(nhb):
                o_ref[:, cols[hh]] = outs[hh]
        else:
            o_ref[...] = pick(outs)
        st_ref[0] = pick(stats)

    in_specs = [q_spec, k_spec, v_spec]
    args = [qa, ka, va]
    if mode == "fox":
        in_specs += [pl.BlockSpec((QB, LANES), lambda p, i: (i, 0)), pl.BlockSpec((8, s), lambda p, i: (0, 0))]
        args += [cum_col, cum_row]
    bw = 2 * LANES if mode == "mla" else LANES
    out_specs = [pl.BlockSpec((QB, bw), lambda p, i: (i, p))]
    out_shape = [jax.ShapeDtypeStruct((s, nblk * bw), F32)]
    out_specs.append(pl.BlockSpec((1, QB, LANES), lambda p, i: (p, i, 0)))
    out_shape.append(jax.ShapeDtypeStruct((nblk, s, LANES), F32))
    side_in, side_out, side_scratch = _side_specs(side)
    res = pl.pallas_call(
        _carry_side_job(body, len(args), len(out_shape), side, (nblk, nq)), name=mode + "_fwd", grid=(nblk, nq),
        in_specs=in_specs + side_in, out_specs=out_specs + side_out,
        out_shape=out_shape + ([] if side is None else side.out_shape), scratch_shapes=side_scratch,
        compiler_params=_cparams(("parallel", "parallel") if side is None else ("arbitrary", "arbitrary")),
    )(*args, *([] if side is None else side.inputs))
    return (res[0], res[1]) if side is None else (res[0], res[1], res[2:])


def _mixer_bwd(mode, qa, q_off, ka, k_off, va, v_off, o, do, *, stat=None, cum_col=None, cum_row=None, side=None):
    s = qa.shape[0]
    nq = s // QB
    nhb, nblk, nsub, q_spec, k_spec, v_spec = _mixer_specs(mode, s, q_off, k_off, v_off)
    w = TQ * nsub
    softmax = mode in ("fox", "mla")

    def body(*refs):
        refs = list(refs)
        q_ref, k_ref, v_ref, o_ref, do_ref = refs[:5]
        refs = refs[5:]
        st_ref = refs[0]
        refs = refs[1:]
        if mode == "fox":
            cc_ref, cr_ref = refs[:2]
            refs = refs[2:]
        dq_ref, dk_ref, dv_ref = refs[:3]
        dck_ref, drs_ref = refs[3:5] if mode == "fox" else (None, None)
        p = pl.program_id(0)
        i = pl.program_id(1)

        @pl.when(i == 0)
        def _():
            dk_ref[...] = jnp.zeros_like(dk_ref)
            dv_ref[...] = jnp.zeros_like(dv_ref)
            if mode == "fox":
                dck_ref[...] = jnp.zeros_like(dck_ref)

        nfull, dist, visible = _mixer_geometry(mode, i, nsub)
        lane = _lane((1, LANES))
        heads = [nhb * p + hh for hh in range(nhb)]
        dov = do_ref[...]
        wide = mode == "mla"
        q_scale = _QK_SCALE if mode in ("fox", "sb") else 1.0
        cols = [slice(hh * LANES, (hh + 1) * LANES) if wide else slice(None) for hh in range(nhb)]
        if wide:
            prod = dov * o_ref[...]
            qs = [q_ref[:, cols[hh]] for hh in range(nhb)]
            dos = [dov[:, cols[hh]].astype(BF16) for hh in range(nhb)]
            deltas = [jnp.sum(prod[:, cols[hh]], axis=-1, keepdims=True) for hh in range(nhb)]
        else:
            qf = q_ref[...].astype(F32) * q_scale
            prod = dov * o_ref[...]
            hms = [(lane // HEAD) == hh for hh in range(nhb)]
            qs = [jnp.where(hm, qf, 0.0).astype(BF16) for hm in hms]
            dos = [jnp.where(hm, dov, 0.0).astype(BF16) for hm in hms]
            deltas = [jnp.sum(jnp.where(hm, prod, 0.0), axis=-1, keepdims=True) for hm in hms]
        st = st_ref[0]
        stats = [st[:, hh * HEAD:hh * HEAD + 1] for hh in range(nhb)]
        if mode == "fox":
            cqs = [_head_pick(cc_ref[...], h, 1) for h in heads]
        if mode == "sb":
            r1 = lax.broadcasted_iota(jnp.int32, (TQ, TQ), 0)
            c1 = lax.broadcasted_iota(jnp.int32, (TQ, TQ), 1)
            u_upto = (r1 <= c1).astype(BF16)
            u_before = (r1 < c1).astype(BF16)

        def chunk(c):
            return pl.ds(pl.multiple_of(c * w, w), w)

        def scores(c):
            js = chunk(c)
            if mode == "sb":
                return tuple((_dot_nt(qs[hh], k_ref[js, cols[hh]]), None) for hh in range(nhb))
            return tuple((_dot_nt(qs[hh], k_ref[js, cols[hh]]), _dot_nt(dos[hh], v_ref[js, cols[hh]])) for hh in range(nhb))

        def emit(hh, js, ds_b, pr_b, dq):
            dk_ref[js, cols[hh]] += _dot_tn(ds_b, qs[hh])
            dv_ref[js, cols[hh]] += _dot_tn(pr_b, dos[hh])
            return dq + _dot(ds_b, k_ref[js, cols[hh]])

        def head_step(hh, c, js, sc_dp, carry, last):
            sc, dp = sc_dp
            if dp is None:
                dp = _dot_nt(dos[hh], v_ref[js, cols[hh]])
            if softmax:
                dq, rsum = carry
                if mode == "fox":
                    ck = _head_pick(cr_ref[:, js], heads[hh], 0)
                    sc = sc + (cqs[hh] - ck)
                else:
                    sc = sc * _MLA_SCALE
                if last:
                    sc = jnp.where(visible(c), sc, NEG)
                pr = jnp.exp(sc - stats[hh])
                ds = pr * (dp - deltas[hh])
                if mode == "fox":
                    dck_ref[0, hh:hh + 1, js] += jnp.sum(ds, axis=0, keepdims=True)
                    rsum = rsum + jnp.sum(ds, axis=-1, keepdims=True)
                if mode == "mla":
                    ds = ds * _MLA_SCALE
                return emit(hh, js, ds.astype(BF16), pr.astype(BF16), dq), rsum
            seen, gsum, dq = carry
            z = sc
            log_beta = jnp.minimum(z, 0.0) - jnp.log(1.0 + jnp.exp(-jnp.abs(z)))
            log_stay = log_beta - z
            if last:
                vis = visible(c)
                log_stay = jnp.where(vis, log_stay, 0.0)
            parts = []
            for b in range(nsub):
                ls_b = log_stay[:, b * TQ:(b + 1) * TQ]
                parts.append((stats[hh] - seen) - _dot2(ls_b, u_upto))
                seen = seen + jnp.sum(ls_b, axis=-1, keepdims=True)
            later = parts[0] if nsub == 1 else jnp.concatenate(parts, axis=1)
            wgt = jnp.exp(log_beta + later)
            if last:
                wgt = jnp.where(vis, wgt, 0.0)
            g = dp * wgt
            parts = []
            for b in range(nsub):
                g_b = g[:, b * TQ:(b + 1) * TQ]
                parts.append(gsum + _dot2(g_b, u_before))
                gsum = gsum + jnp.sum(g_b, axis=-1, keepdims=True)
            before = parts[0] if nsub == 1 else jnp.concatenate(parts, axis=1)
            beta = jnp.exp(log_beta)
            dz = g * (1.0 - beta) - beta * before
            if last:
                dz = jnp.where(vis, dz, 0.0)
            return seen, gsum, emit(hh, js, dz.astype(BF16), wgt.astype(BF16), dq)

        def step(c, c_next, state, last):
            scs, carries = state
            nxt = scores(c_next) if c_next is not None else None
            js = chunk(c)
            return nxt, tuple(head_step(hh, c, js, scs[hh], carries[hh], last) for hh in range(nhb))

        zero_acc = jnp.zeros((QB, LANES), F32)
        zero1 = jnp.zeros((QB, 1), F32)
        if softmax:
            init = tuple((zero_acc, zero1) for _ in range(nhb))
        else:
            init = tuple((zero1, zero1, zero_acc) for _ in range(nhb))
        state = lax.fori_loop(0, nfull, lambda c, st: step(c, c + 1, st, False), (scores(0), init))
        _, carries = step(nfull, None, state, True)
        if softmax:
            dqs = [dq for (dq, rsum) in carries]
        else:
            dqs = [dq for (seen, gsum, dq) in carries]
        hm0 = (lane // HEAD) == 0
        if wide:
            for hh in range(nhb):
                dq_ref[:, cols[hh]] = dqs[hh]
        else:
            dq_ref[...] = jnp.where(hm0, dqs[0], dqs[1]) * q_scale
        if mode == "fox":
            drs_ref[0] = jnp.where(hm0, carries[0][1], carries[1][1])

    bw = 2 * LANES if mode == "mla" else LANES
    pair_blk = pl.BlockSpec((QB, bw), lambda p, i: (i, p))
    full_blk = pl.BlockSpec((s, bw), lambda p, i: (0, p))
    stat_blk = pl.BlockSpec((1, QB, LANES), lambda p, i: (p, i, 0))
    in_specs = [q_spec, k_spec, v_spec, pair_blk, pair_blk]
    args = [qa, ka, va, o, do]
    in_specs.append(stat_blk)
    args.append(stat)
    if mode == "fox":
        in_specs += [pl.BlockSpec((QB, LANES), lambda p, i: (i, 0)), pl.BlockSpec((8, s), lambda p, i: (0, 0))]
        args += [cum_col, cum_row]
    out_specs = [pair_blk, full_blk, full_blk]
    out_shape = [jax.ShapeDtypeStruct((s, nblk * bw), F32)] * 3
    if mode == "fox":
        out_specs += [pl.BlockSpec((1, 8, s), lambda p, i: (p, 0, 0)), stat_blk]
        out_shape += [jax.ShapeDtypeStruct((2, 8, s), F32), jax.ShapeDtypeStruct((2, s, LANES), F32)]
    side_in, side_out, side_scratch = _side_specs(side)
    res = pl.pallas_call(
        _carry_side_job(body, len(args), len(out_shape), side, (nblk, nq)), name=mode + "_bwd", grid=(nblk, nq),
        in_specs=in_specs + side_in, out_specs=out_specs + side_out,
        out_shape=out_shape + ([] if side is None else side.out_shape), scratch_shapes=side_scratch,
        compiler_params=_cparams(("parallel", "arbitrary") if side is None else ("arbitrary", "arbitrary")),
    )(*args, *([] if side is None else side.inputs))
    return res if side is None else (*res[:len(out_shape)], res[len(out_shape):])


def _ret_geometry(p):
    lane = _lane((1, LANES))
    lg_lane = jnp.where(lane < HEAD, _log_gamma_of(2 * p), _log_gamma_of(2 * p + 1))
    a = lax.broadcasted_iota(jnp.int32, (TQ, 1), 0).astype(F32)
    row = lax.broadcasted_iota(jnp.int32, (TQ, TQ), 0)
    col = lax.broadcasted_iota(jnp.int32, (TQ, TQ), 1)
    same_chunk_or_earlier = (col // CHUNK) <= (row // CHUNK)
    gap = jnp.abs(row - col).astype(F32)
    decays = [jnp.where(same_chunk_or_earlier, jnp.exp(_log_gamma_of(2 * p + hh) * gap), 0.0) for hh in range(2)]
    r = lax.broadcasted_iota(jnp.int32, (LANES, LANES), 0)
    c = lax.broadcasted_iota(jnp.int32, (LANES, LANES), 1)
    own_head = (r // HEAD) == (c // HEAD)
    return lane, lg_lane, a, decays, own_head


def _ret_fwd(qa, ka, va, v_off):
    s = qa.shape[0]
    nq = s // TQ

    def body(q_ref, k_ref, v_ref, o_ref, st_ref, state):
        p = pl.program_id(0)

        @pl.when(pl.program_id(1) == 0)
        def _():
            state[...] = jnp.zeros_like(state)

        lane, lg_lane, a, decays, own_head = _ret_geometry(p)
        q = q_ref[...].astype(F32)
        k = k_ref[...]
        v = v_ref[...]
        s_in = state[...]
        st_ref[0, 0] = s_in
        out = _dot((q * jnp.exp(lg_lane * (a + 1.0))).astype(BF16), s_in.astype(BF16))
        for hh in range(2):
            hm = (lane // HEAD) == hh
            qh = jnp.where(hm, q, 0.0).astype(BF16)
            inner = _dot((_dot_nt(qh, k) * decays[hh]).astype(BF16), v)
            out = out + jnp.where(hm, inner, 0.0)
        o_ref[...] = out
        k_tail = (k.astype(F32) * jnp.exp(lg_lane * (TQ - 1.0 - a))).astype(BF16)
        state[...] = jnp.exp(lg_lane * float(TQ)) * s_in + jnp.where(own_head, _dot_tn(k_tail, v), 0.0)

    blk = lambda off: pl.BlockSpec((TQ, LANES), lambda p, i: (i, off + p))
    return pl.pallas_call(
        body, name="ret_fwd", grid=(2, nq), in_specs=[blk(0), blk(0), blk(v_off)],
        out_specs=[blk(0), pl.BlockSpec((1, 1, LANES, LANES), lambda p, i: (p, i, 0, 0))],
        out_shape=[jax.ShapeDtypeStruct((s, 2 * LANES), F32), jax.ShapeDtypeStruct((2, nq, LANES, LANES), F32)],
        scratch_shapes=[pltpu.VMEM((LANES, LANES), F32)],
        compiler_params=_cparams(("parallel", "arbitrary")),
    )(qa, ka, va)


def _ret_bwd(qa, ka, va, v_off, states, do):
    s = qa.shape[0]
    nq = s // TQ

    def body(q_ref, k_ref, v_ref, st_ref, do_ref, dq_ref, dk_ref, dv_ref, dstate):
        p = pl.program_id(0)

        @pl.when(pl.program_id(1) == 0)
        def _():
            dstate[...] = jnp.zeros_like(dstate)

        lane, lg_lane, a, decays, own_head = _ret_geometry(p)
        q = q_ref[...].astype(F32)
        k = k_ref[...]
        kf = k.astype(F32)
        v = v_ref[...]
        dov = do_ref[...]
        s_in = st_ref[0, 0].astype(BF16)
        ds_next = dstate[...]
        ds_b = ds_next.astype(BF16)
        head_decay = jnp.exp(lg_lane * (a + 1.0))
        tail_decay = jnp.exp(lg_lane * (TQ - 1.0 - a))
        k_tail = (kf * tail_decay).astype(BF16)
        dq = _dot_nt(dov.astype(BF16), s_in) * head_decay
        dk = _dot_nt(v, ds_b) * tail_decay
        dv = _dot(k_tail, ds_b)
        for hh in range(2):
            hm = (lane // HEAD) == hh
            qh = jnp.where(hm, q, 0.0).astype(BF16)
            doh = jnp.where(hm, dov, 0.0).astype(BF16)
            att = (_dot_nt(qh, k) * decays[hh]).astype(BF16)
            datt = (_dot_nt(doh, v) * decays[hh]).astype(BF16)
            dv = dv + _dot_tn(att, doh)
            dk = dk + _dot_tn(datt, qh)
            dq = dq + jnp.where(hm, _dot(datt, k), 0.0)
        dq_ref[...] = dq
        dk_ref[...] = dk
        dv_ref[...] = dv
        q_head = (q * head_decay).astype(BF16)
        dstate[...] = jnp.exp(lg_lane * float(TQ)) * ds_next + jnp.where(own_head, _dot_tn(q_head, dov.astype(BF16)), 0.0)

    blk = lambda off: pl.BlockSpec((TQ, LANES), lambda p, i: (nq - 1 - i, off + p))
    return pl.pallas_call(
        body, name="ret_bwd", grid=(2, nq),
        in_specs=[blk(0), blk(0), blk(v_off), pl.BlockSpec((1, 1, LANES, LANES), lambda p, i: (p, nq - 1 - i, 0, 0)), blk(0)],
        out_specs=[blk(0)] * 3, out_shape=[jax.ShapeDtypeStruct((s, 2 * LANES), F32)] * 3,
        scratch_shapes=[pltpu.VMEM((LANES, LANES), F32)],
        compiler_params=_cparams(("parallel", "arbitrary")),
    )(qa, ka, va, states, do)


def _seg_mean_matrix():
    r = lax.broadcasted_iota(jnp.int32, (GROUP, GROUP), 0)
    c = lax.broadcasted_iota(jnp.int32, (GROUP, GROUP), 1)
    return jnp.where((r // HEAD) == (c // HEAD), 1.0 / HEAD, 0.0).astype(BF16)


def _seg_mean(x, seg):
    h = x.astype(BF16)
    r = x - h.astype(F32)
    m = r.astype(BF16)
    lo = (r - m.astype(F32)).astype(BF16)
    return _dot(h, seg) + _dot(m, seg) + _dot(lo, seg)


def _sigmoid(x):
    return 1.0 / (1.0 + jnp.exp(-x))


def _mix_post(oa, ob, oc, od, proj, g):
    s = oa.shape[0]
    tr = _tile(s, 256)

    def body(a_ref, b_ref, c_ref, d_ref, rg_ref, g_ref, o_ref):
        gv = g_ref[...]
        o_ref[:, 0:GROUP] = _rms(a_ref[...], gv[:, 0:GROUP]).astype(BF16)
        o_ref[:, GROUP:2 * GROUP] = _rms(b_ref[...], gv[:, GROUP:2 * GROUP]).astype(BF16)
        seg = _seg_mean_matrix()
        c = c_ref[...]
        cen = c - _seg_mean(c, seg)
        n = cen * lax.rsqrt(_seg_mean(cen * cen, seg) + EPS)
        rg = rg_ref[...]
        o_ref[:, 2 * GROUP:3 * GROUP] = (n * gv[:, 2 * GROUP:3 * GROUP] * (rg * _sigmoid(rg))).astype(BF16)
        o_ref[:, 3 * GROUP:] = _rms(d_ref[...], gv[:, 3 * GROUP:]).astype(BF16)

    blk = pl.BlockSpec((tr, GROUP), lambda i: (i, 0))
    return pl.pallas_call(
        body, name="mix_post", grid=(s // tr,),
        in_specs=[blk] * 4 + [pl.BlockSpec((tr, GROUP), lambda i: (i, OFF_RG // 2)), pl.BlockSpec((1, D_MODEL), lambda i: (0, 0))],
        out_specs=pl.BlockSpec((tr, D_MODEL), lambda i: (i, 0)), out_shape=jax.ShapeDtypeStruct((s, D_MODEL), BF16),
        compiler_params=_cparams(("parallel",)),
    )(oa, ob, oc, od, proj, g.reshape(1, D_MODEL))


def _mix_post_bwd(dmixed, oa, ob, oc, od, proj, g):
    s = oa.shape[0]
    tr = _tile(s, 256)

    def body(dm_ref, a_ref, b_ref, c_ref, d_ref, rg_ref, g_ref, da_ref, db_ref, dc_ref, dd_ref, drg_ref, dg_ref):
        @pl.when(pl.program_id(0) == 0)
        def _():
            dg_ref[...] = jnp.zeros_like(dg_ref)

        gv = g_ref[...]
        dm = dm_ref[...]
        for k, (x_ref, dx_ref) in enumerate(((a_ref, da_ref), (b_ref, db_ref), (None, None), (d_ref, dd_ref))):
            if x_ref is None:
                continue
            cols = slice(k * GROUP, (k + 1) * GROUP)
            dx, gterm = _rms_bwd(x_ref[...], gv[:, cols], dm[:, cols])
            dx_ref[...] = dx
            dg_ref[:, cols] += jnp.sum(gterm, axis=0, keepdims=True)
        cols = slice(2 * GROUP, 3 * GROUP)
        seg = _seg_mean_matrix()
        c = c_ref[...]
        cen = c - _seg_mean(c, seg)
        rstd = lax.rsqrt(_seg_mean(cen * cen, seg) + EPS)
        n = cen * rstd
        rg = rg_ref[...]
        sg = _sigmoid(rg)
        gate = rg * sg
        dy = dm[:, cols]
        gc = gv[:, cols]
        dn = dy * gc * gate
        dg_ref[:, cols] += jnp.sum(dy * n * gate, axis=0, keepdims=True)
        drg_ref[...] = (dy * n * gc * (sg * (1.0 + rg * (1.0 - sg)))).astype(BF16)
        dc_ref[...] = rstd * (dn - _seg_mean(dn, seg) - n * _seg_mean(dn * n, seg))

    blk = pl.BlockSpec((tr, GROUP), lambda i: (i, 0))
    gsp = pl.BlockSpec((1, D_MODEL), lambda i: (0, 0))
    return pl.pallas_call(
        body, name="mix_post_bwd", grid=(s // tr,),
        in_specs=[pl.BlockSpec((tr, D_MODEL), lambda i: (i, 0))] + [blk] * 4 + [pl.BlockSpec((tr, GROUP), lambda i: (i, OFF_RG // 2)), gsp],
        out_specs=[blk] * 5 + [gsp],
        out_shape=[jax.ShapeDtypeStruct((s, GROUP), F32)] * 4 + [jax.ShapeDtypeStruct((s, GROUP), BF16), jax.ShapeDtypeStruct((1, D_MODEL), F32)],
        compiler_params=_cparams(("arbitrary",)),
    )(dmixed, oa, ob, oc, od, proj, g.reshape(1, D_MODEL))


def _pack_w_in(w):
    z = lambda n: jnp.zeros((w.shape[0], n), w.dtype)
    misc = jnp.concatenate([w[:, 768:772], z(KR_LANE - N_HEADS), w[:, 1156:1188], z(LANES - KR_LANE - ROPE_DIM)], axis=1)
    return jnp.concatenate([w[:, 0:768], w[:, 772:1028], w[:, 1188:2980], w[:, 1028:1156], misc], axis=1)


def _unpack_dw_in(d):
    m = OFF_MISC * LANES
    return jnp.concatenate([d[:, 0:768], d[:, m:m + N_HEADS], d[:, 768:1024], d[:, OFF_CKV * LANES:m],
                            d[:, m + KR_LANE:m + KR_LANE + ROPE_DIM], d[:, 1024:OFF_CKV * LANES]], axis=1)


def _pack_w_q(w):
    return jnp.pad(w.reshape(Q_RANK, N_HEADS, HEAD + ROPE_DIM), ((0, 0), (0, 0), (0, LANES - HEAD - ROPE_DIM))).reshape(Q_RANK, 4 * LANES)


def _unpack_dw_q(d):
    return d.reshape(Q_RANK, N_HEADS, LANES)[:, :, :HEAD + ROPE_DIM].reshape(Q_RANK, N_HEADS * (HEAD + ROPE_DIM))


def _pack_w_kv(w):
    w4 = w.reshape(KV_RANK, N_HEADS, 2 * HEAD)
    widen = lambda a: jnp.pad(a, ((0, 0), (0, 0), (0, LANES - HEAD))).reshape(KV_RANK, N_HEADS * LANES)
    return widen(w4[:, :, :HEAD]), widen(w4[:, :, HEAD:])


def _unpack_dw_kv(dk, dv):
    narrow = lambda a: a.reshape(KV_RANK, N_HEADS, LANES)[:, :, :HEAD]
    return jnp.concatenate([narrow(dk), narrow(dv)], axis=2).reshape(KV_RANK, 2 * N_HEADS * HEAD)


def _narrow_heads(a):
    return a.reshape(a.shape[0], N_HEADS, LANES)[:, :, :HEAD].reshape(a.shape[0], N_HEADS * HEAD)


def _widen_heads(a):
    return jnp.pad(a.reshape(a.shape[0], N_HEADS, HEAD), ((0, 0), (0, 0), (0, LANES - HEAD))).reshape(a.shape[0], N_HEADS * LANES)


def _layer_fwd(x, lw, tabs, tag, side=None, fox_side=None, late_weights=None, h1=None, next_gain=None):
    cos_m, sin_m, cos_r, sin_r = tabs
    if h1 is None:
        h1 = _norm_fwd(x, lw["g_mix_pre"], name=tag + "pre_norm")
    proj, projb = _matmul(h1, lw["w_in"], name=tag + "in_proj", also_bf16=True)
    bias_row = jnp.pad(lw["b_forget"], (FF_LANE, LANES - N_HEADS - FF_LANE)).reshape(1, LANES)
    cum_col, cum_row = _fox_cum(proj, bias_row)
    oa, lse_a, *fox_carried = _mixer_fwd("fox", projb, OFF_FQ, projb, OFF_FK, projb, OFF_FV, cum_col=cum_col, cum_row=cum_row,
                                         side=fox_side)
    if late_weights is not None:
        lw = {**lw, **late_weights(fox_carried[0])}
    qm, km, vm, cqn, ckvn = _mla_prep(proj, cos_m, sin_m, lw["g_q_lora"], lw["g_kv_lora"], lw["wq"], lw["wk"], lw["wv"])
    ob_wide, lse_b = _mixer_fwd("mla", qm, 0, km, 0, vm, 0)
    ob = _narrow_heads(ob_wide)
    qr, kr = _ret_prep(proj, cos_r, sin_r)
    oc, ret_states = _ret_fwd(qr, kr, projb, OFF_RV)
    od, tot_d, *carried = _mixer_fwd("sb", projb, OFF_SQ, projb, OFF_SK, projb, OFF_SV, side=side)
    mixed = _mix_post(oa, ob, oc, od, proj, lw["g_mix_out"])
    mix = _matmul(mixed, lw["w_out"], name=tag + "out_proj")
    x1, h2 = _norm_fwd(mix, lw["g_mix_post"], name=tag + "mix_post_norm", resid=x, out_dtype=F32, next_gain=lw["g_ffn_pre"])
    u = _matmul(h2, lw["w_ffn_up"], name=tag + "ffn_up", relu2=True, out_dtype=BF16, col_blocks=True)
    f = _matmul(u, lw["w_ffn_down"], name=tag + "ffn_down")
    x2, h_next = None, None
    if next_gain is not None:
        x2, h_next = _norm_fwd(f, lw["g_ffn_post"], name=tag + "ffn_post_norm", resid=x1, out_dtype=F32, next_gain=next_gain)
    saved = dict(x=x, h1=h1, proj=proj, projb=projb, bias_row=bias_row, cum_col=cum_col, cum_row=cum_row, oa=oa, lse_a=lse_a,
                 qm=qm, km=km, vm=vm, cqn=cqn, ckvn=ckvn, ob=ob, ob_wide=ob_wide, lse_b=lse_b, qr=qr, kr=kr, ret_states=ret_states, oc=oc, od=od, tot_d=tot_d, mixed=mixed,
                 mix=mix, x1=x1, h2=h2, u=u, f=f)
    return x2, saved, lw, (carried[0] if carried else None), h_next


def _layer_bwd(dx2, lw, sv, tabs, tag, side=None, ffn_side=None, fox_side=None, post_given=None, then_prev=None, in_side=None):
    cos_m, sin_m, cos_r, sin_r = tabs
    g = {}
    if post_given is None:
        df, g["g_ffn_post"] = _norm_bwd(sv["f"], lw["g_ffn_post"], dx2, name=tag + "ffn_post_norm_bwd", out_dtype=BF16)
    else:
        df, g["g_ffn_post"] = post_given
    du_pre = _matmul(df, lw["w_ffn_down"], name=tag + "ffn_down_dx", tb=True, out_dtype=BF16, relu2_of=sv["u"], side=ffn_side)
    ffn_carried = None
    if ffn_side is not None:
        du_pre, ffn_carried = du_pre
    g["w_ffn_down"] = _matmul(sv["u"], df, name=tag + "ffn_down_dw", ta=True)
    dh2 = _matmul(du_pre, lw["w_ffn_up"], name=tag + "ffn_up_dx", tb=True, col_blocks=True)
    g["w_ffn_up"] = _matmul(sv["h2"], du_pre, name=tag + "ffn_up_dw", ta=True, col_blocks=True)
    dx1, g["g_ffn_pre"], dmix, g["g_mix_post"] = _norm_bwd(sv["x1"], lw["g_ffn_pre"], dh2, name=tag + "ffn_pre_norm_bwd", add=dx2,
                                                           then=(sv["mix"], lw["g_mix_post"]))
    dmixed = _matmul(dmix, lw["w_out"], name=tag + "out_proj_dx", tb=True)
    g["w_out"] = _matmul(sv["mixed"], dmix, name=tag + "out_proj_dw", ta=True)
    proj, projb = sv["proj"], sv["projb"]
    doa, dob, doc, dod, drg, g["g_mix_out"] = _mix_post_bwd(dmixed, sv["oa"], sv["ob"], sv["oc"], sv["od"], proj, lw["g_mix_out"])
    dfq, dfk, dfv, dck, drs, *fox_carried = _mixer_bwd(
        "fox", projb, OFF_FQ, projb, OFF_FK, projb, OFF_FV, sv["oa"], doa, stat=sv["lse_a"], cum_col=sv["cum_col"],
        cum_row=sv["cum_row"], side=None if fox_side is None else fox_side(g))
    dqm, dkm, dvm = _mixer_bwd("mla", sv["qm"], 0, sv["km"], 0, sv["vm"], 0, sv["ob_wide"], _widen_heads(dob), stat=sv["lse_b"])
    dcq, dckv, dkr, dwq, dwk, dwv, g["g_q_lora"], g["g_kv_lora"] = _mla_prep_bwd(
        dqm, dkm, dvm, proj, sv["cqn"], sv["ckvn"], cos_m, sin_m, lw["g_q_lora"], lw["g_kv_lora"], lw["wq"], lw["wk"], lw["wv"])
    dqr, dkr_ret, drv = _ret_bwd(sv["qr"], sv["kr"], projb, OFF_RV, sv["ret_states"], doc)
    drq, drk = _ret_prep_bwd(dqr, dkr_ret, cos_r, sin_r)
    if callable(side):
        side = side(g, ffn_carried, fox_carried[0] if fox_carried else None)
    dsq, dsk, dsv, *carried = _mixer_bwd("sb", projb, OFF_SQ, projb, OFF_SK, projb, OFF_SV, sv["od"], dod, stat=sv["tot_d"], side=side)
    dmisc, db_row = _fox_gate_bwd(dck, drs, proj, sv["bias_row"], dkr)
    b = lambda a: a.astype(BF16)
    dproj = jnp.concatenate([b(dfq), b(dfk), b(dfv), dcq, drq, drk, b(drv), drg, b(dsq), b(dsk), b(dsv), dckv, dmisc], axis=1)
    g["w_in"] = _matmul(sv["h1"], dproj, name=tag + "in_proj_dw", ta=True)
    g["wq"], g["wk"], g["wv"] = dwq, dwk, dwv
    in_job = None if in_side is None else in_side(g)
    dh1 = _matmul(dproj, lw["w_in"], name=tag + "in_proj_dx", tb=True, side=in_job)
    in_carried = None
    if in_job is not None:
        dh1, in_carried = dh1
    dx, g["g_mix_pre"], *prev_post = _norm_bwd(sv["x"], lw["g_mix_pre"], dh1, name=tag + "pre_norm_bwd", add=dx1, then=then_prev)
    g["b_forget"] = db_row[0, FF_LANE:FF_LANE + N_HEADS]
    return dx, g, (carried[0] if carried else None), (tuple(prev_post) if prev_post else None), in_carried


def _local_step(x, positions, layers, target):
    s = x.shape[0]
    tabs = _rope_tables(positions.reshape(s, 1))
    saved, h1 = [], None
    for li, lw in enumerate(layers):
        nxt = layers[li + 1]["g_mix_pre"] if li + 1 < len(layers) else None
        x, sv, _, _, h1 = _layer_fwd(x, lw, tabs, "l%d_" % li, h1=h1, next_gain=nxt)
        saved.append(sv)
    loss_row, dx, df, dg = _loss_head(saved[-1]["f"], layers[-1]["g_ffn_post"], saved[-1]["x1"], target)
    grads, post = [None] * len(layers), (df, dg)
    for li in reversed(range(len(layers))):
        prev = (saved[li - 1]["f"], layers[li - 1]["g_ffn_post"]) if li > 0 else None
        dx, grads[li], _, post, _ = _layer_bwd(dx, layers[li], saved[li], tabs, "l%d_" % li, post_given=post, then_prev=prev)
    return loss_row[0, 0], dx, grads


def _adamw(w, g, m, v, *, name):
    d, r, c = w.shape
    tr = 256 if r % 256 == 0 else r
    blk = pl.BlockSpec((None, tr, c), lambda l, i: (l, i, 0))
    c1 = 1.0 - ADAM_B1 ** ADAM_STEP
    c2 = 1.0 - ADAM_B2 ** ADAM_STEP

    def body(w_ref, g_ref, m_ref, v_ref, d_ref, mo_ref, vo_ref):
        gv = g_ref[...]
        mn = ADAM_B1 * m_ref[...] + (1.0 - ADAM_B1) * gv
        vn = ADAM_B2 * v_ref[...] + (1.0 - ADAM_B2) * jnp.square(gv)
        mo_ref[...] = mn
        vo_ref[...] = vn
        d_ref[...] = -ADAM_LR * ((mn / c1) / (jnp.sqrt(vn / c2) + ADAM_EPS) + ADAM_WD * w_ref[...])

    return pl.pallas_call(
        body, name=name, grid=(d, r // tr), in_specs=[blk] * 4, out_specs=[blk] * 3,
        out_shape=[jax.ShapeDtypeStruct((d, r, c), F32)] * 3, compiler_params=_cparams(("parallel", "parallel")),
    )(w, g, m, v)


def _adamw_lead(w, g, m, v, *, name, steps):
    a, b, c = w.shape
    blk = pl.BlockSpec((a // steps, b, c), lambda i: (i, 0, 0))
    c1 = 1.0 - ADAM_B1 ** ADAM_STEP
    c2 = 1.0 - ADAM_B2 ** ADAM_STEP

    def body(w_ref, g_ref, m_ref, v_ref, d_ref, mo_ref, vo_ref):
        gv = g_ref[...]
        mn = ADAM_B1 * m_ref[...] + (1.0 - ADAM_B1) * gv
        vn = ADAM_B2 * v_ref[...] + (1.0 - ADAM_B2) * jnp.square(gv)
        mo_ref[...] = mn
        vo_ref[...] = vn
        d_ref[...] = -ADAM_LR * ((mn / c1) / (jnp.sqrt(vn / c2) + ADAM_EPS) + ADAM_WD * w_ref[...])

    return pl.pallas_call(
        body, name=name, grid=(steps,), in_specs=[blk] * 4, out_specs=[blk] * 3,
        out_shape=[jax.ShapeDtypeStruct((a, b, c), F32)] * 3, compiler_params=_cparams(("parallel",)),
    )(w, g, m, v)


SC_TILES = 32
SC_ROWS = 8


def _adamw_sparsecore(ws, gs, ms, vs, *, name):
    n = len(ws)
    c = ws[0].shape[2]
    c1 = 1.0 - ADAM_B1 ** ADAM_STEP
    c2 = 1.0 - ADAM_B2 ** ADAM_STEP
    pieces = [(t, l, r0) for t in range(n) for l in range(ws[t].shape[0]) for r0 in range(0, ws[t].shape[1] // SC_TILES, SC_ROWS)]

    def body(*refs):
        ins, outs = refs[:4 * n], refs[4 * n:7 * n]
        bufs, sem_in, sem_out = refs[7 * n:7 * n + 8], refs[7 * n + 8], refs[7 * n + 9]
        tile = lax.axis_index("subcore") * 2 + lax.axis_index("core")

        def window(k):
            t, l, r0 = pieces[k]
            return t, (l, pl.ds(tile * (ws[t].shape[1] // SC_TILES) + r0, SC_ROWS))

        def loads(k):
            t, at = window(k)
            return [pltpu.make_async_copy(ins[j * n + t].at[at], bufs[4 * (k % 2) + j], sem_in.at[k % 2]) for j in range(4)]

        def stores(k):
            t, at = window(k)
            return [pltpu.make_async_copy(bufs[4 * (k % 2) + j], outs[(j - 1) * n + t].at[at], sem_out.at[k % 2]) for j in (1, 2, 3)]

        def update(k):
            gb, wb, mb, vb = bufs[4 * (k % 2):4 * (k % 2) + 4]

            def adam(gv, wv, mv, vv):
                mn = ADAM_B1 * mv + (1.0 - ADAM_B1) * gv
                vn = ADAM_B2 * vv + (1.0 - ADAM_B2) * (gv * gv)
                return -ADAM_LR * ((mn / c1) / (jnp.sqrt(vn / c2) + ADAM_EPS) + ADAM_WD * wv), mn, vn

            @pl.loop(0, SC_ROWS)
            def _(rr):
                last = (rr, pl.ds(c - 16, 16))
                if c % 16:
                    end = adam(gb[last], wb[last], mb[last], vb[last])

                @pl.loop(0, c // 16 * 16, step=16)
                def _(i):
                    s = (rr, pl.ds(i, 16))
                    wb[s], mb[s], vb[s] = adam(gb[s], wb[s], mb[s], vb[s])

                if c % 16:
                    wb[last], mb[last], vb[last] = end

        for cp in loads(0):
            cp.start()
        for k in range(len(pieces)):
            if k + 1 < len(pieces):
                if k >= 1:
                    for cp in stores(k - 1):
                        cp.wait()
                for cp in loads(k + 1):
                    cp.start()
            for cp in loads(k):
                cp.wait()
            update(k)
            for cp in stores(k):
                cp.start()
        for k in range(max(len(pieces) - 2, 0), len(pieces)):
            for cp in stores(k):
                cp.wait()

    out = pl.kernel(
        body, name=name, out_type=[jax.ShapeDtypeStruct(t.shape, F32) for t in ws] * 3,
        mesh=plsc.VectorSubcoreMesh(core_axis_name="core", subcore_axis_name="subcore"),
        scratch_types=[pltpu.VMEM((SC_ROWS, c), F32)] * 8 + [pltpu.SemaphoreType.DMA((2,)), pltpu.SemaphoreType.DMA((2,))],
    )(*gs, *ws, *ms, *vs)
    return out[:n], out[n:2 * n], out[2 * n:]


BIG = ("w_in", "w_q_up", "w_kv_up", "w_out", "w_ffn_up", "w_ffn_down")
SMALL = ("g_mix_pre", "b_forget", "g_q_lora", "g_kv_lora", "g_mix_out", "g_mix_post", "g_ffn_pre", "g_ffn_post")
N_CHIPS = 4
ANY = pl.BlockSpec(memory_space=pl.ANY)


def _mesh_pos():
    return lax.axis_index("x"), lax.axis_index("y"), lax.axis_index("c")


def _other_chips(x, y):
    return [(1 - x, y), (x, 1 - y), (1 - x, 1 - y)]


def _rows_half(ref, half):
    h = ref.shape[-2] // 2
    return ref.at[(slice(None),) * (len(ref.shape) - 2) + (pl.ds(half * h, h), slice(None))]


def _remote(src, dst, send_sem, recv_sem, device):
    return pltpu.make_async_remote_copy(src_ref=src, dst_ref=dst, send_sem=send_sem, recv_sem=recv_sem, device_id=device,
                                        device_id_type=MESH)


def _comm_call(body, name, args, out_shape, n_sems):
    return pl.pallas_call(
        body, name=name, in_specs=[ANY] * len(args), out_specs=[ANY] * len(out_shape), out_shape=out_shape,
        scratch_shapes=[pltpu.SemaphoreType.DMA((n_sems,)), pltpu.SemaphoreType.DMA((n_sems,))],
        compiler_params=pltpu.CompilerParams(has_side_effects=True),
    )(*args)


def _run_side_job(side, name):
    si = len(side.inputs)

    def body(*refs):
        args = (refs[:si], refs[si:-2], refs[-2], refs[-1])
        sends = side.sends(*args)
        for cp in sends:
            cp.start()
        for cp in side.recvs(*args):
            cp.wait_recv()
        for cp in sends:
            cp.wait_send()

    return _comm_call(body, name, side.inputs, side.out_shape, side.n_sems)


def _gather_job(shards):
    n = len(shards)

    def copies(own_block, ins, outs, send_sems, recv_sems):
        x, y, c = _mesh_pos()
        return [_remote(_rows_half(ins[t], c), _rows_half(outs[t].at[2 * x + y if own_block else 2 * px + py], c),
                        send_sems.at[3 * t + j], recv_sems.at[3 * t + j], (px, py, c))
                for t in range(n) for j, (px, py) in enumerate(_other_chips(x, y))]

    return _SideJob(shards, [jax.ShapeDtypeStruct((N_CHIPS,) + a.shape, a.dtype) for a in shards], 3 * n,
                    functools.partial(copies, True), functools.partial(copies, False))


def _forward_halves(gathered):
    n = len(gathered)

    def body(*refs):
        bufs, send_sems, recv_sems = refs[n:2 * n], refs[-2], refs[-1]
        x, y, c = _mesh_pos()

        def d2d(t, j, block, half):
            region = _rows_half(bufs[t].at[block], half)
            return _remote(region, region, send_sems.at[3 * t + j], recv_sems.at[3 * t + j], (x, y, 1 - c))

        peers = list(enumerate(_other_chips(x, y)))
        sends = [d2d(t, j, 2 * px + py, c) for t in range(n) for j, (px, py) in peers]
        for cp in sends:
            cp.start()
        for t in range(n):
            for j, (px, py) in peers:
                d2d(t, j, 2 * px + py, 1 - c).wait_recv()
        for cp in sends:
            cp.wait_send()

    return pl.pallas_call(
        body, name="gather_forward", in_specs=[ANY] * n, out_specs=[ANY] * n,
        out_shape=[jax.ShapeDtypeStruct(g.shape, g.dtype) for g in gathered], input_output_aliases={t: t for t in range(n)},
        scratch_shapes=[pltpu.SemaphoreType.DMA((3 * n,)), pltpu.SemaphoreType.DMA((3 * n,))],
        compiler_params=pltpu.CompilerParams(has_side_effects=True),
    )(*gathered)


def _exchange_halves_job(gs):
    n = len(gs)

    def copies(ins, outs, send_sems, recv_sems):
        x, y, c = _mesh_pos()
        return [_remote(_rows_half(ins[t], 1 - c), outs[t], send_sems.at[t], recv_sems.at[t], (x, y, 1 - c)) for t in range(n)]

    out_shape = [jax.ShapeDtypeStruct(g.shape[:2] + (g.shape[2] // 2, g.shape[3]), g.dtype) for g in gs]
    return _SideJob(gs, out_shape, n, copies, copies)


def _pair_add(g, r, c_idx, *, name):
    nb, d, rows, cols = g.shape
    h = rows // 2
    tr = min(h, 512)
    nt = h // tr

    def body(c_ref, g_ref, r_ref, p_ref, pb_ref):
        s = g_ref[...] + r_ref[...]
        p_ref[...] = s
        pb_ref[...] = s.astype(BF16)

    blk = pl.BlockSpec((1, 1, tr, cols), lambda k, l, i, c_ref: (k, l, i, 0))
    return pl.pallas_call(
        body, name=name,
        grid_spec=pltpu.PrefetchScalarGridSpec(
            num_scalar_prefetch=1, grid=(nb, d, nt),
            in_specs=[pl.BlockSpec((1, 1, tr, cols), lambda k, l, i, c_ref: (k, l, c_ref[0] * nt + i, 0)), blk],
            out_specs=[blk, blk]),
        out_shape=[jax.ShapeDtypeStruct((nb, d, h, cols), F32), jax.ShapeDtypeStruct((nb, d, h, cols), BF16)],
        compiler_params=_cparams(("parallel", "parallel", "parallel")),
    )(c_idx, g, r)


def _exchange_chips_job(pbs):
    n = len(pbs)

    def copies(ins, outs, send_sems, recv_sems):
        x, y, c = _mesh_pos()
        return [_remote(ins[t].at[2 * px + py], outs[t].at[j], send_sems.at[3 * t + j], recv_sems.at[3 * t + j], (px, py, c))
                for t in range(n) for j, (px, py) in enumerate(_other_chips(x, y))]

    return _SideJob(pbs, [jax.ShapeDtypeStruct((3,) + p.shape[1:], p.dtype) for p in pbs], 3 * n, copies, copies)


def _chip_add(p, r, k_idx, *, name):
    _, d, h, cols = p.shape
    tr = min(h, 512)
    nt = h // tr

    def body(k_ref, p_ref, r_ref, o_ref):
        o_ref[0] = ((p_ref[0, 0] + r_ref[0, 0].astype(F32)) + r_ref[1, 0].astype(F32)) + r_ref[2, 0].astype(F32)

    return pl.pallas_call(
        body, name=name,
        grid_spec=pltpu.PrefetchScalarGridSpec(
            num_scalar_prefetch=1, grid=(d, nt),
            in_specs=[pl.BlockSpec((1, 1, tr, cols), lambda l, i, k_ref: (k_ref[0], l, i, 0)),
                      pl.BlockSpec((3, 1, tr, cols), lambda l, i, k_ref: (0, l, i, 0))],
            out_specs=pl.BlockSpec((1, tr, cols), lambda l, i, k_ref: (l, i, 0))),
        out_shape=jax.ShapeDtypeStruct((d, h, cols), F32), compiler_params=_cparams(("parallel", "parallel")),
    )(k_idx, p, r)


def _share_halves(qs):
    n = len(qs)

    def body(*refs):
        ins, outs, send_sems, recv_sems = refs[:n], refs[n:2 * n], refs[2 * n], refs[2 * n + 1]
        x, y, c = _mesh_pos()
        cps = [_remote(ins[t], outs[t], send_sems.at[t], recv_sems.at[t], (x, y, 1 - c)) for t in range(n)]
        for cp in cps:
            cp.start()
        for cp in cps:
            cp.wait_recv()
        for cp in cps:
            cp.wait_send()

    return _comm_call(body, "grad_pair_share", qs, [jax.ShapeDtypeStruct(q.shape, q.dtype) for q in qs], n)


def _all_reduce_small(v):
    r, cols = v.shape
    n_dev = 8

    def body(v_ref, o_ref, buf, send_sems, recv_sems):
        x, y, c = _mesh_pos()
        me = 4 * x + 2 * y + c
        buf[me] = v_ref[...]

        def peer(j):
            return (1 - x if j & 4 else x, 1 - y if j & 2 else y, 1 - c if j & 1 else c)

        def copy(j, slot):
            return pltpu.make_async_remote_copy(src_ref=v_ref, dst_ref=buf.at[slot], send_sem=send_sems.at[j - 1],
                                                recv_sem=recv_sems.at[j - 1], device_id=peer(j), device_id_type=MESH)

        sends = [copy(j, me) for j in range(1, n_dev)]
        for cp in sends:
            cp.start()
        for j in range(1, n_dev):
            px, py, pc = peer(j)
            copy(j, 4 * px + 2 * py + pc).wait_recv()
        for cp in sends:
            cp.wait_send()
        acc = buf[0]
        for d in range(1, n_dev):
            acc = acc + buf[d]
        o_ref[...] = acc

    vm = pl.BlockSpec(memory_space=pltpu.VMEM)
    return pl.pallas_call(
        body, name="small_all_reduce", in_specs=[vm], out_specs=vm, out_shape=jax.ShapeDtypeStruct((r, cols), F32),
        scratch_shapes=[pltpu.VMEM((n_dev, r, cols), F32), pltpu.SemaphoreType.DMA((n_dev - 1,)), pltpu.SemaphoreType.DMA((n_dev - 1,))],
        compiler_params=pltpu.CompilerParams(has_side_effects=True),
    )(v)


_COL_SHARDED = ("w_in", "w_q_up", "w_kv_up", "w_ffn_up")


def _shard_cols(blocks, a, b):
    c = blocks[0].shape[-1]
    out = []
    while a < b:
        k = a // c
        hi = min(b, (k + 1) * c)
        out.append(blocks[k][:, a - k * c:hi - k * c])
        a = hi
    return out


def _pack_w_in_shards(blocks):
    z = lambda n: [jnp.zeros((blocks[0].shape[0], n), blocks[0].dtype)]
    cols = lambda a, b: _shard_cols(blocks, a, b)
    return jnp.concatenate(cols(0, 768) + cols(772, 1028) + cols(1188, 2980) + cols(1028, 1156) + cols(768, 772)
                           + z(KR_LANE - N_HEADS) + cols(1156, 1188) + z(LANES - KR_LANE - ROPE_DIM), axis=1)


def _whole_layer(name, blocks):
    if name in _COL_SHARDED:
        return jnp.concatenate([blocks[k] for k in range(N_CHIPS)], axis=1)
    return blocks.reshape(N_CHIPS * blocks.shape[1], blocks.shape[2])


def _split_layer(name, whole):
    if name in _COL_SHARDED:
        c = whole.shape[1] // N_CHIPS
        return jnp.stack([whole[:, k * c:(k + 1) * c] for k in range(N_CHIPS)])
    return whole.reshape(N_CHIPS, whole.shape[0] // N_CHIPS, whole.shape[1])


def _small_to_rows(d):
    v = jnp.concatenate([d[k].astype(F32).reshape(-1) for k in SMALL])
    rows = -(-v.shape[0] // (8 * LANES)) * 8
    return jnp.pad(v, (0, rows * LANES - v.shape[0])).reshape(rows, LANES)


def _small_from_rows(rows, shapes):
    v = rows.reshape(-1)
    out, o = {}, 0
    for k in SMALL:
        sz = int(np.prod(shapes[k]))
        out[k] = v[o:o + sz].reshape(shapes[k])
        o += sz
    return out


_ARG_NAMES = ("x", "positions", "g_mix_pre", "w_in", "b_forget", "g_q_lora", "w_q_up", "g_kv_lora", "w_kv_up", "g_mix_out", "w_out",
              "g_mix_post", "g_ffn_pre", "w_ffn_up", "w_ffn_down", "g_ffn_post")
_WEIGHTS = _ARG_NAMES[2:]


def kernel(x, positions, g_mix_pre, w_in, b_forget, g_q_lora, w_q_up, g_kv_lora, w_kv_up, g_mix_out, w_out, g_mix_post, g_ffn_pre, w_ffn_up, w_ffn_down, g_ffn_post, loss_target, m_g_mix_pre, m_w_in, m_b_forget, m_g_q_lora, m_w_q_up, m_g_kv_lora, m_w_kv_up, m_g_mix_out, m_w_out, m_g_mix_post, m_g_ffn_pre, m_w_ffn_up, m_w_ffn_down, m_g_ffn_post, v_g_mix_pre, v_w_in, v_b_forget, v_g_q_lora, v_w_q_up, v_g_kv_lora, v_w_kv_up, v_g_mix_out, v_w_out, v_g_mix_post, v_g_ffn_pre, v_w_ffn_up, v_w_ffn_down, v_g_ffn_post):
    w = dict(g_mix_pre=g_mix_pre, w_in=w_in, b_forget=b_forget, g_q_lora=g_q_lora, w_q_up=w_q_up, g_kv_lora=g_kv_lora, w_kv_up=w_kv_up,
             g_mix_out=g_mix_out, w_out=w_out, g_mix_post=g_mix_post, g_ffn_pre=g_ffn_pre, w_ffn_up=w_ffn_up, w_ffn_down=w_ffn_down,
             g_ffn_post=g_ffn_post)
    m = dict(g_mix_pre=m_g_mix_pre, w_in=m_w_in, b_forget=m_b_forget, g_q_lora=m_g_q_lora, w_q_up=m_w_q_up, g_kv_lora=m_g_kv_lora,
             w_kv_up=m_w_kv_up, g_mix_out=m_g_mix_out, w_out=m_w_out, g_mix_post=m_g_mix_post, g_ffn_pre=m_g_ffn_pre,
             w_ffn_up=m_w_ffn_up, w_ffn_down=m_w_ffn_down, g_ffn_post=m_g_ffn_post)
    v = dict(g_mix_pre=v_g_mix_pre, w_in=v_w_in, b_forget=v_b_forget, g_q_lora=v_g_q_lora, w_q_up=v_w_q_up, g_kv_lora=v_g_kv_lora,
             w_kv_up=v_w_kv_up, g_mix_out=v_g_mix_out, w_out=v_w_out, g_mix_post=v_g_mix_post, g_ffn_pre=v_g_ffn_pre,
             w_ffn_up=v_w_ffn_up, w_ffn_down=v_w_ffn_down, g_ffn_post=v_g_ffn_post)
    small_shapes = {k: w[k].shape for k in SMALL}
    c_idx = lax.axis_index("c").astype(jnp.int32).reshape(1)
    k_idx = (2 * lax.axis_index("x") + lax.axis_index("y")).astype(jnp.int32).reshape(1)
    first_core = lax.axis_index("c") == 0

    mine = 2 * lax.axis_index("x") + lax.axis_index("y")
    shards_b = [{k: w[k][l:l + 1].astype(BF16) for k in BIG} for l in range(DEPTH)]
    gains = [dict(g_mix_pre=g_mix_pre[l], b_forget=b_forget[l], g_q_lora=g_q_lora[l], g_kv_lora=g_kv_lora[l], g_mix_out=g_mix_out[l],
                  g_mix_post=g_mix_post[l], g_ffn_pre=g_ffn_pre[l], g_ffn_post=g_ffn_post[l]) for l in range(DEPTH)]
    FIRST, LATER = ("w_in", "w_q_up", "w_kv_up"), ("w_out", "w_ffn_up", "w_ffn_down")
    EARLY_GRADS, LATE_GRADS = ("w_ffn_down", "w_ffn_up", "w_out"), ("w_in", "w_q_up", "w_kv_up")
    SC_ADAMW = EARLY_GRADS

    def gather_job(l, names):
        return _gather_job([shards_b[l][k] for k in names])

    def weights_of(l, names, gathered):
        four = {k: lax.dynamic_update_slice(g, shards_b[l][k][None], (mine, 0, 0, 0))[:, 0]
                for k, g in zip(names, _forward_halves(gathered))}
        out = {}
        for k in names:
            if k == "w_in":
                out["w_in"] = _pack_w_in_shards(four[k])
            elif k == "w_q_up":
                out["wq"] = _pack_w_q(_whole_layer(k, four[k]))
            elif k == "w_kv_up":
                out["wk"], out["wv"] = _pack_w_kv(_whole_layer(k, four[k]))
            elif k == "w_ffn_up":
                out[k] = four[k]
            else:
                out[k] = _whole_layer(k, four[k])
        return out

    def grad_blocks(names, g):
        whole = dict(w_in=lambda: _unpack_dw_in(g["w_in"]), w_q_up=lambda: _unpack_dw_q(g["wq"]),
                     w_kv_up=lambda: _unpack_dw_kv(g["wk"], g["wv"]), w_out=lambda: g["w_out"], w_ffn_down=lambda: g["w_ffn_down"])
        return [(g[k] if k == "w_ffn_up" else _split_layer(k, whole[k]()))[:, None] for k in names]

    def pair_sums(names, blocks, theirs):
        return [_pair_add(b, r, c_idx, name="grad_pair_add_" + k) for k, b, r in zip(names, blocks, theirs)]

    def exchange_job(*pairs):
        return _exchange_chips_job([pb for pair in pairs for (_, pb) in pair])

    def finish_grads(names, pair, partial):
        half = [_chip_add(p, r, k_idx, name="grad_chip_add_" + k) for k, (p, _), r in zip(names, pair, partial)]
        return {k: jnp.where(first_core, jnp.concatenate([q, s], axis=1), jnp.concatenate([s, q], axis=1))
                for k, q, s in zip(names, half, _share_halves(half))}

    seq = x.shape[1]
    tabs = _rope_tables(positions[0].reshape(seq, 1))
    first0 = weights_of(0, FIRST, _run_side_job(gather_job(0, FIRST), "gather_weights_l0"))
    x1, saved0, lw0, gathered1, h1 = _layer_fwd(x[0], {**gains[0], **first0}, tabs, "l0_", fox_side=gather_job(0, LATER),
                                                late_weights=lambda got: weights_of(0, LATER, got), side=gather_job(1, BIG),
                                                next_gain=gains[1]["g_mix_pre"])
    lw1 = {**gains[1], **weights_of(1, BIG, gathered1)}
    _, saved1, _, _, _ = _layer_fwd(x1, lw1, tabs, "l1_", h1=h1)
    loss_row, dx, df1, dg1 = _loss_head(saved1["f"], lw1["g_ffn_post"], saved1["x1"], loss_target[0])
    loss = lax.psum(loss_row[0, 0], ("x", "y", "c"))
    dx, grads1, _, post0, _ = _layer_bwd(dx, lw1, saved1, tabs, "l1_", post_given=(df1, dg1),
                                      then_prev=(saved0["f"], lw0["g_ffn_post"]))
    blocks1 = grad_blocks(BIG, grads1)
    early_blocks0, pair1, early0 = [], [], []

    def beside_l0_fox_backward(g):
        early_blocks0.extend(grad_blocks(EARLY_GRADS, g))
        return _exchange_halves_job(early_blocks0)

    def beside_l0_sb_backward(g, theirs1, theirs_early0):
        pair1.extend(pair_sums(BIG, blocks1, theirs1))
        early0.extend(pair_sums(EARLY_GRADS, early_blocks0, theirs_early0))
        return exchange_job(pair1, early0)

    late0 = []

    def beside_l0_in_proj_dx(g):
        late_blocks0 = grad_blocks(LATE_GRADS, g)
        late0.extend(pair_sums(LATE_GRADS, late_blocks0, _run_side_job(_exchange_halves_job(late_blocks0), "grad_pair_exchange_l0")))
        return exchange_job(late0)

    dx, grads0, partial, _, late_partial = _layer_bwd(dx, lw0, saved0, tabs, "l0_", ffn_side=_exchange_halves_job(blocks1),
                                                      fox_side=beside_l0_fox_backward, side=beside_l0_sb_backward, post_given=post0,
                                                      in_side=beside_l0_in_proj_dx)
    big1 = finish_grads(BIG, pair1, partial[:len(BIG)])
    big0 = finish_grads(EARLY_GRADS, early0, partial[len(BIG):])
    g_early = {k: jnp.concatenate([big0[k], big1[k]], axis=0) for k in SC_ADAMW}
    sc_delta, sc_m, sc_v = _adamw_sparsecore([w[k] for k in SC_ADAMW], [g_early[k] for k in SC_ADAMW], [m[k] for k in SC_ADAMW],
                                             [v[k] for k in SC_ADAMW], name="adamw_sparsecore")
    big0.update(finish_grads(LATE_GRADS, late0, late_partial))
    g_big = {k: g_early[k] if k in SC_ADAMW else jnp.concatenate([big0[k], big1[k]], axis=0) for k in BIG}
    w_in_lead = _adamw_lead(*[jnp.transpose(t["w_in"], (2, 0, 1)) for t in (w, g_big, m, v)], name="adamw_w_in", steps=5)
    sc_late = [[jnp.transpose(t, (1, 2, 0))] for t in w_in_lead]
    grads = [grads0, grads1]

    g_small_local = {k: jnp.stack([grads[l][k].reshape(small_shapes[k][1:]) for l in range(DEPTH)]) for k in SMALL}
    g_small = _small_from_rows(_all_reduce_small(_small_to_rows(g_small_local)), small_shapes)

    g_all = {**g_big, **g_small}
    delta, new_m, new_v = {}, {}, {}
    for k in BIG:
        if k in SC_ADAMW:
            i = SC_ADAMW.index(k)
            delta[k], new_m[k], new_v[k] = sc_delta[i], sc_m[i], sc_v[i]
        elif k == "w_in":
            delta[k], new_m[k], new_v[k] = [t[0] for t in sc_late]
        elif k == "w_q_up":
            out = _adamw(*[jnp.swapaxes(t[k], 1, 2) for t in (w, g_all, m, v)], name="adamw_" + k)
            delta[k], new_m[k], new_v[k] = [jnp.swapaxes(t, 1, 2) for t in out]
        else:
            delta[k], new_m[k], new_v[k] = _adamw(w[k], g_all[k], m[k], v[k], name="adamw_" + k)
    ds, ms, vs = _adamw(*[_small_to_rows(t)[None] for t in (w, g_small, m, v)], name="adamw_small")
    delta.update(_small_from_rows(ds, small_shapes))
    new_m.update(_small_from_rows(ms, small_shapes))
    new_v.update(_small_from_rows(vs, small_shapes))

    grad_x = dx.reshape(x.shape)
    return (loss, grad_x, *[g_all[k] for k in _WEIGHTS], *[delta[k] for k in _WEIGHTS], *[new_m[k] for k in _WEIGHTS],
            *[new_v[k] for k in _WEIGHTS])
```

```python
import functools
import math

import numpy as np
import jax
import jax.numpy as jnp
from jax import lax
from jax.experimental import pallas as pl
from jax.experimental.pallas import tpu as pltpu
from jax.experimental.pallas import tpu_sc as plsc

F32 = jnp.float32
BF16 = jnp.bfloat16
MESH = pl.DeviceIdType.MESH

D_MODEL = 1024
DEPTH = 2
CHUNK = 64
GROUP = 256
HEAD = 64
N_HEADS = 4
Q_RANK = 256
KV_RANK = 128
ROPE_DIM = 32
D_FF = 4096
D_IN = 2980
D_INP = 3072
ROPE_BASE = 10000.0
EPS = 1e-6
LANES = 128
TQ = 128
GATE_ROWS = 512
CONTRACT_TILE = 4096
NEG = -1e30

ADAM_LR, ADAM_B1, ADAM_B2, ADAM_EPS, ADAM_WD, ADAM_STEP = 0.001, 0.9, 0.999, 1e-08, 0.01, 10

OFF_FQ, OFF_FK, OFF_FV, OFF_CQ = 0, 2, 4, 6
OFF_RQ, OFF_RK, OFF_RV, OFF_RG = 8, 10, 12, 14
OFF_SQ, OFF_SK, OFF_SV = 16, 18, 20
OFF_CKV, OFF_MISC = 22, 23
FF_LANE, KR_LANE = 0, 64

VMEM_LIMIT = 56 * 1024 * 1024


def _tile(dim, pref):
    return pref if dim % pref == 0 else dim


def _cparams(sem, vmem=None):
    return pltpu.CompilerParams(dimension_semantics=sem, vmem_limit_bytes=vmem or VMEM_LIMIT)


def _dot(a, b):
    return jnp.dot(a, b, preferred_element_type=F32)


def _dot_nt(a, b):
    return lax.dot_general(a, b, (((1,), (1,)), ((), ())), preferred_element_type=F32)


def _dot_tn(a, b):
    return lax.dot_general(a, b, (((0,), (0,)), ((), ())), preferred_element_type=F32)


def _dot_exact(a, b):
    return jnp.dot(a, b, precision=lax.Precision.HIGHEST, preferred_element_type=F32)


def _matmul(a, b, *, name, ta=False, tb=False, out_dtype=F32, tm=1024, tn=1024, tk=CONTRACT_TILE,
            relu2=False, relu2_of=None, also_bf16=False, side=None, col_blocks=False):
    if ta:
        kdim, m = a.shape
    else:
        m, kdim = a.shape
    if col_blocks and not ta:
        n = b.shape[1] if tb else b.shape[0] * b.shape[2]
        if tb:
            kdim = b.shape[0] * b.shape[2]
    else:
        n = b.shape[0] if tb else b.shape[1]
    tm, tn, tk = _tile(m, tm), _tile(n, tn), _tile(kdim, tk)
    nk = kdim // tk
    a_spec = pl.BlockSpec((tk, tm), lambda i, j, k: (k, i)) if ta else pl.BlockSpec((tm, tk), lambda i, j, k: (i, k))
    b_spec = pl.BlockSpec((tn, tk), lambda i, j, k: (j, k)) if tb else pl.BlockSpec((tk, tn), lambda i, j, k: (k, j))
    o_spec = pl.BlockSpec((tm, tn), lambda i, j, k: (i, j))
    if col_blocks and ta:
        o_spec = pl.BlockSpec((None, tm, tn), lambda i, j, k: (j, i, 0))
    elif col_blocks and tb:
        assert tk == kdim
        b_spec = pl.BlockSpec((b.shape[0], tn, b.shape[2]), lambda i, j, k: (0, j, 0))
    elif col_blocks:
        assert b.shape[2] == tn
        b_spec = pl.BlockSpec((None, tk, tn), lambda i, j, k: (j, k, 0))
    two = also_bf16

    def body(*refs):
        refs = list(refs)
        a_ref, b_ref = refs[0], refs[1]
        e_ref = refs[2] if relu2_of is not None else None
        pos = 3 if relu2_of is not None else 2
        o_ref = refs[pos]
        o2_ref = refs[pos + 1] if two else None
        acc_ref = refs[-1]
        k = pl.program_id(2)
        av = a_ref[...].astype(BF16)
        if col_blocks and tb:
            bv = jnp.concatenate([b_ref[q] for q in range(b.shape[0])], axis=1).astype(BF16)
        else:
            bv = b_ref[...].astype(BF16)
        if ta:
            part = _dot_tn(av, bv)
        elif tb:
            part = _dot_nt(av, bv)
        else:
            part = _dot(av, bv)

        @pl.when(k == 0)
        def _():
            acc_ref[...] = part

        @pl.when(k > 0)
        def _():
            acc_ref[...] += part

        @pl.when(k == nk - 1)
        def _():
            r = acc_ref[...]
            if relu2_of is not None:
                r = r * (2.0 * jnp.sqrt(e_ref[...].astype(F32)))
            if relu2:
                r = jnp.square(jnp.maximum(r, 0.0))
            o_ref[...] = r.astype(o_ref.dtype)
            if also_bf16:
                o2_ref[...] = r.astype(BF16)

    in_specs = [a_spec, b_spec]
    args = [a, b]
    if relu2_of is not None:
        in_specs.append(o_spec)
        args.append(relu2_of)
    out_shape = [jax.ShapeDtypeStruct((n // tn, m, tn) if (col_blocks and ta) else (m, n), out_dtype)]
    out_specs = [o_spec]
    if two:
        out_shape.append(jax.ShapeDtypeStruct((m, n), BF16))
        out_specs.append(o_spec)
    grid = (m // tm, n // tn, nk)
    side_in, side_out, side_scratch = _side_specs(side)
    res = pl.pallas_call(
        _carry_side_job(body, len(args), len(out_shape), side, grid), name=name, grid=grid,
        in_specs=in_specs + side_in, out_specs=out_specs + side_out,
        out_shape=out_shape + ([] if side is None else side.out_shape),
        scratch_shapes=[pltpu.VMEM((tm, tn), F32)] + side_scratch,
        compiler_params=_cparams(("parallel", "parallel", "arbitrary") if side is None else ("arbitrary",) * 3),
    )(*args, *([] if side is None else side.inputs))
    main = res[:len(out_shape)]
    main = main if two else main[0]
    return main if side is None else (main, res[len(out_shape):])


def _rms(x, g):
    r = lax.rsqrt(jnp.mean(x * x, axis=-1, keepdims=True) + EPS)
    return x * r * g


def _rms_bwd(x, g, dy):
    r = lax.rsqrt(jnp.mean(x * x, axis=-1, keepdims=True) + EPS)
    xh = x * r
    gdy = dy * g
    dx = r * (gdy - xh * jnp.mean(xh * gdy, axis=-1, keepdims=True))
    return dx, xh * dy


def _norm_fwd(x, g, *, name, resid=None, out_dtype=BF16, next_gain=None):
    s, d = x.shape
    tr = _tile(s, 256)
    row = pl.BlockSpec((tr, d), lambda i: (i, 0))
    gsp = pl.BlockSpec((1, d), lambda i: (0, 0))

    def body(*refs):
        refs = list(refs)
        x_ref, g_ref = refs[:2]
        y = _rms(x_ref[...], g_ref[...])
        pos = 2
        if resid is not None:
            y = refs[pos][...] + y
            pos += 1
        if next_gain is None:
            refs[pos][...] = y.astype(refs[pos].dtype)
        else:
            refs[pos + 1][...] = y.astype(refs[pos + 1].dtype)
            refs[pos + 2][...] = _rms(y, refs[pos][...]).astype(BF16)

    args = [x, g.reshape(1, d)] + ([] if resid is None else [resid]) + ([] if next_gain is None else [next_gain.reshape(1, d)])
    in_specs = [row, gsp] + ([] if resid is None else [row]) + ([] if next_gain is None else [gsp])
    first = jax.ShapeDtypeStruct((s, d), out_dtype)
    if next_gain is None:
        out_specs, out_shape = row, first
    else:
        out_specs, out_shape = [row, row], [first, jax.ShapeDtypeStruct((s, d), BF16)]
    return pl.pallas_call(
        body, name=name, grid=(s // tr,), in_specs=in_specs, out_specs=out_specs, out_shape=out_shape,
        compiler_params=_cparams(("parallel",)),
    )(*args)


def _norm_bwd(x, g, dy, *, name, add=None, out_dtype=F32, then=None):
    s, d = x.shape
    tr = _tile(s, 256)
    row = pl.BlockSpec((tr, d), lambda i: (i, 0))
    gsp = pl.BlockSpec((1, d), lambda i: (0, 0))
    n_in = 3 + (add is not None) + (2 if then is not None else 0)

    def body(*refs):
        ins, outs = refs[:n_in], refs[n_in:]
        x_ref, g_ref, dy_ref = ins[:3]
        dx, gterm = _rms_bwd(x_ref[...], g_ref[...], dy_ref[...].astype(F32))
        if add is not None:
            dx = dx + ins[3][...]
        outs[0][...] = dx.astype(outs[0].dtype)
        terms = [(outs[1], gterm)]
        if then is not None:
            dx2, gterm2 = _rms_bwd(ins[-2][...], ins[-1][...], dx)
            outs[2][...] = dx2.astype(BF16)
            terms.append((outs[3], gterm2))

        @pl.when(pl.program_id(0) == 0)
        def _():
            for dg_ref, _ in terms:
                dg_ref[...] = jnp.zeros_like(dg_ref)

        for dg_ref, term in terms:
            dg_ref[...] += jnp.sum(term, axis=0, keepdims=True)

    args = [x, g.reshape(1, d), dy] + ([] if add is None else [add]) + ([] if then is None else [then[0], then[1].reshape(1, d)])
    in_specs = [row, gsp, row] + ([] if add is None else [row]) + ([] if then is None else [row, gsp])
    out_specs = [row, gsp] + ([] if then is None else [row, gsp])
    out_shape = [jax.ShapeDtypeStruct((s, d), out_dtype), jax.ShapeDtypeStruct((1, d), F32)]
    if then is not None:
        out_shape += [jax.ShapeDtypeStruct((s, d), BF16), jax.ShapeDtypeStruct((1, d), F32)]
    return pl.pallas_call(
        body, name=name, grid=(s // tr,), in_specs=in_specs, out_specs=out_specs, out_shape=out_shape,
        compiler_params=_cparams(("arbitrary",)),
    )(*args)


def _loss_head(f, g, resid, target):
    s, d = f.shape
    tr = _tile(s, 256)
    row = pl.BlockSpec((tr, d), lambda i: (i, 0))
    gsp = pl.BlockSpec((1, d), lambda i: (0, 0))
    lsp = pl.BlockSpec((1, LANES), lambda i: (0, 0))

    def body(f_ref, g_ref, r_ref, t_ref, l_ref, dy_ref, df_ref, dg_ref):
        fv, gv = f_ref[...], g_ref[...]
        e = (r_ref[...] + _rms(fv, gv)) - t_ref[...]
        dy = e * (1.0 / d)
        dy_ref[...] = dy
        df, gterm = _rms_bwd(fv, gv, dy)
        df_ref[...] = df.astype(BF16)

        @pl.when(pl.program_id(0) == 0)
        def _():
            l_ref[...] = jnp.zeros_like(l_ref)
            dg_ref[...] = jnp.zeros_like(dg_ref)

        part = 0.5 * jnp.sum(jnp.mean(e * e, axis=-1, keepdims=True), axis=0, keepdims=True)
        l_ref[...] += jnp.broadcast_to(part, (1, LANES))
        dg_ref[...] += jnp.sum(gterm, axis=0, keepdims=True)

    return pl.pallas_call(
        body, name="loss_head", grid=(s // tr,), in_specs=[row, gsp, row, row], out_specs=[lsp, row, row, gsp],
        out_shape=[jax.ShapeDtypeStruct((1, LANES), F32), jax.ShapeDtypeStruct((s, d), F32), jax.ShapeDtypeStruct((s, d), BF16),
                   jax.ShapeDtypeStruct((1, d), F32)],
        compiler_params=_cparams(("arbitrary",)),
    )(f, g.reshape(1, d), resid, target)


def _rope_tables(pos_col):
    s = pos_col.shape[0]
    tr = _tile(s, 512)
    f_mla = ROPE_BASE ** (-jnp.arange(ROPE_DIM // 2, dtype=F32) / (ROPE_DIM // 2))
    f_ret = ROPE_BASE ** (-jnp.arange(HEAD // 2, dtype=F32) / (HEAD // 2))
    fm = jnp.concatenate([jnp.zeros((64,), F32), f_mla, f_mla, jnp.zeros((32,), F32)]).reshape(1, LANES)
    fr = jnp.tile(jnp.concatenate([f_ret, f_ret]), 2).reshape(1, LANES)

    def body(p_ref, fm_ref, fr_ref, cm_ref, sm_ref, cr_ref, sr_ref):
        p = p_ref[...].astype(F32)
        am = p * fm_ref[...]
        ar = p * fr_ref[...]
        cm_ref[...] = jnp.cos(am)
        sm_ref[...] = jnp.sin(am)
        cr_ref[...] = jnp.tile(jnp.cos(ar), (1, 2))
        sr_ref[...] = jnp.tile(jnp.sin(ar), (1, 2))

    return pl.pallas_call(
        body, name="rope_tables", grid=(s // tr,),
        in_specs=[pl.BlockSpec((tr, 1), lambda i: (i, 0)), pl.BlockSpec((1, LANES), lambda i: (0, 0)),
                  pl.BlockSpec((1, LANES), lambda i: (0, 0))],
        out_specs=[pl.BlockSpec((tr, LANES), lambda i: (i, 0))] * 2 + [pl.BlockSpec((tr, 2 * LANES), lambda i: (i, 0))] * 2,
        out_shape=[jax.ShapeDtypeStruct((s, LANES), F32)] * 2 + [jax.ShapeDtypeStruct((s, 2 * LANES), F32)] * 2,
        compiler_params=_cparams(("parallel",)),
    )(pos_col, fm, fr)


def _lane(shape):
    return lax.broadcasted_iota(jnp.int32, shape, len(shape) - 1)


def _rot_mla(z):
    l = _lane(z.shape) % LANES
    n = z.shape[-1]
    return jnp.where(l < 80, -pltpu.roll(z, n - 16, 1), pltpu.roll(z, 16, 1))


def _rot_mla_t(y):
    l = _lane(y.shape) % LANES
    n = y.shape[-1]
    return jnp.where((l >= 64) & (l < 80), pltpu.roll(y, n - 16, 1),
                     jnp.where((l >= 80) & (l < 96), -pltpu.roll(y, 16, 1), 0.0))


def _rot_ret(z):
    l = _lane(z.shape) % HEAD
    n = z.shape[-1]
    return jnp.where(l < 32, -pltpu.roll(z, n - 32, 1), pltpu.roll(z, 32, 1))


def _rot_ret_t(y):
    l = _lane(y.shape) % HEAD
    n = y.shape[-1]
    return jnp.where(l < 32, pltpu.roll(y, n - 32, 1), -pltpu.roll(y, 32, 1))


def _log_sigmoid(x):
    return jnp.minimum(x, 0.0) - jnp.log1p(jnp.exp(-jnp.abs(x)))


def _fox_cum(proj, bias_row):
    s = proj.shape[0]
    fb = _tile(s, GATE_ROWS)
    nb = s // fb

    def body(x_ref, b_ref, cc_ref, cr_ref, carry_ref):
        @pl.when(pl.program_id(0) == 0)
        def _():
            carry_ref[...] = jnp.zeros_like(carry_ref)

        ls = _log_sigmoid(x_ref[...] + b_ref[...])
        r = lax.broadcasted_iota(jnp.int32, (fb, fb), 0)
        c = lax.broadcasted_iota(jnp.int32, (fb, fb), 1)
        tri = (c <= r).astype(F32)
        cum = _dot_exact(tri, ls) + carry_ref[...]
        carry_ref[...] = cum[fb - 1:fb, :]
        cc_ref[...] = cum
        cr_ref[...] = cum.T[0:8, :]

    return pl.pallas_call(
        body, name="fox_cum", grid=(nb,),
        in_specs=[pl.BlockSpec((fb, LANES), lambda i: (i, OFF_MISC)), pl.BlockSpec((1, LANES), lambda i: (0, 0))],
        out_specs=[pl.BlockSpec((fb, LANES), lambda i: (i, 0)), pl.BlockSpec((8, fb), lambda i: (0, i))],
        out_shape=[jax.ShapeDtypeStruct((s, LANES), F32), jax.ShapeDtypeStruct((8, s), F32)],
        scratch_shapes=[pltpu.VMEM((1, LANES), F32)],
        compiler_params=_cparams(("arbitrary",)),
    )(proj, bias_row)


def _fox_gate_bwd(dck, drs, proj, bias_row, dkr):
    s = proj.shape[0]
    fb = _tile(s, GATE_ROWS)
    nb = s // fb

    def body(d_ref, r_ref, x_ref, b_ref, k_ref, o_ref, db_ref, carry_ref):
        @pl.when(pl.program_id(0) == 0)
        def _():
            carry_ref[...] = jnp.zeros_like(carry_ref)
            db_ref[...] = jnp.zeros_like(db_ref)

        rows = jnp.concatenate([d_ref[0], d_ref[1], jnp.zeros((LANES - 16, fb), F32)], axis=0)
        t = rows.T
        l = _lane((fb, LANES))
        r0, r1 = r_ref[0], r_ref[1]
        rsum = jnp.where(l == 0, r0[:, 0:1], jnp.where(l == 1, r0[:, HEAD:HEAD + 1],
                         jnp.where(l == 2, r1[:, 0:1], jnp.where(l == 3, r1[:, HEAD:HEAD + 1], 0.0))))
        dcum = rsum - jnp.where(l < 2, t, pltpu.roll(t, LANES - 6, 1))
        r = lax.broadcasted_iota(jnp.int32, (fb, fb), 0)
        c = lax.broadcasted_iota(jnp.int32, (fb, fb), 1)
        triu = (c >= r).astype(F32)
        rc = _dot_exact(triu, dcum) + carry_ref[...]
        carry_ref[...] = rc[0:1, :]
        f = x_ref[...] + b_ref[...]
        sig_neg = 1.0 / (1.0 + jnp.exp(f))
        df = jnp.where(l < N_HEADS, rc * sig_neg, 0.0)
        db_ref[...] += jnp.sum(df, axis=0, keepdims=True)
        o_ref[...] = (df + k_ref[...]).astype(o_ref.dtype)

    rev = lambda i: nb - 1 - i
    return pl.pallas_call(
        body, name="fox_gate_bwd", grid=(nb,),
        in_specs=[pl.BlockSpec((2, 8, fb), lambda i: (0, 0, rev(i))), pl.BlockSpec((2, fb, LANES), lambda i: (0, rev(i), 0)),
                  pl.BlockSpec((fb, LANES), lambda i: (rev(i), OFF_MISC)),
                  pl.BlockSpec((1, LANES), lambda i: (0, 0)), pl.BlockSpec((fb, LANES), lambda i: (rev(i), 0))],
        out_specs=[pl.BlockSpec((fb, LANES), lambda i: (rev(i), 0)), pl.BlockSpec((1, LANES), lambda i: (0, 0))],
        out_shape=[jax.ShapeDtypeStruct((s, LANES), BF16), jax.ShapeDtypeStruct((1, LANES), F32)],
        scratch_shapes=[pltpu.VMEM((1, LANES), F32)],
        compiler_params=_cparams(("arbitrary",)),
    )(dck, drs, proj, bias_row, dkr)


def _mla_prep(proj, cos_m, sin_m, g_q, g_kv, wq, wk, wv):
    s = proj.shape[0]
    tr = _tile(s, 512)

    def body(cq_ref, ckv_ref, misc_ref, cos_ref, sin_ref, gq_ref, gkv_ref, wq_ref, wk_ref, wv_ref,
             q_ref, k_ref, v_ref, cqn_ref, ckvn_ref):
        cos4 = jnp.tile(cos_ref[...], (1, 4))
        sin4 = jnp.tile(sin_ref[...], (1, 4))
        cqn = _rms(cq_ref[...], gq_ref[...]).astype(BF16)
        ckvn = _rms(ckv_ref[...], gkv_ref[...]).astype(BF16)
        cqn_ref[...] = cqn
        ckvn_ref[...] = ckvn
        zq = _dot(cqn, wq_ref[...])
        q_ref[...] = (zq * cos4 + _rot_mla(zq) * sin4).astype(BF16)
        l = _lane((tr, LANES))
        kr = jnp.where((l >= KR_LANE) & (l < KR_LANE + ROPE_DIM), misc_ref[...], 0.0)
        zk = _dot(ckvn, wk_ref[...]) + jnp.tile(kr, (1, 4))
        k_ref[...] = (zk * cos4 + _rot_mla(zk) * sin4).astype(BF16)
        v_ref[...] = _dot(ckvn, wv_ref[...]).astype(BF16)

    full = lambda a: pl.BlockSpec(a.shape, lambda i: (0, 0))
    rowb = lambda w: pl.BlockSpec((tr, w), lambda i: (i, 0))
    gq2, gkv2 = g_q.reshape(1, Q_RANK), g_kv.reshape(1, KV_RANK)
    return pl.pallas_call(
        body, name="mla_prep", grid=(s // tr,),
        in_specs=[pl.BlockSpec((tr, 256), lambda i: (i, OFF_CQ // 2)), pl.BlockSpec((tr, LANES), lambda i: (i, OFF_CKV)),
                  pl.BlockSpec((tr, LANES), lambda i: (i, OFF_MISC)), rowb(LANES), rowb(LANES),
                  full(gq2), full(gkv2), full(wq), full(wk), full(wv)],
        out_specs=[rowb(512), rowb(512), rowb(512), rowb(256), rowb(128)],
        out_shape=[jax.ShapeDtypeStruct((s, 512), BF16), jax.ShapeDtypeStruct((s, 512), BF16), jax.ShapeDtypeStruct((s, 512), BF16),
                   jax.ShapeDtypeStruct((s, 256), BF16), jax.ShapeDtypeStruct((s, 128), BF16)],
        compiler_params=_cparams(("parallel",)),
    )(proj, proj, proj, cos_m, sin_m, gq2, gkv2, wq, wk, wv)


def _mla_prep_bwd(dq, dk, dv, proj, cqn, ckvn, cos_m, sin_m, g_q, g_kv, wq, wk, wv):
    s = proj.shape[0]
    tr = _tile(s, 512)

    def body(dq_ref, dk_ref, dv_ref, cq_ref, ckv_ref, cqn_ref, ckvn_ref, cos_ref, sin_ref, gq_ref, gkv_ref,
             wq_ref, wk_ref, wv_ref, dcq_ref, dckv_ref, dkr_ref, dwq_ref, dwk_ref, dwv_ref, dgq_ref, dgkv_ref):
        @pl.when(pl.program_id(0) == 0)
        def _():
            for r in (dwq_ref, dwk_ref, dwv_ref, dgq_ref, dgkv_ref):
                r[...] = jnp.zeros_like(r)

        cos4 = jnp.tile(cos_ref[...], (1, 4))
        sin4 = jnp.tile(sin_ref[...], (1, 4))
        dqv = dq_ref[...]
        dzq = dqv * cos4 + _rot_mla_t(dqv * sin4)
        dkv_ = dk_ref[...]
        dzk = dkv_ * cos4 + _rot_mla_t(dkv_ * sin4)
        l = _lane((tr, LANES))
        in_rope = (l >= KR_LANE) & (l < KR_LANE + ROPE_DIM)
        dkr = dzk[:, 0:128] + dzk[:, 128:256] + dzk[:, 256:384] + dzk[:, 384:512]
        dkr_ref[...] = jnp.where(in_rope, dkr, 0.0)
        dzq_b = dzq.astype(BF16)
        dzk_b = dzk.astype(BF16)
        dv_b = dv_ref[...].astype(BF16)
        dcqn = _dot_nt(dzq_b, wq_ref[...])
        dckvn = _dot_nt(dzk_b, wk_ref[...]) + _dot_nt(dv_b, wv_ref[...])
        dwq_ref[...] += _dot_tn(cqn_ref[...], dzq_b)
        dwk_ref[...] += _dot_tn(ckvn_ref[...], dzk_b)
        dwv_ref[...] += _dot_tn(ckvn_ref[...], dv_b)
        dcq, gq_term = _rms_bwd(cq_ref[...], gq_ref[...], dcqn)
        dckv, gkv_term = _rms_bwd(ckv_ref[...], gkv_ref[...], dckvn)
        dcq_ref[...] = dcq.astype(BF16)
        dckv_ref[...] = dckv.astype(BF16)
        dgq_ref[...] += jnp.sum(gq_term, axis=0, keepdims=True)
        dgkv_ref[...] += jnp.sum(gkv_term, axis=0, keepdims=True)

    full = lambda shp: pl.BlockSpec(shp, lambda i: (0, 0))
    rowb = lambda w: pl.BlockSpec((tr, w), lambda i: (i, 0))
    gq2, gkv2 = g_q.reshape(1, Q_RANK), g_kv.reshape(1, KV_RANK)
    return pl.pallas_call(
        body, name="mla_prep_bwd", grid=(s // tr,),
        in_specs=[rowb(512), rowb(512), rowb(512),
                  pl.BlockSpec((tr, 256), lambda i: (i, OFF_CQ // 2)), pl.BlockSpec((tr, LANES), lambda i: (i, OFF_CKV)),
                  rowb(256), rowb(128), rowb(LANES), rowb(LANES), full((1, Q_RANK)), full((1, KV_RANK)),
                  full(wq.shape), full(wk.shape), full(wv.shape)],
        out_specs=[rowb(256), rowb(128), rowb(128), full(wq.shape), full(wk.shape), full(wv.shape),
                   full((1, Q_RANK)), full((1, KV_RANK))],
        out_shape=[jax.ShapeDtypeStruct((s, 256), BF16), jax.ShapeDtypeStruct((s, 128), BF16), jax.ShapeDtypeStruct((s, 128), F32),
                   jax.ShapeDtypeStruct(wq.shape, F32), jax.ShapeDtypeStruct(wk.shape, F32), jax.ShapeDtypeStruct(wv.shape, F32),
                   jax.ShapeDtypeStruct((1, Q_RANK), F32), jax.ShapeDtypeStruct((1, KV_RANK), F32)],
        compiler_params=_cparams(("arbitrary",)),
    )(dq, dk, dv, proj, proj, cqn, ckvn, cos_m, sin_m, gq2, gkv2, wq, wk, wv)


def _ret_prep(proj, cos_r, sin_r):
    s = proj.shape[0]
    tr = _tile(s, 512)

    def body(q_ref, k_ref, cos_ref, sin_ref, qo_ref, ko_ref):
        cos, sin = cos_ref[...], sin_ref[...]
        q, k = q_ref[...], k_ref[...]
        qo_ref[...] = (q * cos + _rot_ret(q) * sin).astype(BF16)
        ko_ref[...] = ((k * cos + _rot_ret(k) * sin) * (HEAD ** -0.5)).astype(BF16)

    rowb = pl.BlockSpec((tr, 256), lambda i: (i, 0))
    return pl.pallas_call(
        body, name="ret_prep", grid=(s // tr,),
        in_specs=[pl.BlockSpec((tr, 256), lambda i: (i, OFF_RQ // 2)), pl.BlockSpec((tr, 256), lambda i: (i, OFF_RK // 2)), rowb, rowb],
        out_specs=[rowb, rowb], out_shape=[jax.ShapeDtypeStruct((s, 256), BF16)] * 2,
        compiler_params=_cparams(("parallel",)),
    )(proj, proj, cos_r, sin_r)


def _ret_prep_bwd(dq, dk, cos_r, sin_r):
    s = dq.shape[0]
    tr = _tile(s, 512)

    def body(dq_ref, dk_ref, cos_ref, sin_ref, qo_ref, ko_ref):
        cos, sin = cos_ref[...], sin_ref[...]
        q, k = dq_ref[...], dk_ref[...] * (HEAD ** -0.5)
        qo_ref[...] = (q * cos + _rot_ret_t(q * sin)).astype(BF16)
        ko_ref[...] = (k * cos + _rot_ret_t(k * sin)).astype(BF16)

    rowb = pl.BlockSpec((tr, 256), lambda i: (i, 0))
    return pl.pallas_call(
        body, name="ret_prep_bwd", grid=(s // tr,), in_specs=[rowb] * 4, out_specs=[rowb, rowb],
        out_shape=[jax.ShapeDtypeStruct((s, 256), BF16)] * 2, compiler_params=_cparams(("parallel",)),
    )(dq, dk, cos_r, sin_r)


_LOG_GAMMA = [float(np.log1p(-np.float32(2.0) ** np.float32(-5.0 - h))) for h in range(N_HEADS)]
_MLA_SCALE = float((HEAD + ROPE_DIM) ** -0.5)
_QK_SCALE = float(HEAD ** -0.5)
KEY_BLOCKS = 4
QB = 512


def _split2(x):
    h = x.astype(BF16)
    return h, (x - h.astype(F32)).astype(BF16)


def _dot2(x, u):
    h, lo = _split2(x)
    return _dot(h, u) + _dot(lo, u)


def _head_pick(block, head, axis):
    idx = lax.broadcasted_iota(jnp.int32, block.shape, axis)
    return jnp.sum(jnp.where(idx == head, block, 0.0), axis=axis, keepdims=True)


def _log_gamma_of(head):
    lg = jnp.float32(_LOG_GAMMA[3])
    for h in (2, 1, 0):
        lg = jnp.where(head == h, jnp.float32(_LOG_GAMMA[h]), lg)
    return lg


def _mixer_specs(mode, s, q_off, k_off, v_off):
    nhb = 2
    bw = 2 * LANES if mode == "mla" else LANES
    nsub = KEY_BLOCKS if (s // TQ) % KEY_BLOCKS == 0 else 1
    q_spec = pl.BlockSpec((QB, bw), lambda p, i: (i, q_off + p))
    k_spec = pl.BlockSpec((s, bw), lambda p, i: (0, k_off + p))
    v_spec = pl.BlockSpec((s, bw), lambda p, i: (0, v_off + p))
    return nhb, N_HEADS // nhb, nsub, q_spec, k_spec, v_spec


def _mixer_geometry(mode, i, nsub):
    w = TQ * nsub
    row = lax.broadcasted_iota(jnp.int32, (QB, w), 0)
    col = lax.broadcasted_iota(jnp.int32, (QB, w), 1)
    nfull = (i * QB) // w
    dist = col - row
    if mode in ("fox", "sb"):
        rel = dist
    else:
        rel = col - (row | (CHUNK - 1))

    def visible(c):
        off = c * w - i * QB
        return (rel + off) < 0 if mode == "sb" else (rel + off) <= 0

    return nfull, dist, visible


class _SideJob:
    def __init__(self, inputs, out_shape, n_sems, sends, recvs):
        self.inputs, self.out_shape, self.n_sems, self.sends, self.recvs = list(inputs), list(out_shape), n_sems, sends, recvs


def _carry_side_job(body, n_in, n_out, side, n_steps):
    if side is None:
        return body
    si, so = len(side.inputs), len(side.out_shape)

    def at(corner):
        ok = pl.program_id(0) == corner[0]
        for d in range(1, len(n_steps)):
            ok = ok & (pl.program_id(d) == corner[d])
        return ok

    def wrapped(*refs):
        ins, s_ins = refs[:n_in], refs[n_in:n_in + si]
        outs, s_outs = refs[n_in + si:n_in + si + n_out], refs[n_in + si + n_out:n_in + si + n_out + so]
        scratch, send, recv = refs[n_in + si + n_out + so:-2], refs[-2], refs[-1]

        @pl.when(at([0] * len(n_steps)))
        def _():
            for cp in side.sends(s_ins, s_outs, send, recv):
                cp.start()

        body(*ins, *outs, *scratch)

        @pl.when(at([n - 1 for n in n_steps]))
        def _():
            for cp in side.recvs(s_ins, s_outs, send, recv):
                cp.wait_recv()
            for cp in side.sends(s_ins, s_outs, send, recv):
                cp.wait_send()

    return wrapped


def _side_specs(side):
    if side is None:
        return [], [], []
    hbm = pl.BlockSpec(memory_space=pl.ANY)
    return ([hbm] * len(side.inputs), [hbm] * len(side.out_shape),
            [pltpu.SemaphoreType.DMA((side.n_sems,)), pltpu.SemaphoreType.DMA((side.n_sems,))])


def _mixer_fwd(mode, qa, q_off, ka, k_off, va, v_off, *, cum_col=None, cum_row=None, side=None):
    s = qa.shape[0]
    nq = s // QB
    nhb, nblk, nsub, q_spec, k_spec, v_spec = _mixer_specs(mode, s, q_off, k_off, v_off)
    w = TQ * nsub
    softmax = mode in ("fox", "mla")

    def body(*refs):
        refs = list(refs)
        q_ref, k_ref, v_ref = refs[:3]
        refs = refs[3:]
        if mode == "fox":
            cc_ref, cr_ref = refs[:2]
            refs = refs[2:]
        o_ref = refs[0]
        st_ref = refs[1]
        p = pl.program_id(0)
        i = pl.program_id(1)
        nfull, dist, visible = _mixer_geometry(mode, i, nsub)
        lane = _lane((1, LANES))
        heads = [nhb * p + hh for hh in range(nhb)]
        wide = mode == "mla"
        q_scale = _QK_SCALE if mode in ("fox", "sb") else 1.0
        cols = [slice(hh * LANES, (hh + 1) * LANES) if wide else slice(None) for hh in range(nhb)]
        if wide:
            qs = [q_ref[:, cols[hh]] for hh in range(nhb)]
        else:
            qf = q_ref[...].astype(F32) * q_scale
            qs = [jnp.where((lane // HEAD) == hh, qf, 0.0).astype(BF16) for hh in range(nhb)]
        if mode == "fox":
            cqs = [_head_pick(cc_ref[...], h, 1) for h in heads]
        if mode == "sb":
            r1 = lax.broadcasted_iota(jnp.int32, (TQ, TQ), 0)
            c1 = lax.broadcasted_iota(jnp.int32, (TQ, TQ), 1)
            u_after = (r1 > c1).astype(BF16)

        def chunk(c):
            return pl.ds(pl.multiple_of(c * w, w), w)

        def scores(c):
            js = chunk(c)
            return tuple(_dot_nt(qs[hh], k_ref[js, cols[hh]]) for hh in range(nhb))

        def head_step(hh, c, js, sc, vj, carry, last):
            if softmax:
                m, l, acc = carry
                if mode == "fox":
                    ck = _head_pick(cr_ref[:, js], heads[hh], 0)
                    sc = sc + (cqs[hh] - ck)
                else:
                    sc = sc * _MLA_SCALE
                if last:
                    sc = jnp.where(visible(c), sc, NEG)
                m_new = jnp.maximum(m, jnp.max(sc, axis=-1, keepdims=True))
                alpha = jnp.exp(m - m_new)
                pr = jnp.exp(sc - m_new)
                l = alpha * l + jnp.sum(pr, axis=-1, keepdims=True)
                acc = alpha * acc + _dot(pr.astype(BF16), vj)
                return m_new, l, acc
            run, acc = carry
            z = sc
            log_beta = jnp.minimum(z, 0.0) - jnp.log(1.0 + jnp.exp(-jnp.abs(z)))
            log_stay = log_beta - z
            if last:
                vis = visible(c)
                log_stay = jnp.where(vis, log_stay, 0.0)
            parts = [None] * nsub
            for b in reversed(range(nsub)):
                ls_b = log_stay[:, b * TQ:(b + 1) * TQ]
                parts[b] = _dot2(ls_b, u_after) + run
                run = run + jnp.sum(ls_b, axis=-1, keepdims=True)
            later = parts[0] if nsub == 1 else jnp.concatenate(parts, axis=1)
            wgt = jnp.exp(log_beta + later)
            if last:
                wgt = jnp.where(vis, wgt, 0.0)
            return run, acc + _dot(wgt.astype(BF16), vj)

        def step(c, c_next, state, last):
            scs, carries = state
            nxt = scores(c_next) if c_next is not None else None
            js = chunk(c)
            return nxt, tuple(head_step(hh, c, js, scs[hh], v_ref[js, cols[hh]], carries[hh], last) for hh in range(nhb))

        zero_acc = jnp.zeros((QB, LANES), F32)
        zero1 = jnp.zeros((QB, 1), F32)
        if softmax:
            init = tuple((jnp.full((QB, 1), NEG, F32), zero1, zero_acc) for _ in range(nhb))
        else:
            init = tuple((zero1, zero_acc) for _ in range(nhb))
        if mode == "sb":
            state = step(nfull, jnp.maximum(nfull - 1, 0), (scores(nfull), init), True)
            _, carries = lax.fori_loop(0, nfull, lambda t, st: step(nfull - 1 - t, jnp.maximum(nfull - 2 - t, 0), st, False), state)
        else:
            state = lax.fori_loop(0, nfull, lambda c, st: step(c, c + 1, st, False), (scores(0), init))
            _, carries = step(nfull, None, state, True)
        if softmax:
            outs = [acc / l for (m, l, acc) in carries]
            stats = [m + jnp.log(l) for (m, l, acc) in carries]
        else:
            outs, stats = [acc for (run, acc) in carries], [run for (run, acc) in carries]
        hm0 = (lane // HEAD) == 0
        pick = lambda a: jnp.where(hm0, a[0], a[1])
        if wide:
            for hh in range(nhb):
                o_ref[:, cols[hh]] = outs[hh]
        else:
            o_ref[...] = pick(outs)
        st_ref[0] = pick(stats)

    in_specs = [q_spec, k_spec, v_spec]
    args = [qa, ka, va]
    if mode == "fox":
        in_specs += [pl.BlockSpec((QB, LANES), lambda p, i: (i, 0)), pl.BlockSpec((8, s), lambda p, i: (0, 0))]
        args += [cum_col, cum_row]
    bw = 2 * LANES if mode == "mla" else LANES
    out_specs = [pl.BlockSpec((QB, bw), lambda p, i: (i, p))]
    out_shape = [jax.ShapeDtypeStruct((s, nblk * bw), F32)]
    out_specs.append(pl.BlockSpec((1, QB, LANES), lambda p, i: (p, i, 0)))
    out_shape.append(jax.ShapeDtypeStruct((nblk, s, LANES), F32))
    side_in, side_out, side_scratch = _side_specs(side)
    res = pl.pallas_call(
        _carry_side_job(body, len(args), len(out_shape), side, (nblk, nq)), name=mode + "_fwd", grid=(nblk, nq),
        in_specs=in_specs + side_in, out_specs=out_specs + side_out,
        out_shape=out_shape + ([] if side is None else side.out_shape), scratch_shapes=side_scratch,
        compiler_params=_cparams(("parallel", "parallel") if side is None else ("arbitrary", "arbitrary")),
    )(*args, *([] if side is None else side.inputs))
    return (res[0], res[1]) if side is None else (res[0], res[1], res[2:])


def _mixer_bwd(mode, qa, q_off, ka, k_off, va, v_off, o, do, *, stat=None, cum_col=None, cum_row=None, side=None):
    s = qa.shape[0]
    nq = s // QB
    nhb, nblk, nsub, q_spec, k_spec, v_spec = _mixer_specs(mode, s, q_off, k_off, v_off)
    w = TQ * nsub
    softmax = mode in ("fox", "mla")

    def body(*refs):
        refs = list(refs)
        q_ref, k_ref, v_ref, o_ref, do_ref = refs[:5]
        refs = refs[5:]
        st_ref = refs[0]
        refs = refs[1:]
        if mode == "fox":
            cc_ref, cr_ref = refs[:2]
            refs = refs[2:]
        dq_ref, dk_ref, dv_ref = refs[:3]
        dck_ref, drs_ref = refs[3:5] if mode == "fox" else (None, None)
        p = pl.program_id(0)
        i = pl.program_id(1)

        @pl.when(i == 0)
        def _():
            dk_ref[...] = jnp.zeros_like(dk_ref)
            dv_ref[...] = jnp.zeros_like(dv_ref)
            if mode == "fox":
                dck_ref[...] = jnp.zeros_like(dck_ref)

        nfull, dist, visible = _mixer_geometry(mode, i, nsub)
        lane = _lane((1, LANES))
        heads = [nhb * p + hh for hh in range(nhb)]
        dov = do_ref[...]
        wide = mode == "mla"
        q_scale = _QK_SCALE if mode in ("fox", "sb") else 1.0
        cols = [slice(hh * LANES, (hh + 1) * LANES) if wide else slice(None) for hh in range(nhb)]
        if wide:
            prod = dov * o_ref[...]
            qs = [q_ref[:, cols[hh]] for hh in range(nhb)]
            dos = [dov[:, cols[hh]].astype(BF16) for hh in range(nhb)]
            deltas = [jnp.sum(prod[:, cols[hh]], axis=-1, keepdims=True) for hh in range(nhb)]
        else:
            qf = q_ref[...].astype(F32) * q_scale
            prod = dov * o_ref[...]
            hms = [(lane // HEAD) == hh for hh in range(nhb)]
            qs = [jnp.where(hm, qf, 0.0).astype(BF16) for hm in hms]
            dos = [jnp.where(hm, dov, 0.0).astype(BF16) for hm in hms]
            deltas = [jnp.sum(jnp.where(hm, prod, 0.0), axis=-1, keepdims=True) for hm in hms]
        st = st_ref[0]
        stats = [st[:, hh * HEAD:hh * HEAD + 1] for hh in range(nhb)]
        if mode == "fox":
            cqs = [_head_pick(cc_ref[...], h, 1) for h in heads]
        if mode == "sb":
            r1 = lax.broadcasted_iota(jnp.int32, (TQ, TQ), 0)
            c1 = lax.broadcasted_iota(jnp.int32, (TQ, TQ), 1)
            u_upto = (r1 <= c1).astype(BF16)
            u_before = (r1 < c1).astype(BF16)

        def chunk(c):
            return pl.ds(pl.multiple_of(c * w, w), w)

        def scores(c):
            js = chunk(c)
            if mode == "sb":
                return tuple((_dot_nt(qs[hh], k_ref[js, cols[hh]]), None) for hh in range(nhb))
            return tuple((_dot_nt(qs[hh], k_ref[js, cols[hh]]), _dot_nt(dos[hh], v_ref[js, cols[hh]])) for hh in range(nhb))

        def emit(hh, js, ds_b, pr_b, dq):
            dk_ref[js, cols[hh]] += _dot_tn(ds_b, qs[hh])
            dv_ref[js, cols[hh]] += _dot_tn(pr_b, dos[hh])
            return dq + _dot(ds_b, k_ref[js, cols[hh]])

        def head_step(hh, c, js, sc_dp, carry, last):
            sc, dp = sc_dp
            if dp is None:
                dp = _dot_nt(dos[hh], v_ref[js, cols[hh]])
            if softmax:
                dq, rsum = carry
                if mode == "fox":
                    ck = _head_pick(cr_ref[:, js], heads[hh], 0)
                    sc = sc + (cqs[hh] - ck)
                else:
                    sc = sc * _MLA_SCALE
                if last:
                    sc = jnp.where(visible(c), sc, NEG)
                pr = jnp.exp(sc - stats[hh])
                ds = pr * (dp - deltas[hh])
                if mode == "fox":
                    dck_ref[0, hh:hh + 1, js] += jnp.sum(ds, axis=0, keepdims=True)
                    rsum = rsum + jnp.sum(ds, axis=-1, keepdims=True)
                if mode == "mla":
                    ds = ds * _MLA_SCALE
                return emit(hh, js, ds.astype(BF16), pr.astype(BF16), dq), rsum
            seen, gsum, dq = carry
            z = sc
            log_beta = jnp.minimum(z, 0.0) - jnp.log(1.0 + jnp.exp(-jnp.abs(z)))
            log_stay = log_beta - z
            if last:
                vis = visible(c)
                log_stay = jnp.where(vis, log_stay, 0.0)
            parts = []
            for b in range(nsub):
                ls_b = log_stay[:, b * TQ:(b + 1) * TQ]
                parts.append((stats[hh] - seen) - _dot2(ls_b, u_upto))
                seen = seen + jnp.sum(ls_b, axis=-1, keepdims=True)
            later = parts[0] if nsub == 1 else jnp.concatenate(parts, axis=1)
            wgt = jnp.exp(log_beta + later)
            if last:
                wgt = jnp.where(vis, wgt, 0.0)
            g = dp * wgt
            parts = []
            for b in range(nsub):
                g_b = g[:, b * TQ:(b + 1) * TQ]
                parts.append(gsum + _dot2(g_b, u_before))
                gsum = gsum + jnp.sum(g_b, axis=-1, keepdims=True)
            before = parts[0] if nsub == 1 else jnp.concatenate(parts, axis=1)
            beta = jnp.exp(log_beta)
            dz = g * (1.0 - beta) - beta * before
            if last:
                dz = jnp.where(vis, dz, 0.0)
            return seen, gsum, emit(hh, js, dz.astype(BF16), wgt.astype(BF16), dq)

        def step(c, c_next, state, last):
            scs, carries = state
            nxt = scores(c_next) if c_next is not None else None
            js = chunk(c)
            return nxt, tuple(head_step(hh, c, js, scs[hh], carries[hh], last) for hh in range(nhb))

        zero_acc = jnp.zeros((QB, LANES), F32)
        zero1 = jnp.zeros((QB, 1), F32)
        if softmax:
            init = tuple((zero_acc, zero1) for _ in range(nhb))
        else:
            init = tuple((zero1, zero1, zero_acc) for _ in range(nhb))
        state = lax.fori_loop(0, nfull, lambda c, st: step(c, c + 1, st, False), (scores(0), init))
        _, carries = step(nfull, None, state, True)
        if softmax:
            dqs = [dq for (dq, rsum) in carries]
        else:
            dqs = [dq for (seen, gsum, dq) in carries]
        hm0 = (lane // HEAD) == 0
        if wide:
            for hh in range(nhb):
                dq_ref[:, cols[hh]] = dqs[hh]
        else:
            dq_ref[...] = jnp.where(hm0, dqs[0], dqs[1]) * q_scale
        if mode == "fox":
            drs_ref[0] = jnp.where(hm0, carries[0][1], carries[1][1])

    bw = 2 * LANES if mode == "mla" else LANES
    pair_blk = pl.BlockSpec((QB, bw), lambda p, i: (i, p))
    full_blk = pl.BlockSpec((s, bw), lambda p, i: (0, p))
    stat_blk = pl.BlockSpec((1, QB, LANES), lambda p, i: (p, i, 0))
    in_specs = [q_spec, k_spec, v_spec, pair_blk, pair_blk]
    args = [qa, ka, va, o, do]
    in_specs.append(stat_blk)
    args.append(stat)
    if mode == "fox":
        in_specs += [pl.BlockSpec((QB, LANES), lambda p, i: (i, 0)), pl.BlockSpec((8, s), lambda p, i: (0, 0))]
        args += [cum_col, cum_row]
    out_specs = [pair_blk, full_blk, full_blk]
    out_shape = [jax.ShapeDtypeStruct((s, nblk * bw), F32)] * 3
    if mode == "fox":
        out_specs += [pl.BlockSpec((1, 8, s), lambda p, i: (p, 0, 0)), stat_blk]
        out_shape += [jax.ShapeDtypeStruct((2, 8, s), F32), jax.ShapeDtypeStruct((2, s, LANES), F32)]
    side_in, side_out, side_scratch = _side_specs(side)
    res = pl.pallas_call(
        _carry_side_job(body, len(args), len(out_shape), side, (nblk, nq)), name=mode + "_bwd", grid=(nblk, nq),
        in_specs=in_specs + side_in, out_specs=out_specs + side_out,
        out_shape=out_shape + ([] if side is None else side.out_shape), scratch_shapes=side_scratch,
        compiler_params=_cparams(("parallel", "arbitrary") if side is None else ("arbitrary", "arbitrary")),
    )(*args, *([] if side is None else side.inputs))
    return res if side is None else (*res[:len(out_shape)], res[len(out_shape):])


def _ret_geometry(p):
    lane = _lane((1, LANES))
    lg_lane = jnp.where(lane < HEAD, _log_gamma_of(2 * p), _log_gamma_of(2 * p + 1))
    a = lax.broadcasted_iota(jnp.int32, (TQ, 1), 0).astype(F32)
    row = lax.broadcasted_iota(jnp.int32, (TQ, TQ), 0)
    col = lax.broadcasted_iota(jnp.int32, (TQ, TQ), 1)
    same_chunk_or_earlier = (col // CHUNK) <= (row // CHUNK)
    gap = jnp.abs(row - col).astype(F32)
    decays = [jnp.where(same_chunk_or_earlier, jnp.exp(_log_gamma_of(2 * p + hh) * gap), 0.0) for hh in range(2)]
    r = lax.broadcasted_iota(jnp.int32, (LANES, LANES), 0)
    c = lax.broadcasted_iota(jnp.int32, (LANES, LANES), 1)
    own_head = (r // HEAD) == (c // HEAD)
    return lane, lg_lane, a, decays, own_head


def _ret_fwd(qa, ka, va, v_off):
    s = qa.shape[0]
    nq = s // TQ

    def body(q_ref, k_ref, v_ref, o_ref, st_ref, state):
        p = pl.program_id(0)

        @pl.when(pl.program_id(1) == 0)
        def _():
            state[...] = jnp.zeros_like(state)

        lane, lg_lane, a, decays, own_head = _ret_geometry(p)
        q = q_ref[...].astype(F32)
        k = k_ref[...]
        v = v_ref[...]
        s_in = state[...]
        st_ref[0, 0] = s_in
        out = _dot((q * jnp.exp(lg_lane * (a + 1.0))).astype(BF16), s_in.astype(BF16))
        for hh in range(2):
            hm = (lane // HEAD) == hh
            qh = jnp.where(hm, q, 0.0).astype(BF16)
            inner = _dot((_dot_nt(qh, k) * decays[hh]).astype(BF16), v)
            out = out + jnp.where(hm, inner, 0.0)
        o_ref[...] = out
        k_tail = (k.astype(F32) * jnp.exp(lg_lane * (TQ - 1.0 - a))).astype(BF16)
        state[...] = jnp.exp(lg_lane * float(TQ)) * s_in + jnp.where(own_head, _dot_tn(k_tail, v), 0.0)

    blk = lambda off: pl.BlockSpec((TQ, LANES), lambda p, i: (i, off + p))
    return pl.pallas_call(
        body, name="ret_fwd", grid=(2, nq), in_specs=[blk(0), blk(0), blk(v_off)],
        out_specs=[blk(0), pl.BlockSpec((1, 1, LANES, LANES), lambda p, i: (p, i, 0, 0))],
        out_shape=[jax.ShapeDtypeStruct((s, 2 * LANES), F32), jax.ShapeDtypeStruct((2, nq, LANES, LANES), F32)],
        scratch_shapes=[pltpu.VMEM((LANES, LANES), F32)],
        compiler_params=_cparams(("parallel", "arbitrary")),
    )(qa, ka, va)


def _ret_bwd(qa, ka, va, v_off, states, do):
    s = qa.shape[0]
    nq = s // TQ

    def body(q_ref, k_ref, v_ref, st_ref, do_ref, dq_ref, dk_ref, dv_ref, dstate):
        p = pl.program_id(0)

        @pl.when(pl.program_id(1) == 0)
        def _():
            dstate[...] = jnp.zeros_like(dstate)

        lane, lg_lane, a, decays, own_head = _ret_geometry(p)
        q = q_ref[...].astype(F32)
        k = k_ref[...]
        kf = k.astype(F32)
        v = v_ref[...]
        dov = do_ref[...]
        s_in = st_ref[0, 0].astype(BF16)
        ds_next = dstate[...]
        ds_b = ds_next.astype(BF16)
        head_decay = jnp.exp(lg_lane * (a + 1.0))
        tail_decay = jnp.exp(lg_lane * (TQ - 1.0 - a))
        k_tail = (kf * tail_decay).astype(BF16)
        dq = _dot_nt(dov.astype(BF16), s_in) * head_decay
        dk = _dot_nt(v, ds_b) * tail_decay
        dv = _dot(k_tail, ds_b)
        for hh in range(2):
            hm = (lane // HEAD) == hh
            qh = jnp.where(hm, q, 0.0).astype(BF16)
            doh = jnp.where(hm, dov, 0.0).astype(BF16)
            att = (_dot_nt(qh, k) * decays[hh]).astype(BF16)
            datt = (_dot_nt(doh, v) * decays[hh]).astype(BF16)
            dv = dv + _dot_tn(att, doh)
            dk = dk + _dot_tn(datt, qh)
            dq = dq + jnp.where(hm, _dot(datt, k), 0.0)
        dq_ref[...] = dq
        dk_ref[...] = dk
        dv_ref[...] = dv
        q_head = (q * head_decay).astype(BF16)
        dstate[...] = jnp.exp(lg_lane * float(TQ)) * ds_next + jnp.where(own_head, _dot_tn(q_head, dov.astype(BF16)), 0.0)

    blk = lambda off: pl.BlockSpec((TQ, LANES), lambda p, i: (nq - 1 - i, off + p))
    return pl.pallas_call(
        body, name="ret_bwd", grid=(2, nq),
        in_specs=[blk(0), blk(0), blk(v_off), pl.BlockSpec((1, 1, LANES, LANES), lambda p, i: (p, nq - 1 - i, 0, 0)), blk(0)],
        out_specs=[blk(0)] * 3, out_shape=[jax.ShapeDtypeStruct((s, 2 * LANES), F32)] * 3,
        scratch_shapes=[pltpu.VMEM((LANES, LANES), F32)],
        compiler_params=_cparams(("parallel", "arbitrary")),
    )(qa, ka, va, states, do)


def _seg_mean_matrix():
    r = lax.broadcasted_iota(jnp.int32, (GROUP, GROUP), 0)
    c = lax.broadcasted_iota(jnp.int32, (GROUP, GROUP), 1)
    return jnp.where((r // HEAD) == (c // HEAD), 1.0 / HEAD, 0.0).astype(BF16)


def _seg_mean(x, seg):
    h = x.astype(BF16)
    r = x - h.astype(F32)
    m = r.astype(BF16)
    lo = (r - m.astype(F32)).astype(BF16)
    return _dot(h, seg) + _dot(m, seg) + _dot(lo, seg)


def _sigmoid(x):
    return 1.0 / (1.0 + jnp.exp(-x))


def _mix_post(oa, ob, oc, od, proj, g):
    s = oa.shape[0]
    tr = _tile(s, 256)

    def body(a_ref, b_ref, c_ref, d_ref, rg_ref, g_ref, o_ref):
        gv = g_ref[...]
        o_ref[:, 0:GROUP] = _rms(a_ref[...], gv[:, 0:GROUP]).astype(BF16)
        o_ref[:, GROUP:2 * GROUP] = _rms(b_ref[...], gv[:, GROUP:2 * GROUP]).astype(BF16)
        seg = _seg_mean_matrix()
        c = c_ref[...]
        cen = c - _seg_mean(c, seg)
        n = cen * lax.rsqrt(_seg_mean(cen * cen, seg) + EPS)
        rg = rg_ref[...]
        o_ref[:, 2 * GROUP:3 * GROUP] = (n * gv[:, 2 * GROUP:3 * GROUP] * (rg * _sigmoid(rg))).astype(BF16)
        o_ref[:, 3 * GROUP:] = _rms(d_ref[...], gv[:, 3 * GROUP:]).astype(BF16)

    blk = pl.BlockSpec((tr, GROUP), lambda i: (i, 0))
    return pl.pallas_call(
        body, name="mix_post", grid=(s // tr,),
        in_specs=[blk] * 4 + [pl.BlockSpec((tr, GROUP), lambda i: (i, OFF_RG // 2)), pl.BlockSpec((1, D_MODEL), lambda i: (0, 0))],
        out_specs=pl.BlockSpec((tr, D_MODEL), lambda i: (i, 0)), out_shape=jax.ShapeDtypeStruct((s, D_MODEL), BF16),
        compiler_params=_cparams(("parallel",)),
    )(oa, ob, oc, od, proj, g.reshape(1, D_MODEL))


def _mix_post_bwd(dmixed, oa, ob, oc, od, proj, g):
    s = oa.shape[0]
    tr = _tile(s, 256)

    def body(dm_ref, a_ref, b_ref, c_ref, d_ref, rg_ref, g_ref, da_ref, db_ref, dc_ref, dd_ref, drg_ref, dg_ref):
        @pl.when(pl.program_id(0) == 0)
        def _():
            dg_ref[...] = jnp.zeros_like(dg_ref)

        gv = g_ref[...]
        dm = dm_ref[...]
        for k, (x_ref, dx_ref) in enumerate(((a_ref, da_ref), (b_ref, db_ref), (None, None), (d_ref, dd_ref))):
            if x_ref is None:
                continue
            cols = slice(k * GROUP, (k + 1) * GROUP)
            dx, gterm = _rms_bwd(x_ref[...], gv[:, cols], dm[:, cols])
            dx_ref[...] = dx
            dg_ref[:, cols] += jnp.sum(gterm, axis=0, keepdims=True)
        cols = slice(2 * GROUP, 3 * GROUP)
        seg = _seg_mean_matrix()
        c = c_ref[...]
        cen = c - _seg_mean(c, seg)
        rstd = lax.rsqrt(_seg_mean(cen * cen, seg) + EPS)
        n = cen * rstd
        rg = rg_ref[...]
        sg = _sigmoid(rg)
        gate = rg * sg
        dy = dm[:, cols]
        gc = gv[:, cols]
        dn = dy * gc * gate
        dg_ref[:, cols] += jnp.sum(dy * n * gate, axis=0, keepdims=True)
        drg_ref[...] = (dy * n * gc * (sg * (1.0 + rg * (1.0 - sg)))).astype(BF16)
        dc_ref[...] = rstd * (dn - _seg_mean(dn, seg) - n * _seg_mean(dn * n, seg))

    blk = pl.BlockSpec((tr, GROUP), lambda i: (i, 0))
    gsp = pl.BlockSpec((1, D_MODEL), lambda i: (0, 0))
    return pl.pallas_call(
        body, name="mix_post_bwd", grid=(s // tr,),
        in_specs=[pl.BlockSpec((tr, D_MODEL), lambda i: (i, 0))] + [blk] * 4 + [pl.BlockSpec((tr, GROUP), lambda i: (i, OFF_RG // 2)), gsp],
        out_specs=[blk] * 5 + [gsp],
        out_shape=[jax.ShapeDtypeStruct((s, GROUP), F32)] * 4 + [jax.ShapeDtypeStruct((s, GROUP), BF16), jax.ShapeDtypeStruct((1, D_MODEL), F32)],
        compiler_params=_cparams(("arbitrary",)),
    )(dmixed, oa, ob, oc, od, proj, g.reshape(1, D_MODEL))


def _pack_w_in(w):
    z = lambda n: jnp.zeros((w.shape[0], n), w.dtype)
    misc = jnp.concatenate([w[:, 768:772], z(KR_LANE - N_HEADS), w[:, 1156:1188], z(LANES - KR_LANE - ROPE_DIM)], axis=1)
    return jnp.concatenate([w[:, 0:768], w[:, 772:1028], w[:, 1188:2980], w[:, 1028:1156], misc], axis=1)


def _unpack_dw_in(d):
    m = OFF_MISC * LANES
    return jnp.concatenate([d[:, 0:768], d[:, m:m + N_HEADS], d[:, 768:1024], d[:, OFF_CKV * LANES:m],
                            d[:, m + KR_LANE:m + KR_LANE + ROPE_DIM], d[:, 1024:OFF_CKV * LANES]], axis=1)


def _pack_w_q(w):
    return jnp.pad(w.reshape(Q_RANK, N_HEADS, HEAD + ROPE_DIM), ((0, 0), (0, 0), (0, LANES - HEAD - ROPE_DIM))).reshape(Q_RANK, 4 * LANES)


def _unpack_dw_q(d):
    return d.reshape(Q_RANK, N_HEADS, LANES)[:, :, :HEAD + ROPE_DIM].reshape(Q_RANK, N_HEADS * (HEAD + ROPE_DIM))


def _pack_w_kv(w):
    w4 = w.reshape(KV_RANK, N_HEADS, 2 * HEAD)
    widen = lambda a: jnp.pad(a, ((0, 0), (0, 0), (0, LANES - HEAD))).reshape(KV_RANK, N_HEADS * LANES)
    return widen(w4[:, :, :HEAD]), widen(w4[:, :, HEAD:])


def _unpack_dw_kv(dk, dv):
    narrow = lambda a: a.reshape(KV_RANK, N_HEADS, LANES)[:, :, :HEAD]
    return jnp.concatenate([narrow(dk), narrow(dv)], axis=2).reshape(KV_RANK, 2 * N_HEADS * HEAD)


def _narrow_heads(a):
    return a.reshape(a.shape[0], N_HEADS, LANES)[:, :, :HEAD].reshape(a.shape[0], N_HEADS * HEAD)


def _widen_heads(a):
    return jnp.pad(a.reshape(a.shape[0], N_HEADS, HEAD), ((0, 0), (0, 0), (0, LANES - HEAD))).reshape(a.shape[0], N_HEADS * LANES)


def _layer_fwd(x, lw, tabs, tag, side=None, fox_side=None, late_weights=None, h1=None, next_gain=None):
    cos_m, sin_m, cos_r, sin_r = tabs
    if h1 is None:
        h1 = _norm_fwd(x, lw["g_mix_pre"], name=tag + "pre_norm")
    proj, projb = _matmul(h1, lw["w_in"], name=tag + "in_proj", also_bf16=True)
    bias_row = jnp.pad(lw["b_forget"], (FF_LANE, LANES - N_HEADS - FF_LANE)).reshape(1, LANES)
    cum_col, cum_row = _fox_cum(proj, bias_row)
    oa, lse_a, *fox_carried = _mixer_fwd("fox", projb, OFF_FQ, projb, OFF_FK, projb, OFF_FV, cum_col=cum_col, cum_row=cum_row,
                                         side=fox_side)
    if late_weights is not None:
        lw = {**lw, **late_weights(fox_carried[0])}
    qm, km, vm, cqn, ckvn = _mla_prep(proj, cos_m, sin_m, lw["g_q_lora"], lw["g_kv_lora"], lw["wq"], lw["wk"], lw["wv"])
    ob_wide, lse_b = _mixer_fwd("mla", qm, 0, km, 0, vm, 0)
    ob = _narrow_heads(ob_wide)
    qr, kr = _ret_prep(proj, cos_r, sin_r)
    oc, ret_states = _ret_fwd(qr, kr, projb, OFF_RV)
    od, tot_d, *carried = _mixer_fwd("sb", projb, OFF_SQ, projb, OFF_SK, projb, OFF_SV, side=side)
    mixed = _mix_post(oa, ob, oc, od, proj, lw["g_mix_out"])
    mix = _matmul(mixed, lw["w_out"], name=tag + "out_proj")
    x1, h2 = _norm_fwd(mix, lw["g_mix_post"], name=tag + "mix_post_norm", resid=x, out_dtype=F32, next_gain=lw["g_ffn_pre"])
    u = _matmul(h2, lw["w_ffn_up"], name=tag + "ffn_up", relu2=True, out_dtype=BF16, col_blocks=True)
    f = _matmul(u, lw["w_ffn_down"], name=tag + "ffn_down")
    x2, h_next = None, None
    if next_gain is not None:
        x2, h_next = _norm_fwd(f, lw["g_ffn_post"], name=tag + "ffn_post_norm", resid=x1, out_dtype=F32, next_gain=next_gain)
    saved = dict(x=x, h1=h1, proj=proj, projb=projb, bias_row=bias_row, cum_col=cum_col, cum_row=cum_row, oa=oa, lse_a=lse_a,
                 qm=qm, km=km, vm=vm, cqn=cqn, ckvn=ckvn, ob=ob, ob_wide=ob_wide, lse_b=lse_b, qr=qr, kr=kr, ret_states=ret_states, oc=oc, od=od, tot_d=tot_d, mixed=mixed,
                 mix=mix, x1=x1, h2=h2, u=u, f=f)
    return x2, saved, lw, (carried[0] if carried else None), h_next


def _layer_bwd(dx2, lw, sv, tabs, tag, side=None, ffn_side=None, fox_side=None, post_given=None, then_prev=None, in_side=None):
    cos_m, sin_m, cos_r, sin_r = tabs
    g = {}
    if post_given is None:
        df, g["g_ffn_post"] = _norm_bwd(sv["f"], lw["g_ffn_post"], dx2, name=tag + "ffn_post_norm_bwd", out_dtype=BF16)
    else:
        df, g["g_ffn_post"] = post_given
    du_pre = _matmul(df, lw["w_ffn_down"], name=tag + "ffn_down_dx", tb=True, out_dtype=BF16, relu2_of=sv["u"], side=ffn_side)
    ffn_carried = None
    if ffn_side is not None:
        du_pre, ffn_carried = du_pre
    g["w_ffn_down"] = _matmul(sv["u"], df, name=tag + "ffn_down_dw", ta=True)
    dh2 = _matmul(du_pre, lw["w_ffn_up"], name=tag + "ffn_up_dx", tb=True, col_blocks=True)
    g["w_ffn_up"] = _matmul(sv["h2"], du_pre, name=tag + "ffn_up_dw", ta=True, col_blocks=True)
    dx1, g["g_ffn_pre"], dmix, g["g_mix_post"] = _norm_bwd(sv["x1"], lw["g_ffn_pre"], dh2, name=tag + "ffn_pre_norm_bwd", add=dx2,
                                                           then=(sv["mix"], lw["g_mix_post"]))
    dmixed = _matmul(dmix, lw["w_out"], name=tag + "out_proj_dx", tb=True)
    g["w_out"] = _matmul(sv["mixed"], dmix, name=tag + "out_proj_dw", ta=True)
    proj, projb = sv["proj"], sv["projb"]
    doa, dob, doc, dod, drg, g["g_mix_out"] = _mix_post_bwd(dmixed, sv["oa"], sv["ob"], sv["oc"], sv["od"], proj, lw["g_mix_out"])
    dfq, dfk, dfv, dck, drs, *fox_carried = _mixer_bwd(
        "fox", projb, OFF_FQ, projb, OFF_FK, projb, OFF_FV, sv["oa"], doa, stat=sv["lse_a"], cum_col=sv["cum_col"],
        cum_row=sv["cum_row"], side=None if fox_side is None else fox_side(g))
    dqm, dkm, dvm = _mixer_bwd("mla", sv["qm"], 0, sv["km"], 0, sv["vm"], 0, sv["ob_wide"], _widen_heads(dob), stat=sv["lse_b"])
    dcq, dckv, dkr, dwq, dwk, dwv, g["g_q_lora"], g["g_kv_lora"] = _mla_prep_bwd(
        dqm, dkm, dvm, proj, sv["cqn"], sv["ckvn"], cos_m, sin_m, lw["g_q_lora"], lw["g_kv_lora"], lw["wq"], lw["wk"], lw["wv"])
    dqr, dkr_ret, drv = _ret_bwd(sv["qr"], sv["kr"], projb, OFF_RV, sv["ret_states"], doc)
    drq, drk = _ret_prep_bwd(dqr, dkr_ret, cos_r, sin_r)
    if callable(side):
        side = side(g, ffn_carried, fox_carried[0] if fox_carried else None)
    dsq, dsk, dsv, *carried = _mixer_bwd("sb", projb, OFF_SQ, projb, OFF_SK, projb, OFF_SV, sv["od"], dod, stat=sv["tot_d"], side=side)
    dmisc, db_row = _fox_gate_bwd(dck, drs, proj, sv["bias_row"], dkr)
    b = lambda a: a.astype(BF16)
    dproj = jnp.concatenate([b(dfq), b(dfk), b(dfv), dcq, drq, drk, b(drv), drg, b(dsq), b(dsk), b(dsv), dckv, dmisc], axis=1)
    g["w_in"] = _matmul(sv["h1"], dproj, name=tag + "in_proj_dw", ta=True)
    g["wq"], g["wk"], g["wv"] = dwq, dwk, dwv
    in_job = None if in_side is None else in_side(g)
    dh1 = _matmul(dproj, lw["w_in"], name=tag + "in_proj_dx", tb=True, side=in_job)
    in_carried = None
    if in_job is not None:
        dh1, in_carried = dh1
    dx, g["g_mix_pre"], *prev_post = _norm_bwd(sv["x"], lw["g_mix_pre"], dh1, name=tag + "pre_norm_bwd", add=dx1, then=then_prev)
    g["b_forget"] = db_row[0, FF_LANE:FF_LANE + N_HEADS]
    return dx, g, (carried[0] if carried else None), (tuple(prev_post) if prev_post else None), in_carried


def _local_step(x, positions, layers, target):
    s = x.shape[0]
    tabs = _rope_tables(positions.reshape(s, 1))
    saved, h1 = [], None
    for li, lw in enumerate(layers):
        nxt = layers[li + 1]["g_mix_pre"] if li + 1 < len(layers) else None
        x, sv, _, _, h1 = _layer_fwd(x, lw, tabs, "l%d_" % li, h1=h1, next_gain=nxt)
        saved.append(sv)
    loss_row, dx, df, dg = _loss_head(saved[-1]["f"], layers[-1]["g_ffn_post"], saved[-1]["x1"], target)
    grads, post = [None] * len(layers), (df, dg)
    for li in reversed(range(len(layers))):
        prev = (saved[li - 1]["f"], layers[li - 1]["g_ffn_post"]) if li > 0 else None
        dx, grads[li], _, post, _ = _layer_bwd(dx, layers[li], saved[li], tabs, "l%d_" % li, post_given=post, then_prev=prev)
    return loss_row[0, 0], dx, grads


def _adamw(w, g, m, v, *, name):
    d, r, c = w.shape
    tr = 256 if r % 256 == 0 else r
    blk = pl.BlockSpec((None, tr, c), lambda l, i: (l, i, 0))
    c1 = 1.0 - ADAM_B1 ** ADAM_STEP
    c2 = 1.0 - ADAM_B2 ** ADAM_STEP

    def body(w_ref, g_ref, m_ref, v_ref, d_ref, mo_ref, vo_ref):
        gv = g_ref[...]
        mn = ADAM_B1 * m_ref[...] + (1.0 - ADAM_B1) * gv
        vn = ADAM_B2 * v_ref[...] + (1.0 - ADAM_B2) * jnp.square(gv)
        mo_ref[...] = mn
        vo_ref[...] = vn
        d_ref[...] = -ADAM_LR * ((mn / c1) / (jnp.sqrt(vn / c2) + ADAM_EPS) + ADAM_WD * w_ref[...])

    return pl.pallas_call(
        body, name=name, grid=(d, r // tr), in_specs=[blk] * 4, out_specs=[blk] * 3,
        out_shape=[jax.ShapeDtypeStruct((d, r, c), F32)] * 3, compiler_params=_cparams(("parallel", "parallel")),
    )(w, g, m, v)


def _adamw_lead(w, g, m, v, *, name, steps):
    a, b, c = w.shape
    blk = pl.BlockSpec((a // steps, b, c), lambda i: (i, 0, 0))
    c1 = 1.0 - ADAM_B1 ** ADAM_STEP
    c2 = 1.0 - ADAM_B2 ** ADAM_STEP

    def body(w_ref, g_ref, m_ref, v_ref, d_ref, mo_ref, vo_ref):
        gv = g_ref[...]
        mn = ADAM_B1 * m_ref[...] + (1.0 - ADAM_B1) * gv
        vn = ADAM_B2 * v_ref[...] + (1.0 - ADAM_B2) * jnp.square(gv)
        mo_ref[...] = mn
        vo_ref[...] = vn
        d_ref[...] = -ADAM_LR * ((mn / c1) / (jnp.sqrt(vn / c2) + ADAM_EPS) + ADAM_WD * w_ref[...])

    return pl.pallas_call(
        body, name=name, grid=(steps,), in_specs=[blk] * 4, out_specs=[blk] * 3,
        out_shape=[jax.ShapeDtypeStruct((a, b, c), F32)] * 3, compiler_params=_cparams(("parallel",)),
    )(w, g, m, v)


SC_TILES = 32
SC_ROWS = 8


def _adamw_sparsecore(ws, gs, ms, vs, *, name):
    n = len(ws)
    c = ws[0].shape[2]
    c1 = 1.0 - ADAM_B1 ** ADAM_STEP
    c2 = 1.0 - ADAM_B2 ** ADAM_STEP
    pieces = [(t, l, r0) for t in range(n) for l in range(ws[t].shape[0]) for r0 in range(0, ws[t].shape[1] // SC_TILES, SC_ROWS)]

    def body(*refs):
        ins, outs = refs[:4 * n], refs[4 * n:7 * n]
        bufs, sem_in, sem_out = refs[7 * n:7 * n + 8], refs[7 * n + 8], refs[7 * n + 9]
        tile = lax.axis_index("subcore") * 2 + lax.axis_index("core")

        def window(k):
            t, l, r0 = pieces[k]
            return t, (l, pl.ds(tile * (ws[t].shape[1] // SC_TILES) + r0, SC_ROWS))

        def loads(k):
            t, at = window(k)
            return [pltpu.make_async_copy(ins[j * n + t].at[at], bufs[4 * (k % 2) + j], sem_in.at[k % 2]) for j in range(4)]

        def stores(k):
            t, at = window(k)
            return [pltpu.make_async_copy(bufs[4 * (k % 2) + j], outs[(j - 1) * n + t].at[at], sem_out.at[k % 2]) for j in (1, 2, 3)]

        def update(k):
            gb, wb, mb, vb = bufs[4 * (k % 2):4 * (k % 2) + 4]

            def adam(gv, wv, mv, vv):
                mn = ADAM_B1 * mv + (1.0 - ADAM_B1) * gv
                vn = ADAM_B2 * vv + (1.0 - ADAM_B2) * (gv * gv)
                return -ADAM_LR * ((mn / c1) / (jnp.sqrt(vn / c2) + ADAM_EPS) + ADAM_WD * wv), mn, vn

            @pl.loop(0, SC_ROWS)
            def _(rr):
                last = (rr, pl.ds(c - 16, 16))
                if c % 16:
                    end = adam(gb[last], wb[last], mb[last], vb[last])

                @pl.loop(0, c // 16 * 16, step=16)
                def _(i):
                    s = (rr, pl.ds(i, 16))
                    wb[s], mb[s], vb[s] = adam(gb[s], wb[s], mb[s], vb[s])

                if c % 16:
                    wb[last], mb[last], vb[last] = end

        for cp in loads(0):
            cp.start()
        for k in range(len(pieces)):
            if k + 1 < len(pieces):
                if k >= 1:
                    for cp in stores(k - 1):
                        cp.wait()
                for cp in loads(k + 1):
                    cp.start()
            for cp in loads(k):
                cp.wait()
            update(k)
            for cp in stores(k):
                cp.start()
        for k in range(max(len(pieces) - 2, 0), len(pieces)):
            for cp in stores(k):
                cp.wait()

    out = pl.kernel(
        body, name=name, out_type=[jax.ShapeDtypeStruct(t.shape, F32) for t in ws] * 3,
        mesh=plsc.VectorSubcoreMesh(core_axis_name="core", subcore_axis_name="subcore"),
        scratch_types=[pltpu.VMEM((SC_ROWS, c), F32)] * 8 + [pltpu.SemaphoreType.DMA((2,)), pltpu.SemaphoreType.DMA((2,))],
    )(*gs, *ws, *ms, *vs)
    return out[:n], out[n:2 * n], out[2 * n:]


BIG = ("w_in", "w_q_up", "w_kv_up", "w_out", "w_ffn_up", "w_ffn_down")
SMALL = ("g_mix_pre", "b_forget", "g_q_lora", "g_kv_lora", "g_mix_out", "g_mix_post", "g_ffn_pre", "g_ffn_post")
N_CHIPS = 4
ANY = pl.BlockSpec(memory_space=pl.ANY)


def _mesh_pos():
    return lax.axis_index("x"), lax.axis_index("y"), lax.axis_index("c")


def _other_chips(x, y):
    return [(1 - x, y), (x, 1 - y), (1 - x, 1 - y)]


def _rows_half(ref, half):
    h = ref.shape[-2] // 2
    return ref.at[(slice(None),) * (len(ref.shape) - 2) + (pl.ds(half * h, h), slice(None))]


def _remote(src, dst, send_sem, recv_sem, device):
    return pltpu.make_async_remote_copy(src_ref=src, dst_ref=dst, send_sem=send_sem, recv_sem=recv_sem, device_id=device,
                                        device_id_type=MESH)


def _comm_call(body, name, args, out_shape, n_sems):
    return pl.pallas_call(
        body, name=name, in_specs=[ANY] * len(args), out_specs=[ANY] * len(out_shape), out_shape=out_shape,
        scratch_shapes=[pltpu.SemaphoreType.DMA((n_sems,)), pltpu.SemaphoreType.DMA((n_sems,))],
        compiler_params=pltpu.CompilerParams(has_side_effects=True),
    )(*args)


def _run_side_job(side, name):
    si = len(side.inputs)

    def body(*refs):
        args = (refs[:si], refs[si:-2], refs[-2], refs[-1])
        sends = side.sends(*args)
        for cp in sends:
            cp.start()
        for cp in side.recvs(*args):
            cp.wait_recv()
        for cp in sends:
            cp.wait_send()

    return _comm_call(body, name, side.inputs, side.out_shape, side.n_sems)


def _gather_job(shards):
    n = len(shards)

    def copies(own_block, ins, outs, send_sems, recv_sems):
        x, y, c = _mesh_pos()
        return [_remote(_rows_half(ins[t], c), _rows_half(outs[t].at[2 * x + y if own_block else 2 * px + py], c),
                        send_sems.at[3 * t + j], recv_sems.at[3 * t + j], (px, py, c))
                for t in range(n) for j, (px, py) in enumerate(_other_chips(x, y))]

    return _SideJob(shards, [jax.ShapeDtypeStruct((N_CHIPS,) + a.shape, a.dtype) for a in shards], 3 * n,
                    functools.partial(copies, True), functools.partial(copies, False))


def _forward_halves(gathered):
    n = len(gathered)

    def body(*refs):
        bufs, send_sems, recv_sems = refs[n:2 * n], refs[-2], refs[-1]
        x, y, c = _mesh_pos()

        def d2d(t, j, block, half):
            region = _rows_half(bufs[t].at[block], half)
            return _remote(region, region, send_sems.at[3 * t + j], recv_sems.at[3 * t + j], (x, y, 1 - c))

        peers = list(enumerate(_other_chips(x, y)))
        sends = [d2d(t, j, 2 * px + py, c) for t in range(n) for j, (px, py) in peers]
        for cp in sends:
            cp.start()
        for t in range(n):
            for j, (px, py) in peers:
                d2d(t, j, 2 * px + py, 1 - c).wait_recv()
        for cp in sends:
            cp.wait_send()

    return pl.pallas_call(
        body, name="gather_forward", in_specs=[ANY] * n, out_specs=[ANY] * n,
        out_shape=[jax.ShapeDtypeStruct(g.shape, g.dtype) for g in gathered], input_output_aliases={t: t for t in range(n)},
        scratch_shapes=[pltpu.SemaphoreType.DMA((3 * n,)), pltpu.SemaphoreType.DMA((3 * n,))],
        compiler_params=pltpu.CompilerParams(has_side_effects=True),
    )(*gathered)


def _exchange_halves_job(gs):
    n = len(gs)

    def copies(ins, outs, send_sems, recv_sems):
        x, y, c = _mesh_pos()
        return [_remote(_rows_half(ins[t], 1 - c), outs[t], send_sems.at[t], recv_sems.at[t], (x, y, 1 - c)) for t in range(n)]

    out_shape = [jax.ShapeDtypeStruct(g.shape[:2] + (g.shape[2] // 2, g.shape[3]), g.dtype) for g in gs]
    return _SideJob(gs, out_shape, n, copies, copies)


def _pair_add(g, r, c_idx, *, name):
    nb, d, rows, cols = g.shape
    h = rows // 2
    tr = min(h, 512)
    nt = h // tr

    def body(c_ref, g_ref, r_ref, p_ref, pb_ref):
        s = g_ref[...] + r_ref[...]
        p_ref[...] = s
        pb_ref[...] = s.astype(BF16)

    blk = pl.BlockSpec((1, 1, tr, cols), lambda k, l, i, c_ref: (k, l, i, 0))
    return pl.pallas_call(
        body, name=name,
        grid_spec=pltpu.PrefetchScalarGridSpec(
            num_scalar_prefetch=1, grid=(nb, d, nt),
            in_specs=[pl.BlockSpec((1, 1, tr, cols), lambda k, l, i, c_ref: (k, l, c_ref[0] * nt + i, 0)), blk],
            out_specs=[blk, blk]),
        out_shape=[jax.ShapeDtypeStruct((nb, d, h, cols), F32), jax.ShapeDtypeStruct((nb, d, h, cols), BF16)],
        compiler_params=_cparams(("parallel", "parallel", "parallel")),
    )(c_idx, g, r)


def _exchange_chips_job(pbs):
    n = len(pbs)

    def copies(ins, outs, send_sems, recv_sems):
        x, y, c = _mesh_pos()
        return [_remote(ins[t].at[2 * px + py], outs[t].at[j], send_sems.at[3 * t + j], recv_sems.at[3 * t + j], (px, py, c))
                for t in range(n) for j, (px, py) in enumerate(_other_chips(x, y))]

    return _SideJob(pbs, [jax.ShapeDtypeStruct((3,) + p.shape[1:], p.dtype) for p in pbs], 3 * n, copies, copies)


def _chip_add(p, r, k_idx, *, name):
    _, d, h, cols = p.shape
    tr = min(h, 512)
    nt = h // tr

    def body(k_ref, p_ref, r_ref, o_ref):
        o_ref[0] = ((p_ref[0, 0] + r_ref[0, 0].astype(F32)) + r_ref[1, 0].astype(F32)) + r_ref[2, 0].astype(F32)

    return pl.pallas_call(
        body, name=name,
        grid_spec=pltpu.PrefetchScalarGridSpec(
            num_scalar_prefetch=1, grid=(d, nt),
            in_specs=[pl.BlockSpec((1, 1, tr, cols), lambda l, i, k_ref: (k_ref[0], l, i, 0)),
                      pl.BlockSpec((3, 1, tr, cols), lambda l, i, k_ref: (0, l, i, 0))],
            out_specs=pl.BlockSpec((1, tr, cols), lambda l, i, k_ref: (l, i, 0))),
        out_shape=jax.ShapeDtypeStruct((d, h, cols), F32), compiler_params=_cparams(("parallel", "parallel")),
    )(k_idx, p, r)


def _share_halves(qs):
    n = len(qs)

    def body(*refs):
        ins, outs, send_sems, recv_sems = refs[:n], refs[n:2 * n], refs[2 * n], refs[2 * n + 1]
        x, y, c = _mesh_pos()
        cps = [_remote(ins[t], outs[t], send_sems.at[t], recv_sems.at[t], (x, y, 1 - c)) for t in range(n)]
        for cp in cps:
            cp.start()
        for cp in cps:
            cp.wait_recv()
        for cp in cps:
            cp.wait_send()

    return _comm_call(body, "grad_pair_share", qs, [jax.ShapeDtypeStruct(q.shape, q.dtype) for q in qs], n)


def _all_reduce_small(v):
    r, cols = v.shape
    n_dev = 8

    def body(v_ref, o_ref, buf, send_sems, recv_sems):
        x, y, c = _mesh_pos()
        me = 4 * x + 2 * y + c
        buf[me] = v_ref[...]

        def peer(j):
            return (1 - x if j & 4 else x, 1 - y if j & 2 else y, 1 - c if j & 1 else c)

        def copy(j, slot):
            return pltpu.make_async_remote_copy(src_ref=v_ref, dst_ref=buf.at[slot], send_sem=send_sems.at[j - 1],
                                                recv_sem=recv_sems.at[j - 1], device_id=peer(j), device_id_type=MESH)

        sends = [copy(j, me) for j in range(1, n_dev)]
        for cp in sends:
            cp.start()
        for j in range(1, n_dev):
            px, py, pc = peer(j)
            copy(j, 4 * px + 2 * py + pc).wait_recv()
        for cp in sends:
            cp.wait_send()
        acc = buf[0]
        for d in range(1, n_dev):
            acc = acc + buf[d]
        o_ref[...] = acc

    vm = pl.BlockSpec(memory_space=pltpu.VMEM)
    return pl.pallas_call(
        body, name="small_all_reduce", in_specs=[vm], out_specs=vm, out_shape=jax.ShapeDtypeStruct((r, cols), F32),
        scratch_shapes=[pltpu.VMEM((n_dev, r, cols), F32), pltpu.SemaphoreType.DMA((n_dev - 1,)), pltpu.SemaphoreType.DMA((n_dev - 1,))],
        compiler_params=pltpu.CompilerParams(has_side_effects=True),
    )(v)


_COL_SHARDED = ("w_in", "w_q_up", "w_kv_up", "w_ffn_up")


def _shard_cols(blocks, a, b):
    c = blocks[0].shape[-1]
    out = []
    while a < b:
        k = a // c
        hi = min(b, (k + 1) * c)
        out.append(blocks[k][:, a - k * c:hi - k * c])
        a = hi
    return out


def _pack_w_in_shards(blocks):
    z = lambda n: [jnp.zeros((blocks[0].shape[0], n), blocks[0].dtype)]
    cols = lambda a, b: _shard_cols(blocks, a, b)
    return jnp.concatenate(cols(0, 768) + cols(772, 1028) + cols(1188, 2980) + cols(1028, 1156) + cols(768, 772)
                           + z(KR_LANE - N_HEADS) + cols(1156, 1188) + z(LANES - KR_LANE - ROPE_DIM), axis=1)


def _whole_layer(name, blocks):
    if name in _COL_SHARDED:
        return jnp.concatenate([blocks[k] for k in range(N_CHIPS)], axis=1)
    return blocks.reshape(N_CHIPS * blocks.shape[1], blocks.shape[2])


def _split_layer(name, whole):
    if name in _COL_SHARDED:
        c = whole.shape[1] // N_CHIPS
        return jnp.stack([whole[:, k * c:(k + 1) * c] for k in range(N_CHIPS)])
    return whole.reshape(N_CHIPS, whole.shape[0] // N_CHIPS, whole.shape[1])


def _small_to_rows(d):
    v = jnp.concatenate([d[k].astype(F32).reshape(-1) for k in SMALL])
    rows = -(-v.shape[0] // (8 * LANES)) * 8
    return jnp.pad(v, (0, rows * LANES - v.shape[0])).reshape(rows, LANES)


def _small_from_rows(rows, shapes):
    v = rows.reshape(-1)
    out, o = {}, 0
    for k in SMALL:
        sz = int(np.prod(shapes[k]))
        out[k] = v[o:o + sz].reshape(shapes[k])
        o += sz
    return out


_ARG_NAMES = ("x", "positions", "g_mix_pre", "w_in", "b_forget", "g_q_lora", "w_q_up", "g_kv_lora", "w_kv_up", "g_mix_out", "w_out",
              "g_mix_post", "g_ffn_pre", "w_ffn_up", "w_ffn_down", "g_ffn_post")
_WEIGHTS = _ARG_NAMES[2:]


def kernel(x, positions, g_mix_pre, w_in, b_forget, g_q_lora, w_q_up, g_kv_lora, w_kv_up, g_mix_out, w_out, g_mix_post, g_ffn_pre, w_ffn_up, w_ffn_down, g_ffn_post, loss_target, m_g_mix_pre, m_w_in, m_b_forget, m_g_q_lora, m_w_q_up, m_g_kv_lora, m_w_kv_up, m_g_mix_out, m_w_out, m_g_mix_post, m_g_ffn_pre, m_w_ffn_up, m_w_ffn_down, m_g_ffn_post, v_g_mix_pre, v_w_in, v_b_forget, v_g_q_lora, v_w_q_up, v_g_kv_lora, v_w_kv_up, v_g_mix_out, v_w_out, v_g_mix_post, v_g_ffn_pre, v_w_ffn_up, v_w_ffn_down, v_g_ffn_post):
    w = dict(g_mix_pre=g_mix_pre, w_in=w_in, b_forget=b_forget, g_q_lora=g_q_lora, w_q_up=w_q_up, g_kv_lora=g_kv_lora, w_kv_up=w_kv_up,
             g_mix_out=g_mix_out, w_out=w_out, g_mix_post=g_mix_post, g_ffn_pre=g_ffn_pre, w_ffn_up=w_ffn_up, w_ffn_down=w_ffn_down,
             g_ffn_post=g_ffn_post)
    m = dict(g_mix_pre=m_g_mix_pre, w_in=m_w_in, b_forget=m_b_forget, g_q_lora=m_g_q_lora, w_q_up=m_w_q_up, g_kv_lora=m_g_kv_lora,
             w_kv_up=m_w_kv_up, g_mix_out=m_g_mix_out, w_out=m_w_out, g_mix_post=m_g_mix_post, g_ffn_pre=m_g_ffn_pre,
             w_ffn_up=m_w_ffn_up, w_ffn_down=m_w_ffn_down, g_ffn_post=m_g_ffn_post)
    v = dict(g_mix_pre=v_g_mix_pre, w_in=v_w_in, b_forget=v_b_forget, g_q_lora=v_g_q_lora, w_q_up=v_w_q_up, g_kv_lora=v_g_kv_lora,
             w_kv_up=v_w_kv_up, g_mix_out=v_g_mix_out, w_out=v_w_out, g_mix_post=v_g_mix_post, g_ffn_pre=v_g_ffn_pre,
             w_ffn_up=v_w_ffn_up, w_ffn_down=v_w_ffn_down, g_ffn_post=v_g_ffn_post)
    small_shapes = {k: w[k].shape for k in SMALL}
    c_idx = lax.axis_index("c").astype(jnp.int32).reshape(1)
    k_idx = (2 * lax.axis_index("x") + lax.axis_index("y")).astype(jnp.int32).reshape(1)
    first_core = lax.axis_index("c") == 0

    mine = 2 * lax.axis_index("x") + lax.axis_index("y")
    shards_b = [{k: w[k][l:l + 1].astype(BF16) for k in BIG} for l in range(DEPTH)]
    gains = [dict(g_mix_pre=g_mix_pre[l], b_forget=b_forget[l], g_q_lora=g_q_lora[l], g_kv_lora=g_kv_lora[l], g_mix_out=g_mix_out[l],
                  g_mix_post=g_mix_post[l], g_ffn_pre=g_ffn_pre[l], g_ffn_post=g_ffn_post[l]) for l in range(DEPTH)]
    FIRST, LATER = ("w_in", "w_q_up", "w_kv_up"), ("w_out", "w_ffn_up", "w_ffn_down")
    EARLY_GRADS, LATE_GRADS = ("w_ffn_down", "w_ffn_up", "w_out"), ("w_in", "w_q_up", "w_kv_up")
    SC_ADAMW = EARLY_GRADS

    def gather_job(l, names):
        return _gather_job([shards_b[l][k] for k in names])

    def weights_of(l, names, gathered):
        four = {k: lax.dynamic_update_slice(g, shards_b[l][k][None], (mine, 0, 0, 0))[:, 0]
                for k, g in zip(names, _forward_halves(gathered))}
        out = {}
        for k in names:
            if k == "w_in":
                out["w_in"] = _pack_w_in_shards(four[k])
            elif k == "w_q_up":
                out["wq"] = _pack_w_q(_whole_layer(k, four[k]))
            elif k == "w_kv_up":
                out["wk"], out["wv"] = _pack_w_kv(_whole_layer(k, four[k]))
            elif k == "w_ffn_up":
                out[k] = four[k]
            else:
                out[k] = _whole_layer(k, four[k])
        return out

    def grad_blocks(names, g):
        whole = dict(w_in=lambda: _unpack_dw_in(g["w_in"]), w_q_up=lambda: _unpack_dw_q(g["wq"]),
                     w_kv_up=lambda: _unpack_dw_kv(g["wk"], g["wv"]), w_out=lambda: g["w_out"], w_ffn_down=lambda: g["w_ffn_down"])
        return [(g[k] if k == "w_ffn_up" else _split_layer(k, whole[k]()))[:, None] for k in names]

    def pair_sums(names, blocks, theirs):
        return [_pair_add(b, r, c_idx, name="grad_pair_add_" + k) for k, b, r in zip(names, blocks, theirs)]

    def exchange_job(*pairs):
        return _exchange_chips_job([pb for pair in pairs for (_, pb) in pair])

    def finish_grads(names, pair, partial):
        half = [_chip_add(p, r, k_idx, name="grad_chip_add_" + k) for k, (p, _), r in zip(names, pair, partial)]
        return {k: jnp.where(first_core, jnp.concatenate([q, s], axis=1), jnp.concatenate([s, q], axis=1))
                for k, q, s in zip(names, half, _share_halves(half))}

    seq = x.shape[1]
    tabs = _rope_tables(positions[0].reshape(seq, 1))
    first0 = weights_of(0, FIRST, _run_side_job(gather_job(0, FIRST), "gather_weights_l0"))
    x1, saved0, lw0, gathered1, h1 = _layer_fwd(x[0], {**gains[0], **first0}, tabs, "l0_", fox_side=gather_job(0, LATER),
                                                late_weights=lambda got: weights_of(0, LATER, got), side=gather_job(1, BIG),
                                                next_gain=gains[1]["g_mix_pre"])
    lw1 = {**gains[1], **weights_of(1, BIG, gathered1)}
    _, saved1, _, _, _ = _layer_fwd(x1, lw1, tabs, "l1_", h1=h1)
    loss_row, dx, df1, dg1 = _loss_head(saved1["f"], lw1["g_ffn_post"], saved1["x1"], loss_target[0])
    loss = lax.psum(loss_row[0, 0], ("x", "y", "c"))
    dx, grads1, _, post0, _ = _layer_bwd(dx, lw1, saved1, tabs, "l1_", post_given=(df1, dg1),
                                      then_prev=(saved0["f"], lw0["g_ffn_post"]))
    blocks1 = grad_blocks(BIG, grads1)
    early_blocks0, pair1, early0 = [], [], []

    def beside_l0_fox_backward(g):
        early_blocks0.extend(grad_blocks(EARLY_GRADS, g))
        return _exchange_halves_job(early_blocks0)

    def beside_l0_sb_backward(g, theirs1, theirs_early0):
        pair1.extend(pair_sums(BIG, blocks1, theirs1))
        early0.extend(pair_sums(EARLY_GRADS, early_blocks0, theirs_early0))
        return exchange_job(pair1, early0)

    late0 = []

    def beside_l0_in_proj_dx(g):
        late_blocks0 = grad_blocks(LATE_GRADS, g)
        late0.extend(pair_sums(LATE_GRADS, late_blocks0, _run_side_job(_exchange_halves_job(late_blocks0), "grad_pair_exchange_l0")))
        return exchange_job(late0)

    dx, grads0, partial, _, late_partial = _layer_bwd(dx, lw0, saved0, tabs, "l0_", ffn_side=_exchange_halves_job(blocks1),
                                                      fox_side=beside_l0_fox_backward, side=beside_l0_sb_backward, post_given=post0,
                                                      in_side=beside_l0_in_proj_dx)
    big1 = finish_grads(BIG, pair1, partial[:len(BIG)])
    big0 = finish_grads(EARLY_GRADS, early0, partial[len(BIG):])
    g_early = {k: jnp.concatenate([big0[k], big1[k]], axis=0) for k in SC_ADAMW}
    sc_delta, sc_m, sc_v = _adamw_sparsecore([w[k] for k in SC_ADAMW], [g_early[k] for k in SC_ADAMW], [m[k] for k in SC_ADAMW],
                                             [v[k] for k in SC_ADAMW], name="adamw_sparsecore")
    big0.update(finish_grads(LATE_GRADS, late0, late_partial))
    g_big = {k: g_early[k] if k in SC_ADAMW else jnp.concatenate([big0[k], big1[k]], axis=0) for k in BIG}
    g_in_lead = jnp.concatenate([jnp.transpose(big0["w_in"], (2, 0, 1)), jnp.transpose(big1["w_in"], (2, 0, 1))], axis=1)
    g_big["w_in"] = jnp.transpose(g_in_lead, (1, 2, 0))
    w_in_lead = _adamw_lead(jnp.transpose(w["w_in"], (2, 0, 1)), g_in_lead, jnp.transpose(m["w_in"], (2, 0, 1)),
                            jnp.transpose(v["w_in"], (2, 0, 1)), name="adamw_w_in", steps=5)
    sc_late = [[jnp.transpose(t, (1, 2, 0))] for t in w_in_lead]
    grads = [grads0, grads1]

    g_small_local = {k: jnp.stack([grads[l][k].reshape(small_shapes[k][1:]) for l in range(DEPTH)]) for k in SMALL}
    g_small = _small_from_rows(_all_reduce_small(_small_to_rows(g_small_local)), small_shapes)

    g_all = {**g_big, **g_small}
    delta, new_m, new_v = {}, {}, {}
    for k in BIG:
        if k in SC_ADAMW:
            i = SC_ADAMW.index(k)
            delta[k], new_m[k], new_v[k] = sc_delta[i], sc_m[i], sc_v[i]
        elif k == "w_in":
            delta[k], new_m[k], new_v[k] = [t[0] for t in sc_late]
        elif k == "w_q_up":
            out = _adamw(*[jnp.swapaxes(t[k], 1, 2) for t in (w, g_all, m, v)], name="adamw_" + k)
            delta[k], new_m[k], new_v[k] = [jnp.swapaxes(t, 1, 2) for t in out]
        else:
            delta[k], new_m[k], new_v[k] = _adamw(w[k], g_all[k], m[k], v[k], name="adamw_" + k)
    ds, ms, vs = _adamw(*[_small_to_rows(t)[None] for t in (w, g_small, m, v)], name="adamw_small")
    delta.update(_small_from_rows(ds, small_shapes))
    new_m.update(_small_from_rows(ms, small_shapes))
    new_v.update(_small_from_rows(vs, small_shapes))

    grad_x = dx.reshape(x.shape)
    return (loss, grad_x, *[g_all[k] for k in _WEIGHTS], *[delta[k] for k in _WEIGHTS], *[new_m[k] for k in _WEIGHTS],
            *[new_v[k] for k in _WEIGHTS])
```
